```python
import math
import jax, jax.numpy as jnp
from jax import lax
import numpy as np

D_MODEL = 1024
BATCH = 32
SEQ = 2048
DEPTH = 1

N_META = 16
SSM_WIDTH = 1024
SSM_GROUP = 16
SSM_GROUPS = SSM_WIDTH // SSM_GROUP
SSM_STATE = 64
DT_MIN = 1e-3
DT_MAX = 1e-1
HGRN_WIDTH = 1024
HGRN_HEAD_DIM = 128
HGRN_HEADS = HGRN_WIDTH // HGRN_HEAD_DIM
HGRN_CHUNK = 16
D_FF = 2816
CONV_WIDTH = 3
EPS = 1e-6
IN_COLS = SSM_WIDTH + 4 * HGRN_WIDTH + 2 * D_MODEL

kernel_name = 'hybrid_s5_hgrn2_gated_merge_block'


def rmsnorm(x, g):
    xf = x.astype(jnp.float32)
    y = xf * lax.rsqrt(jnp.mean(xf * xf, axis=-1, keepdims=True) + EPS)
    return (y * g.astype(jnp.float32)).astype(x.dtype)


def _complex_affine_combine(e1, e2):
    a1r, a1i, b1r, b1i = e1
    a2r, a2i, b2r, b2i = e2
    ar = a1r * a2r - a1i * a2i
    ai = a1r * a2i + a1i * a2r
    br = a2r * b1r - a2i * b1i + b2r
    bi = a2r * b1i + a2i * b1r + b2i
    return ar, ai, br, bi


def s5_mixer(u, lam_re, lam_im, log_dt, b_re, b_im, c_re, c_im, d_skip, w_glu):
    bsz, L, _ = u.shape
    uf = u.astype(jnp.float32).reshape(bsz, L, SSM_GROUPS, SSM_GROUP)
    lr = lam_re.astype(jnp.float32)
    li = lam_im.astype(jnp.float32)
    dt = jnp.exp(log_dt.astype(jnp.float32))[:, None]
    mag = jnp.exp(lr * dt)
    ab_re = mag * jnp.cos(li * dt)
    ab_im = mag * jnp.sin(li * dt)
    den = lr * lr + li * li
    nr = ab_re - 1.0
    coef_re = (nr * lr + ab_im * li) / den
    coef_im = (ab_im * lr - nr * li) / den
    br = b_re.astype(jnp.float32)
    bi = b_im.astype(jnp.float32)
    bb_re = coef_re[..., None] * br - coef_im[..., None] * bi
    bb_im = coef_re[..., None] * bi + coef_im[..., None] * br
    v_re = jnp.einsum('blgh,gph->blgp', uf, bb_re)
    v_im = jnp.einsum('blgh,gph->blgp', uf, bb_im)
    a_re = jnp.broadcast_to(ab_re[None, None], (1, L, SSM_GROUPS, SSM_STATE))
    a_im = jnp.broadcast_to(ab_im[None, None], (1, L, SSM_GROUPS, SSM_STATE))
    _, _, s_re, s_im = lax.associative_scan(_complex_affine_combine, (a_re, a_im, v_re, v_im), axis=1)
    y = (jnp.einsum('blgp,ghp->blgh', s_re, c_re.astype(jnp.float32))
         - jnp.einsum('blgp,ghp->blgh', s_im, c_im.astype(jnp.float32))
         + d_skip.astype(jnp.float32).reshape(SSM_GROUPS, SSM_GROUP) * uf)
    y = jax.nn.gelu(y.reshape(bsz, L, SSM_WIDTH)).astype(u.dtype)
    return y * jax.nn.sigmoid(y @ w_glu)


def hgrn2_mixer(q, f_logit, i_in, og, lb, norm_g):
    bsz, L, _ = q.shape
    n_chunks = L // HGRN_CHUNK

    def heads(t):
        return t.reshape(bsz, n_chunks, HGRN_CHUNK, HGRN_HEADS, HGRN_HEAD_DIM).transpose(1, 0, 3, 2, 4)

    lbf = lb.astype(jnp.float32)
    f = lbf + (1.0 - lbf) * jax.nn.sigmoid(f_logit.astype(jnp.float32))
    log_f = jnp.log(f)
    qh = heads(q.astype(jnp.float32))
    kh = heads(1.0 - f)
    vh = heads(i_in.astype(jnp.float32))
    cum = jnp.cumsum(heads(log_f), axis=3)
    q_in = qh * jnp.exp(cum)
    k_in = kh * jnp.exp(-cum)
    k_out = kh * jnp.exp(cum[..., -1:, :] - cum)
    chunk_decay = jnp.exp(cum[..., -1, :])
    causal = jnp.tril(jnp.ones((HGRN_CHUNK, HGRN_CHUNK), dtype=bool))

    def chunk_step(state, xs):
        qi, ki, ko, v, dec = xs
        scores = jnp.where(causal, jnp.einsum('bhtd,bhsd->bhts', qi, ki), 0.0)
        o = jnp.einsum('bhts,bhsv->bhtv', scores, v) + jnp.einsum('bhtd,bhdv->bhtv', qi, state)
        state = dec[..., None] * state + jnp.einsum('bhsd,bhsv->bhdv', ko, v)
        return state, o

    init = jnp.zeros((bsz, HGRN_HEADS, HGRN_HEAD_DIM, HGRN_HEAD_DIM), jnp.float32)
    _, o = lax.scan(chunk_step, init, (q_in, k_in, k_out, vh, chunk_decay))
    o = o.transpose(1, 0, 3, 2, 4).reshape(bsz, L, HGRN_HEADS, HGRN_HEAD_DIM)
    o = o * lax.rsqrt(jnp.mean(o * o, axis=-1, keepdims=True) + EPS) * norm_g.astype(jnp.float32)
    o = o.reshape(bsz, L, HGRN_WIDTH).astype(q.dtype)
    return o * jax.nn.silu(og)


def causal_dwconv(u, w, b):
    L = u.shape[1]
    up = jnp.pad(u, ((0, 0), (CONV_WIDTH - 1, 0), (0, 0)))
    out = b
    for j in range(CONV_WIDTH):
        out = out + up[:, j:j + L, :] * w[j]
    return out


def _fwd_setup_inputs(seed: int = 0) -> dict:
    key = jax.random.key(seed)
    ks = jax.random.split(key, 24)
    f32 = jnp.float32

    def nrm(k, shape, scale):
        return jax.random.normal(k, shape, f32) * scale

    n_idx = jnp.arange(SSM_STATE, dtype=f32)
    return {
        'x': nrm(ks[0], (BATCH, SEQ, D_MODEL), 1.0),
        'meta_tokens': nrm(ks[1], (N_META, D_MODEL), 1.0),
        'mix_norm_g': 1.0 + nrm(ks[2], (DEPTH, D_MODEL), 0.02),
        'w_in': nrm(ks[3], (DEPTH, D_MODEL, IN_COLS), D_MODEL ** -0.5),
        'ssm_lambda_re': -0.5 + nrm(ks[4], (DEPTH, SSM_GROUPS, SSM_STATE), 0.01),
        'ssm_lambda_im': math.pi * n_idx + nrm(ks[5], (DEPTH, SSM_GROUPS, SSM_STATE), 0.01),
        'ssm_log_dt': jax.random.uniform(ks[6], (DEPTH, SSM_GROUPS), f32, math.log(DT_MIN), math.log(DT_MAX)),
        'ssm_b_re': nrm(ks[7], (DEPTH, SSM_GROUPS, SSM_STATE, SSM_GROUP), (2 * SSM_GROUP) ** -0.5),
        'ssm_b_im': nrm(ks[8], (DEPTH, SSM_GROUPS, SSM_STATE, SSM_GROUP), (2 * SSM_GROUP) ** -0.5),
        'ssm_c_re': nrm(ks[9], (DEPTH, SSM_GROUPS, SSM_GROUP, SSM_STATE), SSM_STATE ** -0.5),
        'ssm_c_im': nrm(ks[10], (DEPTH, SSM_GROUPS, SSM_GROUP, SSM_STATE), SSM_STATE ** -0.5),
        'ssm_d': nrm(ks[11], (DEPTH, SSM_WIDTH), 1.0),
        'ssm_w_glu': nrm(ks[12], (DEPTH, SSM_WIDTH, SSM_WIDTH), SSM_WIDTH ** -0.5),
        'w_ssm_proj': nrm(ks[13], (DEPTH, SSM_WIDTH, D_MODEL), SSM_WIDTH ** -0.5),
        'hgrn_lb_logits': nrm(ks[14], (DEPTH + 1, HGRN_WIDTH), 0.1),
        'hgrn_norm_g': 1.0 + nrm(ks[15], (DEPTH, HGRN_HEAD_DIM), 0.02),
        'w_hgrn_proj': nrm(ks[16], (DEPTH, HGRN_WIDTH, D_MODEL), HGRN_WIDTH ** -0.5),
        'w_out': nrm(ks[17], (DEPTH, D_MODEL, D_MODEL), D_MODEL ** -0.5),
        'ffn_norm_g': 1.0 + nrm(ks[18], (DEPTH, D_MODEL), 0.02),
        'w_up': nrm(ks[19], (DEPTH, D_MODEL, 2 * D_FF), D_MODEL ** -0.5),
        'conv_w': nrm(ks[20], (DEPTH, CONV_WIDTH, 2 * D_FF), CONV_WIDTH ** -0.5),
        'conv_b': nrm(ks[21], (DEPTH, 2 * D_FF), 0.01),
        'w_down': nrm(ks[22], (DEPTH, D_FF, D_MODEL), D_FF ** -0.5),
        'final_norm_g': 1.0 + nrm(ks[23], (D_MODEL,), 0.02),
    }


def _fwd_reference(x, meta_tokens, mix_norm_g, w_in, ssm_lambda_re, ssm_lambda_im, ssm_log_dt,
              ssm_b_re, ssm_b_im, ssm_c_re, ssm_c_im, ssm_d, ssm_w_glu, w_ssm_proj,
              hgrn_lb_logits, hgrn_norm_g, w_hgrn_proj, w_out, ffn_norm_g, w_up, conv_w,
              conv_b, w_down, final_norm_g):
    bsz = x.shape[0]
    meta = jnp.broadcast_to(meta_tokens.astype(x.dtype)[None], (bsz, N_META, D_MODEL))
    h = jnp.concatenate([meta, x], axis=1)
    lower_bounds = jnp.cumsum(jax.nn.softmax(hgrn_lb_logits.astype(jnp.float32), axis=0), axis=0)
    o1 = SSM_WIDTH
    o2 = o1 + HGRN_WIDTH
    o3 = o2 + HGRN_WIDTH
    o4 = o3 + HGRN_WIDTH
    o5 = o4 + HGRN_WIDTH
    o6 = o5 + D_MODEL
    for l in range(DEPTH):
        z = rmsnorm(h, mix_norm_g[l])
        p = z @ w_in[l]
        y_a = s5_mixer(p[..., :o1], ssm_lambda_re[l], ssm_lambda_im[l], ssm_log_dt[l],
                       ssm_b_re[l], ssm_b_im[l], ssm_c_re[l], ssm_c_im[l], ssm_d[l], ssm_w_glu[l])
        y_b = hgrn2_mixer(p[..., o1:o2], p[..., o2:o3], p[..., o3:o4], p[..., o4:o5],
                          lower_bounds[l], hgrn_norm_g[l])
        merged = (jax.nn.sigmoid(p[..., o5:o6]) * (y_a @ w_ssm_proj[l])
                  + jax.nn.sigmoid(p[..., o6:]) * (y_b @ w_hgrn_proj[l]))
        h = h + merged @ w_out[l]
        z = rmsnorm(h, ffn_norm_g[l])
        u = causal_dwconv(z @ w_up[l], conv_w[l], conv_b[l])
        h = h + (jax.nn.silu(u[..., :D_FF]) * u[..., D_FF:]) @ w_down[l]
    return rmsnorm(h[:, N_META:], final_norm_g)


import jax as _jax
import jax.numpy as _jnp

TWIN_FORMAT = 'train_step'
FWD_PARAMS = ['x', 'meta_tokens', 'mix_norm_g', 'w_in', 'ssm_lambda_re', 'ssm_lambda_im', 'ssm_log_dt', 'ssm_b_re', 'ssm_b_im', 'ssm_c_re', 'ssm_c_im', 'ssm_d', 'ssm_w_glu', 'w_ssm_proj', 'hgrn_lb_logits', 'hgrn_norm_g', 'w_hgrn_proj', 'w_out', 'ffn_norm_g', 'w_up', 'conv_w', 'conv_b', 'w_down', 'final_norm_g']
TWIN_WEIGHTS = ['meta_tokens', 'mix_norm_g', 'w_in', 'ssm_lambda_re', 'ssm_lambda_im', 'ssm_log_dt', 'ssm_b_re', 'ssm_b_im', 'ssm_c_re', 'ssm_c_im', 'ssm_d', 'ssm_w_glu', 'w_ssm_proj', 'hgrn_lb_logits', 'hgrn_norm_g', 'w_hgrn_proj', 'w_out', 'ffn_norm_g', 'w_up', 'conv_w', 'conv_b', 'w_down', 'final_norm_g']
TWIN_DIFF_INPUT = 'x'
TWIN_INPUTS = ['x', 'meta_tokens', 'mix_norm_g', 'w_in', 'ssm_lambda_re', 'ssm_lambda_im', 'ssm_log_dt', 'ssm_b_re', 'ssm_b_im', 'ssm_c_re', 'ssm_c_im', 'ssm_d', 'ssm_w_glu', 'w_ssm_proj', 'hgrn_lb_logits', 'hgrn_norm_g', 'w_hgrn_proj', 'w_out', 'ffn_norm_g', 'w_up', 'conv_w', 'conv_b', 'w_down', 'final_norm_g', 'loss_target', 'm_meta_tokens', 'm_mix_norm_g', 'm_w_in', 'm_ssm_lambda_re', 'm_ssm_lambda_im', 'm_ssm_log_dt', 'm_ssm_b_re', 'm_ssm_b_im', 'm_ssm_c_re', 'm_ssm_c_im', 'm_ssm_d', 'm_ssm_w_glu', 'm_w_ssm_proj', 'm_hgrn_lb_logits', 'm_hgrn_norm_g', 'm_w_hgrn_proj', 'm_w_out', 'm_ffn_norm_g', 'm_w_up', 'm_conv_w', 'm_conv_b', 'm_w_down', 'm_final_norm_g', 'v_meta_tokens', 'v_mix_norm_g', 'v_w_in', 'v_ssm_lambda_re', 'v_ssm_lambda_im', 'v_ssm_log_dt', 'v_ssm_b_re', 'v_ssm_b_im', 'v_ssm_c_re', 'v_ssm_c_im', 'v_ssm_d', 'v_ssm_w_glu', 'v_w_ssm_proj', 'v_hgrn_lb_logits', 'v_hgrn_norm_g', 'v_w_hgrn_proj', 'v_w_out', 'v_ffn_norm_g', 'v_w_up', 'v_conv_w', 'v_conv_b', 'v_w_down', 'v_final_norm_g']
TWIN_OUTPUTS = ['loss', 'grad_x', 'grad_meta_tokens', 'grad_mix_norm_g', 'grad_w_in', 'grad_ssm_lambda_re', 'grad_ssm_lambda_im', 'grad_ssm_log_dt', 'grad_ssm_b_re', 'grad_ssm_b_im', 'grad_ssm_c_re', 'grad_ssm_c_im', 'grad_ssm_d', 'grad_ssm_w_glu', 'grad_w_ssm_proj', 'grad_hgrn_lb_logits', 'grad_hgrn_norm_g', 'grad_w_hgrn_proj', 'grad_w_out', 'grad_ffn_norm_g', 'grad_w_up', 'grad_conv_w', 'grad_conv_b', 'grad_w_down', 'grad_final_norm_g', 'delta_meta_tokens', 'delta_mix_norm_g', 'delta_w_in', 'delta_ssm_lambda_re', 'delta_ssm_lambda_im', 'delta_ssm_log_dt', 'delta_ssm_b_re', 'delta_ssm_b_im', 'delta_ssm_c_re', 'delta_ssm_c_im', 'delta_ssm_d', 'delta_ssm_w_glu', 'delta_w_ssm_proj', 'delta_hgrn_lb_logits', 'delta_hgrn_norm_g', 'delta_w_hgrn_proj', 'delta_w_out', 'delta_ffn_norm_g', 'delta_w_up', 'delta_conv_w', 'delta_conv_b', 'delta_w_down', 'delta_final_norm_g', 'new_m_meta_tokens', 'new_m_mix_norm_g', 'new_m_w_in', 'new_m_ssm_lambda_re', 'new_m_ssm_lambda_im', 'new_m_ssm_log_dt', 'new_m_ssm_b_re', 'new_m_ssm_b_im', 'new_m_ssm_c_re', 'new_m_ssm_c_im', 'new_m_ssm_d', 'new_m_ssm_w_glu', 'new_m_w_ssm_proj', 'new_m_hgrn_lb_logits', 'new_m_hgrn_norm_g', 'new_m_w_hgrn_proj', 'new_m_w_out', 'new_m_ffn_norm_g', 'new_m_w_up', 'new_m_conv_w', 'new_m_conv_b', 'new_m_w_down', 'new_m_final_norm_g', 'new_v_meta_tokens', 'new_v_mix_norm_g', 'new_v_w_in', 'new_v_ssm_lambda_re', 'new_v_ssm_lambda_im', 'new_v_ssm_log_dt', 'new_v_ssm_b_re', 'new_v_ssm_b_im', 'new_v_ssm_c_re', 'new_v_ssm_c_im', 'new_v_ssm_d', 'new_v_ssm_w_glu', 'new_v_w_ssm_proj', 'new_v_hgrn_lb_logits', 'new_v_hgrn_norm_g', 'new_v_w_hgrn_proj', 'new_v_w_out', 'new_v_ffn_norm_g', 'new_v_w_up', 'new_v_conv_w', 'new_v_conv_b', 'new_v_w_down', 'new_v_final_norm_g']
TWIN_LEAF_KINDS = {'loss': 'loss', 'grad_x': 'grad_x', 'grad_meta_tokens': 'grad_w', 'grad_mix_norm_g': 'grad_w', 'grad_w_in': 'grad_w', 'grad_ssm_lambda_re': 'grad_w', 'grad_ssm_lambda_im': 'grad_w', 'grad_ssm_log_dt': 'grad_w', 'grad_ssm_b_re': 'grad_w', 'grad_ssm_b_im': 'grad_w', 'grad_ssm_c_re': 'grad_w', 'grad_ssm_c_im': 'grad_w', 'grad_ssm_d': 'grad_w', 'grad_ssm_w_glu': 'grad_w', 'grad_w_ssm_proj': 'grad_w', 'grad_hgrn_lb_logits': 'grad_w', 'grad_hgrn_norm_g': 'grad_w', 'grad_w_hgrn_proj': 'grad_w', 'grad_w_out': 'grad_w', 'grad_ffn_norm_g': 'grad_w', 'grad_w_up': 'grad_w', 'grad_conv_w': 'grad_w', 'grad_conv_b': 'grad_w', 'grad_w_down': 'grad_w', 'grad_final_norm_g': 'grad_w', 'delta_meta_tokens': 'delta_w', 'delta_mix_norm_g': 'delta_w', 'delta_w_in': 'delta_w', 'delta_ssm_lambda_re': 'delta_w', 'delta_ssm_lambda_im': 'delta_w', 'delta_ssm_log_dt': 'delta_w', 'delta_ssm_b_re': 'delta_w', 'delta_ssm_b_im': 'delta_w', 'delta_ssm_c_re': 'delta_w', 'delta_ssm_c_im': 'delta_w', 'delta_ssm_d': 'delta_w', 'delta_ssm_w_glu': 'delta_w', 'delta_w_ssm_proj': 'delta_w', 'delta_hgrn_lb_logits': 'delta_w', 'delta_hgrn_norm_g': 'delta_w', 'delta_w_hgrn_proj': 'delta_w', 'delta_w_out': 'delta_w', 'delta_ffn_norm_g': 'delta_w', 'delta_w_up': 'delta_w', 'delta_conv_w': 'delta_w', 'delta_conv_b': 'delta_w', 'delta_w_down': 'delta_w', 'delta_final_norm_g': 'delta_w', 'new_m_meta_tokens': 'new_m', 'new_m_mix_norm_g': 'new_m', 'new_m_w_in': 'new_m', 'new_m_ssm_lambda_re': 'new_m', 'new_m_ssm_lambda_im': 'new_m', 'new_m_ssm_log_dt': 'new_m', 'new_m_ssm_b_re': 'new_m', 'new_m_ssm_b_im': 'new_m', 'new_m_ssm_c_re': 'new_m', 'new_m_ssm_c_im': 'new_m', 'new_m_ssm_d': 'new_m', 'new_m_ssm_w_glu': 'new_m', 'new_m_w_ssm_proj': 'new_m', 'new_m_hgrn_lb_logits': 'new_m', 'new_m_hgrn_norm_g': 'new_m', 'new_m_w_hgrn_proj': 'new_m', 'new_m_w_out': 'new_m', 'new_m_ffn_norm_g': 'new_m', 'new_m_w_up': 'new_m', 'new_m_conv_w': 'new_m', 'new_m_conv_b': 'new_m', 'new_m_w_down': 'new_m', 'new_m_final_norm_g': 'new_m', 'new_v_meta_tokens': 'new_v', 'new_v_mix_norm_g': 'new_v', 'new_v_w_in': 'new_v', 'new_v_ssm_lambda_re': 'new_v', 'new_v_ssm_lambda_im': 'new_v', 'new_v_ssm_log_dt': 'new_v', 'new_v_ssm_b_re': 'new_v', 'new_v_ssm_b_im': 'new_v', 'new_v_ssm_c_re': 'new_v', 'new_v_ssm_c_im': 'new_v', 'new_v_ssm_d': 'new_v', 'new_v_ssm_w_glu': 'new_v', 'new_v_w_ssm_proj': 'new_v', 'new_v_hgrn_lb_logits': 'new_v', 'new_v_hgrn_norm_g': 'new_v', 'new_v_w_hgrn_proj': 'new_v', 'new_v_w_out': 'new_v', 'new_v_ffn_norm_g': 'new_v', 'new_v_w_up': 'new_v', 'new_v_conv_w': 'new_v', 'new_v_conv_b': 'new_v', 'new_v_w_down': 'new_v', 'new_v_final_norm_g': 'new_v'}


def _forward(args):
    return _fwd_reference(*[args[k] for k in FWD_PARAMS])


def _output_shape():
    out = _jax.eval_shape(lambda: _forward(_fwd_setup_inputs(0)))
    return out.shape, out.dtype

N_MICROBATCH = 1
ADAM_LR = 0.001
ADAM_B1 = 0.9
ADAM_B2 = 0.999
ADAM_EPS = 1e-08
ADAM_WD = 0.01
ADAM_STEP = 10
PER_EXAMPLE_BATCH_AXIS = {'x': 0, 'loss_target': 0}
SHARED_INPUTS = []
_WEIGHT_DTYPES = {'meta_tokens': _jnp.float32, 'mix_norm_g': _jnp.float32, 'w_in': _jnp.float32, 'ssm_lambda_re': _jnp.float32, 'ssm_lambda_im': _jnp.float32, 'ssm_log_dt': _jnp.float32, 'ssm_b_re': _jnp.float32, 'ssm_b_im': _jnp.float32, 'ssm_c_re': _jnp.float32, 'ssm_c_im': _jnp.float32, 'ssm_d': _jnp.float32, 'ssm_w_glu': _jnp.float32, 'w_ssm_proj': _jnp.float32, 'hgrn_lb_logits': _jnp.float32, 'hgrn_norm_g': _jnp.float32, 'w_hgrn_proj': _jnp.float32, 'w_out': _jnp.float32, 'ffn_norm_g': _jnp.float32, 'w_up': _jnp.float32, 'conv_w': _jnp.float32, 'conv_b': _jnp.float32, 'w_down': _jnp.float32, 'final_norm_g': _jnp.float32}
MOMENT_SCALE = {'meta_tokens': 3.792657e-03, 'mix_norm_g': 1.997442e-01, 'w_in': 7.473405e-02, 'ssm_lambda_re': 3.823375e-03, 'ssm_lambda_im': 4.413653e-03, 'ssm_log_dt': 2.462904e+00, 'ssm_b_re': 2.594180e-03, 'ssm_b_im': 2.500681e-03, 'ssm_c_re': 3.619992e-03, 'ssm_c_im': 3.620412e-03, 'ssm_d': 5.810545e-02, 'ssm_w_glu': 1.495497e-02, 'w_ssm_proj': 4.989544e-02, 'hgrn_lb_logits': 5.755278e-02, 'hgrn_norm_g': 2.696044e-01, 'w_hgrn_proj': 8.521275e-02, 'w_out': 9.859094e-02, 'ffn_norm_g': 1.771737e-01, 'w_up': 7.547515e-02, 'conv_w': 7.803057e-02, 'conv_b': 7.801190e-02, 'w_down': 1.233463e-01, 'final_norm_g': 6.393162e+01}


def _to_microbatches(a, axis):
    t = _jnp.moveaxis(a, axis, 0)
    t = t.reshape((N_MICROBATCH, t.shape[0] // N_MICROBATCH) + t.shape[1:])
    return _jnp.moveaxis(t, 1, axis + 1)


def setup_inputs(seed: int = 0) -> dict:
    inp = _fwd_setup_inputs(seed)
    key = _jax.random.fold_in(_jax.random.key(seed), 7919)
    shape, _ = _output_shape()
    out = dict(inp)
    out["loss_target"] = _jax.random.normal(_jax.random.fold_in(key, 0), shape, _jnp.float32)
    for i, name in enumerate(TWIN_WEIGHTS):
        w = inp[name].astype(_jnp.float32)
        if MOMENT_SCALE is None:
            s = _jnp.sqrt(_jnp.mean(_jnp.square(w)) + 1e-30)
        else:
            s = MOMENT_SCALE[name]
        km, kv = _jax.random.split(_jax.random.fold_in(key, i + 1))
        out[name] = w
        out["m_" + name] = s * _jax.random.normal(km, w.shape, _jnp.float32)
        out["v_" + name] = (s * s) * _jax.random.uniform(kv, w.shape, _jnp.float32, 0.5, 1.5)
    if N_MICROBATCH > 1:
        for name, axis in PER_EXAMPLE_BATCH_AXIS.items():
            out[name] = _to_microbatches(out[name], axis)
    return {'x': out['x'], 'meta_tokens': out['meta_tokens'], 'mix_norm_g': out['mix_norm_g'], 'w_in': out['w_in'], 'ssm_lambda_re': out['ssm_lambda_re'], 'ssm_lambda_im': out['ssm_lambda_im'], 'ssm_log_dt': out['ssm_log_dt'], 'ssm_b_re': out['ssm_b_re'], 'ssm_b_im': out['ssm_b_im'], 'ssm_c_re': out['ssm_c_re'], 'ssm_c_im': out['ssm_c_im'], 'ssm_d': out['ssm_d'], 'ssm_w_glu': out['ssm_w_glu'], 'w_ssm_proj': out['w_ssm_proj'], 'hgrn_lb_logits': out['hgrn_lb_logits'], 'hgrn_norm_g': out['hgrn_norm_g'], 'w_hgrn_proj': out['w_hgrn_proj'], 'w_out': out['w_out'], 'ffn_norm_g': out['ffn_norm_g'], 'w_up': out['w_up'], 'conv_w': out['conv_w'], 'conv_b': out['conv_b'], 'w_down': out['w_down'], 'final_norm_g': out['final_norm_g'], 'loss_target': out['loss_target'], 'm_meta_tokens': out['m_meta_tokens'], 'm_mix_norm_g': out['m_mix_norm_g'], 'm_w_in': out['m_w_in'], 'm_ssm_lambda_re': out['m_ssm_lambda_re'], 'm_ssm_lambda_im': out['m_ssm_lambda_im'], 'm_ssm_log_dt': out['m_ssm_log_dt'], 'm_ssm_b_re': out['m_ssm_b_re'], 'm_ssm_b_im': out['m_ssm_b_im'], 'm_ssm_c_re': out['m_ssm_c_re'], 'm_ssm_c_im': out['m_ssm_c_im'], 'm_ssm_d': out['m_ssm_d'], 'm_ssm_w_glu': out['m_ssm_w_glu'], 'm_w_ssm_proj': out['m_w_ssm_proj'], 'm_hgrn_lb_logits': out['m_hgrn_lb_logits'], 'm_hgrn_norm_g': out['m_hgrn_norm_g'], 'm_w_hgrn_proj': out['m_w_hgrn_proj'], 'm_w_out': out['m_w_out'], 'm_ffn_norm_g': out['m_ffn_norm_g'], 'm_w_up': out['m_w_up'], 'm_conv_w': out['m_conv_w'], 'm_conv_b': out['m_conv_b'], 'm_w_down': out['m_w_down'], 'm_final_norm_g': out['m_final_norm_g'], 'v_meta_tokens': out['v_meta_tokens'], 'v_mix_norm_g': out['v_mix_norm_g'], 'v_w_in': out['v_w_in'], 'v_ssm_lambda_re': out['v_ssm_lambda_re'], 'v_ssm_lambda_im': out['v_ssm_lambda_im'], 'v_ssm_log_dt': out['v_ssm_log_dt'], 'v_ssm_b_re': out['v_ssm_b_re'], 'v_ssm_b_im': out['v_ssm_b_im'], 'v_ssm_c_re': out['v_ssm_c_re'], 'v_ssm_c_im': out['v_ssm_c_im'], 'v_ssm_d': out['v_ssm_d'], 'v_ssm_w_glu': out['v_ssm_w_glu'], 'v_w_ssm_proj': out['v_w_ssm_proj'], 'v_hgrn_lb_logits': out['v_hgrn_lb_logits'], 'v_hgrn_norm_g': out['v_hgrn_norm_g'], 'v_w_hgrn_proj': out['v_w_hgrn_proj'], 'v_w_out': out['v_w_out'], 'v_ffn_norm_g': out['v_ffn_norm_g'], 'v_w_up': out['v_w_up'], 'v_conv_w': out['v_conv_w'], 'v_conv_b': out['v_conv_b'], 'v_w_down': out['v_w_down'], 'v_final_norm_g': out['v_final_norm_g']}


def _loss(weights, diff, rest, loss_target):
    with _jax.named_scope("forward"):
        args = {**rest, TWIN_DIFF_INPUT: diff, **{k: w.astype(_WEIGHT_DTYPES[k]) for k, w in weights.items()}}
        y = _forward(args)
    with _jax.named_scope("loss_head"):
        err = _jnp.square(y.astype(_jnp.float32) - loss_target)
        return 0.5 * _jnp.sum(_jnp.mean(err, axis=-1)) if err.ndim else 0.5 * err


def _adamw(w, g, m, v):
    m = ADAM_B1 * m + (1.0 - ADAM_B1) * g
    v = ADAM_B2 * v + (1.0 - ADAM_B2) * _jnp.square(g)
    m_hat = m / (1.0 - ADAM_B1 ** ADAM_STEP)
    v_hat = v / (1.0 - ADAM_B2 ** ADAM_STEP)
    delta = -ADAM_LR * (m_hat / (_jnp.sqrt(v_hat) + ADAM_EPS) + ADAM_WD * w)
    return delta, m, v


def reference(x, meta_tokens, mix_norm_g, w_in, ssm_lambda_re, ssm_lambda_im, ssm_log_dt, ssm_b_re, ssm_b_im, ssm_c_re, ssm_c_im, ssm_d, ssm_w_glu, w_ssm_proj, hgrn_lb_logits, hgrn_norm_g, w_hgrn_proj, w_out, ffn_norm_g, w_up, conv_w, conv_b, w_down, final_norm_g, loss_target, m_meta_tokens, m_mix_norm_g, m_w_in, m_ssm_lambda_re, m_ssm_lambda_im, m_ssm_log_dt, m_ssm_b_re, m_ssm_b_im, m_ssm_c_re, m_ssm_c_im, m_ssm_d, m_ssm_w_glu, m_w_ssm_proj, m_hgrn_lb_logits, m_hgrn_norm_g, m_w_hgrn_proj, m_w_out, m_ffn_norm_g, m_w_up, m_conv_w, m_conv_b, m_w_down, m_final_norm_g, v_meta_tokens, v_mix_norm_g, v_w_in, v_ssm_lambda_re, v_ssm_lambda_im, v_ssm_log_dt, v_ssm_b_re, v_ssm_b_im, v_ssm_c_re, v_ssm_c_im, v_ssm_d, v_ssm_w_glu, v_w_ssm_proj, v_hgrn_lb_logits, v_hgrn_norm_g, v_w_hgrn_proj, v_w_out, v_ffn_norm_g, v_w_up, v_conv_w, v_conv_b, v_w_down, v_final_norm_g):
    given = dict(x=x, meta_tokens=meta_tokens, mix_norm_g=mix_norm_g, w_in=w_in, ssm_lambda_re=ssm_lambda_re, ssm_lambda_im=ssm_lambda_im, ssm_log_dt=ssm_log_dt, ssm_b_re=ssm_b_re, ssm_b_im=ssm_b_im, ssm_c_re=ssm_c_re, ssm_c_im=ssm_c_im, ssm_d=ssm_d, ssm_w_glu=ssm_w_glu, w_ssm_proj=w_ssm_proj, hgrn_lb_logits=hgrn_lb_logits, hgrn_norm_g=hgrn_norm_g, w_hgrn_proj=w_hgrn_proj, w_out=w_out, ffn_norm_g=ffn_norm_g, w_up=w_up, conv_w=conv_w, conv_b=conv_b, w_down=w_down, final_norm_g=final_norm_g, loss_target=loss_target, m_meta_tokens=m_meta_tokens, m_mix_norm_g=m_mix_norm_g, m_w_in=m_w_in, m_ssm_lambda_re=m_ssm_lambda_re, m_ssm_lambda_im=m_ssm_lambda_im, m_ssm_log_dt=m_ssm_log_dt, m_ssm_b_re=m_ssm_b_re, m_ssm_b_im=m_ssm_b_im, m_ssm_c_re=m_ssm_c_re, m_ssm_c_im=m_ssm_c_im, m_ssm_d=m_ssm_d, m_ssm_w_glu=m_ssm_w_glu, m_w_ssm_proj=m_w_ssm_proj, m_hgrn_lb_logits=m_hgrn_lb_logits, m_hgrn_norm_g=m_hgrn_norm_g, m_w_hgrn_proj=m_w_hgrn_proj, m_w_out=m_w_out, m_ffn_norm_g=m_ffn_norm_g, m_w_up=m_w_up, m_conv_w=m_conv_w, m_conv_b=m_conv_b, m_w_down=m_w_down, m_final_norm_g=m_final_norm_g, v_meta_tokens=v_meta_tokens, v_mix_norm_g=v_mix_norm_g, v_w_in=v_w_in, v_ssm_lambda_re=v_ssm_lambda_re, v_ssm_lambda_im=v_ssm_lambda_im, v_ssm_log_dt=v_ssm_log_dt, v_ssm_b_re=v_ssm_b_re, v_ssm_b_im=v_ssm_b_im, v_ssm_c_re=v_ssm_c_re, v_ssm_c_im=v_ssm_c_im, v_ssm_d=v_ssm_d, v_ssm_w_glu=v_ssm_w_glu, v_w_ssm_proj=v_w_ssm_proj, v_hgrn_lb_logits=v_hgrn_lb_logits, v_hgrn_norm_g=v_hgrn_norm_g, v_w_hgrn_proj=v_w_hgrn_proj, v_w_out=v_w_out, v_ffn_norm_g=v_ffn_norm_g, v_w_up=v_w_up, v_conv_w=v_conv_w, v_conv_b=v_conv_b, v_w_down=v_w_down, v_final_norm_g=v_final_norm_g)
    weights = {n: given[n] for n in TWIN_WEIGHTS}
    shared = {n: given[n] for n in SHARED_INPUTS}
    per_example = {n: given[n] for n in ['x']}
    grad_fn = _jax.value_and_grad(_loss, argnums=(0, 1))

    def one_microbatch(ex, loss_target):
        ex = dict(ex)
        diff = ex.pop(TWIN_DIFF_INPUT)
        return grad_fn(weights, diff, {**shared, **ex}, loss_target)

    if N_MICROBATCH == 1:
        loss, (grad_w, grad_x) = one_microbatch(per_example, given["loss_target"])
    else:
        def body(carry, xs):
            loss_sum, grad_sum = carry
            l_k, (gw_k, gx_k) = one_microbatch(xs[0], xs[1])
            with _jax.named_scope("update"):
                return (loss_sum + l_k, _jax.tree.map(_jnp.add, grad_sum, gw_k)), gx_k

        init = (_jnp.zeros((), _jnp.float32), _jax.tree.map(_jnp.zeros_like, weights))
        (loss, grad_w), grad_x = _jax.lax.scan(body, init, (per_example, given["loss_target"]))
    with _jax.named_scope("update"):
        delta_w, new_m, new_v = {}, {}, {}
        for n in TWIN_WEIGHTS:
            delta_w[n], new_m[n], new_v[n] = _adamw(weights[n], grad_w[n], given["m_" + n], given["v_" + n])
    return (loss, grad_x, *[grad_w[n] for n in TWIN_WEIGHTS], *[delta_w[n] for n in TWIN_WEIGHTS],
            *[new_m[n] for n in TWIN_WEIGHTS], *[new_v[n] for n in TWIN_WEIGHTS])
```

```python
import functools
import math

import jax
import jax.numpy as jnp
from jax import lax
from jax.experimental import pallas as pl
from jax.experimental.pallas import tpu as pltpu

f32 = jnp.float32
bf16 = jnp.bfloat16

D_MODEL = 1024
N_META = 16
SSM_GROUP = 16
SSM_GROUPS = 64
SSM_STATE = 64
SLAB_GROUPS = 8
N_SLAB = SSM_GROUPS // SLAB_GROUPS
SLAB_CH = SLAB_GROUPS * SSM_GROUP
SLAB_NS = SLAB_GROUPS * SSM_STATE
HEADS = 8
HEAD_DIM = 128
CHUNK = 16
D_FF = 2816
IN_COLS = 7168
EPS = 1e-6
SUBLANES = 8
LANES = 128
N_CHIPS = 4
N_DEV = 8
ADAM_LR, ADAM_B1, ADAM_B2, ADAM_EPS, ADAM_WD, ADAM_STEP = 0.001, 0.9, 0.999, 1e-08, 0.01, 10
MESH = pl.DeviceIdType.MESH
ANY = pl.BlockSpec(memory_space=pl.ANY)

SEG_Q, SEG_F, SEG_I, SEG_OG, SEG_GA, SEG_GB, SEG_U = range(7)
N_SEG = 7


def _tile(n, target, mult=SUBLANES):
    best = None
    for d in range(mult, min(n, target) + 1, mult):
        if n % d == 0:
            best = d
    return n if best is None else best


def _params(*sem):
    return pltpu.CompilerParams(dimension_semantics=sem)


def _sigmoid(x):
    return 1.0 / (1.0 + jnp.exp(-x))


_DIMS = {"nn": (((1,), (0,)), ((), ())), "nt": (((1,), (1,)), ((), ())), "tn": (((0,), (0,)), ((), ()))}


def _mm(name, a, b, dims, grid, a_spec, b_spec, out_shape, out_spec, acc_shape, res=None, res_spec=None):
    nk = grid[2]
    dn = _DIMS[dims]

    def body(*refs):
        if res is None:
            a_ref, b_ref, o_ref, acc = refs
        else:
            a_ref, b_ref, r_ref, o_ref, acc = refs
        k = pl.program_id(2)

        @pl.when(k == 0)
        def _():
            acc[...] = jnp.zeros_like(acc)

        acc[...] += lax.dot_general(a_ref[...].astype(bf16), b_ref[...].astype(bf16), dn, preferred_element_type=f32)

        @pl.when(k == nk - 1)
        def _():
            r = acc[...]
            if res is not None:
                r = r + r_ref[...]
            o_ref[...] = r.astype(o_ref.dtype)

    ins = [a, b] + ([] if res is None else [res])
    specs = [a_spec, b_spec] + ([] if res is None else [res_spec])
    return pl.pallas_call(
        body, name=name, grid=grid, in_specs=specs, out_specs=out_spec, out_shape=out_shape,
        scratch_shapes=[pltpu.VMEM(acc_shape, f32)],
        compiler_params=_params("parallel", "parallel", "arbitrary"),
    )(*ins)


def _mm_rows(name, a, w, dims, out_dtype, tn, res=None, tk=None):
    T, K = a.shape
    N = w.shape[1] if dims == "nn" else w.shape[0]
    tm = _tile(T, 1032)
    tk = K if tk is None else tk
    grid = (T // tm, N // tn, K // tk)
    a_spec = pl.BlockSpec((tm, tk), lambda i, j, k: (i, k))
    if dims == "nn":
        b_spec = pl.BlockSpec((tk, tn), lambda i, j, k: (k, j))
    else:
        b_spec = pl.BlockSpec((tn, tk), lambda i, j, k: (j, k))
    o_spec = pl.BlockSpec((tm, tn), lambda i, j, k: (i, j))
    return _mm(name, a, w, dims, grid, a_spec, b_spec, jax.ShapeDtypeStruct((T, N), out_dtype), o_spec, (tm, tn),
               res=res, res_spec=None if res is None else o_spec)


def _mm_wgrad(name, a, g, tn=None):
    T, K = a.shape
    N = g.shape[1]
    tk = _tile(T, 688)
    tn = N if tn is None else tn
    grid = (1, N // tn, T // tk)
    a_spec = pl.BlockSpec((tk, K), lambda i, j, k: (k, 0))
    g_spec = pl.BlockSpec((tk, tn), lambda i, j, k: (k, j))
    o_spec = pl.BlockSpec((K, tn), lambda i, j, k: (0, j))
    return _mm(name, a, g, "tn", grid, a_spec, g_spec, jax.ShapeDtypeStruct((K, N), f32), o_spec, (K, tn))


def _rmsnorm_fwd(name, x, g):
    T, Dm = x.shape
    tr = _tile(T, 688)

    def body(x_ref, g_ref, z_ref):
        xv = x_ref[...]
        r = lax.rsqrt(jnp.mean(xv * xv, axis=-1, keepdims=True) + EPS)
        z_ref[...] = (xv * r * g_ref[...]).astype(z_ref.dtype)

    return pl.pallas_call(
        body, name=name, grid=(T // tr,),
        in_specs=[pl.BlockSpec((tr, Dm), lambda i: (i, 0)), pl.BlockSpec((1, Dm), lambda i: (0, 0))],
        out_specs=pl.BlockSpec((tr, Dm), lambda i: (i, 0)),
        out_shape=jax.ShapeDtypeStruct((T, Dm), bf16), compiler_params=_params("parallel"),
    )(x, g)


def _rmsnorm_bwd(name, x, g, dz, dres):
    T, Dm = x.shape
    tr = _tile(T, 688)

    def body(x_ref, g_ref, dz_ref, dres_ref, dx_ref, dg_ref):
        xv = x_ref[...]
        r = lax.rsqrt(jnp.mean(xv * xv, axis=-1, keepdims=True) + EPS)
        xn = xv * r
        dzv = dz_ref[...]
        dzg = dzv * g_ref[...]
        dx_ref[...] = dres_ref[...] + r * (dzg - xn * jnp.mean(dzg * xn, axis=-1, keepdims=True))

        @pl.when(pl.program_id(0) == 0)
        def _():
            dg_ref[...] = jnp.zeros_like(dg_ref)

        dg_ref[...] += jnp.sum(dzv * xn, axis=0, keepdims=True)

    row = pl.BlockSpec((tr, Dm), lambda i: (i, 0))
    par = pl.BlockSpec((1, Dm), lambda i: (0, 0))
    return pl.pallas_call(
        body, name=name, grid=(T // tr,), in_specs=[row, par, row, row], out_specs=[row, par],
        out_shape=[jax.ShapeDtypeStruct((T, Dm), f32), jax.ShapeDtypeStruct((1, Dm), f32)],
        compiler_params=_params("arbitrary"),
    )(x, g, dz, dres)


def _glu_fwd(ya0, gl):
    T, Dm = ya0.shape
    tr = _tile(T, 688)

    def body(y_ref, g_ref, o_ref):
        o_ref[...] = (y_ref[...] * _sigmoid(g_ref[...])).astype(o_ref.dtype)

    row = pl.BlockSpec((tr, Dm), lambda i: (i, 0))
    return pl.pallas_call(body, name="glu_fwd", grid=(T // tr,), in_specs=[row, row], out_specs=row,
                          out_shape=jax.ShapeDtypeStruct((T, Dm), bf16), compiler_params=_params("parallel"))(ya0, gl)


def _glu_bwd(dya, ya0, gl):
    T, Dm = ya0.shape
    tr = _tile(T, 688)

    def body(d_ref, y_ref, g_ref, dg_ref, dy_ref):
        s = _sigmoid(g_ref[...])
        d = d_ref[...]
        dg_ref[...] = (d * y_ref[...] * s * (1.0 - s)).astype(dg_ref.dtype)
        dy_ref[...] = d * s

    row = pl.BlockSpec((tr, Dm), lambda i: (i, 0))
    return pl.pallas_call(body, name="glu_bwd", grid=(T // tr,), in_specs=[row, row, row], out_specs=[row, row],
                          out_shape=[jax.ShapeDtypeStruct((T, Dm), bf16), jax.ShapeDtypeStruct((T, Dm), f32)],
                          compiler_params=_params("parallel"))(dya, ya0, gl)


def _merge_fwd(p, pa, pb):
    T, Dm = pa.shape
    tr = _tile(T, 688)

    def body(ga_ref, gb_ref, pa_ref, pb_ref, o_ref):
        o_ref[...] = (_sigmoid(ga_ref[...]) * pa_ref[...] + _sigmoid(gb_ref[...]) * pb_ref[...]).astype(o_ref.dtype)

    row = pl.BlockSpec((tr, Dm), lambda i: (i, 0))
    return pl.pallas_call(
        body, name="merge_fwd", grid=(T // tr,),
        in_specs=[pl.BlockSpec((tr, Dm), lambda i: (i, SEG_GA)), pl.BlockSpec((tr, Dm), lambda i: (i, SEG_GB)), row, row],
        out_specs=row, out_shape=jax.ShapeDtypeStruct((T, Dm), bf16), compiler_params=_params("parallel"),
    )(p, p, pa, pb)


def _merge_bwd(dm, p, pa, pb):
    T, Dm = pa.shape
    tr = _tile(T, 688)

    def body(dm_ref, ga_ref, gb_ref, pa_ref, pb_ref, dpa_ref, dpb_ref, dp_ref):
        d = dm_ref[...]
        sa = _sigmoid(ga_ref[...])
        sb = _sigmoid(gb_ref[...])
        dpa_ref[...] = (d * sa).astype(dpa_ref.dtype)
        dpb_ref[...] = (d * sb).astype(dpb_ref.dtype)
        dp_ref[0] = (d * pa_ref[...] * sa * (1.0 - sa)).astype(dp_ref.dtype)
        dp_ref[1] = (d * pb_ref[...] * sb * (1.0 - sb)).astype(dp_ref.dtype)

    row = pl.BlockSpec((tr, Dm), lambda i: (i, 0))
    return pl.pallas_call(
        body, name="merge_bwd", grid=(T // tr,),
        in_specs=[row, pl.BlockSpec((tr, Dm), lambda i: (i, SEG_GA)), pl.BlockSpec((tr, Dm), lambda i: (i, SEG_GB)), row, row],
        out_specs=[row, row, pl.BlockSpec((2, tr, Dm), lambda i: (SEG_GA // 2, i, 0))],
        out_shape=[jax.ShapeDtypeStruct((T, Dm), bf16), jax.ShapeDtypeStruct((T, Dm), bf16),
                   jax.ShapeDtypeStruct((N_SEG, T, Dm), bf16)],
        compiler_params=_params("parallel"),
    )(dm, p, p, pa, pb)


def _final_loss(h2x, tgt, g):
    T, Dm = h2x.shape
    tr = _tile(T, 512)

    def body(h_ref, t_ref, g_ref, dh_ref, loss_ref, dg_ref):
        hv = h_ref[...]
        r = lax.rsqrt(jnp.mean(hv * hv, axis=-1, keepdims=True) + EPS)
        xn = hv * r
        gv = g_ref[...]
        err = xn * gv - t_ref[...]
        dy = err * (1.0 / Dm)
        dyg = dy * gv
        dh_ref[...] = r * (dyg - xn * jnp.mean(dyg * xn, axis=-1, keepdims=True))

        @pl.when(pl.program_id(0) == 0)
        def _():
            dg_ref[...] = jnp.zeros_like(dg_ref)
            loss_ref[...] = jnp.zeros_like(loss_ref)

        dg_ref[...] += jnp.sum(dy * xn, axis=0, keepdims=True)
        loss_ref[...] += jnp.sum(err * err) * (0.5 / Dm)

    row = pl.BlockSpec((tr, Dm), lambda i: (i, 0))
    par = pl.BlockSpec((1, Dm), lambda i: (0, 0))
    return pl.pallas_call(
        body, name="final_loss", grid=(T // tr,), in_specs=[row, row, par],
        out_specs=[row, pl.BlockSpec((1, LANES), lambda i: (0, 0)), par],
        out_shape=[jax.ShapeDtypeStruct((T, Dm), f32), jax.ShapeDtypeStruct((1, LANES), f32), jax.ShapeDtypeStruct((1, Dm), f32)],
        compiler_params=_params("arbitrary"),
    )(h2x, tgt, g)


def _meta_grad(dh0_meta):
    B = dh0_meta.shape[0]

    def body(d_ref, o_ref):
        acc = d_ref[0]
        for b in range(1, B):
            acc = acc + d_ref[b]
        o_ref[...] = acc

    return pl.pallas_call(body, name="meta_grad", out_shape=jax.ShapeDtypeStruct(dh0_meta.shape[1:], f32))(dh0_meta)


def _shift_down(x, k, row):
    return jnp.where(row >= k, pltpu.roll(x, k, 0), 0.0)


def _shift_up(x, k, row):
    n = x.shape[0]
    return jnp.where(row < n - k, pltpu.roll(x, n - k, 0), 0.0)


def _conv_fwd(up, conv_w, conv_b, B, L):
    tc = 256
    nt = D_FF // tc

    def body(xa_ref, xb_ref, wa_ref, wb_ref, ba_ref, bb_ref, o_ref):
        row = lax.broadcasted_iota(jnp.int32, (L, tc), 0)

        def conv(x_ref, w_ref, b_ref):
            x = x_ref[...]
            return (b_ref[...] + w_ref[0:1, :] * _shift_down(x, 2, row) + w_ref[1:2, :] * _shift_down(x, 1, row)
                    + w_ref[2:3, :] * x)

        a = conv(xa_ref, wa_ref, ba_ref)
        b = conv(xb_ref, wb_ref, bb_ref)
        o_ref[...] = (a * _sigmoid(a) * b).astype(o_ref.dtype)

    return pl.pallas_call(
        body, name="conv_fwd", grid=(B, nt),
        in_specs=[pl.BlockSpec((L, tc), lambda b, j: (b, j)), pl.BlockSpec((L, tc), lambda b, j: (b, j + nt)),
                  pl.BlockSpec((3, tc), lambda b, j: (0, j)), pl.BlockSpec((3, tc), lambda b, j: (0, j + nt)),
                  pl.BlockSpec((1, tc), lambda b, j: (0, j)), pl.BlockSpec((1, tc), lambda b, j: (0, j + nt))],
        out_specs=pl.BlockSpec((L, tc), lambda b, j: (b, j)),
        out_shape=jax.ShapeDtypeStruct((B * L, D_FF), bf16), compiler_params=_params("parallel", "parallel"),
    )(up, up, conv_w, conv_w, conv_b, conv_b)


def _conv_bwd(up, dff, conv_w, conv_b, B, L):
    tc = 256
    nt = D_FF // tc

    def body(xa_ref, xb_ref, d_ref, wa_ref, wb_ref, ba_ref, bb_ref, dup_ref, dw_ref):
        row = lax.broadcasted_iota(jnp.int32, (L, tc), 0)
        xs, pre = [], []
        for x_ref, w_ref, b_ref in ((xa_ref, wa_ref, ba_ref), (xb_ref, wb_ref, bb_ref)):
            x = x_ref[...]
            x1 = _shift_down(x, 1, row)
            x2 = _shift_down(x, 2, row)
            xs.append((x, x1, x2))
            pre.append(b_ref[...] + w_ref[0:1, :] * x2 + w_ref[1:2, :] * x1 + w_ref[2:3, :] * x)
        a, b = pre
        s = _sigmoid(a)
        d = d_ref[...]
        grads = (d * b * s * (1.0 + a * (1.0 - s)), d * a * s)

        @pl.when(pl.program_id(1) == 0)
        def _():
            dw_ref[...] = jnp.zeros_like(dw_ref)

        for h, (gr, (x, x1, x2), w_ref) in enumerate(zip(grads, xs, (wa_ref, wb_ref))):
            dup_ref[h] = (w_ref[2:3, :] * gr + w_ref[1:2, :] * _shift_up(gr, 1, row)
                          + w_ref[0:1, :] * _shift_up(gr, 2, row)).astype(dup_ref.dtype)
            dw_ref[h, 0:1, :] += jnp.sum(gr * x2, axis=0, keepdims=True)
            dw_ref[h, 1:2, :] += jnp.sum(gr * x1, axis=0, keepdims=True)
            dw_ref[h, 2:3, :] += jnp.sum(gr * x, axis=0, keepdims=True)
            dw_ref[h, 3:4, :] += jnp.sum(gr, axis=0, keepdims=True)

    return pl.pallas_call(
        body, name="conv_bwd", grid=(nt, B),
        in_specs=[pl.BlockSpec((L, tc), lambda j, b: (b, j)), pl.BlockSpec((L, tc), lambda j, b: (b, j + nt)),
                  pl.BlockSpec((L, tc), lambda j, b: (b, j)),
                  pl.BlockSpec((3, tc), lambda j, b: (0, j)), pl.BlockSpec((3, tc), lambda j, b: (0, j + nt)),
                  pl.BlockSpec((1, tc), lambda j, b: (0, j)), pl.BlockSpec((1, tc), lambda j, b: (0, j + nt))],
        out_specs=[pl.BlockSpec((2, L, tc), lambda j, b: (0, b, j)), pl.BlockSpec((2, SUBLANES, tc), lambda j, b: (0, 0, j))],
        out_shape=[jax.ShapeDtypeStruct((2, B * L, D_FF), bf16), jax.ShapeDtypeStruct((2, SUBLANES, D_FF), f32)],
        compiler_params=_params("parallel", "arbitrary"),
    )(up, up, dff, conv_w, conv_w, conv_b, conv_b)


GELU_C = math.sqrt(2.0 / math.pi)
GELU_A = 0.044715


def _gelu(x):
    return 0.5 * x * (1.0 + jnp.tanh(GELU_C * (x + GELU_A * x * x * x)))


def _gelu_grad(x):
    t = jnp.tanh(GELU_C * (x + GELU_A * x * x * x))
    return 0.5 * (1.0 + t) + 0.5 * x * (1.0 - t * t) * GELU_C * (1.0 + 3.0 * GELU_A * x * x)


def _cmul_add(xr, xi, ar, ai, sr, si):
    return xr + ar * sr - ai * si, xi + ar * si + ai * sr


def _s5_scan_fwd(s_ref, pw_ref, L):
    ns = SLAB_NS
    row = lax.broadcasted_iota(jnp.int32, (SUBLANES, ns), 0)
    pr = pw_ref[0, 0:SUBLANES, :]
    pi = pw_ref[1, 0:SUBLANES, :]

    def step(i, carry):
        cr, ci = carry
        r0 = pl.multiple_of(i * SUBLANES, SUBLANES)
        xr = s_ref[pl.ds(r0, SUBLANES), 0:ns]
        xi = s_ref[pl.ds(r0, SUBLANES), ns:2 * ns]
        for k in (1, 2, 4):
            xr, xi = _cmul_add(xr, xi, pr[k - 1:k, :], pi[k - 1:k, :], _shift_down(xr, k, row), _shift_down(xi, k, row))
        xr, xi = _cmul_add(xr, xi, pr, pi, cr, ci)
        s_ref[pl.ds(r0, SUBLANES), 0:ns] = xr
        s_ref[pl.ds(r0, SUBLANES), ns:2 * ns] = xi
        return xr[SUBLANES - 1:SUBLANES, :], xi[SUBLANES - 1:SUBLANES, :]

    z = jnp.zeros((1, ns), f32)
    lax.fori_loop(0, L // SUBLANES, step, (z, z))


def _s5_project_in(u_ref, bs_ref, s_ref, L, rc):
    for r in range(0, L, rc):
        s_ref[r:r + rc, :] = jnp.dot(u_ref[r:r + rc, :].astype(bf16), bs_ref[...], preferred_element_type=f32)


def _s5_fwd(p, bs, cs, pw, d_skip, B, L):
    rc = _tile(L, 344)

    def body(u_ref, bs_ref, cs_ref, pw_ref, d_ref, y_ref, s_ref):
        _s5_project_in(u_ref, bs_ref, s_ref, L, rc)
        _s5_scan_fwd(s_ref, pw_ref, L)
        for r in range(0, L, rc):
            ypre = (jnp.dot(s_ref[r:r + rc, :].astype(bf16), cs_ref[...], preferred_element_type=f32)
                    + d_ref[...] * u_ref[r:r + rc, :])
            y_ref[r:r + rc, :] = _gelu(ypre)

    ucol = SEG_U * (D_MODEL // SLAB_CH)
    return pl.pallas_call(
        body, name="s5_fwd", grid=(B, N_SLAB),
        in_specs=[pl.BlockSpec((L, SLAB_CH), lambda b, s: (b, ucol + s)),
                  pl.BlockSpec((None, SLAB_CH, 2 * SLAB_NS), lambda b, s: (s, 0, 0)),
                  pl.BlockSpec((None, 2 * SLAB_NS, SLAB_CH), lambda b, s: (s, 0, 0)),
                  pl.BlockSpec((None, 2, 2 * SUBLANES, SLAB_NS), lambda b, s: (s, 0, 0, 0)),
                  pl.BlockSpec((1, SLAB_CH), lambda b, s: (0, s))],
        out_specs=pl.BlockSpec((L, SLAB_CH), lambda b, s: (b, s)),
        out_shape=jax.ShapeDtypeStruct((B * L, D_MODEL), f32),
        scratch_shapes=[pltpu.VMEM((L, 2 * SLAB_NS), f32)],
        compiler_params=_params("parallel", "parallel"),
    )(p, bs, cs, pw, d_skip)


def _s5_bwd(p, dya0, dp, bs, cs, pw, d_skip, B, L):
    rc = _tile(L, 344)
    ns = SLAB_NS
    nt = L // SUBLANES

    def body(u_ref, dy_ref, dp_in, bs_ref, cs_ref, pw_ref, d_ref, du_ref, dbs_ref, dcs_ref, da_ref, dd_ref,
             s_ref, lam_ref, dyp_ref):
        del dp_in
        b = pl.program_id(1)

        @pl.when(b == 0)
        def _():
            dbs_ref[...] = jnp.zeros_like(dbs_ref)
            dcs_ref[...] = jnp.zeros_like(dcs_ref)
            da_ref[...] = jnp.zeros_like(da_ref)
            dd_ref[...] = jnp.zeros_like(dd_ref)

        _s5_project_in(u_ref, bs_ref, s_ref, L, rc)
        _s5_scan_fwd(s_ref, pw_ref, L)
        for r in range(0, L, rc):
            u = u_ref[r:r + rc, :]
            sb = s_ref[r:r + rc, :].astype(bf16)
            ypre = jnp.dot(sb, cs_ref[...], preferred_element_type=f32) + d_ref[...] * u
            dyp = dy_ref[r:r + rc, :] * _gelu_grad(ypre)
            dyp_ref[r:r + rc, :] = dyp
            dd_ref[...] += jnp.sum(dyp * u, axis=0, keepdims=True)
            dypb = dyp.astype(bf16)
            dcs_ref[...] += lax.dot_general(sb, dypb, _DIMS["tn"], preferred_element_type=f32)
            lam_ref[r:r + rc, :] = lax.dot_general(dypb, cs_ref[...], _DIMS["nt"], preferred_element_type=f32)

        row = lax.broadcasted_iota(jnp.int32, (SUBLANES, ns), 0)
        pr = pw_ref[0, 0:SUBLANES, :]
        pi = -pw_ref[1, 0:SUBLANES, :]
        qr = pw_ref[0, SUBLANES:2 * SUBLANES, :]
        qi = -pw_ref[1, SUBLANES:2 * SUBLANES, :]

        def step(j, carry):
            cr, ci, ar, ai = carry
            i = nt - 1 - j
            r0 = pl.multiple_of(i * SUBLANES, SUBLANES)
            xr = lam_ref[pl.ds(r0, SUBLANES), 0:ns]
            xi = lam_ref[pl.ds(r0, SUBLANES), ns:2 * ns]
            for k in (1, 2, 4):
                xr, xi = _cmul_add(xr, xi, pr[k - 1:k, :], pi[k - 1:k, :], _shift_up(xr, k, row), _shift_up(xi, k, row))
            xr, xi = _cmul_add(xr, xi, qr, qi, cr, ci)
            lam_ref[pl.ds(r0, SUBLANES), 0:ns] = xr
            lam_ref[pl.ds(r0, SUBLANES), ns:2 * ns] = xi
            rp = pl.multiple_of(jnp.maximum(i - 1, 0) * SUBLANES, SUBLANES)
            live = jnp.where(i > 0, 1.0, 0.0)
            lr_ = s_ref[pl.ds(rp + SUBLANES - 1, 1), 0:ns] * live
            li_ = s_ref[pl.ds(rp + SUBLANES - 1, 1), ns:2 * ns] * live
            spr = jnp.where(row == 0, lr_, pltpu.roll(s_ref[pl.ds(r0, SUBLANES), 0:ns], 1, 0))
            spi = jnp.where(row == 0, li_, pltpu.roll(s_ref[pl.ds(r0, SUBLANES), ns:2 * ns], 1, 0))
            ar = ar + xr * spr + xi * spi
            ai = ai + xi * spr - xr * spi
            return xr[0:1, :], xi[0:1, :], ar, ai

        z1 = jnp.zeros((1, ns), f32)
        z8 = jnp.zeros((SUBLANES, ns), f32)
        _, _, ar, ai = lax.fori_loop(0, nt, step, (z1, z1, z8, z8))
        da_ref[0:1, :] += jnp.sum(ar, axis=0, keepdims=True)
        da_ref[1:2, :] += jnp.sum(ai, axis=0, keepdims=True)

        for r in range(0, L, rc):
            lamb = lam_ref[r:r + rc, :].astype(bf16)
            dbs_ref[...] += lax.dot_general(u_ref[r:r + rc, :].astype(bf16), lamb, _DIMS["tn"], preferred_element_type=f32)
            du = (lax.dot_general(lamb, bs_ref[...], _DIMS["nt"], preferred_element_type=f32)
                  + d_ref[...] * dyp_ref[r:r + rc, :])
            du_ref[r:r + rc, :] = du.astype(du_ref.dtype)

    ucol = SEG_U * (D_MODEL // SLAB_CH)
    T = B * L
    return pl.pallas_call(
        body, name="s5_bwd", grid=(N_SLAB, B),
        in_specs=[pl.BlockSpec((L, SLAB_CH), lambda s, b: (b, ucol + s)),
                  pl.BlockSpec((L, SLAB_CH), lambda s, b: (b, s)),
                  ANY,
                  pl.BlockSpec((None, SLAB_CH, 2 * SLAB_NS), lambda s, b: (s, 0, 0)),
                  pl.BlockSpec((None, 2 * SLAB_NS, SLAB_CH), lambda s, b: (s, 0, 0)),
                  pl.BlockSpec((None, 2, 2 * SUBLANES, SLAB_NS), lambda s, b: (s, 0, 0, 0)),
                  pl.BlockSpec((1, SLAB_CH), lambda s, b: (0, s))],
        out_specs=[pl.BlockSpec((None, L, SLAB_CH), lambda s, b: (SEG_U, b, s)),
                   pl.BlockSpec((None, SLAB_CH, 2 * SLAB_NS), lambda s, b: (s, 0, 0)),
                   pl.BlockSpec((None, 2 * SLAB_NS, SLAB_CH), lambda s, b: (s, 0, 0)),
                   pl.BlockSpec((None, 2, SLAB_NS), lambda s, b: (s, 0, 0)),
                   pl.BlockSpec((1, SLAB_CH), lambda s, b: (0, s))],
        out_shape=[jax.ShapeDtypeStruct((N_SEG, T, D_MODEL), bf16),
                   jax.ShapeDtypeStruct((N_SLAB, SLAB_CH, 2 * SLAB_NS), f32),
                   jax.ShapeDtypeStruct((N_SLAB, 2 * SLAB_NS, SLAB_CH), f32),
                   jax.ShapeDtypeStruct((N_SLAB, 2, SLAB_NS), f32),
                   jax.ShapeDtypeStruct((1, D_MODEL), f32)],
        scratch_shapes=[pltpu.VMEM((L, 2 * SLAB_NS), f32), pltpu.VMEM((L, 2 * SLAB_NS), f32), pltpu.VMEM((L, SLAB_CH), f32)],
        input_output_aliases={2: 0},
        compiler_params=_params("parallel", "arbitrary"),
    )(p, dya0, dp, bs, cs, pw, d_skip)


def _dotb(a, b, dims="nn"):
    return lax.dot_general(a.astype(bf16), b.astype(bf16), _DIMS[dims], preferred_element_type=f32)


def _dot_exact(a, b):
    return jnp.dot(a, b, preferred_element_type=f32, precision=lax.Precision.HIGHEST)


def _hgrn_gates(fl, lb):
    sg = _sigmoid(fl)
    f = lb + (1.0 - lb) * sg
    return sg, f


def _hgrn_chunk(q, fl, lb, tri):
    sg, f = _hgrn_gates(fl, lb)
    cum = _dot_exact(tri, jnp.log(f))
    c_end = cum[CHUNK - 1:CHUNK, :]
    e = jnp.exp(cum)
    em = jnp.exp(-cum)
    eo = jnp.exp(c_end - cum)
    k = 1.0 - f
    return sg, f, e, em, eo, q * e, k * em, k * eo, jnp.exp(c_end)


def _hgrn_masks():
    r = lax.broadcasted_iota(jnp.int32, (CHUNK, CHUNK), 0)
    c = lax.broadcasted_iota(jnp.int32, (CHUNK, CHUNK), 1)
    causal = c <= r
    return causal, jnp.where(causal, 1.0, 0.0).astype(f32), jnp.where(r <= c, 1.0, 0.0).astype(f32)


def _hgrn_specs(L, order):
    hb = D_MODEL // HEAD_DIM

    def spec(seg):
        if order == "bh":
            return pl.BlockSpec((L, HEAD_DIM), lambda b, h: (b, seg * hb + h))
        return pl.BlockSpec((L, HEAD_DIM), lambda h, b: (b, seg * hb + h))

    return [spec(SEG_Q), spec(SEG_F), spec(SEG_I), spec(SEG_OG)]


def _hgrn_fwd(p, lb, norm_g, B, L):
    nc = L // CHUNK

    def body(q_ref, f_ref, v_ref, og_ref, lb_ref, ng_ref, y_ref):
        causal, tri, _ = _hgrn_masks()
        lbv = lb_ref[...]
        ngv = ng_ref[...]

        def step(c, st):
            r0 = pl.multiple_of(c * CHUNK, CHUNK)
            rows = pl.ds(r0, CHUNK)
            v = v_ref[rows, :]
            og = og_ref[rows, :]
            _, _, _, _, _, qt, kt, ko, dec = _hgrn_chunk(q_ref[rows, :], f_ref[rows, :], lbv, tri)
            pm = jnp.where(causal, _dotb(qt, kt, "nt"), 0.0)
            o = _dotb(pm, v) + _dotb(qt, st, "nt")
            st = st * dec + _dotb(v, ko, "tn")
            on = o * lax.rsqrt(jnp.mean(o * o, axis=-1, keepdims=True) + EPS) * ngv
            y_ref[rows, :] = (on * og * _sigmoid(og)).astype(y_ref.dtype)
            return st

        lax.fori_loop(0, nc, step, jnp.zeros((HEAD_DIM, HEAD_DIM), f32))

    return pl.pallas_call(
        body, name="hgrn_fwd", grid=(B, HEADS),
        in_specs=_hgrn_specs(L, "bh") + [pl.BlockSpec((1, HEAD_DIM), lambda b, h: (0, h)),
                                          pl.BlockSpec((1, HEAD_DIM), lambda b, h: (0, 0))],
        out_specs=pl.BlockSpec((L, HEAD_DIM), lambda b, h: (b, h)),
        out_shape=jax.ShapeDtypeStruct((B * L, D_MODEL), bf16),
        compiler_params=_params("parallel", "parallel"),
    )(p, p, p, p, lb, norm_g)


def _hgrn_bwd(p, dyb, dp, lb, norm_g, B, L):
    nc = L // CHUNK

    def body(q_ref, f_ref, v_ref, og_ref, dy_ref, dp_in, lb_ref, ng_ref, dseg_ref, dlb_ref, dng_ref, st_ref):
        del dp_in
        causal, tri, tri_t = _hgrn_masks()
        lbv = lb_ref[...]
        ngv = ng_ref[...]
        row = lax.broadcasted_iota(jnp.int32, (CHUNK, HEAD_DIM), 0)

        @pl.when(pl.program_id(1) == 0)
        def _():
            dlb_ref[...] = jnp.zeros_like(dlb_ref)

        @pl.when((pl.program_id(0) == 0) & (pl.program_id(1) == 0))
        def _():
            dng_ref[...] = jnp.zeros_like(dng_ref)

        def fwd_step(c, st):
            rows = pl.ds(pl.multiple_of(c * CHUNK, CHUNK), CHUNK)
            st_ref[c] = st
            _, _, _, _, _, _, _, ko, dec = _hgrn_chunk(q_ref[rows, :], f_ref[rows, :], lbv, tri)
            return st * dec + _dotb(v_ref[rows, :], ko, "tn")

        lax.fori_loop(0, nc, fwd_step, jnp.zeros((HEAD_DIM, HEAD_DIM), f32))

        def bwd_step(j, carry):
            dst, dlb, dng = carry
            c = nc - 1 - j
            rows = pl.ds(pl.multiple_of(c * CHUNK, CHUNK), CHUNK)
            v = v_ref[rows, :]
            og = og_ref[rows, :]
            stp = st_ref[c]
            sg, f, e, em, eo, qt, kt, ko, dec = _hgrn_chunk(q_ref[rows, :], f_ref[rows, :], lbv, tri)
            pm = jnp.where(causal, _dotb(qt, kt, "nt"), 0.0)
            o = _dotb(pm, v) + _dotb(qt, stp, "nt")
            rs = lax.rsqrt(jnp.mean(o * o, axis=-1, keepdims=True) + EPS)
            xn = o * rs
            so = _sigmoid(og)
            dy = dy_ref[rows, :]
            dog = dy * xn * ngv * so * (1.0 + og * (1.0 - so))
            don = dy * og * so
            dng = dng + jnp.sum(don * xn, axis=0, keepdims=True)
            dxo = don * ngv
            do = rs * (dxo - xn * jnp.mean(dxo * xn, axis=-1, keepdims=True))
            dpm = jnp.where(causal, _dotb(do, v, "nt"), 0.0)
            dqt = _dotb(dpm, kt) + _dotb(do, stp)
            dkt = _dotb(dpm, qt, "tn")
            dv = _dotb(pm, do, "tn") + _dotb(ko, dst, "nt")
            dko = _dotb(v, dst)
            ddec = jnp.sum(dst * stp, axis=0, keepdims=True)
            dst = dst * dec + _dotb(do, qt, "tn")
            dq = dqt * e
            dko_ko = dko * ko
            dcum = dqt * qt - dkt * kt - dko_ko
            dcend = jnp.sum(dko_ko, axis=0, keepdims=True) + ddec * dec
            dcum = dcum + jnp.where(row == CHUNK - 1, dcend, 0.0)
            dk = dkt * em + dko * eo
            df = _dot_exact(tri_t, dcum) / f - dk
            dfl = df * (1.0 - lbv) * sg * (1.0 - sg)
            dlb = dlb + jnp.sum(df * (1.0 - sg), axis=0, keepdims=True)
            dseg_ref[SEG_Q, rows, :] = dq.astype(dseg_ref.dtype)
            dseg_ref[SEG_F, rows, :] = dfl.astype(dseg_ref.dtype)
            dseg_ref[SEG_I, rows, :] = dv.astype(dseg_ref.dtype)
            dseg_ref[SEG_OG, rows, :] = dog.astype(dseg_ref.dtype)
            return dst, dlb, dng

        z = jnp.zeros((1, HEAD_DIM), f32)
        _, dlb, dng = lax.fori_loop(0, nc, bwd_step, (jnp.zeros((HEAD_DIM, HEAD_DIM), f32), z, z))
        dlb_ref[...] += dlb
        dng_ref[...] += dng

    T = B * L
    return pl.pallas_call(
        body, name="hgrn_bwd", grid=(HEADS, B),
        in_specs=_hgrn_specs(L, "hb") + [pl.BlockSpec((L, HEAD_DIM), lambda h, b: (b, h)), ANY,
                                          pl.BlockSpec((1, HEAD_DIM), lambda h, b: (0, h)),
                                          pl.BlockSpec((1, HEAD_DIM), lambda h, b: (0, 0))],
        out_specs=[pl.BlockSpec((4, L, HEAD_DIM), lambda h, b: (0, b, h)),
                   pl.BlockSpec((1, HEAD_DIM), lambda h, b: (0, h)),
                   pl.BlockSpec((1, HEAD_DIM), lambda h, b: (0, 0))],
        out_shape=[jax.ShapeDtypeStruct((N_SEG, T, D_MODEL), bf16), jax.ShapeDtypeStruct((1, D_MODEL), f32),
                   jax.ShapeDtypeStruct((1, HEAD_DIM), f32)],
        scratch_shapes=[pltpu.VMEM((nc, HEAD_DIM, HEAD_DIM), f32)],
        input_output_aliases={5: 0},
        compiler_params=_params("arbitrary", "arbitrary"),
    )(p, p, p, p, dyb, dp, lb, norm_g)


def _dz1(dp, w_in_phys):
    _, T, Dm = dp.shape
    tm = _tile(T, 1032)
    return _mm("dz1", dp, w_in_phys, "nt", (T // tm, 1, N_SEG),
               pl.BlockSpec((None, tm, Dm), lambda i, j, k: (k, i, 0)),
               pl.BlockSpec((Dm, Dm), lambda i, j, k: (0, k)),
               jax.ShapeDtypeStruct((T, Dm), f32), pl.BlockSpec((tm, Dm), lambda i, j, k: (i, 0)), (tm, Dm))


def _dw_in(z1, dp):
    _, T, Dm = dp.shape
    tn = 256
    per_seg = Dm // tn
    per_chip = IN_COLS // N_CHIPS // tn
    tk = _tile(T, 1376)

    def out_idx(i, j, k):
        logical = ((j // per_seg + 1) % N_SEG) * per_seg + j % per_seg
        return (logical // per_chip, 0, logical % per_chip)

    return _mm("dw_in", z1, dp, "tn", (1, IN_COLS // tn, T // tk),
               pl.BlockSpec((tk, Dm), lambda i, j, k: (k, 0)),
               pl.BlockSpec((None, tk, tn), lambda i, j, k: (j // per_seg, k, j % per_seg)),
               jax.ShapeDtypeStruct((N_CHIPS, Dm, IN_COLS // N_CHIPS), f32),
               pl.BlockSpec((None, Dm, tn), out_idx), (Dm, tn))


def _dz2(dup, w_up):
    _, T, _ = dup.shape
    tm = _tile(T, 1032)
    tk = D_FF // 2
    return _mm("dz2", dup, w_up, "nt", (T // tm, 1, 4),
               pl.BlockSpec((None, tm, tk), lambda i, j, k: (k // 2, i, k % 2)),
               pl.BlockSpec((D_MODEL, tk), lambda i, j, k: (0, k)),
               jax.ShapeDtypeStruct((T, D_MODEL), f32), pl.BlockSpec((tm, D_MODEL), lambda i, j, k: (i, 0)), (tm, D_MODEL))


def _dw_up(z2, dup):
    _, T, _ = dup.shape
    tn = D_FF // 2
    tk = _tile(T, 688)
    return _mm("dw_up", z2, dup, "tn", (1, N_CHIPS, T // tk),
               pl.BlockSpec((tk, D_MODEL), lambda i, j, k: (k, 0)),
               pl.BlockSpec((None, tk, tn), lambda i, j, k: (j // 2, k, j % 2)),
               jax.ShapeDtypeStruct((N_CHIPS, D_MODEL, tn), f32),
               pl.BlockSpec((None, D_MODEL, tn), lambda i, j, k: (j, 0, 0)), (D_MODEL, tn))


def _place():
    x, y, c = lax.axis_index("x"), lax.axis_index("y"), lax.axis_index("c")
    chips = [(1 - x, y), (x, 1 - y), (1 - x, 1 - y)]
    return x, y, c, chips


def _allgather_chips(arrs):
    n = len(arrs)

    def body(*refs):
        ins, outs = refs[:n], refs[n:2 * n]
        send, recv, local = refs[2 * n:]
        x, y, c, chips = _place()
        me = 2 * x + y

        def copy(a, k, slot):
            px, py = chips[k]
            return pltpu.make_async_remote_copy(src_ref=ins[a], dst_ref=outs[a].at[slot], send_sem=send.at[3 * a + k],
                                                recv_sem=recv.at[3 * a + k], device_id=(px, py, c), device_id_type=MESH)

        for a in range(n):
            pltpu.make_async_copy(ins[a], outs[a].at[me], local.at[a]).start()
            for k in range(3):
                copy(a, k, me).start()
        for a in range(n):
            for k, (px, py) in enumerate(chips):
                copy(a, k, 2 * px + py).wait_recv()
        for a in range(n):
            pltpu.make_async_copy(ins[a], outs[a].at[me], local.at[a]).wait()
            for k in range(3):
                copy(a, k, me).wait_send()

    return pl.pallas_call(
        body, name="allgather_chips", in_specs=[ANY] * n, out_specs=[ANY] * n,
        out_shape=[jax.ShapeDtypeStruct((N_CHIPS,) + a.shape, a.dtype) for a in arrs],
        scratch_shapes=[pltpu.SemaphoreType.DMA((3 * n,)), pltpu.SemaphoreType.DMA((3 * n,)), pltpu.SemaphoreType.DMA((n,))],
    )(*arrs)


def _sibling_halves(parts):
    n = len(parts)

    def body(*refs):
        ins, outs = refs[:n], refs[n:2 * n]
        send, recv = refs[2 * n:]
        x, y, c, _ = _place()

        def copy(a):
            rh = ins[a].shape[1] // 2
            return pltpu.make_async_remote_copy(src_ref=ins[a].at[:, pl.ds((1 - c) * rh, rh), :], dst_ref=outs[a],
                                                send_sem=send.at[a], recv_sem=recv.at[a], device_id=(x, y, 1 - c),
                                                device_id_type=MESH)

        for a in range(n):
            copy(a).start()
        for a in range(n):
            copy(a).wait_recv()
        for a in range(n):
            copy(a).wait_send()

    return pl.pallas_call(
        body, name="sibling_halves", in_specs=[ANY] * n, out_specs=[ANY] * n,
        out_shape=[jax.ShapeDtypeStruct((a.shape[0], a.shape[1] // 2, a.shape[2]), a.dtype) for a in parts],
        scratch_shapes=[pltpu.SemaphoreType.DMA((n,)), pltpu.SemaphoreType.DMA((n,))],
    )(*parts)


def _add_own_half(name, part, got, core):
    nchip, R, C = part.shape
    rh = R // 2
    tr = _tile(rh, 256)
    nt = rh // tr

    def body(core_ref, a_ref, b_ref, o_ref):
        del core_ref
        o_ref[...] = a_ref[...] + b_ref[...]

    return pl.pallas_call(
        body, name=name,
        grid_spec=pltpu.PrefetchScalarGridSpec(
            num_scalar_prefetch=1, grid=(nchip, nt),
            in_specs=[pl.BlockSpec((None, tr, C), lambda j, i, core_ref: (j, core_ref[0] * nt + i, 0)),
                      pl.BlockSpec((None, tr, C), lambda j, i, core_ref: (j, i, 0))],
            out_specs=pl.BlockSpec((None, tr, C), lambda j, i, core_ref: (j, i, 0))),
        out_shape=jax.ShapeDtypeStruct((nchip, rh, C), f32), compiler_params=_params("parallel", "parallel"),
    )(core, part, got)


def _chip_exchange(sums):
    n = len(sums)

    def body(*refs):
        ins, outs = refs[:n], refs[n:2 * n]
        send, recv, local = refs[2 * n:]
        x, y, c, chips = _place()
        me = 2 * x + y

        def copy(a, k):
            px, py = chips[k]
            return pltpu.make_async_remote_copy(src_ref=ins[a].at[2 * px + py], dst_ref=outs[a].at[me], send_sem=send.at[3 * a + k],
                                                recv_sem=recv.at[3 * a + k], device_id=(px, py, c), device_id_type=MESH)

        def landed(a, k):
            px, py = chips[k]
            return pltpu.make_async_remote_copy(src_ref=ins[a].at[me], dst_ref=outs[a].at[2 * px + py], send_sem=send.at[3 * a + k],
                                                recv_sem=recv.at[3 * a + k], device_id=(px, py, c), device_id_type=MESH)

        for a in range(n):
            pltpu.make_async_copy(ins[a].at[me], outs[a].at[me], local.at[a]).start()
            for k in range(3):
                copy(a, k).start()
        for a in range(n):
            for k in range(3):
                landed(a, k).wait_recv()
        for a in range(n):
            pltpu.make_async_copy(ins[a].at[me], outs[a].at[me], local.at[a]).wait()
            for k in range(3):
                copy(a, k).wait_send()

    return pl.pallas_call(
        body, name="chip_exchange", in_specs=[ANY] * n, out_specs=[ANY] * n,
        out_shape=[jax.ShapeDtypeStruct(a.shape, a.dtype) for a in sums],
        scratch_shapes=[pltpu.SemaphoreType.DMA((3 * n,)), pltpu.SemaphoreType.DMA((3 * n,)), pltpu.SemaphoreType.DMA((n,))],
    )(*sums)


def _sum_slots(name, slots):
    ns, R, C = slots.shape
    tr = _tile(R, 256)

    def body(s_ref, o_ref):
        acc = s_ref[0]
        for j in range(1, ns):
            acc = acc + s_ref[j]
        o_ref[...] = acc

    return pl.pallas_call(
        body, name=name, grid=(R // tr,), in_specs=[pl.BlockSpec((ns, tr, C), lambda i: (0, i, 0))],
        out_specs=pl.BlockSpec((tr, C), lambda i: (i, 0)), out_shape=jax.ShapeDtypeStruct((R, C), f32),
        compiler_params=_params("parallel"),
    )(slots)


def _sibling_join(halves):
    n = len(halves)

    def body(*refs):
        ins, outs = refs[:n], refs[n:2 * n]
        send, recv, local = refs[2 * n:]
        x, y, c, _ = _place()

        def rows(a, core):
            rh = ins[a].shape[0]
            return outs[a].at[pl.ds(core * rh, rh), :]

        def copy(a, core):
            return pltpu.make_async_remote_copy(src_ref=ins[a], dst_ref=rows(a, core), send_sem=send.at[a], recv_sem=recv.at[a],
                                                device_id=(x, y, 1 - c), device_id_type=MESH)

        for a in range(n):
            pltpu.make_async_copy(ins[a], rows(a, c), local.at[a]).start()
            copy(a, c).start()
        for a in range(n):
            copy(a, 1 - c).wait_recv()
        for a in range(n):
            pltpu.make_async_copy(ins[a], rows(a, c), local.at[a]).wait()
            copy(a, c).wait_send()

    return pl.pallas_call(
        body, name="sibling_join", in_specs=[ANY] * n, out_specs=[ANY] * n,
        out_shape=[jax.ShapeDtypeStruct((2 * a.shape[0], a.shape[1]), a.dtype) for a in halves],
        scratch_shapes=[pltpu.SemaphoreType.DMA((n,)), pltpu.SemaphoreType.DMA((n,)), pltpu.SemaphoreType.DMA((n,))],
    )(*halves)


def _allgather_devices(v):
    def body(v_ref, out_ref, send, recv, local):
        x, y, c, _ = _place()
        me = 4 * x + 2 * y + c

        def peer(k):
            return (1 - x if k & 4 else x, 1 - y if k & 2 else y, 1 - c if k & 1 else c)

        def copy(k, slot):
            return pltpu.make_async_remote_copy(src_ref=v_ref, dst_ref=out_ref.at[slot], send_sem=send.at[k - 1],
                                                recv_sem=recv.at[k - 1], device_id=peer(k), device_id_type=MESH)

        own = pltpu.make_async_copy(v_ref, out_ref.at[me], local)
        own.start()
        for k in range(1, N_DEV):
            copy(k, me).start()
        for k in range(1, N_DEV):
            px, py, pc = peer(k)
            copy(k, 4 * px + 2 * py + pc).wait_recv()
        own.wait()
        for k in range(1, N_DEV):
            copy(k, me).wait_send()

    return pl.pallas_call(
        body, name="allgather_devices", in_specs=[ANY], out_specs=ANY,
        out_shape=jax.ShapeDtypeStruct((N_DEV,) + v.shape, v.dtype),
        scratch_shapes=[pltpu.SemaphoreType.DMA((N_DEV - 1,)), pltpu.SemaphoreType.DMA((N_DEV - 1,)), pltpu.SemaphoreType.DMA],
    )(v)


def _adamw(name, w, g, m, v):
    R, C = w.shape
    tr = _tile(R, 256)
    c1 = 1.0 / (1.0 - ADAM_B1 ** ADAM_STEP)
    c2 = 1.0 / (1.0 - ADAM_B2 ** ADAM_STEP)

    def body(w_ref, g_ref, m_ref, v_ref, d_ref, nm_ref, nv_ref):
        gv = g_ref[...]
        nm = ADAM_B1 * m_ref[...] + (1.0 - ADAM_B1) * gv
        nv = ADAM_B2 * v_ref[...] + (1.0 - ADAM_B2) * gv * gv
        d_ref[...] = -ADAM_LR * ((nm * c1) / (jnp.sqrt(nv * c2) + ADAM_EPS) + ADAM_WD * w_ref[...])
        nm_ref[...] = nm
        nv_ref[...] = nv

    row = pl.BlockSpec((tr, C), lambda i: (i, 0))
    sh = jax.ShapeDtypeStruct((R, C), f32)
    return pl.pallas_call(body, name=name, grid=(R // tr,), in_specs=[row] * 4, out_specs=[row] * 3,
                          out_shape=[sh, sh, sh], compiler_params=_params("parallel"))(w, g, m, v)


def _zoh(lr, li, log_dt, b_re, b_im):
    dt = jnp.exp(log_dt)[:, None]
    mag = jnp.exp(lr * dt)
    ab_re = mag * jnp.cos(li * dt)
    ab_im = mag * jnp.sin(li * dt)
    den = lr * lr + li * li
    nr = ab_re - 1.0
    coef_re = (nr * lr + ab_im * li) / den
    coef_im = (ab_im * lr - nr * li) / den
    bb_re = coef_re[..., None] * b_re - coef_im[..., None] * b_im
    bb_im = coef_re[..., None] * b_im + coef_im[..., None] * b_re
    return ab_re, ab_im, bb_re, bb_im


def _s5_tables(ab_re, ab_im, bb_re, bb_im, c_re, c_im):
    eye = jnp.eye(SLAB_GROUPS, dtype=f32)

    def blk_in(bb):
        return jnp.einsum("sgph,gk->sghkp", bb.reshape(N_SLAB, SLAB_GROUPS, SSM_STATE, SSM_GROUP), eye).reshape(
            N_SLAB, SLAB_CH, SLAB_NS)

    def blk_out(cc):
        return jnp.einsum("sghp,gk->skpgh", cc.reshape(N_SLAB, SLAB_GROUPS, SSM_GROUP, SSM_STATE), eye).reshape(
            N_SLAB, SLAB_NS, SLAB_CH)

    bs = jnp.concatenate([blk_in(bb_re), blk_in(bb_im)], axis=2).astype(bf16)
    cs = jnp.concatenate([blk_out(c_re), blk_out(-c_im)], axis=1).astype(bf16)
    pr, pi = [ab_re], [ab_im]
    for _ in range(SUBLANES - 1):
        pr, pi = pr + [pr[-1] * ab_re - pi[-1] * ab_im], pi + [pr[-1] * ab_im + pi[-1] * ab_re]
    pw = jnp.stack([jnp.stack(pr + pr[::-1]), jnp.stack(pi + pi[::-1])])
    pw = pw.reshape(2, 2 * SUBLANES, N_SLAB, SLAB_NS).transpose(2, 0, 1, 3)
    return bs, cs, pw


def _s5_table_grads(dbs, dcs, da):
    eye = jnp.eye(SLAB_GROUPS, dtype=f32)
    d6 = dbs.reshape(N_SLAB, SLAB_GROUPS, SSM_GROUP, 2, SLAB_GROUPS, SSM_STATE)
    dbb = jnp.einsum("sghrkp,gk->rsgph", d6, eye).reshape(2, SSM_GROUPS, SSM_STATE, SSM_GROUP)
    c6 = dcs.reshape(N_SLAB, 2, SLAB_GROUPS, SSM_STATE, SLAB_GROUPS, SSM_GROUP)
    dcc = jnp.einsum("srkpgh,gk->rsghp", c6, eye).reshape(2, SSM_GROUPS, SSM_GROUP, SSM_STATE)
    dab = da.transpose(1, 0, 2).reshape(2, SSM_GROUPS, SSM_STATE)
    return dab[0], dab[1], dbb[0], dbb[1], dcc[0], -dcc[1]


SMALL = ["mix_norm_g", "ssm_lambda_re", "ssm_lambda_im", "ssm_log_dt", "ssm_b_re", "ssm_b_im", "ssm_c_re", "ssm_c_im",
         "ssm_d", "hgrn_lb_logits", "hgrn_norm_g", "ffn_norm_g", "conv_b", "final_norm_g"]
SHARDED_SMALL = ["meta_tokens", "conv_w"]
BIG = ["w_in", "ssm_w_glu", "w_ssm_proj", "w_hgrn_proj", "w_out", "w_up", "w_down"]
WEIGHTS = ['meta_tokens', 'mix_norm_g', 'w_in', 'ssm_lambda_re', 'ssm_lambda_im', 'ssm_log_dt', 'ssm_b_re', 'ssm_b_im',
           'ssm_c_re', 'ssm_c_im', 'ssm_d', 'ssm_w_glu', 'w_ssm_proj', 'hgrn_lb_logits', 'hgrn_norm_g', 'w_hgrn_proj',
           'w_out', 'ffn_norm_g', 'w_up', 'conv_w', 'conv_b', 'w_down', 'final_norm_g']


def _local_grads(x, tgt, meta, w, full):
    B, S, Dm = x.shape
    L = S + N_META
    T = B * L
    h0 = jnp.concatenate([jnp.broadcast_to(meta[None], (B, N_META, Dm)), x], axis=1).reshape(T, Dm)

    lb_all = jax.nn.softmax(w["hgrn_lb_logits"], axis=0)
    lb = lb_all[0:1]
    zoh_out, zoh_vjp = jax.vjp(_zoh, w["ssm_lambda_re"][0], w["ssm_lambda_im"][0], w["ssm_log_dt"][0],
                               w["ssm_b_re"][0], w["ssm_b_im"][0])
    bs, cs, pw = _s5_tables(*zoh_out, w["ssm_c_re"][0], w["ssm_c_im"][0])

    z1 = _rmsnorm_fwd("mix_norm", h0, w["mix_norm_g"])
    p = _mm_rows("in_proj", z1, full["w_in"], "nn", f32, 1024)
    ya0 = _s5_fwd(p, bs, cs, pw, w["ssm_d"], B, L)
    gl = _mm_rows("glu_proj", ya0, full["ssm_w_glu"], "nn", f32, 1024)
    ya = _glu_fwd(ya0, gl)
    yb = _hgrn_fwd(p, lb, w["hgrn_norm_g"], B, L)
    pa = _mm_rows("ssm_proj", ya, full["w_ssm_proj"], "nn", f32, 1024)
    pb = _mm_rows("hgrn_proj", yb, full["w_hgrn_proj"], "nn", f32, 1024)
    merged = _merge_fwd(p, pa, pb)
    h1 = _mm_rows("out_proj", merged, full["w_out"], "nn", f32, 1024, res=h0)
    z2 = _rmsnorm_fwd("ffn_norm", h1, w["ffn_norm_g"])
    up = _mm_rows("up_proj", z2, full["w_up"], "nn", f32, D_FF // 2)
    ff = _conv_fwd(up, full["conv_w"], w["conv_b"], B, L)
    h2 = _mm_rows("down_proj", ff, full["w_down"], "nn", f32, 1024, res=h1, tk=D_FF // 2)

    h2x = h2.reshape(B, L, Dm)[:, N_META:].reshape(B * S, Dm)
    dh2x, loss, d_final_g = _final_loss(h2x, tgt.reshape(B * S, Dm), w["final_norm_g"].reshape(1, Dm))
    dh2 = jnp.pad(dh2x.reshape(B, S, Dm), ((0, 0), (N_META, 0), (0, 0))).reshape(T, Dm)

    dff = _mm_rows("d_ff", dh2, full["w_down"], "nt", f32, D_FF // 2)
    g_w_down = _mm_wgrad("dw_down", ff, dh2, tn=512)
    dup, dconv = _conv_bwd(up, dff, full["conv_w"], w["conv_b"], B, L)
    dz2 = _dz2(dup, full["w_up"])
    g_w_up = _dw_up(z2, dup)
    dh1, d_ffn_g = _rmsnorm_bwd("ffn_norm_bwd", h1, w["ffn_norm_g"], dz2, dh2)

    dmerged = _mm_rows("d_merged", dh1, full["w_out"], "nt", f32, 1024)
    g_w_out = _mm_wgrad("dw_out", merged, dh1)
    dpa, dpb, dp = _merge_bwd(dmerged, p, pa, pb)
    dya = _mm_rows("d_ya", dpa, full["w_ssm_proj"], "nt", f32, 1024)
    g_w_ssm_proj = _mm_wgrad("dw_ssm_proj", ya, dpa)
    dyb = _mm_rows("d_yb", dpb, full["w_hgrn_proj"], "nt", f32, 1024)
    g_w_hgrn_proj = _mm_wgrad("dw_hgrn_proj", yb, dpb)
    dp, d_lb, d_hgrn_g = _hgrn_bwd(p, dyb, dp, lb, w["hgrn_norm_g"], B, L)
    dgl, dya0_direct = _glu_bwd(dya, ya0, gl)
    dya0 = _mm_rows("d_ya0", dgl, full["ssm_w_glu"], "nt", f32, 1024, res=dya0_direct)
    g_w_glu = _mm_wgrad("dw_glu", ya0, dgl)
    dp, dbs, dcs, da, d_skip = _s5_bwd(p, dya0, dp, bs, cs, pw, w["ssm_d"], B, L)
    dz1 = _dz1(dp, full["w_in"])
    g_w_in = _dw_in(z1, dp)
    dh0, d_mix_g = _rmsnorm_bwd("mix_norm_bwd", h0, w["mix_norm_g"], dz1, dh1)

    dh0 = dh0.reshape(B, L, Dm)
    grad_x = dh0[:, N_META:]
    d_meta = _meta_grad(dh0[:, :N_META])

    d_ab_re, d_ab_im, d_bb_re, d_bb_im, d_c_re, d_c_im = _s5_table_grads(dbs, dcs, da)
    d_lr, d_li, d_log_dt, d_b_re, d_b_im = zoh_vjp((d_ab_re, d_ab_im, d_bb_re, d_bb_im))
    sm0, sm1 = lb_all[0:1], lb_all[1:2]
    d_logits = jnp.concatenate([sm0 * (1.0 - sm0) * d_lb, -sm0 * sm1 * d_lb], axis=0)
    small = {
        "meta_tokens": d_meta, "mix_norm_g": d_mix_g, "ssm_lambda_re": d_lr[None], "ssm_lambda_im": d_li[None],
        "ssm_log_dt": d_log_dt[None], "ssm_b_re": d_b_re[None], "ssm_b_im": d_b_im[None], "ssm_c_re": d_c_re[None],
        "ssm_c_im": d_c_im[None], "ssm_d": d_skip, "hgrn_lb_logits": d_logits, "hgrn_norm_g": d_hgrn_g,
        "ffn_norm_g": d_ffn_g, "conv_w": dconv[:, 0:3, :].transpose(1, 0, 2).reshape(3, 2 * D_FF),
        "conv_b": dconv[:, 3, :].reshape(1, 2 * D_FF), "final_norm_g": d_final_g.reshape(Dm),
    }
    big = {
        "w_in": g_w_in, "ssm_w_glu": g_w_glu.reshape(N_CHIPS, Dm // N_CHIPS, Dm),
        "w_ssm_proj": g_w_ssm_proj.reshape(N_CHIPS, Dm // N_CHIPS, Dm),
        "w_hgrn_proj": g_w_hgrn_proj.reshape(N_CHIPS, Dm // N_CHIPS, Dm), "w_out": g_w_out.reshape(N_CHIPS, Dm // N_CHIPS, Dm),
        "w_up": g_w_up, "w_down": g_w_down.reshape(N_CHIPS, D_FF // N_CHIPS, Dm),
    }
    return loss, grad_x, big, small


def _pack(parts):
    flat = jnp.concatenate([parts[k].reshape(-1) for k in parts])
    n = flat.shape[0]
    rows = -(-n // (SUBLANES * LANES)) * SUBLANES
    flat = jnp.pad(flat, (0, rows * LANES - n))
    return flat.reshape(rows, LANES)


def _unpack(packed, like):
    flat = packed.reshape(-1)
    out, o = {}, 0
    for k, ref in like.items():
        n = math.prod(ref.shape)
        out[k] = flat[o:o + n].reshape(ref.shape)
        o += n
    return out


def kernel(x, meta_tokens, mix_norm_g, w_in, ssm_lambda_re, ssm_lambda_im, ssm_log_dt, ssm_b_re, ssm_b_im, ssm_c_re, ssm_c_im, ssm_d, ssm_w_glu, w_ssm_proj, hgrn_lb_logits, hgrn_norm_g, w_hgrn_proj, w_out, ffn_norm_g, w_up, conv_w, conv_b, w_down, final_norm_g, loss_target, m_meta_tokens, m_mix_norm_g, m_w_in, m_ssm_lambda_re, m_ssm_lambda_im, m_ssm_log_dt, m_ssm_b_re, m_ssm_b_im, m_ssm_c_re, m_ssm_c_im, m_ssm_d, m_ssm_w_glu, m_w_ssm_proj, m_hgrn_lb_logits, m_hgrn_norm_g, m_w_hgrn_proj, m_w_out, m_ffn_norm_g, m_w_up, m_conv_w, m_conv_b, m_w_down, m_final_norm_g, v_meta_tokens, v_mix_norm_g, v_w_in, v_ssm_lambda_re, v_ssm_lambda_im, v_ssm_log_dt, v_ssm_b_re, v_ssm_b_im, v_ssm_c_re, v_ssm_c_im, v_ssm_d, v_ssm_w_glu, v_w_ssm_proj, v_hgrn_lb_logits, v_hgrn_norm_g, v_w_hgrn_proj, v_w_out, v_ffn_norm_g, v_w_up, v_conv_w, v_conv_b, v_w_down, v_final_norm_g):
    args = dict(locals())
    w = {k: args[k] for k in WEIGHTS}
    mom = {k: args["m_" + k] for k in WEIGHTS}
    var = {k: args["v_" + k] for k in WEIGHTS}
    Dm = D_MODEL
    cx, cy, cc = lax.axis_index("x"), lax.axis_index("y"), lax.axis_index("c")
    chip = 2 * cx + cy

    shards = [w["w_in"][0].astype(bf16), w["ssm_w_glu"][0].astype(bf16), w["w_ssm_proj"][0].astype(bf16),
              w["w_hgrn_proj"][0].astype(bf16), w["w_out"][0].astype(bf16), w["w_up"][0].astype(bf16),
              w["w_down"][0].astype(bf16), w["meta_tokens"], w["conv_w"][0]]
    g_in, g_glu, g_sp, g_hp, g_out, g_up, g_down, g_meta, g_cw = _allgather_chips(shards)
    w_in_full = jnp.roll(g_in.transpose(1, 0, 2).reshape(Dm, IN_COLS), -Dm, axis=1)
    full = {
        "w_in": w_in_full, "ssm_w_glu": g_glu.reshape(Dm, Dm), "w_ssm_proj": g_sp.reshape(Dm, Dm),
        "w_hgrn_proj": g_hp.reshape(Dm, Dm), "w_out": g_out.reshape(Dm, Dm),
        "w_up": g_up.transpose(1, 0, 2).reshape(Dm, 2 * D_FF), "w_down": g_down.reshape(D_FF, Dm),
        "conv_w": g_cw.transpose(1, 0, 2).reshape(3, 2 * D_FF),
    }
    meta_full = g_meta.transpose(1, 0, 2).reshape(N_META, Dm)

    loss_part, grad_x, big, small = _local_grads(x, loss_target, meta_full, w, full)

    core = cc.reshape(1).astype(jnp.int32)
    parts = [big[k] for k in BIG]
    got = _sibling_halves(parts)
    sums = [_add_own_half("add_half_" + k, pt, gt, core) for k, pt, gt in zip(BIG, parts, got)]
    slots = _chip_exchange(sums)
    halves = [_sum_slots("sum_chips_" + k, s) for k, s in zip(BIG, slots)]
    g_big = dict(zip(BIG, _sibling_join(halves)))

    small_all = dict(small)
    small_all["loss"] = loss_part[0, 0:1]
    packed = _pack(small_all)
    reduced = _unpack(_sum_slots("sum_devices", _allgather_devices(packed)), small_all)
    loss = reduced.pop("loss")[0]
    mcols = Dm // N_CHIPS
    ccols = 2 * D_FF // N_CHIPS
    grads = {k: reduced[k] for k in SMALL}
    grads["meta_tokens"] = lax.dynamic_slice(reduced["meta_tokens"], (0, chip * mcols), (N_META, mcols))
    grads["conv_w"] = lax.dynamic_slice(reduced["conv_w"], (0, chip * ccols), (3, ccols))[None]
    for k in BIG:
        grads[k] = g_big[k][None]

    delta, new_m, new_v = {}, {}, {}
    for k in BIG:
        shp = w[k].shape
        d, nm, nv = _adamw("adamw_" + k, w[k][0], grads[k][0], mom[k][0], var[k][0])
        delta[k], new_m[k], new_v[k] = d.reshape(shp), nm.reshape(shp), nv.reshape(shp)
    rest = SMALL + SHARDED_SMALL
    pk = [_pack({k: t[k] for k in rest}) for t in (w, grads, mom, var)]
    outs = _adamw("adamw_small", *pk)
    like = {k: w[k] for k in rest}
    for dst, o in zip((delta, new_m, new_v), outs):
        dst.update(_unpack(o, like))

    return (loss, grad_x, *[grads[k].reshape(w[k].shape) for k in WEIGHTS], *[delta[k] for k in WEIGHTS],
            *[new_m[k] for k in WEIGHTS], *[new_v[k] for k in WEIGHTS])
```

```python
import functools
import math

import jax
import jax.numpy as jnp
from jax import lax
from jax.experimental import pallas as pl
from jax.experimental.pallas import tpu as pltpu

f32 = jnp.float32
bf16 = jnp.bfloat16

D_MODEL = 1024
N_META = 16
SSM_GROUP = 16
SSM_GROUPS = 64
SSM_STATE = 64
SLAB_GROUPS = 8
N_SLAB = SSM_GROUPS // SLAB_GROUPS
SLAB_CH = SLAB_GROUPS * SSM_GROUP
SLAB_NS = SLAB_GROUPS * SSM_STATE
HEADS = 8
HEAD_DIM = 128
CHUNK = 16
D_FF = 2816
IN_COLS = 7168
EPS = 1e-6
SUBLANES = 8
LANES = 128
N_CHIPS = 4
N_DEV = 8
ADAM_LR, ADAM_B1, ADAM_B2, ADAM_EPS, ADAM_WD, ADAM_STEP = 0.001, 0.9, 0.999, 1e-08, 0.01, 10
MESH = pl.DeviceIdType.MESH
ANY = pl.BlockSpec(memory_space=pl.ANY)

SEG_Q, SEG_F, SEG_I, SEG_OG, SEG_GA, SEG_GB, SEG_U = range(7)
N_SEG = 7


def _tile(n, target, mult=SUBLANES):
    best = None
    for d in range(mult, min(n, target) + 1, mult):
        if n % d == 0:
            best = d
    return n if best is None else best


def _params(*sem):
    return pltpu.CompilerParams(dimension_semantics=sem)


def _sigmoid(x):
    return 1.0 / (1.0 + jnp.exp(-x))


_DIMS = {"nn": (((1,), (0,)), ((), ())), "nt": (((1,), (1,)), ((), ())), "tn": (((0,), (0,)), ((), ()))}


def _mm(name, a, b, dims, grid, a_spec, b_spec, out_shape, out_spec, acc_shape, res=None, res_spec=None):
    nk = grid[2]
    dn = _DIMS[dims]

    def body(*refs):
        if res is None:
            a_ref, b_ref, o_ref, acc = refs
        else:
            a_ref, b_ref, r_ref, o_ref, acc = refs
        k = pl.program_id(2)

        @pl.when(k == 0)
        def _():
            acc[...] = jnp.zeros_like(acc)

        acc[...] += lax.dot_general(a_ref[...].astype(bf16), b_ref[...].astype(bf16), dn, preferred_element_type=f32)

        @pl.when(k == nk - 1)
        def _():
            r = acc[...]
            if res is not None:
                r = r + r_ref[...]
            o_ref[...] = r.astype(o_ref.dtype)

    ins = [a, b] + ([] if res is None else [res])
    specs = [a_spec, b_spec] + ([] if res is None else [res_spec])
    return pl.pallas_call(
        body, name=name, grid=grid, in_specs=specs, out_specs=out_spec, out_shape=out_shape,
        scratch_shapes=[pltpu.VMEM(acc_shape, f32)],
        compiler_params=_params("parallel", "parallel", "arbitrary"),
    )(*ins)


def _mm_rows(name, a, w, dims, out_dtype, tn, res=None, tk=None):
    T, K = a.shape
    N = w.shape[1] if dims == "nn" else w.shape[0]
    tm = _tile(T, 1032)
    tk = K if tk is None else tk
    grid = (T // tm, N // tn, K // tk)
    a_spec = pl.BlockSpec((tm, tk), lambda i, j, k: (i, k))
    if dims == "nn":
        b_spec = pl.BlockSpec((tk, tn), lambda i, j, k: (k, j))
    else:
        b_spec = pl.BlockSpec((tn, tk), lambda i, j, k: (j, k))
    o_spec = pl.BlockSpec((tm, tn), lambda i, j, k: (i, j))
    return _mm(name, a, w, dims, grid, a_spec, b_spec, jax.ShapeDtypeStruct((T, N), out_dtype), o_spec, (tm, tn),
               res=res, res_spec=None if res is None else o_spec)


def _mm_wgrad(name, a, g, tn=None):
    T, K = a.shape
    N = g.shape[1]
    tk = _tile(T, 688)
    tn = N if tn is None else tn
    grid = (1, N // tn, T // tk)
    a_spec = pl.BlockSpec((tk, K), lambda i, j, k: (k, 0))
    g_spec = pl.BlockSpec((tk, tn), lambda i, j, k: (k, j))
    o_spec = pl.BlockSpec((K, tn), lambda i, j, k: (0, j))
    return _mm(name, a, g, "tn", grid, a_spec, g_spec, jax.ShapeDtypeStruct((K, N), f32), o_spec, (K, tn))


def _rmsnorm_fwd(name, x, g):
    T, Dm = x.shape
    tr = _tile(T, 688)

    def body(x_ref, g_ref, z_ref):
        xv = x_ref[...]
        r = lax.rsqrt(jnp.mean(xv * xv, axis=-1, keepdims=True) + EPS)
        z_ref[...] = (xv * r * g_ref[...]).astype(z_ref.dtype)

    return pl.pallas_call(
        body, name=name, grid=(T // tr,),
        in_specs=[pl.BlockSpec((tr, Dm), lambda i: (i, 0)), pl.BlockSpec((1, Dm), lambda i: (0, 0))],
        out_specs=pl.BlockSpec((tr, Dm), lambda i: (i, 0)),
        out_shape=jax.ShapeDtypeStruct((T, Dm), bf16), compiler_params=_params("parallel"),
    )(x, g)


def _rmsnorm_bwd(name, x, g, dz, dres):
    T, Dm = x.shape
    tr = _tile(T, 688)

    def body(x_ref, g_ref, dz_ref, dres_ref, dx_ref, dg_ref):
        xv = x_ref[...]
        r = lax.rsqrt(jnp.mean(xv * xv, axis=-1, keepdims=True) + EPS)
        xn = xv * r
        dzv = dz_ref[...]
        dzg = dzv * g_ref[...]
        dx_ref[...] = dres_ref[...] + r * (dzg - xn * jnp.mean(dzg * xn, axis=-1, keepdims=True))

        @pl.when(pl.program_id(0) == 0)
        def _():
            dg_ref[...] = jnp.zeros_like(dg_ref)

        dg_ref[...] += jnp.sum(dzv * xn, axis=0, keepdims=True)

    row = pl.BlockSpec((tr, Dm), lambda i: (i, 0))
    par = pl.BlockSpec((1, Dm), lambda i: (0, 0))
    return pl.pallas_call(
        body, name=name, grid=(T // tr,), in_specs=[row, par, row, row], out_specs=[row, par],
        out_shape=[jax.ShapeDtypeStruct((T, Dm), f32), jax.ShapeDtypeStruct((1, Dm), f32)],
        compiler_params=_params("arbitrary"),
    )(x, g, dz, dres)


def _glu_fwd(ya0, gl):
    T, Dm = ya0.shape
    tr = _tile(T, 688)

    def body(y_ref, g_ref, o_ref):
        o_ref[...] = (y_ref[...] * _sigmoid(g_ref[...])).astype(o_ref.dtype)

    row = pl.BlockSpec((tr, Dm), lambda i: (i, 0))
    return pl.pallas_call(body, name="glu_fwd", grid=(T // tr,), in_specs=[row, row], out_specs=row,
                          out_shape=jax.ShapeDtypeStruct((T, Dm), bf16), compiler_params=_params("parallel"))(ya0, gl)


def _glu_bwd(dya, ya0, gl):
    T, Dm = ya0.shape
    tr = _tile(T, 688)

    def body(d_ref, y_ref, g_ref, dg_ref, dy_ref):
        s = _sigmoid(g_ref[...])
        d = d_ref[...]
        dg_ref[...] = (d * y_ref[...] * s * (1.0 - s)).astype(dg_ref.dtype)
        dy_ref[...] = d * s

    row = pl.BlockSpec((tr, Dm), lambda i: (i, 0))
    return pl.pallas_call(body, name="glu_bwd", grid=(T // tr,), in_specs=[row, row, row], out_specs=[row, row],
                          out_shape=[jax.ShapeDtypeStruct((T, Dm), bf16), jax.ShapeDtypeStruct((T, Dm), f32)],
                          compiler_params=_params("parallel"))(dya, ya0, gl)


def _merge_fwd(p, pa, pb):
    T, Dm = pa.shape
    tr = _tile(T, 688)

    def body(ga_ref, gb_ref, pa_ref, pb_ref, o_ref):
        o_ref[...] = (_sigmoid(ga_ref[...]) * pa_ref[...] + _sigmoid(gb_ref[...]) * pb_ref[...]).astype(o_ref.dtype)

    row = pl.BlockSpec((tr, Dm), lambda i: (i, 0))
    return pl.pallas_call(
        body, name="merge_fwd", grid=(T // tr,),
        in_specs=[pl.BlockSpec((tr, Dm), lambda i: (i, SEG_GA)), pl.BlockSpec((tr, Dm), lambda i: (i, SEG_GB)), row, row],
        out_specs=row, out_shape=jax.ShapeDtypeStruct((T, Dm), bf16), compiler_params=_params("parallel"),
    )(p, p, pa, pb)


def _merge_bwd(dm, p, pa, pb):
    T, Dm = pa.shape
    tr = _tile(T, 688)

    def body(dm_ref, ga_ref, gb_ref, pa_ref, pb_ref, dpa_ref, dpb_ref, dp_ref):
        d = dm_ref[...]
        sa = _sigmoid(ga_ref[...])
        sb = _sigmoid(gb_ref[...])
        dpa_ref[...] = (d * sa).astype(dpa_ref.dtype)
        dpb_ref[...] = (d * sb).astype(dpb_ref.dtype)
        dp_ref[0] = (d * pa_ref[...] * sa * (1.0 - sa)).astype(dp_ref.dtype)
        dp_ref[1] = (d * pb_ref[...] * sb * (1.0 - sb)).astype(dp_ref.dtype)

    row = pl.BlockSpec((tr, Dm), lambda i: (i, 0))
    return pl.pallas_call(
        body, name="merge_bwd", grid=(T // tr,),
        in_specs=[row, pl.BlockSpec((tr, Dm), lambda i: (i, SEG_GA)), pl.BlockSpec((tr, Dm), lambda i: (i, SEG_GB)), row, row],
        out_specs=[row, row, pl.BlockSpec((2, tr, Dm), lambda i: (SEG_GA // 2, i, 0))],
        out_shape=[jax.ShapeDtypeStruct((T, Dm), bf16), jax.ShapeDtypeStruct((T, Dm), bf16),
                   jax.ShapeDtypeStruct((N_SEG, T, Dm), bf16)],
        compiler_params=_params("parallel"),
    )(dm, p, p, pa, pb)


def _final_loss(h2x, tgt, g):
    T, Dm = h2x.shape
    tr = _tile(T, 512)

    def body(h_ref, t_ref, g_ref, dh_ref, loss_ref, dg_ref):
        hv = h_ref[...]
        r = lax.rsqrt(jnp.mean(hv * hv, axis=-1, keepdims=True) + EPS)
        xn = hv * r
        gv = g_ref[...]
        err = xn * gv - t_ref[...]
        dy = err * (1.0 / Dm)
        dyg = dy * gv
        dh_ref[...] = r * (dyg - xn * jnp.mean(dyg * xn, axis=-1, keepdims=True))

        @pl.when(pl.program_id(0) == 0)
        def _():
            dg_ref[...] = jnp.zeros_like(dg_ref)
            loss_ref[...] = jnp.zeros_like(loss_ref)

        dg_ref[...] += jnp.sum(dy * xn, axis=0, keepdims=True)
        loss_ref[...] += jnp.sum(err * err) * (0.5 / Dm)

    row = pl.BlockSpec((tr, Dm), lambda i: (i, 0))
    par = pl.BlockSpec((1, Dm), lambda i: (0, 0))
    return pl.pallas_call(
        body, name="final_loss", grid=(T // tr,), in_specs=[row, row, par],
        out_specs=[row, pl.BlockSpec((1, LANES), lambda i: (0, 0)), par],
        out_shape=[jax.ShapeDtypeStruct((T, Dm), f32), jax.ShapeDtypeStruct((1, LANES), f32), jax.ShapeDtypeStruct((1, Dm), f32)],
        compiler_params=_params("arbitrary"),
    )(h2x, tgt, g)


def _meta_grad(dh0_meta):
    B = dh0_meta.shape[0]

    def body(d_ref, o_ref):
        acc = d_ref[0]
        for b in range(1, B):
            acc = acc + d_ref[b]
        o_ref[...] = acc

    return pl.pallas_call(body, name="meta_grad", out_shape=jax.ShapeDtypeStruct(dh0_meta.shape[1:], f32))(dh0_meta)


def _shift_down(x, k, row):
    return jnp.where(row >= k, pltpu.roll(x, k, 0), 0.0)


def _shift_up(x, k, row):
    n = x.shape[0]
    return jnp.where(row < n - k, pltpu.roll(x, n - k, 0), 0.0)


def _conv_fwd(up, conv_w, conv_b, B, L):
    tc = 256
    nt = D_FF // tc

    def body(xa_ref, xb_ref, wa_ref, wb_ref, ba_ref, bb_ref, o_ref):
        row = lax.broadcasted_iota(jnp.int32, (L, tc), 0)

        def conv(x_ref, w_ref, b_ref):
            x = x_ref[...]
            return (b_ref[...] + w_ref[0:1, :] * _shift_down(x, 2, row) + w_ref[1:2, :] * _shift_down(x, 1, row)
                    + w_ref[2:3, :] * x)

        a = conv(xa_ref, wa_ref, ba_ref)
        b = conv(xb_ref, wb_ref, bb_ref)
        o_ref[...] = (a * _sigmoid(a) * b).astype(o_ref.dtype)

    return pl.pallas_call(
        body, name="conv_fwd", grid=(B, nt),
        in_specs=[pl.BlockSpec((L, tc), lambda b, j: (b, j)), pl.BlockSpec((L, tc), lambda b, j: (b, j + nt)),
                  pl.BlockSpec((3, tc), lambda b, j: (0, j)), pl.BlockSpec((3, tc), lambda b, j: (0, j + nt)),
                  pl.BlockSpec((1, tc), lambda b, j: (0, j)), pl.BlockSpec((1, tc), lambda b, j: (0, j + nt))],
        out_specs=pl.BlockSpec((L, tc), lambda b, j: (b, j)),
        out_shape=jax.ShapeDtypeStruct((B * L, D_FF), bf16), compiler_params=_params("parallel", "parallel"),
    )(up, up, conv_w, conv_w, conv_b, conv_b)


def _conv_bwd(up, dff, conv_w, conv_b, B, L):
    tc = 256
    nt = D_FF // tc

    def body(xa_ref, xb_ref, d_ref, wa_ref, wb_ref, ba_ref, bb_ref, dup_ref, dw_ref):
        row = lax.broadcasted_iota(jnp.int32, (L, tc), 0)
        xs, pre = [], []
        for x_ref, w_ref, b_ref in ((xa_ref, wa_ref, ba_ref), (xb_ref, wb_ref, bb_ref)):
            x = x_ref[...]
            x1 = _shift_down(x, 1, row)
            x2 = _shift_down(x, 2, row)
            xs.append((x, x1, x2))
            pre.append(b_ref[...] + w_ref[0:1, :] * x2 + w_ref[1:2, :] * x1 + w_ref[2:3, :] * x)
        a, b = pre
        s = _sigmoid(a)
        d = d_ref[...]
        grads = (d * b * s * (1.0 + a * (1.0 - s)), d * a * s)

        @pl.when(pl.program_id(1) == 0)
        def _():
            dw_ref[...] = jnp.zeros_like(dw_ref)

        for h, (gr, (x, x1, x2), w_ref) in enumerate(zip(grads, xs, (wa_ref, wb_ref))):
            dup_ref[h] = (w_ref[2:3, :] * gr + w_ref[1:2, :] * _shift_up(gr, 1, row)
                          + w_ref[0:1, :] * _shift_up(gr, 2, row)).astype(dup_ref.dtype)
            dw_ref[h, 0:1, :] += jnp.sum(gr * x2, axis=0, keepdims=True)
            dw_ref[h, 1:2, :] += jnp.sum(gr * x1, axis=0, keepdims=True)
            dw_ref[h, 2:3, :] += jnp.sum(gr * x, axis=0, keepdims=True)
            dw_ref[h, 3:4, :] += jnp.sum(gr, axis=0, keepdims=True)

    return pl.pallas_call(
        body, name="conv_bwd", grid=(nt, B),
        in_specs=[pl.BlockSpec((L, tc), lambda j, b: (b, j)), pl.BlockSpec((L, tc), lambda j, b: (b, j + nt)),
                  pl.BlockSpec((L, tc), lambda j, b: (b, j)),
                  pl.BlockSpec((3, tc), lambda j, b: (0, j)), pl.BlockSpec((3, tc), lambda j, b: (0, j + nt)),
                  pl.BlockSpec((1, tc), lambda j, b: (0, j)), pl.BlockSpec((1, tc), lambda j, b: (0, j + nt))],
        out_specs=[pl.BlockSpec((2, L, tc), lambda j, b: (0, b, j)), pl.BlockSpec((2, SUBLANES, tc), lambda j, b: (0, 0, j))],
        out_shape=[jax.ShapeDtypeStruct((2, B * L, D_FF), bf16), jax.ShapeDtypeStruct((2, SUBLANES, D_FF), f32)],
        compiler_params=_params("parallel", "arbitrary"),
    )(up, up, dff, conv_w, conv_w, conv_b, conv_b)


GELU_C = math.sqrt(2.0 / math.pi)
GELU_A = 0.044715


def _gelu(x):
    return 0.5 * x * (1.0 + jnp.tanh(GELU_C * (x + GELU_A * x * x * x)))


def _gelu_grad(x):
    t = jnp.tanh(GELU_C * (x + GELU_A * x * x * x))
    return 0.5 * (1.0 + t) + 0.5 * x * (1.0 - t * t) * GELU_C * (1.0 + 3.0 * GELU_A * x * x)


def _cmul_add(xr, xi, ar, ai, sr, si):
    return xr + ar * sr - ai * si, xi + ar * si + ai * sr


def _s5_scan_fwd(s_ref, pw_ref, L):
    ns = SLAB_NS
    row = lax.broadcasted_iota(jnp.int32, (SUBLANES, ns), 0)
    pr = pw_ref[0, 0:SUBLANES, :]
    pi = pw_ref[1, 0:SUBLANES, :]

    def step(i, carry):
        cr, ci = carry
        r0 = pl.multiple_of(i * SUBLANES, SUBLANES)
        xr = s_ref[pl.ds(r0, SUBLANES), 0:ns]
        xi = s_ref[pl.ds(r0, SUBLANES), ns:2 * ns]
        for k in (1, 2, 4):
            xr, xi = _cmul_add(xr, xi, pr[k - 1:k, :], pi[k - 1:k, :], _shift_down(xr, k, row), _shift_down(xi, k, row))
        xr, xi = _cmul_add(xr, xi, pr, pi, cr, ci)
        s_ref[pl.ds(r0, SUBLANES), 0:ns] = xr
        s_ref[pl.ds(r0, SUBLANES), ns:2 * ns] = xi
        return xr[SUBLANES - 1:SUBLANES, :], xi[SUBLANES - 1:SUBLANES, :]

    z = jnp.zeros((1, ns), f32)
    lax.fori_loop(0, L // SUBLANES, step, (z, z))


def _s5_project_in(u_ref, bs_ref, s_ref, L, rc):
    for r in range(0, L, rc):
        s_ref[r:r + rc, :] = jnp.dot(u_ref[r:r + rc, :].astype(bf16), bs_ref[...], preferred_element_type=f32)


def _s5_fwd(p, bs, cs, pw, d_skip, B, L):
    rc = _tile(L, 344)

    def body(u_ref, bs_ref, cs_ref, pw_ref, d_ref, y_ref, s_ref):
        _s5_project_in(u_ref, bs_ref, s_ref, L, rc)
        _s5_scan_fwd(s_ref, pw_ref, L)
        for r in range(0, L, rc):
            ypre = (jnp.dot(s_ref[r:r + rc, :].astype(bf16), cs_ref[...], preferred_element_type=f32)
                    + d_ref[...] * u_ref[r:r + rc, :])
            y_ref[r:r + rc, :] = _gelu(ypre)

    ucol = SEG_U * (D_MODEL // SLAB_CH)
    return pl.pallas_call(
        body, name="s5_fwd", grid=(B, N_SLAB),
        in_specs=[pl.BlockSpec((L, SLAB_CH), lambda b, s: (b, ucol + s)),
                  pl.BlockSpec((None, SLAB_CH, 2 * SLAB_NS), lambda b, s: (s, 0, 0)),
                  pl.BlockSpec((None, 2 * SLAB_NS, SLAB_CH), lambda b, s: (s, 0, 0)),
                  pl.BlockSpec((None, 2, 2 * SUBLANES, SLAB_NS), lambda b, s: (s, 0, 0, 0)),
                  pl.BlockSpec((1, SLAB_CH), lambda b, s: (0, s))],
        out_specs=pl.BlockSpec((L, SLAB_CH), lambda b, s: (b, s)),
        out_shape=jax.ShapeDtypeStruct((B * L, D_MODEL), f32),
        scratch_shapes=[pltpu.VMEM((L, 2 * SLAB_NS), f32)],
        compiler_params=_params("parallel", "parallel"),
    )(p, bs, cs, pw, d_skip)


def _s5_bwd(p, dya0, dp, bs, cs, pw, d_skip, B, L):
    rc = _tile(L, 344)
    ns = SLAB_NS
    nt = L // SUBLANES

    def body(u_ref, dy_ref, dp_in, bs_ref, cs_ref, pw_ref, d_ref, du_ref, dbs_ref, dcs_ref, da_ref, dd_ref,
             s_ref, lam_ref, dyp_ref):
        del dp_in
        b = pl.program_id(1)

        @pl.when(b == 0)
        def _():
            dbs_ref[...] = jnp.zeros_like(dbs_ref)
            dcs_ref[...] = jnp.zeros_like(dcs_ref)
            da_ref[...] = jnp.zeros_like(da_ref)
            dd_ref[...] = jnp.zeros_like(dd_ref)

        _s5_project_in(u_ref, bs_ref, s_ref, L, rc)
        _s5_scan_fwd(s_ref, pw_ref, L)
        for r in range(0, L, rc):
            u = u_ref[r:r + rc, :]
            sb = s_ref[r:r + rc, :].astype(bf16)
            ypre = jnp.dot(sb, cs_ref[...], preferred_element_type=f32) + d_ref[...] * u
            dyp = dy_ref[r:r + rc, :] * _gelu_grad(ypre)
            dyp_ref[r:r + rc, :] = dyp
            dd_ref[...] += jnp.sum(dyp * u, axis=0, keepdims=True)
            dypb = dyp.astype(bf16)
            dcs_ref[...] += lax.dot_general(sb, dypb, _DIMS["tn"], preferred_element_type=f32)
            lam_ref[r:r + rc, :] = lax.dot_general(dypb, cs_ref[...], _DIMS["nt"], preferred_element_type=f32)

        row = lax.broadcasted_iota(jnp.int32, (SUBLANES, ns), 0)
        pr = pw_ref[0, 0:SUBLANES, :]
        pi = -pw_ref[1, 0:SUBLANES, :]
        qr = pw_ref[0, SUBLANES:2 * SUBLANES, :]
        qi = -pw_ref[1, SUBLANES:2 * SUBLANES, :]

        def step(j, carry):
            cr, ci, ar, ai = carry
            i = nt - 1 - j
            r0 = pl.multiple_of(i * SUBLANES, SUBLANES)
            xr = lam_ref[pl.ds(r0, SUBLANES), 0:ns]
            xi = lam_ref[pl.ds(r0, SUBLANES), ns:2 * ns]
            for k in (1, 2, 4):
                xr, xi = _cmul_add(xr, xi, pr[k - 1:k, :], pi[k - 1:k, :], _shift_up(xr, k, row), _shift_up(xi, k, row))
            xr, xi = _cmul_add(xr, xi, qr, qi, cr, ci)
            lam_ref[pl.ds(r0, SUBLANES), 0:ns] = xr
            lam_ref[pl.ds(r0, SUBLANES), ns:2 * ns] = xi
            rp = pl.multiple_of(jnp.maximum(i - 1, 0) * SUBLANES, SUBLANES)
            live = jnp.where(i > 0, 1.0, 0.0)
            lr_ = s_ref[pl.ds(rp + SUBLANES - 1, 1), 0:ns] * live
            li_ = s_ref[pl.ds(rp + SUBLANES - 1, 1), ns:2 * ns] * live
            spr = jnp.where(row == 0, lr_, pltpu.roll(s_ref[pl.ds(r0, SUBLANES), 0:ns], 1, 0))
            spi = jnp.where(row == 0, li_, pltpu.roll(s_ref[pl.ds(r0, SUBLANES), ns:2 * ns], 1, 0))
            ar = ar + xr * spr + xi * spi
            ai = ai + xi * spr - xr * spi
            return xr[0:1, :], xi[0:1, :], ar, ai

        z1 = jnp.zeros((1, ns), f32)
        z8 = jnp.zeros((SUBLANES, ns), f32)
        _, _, ar, ai = lax.fori_loop(0, nt, step, (z1, z1, z8, z8))
        da_ref[0:1, :] += jnp.sum(ar, axis=0, keepdims=True)
        da_ref[1:2, :] += jnp.sum(ai, axis=0, keepdims=True)

        for r in range(0, L, rc):
            lamb = lam_ref[r:r + rc, :].astype(bf16)
            dbs_ref[...] += lax.dot_general(u_ref[r:r + rc, :].astype(bf16), lamb, _DIMS["tn"], preferred_element_type=f32)
            du = (lax.dot_general(lamb, bs_ref[...], _DIMS["nt"], preferred_element_type=f32)
                  + d_ref[...] * dyp_ref[r:r + rc, :])
            du_ref[r:r + rc, :] = du.astype(du_ref.dtype)

    ucol = SEG_U * (D_MODEL // SLAB_CH)
    T = B * L
    return pl.pallas_call(
        body, name="s5_bwd", grid=(N_SLAB, B),
        in_specs=[pl.BlockSpec((L, SLAB_CH), lambda s, b: (b, ucol + s)),
                  pl.BlockSpec((L, SLAB_CH), lambda s, b: (b, s)),
                  ANY,
                  pl.BlockSpec((None, SLAB_CH, 2 * SLAB_NS), lambda s, b: (s, 0, 0)),
                  pl.BlockSpec((None, 2 * SLAB_NS, SLAB_CH), lambda s, b: (s, 0, 0)),
                  pl.BlockSpec((None, 2, 2 * SUBLANES, SLAB_NS), lambda s, b: (s, 0, 0, 0)),
                  pl.BlockSpec((1, SLAB_CH), lambda s, b: (0, s))],
        out_specs=[pl.BlockSpec((None, L, SLAB_CH), lambda s, b: (SEG_U, b, s)),
                   pl.BlockSpec((None, SLAB_CH, 2 * SLAB_NS), lambda s, b: (s, 0, 0)),
                   pl.BlockSpec((None, 2 * SLAB_NS, SLAB_CH), lambda s, b: (s, 0, 0)),
                   pl.BlockSpec((None, 2, SLAB_NS), lambda s, b: (s, 0, 0)),
                   pl.BlockSpec((1, SLAB_CH), lambda s, b: (0, s))],
        out_shape=[jax.ShapeDtypeStruct((N_SEG, T, D_MODEL), bf16),
                   jax.ShapeDtypeStruct((N_SLAB, SLAB_CH, 2 * SLAB_NS), f32),
                   jax.ShapeDtypeStruct((N_SLAB, 2 * SLAB_NS, SLAB_CH), f32),
                   jax.ShapeDtypeStruct((N_SLAB, 2, SLAB_NS), f32),
                   jax.ShapeDtypeStruct((1, D_MODEL), f32)],
        scratch_shapes=[pltpu.VMEM((L, 2 * SLAB_NS), f32), pltpu.VMEM((L, 2 * SLAB_NS), f32), pltpu.VMEM((L, SLAB_CH), f32)],
        input_output_aliases={2: 0},
        compiler_params=_params("parallel", "arbitrary"),
    )(p, dya0, dp, bs, cs, pw, d_skip)


def _dotb(a, b, dims="nn"):
    return lax.dot_general(a.astype(bf16), b.astype(bf16), _DIMS[dims], preferred_element_type=f32)


def _chunk_cumsum(x, pos):
    k = 1
    while k < CHUNK:
        x = x + jnp.where(pos >= k, pltpu.roll(x, k, 0), 0.0)
        k *= 2
    return x


def _chunk_rev_cumsum(x, pos):
    n = x.shape[0]
    k = 1
    while k < CHUNK:
        x = x + jnp.where(pos < CHUNK - k, pltpu.roll(x, n - k, 0), 0.0)
        k *= 2
    return x


def _hgrn_local(q, fl, lb, pos):
    sg = _sigmoid(fl)
    f = lb + (1.0 - lb) * sg
    g = jnp.log(f)
    cum = _chunk_cumsum(g, pos)
    rest = _chunk_rev_cumsum(g, pos) - g
    e = jnp.exp(cum)
    em = jnp.exp(-cum)
    eo = jnp.exp(rest)
    k = 1.0 - f
    return sg, f, e, em, eo, q * e, k * em, k * eo, jnp.exp(cum + rest)


def _hgrn_block_mask(n):
    r = lax.broadcasted_iota(jnp.int32, (n, n), 0)
    c = lax.broadcasted_iota(jnp.int32, (n, n), 1)
    return ((r & -CHUNK) == (c & -CHUNK)) & (c <= r)


def _chunk_pos(n):
    return lax.broadcasted_iota(jnp.int32, (n, HEAD_DIM), 0) & (CHUNK - 1)


def _hgrn_block_rows(L):
    return _tile(L, 688, CHUNK)


def _hgrn_specs(L, order):
    hb = D_MODEL // HEAD_DIM

    def spec(seg):
        if order == "bh":
            return pl.BlockSpec((L, HEAD_DIM), lambda b, h: (b, seg * hb + h))
        return pl.BlockSpec((L, HEAD_DIM), lambda h, b: (b, seg * hb + h))

    return [spec(SEG_Q), spec(SEG_F), spec(SEG_I), spec(SEG_OG)]


def _hgrn_fwd(p, lb, norm_g, B, L):
    nc = L // CHUNK

    rb = _hgrn_block_rows(L)

    def body(q_ref, f_ref, v_ref, og_ref, lb_ref, ng_ref, y_ref, qt_s, ko_s, vb_s, dec_s, o_s):
        lbv = lb_ref[...]
        ngv = ng_ref[...]
        mask = _hgrn_block_mask(rb)
        pos = _chunk_pos(rb)

        for r in range(0, L, rb):
            rows = slice(r, r + rb)
            _, _, _, _, _, qt, kt, ko, dec = _hgrn_local(q_ref[rows, :], f_ref[rows, :], lbv, pos)
            vb = v_ref[rows, :].astype(bf16)
            qtb = qt.astype(bf16)
            pm = jnp.where(mask, _dotb(qtb, kt, "nt"), 0.0)
            o_s[rows, :] = _dotb(pm, vb)
            qt_s[rows, :] = qtb
            ko_s[rows, :] = ko.astype(bf16)
            vb_s[rows, :] = vb
            dec_s[rows, :] = dec

        def step(c, st):
            rows = pl.ds(pl.multiple_of(c * CHUNK, CHUNK), CHUNK)
            o_s[rows, :] += _dotb(qt_s[rows, :], st, "nt")
            return st * dec_s[rows, :][0:1, :] + _dotb(vb_s[rows, :], ko_s[rows, :], "tn")

        lax.fori_loop(0, nc, step, jnp.zeros((HEAD_DIM, HEAD_DIM), f32))

        for r in range(0, L, rb):
            rows = slice(r, r + rb)
            o = o_s[rows, :]
            og = og_ref[rows, :]
            on = o * lax.rsqrt(jnp.mean(o * o, axis=-1, keepdims=True) + EPS) * ngv
            y_ref[rows, :] = (on * og * _sigmoid(og)).astype(y_ref.dtype)

    return pl.pallas_call(
        body, name="hgrn_fwd", grid=(B, HEADS),
        in_specs=_hgrn_specs(L, "bh") + [pl.BlockSpec((1, HEAD_DIM), lambda b, h: (0, h)),
                                          pl.BlockSpec((1, HEAD_DIM), lambda b, h: (0, 0))],
        out_specs=pl.BlockSpec((L, HEAD_DIM), lambda b, h: (b, h)),
        out_shape=jax.ShapeDtypeStruct((B * L, D_MODEL), bf16),
        scratch_shapes=[pltpu.VMEM((L, HEAD_DIM), bf16), pltpu.VMEM((L, HEAD_DIM), bf16), pltpu.VMEM((L, HEAD_DIM), bf16),
                        pltpu.VMEM((L, HEAD_DIM), f32), pltpu.VMEM((L, HEAD_DIM), f32)],
        compiler_params=_params("parallel", "parallel"),
    )(p, p, p, p, lb, norm_g)


def _hgrn_bwd(p, dyb, dp, lb, norm_g, B, L):
    nc = L // CHUNK

    rb = _hgrn_block_rows(L)

    def body(q_ref, f_ref, v_ref, og_ref, dy_ref, dp_in, lb_ref, ng_ref, dseg_ref, dlb_ref, dng_ref,
             st_ref, qt_s, kt_s, ko_s, vb_s, do_s, dec_s, o_s, dqt_s, dkt_s, dko_s, dv_s, ddec_s):
        del dp_in
        lbv = lb_ref[...]
        ngv = ng_ref[...]
        mask = _hgrn_block_mask(rb)
        pos = _chunk_pos(rb)
        blocks = [slice(r, r + rb) for r in range(0, L, rb)]

        @pl.when(pl.program_id(1) == 0)
        def _():
            dlb_ref[...] = jnp.zeros_like(dlb_ref)

        @pl.when((pl.program_id(0) == 0) & (pl.program_id(1) == 0))
        def _():
            dng_ref[...] = jnp.zeros_like(dng_ref)

        def scores(rows):
            return jnp.where(mask, _dotb(qt_s[rows, :], kt_s[rows, :], "nt"), 0.0).astype(bf16)

        for rows in blocks:
            _, _, _, _, _, qt, kt, ko, dec = _hgrn_local(q_ref[rows, :], f_ref[rows, :], lbv, pos)
            qt_s[rows, :] = qt.astype(bf16)
            kt_s[rows, :] = kt.astype(bf16)
            ko_s[rows, :] = ko.astype(bf16)
            vb_s[rows, :] = v_ref[rows, :].astype(bf16)
            dec_s[rows, :] = dec
            o_s[rows, :] = _dotb(scores(rows), vb_s[rows, :])

        def fwd_step(c, st):
            rows = pl.ds(pl.multiple_of(c * CHUNK, CHUNK), CHUNK)
            st_ref[c] = st
            o_s[rows, :] += _dotb(qt_s[rows, :], st, "nt")
            return st * dec_s[rows, :][0:1, :] + _dotb(vb_s[rows, :], ko_s[rows, :], "tn")

        lax.fori_loop(0, nc, fwd_step, jnp.zeros((HEAD_DIM, HEAD_DIM), f32))

        dng = jnp.zeros((1, HEAD_DIM), f32)
        for rows in blocks:
            o = o_s[rows, :]
            og = og_ref[rows, :]
            dy = dy_ref[rows, :]
            rs = lax.rsqrt(jnp.mean(o * o, axis=-1, keepdims=True) + EPS)
            xn = o * rs
            so = _sigmoid(og)
            dseg_ref[SEG_OG, rows, :] = (dy * xn * ngv * so * (1.0 + og * (1.0 - so))).astype(dseg_ref.dtype)
            don = dy * og * so
            dng = dng + jnp.sum(don * xn, axis=0, keepdims=True)
            dxo = don * ngv
            do = (rs * (dxo - xn * jnp.mean(dxo * xn, axis=-1, keepdims=True))).astype(bf16)
            do_s[rows, :] = do
            dpm = jnp.where(mask, _dotb(do, vb_s[rows, :], "nt"), 0.0).astype(bf16)
            dqt_s[rows, :] = _dotb(dpm, kt_s[rows, :])
            dkt_s[rows, :] = _dotb(dpm, qt_s[rows, :], "tn")
            dv_s[rows, :] = _dotb(scores(rows), do, "tn")
        dng_ref[...] += dng

        def bwd_step(j, dst):
            c = nc - 1 - j
            rows = pl.ds(pl.multiple_of(c * CHUNK, CHUNK), CHUNK)
            stp = st_ref[c]
            do = do_s[rows, :]
            dqt_s[rows, :] += _dotb(do, stp)
            dv_s[rows, :] += _dotb(ko_s[rows, :], dst, "nt")
            dko_s[rows, :] = _dotb(vb_s[rows, :], dst)
            ddec_s[rows, :] = jnp.broadcast_to(jnp.sum(dst * stp, axis=0, keepdims=True), (CHUNK, HEAD_DIM))
            return dst * dec_s[rows, :][0:1, :] + _dotb(do, qt_s[rows, :], "tn")

        lax.fori_loop(0, nc, bwd_step, jnp.zeros((HEAD_DIM, HEAD_DIM), f32))

        dlb = jnp.zeros((1, HEAD_DIM), f32)
        for rows in blocks:
            sg, f, e, em, eo, qt, kt, ko, dec = _hgrn_local(q_ref[rows, :], f_ref[rows, :], lbv, pos)
            dqt = dqt_s[rows, :]
            dkt = dkt_s[rows, :]
            dko = dko_s[rows, :]
            dko_ko = dko * ko
            dcum = dqt * qt - dkt * kt - dko_ko
            chunk_tot = _chunk_cumsum(dko_ko, pos) + _chunk_rev_cumsum(dko_ko, pos) - dko_ko
            dcum = dcum + jnp.where(pos == CHUNK - 1, chunk_tot + ddec_s[rows, :] * dec, 0.0)
            df = _chunk_rev_cumsum(dcum, pos) / f - (dkt * em + dko * eo)
            dlb = dlb + jnp.sum(df * (1.0 - sg), axis=0, keepdims=True)
            dseg_ref[SEG_Q, rows, :] = (dqt * e).astype(dseg_ref.dtype)
            dseg_ref[SEG_F, rows, :] = (df * (1.0 - lbv) * sg * (1.0 - sg)).astype(dseg_ref.dtype)
            dseg_ref[SEG_I, rows, :] = dv_s[rows, :].astype(dseg_ref.dtype)
        dlb_ref[...] += dlb

    T = B * L
    sb = pltpu.VMEM((L, HEAD_DIM), bf16)
    sf = pltpu.VMEM((L, HEAD_DIM), f32)
    return pl.pallas_call(
        body, name="hgrn_bwd", grid=(HEADS, B),
        in_specs=_hgrn_specs(L, "hb") + [pl.BlockSpec((L, HEAD_DIM), lambda h, b: (b, h)), ANY,
                                          pl.BlockSpec((1, HEAD_DIM), lambda h, b: (0, h)),
                                          pl.BlockSpec((1, HEAD_DIM), lambda h, b: (0, 0))],
        out_specs=[pl.BlockSpec((4, L, HEAD_DIM), lambda h, b: (0, b, h)),
                   pl.BlockSpec((1, HEAD_DIM), lambda h, b: (0, h)),
                   pl.BlockSpec((1, HEAD_DIM), lambda h, b: (0, 0))],
        out_shape=[jax.ShapeDtypeStruct((N_SEG, T, D_MODEL), bf16), jax.ShapeDtypeStruct((1, D_MODEL), f32),
                   jax.ShapeDtypeStruct((1, HEAD_DIM), f32)],
        scratch_shapes=[pltpu.VMEM((nc, HEAD_DIM, HEAD_DIM), f32), sb, sb, sb, sb, sb, sf, sf, sf, sf, sf, sf, sf],
        input_output_aliases={5: 0},
        compiler_params=_params("arbitrary", "arbitrary"),
    )(p, p, p, p, dyb, dp, lb, norm_g)


def _dz1(dp, w_in_phys):
    _, T, Dm = dp.shape
    tm = _tile(T, 1032)
    return _mm("dz1", dp, w_in_phys, "nt", (T // tm, 1, N_SEG),
               pl.BlockSpec((None, tm, Dm), lambda i, j, k: (k, i, 0)),
               pl.BlockSpec((Dm, Dm), lambda i, j, k: (0, k)),
               jax.ShapeDtypeStruct((T, Dm), f32), pl.BlockSpec((tm, Dm), lambda i, j, k: (i, 0)), (tm, Dm))


def _dw_in(z1, dp):
    _, T, Dm = dp.shape
    tn = 256
    per_seg = Dm // tn
    per_chip = IN_COLS // N_CHIPS // tn
    tk = _tile(T, 1376)

    def out_idx(i, j, k):
        logical = ((j // per_seg + 1) % N_SEG) * per_seg + j % per_seg
        return (logical // per_chip, 0, logical % per_chip)

    return _mm("dw_in", z1, dp, "tn", (1, IN_COLS // tn, T // tk),
               pl.BlockSpec((tk, Dm), lambda i, j, k: (k, 0)),
               pl.BlockSpec((None, tk, tn), lambda i, j, k: (j // per_seg, k, j % per_seg)),
               jax.ShapeDtypeStruct((N_CHIPS, Dm, IN_COLS // N_CHIPS), f32),
               pl.BlockSpec((None, Dm, tn), out_idx), (Dm, tn))


def _dz2(dup, w_up):
    _, T, _ = dup.shape
    tm = _tile(T, 1032)
    tk = D_FF // 2
    return _mm("dz2", dup, w_up, "nt", (T // tm, 1, 4),
               pl.BlockSpec((None, tm, tk), lambda i, j, k: (k // 2, i, k % 2)),
               pl.BlockSpec((D_MODEL, tk), lambda i, j, k: (0, k)),
               jax.ShapeDtypeStruct((T, D_MODEL), f32), pl.BlockSpec((tm, D_MODEL), lambda i, j, k: (i, 0)), (tm, D_MODEL))


def _dw_up(z2, dup):
    _, T, _ = dup.shape
    tn = D_FF // 2
    tk = _tile(T, 688)
    return _mm("dw_up", z2, dup, "tn", (1, N_CHIPS, T // tk),
               pl.BlockSpec((tk, D_MODEL), lambda i, j, k: (k, 0)),
               pl.BlockSpec((None, tk, tn), lambda i, j, k: (j // 2, k, j % 2)),
               jax.ShapeDtypeStruct((N_CHIPS, D_MODEL, tn), f32),
               pl.BlockSpec((None, D_MODEL, tn), lambda i, j, k: (j, 0, 0)), (D_MODEL, tn))


def _place():
    x, y, c = lax.axis_index("x"), lax.axis_index("y"), lax.axis_index("c")
    chips = [(1 - x, y), (x, 1 - y), (1 - x, 1 - y)]
    return x, y, c, chips


def _allgather_chips(arrs):
    n = len(arrs)

    def body(*refs):
        ins, outs = refs[:n], refs[n:2 * n]
        send, recv, local = refs[2 * n:]
        x, y, c, chips = _place()
        me = 2 * x + y

        def copy(a, k, slot):
            px, py = chips[k]
            return pltpu.make_async_remote_copy(src_ref=ins[a], dst_ref=outs[a].at[slot], send_sem=send.at[3 * a + k],
                                                recv_sem=recv.at[3 * a + k], device_id=(px, py, c), device_id_type=MESH)

        for a in range(n):
            pltpu.make_async_copy(ins[a], outs[a].at[me], local.at[a]).start()
            for k in range(3):
                copy(a, k, me).start()
        for a in range(n):
            for k, (px, py) in enumerate(chips):
                copy(a, k, 2 * px + py).wait_recv()
        for a in range(n):
            pltpu.make_async_copy(ins[a], outs[a].at[me], local.at[a]).wait()
            for k in range(3):
                copy(a, k, me).wait_send()

    return pl.pallas_call(
        body, name="allgather_chips", in_specs=[ANY] * n, out_specs=[ANY] * n,
        out_shape=[jax.ShapeDtypeStruct((N_CHIPS,) + a.shape, a.dtype) for a in arrs],
        scratch_shapes=[pltpu.SemaphoreType.DMA((3 * n,)), pltpu.SemaphoreType.DMA((3 * n,)), pltpu.SemaphoreType.DMA((n,))],
    )(*arrs)


def _sibling_halves(parts):
    n = len(parts)

    def body(*refs):
        ins, outs = refs[:n], refs[n:2 * n]
        send, recv = refs[2 * n:]
        x, y, c, _ = _place()

        def copy(a):
            rh = ins[a].shape[1] // 2
            return pltpu.make_async_remote_copy(src_ref=ins[a].at[:, pl.ds((1 - c) * rh, rh), :], dst_ref=outs[a],
                                                send_sem=send.at[a], recv_sem=recv.at[a], device_id=(x, y, 1 - c),
                                                device_id_type=MESH)

        for a in range(n):
            copy(a).start()
        for a in range(n):
            copy(a).wait_recv()
        for a in range(n):
            copy(a).wait_send()

    return pl.pallas_call(
        body, name="sibling_halves", in_specs=[ANY] * n, out_specs=[ANY] * n,
        out_shape=[jax.ShapeDtypeStruct((a.shape[0], a.shape[1] // 2, a.shape[2]), a.dtype) for a in parts],
        scratch_shapes=[pltpu.SemaphoreType.DMA((n,)), pltpu.SemaphoreType.DMA((n,))],
    )(*parts)


def _add_own_half(name, part, got, core):
    nchip, R, C = part.shape
    rh = R // 2
    tr = _tile(rh, 256)
    nt = rh // tr

    def body(core_ref, a_ref, b_ref, o_ref):
        del core_ref
        o_ref[...] = a_ref[...] + b_ref[...]

    return pl.pallas_call(
        body, name=name,
        grid_spec=pltpu.PrefetchScalarGridSpec(
            num_scalar_prefetch=1, grid=(nchip, nt),
            in_specs=[pl.BlockSpec((None, tr, C), lambda j, i, core_ref: (j, core_ref[0] * nt + i, 0)),
                      pl.BlockSpec((None, tr, C), lambda j, i, core_ref: (j, i, 0))],
            out_specs=pl.BlockSpec((None, tr, C), lambda j, i, core_ref: (j, i, 0))),
        out_shape=jax.ShapeDtypeStruct((nchip, rh, C), f32), compiler_params=_params("parallel", "parallel"),
    )(core, part, got)


def _chip_exchange(sums):
    n = len(sums)

    def body(*refs):
        ins, outs = refs[:n], refs[n:2 * n]
        send, recv, local = refs[2 * n:]
        x, y, c, chips = _place()
        me = 2 * x + y

        def copy(a, k):
            px, py = chips[k]
            return pltpu.make_async_remote_copy(src_ref=ins[a].at[2 * px + py], dst_ref=outs[a].at[me], send_sem=send.at[3 * a + k],
                                                recv_sem=recv.at[3 * a + k], device_id=(px, py, c), device_id_type=MESH)

        def landed(a, k):
            px, py = chips[k]
            return pltpu.make_async_remote_copy(src_ref=ins[a].at[me], dst_ref=outs[a].at[2 * px + py], send_sem=send.at[3 * a + k],
                                                recv_sem=recv.at[3 * a + k], device_id=(px, py, c), device_id_type=MESH)

        for a in range(n):
            pltpu.make_async_copy(ins[a].at[me], outs[a].at[me], local.at[a]).start()
            for k in range(3):
                copy(a, k).start()
        for a in range(n):
            for k in range(3):
                landed(a, k).wait_recv()
        for a in range(n):
            pltpu.make_async_copy(ins[a].at[me], outs[a].at[me], local.at[a]).wait()
            for k in range(3):
                copy(a, k).wait_send()

    return pl.pallas_call(
        body, name="chip_exchange", in_specs=[ANY] * n, out_specs=[ANY] * n,
        out_shape=[jax.ShapeDtypeStruct(a.shape, a.dtype) for a in sums],
        scratch_shapes=[pltpu.SemaphoreType.DMA((3 * n,)), pltpu.SemaphoreType.DMA((3 * n,)), pltpu.SemaphoreType.DMA((n,))],
    )(*sums)


def _sum_slots(name, slots):
    ns, R, C = slots.shape
    tr = _tile(R, 256)

    def body(s_ref, o_ref):
        acc = s_ref[0]
        for j in range(1, ns):
            acc = acc + s_ref[j]
        o_ref[...] = acc

    return pl.pallas_call(
        body, name=name, grid=(R // tr,), in_specs=[pl.BlockSpec((ns, tr, C), lambda i: (0, i, 0))],
        out_specs=pl.BlockSpec((tr, C), lambda i: (i, 0)), out_shape=jax.ShapeDtypeStruct((R, C), f32),
        compiler_params=_params("parallel"),
    )(slots)


def _sibling_join(halves):
    n = len(halves)

    def body(*refs):
        ins, outs = refs[:n], refs[n:2 * n]
        send, recv, local = refs[2 * n:]
        x, y, c, _ = _place()

        def rows(a, core):
            rh = ins[a].shape[0]
            return outs[a].at[pl.ds(core * rh, rh), :]

        def copy(a, core):
            return pltpu.make_async_remote_copy(src_ref=ins[a], dst_ref=rows(a, core), send_sem=send.at[a], recv_sem=recv.at[a],
                                                device_id=(x, y, 1 - c), device_id_type=MESH)

        for a in range(n):
            pltpu.make_async_copy(ins[a], rows(a, c), local.at[a]).start()
            copy(a, c).start()
        for a in range(n):
            copy(a, 1 - c).wait_recv()
        for a in range(n):
            pltpu.make_async_copy(ins[a], rows(a, c), local.at[a]).wait()
            copy(a, c).wait_send()

    return pl.pallas_call(
        body, name="sibling_join", in_specs=[ANY] * n, out_specs=[ANY] * n,
        out_shape=[jax.ShapeDtypeStruct((2 * a.shape[0], a.shape[1]), a.dtype) for a in halves],
        scratch_shapes=[pltpu.SemaphoreType.DMA((n,)), pltpu.SemaphoreType.DMA((n,)), pltpu.SemaphoreType.DMA((n,))],
    )(*halves)


def _allgather_devices(v):
    def body(v_ref, out_ref, send, recv, local):
        x, y, c, _ = _place()
        me = 4 * x + 2 * y + c

        def peer(k):
            return (1 - x if k & 4 else x, 1 - y if k & 2 else y, 1 - c if k & 1 else c)

        def copy(k, slot):
            return pltpu.make_async_remote_copy(src_ref=v_ref, dst_ref=out_ref.at[slot], send_sem=send.at[k - 1],
                                                recv_sem=recv.at[k - 1], device_id=peer(k), device_id_type=MESH)

        own = pltpu.make_async_copy(v_ref, out_ref.at[me], local)
        own.start()
        for k in range(1, N_DEV):
            copy(k, me).start()
        for k in range(1, N_DEV):
            px, py, pc = peer(k)
            copy(k, 4 * px + 2 * py + pc).wait_recv()
        own.wait()
        for k in range(1, N_DEV):
            copy(k, me).wait_send()

    return pl.pallas_call(
        body, name="allgather_devices", in_specs=[ANY], out_specs=ANY,
        out_shape=jax.ShapeDtypeStruct((N_DEV,) + v.shape, v.dtype),
        scratch_shapes=[pltpu.SemaphoreType.DMA((N_DEV - 1,)), pltpu.SemaphoreType.DMA((N_DEV - 1,)), pltpu.SemaphoreType.DMA],
    )(v)


def _adamw(name, w, g, m, v):
    R, C = w.shape
    tr = _tile(R, 256)
    c1 = 1.0 / (1.0 - ADAM_B1 ** ADAM_STEP)
    c2 = 1.0 / (1.0 - ADAM_B2 ** ADAM_STEP)

    def body(w_ref, g_ref, m_ref, v_ref, d_ref, nm_ref, nv_ref):
        gv = g_ref[...]
        nm = ADAM_B1 * m_ref[...] + (1.0 - ADAM_B1) * gv
        nv = ADAM_B2 * v_ref[...] + (1.0 - ADAM_B2) * gv * gv
        d_ref[...] = -ADAM_LR * ((nm * c1) / (jnp.sqrt(nv * c2) + ADAM_EPS) + ADAM_WD * w_ref[...])
        nm_ref[...] = nm
        nv_ref[...] = nv

    row = pl.BlockSpec((tr, C), lambda i: (i, 0))
    sh = jax.ShapeDtypeStruct((R, C), f32)
    return pl.pallas_call(body, name=name, grid=(R // tr,), in_specs=[row] * 4, out_specs=[row] * 3,
                          out_shape=[sh, sh, sh], compiler_params=_params("parallel"))(w, g, m, v)


def _zoh(lr, li, log_dt, b_re, b_im):
    dt = jnp.exp(log_dt)[:, None]
    mag = jnp.exp(lr * dt)
    ab_re = mag * jnp.cos(li * dt)
    ab_im = mag * jnp.sin(li * dt)
    den = lr * lr + li * li
    nr = ab_re - 1.0
    coef_re = (nr * lr + ab_im * li) / den
    coef_im = (ab_im * lr - nr * li) / den
    bb_re = coef_re[..., None] * b_re - coef_im[..., None] * b_im
    bb_im = coef_re[..., None] * b_im + coef_im[..., None] * b_re
    return ab_re, ab_im, bb_re, bb_im


def _s5_tables(ab_re, ab_im, bb_re, bb_im, c_re, c_im):
    eye = jnp.eye(SLAB_GROUPS, dtype=f32)

    def blk_in(bb):
        return jnp.einsum("sgph,gk->sghkp", bb.reshape(N_SLAB, SLAB_GROUPS, SSM_STATE, SSM_GROUP), eye).reshape(
            N_SLAB, SLAB_CH, SLAB_NS)

    def blk_out(cc):
        return jnp.einsum("sghp,gk->skpgh", cc.reshape(N_SLAB, SLAB_GROUPS, SSM_GROUP, SSM_STATE), eye).reshape(
            N_SLAB, SLAB_NS, SLAB_CH)

    bs = jnp.concatenate([blk_in(bb_re), blk_in(bb_im)], axis=2).astype(bf16)
    cs = jnp.concatenate([blk_out(c_re), blk_out(-c_im)], axis=1).astype(bf16)
    pr, pi = [ab_re], [ab_im]
    for _ in range(SUBLANES - 1):
        pr, pi = pr + [pr[-1] * ab_re - pi[-1] * ab_im], pi + [pr[-1] * ab_im + pi[-1] * ab_re]
    pw = jnp.stack([jnp.stack(pr + pr[::-1]), jnp.stack(pi + pi[::-1])])
    pw = pw.reshape(2, 2 * SUBLANES, N_SLAB, SLAB_NS).transpose(2, 0, 1, 3)
    return bs, cs, pw


def _s5_table_grads(dbs, dcs, da):
    eye = jnp.eye(SLAB_GROUPS, dtype=f32)
    d6 = dbs.reshape(N_SLAB, SLAB_GROUPS, SSM_GROUP, 2, SLAB_GROUPS, SSM_STATE)
    dbb = jnp.einsum("sghrkp,gk->rsgph", d6, eye).reshape(2, SSM_GROUPS, SSM_STATE, SSM_GROUP)
    c6 = dcs.reshape(N_SLAB, 2, SLAB_GROUPS, SSM_STATE, SLAB_GROUPS, SSM_GROUP)
    dcc = jnp.einsum("srkpgh,gk->rsghp", c6, eye).reshape(2, SSM_GROUPS, SSM_GROUP, SSM_STATE)
    dab = da.transpose(1, 0, 2).reshape(2, SSM_GROUPS, SSM_STATE)
    return dab[0], dab[1], dbb[0], dbb[1], dcc[0], -dcc[1]


SMALL = ["mix_norm_g", "ssm_lambda_re", "ssm_lambda_im", "ssm_log_dt", "ssm_b_re", "ssm_b_im", "ssm_c_re", "ssm_c_im",
         "ssm_d", "hgrn_lb_logits", "hgrn_norm_g", "ffn_norm_g", "conv_b", "final_norm_g"]
SHARDED_SMALL = ["meta_tokens", "conv_w"]
BIG = ["w_in", "ssm_w_glu", "w_ssm_proj", "w_hgrn_proj", "w_out", "w_up", "w_down"]
WEIGHTS = ['meta_tokens', 'mix_norm_g', 'w_in', 'ssm_lambda_re', 'ssm_lambda_im', 'ssm_log_dt', 'ssm_b_re', 'ssm_b_im',
           'ssm_c_re', 'ssm_c_im', 'ssm_d', 'ssm_w_glu', 'w_ssm_proj', 'hgrn_lb_logits', 'hgrn_norm_g', 'w_hgrn_proj',
           'w_out', 'ffn_norm_g', 'w_up', 'conv_w', 'conv_b', 'w_down', 'final_norm_g']


def _local_grads(x, tgt, meta, w, full):
    B, S, Dm = x.shape
    L = S + N_META
    T = B * L
    h0 = jnp.concatenate([jnp.broadcast_to(meta[None], (B, N_META, Dm)), x], axis=1).reshape(T, Dm)

    lb_all = jax.nn.softmax(w["hgrn_lb_logits"], axis=0)
    lb = lb_all[0:1]
    zoh_out, zoh_vjp = jax.vjp(_zoh, w["ssm_lambda_re"][0], w["ssm_lambda_im"][0], w["ssm_log_dt"][0],
                               w["ssm_b_re"][0], w["ssm_b_im"][0])
    bs, cs, pw = _s5_tables(*zoh_out, w["ssm_c_re"][0], w["ssm_c_im"][0])

    z1 = _rmsnorm_fwd("mix_norm", h0, w["mix_norm_g"])
    p = _mm_rows("in_proj", z1, full["w_in"], "nn", f32, 1024)
    ya0 = _s5_fwd(p, bs, cs, pw, w["ssm_d"], B, L)
    gl = _mm_rows("glu_proj", ya0, full["ssm_w_glu"], "nn", f32, 1024)
    ya = _glu_fwd(ya0, gl)
    yb = _hgrn_fwd(p, lb, w["hgrn_norm_g"], B, L)
    pa = _mm_rows("ssm_proj", ya, full["w_ssm_proj"], "nn", f32, 1024)
    pb = _mm_rows("hgrn_proj", yb, full["w_hgrn_proj"], "nn", f32, 1024)
    merged = _merge_fwd(p, pa, pb)
    h1 = _mm_rows("out_proj", merged, full["w_out"], "nn", f32, 1024, res=h0)
    z2 = _rmsnorm_fwd("ffn_norm", h1, w["ffn_norm_g"])
    up = _mm_rows("up_proj", z2, full["w_up"], "nn", f32, D_FF // 2)
    ff = _conv_fwd(up, full["conv_w"], w["conv_b"], B, L)
    h2 = _mm_rows("down_proj", ff, full["w_down"], "nn", f32, 1024, res=h1, tk=D_FF // 2)

    h2x = h2.reshape(B, L, Dm)[:, N_META:].reshape(B * S, Dm)
    dh2x, loss, d_final_g = _final_loss(h2x, tgt.reshape(B * S, Dm), w["final_norm_g"].reshape(1, Dm))
    dh2 = jnp.pad(dh2x.reshape(B, S, Dm), ((0, 0), (N_META, 0), (0, 0))).reshape(T, Dm)

    dff = _mm_rows("d_ff", dh2, full["w_down"], "nt", f32, D_FF // 2)
    g_w_down = _mm_wgrad("dw_down", ff, dh2, tn=512)
    dup, dconv = _conv_bwd(up, dff, full["conv_w"], w["conv_b"], B, L)
    dz2 = _dz2(dup, full["w_up"])
    g_w_up = _dw_up(z2, dup)
    dh1, d_ffn_g = _rmsnorm_bwd("ffn_norm_bwd", h1, w["ffn_norm_g"], dz2, dh2)

    dmerged = _mm_rows("d_merged", dh1, full["w_out"], "nt", f32, 1024)
    g_w_out = _mm_wgrad("dw_out", merged, dh1)
    dpa, dpb, dp = _merge_bwd(dmerged, p, pa, pb)
    dya = _mm_rows("d_ya", dpa, full["w_ssm_proj"], "nt", f32, 1024)
    g_w_ssm_proj = _mm_wgrad("dw_ssm_proj", ya, dpa)
    dyb = _mm_rows("d_yb", dpb, full["w_hgrn_proj"], "nt", f32, 1024)
    g_w_hgrn_proj = _mm_wgrad("dw_hgrn_proj", yb, dpb)
    dp, d_lb, d_hgrn_g = _hgrn_bwd(p, dyb, dp, lb, w["hgrn_norm_g"], B, L)
    dgl, dya0_direct = _glu_bwd(dya, ya0, gl)
    dya0 = _mm_rows("d_ya0", dgl, full["ssm_w_glu"], "nt", f32, 1024, res=dya0_direct)
    g_w_glu = _mm_wgrad("dw_glu", ya0, dgl)
    dp, dbs, dcs, da, d_skip = _s5_bwd(p, dya0, dp, bs, cs, pw, w["ssm_d"], B, L)
    dz1 = _dz1(dp, full["w_in"])
    g_w_in = _dw_in(z1, dp)
    dh0, d_mix_g = _rmsnorm_bwd("mix_norm_bwd", h0, w["mix_norm_g"], dz1, dh1)

    dh0 = dh0.reshape(B, L, Dm)
    grad_x = dh0[:, N_META:]
    d_meta = _meta_grad(dh0[:, :N_META])

    d_ab_re, d_ab_im, d_bb_re, d_bb_im, d_c_re, d_c_im = _s5_table_grads(dbs, dcs, da)
    d_lr, d_li, d_log_dt, d_b_re, d_b_im = zoh_vjp((d_ab_re, d_ab_im, d_bb_re, d_bb_im))
    sm0, sm1 = lb_all[0:1], lb_all[1:2]
    d_logits = jnp.concatenate([sm0 * (1.0 - sm0) * d_lb, -sm0 * sm1 * d_lb], axis=0)
    small = {
        "meta_tokens": d_meta, "mix_norm_g": d_mix_g, "ssm_lambda_re": d_lr[None], "ssm_lambda_im": d_li[None],
        "ssm_log_dt": d_log_dt[None], "ssm_b_re": d_b_re[None], "ssm_b_im": d_b_im[None], "ssm_c_re": d_c_re[None],
        "ssm_c_im": d_c_im[None], "ssm_d": d_skip, "hgrn_lb_logits": d_logits, "hgrn_norm_g": d_hgrn_g,
        "ffn_norm_g": d_ffn_g, "conv_w": dconv[:, 0:3, :].transpose(1, 0, 2).reshape(3, 2 * D_FF),
        "conv_b": dconv[:, 3, :].reshape(1, 2 * D_FF), "final_norm_g": d_final_g.reshape(Dm),
    }
    big = {
        "w_in": g_w_in, "ssm_w_glu": g_w_glu.reshape(N_CHIPS, Dm // N_CHIPS, Dm),
        "w_ssm_proj": g_w_ssm_proj.reshape(N_CHIPS, Dm // N_CHIPS, Dm),
        "w_hgrn_proj": g_w_hgrn_proj.reshape(N_CHIPS, Dm // N_CHIPS, Dm), "w_out": g_w_out.reshape(N_CHIPS, Dm // N_CHIPS, Dm),
        "w_up": g_w_up, "w_down": g_w_down.reshape(N_CHIPS, D_FF // N_CHIPS, Dm),
    }
    return loss, grad_x, big, small


def _pack(parts):
    flat = jnp.concatenate([parts[k].reshape(-1) for k in parts])
    n = flat.shape[0]
    rows = -(-n // (SUBLANES * LANES)) * SUBLANES
    flat = jnp.pad(flat, (0, rows * LANES - n))
    return flat.reshape(rows, LANES)


def _unpack(packed, like):
    flat = packed.reshape(-1)
    out, o = {}, 0
    for k, ref in like.items():
        n = math.prod(ref.shape)
        out[k] = flat[o:o + n].reshape(ref.shape)
        o += n
    return out


def kernel(x, meta_tokens, mix_norm_g, w_in, ssm_lambda_re, ssm_lambda_im, ssm_log_dt, ssm_b_re, ssm_b_im, ssm_c_re, ssm_c_im, ssm_d, ssm_w_glu, w_ssm_proj, hgrn_lb_logits, hgrn_norm_g, w_hgrn_proj, w_out, ffn_norm_g, w_up, conv_w, conv_b, w_down, final_norm_g, loss_target, m_meta_tokens, m_mix_norm_g, m_w_in, m_ssm_lambda_re, m_ssm_lambda_im, m_ssm_log_dt, m_ssm_b_re, m_ssm_b_im, m_ssm_c_re, m_ssm_c_im, m_ssm_d, m_ssm_w_glu, m_w_ssm_proj, m_hgrn_lb_logits, m_hgrn_norm_g, m_w_hgrn_proj, m_w_out, m_ffn_norm_g, m_w_up, m_conv_w, m_conv_b, m_w_down, m_final_norm_g, v_meta_tokens, v_mix_norm_g, v_w_in, v_ssm_lambda_re, v_ssm_lambda_im, v_ssm_log_dt, v_ssm_b_re, v_ssm_b_im, v_ssm_c_re, v_ssm_c_im, v_ssm_d, v_ssm_w_glu, v_w_ssm_proj, v_hgrn_lb_logits, v_hgrn_norm_g, v_w_hgrn_proj, v_w_out, v_ffn_norm_g, v_w_up, v_conv_w, v_conv_b, v_w_down, v_final_norm_g):
    args = dict(locals())
    w = {k: args[k] for k in WEIGHTS}
    mom = {k: args["m_" + k] for k in WEIGHTS}
    var = {k: args["v_" + k] for k in WEIGHTS}
    Dm = D_MODEL
    cx, cy, cc = lax.axis_index("x"), lax.axis_index("y"), lax.axis_index("c")
    chip = 2 * cx + cy

    shards = [w["w_in"][0].astype(bf16), w["ssm_w_glu"][0].astype(bf16), w["w_ssm_proj"][0].astype(bf16),
              w["w_hgrn_proj"][0].astype(bf16), w["w_out"][0].astype(bf16), w["w_up"][0].astype(bf16),
              w["w_down"][0].astype(bf16), w["meta_tokens"], w["conv_w"][0]]
    g_in, g_glu, g_sp, g_hp, g_out, g_up, g_down, g_meta, g_cw = _allgather_chips(shards)
    w_in_full = jnp.roll(g_in.transpose(1, 0, 2).reshape(Dm, IN_COLS), -Dm, axis=1)
    full = {
        "w_in": w_in_full, "ssm_w_glu": g_glu.reshape(Dm, Dm), "w_ssm_proj": g_sp.reshape(Dm, Dm),
        "w_hgrn_proj": g_hp.reshape(Dm, Dm), "w_out": g_out.reshape(Dm, Dm),
        "w_up": g_up.transpose(1, 0, 2).reshape(Dm, 2 * D_FF), "w_down": g_down.reshape(D_FF, Dm),
        "conv_w": g_cw.transpose(1, 0, 2).reshape(3, 2 * D_FF),
    }
    meta_full = g_meta.transpose(1, 0, 2).reshape(N_META, Dm)

    loss_part, grad_x, big, small = _local_grads(x, loss_target, meta_full, w, full)

    core = cc.reshape(1).astype(jnp.int32)
    parts = [big[k] for k in BIG]
    got = _sibling_halves(parts)
    sums = [_add_own_half("add_half_" + k, pt, gt, core) for k, pt, gt in zip(BIG, parts, got)]
    slots = _chip_exchange(sums)
    halves = [_sum_slots("sum_chips_" + k, s) for k, s in zip(BIG, slots)]
    g_big = dict(zip(BIG, _sibling_join(halves)))

    small_all = dict(small)
    small_all["loss"] = loss_part[0, 0:1]
    packed = _pack(small_all)
    reduced = _unpack(_sum_slots("sum_devices", _allgather_devices(packed)), small_all)
    loss = reduced.pop("loss")[0]
    mcols = Dm // N_CHIPS
    ccols = 2 * D_FF // N_CHIPS
    grads = {k: reduced[k] for k in SMALL}
    grads["meta_tokens"] = lax.dynamic_slice(reduced["meta_tokens"], (0, chip * mcols), (N_META, mcols))
    grads["conv_w"] = lax.dynamic_slice(reduced["conv_w"], (0, chip * ccols), (3, ccols))[None]
    for k in BIG:
        grads[k] = g_big[k][None]

    delta, new_m, new_v = {}, {}, {}
    for k in BIG:
        shp = w[k].shape
        d, nm, nv = _adamw("adamw_" + k, w[k][0], grads[k][0], mom[k][0], var[k][0])
        delta[k], new_m[k], new_v[k] = d.reshape(shp), nm.reshape(shp), nv.reshape(shp)
    rest = SMALL + SHARDED_SMALL
    pk = [_pack({k: t[k] for k in rest}) for t in (w, grads, mom, var)]
    outs = _adamw("adamw_small", *pk)
    like = {k: w[k] for k in rest}
    for dst, o in zip((delta, new_m, new_v), outs):
        dst.update(_unpack(o, like))

    return (loss, grad_x, *[grads[k].reshape(w[k].shape) for k in WEIGHTS], *[delta[k] for k in WEIGHTS],
            *[new_m[k] for k in WEIGHTS], *[new_v[k] for k in WEIGHTS])
```

```python
import functools
import math

import jax
import jax.numpy as jnp
from jax import lax
from jax.experimental import pallas as pl
from jax.experimental.pallas import tpu as pltpu

f32 = jnp.float32
bf16 = jnp.bfloat16

D_MODEL = 1024
N_META = 16
SSM_GROUP = 16
SSM_GROUPS = 64
SSM_STATE = 64
SLAB_GROUPS = 8
N_SLAB = SSM_GROUPS // SLAB_GROUPS
SLAB_CH = SLAB_GROUPS * SSM_GROUP
SLAB_NS = SLAB_GROUPS * SSM_STATE
HEADS = 8
HEAD_DIM = 128
CHUNK = 16
D_FF = 2816
IN_COLS = 7168
EPS = 1e-6
SUBLANES = 8
LANES = 128
N_CHIPS = 4
N_DEV = 8
ADAM_LR, ADAM_B1, ADAM_B2, ADAM_EPS, ADAM_WD, ADAM_STEP = 0.001, 0.9, 0.999, 1e-08, 0.01, 10
MESH = pl.DeviceIdType.MESH
ANY = pl.BlockSpec(memory_space=pl.ANY)

SEG_Q, SEG_F, SEG_I, SEG_OG, SEG_GA, SEG_GB, SEG_U = range(7)
N_SEG = 7


def _tile(n, target, mult=SUBLANES):
    best = None
    for d in range(mult, min(n, target) + 1, mult):
        if n % d == 0:
            best = d
    return n if best is None else best


def _params(*sem):
    return pltpu.CompilerParams(dimension_semantics=sem)


def _sigmoid(x):
    return 1.0 / (1.0 + jnp.exp(-x))


_DIMS = {"nn": (((1,), (0,)), ((), ())), "nt": (((1,), (1,)), ((), ())), "tn": (((0,), (0,)), ((), ()))}


def _mm(name, a, b, dims, grid, a_spec, b_spec, out_shape, out_spec, acc_shape, res=None, res_spec=None):
    nk = grid[2]
    dn = _DIMS[dims]

    def body(*refs):
        if res is None:
            a_ref, b_ref, o_ref, acc = refs
        else:
            a_ref, b_ref, r_ref, o_ref, acc = refs
        k = pl.program_id(2)

        @pl.when(k == 0)
        def _():
            acc[...] = jnp.zeros_like(acc)

        acc[...] += lax.dot_general(a_ref[...].astype(bf16), b_ref[...].astype(bf16), dn, preferred_element_type=f32)

        @pl.when(k == nk - 1)
        def _():
            r = acc[...]
            if res is not None:
                r = r + r_ref[...]
            o_ref[...] = r.astype(o_ref.dtype)

    ins = [a, b] + ([] if res is None else [res])
    specs = [a_spec, b_spec] + ([] if res is None else [res_spec])
    return pl.pallas_call(
        body, name=name, grid=grid, in_specs=specs, out_specs=out_spec, out_shape=out_shape,
        scratch_shapes=[pltpu.VMEM(acc_shape, f32)],
        compiler_params=_params("parallel", "parallel", "arbitrary"),
    )(*ins)


def _mm_rows(name, a, w, dims, out_dtype, tn, res=None, tk=None):
    T, K = a.shape
    N = w.shape[1] if dims == "nn" else w.shape[0]
    tm = _tile(T, 1032)
    tk = K if tk is None else tk
    grid = (T // tm, N // tn, K // tk)
    a_spec = pl.BlockSpec((tm, tk), lambda i, j, k: (i, k))
    if dims == "nn":
        b_spec = pl.BlockSpec((tk, tn), lambda i, j, k: (k, j))
    else:
        b_spec = pl.BlockSpec((tn, tk), lambda i, j, k: (j, k))
    o_spec = pl.BlockSpec((tm, tn), lambda i, j, k: (i, j))
    return _mm(name, a, w, dims, grid, a_spec, b_spec, jax.ShapeDtypeStruct((T, N), out_dtype), o_spec, (tm, tn),
               res=res, res_spec=None if res is None else o_spec)


def _mm_wgrad(name, a, g, tn=None):
    T, K = a.shape
    N = g.shape[1]
    tk = _tile(T, 688)
    tn = N if tn is None else tn
    grid = (1, N // tn, T // tk)
    a_spec = pl.BlockSpec((tk, K), lambda i, j, k: (k, 0))
    g_spec = pl.BlockSpec((tk, tn), lambda i, j, k: (k, j))
    o_spec = pl.BlockSpec((K, tn), lambda i, j, k: (0, j))
    return _mm(name, a, g, "tn", grid, a_spec, g_spec, jax.ShapeDtypeStruct((K, N), f32), o_spec, (K, tn))


def _rmsnorm_fwd(name, x, g):
    T, Dm = x.shape
    tr = _tile(T, 688)

    def body(x_ref, g_ref, z_ref):
        xv = x_ref[...]
        r = lax.rsqrt(jnp.mean(xv * xv, axis=-1, keepdims=True) + EPS)
        z_ref[...] = (xv * r * g_ref[...]).astype(z_ref.dtype)

    return pl.pallas_call(
        body, name=name, grid=(T // tr,),
        in_specs=[pl.BlockSpec((tr, Dm), lambda i: (i, 0)), pl.BlockSpec((1, Dm), lambda i: (0, 0))],
        out_specs=pl.BlockSpec((tr, Dm), lambda i: (i, 0)),
        out_shape=jax.ShapeDtypeStruct((T, Dm), bf16), compiler_params=_params("parallel"),
    )(x, g)


def _rmsnorm_bwd(name, x, g, dz, dres):
    T, Dm = x.shape
    tr = _tile(T, 688)

    def body(x_ref, g_ref, dz_ref, dres_ref, dx_ref, dg_ref):
        xv = x_ref[...]
        r = lax.rsqrt(jnp.mean(xv * xv, axis=-1, keepdims=True) + EPS)
        xn = xv * r
        dzv = dz_ref[...]
        dzg = dzv * g_ref[...]
        dx_ref[...] = dres_ref[...] + r * (dzg - xn * jnp.mean(dzg * xn, axis=-1, keepdims=True))

        @pl.when(pl.program_id(0) == 0)
        def _():
            dg_ref[...] = jnp.zeros_like(dg_ref)

        dg_ref[...] += jnp.sum(dzv * xn, axis=0, keepdims=True)

    row = pl.BlockSpec((tr, Dm), lambda i: (i, 0))
    par = pl.BlockSpec((1, Dm), lambda i: (0, 0))
    return pl.pallas_call(
        body, name=name, grid=(T // tr,), in_specs=[row, par, row, row], out_specs=[row, par],
        out_shape=[jax.ShapeDtypeStruct((T, Dm), f32), jax.ShapeDtypeStruct((1, Dm), f32)],
        compiler_params=_params("arbitrary"),
    )(x, g, dz, dres)


def _glu_fwd(ya0, gl):
    T, Dm = ya0.shape
    tr = _tile(T, 688)

    def body(y_ref, g_ref, o_ref):
        o_ref[...] = (y_ref[...] * _sigmoid(g_ref[...])).astype(o_ref.dtype)

    row = pl.BlockSpec((tr, Dm), lambda i: (i, 0))
    return pl.pallas_call(body, name="glu_fwd", grid=(T // tr,), in_specs=[row, row], out_specs=row,
                          out_shape=jax.ShapeDtypeStruct((T, Dm), bf16), compiler_params=_params("parallel"))(ya0, gl)


def _glu_bwd(dya, ya0, gl):
    T, Dm = ya0.shape
    tr = _tile(T, 688)

    def body(d_ref, y_ref, g_ref, dg_ref, dy_ref):
        s = _sigmoid(g_ref[...])
        d = d_ref[...]
        dg_ref[...] = (d * y_ref[...] * s * (1.0 - s)).astype(dg_ref.dtype)
        dy_ref[...] = d * s

    row = pl.BlockSpec((tr, Dm), lambda i: (i, 0))
    return pl.pallas_call(body, name="glu_bwd", grid=(T // tr,), in_specs=[row, row, row], out_specs=[row, row],
                          out_shape=[jax.ShapeDtypeStruct((T, Dm), bf16), jax.ShapeDtypeStruct((T, Dm), f32)],
                          compiler_params=_params("parallel"))(dya, ya0, gl)


def _merge_fwd(p, pa, pb):
    T, Dm = pa.shape
    tr = _tile(T, 688)

    def body(ga_ref, gb_ref, pa_ref, pb_ref, o_ref):
        o_ref[...] = (_sigmoid(ga_ref[...]) * pa_ref[...] + _sigmoid(gb_ref[...]) * pb_ref[...]).astype(o_ref.dtype)

    row = pl.BlockSpec((tr, Dm), lambda i: (i, 0))
    return pl.pallas_call(
        body, name="merge_fwd", grid=(T // tr,),
        in_specs=[pl.BlockSpec((tr, Dm), lambda i: (i, SEG_GA)), pl.BlockSpec((tr, Dm), lambda i: (i, SEG_GB)), row, row],
        out_specs=row, out_shape=jax.ShapeDtypeStruct((T, Dm), bf16), compiler_params=_params("parallel"),
    )(p, p, pa, pb)


def _merge_bwd(dm, p, pa, pb):
    T, Dm = pa.shape
    tr = _tile(T, 688)

    def body(dm_ref, ga_ref, gb_ref, pa_ref, pb_ref, dpa_ref, dpb_ref, dp_ref):
        d = dm_ref[...]
        sa = _sigmoid(ga_ref[...])
        sb = _sigmoid(gb_ref[...])
        dpa_ref[...] = (d * sa).astype(dpa_ref.dtype)
        dpb_ref[...] = (d * sb).astype(dpb_ref.dtype)
        dp_ref[0] = (d * pa_ref[...] * sa * (1.0 - sa)).astype(dp_ref.dtype)
        dp_ref[1] = (d * pb_ref[...] * sb * (1.0 - sb)).astype(dp_ref.dtype)

    row = pl.BlockSpec((tr, Dm), lambda i: (i, 0))
    return pl.pallas_call(
        body, name="merge_bwd", grid=(T // tr,),
        in_specs=[row, pl.BlockSpec((tr, Dm), lambda i: (i, SEG_GA)), pl.BlockSpec((tr, Dm), lambda i: (i, SEG_GB)), row, row],
        out_specs=[row, row, pl.BlockSpec((2, tr, Dm), lambda i: (SEG_GA // 2, i, 0))],
        out_shape=[jax.ShapeDtypeStruct((T, Dm), bf16), jax.ShapeDtypeStruct((T, Dm), bf16),
                   jax.ShapeDtypeStruct((N_SEG, T, Dm), bf16)],
        compiler_params=_params("parallel"),
    )(dm, p, p, pa, pb)


def _final_loss(h2x, tgt, g):
    T, Dm = h2x.shape
    tr = _tile(T, 512)

    def body(h_ref, t_ref, g_ref, dh_ref, loss_ref, dg_ref):
        hv = h_ref[...]
        r = lax.rsqrt(jnp.mean(hv * hv, axis=-1, keepdims=True) + EPS)
        xn = hv * r
        gv = g_ref[...]
        err = xn * gv - t_ref[...]
        dy = err * (1.0 / Dm)
        dyg = dy * gv
        dh_ref[...] = r * (dyg - xn * jnp.mean(dyg * xn, axis=-1, keepdims=True))

        @pl.when(pl.program_id(0) == 0)
        def _():
            dg_ref[...] = jnp.zeros_like(dg_ref)
            loss_ref[...] = jnp.zeros_like(loss_ref)

        dg_ref[...] += jnp.sum(dy * xn, axis=0, keepdims=True)
        loss_ref[...] += jnp.sum(err * err) * (0.5 / Dm)

    row = pl.BlockSpec((tr, Dm), lambda i: (i, 0))
    par = pl.BlockSpec((1, Dm), lambda i: (0, 0))
    return pl.pallas_call(
        body, name="final_loss", grid=(T // tr,), in_specs=[row, row, par],
        out_specs=[row, pl.BlockSpec((1, LANES), lambda i: (0, 0)), par],
        out_shape=[jax.ShapeDtypeStruct((T, Dm), f32), jax.ShapeDtypeStruct((1, LANES), f32), jax.ShapeDtypeStruct((1, Dm), f32)],
        compiler_params=_params("arbitrary"),
    )(h2x, tgt, g)


def _meta_grad(dh0_meta):
    B = dh0_meta.shape[0]

    def body(d_ref, o_ref):
        acc = d_ref[0]
        for b in range(1, B):
            acc = acc + d_ref[b]
        o_ref[...] = acc

    return pl.pallas_call(body, name="meta_grad", out_shape=jax.ShapeDtypeStruct(dh0_meta.shape[1:], f32))(dh0_meta)


def _shift_down(x, k, row):
    return jnp.where(row >= k, pltpu.roll(x, k, 0), 0.0)


def _shift_up(x, k, row):
    n = x.shape[0]
    return jnp.where(row < n - k, pltpu.roll(x, n - k, 0), 0.0)


def _conv_fwd(up, conv_w, conv_b, B, L):
    tc = 256
    nt = D_FF // tc

    def body(xa_ref, xb_ref, wa_ref, wb_ref, ba_ref, bb_ref, o_ref):
        row = lax.broadcasted_iota(jnp.int32, (L, tc), 0)

        def conv(x_ref, w_ref, b_ref):
            x = x_ref[...]
            return (b_ref[...] + w_ref[0:1, :] * _shift_down(x, 2, row) + w_ref[1:2, :] * _shift_down(x, 1, row)
                    + w_ref[2:3, :] * x)

        a = conv(xa_ref, wa_ref, ba_ref)
        b = conv(xb_ref, wb_ref, bb_ref)
        o_ref[...] = (a * _sigmoid(a) * b).astype(o_ref.dtype)

    return pl.pallas_call(
        body, name="conv_fwd", grid=(B, nt),
        in_specs=[pl.BlockSpec((L, tc), lambda b, j: (b, j)), pl.BlockSpec((L, tc), lambda b, j: (b, j + nt)),
                  pl.BlockSpec((3, tc), lambda b, j: (0, j)), pl.BlockSpec((3, tc), lambda b, j: (0, j + nt)),
                  pl.BlockSpec((1, tc), lambda b, j: (0, j)), pl.BlockSpec((1, tc), lambda b, j: (0, j + nt))],
        out_specs=pl.BlockSpec((L, tc), lambda b, j: (b, j)),
        out_shape=jax.ShapeDtypeStruct((B * L, D_FF), bf16), compiler_params=_params("parallel", "parallel"),
    )(up, up, conv_w, conv_w, conv_b, conv_b)


def _conv_bwd(up, dff, conv_w, conv_b, B, L):
    tc = 256
    nt = D_FF // tc

    def body(xa_ref, xb_ref, d_ref, wa_ref, wb_ref, ba_ref, bb_ref, dup_ref, dw_ref):
        row = lax.broadcasted_iota(jnp.int32, (L, tc), 0)
        xs, pre = [], []
        for x_ref, w_ref, b_ref in ((xa_ref, wa_ref, ba_ref), (xb_ref, wb_ref, bb_ref)):
            x = x_ref[...]
            x1 = _shift_down(x, 1, row)
            x2 = _shift_down(x, 2, row)
            xs.append((x, x1, x2))
            pre.append(b_ref[...] + w_ref[0:1, :] * x2 + w_ref[1:2, :] * x1 + w_ref[2:3, :] * x)
        a, b = pre
        s = _sigmoid(a)
        d = d_ref[...]
        grads = (d * b * s * (1.0 + a * (1.0 - s)), d * a * s)

        @pl.when(pl.program_id(1) == 0)
        def _():
            dw_ref[...] = jnp.zeros_like(dw_ref)

        for h, (gr, (x, x1, x2), w_ref) in enumerate(zip(grads, xs, (wa_ref, wb_ref))):
            dup_ref[h] = (w_ref[2:3, :] * gr + w_ref[1:2, :] * _shift_up(gr, 1, row)
                          + w_ref[0:1, :] * _shift_up(gr, 2, row)).astype(dup_ref.dtype)
            dw_ref[h, 0:1, :] += jnp.sum(gr * x2, axis=0, keepdims=True)
            dw_ref[h, 1:2, :] += jnp.sum(gr * x1, axis=0, keepdims=True)
            dw_ref[h, 2:3, :] += jnp.sum(gr * x, axis=0, keepdims=True)
            dw_ref[h, 3:4, :] += jnp.sum(gr, axis=0, keepdims=True)

    return pl.pallas_call(
        body, name="conv_bwd", grid=(nt, B),
        in_specs=[pl.BlockSpec((L, tc), lambda j, b: (b, j)), pl.BlockSpec((L, tc), lambda j, b: (b, j + nt)),
                  pl.BlockSpec((L, tc), lambda j, b: (b, j)),
                  pl.BlockSpec((3, tc), lambda j, b: (0, j)), pl.BlockSpec((3, tc), lambda j, b: (0, j + nt)),
                  pl.BlockSpec((1, tc), lambda j, b: (0, j)), pl.BlockSpec((1, tc), lambda j, b: (0, j + nt))],
        out_specs=[pl.BlockSpec((2, L, tc), lambda j, b: (0, b, j)), pl.BlockSpec((2, SUBLANES, tc), lambda j, b: (0, 0, j))],
        out_shape=[jax.ShapeDtypeStruct((2, B * L, D_FF), bf16), jax.ShapeDtypeStruct((2, SUBLANES, D_FF), f32)],
        compiler_params=_params("parallel", "arbitrary"),
    )(up, up, dff, conv_w, conv_w, conv_b, conv_b)


GELU_C = math.sqrt(2.0 / math.pi)
GELU_A = 0.044715


def _gelu(x):
    return 0.5 * x * (1.0 + jnp.tanh(GELU_C * (x + GELU_A * x * x * x)))


def _gelu_grad(x):
    t = jnp.tanh(GELU_C * (x + GELU_A * x * x * x))
    return 0.5 * (1.0 + t) + 0.5 * x * (1.0 - t * t) * GELU_C * (1.0 + 3.0 * GELU_A * x * x)


def _cmul_add(xr, xi, ar, ai, sr, si):
    return xr + ar * sr - ai * si, xi + ar * si + ai * sr


def _s5_scan_fwd(s_ref, pw_ref, L):
    ns = SLAB_NS
    row = lax.broadcasted_iota(jnp.int32, (SUBLANES, ns), 0)
    pr = pw_ref[0, 0:SUBLANES, :]
    pi = pw_ref[1, 0:SUBLANES, :]

    def step(i, carry):
        cr, ci = carry
        r0 = pl.multiple_of(i * SUBLANES, SUBLANES)
        xr = s_ref[pl.ds(r0, SUBLANES), 0:ns]
        xi = s_ref[pl.ds(r0, SUBLANES), ns:2 * ns]
        for k in (1, 2, 4):
            xr, xi = _cmul_add(xr, xi, pr[k - 1:k, :], pi[k - 1:k, :], _shift_down(xr, k, row), _shift_down(xi, k, row))
        xr, xi = _cmul_add(xr, xi, pr, pi, cr, ci)
        s_ref[pl.ds(r0, SUBLANES), 0:ns] = xr
        s_ref[pl.ds(r0, SUBLANES), ns:2 * ns] = xi
        return xr[SUBLANES - 1:SUBLANES, :], xi[SUBLANES - 1:SUBLANES, :]

    z = jnp.zeros((1, ns), f32)
    lax.fori_loop(0, L // SUBLANES, step, (z, z))


def _s5_project_in(u_ref, bs_ref, s_ref, L, rc):
    for r in range(0, L, rc):
        s_ref[r:r + rc, :] = jnp.dot(u_ref[r:r + rc, :].astype(bf16), bs_ref[...], preferred_element_type=f32)


def _s5_fwd(p, bs, cs, pw, d_skip, B, L):
    rc = _tile(L, 344)

    def body(u_ref, bs_ref, cs_ref, pw_ref, d_ref, y_ref, s_ref):
        _s5_project_in(u_ref, bs_ref, s_ref, L, rc)
        _s5_scan_fwd(s_ref, pw_ref, L)
        for r in range(0, L, rc):
            ypre = (jnp.dot(s_ref[r:r + rc, :].astype(bf16), cs_ref[...], preferred_element_type=f32)
                    + d_ref[...] * u_ref[r:r + rc, :])
            y_ref[r:r + rc, :] = _gelu(ypre)

    ucol = SEG_U * (D_MODEL // SLAB_CH)
    return pl.pallas_call(
        body, name="s5_fwd", grid=(B, N_SLAB),
        in_specs=[pl.BlockSpec((L, SLAB_CH), lambda b, s: (b, ucol + s)),
                  pl.BlockSpec((None, SLAB_CH, 2 * SLAB_NS), lambda b, s: (s, 0, 0)),
                  pl.BlockSpec((None, 2 * SLAB_NS, SLAB_CH), lambda b, s: (s, 0, 0)),
                  pl.BlockSpec((None, 2, 2 * SUBLANES, SLAB_NS), lambda b, s: (s, 0, 0, 0)),
                  pl.BlockSpec((1, SLAB_CH), lambda b, s: (0, s))],
        out_specs=pl.BlockSpec((L, SLAB_CH), lambda b, s: (b, s)),
        out_shape=jax.ShapeDtypeStruct((B * L, D_MODEL), f32),
        scratch_shapes=[pltpu.VMEM((L, 2 * SLAB_NS), f32)],
        compiler_params=_params("parallel", "parallel"),
    )(p, bs, cs, pw, d_skip)


def _s5_bwd(p, dya0, dp, bs, cs, pw, d_skip, B, L):
    rc = _tile(L, 344)
    ns = SLAB_NS
    nt = L // SUBLANES

    def body(u_ref, dy_ref, dp_in, bs_ref, cs_ref, pw_ref, d_ref, du_ref, dbs_ref, dcs_ref, da_ref, dd_ref,
             s_ref, lam_ref, dyp_ref):
        del dp_in
        b = pl.program_id(1)

        @pl.when(b == 0)
        def _():
            dbs_ref[...] = jnp.zeros_like(dbs_ref)
            dcs_ref[...] = jnp.zeros_like(dcs_ref)
            da_ref[...] = jnp.zeros_like(da_ref)
            dd_ref[...] = jnp.zeros_like(dd_ref)

        _s5_project_in(u_ref, bs_ref, s_ref, L, rc)
        _s5_scan_fwd(s_ref, pw_ref, L)
        for r in range(0, L, rc):
            u = u_ref[r:r + rc, :]
            sb = s_ref[r:r + rc, :].astype(bf16)
            ypre = jnp.dot(sb, cs_ref[...], preferred_element_type=f32) + d_ref[...] * u
            dyp = dy_ref[r:r + rc, :] * _gelu_grad(ypre)
            dyp_ref[r:r + rc, :] = dyp
            dd_ref[...] += jnp.sum(dyp * u, axis=0, keepdims=True)
            dypb = dyp.astype(bf16)
            dcs_ref[...] += lax.dot_general(sb, dypb, _DIMS["tn"], preferred_element_type=f32)
            lam_ref[r:r + rc, :] = lax.dot_general(dypb, cs_ref[...], _DIMS["nt"], preferred_element_type=f32)

        row = lax.broadcasted_iota(jnp.int32, (SUBLANES, ns), 0)
        pr = pw_ref[0, 0:SUBLANES, :]
        pi = -pw_ref[1, 0:SUBLANES, :]
        qr = pw_ref[0, SUBLANES:2 * SUBLANES, :]
        qi = -pw_ref[1, SUBLANES:2 * SUBLANES, :]

        def step(j, carry):
            cr, ci, ar, ai = carry
            i = nt - 1 - j
            r0 = pl.multiple_of(i * SUBLANES, SUBLANES)
            xr = lam_ref[pl.ds(r0, SUBLANES), 0:ns]
            xi = lam_ref[pl.ds(r0, SUBLANES), ns:2 * ns]
            for k in (1, 2, 4):
                xr, xi = _cmul_add(xr, xi, pr[k - 1:k, :], pi[k - 1:k, :], _shift_up(xr, k, row), _shift_up(xi, k, row))
            xr, xi = _cmul_add(xr, xi, qr, qi, cr, ci)
            lam_ref[pl.ds(r0, SUBLANES), 0:ns] = xr
            lam_ref[pl.ds(r0, SUBLANES), ns:2 * ns] = xi
            rp = pl.multiple_of(jnp.maximum(i - 1, 0) * SUBLANES, SUBLANES)
            live = jnp.where(i > 0, 1.0, 0.0)
            lr_ = s_ref[pl.ds(rp + SUBLANES - 1, 1), 0:ns] * live
            li_ = s_ref[pl.ds(rp + SUBLANES - 1, 1), ns:2 * ns] * live
            spr = jnp.where(row == 0, lr_, pltpu.roll(s_ref[pl.ds(r0, SUBLANES), 0:ns], 1, 0))
            spi = jnp.where(row == 0, li_, pltpu.roll(s_ref[pl.ds(r0, SUBLANES), ns:2 * ns], 1, 0))
            ar = ar + xr * spr + xi * spi
            ai = ai + xi * spr - xr * spi
            return xr[0:1, :], xi[0:1, :], ar, ai

        z1 = jnp.zeros((1, ns), f32)
        z8 = jnp.zeros((SUBLANES, ns), f32)
        _, _, ar, ai = lax.fori_loop(0, nt, step, (z1, z1, z8, z8))
        da_ref[0:1, :] += jnp.sum(ar, axis=0, keepdims=True)
        da_ref[1:2, :] += jnp.sum(ai, axis=0, keepdims=True)

        for r in range(0, L, rc):
            lamb = lam_ref[r:r + rc, :].astype(bf16)
            dbs_ref[...] += lax.dot_general(u_ref[r:r + rc, :].astype(bf16), lamb, _DIMS["tn"], preferred_element_type=f32)
            du = (lax.dot_general(lamb, bs_ref[...], _DIMS["nt"], preferred_element_type=f32)
                  + d_ref[...] * dyp_ref[r:r + rc, :])
            du_ref[r:r + rc, :] = du.astype(du_ref.dtype)

    ucol = SEG_U * (D_MODEL // SLAB_CH)
    T = B * L
    return pl.pallas_call(
        body, name="s5_bwd", grid=(N_SLAB, B),
        in_specs=[pl.BlockSpec((L, SLAB_CH), lambda s, b: (b, ucol + s)),
                  pl.BlockSpec((L, SLAB_CH), lambda s, b: (b, s)),
                  ANY,
                  pl.BlockSpec((None, SLAB_CH, 2 * SLAB_NS), lambda s, b: (s, 0, 0)),
                  pl.BlockSpec((None, 2 * SLAB_NS, SLAB_CH), lambda s, b: (s, 0, 0)),
                  pl.BlockSpec((None, 2, 2 * SUBLANES, SLAB_NS), lambda s, b: (s, 0, 0, 0)),
                  pl.BlockSpec((1, SLAB_CH), lambda s, b: (0, s))],
        out_specs=[pl.BlockSpec((None, L, SLAB_CH), lambda s, b: (SEG_U, b, s)),
                   pl.BlockSpec((None, SLAB_CH, 2 * SLAB_NS), lambda s, b: (s, 0, 0)),
                   pl.BlockSpec((None, 2 * SLAB_NS, SLAB_CH), lambda s, b: (s, 0, 0)),
                   pl.BlockSpec((None, 2, SLAB_NS), lambda s, b: (s, 0, 0)),
                   pl.BlockSpec((1, SLAB_CH), lambda s, b: (0, s))],
        out_shape=[jax.ShapeDtypeStruct((N_SEG, T, D_MODEL), bf16),
                   jax.ShapeDtypeStruct((N_SLAB, SLAB_CH, 2 * SLAB_NS), f32),
                   jax.ShapeDtypeStruct((N_SLAB, 2 * SLAB_NS, SLAB_CH), f32),
                   jax.ShapeDtypeStruct((N_SLAB, 2, SLAB_NS), f32),
                   jax.ShapeDtypeStruct((1, D_MODEL), f32)],
        scratch_shapes=[pltpu.VMEM((L, 2 * SLAB_NS), f32), pltpu.VMEM((L, 2 * SLAB_NS), f32), pltpu.VMEM((L, SLAB_CH), f32)],
        input_output_aliases={2: 0},
        compiler_params=_params("parallel", "arbitrary"),
    )(p, dya0, dp, bs, cs, pw, d_skip)


def _dotb(a, b, dims="nn"):
    return lax.dot_general(a.astype(bf16), b.astype(bf16), _DIMS[dims], preferred_element_type=f32)


def _chunk_cumsum(x, pos):
    k = 1
    while k < CHUNK:
        x = x + jnp.where(pos >= k, pltpu.roll(x, k, 0), 0.0)
        k *= 2
    return x


def _chunk_rev_cumsum(x, pos):
    n = x.shape[0]
    k = 1
    while k < CHUNK:
        x = x + jnp.where(pos < CHUNK - k, pltpu.roll(x, n - k, 0), 0.0)
        k *= 2
    return x


def _hgrn_local(q, fl, lb, pos):
    sg = _sigmoid(fl)
    f = lb + (1.0 - lb) * sg
    g = jnp.log(f)
    cum = _chunk_cumsum(g, pos)
    rest = _chunk_rev_cumsum(g, pos) - g
    e = jnp.exp(cum)
    em = jnp.exp(-cum)
    eo = jnp.exp(rest)
    k = 1.0 - f
    return sg, f, e, em, eo, q * e, k * em, k * eo, jnp.exp(cum + rest)


def _hgrn_block_mask(n):
    r = lax.broadcasted_iota(jnp.int32, (n, n), 0)
    c = lax.broadcasted_iota(jnp.int32, (n, n), 1)
    return ((r & -CHUNK) == (c & -CHUNK)) & (c <= r)


def _chunk_pos(n):
    return lax.broadcasted_iota(jnp.int32, (n, HEAD_DIM), 0) & (CHUNK - 1)


def _hgrn_block_rows(L):
    return _tile(L, 688, CHUNK)


def _chunk_rows(c):
    return pl.ds(pl.multiple_of(c * CHUNK, CHUNK), CHUNK)


def _chunk_loop(nc, step):
    rep = max(u for u in range(1, 49) if nc % u == 0)

    def body(i, carry):
        for u in range(rep):
            step(i * rep + u)
        return carry

    lax.fori_loop(0, nc // rep, body, 0)


def _hgrn_specs(L, order):
    hb = D_MODEL // HEAD_DIM

    def spec(seg):
        if order == "bh":
            return pl.BlockSpec((L, HEAD_DIM), lambda b, h: (b, seg * hb + h))
        return pl.BlockSpec((L, HEAD_DIM), lambda h, b: (b, seg * hb + h))

    return [spec(SEG_Q), spec(SEG_F), spec(SEG_I), spec(SEG_OG)]


def _hgrn_fwd(p, lb, norm_g, B, L):
    nc = L // CHUNK

    rb = _hgrn_block_rows(L)

    def body(q_ref, f_ref, v_ref, og_ref, lb_ref, ng_ref, y_ref, qt_s, ko_s, vb_s, dec_s, o_s, u_s, sb_s):
        lbv = lb_ref[...]
        ngv = ng_ref[...]
        mask = _hgrn_block_mask(rb)
        pos = _chunk_pos(rb)

        for r in range(0, L, rb):
            rows = slice(r, r + rb)
            _, _, _, _, _, qt, kt, ko, dec = _hgrn_local(q_ref[rows, :], f_ref[rows, :], lbv, pos)
            vb = v_ref[rows, :].astype(bf16)
            qtb = qt.astype(bf16)
            pm = jnp.where(mask, _dotb(qtb, kt, "nt"), 0.0)
            o_s[rows, :] = _dotb(pm, vb)
            qt_s[rows, :] = qtb
            ko_s[rows, :] = ko.astype(bf16)
            vb_s[rows, :] = vb
            dec_s[rows, :] = dec

        def update(c):
            rows = _chunk_rows(c)
            u_s[c] = _dotb(vb_s[rows, :], ko_s[rows, :], "tn")

        def chain(c, st):
            sb_s[c] = st.astype(bf16)
            return st * dec_s[_chunk_rows(c), :][0:1, :] + u_s[c]

        def attend(c):
            rows = _chunk_rows(c)
            o_s[rows, :] += _dotb(qt_s[rows, :], sb_s[c], "nt")

        _chunk_loop(nc, update)
        lax.fori_loop(0, nc, chain, jnp.zeros((HEAD_DIM, HEAD_DIM), f32))
        _chunk_loop(nc, attend)

        for r in range(0, L, rb):
            rows = slice(r, r + rb)
            o = o_s[rows, :]
            og = og_ref[rows, :]
            on = o * lax.rsqrt(jnp.mean(o * o, axis=-1, keepdims=True) + EPS) * ngv
            y_ref[rows, :] = (on * og * _sigmoid(og)).astype(y_ref.dtype)

    return pl.pallas_call(
        body, name="hgrn_fwd", grid=(B, HEADS),
        in_specs=_hgrn_specs(L, "bh") + [pl.BlockSpec((1, HEAD_DIM), lambda b, h: (0, h)),
                                          pl.BlockSpec((1, HEAD_DIM), lambda b, h: (0, 0))],
        out_specs=pl.BlockSpec((L, HEAD_DIM), lambda b, h: (b, h)),
        out_shape=jax.ShapeDtypeStruct((B * L, D_MODEL), bf16),
        scratch_shapes=[pltpu.VMEM((L, HEAD_DIM), bf16), pltpu.VMEM((L, HEAD_DIM), bf16), pltpu.VMEM((L, HEAD_DIM), bf16),
                        pltpu.VMEM((L, HEAD_DIM), f32), pltpu.VMEM((L, HEAD_DIM), f32),
                        pltpu.VMEM((nc, HEAD_DIM, HEAD_DIM), f32), pltpu.VMEM((nc, HEAD_DIM, HEAD_DIM), bf16)],
        compiler_params=_params("parallel", "parallel"),
    )(p, p, p, p, lb, norm_g)


def _hgrn_bwd(p, dyb, dp, lb, norm_g, B, L):
    nc = L // CHUNK

    rb = _hgrn_block_rows(L)

    def body(q_ref, f_ref, v_ref, og_ref, dy_ref, dp_in, lb_ref, ng_ref, dseg_ref, dlb_ref, dng_ref,
             st_ref, u_s, dsb_s, qt_s, kt_s, ko_s, vb_s, do_s, dec_s, o_s, dqt_s, dkt_s, dko_s, dv_s, ddec_s):
        del dp_in
        lbv = lb_ref[...]
        ngv = ng_ref[...]
        mask = _hgrn_block_mask(rb)
        pos = _chunk_pos(rb)
        blocks = [slice(r, r + rb) for r in range(0, L, rb)]

        @pl.when(pl.program_id(1) == 0)
        def _():
            dlb_ref[...] = jnp.zeros_like(dlb_ref)

        @pl.when((pl.program_id(0) == 0) & (pl.program_id(1) == 0))
        def _():
            dng_ref[...] = jnp.zeros_like(dng_ref)

        def scores(rows):
            return jnp.where(mask, _dotb(qt_s[rows, :], kt_s[rows, :], "nt"), 0.0).astype(bf16)

        for rows in blocks:
            _, _, _, _, _, qt, kt, ko, dec = _hgrn_local(q_ref[rows, :], f_ref[rows, :], lbv, pos)
            qt_s[rows, :] = qt.astype(bf16)
            kt_s[rows, :] = kt.astype(bf16)
            ko_s[rows, :] = ko.astype(bf16)
            vb_s[rows, :] = v_ref[rows, :].astype(bf16)
            dec_s[rows, :] = dec
            o_s[rows, :] = _dotb(scores(rows), vb_s[rows, :])

        def update(c):
            rows = _chunk_rows(c)
            u_s[c] = _dotb(vb_s[rows, :], ko_s[rows, :], "tn")

        def chain(c, st):
            st_ref[c] = st
            return st * dec_s[_chunk_rows(c), :][0:1, :] + u_s[c]

        def attend(c):
            rows = _chunk_rows(c)
            o_s[rows, :] += _dotb(qt_s[rows, :], st_ref[c], "nt")

        _chunk_loop(nc, update)
        lax.fori_loop(0, nc, chain, jnp.zeros((HEAD_DIM, HEAD_DIM), f32))
        _chunk_loop(nc, attend)

        dng = jnp.zeros((1, HEAD_DIM), f32)
        for rows in blocks:
            o = o_s[rows, :]
            og = og_ref[rows, :]
            dy = dy_ref[rows, :]
            rs = lax.rsqrt(jnp.mean(o * o, axis=-1, keepdims=True) + EPS)
            xn = o * rs
            so = _sigmoid(og)
            dseg_ref[SEG_OG, rows, :] = (dy * xn * ngv * so * (1.0 + og * (1.0 - so))).astype(dseg_ref.dtype)
            don = dy * og * so
            dng = dng + jnp.sum(don * xn, axis=0, keepdims=True)
            dxo = don * ngv
            do = (rs * (dxo - xn * jnp.mean(dxo * xn, axis=-1, keepdims=True))).astype(bf16)
            do_s[rows, :] = do
            dpm = jnp.where(mask, _dotb(do, vb_s[rows, :], "nt"), 0.0).astype(bf16)
            dqt_s[rows, :] = _dotb(dpm, kt_s[rows, :])
            dkt_s[rows, :] = _dotb(dpm, qt_s[rows, :], "tn")
            dv_s[rows, :] = _dotb(scores(rows), do, "tn")
        dng_ref[...] += dng

        def rupdate(c):
            rows = _chunk_rows(c)
            u_s[c] = _dotb(do_s[rows, :], qt_s[rows, :], "tn")

        def rchain(j, dst):
            c = nc - 1 - j
            rows = _chunk_rows(c)
            dsb_s[c] = dst.astype(bf16)
            ddec_s[rows, :] = jnp.broadcast_to(jnp.sum(dst * st_ref[c], axis=0, keepdims=True), (CHUNK, HEAD_DIM))
            return dst * dec_s[rows, :][0:1, :] + u_s[c]

        def rattend(c):
            rows = _chunk_rows(c)
            dst = dsb_s[c]
            dqt_s[rows, :] += _dotb(do_s[rows, :], st_ref[c])
            dv_s[rows, :] += _dotb(ko_s[rows, :], dst, "nt")
            dko_s[rows, :] = _dotb(vb_s[rows, :], dst)

        _chunk_loop(nc, rupdate)
        lax.fori_loop(0, nc, rchain, jnp.zeros((HEAD_DIM, HEAD_DIM), f32))
        _chunk_loop(nc, rattend)

        dlb = jnp.zeros((1, HEAD_DIM), f32)
        for rows in blocks:
            sg, f, e, em, eo, qt, kt, ko, dec = _hgrn_local(q_ref[rows, :], f_ref[rows, :], lbv, pos)
            dqt = dqt_s[rows, :]
            dkt = dkt_s[rows, :]
            dko = dko_s[rows, :]
            dko_ko = dko * ko
            dcum = dqt * qt - dkt * kt - dko_ko
            chunk_tot = _chunk_cumsum(dko_ko, pos) + _chunk_rev_cumsum(dko_ko, pos) - dko_ko
            dcum = dcum + jnp.where(pos == CHUNK - 1, chunk_tot + ddec_s[rows, :] * dec, 0.0)
            df = _chunk_rev_cumsum(dcum, pos) / f - (dkt * em + dko * eo)
            dlb = dlb + jnp.sum(df * (1.0 - sg), axis=0, keepdims=True)
            dseg_ref[SEG_Q, rows, :] = (dqt * e).astype(dseg_ref.dtype)
            dseg_ref[SEG_F, rows, :] = (df * (1.0 - lbv) * sg * (1.0 - sg)).astype(dseg_ref.dtype)
            dseg_ref[SEG_I, rows, :] = dv_s[rows, :].astype(dseg_ref.dtype)
        dlb_ref[...] += dlb

    T = B * L
    sb = pltpu.VMEM((L, HEAD_DIM), bf16)
    sf = pltpu.VMEM((L, HEAD_DIM), f32)
    return pl.pallas_call(
        body, name="hgrn_bwd", grid=(HEADS, B),
        in_specs=_hgrn_specs(L, "hb") + [pl.BlockSpec((L, HEAD_DIM), lambda h, b: (b, h)), ANY,
                                          pl.BlockSpec((1, HEAD_DIM), lambda h, b: (0, h)),
                                          pl.BlockSpec((1, HEAD_DIM), lambda h, b: (0, 0))],
        out_specs=[pl.BlockSpec((4, L, HEAD_DIM), lambda h, b: (0, b, h)),
                   pl.BlockSpec((1, HEAD_DIM), lambda h, b: (0, h)),
                   pl.BlockSpec((1, HEAD_DIM), lambda h, b: (0, 0))],
        out_shape=[jax.ShapeDtypeStruct((N_SEG, T, D_MODEL), bf16), jax.ShapeDtypeStruct((1, D_MODEL), f32),
                   jax.ShapeDtypeStruct((1, HEAD_DIM), f32)],
        scratch_shapes=[pltpu.VMEM((nc, HEAD_DIM, HEAD_DIM), f32), pltpu.VMEM((nc, HEAD_DIM, HEAD_DIM), f32),
                        pltpu.VMEM((nc, HEAD_DIM, HEAD_DIM), bf16), sb, sb, sb, sb, sb, sf, sf, sf, sf, sf, sf, sf],
        input_output_aliases={5: 0},
        compiler_params=_params("arbitrary", "arbitrary"),
    )(p, p, p, p, dyb, dp, lb, norm_g)


def _dz1(dp, w_in_phys):
    _, T, Dm = dp.shape
    tm = _tile(T, 1032)
    return _mm("dz1", dp, w_in_phys, "nt", (T // tm, 1, N_SEG),
               pl.BlockSpec((None, tm, Dm), lambda i, j, k: (k, i, 0)),
               pl.BlockSpec((Dm, Dm), lambda i, j, k: (0, k)),
               jax.ShapeDtypeStruct((T, Dm), f32), pl.BlockSpec((tm, Dm), lambda i, j, k: (i, 0)), (tm, Dm))


def _dw_in(z1, dp):
    _, T, Dm = dp.shape
    tn = 256
    per_seg = Dm // tn
    per_chip = IN_COLS // N_CHIPS // tn
    tk = _tile(T, 1376)

    def out_idx(i, j, k):
        logical = ((j // per_seg + 1) % N_SEG) * per_seg + j % per_seg
        return (logical // per_chip, 0, logical % per_chip)

    return _mm("dw_in", z1, dp, "tn", (1, IN_COLS // tn, T // tk),
               pl.BlockSpec((tk, Dm), lambda i, j, k: (k, 0)),
               pl.BlockSpec((None, tk, tn), lambda i, j, k: (j // per_seg, k, j % per_seg)),
               jax.ShapeDtypeStruct((N_CHIPS, Dm, IN_COLS // N_CHIPS), f32),
               pl.BlockSpec((None, Dm, tn), out_idx), (Dm, tn))


def _dz2(dup, w_up):
    _, T, _ = dup.shape
    tm = _tile(T, 1032)
    tk = D_FF // 2
    return _mm("dz2", dup, w_up, "nt", (T // tm, 1, 4),
               pl.BlockSpec((None, tm, tk), lambda i, j, k: (k // 2, i, k % 2)),
               pl.BlockSpec((D_MODEL, tk), lambda i, j, k: (0, k)),
               jax.ShapeDtypeStruct((T, D_MODEL), f32), pl.BlockSpec((tm, D_MODEL), lambda i, j, k: (i, 0)), (tm, D_MODEL))


def _dw_up(z2, dup):
    _, T, _ = dup.shape
    tn = D_FF // 2
    tk = _tile(T, 688)
    return _mm("dw_up", z2, dup, "tn", (1, N_CHIPS, T // tk),
               pl.BlockSpec((tk, D_MODEL), lambda i, j, k: (k, 0)),
               pl.BlockSpec((None, tk, tn), lambda i, j, k: (j // 2, k, j % 2)),
               jax.ShapeDtypeStruct((N_CHIPS, D_MODEL, tn), f32),
               pl.BlockSpec((None, D_MODEL, tn), lambda i, j, k: (j, 0, 0)), (D_MODEL, tn))


def _place():
    x, y, c = lax.axis_index("x"), lax.axis_index("y"), lax.axis_index("c")
    chips = [(1 - x, y), (x, 1 - y), (1 - x, 1 - y)]
    return x, y, c, chips


def _allgather_chips(arrs):
    n = len(arrs)

    def body(*refs):
        ins, outs = refs[:n], refs[n:2 * n]
        send, recv, local = refs[2 * n:]
        x, y, c, chips = _place()
        me = 2 * x + y

        def copy(a, k, slot):
            px, py = chips[k]
            return pltpu.make_async_remote_copy(src_ref=ins[a], dst_ref=outs[a].at[slot], send_sem=send.at[3 * a + k],
                                                recv_sem=recv.at[3 * a + k], device_id=(px, py, c), device_id_type=MESH)

        for a in range(n):
            pltpu.make_async_copy(ins[a], outs[a].at[me], local.at[a]).start()
            for k in range(3):
                copy(a, k, me).start()
        for a in range(n):
            for k, (px, py) in enumerate(chips):
                copy(a, k, 2 * px + py).wait_recv()
        for a in range(n):
            pltpu.make_async_copy(ins[a], outs[a].at[me], local.at[a]).wait()
            for k in range(3):
                copy(a, k, me).wait_send()

    return pl.pallas_call(
        body, name="allgather_chips", in_specs=[ANY] * n, out_specs=[ANY] * n,
        out_shape=[jax.ShapeDtypeStruct((N_CHIPS,) + a.shape, a.dtype) for a in arrs],
        scratch_shapes=[pltpu.SemaphoreType.DMA((3 * n,)), pltpu.SemaphoreType.DMA((3 * n,)), pltpu.SemaphoreType.DMA((n,))],
    )(*arrs)


def _sibling_halves(parts):
    n = len(parts)

    def body(*refs):
        ins, outs = refs[:n], refs[n:2 * n]
        send, recv = refs[2 * n:]
        x, y, c, _ = _place()

        def copy(a):
            rh = ins[a].shape[1] // 2
            return pltpu.make_async_remote_copy(src_ref=ins[a].at[:, pl.ds((1 - c) * rh, rh), :], dst_ref=outs[a],
                                                send_sem=send.at[a], recv_sem=recv.at[a], device_id=(x, y, 1 - c),
                                                device_id_type=MESH)

        for a in range(n):
            copy(a).start()
        for a in range(n):
            copy(a).wait_recv()
        for a in range(n):
            copy(a).wait_send()

    return pl.pallas_call(
        body, name="sibling_halves", in_specs=[ANY] * n, out_specs=[ANY] * n,
        out_shape=[jax.ShapeDtypeStruct((a.shape[0], a.shape[1] // 2, a.shape[2]), a.dtype) for a in parts],
        scratch_shapes=[pltpu.SemaphoreType.DMA((n,)), pltpu.SemaphoreType.DMA((n,))],
    )(*parts)


def _add_own_half(name, part, got, core):
    nchip, R, C = part.shape
    rh = R // 2
    tr = _tile(rh, 256)
    nt = rh // tr

    def body(core_ref, a_ref, b_ref, o_ref):
        del core_ref
        o_ref[...] = a_ref[...] + b_ref[...]

    return pl.pallas_call(
        body, name=name,
        grid_spec=pltpu.PrefetchScalarGridSpec(
            num_scalar_prefetch=1, grid=(nchip, nt),
            in_specs=[pl.BlockSpec((None, tr, C), lambda j, i, core_ref: (j, core_ref[0] * nt + i, 0)),
                      pl.BlockSpec((None, tr, C), lambda j, i, core_ref: (j, i, 0))],
            out_specs=pl.BlockSpec((None, tr, C), lambda j, i, core_ref: (j, i, 0))),
        out_shape=jax.ShapeDtypeStruct((nchip, rh, C), f32), compiler_params=_params("parallel", "parallel"),
    )(core, part, got)


def _chip_exchange(sums):
    n = len(sums)

    def body(*refs):
        ins, outs = refs[:n], refs[n:2 * n]
        send, recv, local = refs[2 * n:]
        x, y, c, chips = _place()
        me = 2 * x + y

        def copy(a, k):
            px, py = chips[k]
            return pltpu.make_async_remote_copy(src_ref=ins[a].at[2 * px + py], dst_ref=outs[a].at[me], send_sem=send.at[3 * a + k],
                                                recv_sem=recv.at[3 * a + k], device_id=(px, py, c), device_id_type=MESH)

        def landed(a, k):
            px, py = chips[k]
            return pltpu.make_async_remote_copy(src_ref=ins[a].at[me], dst_ref=outs[a].at[2 * px + py], send_sem=send.at[3 * a + k],
                                                recv_sem=recv.at[3 * a + k], device_id=(px, py, c), device_id_type=MESH)

        for a in range(n):
            pltpu.make_async_copy(ins[a].at[me], outs[a].at[me], local.at[a]).start()
            for k in range(3):
                copy(a, k).start()
        for a in range(n):
            for k in range(3):
                landed(a, k).wait_recv()
        for a in range(n):
            pltpu.make_async_copy(ins[a].at[me], outs[a].at[me], local.at[a]).wait()
            for k in range(3):
                copy(a, k).wait_send()

    return pl.pallas_call(
        body, name="chip_exchange", in_specs=[ANY] * n, out_specs=[ANY] * n,
        out_shape=[jax.ShapeDtypeStruct(a.shape, a.dtype) for a in sums],
        scratch_shapes=[pltpu.SemaphoreType.DMA((3 * n,)), pltpu.SemaphoreType.DMA((3 * n,)), pltpu.SemaphoreType.DMA((n,))],
    )(*sums)


def _sum_slots(name, slots):
    ns, R, C = slots.shape
    tr = _tile(R, 256)

    def body(s_ref, o_ref):
        acc = s_ref[0]
        for j in range(1, ns):
            acc = acc + s_ref[j]
        o_ref[...] = acc

    return pl.pallas_call(
        body, name=name, grid=(R // tr,), in_specs=[pl.BlockSpec((ns, tr, C), lambda i: (0, i, 0))],
        out_specs=pl.BlockSpec((tr, C), lambda i: (i, 0)), out_shape=jax.ShapeDtypeStruct((R, C), f32),
        compiler_params=_params("parallel"),
    )(slots)


def _sibling_join(halves):
    n = len(halves)

    def body(*refs):
        ins, outs = refs[:n], refs[n:2 * n]
        send, recv, local = refs[2 * n:]
        x, y, c, _ = _place()

        def rows(a, core):
            rh = ins[a].shape[0]
            return outs[a].at[pl.ds(core * rh, rh), :]

        def copy(a, core):
            return pltpu.make_async_remote_copy(src_ref=ins[a], dst_ref=rows(a, core), send_sem=send.at[a], recv_sem=recv.at[a],
                                                device_id=(x, y, 1 - c), device_id_type=MESH)

        for a in range(n):
            pltpu.make_async_copy(ins[a], rows(a, c), local.at[a]).start()
            copy(a, c).start()
        for a in range(n):
            copy(a, 1 - c).wait_recv()
        for a in range(n):
            pltpu.make_async_copy(ins[a], rows(a, c), local.at[a]).wait()
            copy(a, c).wait_send()

    return pl.pallas_call(
        body, name="sibling_join", in_specs=[ANY] * n, out_specs=[ANY] * n,
        out_shape=[jax.ShapeDtypeStruct((2 * a.shape[0], a.shape[1]), a.dtype) for a in halves],
        scratch_shapes=[pltpu.SemaphoreType.DMA((n,)), pltpu.SemaphoreType.DMA((n,)), pltpu.SemaphoreType.DMA((n,))],
    )(*halves)


def _allgather_devices(v):
    def body(v_ref, out_ref, send, recv, local):
        x, y, c, _ = _place()
        me = 4 * x + 2 * y + c

        def peer(k):
            return (1 - x if k & 4 else x, 1 - y if k & 2 else y, 1 - c if k & 1 else c)

        def copy(k, slot):
            return pltpu.make_async_remote_copy(src_ref=v_ref, dst_ref=out_ref.at[slot], send_sem=send.at[k - 1],
                                                recv_sem=recv.at[k - 1], device_id=peer(k), device_id_type=MESH)

        own = pltpu.make_async_copy(v_ref, out_ref.at[me], local)
        own.start()
        for k in range(1, N_DEV):
            copy(k, me).start()
        for k in range(1, N_DEV):
            px, py, pc = peer(k)
            copy(k, 4 * px + 2 * py + pc).wait_recv()
        own.wait()
        for k in range(1, N_DEV):
            copy(k, me).wait_send()

    return pl.pallas_call(
        body, name="allgather_devices", in_specs=[ANY], out_specs=ANY,
        out_shape=jax.ShapeDtypeStruct((N_DEV,) + v.shape, v.dtype),
        scratch_shapes=[pltpu.SemaphoreType.DMA((N_DEV - 1,)), pltpu.SemaphoreType.DMA((N_DEV - 1,)), pltpu.SemaphoreType.DMA],
    )(v)


def _adamw(name, w, g, m, v):
    R, C = w.shape
    tr = _tile(R, 256)
    c1 = 1.0 / (1.0 - ADAM_B1 ** ADAM_STEP)
    c2 = 1.0 / (1.0 - ADAM_B2 ** ADAM_STEP)

    def body(w_ref, g_ref, m_ref, v_ref, d_ref, nm_ref, nv_ref):
        gv = g_ref[...]
        nm = ADAM_B1 * m_ref[...] + (1.0 - ADAM_B1) * gv
        nv = ADAM_B2 * v_ref[...] + (1.0 - ADAM_B2) * gv * gv
        d_ref[...] = -ADAM_LR * ((nm * c1) / (jnp.sqrt(nv * c2) + ADAM_EPS) + ADAM_WD * w_ref[...])
        nm_ref[...] = nm
        nv_ref[...] = nv

    row = pl.BlockSpec((tr, C), lambda i: (i, 0))
    sh = jax.ShapeDtypeStruct((R, C), f32)
    return pl.pallas_call(body, name=name, grid=(R // tr,), in_specs=[row] * 4, out_specs=[row] * 3,
                          out_shape=[sh, sh, sh], compiler_params=_params("parallel"))(w, g, m, v)


def _zoh(lr, li, log_dt, b_re, b_im):
    dt = jnp.exp(log_dt)[:, None]
    mag = jnp.exp(lr * dt)
    ab_re = mag * jnp.cos(li * dt)
    ab_im = mag * jnp.sin(li * dt)
    den = lr * lr + li * li
    nr = ab_re - 1.0
    coef_re = (nr * lr + ab_im * li) / den
    coef_im = (ab_im * lr - nr * li) / den
    bb_re = coef_re[..., None] * b_re - coef_im[..., None] * b_im
    bb_im = coef_re[..., None] * b_im + coef_im[..., None] * b_re
    return ab_re, ab_im, bb_re, bb_im


def _s5_tables(ab_re, ab_im, bb_re, bb_im, c_re, c_im):
    eye = jnp.eye(SLAB_GROUPS, dtype=f32)

    def blk_in(bb):
        return jnp.einsum("sgph,gk->sghkp", bb.reshape(N_SLAB, SLAB_GROUPS, SSM_STATE, SSM_GROUP), eye).reshape(
            N_SLAB, SLAB_CH, SLAB_NS)

    def blk_out(cc):
        return jnp.einsum("sghp,gk->skpgh", cc.reshape(N_SLAB, SLAB_GROUPS, SSM_GROUP, SSM_STATE), eye).reshape(
            N_SLAB, SLAB_NS, SLAB_CH)

    bs = jnp.concatenate([blk_in(bb_re), blk_in(bb_im)], axis=2).astype(bf16)
    cs = jnp.concatenate([blk_out(c_re), blk_out(-c_im)], axis=1).astype(bf16)
    pr, pi = [ab_re], [ab_im]
    for _ in range(SUBLANES - 1):
        pr, pi = pr + [pr[-1] * ab_re - pi[-1] * ab_im], pi + [pr[-1] * ab_im + pi[-1] * ab_re]
    pw = jnp.stack([jnp.stack(pr + pr[::-1]), jnp.stack(pi + pi[::-1])])
    pw = pw.reshape(2, 2 * SUBLANES, N_SLAB, SLAB_NS).transpose(2, 0, 1, 3)
    return bs, cs, pw


def _s5_table_grads(dbs, dcs, da):
    eye = jnp.eye(SLAB_GROUPS, dtype=f32)
    d6 = dbs.reshape(N_SLAB, SLAB_GROUPS, SSM_GROUP, 2, SLAB_GROUPS, SSM_STATE)
    dbb = jnp.einsum("sghrkp,gk->rsgph", d6, eye).reshape(2, SSM_GROUPS, SSM_STATE, SSM_GROUP)
    c6 = dcs.reshape(N_SLAB, 2, SLAB_GROUPS, SSM_STATE, SLAB_GROUPS, SSM_GROUP)
    dcc = jnp.einsum("srkpgh,gk->rsghp", c6, eye).reshape(2, SSM_GROUPS, SSM_GROUP, SSM_STATE)
    dab = da.transpose(1, 0, 2).reshape(2, SSM_GROUPS, SSM_STATE)
    return dab[0], dab[1], dbb[0], dbb[1], dcc[0], -dcc[1]


SMALL = ["mix_norm_g", "ssm_lambda_re", "ssm_lambda_im", "ssm_log_dt", "ssm_b_re", "ssm_b_im", "ssm_c_re", "ssm_c_im",
         "ssm_d", "hgrn_lb_logits", "hgrn_norm_g", "ffn_norm_g", "conv_b", "final_norm_g"]
SHARDED_SMALL = ["meta_tokens", "conv_w"]
BIG = ["w_in", "ssm_w_glu", "w_ssm_proj", "w_hgrn_proj", "w_out", "w_up", "w_down"]
WEIGHTS = ['meta_tokens', 'mix_norm_g', 'w_in', 'ssm_lambda_re', 'ssm_lambda_im', 'ssm_log_dt', 'ssm_b_re', 'ssm_b_im',
           'ssm_c_re', 'ssm_c_im', 'ssm_d', 'ssm_w_glu', 'w_ssm_proj', 'hgrn_lb_logits', 'hgrn_norm_g', 'w_hgrn_proj',
           'w_out', 'ffn_norm_g', 'w_up', 'conv_w', 'conv_b', 'w_down', 'final_norm_g']


def _local_grads(x, tgt, meta, w, full):
    B, S, Dm = x.shape
    L = S + N_META
    T = B * L
    h0 = jnp.concatenate([jnp.broadcast_to(meta[None], (B, N_META, Dm)), x], axis=1).reshape(T, Dm)

    lb_all = jax.nn.softmax(w["hgrn_lb_logits"], axis=0)
    lb = lb_all[0:1]
    zoh_out, zoh_vjp = jax.vjp(_zoh, w["ssm_lambda_re"][0], w["ssm_lambda_im"][0], w["ssm_log_dt"][0],
                               w["ssm_b_re"][0], w["ssm_b_im"][0])
    bs, cs, pw = _s5_tables(*zoh_out, w["ssm_c_re"][0], w["ssm_c_im"][0])

    z1 = _rmsnorm_fwd("mix_norm", h0, w["mix_norm_g"])
    p = _mm_rows("in_proj", z1, full["w_in"], "nn", f32, 1024)
    ya0 = _s5_fwd(p, bs, cs, pw, w["ssm_d"], B, L)
    gl = _mm_rows("glu_proj", ya0, full["ssm_w_glu"], "nn", f32, 1024)
    ya = _glu_fwd(ya0, gl)
    yb = _hgrn_fwd(p, lb, w["hgrn_norm_g"], B, L)
    pa = _mm_rows("ssm_proj", ya, full["w_ssm_proj"], "nn", f32, 1024)
    pb = _mm_rows("hgrn_proj", yb, full["w_hgrn_proj"], "nn", f32, 1024)
    merged = _merge_fwd(p, pa, pb)
    h1 = _mm_rows("out_proj", merged, full["w_out"], "nn", f32, 1024, res=h0)
    z2 = _rmsnorm_fwd("ffn_norm", h1, w["ffn_norm_g"])
    up = _mm_rows("up_proj", z2, full["w_up"], "nn", f32, D_FF // 2)
    ff = _conv_fwd(up, full["conv_w"], w["conv_b"], B, L)
    h2 = _mm_rows("down_proj", ff, full["w_down"], "nn", f32, 1024, res=h1, tk=D_FF // 2)

    h2x = h2.reshape(B, L, Dm)[:, N_META:].reshape(B * S, Dm)
    dh2x, loss, d_final_g = _final_loss(h2x, tgt.reshape(B * S, Dm), w["final_norm_g"].reshape(1, Dm))
    dh2 = jnp.pad(dh2x.reshape(B, S, Dm), ((0, 0), (N_META, 0), (0, 0))).reshape(T, Dm)

    dff = _mm_rows("d_ff", dh2, full["w_down"], "nt", f32, D_FF // 2)
    g_w_down = _mm_wgrad("dw_down", ff, dh2, tn=512)
    dup, dconv = _conv_bwd(up, dff, full["conv_w"], w["conv_b"], B, L)
    dz2 = _dz2(dup, full["w_up"])
    g_w_up = _dw_up(z2, dup)
    dh1, d_ffn_g = _rmsnorm_bwd("ffn_norm_bwd", h1, w["ffn_norm_g"], dz2, dh2)

    dmerged = _mm_rows("d_merged", dh1, full["w_out"], "nt", f32, 1024)
    g_w_out = _mm_wgrad("dw_out", merged, dh1)
    dpa, dpb, dp = _merge_bwd(dmerged, p, pa, pb)
    dya = _mm_rows("d_ya", dpa, full["w_ssm_proj"], "nt", f32, 1024)
    g_w_ssm_proj = _mm_wgrad("dw_ssm_proj", ya, dpa)
    dyb = _mm_rows("d_yb", dpb, full["w_hgrn_proj"], "nt", f32, 1024)
    g_w_hgrn_proj = _mm_wgrad("dw_hgrn_proj", yb, dpb)
    dp, d_lb, d_hgrn_g = _hgrn_bwd(p, dyb, dp, lb, w["hgrn_norm_g"], B, L)
    dgl, dya0_direct = _glu_bwd(dya, ya0, gl)
    dya0 = _mm_rows("d_ya0", dgl, full["ssm_w_glu"], "nt", f32, 1024, res=dya0_direct)
    g_w_glu = _mm_wgrad("dw_glu", ya0, dgl)
    dp, dbs, dcs, da, d_skip = _s5_bwd(p, dya0, dp, bs, cs, pw, w["ssm_d"], B, L)
    dz1 = _dz1(dp, full["w_in"])
    g_w_in = _dw_in(z1, dp)
    dh0, d_mix_g = _rmsnorm_bwd("mix_norm_bwd", h0, w["mix_norm_g"], dz1, dh1)

    dh0 = dh0.reshape(B, L, Dm)
    grad_x = dh0[:, N_META:]
    d_meta = _meta_grad(dh0[:, :N_META])

    d_ab_re, d_ab_im, d_bb_re, d_bb_im, d_c_re, d_c_im = _s5_table_grads(dbs, dcs, da)
    d_lr, d_li, d_log_dt, d_b_re, d_b_im = zoh_vjp((d_ab_re, d_ab_im, d_bb_re, d_bb_im))
    sm0, sm1 = lb_all[0:1], lb_all[1:2]
    d_logits = jnp.concatenate([sm0 * (1.0 - sm0) * d_lb, -sm0 * sm1 * d_lb], axis=0)
    small = {
        "meta_tokens": d_meta, "mix_norm_g": d_mix_g, "ssm_lambda_re": d_lr[None], "ssm_lambda_im": d_li[None],
        "ssm_log_dt": d_log_dt[None], "ssm_b_re": d_b_re[None], "ssm_b_im": d_b_im[None], "ssm_c_re": d_c_re[None],
        "ssm_c_im": d_c_im[None], "ssm_d": d_skip, "hgrn_lb_logits": d_logits, "hgrn_norm_g": d_hgrn_g,
        "ffn_norm_g": d_ffn_g, "conv_w": dconv[:, 0:3, :].transpose(1, 0, 2).reshape(3, 2 * D_FF),
        "conv_b": dconv[:, 3, :].reshape(1, 2 * D_FF), "final_norm_g": d_final_g.reshape(Dm),
    }
    big = {
        "w_in": g_w_in, "ssm_w_glu": g_w_glu.reshape(N_CHIPS, Dm // N_CHIPS, Dm),
        "w_ssm_proj": g_w_ssm_proj.reshape(N_CHIPS, Dm // N_CHIPS, Dm),
        "w_hgrn_proj": g_w_hgrn_proj.reshape(N_CHIPS, Dm // N_CHIPS, Dm), "w_out": g_w_out.reshape(N_CHIPS, Dm // N_CHIPS, Dm),
        "w_up": g_w_up, "w_down": g_w_down.reshape(N_CHIPS, D_FF // N_CHIPS, Dm),
    }
    return loss, grad_x, big, small


def _pack(parts):
    flat = jnp.concatenate([parts[k].reshape(-1) for k in parts])
    n = flat.shape[0]
    rows = -(-n // (SUBLANES * LANES)) * SUBLANES
    flat = jnp.pad(flat, (0, rows * LANES - n))
    return flat.reshape(rows, LANES)


def _unpack(packed, like):
    flat = packed.reshape(-1)
    out, o = {}, 0
    for k, ref in like.items():
        n = math.prod(ref.shape)
        out[k] = flat[o:o + n].reshape(ref.shape)
        o += n
    return out


def kernel(x, meta_tokens, mix_norm_g, w_in, ssm_lambda_re, ssm_lambda_im, ssm_log_dt, ssm_b_re, ssm_b_im, ssm_c_re, ssm_c_im, ssm_d, ssm_w_glu, w_ssm_proj, hgrn_lb_logits, hgrn_norm_g, w_hgrn_proj, w_out, ffn_norm_g, w_up, conv_w, conv_b, w_down, final_norm_g, loss_target, m_meta_tokens, m_mix_norm_g, m_w_in, m_ssm_lambda_re, m_ssm_lambda_im, m_ssm_log_dt, m_ssm_b_re, m_ssm_b_im, m_ssm_c_re, m_ssm_c_im, m_ssm_d, m_ssm_w_glu, m_w_ssm_proj, m_hgrn_lb_logits, m_hgrn_norm_g, m_w_hgrn_proj, m_w_out, m_ffn_norm_g, m_w_up, m_conv_w, m_conv_b, m_w_down, m_final_norm_g, v_meta_tokens, v_mix_norm_g, v_w_in, v_ssm_lambda_re, v_ssm_lambda_im, v_ssm_log_dt, v_ssm_b_re, v_ssm_b_im, v_ssm_c_re, v_ssm_c_im, v_ssm_d, v_ssm_w_glu, v_w_ssm_proj, v_hgrn_lb_logits, v_hgrn_norm_g, v_w_hgrn_proj, v_w_out, v_ffn_norm_g, v_w_up, v_conv_w, v_conv_b, v_w_down, v_final_norm_g):
    args = dict(locals())
    w = {k: args[k] for k in WEIGHTS}
    mom = {k: args["m_" + k] for k in WEIGHTS}
    var = {k: args["v_" + k] for k in WEIGHTS}
    Dm = D_MODEL
    cx, cy, cc = lax.axis_index("x"), lax.axis_index("y"), lax.axis_index("c")
    chip = 2 * cx + cy

    shards = [w["w_in"][0].astype(bf16), w["ssm_w_glu"][0].astype(bf16), w["w_ssm_proj"][0].astype(bf16),
              w["w_hgrn_proj"][0].astype(bf16), w["w_out"][0].astype(bf16), w["w_up"][0].astype(bf16),
              w["w_down"][0].astype(bf16), w["meta_tokens"], w["conv_w"][0]]
    g_in, g_glu, g_sp, g_hp, g_out, g_up, g_down, g_meta, g_cw = _allgather_chips(shards)
    w_in_full = jnp.roll(g_in.transpose(1, 0, 2).reshape(Dm, IN_COLS), -Dm, axis=1)
    full = {
        "w_in": w_in_full, "ssm_w_glu": g_glu.reshape(Dm, Dm), "w_ssm_proj": g_sp.reshape(Dm, Dm),
        "w_hgrn_proj": g_hp.reshape(Dm, Dm), "w_out": g_out.reshape(Dm, Dm),
        "w_up": g_up.transpose(1, 0, 2).reshape(Dm, 2 * D_FF), "w_down": g_down.reshape(D_FF, Dm),
        "conv_w": g_cw.transpose(1, 0, 2).reshape(3, 2 * D_FF),
    }
    meta_full = g_meta.transpose(1, 0, 2).reshape(N_META, Dm)

    loss_part, grad_x, big, small = _local_grads(x, loss_target, meta_full, w, full)

    core = cc.reshape(1).astype(jnp.int32)
    parts = [big[k] for k in BIG]
    got = _sibling_halves(parts)
    sums = [_add_own_half("add_half_" + k, pt, gt, core) for k, pt, gt in zip(BIG, parts, got)]
    slots = _chip_exchange(sums)
    halves = [_sum_slots("sum_chips_" + k, s) for k, s in zip(BIG, slots)]
    g_big = dict(zip(BIG, _sibling_join(halves)))

    small_all = dict(small)
    small_all["loss"] = loss_part[0, 0:1]
    packed = _pack(small_all)
    reduced = _unpack(_sum_slots("sum_devices", _allgather_devices(packed)), small_all)
    loss = reduced.pop("loss")[0]
    mcols = Dm // N_CHIPS
    ccols = 2 * D_FF // N_CHIPS
    grads = {k: reduced[k] for k in SMALL}
    grads["meta_tokens"] = lax.dynamic_slice(reduced["meta_tokens"], (0, chip * mcols), (N_META, mcols))
    grads["conv_w"] = lax.dynamic_slice(reduced["conv_w"], (0, chip * ccols), (3, ccols))[None]
    for k in BIG:
        grads[k] = g_big[k][None]

    delta, new_m, new_v = {}, {}, {}
    for k in BIG:
        shp = w[k].shape
        d, nm, nv = _adamw("adamw_" + k, w[k][0], grads[k][0], mom[k][0], var[k][0])
        delta[k], new_m[k], new_v[k] = d.reshape(shp), nm.reshape(shp), nv.reshape(shp)
    rest = SMALL + SHARDED_SMALL
    pk = [_pack({k: t[k] for k in rest}) for t in (w, grads, mom, var)]
    outs = _adamw("adamw_small", *pk)
    like = {k: w[k] for k in rest}
    for dst, o in zip((delta, new_m, new_v), outs):
        dst.update(_unpack(o, like))

    return (loss, grad_x, *[grads[k].reshape(w[k].shape) for k in WEIGHTS], *[delta[k] for k in WEIGHTS],
            *[new_m[k] for k in WEIGHTS], *[new_v[k] for k in WEIGHTS])
```

```python
import functools
import math

import jax
import jax.numpy as jnp
from jax import lax
from jax.experimental import pallas as pl
from jax.experimental.pallas import tpu as pltpu

f32 = jnp.float32
bf16 = jnp.bfloat16

D_MODEL = 1024
N_META = 16
SSM_GROUP = 16
SSM_GROUPS = 64
SSM_STATE = 64
SLAB_GROUPS = 8
N_SLAB = SSM_GROUPS // SLAB_GROUPS
SLAB_CH = SLAB_GROUPS * SSM_GROUP
SLAB_NS = SLAB_GROUPS * SSM_STATE
HEADS = 8
HEAD_DIM = 128
CHUNK = 16
D_FF = 2816
IN_COLS = 7168
EPS = 1e-6
SUBLANES = 8
LANES = 128
N_CHIPS = 4
N_DEV = 8
ADAM_LR, ADAM_B1, ADAM_B2, ADAM_EPS, ADAM_WD, ADAM_STEP = 0.001, 0.9, 0.999, 1e-08, 0.01, 10
MESH = pl.DeviceIdType.MESH
ANY = pl.BlockSpec(memory_space=pl.ANY)

SEG_Q, SEG_F, SEG_I, SEG_OG, SEG_GA, SEG_GB, SEG_U = range(7)
N_SEG = 7


def _tile(n, target, mult=SUBLANES):
    best = None
    for d in range(mult, min(n, target) + 1, mult):
        if n % d == 0:
            best = d
    return n if best is None else best


def _params(*sem):
    return pltpu.CompilerParams(dimension_semantics=sem)


def _sigmoid(x):
    return 1.0 / (1.0 + jnp.exp(-x))


_DIMS = {"nn": (((1,), (0,)), ((), ())), "nt": (((1,), (1,)), ((), ())), "tn": (((0,), (0,)), ((), ()))}


def _mm(name, a, b, dims, grid, a_spec, b_spec, out_shape, out_spec, acc_shape, res=None, res_spec=None):
    nk = grid[2]
    dn = _DIMS[dims]

    def body(*refs):
        if res is None:
            a_ref, b_ref, o_ref, acc = refs
        else:
            a_ref, b_ref, r_ref, o_ref, acc = refs
        k = pl.program_id(2)

        @pl.when(k == 0)
        def _():
            acc[...] = jnp.zeros_like(acc)

        acc[...] += lax.dot_general(a_ref[...].astype(bf16), b_ref[...].astype(bf16), dn, preferred_element_type=f32)

        @pl.when(k == nk - 1)
        def _():
            r = acc[...]
            if res is not None:
                r = r + r_ref[...]
            o_ref[...] = r.astype(o_ref.dtype)

    ins = [a, b] + ([] if res is None else [res])
    specs = [a_spec, b_spec] + ([] if res is None else [res_spec])
    return pl.pallas_call(
        body, name=name, grid=grid, in_specs=specs, out_specs=out_spec, out_shape=out_shape,
        scratch_shapes=[pltpu.VMEM(acc_shape, f32)],
        compiler_params=_params("parallel", "parallel", "arbitrary"),
    )(*ins)


def _mm_rows(name, a, w, dims, out_dtype, tn, res=None, tk=None):
    T, K = a.shape
    N = w.shape[1] if dims == "nn" else w.shape[0]
    tm = _tile(T, 1032)
    tk = K if tk is None else tk
    grid = (T // tm, N // tn, K // tk)
    a_spec = pl.BlockSpec((tm, tk), lambda i, j, k: (i, k))
    if dims == "nn":
        b_spec = pl.BlockSpec((tk, tn), lambda i, j, k: (k, j))
    else:
        b_spec = pl.BlockSpec((tn, tk), lambda i, j, k: (j, k))
    o_spec = pl.BlockSpec((tm, tn), lambda i, j, k: (i, j))
    return _mm(name, a, w, dims, grid, a_spec, b_spec, jax.ShapeDtypeStruct((T, N), out_dtype), o_spec, (tm, tn),
               res=res, res_spec=None if res is None else o_spec)


def _mm_wgrad(name, a, g, tn=None):
    T, K = a.shape
    N = g.shape[1]
    tk = _tile(T, 688)
    tn = N if tn is None else tn
    grid = (1, N // tn, T // tk)
    a_spec = pl.BlockSpec((tk, K), lambda i, j, k: (k, 0))
    g_spec = pl.BlockSpec((tk, tn), lambda i, j, k: (k, j))
    o_spec = pl.BlockSpec((K, tn), lambda i, j, k: (0, j))
    return _mm(name, a, g, "tn", grid, a_spec, g_spec, jax.ShapeDtypeStruct((K, N), f32), o_spec, (K, tn))


def _rmsnorm_fwd(name, x, g):
    T, Dm = x.shape
    tr = _tile(T, 688)

    def body(x_ref, g_ref, z_ref):
        xv = x_ref[...]
        r = lax.rsqrt(jnp.mean(xv * xv, axis=-1, keepdims=True) + EPS)
        z_ref[...] = (xv * r * g_ref[...]).astype(z_ref.dtype)

    return pl.pallas_call(
        body, name=name, grid=(T // tr,),
        in_specs=[pl.BlockSpec((tr, Dm), lambda i: (i, 0)), pl.BlockSpec((1, Dm), lambda i: (0, 0))],
        out_specs=pl.BlockSpec((tr, Dm), lambda i: (i, 0)),
        out_shape=jax.ShapeDtypeStruct((T, Dm), bf16), compiler_params=_params("parallel"),
    )(x, g)


def _rmsnorm_bwd(name, x, g, dz, dres):
    T, Dm = x.shape
    tr = _tile(T, 688)

    def body(x_ref, g_ref, dz_ref, dres_ref, dx_ref, dg_ref):
        xv = x_ref[...]
        r = lax.rsqrt(jnp.mean(xv * xv, axis=-1, keepdims=True) + EPS)
        xn = xv * r
        dzv = dz_ref[...]
        dzg = dzv * g_ref[...]
        dx_ref[...] = dres_ref[...] + r * (dzg - xn * jnp.mean(dzg * xn, axis=-1, keepdims=True))

        @pl.when(pl.program_id(0) == 0)
        def _():
            dg_ref[...] = jnp.zeros_like(dg_ref)

        dg_ref[...] += jnp.sum(dzv * xn, axis=0, keepdims=True)

    row = pl.BlockSpec((tr, Dm), lambda i: (i, 0))
    par = pl.BlockSpec((1, Dm), lambda i: (0, 0))
    return pl.pallas_call(
        body, name=name, grid=(T // tr,), in_specs=[row, par, row, row], out_specs=[row, par],
        out_shape=[jax.ShapeDtypeStruct((T, Dm), f32), jax.ShapeDtypeStruct((1, Dm), f32)],
        compiler_params=_params("arbitrary"),
    )(x, g, dz, dres)


def _glu_fwd(ya0, gl):
    T, Dm = ya0.shape
    tr = _tile(T, 688)

    def body(y_ref, g_ref, o_ref):
        o_ref[...] = (y_ref[...] * _sigmoid(g_ref[...])).astype(o_ref.dtype)

    row = pl.BlockSpec((tr, Dm), lambda i: (i, 0))
    return pl.pallas_call(body, name="glu_fwd", grid=(T // tr,), in_specs=[row, row], out_specs=row,
                          out_shape=jax.ShapeDtypeStruct((T, Dm), bf16), compiler_params=_params("parallel"))(ya0, gl)


def _glu_bwd(dya, ya0, gl):
    T, Dm = ya0.shape
    tr = _tile(T, 688)

    def body(d_ref, y_ref, g_ref, dg_ref, dy_ref):
        s = _sigmoid(g_ref[...])
        d = d_ref[...]
        dg_ref[...] = (d * y_ref[...] * s * (1.0 - s)).astype(dg_ref.dtype)
        dy_ref[...] = d * s

    row = pl.BlockSpec((tr, Dm), lambda i: (i, 0))
    return pl.pallas_call(body, name="glu_bwd", grid=(T // tr,), in_specs=[row, row, row], out_specs=[row, row],
                          out_shape=[jax.ShapeDtypeStruct((T, Dm), bf16), jax.ShapeDtypeStruct((T, Dm), f32)],
                          compiler_params=_params("parallel"))(dya, ya0, gl)


def _merge_fwd(p, pa, pb):
    T, Dm = pa.shape
    tr = _tile(T, 688)

    def body(ga_ref, gb_ref, pa_ref, pb_ref, o_ref):
        o_ref[...] = (_sigmoid(ga_ref[...]) * pa_ref[...] + _sigmoid(gb_ref[...]) * pb_ref[...]).astype(o_ref.dtype)

    row = pl.BlockSpec((tr, Dm), lambda i: (i, 0))
    return pl.pallas_call(
        body, name="merge_fwd", grid=(T // tr,),
        in_specs=[pl.BlockSpec((tr, Dm), lambda i: (i, SEG_GA)), pl.BlockSpec((tr, Dm), lambda i: (i, SEG_GB)), row, row],
        out_specs=row, out_shape=jax.ShapeDtypeStruct((T, Dm), bf16), compiler_params=_params("parallel"),
    )(p, p, pa, pb)


def _merge_bwd(dm, p, pa, pb):
    T, Dm = pa.shape
    tr = _tile(T, 688)

    def body(dm_ref, ga_ref, gb_ref, pa_ref, pb_ref, dpa_ref, dpb_ref, dp_ref):
        d = dm_ref[...]
        sa = _sigmoid(ga_ref[...])
        sb = _sigmoid(gb_ref[...])
        dpa_ref[...] = (d * sa).astype(dpa_ref.dtype)
        dpb_ref[...] = (d * sb).astype(dpb_ref.dtype)
        dp_ref[0] = (d * pa_ref[...] * sa * (1.0 - sa)).astype(dp_ref.dtype)
        dp_ref[1] = (d * pb_ref[...] * sb * (1.0 - sb)).astype(dp_ref.dtype)

    row = pl.BlockSpec((tr, Dm), lambda i: (i, 0))
    return pl.pallas_call(
        body, name="merge_bwd", grid=(T // tr,),
        in_specs=[row, pl.BlockSpec((tr, Dm), lambda i: (i, SEG_GA)), pl.BlockSpec((tr, Dm), lambda i: (i, SEG_GB)), row, row],
        out_specs=[row, row, pl.BlockSpec((2, tr, Dm), lambda i: (SEG_GA // 2, i, 0))],
        out_shape=[jax.ShapeDtypeStruct((T, Dm), bf16), jax.ShapeDtypeStruct((T, Dm), bf16),
                   jax.ShapeDtypeStruct((N_SEG, T, Dm), bf16)],
        compiler_params=_params("parallel"),
    )(dm, p, p, pa, pb)


def _final_loss(h2x, tgt, g):
    T, Dm = h2x.shape
    tr = _tile(T, 512)

    def body(h_ref, t_ref, g_ref, dh_ref, loss_ref, dg_ref):
        hv = h_ref[...]
        r = lax.rsqrt(jnp.mean(hv * hv, axis=-1, keepdims=True) + EPS)
        xn = hv * r
        gv = g_ref[...]
        err = xn * gv - t_ref[...]
        dy = err * (1.0 / Dm)
        dyg = dy * gv
        dh_ref[...] = r * (dyg - xn * jnp.mean(dyg * xn, axis=-1, keepdims=True))

        @pl.when(pl.program_id(0) == 0)
        def _():
            dg_ref[...] = jnp.zeros_like(dg_ref)
            loss_ref[...] = jnp.zeros_like(loss_ref)

        dg_ref[...] += jnp.sum(dy * xn, axis=0, keepdims=True)
        loss_ref[...] += jnp.sum(err * err) * (0.5 / Dm)

    row = pl.BlockSpec((tr, Dm), lambda i: (i, 0))
    par = pl.BlockSpec((1, Dm), lambda i: (0, 0))
    return pl.pallas_call(
        body, name="final_loss", grid=(T // tr,), in_specs=[row, row, par],
        out_specs=[row, pl.BlockSpec((1, LANES), lambda i: (0, 0)), par],
        out_shape=[jax.ShapeDtypeStruct((T, Dm), f32), jax.ShapeDtypeStruct((1, LANES), f32), jax.ShapeDtypeStruct((1, Dm), f32)],
        compiler_params=_params("arbitrary"),
    )(h2x, tgt, g)


def _meta_grad(dh0_meta):
    B = dh0_meta.shape[0]

    def body(d_ref, o_ref):
        acc = d_ref[0]
        for b in range(1, B):
            acc = acc + d_ref[b]
        o_ref[...] = acc

    return pl.pallas_call(body, name="meta_grad", out_shape=jax.ShapeDtypeStruct(dh0_meta.shape[1:], f32))(dh0_meta)


def _shift_down(x, k, row):
    return jnp.where(row >= k, pltpu.roll(x, k, 0), 0.0)


def _shift_up(x, k, row):
    n = x.shape[0]
    return jnp.where(row < n - k, pltpu.roll(x, n - k, 0), 0.0)


def _conv_fwd(up, conv_w, conv_b, B, L):
    tc = 256
    nt = D_FF // tc

    def body(xa_ref, xb_ref, wa_ref, wb_ref, ba_ref, bb_ref, o_ref):
        row = lax.broadcasted_iota(jnp.int32, (L, tc), 0)

        def conv(x_ref, w_ref, b_ref):
            x = x_ref[...]
            return (b_ref[...] + w_ref[0:1, :] * _shift_down(x, 2, row) + w_ref[1:2, :] * _shift_down(x, 1, row)
                    + w_ref[2:3, :] * x)

        a = conv(xa_ref, wa_ref, ba_ref)
        b = conv(xb_ref, wb_ref, bb_ref)
        o_ref[...] = (a * _sigmoid(a) * b).astype(o_ref.dtype)

    return pl.pallas_call(
        body, name="conv_fwd", grid=(B, nt),
        in_specs=[pl.BlockSpec((L, tc), lambda b, j: (b, j)), pl.BlockSpec((L, tc), lambda b, j: (b, j + nt)),
                  pl.BlockSpec((3, tc), lambda b, j: (0, j)), pl.BlockSpec((3, tc), lambda b, j: (0, j + nt)),
                  pl.BlockSpec((1, tc), lambda b, j: (0, j)), pl.BlockSpec((1, tc), lambda b, j: (0, j + nt))],
        out_specs=pl.BlockSpec((L, tc), lambda b, j: (b, j)),
        out_shape=jax.ShapeDtypeStruct((B * L, D_FF), bf16), compiler_params=_params("parallel", "parallel"),
    )(up, up, conv_w, conv_w, conv_b, conv_b)


def _conv_bwd(up, dff, conv_w, conv_b, B, L):
    tc = 256
    nt = D_FF // tc

    def body(xa_ref, xb_ref, d_ref, wa_ref, wb_ref, ba_ref, bb_ref, dup_ref, dw_ref):
        row = lax.broadcasted_iota(jnp.int32, (L, tc), 0)
        xs, pre = [], []
        for x_ref, w_ref, b_ref in ((xa_ref, wa_ref, ba_ref), (xb_ref, wb_ref, bb_ref)):
            x = x_ref[...]
            x1 = _shift_down(x, 1, row)
            x2 = _shift_down(x, 2, row)
            xs.append((x, x1, x2))
            pre.append(b_ref[...] + w_ref[0:1, :] * x2 + w_ref[1:2, :] * x1 + w_ref[2:3, :] * x)
        a, b = pre
        s = _sigmoid(a)
        d = d_ref[...]
        grads = (d * b * s * (1.0 + a * (1.0 - s)), d * a * s)

        @pl.when(pl.program_id(1) == 0)
        def _():
            dw_ref[...] = jnp.zeros_like(dw_ref)

        for h, (gr, (x, x1, x2), w_ref) in enumerate(zip(grads, xs, (wa_ref, wb_ref))):
            dup_ref[h] = (w_ref[2:3, :] * gr + w_ref[1:2, :] * _shift_up(gr, 1, row)
                          + w_ref[0:1, :] * _shift_up(gr, 2, row)).astype(dup_ref.dtype)
            dw_ref[h, 0:1, :] += jnp.sum(gr * x2, axis=0, keepdims=True)
            dw_ref[h, 1:2, :] += jnp.sum(gr * x1, axis=0, keepdims=True)
            dw_ref[h, 2:3, :] += jnp.sum(gr * x, axis=0, keepdims=True)
            dw_ref[h, 3:4, :] += jnp.sum(gr, axis=0, keepdims=True)

    return pl.pallas_call(
        body, name="conv_bwd", grid=(nt, B),
        in_specs=[pl.BlockSpec((L, tc), lambda j, b: (b, j)), pl.BlockSpec((L, tc), lambda j, b: (b, j + nt)),
                  pl.BlockSpec((L, tc), lambda j, b: (b, j)),
                  pl.BlockSpec((3, tc), lambda j, b: (0, j)), pl.BlockSpec((3, tc), lambda j, b: (0, j + nt)),
                  pl.BlockSpec((1, tc), lambda j, b: (0, j)), pl.BlockSpec((1, tc), lambda j, b: (0, j + nt))],
        out_specs=[pl.BlockSpec((2, L, tc), lambda j, b: (0, b, j)), pl.BlockSpec((2, SUBLANES, tc), lambda j, b: (0, 0, j))],
        out_shape=[jax.ShapeDtypeStruct((2, B * L, D_FF), bf16), jax.ShapeDtypeStruct((2, SUBLANES, D_FF), f32)],
        compiler_params=_params("parallel", "arbitrary"),
    )(up, up, dff, conv_w, conv_w, conv_b, conv_b)


GELU_C = math.sqrt(2.0 / math.pi)
GELU_A = 0.044715


def _gelu(x):
    return 0.5 * x * (1.0 + jnp.tanh(GELU_C * (x + GELU_A * x * x * x)))


def _gelu_grad(x):
    t = jnp.tanh(GELU_C * (x + GELU_A * x * x * x))
    return 0.5 * (1.0 + t) + 0.5 * x * (1.0 - t * t) * GELU_C * (1.0 + 3.0 * GELU_A * x * x)


def _cmul_add(xr, xi, ar, ai, sr, si):
    return xr + ar * sr - ai * si, xi + ar * si + ai * sr


def _s5_scan_fwd(s_ref, pw_ref, L):
    ns = SLAB_NS
    row = lax.broadcasted_iota(jnp.int32, (SUBLANES, ns), 0)
    pr = pw_ref[0, 0:SUBLANES, :]
    pi = pw_ref[1, 0:SUBLANES, :]

    def step(i, carry):
        cr, ci = carry
        r0 = pl.multiple_of(i * SUBLANES, SUBLANES)
        xr = s_ref[pl.ds(r0, SUBLANES), 0:ns]
        xi = s_ref[pl.ds(r0, SUBLANES), ns:2 * ns]
        for k in (1, 2, 4):
            xr, xi = _cmul_add(xr, xi, pr[k - 1:k, :], pi[k - 1:k, :], _shift_down(xr, k, row), _shift_down(xi, k, row))
        xr, xi = _cmul_add(xr, xi, pr, pi, cr, ci)
        s_ref[pl.ds(r0, SUBLANES), 0:ns] = xr
        s_ref[pl.ds(r0, SUBLANES), ns:2 * ns] = xi
        return xr[SUBLANES - 1:SUBLANES, :], xi[SUBLANES - 1:SUBLANES, :]

    z = jnp.zeros((1, ns), f32)
    lax.fori_loop(0, L // SUBLANES, step, (z, z))


def _s5_project_in(u_ref, bs_ref, s_ref, L, rc):
    for r in range(0, L, rc):
        s_ref[r:r + rc, :] = jnp.dot(u_ref[r:r + rc, :].astype(bf16), bs_ref[...], preferred_element_type=f32)


def _s5_fwd(p, bs, cs, pw, d_skip, B, L):
    rc = _tile(L, 344)

    def body(u_ref, bs_ref, cs_ref, pw_ref, d_ref, y_ref, s_ref):
        _s5_project_in(u_ref, bs_ref, s_ref, L, rc)
        _s5_scan_fwd(s_ref, pw_ref, L)
        for r in range(0, L, rc):
            ypre = (jnp.dot(s_ref[r:r + rc, :].astype(bf16), cs_ref[...], preferred_element_type=f32)
                    + d_ref[...] * u_ref[r:r + rc, :])
            y_ref[r:r + rc, :] = _gelu(ypre)

    ucol = SEG_U * (D_MODEL // SLAB_CH)
    return pl.pallas_call(
        body, name="s5_fwd", grid=(B, N_SLAB),
        in_specs=[pl.BlockSpec((L, SLAB_CH), lambda b, s: (b, ucol + s)),
                  pl.BlockSpec((None, SLAB_CH, 2 * SLAB_NS), lambda b, s: (s, 0, 0)),
                  pl.BlockSpec((None, 2 * SLAB_NS, SLAB_CH), lambda b, s: (s, 0, 0)),
                  pl.BlockSpec((None, 2, 2 * SUBLANES, SLAB_NS), lambda b, s: (s, 0, 0, 0)),
                  pl.BlockSpec((1, SLAB_CH), lambda b, s: (0, s))],
        out_specs=pl.BlockSpec((L, SLAB_CH), lambda b, s: (b, s)),
        out_shape=jax.ShapeDtypeStruct((B * L, D_MODEL), f32),
        scratch_shapes=[pltpu.VMEM((L, 2 * SLAB_NS), f32)],
        compiler_params=_params("parallel", "parallel"),
    )(p, bs, cs, pw, d_skip)


def _s5_bwd(p, dya0, dp, bs, cs, pw, d_skip, B, L):
    rc = _tile(L, 344)
    ns = SLAB_NS
    nt = L // SUBLANES

    def body(u_ref, dy_ref, dp_in, bs_ref, cs_ref, pw_ref, d_ref, du_ref, dbs_ref, dcs_ref, da_ref, dd_ref,
             s_ref, lam_ref, dyp_ref):
        del dp_in
        b = pl.program_id(1)

        @pl.when(b == 0)
        def _():
            dbs_ref[...] = jnp.zeros_like(dbs_ref)
            dcs_ref[...] = jnp.zeros_like(dcs_ref)
            da_ref[...] = jnp.zeros_like(da_ref)
            dd_ref[...] = jnp.zeros_like(dd_ref)

        _s5_project_in(u_ref, bs_ref, s_ref, L, rc)
        _s5_scan_fwd(s_ref, pw_ref, L)
        for r in range(0, L, rc):
            u = u_ref[r:r + rc, :]
            sb = s_ref[r:r + rc, :].astype(bf16)
            ypre = jnp.dot(sb, cs_ref[...], preferred_element_type=f32) + d_ref[...] * u
            dyp = dy_ref[r:r + rc, :] * _gelu_grad(ypre)
            dyp_ref[r:r + rc, :] = dyp
            dd_ref[...] += jnp.sum(dyp * u, axis=0, keepdims=True)
            dypb = dyp.astype(bf16)
            dcs_ref[...] += lax.dot_general(sb, dypb, _DIMS["tn"], preferred_element_type=f32)
            lam_ref[r:r + rc, :] = lax.dot_general(dypb, cs_ref[...], _DIMS["nt"], preferred_element_type=f32)

        row = lax.broadcasted_iota(jnp.int32, (SUBLANES, ns), 0)
        pr = pw_ref[0, 0:SUBLANES, :]
        pi = -pw_ref[1, 0:SUBLANES, :]
        qr = pw_ref[0, SUBLANES:2 * SUBLANES, :]
        qi = -pw_ref[1, SUBLANES:2 * SUBLANES, :]

        def step(j, carry):
            cr, ci, ar, ai = carry
            i = nt - 1 - j
            r0 = pl.multiple_of(i * SUBLANES, SUBLANES)
            xr = lam_ref[pl.ds(r0, SUBLANES), 0:ns]
            xi = lam_ref[pl.ds(r0, SUBLANES), ns:2 * ns]
            for k in (1, 2, 4):
                xr, xi = _cmul_add(xr, xi, pr[k - 1:k, :], pi[k - 1:k, :], _shift_up(xr, k, row), _shift_up(xi, k, row))
            xr, xi = _cmul_add(xr, xi, qr, qi, cr, ci)
            lam_ref[pl.ds(r0, SUBLANES), 0:ns] = xr
            lam_ref[pl.ds(r0, SUBLANES), ns:2 * ns] = xi
            rp = pl.multiple_of(jnp.maximum(i - 1, 0) * SUBLANES, SUBLANES)
            live = jnp.where(i > 0, 1.0, 0.0)
            lr_ = s_ref[pl.ds(rp + SUBLANES - 1, 1), 0:ns] * live
            li_ = s_ref[pl.ds(rp + SUBLANES - 1, 1), ns:2 * ns] * live
            spr = jnp.where(row == 0, lr_, pltpu.roll(s_ref[pl.ds(r0, SUBLANES), 0:ns], 1, 0))
            spi = jnp.where(row == 0, li_, pltpu.roll(s_ref[pl.ds(r0, SUBLANES), ns:2 * ns], 1, 0))
            ar = ar + xr * spr + xi * spi
            ai = ai + xi * spr - xr * spi
            return xr[0:1, :], xi[0:1, :], ar, ai

        z1 = jnp.zeros((1, ns), f32)
        z8 = jnp.zeros((SUBLANES, ns), f32)
        _, _, ar, ai = lax.fori_loop(0, nt, step, (z1, z1, z8, z8))
        da_ref[0:1, :] += jnp.sum(ar, axis=0, keepdims=True)
        da_ref[1:2, :] += jnp.sum(ai, axis=0, keepdims=True)

        for r in range(0, L, rc):
            lamb = lam_ref[r:r + rc, :].astype(bf16)
            dbs_ref[...] += lax.dot_general(u_ref[r:r + rc, :].astype(bf16), lamb, _DIMS["tn"], preferred_element_type=f32)
            du = (lax.dot_general(lamb, bs_ref[...], _DIMS["nt"], preferred_element_type=f32)
                  + d_ref[...] * dyp_ref[r:r + rc, :])
            du_ref[r:r + rc, :] = du.astype(du_ref.dtype)

    ucol = SEG_U * (D_MODEL // SLAB_CH)
    T = B * L
    return pl.pallas_call(
        body, name="s5_bwd", grid=(N_SLAB, B),
        in_specs=[pl.BlockSpec((L, SLAB_CH), lambda s, b: (b, ucol + s)),
                  pl.BlockSpec((L, SLAB_CH), lambda s, b: (b, s)),
                  ANY,
                  pl.BlockSpec((None, SLAB_CH, 2 * SLAB_NS), lambda s, b: (s, 0, 0)),
                  pl.BlockSpec((None, 2 * SLAB_NS, SLAB_CH), lambda s, b: (s, 0, 0)),
                  pl.BlockSpec((None, 2, 2 * SUBLANES, SLAB_NS), lambda s, b: (s, 0, 0, 0)),
                  pl.BlockSpec((1, SLAB_CH), lambda s, b: (0, s))],
        out_specs=[pl.BlockSpec((None, L, SLAB_CH), lambda s, b: (SEG_U, b, s)),
                   pl.BlockSpec((None, SLAB_CH, 2 * SLAB_NS), lambda s, b: (s, 0, 0)),
                   pl.BlockSpec((None, 2 * SLAB_NS, SLAB_CH), lambda s, b: (s, 0, 0)),
                   pl.BlockSpec((None, 2, SLAB_NS), lambda s, b: (s, 0, 0)),
                   pl.BlockSpec((1, SLAB_CH), lambda s, b: (0, s))],
        out_shape=[jax.ShapeDtypeStruct((N_SEG, T, D_MODEL), bf16),
                   jax.ShapeDtypeStruct((N_SLAB, SLAB_CH, 2 * SLAB_NS), f32),
                   jax.ShapeDtypeStruct((N_SLAB, 2 * SLAB_NS, SLAB_CH), f32),
                   jax.ShapeDtypeStruct((N_SLAB, 2, SLAB_NS), f32),
                   jax.ShapeDtypeStruct((1, D_MODEL), f32)],
        scratch_shapes=[pltpu.VMEM((L, 2 * SLAB_NS), f32), pltpu.VMEM((L, 2 * SLAB_NS), f32), pltpu.VMEM((L, SLAB_CH), f32)],
        input_output_aliases={2: 0},
        compiler_params=_params("parallel", "arbitrary"),
    )(p, dya0, dp, bs, cs, pw, d_skip)


def _dotb(a, b, dims="nn"):
    return lax.dot_general(a.astype(bf16), b.astype(bf16), _DIMS[dims], preferred_element_type=f32)


def _chunk_cumsum(x, pos):
    k = 1
    while k < CHUNK:
        x = x + jnp.where(pos >= k, pltpu.roll(x, k, 0), 0.0)
        k *= 2
    return x


def _chunk_rev_cumsum(x, pos):
    n = x.shape[0]
    k = 1
    while k < CHUNK:
        x = x + jnp.where(pos < CHUNK - k, pltpu.roll(x, n - k, 0), 0.0)
        k *= 2
    return x


def _hgrn_local(q, fl, lb, pos):
    sg = _sigmoid(fl)
    f = lb + (1.0 - lb) * sg
    g = jnp.log(f)
    cum = _chunk_cumsum(g, pos)
    rest = _chunk_rev_cumsum(g, pos) - g
    e = jnp.exp(cum)
    em = jnp.exp(-cum)
    eo = jnp.exp(rest)
    k = 1.0 - f
    return sg, f, e, em, eo, q * e, k * em, k * eo, jnp.exp(cum + rest)


def _hgrn_block_mask(n):
    r = lax.broadcasted_iota(jnp.int32, (n, n), 0)
    c = lax.broadcasted_iota(jnp.int32, (n, n), 1)
    return ((r & -CHUNK) == (c & -CHUNK)) & (c <= r)


def _chunk_pos(n):
    return lax.broadcasted_iota(jnp.int32, (n, HEAD_DIM), 0) & (CHUNK - 1)


def _hgrn_block_rows(L):
    return _tile(L, 688, CHUNK)


def _chunk_rows(c):
    return pl.ds(pl.multiple_of(c * CHUNK, CHUNK), CHUNK)


def _chunk_loop(nc, step):
    rep = max(u for u in range(1, 49) if nc % u == 0)

    def body(i, carry):
        for u in range(rep):
            step(i * rep + u)
        return carry

    lax.fori_loop(0, nc // rep, body, 0)


def _hgrn_specs(L, order):
    hb = D_MODEL // HEAD_DIM

    def spec(seg):
        if order == "bh":
            return pl.BlockSpec((L, HEAD_DIM), lambda b, h: (b, seg * hb + h))
        return pl.BlockSpec((L, HEAD_DIM), lambda h, b: (b, seg * hb + h))

    return [spec(SEG_Q), spec(SEG_F), spec(SEG_I), spec(SEG_OG)]


def _hgrn_fwd(p, lb, norm_g, B, L):
    nc = L // CHUNK

    rb = _hgrn_block_rows(L)

    def body(q_ref, f_ref, v_ref, og_ref, lb_ref, ng_ref, y_ref, qt_s, ko_s, vb_s, dec_s, o_s, u_s, sb_s):
        lbv = lb_ref[...]
        ngv = ng_ref[...]
        mask = _hgrn_block_mask(rb)
        pos = _chunk_pos(rb)

        for r in range(0, L, rb):
            rows = slice(r, r + rb)
            _, _, _, _, _, qt, kt, ko, dec = _hgrn_local(q_ref[rows, :], f_ref[rows, :], lbv, pos)
            vb = v_ref[rows, :].astype(bf16)
            qtb = qt.astype(bf16)
            pm = jnp.where(mask, _dotb(qtb, kt, "nt"), 0.0)
            o_s[rows, :] = _dotb(pm, vb)
            qt_s[rows, :] = qtb
            ko_s[rows, :] = ko.astype(bf16)
            vb_s[rows, :] = vb
            dec_s[rows, :] = dec

        def update(c):
            rows = _chunk_rows(c)
            u_s[c] = _dotb(vb_s[rows, :], ko_s[rows, :], "tn")

        def chain(c, st):
            sb_s[c] = st.astype(bf16)
            return st * dec_s[_chunk_rows(c), :][0:1, :] + u_s[c]

        def attend(c):
            rows = _chunk_rows(c)
            o_s[rows, :] += _dotb(qt_s[rows, :], sb_s[c], "nt")

        _chunk_loop(nc, update)
        lax.fori_loop(0, nc, chain, jnp.zeros((HEAD_DIM, HEAD_DIM), f32))
        _chunk_loop(nc, attend)

        for r in range(0, L, rb):
            rows = slice(r, r + rb)
            o = o_s[rows, :]
            og = og_ref[rows, :]
            on = o * lax.rsqrt(jnp.mean(o * o, axis=-1, keepdims=True) + EPS) * ngv
            y_ref[rows, :] = (on * og * _sigmoid(og)).astype(y_ref.dtype)

    return pl.pallas_call(
        body, name="hgrn_fwd", grid=(B, HEADS),
        in_specs=_hgrn_specs(L, "bh") + [pl.BlockSpec((1, HEAD_DIM), lambda b, h: (0, h)),
                                          pl.BlockSpec((1, HEAD_DIM), lambda b, h: (0, 0))],
        out_specs=pl.BlockSpec((L, HEAD_DIM), lambda b, h: (b, h)),
        out_shape=jax.ShapeDtypeStruct((B * L, D_MODEL), bf16),
        scratch_shapes=[pltpu.VMEM((L, HEAD_DIM), bf16), pltpu.VMEM((L, HEAD_DIM), bf16), pltpu.VMEM((L, HEAD_DIM), bf16),
                        pltpu.VMEM((L, HEAD_DIM), f32), pltpu.VMEM((L, HEAD_DIM), f32),
                        pltpu.VMEM((nc, HEAD_DIM, HEAD_DIM), f32), pltpu.VMEM((nc, HEAD_DIM, HEAD_DIM), bf16)],
        compiler_params=_params("parallel", "parallel"),
    )(p, p, p, p, lb, norm_g)


def _hgrn_bwd(p, dyb, dp, lb, norm_g, B, L):
    nc = L // CHUNK

    rb = _hgrn_block_rows(L)

    def body(q_ref, f_ref, v_ref, og_ref, dy_ref, dp_in, lb_ref, ng_ref, dseg_ref, dlb_ref, dng_ref,
             st_ref, u_s, dsb_s, qt_s, kt_s, ko_s, vb_s, do_s, dec_s, o_s, dqt_s, dkt_s, dko_s, dv_s, ddec_s):
        del dp_in
        lbv = lb_ref[...]
        ngv = ng_ref[...]
        mask = _hgrn_block_mask(rb)
        pos = _chunk_pos(rb)
        blocks = [slice(r, r + rb) for r in range(0, L, rb)]

        @pl.when(pl.program_id(1) == 0)
        def _():
            dlb_ref[...] = jnp.zeros_like(dlb_ref)

        @pl.when((pl.program_id(0) == 0) & (pl.program_id(1) == 0))
        def _():
            dng_ref[...] = jnp.zeros_like(dng_ref)

        def scores(rows):
            return jnp.where(mask, _dotb(qt_s[rows, :], kt_s[rows, :], "nt"), 0.0).astype(bf16)

        for rows in blocks:
            _, _, _, _, _, qt, kt, ko, dec = _hgrn_local(q_ref[rows, :], f_ref[rows, :], lbv, pos)
            qt_s[rows, :] = qt.astype(bf16)
            kt_s[rows, :] = kt.astype(bf16)
            ko_s[rows, :] = ko.astype(bf16)
            vb_s[rows, :] = v_ref[rows, :].astype(bf16)
            dec_s[rows, :] = dec
            o_s[rows, :] = _dotb(scores(rows), vb_s[rows, :])

        def update(c):
            rows = _chunk_rows(c)
            u_s[c] = _dotb(vb_s[rows, :], ko_s[rows, :], "tn")

        def chain(c, st):
            st_ref[c] = st
            return st * dec_s[_chunk_rows(c), :][0:1, :] + u_s[c]

        def attend(c):
            rows = _chunk_rows(c)
            o_s[rows, :] += _dotb(qt_s[rows, :], st_ref[c], "nt")

        _chunk_loop(nc, update)
        lax.fori_loop(0, nc, chain, jnp.zeros((HEAD_DIM, HEAD_DIM), f32))
        _chunk_loop(nc, attend)

        dng = jnp.zeros((1, HEAD_DIM), f32)
        for rows in blocks:
            o = o_s[rows, :]
            og = og_ref[rows, :]
            dy = dy_ref[rows, :]
            rs = lax.rsqrt(jnp.mean(o * o, axis=-1, keepdims=True) + EPS)
            xn = o * rs
            so = _sigmoid(og)
            dseg_ref[SEG_OG, rows, :] = (dy * xn * ngv * so * (1.0 + og * (1.0 - so))).astype(dseg_ref.dtype)
            don = dy * og * so
            dng = dng + jnp.sum(don * xn, axis=0, keepdims=True)
            dxo = don * ngv
            do = (rs * (dxo - xn * jnp.mean(dxo * xn, axis=-1, keepdims=True))).astype(bf16)
            do_s[rows, :] = do
            dpm = jnp.where(mask, _dotb(do, vb_s[rows, :], "nt"), 0.0).astype(bf16)
            dqt_s[rows, :] = _dotb(dpm, kt_s[rows, :])
            dkt_s[rows, :] = _dotb(dpm, qt_s[rows, :], "tn")
            dv_s[rows, :] = _dotb(scores(rows), do, "tn")
        dng_ref[...] += dng

        def rupdate(c):
            rows = _chunk_rows(c)
            u_s[c] = _dotb(do_s[rows, :], qt_s[rows, :], "tn")

        def rchain(j, dst):
            c = nc - 1 - j
            rows = _chunk_rows(c)
            dsb_s[c] = dst.astype(bf16)
            ddec_s[rows, :] = jnp.broadcast_to(jnp.sum(dst * st_ref[c], axis=0, keepdims=True), (CHUNK, HEAD_DIM))
            return dst * dec_s[rows, :][0:1, :] + u_s[c]

        def rattend(c):
            rows = _chunk_rows(c)
            dst = dsb_s[c]
            dqt_s[rows, :] += _dotb(do_s[rows, :], st_ref[c])
            dv_s[rows, :] += _dotb(ko_s[rows, :], dst, "nt")
            dko_s[rows, :] = _dotb(vb_s[rows, :], dst)

        _chunk_loop(nc, rupdate)
        lax.fori_loop(0, nc, rchain, jnp.zeros((HEAD_DIM, HEAD_DIM), f32))
        _chunk_loop(nc, rattend)

        dlb = jnp.zeros((1, HEAD_DIM), f32)
        for rows in blocks:
            sg, f, e, em, eo, qt, kt, ko, dec = _hgrn_local(q_ref[rows, :], f_ref[rows, :], lbv, pos)
            dqt = dqt_s[rows, :]
            dkt = dkt_s[rows, :]
            dko = dko_s[rows, :]
            dko_ko = dko * ko
            dcum = dqt * qt - dkt * kt - dko_ko
            chunk_tot = _chunk_cumsum(dko_ko, pos) + _chunk_rev_cumsum(dko_ko, pos) - dko_ko
            dcum = dcum + jnp.where(pos == CHUNK - 1, chunk_tot + ddec_s[rows, :] * dec, 0.0)
            df = _chunk_rev_cumsum(dcum, pos) / f - (dkt * em + dko * eo)
            dlb = dlb + jnp.sum(df * (1.0 - sg), axis=0, keepdims=True)
            dseg_ref[SEG_Q, rows, :] = (dqt * e).astype(dseg_ref.dtype)
            dseg_ref[SEG_F, rows, :] = (df * (1.0 - lbv) * sg * (1.0 - sg)).astype(dseg_ref.dtype)
            dseg_ref[SEG_I, rows, :] = dv_s[rows, :].astype(dseg_ref.dtype)
        dlb_ref[...] += dlb

    T = B * L
    sb = pltpu.VMEM((L, HEAD_DIM), bf16)
    sf = pltpu.VMEM((L, HEAD_DIM), f32)
    return pl.pallas_call(
        body, name="hgrn_bwd", grid=(HEADS, B),
        in_specs=_hgrn_specs(L, "hb") + [pl.BlockSpec((L, HEAD_DIM), lambda h, b: (b, h)), ANY,
                                          pl.BlockSpec((1, HEAD_DIM), lambda h, b: (0, h)),
                                          pl.BlockSpec((1, HEAD_DIM), lambda h, b: (0, 0))],
        out_specs=[pl.BlockSpec((4, L, HEAD_DIM), lambda h, b: (0, b, h)),
                   pl.BlockSpec((1, HEAD_DIM), lambda h, b: (0, h)),
                   pl.BlockSpec((1, HEAD_DIM), lambda h, b: (0, 0))],
        out_shape=[jax.ShapeDtypeStruct((N_SEG, T, D_MODEL), bf16), jax.ShapeDtypeStruct((1, D_MODEL), f32),
                   jax.ShapeDtypeStruct((1, HEAD_DIM), f32)],
        scratch_shapes=[pltpu.VMEM((nc, HEAD_DIM, HEAD_DIM), f32), pltpu.VMEM((nc, HEAD_DIM, HEAD_DIM), f32),
                        pltpu.VMEM((nc, HEAD_DIM, HEAD_DIM), bf16), sb, sb, sb, sb, sb, sf, sf, sf, sf, sf, sf, sf],
        input_output_aliases={5: 0},
        compiler_params=_params("arbitrary", "arbitrary"),
    )(p, p, p, p, dyb, dp, lb, norm_g)


def _dz1(dp, w_in_phys):
    _, T, Dm = dp.shape
    tm = _tile(T, 1032)
    return _mm("dz1", dp, w_in_phys, "nt", (T // tm, 1, N_SEG),
               pl.BlockSpec((None, tm, Dm), lambda i, j, k: (k, i, 0)),
               pl.BlockSpec((Dm, Dm), lambda i, j, k: (0, k)),
               jax.ShapeDtypeStruct((T, Dm), f32), pl.BlockSpec((tm, Dm), lambda i, j, k: (i, 0)), (tm, Dm))


def _dw_in(z1, dp):
    _, T, Dm = dp.shape
    tn = 256
    per_seg = Dm // tn
    per_chip = IN_COLS // N_CHIPS // tn
    tk = _tile(T, 1376)

    def out_idx(i, j, k):
        logical = ((j // per_seg + 1) % N_SEG) * per_seg + j % per_seg
        return (logical // per_chip, 0, logical % per_chip)

    return _mm("dw_in", z1, dp, "tn", (1, IN_COLS // tn, T // tk),
               pl.BlockSpec((tk, Dm), lambda i, j, k: (k, 0)),
               pl.BlockSpec((None, tk, tn), lambda i, j, k: (j // per_seg, k, j % per_seg)),
               jax.ShapeDtypeStruct((N_CHIPS, Dm, IN_COLS // N_CHIPS), f32),
               pl.BlockSpec((None, Dm, tn), out_idx), (Dm, tn))


def _dz2(dup, w_up):
    _, T, _ = dup.shape
    tm = _tile(T, 1032)
    tk = D_FF // 2
    return _mm("dz2", dup, w_up, "nt", (T // tm, 1, 4),
               pl.BlockSpec((None, tm, tk), lambda i, j, k: (k // 2, i, k % 2)),
               pl.BlockSpec((D_MODEL, tk), lambda i, j, k: (0, k)),
               jax.ShapeDtypeStruct((T, D_MODEL), f32), pl.BlockSpec((tm, D_MODEL), lambda i, j, k: (i, 0)), (tm, D_MODEL))


def _dw_up(z2, dup):
    _, T, _ = dup.shape
    tn = D_FF // 2
    tk = _tile(T, 688)
    return _mm("dw_up", z2, dup, "tn", (1, N_CHIPS, T // tk),
               pl.BlockSpec((tk, D_MODEL), lambda i, j, k: (k, 0)),
               pl.BlockSpec((None, tk, tn), lambda i, j, k: (j // 2, k, j % 2)),
               jax.ShapeDtypeStruct((N_CHIPS, D_MODEL, tn), f32),
               pl.BlockSpec((None, D_MODEL, tn), lambda i, j, k: (j, 0, 0)), (D_MODEL, tn))


def _place():
    x, y, c = lax.axis_index("x"), lax.axis_index("y"), lax.axis_index("c")
    chips = [(1 - x, y), (x, 1 - y), (1 - x, 1 - y)]
    return x, y, c, chips


def _allgather_chips(arrs):
    n = len(arrs)

    def body(*refs):
        ins, outs = refs[:n], refs[n:2 * n]
        send, recv, local = refs[2 * n:]
        x, y, c, chips = _place()
        me = 2 * x + y

        def copy(a, k, slot):
            px, py = chips[k]
            return pltpu.make_async_remote_copy(src_ref=ins[a], dst_ref=outs[a].at[slot], send_sem=send.at[3 * a + k],
                                                recv_sem=recv.at[3 * a + k], device_id=(px, py, c), device_id_type=MESH)

        for a in range(n):
            pltpu.make_async_copy(ins[a], outs[a].at[me], local.at[a]).start()
            for k in range(3):
                copy(a, k, me).start()
        for a in range(n):
            for k, (px, py) in enumerate(chips):
                copy(a, k, 2 * px + py).wait_recv()
        for a in range(n):
            pltpu.make_async_copy(ins[a], outs[a].at[me], local.at[a]).wait()
            for k in range(3):
                copy(a, k, me).wait_send()

    return pl.pallas_call(
        body, name="allgather_chips", in_specs=[ANY] * n, out_specs=[ANY] * n,
        out_shape=[jax.ShapeDtypeStruct((N_CHIPS,) + a.shape, a.dtype) for a in arrs],
        scratch_shapes=[pltpu.SemaphoreType.DMA((3 * n,)), pltpu.SemaphoreType.DMA((3 * n,)), pltpu.SemaphoreType.DMA((n,))],
    )(*arrs)


def _allgather_split(arrs):
    n = len(arrs)

    def body(*refs):
        ins, outs = refs[:n], refs[n:2 * n]
        send, recv, fsend, frecv = refs[2 * n:]
        x, y, c, chips = _place()
        me = 2 * x + y

        def half(a, core):
            rh = ins[a].shape[0] // 2
            return pl.ds(core * rh, rh)

        def copy(a, k, slot):
            px, py = chips[k]
            return pltpu.make_async_remote_copy(src_ref=ins[a].at[half(a, c), :], dst_ref=outs[a].at[slot, half(a, c), :],
                                                send_sem=send.at[3 * a + k], recv_sem=recv.at[3 * a + k],
                                                device_id=(px, py, c), device_id_type=MESH)

        def forward(a, k, core):
            px, py = chips[k]
            rows = outs[a].at[2 * px + py, half(a, core), :]
            return pltpu.make_async_remote_copy(src_ref=rows, dst_ref=rows, send_sem=fsend.at[3 * a + k],
                                                recv_sem=frecv.at[3 * a + k], device_id=(x, y, 1 - c), device_id_type=MESH)

        for a in range(n):
            for k in range(3):
                copy(a, k, me).start()
        for a in range(n):
            for k, (px, py) in enumerate(chips):
                copy(a, k, 2 * px + py).wait_recv()
                forward(a, k, c).start()
        for a in range(n):
            for k in range(3):
                forward(a, k, 1 - c).wait_recv()
        for a in range(n):
            for k in range(3):
                copy(a, k, me).wait_send()
                forward(a, k, c).wait_send()

    return pl.pallas_call(
        body, name="allgather_split", in_specs=[ANY] * n, out_specs=[ANY] * n,
        out_shape=[jax.ShapeDtypeStruct((N_CHIPS,) + a.shape, a.dtype) for a in arrs],
        scratch_shapes=[pltpu.SemaphoreType.DMA((3 * n,)) for _ in range(4)],
    )(*arrs)


def _sibling_halves(parts):
    n = len(parts)

    def body(*refs):
        ins, outs = refs[:n], refs[n:2 * n]
        send, recv = refs[2 * n:]
        x, y, c, _ = _place()

        def copy(a):
            rh = ins[a].shape[1] // 2
            return pltpu.make_async_remote_copy(src_ref=ins[a].at[:, pl.ds((1 - c) * rh, rh), :], dst_ref=outs[a],
                                                send_sem=send.at[a], recv_sem=recv.at[a], device_id=(x, y, 1 - c),
                                                device_id_type=MESH)

        for a in range(n):
            copy(a).start()
        for a in range(n):
            copy(a).wait_recv()
        for a in range(n):
            copy(a).wait_send()

    return pl.pallas_call(
        body, name="sibling_halves", in_specs=[ANY] * n, out_specs=[ANY] * n,
        out_shape=[jax.ShapeDtypeStruct((a.shape[0], a.shape[1] // 2, a.shape[2]), a.dtype) for a in parts],
        scratch_shapes=[pltpu.SemaphoreType.DMA((n,)), pltpu.SemaphoreType.DMA((n,))],
    )(*parts)


def _add_own_half(name, part, got, core):
    nchip, R, C = part.shape
    rh = R // 2
    tr = _tile(rh, 256, 2 * SUBLANES)
    nt = rh // tr

    def body(core_ref, a_ref, b_ref, o_ref):
        del core_ref
        o_ref[...] = (a_ref[...] + b_ref[...]).astype(o_ref.dtype)

    return pl.pallas_call(
        body, name=name,
        grid_spec=pltpu.PrefetchScalarGridSpec(
            num_scalar_prefetch=1, grid=(nchip, nt),
            in_specs=[pl.BlockSpec((None, tr, C), lambda j, i, core_ref: (j, core_ref[0] * nt + i, 0)),
                      pl.BlockSpec((None, tr, C), lambda j, i, core_ref: (j, i, 0))],
            out_specs=pl.BlockSpec((None, tr, C), lambda j, i, core_ref: (j, i, 0))),
        out_shape=jax.ShapeDtypeStruct((nchip, rh, C), bf16), compiler_params=_params("parallel", "parallel"),
    )(core, part, got)


def _chip_exchange(sums):
    n = len(sums)

    def body(*refs):
        ins, outs = refs[:n], refs[n:2 * n]
        send, recv = refs[2 * n:]
        x, y, c, chips = _place()
        me = 2 * x + y

        def copy(a, k, slot):
            px, py = chips[k]
            return pltpu.make_async_remote_copy(src_ref=ins[a].at[2 * px + py], dst_ref=outs[a].at[slot], send_sem=send.at[3 * a + k],
                                                recv_sem=recv.at[3 * a + k], device_id=(px, py, c), device_id_type=MESH)

        for a in range(n):
            for k in range(3):
                copy(a, k, me).start()
        for a in range(n):
            for k, (px, py) in enumerate(chips):
                copy(a, k, 2 * px + py).wait_recv()
        for a in range(n):
            for k in range(3):
                copy(a, k, me).wait_send()

    return pl.pallas_call(
        body, name="chip_exchange", in_specs=[ANY] * n, out_specs=[ANY] * n,
        out_shape=[jax.ShapeDtypeStruct(a.shape, a.dtype) for a in sums],
        scratch_shapes=[pltpu.SemaphoreType.DMA((3 * n,)), pltpu.SemaphoreType.DMA((3 * n,))],
    )(*sums)


def _sum_chips(name, slots, sums, where):
    nchip, rh, C = slots.shape
    tr = _tile(rh, 256, 2 * SUBLANES)
    nt = rh // tr

    def body(where_ref, own_ref, s1_ref, s2_ref, s3_ref, o_ref):
        me = where_ref[0]
        by_dist = [r[...].astype(f32) for r in (own_ref, s1_ref, s2_ref, s3_ref)]
        acc = None
        for j in range(nchip):
            d = me ^ j
            term = jnp.where(d == 0, by_dist[0], jnp.where(d == 1, by_dist[1], jnp.where(d == 2, by_dist[2], by_dist[3])))
            acc = term if acc is None else acc + term
        o_ref[...] = acc

    def other(d):
        return pl.BlockSpec((None, tr, C), lambda i, w: (w[0] ^ d, i, 0))

    return pl.pallas_call(
        body, name=name,
        grid_spec=pltpu.PrefetchScalarGridSpec(
            num_scalar_prefetch=1, grid=(nt,),
            in_specs=[other(0), other(1), other(2), other(3)],
            out_specs=pl.BlockSpec((tr, C), lambda i, w: (w[1] * nt + i, 0))),
        out_shape=jax.ShapeDtypeStruct((2 * rh, C), f32), compiler_params=_params("parallel"),
    )(where, sums, slots, slots, slots)


def _sum_slots(name, slots):
    ns, R, C = slots.shape
    tr = _tile(R, 256)

    def body(s_ref, o_ref):
        acc = s_ref[0]
        for j in range(1, ns):
            acc = acc + s_ref[j]
        o_ref[...] = acc

    return pl.pallas_call(
        body, name=name, grid=(R // tr,), in_specs=[pl.BlockSpec((ns, tr, C), lambda i: (0, i, 0))],
        out_specs=pl.BlockSpec((tr, C), lambda i: (i, 0)), out_shape=jax.ShapeDtypeStruct((R, C), f32),
        compiler_params=_params("parallel"),
    )(slots)


def _sibling_join(fulls):
    n = len(fulls)

    def body(*refs):
        ins, outs = refs[:n], refs[n:2 * n]
        send, recv = refs[2 * n:]
        x, y, c, _ = _place()

        def copy(a, core):
            rh = ins[a].shape[0] // 2
            rows = pl.ds(core * rh, rh)
            return pltpu.make_async_remote_copy(src_ref=ins[a].at[rows, :], dst_ref=outs[a].at[rows, :], send_sem=send.at[a],
                                                recv_sem=recv.at[a], device_id=(x, y, 1 - c), device_id_type=MESH)

        for a in range(n):
            copy(a, c).start()
        for a in range(n):
            copy(a, 1 - c).wait_recv()
        for a in range(n):
            copy(a, c).wait_send()

    return pl.pallas_call(
        body, name="sibling_join", in_specs=[ANY] * n, out_specs=[ANY] * n,
        out_shape=[jax.ShapeDtypeStruct(a.shape, a.dtype) for a in fulls],
        scratch_shapes=[pltpu.SemaphoreType.DMA((n,)), pltpu.SemaphoreType.DMA((n,))],
        input_output_aliases={a: a for a in range(n)},
    )(*fulls)


def _allgather_devices(v):
    def body(v_ref, out_ref, send, recv, local):
        x, y, c, _ = _place()
        me = 4 * x + 2 * y + c

        def peer(k):
            return (1 - x if k & 4 else x, 1 - y if k & 2 else y, 1 - c if k & 1 else c)

        def copy(k, slot):
            return pltpu.make_async_remote_copy(src_ref=v_ref, dst_ref=out_ref.at[slot], send_sem=send.at[k - 1],
                                                recv_sem=recv.at[k - 1], device_id=peer(k), device_id_type=MESH)

        own = pltpu.make_async_copy(v_ref, out_ref.at[me], local)
        own.start()
        for k in range(1, N_DEV):
            copy(k, me).start()
        for k in range(1, N_DEV):
            px, py, pc = peer(k)
            copy(k, 4 * px + 2 * py + pc).wait_recv()
        own.wait()
        for k in range(1, N_DEV):
            copy(k, me).wait_send()

    return pl.pallas_call(
        body, name="allgather_devices", in_specs=[ANY], out_specs=ANY,
        out_shape=jax.ShapeDtypeStruct((N_DEV,) + v.shape, v.dtype),
        scratch_shapes=[pltpu.SemaphoreType.DMA((N_DEV - 1,)), pltpu.SemaphoreType.DMA((N_DEV - 1,)), pltpu.SemaphoreType.DMA],
    )(v)


def _adamw(name, w, g, m, v):
    R, C = w.shape
    tr = _tile(R, 256)
    c1 = 1.0 / (1.0 - ADAM_B1 ** ADAM_STEP)
    c2 = 1.0 / (1.0 - ADAM_B2 ** ADAM_STEP)

    def body(w_ref, g_ref, m_ref, v_ref, d_ref, nm_ref, nv_ref):
        gv = g_ref[...]
        nm = ADAM_B1 * m_ref[...] + (1.0 - ADAM_B1) * gv
        nv = ADAM_B2 * v_ref[...] + (1.0 - ADAM_B2) * gv * gv
        d_ref[...] = -ADAM_LR * ((nm * c1) / (jnp.sqrt(nv * c2) + ADAM_EPS) + ADAM_WD * w_ref[...])
        nm_ref[...] = nm
        nv_ref[...] = nv

    row = pl.BlockSpec((tr, C), lambda i: (i, 0))
    sh = jax.ShapeDtypeStruct((R, C), f32)
    return pl.pallas_call(body, name=name, grid=(R // tr,), in_specs=[row] * 4, out_specs=[row] * 3,
                          out_shape=[sh, sh, sh], compiler_params=_params("parallel"))(w, g, m, v)


def _zoh(lr, li, log_dt, b_re, b_im):
    dt = jnp.exp(log_dt)[:, None]
    mag = jnp.exp(lr * dt)
    ab_re = mag * jnp.cos(li * dt)
    ab_im = mag * jnp.sin(li * dt)
    den = lr * lr + li * li
    nr = ab_re - 1.0
    coef_re = (nr * lr + ab_im * li) / den
    coef_im = (ab_im * lr - nr * li) / den
    bb_re = coef_re[..., None] * b_re - coef_im[..., None] * b_im
    bb_im = coef_re[..., None] * b_im + coef_im[..., None] * b_re
    return ab_re, ab_im, bb_re, bb_im


def _s5_tables(ab_re, ab_im, bb_re, bb_im, c_re, c_im):
    eye = jnp.eye(SLAB_GROUPS, dtype=f32)

    def blk_in(bb):
        return jnp.einsum("sgph,gk->sghkp", bb.reshape(N_SLAB, SLAB_GROUPS, SSM_STATE, SSM_GROUP), eye).reshape(
            N_SLAB, SLAB_CH, SLAB_NS)

    def blk_out(cc):
        return jnp.einsum("sghp,gk->skpgh", cc.reshape(N_SLAB, SLAB_GROUPS, SSM_GROUP, SSM_STATE), eye).reshape(
            N_SLAB, SLAB_NS, SLAB_CH)

    bs = jnp.concatenate([blk_in(bb_re), blk_in(bb_im)], axis=2).astype(bf16)
    cs = jnp.concatenate([blk_out(c_re), blk_out(-c_im)], axis=1).astype(bf16)
    pr, pi = [ab_re], [ab_im]
    for _ in range(SUBLANES - 1):
        pr, pi = pr + [pr[-1] * ab_re - pi[-1] * ab_im], pi + [pr[-1] * ab_im + pi[-1] * ab_re]
    pw = jnp.stack([jnp.stack(pr + pr[::-1]), jnp.stack(pi + pi[::-1])])
    pw = pw.reshape(2, 2 * SUBLANES, N_SLAB, SLAB_NS).transpose(2, 0, 1, 3)
    return bs, cs, pw


def _s5_table_grads(dbs, dcs, da):
    eye = jnp.eye(SLAB_GROUPS, dtype=f32)
    d6 = dbs.reshape(N_SLAB, SLAB_GROUPS, SSM_GROUP, 2, SLAB_GROUPS, SSM_STATE)
    dbb = jnp.einsum("sghrkp,gk->rsgph", d6, eye).reshape(2, SSM_GROUPS, SSM_STATE, SSM_GROUP)
    c6 = dcs.reshape(N_SLAB, 2, SLAB_GROUPS, SSM_STATE, SLAB_GROUPS, SSM_GROUP)
    dcc = jnp.einsum("srkpgh,gk->rsghp", c6, eye).reshape(2, SSM_GROUPS, SSM_GROUP, SSM_STATE)
    dab = da.transpose(1, 0, 2).reshape(2, SSM_GROUPS, SSM_STATE)
    return dab[0], dab[1], dbb[0], dbb[1], dcc[0], -dcc[1]


SMALL = ["mix_norm_g", "ssm_lambda_re", "ssm_lambda_im", "ssm_log_dt", "ssm_b_re", "ssm_b_im", "ssm_c_re", "ssm_c_im",
         "ssm_d", "hgrn_lb_logits", "hgrn_norm_g", "ffn_norm_g", "conv_b", "final_norm_g"]
SHARDED_SMALL = ["meta_tokens", "conv_w"]
BIG = ["w_in", "ssm_w_glu", "w_ssm_proj", "w_hgrn_proj", "w_out", "w_up", "w_down"]
WEIGHTS = ['meta_tokens', 'mix_norm_g', 'w_in', 'ssm_lambda_re', 'ssm_lambda_im', 'ssm_log_dt', 'ssm_b_re', 'ssm_b_im',
           'ssm_c_re', 'ssm_c_im', 'ssm_d', 'ssm_w_glu', 'w_ssm_proj', 'hgrn_lb_logits', 'hgrn_norm_g', 'w_hgrn_proj',
           'w_out', 'ffn_norm_g', 'w_up', 'conv_w', 'conv_b', 'w_down', 'final_norm_g']


def _local_grads(x, tgt, meta, w, full):
    B, S, Dm = x.shape
    L = S + N_META
    T = B * L
    h0 = jnp.concatenate([jnp.broadcast_to(meta[None], (B, N_META, Dm)), x], axis=1).reshape(T, Dm)

    lb_all = jax.nn.softmax(w["hgrn_lb_logits"], axis=0)
    lb = lb_all[0:1]
    zoh_out, zoh_vjp = jax.vjp(_zoh, w["ssm_lambda_re"][0], w["ssm_lambda_im"][0], w["ssm_log_dt"][0],
                               w["ssm_b_re"][0], w["ssm_b_im"][0])
    bs, cs, pw = _s5_tables(*zoh_out, w["ssm_c_re"][0], w["ssm_c_im"][0])

    z1 = _rmsnorm_fwd("mix_norm", h0, w["mix_norm_g"])
    p = _mm_rows("in_proj", z1, full["w_in"], "nn", f32, 1024)
    ya0 = _s5_fwd(p, bs, cs, pw, w["ssm_d"], B, L)
    gl = _mm_rows("glu_proj", ya0, full["ssm_w_glu"], "nn", f32, 1024)
    ya = _glu_fwd(ya0, gl)
    yb = _hgrn_fwd(p, lb, w["hgrn_norm_g"], B, L)
    pa = _mm_rows("ssm_proj", ya, full["w_ssm_proj"], "nn", f32, 1024)
    pb = _mm_rows("hgrn_proj", yb, full["w_hgrn_proj"], "nn", f32, 1024)
    merged = _merge_fwd(p, pa, pb)
    h1 = _mm_rows("out_proj", merged, full["w_out"], "nn", f32, 1024, res=h0)
    z2 = _rmsnorm_fwd("ffn_norm", h1, w["ffn_norm_g"])
    up = _mm_rows("up_proj", z2, full["w_up"], "nn", f32, D_FF // 2)
    ff = _conv_fwd(up, full["conv_w"], w["conv_b"], B, L)
    h2 = _mm_rows("down_proj", ff, full["w_down"], "nn", f32, 1024, res=h1, tk=D_FF // 2)

    h2x = h2.reshape(B, L, Dm)[:, N_META:].reshape(B * S, Dm)
    dh2x, loss, d_final_g = _final_loss(h2x, tgt.reshape(B * S, Dm), w["final_norm_g"].reshape(1, Dm))
    dh2 = jnp.pad(dh2x.reshape(B, S, Dm), ((0, 0), (N_META, 0), (0, 0))).reshape(T, Dm)

    dff = _mm_rows("d_ff", dh2, full["w_down"], "nt", f32, D_FF // 2)
    g_w_down = _mm_wgrad("dw_down", ff, dh2, tn=512)
    dup, dconv = _conv_bwd(up, dff, full["conv_w"], w["conv_b"], B, L)
    dz2 = _dz2(dup, full["w_up"])
    g_w_up = _dw_up(z2, dup)
    dh1, d_ffn_g = _rmsnorm_bwd("ffn_norm_bwd", h1, w["ffn_norm_g"], dz2, dh2)

    dmerged = _mm_rows("d_merged", dh1, full["w_out"], "nt", f32, 1024)
    g_w_out = _mm_wgrad("dw_out", merged, dh1)
    dpa, dpb, dp = _merge_bwd(dmerged, p, pa, pb)
    dya = _mm_rows("d_ya", dpa, full["w_ssm_proj"], "nt", f32, 1024)
    g_w_ssm_proj = _mm_wgrad("dw_ssm_proj", ya, dpa)
    dyb = _mm_rows("d_yb", dpb, full["w_hgrn_proj"], "nt", f32, 1024)
    g_w_hgrn_proj = _mm_wgrad("dw_hgrn_proj", yb, dpb)
    dp, d_lb, d_hgrn_g = _hgrn_bwd(p, dyb, dp, lb, w["hgrn_norm_g"], B, L)
    dgl, dya0_direct = _glu_bwd(dya, ya0, gl)
    dya0 = _mm_rows("d_ya0", dgl, full["ssm_w_glu"], "nt", f32, 1024, res=dya0_direct)
    g_w_glu = _mm_wgrad("dw_glu", ya0, dgl)
    dp, dbs, dcs, da, d_skip = _s5_bwd(p, dya0, dp, bs, cs, pw, w["ssm_d"], B, L)
    dz1 = _dz1(dp, full["w_in"])
    g_w_in = _dw_in(z1, dp)
    dh0, d_mix_g = _rmsnorm_bwd("mix_norm_bwd", h0, w["mix_norm_g"], dz1, dh1)

    dh0 = dh0.reshape(B, L, Dm)
    grad_x = dh0[:, N_META:]
    d_meta = _meta_grad(dh0[:, :N_META])

    d_ab_re, d_ab_im, d_bb_re, d_bb_im, d_c_re, d_c_im = _s5_table_grads(dbs, dcs, da)
    d_lr, d_li, d_log_dt, d_b_re, d_b_im = zoh_vjp((d_ab_re, d_ab_im, d_bb_re, d_bb_im))
    sm0, sm1 = lb_all[0:1], lb_all[1:2]
    d_logits = jnp.concatenate([sm0 * (1.0 - sm0) * d_lb, -sm0 * sm1 * d_lb], axis=0)
    small = {
        "meta_tokens": d_meta, "mix_norm_g": d_mix_g, "ssm_lambda_re": d_lr[None], "ssm_lambda_im": d_li[None],
        "ssm_log_dt": d_log_dt[None], "ssm_b_re": d_b_re[None], "ssm_b_im": d_b_im[None], "ssm_c_re": d_c_re[None],
        "ssm_c_im": d_c_im[None], "ssm_d": d_skip, "hgrn_lb_logits": d_logits, "hgrn_norm_g": d_hgrn_g,
        "ffn_norm_g": d_ffn_g, "conv_w": dconv[:, 0:3, :].transpose(1, 0, 2).reshape(3, 2 * D_FF),
        "conv_b": dconv[:, 3, :].reshape(1, 2 * D_FF), "final_norm_g": d_final_g.reshape(Dm),
    }
    big = {
        "w_in": g_w_in, "ssm_w_glu": g_w_glu.reshape(N_CHIPS, Dm // N_CHIPS, Dm),
        "w_ssm_proj": g_w_ssm_proj.reshape(N_CHIPS, Dm // N_CHIPS, Dm),
        "w_hgrn_proj": g_w_hgrn_proj.reshape(N_CHIPS, Dm // N_CHIPS, Dm), "w_out": g_w_out.reshape(N_CHIPS, Dm // N_CHIPS, Dm),
        "w_up": g_w_up, "w_down": g_w_down.reshape(N_CHIPS, D_FF // N_CHIPS, Dm),
    }
    return loss, grad_x, big, small


def _pack(parts):
    flat = jnp.concatenate([parts[k].reshape(-1) for k in parts])
    n = flat.shape[0]
    rows = -(-n // (SUBLANES * LANES)) * SUBLANES
    flat = jnp.pad(flat, (0, rows * LANES - n))
    return flat.reshape(rows, LANES)


def _unpack(packed, like):
    flat = packed.reshape(-1)
    out, o = {}, 0
    for k, ref in like.items():
        n = math.prod(ref.shape)
        out[k] = flat[o:o + n].reshape(ref.shape)
        o += n
    return out


def kernel(x, meta_tokens, mix_norm_g, w_in, ssm_lambda_re, ssm_lambda_im, ssm_log_dt, ssm_b_re, ssm_b_im, ssm_c_re, ssm_c_im, ssm_d, ssm_w_glu, w_ssm_proj, hgrn_lb_logits, hgrn_norm_g, w_hgrn_proj, w_out, ffn_norm_g, w_up, conv_w, conv_b, w_down, final_norm_g, loss_target, m_meta_tokens, m_mix_norm_g, m_w_in, m_ssm_lambda_re, m_ssm_lambda_im, m_ssm_log_dt, m_ssm_b_re, m_ssm_b_im, m_ssm_c_re, m_ssm_c_im, m_ssm_d, m_ssm_w_glu, m_w_ssm_proj, m_hgrn_lb_logits, m_hgrn_norm_g, m_w_hgrn_proj, m_w_out, m_ffn_norm_g, m_w_up, m_conv_w, m_conv_b, m_w_down, m_final_norm_g, v_meta_tokens, v_mix_norm_g, v_w_in, v_ssm_lambda_re, v_ssm_lambda_im, v_ssm_log_dt, v_ssm_b_re, v_ssm_b_im, v_ssm_c_re, v_ssm_c_im, v_ssm_d, v_ssm_w_glu, v_w_ssm_proj, v_hgrn_lb_logits, v_hgrn_norm_g, v_w_hgrn_proj, v_w_out, v_ffn_norm_g, v_w_up, v_conv_w, v_conv_b, v_w_down, v_final_norm_g):
    args = dict(locals())
    w = {k: args[k] for k in WEIGHTS}
    mom = {k: args["m_" + k] for k in WEIGHTS}
    var = {k: args["v_" + k] for k in WEIGHTS}
    Dm = D_MODEL
    cx, cy, cc = lax.axis_index("x"), lax.axis_index("y"), lax.axis_index("c")
    chip = 2 * cx + cy

    shards = [w[k][0].astype(bf16) for k in BIG]
    gathered = _allgather_split(shards)
    g_in, g_glu, g_sp, g_hp, g_out, g_up, g_down = [
        lax.dynamic_update_slice(g, s[None], (chip, 0, 0)) for g, s in zip(gathered, shards)]
    g_meta, g_cw = _allgather_chips([w["meta_tokens"], w["conv_w"][0]])
    w_in_full = jnp.roll(g_in.transpose(1, 0, 2).reshape(Dm, IN_COLS), -Dm, axis=1)
    full = {
        "w_in": w_in_full, "ssm_w_glu": g_glu.reshape(Dm, Dm), "w_ssm_proj": g_sp.reshape(Dm, Dm),
        "w_hgrn_proj": g_hp.reshape(Dm, Dm), "w_out": g_out.reshape(Dm, Dm),
        "w_up": g_up.transpose(1, 0, 2).reshape(Dm, 2 * D_FF), "w_down": g_down.reshape(D_FF, Dm),
        "conv_w": g_cw.transpose(1, 0, 2).reshape(3, 2 * D_FF),
    }
    meta_full = g_meta.transpose(1, 0, 2).reshape(N_META, Dm)

    loss_part, grad_x, big, small = _local_grads(x, loss_target, meta_full, w, full)

    core = cc.reshape(1).astype(jnp.int32)
    parts = [big[k] for k in BIG]
    got = _sibling_halves(parts)
    sums = [_add_own_half("add_half_" + k, pt, gt, core) for k, pt, gt in zip(BIG, parts, got)]
    slots = _chip_exchange(sums)
    where = jnp.stack([chip, cc]).astype(jnp.int32)
    fulls = [_sum_chips("sum_chips_" + k, sl, sm, where) for k, sl, sm in zip(BIG, slots, sums)]
    g_big = dict(zip(BIG, _sibling_join(fulls)))

    small_all = dict(small)
    small_all["loss"] = loss_part[0, 0:1]
    packed = _pack(small_all)
    reduced = _unpack(_sum_slots("sum_devices", _allgather_devices(packed)), small_all)
    loss = reduced.pop("loss")[0]
    mcols = Dm // N_CHIPS
    ccols = 2 * D_FF // N_CHIPS
    grads = {k: reduced[k] for k in SMALL}
    grads["meta_tokens"] = lax.dynamic_slice(reduced["meta_tokens"], (0, chip * mcols), (N_META, mcols))
    grads["conv_w"] = lax.dynamic_slice(reduced["conv_w"], (0, chip * ccols), (3, ccols))[None]
    for k in BIG:
        grads[k] = g_big[k][None]

    delta, new_m, new_v = {}, {}, {}
    for k in BIG:
        shp = w[k].shape
        d, nm, nv = _adamw("adamw_" + k, w[k][0], grads[k][0], mom[k][0], var[k][0])
        delta[k], new_m[k], new_v[k] = d.reshape(shp), nm.reshape(shp), nv.reshape(shp)
    rest = SMALL + SHARDED_SMALL
    pk = [_pack({k: t[k] for k in rest}) for t in (w, grads, mom, var)]
    outs = _adamw("adamw_small", *pk)
    like = {k: w[k] for k in rest}
    for dst, o in zip((delta, new_m, new_v), outs):
        dst.update(_unpack(o, like))

    return (loss, grad_x, *[grads[k].reshape(w[k].shape) for k in WEIGHTS], *[delta[k] for k in WEIGHTS],
            *[new_m[k] for k in WEIGHTS], *[new_v[k] for k in WEIGHTS])
```

```python
import functools
import math

import jax
import jax.numpy as jnp
from jax import lax
from jax.experimental import pallas as pl
from jax.experimental.pallas import tpu as pltpu

f32 = jnp.float32
bf16 = jnp.bfloat16

D_MODEL = 1024
N_META = 16
SSM_GROUP = 16
SSM_GROUPS = 64
SSM_STATE = 64
SLAB_GROUPS = 8
N_SLAB = SSM_GROUPS // SLAB_GROUPS
SLAB_CH = SLAB_GROUPS * SSM_GROUP
SLAB_NS = SLAB_GROUPS * SSM_STATE
HEADS = 8
HEAD_DIM = 128
CHUNK = 16
D_FF = 2816
IN_COLS = 7168
EPS = 1e-6
SUBLANES = 8
LANES = 128
N_CHIPS = 4
N_DEV = 8
ADAM_LR, ADAM_B1, ADAM_B2, ADAM_EPS, ADAM_WD, ADAM_STEP = 0.001, 0.9, 0.999, 1e-08, 0.01, 10
MESH = pl.DeviceIdType.MESH
ANY = pl.BlockSpec(memory_space=pl.ANY)

SEG_Q, SEG_F, SEG_I, SEG_OG, SEG_GA, SEG_GB, SEG_U = range(7)
N_SEG = 7


def _tile(n, target, mult=SUBLANES):
    best = None
    for d in range(mult, min(n, target) + 1, mult):
        if n % d == 0:
            best = d
    return n if best is None else best


def _params(*sem):
    return pltpu.CompilerParams(dimension_semantics=sem)


def _sigmoid(x):
    return 1.0 / (1.0 + jnp.exp(-x))


_DIMS = {"nn": (((1,), (0,)), ((), ())), "nt": (((1,), (1,)), ((), ())), "tn": (((0,), (0,)), ((), ()))}


def _mm(name, a, b, dims, grid, a_spec, b_spec, out_shape, out_spec, acc_shape, res=None, res_spec=None):
    nk = grid[2]
    dn = _DIMS[dims]

    def body(*refs):
        if res is None:
            a_ref, b_ref, o_ref, acc = refs
        else:
            a_ref, b_ref, r_ref, o_ref, acc = refs
        k = pl.program_id(2)

        @pl.when(k == 0)
        def _():
            acc[...] = jnp.zeros_like(acc)

        acc[...] += lax.dot_general(a_ref[...].astype(bf16), b_ref[...].astype(bf16), dn, preferred_element_type=f32)

        @pl.when(k == nk - 1)
        def _():
            r = acc[...]
            if res is not None:
                r = r + r_ref[...]
            o_ref[...] = r.astype(o_ref.dtype)

    ins = [a, b] + ([] if res is None else [res])
    specs = [a_spec, b_spec] + ([] if res is None else [res_spec])
    return pl.pallas_call(
        body, name=name, grid=grid, in_specs=specs, out_specs=out_spec, out_shape=out_shape,
        scratch_shapes=[pltpu.VMEM(acc_shape, f32)],
        compiler_params=_params("parallel", "parallel", "arbitrary"),
    )(*ins)


def _mm_rows(name, a, w, dims, out_dtype, tn, res=None, tk=None):
    T, K = a.shape
    N = w.shape[1] if dims == "nn" else w.shape[0]
    tm = _tile(T, 1032)
    tk = K if tk is None else tk
    grid = (T // tm, N // tn, K // tk)
    a_spec = pl.BlockSpec((tm, tk), lambda i, j, k: (i, k))
    if dims == "nn":
        b_spec = pl.BlockSpec((tk, tn), lambda i, j, k: (k, j))
    else:
        b_spec = pl.BlockSpec((tn, tk), lambda i, j, k: (j, k))
    o_spec = pl.BlockSpec((tm, tn), lambda i, j, k: (i, j))
    return _mm(name, a, w, dims, grid, a_spec, b_spec, jax.ShapeDtypeStruct((T, N), out_dtype), o_spec, (tm, tn),
               res=res, res_spec=None if res is None else o_spec)


def _mm_wgrad(name, a, g, tn=None):
    T, K = a.shape
    N = g.shape[1]
    tk = _tile(T, 688)
    tn = N if tn is None else tn
    grid = (1, N // tn, T // tk)
    a_spec = pl.BlockSpec((tk, K), lambda i, j, k: (k, 0))
    g_spec = pl.BlockSpec((tk, tn), lambda i, j, k: (k, j))
    o_spec = pl.BlockSpec((K, tn), lambda i, j, k: (0, j))
    return _mm(name, a, g, "tn", grid, a_spec, g_spec, jax.ShapeDtypeStruct((K, N), f32), o_spec, (K, tn))


def _rmsnorm_fwd(name, x, g):
    T, Dm = x.shape
    tr = _tile(T, 688)

    def body(x_ref, g_ref, z_ref):
        xv = x_ref[...]
        r = lax.rsqrt(jnp.mean(xv * xv, axis=-1, keepdims=True) + EPS)
        z_ref[...] = (xv * r * g_ref[...]).astype(z_ref.dtype)

    return pl.pallas_call(
        body, name=name, grid=(T // tr,),
        in_specs=[pl.BlockSpec((tr, Dm), lambda i: (i, 0)), pl.BlockSpec((1, Dm), lambda i: (0, 0))],
        out_specs=pl.BlockSpec((tr, Dm), lambda i: (i, 0)),
        out_shape=jax.ShapeDtypeStruct((T, Dm), bf16), compiler_params=_params("parallel"),
    )(x, g)


def _rmsnorm_bwd(name, x, g, dz, dres):
    T, Dm = x.shape
    tr = _tile(T, 688)

    def body(x_ref, g_ref, dz_ref, dres_ref, dx_ref, dg_ref):
        xv = x_ref[...]
        r = lax.rsqrt(jnp.mean(xv * xv, axis=-1, keepdims=True) + EPS)
        xn = xv * r
        dzv = dz_ref[...]
        dzg = dzv * g_ref[...]
        dx_ref[...] = dres_ref[...] + r * (dzg - xn * jnp.mean(dzg * xn, axis=-1, keepdims=True))

        @pl.when(pl.program_id(0) == 0)
        def _():
            dg_ref[...] = jnp.zeros_like(dg_ref)

        dg_ref[...] += jnp.sum(dzv * xn, axis=0, keepdims=True)

    row = pl.BlockSpec((tr, Dm), lambda i: (i, 0))
    par = pl.BlockSpec((1, Dm), lambda i: (0, 0))
    return pl.pallas_call(
        body, name=name, grid=(T // tr,), in_specs=[row, par, row, row], out_specs=[row, par],
        out_shape=[jax.ShapeDtypeStruct((T, Dm), f32), jax.ShapeDtypeStruct((1, Dm), f32)],
        compiler_params=_params("arbitrary"),
    )(x, g, dz, dres)


def _glu_fwd(ya0, gl):
    T, Dm = ya0.shape
    tr = _tile(T, 688)

    def body(y_ref, g_ref, o_ref):
        o_ref[...] = (y_ref[...] * _sigmoid(g_ref[...])).astype(o_ref.dtype)

    row = pl.BlockSpec((tr, Dm), lambda i: (i, 0))
    return pl.pallas_call(body, name="glu_fwd", grid=(T // tr,), in_specs=[row, row], out_specs=row,
                          out_shape=jax.ShapeDtypeStruct((T, Dm), bf16), compiler_params=_params("parallel"))(ya0, gl)


def _glu_bwd(dya, ya0, gl):
    T, Dm = ya0.shape
    tr = _tile(T, 688)

    def body(d_ref, y_ref, g_ref, dg_ref, dy_ref):
        s = _sigmoid(g_ref[...])
        d = d_ref[...]
        dg_ref[...] = (d * y_ref[...] * s * (1.0 - s)).astype(dg_ref.dtype)
        dy_ref[...] = d * s

    row = pl.BlockSpec((tr, Dm), lambda i: (i, 0))
    return pl.pallas_call(body, name="glu_bwd", grid=(T // tr,), in_specs=[row, row, row], out_specs=[row, row],
                          out_shape=[jax.ShapeDtypeStruct((T, Dm), bf16), jax.ShapeDtypeStruct((T, Dm), f32)],
                          compiler_params=_params("parallel"))(dya, ya0, gl)


def _merge_fwd(p, pa, pb):
    T, Dm = pa.shape
    tr = _tile(T, 688)

    def body(ga_ref, gb_ref, pa_ref, pb_ref, o_ref):
        o_ref[...] = (_sigmoid(ga_ref[...]) * pa_ref[...] + _sigmoid(gb_ref[...]) * pb_ref[...]).astype(o_ref.dtype)

    row = pl.BlockSpec((tr, Dm), lambda i: (i, 0))
    return pl.pallas_call(
        body, name="merge_fwd", grid=(T // tr,),
        in_specs=[pl.BlockSpec((tr, Dm), lambda i: (i, SEG_GA)), pl.BlockSpec((tr, Dm), lambda i: (i, SEG_GB)), row, row],
        out_specs=row, out_shape=jax.ShapeDtypeStruct((T, Dm), bf16), compiler_params=_params("parallel"),
    )(p, p, pa, pb)


def _merge_bwd(dm, p, pa, pb):
    T, Dm = pa.shape
    tr = _tile(T, 688)

    def body(dm_ref, ga_ref, gb_ref, pa_ref, pb_ref, dpa_ref, dpb_ref, dp_ref):
        d = dm_ref[...]
        sa = _sigmoid(ga_ref[...])
        sb = _sigmoid(gb_ref[...])
        dpa_ref[...] = (d * sa).astype(dpa_ref.dtype)
        dpb_ref[...] = (d * sb).astype(dpb_ref.dtype)
        dp_ref[0] = (d * pa_ref[...] * sa * (1.0 - sa)).astype(dp_ref.dtype)
        dp_ref[1] = (d * pb_ref[...] * sb * (1.0 - sb)).astype(dp_ref.dtype)

    row = pl.BlockSpec((tr, Dm), lambda i: (i, 0))
    return pl.pallas_call(
        body, name="merge_bwd", grid=(T // tr,),
        in_specs=[row, pl.BlockSpec((tr, Dm), lambda i: (i, SEG_GA)), pl.BlockSpec((tr, Dm), lambda i: (i, SEG_GB)), row, row],
        out_specs=[row, row, pl.BlockSpec((2, tr, Dm), lambda i: (SEG_GA // 2, i, 0))],
        out_shape=[jax.ShapeDtypeStruct((T, Dm), bf16), jax.ShapeDtypeStruct((T, Dm), bf16),
                   jax.ShapeDtypeStruct((N_SEG, T, Dm), bf16)],
        compiler_params=_params("parallel"),
    )(dm, p, p, pa, pb)


def _final_loss(h2x, tgt, g):
    T, Dm = h2x.shape
    tr = _tile(T, 512)

    def body(h_ref, t_ref, g_ref, dh_ref, loss_ref, dg_ref):
        hv = h_ref[...]
        r = lax.rsqrt(jnp.mean(hv * hv, axis=-1, keepdims=True) + EPS)
        xn = hv * r
        gv = g_ref[...]
        err = xn * gv - t_ref[...]
        dy = err * (1.0 / Dm)
        dyg = dy * gv
        dh_ref[...] = r * (dyg - xn * jnp.mean(dyg * xn, axis=-1, keepdims=True))

        @pl.when(pl.program_id(0) == 0)
        def _():
            dg_ref[...] = jnp.zeros_like(dg_ref)
            loss_ref[...] = jnp.zeros_like(loss_ref)

        dg_ref[...] += jnp.sum(dy * xn, axis=0, keepdims=True)
        loss_ref[...] += jnp.sum(err * err) * (0.5 / Dm)

    row = pl.BlockSpec((tr, Dm), lambda i: (i, 0))
    par = pl.BlockSpec((1, Dm), lambda i: (0, 0))
    return pl.pallas_call(
        body, name="final_loss", grid=(T // tr,), in_specs=[row, row, par],
        out_specs=[row, pl.BlockSpec((1, LANES), lambda i: (0, 0)), par],
        out_shape=[jax.ShapeDtypeStruct((T, Dm), f32), jax.ShapeDtypeStruct((1, LANES), f32), jax.ShapeDtypeStruct((1, Dm), f32)],
        compiler_params=_params("arbitrary"),
    )(h2x, tgt, g)


def _meta_grad(dh0_meta):
    B = dh0_meta.shape[0]

    def body(d_ref, o_ref):
        acc = d_ref[0]
        for b in range(1, B):
            acc = acc + d_ref[b]
        o_ref[...] = acc

    return pl.pallas_call(body, name="meta_grad", out_shape=jax.ShapeDtypeStruct(dh0_meta.shape[1:], f32))(dh0_meta)


def _shift_down(x, k, row):
    return jnp.where(row >= k, pltpu.roll(x, k, 0), 0.0)


def _shift_up(x, k, row):
    n = x.shape[0]
    return jnp.where(row < n - k, pltpu.roll(x, n - k, 0), 0.0)


def _conv_fwd(up, conv_w, conv_b, B, L):
    tc = 256
    nt = D_FF // tc

    def body(xa_ref, xb_ref, wa_ref, wb_ref, ba_ref, bb_ref, o_ref):
        row = lax.broadcasted_iota(jnp.int32, (L, tc), 0)

        def conv(x_ref, w_ref, b_ref):
            x = x_ref[...]
            return (b_ref[...] + w_ref[0:1, :] * _shift_down(x, 2, row) + w_ref[1:2, :] * _shift_down(x, 1, row)
                    + w_ref[2:3, :] * x)

        a = conv(xa_ref, wa_ref, ba_ref)
        b = conv(xb_ref, wb_ref, bb_ref)
        o_ref[...] = (a * _sigmoid(a) * b).astype(o_ref.dtype)

    return pl.pallas_call(
        body, name="conv_fwd", grid=(B, nt),
        in_specs=[pl.BlockSpec((L, tc), lambda b, j: (b, j)), pl.BlockSpec((L, tc), lambda b, j: (b, j + nt)),
                  pl.BlockSpec((3, tc), lambda b, j: (0, j)), pl.BlockSpec((3, tc), lambda b, j: (0, j + nt)),
                  pl.BlockSpec((1, tc), lambda b, j: (0, j)), pl.BlockSpec((1, tc), lambda b, j: (0, j + nt))],
        out_specs=pl.BlockSpec((L, tc), lambda b, j: (b, j)),
        out_shape=jax.ShapeDtypeStruct((B * L, D_FF), bf16), compiler_params=_params("parallel", "parallel"),
    )(up, up, conv_w, conv_w, conv_b, conv_b)


def _conv_bwd(up, dff, conv_w, conv_b, B, L):
    tc = 256
    nt = D_FF // tc

    def body(xa_ref, xb_ref, d_ref, wa_ref, wb_ref, ba_ref, bb_ref, dup_ref, dw_ref):
        row = lax.broadcasted_iota(jnp.int32, (L, tc), 0)
        xs, pre = [], []
        for x_ref, w_ref, b_ref in ((xa_ref, wa_ref, ba_ref), (xb_ref, wb_ref, bb_ref)):
            x = x_ref[...]
            x1 = _shift_down(x, 1, row)
            x2 = _shift_down(x, 2, row)
            xs.append((x, x1, x2))
            pre.append(b_ref[...] + w_ref[0:1, :] * x2 + w_ref[1:2, :] * x1 + w_ref[2:3, :] * x)
        a, b = pre
        s = _sigmoid(a)
        d = d_ref[...]
        grads = (d * b * s * (1.0 + a * (1.0 - s)), d * a * s)

        @pl.when(pl.program_id(1) == 0)
        def _():
            dw_ref[...] = jnp.zeros_like(dw_ref)

        for h, (gr, (x, x1, x2), w_ref) in enumerate(zip(grads, xs, (wa_ref, wb_ref))):
            dup_ref[h] = (w_ref[2:3, :] * gr + w_ref[1:2, :] * _shift_up(gr, 1, row)
                          + w_ref[0:1, :] * _shift_up(gr, 2, row)).astype(dup_ref.dtype)
            dw_ref[h, 0:1, :] += jnp.sum(gr * x2, axis=0, keepdims=True)
            dw_ref[h, 1:2, :] += jnp.sum(gr * x1, axis=0, keepdims=True)
            dw_ref[h, 2:3, :] += jnp.sum(gr * x, axis=0, keepdims=True)
            dw_ref[h, 3:4, :] += jnp.sum(gr, axis=0, keepdims=True)

    return pl.pallas_call(
        body, name="conv_bwd", grid=(nt, B),
        in_specs=[pl.BlockSpec((L, tc), lambda j, b: (b, j)), pl.BlockSpec((L, tc), lambda j, b: (b, j + nt)),
                  pl.BlockSpec((L, tc), lambda j, b: (b, j)),
                  pl.BlockSpec((3, tc), lambda j, b: (0, j)), pl.BlockSpec((3, tc), lambda j, b: (0, j + nt)),
                  pl.BlockSpec((1, tc), lambda j, b: (0, j)), pl.BlockSpec((1, tc), lambda j, b: (0, j + nt))],
        out_specs=[pl.BlockSpec((2, L, tc), lambda j, b: (0, b, j)), pl.BlockSpec((2, SUBLANES, tc), lambda j, b: (0, 0, j))],
        out_shape=[jax.ShapeDtypeStruct((2, B * L, D_FF), bf16), jax.ShapeDtypeStruct((2, SUBLANES, D_FF), f32)],
        compiler_params=_params("parallel", "arbitrary"),
    )(up, up, dff, conv_w, conv_w, conv_b, conv_b)


GELU_C = math.sqrt(2.0 / math.pi)
GELU_A = 0.044715


def _gelu(x):
    return 0.5 * x * (1.0 + jnp.tanh(GELU_C * (x + GELU_A * x * x * x)))


def _gelu_grad(x):
    t = jnp.tanh(GELU_C * (x + GELU_A * x * x * x))
    return 0.5 * (1.0 + t) + 0.5 * x * (1.0 - t * t) * GELU_C * (1.0 + 3.0 * GELU_A * x * x)


def _cmul_add(xr, xi, ar, ai, sr, si):
    return xr + ar * sr - ai * si, xi + ar * si + ai * sr


def _s5_scan_fwd(s_ref, pw_ref, L):
    ns = SLAB_NS
    row = lax.broadcasted_iota(jnp.int32, (SUBLANES, ns), 0)
    pr = pw_ref[0, 0:SUBLANES, :]
    pi = pw_ref[1, 0:SUBLANES, :]

    def step(i, carry):
        cr, ci = carry
        r0 = pl.multiple_of(i * SUBLANES, SUBLANES)
        xr = s_ref[pl.ds(r0, SUBLANES), 0:ns]
        xi = s_ref[pl.ds(r0, SUBLANES), ns:2 * ns]
        for k in (1, 2, 4):
            xr, xi = _cmul_add(xr, xi, pr[k - 1:k, :], pi[k - 1:k, :], _shift_down(xr, k, row), _shift_down(xi, k, row))
        xr, xi = _cmul_add(xr, xi, pr, pi, cr, ci)
        s_ref[pl.ds(r0, SUBLANES), 0:ns] = xr
        s_ref[pl.ds(r0, SUBLANES), ns:2 * ns] = xi
        return xr[SUBLANES - 1:SUBLANES, :], xi[SUBLANES - 1:SUBLANES, :]

    z = jnp.zeros((1, ns), f32)
    lax.fori_loop(0, L // SUBLANES, step, (z, z))


def _s5_project_in(u_ref, bs_ref, s_ref, L, rc):
    for r in range(0, L, rc):
        s_ref[r:r + rc, :] = jnp.dot(u_ref[r:r + rc, :].astype(bf16), bs_ref[...], preferred_element_type=f32)


def _s5_fwd(p, bs, cs, pw, d_skip, B, L):
    rc = _tile(L, 344)

    def body(u_ref, bs_ref, cs_ref, pw_ref, d_ref, y_ref, s_ref):
        _s5_project_in(u_ref, bs_ref, s_ref, L, rc)
        _s5_scan_fwd(s_ref, pw_ref, L)
        for r in range(0, L, rc):
            ypre = (jnp.dot(s_ref[r:r + rc, :].astype(bf16), cs_ref[...], preferred_element_type=f32)
                    + d_ref[...] * u_ref[r:r + rc, :])
            y_ref[r:r + rc, :] = _gelu(ypre)

    ucol = SEG_U * (D_MODEL // SLAB_CH)
    return pl.pallas_call(
        body, name="s5_fwd", grid=(B, N_SLAB),
        in_specs=[pl.BlockSpec((L, SLAB_CH), lambda b, s: (b, ucol + s)),
                  pl.BlockSpec((None, SLAB_CH, 2 * SLAB_NS), lambda b, s: (s, 0, 0)),
                  pl.BlockSpec((None, 2 * SLAB_NS, SLAB_CH), lambda b, s: (s, 0, 0)),
                  pl.BlockSpec((None, 2, 2 * SUBLANES, SLAB_NS), lambda b, s: (s, 0, 0, 0)),
                  pl.BlockSpec((1, SLAB_CH), lambda b, s: (0, s))],
        out_specs=pl.BlockSpec((L, SLAB_CH), lambda b, s: (b, s)),
        out_shape=jax.ShapeDtypeStruct((B * L, D_MODEL), f32),
        scratch_shapes=[pltpu.VMEM((L, 2 * SLAB_NS), f32)],
        compiler_params=_params("parallel", "parallel"),
    )(p, bs, cs, pw, d_skip)


def _s5_bwd(p, dya0, dp, bs, cs, pw, d_skip, B, L):
    rc = _tile(L, 344)
    ns = SLAB_NS
    nt = L // SUBLANES

    def body(u_ref, dy_ref, dp_in, bs_ref, cs_ref, pw_ref, d_ref, du_ref, dbs_ref, dcs_ref, da_ref, dd_ref,
             s_ref, lam_ref, dyp_ref):
        del dp_in
        b = pl.program_id(1)

        @pl.when(b == 0)
        def _():
            dbs_ref[...] = jnp.zeros_like(dbs_ref)
            dcs_ref[...] = jnp.zeros_like(dcs_ref)
            da_ref[...] = jnp.zeros_like(da_ref)
            dd_ref[...] = jnp.zeros_like(dd_ref)

        _s5_project_in(u_ref, bs_ref, s_ref, L, rc)
        _s5_scan_fwd(s_ref, pw_ref, L)
        for r in range(0, L, rc):
            u = u_ref[r:r + rc, :]
            sb = s_ref[r:r + rc, :].astype(bf16)
            ypre = jnp.dot(sb, cs_ref[...], preferred_element_type=f32) + d_ref[...] * u
            dyp = dy_ref[r:r + rc, :] * _gelu_grad(ypre)
            dyp_ref[r:r + rc, :] = dyp
            dd_ref[...] += jnp.sum(dyp * u, axis=0, keepdims=True)
            dypb = dyp.astype(bf16)
            dcs_ref[...] += lax.dot_general(sb, dypb, _DIMS["tn"], preferred_element_type=f32)
            lam_ref[r:r + rc, :] = lax.dot_general(dypb, cs_ref[...], _DIMS["nt"], preferred_element_type=f32)

        row = lax.broadcasted_iota(jnp.int32, (SUBLANES, ns), 0)
        pr = pw_ref[0, 0:SUBLANES, :]
        pi = -pw_ref[1, 0:SUBLANES, :]
        qr = pw_ref[0, SUBLANES:2 * SUBLANES, :]
        qi = -pw_ref[1, SUBLANES:2 * SUBLANES, :]

        def step(j, carry):
            cr, ci, ar, ai = carry
            i = nt - 1 - j
            r0 = pl.multiple_of(i * SUBLANES, SUBLANES)
            xr = lam_ref[pl.ds(r0, SUBLANES), 0:ns]
            xi = lam_ref[pl.ds(r0, SUBLANES), ns:2 * ns]
            for k in (1, 2, 4):
                xr, xi = _cmul_add(xr, xi, pr[k - 1:k, :], pi[k - 1:k, :], _shift_up(xr, k, row), _shift_up(xi, k, row))
            xr, xi = _cmul_add(xr, xi, qr, qi, cr, ci)
            lam_ref[pl.ds(r0, SUBLANES), 0:ns] = xr
            lam_ref[pl.ds(r0, SUBLANES), ns:2 * ns] = xi
            rp = pl.multiple_of(jnp.maximum(i - 1, 0) * SUBLANES, SUBLANES)
            live = jnp.where(i > 0, 1.0, 0.0)
            lr_ = s_ref[pl.ds(rp + SUBLANES - 1, 1), 0:ns] * live
            li_ = s_ref[pl.ds(rp + SUBLANES - 1, 1), ns:2 * ns] * live
            spr = jnp.where(row == 0, lr_, pltpu.roll(s_ref[pl.ds(r0, SUBLANES), 0:ns], 1, 0))
            spi = jnp.where(row == 0, li_, pltpu.roll(s_ref[pl.ds(r0, SUBLANES), ns:2 * ns], 1, 0))
            ar = ar + xr * spr + xi * spi
            ai = ai + xi * spr - xr * spi
            return xr[0:1, :], xi[0:1, :], ar, ai

        z1 = jnp.zeros((1, ns), f32)
        z8 = jnp.zeros((SUBLANES, ns), f32)
        _, _, ar, ai = lax.fori_loop(0, nt, step, (z1, z1, z8, z8))
        da_ref[0:1, :] += jnp.sum(ar, axis=0, keepdims=True)
        da_ref[1:2, :] += jnp.sum(ai, axis=0, keepdims=True)

        for r in range(0, L, rc):
            lamb = lam_ref[r:r + rc, :].astype(bf16)
            dbs_ref[...] += lax.dot_general(u_ref[r:r + rc, :].astype(bf16), lamb, _DIMS["tn"], preferred_element_type=f32)
            du = (lax.dot_general(lamb, bs_ref[...], _DIMS["nt"], preferred_element_type=f32)
                  + d_ref[...] * dyp_ref[r:r + rc, :])
            du_ref[r:r + rc, :] = du.astype(du_ref.dtype)

    ucol = SEG_U * (D_MODEL // SLAB_CH)
    T = B * L
    return pl.pallas_call(
        body, name="s5_bwd", grid=(N_SLAB, B),
        in_specs=[pl.BlockSpec((L, SLAB_CH), lambda s, b: (b, ucol + s)),
                  pl.BlockSpec((L, SLAB_CH), lambda s, b: (b, s)),
                  ANY,
                  pl.BlockSpec((None, SLAB_CH, 2 * SLAB_NS), lambda s, b: (s, 0, 0)),
                  pl.BlockSpec((None, 2 * SLAB_NS, SLAB_CH), lambda s, b: (s, 0, 0)),
                  pl.BlockSpec((None, 2, 2 * SUBLANES, SLAB_NS), lambda s, b: (s, 0, 0, 0)),
                  pl.BlockSpec((1, SLAB_CH), lambda s, b: (0, s))],
        out_specs=[pl.BlockSpec((None, L, SLAB_CH), lambda s, b: (SEG_U, b, s)),
                   pl.BlockSpec((None, SLAB_CH, 2 * SLAB_NS), lambda s, b: (s, 0, 0)),
                   pl.BlockSpec((None, 2 * SLAB_NS, SLAB_CH), lambda s, b: (s, 0, 0)),
                   pl.BlockSpec((None, 2, SLAB_NS), lambda s, b: (s, 0, 0)),
                   pl.BlockSpec((1, SLAB_CH), lambda s, b: (0, s))],
        out_shape=[jax.ShapeDtypeStruct((N_SEG, T, D_MODEL), bf16),
                   jax.ShapeDtypeStruct((N_SLAB, SLAB_CH, 2 * SLAB_NS), f32),
                   jax.ShapeDtypeStruct((N_SLAB, 2 * SLAB_NS, SLAB_CH), f32),
                   jax.ShapeDtypeStruct((N_SLAB, 2, SLAB_NS), f32),
                   jax.ShapeDtypeStruct((1, D_MODEL), f32)],
        scratch_shapes=[pltpu.VMEM((L, 2 * SLAB_NS), f32), pltpu.VMEM((L, 2 * SLAB_NS), f32), pltpu.VMEM((L, SLAB_CH), f32)],
        input_output_aliases={2: 0},
        compiler_params=_params("parallel", "arbitrary"),
    )(p, dya0, dp, bs, cs, pw, d_skip)


def _rows8(i):
    return pl.ds(pl.multiple_of(i * SUBLANES, SUBLANES), SUBLANES)


def _repeat_loop(n, step, init):
    rep = max(u for u in (6, 4, 3, 2, 1) if n % u == 0)

    def body(t, carry):
        for u in range(rep):
            carry = step(t * rep + u, carry)
        return carry

    return lax.fori_loop(0, n // rep, body, init)


def _to_segments(src_ref, dst_ref, seg):
    def step(i, c):
        dst_ref[_rows8(i), :] = src_ref[pl.ds(i, SUBLANES, stride=seg), :]
        return c

    _repeat_loop(seg, step, 0)


def _from_segments(src_ref, dst_ref, seg):
    def step(i, c):
        dst_ref[pl.ds(i, SUBLANES, stride=seg), :] = src_ref[_rows8(i), :]
        return c

    _repeat_loop(seg, step, 0)


def _seg_local_scan(s_ref, ar, ai, seg, reverse):
    ns = SLAB_NS

    def step(j, carry):
        cr, ci = carry
        rows = _rows8(seg - 1 - j if reverse else j)
        cr, ci = _cmul_add(s_ref[rows, 0:ns], s_ref[rows, ns:2 * ns], ar, ai, cr, ci)
        s_ref[rows, 0:ns] = cr
        s_ref[rows, ns:2 * ns] = ci
        return cr, ci

    z = jnp.zeros((SUBLANES, ns), f32)
    return _repeat_loop(seg, step, (z, z))


def _seg_boundaries(fr, fi, alr, ali, reverse):
    row = lax.broadcasted_iota(jnp.int32, fr.shape, 0)
    br = jnp.zeros_like(fr)
    bi = jnp.zeros_like(fi)
    for r in (range(SUBLANES - 2, -1, -1) if reverse else range(1, SUBLANES)):
        s = r + 1 if reverse else r - 1
        nr, ni = _cmul_add(fr[s:s + 1, :], fi[s:s + 1, :], alr, ali, br[s:s + 1, :], bi[s:s + 1, :])
        br = jnp.where(row == r, nr, br)
        bi = jnp.where(row == r, ni, bi)
    return br, bi


def _s5_states(u_ref, bs_ref, pw_ref, up_ref, s_ref, L, rc):
    seg = L // SUBLANES
    ns = SLAB_NS
    _to_segments(u_ref, up_ref, seg)
    _s5_project_in(up_ref, bs_ref, s_ref, L, rc)
    ar, ai = pw_ref[0, 0:1, :], pw_ref[1, 0:1, :]
    fr, fi = _seg_local_scan(s_ref, ar, ai, seg, False)
    br, bi = _seg_boundaries(fr, fi, pw_ref[0, seg - 1:seg, :], pw_ref[1, seg - 1:seg, :], False)

    def fix(i, c):
        rows = _rows8(i)
        xr, xi = _cmul_add(s_ref[rows, 0:ns], s_ref[rows, ns:2 * ns], pw_ref[0, pl.ds(i, 1), :], pw_ref[1, pl.ds(i, 1), :], br, bi)
        s_ref[rows, 0:ns] = xr
        s_ref[rows, ns:2 * ns] = xi
        return c

    _repeat_loop(seg, fix, 0)


def _pw_spec(seg_rows, order):
    if order == "bs":
        return pl.BlockSpec((None, 2, seg_rows, SLAB_NS), lambda b, s: (s, 0, 0, 0))
    return pl.BlockSpec((None, 2, seg_rows, SLAB_NS), lambda s, b: (s, 0, 0, 0))


def _s5_fwd(p, bs, cs, pw, d_skip, B, L):
    rc = _tile(L, 344)
    seg = L // SUBLANES

    def body(u_ref, bs_ref, cs_ref, pw_ref, d_ref, y_ref, s_ref, up_ref, yp_ref):
        _s5_states(u_ref, bs_ref, pw_ref, up_ref, s_ref, L, rc)
        for r in range(0, L, rc):
            ypre = (jnp.dot(s_ref[r:r + rc, :].astype(bf16), cs_ref[...], preferred_element_type=f32)
                    + d_ref[...] * up_ref[r:r + rc, :])
            yp_ref[r:r + rc, :] = _gelu(ypre)
        _from_segments(yp_ref, y_ref, seg)

    ucol = SEG_U * (D_MODEL // SLAB_CH)
    return pl.pallas_call(
        body, name="s5_fwd", grid=(B, N_SLAB),
        in_specs=[pl.BlockSpec((L, SLAB_CH), lambda b, s: (b, ucol + s)),
                  pl.BlockSpec((None, SLAB_CH, 2 * SLAB_NS), lambda b, s: (s, 0, 0)),
                  pl.BlockSpec((None, 2 * SLAB_NS, SLAB_CH), lambda b, s: (s, 0, 0)),
                  _pw_spec(pw.shape[2], "bs"),
                  pl.BlockSpec((1, SLAB_CH), lambda b, s: (0, s))],
        out_specs=pl.BlockSpec((L, SLAB_CH), lambda b, s: (b, s)),
        out_shape=jax.ShapeDtypeStruct((B * L, D_MODEL), f32),
        scratch_shapes=[pltpu.VMEM((L, 2 * SLAB_NS), f32), pltpu.VMEM((L, SLAB_CH), f32), pltpu.VMEM((L, SLAB_CH), f32)],
        compiler_params=_params("parallel", "parallel"),
    )(p, bs, cs, pw, d_skip)


def _s5_bwd(p, dya0, dp, bs, cs, pw, d_skip, B, L):
    rc = _tile(L, 344)
    ns = SLAB_NS
    seg = L // SUBLANES

    def body(u_ref, dy_ref, dp_in, bs_ref, cs_ref, pw_ref, d_ref, du_ref, dbs_ref, dcs_ref, da_ref, dd_ref,
             s_ref, lam_ref, up_ref, dyp_ref, nat_ref):
        del dp_in

        @pl.when(pl.program_id(1) == 0)
        def _():
            dbs_ref[...] = jnp.zeros_like(dbs_ref)
            dcs_ref[...] = jnp.zeros_like(dcs_ref)
            da_ref[...] = jnp.zeros_like(da_ref)
            dd_ref[...] = jnp.zeros_like(dd_ref)

        _s5_states(u_ref, bs_ref, pw_ref, up_ref, s_ref, L, rc)
        _to_segments(dy_ref, dyp_ref, seg)
        for r in range(0, L, rc):
            u = up_ref[r:r + rc, :]
            sb = s_ref[r:r + rc, :].astype(bf16)
            ypre = jnp.dot(sb, cs_ref[...], preferred_element_type=f32) + d_ref[...] * u
            dyp = dyp_ref[r:r + rc, :] * _gelu_grad(ypre)
            dyp_ref[r:r + rc, :] = dyp
            dd_ref[...] += jnp.sum(dyp * u, axis=0, keepdims=True)
            dypb = dyp.astype(bf16)
            dcs_ref[...] += lax.dot_general(sb, dypb, _DIMS["tn"], preferred_element_type=f32)
            lam_ref[r:r + rc, :] = lax.dot_general(dypb, cs_ref[...], _DIMS["nt"], preferred_element_type=f32)

        ar, ai = pw_ref[0, 0:1, :], -pw_ref[1, 0:1, :]
        fr, fi = _seg_local_scan(lam_ref, ar, ai, seg, True)
        br, bi = _seg_boundaries(fr, fi, pw_ref[0, seg - 1:seg, :], -pw_ref[1, seg - 1:seg, :], True)

        def fix(i, acc):
            accr, acci = acc
            rows = _rows8(i)
            k = seg - 1 - i
            xr, xi = _cmul_add(lam_ref[rows, 0:ns], lam_ref[rows, ns:2 * ns], pw_ref[0, pl.ds(k, 1), :],
                               -pw_ref[1, pl.ds(k, 1), :], br, bi)
            lam_ref[rows, 0:ns] = xr
            lam_ref[rows, ns:2 * ns] = xi
            prev = _rows8(jnp.maximum(i - 1, 0))
            live = jnp.where(i > 0, 1.0, 0.0)
            spr = s_ref[prev, 0:ns] * live
            spi = s_ref[prev, ns:2 * ns] * live
            return accr + xr * spr + xi * spi, acci + xi * spr - xr * spi

        z = jnp.zeros((SUBLANES, ns), f32)
        accr, acci = _repeat_loop(seg, fix, (z, z))
        row = lax.broadcasted_iota(jnp.int32, (SUBLANES, ns), 0)
        last = _rows8(seg - 1)
        spr = jnp.where(row == 0, 0.0, pltpu.roll(s_ref[last, 0:ns], 1, 0))
        spi = jnp.where(row == 0, 0.0, pltpu.roll(s_ref[last, ns:2 * ns], 1, 0))
        xr, xi = lam_ref[0:SUBLANES, 0:ns], lam_ref[0:SUBLANES, ns:2 * ns]
        accr = accr + xr * spr + xi * spi
        acci = acci + xi * spr - xr * spi
        da_ref[0:1, :] += jnp.sum(accr, axis=0, keepdims=True)
        da_ref[1:2, :] += jnp.sum(acci, axis=0, keepdims=True)

        for r in range(0, L, rc):
            lamb = lam_ref[r:r + rc, :].astype(bf16)
            dbs_ref[...] += lax.dot_general(up_ref[r:r + rc, :].astype(bf16), lamb, _DIMS["tn"], preferred_element_type=f32)
            nat_ref[r:r + rc, :] = (lax.dot_general(lamb, bs_ref[...], _DIMS["nt"], preferred_element_type=f32)
                                    + d_ref[...] * dyp_ref[r:r + rc, :])
        _from_segments(nat_ref, up_ref, seg)
        du_ref[...] = up_ref[...].astype(du_ref.dtype)

    ucol = SEG_U * (D_MODEL // SLAB_CH)
    T = B * L
    col = pltpu.VMEM((L, SLAB_CH), f32)
    return pl.pallas_call(
        body, name="s5_bwd", grid=(N_SLAB, B),
        in_specs=[pl.BlockSpec((L, SLAB_CH), lambda s, b: (b, ucol + s)),
                  pl.BlockSpec((L, SLAB_CH), lambda s, b: (b, s)),
                  ANY,
                  pl.BlockSpec((None, SLAB_CH, 2 * SLAB_NS), lambda s, b: (s, 0, 0)),
                  pl.BlockSpec((None, 2 * SLAB_NS, SLAB_CH), lambda s, b: (s, 0, 0)),
                  _pw_spec(pw.shape[2], "sb"),
                  pl.BlockSpec((1, SLAB_CH), lambda s, b: (0, s))],
        out_specs=[pl.BlockSpec((None, L, SLAB_CH), lambda s, b: (SEG_U, b, s)),
                   pl.BlockSpec((None, SLAB_CH, 2 * SLAB_NS), lambda s, b: (s, 0, 0)),
                   pl.BlockSpec((None, 2 * SLAB_NS, SLAB_CH), lambda s, b: (s, 0, 0)),
                   pl.BlockSpec((None, 2, SLAB_NS), lambda s, b: (s, 0, 0)),
                   pl.BlockSpec((1, SLAB_CH), lambda s, b: (0, s))],
        out_shape=[jax.ShapeDtypeStruct((N_SEG, T, D_MODEL), bf16),
                   jax.ShapeDtypeStruct((N_SLAB, SLAB_CH, 2 * SLAB_NS), f32),
                   jax.ShapeDtypeStruct((N_SLAB, 2 * SLAB_NS, SLAB_CH), f32),
                   jax.ShapeDtypeStruct((N_SLAB, 2, SLAB_NS), f32),
                   jax.ShapeDtypeStruct((1, D_MODEL), f32)],
        scratch_shapes=[pltpu.VMEM((L, 2 * SLAB_NS), f32), pltpu.VMEM((L, 2 * SLAB_NS), f32), col, col, col],
        input_output_aliases={2: 0},
        compiler_params=_params("parallel", "arbitrary"),
    )(p, dya0, dp, bs, cs, pw, d_skip)


def _dotb(a, b, dims="nn"):
    return lax.dot_general(a.astype(bf16), b.astype(bf16), _DIMS[dims], preferred_element_type=f32)


def _chunk_cumsum(x, pos):
    k = 1
    while k < CHUNK:
        x = x + jnp.where(pos >= k, pltpu.roll(x, k, 0), 0.0)
        k *= 2
    return x


def _chunk_rev_cumsum(x, pos):
    n = x.shape[0]
    k = 1
    while k < CHUNK:
        x = x + jnp.where(pos < CHUNK - k, pltpu.roll(x, n - k, 0), 0.0)
        k *= 2
    return x


def _hgrn_local(q, fl, lb, pos):
    sg = _sigmoid(fl)
    f = lb + (1.0 - lb) * sg
    g = jnp.log(f)
    cum = _chunk_cumsum(g, pos)
    rest = _chunk_rev_cumsum(g, pos) - g
    e = jnp.exp(cum)
    em = jnp.exp(-cum)
    eo = jnp.exp(rest)
    k = 1.0 - f
    return sg, f, e, em, eo, q * e, k * em, k * eo, jnp.exp(cum + rest)


def _hgrn_block_mask(n):
    r = lax.broadcasted_iota(jnp.int32, (n, n), 0)
    c = lax.broadcasted_iota(jnp.int32, (n, n), 1)
    return ((r & -CHUNK) == (c & -CHUNK)) & (c <= r)


def _chunk_pos(n):
    return lax.broadcasted_iota(jnp.int32, (n, HEAD_DIM), 0) & (CHUNK - 1)


def _hgrn_block_rows(L):
    return _tile(L, 688, CHUNK)


def _chunk_rows(c):
    return pl.ds(pl.multiple_of(c * CHUNK, CHUNK), CHUNK)


def _chunk_loop(nc, step):
    rep = max(u for u in range(1, 49) if nc % u == 0)

    def body(i, carry):
        for u in range(rep):
            step(i * rep + u)
        return carry

    lax.fori_loop(0, nc // rep, body, 0)


def _hgrn_specs(L, order):
    hb = D_MODEL // HEAD_DIM

    def spec(seg):
        if order == "bh":
            return pl.BlockSpec((L, HEAD_DIM), lambda b, h: (b, seg * hb + h))
        return pl.BlockSpec((L, HEAD_DIM), lambda h, b: (b, seg * hb + h))

    return [spec(SEG_Q), spec(SEG_F), spec(SEG_I), spec(SEG_OG)]


def _hgrn_fwd(p, lb, norm_g, B, L):
    nc = L // CHUNK

    rb = _hgrn_block_rows(L)

    def body(q_ref, f_ref, v_ref, og_ref, lb_ref, ng_ref, y_ref, qt_s, ko_s, vb_s, dec_s, o_s, u_s, sb_s):
        lbv = lb_ref[...]
        ngv = ng_ref[...]
        mask = _hgrn_block_mask(rb)
        pos = _chunk_pos(rb)

        for r in range(0, L, rb):
            rows = slice(r, r + rb)
            _, _, _, _, _, qt, kt, ko, dec = _hgrn_local(q_ref[rows, :], f_ref[rows, :], lbv, pos)
            vb = v_ref[rows, :].astype(bf16)
            qtb = qt.astype(bf16)
            pm = jnp.where(mask, _dotb(qtb, kt, "nt"), 0.0)
            o_s[rows, :] = _dotb(pm, vb)
            qt_s[rows, :] = qtb
            ko_s[rows, :] = ko.astype(bf16)
            vb_s[rows, :] = vb
            dec_s[rows, :] = dec

        def update(c):
            rows = _chunk_rows(c)
            u_s[c] = _dotb(vb_s[rows, :], ko_s[rows, :], "tn")

        def chain(c, st):
            sb_s[c] = st.astype(bf16)
            return st * dec_s[_chunk_rows(c), :][0:1, :] + u_s[c]

        def attend(c):
            rows = _chunk_rows(c)
            o_s[rows, :] += _dotb(qt_s[rows, :], sb_s[c], "nt")

        _chunk_loop(nc, update)
        lax.fori_loop(0, nc, chain, jnp.zeros((HEAD_DIM, HEAD_DIM), f32))
        _chunk_loop(nc, attend)

        for r in range(0, L, rb):
            rows = slice(r, r + rb)
            o = o_s[rows, :]
            og = og_ref[rows, :]
            on = o * lax.rsqrt(jnp.mean(o * o, axis=-1, keepdims=True) + EPS) * ngv
            y_ref[rows, :] = (on * og * _sigmoid(og)).astype(y_ref.dtype)

    return pl.pallas_call(
        body, name="hgrn_fwd", grid=(B, HEADS),
        in_specs=_hgrn_specs(L, "bh") + [pl.BlockSpec((1, HEAD_DIM), lambda b, h: (0, h)),
                                          pl.BlockSpec((1, HEAD_DIM), lambda b, h: (0, 0))],
        out_specs=pl.BlockSpec((L, HEAD_DIM), lambda b, h: (b, h)),
        out_shape=jax.ShapeDtypeStruct((B * L, D_MODEL), bf16),
        scratch_shapes=[pltpu.VMEM((L, HEAD_DIM), bf16), pltpu.VMEM((L, HEAD_DIM), bf16), pltpu.VMEM((L, HEAD_DIM), bf16),
                        pltpu.VMEM((L, HEAD_DIM), f32), pltpu.VMEM((L, HEAD_DIM), f32),
                        pltpu.VMEM((nc, HEAD_DIM, HEAD_DIM), f32), pltpu.VMEM((nc, HEAD_DIM, HEAD_DIM), bf16)],
        compiler_params=_params("parallel", "parallel"),
    )(p, p, p, p, lb, norm_g)


def _hgrn_bwd(p, dyb, dp, lb, norm_g, B, L):
    nc = L // CHUNK

    rb = _hgrn_block_rows(L)

    def body(q_ref, f_ref, v_ref, og_ref, dy_ref, dp_in, lb_ref, ng_ref, dseg_ref, dlb_ref, dng_ref,
             st_ref, u_s, dsb_s, qt_s, kt_s, ko_s, vb_s, do_s, dec_s, o_s, dqt_s, dkt_s, dko_s, dv_s, ddec_s):
        del dp_in
        lbv = lb_ref[...]
        ngv = ng_ref[...]
        mask = _hgrn_block_mask(rb)
        pos = _chunk_pos(rb)
        blocks = [slice(r, r + rb) for r in range(0, L, rb)]

        @pl.when(pl.program_id(1) == 0)
        def _():
            dlb_ref[...] = jnp.zeros_like(dlb_ref)

        @pl.when((pl.program_id(0) == 0) & (pl.program_id(1) == 0))
        def _():
            dng_ref[...] = jnp.zeros_like(dng_ref)

        def scores(rows):
            return jnp.where(mask, _dotb(qt_s[rows, :], kt_s[rows, :], "nt"), 0.0).astype(bf16)

        for rows in blocks:
            _, _, _, _, _, qt, kt, ko, dec = _hgrn_local(q_ref[rows, :], f_ref[rows, :], lbv, pos)
            qt_s[rows, :] = qt.astype(bf16)
            kt_s[rows, :] = kt.astype(bf16)
            ko_s[rows, :] = ko.astype(bf16)
            vb_s[rows, :] = v_ref[rows, :].astype(bf16)
            dec_s[rows, :] = dec
            o_s[rows, :] = _dotb(scores(rows), vb_s[rows, :])

        def update(c):
            rows = _chunk_rows(c)
            u_s[c] = _dotb(vb_s[rows, :], ko_s[rows, :], "tn")

        def chain(c, st):
            st_ref[c] = st
            return st * dec_s[_chunk_rows(c), :][0:1, :] + u_s[c]

        def attend(c):
            rows = _chunk_rows(c)
            o_s[rows, :] += _dotb(qt_s[rows, :], st_ref[c], "nt")

        _chunk_loop(nc, update)
        lax.fori_loop(0, nc, chain, jnp.zeros((HEAD_DIM, HEAD_DIM), f32))
        _chunk_loop(nc, attend)

        dng = jnp.zeros((1, HEAD_DIM), f32)
        for rows in blocks:
            o = o_s[rows, :]
            og = og_ref[rows, :]
            dy = dy_ref[rows, :]
            rs = lax.rsqrt(jnp.mean(o * o, axis=-1, keepdims=True) + EPS)
            xn = o * rs
            so = _sigmoid(og)
            dseg_ref[SEG_OG, rows, :] = (dy * xn * ngv * so * (1.0 + og * (1.0 - so))).astype(dseg_ref.dtype)
            don = dy * og * so
            dng = dng + jnp.sum(don * xn, axis=0, keepdims=True)
            dxo = don * ngv
            do = (rs * (dxo - xn * jnp.mean(dxo * xn, axis=-1, keepdims=True))).astype(bf16)
            do_s[rows, :] = do
            dpm = jnp.where(mask, _dotb(do, vb_s[rows, :], "nt"), 0.0).astype(bf16)
            dqt_s[rows, :] = _dotb(dpm, kt_s[rows, :])
            dkt_s[rows, :] = _dotb(dpm, qt_s[rows, :], "tn")
            dv_s[rows, :] = _dotb(scores(rows), do, "tn")
        dng_ref[...] += dng

        def rupdate(c):
            rows = _chunk_rows(c)
            u_s[c] = _dotb(do_s[rows, :], qt_s[rows, :], "tn")

        def rchain(j, dst):
            c = nc - 1 - j
            rows = _chunk_rows(c)
            dsb_s[c] = dst.astype(bf16)
            ddec_s[rows, :] = jnp.broadcast_to(jnp.sum(dst * st_ref[c], axis=0, keepdims=True), (CHUNK, HEAD_DIM))
            return dst * dec_s[rows, :][0:1, :] + u_s[c]

        def rattend(c):
            rows = _chunk_rows(c)
            dst = dsb_s[c]
            dqt_s[rows, :] += _dotb(do_s[rows, :], st_ref[c])
            dv_s[rows, :] += _dotb(ko_s[rows, :], dst, "nt")
            dko_s[rows, :] = _dotb(vb_s[rows, :], dst)

        _chunk_loop(nc, rupdate)
        lax.fori_loop(0, nc, rchain, jnp.zeros((HEAD_DIM, HEAD_DIM), f32))
        _chunk_loop(nc, rattend)

        dlb = jnp.zeros((1, HEAD_DIM), f32)
        for rows in blocks:
            sg, f, e, em, eo, qt, kt, ko, dec = _hgrn_local(q_ref[rows, :], f_ref[rows, :], lbv, pos)
            dqt = dqt_s[rows, :]
            dkt = dkt_s[rows, :]
            dko = dko_s[rows, :]
            dko_ko = dko * ko
            dcum = dqt * qt - dkt * kt - dko_ko
            chunk_tot = _chunk_cumsum(dko_ko, pos) + _chunk_rev_cumsum(dko_ko, pos) - dko_ko
            dcum = dcum + jnp.where(pos == CHUNK - 1, chunk_tot + ddec_s[rows, :] * dec, 0.0)
            df = _chunk_rev_cumsum(dcum, pos) / f - (dkt * em + dko * eo)
            dlb = dlb + jnp.sum(df * (1.0 - sg), axis=0, keepdims=True)
            dseg_ref[SEG_Q, rows, :] = (dqt * e).astype(dseg_ref.dtype)
            dseg_ref[SEG_F, rows, :] = (df * (1.0 - lbv) * sg * (1.0 - sg)).astype(dseg_ref.dtype)
            dseg_ref[SEG_I, rows, :] = dv_s[rows, :].astype(dseg_ref.dtype)
        dlb_ref[...] += dlb

    T = B * L
    sb = pltpu.VMEM((L, HEAD_DIM), bf16)
    sf = pltpu.VMEM((L, HEAD_DIM), f32)
    return pl.pallas_call(
        body, name="hgrn_bwd", grid=(HEADS, B),
        in_specs=_hgrn_specs(L, "hb") + [pl.BlockSpec((L, HEAD_DIM), lambda h, b: (b, h)), ANY,
                                          pl.BlockSpec((1, HEAD_DIM), lambda h, b: (0, h)),
                                          pl.BlockSpec((1, HEAD_DIM), lambda h, b: (0, 0))],
        out_specs=[pl.BlockSpec((4, L, HEAD_DIM), lambda h, b: (0, b, h)),
                   pl.BlockSpec((1, HEAD_DIM), lambda h, b: (0, h)),
                   pl.BlockSpec((1, HEAD_DIM), lambda h, b: (0, 0))],
        out_shape=[jax.ShapeDtypeStruct((N_SEG, T, D_MODEL), bf16), jax.ShapeDtypeStruct((1, D_MODEL), f32),
                   jax.ShapeDtypeStruct((1, HEAD_DIM), f32)],
        scratch_shapes=[pltpu.VMEM((nc, HEAD_DIM, HEAD_DIM), f32), pltpu.VMEM((nc, HEAD_DIM, HEAD_DIM), f32),
                        pltpu.VMEM((nc, HEAD_DIM, HEAD_DIM), bf16), sb, sb, sb, sb, sb, sf, sf, sf, sf, sf, sf, sf],
        input_output_aliases={5: 0},
        compiler_params=_params("arbitrary", "arbitrary"),
    )(p, p, p, p, dyb, dp, lb, norm_g)


def _dz1(dp, w_in_phys):
    _, T, Dm = dp.shape
    tm = _tile(T, 1032)
    return _mm("dz1", dp, w_in_phys, "nt", (T // tm, 1, N_SEG),
               pl.BlockSpec((None, tm, Dm), lambda i, j, k: (k, i, 0)),
               pl.BlockSpec((Dm, Dm), lambda i, j, k: (0, k)),
               jax.ShapeDtypeStruct((T, Dm), f32), pl.BlockSpec((tm, Dm), lambda i, j, k: (i, 0)), (tm, Dm))


def _dw_in(z1, dp):
    _, T, Dm = dp.shape
    tn = 256
    per_seg = Dm // tn
    per_chip = IN_COLS // N_CHIPS // tn
    tk = _tile(T, 1376)

    def out_idx(i, j, k):
        logical = ((j // per_seg + 1) % N_SEG) * per_seg + j % per_seg
        return (logical // per_chip, 0, logical % per_chip)

    return _mm("dw_in", z1, dp, "tn", (1, IN_COLS // tn, T // tk),
               pl.BlockSpec((tk, Dm), lambda i, j, k: (k, 0)),
               pl.BlockSpec((None, tk, tn), lambda i, j, k: (j // per_seg, k, j % per_seg)),
               jax.ShapeDtypeStruct((N_CHIPS, Dm, IN_COLS // N_CHIPS), f32),
               pl.BlockSpec((None, Dm, tn), out_idx), (Dm, tn))


def _dz2(dup, w_up):
    _, T, _ = dup.shape
    tm = _tile(T, 1032)
    tk = D_FF // 2
    return _mm("dz2", dup, w_up, "nt", (T // tm, 1, 4),
               pl.BlockSpec((None, tm, tk), lambda i, j, k: (k // 2, i, k % 2)),
               pl.BlockSpec((D_MODEL, tk), lambda i, j, k: (0, k)),
               jax.ShapeDtypeStruct((T, D_MODEL), f32), pl.BlockSpec((tm, D_MODEL), lambda i, j, k: (i, 0)), (tm, D_MODEL))


def _dw_up(z2, dup):
    _, T, _ = dup.shape
    tn = D_FF // 2
    tk = _tile(T, 688)
    return _mm("dw_up", z2, dup, "tn", (1, N_CHIPS, T // tk),
               pl.BlockSpec((tk, D_MODEL), lambda i, j, k: (k, 0)),
               pl.BlockSpec((None, tk, tn), lambda i, j, k: (j // 2, k, j % 2)),
               jax.ShapeDtypeStruct((N_CHIPS, D_MODEL, tn), f32),
               pl.BlockSpec((None, D_MODEL, tn), lambda i, j, k: (j, 0, 0)), (D_MODEL, tn))


def _place():
    x, y, c = lax.axis_index("x"), lax.axis_index("y"), lax.axis_index("c")
    chips = [(1 - x, y), (x, 1 - y), (1 - x, 1 - y)]
    return x, y, c, chips


def _allgather_chips(arrs):
    n = len(arrs)

    def body(*refs):
        ins, outs = refs[:n], refs[n:2 * n]
        send, recv, local = refs[2 * n:]
        x, y, c, chips = _place()
        me = 2 * x + y

        def copy(a, k, slot):
            px, py = chips[k]
            return pltpu.make_async_remote_copy(src_ref=ins[a], dst_ref=outs[a].at[slot], send_sem=send.at[3 * a + k],
                                                recv_sem=recv.at[3 * a + k], device_id=(px, py, c), device_id_type=MESH)

        for a in range(n):
            pltpu.make_async_copy(ins[a], outs[a].at[me], local.at[a]).start()
            for k in range(3):
                copy(a, k, me).start()
        for a in range(n):
            for k, (px, py) in enumerate(chips):
                copy(a, k, 2 * px + py).wait_recv()
        for a in range(n):
            pltpu.make_async_copy(ins[a], outs[a].at[me], local.at[a]).wait()
            for k in range(3):
                copy(a, k, me).wait_send()

    return pl.pallas_call(
        body, name="allgather_chips", in_specs=[ANY] * n, out_specs=[ANY] * n,
        out_shape=[jax.ShapeDtypeStruct((N_CHIPS,) + a.shape, a.dtype) for a in arrs],
        scratch_shapes=[pltpu.SemaphoreType.DMA((3 * n,)), pltpu.SemaphoreType.DMA((3 * n,)), pltpu.SemaphoreType.DMA((n,))],
    )(*arrs)


def _allgather_split(arrs):
    n = len(arrs)

    def body(*refs):
        ins, outs = refs[:n], refs[n:2 * n]
        send, recv, fsend, frecv = refs[2 * n:]
        x, y, c, chips = _place()
        me = 2 * x + y

        def half(a, core):
            rh = ins[a].shape[0] // 2
            return pl.ds(core * rh, rh)

        def copy(a, k, slot):
            px, py = chips[k]
            return pltpu.make_async_remote_copy(src_ref=ins[a].at[half(a, c), :], dst_ref=outs[a].at[slot, half(a, c), :],
                                                send_sem=send.at[3 * a + k], recv_sem=recv.at[3 * a + k],
                                                device_id=(px, py, c), device_id_type=MESH)

        def forward(a, k, core):
            px, py = chips[k]
            rows = outs[a].at[2 * px + py, half(a, core), :]
            return pltpu.make_async_remote_copy(src_ref=rows, dst_ref=rows, send_sem=fsend.at[3 * a + k],
                                                recv_sem=frecv.at[3 * a + k], device_id=(x, y, 1 - c), device_id_type=MESH)

        for a in range(n):
            for k in range(3):
                copy(a, k, me).start()
        for a in range(n):
            for k, (px, py) in enumerate(chips):
                copy(a, k, 2 * px + py).wait_recv()
                forward(a, k, c).start()
        for a in range(n):
            for k in range(3):
                forward(a, k, 1 - c).wait_recv()
        for a in range(n):
            for k in range(3):
                copy(a, k, me).wait_send()
                forward(a, k, c).wait_send()

    return pl.pallas_call(
        body, name="allgather_split", in_specs=[ANY] * n, out_specs=[ANY] * n,
        out_shape=[jax.ShapeDtypeStruct((N_CHIPS,) + a.shape, a.dtype) for a in arrs],
        scratch_shapes=[pltpu.SemaphoreType.DMA((3 * n,)) for _ in range(4)],
    )(*arrs)


def _sibling_halves(parts):
    n = len(parts)

    def body(*refs):
        ins, outs = refs[:n], refs[n:2 * n]
        send, recv = refs[2 * n:]
        x, y, c, _ = _place()

        def copy(a):
            rh = ins[a].shape[1] // 2
            return pltpu.make_async_remote_copy(src_ref=ins[a].at[:, pl.ds((1 - c) * rh, rh), :], dst_ref=outs[a],
                                                send_sem=send.at[a], recv_sem=recv.at[a], device_id=(x, y, 1 - c),
                                                device_id_type=MESH)

        for a in range(n):
            copy(a).start()
        for a in range(n):
            copy(a).wait_recv()
        for a in range(n):
            copy(a).wait_send()

    return pl.pallas_call(
        body, name="sibling_halves", in_specs=[ANY] * n, out_specs=[ANY] * n,
        out_shape=[jax.ShapeDtypeStruct((a.shape[0], a.shape[1] // 2, a.shape[2]), a.dtype) for a in parts],
        scratch_shapes=[pltpu.SemaphoreType.DMA((n,)), pltpu.SemaphoreType.DMA((n,))],
    )(*parts)


def _add_own_half(name, part, got, core):
    nchip, R, C = part.shape
    rh = R // 2
    tr = _tile(rh, 256, 2 * SUBLANES)
    nt = rh // tr

    def body(core_ref, a_ref, b_ref, o_ref):
        del core_ref
        o_ref[...] = (a_ref[...] + b_ref[...]).astype(o_ref.dtype)

    return pl.pallas_call(
        body, name=name,
        grid_spec=pltpu.PrefetchScalarGridSpec(
            num_scalar_prefetch=1, grid=(nchip, nt),
            in_specs=[pl.BlockSpec((None, tr, C), lambda j, i, core_ref: (j, core_ref[0] * nt + i, 0)),
                      pl.BlockSpec((None, tr, C), lambda j, i, core_ref: (j, i, 0))],
            out_specs=pl.BlockSpec((None, tr, C), lambda j, i, core_ref: (j, i, 0))),
        out_shape=jax.ShapeDtypeStruct((nchip, rh, C), bf16), compiler_params=_params("parallel", "parallel"),
    )(core, part, got)


def _chip_exchange(sums):
    n = len(sums)

    def body(*refs):
        ins, outs = refs[:n], refs[n:2 * n]
        send, recv = refs[2 * n:]
        x, y, c, chips = _place()
        me = 2 * x + y

        def copy(a, k, slot):
            px, py = chips[k]
            return pltpu.make_async_remote_copy(src_ref=ins[a].at[2 * px + py], dst_ref=outs[a].at[slot], send_sem=send.at[3 * a + k],
                                                recv_sem=recv.at[3 * a + k], device_id=(px, py, c), device_id_type=MESH)

        for a in range(n):
            for k in range(3):
                copy(a, k, me).start()
        for a in range(n):
            for k, (px, py) in enumerate(chips):
                copy(a, k, 2 * px + py).wait_recv()
        for a in range(n):
            for k in range(3):
                copy(a, k, me).wait_send()

    return pl.pallas_call(
        body, name="chip_exchange", in_specs=[ANY] * n, out_specs=[ANY] * n,
        out_shape=[jax.ShapeDtypeStruct(a.shape, a.dtype) for a in sums],
        scratch_shapes=[pltpu.SemaphoreType.DMA((3 * n,)), pltpu.SemaphoreType.DMA((3 * n,))],
    )(*sums)


def _sum_chips(name, slots, sums, where):
    nchip, rh, C = slots.shape
    tr = _tile(rh, 256, 2 * SUBLANES)
    nt = rh // tr

    def body(where_ref, own_ref, s1_ref, s2_ref, s3_ref, o_ref):
        me = where_ref[0]
        by_dist = [r[...].astype(f32) for r in (own_ref, s1_ref, s2_ref, s3_ref)]
        acc = None
        for j in range(nchip):
            d = me ^ j
            term = jnp.where(d == 0, by_dist[0], jnp.where(d == 1, by_dist[1], jnp.where(d == 2, by_dist[2], by_dist[3])))
            acc = term if acc is None else acc + term
        o_ref[...] = acc

    def other(d):
        return pl.BlockSpec((None, tr, C), lambda i, w: (w[0] ^ d, i, 0))

    return pl.pallas_call(
        body, name=name,
        grid_spec=pltpu.PrefetchScalarGridSpec(
            num_scalar_prefetch=1, grid=(nt,),
            in_specs=[other(0), other(1), other(2), other(3)],
            out_specs=pl.BlockSpec((tr, C), lambda i, w: (w[1] * nt + i, 0))),
        out_shape=jax.ShapeDtypeStruct((2 * rh, C), f32), compiler_params=_params("parallel"),
    )(where, sums, slots, slots, slots)


def _sum_slots(name, slots):
    ns, R, C = slots.shape
    tr = _tile(R, 256)

    def body(s_ref, o_ref):
        acc = s_ref[0]
        for j in range(1, ns):
            acc = acc + s_ref[j]
        o_ref[...] = acc

    return pl.pallas_call(
        body, name=name, grid=(R // tr,), in_specs=[pl.BlockSpec((ns, tr, C), lambda i: (0, i, 0))],
        out_specs=pl.BlockSpec((tr, C), lambda i: (i, 0)), out_shape=jax.ShapeDtypeStruct((R, C), f32),
        compiler_params=_params("parallel"),
    )(slots)


def _sibling_join(fulls):
    n = len(fulls)

    def body(*refs):
        ins, outs = refs[:n], refs[n:2 * n]
        send, recv = refs[2 * n:]
        x, y, c, _ = _place()

        def copy(a, core):
            rh = ins[a].shape[0] // 2
            rows = pl.ds(core * rh, rh)
            return pltpu.make_async_remote_copy(src_ref=ins[a].at[rows, :], dst_ref=outs[a].at[rows, :], send_sem=send.at[a],
                                                recv_sem=recv.at[a], device_id=(x, y, 1 - c), device_id_type=MESH)

        for a in range(n):
            copy(a, c).start()
        for a in range(n):
            copy(a, 1 - c).wait_recv()
        for a in range(n):
            copy(a, c).wait_send()

    return pl.pallas_call(
        body, name="sibling_join", in_specs=[ANY] * n, out_specs=[ANY] * n,
        out_shape=[jax.ShapeDtypeStruct(a.shape, a.dtype) for a in fulls],
        scratch_shapes=[pltpu.SemaphoreType.DMA((n,)), pltpu.SemaphoreType.DMA((n,))],
        input_output_aliases={a: a for a in range(n)},
    )(*fulls)


def _allgather_devices(v):
    def body(v_ref, out_ref, send, recv, local):
        x, y, c, _ = _place()
        me = 4 * x + 2 * y + c

        def peer(k):
            return (1 - x if k & 4 else x, 1 - y if k & 2 else y, 1 - c if k & 1 else c)

        def copy(k, slot):
            return pltpu.make_async_remote_copy(src_ref=v_ref, dst_ref=out_ref.at[slot], send_sem=send.at[k - 1],
                                                recv_sem=recv.at[k - 1], device_id=peer(k), device_id_type=MESH)

        own = pltpu.make_async_copy(v_ref, out_ref.at[me], local)
        own.start()
        for k in range(1, N_DEV):
            copy(k, me).start()
        for k in range(1, N_DEV):
            px, py, pc = peer(k)
            copy(k, 4 * px + 2 * py + pc).wait_recv()
        own.wait()
        for k in range(1, N_DEV):
            copy(k, me).wait_send()

    return pl.pallas_call(
        body, name="allgather_devices", in_specs=[ANY], out_specs=ANY,
        out_shape=jax.ShapeDtypeStruct((N_DEV,) + v.shape, v.dtype),
        scratch_shapes=[pltpu.SemaphoreType.DMA((N_DEV - 1,)), pltpu.SemaphoreType.DMA((N_DEV - 1,)), pltpu.SemaphoreType.DMA],
    )(v)


def _adamw(name, w, g, m, v):
    R, C = w.shape
    tr = _tile(R, 256)
    c1 = 1.0 / (1.0 - ADAM_B1 ** ADAM_STEP)
    c2 = 1.0 / (1.0 - ADAM_B2 ** ADAM_STEP)

    def body(w_ref, g_ref, m_ref, v_ref, d_ref, nm_ref, nv_ref):
        gv = g_ref[...]
        nm = ADAM_B1 * m_ref[...] + (1.0 - ADAM_B1) * gv
        nv = ADAM_B2 * v_ref[...] + (1.0 - ADAM_B2) * gv * gv
        d_ref[...] = -ADAM_LR * ((nm * c1) / (jnp.sqrt(nv * c2) + ADAM_EPS) + ADAM_WD * w_ref[...])
        nm_ref[...] = nm
        nv_ref[...] = nv

    row = pl.BlockSpec((tr, C), lambda i: (i, 0))
    sh = jax.ShapeDtypeStruct((R, C), f32)
    return pl.pallas_call(body, name=name, grid=(R // tr,), in_specs=[row] * 4, out_specs=[row] * 3,
                          out_shape=[sh, sh, sh], compiler_params=_params("parallel"))(w, g, m, v)


def _zoh(lr, li, log_dt, b_re, b_im):
    dt = jnp.exp(log_dt)[:, None]
    mag = jnp.exp(lr * dt)
    ab_re = mag * jnp.cos(li * dt)
    ab_im = mag * jnp.sin(li * dt)
    den = lr * lr + li * li
    nr = ab_re - 1.0
    coef_re = (nr * lr + ab_im * li) / den
    coef_im = (ab_im * lr - nr * li) / den
    bb_re = coef_re[..., None] * b_re - coef_im[..., None] * b_im
    bb_im = coef_re[..., None] * b_im + coef_im[..., None] * b_re
    return ab_re, ab_im, bb_re, bb_im


def _s5_tables(ab_re, ab_im, bb_re, bb_im, c_re, c_im, seg):
    eye = jnp.eye(SLAB_GROUPS, dtype=f32)

    def blk_in(bb):
        return jnp.einsum("sgph,gk->sghkp", bb.reshape(N_SLAB, SLAB_GROUPS, SSM_STATE, SSM_GROUP), eye).reshape(
            N_SLAB, SLAB_CH, SLAB_NS)

    def blk_out(cc):
        return jnp.einsum("sghp,gk->skpgh", cc.reshape(N_SLAB, SLAB_GROUPS, SSM_GROUP, SSM_STATE), eye).reshape(
            N_SLAB, SLAB_NS, SLAB_CH)

    bs = jnp.concatenate([blk_in(bb_re), blk_in(bb_im)], axis=2).astype(bf16)
    cs = jnp.concatenate([blk_out(c_re), blk_out(-c_im)], axis=1).astype(bf16)
    rows = -(-seg // SUBLANES) * SUBLANES

    def cmul(x, y):
        return x[0] * y[0] - x[1] * y[1], x[0] * y[1] + x[1] * y[0]

    pr, pi = lax.associative_scan(cmul, (jnp.broadcast_to(ab_re, (seg,) + ab_re.shape),
                                         jnp.broadcast_to(ab_im, (seg,) + ab_im.shape)), axis=0)
    pw = jnp.pad(jnp.stack([pr, pi]), ((0, 0), (0, rows - seg), (0, 0), (0, 0)))
    pw = pw.reshape(2, rows, N_SLAB, SLAB_NS).transpose(2, 0, 1, 3)
    return bs, cs, pw


def _s5_table_grads(dbs, dcs, da):
    eye = jnp.eye(SLAB_GROUPS, dtype=f32)
    d6 = dbs.reshape(N_SLAB, SLAB_GROUPS, SSM_GROUP, 2, SLAB_GROUPS, SSM_STATE)
    dbb = jnp.einsum("sghrkp,gk->rsgph", d6, eye).reshape(2, SSM_GROUPS, SSM_STATE, SSM_GROUP)
    c6 = dcs.reshape(N_SLAB, 2, SLAB_GROUPS, SSM_STATE, SLAB_GROUPS, SSM_GROUP)
    dcc = jnp.einsum("srkpgh,gk->rsghp", c6, eye).reshape(2, SSM_GROUPS, SSM_GROUP, SSM_STATE)
    dab = da.transpose(1, 0, 2).reshape(2, SSM_GROUPS, SSM_STATE)
    return dab[0], dab[1], dbb[0], dbb[1], dcc[0], -dcc[1]


SMALL = ["mix_norm_g", "ssm_lambda_re", "ssm_lambda_im", "ssm_log_dt", "ssm_b_re", "ssm_b_im", "ssm_c_re", "ssm_c_im",
         "ssm_d", "hgrn_lb_logits", "hgrn_norm_g", "ffn_norm_g", "conv_b", "final_norm_g"]
SHARDED_SMALL = ["meta_tokens", "conv_w"]
BIG = ["w_in", "ssm_w_glu", "w_ssm_proj", "w_hgrn_proj", "w_out", "w_up", "w_down"]
WEIGHTS = ['meta_tokens', 'mix_norm_g', 'w_in', 'ssm_lambda_re', 'ssm_lambda_im', 'ssm_log_dt', 'ssm_b_re', 'ssm_b_im',
           'ssm_c_re', 'ssm_c_im', 'ssm_d', 'ssm_w_glu', 'w_ssm_proj', 'hgrn_lb_logits', 'hgrn_norm_g', 'w_hgrn_proj',
           'w_out', 'ffn_norm_g', 'w_up', 'conv_w', 'conv_b', 'w_down', 'final_norm_g']


def _local_grads(x, tgt, meta, w, full):
    B, S, Dm = x.shape
    L = S + N_META
    T = B * L
    h0 = jnp.concatenate([jnp.broadcast_to(meta[None], (B, N_META, Dm)), x], axis=1).reshape(T, Dm)

    lb_all = jax.nn.softmax(w["hgrn_lb_logits"], axis=0)
    lb = lb_all[0:1]
    zoh_out, zoh_vjp = jax.vjp(_zoh, w["ssm_lambda_re"][0], w["ssm_lambda_im"][0], w["ssm_log_dt"][0],
                               w["ssm_b_re"][0], w["ssm_b_im"][0])
    bs, cs, pw = _s5_tables(*zoh_out, w["ssm_c_re"][0], w["ssm_c_im"][0], L // SUBLANES)

    z1 = _rmsnorm_fwd("mix_norm", h0, w["mix_norm_g"])
    p = _mm_rows("in_proj", z1, full["w_in"], "nn", f32, 1024)
    ya0 = _s5_fwd(p, bs, cs, pw, w["ssm_d"], B, L)
    gl = _mm_rows("glu_proj", ya0, full["ssm_w_glu"], "nn", f32, 1024)
    ya = _glu_fwd(ya0, gl)
    yb = _hgrn_fwd(p, lb, w["hgrn_norm_g"], B, L)
    pa = _mm_rows("ssm_proj", ya, full["w_ssm_proj"], "nn", f32, 1024)
    pb = _mm_rows("hgrn_proj", yb, full["w_hgrn_proj"], "nn", f32, 1024)
    merged = _merge_fwd(p, pa, pb)
    h1 = _mm_rows("out_proj", merged, full["w_out"], "nn", f32, 1024, res=h0)
    z2 = _rmsnorm_fwd("ffn_norm", h1, w["ffn_norm_g"])
    up = _mm_rows("up_proj", z2, full["w_up"], "nn", f32, D_FF // 2)
    ff = _conv_fwd(up, full["conv_w"], w["conv_b"], B, L)
    h2 = _mm_rows("down_proj", ff, full["w_down"], "nn", f32, 1024, res=h1, tk=D_FF // 2)

    h2x = h2.reshape(B, L, Dm)[:, N_META:].reshape(B * S, Dm)
    dh2x, loss, d_final_g = _final_loss(h2x, tgt.reshape(B * S, Dm), w["final_norm_g"].reshape(1, Dm))
    dh2 = jnp.pad(dh2x.reshape(B, S, Dm), ((0, 0), (N_META, 0), (0, 0))).reshape(T, Dm)

    dff = _mm_rows("d_ff", dh2, full["w_down"], "nt", f32, D_FF // 2)
    g_w_down = _mm_wgrad("dw_down", ff, dh2, tn=512)
    dup, dconv = _conv_bwd(up, dff, full["conv_w"], w["conv_b"], B, L)
    dz2 = _dz2(dup, full["w_up"])
    g_w_up = _dw_up(z2, dup)
    dh1, d_ffn_g = _rmsnorm_bwd("ffn_norm_bwd", h1, w["ffn_norm_g"], dz2, dh2)

    dmerged = _mm_rows("d_merged", dh1, full["w_out"], "nt", f32, 1024)
    g_w_out = _mm_wgrad("dw_out", merged, dh1)
    dpa, dpb, dp = _merge_bwd(dmerged, p, pa, pb)
    dya = _mm_rows("d_ya", dpa, full["w_ssm_proj"], "nt", f32, 1024)
    g_w_ssm_proj = _mm_wgrad("dw_ssm_proj", ya, dpa)
    dyb = _mm_rows("d_yb", dpb, full["w_hgrn_proj"], "nt", f32, 1024)
    g_w_hgrn_proj = _mm_wgrad("dw_hgrn_proj", yb, dpb)
    dp, d_lb, d_hgrn_g = _hgrn_bwd(p, dyb, dp, lb, w["hgrn_norm_g"], B, L)
    dgl, dya0_direct = _glu_bwd(dya, ya0, gl)
    dya0 = _mm_rows("d_ya0", dgl, full["ssm_w_glu"], "nt", f32, 1024, res=dya0_direct)
    g_w_glu = _mm_wgrad("dw_glu", ya0, dgl)
    dp, dbs, dcs, da, d_skip = _s5_bwd(p, dya0, dp, bs, cs, pw, w["ssm_d"], B, L)
    dz1 = _dz1(dp, full["w_in"])
    g_w_in = _dw_in(z1, dp)
    dh0, d_mix_g = _rmsnorm_bwd("mix_norm_bwd", h0, w["mix_norm_g"], dz1, dh1)

    dh0 = dh0.reshape(B, L, Dm)
    grad_x = dh0[:, N_META:]
    d_meta = _meta_grad(dh0[:, :N_META])

    d_ab_re, d_ab_im, d_bb_re, d_bb_im, d_c_re, d_c_im = _s5_table_grads(dbs, dcs, da)
    d_lr, d_li, d_log_dt, d_b_re, d_b_im = zoh_vjp((d_ab_re, d_ab_im, d_bb_re, d_bb_im))
    sm0, sm1 = lb_all[0:1], lb_all[1:2]
    d_logits = jnp.concatenate([sm0 * (1.0 - sm0) * d_lb, -sm0 * sm1 * d_lb], axis=0)
    small = {
        "meta_tokens": d_meta, "mix_norm_g": d_mix_g, "ssm_lambda_re": d_lr[None], "ssm_lambda_im": d_li[None],
        "ssm_log_dt": d_log_dt[None], "ssm_b_re": d_b_re[None], "ssm_b_im": d_b_im[None], "ssm_c_re": d_c_re[None],
        "ssm_c_im": d_c_im[None], "ssm_d": d_skip, "hgrn_lb_logits": d_logits, "hgrn_norm_g": d_hgrn_g,
        "ffn_norm_g": d_ffn_g, "conv_w": dconv[:, 0:3, :].transpose(1, 0, 2).reshape(3, 2 * D_FF),
        "conv_b": dconv[:, 3, :].reshape(1, 2 * D_FF), "final_norm_g": d_final_g.reshape(Dm),
    }
    big = {
        "w_in": g_w_in, "ssm_w_glu": g_w_glu.reshape(N_CHIPS, Dm // N_CHIPS, Dm),
        "w_ssm_proj": g_w_ssm_proj.reshape(N_CHIPS, Dm // N_CHIPS, Dm),
        "w_hgrn_proj": g_w_hgrn_proj.reshape(N_CHIPS, Dm // N_CHIPS, Dm), "w_out": g_w_out.reshape(N_CHIPS, Dm // N_CHIPS, Dm),
        "w_up": g_w_up, "w_down": g_w_down.reshape(N_CHIPS, D_FF // N_CHIPS, Dm),
    }
    return loss, grad_x, big, small


def _pack(parts):
    flat = jnp.concatenate([parts[k].reshape(-1) for k in parts])
    n = flat.shape[0]
    rows = -(-n // (SUBLANES * LANES)) * SUBLANES
    flat = jnp.pad(flat, (0, rows * LANES - n))
    return flat.reshape(rows, LANES)


def _unpack(packed, like):
    flat = packed.reshape(-1)
    out, o = {}, 0
    for k, ref in like.items():
        n = math.prod(ref.shape)
        out[k] = flat[o:o + n].reshape(ref.shape)
        o += n
    return out


def kernel(x, meta_tokens, mix_norm_g, w_in, ssm_lambda_re, ssm_lambda_im, ssm_log_dt, ssm_b_re, ssm_b_im, ssm_c_re, ssm_c_im, ssm_d, ssm_w_glu, w_ssm_proj, hgrn_lb_logits, hgrn_norm_g, w_hgrn_proj, w_out, ffn_norm_g, w_up, conv_w, conv_b, w_down, final_norm_g, loss_target, m_meta_tokens, m_mix_norm_g, m_w_in, m_ssm_lambda_re, m_ssm_lambda_im, m_ssm_log_dt, m_ssm_b_re, m_ssm_b_im, m_ssm_c_re, m_ssm_c_im, m_ssm_d, m_ssm_w_glu, m_w_ssm_proj, m_hgrn_lb_logits, m_hgrn_norm_g, m_w_hgrn_proj, m_w_out, m_ffn_norm_g, m_w_up, m_conv_w, m_conv_b, m_w_down, m_final_norm_g, v_meta_tokens, v_mix_norm_g, v_w_in, v_ssm_lambda_re, v_ssm_lambda_im, v_ssm_log_dt, v_ssm_b_re, v_ssm_b_im, v_ssm_c_re, v_ssm_c_im, v_ssm_d, v_ssm_w_glu, v_w_ssm_proj, v_hgrn_lb_logits, v_hgrn_norm_g, v_w_hgrn_proj, v_w_out, v_ffn_norm_g, v_w_up, v_conv_w, v_conv_b, v_w_down, v_final_norm_g):
    args = dict(locals())
    w = {k: args[k] for k in WEIGHTS}
    mom = {k: args["m_" + k] for k in WEIGHTS}
    var = {k: args["v_" + k] for k in WEIGHTS}
    Dm = D_MODEL
    cx, cy, cc = lax.axis_index("x"), lax.axis_index("y"), lax.axis_index("c")
    chip = 2 * cx + cy

    shards = [w[k][0].astype(bf16) for k in BIG]
    gathered = _allgather_split(shards)
    g_in, g_glu, g_sp, g_hp, g_out, g_up, g_down = [
        lax.dynamic_update_slice(g, s[None], (chip, 0, 0)) for g, s in zip(gathered, shards)]
    g_meta, g_cw = _allgather_chips([w["meta_tokens"], w["conv_w"][0]])
    w_in_full = jnp.roll(g_in.transpose(1, 0, 2).reshape(Dm, IN_COLS), -Dm, axis=1)
    full = {
        "w_in": w_in_full, "ssm_w_glu": g_glu.reshape(Dm, Dm), "w_ssm_proj": g_sp.reshape(Dm, Dm),
        "w_hgrn_proj": g_hp.reshape(Dm, Dm), "w_out": g_out.reshape(Dm, Dm),
        "w_up": g_up.transpose(1, 0, 2).reshape(Dm, 2 * D_FF), "w_down": g_down.reshape(D_FF, Dm),
        "conv_w": g_cw.transpose(1, 0, 2).reshape(3, 2 * D_FF),
    }
    meta_full = g_meta.transpose(1, 0, 2).reshape(N_META, Dm)

    loss_part, grad_x, big, small = _local_grads(x, loss_target, meta_full, w, full)

    core = cc.reshape(1).astype(jnp.int32)
    parts = [big[k] for k in BIG]
    got = _sibling_halves(parts)
    sums = [_add_own_half("add_half_" + k, pt, gt, core) for k, pt, gt in zip(BIG, parts, got)]
    slots = _chip_exchange(sums)
    where = jnp.stack([chip, cc]).astype(jnp.int32)
    fulls = [_sum_chips("sum_chips_" + k, sl, sm, where) for k, sl, sm in zip(BIG, slots, sums)]
    g_big = dict(zip(BIG, _sibling_join(fulls)))

    small_all = dict(small)
    small_all["loss"] = loss_part[0, 0:1]
    packed = _pack(small_all)
    reduced = _unpack(_sum_slots("sum_devices", _allgather_devices(packed)), small_all)
    loss = reduced.pop("loss")[0]
    mcols = Dm // N_CHIPS
    ccols = 2 * D_FF // N_CHIPS
    grads = {k: reduced[k] for k in SMALL}
    grads["meta_tokens"] = lax.dynamic_slice(reduced["meta_tokens"], (0, chip * mcols), (N_META, mcols))
    grads["conv_w"] = lax.dynamic_slice(reduced["conv_w"], (0, chip * ccols), (3, ccols))[None]
    for k in BIG:
        grads[k] = g_big[k][None]

    delta, new_m, new_v = {}, {}, {}
    for k in BIG:
        shp = w[k].shape
        d, nm, nv = _adamw("adamw_" + k, w[k][0], grads[k][0], mom[k][0], var[k][0])
        delta[k], new_m[k], new_v[k] = d.reshape(shp), nm.reshape(shp), nv.reshape(shp)
    rest = SMALL + SHARDED_SMALL
    pk = [_pack({k: t[k] for k in rest}) for t in (w, grads, mom, var)]
    outs = _adamw("adamw_small", *pk)
    like = {k: w[k] for k in rest}
    for dst, o in zip((delta, new_m, new_v), outs):
        dst.update(_unpack(o, like))

    return (loss, grad_x, *[grads[k].reshape(w[k].shape) for k in WEIGHTS], *[delta[k] for k in WEIGHTS],
            *[new_m[k] for k in WEIGHTS], *[new_v[k] for k in WEIGHTS])
```

```python
import functools
import math

import jax
import jax.numpy as jnp
from jax import lax
from jax.experimental import pallas as pl
from jax.experimental.pallas import tpu as pltpu

f32 = jnp.float32
bf16 = jnp.bfloat16

D_MODEL = 1024
N_META = 16
SSM_GROUP = 16
SSM_GROUPS = 64
SSM_STATE = 64
SLAB_GROUPS = 8
N_SLAB = SSM_GROUPS // SLAB_GROUPS
SLAB_CH = SLAB_GROUPS * SSM_GROUP
SLAB_NS = SLAB_GROUPS * SSM_STATE
HEADS = 8
HEAD_DIM = 128
CHUNK = 16
D_FF = 2816
IN_COLS = 7168
EPS = 1e-6
SUBLANES = 8
LANES = 128
N_CHIPS = 4
N_DEV = 8
ADAM_LR, ADAM_B1, ADAM_B2, ADAM_EPS, ADAM_WD, ADAM_STEP = 0.001, 0.9, 0.999, 1e-08, 0.01, 10
MESH = pl.DeviceIdType.MESH
ANY = pl.BlockSpec(memory_space=pl.ANY)

SEG_Q, SEG_F, SEG_I, SEG_OG, SEG_GA, SEG_GB, SEG_U = range(7)
N_SEG = 7


def _tile(n, target, mult=SUBLANES):
    best = None
    for d in range(mult, min(n, target) + 1, mult):
        if n % d == 0:
            best = d
    return n if best is None else best


def _params(*sem):
    return pltpu.CompilerParams(dimension_semantics=sem)


def _sigmoid(x):
    return 1.0 / (1.0 + jnp.exp(-x))


_DIMS = {"nn": (((1,), (0,)), ((), ())), "nt": (((1,), (1,)), ((), ())), "tn": (((0,), (0,)), ((), ()))}


def _mm(name, a, b, dims, grid, a_spec, b_spec, out_shape, out_spec, acc_shape, res=None, res_spec=None):
    nk = grid[2]
    dn = _DIMS[dims]

    def body(*refs):
        if res is None:
            a_ref, b_ref, o_ref, acc = refs
        else:
            a_ref, b_ref, r_ref, o_ref, acc = refs
        k = pl.program_id(2)

        @pl.when(k == 0)
        def _():
            acc[...] = jnp.zeros_like(acc)

        acc[...] += lax.dot_general(a_ref[...].astype(bf16), b_ref[...].astype(bf16), dn, preferred_element_type=f32)

        @pl.when(k == nk - 1)
        def _():
            r = acc[...]
            if res is not None:
                r = r + r_ref[...]
            o_ref[...] = r.astype(o_ref.dtype)

    ins = [a, b] + ([] if res is None else [res])
    specs = [a_spec, b_spec] + ([] if res is None else [res_spec])
    return pl.pallas_call(
        body, name=name, grid=grid, in_specs=specs, out_specs=out_spec, out_shape=out_shape,
        scratch_shapes=[pltpu.VMEM(acc_shape, f32)],
        compiler_params=_params("parallel", "parallel", "arbitrary"),
    )(*ins)


def _mm_rows(name, a, w, dims, out_dtype, tn, res=None, tk=None):
    T, K = a.shape
    N = w.shape[1] if dims == "nn" else w.shape[0]
    tm = _tile(T, 1032)
    tk = K if tk is None else tk
    grid = (T // tm, N // tn, K // tk)
    a_spec = pl.BlockSpec((tm, tk), lambda i, j, k: (i, k))
    if dims == "nn":
        b_spec = pl.BlockSpec((tk, tn), lambda i, j, k: (k, j))
    else:
        b_spec = pl.BlockSpec((tn, tk), lambda i, j, k: (j, k))
    o_spec = pl.BlockSpec((tm, tn), lambda i, j, k: (i, j))
    return _mm(name, a, w, dims, grid, a_spec, b_spec, jax.ShapeDtypeStruct((T, N), out_dtype), o_spec, (tm, tn),
               res=res, res_spec=None if res is None else o_spec)


def _mm_wgrad(name, a, g, tn=None):
    T, K = a.shape
    N = g.shape[1]
    tk = _tile(T, 688)
    tn = N if tn is None else tn
    grid = (1, N // tn, T // tk)
    a_spec = pl.BlockSpec((tk, K), lambda i, j, k: (k, 0))
    g_spec = pl.BlockSpec((tk, tn), lambda i, j, k: (k, j))
    o_spec = pl.BlockSpec((K, tn), lambda i, j, k: (0, j))
    return _mm(name, a, g, "tn", grid, a_spec, g_spec, jax.ShapeDtypeStruct((K, N), f32), o_spec, (K, tn))


def _rmsnorm_fwd(name, x, g):
    T, Dm = x.shape
    tr = _tile(T, 688)

    def body(x_ref, g_ref, z_ref):
        xv = x_ref[...]
        r = lax.rsqrt(jnp.mean(xv * xv, axis=-1, keepdims=True) + EPS)
        z_ref[...] = (xv * r * g_ref[...]).astype(z_ref.dtype)

    return pl.pallas_call(
        body, name=name, grid=(T // tr,),
        in_specs=[pl.BlockSpec((tr, Dm), lambda i: (i, 0)), pl.BlockSpec((1, Dm), lambda i: (0, 0))],
        out_specs=pl.BlockSpec((tr, Dm), lambda i: (i, 0)),
        out_shape=jax.ShapeDtypeStruct((T, Dm), bf16), compiler_params=_params("parallel"),
    )(x, g)


def _rmsnorm_bwd(name, x, g, dz, dres):
    T, Dm = x.shape
    tr = _tile(T, 688)

    def body(x_ref, g_ref, dz_ref, dres_ref, dx_ref, dg_ref):
        xv = x_ref[...]
        r = lax.rsqrt(jnp.mean(xv * xv, axis=-1, keepdims=True) + EPS)
        xn = xv * r
        dzv = dz_ref[...]
        dzg = dzv * g_ref[...]
        dx_ref[...] = dres_ref[...] + r * (dzg - xn * jnp.mean(dzg * xn, axis=-1, keepdims=True))

        @pl.when(pl.program_id(0) == 0)
        def _():
            dg_ref[...] = jnp.zeros_like(dg_ref)

        dg_ref[...] += jnp.sum(dzv * xn, axis=0, keepdims=True)

    row = pl.BlockSpec((tr, Dm), lambda i: (i, 0))
    par = pl.BlockSpec((1, Dm), lambda i: (0, 0))
    return pl.pallas_call(
        body, name=name, grid=(T // tr,), in_specs=[row, par, row, row], out_specs=[row, par],
        out_shape=[jax.ShapeDtypeStruct((T, Dm), f32), jax.ShapeDtypeStruct((1, Dm), f32)],
        compiler_params=_params("arbitrary"),
    )(x, g, dz, dres)


def _glu_fwd(ya0, gl):
    T, Dm = ya0.shape
    tr = _tile(T, 688)

    def body(y_ref, g_ref, o_ref):
        o_ref[...] = (y_ref[...] * _sigmoid(g_ref[...])).astype(o_ref.dtype)

    row = pl.BlockSpec((tr, Dm), lambda i: (i, 0))
    return pl.pallas_call(body, name="glu_fwd", grid=(T // tr,), in_specs=[row, row], out_specs=row,
                          out_shape=jax.ShapeDtypeStruct((T, Dm), bf16), compiler_params=_params("parallel"))(ya0, gl)


def _glu_bwd(dya, ya0, gl):
    T, Dm = ya0.shape
    tr = _tile(T, 688)

    def body(d_ref, y_ref, g_ref, dg_ref, dy_ref):
        s = _sigmoid(g_ref[...])
        d = d_ref[...]
        dg_ref[...] = (d * y_ref[...] * s * (1.0 - s)).astype(dg_ref.dtype)
        dy_ref[...] = d * s

    row = pl.BlockSpec((tr, Dm), lambda i: (i, 0))
    return pl.pallas_call(body, name="glu_bwd", grid=(T // tr,), in_specs=[row, row, row], out_specs=[row, row],
                          out_shape=[jax.ShapeDtypeStruct((T, Dm), bf16), jax.ShapeDtypeStruct((T, Dm), f32)],
                          compiler_params=_params("parallel"))(dya, ya0, gl)


def _merge_fwd(p, pa, pb):
    T, Dm = pa.shape
    tr = _tile(T, 688)

    def body(ga_ref, gb_ref, pa_ref, pb_ref, o_ref):
        o_ref[...] = (_sigmoid(ga_ref[...]) * pa_ref[...] + _sigmoid(gb_ref[...]) * pb_ref[...]).astype(o_ref.dtype)

    row = pl.BlockSpec((tr, Dm), lambda i: (i, 0))
    return pl.pallas_call(
        body, name="merge_fwd", grid=(T // tr,),
        in_specs=[pl.BlockSpec((tr, Dm), lambda i: (i, SEG_GA)), pl.BlockSpec((tr, Dm), lambda i: (i, SEG_GB)), row, row],
        out_specs=row, out_shape=jax.ShapeDtypeStruct((T, Dm), bf16), compiler_params=_params("parallel"),
    )(p, p, pa, pb)


def _merge_bwd(dm, p, pa, pb):
    T, Dm = pa.shape
    tr = _tile(T, 688)

    def body(dm_ref, ga_ref, gb_ref, pa_ref, pb_ref, dpa_ref, dpb_ref, dp_ref):
        d = dm_ref[...]
        sa = _sigmoid(ga_ref[...])
        sb = _sigmoid(gb_ref[...])
        dpa_ref[...] = (d * sa).astype(dpa_ref.dtype)
        dpb_ref[...] = (d * sb).astype(dpb_ref.dtype)
        dp_ref[0] = (d * pa_ref[...] * sa * (1.0 - sa)).astype(dp_ref.dtype)
        dp_ref[1] = (d * pb_ref[...] * sb * (1.0 - sb)).astype(dp_ref.dtype)

    row = pl.BlockSpec((tr, Dm), lambda i: (i, 0))
    return pl.pallas_call(
        body, name="merge_bwd", grid=(T // tr,),
        in_specs=[row, pl.BlockSpec((tr, Dm), lambda i: (i, SEG_GA)), pl.BlockSpec((tr, Dm), lambda i: (i, SEG_GB)), row, row],
        out_specs=[row, row, pl.BlockSpec((2, tr, Dm), lambda i: (SEG_GA // 2, i, 0))],
        out_shape=[jax.ShapeDtypeStruct((T, Dm), bf16), jax.ShapeDtypeStruct((T, Dm), bf16),
                   jax.ShapeDtypeStruct((N_SEG, T, Dm), bf16)],
        compiler_params=_params("parallel"),
    )(dm, p, p, pa, pb)


def _final_loss(h2x, tgt, g):
    T, Dm = h2x.shape
    tr = _tile(T, 512)

    def body(h_ref, t_ref, g_ref, dh_ref, loss_ref, dg_ref):
        hv = h_ref[...]
        r = lax.rsqrt(jnp.mean(hv * hv, axis=-1, keepdims=True) + EPS)
        xn = hv * r
        gv = g_ref[...]
        err = xn * gv - t_ref[...]
        dy = err * (1.0 / Dm)
        dyg = dy * gv
        dh_ref[...] = r * (dyg - xn * jnp.mean(dyg * xn, axis=-1, keepdims=True))

        @pl.when(pl.program_id(0) == 0)
        def _():
            dg_ref[...] = jnp.zeros_like(dg_ref)
            loss_ref[...] = jnp.zeros_like(loss_ref)

        dg_ref[...] += jnp.sum(dy * xn, axis=0, keepdims=True)
        loss_ref[...] += jnp.sum(err * err) * (0.5 / Dm)

    row = pl.BlockSpec((tr, Dm), lambda i: (i, 0))
    par = pl.BlockSpec((1, Dm), lambda i: (0, 0))
    return pl.pallas_call(
        body, name="final_loss", grid=(T // tr,), in_specs=[row, row, par],
        out_specs=[row, pl.BlockSpec((1, LANES), lambda i: (0, 0)), par],
        out_shape=[jax.ShapeDtypeStruct((T, Dm), f32), jax.ShapeDtypeStruct((1, LANES), f32), jax.ShapeDtypeStruct((1, Dm), f32)],
        compiler_params=_params("arbitrary"),
    )(h2x, tgt, g)


def _meta_grad(dh0_meta):
    B = dh0_meta.shape[0]

    def body(d_ref, o_ref):
        acc = d_ref[0]
        for b in range(1, B):
            acc = acc + d_ref[b]
        o_ref[...] = acc

    return pl.pallas_call(body, name="meta_grad", out_shape=jax.ShapeDtypeStruct(dh0_meta.shape[1:], f32))(dh0_meta)


def _shift_down(x, k, row):
    return jnp.where(row >= k, pltpu.roll(x, k, 0), 0.0)


def _shift_up(x, k, row):
    n = x.shape[0]
    return jnp.where(row < n - k, pltpu.roll(x, n - k, 0), 0.0)


def _conv_fwd(up, conv_w, conv_b, B, L):
    tc = 256
    nt = D_FF // tc

    def body(xa_ref, xb_ref, wa_ref, wb_ref, ba_ref, bb_ref, o_ref):
        row = lax.broadcasted_iota(jnp.int32, (L, tc), 0)

        def conv(x_ref, w_ref, b_ref):
            x = x_ref[...]
            return (b_ref[...] + w_ref[0:1, :] * _shift_down(x, 2, row) + w_ref[1:2, :] * _shift_down(x, 1, row)
                    + w_ref[2:3, :] * x)

        a = conv(xa_ref, wa_ref, ba_ref)
        b = conv(xb_ref, wb_ref, bb_ref)
        o_ref[...] = (a * _sigmoid(a) * b).astype(o_ref.dtype)

    return pl.pallas_call(
        body, name="conv_fwd", grid=(B, nt),
        in_specs=[pl.BlockSpec((L, tc), lambda b, j: (b, j)), pl.BlockSpec((L, tc), lambda b, j: (b, j + nt)),
                  pl.BlockSpec((3, tc), lambda b, j: (0, j)), pl.BlockSpec((3, tc), lambda b, j: (0, j + nt)),
                  pl.BlockSpec((1, tc), lambda b, j: (0, j)), pl.BlockSpec((1, tc), lambda b, j: (0, j + nt))],
        out_specs=pl.BlockSpec((L, tc), lambda b, j: (b, j)),
        out_shape=jax.ShapeDtypeStruct((B * L, D_FF), bf16), compiler_params=_params("parallel", "parallel"),
    )(up, up, conv_w, conv_w, conv_b, conv_b)


def _conv_bwd(up, dff, conv_w, conv_b, B, L):
    tc = 256
    nt = D_FF // tc

    def body(xa_ref, xb_ref, d_ref, wa_ref, wb_ref, ba_ref, bb_ref, dup_ref, dw_ref):
        row = lax.broadcasted_iota(jnp.int32, (L, tc), 0)
        xs, pre = [], []
        for x_ref, w_ref, b_ref in ((xa_ref, wa_ref, ba_ref), (xb_ref, wb_ref, bb_ref)):
            x = x_ref[...]
            x1 = _shift_down(x, 1, row)
            x2 = _shift_down(x, 2, row)
            xs.append((x, x1, x2))
            pre.append(b_ref[...] + w_ref[0:1, :] * x2 + w_ref[1:2, :] * x1 + w_ref[2:3, :] * x)
        a, b = pre
        s = _sigmoid(a)
        d = d_ref[...]
        grads = (d * b * s * (1.0 + a * (1.0 - s)), d * a * s)

        @pl.when(pl.program_id(1) == 0)
        def _():
            dw_ref[...] = jnp.zeros_like(dw_ref)

        for h, (gr, (x, x1, x2), w_ref) in enumerate(zip(grads, xs, (wa_ref, wb_ref))):
            dup_ref[h] = (w_ref[2:3, :] * gr + w_ref[1:2, :] * _shift_up(gr, 1, row)
                          + w_ref[0:1, :] * _shift_up(gr, 2, row)).astype(dup_ref.dtype)
            dw_ref[h, 0:1, :] += jnp.sum(gr * x2, axis=0, keepdims=True)
            dw_ref[h, 1:2, :] += jnp.sum(gr * x1, axis=0, keepdims=True)
            dw_ref[h, 2:3, :] += jnp.sum(gr * x, axis=0, keepdims=True)
            dw_ref[h, 3:4, :] += jnp.sum(gr, axis=0, keepdims=True)

    return pl.pallas_call(
        body, name="conv_bwd", grid=(nt, B),
        in_specs=[pl.BlockSpec((L, tc), lambda j, b: (b, j)), pl.BlockSpec((L, tc), lambda j, b: (b, j + nt)),
                  pl.BlockSpec((L, tc), lambda j, b: (b, j)),
                  pl.BlockSpec((3, tc), lambda j, b: (0, j)), pl.BlockSpec((3, tc), lambda j, b: (0, j + nt)),
                  pl.BlockSpec((1, tc), lambda j, b: (0, j)), pl.BlockSpec((1, tc), lambda j, b: (0, j + nt))],
        out_specs=[pl.BlockSpec((2, L, tc), lambda j, b: (0, b, j)), pl.BlockSpec((2, SUBLANES, tc), lambda j, b: (0, 0, j))],
        out_shape=[jax.ShapeDtypeStruct((2, B * L, D_FF), bf16), jax.ShapeDtypeStruct((2, SUBLANES, D_FF), f32)],
        compiler_params=_params("parallel", "arbitrary"),
    )(up, up, dff, conv_w, conv_w, conv_b, conv_b)


GELU_C = math.sqrt(2.0 / math.pi)
GELU_A = 0.044715


def _gelu(x):
    return 0.5 * x * (1.0 + jnp.tanh(GELU_C * (x + GELU_A * x * x * x)))


def _gelu_grad(x):
    t = jnp.tanh(GELU_C * (x + GELU_A * x * x * x))
    return 0.5 * (1.0 + t) + 0.5 * x * (1.0 - t * t) * GELU_C * (1.0 + 3.0 * GELU_A * x * x)


def _cmul_add(xr, xi, ar, ai, sr, si):
    return xr + ar * sr - ai * si, xi + ar * si + ai * sr


def _s5_scan_fwd(s_ref, pw_ref, L):
    ns = SLAB_NS
    row = lax.broadcasted_iota(jnp.int32, (SUBLANES, ns), 0)
    pr = pw_ref[0, 0:SUBLANES, :]
    pi = pw_ref[1, 0:SUBLANES, :]

    def step(i, carry):
        cr, ci = carry
        r0 = pl.multiple_of(i * SUBLANES, SUBLANES)
        xr = s_ref[pl.ds(r0, SUBLANES), 0:ns]
        xi = s_ref[pl.ds(r0, SUBLANES), ns:2 * ns]
        for k in (1, 2, 4):
            xr, xi = _cmul_add(xr, xi, pr[k - 1:k, :], pi[k - 1:k, :], _shift_down(xr, k, row), _shift_down(xi, k, row))
        xr, xi = _cmul_add(xr, xi, pr, pi, cr, ci)
        s_ref[pl.ds(r0, SUBLANES), 0:ns] = xr
        s_ref[pl.ds(r0, SUBLANES), ns:2 * ns] = xi
        return xr[SUBLANES - 1:SUBLANES, :], xi[SUBLANES - 1:SUBLANES, :]

    z = jnp.zeros((1, ns), f32)
    lax.fori_loop(0, L // SUBLANES, step, (z, z))


def _s5_project_in(u_ref, bs_ref, s_ref, L, rc):
    for r in range(0, L, rc):
        s_ref[r:r + rc, :] = jnp.dot(u_ref[r:r + rc, :].astype(bf16), bs_ref[...], preferred_element_type=f32)


def _s5_fwd(p, bs, cs, pw, d_skip, B, L):
    rc = _tile(L, 344)

    def body(u_ref, bs_ref, cs_ref, pw_ref, d_ref, y_ref, s_ref):
        _s5_project_in(u_ref, bs_ref, s_ref, L, rc)
        _s5_scan_fwd(s_ref, pw_ref, L)
        for r in range(0, L, rc):
            ypre = (jnp.dot(s_ref[r:r + rc, :].astype(bf16), cs_ref[...], preferred_element_type=f32)
                    + d_ref[...] * u_ref[r:r + rc, :])
            y_ref[r:r + rc, :] = _gelu(ypre)

    ucol = SEG_U * (D_MODEL // SLAB_CH)
    return pl.pallas_call(
        body, name="s5_fwd", grid=(B, N_SLAB),
        in_specs=[pl.BlockSpec((L, SLAB_CH), lambda b, s: (b, ucol + s)),
                  pl.BlockSpec((None, SLAB_CH, 2 * SLAB_NS), lambda b, s: (s, 0, 0)),
                  pl.BlockSpec((None, 2 * SLAB_NS, SLAB_CH), lambda b, s: (s, 0, 0)),
                  pl.BlockSpec((None, 2, 2 * SUBLANES, SLAB_NS), lambda b, s: (s, 0, 0, 0)),
                  pl.BlockSpec((1, SLAB_CH), lambda b, s: (0, s))],
        out_specs=pl.BlockSpec((L, SLAB_CH), lambda b, s: (b, s)),
        out_shape=jax.ShapeDtypeStruct((B * L, D_MODEL), f32),
        scratch_shapes=[pltpu.VMEM((L, 2 * SLAB_NS), f32)],
        compiler_params=_params("parallel", "parallel"),
    )(p, bs, cs, pw, d_skip)


def _s5_bwd(p, dya0, dp, bs, cs, pw, d_skip, B, L):
    rc = _tile(L, 344)
    ns = SLAB_NS
    nt = L // SUBLANES

    def body(u_ref, dy_ref, dp_in, bs_ref, cs_ref, pw_ref, d_ref, du_ref, dbs_ref, dcs_ref, da_ref, dd_ref,
             s_ref, lam_ref, dyp_ref):
        del dp_in
        b = pl.program_id(1)

        @pl.when(b == 0)
        def _():
            dbs_ref[...] = jnp.zeros_like(dbs_ref)
            dcs_ref[...] = jnp.zeros_like(dcs_ref)
            da_ref[...] = jnp.zeros_like(da_ref)
            dd_ref[...] = jnp.zeros_like(dd_ref)

        _s5_project_in(u_ref, bs_ref, s_ref, L, rc)
        _s5_scan_fwd(s_ref, pw_ref, L)
        for r in range(0, L, rc):
            u = u_ref[r:r + rc, :]
            sb = s_ref[r:r + rc, :].astype(bf16)
            ypre = jnp.dot(sb, cs_ref[...], preferred_element_type=f32) + d_ref[...] * u
            dyp = dy_ref[r:r + rc, :] * _gelu_grad(ypre)
            dyp_ref[r:r + rc, :] = dyp
            dd_ref[...] += jnp.sum(dyp * u, axis=0, keepdims=True)
            dypb = dyp.astype(bf16)
            dcs_ref[...] += lax.dot_general(sb, dypb, _DIMS["tn"], preferred_element_type=f32)
            lam_ref[r:r + rc, :] = lax.dot_general(dypb, cs_ref[...], _DIMS["nt"], preferred_element_type=f32)

        row = lax.broadcasted_iota(jnp.int32, (SUBLANES, ns), 0)
        pr = pw_ref[0, 0:SUBLANES, :]
        pi = -pw_ref[1, 0:SUBLANES, :]
        qr = pw_ref[0, SUBLANES:2 * SUBLANES, :]
        qi = -pw_ref[1, SUBLANES:2 * SUBLANES, :]

        def step(j, carry):
            cr, ci, ar, ai = carry
            i = nt - 1 - j
            r0 = pl.multiple_of(i * SUBLANES, SUBLANES)
            xr = lam_ref[pl.ds(r0, SUBLANES), 0:ns]
            xi = lam_ref[pl.ds(r0, SUBLANES), ns:2 * ns]
            for k in (1, 2, 4):
                xr, xi = _cmul_add(xr, xi, pr[k - 1:k, :], pi[k - 1:k, :], _shift_up(xr, k, row), _shift_up(xi, k, row))
            xr, xi = _cmul_add(xr, xi, qr, qi, cr, ci)
            lam_ref[pl.ds(r0, SUBLANES), 0:ns] = xr
            lam_ref[pl.ds(r0, SUBLANES), ns:2 * ns] = xi
            rp = pl.multiple_of(jnp.maximum(i - 1, 0) * SUBLANES, SUBLANES)
            live = jnp.where(i > 0, 1.0, 0.0)
            lr_ = s_ref[pl.ds(rp + SUBLANES - 1, 1), 0:ns] * live
            li_ = s_ref[pl.ds(rp + SUBLANES - 1, 1), ns:2 * ns] * live
            spr = jnp.where(row == 0, lr_, pltpu.roll(s_ref[pl.ds(r0, SUBLANES), 0:ns], 1, 0))
            spi = jnp.where(row == 0, li_, pltpu.roll(s_ref[pl.ds(r0, SUBLANES), ns:2 * ns], 1, 0))
            ar = ar + xr * spr + xi * spi
            ai = ai + xi * spr - xr * spi
            return xr[0:1, :], xi[0:1, :], ar, ai

        z1 = jnp.zeros((1, ns), f32)
        z8 = jnp.zeros((SUBLANES, ns), f32)
        _, _, ar, ai = lax.fori_loop(0, nt, step, (z1, z1, z8, z8))
        da_ref[0:1, :] += jnp.sum(ar, axis=0, keepdims=True)
        da_ref[1:2, :] += jnp.sum(ai, axis=0, keepdims=True)

        for r in range(0, L, rc):
            lamb = lam_ref[r:r + rc, :].astype(bf16)
            dbs_ref[...] += lax.dot_general(u_ref[r:r + rc, :].astype(bf16), lamb, _DIMS["tn"], preferred_element_type=f32)
            du = (lax.dot_general(lamb, bs_ref[...], _DIMS["nt"], preferred_element_type=f32)
                  + d_ref[...] * dyp_ref[r:r + rc, :])
            du_ref[r:r + rc, :] = du.astype(du_ref.dtype)

    ucol = SEG_U * (D_MODEL // SLAB_CH)
    T = B * L
    return pl.pallas_call(
        body, name="s5_bwd", grid=(N_SLAB, B),
        in_specs=[pl.BlockSpec((L, SLAB_CH), lambda s, b: (b, ucol + s)),
                  pl.BlockSpec((L, SLAB_CH), lambda s, b: (b, s)),
                  ANY,
                  pl.BlockSpec((None, SLAB_CH, 2 * SLAB_NS), lambda s, b: (s, 0, 0)),
                  pl.BlockSpec((None, 2 * SLAB_NS, SLAB_CH), lambda s, b: (s, 0, 0)),
                  pl.BlockSpec((None, 2, 2 * SUBLANES, SLAB_NS), lambda s, b: (s, 0, 0, 0)),
                  pl.BlockSpec((1, SLAB_CH), lambda s, b: (0, s))],
        out_specs=[pl.BlockSpec((None, L, SLAB_CH), lambda s, b: (SEG_U, b, s)),
                   pl.BlockSpec((None, SLAB_CH, 2 * SLAB_NS), lambda s, b: (s, 0, 0)),
                   pl.BlockSpec((None, 2 * SLAB_NS, SLAB_CH), lambda s, b: (s, 0, 0)),
                   pl.BlockSpec((None, 2, SLAB_NS), lambda s, b: (s, 0, 0)),
                   pl.BlockSpec((1, SLAB_CH), lambda s, b: (0, s))],
        out_shape=[jax.ShapeDtypeStruct((N_SEG, T, D_MODEL), bf16),
                   jax.ShapeDtypeStruct((N_SLAB, SLAB_CH, 2 * SLAB_NS), f32),
                   jax.ShapeDtypeStruct((N_SLAB, 2 * SLAB_NS, SLAB_CH), f32),
                   jax.ShapeDtypeStruct((N_SLAB, 2, SLAB_NS), f32),
                   jax.ShapeDtypeStruct((1, D_MODEL), f32)],
        scratch_shapes=[pltpu.VMEM((L, 2 * SLAB_NS), f32), pltpu.VMEM((L, 2 * SLAB_NS), f32), pltpu.VMEM((L, SLAB_CH), f32)],
        input_output_aliases={2: 0},
        compiler_params=_params("parallel", "arbitrary"),
    )(p, dya0, dp, bs, cs, pw, d_skip)


def _rows8(i):
    return pl.ds(pl.multiple_of(i * SUBLANES, SUBLANES), SUBLANES)


def _repeat_loop(n, step, init):
    rep = max(u for u in (6, 4, 3, 2, 1) if n % u == 0)

    def body(t, carry):
        for u in range(rep):
            carry = step(t * rep + u, carry)
        return carry

    return lax.fori_loop(0, n // rep, body, init)


def _to_segments(src_ref, dst_ref, seg):
    def step(i, c):
        dst_ref[_rows8(i), :] = src_ref[pl.ds(i, SUBLANES, stride=seg), :]
        return c

    _repeat_loop(seg, step, 0)


def _from_segments(src_ref, dst_ref, seg):
    def step(i, c):
        dst_ref[pl.ds(i, SUBLANES, stride=seg), :] = src_ref[_rows8(i), :]
        return c

    _repeat_loop(seg, step, 0)


def _seg_local_scan(s_ref, ar, ai, seg, reverse):
    ns = SLAB_NS

    def step(j, carry):
        cr, ci = carry
        rows = _rows8(seg - 1 - j if reverse else j)
        cr, ci = _cmul_add(s_ref[rows, 0:ns], s_ref[rows, ns:2 * ns], ar, ai, cr, ci)
        s_ref[rows, 0:ns] = cr
        s_ref[rows, ns:2 * ns] = ci
        return cr, ci

    z = jnp.zeros((SUBLANES, ns), f32)
    return _repeat_loop(seg, step, (z, z))


def _seg_boundaries(fr, fi, alr, ali, reverse):
    row = lax.broadcasted_iota(jnp.int32, fr.shape, 0)
    br = jnp.zeros_like(fr)
    bi = jnp.zeros_like(fi)
    for r in (range(SUBLANES - 2, -1, -1) if reverse else range(1, SUBLANES)):
        s = r + 1 if reverse else r - 1
        nr, ni = _cmul_add(fr[s:s + 1, :], fi[s:s + 1, :], alr, ali, br[s:s + 1, :], bi[s:s + 1, :])
        br = jnp.where(row == r, nr, br)
        bi = jnp.where(row == r, ni, bi)
    return br, bi


def _s5_states(u_ref, bs_ref, pw_ref, up_ref, s_ref, L, rc):
    seg = L // SUBLANES
    ns = SLAB_NS
    _to_segments(u_ref, up_ref, seg)
    _s5_project_in(up_ref, bs_ref, s_ref, L, rc)
    ar, ai = pw_ref[0, 0:1, :], pw_ref[1, 0:1, :]
    fr, fi = _seg_local_scan(s_ref, ar, ai, seg, False)
    br, bi = _seg_boundaries(fr, fi, pw_ref[0, seg - 1:seg, :], pw_ref[1, seg - 1:seg, :], False)

    def fix(i, c):
        rows = _rows8(i)
        xr, xi = _cmul_add(s_ref[rows, 0:ns], s_ref[rows, ns:2 * ns], pw_ref[0, pl.ds(i, 1), :], pw_ref[1, pl.ds(i, 1), :], br, bi)
        s_ref[rows, 0:ns] = xr
        s_ref[rows, ns:2 * ns] = xi
        return c

    _repeat_loop(seg, fix, 0)


def _pw_spec(seg_rows, order):
    if order == "bs":
        return pl.BlockSpec((2, seg_rows, SLAB_NS), lambda b, s: (0, 0, s))
    return pl.BlockSpec((2, seg_rows, SLAB_NS), lambda s, b: (0, 0, s))


def _s5_fwd(p, bs, cs, pw, d_skip, B, L):
    rc = _tile(L, 344)
    seg = L // SUBLANES

    def body(u_ref, bs_ref, cs_ref, pw_ref, d_ref, y_ref, s_ref, up_ref, yp_ref):
        _s5_states(u_ref, bs_ref, pw_ref, up_ref, s_ref, L, rc)
        for r in range(0, L, rc):
            ypre = (jnp.dot(s_ref[r:r + rc, :].astype(bf16), cs_ref[...], preferred_element_type=f32)
                    + d_ref[...] * up_ref[r:r + rc, :])
            yp_ref[r:r + rc, :] = _gelu(ypre)
        _from_segments(yp_ref, y_ref, seg)

    ucol = SEG_U * (D_MODEL // SLAB_CH)
    return pl.pallas_call(
        body, name="s5_fwd", grid=(B, N_SLAB),
        in_specs=[pl.BlockSpec((L, SLAB_CH), lambda b, s: (b, ucol + s)),
                  pl.BlockSpec((None, SLAB_CH, 2 * SLAB_NS), lambda b, s: (s, 0, 0)),
                  pl.BlockSpec((None, 2 * SLAB_NS, SLAB_CH), lambda b, s: (s, 0, 0)),
                  _pw_spec(pw.shape[1], "bs"),
                  pl.BlockSpec((1, SLAB_CH), lambda b, s: (0, s))],
        out_specs=pl.BlockSpec((L, SLAB_CH), lambda b, s: (b, s)),
        out_shape=jax.ShapeDtypeStruct((B * L, D_MODEL), f32),
        scratch_shapes=[pltpu.VMEM((L, 2 * SLAB_NS), f32), pltpu.VMEM((L, SLAB_CH), f32), pltpu.VMEM((L, SLAB_CH), f32)],
        compiler_params=_params("parallel", "parallel"),
    )(p, bs, cs, pw, d_skip)


def _s5_bwd(p, dya0, dp, bs, cs, pw, d_skip, B, L):
    rc = _tile(L, 344)
    ns = SLAB_NS
    seg = L // SUBLANES

    def body(u_ref, dy_ref, dp_in, bs_ref, cs_ref, pw_ref, d_ref, du_ref, dbs_ref, dcs_ref, da_ref, dd_ref,
             s_ref, lam_ref, up_ref, dyp_ref, nat_ref):
        del dp_in

        @pl.when(pl.program_id(1) == 0)
        def _():
            dbs_ref[...] = jnp.zeros_like(dbs_ref)
            dcs_ref[...] = jnp.zeros_like(dcs_ref)
            da_ref[...] = jnp.zeros_like(da_ref)
            dd_ref[...] = jnp.zeros_like(dd_ref)

        _s5_states(u_ref, bs_ref, pw_ref, up_ref, s_ref, L, rc)
        _to_segments(dy_ref, dyp_ref, seg)
        for r in range(0, L, rc):
            u = up_ref[r:r + rc, :]
            sb = s_ref[r:r + rc, :].astype(bf16)
            ypre = jnp.dot(sb, cs_ref[...], preferred_element_type=f32) + d_ref[...] * u
            dyp = dyp_ref[r:r + rc, :] * _gelu_grad(ypre)
            dyp_ref[r:r + rc, :] = dyp
            dd_ref[...] += jnp.sum(dyp * u, axis=0, keepdims=True)
            dypb = dyp.astype(bf16)
            dcs_ref[...] += lax.dot_general(sb, dypb, _DIMS["tn"], preferred_element_type=f32)
            lam_ref[r:r + rc, :] = lax.dot_general(dypb, cs_ref[...], _DIMS["nt"], preferred_element_type=f32)

        ar, ai = pw_ref[0, 0:1, :], -pw_ref[1, 0:1, :]
        fr, fi = _seg_local_scan(lam_ref, ar, ai, seg, True)
        br, bi = _seg_boundaries(fr, fi, pw_ref[0, seg - 1:seg, :], -pw_ref[1, seg - 1:seg, :], True)

        def fix(i, acc):
            accr, acci = acc
            rows = _rows8(i)
            k = seg - 1 - i
            xr, xi = _cmul_add(lam_ref[rows, 0:ns], lam_ref[rows, ns:2 * ns], pw_ref[0, pl.ds(k, 1), :],
                               -pw_ref[1, pl.ds(k, 1), :], br, bi)
            lam_ref[rows, 0:ns] = xr
            lam_ref[rows, ns:2 * ns] = xi
            prev = _rows8(jnp.maximum(i - 1, 0))
            live = jnp.where(i > 0, 1.0, 0.0)
            spr = s_ref[prev, 0:ns] * live
            spi = s_ref[prev, ns:2 * ns] * live
            return accr + xr * spr + xi * spi, acci + xi * spr - xr * spi

        z = jnp.zeros((SUBLANES, ns), f32)
        accr, acci = _repeat_loop(seg, fix, (z, z))
        row = lax.broadcasted_iota(jnp.int32, (SUBLANES, ns), 0)
        last = _rows8(seg - 1)
        spr = jnp.where(row == 0, 0.0, pltpu.roll(s_ref[last, 0:ns], 1, 0))
        spi = jnp.where(row == 0, 0.0, pltpu.roll(s_ref[last, ns:2 * ns], 1, 0))
        xr, xi = lam_ref[0:SUBLANES, 0:ns], lam_ref[0:SUBLANES, ns:2 * ns]
        accr = accr + xr * spr + xi * spi
        acci = acci + xi * spr - xr * spi
        da_ref[0:1, :] += jnp.sum(accr, axis=0, keepdims=True)
        da_ref[1:2, :] += jnp.sum(acci, axis=0, keepdims=True)

        for r in range(0, L, rc):
            lamb = lam_ref[r:r + rc, :].astype(bf16)
            dbs_ref[...] += lax.dot_general(up_ref[r:r + rc, :].astype(bf16), lamb, _DIMS["tn"], preferred_element_type=f32)
            nat_ref[r:r + rc, :] = (lax.dot_general(lamb, bs_ref[...], _DIMS["nt"], preferred_element_type=f32)
                                    + d_ref[...] * dyp_ref[r:r + rc, :])
        _from_segments(nat_ref, up_ref, seg)
        du_ref[...] = up_ref[...].astype(du_ref.dtype)

    ucol = SEG_U * (D_MODEL // SLAB_CH)
    T = B * L
    col = pltpu.VMEM((L, SLAB_CH), f32)
    return pl.pallas_call(
        body, name="s5_bwd", grid=(N_SLAB, B),
        in_specs=[pl.BlockSpec((L, SLAB_CH), lambda s, b: (b, ucol + s)),
                  pl.BlockSpec((L, SLAB_CH), lambda s, b: (b, s)),
                  ANY,
                  pl.BlockSpec((None, SLAB_CH, 2 * SLAB_NS), lambda s, b: (s, 0, 0)),
                  pl.BlockSpec((None, 2 * SLAB_NS, SLAB_CH), lambda s, b: (s, 0, 0)),
                  _pw_spec(pw.shape[1], "sb"),
                  pl.BlockSpec((1, SLAB_CH), lambda s, b: (0, s))],
        out_specs=[pl.BlockSpec((None, L, SLAB_CH), lambda s, b: (SEG_U, b, s)),
                   pl.BlockSpec((None, SLAB_CH, 2 * SLAB_NS), lambda s, b: (s, 0, 0)),
                   pl.BlockSpec((None, 2 * SLAB_NS, SLAB_CH), lambda s, b: (s, 0, 0)),
                   pl.BlockSpec((None, 2, SLAB_NS), lambda s, b: (s, 0, 0)),
                   pl.BlockSpec((1, SLAB_CH), lambda s, b: (0, s))],
        out_shape=[jax.ShapeDtypeStruct((N_SEG, T, D_MODEL), bf16),
                   jax.ShapeDtypeStruct((N_SLAB, SLAB_CH, 2 * SLAB_NS), f32),
                   jax.ShapeDtypeStruct((N_SLAB, 2 * SLAB_NS, SLAB_CH), f32),
                   jax.ShapeDtypeStruct((N_SLAB, 2, SLAB_NS), f32),
                   jax.ShapeDtypeStruct((1, D_MODEL), f32)],
        scratch_shapes=[pltpu.VMEM((L, 2 * SLAB_NS), f32), pltpu.VMEM((L, 2 * SLAB_NS), f32), col, col, col],
        input_output_aliases={2: 0},
        compiler_params=_params("parallel", "arbitrary"),
    )(p, dya0, dp, bs, cs, pw, d_skip)


def _dotb(a, b, dims="nn"):
    return lax.dot_general(a.astype(bf16), b.astype(bf16), _DIMS[dims], preferred_element_type=f32)


def _chunk_cumsum(x, pos):
    k = 1
    while k < CHUNK:
        x = x + jnp.where(pos >= k, pltpu.roll(x, k, 0), 0.0)
        k *= 2
    return x


def _chunk_rev_cumsum(x, pos):
    n = x.shape[0]
    k = 1
    while k < CHUNK:
        x = x + jnp.where(pos < CHUNK - k, pltpu.roll(x, n - k, 0), 0.0)
        k *= 2
    return x


def _hgrn_local(q, fl, lb, pos):
    sg = _sigmoid(fl)
    f = lb + (1.0 - lb) * sg
    g = jnp.log(f)
    cum = _chunk_cumsum(g, pos)
    rest = _chunk_rev_cumsum(g, pos) - g
    e = jnp.exp(cum)
    em = jnp.exp(-cum)
    eo = jnp.exp(rest)
    k = 1.0 - f
    return sg, f, e, em, eo, q * e, k * em, k * eo, jnp.exp(cum + rest)


def _hgrn_block_mask(n):
    r = lax.broadcasted_iota(jnp.int32, (n, n), 0)
    c = lax.broadcasted_iota(jnp.int32, (n, n), 1)
    return ((r & -CHUNK) == (c & -CHUNK)) & (c <= r)


def _chunk_pos(n):
    return lax.broadcasted_iota(jnp.int32, (n, HEAD_DIM), 0) & (CHUNK - 1)


def _hgrn_block_rows(L):
    return _tile(L, 688, CHUNK)


def _chunk_rows(c):
    return pl.ds(pl.multiple_of(c * CHUNK, CHUNK), CHUNK)


def _chunk_loop(nc, step):
    rep = max(u for u in range(1, 49) if nc % u == 0)

    def body(i, carry):
        for u in range(rep):
            step(i * rep + u)
        return carry

    lax.fori_loop(0, nc // rep, body, 0)


def _hgrn_specs(L, order):
    hb = D_MODEL // HEAD_DIM

    def spec(seg):
        if order == "bh":
            return pl.BlockSpec((L, HEAD_DIM), lambda b, h: (b, seg * hb + h))
        return pl.BlockSpec((L, HEAD_DIM), lambda h, b: (b, seg * hb + h))

    return [spec(SEG_Q), spec(SEG_F), spec(SEG_I), spec(SEG_OG)]


def _hgrn_fwd(p, lb, norm_g, B, L):
    nc = L // CHUNK

    rb = _hgrn_block_rows(L)

    def body(q_ref, f_ref, v_ref, og_ref, lb_ref, ng_ref, y_ref, qt_s, ko_s, vb_s, dec_s, o_s, u_s, sb_s):
        lbv = lb_ref[...]
        ngv = ng_ref[...]
        mask = _hgrn_block_mask(rb)
        pos = _chunk_pos(rb)

        for r in range(0, L, rb):
            rows = slice(r, r + rb)
            _, _, _, _, _, qt, kt, ko, dec = _hgrn_local(q_ref[rows, :], f_ref[rows, :], lbv, pos)
            vb = v_ref[rows, :].astype(bf16)
            qtb = qt.astype(bf16)
            pm = jnp.where(mask, _dotb(qtb, kt, "nt"), 0.0)
            o_s[rows, :] = _dotb(pm, vb)
            qt_s[rows, :] = qtb
            ko_s[rows, :] = ko.astype(bf16)
            vb_s[rows, :] = vb
            dec_s[rows, :] = dec

        def update(c):
            rows = _chunk_rows(c)
            u_s[c] = _dotb(vb_s[rows, :], ko_s[rows, :], "tn")

        def chain(c, st):
            sb_s[c] = st.astype(bf16)
            return st * dec_s[_chunk_rows(c), :][0:1, :] + u_s[c]

        def attend(c):
            rows = _chunk_rows(c)
            o_s[rows, :] += _dotb(qt_s[rows, :], sb_s[c], "nt")

        _chunk_loop(nc, update)
        lax.fori_loop(0, nc, chain, jnp.zeros((HEAD_DIM, HEAD_DIM), f32))
        _chunk_loop(nc, attend)

        for r in range(0, L, rb):
            rows = slice(r, r + rb)
            o = o_s[rows, :]
            og = og_ref[rows, :]
            on = o * lax.rsqrt(jnp.mean(o * o, axis=-1, keepdims=True) + EPS) * ngv
            y_ref[rows, :] = (on * og * _sigmoid(og)).astype(y_ref.dtype)

    return pl.pallas_call(
        body, name="hgrn_fwd", grid=(B, HEADS),
        in_specs=_hgrn_specs(L, "bh") + [pl.BlockSpec((1, HEAD_DIM), lambda b, h: (0, h)),
                                          pl.BlockSpec((1, HEAD_DIM), lambda b, h: (0, 0))],
        out_specs=pl.BlockSpec((L, HEAD_DIM), lambda b, h: (b, h)),
        out_shape=jax.ShapeDtypeStruct((B * L, D_MODEL), bf16),
        scratch_shapes=[pltpu.VMEM((L, HEAD_DIM), bf16), pltpu.VMEM((L, HEAD_DIM), bf16), pltpu.VMEM((L, HEAD_DIM), bf16),
                        pltpu.VMEM((L, HEAD_DIM), f32), pltpu.VMEM((L, HEAD_DIM), f32),
                        pltpu.VMEM((nc, HEAD_DIM, HEAD_DIM), f32), pltpu.VMEM((nc, HEAD_DIM, HEAD_DIM), bf16)],
        compiler_params=_params("parallel", "parallel"),
    )(p, p, p, p, lb, norm_g)


def _hgrn_bwd(p, dyb, dp, lb, norm_g, B, L):
    nc = L // CHUNK

    rb = _hgrn_block_rows(L)

    def body(q_ref, f_ref, v_ref, og_ref, dy_ref, dp_in, lb_ref, ng_ref, dseg_ref, dlb_ref, dng_ref,
             st_ref, u_s, dsb_s, qt_s, kt_s, ko_s, vb_s, do_s, dec_s, o_s, dqt_s, dkt_s, dko_s, dv_s, ddec_s):
        del dp_in
        lbv = lb_ref[...]
        ngv = ng_ref[...]
        mask = _hgrn_block_mask(rb)
        pos = _chunk_pos(rb)
        blocks = [slice(r, r + rb) for r in range(0, L, rb)]

        @pl.when(pl.program_id(1) == 0)
        def _():
            dlb_ref[...] = jnp.zeros_like(dlb_ref)

        @pl.when((pl.program_id(0) == 0) & (pl.program_id(1) == 0))
        def _():
            dng_ref[...] = jnp.zeros_like(dng_ref)

        def scores(rows):
            return jnp.where(mask, _dotb(qt_s[rows, :], kt_s[rows, :], "nt"), 0.0).astype(bf16)

        for rows in blocks:
            _, _, _, _, _, qt, kt, ko, dec = _hgrn_local(q_ref[rows, :], f_ref[rows, :], lbv, pos)
            qt_s[rows, :] = qt.astype(bf16)
            kt_s[rows, :] = kt.astype(bf16)
            ko_s[rows, :] = ko.astype(bf16)
            vb_s[rows, :] = v_ref[rows, :].astype(bf16)
            dec_s[rows, :] = dec
            o_s[rows, :] = _dotb(scores(rows), vb_s[rows, :])

        def update(c):
            rows = _chunk_rows(c)
            u_s[c] = _dotb(vb_s[rows, :], ko_s[rows, :], "tn")

        def chain(c, st):
            st_ref[c] = st
            return st * dec_s[_chunk_rows(c), :][0:1, :] + u_s[c]

        def attend(c):
            rows = _chunk_rows(c)
            o_s[rows, :] += _dotb(qt_s[rows, :], st_ref[c], "nt")

        _chunk_loop(nc, update)
        lax.fori_loop(0, nc, chain, jnp.zeros((HEAD_DIM, HEAD_DIM), f32))
        _chunk_loop(nc, attend)

        dng = jnp.zeros((1, HEAD_DIM), f32)
        for rows in blocks:
            o = o_s[rows, :]
            og = og_ref[rows, :]
            dy = dy_ref[rows, :]
            rs = lax.rsqrt(jnp.mean(o * o, axis=-1, keepdims=True) + EPS)
            xn = o * rs
            so = _sigmoid(og)
            dseg_ref[SEG_OG, rows, :] = (dy * xn * ngv * so * (1.0 + og * (1.0 - so))).astype(dseg_ref.dtype)
            don = dy * og * so
            dng = dng + jnp.sum(don * xn, axis=0, keepdims=True)
            dxo = don * ngv
            do = (rs * (dxo - xn * jnp.mean(dxo * xn, axis=-1, keepdims=True))).astype(bf16)
            do_s[rows, :] = do
            dpm = jnp.where(mask, _dotb(do, vb_s[rows, :], "nt"), 0.0).astype(bf16)
            dqt_s[rows, :] = _dotb(dpm, kt_s[rows, :])
            dkt_s[rows, :] = _dotb(dpm, qt_s[rows, :], "tn")
            dv_s[rows, :] = _dotb(scores(rows), do, "tn")
        dng_ref[...] += dng

        def rupdate(c):
            rows = _chunk_rows(c)
            u_s[c] = _dotb(do_s[rows, :], qt_s[rows, :], "tn")

        def rchain(j, dst):
            c = nc - 1 - j
            rows = _chunk_rows(c)
            dsb_s[c] = dst.astype(bf16)
            ddec_s[rows, :] = jnp.broadcast_to(jnp.sum(dst * st_ref[c], axis=0, keepdims=True), (CHUNK, HEAD_DIM))
            return dst * dec_s[rows, :][0:1, :] + u_s[c]

        def rattend(c):
            rows = _chunk_rows(c)
            dst = dsb_s[c]
            dqt_s[rows, :] += _dotb(do_s[rows, :], st_ref[c])
            dv_s[rows, :] += _dotb(ko_s[rows, :], dst, "nt")
            dko_s[rows, :] = _dotb(vb_s[rows, :], dst)

        _chunk_loop(nc, rupdate)
        lax.fori_loop(0, nc, rchain, jnp.zeros((HEAD_DIM, HEAD_DIM), f32))
        _chunk_loop(nc, rattend)

        dlb = jnp.zeros((1, HEAD_DIM), f32)
        for rows in blocks:
            sg, f, e, em, eo, qt, kt, ko, dec = _hgrn_local(q_ref[rows, :], f_ref[rows, :], lbv, pos)
            dqt = dqt_s[rows, :]
            dkt = dkt_s[rows, :]
            dko = dko_s[rows, :]
            dko_ko = dko * ko
            dcum = dqt * qt - dkt * kt - dko_ko
            chunk_tot = _chunk_cumsum(dko_ko, pos) + _chunk_rev_cumsum(dko_ko, pos) - dko_ko
            dcum = dcum + jnp.where(pos == CHUNK - 1, chunk_tot + ddec_s[rows, :] * dec, 0.0)
            df = _chunk_rev_cumsum(dcum, pos) / f - (dkt * em + dko * eo)
            dlb = dlb + jnp.sum(df * (1.0 - sg), axis=0, keepdims=True)
            dseg_ref[SEG_Q, rows, :] = (dqt * e).astype(dseg_ref.dtype)
            dseg_ref[SEG_F, rows, :] = (df * (1.0 - lbv) * sg * (1.0 - sg)).astype(dseg_ref.dtype)
            dseg_ref[SEG_I, rows, :] = dv_s[rows, :].astype(dseg_ref.dtype)
        dlb_ref[...] += dlb

    T = B * L
    sb = pltpu.VMEM((L, HEAD_DIM), bf16)
    sf = pltpu.VMEM((L, HEAD_DIM), f32)
    return pl.pallas_call(
        body, name="hgrn_bwd", grid=(HEADS, B),
        in_specs=_hgrn_specs(L, "hb") + [pl.BlockSpec((L, HEAD_DIM), lambda h, b: (b, h)), ANY,
                                          pl.BlockSpec((1, HEAD_DIM), lambda h, b: (0, h)),
                                          pl.BlockSpec((1, HEAD_DIM), lambda h, b: (0, 0))],
        out_specs=[pl.BlockSpec((4, L, HEAD_DIM), lambda h, b: (0, b, h)),
                   pl.BlockSpec((1, HEAD_DIM), lambda h, b: (0, h)),
                   pl.BlockSpec((1, HEAD_DIM), lambda h, b: (0, 0))],
        out_shape=[jax.ShapeDtypeStruct((N_SEG, T, D_MODEL), bf16), jax.ShapeDtypeStruct((1, D_MODEL), f32),
                   jax.ShapeDtypeStruct((1, HEAD_DIM), f32)],
        scratch_shapes=[pltpu.VMEM((nc, HEAD_DIM, HEAD_DIM), f32), pltpu.VMEM((nc, HEAD_DIM, HEAD_DIM), f32),
                        pltpu.VMEM((nc, HEAD_DIM, HEAD_DIM), bf16), sb, sb, sb, sb, sb, sf, sf, sf, sf, sf, sf, sf],
        input_output_aliases={5: 0},
        compiler_params=_params("arbitrary", "arbitrary"),
    )(p, p, p, p, dyb, dp, lb, norm_g)


def _dz1(dp, w_in_phys):
    _, T, Dm = dp.shape
    tm = _tile(T, 1032)
    return _mm("dz1", dp, w_in_phys, "nt", (T // tm, 1, N_SEG),
               pl.BlockSpec((None, tm, Dm), lambda i, j, k: (k, i, 0)),
               pl.BlockSpec((Dm, Dm), lambda i, j, k: (0, k)),
               jax.ShapeDtypeStruct((T, Dm), f32), pl.BlockSpec((tm, Dm), lambda i, j, k: (i, 0)), (tm, Dm))


def _dw_in(z1, dp):
    _, T, Dm = dp.shape
    tk = _tile(T, 1376)
    return _mm("dw_in", z1, dp, "tn", (1, N_SEG, T // tk),
               pl.BlockSpec((tk, Dm), lambda i, j, k: (k, 0)),
               pl.BlockSpec((None, tk, Dm), lambda i, j, k: (j, k, 0)),
               jax.ShapeDtypeStruct((N_SEG, Dm, Dm), f32),
               pl.BlockSpec((None, Dm, Dm), lambda i, j, k: (j, 0, 0)), (Dm, Dm))


def _dz2(dup, w_up):
    _, T, _ = dup.shape
    tm = _tile(T, 1032)
    tk = D_FF // 2
    return _mm("dz2", dup, w_up, "nt", (T // tm, 1, 4),
               pl.BlockSpec((None, tm, tk), lambda i, j, k: (k // 2, i, k % 2)),
               pl.BlockSpec((D_MODEL, tk), lambda i, j, k: (0, k)),
               jax.ShapeDtypeStruct((T, D_MODEL), f32), pl.BlockSpec((tm, D_MODEL), lambda i, j, k: (i, 0)), (tm, D_MODEL))


def _dw_up(z2, dup):
    _, T, _ = dup.shape
    tn = D_FF // 2
    tk = _tile(T, 688)
    return _mm("dw_up", z2, dup, "tn", (1, N_CHIPS, T // tk),
               pl.BlockSpec((tk, D_MODEL), lambda i, j, k: (k, 0)),
               pl.BlockSpec((None, tk, tn), lambda i, j, k: (j // 2, k, j % 2)),
               jax.ShapeDtypeStruct((N_CHIPS, D_MODEL, tn), f32),
               pl.BlockSpec((None, D_MODEL, tn), lambda i, j, k: (j, 0, 0)), (D_MODEL, tn))


def _place():
    x, y, c = lax.axis_index("x"), lax.axis_index("y"), lax.axis_index("c")
    chips = [(1 - x, y), (x, 1 - y), (1 - x, 1 - y)]
    return x, y, c, chips


def _allgather_chips(arrs):
    n = len(arrs)

    def body(*refs):
        ins, outs = refs[:n], refs[n:2 * n]
        send, recv, local = refs[2 * n:]
        x, y, c, chips = _place()
        me = 2 * x + y

        def copy(a, k, slot):
            px, py = chips[k]
            return pltpu.make_async_remote_copy(src_ref=ins[a], dst_ref=outs[a].at[slot], send_sem=send.at[3 * a + k],
                                                recv_sem=recv.at[3 * a + k], device_id=(px, py, c), device_id_type=MESH)

        for a in range(n):
            pltpu.make_async_copy(ins[a], outs[a].at[me], local.at[a]).start()
            for k in range(3):
                copy(a, k, me).start()
        for a in range(n):
            for k, (px, py) in enumerate(chips):
                copy(a, k, 2 * px + py).wait_recv()
        for a in range(n):
            pltpu.make_async_copy(ins[a], outs[a].at[me], local.at[a]).wait()
            for k in range(3):
                copy(a, k, me).wait_send()

    return pl.pallas_call(
        body, name="allgather_chips", in_specs=[ANY] * n, out_specs=[ANY] * n,
        out_shape=[jax.ShapeDtypeStruct((N_CHIPS,) + a.shape, a.dtype) for a in arrs],
        scratch_shapes=[pltpu.SemaphoreType.DMA((3 * n,)), pltpu.SemaphoreType.DMA((3 * n,)), pltpu.SemaphoreType.DMA((n,))],
    )(*arrs)


def _allgather_split(arrs):
    n = len(arrs)

    def body(*refs):
        ins, outs = refs[:n], refs[n:2 * n]
        send, recv, fsend, frecv = refs[2 * n:]
        x, y, c, chips = _place()
        me = 2 * x + y

        def half(a, core):
            rh = ins[a].shape[0] // 2
            return pl.ds(core * rh, rh)

        def copy(a, k, slot):
            px, py = chips[k]
            return pltpu.make_async_remote_copy(src_ref=ins[a].at[half(a, c), :], dst_ref=outs[a].at[slot, half(a, c), :],
                                                send_sem=send.at[3 * a + k], recv_sem=recv.at[3 * a + k],
                                                device_id=(px, py, c), device_id_type=MESH)

        def forward(a, k, core):
            px, py = chips[k]
            rows = outs[a].at[2 * px + py, half(a, core), :]
            return pltpu.make_async_remote_copy(src_ref=rows, dst_ref=rows, send_sem=fsend.at[3 * a + k],
                                                recv_sem=frecv.at[3 * a + k], device_id=(x, y, 1 - c), device_id_type=MESH)

        for a in range(n):
            for k in range(3):
                copy(a, k, me).start()
        for a in range(n):
            for k, (px, py) in enumerate(chips):
                copy(a, k, 2 * px + py).wait_recv()
                forward(a, k, c).start()
        for a in range(n):
            for k in range(3):
                forward(a, k, 1 - c).wait_recv()
        for a in range(n):
            for k in range(3):
                copy(a, k, me).wait_send()
                forward(a, k, c).wait_send()

    return pl.pallas_call(
        body, name="allgather_split", in_specs=[ANY] * n, out_specs=[ANY] * n,
        out_shape=[jax.ShapeDtypeStruct((N_CHIPS,) + a.shape, a.dtype) for a in arrs],
        scratch_shapes=[pltpu.SemaphoreType.DMA((3 * n,)) for _ in range(4)],
    )(*arrs)


def _sibling_halves(parts):
    n = len(parts)

    def body(*refs):
        ins, outs = refs[:n], refs[n:2 * n]
        send, recv = refs[2 * n:]
        x, y, c, _ = _place()

        def copy(a):
            rh = ins[a].shape[1] // 2
            return pltpu.make_async_remote_copy(src_ref=ins[a].at[:, pl.ds((1 - c) * rh, rh), :], dst_ref=outs[a],
                                                send_sem=send.at[a], recv_sem=recv.at[a], device_id=(x, y, 1 - c),
                                                device_id_type=MESH)

        for a in range(n):
            copy(a).start()
        for a in range(n):
            copy(a).wait_recv()
        for a in range(n):
            copy(a).wait_send()

    return pl.pallas_call(
        body, name="sibling_halves", in_specs=[ANY] * n, out_specs=[ANY] * n,
        out_shape=[jax.ShapeDtypeStruct((a.shape[0], a.shape[1] // 2, a.shape[2]), a.dtype) for a in parts],
        scratch_shapes=[pltpu.SemaphoreType.DMA((n,)), pltpu.SemaphoreType.DMA((n,))],
    )(*parts)


def _add_own_half(name, part, got, core):
    nchip, R, C = part.shape
    rh = R // 2
    tr = _tile(rh, 256, 2 * SUBLANES)
    nt = rh // tr

    def body(core_ref, a_ref, b_ref, o_ref):
        del core_ref
        o_ref[...] = (a_ref[...] + b_ref[...]).astype(o_ref.dtype)

    return pl.pallas_call(
        body, name=name,
        grid_spec=pltpu.PrefetchScalarGridSpec(
            num_scalar_prefetch=1, grid=(nchip, nt),
            in_specs=[pl.BlockSpec((None, tr, C), lambda j, i, core_ref: (j, core_ref[0] * nt + i, 0)),
                      pl.BlockSpec((None, tr, C), lambda j, i, core_ref: (j, i, 0))],
            out_specs=pl.BlockSpec((None, tr, C), lambda j, i, core_ref: (j, i, 0))),
        out_shape=jax.ShapeDtypeStruct((nchip, rh, C), bf16), compiler_params=_params("parallel", "parallel"),
    )(core, part, got)


def _add_own_half_w_in(part, got, core):
    _, R, C = part.shape
    rh = R // 2
    tr = _tile(rh, 256, 2 * SUBLANES)
    nt = rh // tr
    tn = 256
    per_seg = C // tn
    per_chip = IN_COLS // N_CHIPS // tn

    def src(j):
        return ((j // per_seg + N_SEG - 1) % N_SEG, j % per_seg)

    def body(core_ref, a_ref, b_ref, o_ref):
        del core_ref
        o_ref[...] = (a_ref[...] + b_ref[...]).astype(o_ref.dtype)

    return pl.pallas_call(
        body, name="add_half_w_in",
        grid_spec=pltpu.PrefetchScalarGridSpec(
            num_scalar_prefetch=1, grid=(IN_COLS // tn, nt),
            in_specs=[pl.BlockSpec((None, tr, tn), lambda j, i, core_ref: (src(j)[0], core_ref[0] * nt + i, src(j)[1])),
                      pl.BlockSpec((None, tr, tn), lambda j, i, core_ref: (src(j)[0], i, src(j)[1]))],
            out_specs=pl.BlockSpec((None, tr, tn), lambda j, i, core_ref: (j // per_chip, i, j % per_chip))),
        out_shape=jax.ShapeDtypeStruct((N_CHIPS, rh, IN_COLS // N_CHIPS), bf16), compiler_params=_params("parallel", "parallel"),
    )(core, part, got)


def _chip_exchange(sums):
    n = len(sums)

    def body(*refs):
        ins, outs = refs[:n], refs[n:2 * n]
        send, recv = refs[2 * n:]
        x, y, c, chips = _place()
        me = 2 * x + y

        def copy(a, k, slot):
            px, py = chips[k]
            return pltpu.make_async_remote_copy(src_ref=ins[a].at[2 * px + py], dst_ref=outs[a].at[slot], send_sem=send.at[3 * a + k],
                                                recv_sem=recv.at[3 * a + k], device_id=(px, py, c), device_id_type=MESH)

        for a in range(n):
            for k in range(3):
                copy(a, k, me).start()
        for a in range(n):
            for k, (px, py) in enumerate(chips):
                copy(a, k, 2 * px + py).wait_recv()
        for a in range(n):
            for k in range(3):
                copy(a, k, me).wait_send()

    return pl.pallas_call(
        body, name="chip_exchange", in_specs=[ANY] * n, out_specs=[ANY] * n,
        out_shape=[jax.ShapeDtypeStruct(a.shape, a.dtype) for a in sums],
        scratch_shapes=[pltpu.SemaphoreType.DMA((3 * n,)), pltpu.SemaphoreType.DMA((3 * n,))],
    )(*sums)


def _sum_chips(name, slots, sums, where):
    nchip, rh, C = slots.shape
    tr = _tile(rh, 256, 2 * SUBLANES)
    nt = rh // tr

    def body(where_ref, own_ref, s1_ref, s2_ref, s3_ref, o_ref):
        me = where_ref[0]
        by_dist = [r[...].astype(f32) for r in (own_ref, s1_ref, s2_ref, s3_ref)]
        acc = None
        for j in range(nchip):
            d = me ^ j
            term = jnp.where(d == 0, by_dist[0], jnp.where(d == 1, by_dist[1], jnp.where(d == 2, by_dist[2], by_dist[3])))
            acc = term if acc is None else acc + term
        o_ref[...] = acc

    def other(d):
        return pl.BlockSpec((None, tr, C), lambda i, w: (w[0] ^ d, i, 0))

    return pl.pallas_call(
        body, name=name,
        grid_spec=pltpu.PrefetchScalarGridSpec(
            num_scalar_prefetch=1, grid=(nt,),
            in_specs=[other(0), other(1), other(2), other(3)],
            out_specs=pl.BlockSpec((tr, C), lambda i, w: (w[1] * nt + i, 0))),
        out_shape=jax.ShapeDtypeStruct((2 * rh, C), f32), compiler_params=_params("parallel"),
    )(where, sums, slots, slots, slots)


def _sum_slots(name, slots):
    ns, R, C = slots.shape
    tr = _tile(R, 256)

    def body(s_ref, o_ref):
        acc = s_ref[0]
        for j in range(1, ns):
            acc = acc + s_ref[j]
        o_ref[...] = acc

    return pl.pallas_call(
        body, name=name, grid=(R // tr,), in_specs=[pl.BlockSpec((ns, tr, C), lambda i: (0, i, 0))],
        out_specs=pl.BlockSpec((tr, C), lambda i: (i, 0)), out_shape=jax.ShapeDtypeStruct((R, C), f32),
        compiler_params=_params("parallel"),
    )(slots)


def _sibling_join(fulls):
    n = len(fulls)

    def body(*refs):
        ins, outs = refs[:n], refs[n:2 * n]
        send, recv = refs[2 * n:]
        x, y, c, _ = _place()

        def copy(a, core):
            rh = ins[a].shape[0] // 2
            rows = pl.ds(core * rh, rh)
            return pltpu.make_async_remote_copy(src_ref=ins[a].at[rows, :], dst_ref=outs[a].at[rows, :], send_sem=send.at[a],
                                                recv_sem=recv.at[a], device_id=(x, y, 1 - c), device_id_type=MESH)

        for a in range(n):
            copy(a, c).start()
        for a in range(n):
            copy(a, 1 - c).wait_recv()
        for a in range(n):
            copy(a, c).wait_send()

    return pl.pallas_call(
        body, name="sibling_join", in_specs=[ANY] * n, out_specs=[ANY] * n,
        out_shape=[jax.ShapeDtypeStruct(a.shape, a.dtype) for a in fulls],
        scratch_shapes=[pltpu.SemaphoreType.DMA((n,)), pltpu.SemaphoreType.DMA((n,))],
        input_output_aliases={a: a for a in range(n)},
    )(*fulls)


def _allgather_devices(v):
    def body(v_ref, out_ref, send, recv, local):
        x, y, c, _ = _place()
        me = 4 * x + 2 * y + c

        def peer(k):
            return (1 - x if k & 4 else x, 1 - y if k & 2 else y, 1 - c if k & 1 else c)

        def copy(k, slot):
            return pltpu.make_async_remote_copy(src_ref=v_ref, dst_ref=out_ref.at[slot], send_sem=send.at[k - 1],
                                                recv_sem=recv.at[k - 1], device_id=peer(k), device_id_type=MESH)

        own = pltpu.make_async_copy(v_ref, out_ref.at[me], local)
        own.start()
        for k in range(1, N_DEV):
            copy(k, me).start()
        for k in range(1, N_DEV):
            px, py, pc = peer(k)
            copy(k, 4 * px + 2 * py + pc).wait_recv()
        own.wait()
        for k in range(1, N_DEV):
            copy(k, me).wait_send()

    return pl.pallas_call(
        body, name="allgather_devices", in_specs=[ANY], out_specs=ANY,
        out_shape=jax.ShapeDtypeStruct((N_DEV,) + v.shape, v.dtype),
        scratch_shapes=[pltpu.SemaphoreType.DMA((N_DEV - 1,)), pltpu.SemaphoreType.DMA((N_DEV - 1,)), pltpu.SemaphoreType.DMA],
    )(v)


def _adamw(name, w, g, m, v):
    R, C = w.shape
    tr = _tile(R, 256)
    c1 = 1.0 / (1.0 - ADAM_B1 ** ADAM_STEP)
    c2 = 1.0 / (1.0 - ADAM_B2 ** ADAM_STEP)

    def body(w_ref, g_ref, m_ref, v_ref, d_ref, nm_ref, nv_ref):
        gv = g_ref[...]
        nm = ADAM_B1 * m_ref[...] + (1.0 - ADAM_B1) * gv
        nv = ADAM_B2 * v_ref[...] + (1.0 - ADAM_B2) * gv * gv
        d_ref[...] = -ADAM_LR * ((nm * c1) / (jnp.sqrt(nv * c2) + ADAM_EPS) + ADAM_WD * w_ref[...])
        nm_ref[...] = nm
        nv_ref[...] = nv

    row = pl.BlockSpec((tr, C), lambda i: (i, 0))
    sh = jax.ShapeDtypeStruct((R, C), f32)
    return pl.pallas_call(body, name=name, grid=(R // tr,), in_specs=[row] * 4, out_specs=[row] * 3,
                          out_shape=[sh, sh, sh], compiler_params=_params("parallel"))(w, g, m, v)


def _zoh(lr, li, log_dt, b_re, b_im):
    dt = jnp.exp(log_dt)[:, None]
    mag = jnp.exp(lr * dt)
    ab_re = mag * jnp.cos(li * dt)
    ab_im = mag * jnp.sin(li * dt)
    den = lr * lr + li * li
    nr = ab_re - 1.0
    coef_re = (nr * lr + ab_im * li) / den
    coef_im = (ab_im * lr - nr * li) / den
    bb_re = coef_re[..., None] * b_re - coef_im[..., None] * b_im
    bb_im = coef_re[..., None] * b_im + coef_im[..., None] * b_re
    return ab_re, ab_im, bb_re, bb_im


def _s5_tables(ab_re, ab_im, bb_re, bb_im, c_re, c_im, seg):
    eye = jnp.eye(SLAB_GROUPS, dtype=f32)

    def blk_in(bb):
        return jnp.einsum("sgph,gk->sghkp", bb.reshape(N_SLAB, SLAB_GROUPS, SSM_STATE, SSM_GROUP), eye).reshape(
            N_SLAB, SLAB_CH, SLAB_NS)

    def blk_out(cc):
        return jnp.einsum("sghp,gk->skpgh", cc.reshape(N_SLAB, SLAB_GROUPS, SSM_GROUP, SSM_STATE), eye).reshape(
            N_SLAB, SLAB_NS, SLAB_CH)

    bs = jnp.concatenate([blk_in(bb_re), blk_in(bb_im)], axis=2).astype(bf16)
    cs = jnp.concatenate([blk_out(c_re), blk_out(-c_im)], axis=1).astype(bf16)
    n = SSM_GROUPS * SSM_STATE
    pw = _power_table(jnp.stack([ab_re.reshape(1, n), ab_im.reshape(1, n)]), -(-seg // SUBLANES))
    return bs, cs, pw


def _power_table(ab, tiles):
    n = ab.shape[2]

    def body(a_ref, o_ref):
        row = lax.broadcasted_iota(jnp.int32, (SUBLANES, n), 0)
        ar, ai = a_ref[0], a_ref[1]
        tr, ti = jnp.broadcast_to(ar, (SUBLANES, n)), jnp.broadcast_to(ai, (SUBLANES, n))
        pr, pi = ar, ai
        for r in range(1, SUBLANES):
            pr, pi = pr * ar - pi * ai, pr * ai + pi * ar
            tr = jnp.where(row == r, pr, tr)
            ti = jnp.where(row == r, pi, ti)
        o_ref[0, 0:SUBLANES, :] = tr
        o_ref[1, 0:SUBLANES, :] = ti

        def step(j, carry):
            cr, ci = carry
            cr, ci = cr * pr - ci * pi, cr * pi + ci * pr
            o_ref[0, _rows8(j), :] = cr
            o_ref[1, _rows8(j), :] = ci
            return cr, ci

        lax.fori_loop(1, tiles, step, (tr, ti))

    return pl.pallas_call(body, name="power_table", out_shape=jax.ShapeDtypeStruct((2, SUBLANES * tiles, n), f32))(ab)


def _s5_table_grads(dbs, dcs, da):
    eye = jnp.eye(SLAB_GROUPS, dtype=f32)
    d6 = dbs.reshape(N_SLAB, SLAB_GROUPS, SSM_GROUP, 2, SLAB_GROUPS, SSM_STATE)
    dbb = jnp.einsum("sghrkp,gk->rsgph", d6, eye).reshape(2, SSM_GROUPS, SSM_STATE, SSM_GROUP)
    c6 = dcs.reshape(N_SLAB, 2, SLAB_GROUPS, SSM_STATE, SLAB_GROUPS, SSM_GROUP)
    dcc = jnp.einsum("srkpgh,gk->rsghp", c6, eye).reshape(2, SSM_GROUPS, SSM_GROUP, SSM_STATE)
    dab = da.transpose(1, 0, 2).reshape(2, SSM_GROUPS, SSM_STATE)
    return dab[0], dab[1], dbb[0], dbb[1], dcc[0], -dcc[1]


SMALL = ["mix_norm_g", "ssm_lambda_re", "ssm_lambda_im", "ssm_log_dt", "ssm_b_re", "ssm_b_im", "ssm_c_re", "ssm_c_im",
         "ssm_d", "hgrn_lb_logits", "hgrn_norm_g", "ffn_norm_g", "conv_b", "final_norm_g"]
SHARDED_SMALL = ["meta_tokens", "conv_w"]
BIG = ["w_in", "ssm_w_glu", "w_ssm_proj", "w_hgrn_proj", "w_out", "w_up", "w_down"]
WEIGHTS = ['meta_tokens', 'mix_norm_g', 'w_in', 'ssm_lambda_re', 'ssm_lambda_im', 'ssm_log_dt', 'ssm_b_re', 'ssm_b_im',
           'ssm_c_re', 'ssm_c_im', 'ssm_d', 'ssm_w_glu', 'w_ssm_proj', 'hgrn_lb_logits', 'hgrn_norm_g', 'w_hgrn_proj',
           'w_out', 'ffn_norm_g', 'w_up', 'conv_w', 'conv_b', 'w_down', 'final_norm_g']


def _local_grads(x, tgt, meta, w, full):
    B, S, Dm = x.shape
    L = S + N_META
    T = B * L
    h0 = jnp.concatenate([jnp.broadcast_to(meta[None], (B, N_META, Dm)), x], axis=1).reshape(T, Dm)

    lb_all = jax.nn.softmax(w["hgrn_lb_logits"], axis=0)
    lb = lb_all[0:1]
    zoh_out, zoh_vjp = jax.vjp(_zoh, w["ssm_lambda_re"][0], w["ssm_lambda_im"][0], w["ssm_log_dt"][0],
                               w["ssm_b_re"][0], w["ssm_b_im"][0])
    bs, cs, pw = _s5_tables(*zoh_out, w["ssm_c_re"][0], w["ssm_c_im"][0], L // SUBLANES)

    z1 = _rmsnorm_fwd("mix_norm", h0, w["mix_norm_g"])
    p = _mm_rows("in_proj", z1, full["w_in"], "nn", f32, 1024)
    ya0 = _s5_fwd(p, bs, cs, pw, w["ssm_d"], B, L)
    gl = _mm_rows("glu_proj", ya0, full["ssm_w_glu"], "nn", f32, 1024)
    ya = _glu_fwd(ya0, gl)
    yb = _hgrn_fwd(p, lb, w["hgrn_norm_g"], B, L)
    pa = _mm_rows("ssm_proj", ya, full["w_ssm_proj"], "nn", f32, 1024)
    pb = _mm_rows("hgrn_proj", yb, full["w_hgrn_proj"], "nn", f32, 1024)
    merged = _merge_fwd(p, pa, pb)
    h1 = _mm_rows("out_proj", merged, full["w_out"], "nn", f32, 1024, res=h0)
    z2 = _rmsnorm_fwd("ffn_norm", h1, w["ffn_norm_g"])
    up = _mm_rows("up_proj", z2, full["w_up"], "nn", f32, D_FF // 2)
    ff = _conv_fwd(up, full["conv_w"], w["conv_b"], B, L)
    h2 = _mm_rows("down_proj", ff, full["w_down"], "nn", f32, 1024, res=h1, tk=D_FF // 2)

    h2x = h2.reshape(B, L, Dm)[:, N_META:].reshape(B * S, Dm)
    dh2x, loss, d_final_g = _final_loss(h2x, tgt.reshape(B * S, Dm), w["final_norm_g"].reshape(1, Dm))
    dh2 = jnp.pad(dh2x.reshape(B, S, Dm), ((0, 0), (N_META, 0), (0, 0))).reshape(T, Dm)

    dff = _mm_rows("d_ff", dh2, full["w_down"], "nt", f32, D_FF // 2)
    g_w_down = _mm_wgrad("dw_down", ff, dh2, tn=512)
    dup, dconv = _conv_bwd(up, dff, full["conv_w"], w["conv_b"], B, L)
    dz2 = _dz2(dup, full["w_up"])
    g_w_up = _dw_up(z2, dup)
    dh1, d_ffn_g = _rmsnorm_bwd("ffn_norm_bwd", h1, w["ffn_norm_g"], dz2, dh2)

    dmerged = _mm_rows("d_merged", dh1, full["w_out"], "nt", f32, 1024)
    g_w_out = _mm_wgrad("dw_out", merged, dh1)
    dpa, dpb, dp = _merge_bwd(dmerged, p, pa, pb)
    dya = _mm_rows("d_ya", dpa, full["w_ssm_proj"], "nt", f32, 1024)
    g_w_ssm_proj = _mm_wgrad("dw_ssm_proj", ya, dpa)
    dyb = _mm_rows("d_yb", dpb, full["w_hgrn_proj"], "nt", f32, 1024)
    g_w_hgrn_proj = _mm_wgrad("dw_hgrn_proj", yb, dpb)
    dp, d_lb, d_hgrn_g = _hgrn_bwd(p, dyb, dp, lb, w["hgrn_norm_g"], B, L)
    dgl, dya0_direct = _glu_bwd(dya, ya0, gl)
    dya0 = _mm_rows("d_ya0", dgl, full["ssm_w_glu"], "nt", f32, 1024, res=dya0_direct)
    g_w_glu = _mm_wgrad("dw_glu", ya0, dgl)
    dp, dbs, dcs, da, d_skip = _s5_bwd(p, dya0, dp, bs, cs, pw, w["ssm_d"], B, L)
    dz1 = _dz1(dp, full["w_in"])
    g_w_in = _dw_in(z1, dp)
    dh0, d_mix_g = _rmsnorm_bwd("mix_norm_bwd", h0, w["mix_norm_g"], dz1, dh1)

    dh0 = dh0.reshape(B, L, Dm)
    grad_x = dh0[:, N_META:]
    d_meta = _meta_grad(dh0[:, :N_META])

    d_ab_re, d_ab_im, d_bb_re, d_bb_im, d_c_re, d_c_im = _s5_table_grads(dbs, dcs, da)
    d_lr, d_li, d_log_dt, d_b_re, d_b_im = zoh_vjp((d_ab_re, d_ab_im, d_bb_re, d_bb_im))
    sm0, sm1 = lb_all[0:1], lb_all[1:2]
    d_logits = jnp.concatenate([sm0 * (1.0 - sm0) * d_lb, -sm0 * sm1 * d_lb], axis=0)
    small = {
        "meta_tokens": d_meta, "mix_norm_g": d_mix_g, "ssm_lambda_re": d_lr[None], "ssm_lambda_im": d_li[None],
        "ssm_log_dt": d_log_dt[None], "ssm_b_re": d_b_re[None], "ssm_b_im": d_b_im[None], "ssm_c_re": d_c_re[None],
        "ssm_c_im": d_c_im[None], "ssm_d": d_skip, "hgrn_lb_logits": d_logits, "hgrn_norm_g": d_hgrn_g,
        "ffn_norm_g": d_ffn_g, "conv_w": dconv[:, 0:3, :].transpose(1, 0, 2).reshape(3, 2 * D_FF),
        "conv_b": dconv[:, 3, :].reshape(1, 2 * D_FF), "final_norm_g": d_final_g.reshape(Dm),
    }
    big = {
        "w_in": g_w_in, "ssm_w_glu": g_w_glu.reshape(N_CHIPS, Dm // N_CHIPS, Dm),
        "w_ssm_proj": g_w_ssm_proj.reshape(N_CHIPS, Dm // N_CHIPS, Dm),
        "w_hgrn_proj": g_w_hgrn_proj.reshape(N_CHIPS, Dm // N_CHIPS, Dm), "w_out": g_w_out.reshape(N_CHIPS, Dm // N_CHIPS, Dm),
        "w_up": g_w_up, "w_down": g_w_down.reshape(N_CHIPS, D_FF // N_CHIPS, Dm),
    }
    return loss, grad_x, big, small


def _pack(parts):
    flat = jnp.concatenate([parts[k].reshape(-1) for k in parts])
    n = flat.shape[0]
    rows = -(-n // (SUBLANES * LANES)) * SUBLANES
    flat = jnp.pad(flat, (0, rows * LANES - n))
    return flat.reshape(rows, LANES)


def _unpack(packed, like):
    flat = packed.reshape(-1)
    out, o = {}, 0
    for k, ref in like.items():
        n = math.prod(ref.shape)
        out[k] = flat[o:o + n].reshape(ref.shape)
        o += n
    return out


def kernel(x, meta_tokens, mix_norm_g, w_in, ssm_lambda_re, ssm_lambda_im, ssm_log_dt, ssm_b_re, ssm_b_im, ssm_c_re, ssm_c_im, ssm_d, ssm_w_glu, w_ssm_proj, hgrn_lb_logits, hgrn_norm_g, w_hgrn_proj, w_out, ffn_norm_g, w_up, conv_w, conv_b, w_down, final_norm_g, loss_target, m_meta_tokens, m_mix_norm_g, m_w_in, m_ssm_lambda_re, m_ssm_lambda_im, m_ssm_log_dt, m_ssm_b_re, m_ssm_b_im, m_ssm_c_re, m_ssm_c_im, m_ssm_d, m_ssm_w_glu, m_w_ssm_proj, m_hgrn_lb_logits, m_hgrn_norm_g, m_w_hgrn_proj, m_w_out, m_ffn_norm_g, m_w_up, m_conv_w, m_conv_b, m_w_down, m_final_norm_g, v_meta_tokens, v_mix_norm_g, v_w_in, v_ssm_lambda_re, v_ssm_lambda_im, v_ssm_log_dt, v_ssm_b_re, v_ssm_b_im, v_ssm_c_re, v_ssm_c_im, v_ssm_d, v_ssm_w_glu, v_w_ssm_proj, v_hgrn_lb_logits, v_hgrn_norm_g, v_w_hgrn_proj, v_w_out, v_ffn_norm_g, v_w_up, v_conv_w, v_conv_b, v_w_down, v_final_norm_g):
    args = dict(locals())
    w = {k: args[k] for k in WEIGHTS}
    mom = {k: args["m_" + k] for k in WEIGHTS}
    var = {k: args["v_" + k] for k in WEIGHTS}
    Dm = D_MODEL
    cx, cy, cc = lax.axis_index("x"), lax.axis_index("y"), lax.axis_index("c")
    chip = 2 * cx + cy

    shards = [w[k][0].astype(bf16) for k in BIG]
    gathered = _allgather_split(shards)
    g_in, g_glu, g_sp, g_hp, g_out, g_up, g_down = [
        lax.dynamic_update_slice(g, s[None], (chip, 0, 0)) for g, s in zip(gathered, shards)]
    g_meta, g_cw = _allgather_chips([w["meta_tokens"], w["conv_w"][0]])
    w_in_full = jnp.roll(g_in.transpose(1, 0, 2).reshape(Dm, IN_COLS), -Dm, axis=1)
    full = {
        "w_in": w_in_full, "ssm_w_glu": g_glu.reshape(Dm, Dm), "w_ssm_proj": g_sp.reshape(Dm, Dm),
        "w_hgrn_proj": g_hp.reshape(Dm, Dm), "w_out": g_out.reshape(Dm, Dm),
        "w_up": g_up.transpose(1, 0, 2).reshape(Dm, 2 * D_FF), "w_down": g_down.reshape(D_FF, Dm),
        "conv_w": g_cw.transpose(1, 0, 2).reshape(3, 2 * D_FF),
    }
    meta_full = g_meta.transpose(1, 0, 2).reshape(N_META, Dm)

    loss_part, grad_x, big, small = _local_grads(x, loss_target, meta_full, w, full)

    core = cc.reshape(1).astype(jnp.int32)
    parts = [big[k] for k in BIG]
    got = _sibling_halves(parts)
    sums = [_add_own_half_w_in(pt, gt, core) if k == "w_in" else _add_own_half("add_half_" + k, pt, gt, core)
            for k, pt, gt in zip(BIG, parts, got)]
    slots = _chip_exchange(sums)
    where = jnp.stack([chip, cc]).astype(jnp.int32)
    fulls = [_sum_chips("sum_chips_" + k, sl, sm, where) for k, sl, sm in zip(BIG, slots, sums)]
    g_big = dict(zip(BIG, _sibling_join(fulls)))

    small_all = dict(small)
    small_all["loss"] = loss_part[0, 0:1]
    packed = _pack(small_all)
    reduced = _unpack(_sum_slots("sum_devices", _allgather_devices(packed)), small_all)
    loss = reduced.pop("loss")[0]
    mcols = Dm // N_CHIPS
    ccols = 2 * D_FF // N_CHIPS
    grads = {k: reduced[k] for k in SMALL}
    grads["meta_tokens"] = lax.dynamic_slice(reduced["meta_tokens"], (0, chip * mcols), (N_META, mcols))
    grads["conv_w"] = lax.dynamic_slice(reduced["conv_w"], (0, chip * ccols), (3, ccols))[None]
    for k in BIG:
        grads[k] = g_big[k][None]

    delta, new_m, new_v = {}, {}, {}
    for k in BIG:
        shp = w[k].shape
        d, nm, nv = _adamw("adamw_" + k, w[k][0], grads[k][0], mom[k][0], var[k][0])
        delta[k], new_m[k], new_v[k] = d.reshape(shp), nm.reshape(shp), nv.reshape(shp)
    rest = SMALL + SHARDED_SMALL
    pk = [_pack({k: t[k] for k in rest}) for t in (w, grads, mom, var)]
    outs = _adamw("adamw_small", *pk)
    like = {k: w[k] for k in rest}
    for dst, o in zip((delta, new_m, new_v), outs):
        dst.update(_unpack(o, like))

    return (loss, grad_x, *[grads[k].reshape(w[k].shape) for k in WEIGHTS], *[delta[k] for k in WEIGHTS],
            *[new_m[k] for k in WEIGHTS], *[new_v[k] for k in WEIGHTS])
```

```python
import functools
import math

import jax
import jax.numpy as jnp
from jax import lax
from jax.experimental import pallas as pl
from jax.experimental.pallas import tpu as pltpu

f32 = jnp.float32
bf16 = jnp.bfloat16

D_MODEL = 1024
N_META = 16
SSM_GROUP = 16
SSM_GROUPS = 64
SSM_STATE = 64
SLAB_GROUPS = 8
N_SLAB = SSM_GROUPS // SLAB_GROUPS
SLAB_CH = SLAB_GROUPS * SSM_GROUP
SLAB_NS = SLAB_GROUPS * SSM_STATE
HEADS = 8
HEAD_DIM = 128
CHUNK = 16
D_FF = 2816
IN_COLS = 7168
EPS = 1e-6
SUBLANES = 8
LANES = 128
N_CHIPS = 4
N_DEV = 8
ADAM_LR, ADAM_B1, ADAM_B2, ADAM_EPS, ADAM_WD, ADAM_STEP = 0.001, 0.9, 0.999, 1e-08, 0.01, 10
MESH = pl.DeviceIdType.MESH
ANY = pl.BlockSpec(memory_space=pl.ANY)

SEG_Q, SEG_F, SEG_I, SEG_OG, SEG_GA, SEG_GB, SEG_U = range(7)
N_SEG = 7


def _tile(n, target, mult=SUBLANES):
    best = None
    for d in range(mult, min(n, target) + 1, mult):
        if n % d == 0:
            best = d
    return n if best is None else best


def _params(*sem):
    return pltpu.CompilerParams(dimension_semantics=sem)


def _sigmoid(x):
    return 1.0 / (1.0 + jnp.exp(-x))


_DIMS = {"nn": (((1,), (0,)), ((), ())), "nt": (((1,), (1,)), ((), ())), "tn": (((0,), (0,)), ((), ()))}


def _mm(name, a, b, dims, grid, a_spec, b_spec, out_shape, out_spec, acc_shape, res=None, res_spec=None):
    nk = grid[2]
    dn = _DIMS[dims]

    def body(*refs):
        if res is None:
            a_ref, b_ref, o_ref, acc = refs
        else:
            a_ref, b_ref, r_ref, o_ref, acc = refs
        k = pl.program_id(2)

        @pl.when(k == 0)
        def _():
            acc[...] = jnp.zeros_like(acc)

        acc[...] += lax.dot_general(a_ref[...].astype(bf16), b_ref[...].astype(bf16), dn, preferred_element_type=f32)

        @pl.when(k == nk - 1)
        def _():
            r = acc[...]
            if res is not None:
                r = r + r_ref[...]
            o_ref[...] = r.astype(o_ref.dtype)

    ins = [a, b] + ([] if res is None else [res])
    specs = [a_spec, b_spec] + ([] if res is None else [res_spec])
    return pl.pallas_call(
        body, name=name, grid=grid, in_specs=specs, out_specs=out_spec, out_shape=out_shape,
        scratch_shapes=[pltpu.VMEM(acc_shape, f32)],
        compiler_params=_params("parallel", "parallel", "arbitrary"),
    )(*ins)


def _mm_rows(name, a, w, dims, out_dtype, tn, res=None, tk=None):
    T, K = a.shape
    N = w.shape[1] if dims == "nn" else w.shape[0]
    tm = _tile(T, 1032)
    tk = K if tk is None else tk
    grid = (T // tm, N // tn, K // tk)
    a_spec = pl.BlockSpec((tm, tk), lambda i, j, k: (i, k))
    if dims == "nn":
        b_spec = pl.BlockSpec((tk, tn), lambda i, j, k: (k, j))
    else:
        b_spec = pl.BlockSpec((tn, tk), lambda i, j, k: (j, k))
    o_spec = pl.BlockSpec((tm, tn), lambda i, j, k: (i, j))
    return _mm(name, a, w, dims, grid, a_spec, b_spec, jax.ShapeDtypeStruct((T, N), out_dtype), o_spec, (tm, tn),
               res=res, res_spec=None if res is None else o_spec)


def _mm_wgrad(name, a, g, tn=None):
    T, K = a.shape
    N = g.shape[1]
    tk = _tile(T, 688)
    tn = N if tn is None else tn
    grid = (1, N // tn, T // tk)
    a_spec = pl.BlockSpec((tk, K), lambda i, j, k: (k, 0))
    g_spec = pl.BlockSpec((tk, tn), lambda i, j, k: (k, j))
    o_spec = pl.BlockSpec((K, tn), lambda i, j, k: (0, j))
    return _mm(name, a, g, "tn", grid, a_spec, g_spec, jax.ShapeDtypeStruct((K, N), f32), o_spec, (K, tn))


def _rmsnorm_fwd(name, x, g):
    T, Dm = x.shape
    tr = _tile(T, 688)

    def body(x_ref, g_ref, z_ref):
        xv = x_ref[...]
        r = lax.rsqrt(jnp.mean(xv * xv, axis=-1, keepdims=True) + EPS)
        z_ref[...] = (xv * r * g_ref[...]).astype(z_ref.dtype)

    return pl.pallas_call(
        body, name=name, grid=(T // tr,),
        in_specs=[pl.BlockSpec((tr, Dm), lambda i: (i, 0)), pl.BlockSpec((1, Dm), lambda i: (0, 0))],
        out_specs=pl.BlockSpec((tr, Dm), lambda i: (i, 0)),
        out_shape=jax.ShapeDtypeStruct((T, Dm), bf16), compiler_params=_params("parallel"),
    )(x, g)


def _rmsnorm_bwd(name, x, g, dz, dres):
    T, Dm = x.shape
    tr = _tile(T, 688)

    def body(x_ref, g_ref, dz_ref, dres_ref, dx_ref, dg_ref):
        xv = x_ref[...]
        r = lax.rsqrt(jnp.mean(xv * xv, axis=-1, keepdims=True) + EPS)
        xn = xv * r
        dzv = dz_ref[...]
        dzg = dzv * g_ref[...]
        dx_ref[...] = dres_ref[...] + r * (dzg - xn * jnp.mean(dzg * xn, axis=-1, keepdims=True))

        @pl.when(pl.program_id(0) == 0)
        def _():
            dg_ref[...] = jnp.zeros_like(dg_ref)

        dg_ref[...] += jnp.sum(dzv * xn, axis=0, keepdims=True)

    row = pl.BlockSpec((tr, Dm), lambda i: (i, 0))
    par = pl.BlockSpec((1, Dm), lambda i: (0, 0))
    return pl.pallas_call(
        body, name=name, grid=(T // tr,), in_specs=[row, par, row, row], out_specs=[row, par],
        out_shape=[jax.ShapeDtypeStruct((T, Dm), f32), jax.ShapeDtypeStruct((1, Dm), f32)],
        compiler_params=_params("arbitrary"),
    )(x, g, dz, dres)


def _glu_fwd(ya0, gl):
    T, Dm = ya0.shape
    tr = _tile(T, 688)

    def body(y_ref, g_ref, o_ref):
        o_ref[...] = (y_ref[...] * _sigmoid(g_ref[...])).astype(o_ref.dtype)

    row = pl.BlockSpec((tr, Dm), lambda i: (i, 0))
    return pl.pallas_call(body, name="glu_fwd", grid=(T // tr,), in_specs=[row, row], out_specs=row,
                          out_shape=jax.ShapeDtypeStruct((T, Dm), bf16), compiler_params=_params("parallel"))(ya0, gl)


def _glu_bwd(dya, ya0, gl):
    T, Dm = ya0.shape
    tr = _tile(T, 688)

    def body(d_ref, y_ref, g_ref, dg_ref, dy_ref):
        s = _sigmoid(g_ref[...])
        d = d_ref[...]
        dg_ref[...] = (d * y_ref[...] * s * (1.0 - s)).astype(dg_ref.dtype)
        dy_ref[...] = d * s

    row = pl.BlockSpec((tr, Dm), lambda i: (i, 0))
    return pl.pallas_call(body, name="glu_bwd", grid=(T // tr,), in_specs=[row, row, row], out_specs=[row, row],
                          out_shape=[jax.ShapeDtypeStruct((T, Dm), bf16), jax.ShapeDtypeStruct((T, Dm), f32)],
                          compiler_params=_params("parallel"))(dya, ya0, gl)


def _merge_fwd(p, pa, pb):
    T, Dm = pa.shape
    tr = _tile(T, 688)

    def body(ga_ref, gb_ref, pa_ref, pb_ref, o_ref):
        o_ref[...] = (_sigmoid(ga_ref[...]) * pa_ref[...] + _sigmoid(gb_ref[...]) * pb_ref[...]).astype(o_ref.dtype)

    row = pl.BlockSpec((tr, Dm), lambda i: (i, 0))
    return pl.pallas_call(
        body, name="merge_fwd", grid=(T // tr,),
        in_specs=[pl.BlockSpec((tr, Dm), lambda i: (i, SEG_GA)), pl.BlockSpec((tr, Dm), lambda i: (i, SEG_GB)), row, row],
        out_specs=row, out_shape=jax.ShapeDtypeStruct((T, Dm), bf16), compiler_params=_params("parallel"),
    )(p, p, pa, pb)


def _merge_bwd(dm, p, pa, pb):
    T, Dm = pa.shape
    tr = _tile(T, 688)

    def body(dm_ref, ga_ref, gb_ref, pa_ref, pb_ref, dpa_ref, dpb_ref, dp_ref):
        d = dm_ref[...]
        sa = _sigmoid(ga_ref[...])
        sb = _sigmoid(gb_ref[...])
        dpa_ref[...] = (d * sa).astype(dpa_ref.dtype)
        dpb_ref[...] = (d * sb).astype(dpb_ref.dtype)
        dp_ref[0] = (d * pa_ref[...] * sa * (1.0 - sa)).astype(dp_ref.dtype)
        dp_ref[1] = (d * pb_ref[...] * sb * (1.0 - sb)).astype(dp_ref.dtype)

    row = pl.BlockSpec((tr, Dm), lambda i: (i, 0))
    return pl.pallas_call(
        body, name="merge_bwd", grid=(T // tr,),
        in_specs=[row, pl.BlockSpec((tr, Dm), lambda i: (i, SEG_GA)), pl.BlockSpec((tr, Dm), lambda i: (i, SEG_GB)), row, row],
        out_specs=[row, row, pl.BlockSpec((2, tr, Dm), lambda i: (SEG_GA // 2, i, 0))],
        out_shape=[jax.ShapeDtypeStruct((T, Dm), bf16), jax.ShapeDtypeStruct((T, Dm), bf16),
                   jax.ShapeDtypeStruct((N_SEG, T, Dm), bf16)],
        compiler_params=_params("parallel"),
    )(dm, p, p, pa, pb)


def _final_loss(h2x, tgt, g):
    T, Dm = h2x.shape
    tr = _tile(T, 512)

    def body(h_ref, t_ref, g_ref, dh_ref, loss_ref, dg_ref):
        hv = h_ref[...]
        r = lax.rsqrt(jnp.mean(hv * hv, axis=-1, keepdims=True) + EPS)
        xn = hv * r
        gv = g_ref[...]
        err = xn * gv - t_ref[...]
        dy = err * (1.0 / Dm)
        dyg = dy * gv
        dh_ref[...] = r * (dyg - xn * jnp.mean(dyg * xn, axis=-1, keepdims=True))

        @pl.when(pl.program_id(0) == 0)
        def _():
            dg_ref[...] = jnp.zeros_like(dg_ref)
            loss_ref[...] = jnp.zeros_like(loss_ref)

        dg_ref[...] += jnp.sum(dy * xn, axis=0, keepdims=True)
        loss_ref[...] += jnp.sum(err * err) * (0.5 / Dm)

    row = pl.BlockSpec((tr, Dm), lambda i: (i, 0))
    par = pl.BlockSpec((1, Dm), lambda i: (0, 0))
    return pl.pallas_call(
        body, name="final_loss", grid=(T // tr,), in_specs=[row, row, par],
        out_specs=[row, pl.BlockSpec((1, LANES), lambda i: (0, 0)), par],
        out_shape=[jax.ShapeDtypeStruct((T, Dm), f32), jax.ShapeDtypeStruct((1, LANES), f32), jax.ShapeDtypeStruct((1, Dm), f32)],
        compiler_params=_params("arbitrary"),
    )(h2x, tgt, g)


def _meta_grad(dh0_meta):
    B = dh0_meta.shape[0]

    def body(d_ref, o_ref):
        acc = d_ref[0]
        for b in range(1, B):
            acc = acc + d_ref[b]
        o_ref[...] = acc

    return pl.pallas_call(body, name="meta_grad", out_shape=jax.ShapeDtypeStruct(dh0_meta.shape[1:], f32))(dh0_meta)


def _shift_down(x, k, row):
    return jnp.where(row >= k, pltpu.roll(x, k, 0), 0.0)


def _shift_up(x, k, row):
    n = x.shape[0]
    return jnp.where(row < n - k, pltpu.roll(x, n - k, 0), 0.0)


def _conv_fwd(up, conv_w, conv_b, B, L):
    tc = 256
    nt = D_FF // tc

    def body(xa_ref, xb_ref, wa_ref, wb_ref, ba_ref, bb_ref, o_ref):
        row = lax.broadcasted_iota(jnp.int32, (L, tc), 0)

        def conv(x_ref, w_ref, b_ref):
            x = x_ref[...]
            return (b_ref[...] + w_ref[0:1, :] * _shift_down(x, 2, row) + w_ref[1:2, :] * _shift_down(x, 1, row)
                    + w_ref[2:3, :] * x)

        a = conv(xa_ref, wa_ref, ba_ref)
        b = conv(xb_ref, wb_ref, bb_ref)
        o_ref[...] = (a * _sigmoid(a) * b).astype(o_ref.dtype)

    return pl.pallas_call(
        body, name="conv_fwd", grid=(B, nt),
        in_specs=[pl.BlockSpec((L, tc), lambda b, j: (b, j)), pl.BlockSpec((L, tc), lambda b, j: (b, j + nt)),
                  pl.BlockSpec((3, tc), lambda b, j: (0, j)), pl.BlockSpec((3, tc), lambda b, j: (0, j + nt)),
                  pl.BlockSpec((1, tc), lambda b, j: (0, j)), pl.BlockSpec((1, tc), lambda b, j: (0, j + nt))],
        out_specs=pl.BlockSpec((L, tc), lambda b, j: (b, j)),
        out_shape=jax.ShapeDtypeStruct((B * L, D_FF), bf16), compiler_params=_params("parallel", "parallel"),
    )(up, up, conv_w, conv_w, conv_b, conv_b)


def _conv_bwd(up, dff, conv_w, conv_b, B, L):
    tc = 256
    nt = D_FF // tc

    def body(xa_ref, xb_ref, d_ref, wa_ref, wb_ref, ba_ref, bb_ref, dup_ref, dw_ref):
        row = lax.broadcasted_iota(jnp.int32, (L, tc), 0)
        xs, pre = [], []
        for x_ref, w_ref, b_ref in ((xa_ref, wa_ref, ba_ref), (xb_ref, wb_ref, bb_ref)):
            x = x_ref[...]
            x1 = _shift_down(x, 1, row)
            x2 = _shift_down(x, 2, row)
            xs.append((x, x1, x2))
            pre.append(b_ref[...] + w_ref[0:1, :] * x2 + w_ref[1:2, :] * x1 + w_ref[2:3, :] * x)
        a, b = pre
        s = _sigmoid(a)
        d = d_ref[...]
        grads = (d * b * s * (1.0 + a * (1.0 - s)), d * a * s)

        @pl.when(pl.program_id(1) == 0)
        def _():
            dw_ref[...] = jnp.zeros_like(dw_ref)

        for h, (gr, (x, x1, x2), w_ref) in enumerate(zip(grads, xs, (wa_ref, wb_ref))):
            dup_ref[h] = (w_ref[2:3, :] * gr + w_ref[1:2, :] * _shift_up(gr, 1, row)
                          + w_ref[0:1, :] * _shift_up(gr, 2, row)).astype(dup_ref.dtype)
            dw_ref[h, 0:1, :] += jnp.sum(gr * x2, axis=0, keepdims=True)
            dw_ref[h, 1:2, :] += jnp.sum(gr * x1, axis=0, keepdims=True)
            dw_ref[h, 2:3, :] += jnp.sum(gr * x, axis=0, keepdims=True)
            dw_ref[h, 3:4, :] += jnp.sum(gr, axis=0, keepdims=True)

    return pl.pallas_call(
        body, name="conv_bwd", grid=(nt, B),
        in_specs=[pl.BlockSpec((L, tc), lambda j, b: (b, j)), pl.BlockSpec((L, tc), lambda j, b: (b, j + nt)),
                  pl.BlockSpec((L, tc), lambda j, b: (b, j)),
                  pl.BlockSpec((3, tc), lambda j, b: (0, j)), pl.BlockSpec((3, tc), lambda j, b: (0, j + nt)),
                  pl.BlockSpec((1, tc), lambda j, b: (0, j)), pl.BlockSpec((1, tc), lambda j, b: (0, j + nt))],
        out_specs=[pl.BlockSpec((2, L, tc), lambda j, b: (0, b, j)), pl.BlockSpec((2, SUBLANES, tc), lambda j, b: (0, 0, j))],
        out_shape=[jax.ShapeDtypeStruct((2, B * L, D_FF), bf16), jax.ShapeDtypeStruct((2, SUBLANES, D_FF), f32)],
        compiler_params=_params("parallel", "arbitrary"),
    )(up, up, dff, conv_w, conv_w, conv_b, conv_b)


CONV_ROWS = 2 * SUBLANES


def _rows16(i):
    return pl.ds(pl.multiple_of(i * CONV_ROWS, CONV_ROWS), CONV_ROWS)


def _conv_taps(x_ref, i, row):
    x = x_ref[_rows16(i), :]
    live = jnp.where(i > 0, 1.0, 0.0)
    r0 = jnp.maximum(i * CONV_ROWS, 2)
    p1 = x_ref[pl.ds(r0 - 1, 1), :] * live
    p2 = x_ref[pl.ds(r0 - 2, 1), :] * live
    x1 = jnp.where(row == 0, p1, pltpu.roll(x, 1, 0))
    x2 = jnp.where(row == 0, p2, jnp.where(row == 1, p1, pltpu.roll(x, 2, 0)))
    return x, x1, x2


def _conv_bwd(up, dff, conv_w, conv_b, B, L):
    tc = 256
    nt = D_FF // tc
    n = L // CONV_ROWS

    def body(xa_ref, xb_ref, d_ref, wa_ref, wb_ref, ba_ref, bb_ref, dup_ref, dw_ref, ga_ref, gb_ref):
        row = lax.broadcasted_iota(jnp.int32, (CONV_ROWS, tc), 0)

        @pl.when(pl.program_id(1) == 0)
        def _():
            dw_ref[...] = jnp.zeros_like(dw_ref)

        zero_tail = jnp.zeros((CONV_ROWS, tc), f32)
        ga_ref[L:L + CONV_ROWS, :] = zero_tail
        gb_ref[L:L + CONV_ROWS, :] = zero_tail

        def fold(v):
            return v[0:SUBLANES, :] + v[SUBLANES:CONV_ROWS, :]

        def step(i, acc):
            taps_a = _conv_taps(xa_ref, i, row)
            taps_b = _conv_taps(xb_ref, i, row)
            a = ba_ref[...] + wa_ref[0:1, :] * taps_a[2] + wa_ref[1:2, :] * taps_a[1] + wa_ref[2:3, :] * taps_a[0]
            b = bb_ref[...] + wb_ref[0:1, :] * taps_b[2] + wb_ref[1:2, :] * taps_b[1] + wb_ref[2:3, :] * taps_b[0]
            s = _sigmoid(a)
            d = d_ref[_rows16(i), :]
            g_a = d * b * s * (1.0 + a * (1.0 - s))
            g_b = d * a * s
            ga_ref[_rows16(i), :] = g_a
            gb_ref[_rows16(i), :] = g_b
            new = []
            for g, (x, x1, x2) in ((g_a, taps_a), (g_b, taps_b)):
                new += [fold(g * x2), fold(g * x1), fold(g * x), fold(g)]
            return tuple(o + v for o, v in zip(acc, new))

        z = jnp.zeros((SUBLANES, tc), f32)
        acc = _repeat_loop(n, step, (z,) * 8)
        for h in range(2):
            for t in range(4):
                dw_ref[h, t:t + 1, :] += jnp.sum(acc[4 * h + t], axis=0, keepdims=True)

        def back(i, c):
            for h, (g_ref, w_ref) in enumerate(((ga_ref, wa_ref), (gb_ref, wb_ref))):
                g = g_ref[_rows16(i), :]
                n1 = g_ref[pl.ds(i * CONV_ROWS + CONV_ROWS, 1), :]
                n2 = g_ref[pl.ds(i * CONV_ROWS + CONV_ROWS + 1, 1), :]
                u1 = jnp.where(row == CONV_ROWS - 1, n1, pltpu.roll(g, CONV_ROWS - 1, 0))
                u2 = jnp.where(row == CONV_ROWS - 1, n2, jnp.where(row == CONV_ROWS - 2, n1, pltpu.roll(g, CONV_ROWS - 2, 0)))
                dup_ref[h, _rows16(i), :] = (w_ref[2:3, :] * g + w_ref[1:2, :] * u1 + w_ref[0:1, :] * u2).astype(dup_ref.dtype)
            return c

        _repeat_loop(n, back, 0)

    return pl.pallas_call(
        body, name="conv_bwd", grid=(nt, B),
        in_specs=[pl.BlockSpec((L, tc), lambda j, b: (b, j)), pl.BlockSpec((L, tc), lambda j, b: (b, j + nt)),
                  pl.BlockSpec((L, tc), lambda j, b: (b, j)),
                  pl.BlockSpec((3, tc), lambda j, b: (0, j)), pl.BlockSpec((3, tc), lambda j, b: (0, j + nt)),
                  pl.BlockSpec((1, tc), lambda j, b: (0, j)), pl.BlockSpec((1, tc), lambda j, b: (0, j + nt))],
        out_specs=[pl.BlockSpec((2, L, tc), lambda j, b: (0, b, j)), pl.BlockSpec((2, SUBLANES, tc), lambda j, b: (0, 0, j))],
        out_shape=[jax.ShapeDtypeStruct((2, B * L, D_FF), bf16), jax.ShapeDtypeStruct((2, SUBLANES, D_FF), f32)],
        scratch_shapes=[pltpu.VMEM((L + CONV_ROWS, tc), f32), pltpu.VMEM((L + CONV_ROWS, tc), f32)],
        compiler_params=_params("parallel", "arbitrary"),
    )(up, up, dff, conv_w, conv_w, conv_b, conv_b)


GELU_C = math.sqrt(2.0 / math.pi)
GELU_A = 0.044715


def _gelu(x):
    return 0.5 * x * (1.0 + jnp.tanh(GELU_C * (x + GELU_A * x * x * x)))


def _gelu_grad(x):
    t = jnp.tanh(GELU_C * (x + GELU_A * x * x * x))
    return 0.5 * (1.0 + t) + 0.5 * x * (1.0 - t * t) * GELU_C * (1.0 + 3.0 * GELU_A * x * x)


def _cmul_add(xr, xi, ar, ai, sr, si):
    return xr + ar * sr - ai * si, xi + ar * si + ai * sr


def _s5_scan_fwd(s_ref, pw_ref, L):
    ns = SLAB_NS
    row = lax.broadcasted_iota(jnp.int32, (SUBLANES, ns), 0)
    pr = pw_ref[0, 0:SUBLANES, :]
    pi = pw_ref[1, 0:SUBLANES, :]

    def step(i, carry):
        cr, ci = carry
        r0 = pl.multiple_of(i * SUBLANES, SUBLANES)
        xr = s_ref[pl.ds(r0, SUBLANES), 0:ns]
        xi = s_ref[pl.ds(r0, SUBLANES), ns:2 * ns]
        for k in (1, 2, 4):
            xr, xi = _cmul_add(xr, xi, pr[k - 1:k, :], pi[k - 1:k, :], _shift_down(xr, k, row), _shift_down(xi, k, row))
        xr, xi = _cmul_add(xr, xi, pr, pi, cr, ci)
        s_ref[pl.ds(r0, SUBLANES), 0:ns] = xr
        s_ref[pl.ds(r0, SUBLANES), ns:2 * ns] = xi
        return xr[SUBLANES - 1:SUBLANES, :], xi[SUBLANES - 1:SUBLANES, :]

    z = jnp.zeros((1, ns), f32)
    lax.fori_loop(0, L // SUBLANES, step, (z, z))


def _s5_project_in(u_ref, bs_ref, s_ref, L, rc):
    for r in range(0, L, rc):
        s_ref[r:r + rc, :] = jnp.dot(u_ref[r:r + rc, :].astype(bf16), bs_ref[...], preferred_element_type=f32)


def _s5_fwd(p, bs, cs, pw, d_skip, B, L):
    rc = _tile(L, 344)

    def body(u_ref, bs_ref, cs_ref, pw_ref, d_ref, y_ref, s_ref):
        _s5_project_in(u_ref, bs_ref, s_ref, L, rc)
        _s5_scan_fwd(s_ref, pw_ref, L)
        for r in range(0, L, rc):
            ypre = (jnp.dot(s_ref[r:r + rc, :].astype(bf16), cs_ref[...], preferred_element_type=f32)
                    + d_ref[...] * u_ref[r:r + rc, :])
            y_ref[r:r + rc, :] = _gelu(ypre)

    ucol = SEG_U * (D_MODEL // SLAB_CH)
    return pl.pallas_call(
        body, name="s5_fwd", grid=(B, N_SLAB),
        in_specs=[pl.BlockSpec((L, SLAB_CH), lambda b, s: (b, ucol + s)),
                  pl.BlockSpec((None, SLAB_CH, 2 * SLAB_NS), lambda b, s: (s, 0, 0)),
                  pl.BlockSpec((None, 2 * SLAB_NS, SLAB_CH), lambda b, s: (s, 0, 0)),
                  pl.BlockSpec((None, 2, 2 * SUBLANES, SLAB_NS), lambda b, s: (s, 0, 0, 0)),
                  pl.BlockSpec((1, SLAB_CH), lambda b, s: (0, s))],
        out_specs=pl.BlockSpec((L, SLAB_CH), lambda b, s: (b, s)),
        out_shape=jax.ShapeDtypeStruct((B * L, D_MODEL), f32),
        scratch_shapes=[pltpu.VMEM((L, 2 * SLAB_NS), f32)],
        compiler_params=_params("parallel", "parallel"),
    )(p, bs, cs, pw, d_skip)


def _s5_bwd(p, dya0, dp, bs, cs, pw, d_skip, B, L):
    rc = _tile(L, 344)
    ns = SLAB_NS
    nt = L // SUBLANES

    def body(u_ref, dy_ref, dp_in, bs_ref, cs_ref, pw_ref, d_ref, du_ref, dbs_ref, dcs_ref, da_ref, dd_ref,
             s_ref, lam_ref, dyp_ref):
        del dp_in
        b = pl.program_id(1)

        @pl.when(b == 0)
        def _():
            dbs_ref[...] = jnp.zeros_like(dbs_ref)
            dcs_ref[...] = jnp.zeros_like(dcs_ref)
            da_ref[...] = jnp.zeros_like(da_ref)
            dd_ref[...] = jnp.zeros_like(dd_ref)

        _s5_project_in(u_ref, bs_ref, s_ref, L, rc)
        _s5_scan_fwd(s_ref, pw_ref, L)
        for r in range(0, L, rc):
            u = u_ref[r:r + rc, :]
            sb = s_ref[r:r + rc, :].astype(bf16)
            ypre = jnp.dot(sb, cs_ref[...], preferred_element_type=f32) + d_ref[...] * u
            dyp = dy_ref[r:r + rc, :] * _gelu_grad(ypre)
            dyp_ref[r:r + rc, :] = dyp
            dd_ref[...] += jnp.sum(dyp * u, axis=0, keepdims=True)
            dypb = dyp.astype(bf16)
            dcs_ref[...] += lax.dot_general(sb, dypb, _DIMS["tn"], preferred_element_type=f32)
            lam_ref[r:r + rc, :] = lax.dot_general(dypb, cs_ref[...], _DIMS["nt"], preferred_element_type=f32)

        row = lax.broadcasted_iota(jnp.int32, (SUBLANES, ns), 0)
        pr = pw_ref[0, 0:SUBLANES, :]
        pi = -pw_ref[1, 0:SUBLANES, :]
        qr = pw_ref[0, SUBLANES:2 * SUBLANES, :]
        qi = -pw_ref[1, SUBLANES:2 * SUBLANES, :]

        def step(j, carry):
            cr, ci, ar, ai = carry
            i = nt - 1 - j
            r0 = pl.multiple_of(i * SUBLANES, SUBLANES)
            xr = lam_ref[pl.ds(r0, SUBLANES), 0:ns]
            xi = lam_ref[pl.ds(r0, SUBLANES), ns:2 * ns]
            for k in (1, 2, 4):
                xr, xi = _cmul_add(xr, xi, pr[k - 1:k, :], pi[k - 1:k, :], _shift_up(xr, k, row), _shift_up(xi, k, row))
            xr, xi = _cmul_add(xr, xi, qr, qi, cr, ci)
            lam_ref[pl.ds(r0, SUBLANES), 0:ns] = xr
            lam_ref[pl.ds(r0, SUBLANES), ns:2 * ns] = xi
            rp = pl.multiple_of(jnp.maximum(i - 1, 0) * SUBLANES, SUBLANES)
            live = jnp.where(i > 0, 1.0, 0.0)
            lr_ = s_ref[pl.ds(rp + SUBLANES - 1, 1), 0:ns] * live
            li_ = s_ref[pl.ds(rp + SUBLANES - 1, 1), ns:2 * ns] * live
            spr = jnp.where(row == 0, lr_, pltpu.roll(s_ref[pl.ds(r0, SUBLANES), 0:ns], 1, 0))
            spi = jnp.where(row == 0, li_, pltpu.roll(s_ref[pl.ds(r0, SUBLANES), ns:2 * ns], 1, 0))
            ar = ar + xr * spr + xi * spi
            ai = ai + xi * spr - xr * spi
            return xr[0:1, :], xi[0:1, :], ar, ai

        z1 = jnp.zeros((1, ns), f32)
        z8 = jnp.zeros((SUBLANES, ns), f32)
        _, _, ar, ai = lax.fori_loop(0, nt, step, (z1, z1, z8, z8))
        da_ref[0:1, :] += jnp.sum(ar, axis=0, keepdims=True)
        da_ref[1:2, :] += jnp.sum(ai, axis=0, keepdims=True)

        for r in range(0, L, rc):
            lamb = lam_ref[r:r + rc, :].astype(bf16)
            dbs_ref[...] += lax.dot_general(u_ref[r:r + rc, :].astype(bf16), lamb, _DIMS["tn"], preferred_element_type=f32)
            du = (lax.dot_general(lamb, bs_ref[...], _DIMS["nt"], preferred_element_type=f32)
                  + d_ref[...] * dyp_ref[r:r + rc, :])
            du_ref[r:r + rc, :] = du.astype(du_ref.dtype)

    ucol = SEG_U * (D_MODEL // SLAB_CH)
    T = B * L
    return pl.pallas_call(
        body, name="s5_bwd", grid=(N_SLAB, B),
        in_specs=[pl.BlockSpec((L, SLAB_CH), lambda s, b: (b, ucol + s)),
                  pl.BlockSpec((L, SLAB_CH), lambda s, b: (b, s)),
                  ANY,
                  pl.BlockSpec((None, SLAB_CH, 2 * SLAB_NS), lambda s, b: (s, 0, 0)),
                  pl.BlockSpec((None, 2 * SLAB_NS, SLAB_CH), lambda s, b: (s, 0, 0)),
                  pl.BlockSpec((None, 2, 2 * SUBLANES, SLAB_NS), lambda s, b: (s, 0, 0, 0)),
                  pl.BlockSpec((1, SLAB_CH), lambda s, b: (0, s))],
        out_specs=[pl.BlockSpec((None, L, SLAB_CH), lambda s, b: (SEG_U, b, s)),
                   pl.BlockSpec((None, SLAB_CH, 2 * SLAB_NS), lambda s, b: (s, 0, 0)),
                   pl.BlockSpec((None, 2 * SLAB_NS, SLAB_CH), lambda s, b: (s, 0, 0)),
                   pl.BlockSpec((None, 2, SLAB_NS), lambda s, b: (s, 0, 0)),
                   pl.BlockSpec((1, SLAB_CH), lambda s, b: (0, s))],
        out_shape=[jax.ShapeDtypeStruct((N_SEG, T, D_MODEL), bf16),
                   jax.ShapeDtypeStruct((N_SLAB, SLAB_CH, 2 * SLAB_NS), f32),
                   jax.ShapeDtypeStruct((N_SLAB, 2 * SLAB_NS, SLAB_CH), f32),
                   jax.ShapeDtypeStruct((N_SLAB, 2, SLAB_NS), f32),
                   jax.ShapeDtypeStruct((1, D_MODEL), f32)],
        scratch_shapes=[pltpu.VMEM((L, 2 * SLAB_NS), f32), pltpu.VMEM((L, 2 * SLAB_NS), f32), pltpu.VMEM((L, SLAB_CH), f32)],
        input_output_aliases={2: 0},
        compiler_params=_params("parallel", "arbitrary"),
    )(p, dya0, dp, bs, cs, pw, d_skip)


def _rows8(i):
    return pl.ds(pl.multiple_of(i * SUBLANES, SUBLANES), SUBLANES)


def _repeat_loop(n, step, init):
    rep = max(u for u in (6, 4, 3, 2, 1) if n % u == 0)

    def body(t, carry):
        for u in range(rep):
            carry = step(t * rep + u, carry)
        return carry

    return lax.fori_loop(0, n // rep, body, init)


def _to_segments(src_ref, dst_ref, seg):
    def step(i, c):
        dst_ref[_rows8(i), :] = src_ref[pl.ds(i, SUBLANES, stride=seg), :]
        return c

    _repeat_loop(seg, step, 0)


def _from_segments(src_ref, dst_ref, seg):
    def step(i, c):
        dst_ref[pl.ds(i, SUBLANES, stride=seg), :] = src_ref[_rows8(i), :]
        return c

    _repeat_loop(seg, step, 0)


def _seg_local_scan(s_ref, ar, ai, seg, reverse):
    ns = SLAB_NS

    def step(j, carry):
        cr, ci = carry
        rows = _rows8(seg - 1 - j if reverse else j)
        cr, ci = _cmul_add(s_ref[rows, 0:ns], s_ref[rows, ns:2 * ns], ar, ai, cr, ci)
        s_ref[rows, 0:ns] = cr
        s_ref[rows, ns:2 * ns] = ci
        return cr, ci

    z = jnp.zeros((SUBLANES, ns), f32)
    return _repeat_loop(seg, step, (z, z))


def _seg_boundaries(fr, fi, alr, ali, reverse):
    row = lax.broadcasted_iota(jnp.int32, fr.shape, 0)
    br = jnp.zeros_like(fr)
    bi = jnp.zeros_like(fi)
    for r in (range(SUBLANES - 2, -1, -1) if reverse else range(1, SUBLANES)):
        s = r + 1 if reverse else r - 1
        nr, ni = _cmul_add(fr[s:s + 1, :], fi[s:s + 1, :], alr, ali, br[s:s + 1, :], bi[s:s + 1, :])
        br = jnp.where(row == r, nr, br)
        bi = jnp.where(row == r, ni, bi)
    return br, bi


def _s5_states(u_ref, bs_ref, pw_ref, up_ref, s_ref, L, rc):
    seg = L // SUBLANES
    ns = SLAB_NS
    _to_segments(u_ref, up_ref, seg)
    _s5_project_in(up_ref, bs_ref, s_ref, L, rc)
    ar, ai = pw_ref[0, 0:1, :], pw_ref[1, 0:1, :]
    fr, fi = _seg_local_scan(s_ref, ar, ai, seg, False)
    br, bi = _seg_boundaries(fr, fi, pw_ref[0, seg - 1:seg, :], pw_ref[1, seg - 1:seg, :], False)

    def fix(i, c):
        rows = _rows8(i)
        xr, xi = _cmul_add(s_ref[rows, 0:ns], s_ref[rows, ns:2 * ns], pw_ref[0, pl.ds(i, 1), :], pw_ref[1, pl.ds(i, 1), :], br, bi)
        s_ref[rows, 0:ns] = xr
        s_ref[rows, ns:2 * ns] = xi
        return c

    _repeat_loop(seg, fix, 0)


def _pw_spec(seg_rows, order):
    if order == "bs":
        return pl.BlockSpec((2, seg_rows, SLAB_NS), lambda b, s: (0, 0, s))
    return pl.BlockSpec((2, seg_rows, SLAB_NS), lambda s, b: (0, 0, s))


def _s5_fwd(p, bs, cs, pw, d_skip, B, L):
    rc = _tile(L, 344)
    seg = L // SUBLANES

    def body(u_ref, bs_ref, cs_ref, pw_ref, d_ref, y_ref, s_ref, up_ref, yp_ref):
        _s5_states(u_ref, bs_ref, pw_ref, up_ref, s_ref, L, rc)
        for r in range(0, L, rc):
            ypre = (jnp.dot(s_ref[r:r + rc, :].astype(bf16), cs_ref[...], preferred_element_type=f32)
                    + d_ref[...] * up_ref[r:r + rc, :])
            yp_ref[r:r + rc, :] = _gelu(ypre)
        _from_segments(yp_ref, y_ref, seg)

    ucol = SEG_U * (D_MODEL // SLAB_CH)
    return pl.pallas_call(
        body, name="s5_fwd", grid=(B, N_SLAB),
        in_specs=[pl.BlockSpec((L, SLAB_CH), lambda b, s: (b, ucol + s)),
                  pl.BlockSpec((None, SLAB_CH, 2 * SLAB_NS), lambda b, s: (s, 0, 0)),
                  pl.BlockSpec((None, 2 * SLAB_NS, SLAB_CH), lambda b, s: (s, 0, 0)),
                  _pw_spec(pw.shape[1], "bs"),
                  pl.BlockSpec((1, SLAB_CH), lambda b, s: (0, s))],
        out_specs=pl.BlockSpec((L, SLAB_CH), lambda b, s: (b, s)),
        out_shape=jax.ShapeDtypeStruct((B * L, D_MODEL), f32),
        scratch_shapes=[pltpu.VMEM((L, 2 * SLAB_NS), f32), pltpu.VMEM((L, SLAB_CH), f32), pltpu.VMEM((L, SLAB_CH), f32)],
        compiler_params=_params("parallel", "parallel"),
    )(p, bs, cs, pw, d_skip)


def _s5_bwd(p, dya0, dp, bs, cs, pw, d_skip, B, L):
    rc = _tile(L, 344)
    ns = SLAB_NS
    seg = L // SUBLANES

    def body(u_ref, dy_ref, dp_in, bs_ref, cs_ref, pw_ref, d_ref, du_ref, dbs_ref, dcs_ref, da_ref, dd_ref,
             s_ref, lam_ref, up_ref, dyp_ref, nat_ref):
        del dp_in

        @pl.when(pl.program_id(1) == 0)
        def _():
            dbs_ref[...] = jnp.zeros_like(dbs_ref)
            dcs_ref[...] = jnp.zeros_like(dcs_ref)
            da_ref[...] = jnp.zeros_like(da_ref)
            dd_ref[...] = jnp.zeros_like(dd_ref)

        _s5_states(u_ref, bs_ref, pw_ref, up_ref, s_ref, L, rc)
        _to_segments(dy_ref, dyp_ref, seg)
        for r in range(0, L, rc):
            u = up_ref[r:r + rc, :]
            sb = s_ref[r:r + rc, :].astype(bf16)
            ypre = jnp.dot(sb, cs_ref[...], preferred_element_type=f32) + d_ref[...] * u
            dyp = dyp_ref[r:r + rc, :] * _gelu_grad(ypre)
            dyp_ref[r:r + rc, :] = dyp
            dd_ref[...] += jnp.sum(dyp * u, axis=0, keepdims=True)
            dypb = dyp.astype(bf16)
            dcs_ref[...] += lax.dot_general(sb, dypb, _DIMS["tn"], preferred_element_type=f32)
            lam_ref[r:r + rc, :] = lax.dot_general(dypb, cs_ref[...], _DIMS["nt"], preferred_element_type=f32)

        ar, ai = pw_ref[0, 0:1, :], -pw_ref[1, 0:1, :]
        fr, fi = _seg_local_scan(lam_ref, ar, ai, seg, True)
        br, bi = _seg_boundaries(fr, fi, pw_ref[0, seg - 1:seg, :], -pw_ref[1, seg - 1:seg, :], True)

        def fix(i, acc):
            accr, acci = acc
            rows = _rows8(i)
            k = seg - 1 - i
            xr, xi = _cmul_add(lam_ref[rows, 0:ns], lam_ref[rows, ns:2 * ns], pw_ref[0, pl.ds(k, 1), :],
                               -pw_ref[1, pl.ds(k, 1), :], br, bi)
            lam_ref[rows, 0:ns] = xr
            lam_ref[rows, ns:2 * ns] = xi
            prev = _rows8(jnp.maximum(i - 1, 0))
            live = jnp.where(i > 0, 1.0, 0.0)
            spr = s_ref[prev, 0:ns] * live
            spi = s_ref[prev, ns:2 * ns] * live
            return accr + xr * spr + xi * spi, acci + xi * spr - xr * spi

        z = jnp.zeros((SUBLANES, ns), f32)
        accr, acci = _repeat_loop(seg, fix, (z, z))
        row = lax.broadcasted_iota(jnp.int32, (SUBLANES, ns), 0)
        last = _rows8(seg - 1)
        spr = jnp.where(row == 0, 0.0, pltpu.roll(s_ref[last, 0:ns], 1, 0))
        spi = jnp.where(row == 0, 0.0, pltpu.roll(s_ref[last, ns:2 * ns], 1, 0))
        xr, xi = lam_ref[0:SUBLANES, 0:ns], lam_ref[0:SUBLANES, ns:2 * ns]
        accr = accr + xr * spr + xi * spi
        acci = acci + xi * spr - xr * spi
        da_ref[0:1, :] += jnp.sum(accr, axis=0, keepdims=True)
        da_ref[1:2, :] += jnp.sum(acci, axis=0, keepdims=True)

        for r in range(0, L, rc):
            lamb = lam_ref[r:r + rc, :].astype(bf16)
            dbs_ref[...] += lax.dot_general(up_ref[r:r + rc, :].astype(bf16), lamb, _DIMS["tn"], preferred_element_type=f32)
            nat_ref[r:r + rc, :] = (lax.dot_general(lamb, bs_ref[...], _DIMS["nt"], preferred_element_type=f32)
                                    + d_ref[...] * dyp_ref[r:r + rc, :])
        _from_segments(nat_ref, up_ref, seg)
        du_ref[...] = up_ref[...].astype(du_ref.dtype)

    ucol = SEG_U * (D_MODEL // SLAB_CH)
    T = B * L
    col = pltpu.VMEM((L, SLAB_CH), f32)
    return pl.pallas_call(
        body, name="s5_bwd", grid=(N_SLAB, B),
        in_specs=[pl.BlockSpec((L, SLAB_CH), lambda s, b: (b, ucol + s)),
                  pl.BlockSpec((L, SLAB_CH), lambda s, b: (b, s)),
                  ANY,
                  pl.BlockSpec((None, SLAB_CH, 2 * SLAB_NS), lambda s, b: (s, 0, 0)),
                  pl.BlockSpec((None, 2 * SLAB_NS, SLAB_CH), lambda s, b: (s, 0, 0)),
                  _pw_spec(pw.shape[1], "sb"),
                  pl.BlockSpec((1, SLAB_CH), lambda s, b: (0, s))],
        out_specs=[pl.BlockSpec((None, L, SLAB_CH), lambda s, b: (SEG_U, b, s)),
                   pl.BlockSpec((None, SLAB_CH, 2 * SLAB_NS), lambda s, b: (s, 0, 0)),
                   pl.BlockSpec((None, 2 * SLAB_NS, SLAB_CH), lambda s, b: (s, 0, 0)),
                   pl.BlockSpec((None, 2, SLAB_NS), lambda s, b: (s, 0, 0)),
                   pl.BlockSpec((1, SLAB_CH), lambda s, b: (0, s))],
        out_shape=[jax.ShapeDtypeStruct((N_SEG, T, D_MODEL), bf16),
                   jax.ShapeDtypeStruct((N_SLAB, SLAB_CH, 2 * SLAB_NS), f32),
                   jax.ShapeDtypeStruct((N_SLAB, 2 * SLAB_NS, SLAB_CH), f32),
                   jax.ShapeDtypeStruct((N_SLAB, 2, SLAB_NS), f32),
                   jax.ShapeDtypeStruct((1, D_MODEL), f32)],
        scratch_shapes=[pltpu.VMEM((L, 2 * SLAB_NS), f32), pltpu.VMEM((L, 2 * SLAB_NS), f32), col, col, col],
        input_output_aliases={2: 0},
        compiler_params=_params("parallel", "arbitrary"),
    )(p, dya0, dp, bs, cs, pw, d_skip)


def _dotb(a, b, dims="nn"):
    return lax.dot_general(a.astype(bf16), b.astype(bf16), _DIMS[dims], preferred_element_type=f32)


def _chunk_cumsum(x, pos):
    k = 1
    while k < CHUNK:
        x = x + jnp.where(pos >= k, pltpu.roll(x, k, 0), 0.0)
        k *= 2
    return x


def _chunk_rev_cumsum(x, pos):
    n = x.shape[0]
    k = 1
    while k < CHUNK:
        x = x + jnp.where(pos < CHUNK - k, pltpu.roll(x, n - k, 0), 0.0)
        k *= 2
    return x


def _hgrn_local(q, fl, lb, pos):
    sg = _sigmoid(fl)
    f = lb + (1.0 - lb) * sg
    g = jnp.log(f)
    cum = _chunk_cumsum(g, pos)
    rest = _chunk_rev_cumsum(g, pos) - g
    e = jnp.exp(cum)
    em = jnp.exp(-cum)
    eo = jnp.exp(rest)
    k = 1.0 - f
    return sg, f, e, em, eo, q * e, k * em, k * eo, jnp.exp(cum + rest)


def _hgrn_block_mask(n):
    r = lax.broadcasted_iota(jnp.int32, (n, n), 0)
    c = lax.broadcasted_iota(jnp.int32, (n, n), 1)
    return ((r & -CHUNK) == (c & -CHUNK)) & (c <= r)


def _chunk_pos(n):
    return lax.broadcasted_iota(jnp.int32, (n, HEAD_DIM), 0) & (CHUNK - 1)


def _hgrn_block_rows(L):
    return _tile(L, 688, CHUNK)


def _chunk_rows(c):
    return pl.ds(pl.multiple_of(c * CHUNK, CHUNK), CHUNK)


def _chunk_loop(nc, step):
    rep = max(u for u in range(1, 49) if nc % u == 0)

    def body(i, carry):
        for u in range(rep):
            step(i * rep + u)
        return carry

    lax.fori_loop(0, nc // rep, body, 0)


def _hgrn_specs(L, order):
    hb = D_MODEL // HEAD_DIM

    def spec(seg):
        if order == "bh":
            return pl.BlockSpec((L, HEAD_DIM), lambda b, h: (b, seg * hb + h))
        return pl.BlockSpec((L, HEAD_DIM), lambda h, b: (b, seg * hb + h))

    return [spec(SEG_Q), spec(SEG_F), spec(SEG_I), spec(SEG_OG)]


def _hgrn_fwd(p, lb, norm_g, B, L):
    nc = L // CHUNK

    rb = _hgrn_block_rows(L)

    def body(q_ref, f_ref, v_ref, og_ref, lb_ref, ng_ref, y_ref, qt_s, ko_s, vb_s, dec_s, o_s, u_s, sb_s):
        lbv = lb_ref[...]
        ngv = ng_ref[...]
        mask = _hgrn_block_mask(rb)
        pos = _chunk_pos(rb)

        for r in range(0, L, rb):
            rows = slice(r, r + rb)
            _, _, _, _, _, qt, kt, ko, dec = _hgrn_local(q_ref[rows, :], f_ref[rows, :], lbv, pos)
            vb = v_ref[rows, :].astype(bf16)
            qtb = qt.astype(bf16)
            pm = jnp.where(mask, _dotb(qtb, kt, "nt"), 0.0)
            o_s[rows, :] = _dotb(pm, vb)
            qt_s[rows, :] = qtb
            ko_s[rows, :] = ko.astype(bf16)
            vb_s[rows, :] = vb
            dec_s[rows, :] = dec

        def update(c):
            rows = _chunk_rows(c)
            u_s[c] = _dotb(vb_s[rows, :], ko_s[rows, :], "tn")

        def chain(c, st):
            sb_s[c] = st.astype(bf16)
            return st * dec_s[_chunk_rows(c), :][0:1, :] + u_s[c]

        def attend(c):
            rows = _chunk_rows(c)
            o_s[rows, :] += _dotb(qt_s[rows, :], sb_s[c], "nt")

        _chunk_loop(nc, update)
        lax.fori_loop(0, nc, chain, jnp.zeros((HEAD_DIM, HEAD_DIM), f32))
        _chunk_loop(nc, attend)

        for r in range(0, L, rb):
            rows = slice(r, r + rb)
            o = o_s[rows, :]
            og = og_ref[rows, :]
            on = o * lax.rsqrt(jnp.mean(o * o, axis=-1, keepdims=True) + EPS) * ngv
            y_ref[rows, :] = (on * og * _sigmoid(og)).astype(y_ref.dtype)

    return pl.pallas_call(
        body, name="hgrn_fwd", grid=(B, HEADS),
        in_specs=_hgrn_specs(L, "bh") + [pl.BlockSpec((1, HEAD_DIM), lambda b, h: (0, h)),
                                          pl.BlockSpec((1, HEAD_DIM), lambda b, h: (0, 0))],
        out_specs=pl.BlockSpec((L, HEAD_DIM), lambda b, h: (b, h)),
        out_shape=jax.ShapeDtypeStruct((B * L, D_MODEL), bf16),
        scratch_shapes=[pltpu.VMEM((L, HEAD_DIM), bf16), pltpu.VMEM((L, HEAD_DIM), bf16), pltpu.VMEM((L, HEAD_DIM), bf16),
                        pltpu.VMEM((L, HEAD_DIM), f32), pltpu.VMEM((L, HEAD_DIM), f32),
                        pltpu.VMEM((nc, HEAD_DIM, HEAD_DIM), f32), pltpu.VMEM((nc, HEAD_DIM, HEAD_DIM), bf16)],
        compiler_params=_params("parallel", "parallel"),
    )(p, p, p, p, lb, norm_g)


def _hgrn_bwd(p, dyb, dp, lb, norm_g, B, L):
    nc = L // CHUNK

    rb = _hgrn_block_rows(L)

    def body(q_ref, f_ref, v_ref, og_ref, dy_ref, dp_in, lb_ref, ng_ref, dseg_ref, dlb_ref, dng_ref,
             st_ref, u_s, dsb_s, qt_s, kt_s, ko_s, vb_s, do_s, dec_s, o_s, dqt_s, dkt_s, dko_s, dv_s, ddec_s):
        del dp_in
        lbv = lb_ref[...]
        ngv = ng_ref[...]
        mask = _hgrn_block_mask(rb)
        pos = _chunk_pos(rb)
        blocks = [slice(r, r + rb) for r in range(0, L, rb)]

        @pl.when(pl.program_id(1) == 0)
        def _():
            dlb_ref[...] = jnp.zeros_like(dlb_ref)

        @pl.when((pl.program_id(0) == 0) & (pl.program_id(1) == 0))
        def _():
            dng_ref[...] = jnp.zeros_like(dng_ref)

        def scores(rows):
            return jnp.where(mask, _dotb(qt_s[rows, :], kt_s[rows, :], "nt"), 0.0).astype(bf16)

        for rows in blocks:
            _, _, _, _, _, qt, kt, ko, dec = _hgrn_local(q_ref[rows, :], f_ref[rows, :], lbv, pos)
            qt_s[rows, :] = qt.astype(bf16)
            kt_s[rows, :] = kt.astype(bf16)
            ko_s[rows, :] = ko.astype(bf16)
            vb_s[rows, :] = v_ref[rows, :].astype(bf16)
            dec_s[rows, :] = dec
            o_s[rows, :] = _dotb(scores(rows), vb_s[rows, :])

        def update(c):
            rows = _chunk_rows(c)
            u_s[c] = _dotb(vb_s[rows, :], ko_s[rows, :], "tn")

        def chain(c, st):
            st_ref[c] = st
            return st * dec_s[_chunk_rows(c), :][0:1, :] + u_s[c]

        def attend(c):
            rows = _chunk_rows(c)
            o_s[rows, :] += _dotb(qt_s[rows, :], st_ref[c], "nt")

        _chunk_loop(nc, update)
        lax.fori_loop(0, nc, chain, jnp.zeros((HEAD_DIM, HEAD_DIM), f32))
        _chunk_loop(nc, attend)

        dng = jnp.zeros((1, HEAD_DIM), f32)
        for rows in blocks:
            o = o_s[rows, :]
            og = og_ref[rows, :]
            dy = dy_ref[rows, :]
            rs = lax.rsqrt(jnp.mean(o * o, axis=-1, keepdims=True) + EPS)
            xn = o * rs
            so = _sigmoid(og)
            dseg_ref[SEG_OG, rows, :] = (dy * xn * ngv * so * (1.0 + og * (1.0 - so))).astype(dseg_ref.dtype)
            don = dy * og * so
            dng = dng + jnp.sum(don * xn, axis=0, keepdims=True)
            dxo = don * ngv
            do = (rs * (dxo - xn * jnp.mean(dxo * xn, axis=-1, keepdims=True))).astype(bf16)
            do_s[rows, :] = do
            dpm = jnp.where(mask, _dotb(do, vb_s[rows, :], "nt"), 0.0).astype(bf16)
            dqt_s[rows, :] = _dotb(dpm, kt_s[rows, :])
            dkt_s[rows, :] = _dotb(dpm, qt_s[rows, :], "tn")
            dv_s[rows, :] = _dotb(scores(rows), do, "tn")
        dng_ref[...] += dng

        def rupdate(c):
            rows = _chunk_rows(c)
            u_s[c] = _dotb(do_s[rows, :], qt_s[rows, :], "tn")

        def rchain(j, dst):
            c = nc - 1 - j
            rows = _chunk_rows(c)
            dsb_s[c] = dst.astype(bf16)
            ddec_s[rows, :] = jnp.broadcast_to(jnp.sum(dst * st_ref[c], axis=0, keepdims=True), (CHUNK, HEAD_DIM))
            return dst * dec_s[rows, :][0:1, :] + u_s[c]

        def rattend(c):
            rows = _chunk_rows(c)
            dst = dsb_s[c]
            dqt_s[rows, :] += _dotb(do_s[rows, :], st_ref[c])
            dv_s[rows, :] += _dotb(ko_s[rows, :], dst, "nt")
            dko_s[rows, :] = _dotb(vb_s[rows, :], dst)

        _chunk_loop(nc, rupdate)
        lax.fori_loop(0, nc, rchain, jnp.zeros((HEAD_DIM, HEAD_DIM), f32))
        _chunk_loop(nc, rattend)

        dlb = jnp.zeros((1, HEAD_DIM), f32)
        for rows in blocks:
            sg, f, e, em, eo, qt, kt, ko, dec = _hgrn_local(q_ref[rows, :], f_ref[rows, :], lbv, pos)
            dqt = dqt_s[rows, :]
            dkt = dkt_s[rows, :]
            dko = dko_s[rows, :]
            dko_ko = dko * ko
            dcum = dqt * qt - dkt * kt - dko_ko
            chunk_tot = _chunk_cumsum(dko_ko, pos) + _chunk_rev_cumsum(dko_ko, pos) - dko_ko
            dcum = dcum + jnp.where(pos == CHUNK - 1, chunk_tot + ddec_s[rows, :] * dec, 0.0)
            df = _chunk_rev_cumsum(dcum, pos) / f - (dkt * em + dko * eo)
            dlb = dlb + jnp.sum(df * (1.0 - sg), axis=0, keepdims=True)
            dseg_ref[SEG_Q, rows, :] = (dqt * e).astype(dseg_ref.dtype)
            dseg_ref[SEG_F, rows, :] = (df * (1.0 - lbv) * sg * (1.0 - sg)).astype(dseg_ref.dtype)
            dseg_ref[SEG_I, rows, :] = dv_s[rows, :].astype(dseg_ref.dtype)
        dlb_ref[...] += dlb

    T = B * L
    sb = pltpu.VMEM((L, HEAD_DIM), bf16)
    sf = pltpu.VMEM((L, HEAD_DIM), f32)
    return pl.pallas_call(
        body, name="hgrn_bwd", grid=(HEADS, B),
        in_specs=_hgrn_specs(L, "hb") + [pl.BlockSpec((L, HEAD_DIM), lambda h, b: (b, h)), ANY,
                                          pl.BlockSpec((1, HEAD_DIM), lambda h, b: (0, h)),
                                          pl.BlockSpec((1, HEAD_DIM), lambda h, b: (0, 0))],
        out_specs=[pl.BlockSpec((4, L, HEAD_DIM), lambda h, b: (0, b, h)),
                   pl.BlockSpec((1, HEAD_DIM), lambda h, b: (0, h)),
                   pl.BlockSpec((1, HEAD_DIM), lambda h, b: (0, 0))],
        out_shape=[jax.ShapeDtypeStruct((N_SEG, T, D_MODEL), bf16), jax.ShapeDtypeStruct((1, D_MODEL), f32),
                   jax.ShapeDtypeStruct((1, HEAD_DIM), f32)],
        scratch_shapes=[pltpu.VMEM((nc, HEAD_DIM, HEAD_DIM), f32), pltpu.VMEM((nc, HEAD_DIM, HEAD_DIM), f32),
                        pltpu.VMEM((nc, HEAD_DIM, HEAD_DIM), bf16), sb, sb, sb, sb, sb, sf, sf, sf, sf, sf, sf, sf],
        input_output_aliases={5: 0},
        compiler_params=_params("arbitrary", "arbitrary"),
    )(p, p, p, p, dyb, dp, lb, norm_g)


def _dz1(dp, w_in_phys):
    _, T, Dm = dp.shape
    tm = _tile(T, 1032)
    return _mm("dz1", dp, w_in_phys, "nt", (T // tm, 1, N_SEG),
               pl.BlockSpec((None, tm, Dm), lambda i, j, k: (k, i, 0)),
               pl.BlockSpec((Dm, Dm), lambda i, j, k: (0, k)),
               jax.ShapeDtypeStruct((T, Dm), f32), pl.BlockSpec((tm, Dm), lambda i, j, k: (i, 0)), (tm, Dm))


def _dw_in(z1, dp):
    _, T, Dm = dp.shape
    tk = _tile(T, 1376)
    return _mm("dw_in", z1, dp, "tn", (1, N_SEG, T // tk),
               pl.BlockSpec((tk, Dm), lambda i, j, k: (k, 0)),
               pl.BlockSpec((None, tk, Dm), lambda i, j, k: (j, k, 0)),
               jax.ShapeDtypeStruct((N_SEG, Dm, Dm), f32),
               pl.BlockSpec((None, Dm, Dm), lambda i, j, k: (j, 0, 0)), (Dm, Dm))


def _dz2(dup, w_up):
    _, T, _ = dup.shape
    tm = _tile(T, 1032)
    tk = D_FF // 2
    return _mm("dz2", dup, w_up, "nt", (T // tm, 1, 4),
               pl.BlockSpec((None, tm, tk), lambda i, j, k: (k // 2, i, k % 2)),
               pl.BlockSpec((D_MODEL, tk), lambda i, j, k: (0, k)),
               jax.ShapeDtypeStruct((T, D_MODEL), f32), pl.BlockSpec((tm, D_MODEL), lambda i, j, k: (i, 0)), (tm, D_MODEL))


def _dw_up(z2, dup):
    _, T, _ = dup.shape
    tn = D_FF // 2
    tk = _tile(T, 688)
    return _mm("dw_up", z2, dup, "tn", (1, N_CHIPS, T // tk),
               pl.BlockSpec((tk, D_MODEL), lambda i, j, k: (k, 0)),
               pl.BlockSpec((None, tk, tn), lambda i, j, k: (j // 2, k, j % 2)),
               jax.ShapeDtypeStruct((N_CHIPS, D_MODEL, tn), f32),
               pl.BlockSpec((None, D_MODEL, tn), lambda i, j, k: (j, 0, 0)), (D_MODEL, tn))


def _place():
    x, y, c = lax.axis_index("x"), lax.axis_index("y"), lax.axis_index("c")
    chips = [(1 - x, y), (x, 1 - y), (1 - x, 1 - y)]
    return x, y, c, chips


def _allgather_chips(arrs):
    n = len(arrs)

    def body(*refs):
        ins, outs = refs[:n], refs[n:2 * n]
        send, recv, local = refs[2 * n:]
        x, y, c, chips = _place()
        me = 2 * x + y

        def copy(a, k, slot):
            px, py = chips[k]
            return pltpu.make_async_remote_copy(src_ref=ins[a], dst_ref=outs[a].at[slot], send_sem=send.at[3 * a + k],
                                                recv_sem=recv.at[3 * a + k], device_id=(px, py, c), device_id_type=MESH)

        for a in range(n):
            pltpu.make_async_copy(ins[a], outs[a].at[me], local.at[a]).start()
            for k in range(3):
                copy(a, k, me).start()
        for a in range(n):
            for k, (px, py) in enumerate(chips):
                copy(a, k, 2 * px + py).wait_recv()
        for a in range(n):
            pltpu.make_async_copy(ins[a], outs[a].at[me], local.at[a]).wait()
            for k in range(3):
                copy(a, k, me).wait_send()

    return pl.pallas_call(
        body, name="allgather_chips", in_specs=[ANY] * n, out_specs=[ANY] * n,
        out_shape=[jax.ShapeDtypeStruct((N_CHIPS,) + a.shape, a.dtype) for a in arrs],
        scratch_shapes=[pltpu.SemaphoreType.DMA((3 * n,)), pltpu.SemaphoreType.DMA((3 * n,)), pltpu.SemaphoreType.DMA((n,))],
    )(*arrs)


def _allgather_split(arrs):
    n = len(arrs)

    def body(*refs):
        ins, outs = refs[:n], refs[n:2 * n]
        send, recv, fsend, frecv = refs[2 * n:]
        x, y, c, chips = _place()
        me = 2 * x + y

        def half(a, core):
            rh = ins[a].shape[0] // 2
            return pl.ds(core * rh, rh)

        def copy(a, k, slot):
            px, py = chips[k]
            return pltpu.make_async_remote_copy(src_ref=ins[a].at[half(a, c), :], dst_ref=outs[a].at[slot, half(a, c), :],
                                                send_sem=send.at[3 * a + k], recv_sem=recv.at[3 * a + k],
                                                device_id=(px, py, c), device_id_type=MESH)

        def forward(a, k, core):
            px, py = chips[k]
            rows = outs[a].at[2 * px + py, half(a, core), :]
            return pltpu.make_async_remote_copy(src_ref=rows, dst_ref=rows, send_sem=fsend.at[3 * a + k],
                                                recv_sem=frecv.at[3 * a + k], device_id=(x, y, 1 - c), device_id_type=MESH)

        for a in range(n):
            for k in range(3):
                copy(a, k, me).start()
        for a in range(n):
            for k, (px, py) in enumerate(chips):
                copy(a, k, 2 * px + py).wait_recv()
                forward(a, k, c).start()
        for a in range(n):
            for k in range(3):
                forward(a, k, 1 - c).wait_recv()
        for a in range(n):
            for k in range(3):
                copy(a, k, me).wait_send()
                forward(a, k, c).wait_send()

    return pl.pallas_call(
        body, name="allgather_split", in_specs=[ANY] * n, out_specs=[ANY] * n,
        out_shape=[jax.ShapeDtypeStruct((N_CHIPS,) + a.shape, a.dtype) for a in arrs],
        scratch_shapes=[pltpu.SemaphoreType.DMA((3 * n,)) for _ in range(4)],
    )(*arrs)


def _sibling_halves(parts):
    n = len(parts)

    def body(*refs):
        ins, outs = refs[:n], refs[n:2 * n]
        send, recv = refs[2 * n:]
        x, y, c, _ = _place()

        def copy(a):
            rh = ins[a].shape[1] // 2
            return pltpu.make_async_remote_copy(src_ref=ins[a].at[:, pl.ds((1 - c) * rh, rh), :], dst_ref=outs[a],
                                                send_sem=send.at[a], recv_sem=recv.at[a], device_id=(x, y, 1 - c),
                                                device_id_type=MESH)

        for a in range(n):
            copy(a).start()
        for a in range(n):
            copy(a).wait_recv()
        for a in range(n):
            copy(a).wait_send()

    return pl.pallas_call(
        body, name="sibling_halves", in_specs=[ANY] * n, out_specs=[ANY] * n,
        out_shape=[jax.ShapeDtypeStruct((a.shape[0], a.shape[1] // 2, a.shape[2]), a.dtype) for a in parts],
        scratch_shapes=[pltpu.SemaphoreType.DMA((n,)), pltpu.SemaphoreType.DMA((n,))],
    )(*parts)


def _add_own_half(name, part, got, core):
    nchip, R, C = part.shape
    rh = R // 2
    tr = _tile(rh, 256, 2 * SUBLANES)
    nt = rh // tr

    def body(core_ref, a_ref, b_ref, o_ref):
        del core_ref
        o_ref[...] = (a_ref[...] + b_ref[...]).astype(o_ref.dtype)

    return pl.pallas_call(
        body, name=name,
        grid_spec=pltpu.PrefetchScalarGridSpec(
            num_scalar_prefetch=1, grid=(nchip, nt),
            in_specs=[pl.BlockSpec((None, tr, C), lambda j, i, core_ref: (j, core_ref[0] * nt + i, 0)),
                      pl.BlockSpec((None, tr, C), lambda j, i, core_ref: (j, i, 0))],
            out_specs=pl.BlockSpec((None, tr, C), lambda j, i, core_ref: (j, i, 0))),
        out_shape=jax.ShapeDtypeStruct((nchip, rh, C), bf16), compiler_params=_params("parallel", "parallel"),
    )(core, part, got)


def _add_own_half_w_in(part, got, core):
    _, R, C = part.shape
    rh = R // 2
    tr = _tile(rh, 256, 2 * SUBLANES)
    nt = rh // tr
    tn = 256
    per_seg = C // tn
    per_chip = IN_COLS // N_CHIPS // tn

    def src(j):
        return ((j // per_seg + N_SEG - 1) % N_SEG, j % per_seg)

    def body(core_ref, a_ref, b_ref, o_ref):
        del core_ref
        o_ref[...] = (a_ref[...] + b_ref[...]).astype(o_ref.dtype)

    return pl.pallas_call(
        body, name="add_half_w_in",
        grid_spec=pltpu.PrefetchScalarGridSpec(
            num_scalar_prefetch=1, grid=(IN_COLS // tn, nt),
            in_specs=[pl.BlockSpec((None, tr, tn), lambda j, i, core_ref: (src(j)[0], core_ref[0] * nt + i, src(j)[1])),
                      pl.BlockSpec((None, tr, tn), lambda j, i, core_ref: (src(j)[0], i, src(j)[1]))],
            out_specs=pl.BlockSpec((None, tr, tn), lambda j, i, core_ref: (j // per_chip, i, j % per_chip))),
        out_shape=jax.ShapeDtypeStruct((N_CHIPS, rh, IN_COLS // N_CHIPS), bf16), compiler_params=_params("parallel", "parallel"),
    )(core, part, got)


def _chip_exchange(sums):
    n = len(sums)

    def body(*refs):
        ins, outs = refs[:n], refs[n:2 * n]
        send, recv = refs[2 * n:]
        x, y, c, chips = _place()
        me = 2 * x + y

        def copy(a, k, slot):
            px, py = chips[k]
            return pltpu.make_async_remote_copy(src_ref=ins[a].at[2 * px + py], dst_ref=outs[a].at[slot], send_sem=send.at[3 * a + k],
                                                recv_sem=recv.at[3 * a + k], device_id=(px, py, c), device_id_type=MESH)

        for a in range(n):
            for k in range(3):
                copy(a, k, me).start()
        for a in range(n):
            for k, (px, py) in enumerate(chips):
                copy(a, k, 2 * px + py).wait_recv()
        for a in range(n):
            for k in range(3):
                copy(a, k, me).wait_send()

    return pl.pallas_call(
        body, name="chip_exchange", in_specs=[ANY] * n, out_specs=[ANY] * n,
        out_shape=[jax.ShapeDtypeStruct(a.shape, a.dtype) for a in sums],
        scratch_shapes=[pltpu.SemaphoreType.DMA((3 * n,)), pltpu.SemaphoreType.DMA((3 * n,))],
    )(*sums)


def _sum_chips(name, slots, sums, where):
    nchip, rh, C = slots.shape
    tr = _tile(rh, 256, 2 * SUBLANES)
    nt = rh // tr

    def body(where_ref, own_ref, s1_ref, s2_ref, s3_ref, o_ref):
        me = where_ref[0]
        by_dist = [r[...].astype(f32) for r in (own_ref, s1_ref, s2_ref, s3_ref)]
        acc = None
        for j in range(nchip):
            d = me ^ j
            term = jnp.where(d == 0, by_dist[0], jnp.where(d == 1, by_dist[1], jnp.where(d == 2, by_dist[2], by_dist[3])))
            acc = term if acc is None else acc + term
        o_ref[...] = acc

    def other(d):
        return pl.BlockSpec((None, tr, C), lambda i, w: (w[0] ^ d, i, 0))

    return pl.pallas_call(
        body, name=name,
        grid_spec=pltpu.PrefetchScalarGridSpec(
            num_scalar_prefetch=1, grid=(nt,),
            in_specs=[other(0), other(1), other(2), other(3)],
            out_specs=pl.BlockSpec((tr, C), lambda i, w: (w[1] * nt + i, 0))),
        out_shape=jax.ShapeDtypeStruct((2 * rh, C), f32), compiler_params=_params("parallel"),
    )(where, sums, slots, slots, slots)


def _sum_slots(name, slots):
    ns, R, C = slots.shape
    tr = _tile(R, 256)

    def body(s_ref, o_ref):
        acc = s_ref[0]
        for j in range(1, ns):
            acc = acc + s_ref[j]
        o_ref[...] = acc

    return pl.pallas_call(
        body, name=name, grid=(R // tr,), in_specs=[pl.BlockSpec((ns, tr, C), lambda i: (0, i, 0))],
        out_specs=pl.BlockSpec((tr, C), lambda i: (i, 0)), out_shape=jax.ShapeDtypeStruct((R, C), f32),
        compiler_params=_params("parallel"),
    )(slots)


def _sibling_join(fulls):
    n = len(fulls)

    def body(*refs):
        ins, outs = refs[:n], refs[n:2 * n]
        send, recv = refs[2 * n:]
        x, y, c, _ = _place()

        def copy(a, core):
            rh = ins[a].shape[0] // 2
            rows = pl.ds(core * rh, rh)
            return pltpu.make_async_remote_copy(src_ref=ins[a].at[rows, :], dst_ref=outs[a].at[rows, :], send_sem=send.at[a],
                                                recv_sem=recv.at[a], device_id=(x, y, 1 - c), device_id_type=MESH)

        for a in range(n):
            copy(a, c).start()
        for a in range(n):
            copy(a, 1 - c).wait_recv()
        for a in range(n):
            copy(a, c).wait_send()

    return pl.pallas_call(
        body, name="sibling_join", in_specs=[ANY] * n, out_specs=[ANY] * n,
        out_shape=[jax.ShapeDtypeStruct(a.shape, a.dtype) for a in fulls],
        scratch_shapes=[pltpu.SemaphoreType.DMA((n,)), pltpu.SemaphoreType.DMA((n,))],
        input_output_aliases={a: a for a in range(n)},
    )(*fulls)


def _allgather_devices(v):
    def body(v_ref, out_ref, send, recv):
        x, y, c, chips = _place()
        me, sibling = (x, y, c), (x, y, 1 - c)

        def slot(px, py, pc):
            return out_ref.at[4 * px + 2 * py + pc]

        def copy(k, block, to, src=None):
            return pltpu.make_async_remote_copy(src_ref=slot(*block) if src is None else src, dst_ref=slot(*block),
                                                send_sem=send.at[k], recv_sem=recv.at[k], device_id=to, device_id_type=MESH)

        first = [copy(0, me, sibling, src=v_ref)] + [copy(1 + j, me, (*chip, c), src=v_ref) for j, chip in enumerate(chips)]
        for cp in first:
            cp.start()
        passed = [copy(4 + j, (*chip, c), sibling) for j, chip in enumerate(chips)]
        for j, chip in enumerate(chips):
            copy(1 + j, (*chip, c), me).wait_recv()
            passed[j].start()
        copy(0, sibling, me).wait_recv()
        for j, chip in enumerate(chips):
            copy(4 + j, (*chip, 1 - c), me).wait_recv()
        for cp in first + passed:
            cp.wait_send()

    return pl.pallas_call(
        body, name="allgather_devices", in_specs=[ANY], out_specs=ANY,
        out_shape=jax.ShapeDtypeStruct((N_DEV,) + v.shape, v.dtype),
        scratch_shapes=[pltpu.SemaphoreType.DMA((N_DEV - 1,)), pltpu.SemaphoreType.DMA((N_DEV - 1,))],
    )(v)


def _adamw(name, w, g, m, v):
    R, C = w.shape
    tr = _tile(R, 256)
    c1 = 1.0 / (1.0 - ADAM_B1 ** ADAM_STEP)
    c2 = 1.0 / (1.0 - ADAM_B2 ** ADAM_STEP)

    def body(w_ref, g_ref, m_ref, v_ref, d_ref, nm_ref, nv_ref):
        gv = g_ref[...]
        nm = ADAM_B1 * m_ref[...] + (1.0 - ADAM_B1) * gv
        nv = ADAM_B2 * v_ref[...] + (1.0 - ADAM_B2) * gv * gv
        d_ref[...] = -ADAM_LR * ((nm * c1) / (jnp.sqrt(nv * c2) + ADAM_EPS) + ADAM_WD * w_ref[...])
        nm_ref[...] = nm
        nv_ref[...] = nv

    row = pl.BlockSpec((tr, C), lambda i: (i, 0))
    sh = jax.ShapeDtypeStruct((R, C), f32)
    return pl.pallas_call(body, name=name, grid=(R // tr,), in_specs=[row] * 4, out_specs=[row] * 3,
                          out_shape=[sh, sh, sh], compiler_params=_params("parallel"))(w, g, m, v)


def _zoh(lr, li, log_dt, b_re, b_im):
    dt = jnp.exp(log_dt)[:, None]
    mag = jnp.exp(lr * dt)
    ab_re = mag * jnp.cos(li * dt)
    ab_im = mag * jnp.sin(li * dt)
    den = lr * lr + li * li
    nr = ab_re - 1.0
    coef_re = (nr * lr + ab_im * li) / den
    coef_im = (ab_im * lr - nr * li) / den
    bb_re = coef_re[..., None] * b_re - coef_im[..., None] * b_im
    bb_im = coef_re[..., None] * b_im + coef_im[..., None] * b_re
    return ab_re, ab_im, bb_re, bb_im


def _s5_tables(ab_re, ab_im, bb_re, bb_im, c_re, c_im, seg):
    eye = jnp.eye(SLAB_GROUPS, dtype=f32)

    def blk_in(bb):
        return jnp.einsum("sgph,gk->sghkp", bb.reshape(N_SLAB, SLAB_GROUPS, SSM_STATE, SSM_GROUP), eye).reshape(
            N_SLAB, SLAB_CH, SLAB_NS)

    def blk_out(cc):
        return jnp.einsum("sghp,gk->skpgh", cc.reshape(N_SLAB, SLAB_GROUPS, SSM_GROUP, SSM_STATE), eye).reshape(
            N_SLAB, SLAB_NS, SLAB_CH)

    bs = jnp.concatenate([blk_in(bb_re), blk_in(bb_im)], axis=2).astype(bf16)
    cs = jnp.concatenate([blk_out(c_re), blk_out(-c_im)], axis=1).astype(bf16)
    n = SSM_GROUPS * SSM_STATE
    pw = _power_table(jnp.stack([ab_re.reshape(1, n), ab_im.reshape(1, n)]), -(-seg // SUBLANES))
    return bs, cs, pw


def _power_table(ab, tiles):
    n = ab.shape[2]

    def body(a_ref, o_ref):
        row = lax.broadcasted_iota(jnp.int32, (SUBLANES, n), 0)
        ar, ai = a_ref[0], a_ref[1]
        tr, ti = jnp.broadcast_to(ar, (SUBLANES, n)), jnp.broadcast_to(ai, (SUBLANES, n))
        pr, pi = ar, ai
        for r in range(1, SUBLANES):
            pr, pi = pr * ar - pi * ai, pr * ai + pi * ar
            tr = jnp.where(row == r, pr, tr)
            ti = jnp.where(row == r, pi, ti)
        o_ref[0, 0:SUBLANES, :] = tr
        o_ref[1, 0:SUBLANES, :] = ti

        def step(j, carry):
            cr, ci = carry
            cr, ci = cr * pr - ci * pi, cr * pi + ci * pr
            o_ref[0, _rows8(j), :] = cr
            o_ref[1, _rows8(j), :] = ci
            return cr, ci

        lax.fori_loop(1, tiles, step, (tr, ti))

    return pl.pallas_call(body, name="power_table", out_shape=jax.ShapeDtypeStruct((2, SUBLANES * tiles, n), f32))(ab)


def _s5_table_grads(dbs, dcs, da):
    eye = jnp.eye(SLAB_GROUPS, dtype=f32)
    d6 = dbs.reshape(N_SLAB, SLAB_GROUPS, SSM_GROUP, 2, SLAB_GROUPS, SSM_STATE)
    dbb = jnp.einsum("sghrkp,gk->rsgph", d6, eye).reshape(2, SSM_GROUPS, SSM_STATE, SSM_GROUP)
    c6 = dcs.reshape(N_SLAB, 2, SLAB_GROUPS, SSM_STATE, SLAB_GROUPS, SSM_GROUP)
    dcc = jnp.einsum("srkpgh,gk->rsghp", c6, eye).reshape(2, SSM_GROUPS, SSM_GROUP, SSM_STATE)
    dab = da.transpose(1, 0, 2).reshape(2, SSM_GROUPS, SSM_STATE)
    return dab[0], dab[1], dbb[0], dbb[1], dcc[0], -dcc[1]


SMALL = ["mix_norm_g", "ssm_lambda_re", "ssm_lambda_im", "ssm_log_dt", "ssm_b_re", "ssm_b_im", "ssm_c_re", "ssm_c_im",
         "ssm_d", "hgrn_lb_logits", "hgrn_norm_g", "ffn_norm_g", "conv_b", "final_norm_g"]
SHARDED_SMALL = ["meta_tokens", "conv_w"]
BIG = ["w_in", "ssm_w_glu", "w_ssm_proj", "w_hgrn_proj", "w_out", "w_up", "w_down"]
WEIGHTS = ['meta_tokens', 'mix_norm_g', 'w_in', 'ssm_lambda_re', 'ssm_lambda_im', 'ssm_log_dt', 'ssm_b_re', 'ssm_b_im',
           'ssm_c_re', 'ssm_c_im', 'ssm_d', 'ssm_w_glu', 'w_ssm_proj', 'hgrn_lb_logits', 'hgrn_norm_g', 'w_hgrn_proj',
           'w_out', 'ffn_norm_g', 'w_up', 'conv_w', 'conv_b', 'w_down', 'final_norm_g']


def _local_grads(x, tgt, meta, w, full):
    B, S, Dm = x.shape
    L = S + N_META
    T = B * L
    h0 = jnp.concatenate([jnp.broadcast_to(meta[None], (B, N_META, Dm)), x], axis=1).reshape(T, Dm)

    lb_all = jax.nn.softmax(w["hgrn_lb_logits"], axis=0)
    lb = lb_all[0:1]
    zoh_out, zoh_vjp = jax.vjp(_zoh, w["ssm_lambda_re"][0], w["ssm_lambda_im"][0], w["ssm_log_dt"][0],
                               w["ssm_b_re"][0], w["ssm_b_im"][0])
    bs, cs, pw = _s5_tables(*zoh_out, w["ssm_c_re"][0], w["ssm_c_im"][0], L // SUBLANES)

    z1 = _rmsnorm_fwd("mix_norm", h0, w["mix_norm_g"])
    p = _mm_rows("in_proj", z1, full["w_in"], "nn", f32, 1024)
    ya0 = _s5_fwd(p, bs, cs, pw, w["ssm_d"], B, L)
    gl = _mm_rows("glu_proj", ya0, full["ssm_w_glu"], "nn", f32, 1024)
    ya = _glu_fwd(ya0, gl)
    yb = _hgrn_fwd(p, lb, w["hgrn_norm_g"], B, L)
    pa = _mm_rows("ssm_proj", ya, full["w_ssm_proj"], "nn", f32, 1024)
    pb = _mm_rows("hgrn_proj", yb, full["w_hgrn_proj"], "nn", f32, 1024)
    merged = _merge_fwd(p, pa, pb)
    h1 = _mm_rows("out_proj", merged, full["w_out"], "nn", f32, 1024, res=h0)
    z2 = _rmsnorm_fwd("ffn_norm", h1, w["ffn_norm_g"])
    up = _mm_rows("up_proj", z2, full["w_up"], "nn", f32, D_FF // 2)
    ff = _conv_fwd(up, full["conv_w"], w["conv_b"], B, L)
    h2 = _mm_rows("down_proj", ff, full["w_down"], "nn", f32, 1024, res=h1, tk=D_FF // 2)

    h2x = h2.reshape(B, L, Dm)[:, N_META:].reshape(B * S, Dm)
    dh2x, loss, d_final_g = _final_loss(h2x, tgt.reshape(B * S, Dm), w["final_norm_g"].reshape(1, Dm))
    dh2 = jnp.pad(dh2x.reshape(B, S, Dm), ((0, 0), (N_META, 0), (0, 0))).reshape(T, Dm)

    dff = _mm_rows("d_ff", dh2, full["w_down"], "nt", f32, D_FF // 2)
    g_w_down = _mm_wgrad("dw_down", ff, dh2, tn=512)
    dup, dconv = _conv_bwd(up, dff, full["conv_w"], w["conv_b"], B, L)
    dz2 = _dz2(dup, full["w_up"])
    g_w_up = _dw_up(z2, dup)
    dh1, d_ffn_g = _rmsnorm_bwd("ffn_norm_bwd", h1, w["ffn_norm_g"], dz2, dh2)

    dmerged = _mm_rows("d_merged", dh1, full["w_out"], "nt", f32, 1024)
    g_w_out = _mm_wgrad("dw_out", merged, dh1)
    dpa, dpb, dp = _merge_bwd(dmerged, p, pa, pb)
    dya = _mm_rows("d_ya", dpa, full["w_ssm_proj"], "nt", f32, 1024)
    g_w_ssm_proj = _mm_wgrad("dw_ssm_proj", ya, dpa)
    dyb = _mm_rows("d_yb", dpb, full["w_hgrn_proj"], "nt", f32, 1024)
    g_w_hgrn_proj = _mm_wgrad("dw_hgrn_proj", yb, dpb)
    dp, d_lb, d_hgrn_g = _hgrn_bwd(p, dyb, dp, lb, w["hgrn_norm_g"], B, L)
    dgl, dya0_direct = _glu_bwd(dya, ya0, gl)
    dya0 = _mm_rows("d_ya0", dgl, full["ssm_w_glu"], "nt", f32, 1024, res=dya0_direct)
    g_w_glu = _mm_wgrad("dw_glu", ya0, dgl)
    dp, dbs, dcs, da, d_skip = _s5_bwd(p, dya0, dp, bs, cs, pw, w["ssm_d"], B, L)
    dz1 = _dz1(dp, full["w_in"])
    g_w_in = _dw_in(z1, dp)
    dh0, d_mix_g = _rmsnorm_bwd("mix_norm_bwd", h0, w["mix_norm_g"], dz1, dh1)

    dh0 = dh0.reshape(B, L, Dm)
    grad_x = dh0[:, N_META:]
    d_meta = _meta_grad(dh0[:, :N_META])

    d_ab_re, d_ab_im, d_bb_re, d_bb_im, d_c_re, d_c_im = _s5_table_grads(dbs, dcs, da)
    d_lr, d_li, d_log_dt, d_b_re, d_b_im = zoh_vjp((d_ab_re, d_ab_im, d_bb_re, d_bb_im))
    sm0, sm1 = lb_all[0:1], lb_all[1:2]
    d_logits = jnp.concatenate([sm0 * (1.0 - sm0) * d_lb, -sm0 * sm1 * d_lb], axis=0)
    small = {
        "meta_tokens": d_meta, "mix_norm_g": d_mix_g, "ssm_lambda_re": d_lr[None], "ssm_lambda_im": d_li[None],
        "ssm_log_dt": d_log_dt[None], "ssm_b_re": d_b_re[None], "ssm_b_im": d_b_im[None], "ssm_c_re": d_c_re[None],
        "ssm_c_im": d_c_im[None], "ssm_d": d_skip, "hgrn_lb_logits": d_logits, "hgrn_norm_g": d_hgrn_g,
        "ffn_norm_g": d_ffn_g, "conv_w": dconv[:, 0:3, :].transpose(1, 0, 2).reshape(3, 2 * D_FF),
        "conv_b": dconv[:, 3, :].reshape(1, 2 * D_FF), "final_norm_g": d_final_g.reshape(Dm),
    }
    big = {
        "w_in": g_w_in, "ssm_w_glu": g_w_glu.reshape(N_CHIPS, Dm // N_CHIPS, Dm),
        "w_ssm_proj": g_w_ssm_proj.reshape(N_CHIPS, Dm // N_CHIPS, Dm),
        "w_hgrn_proj": g_w_hgrn_proj.reshape(N_CHIPS, Dm // N_CHIPS, Dm), "w_out": g_w_out.reshape(N_CHIPS, Dm // N_CHIPS, Dm),
        "w_up": g_w_up, "w_down": g_w_down.reshape(N_CHIPS, D_FF // N_CHIPS, Dm),
    }
    return loss, grad_x, big, small


PACK_ROWS = 256


def _pack(parts):
    flat = jnp.concatenate([parts[k].reshape(-1) for k in parts])
    n = flat.shape[0]
    rows = -(-n // (PACK_ROWS * LANES)) * PACK_ROWS
    flat = jnp.pad(flat, (0, rows * LANES - n))
    return flat.reshape(rows, LANES)


def _unpack(packed, like):
    flat = packed.reshape(-1)
    out, o = {}, 0
    for k, ref in like.items():
        n = math.prod(ref.shape)
        out[k] = flat[o:o + n].reshape(ref.shape)
        o += n
    return out


def kernel(x, meta_tokens, mix_norm_g, w_in, ssm_lambda_re, ssm_lambda_im, ssm_log_dt, ssm_b_re, ssm_b_im, ssm_c_re, ssm_c_im, ssm_d, ssm_w_glu, w_ssm_proj, hgrn_lb_logits, hgrn_norm_g, w_hgrn_proj, w_out, ffn_norm_g, w_up, conv_w, conv_b, w_down, final_norm_g, loss_target, m_meta_tokens, m_mix_norm_g, m_w_in, m_ssm_lambda_re, m_ssm_lambda_im, m_ssm_log_dt, m_ssm_b_re, m_ssm_b_im, m_ssm_c_re, m_ssm_c_im, m_ssm_d, m_ssm_w_glu, m_w_ssm_proj, m_hgrn_lb_logits, m_hgrn_norm_g, m_w_hgrn_proj, m_w_out, m_ffn_norm_g, m_w_up, m_conv_w, m_conv_b, m_w_down, m_final_norm_g, v_meta_tokens, v_mix_norm_g, v_w_in, v_ssm_lambda_re, v_ssm_lambda_im, v_ssm_log_dt, v_ssm_b_re, v_ssm_b_im, v_ssm_c_re, v_ssm_c_im, v_ssm_d, v_ssm_w_glu, v_w_ssm_proj, v_hgrn_lb_logits, v_hgrn_norm_g, v_w_hgrn_proj, v_w_out, v_ffn_norm_g, v_w_up, v_conv_w, v_conv_b, v_w_down, v_final_norm_g):
    args = dict(locals())
    w = {k: args[k] for k in WEIGHTS}
    mom = {k: args["m_" + k] for k in WEIGHTS}
    var = {k: args["v_" + k] for k in WEIGHTS}
    Dm = D_MODEL
    cx, cy, cc = lax.axis_index("x"), lax.axis_index("y"), lax.axis_index("c")
    chip = 2 * cx + cy

    shards = [w[k][0].astype(bf16) for k in BIG]
    gathered = _allgather_split(shards)
    g_in, g_glu, g_sp, g_hp, g_out, g_up, g_down = [
        lax.dynamic_update_slice(g, s[None], (chip, 0, 0)) for g, s in zip(gathered, shards)]
    g_meta, g_cw = _allgather_chips([w["meta_tokens"], w["conv_w"][0]])
    w_in_full = jnp.roll(g_in.transpose(1, 0, 2).reshape(Dm, IN_COLS), -Dm, axis=1)
    full = {
        "w_in": w_in_full, "ssm_w_glu": g_glu.reshape(Dm, Dm), "w_ssm_proj": g_sp.reshape(Dm, Dm),
        "w_hgrn_proj": g_hp.reshape(Dm, Dm), "w_out": g_out.reshape(Dm, Dm),
        "w_up": g_up.transpose(1, 0, 2).reshape(Dm, 2 * D_FF), "w_down": g_down.reshape(D_FF, Dm),
        "conv_w": g_cw.transpose(1, 0, 2).reshape(3, 2 * D_FF),
    }
    meta_full = g_meta.transpose(1, 0, 2).reshape(N_META, Dm)

    loss_part, grad_x, big, small = _local_grads(x, loss_target, meta_full, w, full)

    core = cc.reshape(1).astype(jnp.int32)
    parts = [big[k] for k in BIG]
    got = _sibling_halves(parts)
    sums = [_add_own_half_w_in(pt, gt, core) if k == "w_in" else _add_own_half("add_half_" + k, pt, gt, core)
            for k, pt, gt in zip(BIG, parts, got)]
    slots = _chip_exchange(sums)
    where = jnp.stack([chip, cc]).astype(jnp.int32)
    fulls = [_sum_chips("sum_chips_" + k, sl, sm, where) for k, sl, sm in zip(BIG, slots, sums)]
    g_big = dict(zip(BIG, _sibling_join(fulls)))

    small_all = dict(small)
    small_all["loss"] = loss_part[0, 0:1]
    packed = _pack(small_all)
    slots_dev = lax.dynamic_update_slice(_allgather_devices(packed), packed[None], (2 * chip + cc, 0, 0))
    reduced = _unpack(_sum_slots("sum_devices", slots_dev), small_all)
    loss = reduced.pop("loss")[0]
    mcols = Dm // N_CHIPS
    ccols = 2 * D_FF // N_CHIPS
    grads = {k: reduced[k] for k in SMALL}
    grads["meta_tokens"] = lax.dynamic_slice(reduced["meta_tokens"], (0, chip * mcols), (N_META, mcols))
    grads["conv_w"] = lax.dynamic_slice(reduced["conv_w"], (0, chip * ccols), (3, ccols))[None]
    for k in BIG:
        grads[k] = g_big[k][None]

    delta, new_m, new_v = {}, {}, {}
    for k in BIG:
        shp = w[k].shape
        d, nm, nv = _adamw("adamw_" + k, w[k][0], grads[k][0], mom[k][0], var[k][0])
        delta[k], new_m[k], new_v[k] = d.reshape(shp), nm.reshape(shp), nv.reshape(shp)
    rest = SMALL + SHARDED_SMALL
    pk = [_pack({k: t[k] for k in rest}) for t in (w, grads, mom, var)]
    outs = _adamw("adamw_small", *pk)
    like = {k: w[k] for k in rest}
    for dst, o in zip((delta, new_m, new_v), outs):
        dst.update(_unpack(o, like))

    return (loss, grad_x, *[grads[k].reshape(w[k].shape) for k in WEIGHTS], *[delta[k] for k in WEIGHTS],
            *[new_m[k] for k in WEIGHTS], *[new_v[k] for k in WEIGHTS])
```

```python
import functools
import math

import jax
import jax.numpy as jnp
from jax import lax
from jax.experimental import pallas as pl
from jax.experimental.pallas import tpu as pltpu

f32 = jnp.float32
bf16 = jnp.bfloat16

D_MODEL = 1024
N_META = 16
SSM_GROUP = 16
SSM_GROUPS = 64
SSM_STATE = 64
SLAB_GROUPS = 8
N_SLAB = SSM_GROUPS // SLAB_GROUPS
SLAB_CH = SLAB_GROUPS * SSM_GROUP
SLAB_NS = SLAB_GROUPS * SSM_STATE
HEADS = 8
HEAD_DIM = 128
CHUNK = 16
D_FF = 2816
IN_COLS = 7168
EPS = 1e-6
SUBLANES = 8
LANES = 128
N_CHIPS = 4
N_DEV = 8
ADAM_LR, ADAM_B1, ADAM_B2, ADAM_EPS, ADAM_WD, ADAM_STEP = 0.001, 0.9, 0.999, 1e-08, 0.01, 10
MESH = pl.DeviceIdType.MESH
ANY = pl.BlockSpec(memory_space=pl.ANY)

SEG_Q, SEG_F, SEG_I, SEG_OG, SEG_GA, SEG_GB, SEG_U = range(7)
N_SEG = 7


def _tile(n, target, mult=SUBLANES):
    best = None
    for d in range(mult, min(n, target) + 1, mult):
        if n % d == 0:
            best = d
    return n if best is None else best


def _params(*sem):
    return pltpu.CompilerParams(dimension_semantics=sem)


def _sigmoid(x):
    return 1.0 / (1.0 + jnp.exp(-x))


_DIMS = {"nn": (((1,), (0,)), ((), ())), "nt": (((1,), (1,)), ((), ())), "tn": (((0,), (0,)), ((), ()))}


def _mm(name, a, b, dims, grid, a_spec, b_spec, out_shape, out_spec, acc_shape, res=None, res_spec=None):
    nk = grid[2]
    dn = _DIMS[dims]

    def body(*refs):
        if res is None:
            a_ref, b_ref, o_ref, acc = refs
        else:
            a_ref, b_ref, r_ref, o_ref, acc = refs
        k = pl.program_id(2)

        @pl.when(k == 0)
        def _():
            acc[...] = jnp.zeros_like(acc)

        acc[...] += lax.dot_general(a_ref[...].astype(bf16), b_ref[...].astype(bf16), dn, preferred_element_type=f32)

        @pl.when(k == nk - 1)
        def _():
            r = acc[...]
            if res is not None:
                r = r + r_ref[...]
            o_ref[...] = r.astype(o_ref.dtype)

    ins = [a, b] + ([] if res is None else [res])
    specs = [a_spec, b_spec] + ([] if res is None else [res_spec])
    return pl.pallas_call(
        body, name=name, grid=grid, in_specs=specs, out_specs=out_spec, out_shape=out_shape,
        scratch_shapes=[pltpu.VMEM(acc_shape, f32)],
        compiler_params=_params("parallel", "parallel", "arbitrary"),
    )(*ins)


def _mm_rows(name, a, w, dims, out_dtype, tn, res=None, tk=None):
    T, K = a.shape
    N = w.shape[1] if dims == "nn" else w.shape[0]
    tm = _tile(T, 1032)
    tk = K if tk is None else tk
    grid = (T // tm, N // tn, K // tk)
    a_spec = pl.BlockSpec((tm, tk), lambda i, j, k: (i, k))
    if dims == "nn":
        b_spec = pl.BlockSpec((tk, tn), lambda i, j, k: (k, j))
    else:
        b_spec = pl.BlockSpec((tn, tk), lambda i, j, k: (j, k))
    o_spec = pl.BlockSpec((tm, tn), lambda i, j, k: (i, j))
    return _mm(name, a, w, dims, grid, a_spec, b_spec, jax.ShapeDtypeStruct((T, N), out_dtype), o_spec, (tm, tn),
               res=res, res_spec=None if res is None else o_spec)


def _mm_wgrad(name, a, g, tn=None):
    T, K = a.shape
    N = g.shape[1]
    tk = _tile(T, 688)
    tn = N if tn is None else tn
    grid = (1, N // tn, T // tk)
    a_spec = pl.BlockSpec((tk, K), lambda i, j, k: (k, 0))
    g_spec = pl.BlockSpec((tk, tn), lambda i, j, k: (k, j))
    o_spec = pl.BlockSpec((K, tn), lambda i, j, k: (0, j))
    return _mm(name, a, g, "tn", grid, a_spec, g_spec, jax.ShapeDtypeStruct((K, N), f32), o_spec, (K, tn))


def _rmsnorm_fwd(name, x, g):
    T, Dm = x.shape
    tr = _tile(T, 688)

    def body(x_ref, g_ref, z_ref):
        xv = x_ref[...]
        r = lax.rsqrt(jnp.mean(xv * xv, axis=-1, keepdims=True) + EPS)
        z_ref[...] = (xv * r * g_ref[...]).astype(z_ref.dtype)

    return pl.pallas_call(
        body, name=name, grid=(T // tr,),
        in_specs=[pl.BlockSpec((tr, Dm), lambda i: (i, 0)), pl.BlockSpec((1, Dm), lambda i: (0, 0))],
        out_specs=pl.BlockSpec((tr, Dm), lambda i: (i, 0)),
        out_shape=jax.ShapeDtypeStruct((T, Dm), bf16), compiler_params=_params("parallel"),
    )(x, g)


def _rmsnorm_bwd(name, x, g, dz, dres):
    T, Dm = x.shape
    tr = _tile(T, 688)

    def body(x_ref, g_ref, dz_ref, dres_ref, dx_ref, dg_ref):
        xv = x_ref[...]
        r = lax.rsqrt(jnp.mean(xv * xv, axis=-1, keepdims=True) + EPS)
        xn = xv * r
        dzv = dz_ref[...]
        dzg = dzv * g_ref[...]
        dx_ref[...] = dres_ref[...] + r * (dzg - xn * jnp.mean(dzg * xn, axis=-1, keepdims=True))

        @pl.when(pl.program_id(0) == 0)
        def _():
            dg_ref[...] = jnp.zeros_like(dg_ref)

        dg_ref[...] += jnp.sum(dzv * xn, axis=0, keepdims=True)

    row = pl.BlockSpec((tr, Dm), lambda i: (i, 0))
    par = pl.BlockSpec((1, Dm), lambda i: (0, 0))
    return pl.pallas_call(
        body, name=name, grid=(T // tr,), in_specs=[row, par, row, row], out_specs=[row, par],
        out_shape=[jax.ShapeDtypeStruct((T, Dm), f32), jax.ShapeDtypeStruct((1, Dm), f32)],
        compiler_params=_params("arbitrary"),
    )(x, g, dz, dres)


def _glu_fwd(ya0, gl):
    T, Dm = ya0.shape
    tr = _tile(T, 688)

    def body(y_ref, g_ref, o_ref):
        o_ref[...] = (y_ref[...] * _sigmoid(g_ref[...])).astype(o_ref.dtype)

    row = pl.BlockSpec((tr, Dm), lambda i: (i, 0))
    return pl.pallas_call(body, name="glu_fwd", grid=(T // tr,), in_specs=[row, row], out_specs=row,
                          out_shape=jax.ShapeDtypeStruct((T, Dm), bf16), compiler_params=_params("parallel"))(ya0, gl)


def _glu_bwd(dya, ya0, gl):
    T, Dm = ya0.shape
    tr = _tile(T, 688)

    def body(d_ref, y_ref, g_ref, dg_ref, dy_ref):
        s = _sigmoid(g_ref[...])
        d = d_ref[...]
        dg_ref[...] = (d * y_ref[...] * s * (1.0 - s)).astype(dg_ref.dtype)
        dy_ref[...] = d * s

    row = pl.BlockSpec((tr, Dm), lambda i: (i, 0))
    return pl.pallas_call(body, name="glu_bwd", grid=(T // tr,), in_specs=[row, row, row], out_specs=[row, row],
                          out_shape=[jax.ShapeDtypeStruct((T, Dm), bf16), jax.ShapeDtypeStruct((T, Dm), f32)],
                          compiler_params=_params("parallel"))(dya, ya0, gl)


def _merge_fwd(p, pa, pb):
    T, Dm = pa.shape
    tr = _tile(T, 688)

    def body(ga_ref, gb_ref, pa_ref, pb_ref, o_ref):
        o_ref[...] = (_sigmoid(ga_ref[...]) * pa_ref[...] + _sigmoid(gb_ref[...]) * pb_ref[...]).astype(o_ref.dtype)

    row = pl.BlockSpec((tr, Dm), lambda i: (i, 0))
    return pl.pallas_call(
        body, name="merge_fwd", grid=(T // tr,),
        in_specs=[pl.BlockSpec((tr, Dm), lambda i: (i, SEG_GA)), pl.BlockSpec((tr, Dm), lambda i: (i, SEG_GB)), row, row],
        out_specs=row, out_shape=jax.ShapeDtypeStruct((T, Dm), bf16), compiler_params=_params("parallel"),
    )(p, p, pa, pb)


def _merge_bwd(dm, p, pa, pb):
    T, Dm = pa.shape
    tr = _tile(T, 688)

    def body(dm_ref, ga_ref, gb_ref, pa_ref, pb_ref, dpa_ref, dpb_ref, dp_ref):
        d = dm_ref[...]
        sa = _sigmoid(ga_ref[...])
        sb = _sigmoid(gb_ref[...])
        dpa_ref[...] = (d * sa).astype(dpa_ref.dtype)
        dpb_ref[...] = (d * sb).astype(dpb_ref.dtype)
        dp_ref[0] = (d * pa_ref[...] * sa * (1.0 - sa)).astype(dp_ref.dtype)
        dp_ref[1] = (d * pb_ref[...] * sb * (1.0 - sb)).astype(dp_ref.dtype)

    row = pl.BlockSpec((tr, Dm), lambda i: (i, 0))
    return pl.pallas_call(
        body, name="merge_bwd", grid=(T // tr,),
        in_specs=[row, pl.BlockSpec((tr, Dm), lambda i: (i, SEG_GA)), pl.BlockSpec((tr, Dm), lambda i: (i, SEG_GB)), row, row],
        out_specs=[row, row, pl.BlockSpec((2, tr, Dm), lambda i: (SEG_GA // 2, i, 0))],
        out_shape=[jax.ShapeDtypeStruct((T, Dm), bf16), jax.ShapeDtypeStruct((T, Dm), bf16),
                   jax.ShapeDtypeStruct((N_SEG, T, Dm), bf16)],
        compiler_params=_params("parallel"),
    )(dm, p, p, pa, pb)


def _final_loss(h2x, tgt, g):
    T, Dm = h2x.shape
    tr = _tile(T, 512)

    def body(h_ref, t_ref, g_ref, dh_ref, loss_ref, dg_ref):
        hv = h_ref[...]
        r = lax.rsqrt(jnp.mean(hv * hv, axis=-1, keepdims=True) + EPS)
        xn = hv * r
        gv = g_ref[...]
        err = xn * gv - t_ref[...]
        dy = err * (1.0 / Dm)
        dyg = dy * gv
        dh_ref[...] = r * (dyg - xn * jnp.mean(dyg * xn, axis=-1, keepdims=True))

        @pl.when(pl.program_id(0) == 0)
        def _():
            dg_ref[...] = jnp.zeros_like(dg_ref)
            loss_ref[...] = jnp.zeros_like(loss_ref)

        dg_ref[...] += jnp.sum(dy * xn, axis=0, keepdims=True)
        loss_ref[...] += jnp.sum(err * err) * (0.5 / Dm)

    row = pl.BlockSpec((tr, Dm), lambda i: (i, 0))
    par = pl.BlockSpec((1, Dm), lambda i: (0, 0))
    return pl.pallas_call(
        body, name="final_loss", grid=(T // tr,), in_specs=[row, row, par],
        out_specs=[row, pl.BlockSpec((1, LANES), lambda i: (0, 0)), par],
        out_shape=[jax.ShapeDtypeStruct((T, Dm), f32), jax.ShapeDtypeStruct((1, LANES), f32), jax.ShapeDtypeStruct((1, Dm), f32)],
        compiler_params=_params("arbitrary"),
    )(h2x, tgt, g)


def _meta_grad(dh0_meta):
    B = dh0_meta.shape[0]

    def body(d_ref, o_ref):
        acc = d_ref[0]
        for b in range(1, B):
            acc = acc + d_ref[b]
        o_ref[...] = acc

    return pl.pallas_call(body, name="meta_grad", out_shape=jax.ShapeDtypeStruct(dh0_meta.shape[1:], f32))(dh0_meta)


def _shift_down(x, k, row):
    return jnp.where(row >= k, pltpu.roll(x, k, 0), 0.0)


def _shift_up(x, k, row):
    n = x.shape[0]
    return jnp.where(row < n - k, pltpu.roll(x, n - k, 0), 0.0)


def _conv_fwd(up, conv_w, conv_b, B, L):
    tc = 256
    nt = D_FF // tc

    def body(xa_ref, xb_ref, wa_ref, wb_ref, ba_ref, bb_ref, o_ref):
        row = lax.broadcasted_iota(jnp.int32, (L, tc), 0)

        def conv(x_ref, w_ref, b_ref):
            x = x_ref[...]
            return (b_ref[...] + w_ref[0:1, :] * _shift_down(x, 2, row) + w_ref[1:2, :] * _shift_down(x, 1, row)
                    + w_ref[2:3, :] * x)

        a = conv(xa_ref, wa_ref, ba_ref)
        b = conv(xb_ref, wb_ref, bb_ref)
        o_ref[...] = (a * _sigmoid(a) * b).astype(o_ref.dtype)

    return pl.pallas_call(
        body, name="conv_fwd", grid=(B, nt),
        in_specs=[pl.BlockSpec((L, tc), lambda b, j: (b, j)), pl.BlockSpec((L, tc), lambda b, j: (b, j + nt)),
                  pl.BlockSpec((3, tc), lambda b, j: (0, j)), pl.BlockSpec((3, tc), lambda b, j: (0, j + nt)),
                  pl.BlockSpec((1, tc), lambda b, j: (0, j)), pl.BlockSpec((1, tc), lambda b, j: (0, j + nt))],
        out_specs=pl.BlockSpec((L, tc), lambda b, j: (b, j)),
        out_shape=jax.ShapeDtypeStruct((B * L, D_FF), bf16), compiler_params=_params("parallel", "parallel"),
    )(up, up, conv_w, conv_w, conv_b, conv_b)


def _conv_bwd(up, dff, conv_w, conv_b, B, L):
    tc = 256
    nt = D_FF // tc

    def body(xa_ref, xb_ref, d_ref, wa_ref, wb_ref, ba_ref, bb_ref, dup_ref, dw_ref):
        row = lax.broadcasted_iota(jnp.int32, (L, tc), 0)
        xs, pre = [], []
        for x_ref, w_ref, b_ref in ((xa_ref, wa_ref, ba_ref), (xb_ref, wb_ref, bb_ref)):
            x = x_ref[...]
            x1 = _shift_down(x, 1, row)
            x2 = _shift_down(x, 2, row)
            xs.append((x, x1, x2))
            pre.append(b_ref[...] + w_ref[0:1, :] * x2 + w_ref[1:2, :] * x1 + w_ref[2:3, :] * x)
        a, b = pre
        s = _sigmoid(a)
        d = d_ref[...]
        grads = (d * b * s * (1.0 + a * (1.0 - s)), d * a * s)

        @pl.when(pl.program_id(1) == 0)
        def _():
            dw_ref[...] = jnp.zeros_like(dw_ref)

        for h, (gr, (x, x1, x2), w_ref) in enumerate(zip(grads, xs, (wa_ref, wb_ref))):
            dup_ref[h] = (w_ref[2:3, :] * gr + w_ref[1:2, :] * _shift_up(gr, 1, row)
                          + w_ref[0:1, :] * _shift_up(gr, 2, row)).astype(dup_ref.dtype)
            dw_ref[h, 0:1, :] += jnp.sum(gr * x2, axis=0, keepdims=True)
            dw_ref[h, 1:2, :] += jnp.sum(gr * x1, axis=0, keepdims=True)
            dw_ref[h, 2:3, :] += jnp.sum(gr * x, axis=0, keepdims=True)
            dw_ref[h, 3:4, :] += jnp.sum(gr, axis=0, keepdims=True)

    return pl.pallas_call(
        body, name="conv_bwd", grid=(nt, B),
        in_specs=[pl.BlockSpec((L, tc), lambda j, b: (b, j)), pl.BlockSpec((L, tc), lambda j, b: (b, j + nt)),
                  pl.BlockSpec((L, tc), lambda j, b: (b, j)),
                  pl.BlockSpec((3, tc), lambda j, b: (0, j)), pl.BlockSpec((3, tc), lambda j, b: (0, j + nt)),
                  pl.BlockSpec((1, tc), lambda j, b: (0, j)), pl.BlockSpec((1, tc), lambda j, b: (0, j + nt))],
        out_specs=[pl.BlockSpec((2, L, tc), lambda j, b: (0, b, j)), pl.BlockSpec((2, SUBLANES, tc), lambda j, b: (0, 0, j))],
        out_shape=[jax.ShapeDtypeStruct((2, B * L, D_FF), bf16), jax.ShapeDtypeStruct((2, SUBLANES, D_FF), f32)],
        compiler_params=_params("parallel", "arbitrary"),
    )(up, up, dff, conv_w, conv_w, conv_b, conv_b)


CONV_ROWS = 2 * SUBLANES


def _rows16(i):
    return pl.ds(pl.multiple_of(i * CONV_ROWS, CONV_ROWS), CONV_ROWS)


def _conv_taps(x_ref, i, row):
    x = x_ref[_rows16(i), :]
    live = jnp.where(i > 0, 1.0, 0.0)
    r0 = jnp.maximum(i * CONV_ROWS, 2)
    p1 = x_ref[pl.ds(r0 - 1, 1), :] * live
    p2 = x_ref[pl.ds(r0 - 2, 1), :] * live
    x1 = jnp.where(row == 0, p1, pltpu.roll(x, 1, 0))
    x2 = jnp.where(row == 0, p2, jnp.where(row == 1, p1, pltpu.roll(x, 2, 0)))
    return x, x1, x2


def _conv_bwd(up, dff, conv_w, conv_b, B, L):
    tc = 256
    nt = D_FF // tc
    n = L // CONV_ROWS

    def body(xa_ref, xb_ref, d_ref, wa_ref, wb_ref, ba_ref, bb_ref, dup_ref, dw_ref, ga_ref, gb_ref):
        row = lax.broadcasted_iota(jnp.int32, (CONV_ROWS, tc), 0)

        @pl.when(pl.program_id(1) == 0)
        def _():
            dw_ref[...] = jnp.zeros_like(dw_ref)

        zero_tail = jnp.zeros((CONV_ROWS, tc), f32)
        ga_ref[L:L + CONV_ROWS, :] = zero_tail
        gb_ref[L:L + CONV_ROWS, :] = zero_tail

        def fold(v):
            return v[0:SUBLANES, :] + v[SUBLANES:CONV_ROWS, :]

        def step(i, acc):
            taps_a = _conv_taps(xa_ref, i, row)
            taps_b = _conv_taps(xb_ref, i, row)
            a = ba_ref[...] + wa_ref[0:1, :] * taps_a[2] + wa_ref[1:2, :] * taps_a[1] + wa_ref[2:3, :] * taps_a[0]
            b = bb_ref[...] + wb_ref[0:1, :] * taps_b[2] + wb_ref[1:2, :] * taps_b[1] + wb_ref[2:3, :] * taps_b[0]
            s = _sigmoid(a)
            d = d_ref[_rows16(i), :]
            g_a = d * b * s * (1.0 + a * (1.0 - s))
            g_b = d * a * s
            ga_ref[_rows16(i), :] = g_a
            gb_ref[_rows16(i), :] = g_b
            new = []
            for g, (x, x1, x2) in ((g_a, taps_a), (g_b, taps_b)):
                new += [fold(g * x2), fold(g * x1), fold(g * x), fold(g)]
            return tuple(o + v for o, v in zip(acc, new))

        z = jnp.zeros((SUBLANES, tc), f32)
        acc = _repeat_loop(n, step, (z,) * 8)
        for h in range(2):
            for t in range(4):
                dw_ref[h, t:t + 1, :] += jnp.sum(acc[4 * h + t], axis=0, keepdims=True)

        def back(i, c):
            for h, (g_ref, w_ref) in enumerate(((ga_ref, wa_ref), (gb_ref, wb_ref))):
                g = g_ref[_rows16(i), :]
                n1 = g_ref[pl.ds(i * CONV_ROWS + CONV_ROWS, 1), :]
                n2 = g_ref[pl.ds(i * CONV_ROWS + CONV_ROWS + 1, 1), :]
                u1 = jnp.where(row == CONV_ROWS - 1, n1, pltpu.roll(g, CONV_ROWS - 1, 0))
                u2 = jnp.where(row == CONV_ROWS - 1, n2, jnp.where(row == CONV_ROWS - 2, n1, pltpu.roll(g, CONV_ROWS - 2, 0)))
                dup_ref[h, _rows16(i), :] = (w_ref[2:3, :] * g + w_ref[1:2, :] * u1 + w_ref[0:1, :] * u2).astype(dup_ref.dtype)
            return c

        _repeat_loop(n, back, 0)

    return pl.pallas_call(
        body, name="conv_bwd", grid=(nt, B),
        in_specs=[pl.BlockSpec((L, tc), lambda j, b: (b, j)), pl.BlockSpec((L, tc), lambda j, b: (b, j + nt)),
                  pl.BlockSpec((L, tc), lambda j, b: (b, j)),
                  pl.BlockSpec((3, tc), lambda j, b: (0, j)), pl.BlockSpec((3, tc), lambda j, b: (0, j + nt)),
                  pl.BlockSpec((1, tc), lambda j, b: (0, j)), pl.BlockSpec((1, tc), lambda j, b: (0, j + nt))],
        out_specs=[pl.BlockSpec((2, L, tc), lambda j, b: (0, b, j)), pl.BlockSpec((2, SUBLANES, tc), lambda j, b: (0, 0, j))],
        out_shape=[jax.ShapeDtypeStruct((2, B * L, D_FF), bf16), jax.ShapeDtypeStruct((2, SUBLANES, D_FF), f32)],
        scratch_shapes=[pltpu.VMEM((L + CONV_ROWS, tc), f32), pltpu.VMEM((L + CONV_ROWS, tc), f32)],
        compiler_params=_params("parallel", "arbitrary"),
    )(up, up, dff, conv_w, conv_w, conv_b, conv_b)


GELU_C = math.sqrt(2.0 / math.pi)
GELU_A = 0.044715


def _gelu(x):
    return 0.5 * x * (1.0 + jnp.tanh(GELU_C * (x + GELU_A * x * x * x)))


def _gelu_grad(x):
    t = jnp.tanh(GELU_C * (x + GELU_A * x * x * x))
    return 0.5 * (1.0 + t) + 0.5 * x * (1.0 - t * t) * GELU_C * (1.0 + 3.0 * GELU_A * x * x)


def _cmul_add(xr, xi, ar, ai, sr, si):
    return xr + ar * sr - ai * si, xi + ar * si + ai * sr


def _s5_scan_fwd(s_ref, pw_ref, L):
    ns = SLAB_NS
    row = lax.broadcasted_iota(jnp.int32, (SUBLANES, ns), 0)
    pr = pw_ref[0, 0:SUBLANES, :]
    pi = pw_ref[1, 0:SUBLANES, :]

    def step(i, carry):
        cr, ci = carry
        r0 = pl.multiple_of(i * SUBLANES, SUBLANES)
        xr = s_ref[pl.ds(r0, SUBLANES), 0:ns]
        xi = s_ref[pl.ds(r0, SUBLANES), ns:2 * ns]
        for k in (1, 2, 4):
            xr, xi = _cmul_add(xr, xi, pr[k - 1:k, :], pi[k - 1:k, :], _shift_down(xr, k, row), _shift_down(xi, k, row))
        xr, xi = _cmul_add(xr, xi, pr, pi, cr, ci)
        s_ref[pl.ds(r0, SUBLANES), 0:ns] = xr
        s_ref[pl.ds(r0, SUBLANES), ns:2 * ns] = xi
        return xr[SUBLANES - 1:SUBLANES, :], xi[SUBLANES - 1:SUBLANES, :]

    z = jnp.zeros((1, ns), f32)
    lax.fori_loop(0, L // SUBLANES, step, (z, z))


def _s5_project_in(u_ref, bs_ref, s_ref, L, rc):
    for r in range(0, L, rc):
        s_ref[r:r + rc, :] = jnp.dot(u_ref[r:r + rc, :].astype(bf16), bs_ref[...], preferred_element_type=f32)


def _s5_fwd(p, bs, cs, pw, d_skip, B, L):
    rc = _tile(L, 344)

    def body(u_ref, bs_ref, cs_ref, pw_ref, d_ref, y_ref, s_ref):
        _s5_project_in(u_ref, bs_ref, s_ref, L, rc)
        _s5_scan_fwd(s_ref, pw_ref, L)
        for r in range(0, L, rc):
            ypre = (jnp.dot(s_ref[r:r + rc, :].astype(bf16), cs_ref[...], preferred_element_type=f32)
                    + d_ref[...] * u_ref[r:r + rc, :])
            y_ref[r:r + rc, :] = _gelu(ypre)

    ucol = SEG_U * (D_MODEL // SLAB_CH)
    return pl.pallas_call(
        body, name="s5_fwd", grid=(B, N_SLAB),
        in_specs=[pl.BlockSpec((L, SLAB_CH), lambda b, s: (b, ucol + s)),
                  pl.BlockSpec((None, SLAB_CH, 2 * SLAB_NS), lambda b, s: (s, 0, 0)),
                  pl.BlockSpec((None, 2 * SLAB_NS, SLAB_CH), lambda b, s: (s, 0, 0)),
                  pl.BlockSpec((None, 2, 2 * SUBLANES, SLAB_NS), lambda b, s: (s, 0, 0, 0)),
                  pl.BlockSpec((1, SLAB_CH), lambda b, s: (0, s))],
        out_specs=pl.BlockSpec((L, SLAB_CH), lambda b, s: (b, s)),
        out_shape=jax.ShapeDtypeStruct((B * L, D_MODEL), f32),
        scratch_shapes=[pltpu.VMEM((L, 2 * SLAB_NS), f32)],
        compiler_params=_params("parallel", "parallel"),
    )(p, bs, cs, pw, d_skip)


def _s5_bwd(p, dya0, dp, bs, cs, pw, d_skip, B, L):
    rc = _tile(L, 344)
    ns = SLAB_NS
    nt = L // SUBLANES

    def body(u_ref, dy_ref, dp_in, bs_ref, cs_ref, pw_ref, d_ref, du_ref, dbs_ref, dcs_ref, da_ref, dd_ref,
             s_ref, lam_ref, dyp_ref):
        del dp_in
        b = pl.program_id(1)

        @pl.when(b == 0)
        def _():
            dbs_ref[...] = jnp.zeros_like(dbs_ref)
            dcs_ref[...] = jnp.zeros_like(dcs_ref)
            da_ref[...] = jnp.zeros_like(da_ref)
            dd_ref[...] = jnp.zeros_like(dd_ref)

        _s5_project_in(u_ref, bs_ref, s_ref, L, rc)
        _s5_scan_fwd(s_ref, pw_ref, L)
        for r in range(0, L, rc):
            u = u_ref[r:r + rc, :]
            sb = s_ref[r:r + rc, :].astype(bf16)
            ypre = jnp.dot(sb, cs_ref[...], preferred_element_type=f32) + d_ref[...] * u
            dyp = dy_ref[r:r + rc, :] * _gelu_grad(ypre)
            dyp_ref[r:r + rc, :] = dyp
            dd_ref[...] += jnp.sum(dyp * u, axis=0, keepdims=True)
            dypb = dyp.astype(bf16)
            dcs_ref[...] += lax.dot_general(sb, dypb, _DIMS["tn"], preferred_element_type=f32)
            lam_ref[r:r + rc, :] = lax.dot_general(dypb, cs_ref[...], _DIMS["nt"], preferred_element_type=f32)

        row = lax.broadcasted_iota(jnp.int32, (SUBLANES, ns), 0)
        pr = pw_ref[0, 0:SUBLANES, :]
        pi = -pw_ref[1, 0:SUBLANES, :]
        qr = pw_ref[0, SUBLANES:2 * SUBLANES, :]
        qi = -pw_ref[1, SUBLANES:2 * SUBLANES, :]

        def step(j, carry):
            cr, ci, ar, ai = carry
            i = nt - 1 - j
            r0 = pl.multiple_of(i * SUBLANES, SUBLANES)
            xr = lam_ref[pl.ds(r0, SUBLANES), 0:ns]
            xi = lam_ref[pl.ds(r0, SUBLANES), ns:2 * ns]
            for k in (1, 2, 4):
                xr, xi = _cmul_add(xr, xi, pr[k - 1:k, :], pi[k - 1:k, :], _shift_up(xr, k, row), _shift_up(xi, k, row))
            xr, xi = _cmul_add(xr, xi, qr, qi, cr, ci)
            lam_ref[pl.ds(r0, SUBLANES), 0:ns] = xr
            lam_ref[pl.ds(r0, SUBLANES), ns:2 * ns] = xi
            rp = pl.multiple_of(jnp.maximum(i - 1, 0) * SUBLANES, SUBLANES)
            live = jnp.where(i > 0, 1.0, 0.0)
            lr_ = s_ref[pl.ds(rp + SUBLANES - 1, 1), 0:ns] * live
            li_ = s_ref[pl.ds(rp + SUBLANES - 1, 1), ns:2 * ns] * live
            spr = jnp.where(row == 0, lr_, pltpu.roll(s_ref[pl.ds(r0, SUBLANES), 0:ns], 1, 0))
            spi = jnp.where(row == 0, li_, pltpu.roll(s_ref[pl.ds(r0, SUBLANES), ns:2 * ns], 1, 0))
            ar = ar + xr * spr + xi * spi
            ai = ai + xi * spr - xr * spi
            return xr[0:1, :], xi[0:1, :], ar, ai

        z1 = jnp.zeros((1, ns), f32)
        z8 = jnp.zeros((SUBLANES, ns), f32)
        _, _, ar, ai = lax.fori_loop(0, nt, step, (z1, z1, z8, z8))
        da_ref[0:1, :] += jnp.sum(ar, axis=0, keepdims=True)
        da_ref[1:2, :] += jnp.sum(ai, axis=0, keepdims=True)

        for r in range(0, L, rc):
            lamb = lam_ref[r:r + rc, :].astype(bf16)
            dbs_ref[...] += lax.dot_general(u_ref[r:r + rc, :].astype(bf16), lamb, _DIMS["tn"], preferred_element_type=f32)
            du = (lax.dot_general(lamb, bs_ref[...], _DIMS["nt"], preferred_element_type=f32)
                  + d_ref[...] * dyp_ref[r:r + rc, :])
            du_ref[r:r + rc, :] = du.astype(du_ref.dtype)

    ucol = SEG_U * (D_MODEL // SLAB_CH)
    T = B * L
    return pl.pallas_call(
        body, name="s5_bwd", grid=(N_SLAB, B),
        in_specs=[pl.BlockSpec((L, SLAB_CH), lambda s, b: (b, ucol + s)),
                  pl.BlockSpec((L, SLAB_CH), lambda s, b: (b, s)),
                  ANY,
                  pl.BlockSpec((None, SLAB_CH, 2 * SLAB_NS), lambda s, b: (s, 0, 0)),
                  pl.BlockSpec((None, 2 * SLAB_NS, SLAB_CH), lambda s, b: (s, 0, 0)),
                  pl.BlockSpec((None, 2, 2 * SUBLANES, SLAB_NS), lambda s, b: (s, 0, 0, 0)),
                  pl.BlockSpec((1, SLAB_CH), lambda s, b: (0, s))],
        out_specs=[pl.BlockSpec((None, L, SLAB_CH), lambda s, b: (SEG_U, b, s)),
                   pl.BlockSpec((None, SLAB_CH, 2 * SLAB_NS), lambda s, b: (s, 0, 0)),
                   pl.BlockSpec((None, 2 * SLAB_NS, SLAB_CH), lambda s, b: (s, 0, 0)),
                   pl.BlockSpec((None, 2, SLAB_NS), lambda s, b: (s, 0, 0)),
                   pl.BlockSpec((1, SLAB_CH), lambda s, b: (0, s))],
        out_shape=[jax.ShapeDtypeStruct((N_SEG, T, D_MODEL), bf16),
                   jax.ShapeDtypeStruct((N_SLAB, SLAB_CH, 2 * SLAB_NS), f32),
                   jax.ShapeDtypeStruct((N_SLAB, 2 * SLAB_NS, SLAB_CH), f32),
                   jax.ShapeDtypeStruct((N_SLAB, 2, SLAB_NS), f32),
                   jax.ShapeDtypeStruct((1, D_MODEL), f32)],
        scratch_shapes=[pltpu.VMEM((L, 2 * SLAB_NS), f32), pltpu.VMEM((L, 2 * SLAB_NS), f32), pltpu.VMEM((L, SLAB_CH), f32)],
        input_output_aliases={2: 0},
        compiler_params=_params("parallel", "arbitrary"),
    )(p, dya0, dp, bs, cs, pw, d_skip)


def _rows8(i):
    return pl.ds(pl.multiple_of(i * SUBLANES, SUBLANES), SUBLANES)


def _repeat_loop(n, step, init):
    rep = max(u for u in (6, 4, 3, 2, 1) if n % u == 0)

    def body(t, carry):
        for u in range(rep):
            carry = step(t * rep + u, carry)
        return carry

    return lax.fori_loop(0, n // rep, body, init)


def _to_segments(src_ref, dst_ref, seg):
    def step(i, c):
        dst_ref[_rows8(i), :] = src_ref[pl.ds(i, SUBLANES, stride=seg), :]
        return c

    _repeat_loop(seg, step, 0)


def _from_segments(src_ref, dst_ref, seg):
    def step(i, c):
        dst_ref[pl.ds(i, SUBLANES, stride=seg), :] = src_ref[_rows8(i), :]
        return c

    _repeat_loop(seg, step, 0)


def _seg_local_scan(s_ref, ar, ai, seg, reverse):
    ns = SLAB_NS

    def step(j, carry):
        cr, ci = carry
        rows = _rows8(seg - 1 - j if reverse else j)
        cr, ci = _cmul_add(s_ref[rows, 0:ns], s_ref[rows, ns:2 * ns], ar, ai, cr, ci)
        s_ref[rows, 0:ns] = cr
        s_ref[rows, ns:2 * ns] = ci
        return cr, ci

    z = jnp.zeros((SUBLANES, ns), f32)
    return _repeat_loop(seg, step, (z, z))


def _seg_boundaries(fr, fi, alr, ali, reverse):
    row = lax.broadcasted_iota(jnp.int32, fr.shape, 0)
    br = jnp.zeros_like(fr)
    bi = jnp.zeros_like(fi)
    for r in (range(SUBLANES - 2, -1, -1) if reverse else range(1, SUBLANES)):
        s = r + 1 if reverse else r - 1
        nr, ni = _cmul_add(fr[s:s + 1, :], fi[s:s + 1, :], alr, ali, br[s:s + 1, :], bi[s:s + 1, :])
        br = jnp.where(row == r, nr, br)
        bi = jnp.where(row == r, ni, bi)
    return br, bi


def _s5_states(u_ref, bs_ref, pw_ref, up_ref, s_ref, L, rc):
    seg = L // SUBLANES
    ns = SLAB_NS
    _to_segments(u_ref, up_ref, seg)
    _s5_project_in(up_ref, bs_ref, s_ref, L, rc)
    ar, ai = pw_ref[0, 0:1, :], pw_ref[1, 0:1, :]
    fr, fi = _seg_local_scan(s_ref, ar, ai, seg, False)
    br, bi = _seg_boundaries(fr, fi, pw_ref[0, seg - 1:seg, :], pw_ref[1, seg - 1:seg, :], False)

    def fix(i, c):
        rows = _rows8(i)
        xr, xi = _cmul_add(s_ref[rows, 0:ns], s_ref[rows, ns:2 * ns], pw_ref[0, pl.ds(i, 1), :], pw_ref[1, pl.ds(i, 1), :], br, bi)
        s_ref[rows, 0:ns] = xr
        s_ref[rows, ns:2 * ns] = xi
        return c

    _repeat_loop(seg, fix, 0)


def _pw_spec(seg_rows, order):
    if order == "bs":
        return pl.BlockSpec((2, seg_rows, SLAB_NS), lambda b, s: (0, 0, s))
    return pl.BlockSpec((2, seg_rows, SLAB_NS), lambda s, b: (0, 0, s))


def _s5_fwd(p, bs, cs, pw, d_skip, B, L):
    rc = _tile(L, 344)
    seg = L // SUBLANES

    def body(u_ref, bs_ref, cs_ref, pw_ref, d_ref, y_ref, s_ref, up_ref, yp_ref):
        _s5_states(u_ref, bs_ref, pw_ref, up_ref, s_ref, L, rc)
        for r in range(0, L, rc):
            ypre = (jnp.dot(s_ref[r:r + rc, :].astype(bf16), cs_ref[...], preferred_element_type=f32)
                    + d_ref[...] * up_ref[r:r + rc, :])
            yp_ref[r:r + rc, :] = _gelu(ypre)
        _from_segments(yp_ref, y_ref, seg)

    ucol = SEG_U * (D_MODEL // SLAB_CH)
    return pl.pallas_call(
        body, name="s5_fwd", grid=(B, N_SLAB),
        in_specs=[pl.BlockSpec((L, SLAB_CH), lambda b, s: (b, ucol + s)),
                  pl.BlockSpec((None, SLAB_CH, 2 * SLAB_NS), lambda b, s: (s, 0, 0)),
                  pl.BlockSpec((None, 2 * SLAB_NS, SLAB_CH), lambda b, s: (s, 0, 0)),
                  _pw_spec(pw.shape[1], "bs"),
                  pl.BlockSpec((1, SLAB_CH), lambda b, s: (0, s))],
        out_specs=pl.BlockSpec((L, SLAB_CH), lambda b, s: (b, s)),
        out_shape=jax.ShapeDtypeStruct((B * L, D_MODEL), f32),
        scratch_shapes=[pltpu.VMEM((L, 2 * SLAB_NS), f32), pltpu.VMEM((L, SLAB_CH), f32), pltpu.VMEM((L, SLAB_CH), f32)],
        compiler_params=_params("parallel", "parallel"),
    )(p, bs, cs, pw, d_skip)


def _s5_bwd(p, dya0, dp, bs, cs, pw, d_skip, B, L, sums):
    rc = _tile(L, 344)
    ns = SLAB_NS
    seg = L // SUBLANES
    nx = len(sums)

    def body(u_ref, dy_ref, dp_in, bs_ref, cs_ref, pw_ref, d_ref, *rest):
        xin, (du_ref, dbs_ref, dcs_ref, da_ref, dd_ref), xout = rest[:nx], rest[nx:nx + 5], rest[nx + 5:2 * nx + 5]
        s_ref, lam_ref, up_ref, dyp_ref, nat_ref, send, recv = rest[2 * nx + 5:]
        del dp_in
        start, finish = _chip_exchange_steps(xin, xout, send, recv)

        @pl.when((pl.program_id(0) == 0) & (pl.program_id(1) == 0))
        def _():
            start()

        @pl.when(pl.program_id(1) == 0)
        def _():
            dbs_ref[...] = jnp.zeros_like(dbs_ref)
            dcs_ref[...] = jnp.zeros_like(dcs_ref)
            da_ref[...] = jnp.zeros_like(da_ref)
            dd_ref[...] = jnp.zeros_like(dd_ref)

        _s5_states(u_ref, bs_ref, pw_ref, up_ref, s_ref, L, rc)
        _to_segments(dy_ref, dyp_ref, seg)
        for r in range(0, L, rc):
            u = up_ref[r:r + rc, :]
            sb = s_ref[r:r + rc, :].astype(bf16)
            ypre = jnp.dot(sb, cs_ref[...], preferred_element_type=f32) + d_ref[...] * u
            dyp = dyp_ref[r:r + rc, :] * _gelu_grad(ypre)
            dyp_ref[r:r + rc, :] = dyp
            dd_ref[...] += jnp.sum(dyp * u, axis=0, keepdims=True)
            dypb = dyp.astype(bf16)
            dcs_ref[...] += lax.dot_general(sb, dypb, _DIMS["tn"], preferred_element_type=f32)
            lam_ref[r:r + rc, :] = lax.dot_general(dypb, cs_ref[...], _DIMS["nt"], preferred_element_type=f32)

        ar, ai = pw_ref[0, 0:1, :], -pw_ref[1, 0:1, :]
        fr, fi = _seg_local_scan(lam_ref, ar, ai, seg, True)
        br, bi = _seg_boundaries(fr, fi, pw_ref[0, seg - 1:seg, :], -pw_ref[1, seg - 1:seg, :], True)

        def fix(i, acc):
            accr, acci = acc
            rows = _rows8(i)
            k = seg - 1 - i
            xr, xi = _cmul_add(lam_ref[rows, 0:ns], lam_ref[rows, ns:2 * ns], pw_ref[0, pl.ds(k, 1), :],
                               -pw_ref[1, pl.ds(k, 1), :], br, bi)
            lam_ref[rows, 0:ns] = xr
            lam_ref[rows, ns:2 * ns] = xi
            prev = _rows8(jnp.maximum(i - 1, 0))
            live = jnp.where(i > 0, 1.0, 0.0)
            spr = s_ref[prev, 0:ns] * live
            spi = s_ref[prev, ns:2 * ns] * live
            return accr + xr * spr + xi * spi, acci + xi * spr - xr * spi

        z = jnp.zeros((SUBLANES, ns), f32)
        accr, acci = _repeat_loop(seg, fix, (z, z))
        row = lax.broadcasted_iota(jnp.int32, (SUBLANES, ns), 0)
        last = _rows8(seg - 1)
        spr = jnp.where(row == 0, 0.0, pltpu.roll(s_ref[last, 0:ns], 1, 0))
        spi = jnp.where(row == 0, 0.0, pltpu.roll(s_ref[last, ns:2 * ns], 1, 0))
        xr, xi = lam_ref[0:SUBLANES, 0:ns], lam_ref[0:SUBLANES, ns:2 * ns]
        accr = accr + xr * spr + xi * spi
        acci = acci + xi * spr - xr * spi
        da_ref[0:1, :] += jnp.sum(accr, axis=0, keepdims=True)
        da_ref[1:2, :] += jnp.sum(acci, axis=0, keepdims=True)

        for r in range(0, L, rc):
            lamb = lam_ref[r:r + rc, :].astype(bf16)
            dbs_ref[...] += lax.dot_general(up_ref[r:r + rc, :].astype(bf16), lamb, _DIMS["tn"], preferred_element_type=f32)
            nat_ref[r:r + rc, :] = (lax.dot_general(lamb, bs_ref[...], _DIMS["nt"], preferred_element_type=f32)
                                    + d_ref[...] * dyp_ref[r:r + rc, :])
        _from_segments(nat_ref, up_ref, seg)
        du_ref[...] = up_ref[...].astype(du_ref.dtype)

        @pl.when((pl.program_id(0) == N_SLAB - 1) & (pl.program_id(1) == B - 1))
        def _():
            finish()

    ucol = SEG_U * (D_MODEL // SLAB_CH)
    T = B * L
    col = pltpu.VMEM((L, SLAB_CH), f32)
    res = pl.pallas_call(
        body, name="s5_bwd", grid=(N_SLAB, B),
        in_specs=[pl.BlockSpec((L, SLAB_CH), lambda s, b: (b, ucol + s)),
                  pl.BlockSpec((L, SLAB_CH), lambda s, b: (b, s)),
                  ANY,
                  pl.BlockSpec((None, SLAB_CH, 2 * SLAB_NS), lambda s, b: (s, 0, 0)),
                  pl.BlockSpec((None, 2 * SLAB_NS, SLAB_CH), lambda s, b: (s, 0, 0)),
                  _pw_spec(pw.shape[1], "sb"),
                  pl.BlockSpec((1, SLAB_CH), lambda s, b: (0, s))] + [ANY] * nx,
        out_specs=[pl.BlockSpec((None, L, SLAB_CH), lambda s, b: (SEG_U, b, s)),
                   pl.BlockSpec((None, SLAB_CH, 2 * SLAB_NS), lambda s, b: (s, 0, 0)),
                   pl.BlockSpec((None, 2 * SLAB_NS, SLAB_CH), lambda s, b: (s, 0, 0)),
                   pl.BlockSpec((None, 2, SLAB_NS), lambda s, b: (s, 0, 0)),
                   pl.BlockSpec((1, SLAB_CH), lambda s, b: (0, s))] + [ANY] * nx,
        out_shape=[jax.ShapeDtypeStruct((N_SEG, T, D_MODEL), bf16),
                   jax.ShapeDtypeStruct((N_SLAB, SLAB_CH, 2 * SLAB_NS), f32),
                   jax.ShapeDtypeStruct((N_SLAB, 2 * SLAB_NS, SLAB_CH), f32),
                   jax.ShapeDtypeStruct((N_SLAB, 2, SLAB_NS), f32),
                   jax.ShapeDtypeStruct((1, D_MODEL), f32)] + [jax.ShapeDtypeStruct(a.shape, a.dtype) for a in sums],
        scratch_shapes=[pltpu.VMEM((L, 2 * SLAB_NS), f32), pltpu.VMEM((L, 2 * SLAB_NS), f32), col, col, col]
        + _chip_exchange_sems(nx),
        input_output_aliases={2: 0},
        compiler_params=_params("arbitrary", "arbitrary"),
    )(p, dya0, dp, bs, cs, pw, d_skip, *sums)
    return res[:5], res[5:]


def _dotb(a, b, dims="nn"):
    return lax.dot_general(a.astype(bf16), b.astype(bf16), _DIMS[dims], preferred_element_type=f32)


def _chunk_cumsum(x, pos):
    k = 1
    while k < CHUNK:
        x = x + jnp.where(pos >= k, pltpu.roll(x, k, 0), 0.0)
        k *= 2
    return x


def _chunk_rev_cumsum(x, pos):
    n = x.shape[0]
    k = 1
    while k < CHUNK:
        x = x + jnp.where(pos < CHUNK - k, pltpu.roll(x, n - k, 0), 0.0)
        k *= 2
    return x


def _hgrn_local(q, fl, lb, pos):
    sg = _sigmoid(fl)
    f = lb + (1.0 - lb) * sg
    g = jnp.log(f)
    cum = _chunk_cumsum(g, pos)
    rest = _chunk_rev_cumsum(g, pos) - g
    e = jnp.exp(cum)
    em = jnp.exp(-cum)
    eo = jnp.exp(rest)
    k = 1.0 - f
    return sg, f, e, em, eo, q * e, k * em, k * eo, jnp.exp(cum + rest)


def _hgrn_block_mask(n):
    r = lax.broadcasted_iota(jnp.int32, (n, n), 0)
    c = lax.broadcasted_iota(jnp.int32, (n, n), 1)
    return ((r & -CHUNK) == (c & -CHUNK)) & (c <= r)


def _chunk_pos(n):
    return lax.broadcasted_iota(jnp.int32, (n, HEAD_DIM), 0) & (CHUNK - 1)


def _hgrn_block_rows(L):
    return _tile(L, 688, CHUNK)


def _chunk_rows(c):
    return pl.ds(pl.multiple_of(c * CHUNK, CHUNK), CHUNK)


def _chunk_loop(nc, step):
    rep = max(u for u in range(1, 49) if nc % u == 0)

    def body(i, carry):
        for u in range(rep):
            step(i * rep + u)
        return carry

    lax.fori_loop(0, nc // rep, body, 0)


def _hgrn_specs(L, order):
    hb = D_MODEL // HEAD_DIM

    def spec(seg):
        if order == "bh":
            return pl.BlockSpec((L, HEAD_DIM), lambda b, h: (b, seg * hb + h))
        return pl.BlockSpec((L, HEAD_DIM), lambda h, b: (b, seg * hb + h))

    return [spec(SEG_Q), spec(SEG_F), spec(SEG_I), spec(SEG_OG)]


def _hgrn_fwd(p, lb, norm_g, B, L):
    nc = L // CHUNK

    rb = _hgrn_block_rows(L)

    def body(q_ref, f_ref, v_ref, og_ref, lb_ref, ng_ref, y_ref, qt_s, ko_s, vb_s, dec_s, o_s, u_s, sb_s):
        lbv = lb_ref[...]
        ngv = ng_ref[...]
        mask = _hgrn_block_mask(rb)
        pos = _chunk_pos(rb)

        for r in range(0, L, rb):
            rows = slice(r, r + rb)
            _, _, _, _, _, qt, kt, ko, dec = _hgrn_local(q_ref[rows, :], f_ref[rows, :], lbv, pos)
            vb = v_ref[rows, :].astype(bf16)
            qtb = qt.astype(bf16)
            pm = jnp.where(mask, _dotb(qtb, kt, "nt"), 0.0)
            o_s[rows, :] = _dotb(pm, vb)
            qt_s[rows, :] = qtb
            ko_s[rows, :] = ko.astype(bf16)
            vb_s[rows, :] = vb
            dec_s[rows, :] = dec

        def update(c):
            rows = _chunk_rows(c)
            u_s[c] = _dotb(vb_s[rows, :], ko_s[rows, :], "tn")

        def chain(c, st):
            sb_s[c] = st.astype(bf16)
            return st * dec_s[_chunk_rows(c), :][0:1, :] + u_s[c]

        def attend(c):
            rows = _chunk_rows(c)
            o_s[rows, :] += _dotb(qt_s[rows, :], sb_s[c], "nt")

        _chunk_loop(nc, update)
        lax.fori_loop(0, nc, chain, jnp.zeros((HEAD_DIM, HEAD_DIM), f32))
        _chunk_loop(nc, attend)

        for r in range(0, L, rb):
            rows = slice(r, r + rb)
            o = o_s[rows, :]
            og = og_ref[rows, :]
            on = o * lax.rsqrt(jnp.mean(o * o, axis=-1, keepdims=True) + EPS) * ngv
            y_ref[rows, :] = (on * og * _sigmoid(og)).astype(y_ref.dtype)

    return pl.pallas_call(
        body, name="hgrn_fwd", grid=(B, HEADS),
        in_specs=_hgrn_specs(L, "bh") + [pl.BlockSpec((1, HEAD_DIM), lambda b, h: (0, h)),
                                          pl.BlockSpec((1, HEAD_DIM), lambda b, h: (0, 0))],
        out_specs=pl.BlockSpec((L, HEAD_DIM), lambda b, h: (b, h)),
        out_shape=jax.ShapeDtypeStruct((B * L, D_MODEL), bf16),
        scratch_shapes=[pltpu.VMEM((L, HEAD_DIM), bf16), pltpu.VMEM((L, HEAD_DIM), bf16), pltpu.VMEM((L, HEAD_DIM), bf16),
                        pltpu.VMEM((L, HEAD_DIM), f32), pltpu.VMEM((L, HEAD_DIM), f32),
                        pltpu.VMEM((nc, HEAD_DIM, HEAD_DIM), f32), pltpu.VMEM((nc, HEAD_DIM, HEAD_DIM), bf16)],
        compiler_params=_params("parallel", "parallel"),
    )(p, p, p, p, lb, norm_g)


def _hgrn_bwd(p, dyb, dp, lb, norm_g, B, L):
    nc = L // CHUNK

    rb = _hgrn_block_rows(L)

    def body(q_ref, f_ref, v_ref, og_ref, dy_ref, dp_in, lb_ref, ng_ref, dseg_ref, dlb_ref, dng_ref,
             st_ref, u_s, dsb_s, qt_s, kt_s, ko_s, vb_s, do_s, dec_s, o_s, dqt_s, dkt_s, dko_s, dv_s, ddec_s):
        del dp_in
        lbv = lb_ref[...]
        ngv = ng_ref[...]
        mask = _hgrn_block_mask(rb)
        pos = _chunk_pos(rb)
        blocks = [slice(r, r + rb) for r in range(0, L, rb)]

        @pl.when(pl.program_id(1) == 0)
        def _():
            dlb_ref[...] = jnp.zeros_like(dlb_ref)

        @pl.when((pl.program_id(0) == 0) & (pl.program_id(1) == 0))
        def _():
            dng_ref[...] = jnp.zeros_like(dng_ref)

        def scores(rows):
            return jnp.where(mask, _dotb(qt_s[rows, :], kt_s[rows, :], "nt"), 0.0).astype(bf16)

        for rows in blocks:
            _, _, _, _, _, qt, kt, ko, dec = _hgrn_local(q_ref[rows, :], f_ref[rows, :], lbv, pos)
            qt_s[rows, :] = qt.astype(bf16)
            kt_s[rows, :] = kt.astype(bf16)
            ko_s[rows, :] = ko.astype(bf16)
            vb_s[rows, :] = v_ref[rows, :].astype(bf16)
            dec_s[rows, :] = dec
            o_s[rows, :] = _dotb(scores(rows), vb_s[rows, :])

        def update(c):
            rows = _chunk_rows(c)
            u_s[c] = _dotb(vb_s[rows, :], ko_s[rows, :], "tn")

        def chain(c, st):
            st_ref[c] = st
            return st * dec_s[_chunk_rows(c), :][0:1, :] + u_s[c]

        def attend(c):
            rows = _chunk_rows(c)
            o_s[rows, :] += _dotb(qt_s[rows, :], st_ref[c], "nt")

        _chunk_loop(nc, update)
        lax.fori_loop(0, nc, chain, jnp.zeros((HEAD_DIM, HEAD_DIM), f32))
        _chunk_loop(nc, attend)

        dng = jnp.zeros((1, HEAD_DIM), f32)
        for rows in blocks:
            o = o_s[rows, :]
            og = og_ref[rows, :]
            dy = dy_ref[rows, :]
            rs = lax.rsqrt(jnp.mean(o * o, axis=-1, keepdims=True) + EPS)
            xn = o * rs
            so = _sigmoid(og)
            dseg_ref[SEG_OG, rows, :] = (dy * xn * ngv * so * (1.0 + og * (1.0 - so))).astype(dseg_ref.dtype)
            don = dy * og * so
            dng = dng + jnp.sum(don * xn, axis=0, keepdims=True)
            dxo = don * ngv
            do = (rs * (dxo - xn * jnp.mean(dxo * xn, axis=-1, keepdims=True))).astype(bf16)
            do_s[rows, :] = do
            dpm = jnp.where(mask, _dotb(do, vb_s[rows, :], "nt"), 0.0).astype(bf16)
            dqt_s[rows, :] = _dotb(dpm, kt_s[rows, :])
            dkt_s[rows, :] = _dotb(dpm, qt_s[rows, :], "tn")
            dv_s[rows, :] = _dotb(scores(rows), do, "tn")
        dng_ref[...] += dng

        def rupdate(c):
            rows = _chunk_rows(c)
            u_s[c] = _dotb(do_s[rows, :], qt_s[rows, :], "tn")

        def rchain(j, dst):
            c = nc - 1 - j
            rows = _chunk_rows(c)
            dsb_s[c] = dst.astype(bf16)
            ddec_s[rows, :] = jnp.broadcast_to(jnp.sum(dst * st_ref[c], axis=0, keepdims=True), (CHUNK, HEAD_DIM))
            return dst * dec_s[rows, :][0:1, :] + u_s[c]

        def rattend(c):
            rows = _chunk_rows(c)
            dst = dsb_s[c]
            dqt_s[rows, :] += _dotb(do_s[rows, :], st_ref[c])
            dv_s[rows, :] += _dotb(ko_s[rows, :], dst, "nt")
            dko_s[rows, :] = _dotb(vb_s[rows, :], dst)

        _chunk_loop(nc, rupdate)
        lax.fori_loop(0, nc, rchain, jnp.zeros((HEAD_DIM, HEAD_DIM), f32))
        _chunk_loop(nc, rattend)

        dlb = jnp.zeros((1, HEAD_DIM), f32)
        for rows in blocks:
            sg, f, e, em, eo, qt, kt, ko, dec = _hgrn_local(q_ref[rows, :], f_ref[rows, :], lbv, pos)
            dqt = dqt_s[rows, :]
            dkt = dkt_s[rows, :]
            dko = dko_s[rows, :]
            dko_ko = dko * ko
            dcum = dqt * qt - dkt * kt - dko_ko
            chunk_tot = _chunk_cumsum(dko_ko, pos) + _chunk_rev_cumsum(dko_ko, pos) - dko_ko
            dcum = dcum + jnp.where(pos == CHUNK - 1, chunk_tot + ddec_s[rows, :] * dec, 0.0)
            df = _chunk_rev_cumsum(dcum, pos) / f - (dkt * em + dko * eo)
            dlb = dlb + jnp.sum(df * (1.0 - sg), axis=0, keepdims=True)
            dseg_ref[SEG_Q, rows, :] = (dqt * e).astype(dseg_ref.dtype)
            dseg_ref[SEG_F, rows, :] = (df * (1.0 - lbv) * sg * (1.0 - sg)).astype(dseg_ref.dtype)
            dseg_ref[SEG_I, rows, :] = dv_s[rows, :].astype(dseg_ref.dtype)
        dlb_ref[...] += dlb

    T = B * L
    sb = pltpu.VMEM((L, HEAD_DIM), bf16)
    sf = pltpu.VMEM((L, HEAD_DIM), f32)
    return pl.pallas_call(
        body, name="hgrn_bwd", grid=(HEADS, B),
        in_specs=_hgrn_specs(L, "hb") + [pl.BlockSpec((L, HEAD_DIM), lambda h, b: (b, h)), ANY,
                                          pl.BlockSpec((1, HEAD_DIM), lambda h, b: (0, h)),
                                          pl.BlockSpec((1, HEAD_DIM), lambda h, b: (0, 0))],
        out_specs=[pl.BlockSpec((4, L, HEAD_DIM), lambda h, b: (0, b, h)),
                   pl.BlockSpec((1, HEAD_DIM), lambda h, b: (0, h)),
                   pl.BlockSpec((1, HEAD_DIM), lambda h, b: (0, 0))],
        out_shape=[jax.ShapeDtypeStruct((N_SEG, T, D_MODEL), bf16), jax.ShapeDtypeStruct((1, D_MODEL), f32),
                   jax.ShapeDtypeStruct((1, HEAD_DIM), f32)],
        scratch_shapes=[pltpu.VMEM((nc, HEAD_DIM, HEAD_DIM), f32), pltpu.VMEM((nc, HEAD_DIM, HEAD_DIM), f32),
                        pltpu.VMEM((nc, HEAD_DIM, HEAD_DIM), bf16), sb, sb, sb, sb, sb, sf, sf, sf, sf, sf, sf, sf],
        input_output_aliases={5: 0},
        compiler_params=_params("arbitrary", "arbitrary"),
    )(p, p, p, p, dyb, dp, lb, norm_g)


def _dz1(dp, w_in_phys):
    _, T, Dm = dp.shape
    tm = _tile(T, 1032)
    return _mm("dz1", dp, w_in_phys, "nt", (T // tm, 1, N_SEG),
               pl.BlockSpec((None, tm, Dm), lambda i, j, k: (k, i, 0)),
               pl.BlockSpec((Dm, Dm), lambda i, j, k: (0, k)),
               jax.ShapeDtypeStruct((T, Dm), f32), pl.BlockSpec((tm, Dm), lambda i, j, k: (i, 0)), (tm, Dm))


def _dw_in(z1, dp):
    _, T, Dm = dp.shape
    tk = _tile(T, 1376)
    return _mm("dw_in", z1, dp, "tn", (1, N_SEG, T // tk),
               pl.BlockSpec((tk, Dm), lambda i, j, k: (k, 0)),
               pl.BlockSpec((None, tk, Dm), lambda i, j, k: (j, k, 0)),
               jax.ShapeDtypeStruct((N_SEG, Dm, Dm), f32),
               pl.BlockSpec((None, Dm, Dm), lambda i, j, k: (j, 0, 0)), (Dm, Dm))


def _dz2(dup, w_up):
    _, T, _ = dup.shape
    tm = _tile(T, 1032)
    tk = D_FF // 2
    return _mm("dz2", dup, w_up, "nt", (T // tm, 1, 4),
               pl.BlockSpec((None, tm, tk), lambda i, j, k: (k // 2, i, k % 2)),
               pl.BlockSpec((D_MODEL, tk), lambda i, j, k: (0, k)),
               jax.ShapeDtypeStruct((T, D_MODEL), f32), pl.BlockSpec((tm, D_MODEL), lambda i, j, k: (i, 0)), (tm, D_MODEL))


def _dw_up(z2, dup):
    _, T, _ = dup.shape
    tn = D_FF // 2
    tk = _tile(T, 688)
    return _mm("dw_up", z2, dup, "tn", (1, N_CHIPS, T // tk),
               pl.BlockSpec((tk, D_MODEL), lambda i, j, k: (k, 0)),
               pl.BlockSpec((None, tk, tn), lambda i, j, k: (j // 2, k, j % 2)),
               jax.ShapeDtypeStruct((N_CHIPS, D_MODEL, tn), f32),
               pl.BlockSpec((None, D_MODEL, tn), lambda i, j, k: (j, 0, 0)), (D_MODEL, tn))


def _place():
    x, y, c = lax.axis_index("x"), lax.axis_index("y"), lax.axis_index("c")
    chips = [(1 - x, y), (x, 1 - y), (1 - x, 1 - y)]
    return x, y, c, chips


def _allgather_chips(arrs):
    n = len(arrs)

    def body(*refs):
        ins, outs = refs[:n], refs[n:2 * n]
        send, recv, local = refs[2 * n:]
        x, y, c, chips = _place()
        me = 2 * x + y

        def copy(a, k, slot):
            px, py = chips[k]
            return pltpu.make_async_remote_copy(src_ref=ins[a], dst_ref=outs[a].at[slot], send_sem=send.at[3 * a + k],
                                                recv_sem=recv.at[3 * a + k], device_id=(px, py, c), device_id_type=MESH)

        for a in range(n):
            pltpu.make_async_copy(ins[a], outs[a].at[me], local.at[a]).start()
            for k in range(3):
                copy(a, k, me).start()
        for a in range(n):
            for k, (px, py) in enumerate(chips):
                copy(a, k, 2 * px + py).wait_recv()
        for a in range(n):
            pltpu.make_async_copy(ins[a], outs[a].at[me], local.at[a]).wait()
            for k in range(3):
                copy(a, k, me).wait_send()

    return pl.pallas_call(
        body, name="allgather_chips", in_specs=[ANY] * n, out_specs=[ANY] * n,
        out_shape=[jax.ShapeDtypeStruct((N_CHIPS,) + a.shape, a.dtype) for a in arrs],
        scratch_shapes=[pltpu.SemaphoreType.DMA((3 * n,)), pltpu.SemaphoreType.DMA((3 * n,)), pltpu.SemaphoreType.DMA((n,))],
    )(*arrs)


def _allgather_split(arrs):
    n = len(arrs)

    def body(*refs):
        start, finish = _gather_split_steps(refs[:n], refs[n:2 * n], *refs[2 * n:])
        start()
        finish()

    return pl.pallas_call(
        body, name="allgather_split", in_specs=[ANY] * n, out_specs=[ANY] * n,
        out_shape=[jax.ShapeDtypeStruct((N_CHIPS,) + a.shape, a.dtype) for a in arrs],
        scratch_shapes=_gather_split_sems(n),
    )(*arrs)


def _gather_split_sems(n):
    return [pltpu.SemaphoreType.DMA((3 * n,)) for _ in range(4)]


def _gather_split_steps(ins, outs, send, recv, fsend, frecv):
    n = len(ins)

    def place():
        x, y, c, chips = _place()
        return x, y, c, chips, 2 * x + y

    def half(a, core):
        rh = ins[a].shape[0] // 2
        return pl.ds(core * rh, rh)

    def copy(a, k, slot):
        x, y, c, chips, _ = place()
        px, py = chips[k]
        return pltpu.make_async_remote_copy(src_ref=ins[a].at[half(a, c), :], dst_ref=outs[a].at[slot, half(a, c), :],
                                            send_sem=send.at[3 * a + k], recv_sem=recv.at[3 * a + k],
                                            device_id=(px, py, c), device_id_type=MESH)

    def forward(a, k, core):
        x, y, c, chips, _ = place()
        px, py = chips[k]
        rows = outs[a].at[2 * px + py, half(a, core), :]
        return pltpu.make_async_remote_copy(src_ref=rows, dst_ref=rows, send_sem=fsend.at[3 * a + k],
                                            recv_sem=frecv.at[3 * a + k], device_id=(x, y, 1 - c), device_id_type=MESH)

    def start():
        me = place()[4]
        for a in range(n):
            for k in range(3):
                copy(a, k, me).start()

    def finish():
        x, y, c, chips, me = place()
        for a in range(n):
            for k, (px, py) in enumerate(chips):
                copy(a, k, 2 * px + py).wait_recv()
                forward(a, k, c).start()
        for a in range(n):
            for k in range(3):
                forward(a, k, 1 - c).wait_recv()
        for a in range(n):
            for k in range(3):
                copy(a, k, me).wait_send()
                forward(a, k, c).wait_send()

    return start, finish


def _in_proj_gather(z1, w_in, shards):
    n = len(shards)
    T, K = z1.shape
    N = w_in.shape[1]
    tm = _tile(T, 1032)
    tn = 1024
    grid = (T // tm, N // tn)

    def body(a_ref, b_ref, *rest):
        ins, o_ref, outs, sems = rest[:n], rest[n], rest[n + 1:2 * n + 1], rest[2 * n + 1:]
        start, finish = _gather_split_steps(ins, outs, *sems)
        i, j = pl.program_id(0), pl.program_id(1)

        @pl.when((i == 0) & (j == 0))
        def _():
            start()

        o_ref[...] = jnp.dot(a_ref[...], b_ref[...], preferred_element_type=f32)

        @pl.when((i == grid[0] - 1) & (j == grid[1] - 1))
        def _():
            finish()

    res = pl.pallas_call(
        body, name="in_proj", grid=grid,
        in_specs=[pl.BlockSpec((tm, K), lambda i, j: (i, 0)), pl.BlockSpec((K, tn), lambda i, j: (0, j))] + [ANY] * n,
        out_specs=[pl.BlockSpec((tm, tn), lambda i, j: (i, j))] + [ANY] * n,
        out_shape=[jax.ShapeDtypeStruct((T, N), f32)] + [jax.ShapeDtypeStruct((N_CHIPS,) + a.shape, a.dtype) for a in shards],
        scratch_shapes=_gather_split_sems(n),
        compiler_params=_params("arbitrary", "arbitrary"),
    )(z1, w_in, *shards)
    return res[0], res[1:]


def _sibling_halves(parts, name="sibling_halves"):
    n = len(parts)

    def body(*refs):
        ins, outs = refs[:n], refs[n:2 * n]
        send, recv = refs[2 * n:]
        x, y, c, _ = _place()

        def copy(a):
            rh = ins[a].shape[1] // 2
            return pltpu.make_async_remote_copy(src_ref=ins[a].at[:, pl.ds((1 - c) * rh, rh), :], dst_ref=outs[a],
                                                send_sem=send.at[a], recv_sem=recv.at[a], device_id=(x, y, 1 - c),
                                                device_id_type=MESH)

        for a in range(n):
            copy(a).start()
        for a in range(n):
            copy(a).wait_recv()
        for a in range(n):
            copy(a).wait_send()

    return pl.pallas_call(
        body, name=name, in_specs=[ANY] * n, out_specs=[ANY] * n,
        out_shape=[jax.ShapeDtypeStruct((a.shape[0], a.shape[1] // 2, a.shape[2]), a.dtype) for a in parts],
        scratch_shapes=[pltpu.SemaphoreType.DMA((n,)), pltpu.SemaphoreType.DMA((n,))],
    )(*parts)


def _add_own_half(name, part, got, core):
    nchip, R, C = part.shape
    rh = R // 2
    tr = _tile(rh, 256, 2 * SUBLANES)
    nt = rh // tr

    def body(core_ref, a_ref, b_ref, o_ref):
        del core_ref
        o_ref[...] = (a_ref[...] + b_ref[...]).astype(o_ref.dtype)

    return pl.pallas_call(
        body, name=name,
        grid_spec=pltpu.PrefetchScalarGridSpec(
            num_scalar_prefetch=1, grid=(nchip, nt),
            in_specs=[pl.BlockSpec((None, tr, C), lambda j, i, core_ref: (j, core_ref[0] * nt + i, 0)),
                      pl.BlockSpec((None, tr, C), lambda j, i, core_ref: (j, i, 0))],
            out_specs=pl.BlockSpec((None, tr, C), lambda j, i, core_ref: (j, i, 0))),
        out_shape=jax.ShapeDtypeStruct((nchip, rh, C), bf16), compiler_params=_params("parallel", "parallel"),
    )(core, part, got)


def _add_own_half_w_in(part, got, core):
    _, R, C = part.shape
    rh = R // 2
    tr = _tile(rh, 256, 2 * SUBLANES)
    nt = rh // tr
    tn = 256
    per_seg = C // tn
    per_chip = IN_COLS // N_CHIPS // tn

    def src(j):
        return ((j // per_seg + N_SEG - 1) % N_SEG, j % per_seg)

    def body(core_ref, a_ref, b_ref, o_ref):
        del core_ref
        o_ref[...] = (a_ref[...] + b_ref[...]).astype(o_ref.dtype)

    return pl.pallas_call(
        body, name="add_half_w_in",
        grid_spec=pltpu.PrefetchScalarGridSpec(
            num_scalar_prefetch=1, grid=(IN_COLS // tn, nt),
            in_specs=[pl.BlockSpec((None, tr, tn), lambda j, i, core_ref: (src(j)[0], core_ref[0] * nt + i, src(j)[1])),
                      pl.BlockSpec((None, tr, tn), lambda j, i, core_ref: (src(j)[0], i, src(j)[1]))],
            out_specs=pl.BlockSpec((None, tr, tn), lambda j, i, core_ref: (j // per_chip, i, j % per_chip))),
        out_shape=jax.ShapeDtypeStruct((N_CHIPS, rh, IN_COLS // N_CHIPS), bf16), compiler_params=_params("parallel", "parallel"),
    )(core, part, got)


def _chip_exchange(sums):
    n = len(sums)

    def body(*refs):
        start, finish = _chip_exchange_steps(refs[:n], refs[n:2 * n], *refs[2 * n:])
        start()
        finish()

    return pl.pallas_call(
        body, name="chip_exchange", in_specs=[ANY] * n, out_specs=[ANY] * n,
        out_shape=[jax.ShapeDtypeStruct(a.shape, a.dtype) for a in sums],
        scratch_shapes=_chip_exchange_sems(n),
    )(*sums)


def _chip_exchange_sems(n):
    return [pltpu.SemaphoreType.DMA((3 * n,)), pltpu.SemaphoreType.DMA((3 * n,))]


def _chip_exchange_steps(ins, outs, send, recv):
    n = len(ins)

    def copy(a, k, own_slot):
        x, y, c, chips = _place()
        px, py = chips[k]
        slot = 2 * x + y if own_slot else 2 * px + py
        return pltpu.make_async_remote_copy(src_ref=ins[a].at[2 * px + py], dst_ref=outs[a].at[slot], send_sem=send.at[3 * a + k],
                                            recv_sem=recv.at[3 * a + k], device_id=(px, py, c), device_id_type=MESH)

    def start():
        for a in range(n):
            for k in range(3):
                copy(a, k, True).start()

    def finish():
        for a in range(n):
            for k in range(3):
                copy(a, k, False).wait_recv()
        for a in range(n):
            for k in range(3):
                copy(a, k, True).wait_send()

    return start, finish


def _sum_chips(name, slots, sums, where):
    nchip, rh, C = slots.shape
    tr = _tile(rh, 256, 2 * SUBLANES)
    nt = rh // tr

    def body(where_ref, own_ref, s1_ref, s2_ref, s3_ref, o_ref):
        me = where_ref[0]
        by_dist = [r[...].astype(f32) for r in (own_ref, s1_ref, s2_ref, s3_ref)]
        acc = None
        for j in range(nchip):
            d = me ^ j
            term = jnp.where(d == 0, by_dist[0], jnp.where(d == 1, by_dist[1], jnp.where(d == 2, by_dist[2], by_dist[3])))
            acc = term if acc is None else acc + term
        o_ref[...] = acc

    def other(d):
        return pl.BlockSpec((None, tr, C), lambda i, w: (w[0] ^ d, i, 0))

    return pl.pallas_call(
        body, name=name,
        grid_spec=pltpu.PrefetchScalarGridSpec(
            num_scalar_prefetch=1, grid=(nt,),
            in_specs=[other(0), other(1), other(2), other(3)],
            out_specs=pl.BlockSpec((tr, C), lambda i, w: (w[1] * nt + i, 0))),
        out_shape=jax.ShapeDtypeStruct((2 * rh, C), f32), compiler_params=_params("parallel"),
    )(where, sums, slots, slots, slots)


def _sum_slots(name, slots):
    ns, R, C = slots.shape
    tr = _tile(R, 256)

    def body(s_ref, o_ref):
        acc = s_ref[0]
        for j in range(1, ns):
            acc = acc + s_ref[j]
        o_ref[...] = acc

    return pl.pallas_call(
        body, name=name, grid=(R // tr,), in_specs=[pl.BlockSpec((ns, tr, C), lambda i: (0, i, 0))],
        out_specs=pl.BlockSpec((tr, C), lambda i: (i, 0)), out_shape=jax.ShapeDtypeStruct((R, C), f32),
        compiler_params=_params("parallel"),
    )(slots)


def _sibling_join(fulls):
    n = len(fulls)

    def body(*refs):
        ins, outs = refs[:n], refs[n:2 * n]
        send, recv = refs[2 * n:]
        x, y, c, _ = _place()

        def copy(a, core):
            rh = ins[a].shape[0] // 2
            rows = pl.ds(core * rh, rh)
            return pltpu.make_async_remote_copy(src_ref=ins[a].at[rows, :], dst_ref=outs[a].at[rows, :], send_sem=send.at[a],
                                                recv_sem=recv.at[a], device_id=(x, y, 1 - c), device_id_type=MESH)

        for a in range(n):
            copy(a, c).start()
        for a in range(n):
            copy(a, 1 - c).wait_recv()
        for a in range(n):
            copy(a, c).wait_send()

    return pl.pallas_call(
        body, name="sibling_join", in_specs=[ANY] * n, out_specs=[ANY] * n,
        out_shape=[jax.ShapeDtypeStruct(a.shape, a.dtype) for a in fulls],
        scratch_shapes=[pltpu.SemaphoreType.DMA((n,)), pltpu.SemaphoreType.DMA((n,))],
        input_output_aliases={a: a for a in range(n)},
    )(*fulls)


def _allgather_devices(v):
    def body(v_ref, out_ref, send, recv):
        x, y, c, chips = _place()
        me, sibling = (x, y, c), (x, y, 1 - c)

        def slot(px, py, pc):
            return out_ref.at[4 * px + 2 * py + pc]

        def copy(k, block, to, src=None):
            return pltpu.make_async_remote_copy(src_ref=slot(*block) if src is None else src, dst_ref=slot(*block),
                                                send_sem=send.at[k], recv_sem=recv.at[k], device_id=to, device_id_type=MESH)

        first = [copy(0, me, sibling, src=v_ref)] + [copy(1 + j, me, (*chip, c), src=v_ref) for j, chip in enumerate(chips)]
        for cp in first:
            cp.start()
        passed = [copy(4 + j, (*chip, c), sibling) for j, chip in enumerate(chips)]
        for j, chip in enumerate(chips):
            copy(1 + j, (*chip, c), me).wait_recv()
            passed[j].start()
        copy(0, sibling, me).wait_recv()
        for j, chip in enumerate(chips):
            copy(4 + j, (*chip, 1 - c), me).wait_recv()
        for cp in first + passed:
            cp.wait_send()

    return pl.pallas_call(
        body, name="allgather_devices", in_specs=[ANY], out_specs=ANY,
        out_shape=jax.ShapeDtypeStruct((N_DEV,) + v.shape, v.dtype),
        scratch_shapes=[pltpu.SemaphoreType.DMA((N_DEV - 1,)), pltpu.SemaphoreType.DMA((N_DEV - 1,))],
    )(v)


def _adamw(name, w, g, m, v):
    R, C = w.shape
    tr = _tile(R, 256)
    c1 = 1.0 / (1.0 - ADAM_B1 ** ADAM_STEP)
    c2 = 1.0 / (1.0 - ADAM_B2 ** ADAM_STEP)

    def body(w_ref, g_ref, m_ref, v_ref, d_ref, nm_ref, nv_ref):
        gv = g_ref[...]
        nm = ADAM_B1 * m_ref[...] + (1.0 - ADAM_B1) * gv
        nv = ADAM_B2 * v_ref[...] + (1.0 - ADAM_B2) * gv * gv
        d_ref[...] = -ADAM_LR * ((nm * c1) / (jnp.sqrt(nv * c2) + ADAM_EPS) + ADAM_WD * w_ref[...])
        nm_ref[...] = nm
        nv_ref[...] = nv

    row = pl.BlockSpec((tr, C), lambda i: (i, 0))
    sh = jax.ShapeDtypeStruct((R, C), f32)
    return pl.pallas_call(body, name=name, grid=(R // tr,), in_specs=[row] * 4, out_specs=[row] * 3,
                          out_shape=[sh, sh, sh], compiler_params=_params("parallel"))(w, g, m, v)


def _zoh(lr, li, log_dt, b_re, b_im):
    dt = jnp.exp(log_dt)[:, None]
    mag = jnp.exp(lr * dt)
    ab_re = mag * jnp.cos(li * dt)
    ab_im = mag * jnp.sin(li * dt)
    den = lr * lr + li * li
    nr = ab_re - 1.0
    coef_re = (nr * lr + ab_im * li) / den
    coef_im = (ab_im * lr - nr * li) / den
    bb_re = coef_re[..., None] * b_re - coef_im[..., None] * b_im
    bb_im = coef_re[..., None] * b_im + coef_im[..., None] * b_re
    return ab_re, ab_im, bb_re, bb_im


def _s5_tables(ab_re, ab_im, bb_re, bb_im, c_re, c_im, seg):
    eye = jnp.eye(SLAB_GROUPS, dtype=f32)

    def blk_in(bb):
        return jnp.einsum("sgph,gk->sghkp", bb.reshape(N_SLAB, SLAB_GROUPS, SSM_STATE, SSM_GROUP), eye).reshape(
            N_SLAB, SLAB_CH, SLAB_NS)

    def blk_out(cc):
        return jnp.einsum("sghp,gk->skpgh", cc.reshape(N_SLAB, SLAB_GROUPS, SSM_GROUP, SSM_STATE), eye).reshape(
            N_SLAB, SLAB_NS, SLAB_CH)

    bs = jnp.concatenate([blk_in(bb_re), blk_in(bb_im)], axis=2).astype(bf16)
    cs = jnp.concatenate([blk_out(c_re), blk_out(-c_im)], axis=1).astype(bf16)
    n = SSM_GROUPS * SSM_STATE
    pw = _power_table(jnp.stack([ab_re.reshape(1, n), ab_im.reshape(1, n)]), -(-seg // SUBLANES))
    return bs, cs, pw


def _power_table(ab, tiles):
    n = ab.shape[2]

    def body(a_ref, o_ref):
        row = lax.broadcasted_iota(jnp.int32, (SUBLANES, n), 0)
        ar, ai = a_ref[0], a_ref[1]
        tr, ti = jnp.broadcast_to(ar, (SUBLANES, n)), jnp.broadcast_to(ai, (SUBLANES, n))
        pr, pi = ar, ai
        for r in range(1, SUBLANES):
            pr, pi = pr * ar - pi * ai, pr * ai + pi * ar
            tr = jnp.where(row == r, pr, tr)
            ti = jnp.where(row == r, pi, ti)
        o_ref[0, 0:SUBLANES, :] = tr
        o_ref[1, 0:SUBLANES, :] = ti

        def step(j, carry):
            cr, ci = carry
            cr, ci = cr * pr - ci * pi, cr * pi + ci * pr
            o_ref[0, _rows8(j), :] = cr
            o_ref[1, _rows8(j), :] = ci
            return cr, ci

        lax.fori_loop(1, tiles, step, (tr, ti))

    return pl.pallas_call(body, name="power_table", out_shape=jax.ShapeDtypeStruct((2, SUBLANES * tiles, n), f32))(ab)


def _s5_table_grads(dbs, dcs, da):
    eye = jnp.eye(SLAB_GROUPS, dtype=f32)
    d6 = dbs.reshape(N_SLAB, SLAB_GROUPS, SSM_GROUP, 2, SLAB_GROUPS, SSM_STATE)
    dbb = jnp.einsum("sghrkp,gk->rsgph", d6, eye).reshape(2, SSM_GROUPS, SSM_STATE, SSM_GROUP)
    c6 = dcs.reshape(N_SLAB, 2, SLAB_GROUPS, SSM_STATE, SLAB_GROUPS, SSM_GROUP)
    dcc = jnp.einsum("srkpgh,gk->rsghp", c6, eye).reshape(2, SSM_GROUPS, SSM_GROUP, SSM_STATE)
    dab = da.transpose(1, 0, 2).reshape(2, SSM_GROUPS, SSM_STATE)
    return dab[0], dab[1], dbb[0], dbb[1], dcc[0], -dcc[1]


SMALL = ["mix_norm_g", "ssm_lambda_re", "ssm_lambda_im", "ssm_log_dt", "ssm_b_re", "ssm_b_im", "ssm_c_re", "ssm_c_im",
         "ssm_d", "hgrn_lb_logits", "hgrn_norm_g", "ffn_norm_g", "conv_b", "final_norm_g"]
SHARDED_SMALL = ["meta_tokens", "conv_w"]
BIG = ["w_in", "ssm_w_glu", "w_ssm_proj", "w_hgrn_proj", "w_out", "w_up", "w_down"]
WEIGHTS = ['meta_tokens', 'mix_norm_g', 'w_in', 'ssm_lambda_re', 'ssm_lambda_im', 'ssm_log_dt', 'ssm_b_re', 'ssm_b_im',
           'ssm_c_re', 'ssm_c_im', 'ssm_d', 'ssm_w_glu', 'w_ssm_proj', 'hgrn_lb_logits', 'hgrn_norm_g', 'w_hgrn_proj',
           'w_out', 'ffn_norm_g', 'w_up', 'conv_w', 'conv_b', 'w_down', 'final_norm_g']


LATER = [k for k in BIG if k != "w_in"]


def _full_weights(gathered, shards, chip):
    Dm = D_MODEL
    g = {k: lax.dynamic_update_slice(gathered[k], shards[k][None], (chip, 0, 0)) for k in gathered}
    full = {}
    for k, v in g.items():
        if k == "w_in":
            full[k] = jnp.roll(v.transpose(1, 0, 2).reshape(Dm, IN_COLS), -Dm, axis=1)
        elif k == "w_up":
            full[k] = v.transpose(1, 0, 2).reshape(Dm, 2 * D_FF)
        else:
            full[k] = v.reshape(-1, Dm)
    return full


def _local_grads(x, tgt, meta, w, full, shards, chip, core):
    B, S, Dm = x.shape
    L = S + N_META
    T = B * L
    h0 = jnp.concatenate([jnp.broadcast_to(meta[None], (B, N_META, Dm)), x], axis=1).reshape(T, Dm)

    lb_all = jax.nn.softmax(w["hgrn_lb_logits"], axis=0)
    lb = lb_all[0:1]
    zoh_out, zoh_vjp = jax.vjp(_zoh, w["ssm_lambda_re"][0], w["ssm_lambda_im"][0], w["ssm_log_dt"][0],
                               w["ssm_b_re"][0], w["ssm_b_im"][0])
    bs, cs, pw = _s5_tables(*zoh_out, w["ssm_c_re"][0], w["ssm_c_im"][0], L // SUBLANES)

    z1 = _rmsnorm_fwd("mix_norm", h0, w["mix_norm_g"])
    p, gathered = _in_proj_gather(z1, full["w_in"], [shards[k] for k in LATER])
    full = {**full, **_full_weights(dict(zip(LATER, gathered)), shards, chip)}
    ya0 = _s5_fwd(p, bs, cs, pw, w["ssm_d"], B, L)
    gl = _mm_rows("glu_proj", ya0, full["ssm_w_glu"], "nn", f32, 1024)
    ya = _glu_fwd(ya0, gl)
    yb = _hgrn_fwd(p, lb, w["hgrn_norm_g"], B, L)
    pa = _mm_rows("ssm_proj", ya, full["w_ssm_proj"], "nn", f32, 1024)
    pb = _mm_rows("hgrn_proj", yb, full["w_hgrn_proj"], "nn", f32, 1024)
    merged = _merge_fwd(p, pa, pb)
    h1 = _mm_rows("out_proj", merged, full["w_out"], "nn", f32, 1024, res=h0)
    z2 = _rmsnorm_fwd("ffn_norm", h1, w["ffn_norm_g"])
    up = _mm_rows("up_proj", z2, full["w_up"], "nn", f32, D_FF // 2)
    ff = _conv_fwd(up, full["conv_w"], w["conv_b"], B, L)
    h2 = _mm_rows("down_proj", ff, full["w_down"], "nn", f32, 1024, res=h1, tk=D_FF // 2)

    h2x = h2.reshape(B, L, Dm)[:, N_META:].reshape(B * S, Dm)
    dh2x, loss, d_final_g = _final_loss(h2x, tgt.reshape(B * S, Dm), w["final_norm_g"].reshape(1, Dm))
    dh2 = jnp.pad(dh2x.reshape(B, S, Dm), ((0, 0), (N_META, 0), (0, 0))).reshape(T, Dm)

    dff = _mm_rows("d_ff", dh2, full["w_down"], "nt", f32, D_FF // 2)
    g_w_down = _mm_wgrad("dw_down", ff, dh2, tn=512)
    dup, dconv = _conv_bwd(up, dff, full["conv_w"], w["conv_b"], B, L)
    dz2 = _dz2(dup, full["w_up"])
    g_w_up = _dw_up(z2, dup)
    dh1, d_ffn_g = _rmsnorm_bwd("ffn_norm_bwd", h1, w["ffn_norm_g"], dz2, dh2)

    dmerged = _mm_rows("d_merged", dh1, full["w_out"], "nt", f32, 1024)
    g_w_out = _mm_wgrad("dw_out", merged, dh1)
    dpa, dpb, dp = _merge_bwd(dmerged, p, pa, pb)
    dya = _mm_rows("d_ya", dpa, full["w_ssm_proj"], "nt", f32, 1024)
    g_w_ssm_proj = _mm_wgrad("dw_ssm_proj", ya, dpa)
    dyb = _mm_rows("d_yb", dpb, full["w_hgrn_proj"], "nt", f32, 1024)
    g_w_hgrn_proj = _mm_wgrad("dw_hgrn_proj", yb, dpb)
    dp, d_lb, d_hgrn_g = _hgrn_bwd(p, dyb, dp, lb, w["hgrn_norm_g"], B, L)
    dgl, dya0_direct = _glu_bwd(dya, ya0, gl)
    dya0 = _mm_rows("d_ya0", dgl, full["ssm_w_glu"], "nt", f32, 1024, res=dya0_direct)
    g_w_glu = _mm_wgrad("dw_glu", ya0, dgl)
    parts = {
        "ssm_w_glu": g_w_glu.reshape(N_CHIPS, Dm // N_CHIPS, Dm), "w_ssm_proj": g_w_ssm_proj.reshape(N_CHIPS, Dm // N_CHIPS, Dm),
        "w_hgrn_proj": g_w_hgrn_proj.reshape(N_CHIPS, Dm // N_CHIPS, Dm), "w_out": g_w_out.reshape(N_CHIPS, Dm // N_CHIPS, Dm),
        "w_up": g_w_up, "w_down": g_w_down.reshape(N_CHIPS, D_FF // N_CHIPS, Dm),
    }
    got = _sibling_halves([parts[k] for k in LATER])
    sums = {k: _add_own_half("add_half_" + k, parts[k], gt, core) for k, gt in zip(LATER, got)}
    (dp, dbs, dcs, da, d_skip), slots_later = _s5_bwd(p, dya0, dp, bs, cs, pw, w["ssm_d"], B, L, [sums[k] for k in LATER])
    slots = dict(zip(LATER, slots_later))
    dz1 = _dz1(dp, full["w_in"])
    g_w_in = _dw_in(z1, dp)
    dh0, d_mix_g = _rmsnorm_bwd("mix_norm_bwd", h0, w["mix_norm_g"], dz1, dh1)

    dh0 = dh0.reshape(B, L, Dm)
    grad_x = dh0[:, N_META:]
    d_meta = _meta_grad(dh0[:, :N_META])

    d_ab_re, d_ab_im, d_bb_re, d_bb_im, d_c_re, d_c_im = _s5_table_grads(dbs, dcs, da)
    d_lr, d_li, d_log_dt, d_b_re, d_b_im = zoh_vjp((d_ab_re, d_ab_im, d_bb_re, d_bb_im))
    sm0, sm1 = lb_all[0:1], lb_all[1:2]
    d_logits = jnp.concatenate([sm0 * (1.0 - sm0) * d_lb, -sm0 * sm1 * d_lb], axis=0)
    small = {
        "meta_tokens": d_meta, "mix_norm_g": d_mix_g, "ssm_lambda_re": d_lr[None], "ssm_lambda_im": d_li[None],
        "ssm_log_dt": d_log_dt[None], "ssm_b_re": d_b_re[None], "ssm_b_im": d_b_im[None], "ssm_c_re": d_c_re[None],
        "ssm_c_im": d_c_im[None], "ssm_d": d_skip, "hgrn_lb_logits": d_logits, "hgrn_norm_g": d_hgrn_g,
        "ffn_norm_g": d_ffn_g, "conv_w": dconv[:, 0:3, :].transpose(1, 0, 2).reshape(3, 2 * D_FF),
        "conv_b": dconv[:, 3, :].reshape(1, 2 * D_FF), "final_norm_g": d_final_g.reshape(Dm),
    }
    sums["w_in"] = _add_own_half_w_in(g_w_in, _sibling_halves([g_w_in], "sibling_halves_w_in")[0], core)
    slots["w_in"] = _chip_exchange([sums["w_in"]])[0]
    return loss, grad_x, sums, slots, small


PACK_ROWS = 256


def _pack(parts):
    flat = jnp.concatenate([parts[k].reshape(-1) for k in parts])
    n = flat.shape[0]
    rows = -(-n // (PACK_ROWS * LANES)) * PACK_ROWS
    flat = jnp.pad(flat, (0, rows * LANES - n))
    return flat.reshape(rows, LANES)


def _unpack(packed, like):
    flat = packed.reshape(-1)
    out, o = {}, 0
    for k, ref in like.items():
        n = math.prod(ref.shape)
        out[k] = flat[o:o + n].reshape(ref.shape)
        o += n
    return out


def kernel(x, meta_tokens, mix_norm_g, w_in, ssm_lambda_re, ssm_lambda_im, ssm_log_dt, ssm_b_re, ssm_b_im, ssm_c_re, ssm_c_im, ssm_d, ssm_w_glu, w_ssm_proj, hgrn_lb_logits, hgrn_norm_g, w_hgrn_proj, w_out, ffn_norm_g, w_up, conv_w, conv_b, w_down, final_norm_g, loss_target, m_meta_tokens, m_mix_norm_g, m_w_in, m_ssm_lambda_re, m_ssm_lambda_im, m_ssm_log_dt, m_ssm_b_re, m_ssm_b_im, m_ssm_c_re, m_ssm_c_im, m_ssm_d, m_ssm_w_glu, m_w_ssm_proj, m_hgrn_lb_logits, m_hgrn_norm_g, m_w_hgrn_proj, m_w_out, m_ffn_norm_g, m_w_up, m_conv_w, m_conv_b, m_w_down, m_final_norm_g, v_meta_tokens, v_mix_norm_g, v_w_in, v_ssm_lambda_re, v_ssm_lambda_im, v_ssm_log_dt, v_ssm_b_re, v_ssm_b_im, v_ssm_c_re, v_ssm_c_im, v_ssm_d, v_ssm_w_glu, v_w_ssm_proj, v_hgrn_lb_logits, v_hgrn_norm_g, v_w_hgrn_proj, v_w_out, v_ffn_norm_g, v_w_up, v_conv_w, v_conv_b, v_w_down, v_final_norm_g):
    args = dict(locals())
    w = {k: args[k] for k in WEIGHTS}
    mom = {k: args["m_" + k] for k in WEIGHTS}
    var = {k: args["v_" + k] for k in WEIGHTS}
    Dm = D_MODEL
    cx, cy, cc = lax.axis_index("x"), lax.axis_index("y"), lax.axis_index("c")
    chip = 2 * cx + cy

    shards = {k: w[k][0].astype(bf16) for k in BIG}
    g_meta, g_cw = _allgather_chips([w["meta_tokens"], w["conv_w"][0]])
    full = _full_weights({"w_in": _allgather_split([shards["w_in"]])[0]}, shards, chip)
    full["conv_w"] = g_cw.transpose(1, 0, 2).reshape(3, 2 * D_FF)
    meta_full = g_meta.transpose(1, 0, 2).reshape(N_META, Dm)

    core = cc.reshape(1).astype(jnp.int32)
    loss_part, grad_x, sums, slots, small = _local_grads(x, loss_target, meta_full, w, full, shards, chip, core)

    where = jnp.stack([chip, cc]).astype(jnp.int32)
    fulls = [_sum_chips("sum_chips_" + k, slots[k], sums[k], where) for k in BIG]
    g_big = dict(zip(BIG, _sibling_join(fulls)))

    small_all = dict(small)
    small_all["loss"] = loss_part[0, 0:1]
    packed = _pack(small_all)
    slots_dev = lax.dynamic_update_slice(_allgather_devices(packed), packed[None], (2 * chip + cc, 0, 0))
    reduced = _unpack(_sum_slots("sum_devices", slots_dev), small_all)
    loss = reduced.pop("loss")[0]
    mcols = Dm // N_CHIPS
    ccols = 2 * D_FF // N_CHIPS
    grads = {k: reduced[k] for k in SMALL}
    grads["meta_tokens"] = lax.dynamic_slice(reduced["meta_tokens"], (0, chip * mcols), (N_META, mcols))
    grads["conv_w"] = lax.dynamic_slice(reduced["conv_w"], (0, chip * ccols), (3, ccols))[None]
    for k in BIG:
        grads[k] = g_big[k][None]

    delta, new_m, new_v = {}, {}, {}
    for k in BIG:
        shp = w[k].shape
        d, nm, nv = _adamw("adamw_" + k, w[k][0], grads[k][0], mom[k][0], var[k][0])
        delta[k], new_m[k], new_v[k] = d.reshape(shp), nm.reshape(shp), nv.reshape(shp)
    rest = SMALL + SHARDED_SMALL
    pk = [_pack({k: t[k] for k in rest}) for t in (w, grads, mom, var)]
    outs = _adamw("adamw_small", *pk)
    like = {k: w[k] for k in rest}
    for dst, o in zip((delta, new_m, new_v), outs):
        dst.update(_unpack(o, like))

    return (loss, grad_x, *[grads[k].reshape(w[k].shape) for k in WEIGHTS], *[delta[k] for k in WEIGHTS],
            *[new_m[k] for k in WEIGHTS], *[new_v[k] for k in WEIGHTS])
```

```python
import functools
import math

import jax
import jax.numpy as jnp
from jax import lax
from jax.experimental import pallas as pl
from jax.experimental.pallas import tpu as pltpu

f32 = jnp.float32
bf16 = jnp.bfloat16

D_MODEL = 1024
N_META = 16
SSM_GROUP = 16
SSM_GROUPS = 64
SSM_STATE = 64
SLAB_GROUPS = 8
N_SLAB = SSM_GROUPS // SLAB_GROUPS
SLAB_CH = SLAB_GROUPS * SSM_GROUP
SLAB_NS = SLAB_GROUPS * SSM_STATE
HEADS = 8
HEAD_DIM = 128
CHUNK = 16
D_FF = 2816
IN_COLS = 7168
EPS = 1e-6
SUBLANES = 8
LANES = 128
N_CHIPS = 4
N_DEV = 8
ADAM_LR, ADAM_B1, ADAM_B2, ADAM_EPS, ADAM_WD, ADAM_STEP = 0.001, 0.9, 0.999, 1e-08, 0.01, 10
MESH = pl.DeviceIdType.MESH
ANY = pl.BlockSpec(memory_space=pl.ANY)

SEG_Q, SEG_F, SEG_I, SEG_OG, SEG_GA, SEG_GB, SEG_U = range(7)
N_SEG = 7


def _tile(n, target, mult=SUBLANES):
    best = None
    for d in range(mult, min(n, target) + 1, mult):
        if n % d == 0:
            best = d
    return n if best is None else best


def _params(*sem):
    return pltpu.CompilerParams(dimension_semantics=sem)


def _sigmoid(x):
    return 1.0 / (1.0 + jnp.exp(-x))


_DIMS = {"nn": (((1,), (0,)), ((), ())), "nt": (((1,), (1,)), ((), ())), "tn": (((0,), (0,)), ((), ()))}


def _mm(name, a, b, dims, grid, a_spec, b_spec, out_shape, out_spec, acc_shape, res=None, res_spec=None):
    nk = grid[2]
    dn = _DIMS[dims]

    def body(*refs):
        if res is None:
            a_ref, b_ref, o_ref, acc = refs
        else:
            a_ref, b_ref, r_ref, o_ref, acc = refs
        k = pl.program_id(2)

        @pl.when(k == 0)
        def _():
            acc[...] = jnp.zeros_like(acc)

        acc[...] += lax.dot_general(a_ref[...].astype(bf16), b_ref[...].astype(bf16), dn, preferred_element_type=f32)

        @pl.when(k == nk - 1)
        def _():
            r = acc[...]
            if res is not None:
                r = r + r_ref[...]
            o_ref[...] = r.astype(o_ref.dtype)

    ins = [a, b] + ([] if res is None else [res])
    specs = [a_spec, b_spec] + ([] if res is None else [res_spec])
    return pl.pallas_call(
        body, name=name, grid=grid, in_specs=specs, out_specs=out_spec, out_shape=out_shape,
        scratch_shapes=[pltpu.VMEM(acc_shape, f32)],
        compiler_params=_params("parallel", "parallel", "arbitrary"),
    )(*ins)


def _mm_rows(name, a, w, dims, out_dtype, tn, res=None, tk=None):
    T, K = a.shape
    N = w.shape[1] if dims == "nn" else w.shape[0]
    tm = _tile(T, 1032)
    tk = K if tk is None else tk
    grid = (T // tm, N // tn, K // tk)
    a_spec = pl.BlockSpec((tm, tk), lambda i, j, k: (i, k))
    if dims == "nn":
        b_spec = pl.BlockSpec((tk, tn), lambda i, j, k: (k, j))
    else:
        b_spec = pl.BlockSpec((tn, tk), lambda i, j, k: (j, k))
    o_spec = pl.BlockSpec((tm, tn), lambda i, j, k: (i, j))
    return _mm(name, a, w, dims, grid, a_spec, b_spec, jax.ShapeDtypeStruct((T, N), out_dtype), o_spec, (tm, tn),
               res=res, res_spec=None if res is None else o_spec)


def _mm_fused(name, pairs, dims, extras, epilogue, outs, rows=()):
    T, K = pairs[0][0].shape
    N = pairs[0][1].shape[1] if dims == "nn" else pairs[0][1].shape[0]
    tm = _tile(T, 344)
    tn = N
    grid = (T // tm, N // tn)
    npair, nex = len(pairs), len(extras) + len(rows)
    dn = _DIMS[dims]

    def body(*refs):
        ab = refs[:2 * npair]
        ex = refs[2 * npair:2 * npair + nex]
        o_refs = refs[2 * npair + nex:]
        accs = [lax.dot_general(ab[2 * q][...].astype(bf16), ab[2 * q + 1][...].astype(bf16), dn, preferred_element_type=f32)
                for q in range(npair)]
        vals = epilogue(accs, [e[...] for e in ex])
        for o_ref, v in zip(o_refs, vals):
            if isinstance(v, (list, tuple)):
                for s_, vs in enumerate(v):
                    o_ref[s_] = vs.astype(o_ref.dtype)
            else:
                o_ref[...] = v.astype(o_ref.dtype)

    ins, specs = [], []
    for a, w in pairs:
        ins += [a, w]
        specs.append(pl.BlockSpec((tm, K), lambda i, j: (i, 0)))
        specs.append(pl.BlockSpec((K, tn), lambda i, j: (0, j)) if dims == "nn" else pl.BlockSpec((tn, K), lambda i, j: (j, 0)))
    for arr, off in extras:
        ins.append(arr)
        specs.append(pl.BlockSpec((tm, tn), lambda i, j, off=off: (i, off + j)))
    for arr in rows:
        ins.append(arr)
        specs.append(pl.BlockSpec((1, tn), lambda i, j: (0, j)))
    shapes, ospecs = [], []
    for o in outs:
        if isinstance(o, tuple):
            dt, nseg, total, blk = o
            shapes.append(jax.ShapeDtypeStruct((total, T, N), dt))
            ospecs.append(pl.BlockSpec((nseg, tm, tn), lambda i, j, blk=blk: (blk, i, j)))
        else:
            shapes.append(jax.ShapeDtypeStruct((T, N), o))
            ospecs.append(pl.BlockSpec((tm, tn), lambda i, j: (i, j)))
    return pl.pallas_call(body, name=name, grid=grid, in_specs=specs, out_specs=ospecs, out_shape=shapes,
                          compiler_params=_params("parallel", "parallel"))(*ins)


def _glu_proj_fwd(ya0, w_glu):
    def epi(accs, tiles):
        return accs[0], tiles[0] * _sigmoid(accs[0])

    return _mm_fused("glu_proj", [(ya0, w_glu)], "nn", [(ya0, 0)], epi, [f32, bf16])


def _proj_merge_fwd(ya, yb, w_sp, w_hp, p):
    def epi(accs, tiles):
        return accs[0], accs[1], _sigmoid(tiles[0]) * accs[0] + _sigmoid(tiles[1]) * accs[1]

    return _mm_fused("proj_merge", [(ya, w_sp), (yb, w_hp)], "nn", [(p, SEG_GA), (p, SEG_GB)], epi, [f32, f32, bf16])


def _merge_bwd_fused(dh1, w_out, p, pa, pb):
    def epi(accs, tiles):
        d = accs[0]
        sa, sb = _sigmoid(tiles[0]), _sigmoid(tiles[1])
        return d * sa, d * sb, [d * tiles[2] * sa * (1.0 - sa), d * tiles[3] * sb * (1.0 - sb)]

    return _mm_fused("d_merged", [(dh1, w_out)], "nt", [(p, SEG_GA), (p, SEG_GB), (pa, 0), (pb, 0)], epi,
                     [bf16, bf16, (bf16, 2, N_SEG, SEG_GA // 2)])


def _out_proj_norm(merged, w_out, h0, g):
    def epi(accs, tiles):
        h1 = tiles[0] + accs[0]
        r = lax.rsqrt(jnp.mean(h1 * h1, axis=-1, keepdims=True) + EPS)
        return h1, h1 * r * tiles[1]

    return _mm_fused("out_proj", [(merged, w_out)], "nn", [(h0, 0)], epi, [f32, bf16], rows=[g])


def _mm_rmsnorm_bwd(name, a, b, grid, a_spec, b_spec, x, g, dres):
    T, Dm = x.shape
    tm = T // grid[0]
    nk = grid[2]

    def body(a_ref, b_ref, x_ref, g_ref, dres_ref, dx_ref, dg_ref, acc):
        i, k = pl.program_id(0), pl.program_id(2)

        @pl.when(k == 0)
        def _():
            acc[...] = jnp.zeros_like(acc)

        @pl.when((i == 0) & (k == 0))
        def _():
            dg_ref[...] = jnp.zeros_like(dg_ref)

        acc[...] += lax.dot_general(a_ref[...].astype(bf16), b_ref[...].astype(bf16), _DIMS["nt"], preferred_element_type=f32)

        @pl.when(k == nk - 1)
        def _():
            xv = x_ref[...]
            r = lax.rsqrt(jnp.mean(xv * xv, axis=-1, keepdims=True) + EPS)
            xn = xv * r
            dzv = acc[...]
            dzg = dzv * g_ref[...]
            dx_ref[...] = dres_ref[...] + r * (dzg - xn * jnp.mean(dzg * xn, axis=-1, keepdims=True))
            dg_ref[...] += jnp.sum(dzv * xn, axis=0, keepdims=True)

    row = pl.BlockSpec((tm, Dm), lambda i, j, k: (i, 0))
    par = pl.BlockSpec((1, Dm), lambda i, j, k: (0, 0))
    return pl.pallas_call(
        body, name=name, grid=grid, in_specs=[a_spec, b_spec, row, par, row], out_specs=[row, par],
        out_shape=[jax.ShapeDtypeStruct((T, Dm), f32), jax.ShapeDtypeStruct((1, Dm), f32)],
        scratch_shapes=[pltpu.VMEM((tm, Dm), f32)],
        compiler_params=_params("arbitrary", "arbitrary", "arbitrary"),
    )(a, b, x, g, dres)


def _glu_bwd_fused(dpa, w_sp, ya0, gl):
    def epi(accs, tiles):
        d = accs[0]
        s = _sigmoid(tiles[1])
        return d * tiles[0] * s * (1.0 - s), d * s

    return _mm_fused("d_ya", [(dpa, w_sp)], "nt", [(ya0, 0), (gl, 0)], epi, [bf16, f32])


def _mm_wgrad(name, a, g, tn=None):
    T, K = a.shape
    N = g.shape[1]
    tk = _tile(T, 688)
    tn = N if tn is None else tn
    grid = (1, N // tn, T // tk)
    a_spec = pl.BlockSpec((tk, K), lambda i, j, k: (k, 0))
    g_spec = pl.BlockSpec((tk, tn), lambda i, j, k: (k, j))
    o_spec = pl.BlockSpec((K, tn), lambda i, j, k: (0, j))
    return _mm(name, a, g, "tn", grid, a_spec, g_spec, jax.ShapeDtypeStruct((K, N), f32), o_spec, (K, tn))


def _rmsnorm_fwd(name, x, g):
    T, Dm = x.shape
    tr = _tile(T, 688)

    def body(x_ref, g_ref, z_ref):
        xv = x_ref[...]
        r = lax.rsqrt(jnp.mean(xv * xv, axis=-1, keepdims=True) + EPS)
        z_ref[...] = (xv * r * g_ref[...]).astype(z_ref.dtype)

    return pl.pallas_call(
        body, name=name, grid=(T // tr,),
        in_specs=[pl.BlockSpec((tr, Dm), lambda i: (i, 0)), pl.BlockSpec((1, Dm), lambda i: (0, 0))],
        out_specs=pl.BlockSpec((tr, Dm), lambda i: (i, 0)),
        out_shape=jax.ShapeDtypeStruct((T, Dm), bf16), compiler_params=_params("parallel"),
    )(x, g)


def _rmsnorm_bwd(name, x, g, dz, dres):
    T, Dm = x.shape
    tr = _tile(T, 688)

    def body(x_ref, g_ref, dz_ref, dres_ref, dx_ref, dg_ref):
        xv = x_ref[...]
        r = lax.rsqrt(jnp.mean(xv * xv, axis=-1, keepdims=True) + EPS)
        xn = xv * r
        dzv = dz_ref[...]
        dzg = dzv * g_ref[...]
        dx_ref[...] = dres_ref[...] + r * (dzg - xn * jnp.mean(dzg * xn, axis=-1, keepdims=True))

        @pl.when(pl.program_id(0) == 0)
        def _():
            dg_ref[...] = jnp.zeros_like(dg_ref)

        dg_ref[...] += jnp.sum(dzv * xn, axis=0, keepdims=True)

    row = pl.BlockSpec((tr, Dm), lambda i: (i, 0))
    par = pl.BlockSpec((1, Dm), lambda i: (0, 0))
    return pl.pallas_call(
        body, name=name, grid=(T // tr,), in_specs=[row, par, row, row], out_specs=[row, par],
        out_shape=[jax.ShapeDtypeStruct((T, Dm), f32), jax.ShapeDtypeStruct((1, Dm), f32)],
        compiler_params=_params("arbitrary"),
    )(x, g, dz, dres)


def _glu_fwd(ya0, gl):
    T, Dm = ya0.shape
    tr = _tile(T, 688)

    def body(y_ref, g_ref, o_ref):
        o_ref[...] = (y_ref[...] * _sigmoid(g_ref[...])).astype(o_ref.dtype)

    row = pl.BlockSpec((tr, Dm), lambda i: (i, 0))
    return pl.pallas_call(body, name="glu_fwd", grid=(T // tr,), in_specs=[row, row], out_specs=row,
                          out_shape=jax.ShapeDtypeStruct((T, Dm), bf16), compiler_params=_params("parallel"))(ya0, gl)


def _glu_bwd(dya, ya0, gl):
    T, Dm = ya0.shape
    tr = _tile(T, 688)

    def body(d_ref, y_ref, g_ref, dg_ref, dy_ref):
        s = _sigmoid(g_ref[...])
        d = d_ref[...]
        dg_ref[...] = (d * y_ref[...] * s * (1.0 - s)).astype(dg_ref.dtype)
        dy_ref[...] = d * s

    row = pl.BlockSpec((tr, Dm), lambda i: (i, 0))
    return pl.pallas_call(body, name="glu_bwd", grid=(T // tr,), in_specs=[row, row, row], out_specs=[row, row],
                          out_shape=[jax.ShapeDtypeStruct((T, Dm), bf16), jax.ShapeDtypeStruct((T, Dm), f32)],
                          compiler_params=_params("parallel"))(dya, ya0, gl)


def _merge_fwd(p, pa, pb):
    T, Dm = pa.shape
    tr = _tile(T, 688)

    def body(ga_ref, gb_ref, pa_ref, pb_ref, o_ref):
        o_ref[...] = (_sigmoid(ga_ref[...]) * pa_ref[...] + _sigmoid(gb_ref[...]) * pb_ref[...]).astype(o_ref.dtype)

    row = pl.BlockSpec((tr, Dm), lambda i: (i, 0))
    return pl.pallas_call(
        body, name="merge_fwd", grid=(T // tr,),
        in_specs=[pl.BlockSpec((tr, Dm), lambda i: (i, SEG_GA)), pl.BlockSpec((tr, Dm), lambda i: (i, SEG_GB)), row, row],
        out_specs=row, out_shape=jax.ShapeDtypeStruct((T, Dm), bf16), compiler_params=_params("parallel"),
    )(p, p, pa, pb)


def _merge_bwd(dm, p, pa, pb):
    T, Dm = pa.shape
    tr = _tile(T, 688)

    def body(dm_ref, ga_ref, gb_ref, pa_ref, pb_ref, dpa_ref, dpb_ref, dp_ref):
        d = dm_ref[...]
        sa = _sigmoid(ga_ref[...])
        sb = _sigmoid(gb_ref[...])
        dpa_ref[...] = (d * sa).astype(dpa_ref.dtype)
        dpb_ref[...] = (d * sb).astype(dpb_ref.dtype)
        dp_ref[0] = (d * pa_ref[...] * sa * (1.0 - sa)).astype(dp_ref.dtype)
        dp_ref[1] = (d * pb_ref[...] * sb * (1.0 - sb)).astype(dp_ref.dtype)

    row = pl.BlockSpec((tr, Dm), lambda i: (i, 0))
    return pl.pallas_call(
        body, name="merge_bwd", grid=(T // tr,),
        in_specs=[row, pl.BlockSpec((tr, Dm), lambda i: (i, SEG_GA)), pl.BlockSpec((tr, Dm), lambda i: (i, SEG_GB)), row, row],
        out_specs=[row, row, pl.BlockSpec((2, tr, Dm), lambda i: (SEG_GA // 2, i, 0))],
        out_shape=[jax.ShapeDtypeStruct((T, Dm), bf16), jax.ShapeDtypeStruct((T, Dm), bf16),
                   jax.ShapeDtypeStruct((N_SEG, T, Dm), bf16)],
        compiler_params=_params("parallel"),
    )(dm, p, p, pa, pb)


def _final_loss(h2x, tgt, g):
    T, Dm = h2x.shape
    tr = _tile(T, 512)

    def body(h_ref, t_ref, g_ref, dh_ref, loss_ref, dg_ref):
        hv = h_ref[...]
        r = lax.rsqrt(jnp.mean(hv * hv, axis=-1, keepdims=True) + EPS)
        xn = hv * r
        gv = g_ref[...]
        err = xn * gv - t_ref[...]
        dy = err * (1.0 / Dm)
        dyg = dy * gv
        dh_ref[...] = r * (dyg - xn * jnp.mean(dyg * xn, axis=-1, keepdims=True))

        @pl.when(pl.program_id(0) == 0)
        def _():
            dg_ref[...] = jnp.zeros_like(dg_ref)
            loss_ref[...] = jnp.zeros_like(loss_ref)

        dg_ref[...] += jnp.sum(dy * xn, axis=0, keepdims=True)
        loss_ref[...] += jnp.sum(err * err) * (0.5 / Dm)

    row = pl.BlockSpec((tr, Dm), lambda i: (i, 0))
    par = pl.BlockSpec((1, Dm), lambda i: (0, 0))
    return pl.pallas_call(
        body, name="final_loss", grid=(T // tr,), in_specs=[row, row, par],
        out_specs=[row, pl.BlockSpec((1, LANES), lambda i: (0, 0)), par],
        out_shape=[jax.ShapeDtypeStruct((T, Dm), f32), jax.ShapeDtypeStruct((1, LANES), f32), jax.ShapeDtypeStruct((1, Dm), f32)],
        compiler_params=_params("arbitrary"),
    )(h2x, tgt, g)


def _meta_grad(dh0_meta):
    B = dh0_meta.shape[0]

    def body(d_ref, o_ref):
        acc = d_ref[0]
        for b in range(1, B):
            acc = acc + d_ref[b]
        o_ref[...] = acc

    return pl.pallas_call(body, name="meta_grad", out_shape=jax.ShapeDtypeStruct(dh0_meta.shape[1:], f32))(dh0_meta)


def _shift_down(x, k, row):
    return jnp.where(row >= k, pltpu.roll(x, k, 0), 0.0)


def _shift_up(x, k, row):
    n = x.shape[0]
    return jnp.where(row < n - k, pltpu.roll(x, n - k, 0), 0.0)


def _conv_fwd(up, conv_w, conv_b, B, L):
    tc = 256
    nt = D_FF // tc

    def body(xa_ref, xb_ref, wa_ref, wb_ref, ba_ref, bb_ref, o_ref):
        row = lax.broadcasted_iota(jnp.int32, (L, tc), 0)

        def conv(x_ref, w_ref, b_ref):
            x = x_ref[...]
            return (b_ref[...] + w_ref[0:1, :] * _shift_down(x, 2, row) + w_ref[1:2, :] * _shift_down(x, 1, row)
                    + w_ref[2:3, :] * x)

        a = conv(xa_ref, wa_ref, ba_ref)
        b = conv(xb_ref, wb_ref, bb_ref)
        o_ref[...] = (a * _sigmoid(a) * b).astype(o_ref.dtype)

    return pl.pallas_call(
        body, name="conv_fwd", grid=(B, nt),
        in_specs=[pl.BlockSpec((L, tc), lambda b, j: (b, j)), pl.BlockSpec((L, tc), lambda b, j: (b, j + nt)),
                  pl.BlockSpec((3, tc), lambda b, j: (0, j)), pl.BlockSpec((3, tc), lambda b, j: (0, j + nt)),
                  pl.BlockSpec((1, tc), lambda b, j: (0, j)), pl.BlockSpec((1, tc), lambda b, j: (0, j + nt))],
        out_specs=pl.BlockSpec((L, tc), lambda b, j: (b, j)),
        out_shape=jax.ShapeDtypeStruct((B * L, D_FF), bf16), compiler_params=_params("parallel", "parallel"),
    )(up, up, conv_w, conv_w, conv_b, conv_b)


def _conv_bwd(up, dff, conv_w, conv_b, B, L):
    tc = 256
    nt = D_FF // tc

    def body(xa_ref, xb_ref, d_ref, wa_ref, wb_ref, ba_ref, bb_ref, dup_ref, dw_ref):
        row = lax.broadcasted_iota(jnp.int32, (L, tc), 0)
        xs, pre = [], []
        for x_ref, w_ref, b_ref in ((xa_ref, wa_ref, ba_ref), (xb_ref, wb_ref, bb_ref)):
            x = x_ref[...]
            x1 = _shift_down(x, 1, row)
            x2 = _shift_down(x, 2, row)
            xs.append((x, x1, x2))
            pre.append(b_ref[...] + w_ref[0:1, :] * x2 + w_ref[1:2, :] * x1 + w_ref[2:3, :] * x)
        a, b = pre
        s = _sigmoid(a)
        d = d_ref[...]
        grads = (d * b * s * (1.0 + a * (1.0 - s)), d * a * s)

        @pl.when(pl.program_id(1) == 0)
        def _():
            dw_ref[...] = jnp.zeros_like(dw_ref)

        for h, (gr, (x, x1, x2), w_ref) in enumerate(zip(grads, xs, (wa_ref, wb_ref))):
            dup_ref[h] = (w_ref[2:3, :] * gr + w_ref[1:2, :] * _shift_up(gr, 1, row)
                          + w_ref[0:1, :] * _shift_up(gr, 2, row)).astype(dup_ref.dtype)
            dw_ref[h, 0:1, :] += jnp.sum(gr * x2, axis=0, keepdims=True)
            dw_ref[h, 1:2, :] += jnp.sum(gr * x1, axis=0, keepdims=True)
            dw_ref[h, 2:3, :] += jnp.sum(gr * x, axis=0, keepdims=True)
            dw_ref[h, 3:4, :] += jnp.sum(gr, axis=0, keepdims=True)

    return pl.pallas_call(
        body, name="conv_bwd", grid=(nt, B),
        in_specs=[pl.BlockSpec((L, tc), lambda j, b: (b, j)), pl.BlockSpec((L, tc), lambda j, b: (b, j + nt)),
                  pl.BlockSpec((L, tc), lambda j, b: (b, j)),
                  pl.BlockSpec((3, tc), lambda j, b: (0, j)), pl.BlockSpec((3, tc), lambda j, b: (0, j + nt)),
                  pl.BlockSpec((1, tc), lambda j, b: (0, j)), pl.BlockSpec((1, tc), lambda j, b: (0, j + nt))],
        out_specs=[pl.BlockSpec((2, L, tc), lambda j, b: (0, b, j)), pl.BlockSpec((2, SUBLANES, tc), lambda j, b: (0, 0, j))],
        out_shape=[jax.ShapeDtypeStruct((2, B * L, D_FF), bf16), jax.ShapeDtypeStruct((2, SUBLANES, D_FF), f32)],
        compiler_params=_params("parallel", "arbitrary"),
    )(up, up, dff, conv_w, conv_w, conv_b, conv_b)


CONV_ROWS = 2 * SUBLANES


def _rows16(i):
    return pl.ds(pl.multiple_of(i * CONV_ROWS, CONV_ROWS), CONV_ROWS)


def _conv_taps(x_ref, i, row):
    x = x_ref[_rows16(i), :]
    live = jnp.where(i > 0, 1.0, 0.0)
    r0 = jnp.maximum(i * CONV_ROWS, 2)
    p1 = x_ref[pl.ds(r0 - 1, 1), :] * live
    p2 = x_ref[pl.ds(r0 - 2, 1), :] * live
    x1 = jnp.where(row == 0, p1, pltpu.roll(x, 1, 0))
    x2 = jnp.where(row == 0, p2, jnp.where(row == 1, p1, pltpu.roll(x, 2, 0)))
    return x, x1, x2


def _conv_bwd(up, dff, conv_w, conv_b, B, L):
    tc = 256
    nt = D_FF // tc
    n = L // CONV_ROWS

    def body(xa_ref, xb_ref, d_ref, wa_ref, wb_ref, ba_ref, bb_ref, dup_ref, dw_ref, ga_ref, gb_ref):
        row = lax.broadcasted_iota(jnp.int32, (CONV_ROWS, tc), 0)

        @pl.when(pl.program_id(1) == 0)
        def _():
            dw_ref[...] = jnp.zeros_like(dw_ref)

        zero_tail = jnp.zeros((CONV_ROWS, tc), f32)
        ga_ref[L:L + CONV_ROWS, :] = zero_tail
        gb_ref[L:L + CONV_ROWS, :] = zero_tail

        def fold(v):
            return v[0:SUBLANES, :] + v[SUBLANES:CONV_ROWS, :]

        def step(i, acc):
            taps_a = _conv_taps(xa_ref, i, row)
            taps_b = _conv_taps(xb_ref, i, row)
            a = ba_ref[...] + wa_ref[0:1, :] * taps_a[2] + wa_ref[1:2, :] * taps_a[1] + wa_ref[2:3, :] * taps_a[0]
            b = bb_ref[...] + wb_ref[0:1, :] * taps_b[2] + wb_ref[1:2, :] * taps_b[1] + wb_ref[2:3, :] * taps_b[0]
            s = _sigmoid(a)
            d = d_ref[_rows16(i), :]
            g_a = d * b * s * (1.0 + a * (1.0 - s))
            g_b = d * a * s
            ga_ref[_rows16(i), :] = g_a
            gb_ref[_rows16(i), :] = g_b
            new = []
            for g, (x, x1, x2) in ((g_a, taps_a), (g_b, taps_b)):
                new += [fold(g * x2), fold(g * x1), fold(g * x), fold(g)]
            return tuple(o + v for o, v in zip(acc, new))

        z = jnp.zeros((SUBLANES, tc), f32)
        acc = _repeat_loop(n, step, (z,) * 8)
        for h in range(2):
            for t in range(4):
                dw_ref[h, t:t + 1, :] += jnp.sum(acc[4 * h + t], axis=0, keepdims=True)

        def back(i, c):
            for h, (g_ref, w_ref) in enumerate(((ga_ref, wa_ref), (gb_ref, wb_ref))):
                g = g_ref[_rows16(i), :]
                n1 = g_ref[pl.ds(i * CONV_ROWS + CONV_ROWS, 1), :]
                n2 = g_ref[pl.ds(i * CONV_ROWS + CONV_ROWS + 1, 1), :]
                u1 = jnp.where(row == CONV_ROWS - 1, n1, pltpu.roll(g, CONV_ROWS - 1, 0))
                u2 = jnp.where(row == CONV_ROWS - 1, n2, jnp.where(row == CONV_ROWS - 2, n1, pltpu.roll(g, CONV_ROWS - 2, 0)))
                dup_ref[h, _rows16(i), :] = (w_ref[2:3, :] * g + w_ref[1:2, :] * u1 + w_ref[0:1, :] * u2).astype(dup_ref.dtype)
            return c

        _repeat_loop(n, back, 0)

    return pl.pallas_call(
        body, name="conv_bwd", grid=(nt, B),
        in_specs=[pl.BlockSpec((L, tc), lambda j, b: (b, j)), pl.BlockSpec((L, tc), lambda j, b: (b, j + nt)),
                  pl.BlockSpec((L, tc), lambda j, b: (b, j)),
                  pl.BlockSpec((3, tc), lambda j, b: (0, j)), pl.BlockSpec((3, tc), lambda j, b: (0, j + nt)),
                  pl.BlockSpec((1, tc), lambda j, b: (0, j)), pl.BlockSpec((1, tc), lambda j, b: (0, j + nt))],
        out_specs=[pl.BlockSpec((2, L, tc), lambda j, b: (0, b, j)), pl.BlockSpec((2, SUBLANES, tc), lambda j, b: (0, 0, j))],
        out_shape=[jax.ShapeDtypeStruct((2, B * L, D_FF), bf16), jax.ShapeDtypeStruct((2, SUBLANES, D_FF), f32)],
        scratch_shapes=[pltpu.VMEM((L + CONV_ROWS, tc), f32), pltpu.VMEM((L + CONV_ROWS, tc), f32)],
        compiler_params=_params("parallel", "arbitrary"),
    )(up, up, dff, conv_w, conv_w, conv_b, conv_b)


GELU_C = math.sqrt(2.0 / math.pi)
GELU_A = 0.044715


def _gelu(x):
    return 0.5 * x * (1.0 + jnp.tanh(GELU_C * (x + GELU_A * x * x * x)))


def _gelu_grad(x):
    t = jnp.tanh(GELU_C * (x + GELU_A * x * x * x))
    return 0.5 * (1.0 + t) + 0.5 * x * (1.0 - t * t) * GELU_C * (1.0 + 3.0 * GELU_A * x * x)


def _cmul_add(xr, xi, ar, ai, sr, si):
    return xr + ar * sr - ai * si, xi + ar * si + ai * sr


def _s5_scan_fwd(s_ref, pw_ref, L):
    ns = SLAB_NS
    row = lax.broadcasted_iota(jnp.int32, (SUBLANES, ns), 0)
    pr = pw_ref[0, 0:SUBLANES, :]
    pi = pw_ref[1, 0:SUBLANES, :]

    def step(i, carry):
        cr, ci = carry
        r0 = pl.multiple_of(i * SUBLANES, SUBLANES)
        xr = s_ref[pl.ds(r0, SUBLANES), 0:ns]
        xi = s_ref[pl.ds(r0, SUBLANES), ns:2 * ns]
        for k in (1, 2, 4):
            xr, xi = _cmul_add(xr, xi, pr[k - 1:k, :], pi[k - 1:k, :], _shift_down(xr, k, row), _shift_down(xi, k, row))
        xr, xi = _cmul_add(xr, xi, pr, pi, cr, ci)
        s_ref[pl.ds(r0, SUBLANES), 0:ns] = xr
        s_ref[pl.ds(r0, SUBLANES), ns:2 * ns] = xi
        return xr[SUBLANES - 1:SUBLANES, :], xi[SUBLANES - 1:SUBLANES, :]

    z = jnp.zeros((1, ns), f32)
    lax.fori_loop(0, L // SUBLANES, step, (z, z))


def _s5_project_in(u_ref, bs_ref, s_ref, L, rc):
    for r in range(0, L, rc):
        s_ref[r:r + rc, :] = jnp.dot(u_ref[r:r + rc, :].astype(bf16), bs_ref[...], preferred_element_type=f32)


def _s5_fwd(p, bs, cs, pw, d_skip, B, L):
    rc = _tile(L, 344)

    def body(u_ref, bs_ref, cs_ref, pw_ref, d_ref, y_ref, s_ref):
        _s5_project_in(u_ref, bs_ref, s_ref, L, rc)
        _s5_scan_fwd(s_ref, pw_ref, L)
        for r in range(0, L, rc):
            ypre = (jnp.dot(s_ref[r:r + rc, :].astype(bf16), cs_ref[...], preferred_element_type=f32)
                    + d_ref[...] * u_ref[r:r + rc, :])
            y_ref[r:r + rc, :] = _gelu(ypre)

    ucol = SEG_U * (D_MODEL // SLAB_CH)
    return pl.pallas_call(
        body, name="s5_fwd", grid=(B, N_SLAB),
        in_specs=[pl.BlockSpec((L, SLAB_CH), lambda b, s: (b, ucol + s)),
                  pl.BlockSpec((None, SLAB_CH, 2 * SLAB_NS), lambda b, s: (s, 0, 0)),
                  pl.BlockSpec((None, 2 * SLAB_NS, SLAB_CH), lambda b, s: (s, 0, 0)),
                  pl.BlockSpec((None, 2, 2 * SUBLANES, SLAB_NS), lambda b, s: (s, 0, 0, 0)),
                  pl.BlockSpec((1, SLAB_CH), lambda b, s: (0, s))],
        out_specs=pl.BlockSpec((L, SLAB_CH), lambda b, s: (b, s)),
        out_shape=jax.ShapeDtypeStruct((B * L, D_MODEL), f32),
        scratch_shapes=[pltpu.VMEM((L, 2 * SLAB_NS), f32)],
        compiler_params=_params("parallel", "parallel"),
    )(p, bs, cs, pw, d_skip)


def _s5_bwd(p, dya0, dp, bs, cs, pw, d_skip, B, L):
    rc = _tile(L, 344)
    ns = SLAB_NS
    nt = L // SUBLANES

    def body(u_ref, dy_ref, dp_in, bs_ref, cs_ref, pw_ref, d_ref, du_ref, dbs_ref, dcs_ref, da_ref, dd_ref,
             s_ref, lam_ref, dyp_ref):
        del dp_in
        b = pl.program_id(1)

        @pl.when(b == 0)
        def _():
            dbs_ref[...] = jnp.zeros_like(dbs_ref)
            dcs_ref[...] = jnp.zeros_like(dcs_ref)
            da_ref[...] = jnp.zeros_like(da_ref)
            dd_ref[...] = jnp.zeros_like(dd_ref)

        _s5_project_in(u_ref, bs_ref, s_ref, L, rc)
        _s5_scan_fwd(s_ref, pw_ref, L)
        for r in range(0, L, rc):
            u = u_ref[r:r + rc, :]
            sb = s_ref[r:r + rc, :].astype(bf16)
            ypre = jnp.dot(sb, cs_ref[...], preferred_element_type=f32) + d_ref[...] * u
            dyp = dy_ref[r:r + rc, :] * _gelu_grad(ypre)
            dyp_ref[r:r + rc, :] = dyp
            dd_ref[...] += jnp.sum(dyp * u, axis=0, keepdims=True)
            dypb = dyp.astype(bf16)
            dcs_ref[...] += lax.dot_general(sb, dypb, _DIMS["tn"], preferred_element_type=f32)
            lam_ref[r:r + rc, :] = lax.dot_general(dypb, cs_ref[...], _DIMS["nt"], preferred_element_type=f32)

        row = lax.broadcasted_iota(jnp.int32, (SUBLANES, ns), 0)
        pr = pw_ref[0, 0:SUBLANES, :]
        pi = -pw_ref[1, 0:SUBLANES, :]
        qr = pw_ref[0, SUBLANES:2 * SUBLANES, :]
        qi = -pw_ref[1, SUBLANES:2 * SUBLANES, :]

        def step(j, carry):
            cr, ci, ar, ai = carry
            i = nt - 1 - j
            r0 = pl.multiple_of(i * SUBLANES, SUBLANES)
            xr = lam_ref[pl.ds(r0, SUBLANES), 0:ns]
            xi = lam_ref[pl.ds(r0, SUBLANES), ns:2 * ns]
            for k in (1, 2, 4):
                xr, xi = _cmul_add(xr, xi, pr[k - 1:k, :], pi[k - 1:k, :], _shift_up(xr, k, row), _shift_up(xi, k, row))
            xr, xi = _cmul_add(xr, xi, qr, qi, cr, ci)
            lam_ref[pl.ds(r0, SUBLANES), 0:ns] = xr
            lam_ref[pl.ds(r0, SUBLANES), ns:2 * ns] = xi
            rp = pl.multiple_of(jnp.maximum(i - 1, 0) * SUBLANES, SUBLANES)
            live = jnp.where(i > 0, 1.0, 0.0)
            lr_ = s_ref[pl.ds(rp + SUBLANES - 1, 1), 0:ns] * live
            li_ = s_ref[pl.ds(rp + SUBLANES - 1, 1), ns:2 * ns] * live
            spr = jnp.where(row == 0, lr_, pltpu.roll(s_ref[pl.ds(r0, SUBLANES), 0:ns], 1, 0))
            spi = jnp.where(row == 0, li_, pltpu.roll(s_ref[pl.ds(r0, SUBLANES), ns:2 * ns], 1, 0))
            ar = ar + xr * spr + xi * spi
            ai = ai + xi * spr - xr * spi
            return xr[0:1, :], xi[0:1, :], ar, ai

        z1 = jnp.zeros((1, ns), f32)
        z8 = jnp.zeros((SUBLANES, ns), f32)
        _, _, ar, ai = lax.fori_loop(0, nt, step, (z1, z1, z8, z8))
        da_ref[0:1, :] += jnp.sum(ar, axis=0, keepdims=True)
        da_ref[1:2, :] += jnp.sum(ai, axis=0, keepdims=True)

        for r in range(0, L, rc):
            lamb = lam_ref[r:r + rc, :].astype(bf16)
            dbs_ref[...] += lax.dot_general(u_ref[r:r + rc, :].astype(bf16), lamb, _DIMS["tn"], preferred_element_type=f32)
            du = (lax.dot_general(lamb, bs_ref[...], _DIMS["nt"], preferred_element_type=f32)
                  + d_ref[...] * dyp_ref[r:r + rc, :])
            du_ref[r:r + rc, :] = du.astype(du_ref.dtype)

    ucol = SEG_U * (D_MODEL // SLAB_CH)
    T = B * L
    return pl.pallas_call(
        body, name="s5_bwd", grid=(N_SLAB, B),
        in_specs=[pl.BlockSpec((L, SLAB_CH), lambda s, b: (b, ucol + s)),
                  pl.BlockSpec((L, SLAB_CH), lambda s, b: (b, s)),
                  ANY,
                  pl.BlockSpec((None, SLAB_CH, 2 * SLAB_NS), lambda s, b: (s, 0, 0)),
                  pl.BlockSpec((None, 2 * SLAB_NS, SLAB_CH), lambda s, b: (s, 0, 0)),
                  pl.BlockSpec((None, 2, 2 * SUBLANES, SLAB_NS), lambda s, b: (s, 0, 0, 0)),
                  pl.BlockSpec((1, SLAB_CH), lambda s, b: (0, s))],
        out_specs=[pl.BlockSpec((None, L, SLAB_CH), lambda s, b: (SEG_U, b, s)),
                   pl.BlockSpec((None, SLAB_CH, 2 * SLAB_NS), lambda s, b: (s, 0, 0)),
                   pl.BlockSpec((None, 2 * SLAB_NS, SLAB_CH), lambda s, b: (s, 0, 0)),
                   pl.BlockSpec((None, 2, SLAB_NS), lambda s, b: (s, 0, 0)),
                   pl.BlockSpec((1, SLAB_CH), lambda s, b: (0, s))],
        out_shape=[jax.ShapeDtypeStruct((N_SEG, T, D_MODEL), bf16),
                   jax.ShapeDtypeStruct((N_SLAB, SLAB_CH, 2 * SLAB_NS), f32),
                   jax.ShapeDtypeStruct((N_SLAB, 2 * SLAB_NS, SLAB_CH), f32),
                   jax.ShapeDtypeStruct((N_SLAB, 2, SLAB_NS), f32),
                   jax.ShapeDtypeStruct((1, D_MODEL), f32)],
        scratch_shapes=[pltpu.VMEM((L, 2 * SLAB_NS), f32), pltpu.VMEM((L, 2 * SLAB_NS), f32), pltpu.VMEM((L, SLAB_CH), f32)],
        input_output_aliases={2: 0},
        compiler_params=_params("parallel", "arbitrary"),
    )(p, dya0, dp, bs, cs, pw, d_skip)


def _rows8(i):
    return pl.ds(pl.multiple_of(i * SUBLANES, SUBLANES), SUBLANES)


def _repeat_loop(n, step, init):
    rep = max(u for u in (6, 4, 3, 2, 1) if n % u == 0)

    def body(t, carry):
        for u in range(rep):
            carry = step(t * rep + u, carry)
        return carry

    return lax.fori_loop(0, n // rep, body, init)


def _to_segments(src_ref, dst_ref, seg):
    def step(i, c):
        dst_ref[_rows8(i), :] = src_ref[pl.ds(i, SUBLANES, stride=seg), :]
        return c

    _repeat_loop(seg, step, 0)


def _from_segments(src_ref, dst_ref, seg):
    def step(i, c):
        dst_ref[pl.ds(i, SUBLANES, stride=seg), :] = src_ref[_rows8(i), :]
        return c

    _repeat_loop(seg, step, 0)


def _seg_local_scan(s_ref, ar, ai, seg, reverse):
    ns = SLAB_NS

    def step(j, carry):
        cr, ci = carry
        rows = _rows8(seg - 1 - j if reverse else j)
        cr, ci = _cmul_add(s_ref[rows, 0:ns], s_ref[rows, ns:2 * ns], ar, ai, cr, ci)
        s_ref[rows, 0:ns] = cr
        s_ref[rows, ns:2 * ns] = ci
        return cr, ci

    z = jnp.zeros((SUBLANES, ns), f32)
    return _repeat_loop(seg, step, (z, z))


def _seg_boundaries(fr, fi, alr, ali, reverse):
    row = lax.broadcasted_iota(jnp.int32, fr.shape, 0)
    br = jnp.zeros_like(fr)
    bi = jnp.zeros_like(fi)
    for r in (range(SUBLANES - 2, -1, -1) if reverse else range(1, SUBLANES)):
        s = r + 1 if reverse else r - 1
        nr, ni = _cmul_add(fr[s:s + 1, :], fi[s:s + 1, :], alr, ali, br[s:s + 1, :], bi[s:s + 1, :])
        br = jnp.where(row == r, nr, br)
        bi = jnp.where(row == r, ni, bi)
    return br, bi


def _s5_states(u_ref, bs_ref, pw_ref, up_ref, s_ref, L, rc):
    seg = L // SUBLANES
    ns = SLAB_NS
    _to_segments(u_ref, up_ref, seg)
    _s5_project_in(up_ref, bs_ref, s_ref, L, rc)
    ar, ai = pw_ref[0, 0:1, :], pw_ref[1, 0:1, :]
    fr, fi = _seg_local_scan(s_ref, ar, ai, seg, False)
    br, bi = _seg_boundaries(fr, fi, pw_ref[0, seg - 1:seg, :], pw_ref[1, seg - 1:seg, :], False)

    def fix(i, c):
        rows = _rows8(i)
        xr, xi = _cmul_add(s_ref[rows, 0:ns], s_ref[rows, ns:2 * ns], pw_ref[0, pl.ds(i, 1), :], pw_ref[1, pl.ds(i, 1), :], br, bi)
        s_ref[rows, 0:ns] = xr
        s_ref[rows, ns:2 * ns] = xi
        return c

    _repeat_loop(seg, fix, 0)


def _pw_spec(seg_rows, order):
    if order == "bs":
        return pl.BlockSpec((2, seg_rows, SLAB_NS), lambda b, s: (0, 0, s))
    return pl.BlockSpec((2, seg_rows, SLAB_NS), lambda s, b: (0, 0, s))


def _s5_fwd(p, bs, cs, pw, d_skip, B, L):
    rc = _tile(L, 344)
    seg = L // SUBLANES

    def body(u_ref, bs_ref, cs_ref, pw_ref, d_ref, y_ref, s_ref, up_ref, yp_ref):
        _s5_states(u_ref, bs_ref, pw_ref, up_ref, s_ref, L, rc)
        for r in range(0, L, rc):
            ypre = (jnp.dot(s_ref[r:r + rc, :].astype(bf16), cs_ref[...], preferred_element_type=f32)
                    + d_ref[...] * up_ref[r:r + rc, :])
            yp_ref[r:r + rc, :] = _gelu(ypre)
        _from_segments(yp_ref, y_ref, seg)

    ucol = SEG_U * (D_MODEL // SLAB_CH)
    return pl.pallas_call(
        body, name="s5_fwd", grid=(B, N_SLAB),
        in_specs=[pl.BlockSpec((L, SLAB_CH), lambda b, s: (b, ucol + s)),
                  pl.BlockSpec((None, SLAB_CH, 2 * SLAB_NS), lambda b, s: (s, 0, 0)),
                  pl.BlockSpec((None, 2 * SLAB_NS, SLAB_CH), lambda b, s: (s, 0, 0)),
                  _pw_spec(pw.shape[1], "bs"),
                  pl.BlockSpec((1, SLAB_CH), lambda b, s: (0, s))],
        out_specs=pl.BlockSpec((L, SLAB_CH), lambda b, s: (b, s)),
        out_shape=jax.ShapeDtypeStruct((B * L, D_MODEL), f32),
        scratch_shapes=[pltpu.VMEM((L, 2 * SLAB_NS), f32), pltpu.VMEM((L, SLAB_CH), f32), pltpu.VMEM((L, SLAB_CH), f32)],
        compiler_params=_params("parallel", "parallel"),
    )(p, bs, cs, pw, d_skip)


def _s5_bwd(p, dya0, dp, bs, cs, pw, d_skip, B, L, sums):
    rc = _tile(L, 344)
    ns = SLAB_NS
    seg = L // SUBLANES
    nx = len(sums)

    def body(u_ref, dy_ref, dp_in, bs_ref, cs_ref, pw_ref, d_ref, *rest):
        xin, (du_ref, dbs_ref, dcs_ref, da_ref, dd_ref), xout = rest[:nx], rest[nx:nx + 5], rest[nx + 5:2 * nx + 5]
        s_ref, lam_ref, up_ref, dyp_ref, nat_ref, send, recv = rest[2 * nx + 5:]
        del dp_in
        start, finish = _chip_exchange_steps(xin, xout, send, recv)

        @pl.when((pl.program_id(0) == 0) & (pl.program_id(1) == 0))
        def _():
            start()

        @pl.when(pl.program_id(1) == 0)
        def _():
            dbs_ref[...] = jnp.zeros_like(dbs_ref)
            dcs_ref[...] = jnp.zeros_like(dcs_ref)
            da_ref[...] = jnp.zeros_like(da_ref)
            dd_ref[...] = jnp.zeros_like(dd_ref)

        _s5_states(u_ref, bs_ref, pw_ref, up_ref, s_ref, L, rc)
        _to_segments(dy_ref, dyp_ref, seg)
        for r in range(0, L, rc):
            u = up_ref[r:r + rc, :]
            sb = s_ref[r:r + rc, :].astype(bf16)
            ypre = jnp.dot(sb, cs_ref[...], preferred_element_type=f32) + d_ref[...] * u
            dyp = dyp_ref[r:r + rc, :] * _gelu_grad(ypre)
            dyp_ref[r:r + rc, :] = dyp
            dd_ref[...] += jnp.sum(dyp * u, axis=0, keepdims=True)
            dypb = dyp.astype(bf16)
            dcs_ref[...] += lax.dot_general(sb, dypb, _DIMS["tn"], preferred_element_type=f32)
            lam_ref[r:r + rc, :] = lax.dot_general(dypb, cs_ref[...], _DIMS["nt"], preferred_element_type=f32)

        ar, ai = pw_ref[0, 0:1, :], -pw_ref[1, 0:1, :]
        fr, fi = _seg_local_scan(lam_ref, ar, ai, seg, True)
        br, bi = _seg_boundaries(fr, fi, pw_ref[0, seg - 1:seg, :], -pw_ref[1, seg - 1:seg, :], True)

        def fix(i, acc):
            accr, acci = acc
            rows = _rows8(i)
            k = seg - 1 - i
            xr, xi = _cmul_add(lam_ref[rows, 0:ns], lam_ref[rows, ns:2 * ns], pw_ref[0, pl.ds(k, 1), :],
                               -pw_ref[1, pl.ds(k, 1), :], br, bi)
            lam_ref[rows, 0:ns] = xr
            lam_ref[rows, ns:2 * ns] = xi
            prev = _rows8(jnp.maximum(i - 1, 0))
            live = jnp.where(i > 0, 1.0, 0.0)
            spr = s_ref[prev, 0:ns] * live
            spi = s_ref[prev, ns:2 * ns] * live
            return accr + xr * spr + xi * spi, acci + xi * spr - xr * spi

        z = jnp.zeros((SUBLANES, ns), f32)
        accr, acci = _repeat_loop(seg, fix, (z, z))
        row = lax.broadcasted_iota(jnp.int32, (SUBLANES, ns), 0)
        last = _rows8(seg - 1)
        spr = jnp.where(row == 0, 0.0, pltpu.roll(s_ref[last, 0:ns], 1, 0))
        spi = jnp.where(row == 0, 0.0, pltpu.roll(s_ref[last, ns:2 * ns], 1, 0))
        xr, xi = lam_ref[0:SUBLANES, 0:ns], lam_ref[0:SUBLANES, ns:2 * ns]
        accr = accr + xr * spr + xi * spi
        acci = acci + xi * spr - xr * spi
        da_ref[0:1, :] += jnp.sum(accr, axis=0, keepdims=True)
        da_ref[1:2, :] += jnp.sum(acci, axis=0, keepdims=True)

        for r in range(0, L, rc):
            lamb = lam_ref[r:r + rc, :].astype(bf16)
            dbs_ref[...] += lax.dot_general(up_ref[r:r + rc, :].astype(bf16), lamb, _DIMS["tn"], preferred_element_type=f32)
            nat_ref[r:r + rc, :] = (lax.dot_general(lamb, bs_ref[...], _DIMS["nt"], preferred_element_type=f32)
                                    + d_ref[...] * dyp_ref[r:r + rc, :])
        _from_segments(nat_ref, up_ref, seg)
        du_ref[...] = up_ref[...].astype(du_ref.dtype)

        @pl.when((pl.program_id(0) == N_SLAB - 1) & (pl.program_id(1) == B - 1))
        def _():
            finish()

    ucol = SEG_U * (D_MODEL // SLAB_CH)
    T = B * L
    col = pltpu.VMEM((L, SLAB_CH), f32)
    res = pl.pallas_call(
        body, name="s5_bwd", grid=(N_SLAB, B),
        in_specs=[pl.BlockSpec((L, SLAB_CH), lambda s, b: (b, ucol + s)),
                  pl.BlockSpec((L, SLAB_CH), lambda s, b: (b, s)),
                  ANY,
                  pl.BlockSpec((None, SLAB_CH, 2 * SLAB_NS), lambda s, b: (s, 0, 0)),
                  pl.BlockSpec((None, 2 * SLAB_NS, SLAB_CH), lambda s, b: (s, 0, 0)),
                  _pw_spec(pw.shape[1], "sb"),
                  pl.BlockSpec((1, SLAB_CH), lambda s, b: (0, s))] + [ANY] * nx,
        out_specs=[pl.BlockSpec((None, L, SLAB_CH), lambda s, b: (SEG_U, b, s)),
                   pl.BlockSpec((None, SLAB_CH, 2 * SLAB_NS), lambda s, b: (s, 0, 0)),
                   pl.BlockSpec((None, 2 * SLAB_NS, SLAB_CH), lambda s, b: (s, 0, 0)),
                   pl.BlockSpec((None, 2, SLAB_NS), lambda s, b: (s, 0, 0)),
                   pl.BlockSpec((1, SLAB_CH), lambda s, b: (0, s))] + [ANY] * nx,
        out_shape=[jax.ShapeDtypeStruct((N_SEG, T, D_MODEL), bf16),
                   jax.ShapeDtypeStruct((N_SLAB, SLAB_CH, 2 * SLAB_NS), f32),
                   jax.ShapeDtypeStruct((N_SLAB, 2 * SLAB_NS, SLAB_CH), f32),
                   jax.ShapeDtypeStruct((N_SLAB, 2, SLAB_NS), f32),
                   jax.ShapeDtypeStruct((1, D_MODEL), f32)] + [jax.ShapeDtypeStruct(a.shape, a.dtype) for a in sums],
        scratch_shapes=[pltpu.VMEM((L, 2 * SLAB_NS), f32), pltpu.VMEM((L, 2 * SLAB_NS), f32), col, col, col]
        + _chip_exchange_sems(nx),
        input_output_aliases={2: 0},
        compiler_params=_params("arbitrary", "arbitrary"),
    )(p, dya0, dp, bs, cs, pw, d_skip, *sums)
    return res[:5], res[5:]


def _dotb(a, b, dims="nn"):
    return lax.dot_general(a.astype(bf16), b.astype(bf16), _DIMS[dims], preferred_element_type=f32)


def _chunk_cumsum(x, pos):
    k = 1
    while k < CHUNK:
        x = x + jnp.where(pos >= k, pltpu.roll(x, k, 0), 0.0)
        k *= 2
    return x


def _chunk_rev_cumsum(x, pos):
    n = x.shape[0]
    k = 1
    while k < CHUNK:
        x = x + jnp.where(pos < CHUNK - k, pltpu.roll(x, n - k, 0), 0.0)
        k *= 2
    return x


def _hgrn_local(q, fl, lb, pos):
    sg = _sigmoid(fl)
    f = lb + (1.0 - lb) * sg
    g = jnp.log(f)
    cum = _chunk_cumsum(g, pos)
    rest = _chunk_rev_cumsum(g, pos) - g
    e = jnp.exp(cum)
    em = jnp.exp(-cum)
    eo = jnp.exp(rest)
    k = 1.0 - f
    return sg, f, e, em, eo, q * e, k * em, k * eo, jnp.exp(cum + rest)


def _hgrn_block_mask(n):
    r = lax.broadcasted_iota(jnp.int32, (n, n), 0)
    c = lax.broadcasted_iota(jnp.int32, (n, n), 1)
    return ((r & -CHUNK) == (c & -CHUNK)) & (c <= r)


def _chunk_pos(n):
    return lax.broadcasted_iota(jnp.int32, (n, HEAD_DIM), 0) & (CHUNK - 1)


def _hgrn_block_rows(L):
    return _tile(L, 688, CHUNK)


def _chunk_rows(c):
    return pl.ds(pl.multiple_of(c * CHUNK, CHUNK), CHUNK)


def _chunk_loop(nc, step):
    rep = max(u for u in range(1, 49) if nc % u == 0)

    def body(i, carry):
        for u in range(rep):
            step(i * rep + u)
        return carry

    lax.fori_loop(0, nc // rep, body, 0)


def _hgrn_specs(L, order):
    hb = D_MODEL // HEAD_DIM

    def spec(seg):
        if order == "bh":
            return pl.BlockSpec((L, HEAD_DIM), lambda b, h: (b, seg * hb + h))
        return pl.BlockSpec((L, HEAD_DIM), lambda h, b: (b, seg * hb + h))

    return [spec(SEG_Q), spec(SEG_F), spec(SEG_I), spec(SEG_OG)]


def _hgrn_fwd(p, lb, norm_g, B, L):
    nc = L // CHUNK

    rb = _hgrn_block_rows(L)

    def body(q_ref, f_ref, v_ref, og_ref, lb_ref, ng_ref, y_ref, qt_s, ko_s, vb_s, dec_s, o_s, u_s, sb_s):
        lbv = lb_ref[...]
        ngv = ng_ref[...]
        mask = _hgrn_block_mask(rb)
        pos = _chunk_pos(rb)

        for r in range(0, L, rb):
            rows = slice(r, r + rb)
            _, _, _, _, _, qt, kt, ko, dec = _hgrn_local(q_ref[rows, :], f_ref[rows, :], lbv, pos)
            vb = v_ref[rows, :].astype(bf16)
            qtb = qt.astype(bf16)
            pm = jnp.where(mask, _dotb(qtb, kt, "nt"), 0.0)
            o_s[rows, :] = _dotb(pm, vb)
            qt_s[rows, :] = qtb
            ko_s[rows, :] = ko.astype(bf16)
            vb_s[rows, :] = vb
            dec_s[rows, :] = dec

        def update(c):
            rows = _chunk_rows(c)
            u_s[c] = _dotb(vb_s[rows, :], ko_s[rows, :], "tn")

        def chain(c, st):
            sb_s[c] = st.astype(bf16)
            return st * dec_s[_chunk_rows(c), :][0:1, :] + u_s[c]

        def attend(c):
            rows = _chunk_rows(c)
            o_s[rows, :] += _dotb(qt_s[rows, :], sb_s[c], "nt")

        _chunk_loop(nc, update)
        lax.fori_loop(0, nc, chain, jnp.zeros((HEAD_DIM, HEAD_DIM), f32))
        _chunk_loop(nc, attend)

        for r in range(0, L, rb):
            rows = slice(r, r + rb)
            o = o_s[rows, :]
            og = og_ref[rows, :]
            on = o * lax.rsqrt(jnp.mean(o * o, axis=-1, keepdims=True) + EPS) * ngv
            y_ref[rows, :] = (on * og * _sigmoid(og)).astype(y_ref.dtype)

    return pl.pallas_call(
        body, name="hgrn_fwd", grid=(B, HEADS),
        in_specs=_hgrn_specs(L, "bh") + [pl.BlockSpec((1, HEAD_DIM), lambda b, h: (0, h)),
                                          pl.BlockSpec((1, HEAD_DIM), lambda b, h: (0, 0))],
        out_specs=pl.BlockSpec((L, HEAD_DIM), lambda b, h: (b, h)),
        out_shape=jax.ShapeDtypeStruct((B * L, D_MODEL), bf16),
        scratch_shapes=[pltpu.VMEM((L, HEAD_DIM), bf16), pltpu.VMEM((L, HEAD_DIM), bf16), pltpu.VMEM((L, HEAD_DIM), bf16),
                        pltpu.VMEM((L, HEAD_DIM), f32), pltpu.VMEM((L, HEAD_DIM), f32),
                        pltpu.VMEM((nc, HEAD_DIM, HEAD_DIM), f32), pltpu.VMEM((nc, HEAD_DIM, HEAD_DIM), bf16)],
        compiler_params=_params("parallel", "parallel"),
    )(p, p, p, p, lb, norm_g)


def _hgrn_bwd(p, dyb, dp, lb, norm_g, B, L):
    nc = L // CHUNK

    rb = _hgrn_block_rows(L)

    def body(q_ref, f_ref, v_ref, og_ref, dy_ref, dp_in, lb_ref, ng_ref, dseg_ref, dlb_ref, dng_ref,
             st_ref, u_s, dsb_s, qt_s, kt_s, ko_s, vb_s, do_s, dec_s, o_s, dqt_s, dkt_s, dko_s, dv_s, ddec_s):
        del dp_in
        lbv = lb_ref[...]
        ngv = ng_ref[...]
        mask = _hgrn_block_mask(rb)
        pos = _chunk_pos(rb)
        blocks = [slice(r, r + rb) for r in range(0, L, rb)]

        @pl.when(pl.program_id(1) == 0)
        def _():
            dlb_ref[...] = jnp.zeros_like(dlb_ref)

        @pl.when((pl.program_id(0) == 0) & (pl.program_id(1) == 0))
        def _():
            dng_ref[...] = jnp.zeros_like(dng_ref)

        def scores(rows):
            return jnp.where(mask, _dotb(qt_s[rows, :], kt_s[rows, :], "nt"), 0.0).astype(bf16)

        for rows in blocks:
            _, _, _, _, _, qt, kt, ko, dec = _hgrn_local(q_ref[rows, :], f_ref[rows, :], lbv, pos)
            qt_s[rows, :] = qt.astype(bf16)
            kt_s[rows, :] = kt.astype(bf16)
            ko_s[rows, :] = ko.astype(bf16)
            vb_s[rows, :] = v_ref[rows, :].astype(bf16)
            dec_s[rows, :] = dec
            o_s[rows, :] = _dotb(scores(rows), vb_s[rows, :])

        def update(c):
            rows = _chunk_rows(c)
            u_s[c] = _dotb(vb_s[rows, :], ko_s[rows, :], "tn")

        def chain(c, st):
            st_ref[c] = st
            return st * dec_s[_chunk_rows(c), :][0:1, :] + u_s[c]

        def attend(c):
            rows = _chunk_rows(c)
            o_s[rows, :] += _dotb(qt_s[rows, :], st_ref[c], "nt")

        _chunk_loop(nc, update)
        lax.fori_loop(0, nc, chain, jnp.zeros((HEAD_DIM, HEAD_DIM), f32))
        _chunk_loop(nc, attend)

        dng = jnp.zeros((1, HEAD_DIM), f32)
        for rows in blocks:
            o = o_s[rows, :]
            og = og_ref[rows, :]
            dy = dy_ref[rows, :]
            rs = lax.rsqrt(jnp.mean(o * o, axis=-1, keepdims=True) + EPS)
            xn = o * rs
            so = _sigmoid(og)
            dseg_ref[SEG_OG, rows, :] = (dy * xn * ngv * so * (1.0 + og * (1.0 - so))).astype(dseg_ref.dtype)
            don = dy * og * so
            dng = dng + jnp.sum(don * xn, axis=0, keepdims=True)
            dxo = don * ngv
            do = (rs * (dxo - xn * jnp.mean(dxo * xn, axis=-1, keepdims=True))).astype(bf16)
            do_s[rows, :] = do
            dpm = jnp.where(mask, _dotb(do, vb_s[rows, :], "nt"), 0.0).astype(bf16)
            dqt_s[rows, :] = _dotb(dpm, kt_s[rows, :])
            dkt_s[rows, :] = _dotb(dpm, qt_s[rows, :], "tn")
            dv_s[rows, :] = _dotb(scores(rows), do, "tn")
        dng_ref[...] += dng

        def rupdate(c):
            rows = _chunk_rows(c)
            u_s[c] = _dotb(do_s[rows, :], qt_s[rows, :], "tn")

        def rchain(j, dst):
            c = nc - 1 - j
            rows = _chunk_rows(c)
            dsb_s[c] = dst.astype(bf16)
            ddec_s[rows, :] = jnp.broadcast_to(jnp.sum(dst * st_ref[c], axis=0, keepdims=True), (CHUNK, HEAD_DIM))
            return dst * dec_s[rows, :][0:1, :] + u_s[c]

        def rattend(c):
            rows = _chunk_rows(c)
            dst = dsb_s[c]
            dqt_s[rows, :] += _dotb(do_s[rows, :], st_ref[c])
            dv_s[rows, :] += _dotb(ko_s[rows, :], dst, "nt")
            dko_s[rows, :] = _dotb(vb_s[rows, :], dst)

        _chunk_loop(nc, rupdate)
        lax.fori_loop(0, nc, rchain, jnp.zeros((HEAD_DIM, HEAD_DIM), f32))
        _chunk_loop(nc, rattend)

        dlb = jnp.zeros((1, HEAD_DIM), f32)
        for rows in blocks:
            sg, f, e, em, eo, qt, kt, ko, dec = _hgrn_local(q_ref[rows, :], f_ref[rows, :], lbv, pos)
            dqt = dqt_s[rows, :]
            dkt = dkt_s[rows, :]
            dko = dko_s[rows, :]
            dko_ko = dko * ko
            dcum = dqt * qt - dkt * kt - dko_ko
            chunk_tot = _chunk_cumsum(dko_ko, pos) + _chunk_rev_cumsum(dko_ko, pos) - dko_ko
            dcum = dcum + jnp.where(pos == CHUNK - 1, chunk_tot + ddec_s[rows, :] * dec, 0.0)
            df = _chunk_rev_cumsum(dcum, pos) / f - (dkt * em + dko * eo)
            dlb = dlb + jnp.sum(df * (1.0 - sg), axis=0, keepdims=True)
            dseg_ref[SEG_Q, rows, :] = (dqt * e).astype(dseg_ref.dtype)
            dseg_ref[SEG_F, rows, :] = (df * (1.0 - lbv) * sg * (1.0 - sg)).astype(dseg_ref.dtype)
            dseg_ref[SEG_I, rows, :] = dv_s[rows, :].astype(dseg_ref.dtype)
        dlb_ref[...] += dlb

    T = B * L
    sb = pltpu.VMEM((L, HEAD_DIM), bf16)
    sf = pltpu.VMEM((L, HEAD_DIM), f32)
    return pl.pallas_call(
        body, name="hgrn_bwd", grid=(HEADS, B),
        in_specs=_hgrn_specs(L, "hb") + [pl.BlockSpec((L, HEAD_DIM), lambda h, b: (b, h)), ANY,
                                          pl.BlockSpec((1, HEAD_DIM), lambda h, b: (0, h)),
                                          pl.BlockSpec((1, HEAD_DIM), lambda h, b: (0, 0))],
        out_specs=[pl.BlockSpec((4, L, HEAD_DIM), lambda h, b: (0, b, h)),
                   pl.BlockSpec((1, HEAD_DIM), lambda h, b: (0, h)),
                   pl.BlockSpec((1, HEAD_DIM), lambda h, b: (0, 0))],
        out_shape=[jax.ShapeDtypeStruct((N_SEG, T, D_MODEL), bf16), jax.ShapeDtypeStruct((1, D_MODEL), f32),
                   jax.ShapeDtypeStruct((1, HEAD_DIM), f32)],
        scratch_shapes=[pltpu.VMEM((nc, HEAD_DIM, HEAD_DIM), f32), pltpu.VMEM((nc, HEAD_DIM, HEAD_DIM), f32),
                        pltpu.VMEM((nc, HEAD_DIM, HEAD_DIM), bf16), sb, sb, sb, sb, sb, sf, sf, sf, sf, sf, sf, sf],
        input_output_aliases={5: 0},
        compiler_params=_params("arbitrary", "arbitrary"),
    )(p, p, p, p, dyb, dp, lb, norm_g)


def _dz1(dp, w_in_phys):
    _, T, Dm = dp.shape
    tm = _tile(T, 1032)
    return _mm("dz1", dp, w_in_phys, "nt", (T // tm, 1, N_SEG),
               pl.BlockSpec((None, tm, Dm), lambda i, j, k: (k, i, 0)),
               pl.BlockSpec((Dm, Dm), lambda i, j, k: (0, k)),
               jax.ShapeDtypeStruct((T, Dm), f32), pl.BlockSpec((tm, Dm), lambda i, j, k: (i, 0)), (tm, Dm))


def _dz1_norm(dp, w_in_phys, h0, g, dh1):
    _, T, Dm = dp.shape
    tm = _tile(T, 688)
    return _mm_rmsnorm_bwd("dz1", dp, w_in_phys, (T // tm, 1, N_SEG),
                           pl.BlockSpec((None, tm, Dm), lambda i, j, k: (k, i, 0)),
                           pl.BlockSpec((Dm, Dm), lambda i, j, k: (0, k)), h0, g, dh1)


def _dz2_norm(dup, w_up, h1, g, dh2):
    _, T, _ = dup.shape
    tm = _tile(T, 688)
    tk = D_FF // 2
    return _mm_rmsnorm_bwd("dz2", dup, w_up, (T // tm, 1, 4),
                           pl.BlockSpec((None, tm, tk), lambda i, j, k: (k // 2, i, k % 2)),
                           pl.BlockSpec((D_MODEL, tk), lambda i, j, k: (0, k)), h1, g, dh2)


def _dw_in(z1, dp):
    _, T, Dm = dp.shape
    tk = _tile(T, 1376)
    return _mm("dw_in", z1, dp, "tn", (1, N_SEG, T // tk),
               pl.BlockSpec((tk, Dm), lambda i, j, k: (k, 0)),
               pl.BlockSpec((None, tk, Dm), lambda i, j, k: (j, k, 0)),
               jax.ShapeDtypeStruct((N_SEG, Dm, Dm), f32),
               pl.BlockSpec((None, Dm, Dm), lambda i, j, k: (j, 0, 0)), (Dm, Dm))


def _dz2(dup, w_up):
    _, T, _ = dup.shape
    tm = _tile(T, 1032)
    tk = D_FF // 2
    return _mm("dz2", dup, w_up, "nt", (T // tm, 1, 4),
               pl.BlockSpec((None, tm, tk), lambda i, j, k: (k // 2, i, k % 2)),
               pl.BlockSpec((D_MODEL, tk), lambda i, j, k: (0, k)),
               jax.ShapeDtypeStruct((T, D_MODEL), f32), pl.BlockSpec((tm, D_MODEL), lambda i, j, k: (i, 0)), (tm, D_MODEL))


def _dw_up(z2, dup):
    _, T, _ = dup.shape
    tn = D_FF // 2
    tk = _tile(T, 688)
    return _mm("dw_up", z2, dup, "tn", (1, N_CHIPS, T // tk),
               pl.BlockSpec((tk, D_MODEL), lambda i, j, k: (k, 0)),
               pl.BlockSpec((None, tk, tn), lambda i, j, k: (j // 2, k, j % 2)),
               jax.ShapeDtypeStruct((N_CHIPS, D_MODEL, tn), f32),
               pl.BlockSpec((None, D_MODEL, tn), lambda i, j, k: (j, 0, 0)), (D_MODEL, tn))


def _place():
    x, y, c = lax.axis_index("x"), lax.axis_index("y"), lax.axis_index("c")
    chips = [(1 - x, y), (x, 1 - y), (1 - x, 1 - y)]
    return x, y, c, chips


def _allgather_chips(arrs):
    n = len(arrs)

    def body(*refs):
        ins, outs = refs[:n], refs[n:2 * n]
        send, recv, local = refs[2 * n:]
        x, y, c, chips = _place()
        me = 2 * x + y

        def copy(a, k, slot):
            px, py = chips[k]
            return pltpu.make_async_remote_copy(src_ref=ins[a], dst_ref=outs[a].at[slot], send_sem=send.at[3 * a + k],
                                                recv_sem=recv.at[3 * a + k], device_id=(px, py, c), device_id_type=MESH)

        for a in range(n):
            pltpu.make_async_copy(ins[a], outs[a].at[me], local.at[a]).start()
            for k in range(3):
                copy(a, k, me).start()
        for a in range(n):
            for k, (px, py) in enumerate(chips):
                copy(a, k, 2 * px + py).wait_recv()
        for a in range(n):
            pltpu.make_async_copy(ins[a], outs[a].at[me], local.at[a]).wait()
            for k in range(3):
                copy(a, k, me).wait_send()

    return pl.pallas_call(
        body, name="allgather_chips", in_specs=[ANY] * n, out_specs=[ANY] * n,
        out_shape=[jax.ShapeDtypeStruct((N_CHIPS,) + a.shape, a.dtype) for a in arrs],
        scratch_shapes=[pltpu.SemaphoreType.DMA((3 * n,)), pltpu.SemaphoreType.DMA((3 * n,)), pltpu.SemaphoreType.DMA((n,))],
    )(*arrs)


def _allgather_split(arrs):
    n = len(arrs)

    def body(*refs):
        start, finish = _gather_split_steps(refs[:n], refs[n:2 * n], *refs[2 * n:])
        start()
        finish()

    return pl.pallas_call(
        body, name="allgather_split", in_specs=[ANY] * n, out_specs=[ANY] * n,
        out_shape=[jax.ShapeDtypeStruct((N_CHIPS,) + a.shape, a.dtype) for a in arrs],
        scratch_shapes=_gather_split_sems(n),
    )(*arrs)


def _gather_split_sems(n):
    return [pltpu.SemaphoreType.DMA((3 * n,)) for _ in range(4)]


def _gather_split_steps(ins, outs, send, recv, fsend, frecv):
    n = len(ins)

    def place():
        x, y, c, chips = _place()
        return x, y, c, chips, 2 * x + y

    def half(a, core):
        rh = ins[a].shape[0] // 2
        return pl.ds(core * rh, rh)

    def copy(a, k, slot):
        x, y, c, chips, _ = place()
        px, py = chips[k]
        return pltpu.make_async_remote_copy(src_ref=ins[a].at[half(a, c), :], dst_ref=outs[a].at[slot, half(a, c), :],
                                            send_sem=send.at[3 * a + k], recv_sem=recv.at[3 * a + k],
                                            device_id=(px, py, c), device_id_type=MESH)

    def forward(a, k, core):
        x, y, c, chips, _ = place()
        px, py = chips[k]
        rows = outs[a].at[2 * px + py, half(a, core), :]
        return pltpu.make_async_remote_copy(src_ref=rows, dst_ref=rows, send_sem=fsend.at[3 * a + k],
                                            recv_sem=frecv.at[3 * a + k], device_id=(x, y, 1 - c), device_id_type=MESH)

    def start():
        me = place()[4]
        for a in range(n):
            for k in range(3):
                copy(a, k, me).start()

    def finish():
        x, y, c, chips, me = place()
        for a in range(n):
            for k, (px, py) in enumerate(chips):
                copy(a, k, 2 * px + py).wait_recv()
                forward(a, k, c).start()
        for a in range(n):
            for k in range(3):
                forward(a, k, 1 - c).wait_recv()
        for a in range(n):
            for k in range(3):
                copy(a, k, me).wait_send()
                forward(a, k, c).wait_send()

    return start, finish


def _in_proj_gather(z1, w_in, shards):
    n = len(shards)
    T, K = z1.shape
    N = w_in.shape[1]
    tm = _tile(T, 1032)
    tn = 1024
    grid = (T // tm, N // tn)

    def body(a_ref, b_ref, *rest):
        ins, o_ref, outs, sems = rest[:n], rest[n], rest[n + 1:2 * n + 1], rest[2 * n + 1:]
        start, finish = _gather_split_steps(ins, outs, *sems)
        i, j = pl.program_id(0), pl.program_id(1)

        @pl.when((i == 0) & (j == 0))
        def _():
            start()

        o_ref[...] = jnp.dot(a_ref[...], b_ref[...], preferred_element_type=f32)

        @pl.when((i == grid[0] - 1) & (j == grid[1] - 1))
        def _():
            finish()

    res = pl.pallas_call(
        body, name="in_proj", grid=grid,
        in_specs=[pl.BlockSpec((tm, K), lambda i, j: (i, 0)), pl.BlockSpec((K, tn), lambda i, j: (0, j))] + [ANY] * n,
        out_specs=[pl.BlockSpec((tm, tn), lambda i, j: (i, j))] + [ANY] * n,
        out_shape=[jax.ShapeDtypeStruct((T, N), f32)] + [jax.ShapeDtypeStruct((N_CHIPS,) + a.shape, a.dtype) for a in shards],
        scratch_shapes=_gather_split_sems(n),
        compiler_params=_params("arbitrary", "arbitrary"),
    )(z1, w_in, *shards)
    return res[0], res[1:]


def _sibling_halves(parts, name="sibling_halves"):
    n = len(parts)

    def body(*refs):
        ins, outs = refs[:n], refs[n:2 * n]
        send, recv = refs[2 * n:]
        x, y, c, _ = _place()

        def copy(a):
            rh = ins[a].shape[1] // 2
            return pltpu.make_async_remote_copy(src_ref=ins[a].at[:, pl.ds((1 - c) * rh, rh), :], dst_ref=outs[a],
                                                send_sem=send.at[a], recv_sem=recv.at[a], device_id=(x, y, 1 - c),
                                                device_id_type=MESH)

        for a in range(n):
            copy(a).start()
        for a in range(n):
            copy(a).wait_recv()
        for a in range(n):
            copy(a).wait_send()

    return pl.pallas_call(
        body, name=name, in_specs=[ANY] * n, out_specs=[ANY] * n,
        out_shape=[jax.ShapeDtypeStruct((a.shape[0], a.shape[1] // 2, a.shape[2]), a.dtype) for a in parts],
        scratch_shapes=[pltpu.SemaphoreType.DMA((n,)), pltpu.SemaphoreType.DMA((n,))],
    )(*parts)


def _add_own_half(name, part, got, core):
    nchip, R, C = part.shape
    rh = R // 2
    tr = _tile(rh, 256, 2 * SUBLANES)
    nt = rh // tr

    def body(core_ref, a_ref, b_ref, o_ref):
        del core_ref
        o_ref[...] = (a_ref[...] + b_ref[...]).astype(o_ref.dtype)

    return pl.pallas_call(
        body, name=name,
        grid_spec=pltpu.PrefetchScalarGridSpec(
            num_scalar_prefetch=1, grid=(nchip, nt),
            in_specs=[pl.BlockSpec((None, tr, C), lambda j, i, core_ref: (j, core_ref[0] * nt + i, 0)),
                      pl.BlockSpec((None, tr, C), lambda j, i, core_ref: (j, i, 0))],
            out_specs=pl.BlockSpec((None, tr, C), lambda j, i, core_ref: (j, i, 0))),
        out_shape=jax.ShapeDtypeStruct((nchip, rh, C), bf16), compiler_params=_params("parallel", "parallel"),
    )(core, part, got)


def _add_own_half_w_in(part, got, core):
    _, R, C = part.shape
    rh = R // 2
    tr = _tile(rh, 256, 2 * SUBLANES)
    nt = rh // tr
    tn = 256
    per_seg = C // tn
    per_chip = IN_COLS // N_CHIPS // tn

    def src(j):
        return ((j // per_seg + N_SEG - 1) % N_SEG, j % per_seg)

    def body(core_ref, a_ref, b_ref, o_ref):
        del core_ref
        o_ref[...] = (a_ref[...] + b_ref[...]).astype(o_ref.dtype)

    return pl.pallas_call(
        body, name="add_half_w_in",
        grid_spec=pltpu.PrefetchScalarGridSpec(
            num_scalar_prefetch=1, grid=(IN_COLS // tn, nt),
            in_specs=[pl.BlockSpec((None, tr, tn), lambda j, i, core_ref: (src(j)[0], core_ref[0] * nt + i, src(j)[1])),
                      pl.BlockSpec((None, tr, tn), lambda j, i, core_ref: (src(j)[0], i, src(j)[1]))],
            out_specs=pl.BlockSpec((None, tr, tn), lambda j, i, core_ref: (j // per_chip, i, j % per_chip))),
        out_shape=jax.ShapeDtypeStruct((N_CHIPS, rh, IN_COLS // N_CHIPS), bf16), compiler_params=_params("parallel", "parallel"),
    )(core, part, got)


def _chip_exchange(sums):
    n = len(sums)

    def body(*refs):
        start, finish = _chip_exchange_steps(refs[:n], refs[n:2 * n], *refs[2 * n:])
        start()
        finish()

    return pl.pallas_call(
        body, name="chip_exchange", in_specs=[ANY] * n, out_specs=[ANY] * n,
        out_shape=[jax.ShapeDtypeStruct(a.shape, a.dtype) for a in sums],
        scratch_shapes=_chip_exchange_sems(n),
    )(*sums)


def _chip_exchange_sems(n):
    return [pltpu.SemaphoreType.DMA((3 * n,)), pltpu.SemaphoreType.DMA((3 * n,))]


def _chip_exchange_steps(ins, outs, send, recv):
    n = len(ins)

    def copy(a, k, own_slot):
        x, y, c, chips = _place()
        px, py = chips[k]
        slot = 2 * x + y if own_slot else 2 * px + py
        return pltpu.make_async_remote_copy(src_ref=ins[a].at[2 * px + py], dst_ref=outs[a].at[slot], send_sem=send.at[3 * a + k],
                                            recv_sem=recv.at[3 * a + k], device_id=(px, py, c), device_id_type=MESH)

    def start():
        for a in range(n):
            for k in range(3):
                copy(a, k, True).start()

    def finish():
        for a in range(n):
            for k in range(3):
                copy(a, k, False).wait_recv()
        for a in range(n):
            for k in range(3):
                copy(a, k, True).wait_send()

    return start, finish


def _sum_chips(name, slots, sums, where):
    nchip, rh, C = slots.shape
    tr = _tile(rh, 256, 2 * SUBLANES)
    nt = rh // tr

    def body(where_ref, own_ref, s1_ref, s2_ref, s3_ref, o_ref):
        me = where_ref[0]
        by_dist = [r[...].astype(f32) for r in (own_ref, s1_ref, s2_ref, s3_ref)]
        acc = None
        for j in range(nchip):
            d = me ^ j
            term = jnp.where(d == 0, by_dist[0], jnp.where(d == 1, by_dist[1], jnp.where(d == 2, by_dist[2], by_dist[3])))
            acc = term if acc is None else acc + term
        o_ref[...] = acc

    def other(d):
        return pl.BlockSpec((None, tr, C), lambda i, w: (w[0] ^ d, i, 0))

    return pl.pallas_call(
        body, name=name,
        grid_spec=pltpu.PrefetchScalarGridSpec(
            num_scalar_prefetch=1, grid=(nt,),
            in_specs=[other(0), other(1), other(2), other(3)],
            out_specs=pl.BlockSpec((tr, C), lambda i, w: (w[1] * nt + i, 0))),
        out_shape=jax.ShapeDtypeStruct((2 * rh, C), f32), compiler_params=_params("parallel"),
    )(where, sums, slots, slots, slots)


def _sum_slots(name, slots):
    ns, R, C = slots.shape
    tr = _tile(R, 256)

    def body(s_ref, o_ref):
        acc = s_ref[0]
        for j in range(1, ns):
            acc = acc + s_ref[j]
        o_ref[...] = acc

    return pl.pallas_call(
        body, name=name, grid=(R // tr,), in_specs=[pl.BlockSpec((ns, tr, C), lambda i: (0, i, 0))],
        out_specs=pl.BlockSpec((tr, C), lambda i: (i, 0)), out_shape=jax.ShapeDtypeStruct((R, C), f32),
        compiler_params=_params("parallel"),
    )(slots)


def _sibling_join(fulls):
    n = len(fulls)

    def body(*refs):
        ins, outs = refs[:n], refs[n:2 * n]
        send, recv = refs[2 * n:]
        x, y, c, _ = _place()

        def copy(a, core):
            rh = ins[a].shape[0] // 2
            rows = pl.ds(core * rh, rh)
            return pltpu.make_async_remote_copy(src_ref=ins[a].at[rows, :], dst_ref=outs[a].at[rows, :], send_sem=send.at[a],
                                                recv_sem=recv.at[a], device_id=(x, y, 1 - c), device_id_type=MESH)

        for a in range(n):
            copy(a, c).start()
        for a in range(n):
            copy(a, 1 - c).wait_recv()
        for a in range(n):
            copy(a, c).wait_send()

    return pl.pallas_call(
        body, name="sibling_join", in_specs=[ANY] * n, out_specs=[ANY] * n,
        out_shape=[jax.ShapeDtypeStruct(a.shape, a.dtype) for a in fulls],
        scratch_shapes=[pltpu.SemaphoreType.DMA((n,)), pltpu.SemaphoreType.DMA((n,))],
        input_output_aliases={a: a for a in range(n)},
    )(*fulls)


def _allgather_devices(v):
    def body(v_ref, out_ref, send, recv):
        x, y, c, chips = _place()
        me, sibling = (x, y, c), (x, y, 1 - c)

        def slot(px, py, pc):
            return out_ref.at[4 * px + 2 * py + pc]

        def copy(k, block, to, src=None):
            return pltpu.make_async_remote_copy(src_ref=slot(*block) if src is None else src, dst_ref=slot(*block),
                                                send_sem=send.at[k], recv_sem=recv.at[k], device_id=to, device_id_type=MESH)

        first = [copy(0, me, sibling, src=v_ref)] + [copy(1 + j, me, (*chip, c), src=v_ref) for j, chip in enumerate(chips)]
        for cp in first:
            cp.start()
        passed = [copy(4 + j, (*chip, c), sibling) for j, chip in enumerate(chips)]
        for j, chip in enumerate(chips):
            copy(1 + j, (*chip, c), me).wait_recv()
            passed[j].start()
        copy(0, sibling, me).wait_recv()
        for j, chip in enumerate(chips):
            copy(4 + j, (*chip, 1 - c), me).wait_recv()
        for cp in first + passed:
            cp.wait_send()

    return pl.pallas_call(
        body, name="allgather_devices", in_specs=[ANY], out_specs=ANY,
        out_shape=jax.ShapeDtypeStruct((N_DEV,) + v.shape, v.dtype),
        scratch_shapes=[pltpu.SemaphoreType.DMA((N_DEV - 1,)), pltpu.SemaphoreType.DMA((N_DEV - 1,))],
    )(v)


def _adamw(name, w, g, m, v):
    R, C = w.shape
    tr = _tile(R, 256)
    c1 = 1.0 / (1.0 - ADAM_B1 ** ADAM_STEP)
    c2 = 1.0 / (1.0 - ADAM_B2 ** ADAM_STEP)

    def body(w_ref, g_ref, m_ref, v_ref, d_ref, nm_ref, nv_ref):
        gv = g_ref[...]
        nm = ADAM_B1 * m_ref[...] + (1.0 - ADAM_B1) * gv
        nv = ADAM_B2 * v_ref[...] + (1.0 - ADAM_B2) * gv * gv
        d_ref[...] = -ADAM_LR * ((nm * c1) / (jnp.sqrt(nv * c2) + ADAM_EPS) + ADAM_WD * w_ref[...])
        nm_ref[...] = nm
        nv_ref[...] = nv

    row = pl.BlockSpec((tr, C), lambda i: (i, 0))
    sh = jax.ShapeDtypeStruct((R, C), f32)
    return pl.pallas_call(body, name=name, grid=(R // tr,), in_specs=[row] * 4, out_specs=[row] * 3,
                          out_shape=[sh, sh, sh], compiler_params=_params("parallel"))(w, g, m, v)


def _zoh(lr, li, log_dt, b_re, b_im):
    dt = jnp.exp(log_dt)[:, None]
    mag = jnp.exp(lr * dt)
    ab_re = mag * jnp.cos(li * dt)
    ab_im = mag * jnp.sin(li * dt)
    den = lr * lr + li * li
    nr = ab_re - 1.0
    coef_re = (nr * lr + ab_im * li) / den
    coef_im = (ab_im * lr - nr * li) / den
    bb_re = coef_re[..., None] * b_re - coef_im[..., None] * b_im
    bb_im = coef_re[..., None] * b_im + coef_im[..., None] * b_re
    return ab_re, ab_im, bb_re, bb_im


def _s5_tables(ab_re, ab_im, bb_re, bb_im, c_re, c_im, seg):
    eye = jnp.eye(SLAB_GROUPS, dtype=f32)

    def blk_in(bb):
        return jnp.einsum("sgph,gk->sghkp", bb.reshape(N_SLAB, SLAB_GROUPS, SSM_STATE, SSM_GROUP), eye).reshape(
            N_SLAB, SLAB_CH, SLAB_NS)

    def blk_out(cc):
        return jnp.einsum("sghp,gk->skpgh", cc.reshape(N_SLAB, SLAB_GROUPS, SSM_GROUP, SSM_STATE), eye).reshape(
            N_SLAB, SLAB_NS, SLAB_CH)

    bs = jnp.concatenate([blk_in(bb_re), blk_in(bb_im)], axis=2).astype(bf16)
    cs = jnp.concatenate([blk_out(c_re), blk_out(-c_im)], axis=1).astype(bf16)
    n = SSM_GROUPS * SSM_STATE
    pw = _power_table(jnp.stack([ab_re.reshape(1, n), ab_im.reshape(1, n)]), -(-seg // SUBLANES))
    return bs, cs, pw


def _power_table(ab, tiles):
    n = ab.shape[2]

    def body(a_ref, o_ref):
        row = lax.broadcasted_iota(jnp.int32, (SUBLANES, n), 0)
        ar, ai = a_ref[0], a_ref[1]
        tr, ti = jnp.broadcast_to(ar, (SUBLANES, n)), jnp.broadcast_to(ai, (SUBLANES, n))
        pr, pi = ar, ai
        for r in range(1, SUBLANES):
            pr, pi = pr * ar - pi * ai, pr * ai + pi * ar
            tr = jnp.where(row == r, pr, tr)
            ti = jnp.where(row == r, pi, ti)
        o_ref[0, 0:SUBLANES, :] = tr
        o_ref[1, 0:SUBLANES, :] = ti

        def step(j, carry):
            cr, ci = carry
            cr, ci = cr * pr - ci * pi, cr * pi + ci * pr
            o_ref[0, _rows8(j), :] = cr
            o_ref[1, _rows8(j), :] = ci
            return cr, ci

        lax.fori_loop(1, tiles, step, (tr, ti))

    return pl.pallas_call(body, name="power_table", out_shape=jax.ShapeDtypeStruct((2, SUBLANES * tiles, n), f32))(ab)


def _s5_table_grads(dbs, dcs, da):
    eye = jnp.eye(SLAB_GROUPS, dtype=f32)
    d6 = dbs.reshape(N_SLAB, SLAB_GROUPS, SSM_GROUP, 2, SLAB_GROUPS, SSM_STATE)
    dbb = jnp.einsum("sghrkp,gk->rsgph", d6, eye).reshape(2, SSM_GROUPS, SSM_STATE, SSM_GROUP)
    c6 = dcs.reshape(N_SLAB, 2, SLAB_GROUPS, SSM_STATE, SLAB_GROUPS, SSM_GROUP)
    dcc = jnp.einsum("srkpgh,gk->rsghp", c6, eye).reshape(2, SSM_GROUPS, SSM_GROUP, SSM_STATE)
    dab = da.transpose(1, 0, 2).reshape(2, SSM_GROUPS, SSM_STATE)
    return dab[0], dab[1], dbb[0], dbb[1], dcc[0], -dcc[1]


SMALL = ["mix_norm_g", "ssm_lambda_re", "ssm_lambda_im", "ssm_log_dt", "ssm_b_re", "ssm_b_im", "ssm_c_re", "ssm_c_im",
         "ssm_d", "hgrn_lb_logits", "hgrn_norm_g", "ffn_norm_g", "conv_b", "final_norm_g"]
SHARDED_SMALL = ["meta_tokens", "conv_w"]
BIG = ["w_in", "ssm_w_glu", "w_ssm_proj", "w_hgrn_proj", "w_out", "w_up", "w_down"]
WEIGHTS = ['meta_tokens', 'mix_norm_g', 'w_in', 'ssm_lambda_re', 'ssm_lambda_im', 'ssm_log_dt', 'ssm_b_re', 'ssm_b_im',
           'ssm_c_re', 'ssm_c_im', 'ssm_d', 'ssm_w_glu', 'w_ssm_proj', 'hgrn_lb_logits', 'hgrn_norm_g', 'w_hgrn_proj',
           'w_out', 'ffn_norm_g', 'w_up', 'conv_w', 'conv_b', 'w_down', 'final_norm_g']


LATER = [k for k in BIG if k != "w_in"]


def _full_weights(gathered, shards, chip):
    Dm = D_MODEL
    g = {k: lax.dynamic_update_slice(gathered[k], shards[k][None], (chip, 0, 0)) for k in gathered}
    full = {}
    for k, v in g.items():
        if k == "w_in":
            full[k] = jnp.roll(v.transpose(1, 0, 2).reshape(Dm, IN_COLS), -Dm, axis=1)
        elif k == "w_up":
            full[k] = v.transpose(1, 0, 2).reshape(Dm, 2 * D_FF)
        else:
            full[k] = v.reshape(-1, Dm)
    return full


def _local_grads(x, tgt, meta, w, full, shards, chip, core):
    B, S, Dm = x.shape
    L = S + N_META
    T = B * L
    h0 = jnp.concatenate([jnp.broadcast_to(meta[None], (B, N_META, Dm)), x], axis=1).reshape(T, Dm)

    lb_all = jax.nn.softmax(w["hgrn_lb_logits"], axis=0)
    lb = lb_all[0:1]
    zoh_out, zoh_vjp = jax.vjp(_zoh, w["ssm_lambda_re"][0], w["ssm_lambda_im"][0], w["ssm_log_dt"][0],
                               w["ssm_b_re"][0], w["ssm_b_im"][0])
    bs, cs, pw = _s5_tables(*zoh_out, w["ssm_c_re"][0], w["ssm_c_im"][0], L // SUBLANES)

    z1 = _rmsnorm_fwd("mix_norm", h0, w["mix_norm_g"])
    p, gathered = _in_proj_gather(z1, full["w_in"], [shards[k] for k in LATER])
    full = {**full, **_full_weights(dict(zip(LATER, gathered)), shards, chip)}
    ya0 = _s5_fwd(p, bs, cs, pw, w["ssm_d"], B, L)
    gl, ya = _glu_proj_fwd(ya0, full["ssm_w_glu"])
    yb = _hgrn_fwd(p, lb, w["hgrn_norm_g"], B, L)
    pa, pb, merged = _proj_merge_fwd(ya, yb, full["w_ssm_proj"], full["w_hgrn_proj"], p)
    h1, z2 = _out_proj_norm(merged, full["w_out"], h0, w["ffn_norm_g"])
    up = _mm_rows("up_proj", z2, full["w_up"], "nn", f32, D_FF // 2)
    ff = _conv_fwd(up, full["conv_w"], w["conv_b"], B, L)
    h2 = _mm_rows("down_proj", ff, full["w_down"], "nn", f32, 1024, res=h1, tk=D_FF // 2)

    h2x = h2.reshape(B, L, Dm)[:, N_META:].reshape(B * S, Dm)
    dh2x, loss, d_final_g = _final_loss(h2x, tgt.reshape(B * S, Dm), w["final_norm_g"].reshape(1, Dm))
    dh2 = jnp.pad(dh2x.reshape(B, S, Dm), ((0, 0), (N_META, 0), (0, 0))).reshape(T, Dm)

    dff = _mm_rows("d_ff", dh2, full["w_down"], "nt", f32, D_FF // 2)
    g_w_down = _mm_wgrad("dw_down", ff, dh2, tn=512)
    dup, dconv = _conv_bwd(up, dff, full["conv_w"], w["conv_b"], B, L)
    g_w_up = _dw_up(z2, dup)
    dh1, d_ffn_g = _dz2_norm(dup, full["w_up"], h1, w["ffn_norm_g"], dh2)

    g_w_out = _mm_wgrad("dw_out", merged, dh1)
    dpa, dpb, dp = _merge_bwd_fused(dh1, full["w_out"], p, pa, pb)
    dgl, dya0_direct = _glu_bwd_fused(dpa, full["w_ssm_proj"], ya0, gl)
    g_w_ssm_proj = _mm_wgrad("dw_ssm_proj", ya, dpa)
    dyb = _mm_rows("d_yb", dpb, full["w_hgrn_proj"], "nt", f32, 1024)
    g_w_hgrn_proj = _mm_wgrad("dw_hgrn_proj", yb, dpb)
    dp, d_lb, d_hgrn_g = _hgrn_bwd(p, dyb, dp, lb, w["hgrn_norm_g"], B, L)
    dya0 = _mm_rows("d_ya0", dgl, full["ssm_w_glu"], "nt", f32, 1024, res=dya0_direct)
    g_w_glu = _mm_wgrad("dw_glu", ya0, dgl)
    parts = {
        "ssm_w_glu": g_w_glu.reshape(N_CHIPS, Dm // N_CHIPS, Dm), "w_ssm_proj": g_w_ssm_proj.reshape(N_CHIPS, Dm // N_CHIPS, Dm),
        "w_hgrn_proj": g_w_hgrn_proj.reshape(N_CHIPS, Dm // N_CHIPS, Dm), "w_out": g_w_out.reshape(N_CHIPS, Dm // N_CHIPS, Dm),
        "w_up": g_w_up, "w_down": g_w_down.reshape(N_CHIPS, D_FF // N_CHIPS, Dm),
    }
    got = _sibling_halves([parts[k] for k in LATER])
    sums = {k: _add_own_half("add_half_" + k, parts[k], gt, core) for k, gt in zip(LATER, got)}
    (dp, dbs, dcs, da, d_skip), slots_later = _s5_bwd(p, dya0, dp, bs, cs, pw, w["ssm_d"], B, L, [sums[k] for k in LATER])
    slots = dict(zip(LATER, slots_later))
    g_w_in = _dw_in(z1, dp)
    dh0, d_mix_g = _dz1_norm(dp, full["w_in"], h0, w["mix_norm_g"], dh1)

    dh0 = dh0.reshape(B, L, Dm)
    grad_x = dh0[:, N_META:]
    d_meta = _meta_grad(dh0[:, :N_META])

    d_ab_re, d_ab_im, d_bb_re, d_bb_im, d_c_re, d_c_im = _s5_table_grads(dbs, dcs, da)
    d_lr, d_li, d_log_dt, d_b_re, d_b_im = zoh_vjp((d_ab_re, d_ab_im, d_bb_re, d_bb_im))
    sm0, sm1 = lb_all[0:1], lb_all[1:2]
    d_logits = jnp.concatenate([sm0 * (1.0 - sm0) * d_lb, -sm0 * sm1 * d_lb], axis=0)
    small = {
        "meta_tokens": d_meta, "mix_norm_g": d_mix_g, "ssm_lambda_re": d_lr[None], "ssm_lambda_im": d_li[None],
        "ssm_log_dt": d_log_dt[None], "ssm_b_re": d_b_re[None], "ssm_b_im": d_b_im[None], "ssm_c_re": d_c_re[None],
        "ssm_c_im": d_c_im[None], "ssm_d": d_skip, "hgrn_lb_logits": d_logits, "hgrn_norm_g": d_hgrn_g,
        "ffn_norm_g": d_ffn_g, "conv_w": dconv[:, 0:3, :].transpose(1, 0, 2).reshape(3, 2 * D_FF),
        "conv_b": dconv[:, 3, :].reshape(1, 2 * D_FF), "final_norm_g": d_final_g.reshape(Dm),
    }
    sums["w_in"] = _add_own_half_w_in(g_w_in, _sibling_halves([g_w_in], "sibling_halves_w_in")[0], core)
    slots["w_in"] = _chip_exchange([sums["w_in"]])[0]
    return loss, grad_x, sums, slots, small


PACK_ROWS = 256


def _pack(parts):
    flat = jnp.concatenate([parts[k].reshape(-1) for k in parts])
    n = flat.shape[0]
    rows = -(-n // (PACK_ROWS * LANES)) * PACK_ROWS
    flat = jnp.pad(flat, (0, rows * LANES - n))
    return flat.reshape(rows, LANES)


def _unpack(packed, like):
    flat = packed.reshape(-1)
    out, o = {}, 0
    for k, ref in like.items():
        n = math.prod(ref.shape)
        out[k] = flat[o:o + n].reshape(ref.shape)
        o += n
    return out


def kernel(x, meta_tokens, mix_norm_g, w_in, ssm_lambda_re, ssm_lambda_im, ssm_log_dt, ssm_b_re, ssm_b_im, ssm_c_re, ssm_c_im, ssm_d, ssm_w_glu, w_ssm_proj, hgrn_lb_logits, hgrn_norm_g, w_hgrn_proj, w_out, ffn_norm_g, w_up, conv_w, conv_b, w_down, final_norm_g, loss_target, m_meta_tokens, m_mix_norm_g, m_w_in, m_ssm_lambda_re, m_ssm_lambda_im, m_ssm_log_dt, m_ssm_b_re, m_ssm_b_im, m_ssm_c_re, m_ssm_c_im, m_ssm_d, m_ssm_w_glu, m_w_ssm_proj, m_hgrn_lb_logits, m_hgrn_norm_g, m_w_hgrn_proj, m_w_out, m_ffn_norm_g, m_w_up, m_conv_w, m_conv_b, m_w_down, m_final_norm_g, v_meta_tokens, v_mix_norm_g, v_w_in, v_ssm_lambda_re, v_ssm_lambda_im, v_ssm_log_dt, v_ssm_b_re, v_ssm_b_im, v_ssm_c_re, v_ssm_c_im, v_ssm_d, v_ssm_w_glu, v_w_ssm_proj, v_hgrn_lb_logits, v_hgrn_norm_g, v_w_hgrn_proj, v_w_out, v_ffn_norm_g, v_w_up, v_conv_w, v_conv_b, v_w_down, v_final_norm_g):
    args = dict(locals())
    w = {k: args[k] for k in WEIGHTS}
    mom = {k: args["m_" + k] for k in WEIGHTS}
    var = {k: args["v_" + k] for k in WEIGHTS}
    Dm = D_MODEL
    cx, cy, cc = lax.axis_index("x"), lax.axis_index("y"), lax.axis_index("c")
    chip = 2 * cx + cy

    shards = {k: w[k][0].astype(bf16) for k in BIG}
    g_meta, g_cw = _allgather_chips([w["meta_tokens"], w["conv_w"][0]])
    full = _full_weights({"w_in": _allgather_split([shards["w_in"]])[0]}, shards, chip)
    full["conv_w"] = g_cw.transpose(1, 0, 2).reshape(3, 2 * D_FF)
    meta_full = g_meta.transpose(1, 0, 2).reshape(N_META, Dm)

    core = cc.reshape(1).astype(jnp.int32)
    loss_part, grad_x, sums, slots, small = _local_grads(x, loss_target, meta_full, w, full, shards, chip, core)

    where = jnp.stack([chip, cc]).astype(jnp.int32)
    fulls = [_sum_chips("sum_chips_" + k, slots[k], sums[k], where) for k in BIG]
    g_big = dict(zip(BIG, _sibling_join(fulls)))

    small_all = dict(small)
    small_all["loss"] = loss_part[0, 0:1]
    packed = _pack(small_all)
    slots_dev = lax.dynamic_update_slice(_allgather_devices(packed), packed[None], (2 * chip + cc, 0, 0))
    reduced = _unpack(_sum_slots("sum_devices", slots_dev), small_all)
    loss = reduced.pop("loss")[0]
    mcols = Dm // N_CHIPS
    ccols = 2 * D_FF // N_CHIPS
    grads = {k: reduced[k] for k in SMALL}
    grads["meta_tokens"] = lax.dynamic_slice(reduced["meta_tokens"], (0, chip * mcols), (N_META, mcols))
    grads["conv_w"] = lax.dynamic_slice(reduced["conv_w"], (0, chip * ccols), (3, ccols))[None]
    for k in BIG:
        grads[k] = g_big[k][None]

    delta, new_m, new_v = {}, {}, {}
    for k in BIG:
        shp = w[k].shape
        d, nm, nv = _adamw("adamw_" + k, w[k][0], grads[k][0], mom[k][0], var[k][0])
        delta[k], new_m[k], new_v[k] = d.reshape(shp), nm.reshape(shp), nv.reshape(shp)
    rest = SMALL + SHARDED_SMALL
    pk = [_pack({k: t[k] for k in rest}) for t in (w, grads, mom, var)]
    outs = _adamw("adamw_small", *pk)
    like = {k: w[k] for k in rest}
    for dst, o in zip((delta, new_m, new_v), outs):
        dst.update(_unpack(o, like))

    return (loss, grad_x, *[grads[k].reshape(w[k].shape) for k in WEIGHTS], *[delta[k] for k in WEIGHTS],
            *[new_m[k] for k in WEIGHTS], *[new_v[k] for k in WEIGHTS])
```

```python
import functools
import math

import jax
import jax.numpy as jnp
from jax import lax
from jax.experimental import pallas as pl
from jax.experimental.pallas import tpu as pltpu

f32 = jnp.float32
bf16 = jnp.bfloat16

D_MODEL = 1024
N_META = 16
SSM_GROUP = 16
SSM_GROUPS = 64
SSM_STATE = 64
SLAB_GROUPS = 8
N_SLAB = SSM_GROUPS // SLAB_GROUPS
SLAB_CH = SLAB_GROUPS * SSM_GROUP
SLAB_NS = SLAB_GROUPS * SSM_STATE
HEADS = 8
HEAD_DIM = 128
CHUNK = 16
D_FF = 2816
IN_COLS = 7168
EPS = 1e-6
SUBLANES = 8
LANES = 128
N_CHIPS = 4
N_DEV = 8
ADAM_LR, ADAM_B1, ADAM_B2, ADAM_EPS, ADAM_WD, ADAM_STEP = 0.001, 0.9, 0.999, 1e-08, 0.01, 10
MESH = pl.DeviceIdType.MESH
ANY = pl.BlockSpec(memory_space=pl.ANY)

SEG_Q, SEG_F, SEG_I, SEG_OG, SEG_GA, SEG_GB, SEG_U = range(7)
N_SEG = 7


def _tile(n, target, mult=SUBLANES):
    best = None
    for d in range(mult, min(n, target) + 1, mult):
        if n % d == 0:
            best = d
    return n if best is None else best


def _params(*sem):
    return pltpu.CompilerParams(dimension_semantics=sem)


def _sigmoid(x):
    return 1.0 / (1.0 + jnp.exp(-x))


_DIMS = {"nn": (((1,), (0,)), ((), ())), "nt": (((1,), (1,)), ((), ())), "tn": (((0,), (0,)), ((), ()))}


def _mm(name, a, b, dims, grid, a_spec, b_spec, out_shape, out_spec, acc_shape, res=None, res_spec=None):
    nk = grid[2]
    dn = _DIMS[dims]

    def body(*refs):
        if res is None:
            a_ref, b_ref, o_ref, acc = refs
        else:
            a_ref, b_ref, r_ref, o_ref, acc = refs
        k = pl.program_id(2)

        @pl.when(k == 0)
        def _():
            acc[...] = jnp.zeros_like(acc)

        acc[...] += lax.dot_general(a_ref[...].astype(bf16), b_ref[...].astype(bf16), dn, preferred_element_type=f32)

        @pl.when(k == nk - 1)
        def _():
            r = acc[...]
            if res is not None:
                r = r + r_ref[...]
            o_ref[...] = r.astype(o_ref.dtype)

    ins = [a, b] + ([] if res is None else [res])
    specs = [a_spec, b_spec] + ([] if res is None else [res_spec])
    return pl.pallas_call(
        body, name=name, grid=grid, in_specs=specs, out_specs=out_spec, out_shape=out_shape,
        scratch_shapes=[pltpu.VMEM(acc_shape, f32)],
        compiler_params=_params("parallel", "parallel", "arbitrary"),
    )(*ins)


def _mm_rows(name, a, w, dims, out_dtype, tn, res=None, tk=None):
    T, K = a.shape
    N = w.shape[1] if dims == "nn" else w.shape[0]
    tm = _tile(T, 1032)
    tk = K if tk is None else tk
    grid = (T // tm, N // tn, K // tk)
    a_spec = pl.BlockSpec((tm, tk), lambda i, j, k: (i, k))
    if dims == "nn":
        b_spec = pl.BlockSpec((tk, tn), lambda i, j, k: (k, j))
    else:
        b_spec = pl.BlockSpec((tn, tk), lambda i, j, k: (j, k))
    o_spec = pl.BlockSpec((tm, tn), lambda i, j, k: (i, j))
    return _mm(name, a, w, dims, grid, a_spec, b_spec, jax.ShapeDtypeStruct((T, N), out_dtype), o_spec, (tm, tn),
               res=res, res_spec=None if res is None else o_spec)


def _mm_fused(name, pairs, dims, extras, epilogue, outs, rows=()):
    T, K = pairs[0][0].shape
    N = pairs[0][1].shape[1] if dims == "nn" else pairs[0][1].shape[0]
    tm = _tile(T, 344)
    tn = N
    grid = (T // tm, N // tn)
    npair, nex = len(pairs), len(extras) + len(rows)
    dn = _DIMS[dims]

    def body(*refs):
        ab = refs[:2 * npair]
        ex = refs[2 * npair:2 * npair + nex]
        o_refs = refs[2 * npair + nex:]
        accs = [lax.dot_general(ab[2 * q][...].astype(bf16), ab[2 * q + 1][...].astype(bf16), dn, preferred_element_type=f32)
                for q in range(npair)]
        vals = epilogue(accs, [e[...] for e in ex])
        for o_ref, v in zip(o_refs, vals):
            if isinstance(v, (list, tuple)):
                for s_, vs in enumerate(v):
                    o_ref[s_] = vs.astype(o_ref.dtype)
            else:
                o_ref[...] = v.astype(o_ref.dtype)

    ins, specs = [], []
    for a, w in pairs:
        ins += [a, w]
        specs.append(pl.BlockSpec((tm, K), lambda i, j: (i, 0)))
        specs.append(pl.BlockSpec((K, tn), lambda i, j: (0, j)) if dims == "nn" else pl.BlockSpec((tn, K), lambda i, j: (j, 0)))
    for arr, off in extras:
        ins.append(arr)
        specs.append(pl.BlockSpec((tm, tn), lambda i, j, off=off: (i, off + j)))
    for arr in rows:
        ins.append(arr)
        specs.append(pl.BlockSpec((1, tn), lambda i, j: (0, j)))
    shapes, ospecs = [], []
    for o in outs:
        if isinstance(o, tuple):
            dt, nseg, total, blk = o
            shapes.append(jax.ShapeDtypeStruct((total, T, N), dt))
            ospecs.append(pl.BlockSpec((nseg, tm, tn), lambda i, j, blk=blk: (blk, i, j)))
        else:
            shapes.append(jax.ShapeDtypeStruct((T, N), o))
            ospecs.append(pl.BlockSpec((tm, tn), lambda i, j: (i, j)))
    return pl.pallas_call(body, name=name, grid=grid, in_specs=specs, out_specs=ospecs, out_shape=shapes,
                          compiler_params=_params("parallel", "parallel"))(*ins)


def _glu_proj_fwd(ya0, w_glu):
    def epi(accs, tiles):
        return accs[0], tiles[0] * _sigmoid(accs[0])

    return _mm_fused("glu_proj", [(ya0, w_glu)], "nn", [(ya0, 0)], epi, [f32, bf16])


def _proj_merge_fwd(ya, yb, w_sp, w_hp, p):
    def epi(accs, tiles):
        return accs[0], accs[1], _sigmoid(tiles[0]) * accs[0] + _sigmoid(tiles[1]) * accs[1]

    return _mm_fused("proj_merge", [(ya, w_sp), (yb, w_hp)], "nn", [(p, SEG_GA), (p, SEG_GB)], epi, [f32, f32, bf16])


def _merge_bwd_fused(dh1, w_out, p, pa, pb):
    def epi(accs, tiles):
        d = accs[0]
        sa, sb = _sigmoid(tiles[0]), _sigmoid(tiles[1])
        return d * sa, d * sb, [d * tiles[2] * sa * (1.0 - sa), d * tiles[3] * sb * (1.0 - sb)]

    return _mm_fused("d_merged", [(dh1, w_out)], "nt", [(p, SEG_GA), (p, SEG_GB), (pa, 0), (pb, 0)], epi,
                     [bf16, bf16, (bf16, 2, N_SEG, SEG_GA // 2)])


def _out_proj_norm(merged, w_out, h0, g):
    def epi(accs, tiles):
        h1 = tiles[0] + accs[0]
        r = lax.rsqrt(jnp.mean(h1 * h1, axis=-1, keepdims=True) + EPS)
        return h1, h1 * r * tiles[1]

    return _mm_fused("out_proj", [(merged, w_out)], "nn", [(h0, 0)], epi, [f32, bf16], rows=[g])


def _mm_rmsnorm_bwd(name, a, b, grid, a_spec, b_spec, x, g, dres):
    T, Dm = x.shape
    tm = T // grid[0]
    nk = grid[2]

    def body(a_ref, b_ref, x_ref, g_ref, dres_ref, dx_ref, dg_ref, acc):
        i, k = pl.program_id(0), pl.program_id(2)

        @pl.when(k == 0)
        def _():
            acc[...] = jnp.zeros_like(acc)

        @pl.when((i == 0) & (k == 0))
        def _():
            dg_ref[...] = jnp.zeros_like(dg_ref)

        acc[...] += lax.dot_general(a_ref[...].astype(bf16), b_ref[...].astype(bf16), _DIMS["nt"], preferred_element_type=f32)

        @pl.when(k == nk - 1)
        def _():
            xv = x_ref[...]
            r = lax.rsqrt(jnp.mean(xv * xv, axis=-1, keepdims=True) + EPS)
            xn = xv * r
            dzv = acc[...]
            dzg = dzv * g_ref[...]
            dx_ref[...] = dres_ref[...] + r * (dzg - xn * jnp.mean(dzg * xn, axis=-1, keepdims=True))
            dg_ref[...] += jnp.sum(dzv * xn, axis=0, keepdims=True)

    row = pl.BlockSpec((tm, Dm), lambda i, j, k: (i, 0))
    par = pl.BlockSpec((1, Dm), lambda i, j, k: (0, 0))
    return pl.pallas_call(
        body, name=name, grid=grid, in_specs=[a_spec, b_spec, row, par, row], out_specs=[row, par],
        out_shape=[jax.ShapeDtypeStruct((T, Dm), f32), jax.ShapeDtypeStruct((1, Dm), f32)],
        scratch_shapes=[pltpu.VMEM((tm, Dm), f32)],
        compiler_params=_params("arbitrary", "arbitrary", "arbitrary"),
    )(a, b, x, g, dres)


def _glu_bwd_fused(dpa, w_sp, ya0, gl):
    def epi(accs, tiles):
        d = accs[0]
        s = _sigmoid(tiles[1])
        return d * tiles[0] * s * (1.0 - s), d * s

    return _mm_fused("d_ya", [(dpa, w_sp)], "nt", [(ya0, 0), (gl, 0)], epi, [bf16, f32])


def _mm_wgrad(name, a, g, tn=None):
    T, K = a.shape
    N = g.shape[1]
    tk = _tile(T, 688)
    tn = N if tn is None else tn
    grid = (1, N // tn, T // tk)
    a_spec = pl.BlockSpec((tk, K), lambda i, j, k: (k, 0))
    g_spec = pl.BlockSpec((tk, tn), lambda i, j, k: (k, j))
    o_spec = pl.BlockSpec((K, tn), lambda i, j, k: (0, j))
    return _mm(name, a, g, "tn", grid, a_spec, g_spec, jax.ShapeDtypeStruct((K, N), f32), o_spec, (K, tn))


def _rmsnorm_fwd(name, x, g):
    T, Dm = x.shape
    tr = _tile(T, 688)

    def body(x_ref, g_ref, z_ref):
        xv = x_ref[...]
        r = lax.rsqrt(jnp.mean(xv * xv, axis=-1, keepdims=True) + EPS)
        z_ref[...] = (xv * r * g_ref[...]).astype(z_ref.dtype)

    return pl.pallas_call(
        body, name=name, grid=(T // tr,),
        in_specs=[pl.BlockSpec((tr, Dm), lambda i: (i, 0)), pl.BlockSpec((1, Dm), lambda i: (0, 0))],
        out_specs=pl.BlockSpec((tr, Dm), lambda i: (i, 0)),
        out_shape=jax.ShapeDtypeStruct((T, Dm), bf16), compiler_params=_params("parallel"),
    )(x, g)


def _rmsnorm_bwd(name, x, g, dz, dres):
    T, Dm = x.shape
    tr = _tile(T, 688)

    def body(x_ref, g_ref, dz_ref, dres_ref, dx_ref, dg_ref):
        xv = x_ref[...]
        r = lax.rsqrt(jnp.mean(xv * xv, axis=-1, keepdims=True) + EPS)
        xn = xv * r
        dzv = dz_ref[...]
        dzg = dzv * g_ref[...]
        dx_ref[...] = dres_ref[...] + r * (dzg - xn * jnp.mean(dzg * xn, axis=-1, keepdims=True))

        @pl.when(pl.program_id(0) == 0)
        def _():
            dg_ref[...] = jnp.zeros_like(dg_ref)

        dg_ref[...] += jnp.sum(dzv * xn, axis=0, keepdims=True)

    row = pl.BlockSpec((tr, Dm), lambda i: (i, 0))
    par = pl.BlockSpec((1, Dm), lambda i: (0, 0))
    return pl.pallas_call(
        body, name=name, grid=(T // tr,), in_specs=[row, par, row, row], out_specs=[row, par],
        out_shape=[jax.ShapeDtypeStruct((T, Dm), f32), jax.ShapeDtypeStruct((1, Dm), f32)],
        compiler_params=_params("arbitrary"),
    )(x, g, dz, dres)


def _glu_fwd(ya0, gl):
    T, Dm = ya0.shape
    tr = _tile(T, 688)

    def body(y_ref, g_ref, o_ref):
        o_ref[...] = (y_ref[...] * _sigmoid(g_ref[...])).astype(o_ref.dtype)

    row = pl.BlockSpec((tr, Dm), lambda i: (i, 0))
    return pl.pallas_call(body, name="glu_fwd", grid=(T // tr,), in_specs=[row, row], out_specs=row,
                          out_shape=jax.ShapeDtypeStruct((T, Dm), bf16), compiler_params=_params("parallel"))(ya0, gl)


def _glu_bwd(dya, ya0, gl):
    T, Dm = ya0.shape
    tr = _tile(T, 688)

    def body(d_ref, y_ref, g_ref, dg_ref, dy_ref):
        s = _sigmoid(g_ref[...])
        d = d_ref[...]
        dg_ref[...] = (d * y_ref[...] * s * (1.0 - s)).astype(dg_ref.dtype)
        dy_ref[...] = d * s

    row = pl.BlockSpec((tr, Dm), lambda i: (i, 0))
    return pl.pallas_call(body, name="glu_bwd", grid=(T // tr,), in_specs=[row, row, row], out_specs=[row, row],
                          out_shape=[jax.ShapeDtypeStruct((T, Dm), bf16), jax.ShapeDtypeStruct((T, Dm), f32)],
                          compiler_params=_params("parallel"))(dya, ya0, gl)


def _merge_fwd(p, pa, pb):
    T, Dm = pa.shape
    tr = _tile(T, 688)

    def body(ga_ref, gb_ref, pa_ref, pb_ref, o_ref):
        o_ref[...] = (_sigmoid(ga_ref[...]) * pa_ref[...] + _sigmoid(gb_ref[...]) * pb_ref[...]).astype(o_ref.dtype)

    row = pl.BlockSpec((tr, Dm), lambda i: (i, 0))
    return pl.pallas_call(
        body, name="merge_fwd", grid=(T // tr,),
        in_specs=[pl.BlockSpec((tr, Dm), lambda i: (i, SEG_GA)), pl.BlockSpec((tr, Dm), lambda i: (i, SEG_GB)), row, row],
        out_specs=row, out_shape=jax.ShapeDtypeStruct((T, Dm), bf16), compiler_params=_params("parallel"),
    )(p, p, pa, pb)


def _merge_bwd(dm, p, pa, pb):
    T, Dm = pa.shape
    tr = _tile(T, 688)

    def body(dm_ref, ga_ref, gb_ref, pa_ref, pb_ref, dpa_ref, dpb_ref, dp_ref):
        d = dm_ref[...]
        sa = _sigmoid(ga_ref[...])
        sb = _sigmoid(gb_ref[...])
        dpa_ref[...] = (d * sa).astype(dpa_ref.dtype)
        dpb_ref[...] = (d * sb).astype(dpb_ref.dtype)
        dp_ref[0] = (d * pa_ref[...] * sa * (1.0 - sa)).astype(dp_ref.dtype)
        dp_ref[1] = (d * pb_ref[...] * sb * (1.0 - sb)).astype(dp_ref.dtype)

    row = pl.BlockSpec((tr, Dm), lambda i: (i, 0))
    return pl.pallas_call(
        body, name="merge_bwd", grid=(T // tr,),
        in_specs=[row, pl.BlockSpec((tr, Dm), lambda i: (i, SEG_GA)), pl.BlockSpec((tr, Dm), lambda i: (i, SEG_GB)), row, row],
        out_specs=[row, row, pl.BlockSpec((2, tr, Dm), lambda i: (SEG_GA // 2, i, 0))],
        out_shape=[jax.ShapeDtypeStruct((T, Dm), bf16), jax.ShapeDtypeStruct((T, Dm), bf16),
                   jax.ShapeDtypeStruct((N_SEG, T, Dm), bf16)],
        compiler_params=_params("parallel"),
    )(dm, p, p, pa, pb)


def _final_loss(h2x, tgt, g):
    T, Dm = h2x.shape
    tr = _tile(T, 512)

    def body(h_ref, t_ref, g_ref, dh_ref, loss_ref, dg_ref):
        hv = h_ref[...]
        r = lax.rsqrt(jnp.mean(hv * hv, axis=-1, keepdims=True) + EPS)
        xn = hv * r
        gv = g_ref[...]
        err = xn * gv - t_ref[...]
        dy = err * (1.0 / Dm)
        dyg = dy * gv
        dh_ref[...] = r * (dyg - xn * jnp.mean(dyg * xn, axis=-1, keepdims=True))

        @pl.when(pl.program_id(0) == 0)
        def _():
            dg_ref[...] = jnp.zeros_like(dg_ref)
            loss_ref[...] = jnp.zeros_like(loss_ref)

        dg_ref[...] += jnp.sum(dy * xn, axis=0, keepdims=True)
        loss_ref[...] += jnp.sum(err * err) * (0.5 / Dm)

    row = pl.BlockSpec((tr, Dm), lambda i: (i, 0))
    par = pl.BlockSpec((1, Dm), lambda i: (0, 0))
    return pl.pallas_call(
        body, name="final_loss", grid=(T // tr,), in_specs=[row, row, par],
        out_specs=[row, pl.BlockSpec((1, LANES), lambda i: (0, 0)), par],
        out_shape=[jax.ShapeDtypeStruct((T, Dm), f32), jax.ShapeDtypeStruct((1, LANES), f32), jax.ShapeDtypeStruct((1, Dm), f32)],
        compiler_params=_params("arbitrary"),
    )(h2x, tgt, g)


def _meta_grad(dh0_meta):
    B = dh0_meta.shape[0]

    def body(d_ref, o_ref):
        acc = d_ref[0]
        for b in range(1, B):
            acc = acc + d_ref[b]
        o_ref[...] = acc

    return pl.pallas_call(body, name="meta_grad", out_shape=jax.ShapeDtypeStruct(dh0_meta.shape[1:], f32))(dh0_meta)


def _shift_down(x, k, row):
    return jnp.where(row >= k, pltpu.roll(x, k, 0), 0.0)


def _shift_up(x, k, row):
    n = x.shape[0]
    return jnp.where(row < n - k, pltpu.roll(x, n - k, 0), 0.0)


def _conv_fwd(up, conv_w, conv_b, B, L):
    tc = 256
    nt = D_FF // tc

    def body(xa_ref, xb_ref, wa_ref, wb_ref, ba_ref, bb_ref, o_ref):
        row = lax.broadcasted_iota(jnp.int32, (L, tc), 0)

        def conv(x_ref, w_ref, b_ref):
            x = x_ref[...]
            return (b_ref[...] + w_ref[0:1, :] * _shift_down(x, 2, row) + w_ref[1:2, :] * _shift_down(x, 1, row)
                    + w_ref[2:3, :] * x)

        a = conv(xa_ref, wa_ref, ba_ref)
        b = conv(xb_ref, wb_ref, bb_ref)
        o_ref[...] = (a * _sigmoid(a) * b).astype(o_ref.dtype)

    return pl.pallas_call(
        body, name="conv_fwd", grid=(B, nt),
        in_specs=[pl.BlockSpec((L, tc), lambda b, j: (b, j)), pl.BlockSpec((L, tc), lambda b, j: (b, j + nt)),
                  pl.BlockSpec((3, tc), lambda b, j: (0, j)), pl.BlockSpec((3, tc), lambda b, j: (0, j + nt)),
                  pl.BlockSpec((1, tc), lambda b, j: (0, j)), pl.BlockSpec((1, tc), lambda b, j: (0, j + nt))],
        out_specs=pl.BlockSpec((L, tc), lambda b, j: (b, j)),
        out_shape=jax.ShapeDtypeStruct((B * L, D_FF), bf16), compiler_params=_params("parallel", "parallel"),
    )(up, up, conv_w, conv_w, conv_b, conv_b)


def _conv_bwd(up, dff, conv_w, conv_b, B, L):
    tc = 256
    nt = D_FF // tc

    def body(xa_ref, xb_ref, d_ref, wa_ref, wb_ref, ba_ref, bb_ref, dup_ref, dw_ref):
        row = lax.broadcasted_iota(jnp.int32, (L, tc), 0)
        xs, pre = [], []
        for x_ref, w_ref, b_ref in ((xa_ref, wa_ref, ba_ref), (xb_ref, wb_ref, bb_ref)):
            x = x_ref[...]
            x1 = _shift_down(x, 1, row)
            x2 = _shift_down(x, 2, row)
            xs.append((x, x1, x2))
            pre.append(b_ref[...] + w_ref[0:1, :] * x2 + w_ref[1:2, :] * x1 + w_ref[2:3, :] * x)
        a, b = pre
        s = _sigmoid(a)
        d = d_ref[...]
        grads = (d * b * s * (1.0 + a * (1.0 - s)), d * a * s)

        @pl.when(pl.program_id(1) == 0)
        def _():
            dw_ref[...] = jnp.zeros_like(dw_ref)

        for h, (gr, (x, x1, x2), w_ref) in enumerate(zip(grads, xs, (wa_ref, wb_ref))):
            dup_ref[h] = (w_ref[2:3, :] * gr + w_ref[1:2, :] * _shift_up(gr, 1, row)
                          + w_ref[0:1, :] * _shift_up(gr, 2, row)).astype(dup_ref.dtype)
            dw_ref[h, 0:1, :] += jnp.sum(gr * x2, axis=0, keepdims=True)
            dw_ref[h, 1:2, :] += jnp.sum(gr * x1, axis=0, keepdims=True)
            dw_ref[h, 2:3, :] += jnp.sum(gr * x, axis=0, keepdims=True)
            dw_ref[h, 3:4, :] += jnp.sum(gr, axis=0, keepdims=True)

    return pl.pallas_call(
        body, name="conv_bwd", grid=(nt, B),
        in_specs=[pl.BlockSpec((L, tc), lambda j, b: (b, j)), pl.BlockSpec((L, tc), lambda j, b: (b, j + nt)),
                  pl.BlockSpec((L, tc), lambda j, b: (b, j)),
                  pl.BlockSpec((3, tc), lambda j, b: (0, j)), pl.BlockSpec((3, tc), lambda j, b: (0, j + nt)),
                  pl.BlockSpec((1, tc), lambda j, b: (0, j)), pl.BlockSpec((1, tc), lambda j, b: (0, j + nt))],
        out_specs=[pl.BlockSpec((2, L, tc), lambda j, b: (0, b, j)), pl.BlockSpec((2, SUBLANES, tc), lambda j, b: (0, 0, j))],
        out_shape=[jax.ShapeDtypeStruct((2, B * L, D_FF), bf16), jax.ShapeDtypeStruct((2, SUBLANES, D_FF), f32)],
        compiler_params=_params("parallel", "arbitrary"),
    )(up, up, dff, conv_w, conv_w, conv_b, conv_b)


CONV_ROWS = 2 * SUBLANES


def _rows16(i):
    return pl.ds(pl.multiple_of(i * CONV_ROWS, CONV_ROWS), CONV_ROWS)


def _conv_taps(x_ref, i, row):
    x = x_ref[_rows16(i), :]
    live = jnp.where(i > 0, 1.0, 0.0)
    r0 = jnp.maximum(i * CONV_ROWS, 2)
    p1 = x_ref[pl.ds(r0 - 1, 1), :] * live
    p2 = x_ref[pl.ds(r0 - 2, 1), :] * live
    x1 = jnp.where(row == 0, p1, pltpu.roll(x, 1, 0))
    x2 = jnp.where(row == 0, p2, jnp.where(row == 1, p1, pltpu.roll(x, 2, 0)))
    return x, x1, x2


def _conv_bwd(up, dff, conv_w, conv_b, B, L):
    tc = 256
    nt = D_FF // tc
    n = L // CONV_ROWS

    def body(xa_ref, xb_ref, d_ref, wa_ref, wb_ref, ba_ref, bb_ref, dup_ref, dw_ref, ga_ref, gb_ref):
        row = lax.broadcasted_iota(jnp.int32, (CONV_ROWS, tc), 0)

        @pl.when(pl.program_id(1) == 0)
        def _():
            dw_ref[...] = jnp.zeros_like(dw_ref)

        zero_tail = jnp.zeros((CONV_ROWS, tc), f32)
        ga_ref[L:L + CONV_ROWS, :] = zero_tail
        gb_ref[L:L + CONV_ROWS, :] = zero_tail

        def fold(v):
            return v[0:SUBLANES, :] + v[SUBLANES:CONV_ROWS, :]

        def step(i, acc):
            taps_a = _conv_taps(xa_ref, i, row)
            taps_b = _conv_taps(xb_ref, i, row)
            a = ba_ref[...] + wa_ref[0:1, :] * taps_a[2] + wa_ref[1:2, :] * taps_a[1] + wa_ref[2:3, :] * taps_a[0]
            b = bb_ref[...] + wb_ref[0:1, :] * taps_b[2] + wb_ref[1:2, :] * taps_b[1] + wb_ref[2:3, :] * taps_b[0]
            s = _sigmoid(a)
            d = d_ref[_rows16(i), :]
            g_a = d * b * s * (1.0 + a * (1.0 - s))
            g_b = d * a * s
            ga_ref[_rows16(i), :] = g_a
            gb_ref[_rows16(i), :] = g_b
            new = []
            for g, (x, x1, x2) in ((g_a, taps_a), (g_b, taps_b)):
                new += [fold(g * x2), fold(g * x1), fold(g * x), fold(g)]
            return tuple(o + v for o, v in zip(acc, new))

        z = jnp.zeros((SUBLANES, tc), f32)
        acc = _repeat_loop(n, step, (z,) * 8)
        for h in range(2):
            for t in range(4):
                dw_ref[h, t:t + 1, :] += jnp.sum(acc[4 * h + t], axis=0, keepdims=True)

        def back(i, c):
            for h, (g_ref, w_ref) in enumerate(((ga_ref, wa_ref), (gb_ref, wb_ref))):
                g = g_ref[_rows16(i), :]
                n1 = g_ref[pl.ds(i * CONV_ROWS + CONV_ROWS, 1), :]
                n2 = g_ref[pl.ds(i * CONV_ROWS + CONV_ROWS + 1, 1), :]
                u1 = jnp.where(row == CONV_ROWS - 1, n1, pltpu.roll(g, CONV_ROWS - 1, 0))
                u2 = jnp.where(row == CONV_ROWS - 1, n2, jnp.where(row == CONV_ROWS - 2, n1, pltpu.roll(g, CONV_ROWS - 2, 0)))
                dup_ref[h, _rows16(i), :] = (w_ref[2:3, :] * g + w_ref[1:2, :] * u1 + w_ref[0:1, :] * u2).astype(dup_ref.dtype)
            return c

        _repeat_loop(n, back, 0)

    return pl.pallas_call(
        body, name="conv_bwd", grid=(nt, B),
        in_specs=[pl.BlockSpec((L, tc), lambda j, b: (b, j)), pl.BlockSpec((L, tc), lambda j, b: (b, j + nt)),
                  pl.BlockSpec((L, tc), lambda j, b: (b, j)),
                  pl.BlockSpec((3, tc), lambda j, b: (0, j)), pl.BlockSpec((3, tc), lambda j, b: (0, j + nt)),
                  pl.BlockSpec((1, tc), lambda j, b: (0, j)), pl.BlockSpec((1, tc), lambda j, b: (0, j + nt))],
        out_specs=[pl.BlockSpec((2, L, tc), lambda j, b: (0, b, j)), pl.BlockSpec((2, SUBLANES, tc), lambda j, b: (0, 0, j))],
        out_shape=[jax.ShapeDtypeStruct((2, B * L, D_FF), bf16), jax.ShapeDtypeStruct((2, SUBLANES, D_FF), f32)],
        scratch_shapes=[pltpu.VMEM((L + CONV_ROWS, tc), f32), pltpu.VMEM((L + CONV_ROWS, tc), f32)],
        compiler_params=_params("parallel", "arbitrary"),
    )(up, up, dff, conv_w, conv_w, conv_b, conv_b)


GELU_C = math.sqrt(2.0 / math.pi)
GELU_A = 0.044715


def _gelu(x):
    return 0.5 * x * (1.0 + jnp.tanh(GELU_C * (x + GELU_A * x * x * x)))


def _gelu_grad(x):
    t = jnp.tanh(GELU_C * (x + GELU_A * x * x * x))
    return 0.5 * (1.0 + t) + 0.5 * x * (1.0 - t * t) * GELU_C * (1.0 + 3.0 * GELU_A * x * x)


def _cmul_add(xr, xi, ar, ai, sr, si):
    return xr + ar * sr - ai * si, xi + ar * si + ai * sr


def _s5_scan_fwd(s_ref, pw_ref, L):
    ns = SLAB_NS
    row = lax.broadcasted_iota(jnp.int32, (SUBLANES, ns), 0)
    pr = pw_ref[0, 0:SUBLANES, :]
    pi = pw_ref[1, 0:SUBLANES, :]

    def step(i, carry):
        cr, ci = carry
        r0 = pl.multiple_of(i * SUBLANES, SUBLANES)
        xr = s_ref[pl.ds(r0, SUBLANES), 0:ns]
        xi = s_ref[pl.ds(r0, SUBLANES), ns:2 * ns]
        for k in (1, 2, 4):
            xr, xi = _cmul_add(xr, xi, pr[k - 1:k, :], pi[k - 1:k, :], _shift_down(xr, k, row), _shift_down(xi, k, row))
        xr, xi = _cmul_add(xr, xi, pr, pi, cr, ci)
        s_ref[pl.ds(r0, SUBLANES), 0:ns] = xr
        s_ref[pl.ds(r0, SUBLANES), ns:2 * ns] = xi
        return xr[SUBLANES - 1:SUBLANES, :], xi[SUBLANES - 1:SUBLANES, :]

    z = jnp.zeros((1, ns), f32)
    lax.fori_loop(0, L // SUBLANES, step, (z, z))


def _s5_project_in(u_ref, bs_ref, s_ref, L, rc):
    for r in range(0, L, rc):
        s_ref[r:r + rc, :] = jnp.dot(u_ref[r:r + rc, :].astype(bf16), bs_ref[...], preferred_element_type=f32)


def _s5_fwd(p, bs, cs, pw, d_skip, B, L):
    rc = _tile(L, 344)

    def body(u_ref, bs_ref, cs_ref, pw_ref, d_ref, y_ref, s_ref):
        _s5_project_in(u_ref, bs_ref, s_ref, L, rc)
        _s5_scan_fwd(s_ref, pw_ref, L)
        for r in range(0, L, rc):
            ypre = (jnp.dot(s_ref[r:r + rc, :].astype(bf16), cs_ref[...], preferred_element_type=f32)
                    + d_ref[...] * u_ref[r:r + rc, :])
            y_ref[r:r + rc, :] = _gelu(ypre)

    ucol = SEG_U * (D_MODEL // SLAB_CH)
    return pl.pallas_call(
        body, name="s5_fwd", grid=(B, N_SLAB),
        in_specs=[pl.BlockSpec((L, SLAB_CH), lambda b, s: (b, ucol + s)),
                  pl.BlockSpec((None, SLAB_CH, 2 * SLAB_NS), lambda b, s: (s, 0, 0)),
                  pl.BlockSpec((None, 2 * SLAB_NS, SLAB_CH), lambda b, s: (s, 0, 0)),
                  pl.BlockSpec((None, 2, 2 * SUBLANES, SLAB_NS), lambda b, s: (s, 0, 0, 0)),
                  pl.BlockSpec((1, SLAB_CH), lambda b, s: (0, s))],
        out_specs=pl.BlockSpec((L, SLAB_CH), lambda b, s: (b, s)),
        out_shape=jax.ShapeDtypeStruct((B * L, D_MODEL), f32),
        scratch_shapes=[pltpu.VMEM((L, 2 * SLAB_NS), f32)],
        compiler_params=_params("parallel", "parallel"),
    )(p, bs, cs, pw, d_skip)


def _s5_bwd(p, dya0, dp, bs, cs, pw, d_skip, B, L):
    rc = _tile(L, 344)
    ns = SLAB_NS
    nt = L // SUBLANES

    def body(u_ref, dy_ref, dp_in, bs_ref, cs_ref, pw_ref, d_ref, du_ref, dbs_ref, dcs_ref, da_ref, dd_ref,
             s_ref, lam_ref, dyp_ref):
        del dp_in
        b = pl.program_id(1)

        @pl.when(b == 0)
        def _():
            dbs_ref[...] = jnp.zeros_like(dbs_ref)
            dcs_ref[...] = jnp.zeros_like(dcs_ref)
            da_ref[...] = jnp.zeros_like(da_ref)
            dd_ref[...] = jnp.zeros_like(dd_ref)

        _s5_project_in(u_ref, bs_ref, s_ref, L, rc)
        _s5_scan_fwd(s_ref, pw_ref, L)
        for r in range(0, L, rc):
            u = u_ref[r:r + rc, :]
            sb = s_ref[r:r + rc, :].astype(bf16)
            ypre = jnp.dot(sb, cs_ref[...], preferred_element_type=f32) + d_ref[...] * u
            dyp = dy_ref[r:r + rc, :] * _gelu_grad(ypre)
            dyp_ref[r:r + rc, :] = dyp
            dd_ref[...] += jnp.sum(dyp * u, axis=0, keepdims=True)
            dypb = dyp.astype(bf16)
            dcs_ref[...] += lax.dot_general(sb, dypb, _DIMS["tn"], preferred_element_type=f32)
            lam_ref[r:r + rc, :] = lax.dot_general(dypb, cs_ref[...], _DIMS["nt"], preferred_element_type=f32)

        row = lax.broadcasted_iota(jnp.int32, (SUBLANES, ns), 0)
        pr = pw_ref[0, 0:SUBLANES, :]
        pi = -pw_ref[1, 0:SUBLANES, :]
        qr = pw_ref[0, SUBLANES:2 * SUBLANES, :]
        qi = -pw_ref[1, SUBLANES:2 * SUBLANES, :]

        def step(j, carry):
            cr, ci, ar, ai = carry
            i = nt - 1 - j
            r0 = pl.multiple_of(i * SUBLANES, SUBLANES)
            xr = lam_ref[pl.ds(r0, SUBLANES), 0:ns]
            xi = lam_ref[pl.ds(r0, SUBLANES), ns:2 * ns]
            for k in (1, 2, 4):
                xr, xi = _cmul_add(xr, xi, pr[k - 1:k, :], pi[k - 1:k, :], _shift_up(xr, k, row), _shift_up(xi, k, row))
            xr, xi = _cmul_add(xr, xi, qr, qi, cr, ci)
            lam_ref[pl.ds(r0, SUBLANES), 0:ns] = xr
            lam_ref[pl.ds(r0, SUBLANES), ns:2 * ns] = xi
            rp = pl.multiple_of(jnp.maximum(i - 1, 0) * SUBLANES, SUBLANES)
            live = jnp.where(i > 0, 1.0, 0.0)
            lr_ = s_ref[pl.ds(rp + SUBLANES - 1, 1), 0:ns] * live
            li_ = s_ref[pl.ds(rp + SUBLANES - 1, 1), ns:2 * ns] * live
            spr = jnp.where(row == 0, lr_, pltpu.roll(s_ref[pl.ds(r0, SUBLANES), 0:ns], 1, 0))
            spi = jnp.where(row == 0, li_, pltpu.roll(s_ref[pl.ds(r0, SUBLANES), ns:2 * ns], 1, 0))
            ar = ar + xr * spr + xi * spi
            ai = ai + xi * spr - xr * spi
            return xr[0:1, :], xi[0:1, :], ar, ai

        z1 = jnp.zeros((1, ns), f32)
        z8 = jnp.zeros((SUBLANES, ns), f32)
        _, _, ar, ai = lax.fori_loop(0, nt, step, (z1, z1, z8, z8))
        da_ref[0:1, :] += jnp.sum(ar, axis=0, keepdims=True)
        da_ref[1:2, :] += jnp.sum(ai, axis=0, keepdims=True)

        for r in range(0, L, rc):
            lamb = lam_ref[r:r + rc, :].astype(bf16)
            dbs_ref[...] += lax.dot_general(u_ref[r:r + rc, :].astype(bf16), lamb, _DIMS["tn"], preferred_element_type=f32)
            du = (lax.dot_general(lamb, bs_ref[...], _DIMS["nt"], preferred_element_type=f32)
                  + d_ref[...] * dyp_ref[r:r + rc, :])
            du_ref[r:r + rc, :] = du.astype(du_ref.dtype)

    ucol = SEG_U * (D_MODEL // SLAB_CH)
    T = B * L
    return pl.pallas_call(
        body, name="s5_bwd", grid=(N_SLAB, B),
        in_specs=[pl.BlockSpec((L, SLAB_CH), lambda s, b: (b, ucol + s)),
                  pl.BlockSpec((L, SLAB_CH), lambda s, b: (b, s)),
                  ANY,
                  pl.BlockSpec((None, SLAB_CH, 2 * SLAB_NS), lambda s, b: (s, 0, 0)),
                  pl.BlockSpec((None, 2 * SLAB_NS, SLAB_CH), lambda s, b: (s, 0, 0)),
                  pl.BlockSpec((None, 2, 2 * SUBLANES, SLAB_NS), lambda s, b: (s, 0, 0, 0)),
                  pl.BlockSpec((1, SLAB_CH), lambda s, b: (0, s))],
        out_specs=[pl.BlockSpec((None, L, SLAB_CH), lambda s, b: (SEG_U, b, s)),
                   pl.BlockSpec((None, SLAB_CH, 2 * SLAB_NS), lambda s, b: (s, 0, 0)),
                   pl.BlockSpec((None, 2 * SLAB_NS, SLAB_CH), lambda s, b: (s, 0, 0)),
                   pl.BlockSpec((None, 2, SLAB_NS), lambda s, b: (s, 0, 0)),
                   pl.BlockSpec((1, SLAB_CH), lambda s, b: (0, s))],
        out_shape=[jax.ShapeDtypeStruct((N_SEG, T, D_MODEL), bf16),
                   jax.ShapeDtypeStruct((N_SLAB, SLAB_CH, 2 * SLAB_NS), f32),
                   jax.ShapeDtypeStruct((N_SLAB, 2 * SLAB_NS, SLAB_CH), f32),
                   jax.ShapeDtypeStruct((N_SLAB, 2, SLAB_NS), f32),
                   jax.ShapeDtypeStruct((1, D_MODEL), f32)],
        scratch_shapes=[pltpu.VMEM((L, 2 * SLAB_NS), f32), pltpu.VMEM((L, 2 * SLAB_NS), f32), pltpu.VMEM((L, SLAB_CH), f32)],
        input_output_aliases={2: 0},
        compiler_params=_params("parallel", "arbitrary"),
    )(p, dya0, dp, bs, cs, pw, d_skip)


def _rows8(i):
    return pl.ds(pl.multiple_of(i * SUBLANES, SUBLANES), SUBLANES)


def _repeat_loop(n, step, init):
    rep = max(u for u in (6, 4, 3, 2, 1) if n % u == 0)

    def body(t, carry):
        for u in range(rep):
            carry = step(t * rep + u, carry)
        return carry

    return lax.fori_loop(0, n // rep, body, init)


def _to_segments(src_ref, dst_ref, seg):
    def step(i, c):
        dst_ref[_rows8(i), :] = src_ref[pl.ds(i, SUBLANES, stride=seg), :]
        return c

    _repeat_loop(seg, step, 0)


def _from_segments(src_ref, dst_ref, seg):
    def step(i, c):
        dst_ref[pl.ds(i, SUBLANES, stride=seg), :] = src_ref[_rows8(i), :]
        return c

    _repeat_loop(seg, step, 0)


def _seg_local_scan(s_ref, ar, ai, seg, reverse):
    ns = SLAB_NS

    def step(j, carry):
        cr, ci = carry
        rows = _rows8(seg - 1 - j if reverse else j)
        cr, ci = _cmul_add(s_ref[rows, 0:ns], s_ref[rows, ns:2 * ns], ar, ai, cr, ci)
        s_ref[rows, 0:ns] = cr
        s_ref[rows, ns:2 * ns] = ci
        return cr, ci

    z = jnp.zeros((SUBLANES, ns), f32)
    return _repeat_loop(seg, step, (z, z))


def _seg_boundaries(fr, fi, alr, ali, reverse):
    row = lax.broadcasted_iota(jnp.int32, fr.shape, 0)
    br = jnp.zeros_like(fr)
    bi = jnp.zeros_like(fi)
    for r in (range(SUBLANES - 2, -1, -1) if reverse else range(1, SUBLANES)):
        s = r + 1 if reverse else r - 1
        nr, ni = _cmul_add(fr[s:s + 1, :], fi[s:s + 1, :], alr, ali, br[s:s + 1, :], bi[s:s + 1, :])
        br = jnp.where(row == r, nr, br)
        bi = jnp.where(row == r, ni, bi)
    return br, bi


def _s5_states(u_ref, bs_ref, pw_ref, up_ref, s_ref, L, rc):
    seg = L // SUBLANES
    ns = SLAB_NS
    _to_segments(u_ref, up_ref, seg)
    _s5_project_in(up_ref, bs_ref, s_ref, L, rc)
    ar, ai = pw_ref[0, 0:1, :], pw_ref[1, 0:1, :]
    fr, fi = _seg_local_scan(s_ref, ar, ai, seg, False)
    br, bi = _seg_boundaries(fr, fi, pw_ref[0, seg - 1:seg, :], pw_ref[1, seg - 1:seg, :], False)

    def fix(i, c):
        rows = _rows8(i)
        xr, xi = _cmul_add(s_ref[rows, 0:ns], s_ref[rows, ns:2 * ns], pw_ref[0, pl.ds(i, 1), :], pw_ref[1, pl.ds(i, 1), :], br, bi)
        s_ref[rows, 0:ns] = xr
        s_ref[rows, ns:2 * ns] = xi
        return c

    _repeat_loop(seg, fix, 0)


def _pw_spec(seg_rows, order):
    if order == "bs":
        return pl.BlockSpec((2, seg_rows, SLAB_NS), lambda b, s: (0, 0, s))
    return pl.BlockSpec((2, seg_rows, SLAB_NS), lambda s, b: (0, 0, s))


def _s5_fwd(p, bs, cs, pw, d_skip, B, L):
    rc = _tile(L, 344)
    seg = L // SUBLANES

    def body(u_ref, bs_ref, cs_ref, pw_ref, d_ref, y_ref, s_ref, up_ref, yp_ref):
        _s5_states(u_ref, bs_ref, pw_ref, up_ref, s_ref, L, rc)
        for r in range(0, L, rc):
            ypre = (jnp.dot(s_ref[r:r + rc, :].astype(bf16), cs_ref[...], preferred_element_type=f32)
                    + d_ref[...] * up_ref[r:r + rc, :])
            yp_ref[r:r + rc, :] = _gelu(ypre)
        _from_segments(yp_ref, y_ref, seg)

    ucol = SEG_U * (D_MODEL // SLAB_CH)
    return pl.pallas_call(
        body, name="s5_fwd", grid=(B, N_SLAB),
        in_specs=[pl.BlockSpec((L, SLAB_CH), lambda b, s: (b, ucol + s)),
                  pl.BlockSpec((None, SLAB_CH, 2 * SLAB_NS), lambda b, s: (s, 0, 0)),
                  pl.BlockSpec((None, 2 * SLAB_NS, SLAB_CH), lambda b, s: (s, 0, 0)),
                  _pw_spec(pw.shape[1], "bs"),
                  pl.BlockSpec((1, SLAB_CH), lambda b, s: (0, s))],
        out_specs=pl.BlockSpec((L, SLAB_CH), lambda b, s: (b, s)),
        out_shape=jax.ShapeDtypeStruct((B * L, D_MODEL), f32),
        scratch_shapes=[pltpu.VMEM((L, 2 * SLAB_NS), f32), pltpu.VMEM((L, SLAB_CH), f32), pltpu.VMEM((L, SLAB_CH), f32)],
        compiler_params=_params("parallel", "parallel"),
    )(p, bs, cs, pw, d_skip)


def _s5_bwd(p, dya0, dp, bs, cs, pw, d_skip, B, L, sums):
    rc = _tile(L, 344)
    ns = SLAB_NS
    seg = L // SUBLANES
    nx = len(sums)

    def body(u_ref, dy_ref, dp_in, bs_ref, cs_ref, pw_ref, d_ref, *rest):
        xin, (du_ref, dbs_ref, dcs_ref, da_ref, dd_ref), xout = rest[:nx], rest[nx:nx + 5], rest[nx + 5:2 * nx + 5]
        s_ref, lam_ref, up_ref, dyp_ref, nat_ref, send, recv = rest[2 * nx + 5:]
        del dp_in
        start, finish = _chip_exchange_steps(xin, xout, send, recv)

        @pl.when((pl.program_id(0) == 0) & (pl.program_id(1) == 0))
        def _():
            start()

        @pl.when(pl.program_id(1) == 0)
        def _():
            dbs_ref[...] = jnp.zeros_like(dbs_ref)
            dcs_ref[...] = jnp.zeros_like(dcs_ref)
            da_ref[...] = jnp.zeros_like(da_ref)
            dd_ref[...] = jnp.zeros_like(dd_ref)

        _s5_states(u_ref, bs_ref, pw_ref, up_ref, s_ref, L, rc)
        _to_segments(dy_ref, dyp_ref, seg)
        for r in range(0, L, rc):
            u = up_ref[r:r + rc, :]
            sb = s_ref[r:r + rc, :].astype(bf16)
            ypre = jnp.dot(sb, cs_ref[...], preferred_element_type=f32) + d_ref[...] * u
            dyp = dyp_ref[r:r + rc, :] * _gelu_grad(ypre)
            dyp_ref[r:r + rc, :] = dyp
            dd_ref[...] += jnp.sum(dyp * u, axis=0, keepdims=True)
            dypb = dyp.astype(bf16)
            dcs_ref[...] += lax.dot_general(sb, dypb, _DIMS["tn"], preferred_element_type=f32)
            lam_ref[r:r + rc, :] = lax.dot_general(dypb, cs_ref[...], _DIMS["nt"], preferred_element_type=f32)

        ar, ai = pw_ref[0, 0:1, :], -pw_ref[1, 0:1, :]
        fr, fi = _seg_local_scan(lam_ref, ar, ai, seg, True)
        br, bi = _seg_boundaries(fr, fi, pw_ref[0, seg - 1:seg, :], -pw_ref[1, seg - 1:seg, :], True)

        def fix(i, acc):
            accr, acci = acc
            rows = _rows8(i)
            k = seg - 1 - i
            xr, xi = _cmul_add(lam_ref[rows, 0:ns], lam_ref[rows, ns:2 * ns], pw_ref[0, pl.ds(k, 1), :],
                               -pw_ref[1, pl.ds(k, 1), :], br, bi)
            lam_ref[rows, 0:ns] = xr
            lam_ref[rows, ns:2 * ns] = xi
            prev = _rows8(jnp.maximum(i - 1, 0))
            live = jnp.where(i > 0, 1.0, 0.0)
            spr = s_ref[prev, 0:ns] * live
            spi = s_ref[prev, ns:2 * ns] * live
            return accr + xr * spr + xi * spi, acci + xi * spr - xr * spi

        z = jnp.zeros((SUBLANES, ns), f32)
        accr, acci = _repeat_loop(seg, fix, (z, z))
        row = lax.broadcasted_iota(jnp.int32, (SUBLANES, ns), 0)
        last = _rows8(seg - 1)
        spr = jnp.where(row == 0, 0.0, pltpu.roll(s_ref[last, 0:ns], 1, 0))
        spi = jnp.where(row == 0, 0.0, pltpu.roll(s_ref[last, ns:2 * ns], 1, 0))
        xr, xi = lam_ref[0:SUBLANES, 0:ns], lam_ref[0:SUBLANES, ns:2 * ns]
        accr = accr + xr * spr + xi * spi
        acci = acci + xi * spr - xr * spi
        da_ref[0:1, :] += jnp.sum(accr, axis=0, keepdims=True)
        da_ref[1:2, :] += jnp.sum(acci, axis=0, keepdims=True)

        for r in range(0, L, rc):
            lamb = lam_ref[r:r + rc, :].astype(bf16)
            dbs_ref[...] += lax.dot_general(up_ref[r:r + rc, :].astype(bf16), lamb, _DIMS["tn"], preferred_element_type=f32)
            nat_ref[r:r + rc, :] = (lax.dot_general(lamb, bs_ref[...], _DIMS["nt"], preferred_element_type=f32)
                                    + d_ref[...] * dyp_ref[r:r + rc, :])
        _from_segments(nat_ref, up_ref, seg)
        du_ref[...] = up_ref[...].astype(du_ref.dtype)

        @pl.when((pl.program_id(0) == N_SLAB - 1) & (pl.program_id(1) == B - 1))
        def _():
            finish()

    ucol = SEG_U * (D_MODEL // SLAB_CH)
    T = B * L
    col = pltpu.VMEM((L, SLAB_CH), f32)
    res = pl.pallas_call(
        body, name="s5_bwd", grid=(N_SLAB, B),
        in_specs=[pl.BlockSpec((L, SLAB_CH), lambda s, b: (b, ucol + s)),
                  pl.BlockSpec((L, SLAB_CH), lambda s, b: (b, s)),
                  ANY,
                  pl.BlockSpec((None, SLAB_CH, 2 * SLAB_NS), lambda s, b: (s, 0, 0)),
                  pl.BlockSpec((None, 2 * SLAB_NS, SLAB_CH), lambda s, b: (s, 0, 0)),
                  _pw_spec(pw.shape[1], "sb"),
                  pl.BlockSpec((1, SLAB_CH), lambda s, b: (0, s))] + [ANY] * nx,
        out_specs=[pl.BlockSpec((None, L, SLAB_CH), lambda s, b: (SEG_U, b, s)),
                   pl.BlockSpec((None, SLAB_CH, 2 * SLAB_NS), lambda s, b: (s, 0, 0)),
                   pl.BlockSpec((None, 2 * SLAB_NS, SLAB_CH), lambda s, b: (s, 0, 0)),
                   pl.BlockSpec((None, 2, SLAB_NS), lambda s, b: (s, 0, 0)),
                   pl.BlockSpec((1, SLAB_CH), lambda s, b: (0, s))] + [ANY] * nx,
        out_shape=[jax.ShapeDtypeStruct((N_SEG, T, D_MODEL), bf16),
                   jax.ShapeDtypeStruct((N_SLAB, SLAB_CH, 2 * SLAB_NS), f32),
                   jax.ShapeDtypeStruct((N_SLAB, 2 * SLAB_NS, SLAB_CH), f32),
                   jax.ShapeDtypeStruct((N_SLAB, 2, SLAB_NS), f32),
                   jax.ShapeDtypeStruct((1, D_MODEL), f32)] + [jax.ShapeDtypeStruct(a.shape, a.dtype) for a in sums],
        scratch_shapes=[pltpu.VMEM((L, 2 * SLAB_NS), f32), pltpu.VMEM((L, 2 * SLAB_NS), f32), col, col, col]
        + _chip_exchange_sems(nx),
        input_output_aliases={2: 0},
        compiler_params=_params("arbitrary", "arbitrary"),
    )(p, dya0, dp, bs, cs, pw, d_skip, *sums)
    return res[:5], res[5:]


def _dotb(a, b, dims="nn"):
    return lax.dot_general(a.astype(bf16), b.astype(bf16), _DIMS[dims], preferred_element_type=f32)


def _tile_scan(x, reverse):
    n, w = x.shape
    v = x.reshape(n // SUBLANES, SUBLANES, w)
    row = lax.broadcasted_iota(jnp.int32, v.shape, 1)
    for k in (1, 2, 4):
        if reverse:
            v = v + jnp.where(row < SUBLANES - k, pltpu.roll(v, SUBLANES - k, 1), 0.0)
        else:
            v = v + jnp.where(row >= k, pltpu.roll(v, k, 1), 0.0)
    p = v.reshape(n // CHUNK, 2, SUBLANES, w)
    lo, hi = p[:, 0], p[:, 1]
    if reverse:
        lo = lo + hi[:, 0:1, :]
    else:
        hi = hi + lo[:, SUBLANES - 1:SUBLANES, :]
    return jnp.stack([lo, hi], axis=1).reshape(n, w)


def _chunk_cumsum(x):
    return _tile_scan(x, False)


def _chunk_rev_cumsum(x):
    return _tile_scan(x, True)


def _chunk_last(x):
    n, w = x.shape
    p = x.reshape(n // CHUNK, CHUNK, w)
    return jnp.broadcast_to(p[:, CHUNK - 1:CHUNK, :], p.shape).reshape(n, w)


def _hgrn_local(q, fl, lb):
    sg = _sigmoid(fl)
    f = lb + (1.0 - lb) * sg
    g = jnp.log(f)
    cum = _chunk_cumsum(g)
    rest = _chunk_last(cum) - cum
    e = jnp.exp(cum)
    em = jnp.exp(-cum)
    eo = jnp.exp(rest)
    k = 1.0 - f
    return sg, f, e, em, eo, q * e, k * em, k * eo, jnp.exp(cum + rest)


def _hgrn_block_mask(n):
    r = lax.broadcasted_iota(jnp.int32, (n, n), 0)
    c = lax.broadcasted_iota(jnp.int32, (n, n), 1)
    return ((r & -CHUNK) == (c & -CHUNK)) & (c <= r)


def _chunk_pos(n):
    return lax.broadcasted_iota(jnp.int32, (n, HEAD_DIM), 0) & (CHUNK - 1)


def _hgrn_block_rows(L):
    return _tile(L, 688, CHUNK)


def _chunk_rows(c):
    return pl.ds(pl.multiple_of(c * CHUNK, CHUNK), CHUNK)


def _chunk_loop(nc, step):
    rep = max(u for u in range(1, 49) if nc % u == 0)

    def body(i, carry):
        for u in range(rep):
            step(i * rep + u)
        return carry

    lax.fori_loop(0, nc // rep, body, 0)


def _hgrn_specs(L, order):
    hb = D_MODEL // HEAD_DIM

    def spec(seg):
        if order == "bh":
            return pl.BlockSpec((L, HEAD_DIM), lambda b, h: (b, seg * hb + h))
        return pl.BlockSpec((L, HEAD_DIM), lambda h, b: (b, seg * hb + h))

    return [spec(SEG_Q), spec(SEG_F), spec(SEG_I), spec(SEG_OG)]


def _hgrn_fwd(p, lb, norm_g, B, L):
    nc = L // CHUNK

    rb = _hgrn_block_rows(L)

    def body(q_ref, f_ref, v_ref, og_ref, lb_ref, ng_ref, y_ref, qt_s, ko_s, vb_s, dec_s, o_s, u_s, sb_s):
        lbv = lb_ref[...]
        ngv = ng_ref[...]
        mask = _hgrn_block_mask(rb)

        for r in range(0, L, rb):
            rows = slice(r, r + rb)
            _, _, _, _, _, qt, kt, ko, dec = _hgrn_local(q_ref[rows, :], f_ref[rows, :], lbv)
            vb = v_ref[rows, :].astype(bf16)
            qtb = qt.astype(bf16)
            pm = jnp.where(mask, _dotb(qtb, kt, "nt"), 0.0)
            o_s[rows, :] = _dotb(pm, vb)
            qt_s[rows, :] = qtb
            ko_s[rows, :] = ko.astype(bf16)
            vb_s[rows, :] = vb
            dec_s[rows, :] = dec

        def update(c):
            rows = _chunk_rows(c)
            u_s[c] = _dotb(vb_s[rows, :], ko_s[rows, :], "tn")

        def chain(c, st):
            sb_s[c] = st.astype(bf16)
            return st * dec_s[_chunk_rows(c), :][0:1, :] + u_s[c]

        def attend(c):
            rows = _chunk_rows(c)
            o_s[rows, :] += _dotb(qt_s[rows, :], sb_s[c], "nt")

        _chunk_loop(nc, update)
        lax.fori_loop(0, nc, chain, jnp.zeros((HEAD_DIM, HEAD_DIM), f32))
        _chunk_loop(nc, attend)

        for r in range(0, L, rb):
            rows = slice(r, r + rb)
            o = o_s[rows, :]
            og = og_ref[rows, :]
            on = o * lax.rsqrt(jnp.mean(o * o, axis=-1, keepdims=True) + EPS) * ngv
            y_ref[rows, :] = (on * og * _sigmoid(og)).astype(y_ref.dtype)

    return pl.pallas_call(
        body, name="hgrn_fwd", grid=(B, HEADS),
        in_specs=_hgrn_specs(L, "bh") + [pl.BlockSpec((1, HEAD_DIM), lambda b, h: (0, h)),
                                          pl.BlockSpec((1, HEAD_DIM), lambda b, h: (0, 0))],
        out_specs=pl.BlockSpec((L, HEAD_DIM), lambda b, h: (b, h)),
        out_shape=jax.ShapeDtypeStruct((B * L, D_MODEL), bf16),
        scratch_shapes=[pltpu.VMEM((L, HEAD_DIM), bf16), pltpu.VMEM((L, HEAD_DIM), bf16), pltpu.VMEM((L, HEAD_DIM), bf16),
                        pltpu.VMEM((L, HEAD_DIM), f32), pltpu.VMEM((L, HEAD_DIM), f32),
                        pltpu.VMEM((nc, HEAD_DIM, HEAD_DIM), f32), pltpu.VMEM((nc, HEAD_DIM, HEAD_DIM), bf16)],
        compiler_params=_params("parallel", "parallel"),
    )(p, p, p, p, lb, norm_g)


def _hgrn_bwd(p, dyb, dp, lb, norm_g, B, L):
    nc = L // CHUNK

    rb = _hgrn_block_rows(L)

    def body(q_ref, f_ref, v_ref, og_ref, dy_ref, dp_in, lb_ref, ng_ref, dseg_ref, dlb_ref, dng_ref,
             st_ref, u_s, dsb_s, qt_s, kt_s, ko_s, vb_s, do_s, dec_s, o_s, dqt_s, dkt_s, dko_s, dv_s, ddec_s):
        del dp_in
        lbv = lb_ref[...]
        ngv = ng_ref[...]
        mask = _hgrn_block_mask(rb)
        pos = _chunk_pos(rb)
        blocks = [slice(r, r + rb) for r in range(0, L, rb)]

        @pl.when(pl.program_id(1) == 0)
        def _():
            dlb_ref[...] = jnp.zeros_like(dlb_ref)

        @pl.when((pl.program_id(0) == 0) & (pl.program_id(1) == 0))
        def _():
            dng_ref[...] = jnp.zeros_like(dng_ref)

        def scores(rows):
            return jnp.where(mask, _dotb(qt_s[rows, :], kt_s[rows, :], "nt"), 0.0).astype(bf16)

        for rows in blocks:
            _, _, _, _, _, qt, kt, ko, dec = _hgrn_local(q_ref[rows, :], f_ref[rows, :], lbv)
            qt_s[rows, :] = qt.astype(bf16)
            kt_s[rows, :] = kt.astype(bf16)
            ko_s[rows, :] = ko.astype(bf16)
            vb_s[rows, :] = v_ref[rows, :].astype(bf16)
            dec_s[rows, :] = dec
            o_s[rows, :] = _dotb(scores(rows), vb_s[rows, :])

        def update(c):
            rows = _chunk_rows(c)
            u_s[c] = _dotb(vb_s[rows, :], ko_s[rows, :], "tn")

        def chain(c, st):
            st_ref[c] = st
            return st * dec_s[_chunk_rows(c), :][0:1, :] + u_s[c]

        def attend(c):
            rows = _chunk_rows(c)
            o_s[rows, :] += _dotb(qt_s[rows, :], st_ref[c], "nt")

        _chunk_loop(nc, update)
        lax.fori_loop(0, nc, chain, jnp.zeros((HEAD_DIM, HEAD_DIM), f32))
        _chunk_loop(nc, attend)

        dng = jnp.zeros((1, HEAD_DIM), f32)
        for rows in blocks:
            o = o_s[rows, :]
            og = og_ref[rows, :]
            dy = dy_ref[rows, :]
            rs = lax.rsqrt(jnp.mean(o * o, axis=-1, keepdims=True) + EPS)
            xn = o * rs
            so = _sigmoid(og)
            dseg_ref[SEG_OG, rows, :] = (dy * xn * ngv * so * (1.0 + og * (1.0 - so))).astype(dseg_ref.dtype)
            don = dy * og * so
            dng = dng + jnp.sum(don * xn, axis=0, keepdims=True)
            dxo = don * ngv
            do = (rs * (dxo - xn * jnp.mean(dxo * xn, axis=-1, keepdims=True))).astype(bf16)
            do_s[rows, :] = do
            dpm = jnp.where(mask, _dotb(do, vb_s[rows, :], "nt"), 0.0).astype(bf16)
            dqt_s[rows, :] = _dotb(dpm, kt_s[rows, :])
            dkt_s[rows, :] = _dotb(dpm, qt_s[rows, :], "tn")
            dv_s[rows, :] = _dotb(scores(rows), do, "tn")
        dng_ref[...] += dng

        def rupdate(c):
            rows = _chunk_rows(c)
            u_s[c] = _dotb(do_s[rows, :], qt_s[rows, :], "tn")

        def rchain(j, dst):
            c = nc - 1 - j
            rows = _chunk_rows(c)
            dsb_s[c] = dst.astype(bf16)
            ddec_s[rows, :] = jnp.broadcast_to(jnp.sum(dst * st_ref[c], axis=0, keepdims=True), (CHUNK, HEAD_DIM))
            return dst * dec_s[rows, :][0:1, :] + u_s[c]

        def rattend(c):
            rows = _chunk_rows(c)
            dst = dsb_s[c]
            dqt_s[rows, :] += _dotb(do_s[rows, :], st_ref[c])
            dv_s[rows, :] += _dotb(ko_s[rows, :], dst, "nt")
            dko_s[rows, :] = _dotb(vb_s[rows, :], dst)

        _chunk_loop(nc, rupdate)
        lax.fori_loop(0, nc, rchain, jnp.zeros((HEAD_DIM, HEAD_DIM), f32))
        _chunk_loop(nc, rattend)

        dlb = jnp.zeros((1, HEAD_DIM), f32)
        for rows in blocks:
            sg, f, e, em, eo, qt, kt, ko, dec = _hgrn_local(q_ref[rows, :], f_ref[rows, :], lbv)
            dqt = dqt_s[rows, :]
            dkt = dkt_s[rows, :]
            dko = dko_s[rows, :]
            dko_ko = dko * ko
            dcum = dqt * qt - dkt * kt - dko_ko
            dcum = dcum + jnp.where(pos == CHUNK - 1, _chunk_cumsum(dko_ko) + ddec_s[rows, :] * dec, 0.0)
            df = _chunk_rev_cumsum(dcum) / f - (dkt * em + dko * eo)
            dlb = dlb + jnp.sum(df * (1.0 - sg), axis=0, keepdims=True)
            dseg_ref[SEG_Q, rows, :] = (dqt * e).astype(dseg_ref.dtype)
            dseg_ref[SEG_F, rows, :] = (df * (1.0 - lbv) * sg * (1.0 - sg)).astype(dseg_ref.dtype)
            dseg_ref[SEG_I, rows, :] = dv_s[rows, :].astype(dseg_ref.dtype)
        dlb_ref[...] += dlb

    T = B * L
    sb = pltpu.VMEM((L, HEAD_DIM), bf16)
    sf = pltpu.VMEM((L, HEAD_DIM), f32)
    return pl.pallas_call(
        body, name="hgrn_bwd", grid=(HEADS, B),
        in_specs=_hgrn_specs(L, "hb") + [pl.BlockSpec((L, HEAD_DIM), lambda h, b: (b, h)), ANY,
                                          pl.BlockSpec((1, HEAD_DIM), lambda h, b: (0, h)),
                                          pl.BlockSpec((1, HEAD_DIM), lambda h, b: (0, 0))],
        out_specs=[pl.BlockSpec((4, L, HEAD_DIM), lambda h, b: (0, b, h)),
                   pl.BlockSpec((1, HEAD_DIM), lambda h, b: (0, h)),
                   pl.BlockSpec((1, HEAD_DIM), lambda h, b: (0, 0))],
        out_shape=[jax.ShapeDtypeStruct((N_SEG, T, D_MODEL), bf16), jax.ShapeDtypeStruct((1, D_MODEL), f32),
                   jax.ShapeDtypeStruct((1, HEAD_DIM), f32)],
        scratch_shapes=[pltpu.VMEM((nc, HEAD_DIM, HEAD_DIM), f32), pltpu.VMEM((nc, HEAD_DIM, HEAD_DIM), f32),
                        pltpu.VMEM((nc, HEAD_DIM, HEAD_DIM), bf16), sb, sb, sb, sb, sb, sf, sf, sf, sf, sf, sf, sf],
        input_output_aliases={5: 0},
        compiler_params=_params("arbitrary", "arbitrary"),
    )(p, p, p, p, dyb, dp, lb, norm_g)


def _dz1(dp, w_in_phys):
    _, T, Dm = dp.shape
    tm = _tile(T, 1032)
    return _mm("dz1", dp, w_in_phys, "nt", (T // tm, 1, N_SEG),
               pl.BlockSpec((None, tm, Dm), lambda i, j, k: (k, i, 0)),
               pl.BlockSpec((Dm, Dm), lambda i, j, k: (0, k)),
               jax.ShapeDtypeStruct((T, Dm), f32), pl.BlockSpec((tm, Dm), lambda i, j, k: (i, 0)), (tm, Dm))


def _dz1_norm(dp, w_in_phys, h0, g, dh1):
    _, T, Dm = dp.shape
    tm = _tile(T, 688)
    return _mm_rmsnorm_bwd("dz1", dp, w_in_phys, (T // tm, 1, N_SEG),
                           pl.BlockSpec((None, tm, Dm), lambda i, j, k: (k, i, 0)),
                           pl.BlockSpec((Dm, Dm), lambda i, j, k: (0, k)), h0, g, dh1)


def _dz2_norm(dup, w_up, h1, g, dh2):
    _, T, _ = dup.shape
    tm = _tile(T, 688)
    tk = D_FF // 2
    return _mm_rmsnorm_bwd("dz2", dup, w_up, (T // tm, 1, 4),
                           pl.BlockSpec((None, tm, tk), lambda i, j, k: (k // 2, i, k % 2)),
                           pl.BlockSpec((D_MODEL, tk), lambda i, j, k: (0, k)), h1, g, dh2)


def _dw_in(z1, dp):
    _, T, Dm = dp.shape
    tk = _tile(T, 1376)
    return _mm("dw_in", z1, dp, "tn", (1, N_SEG, T // tk),
               pl.BlockSpec((tk, Dm), lambda i, j, k: (k, 0)),
               pl.BlockSpec((None, tk, Dm), lambda i, j, k: (j, k, 0)),
               jax.ShapeDtypeStruct((N_SEG, Dm, Dm), f32),
               pl.BlockSpec((None, Dm, Dm), lambda i, j, k: (j, 0, 0)), (Dm, Dm))


def _dz2(dup, w_up):
    _, T, _ = dup.shape
    tm = _tile(T, 1032)
    tk = D_FF // 2
    return _mm("dz2", dup, w_up, "nt", (T // tm, 1, 4),
               pl.BlockSpec((None, tm, tk), lambda i, j, k: (k // 2, i, k % 2)),
               pl.BlockSpec((D_MODEL, tk), lambda i, j, k: (0, k)),
               jax.ShapeDtypeStruct((T, D_MODEL), f32), pl.BlockSpec((tm, D_MODEL), lambda i, j, k: (i, 0)), (tm, D_MODEL))


def _dw_up(z2, dup):
    _, T, _ = dup.shape
    tn = D_FF // 2
    tk = _tile(T, 688)
    return _mm("dw_up", z2, dup, "tn", (1, N_CHIPS, T // tk),
               pl.BlockSpec((tk, D_MODEL), lambda i, j, k: (k, 0)),
               pl.BlockSpec((None, tk, tn), lambda i, j, k: (j // 2, k, j % 2)),
               jax.ShapeDtypeStruct((N_CHIPS, D_MODEL, tn), f32),
               pl.BlockSpec((None, D_MODEL, tn), lambda i, j, k: (j, 0, 0)), (D_MODEL, tn))


def _place():
    x, y, c = lax.axis_index("x"), lax.axis_index("y"), lax.axis_index("c")
    chips = [(1 - x, y), (x, 1 - y), (1 - x, 1 - y)]
    return x, y, c, chips


def _allgather_chips(arrs):
    n = len(arrs)

    def body(*refs):
        ins, outs = refs[:n], refs[n:2 * n]
        send, recv, local = refs[2 * n:]
        x, y, c, chips = _place()
        me = 2 * x + y

        def copy(a, k, slot):
            px, py = chips[k]
            return pltpu.make_async_remote_copy(src_ref=ins[a], dst_ref=outs[a].at[slot], send_sem=send.at[3 * a + k],
                                                recv_sem=recv.at[3 * a + k], device_id=(px, py, c), device_id_type=MESH)

        for a in range(n):
            pltpu.make_async_copy(ins[a], outs[a].at[me], local.at[a]).start()
            for k in range(3):
                copy(a, k, me).start()
        for a in range(n):
            for k, (px, py) in enumerate(chips):
                copy(a, k, 2 * px + py).wait_recv()
        for a in range(n):
            pltpu.make_async_copy(ins[a], outs[a].at[me], local.at[a]).wait()
            for k in range(3):
                copy(a, k, me).wait_send()

    return pl.pallas_call(
        body, name="allgather_chips", in_specs=[ANY] * n, out_specs=[ANY] * n,
        out_shape=[jax.ShapeDtypeStruct((N_CHIPS,) + a.shape, a.dtype) for a in arrs],
        scratch_shapes=[pltpu.SemaphoreType.DMA((3 * n,)), pltpu.SemaphoreType.DMA((3 * n,)), pltpu.SemaphoreType.DMA((n,))],
    )(*arrs)


def _allgather_split(arrs):
    n = len(arrs)

    def body(*refs):
        start, finish = _gather_split_steps(refs[:n], refs[n:2 * n], *refs[2 * n:])
        start()
        finish()

    return pl.pallas_call(
        body, name="allgather_split", in_specs=[ANY] * n, out_specs=[ANY] * n,
        out_shape=[jax.ShapeDtypeStruct((N_CHIPS,) + a.shape, a.dtype) for a in arrs],
        scratch_shapes=_gather_split_sems(n),
    )(*arrs)


def _gather_split_sems(n):
    return [pltpu.SemaphoreType.DMA((3 * n,)) for _ in range(4)]


def _gather_split_steps(ins, outs, send, recv, fsend, frecv):
    n = len(ins)

    def place():
        x, y, c, chips = _place()
        return x, y, c, chips, 2 * x + y

    def half(a, core):
        rh = ins[a].shape[0] // 2
        return pl.ds(core * rh, rh)

    def copy(a, k, slot):
        x, y, c, chips, _ = place()
        px, py = chips[k]
        return pltpu.make_async_remote_copy(src_ref=ins[a].at[half(a, c), :], dst_ref=outs[a].at[slot, half(a, c), :],
                                            send_sem=send.at[3 * a + k], recv_sem=recv.at[3 * a + k],
                                            device_id=(px, py, c), device_id_type=MESH)

    def forward(a, k, core):
        x, y, c, chips, _ = place()
        px, py = chips[k]
        rows = outs[a].at[2 * px + py, half(a, core), :]
        return pltpu.make_async_remote_copy(src_ref=rows, dst_ref=rows, send_sem=fsend.at[3 * a + k],
                                            recv_sem=frecv.at[3 * a + k], device_id=(x, y, 1 - c), device_id_type=MESH)

    def start():
        me = place()[4]
        for a in range(n):
            for k in range(3):
                copy(a, k, me).start()

    def finish():
        x, y, c, chips, me = place()
        for a in range(n):
            for k, (px, py) in enumerate(chips):
                copy(a, k, 2 * px + py).wait_recv()
                forward(a, k, c).start()
        for a in range(n):
            for k in range(3):
                forward(a, k, 1 - c).wait_recv()
        for a in range(n):
            for k in range(3):
                copy(a, k, me).wait_send()
                forward(a, k, c).wait_send()

    return start, finish


def _in_proj_gather(z1, w_in, shards):
    n = len(shards)
    T, K = z1.shape
    N = w_in.shape[1]
    tm = _tile(T, 1032)
    tn = 1024
    grid = (T // tm, N // tn)

    def body(a_ref, b_ref, *rest):
        ins, o_ref, outs, sems = rest[:n], rest[n], rest[n + 1:2 * n + 1], rest[2 * n + 1:]
        start, finish = _gather_split_steps(ins, outs, *sems)
        i, j = pl.program_id(0), pl.program_id(1)

        @pl.when((i == 0) & (j == 0))
        def _():
            start()

        o_ref[...] = jnp.dot(a_ref[...], b_ref[...], preferred_element_type=f32)

        @pl.when((i == grid[0] - 1) & (j == grid[1] - 1))
        def _():
            finish()

    res = pl.pallas_call(
        body, name="in_proj", grid=grid,
        in_specs=[pl.BlockSpec((tm, K), lambda i, j: (i, 0)), pl.BlockSpec((K, tn), lambda i, j: (0, j))] + [ANY] * n,
        out_specs=[pl.BlockSpec((tm, tn), lambda i, j: (i, j))] + [ANY] * n,
        out_shape=[jax.ShapeDtypeStruct((T, N), f32)] + [jax.ShapeDtypeStruct((N_CHIPS,) + a.shape, a.dtype) for a in shards],
        scratch_shapes=_gather_split_sems(n),
        compiler_params=_params("arbitrary", "arbitrary"),
    )(z1, w_in, *shards)
    return res[0], res[1:]


def _sibling_halves(parts, name="sibling_halves"):
    n = len(parts)

    def body(*refs):
        ins, outs = refs[:n], refs[n:2 * n]
        send, recv = refs[2 * n:]
        x, y, c, _ = _place()

        def copy(a):
            rh = ins[a].shape[1] // 2
            return pltpu.make_async_remote_copy(src_ref=ins[a].at[:, pl.ds((1 - c) * rh, rh), :], dst_ref=outs[a],
                                                send_sem=send.at[a], recv_sem=recv.at[a], device_id=(x, y, 1 - c),
                                                device_id_type=MESH)

        for a in range(n):
            copy(a).start()
        for a in range(n):
            copy(a).wait_recv()
        for a in range(n):
            copy(a).wait_send()

    return pl.pallas_call(
        body, name=name, in_specs=[ANY] * n, out_specs=[ANY] * n,
        out_shape=[jax.ShapeDtypeStruct((a.shape[0], a.shape[1] // 2, a.shape[2]), a.dtype) for a in parts],
        scratch_shapes=[pltpu.SemaphoreType.DMA((n,)), pltpu.SemaphoreType.DMA((n,))],
    )(*parts)


def _add_own_half(name, part, got, core):
    nchip, R, C = part.shape
    rh = R // 2
    tr = _tile(rh, 256, 2 * SUBLANES)
    nt = rh // tr

    def body(core_ref, a_ref, b_ref, o_ref):
        del core_ref
        o_ref[...] = (a_ref[...] + b_ref[...]).astype(o_ref.dtype)

    return pl.pallas_call(
        body, name=name,
        grid_spec=pltpu.PrefetchScalarGridSpec(
            num_scalar_prefetch=1, grid=(nchip, nt),
            in_specs=[pl.BlockSpec((None, tr, C), lambda j, i, core_ref: (j, core_ref[0] * nt + i, 0)),
                      pl.BlockSpec((None, tr, C), lambda j, i, core_ref: (j, i, 0))],
            out_specs=pl.BlockSpec((None, tr, C), lambda j, i, core_ref: (j, i, 0))),
        out_shape=jax.ShapeDtypeStruct((nchip, rh, C), bf16), compiler_params=_params("parallel", "parallel"),
    )(core, part, got)


def _add_own_half_w_in(part, got, core):
    _, R, C = part.shape
    rh = R // 2
    tr = _tile(rh, 256, 2 * SUBLANES)
    nt = rh // tr
    tn = 256
    per_seg = C // tn
    per_chip = IN_COLS // N_CHIPS // tn

    def src(j):
        return ((j // per_seg + N_SEG - 1) % N_SEG, j % per_seg)

    def body(core_ref, a_ref, b_ref, o_ref):
        del core_ref
        o_ref[...] = (a_ref[...] + b_ref[...]).astype(o_ref.dtype)

    return pl.pallas_call(
        body, name="add_half_w_in",
        grid_spec=pltpu.PrefetchScalarGridSpec(
            num_scalar_prefetch=1, grid=(IN_COLS // tn, nt),
            in_specs=[pl.BlockSpec((None, tr, tn), lambda j, i, core_ref: (src(j)[0], core_ref[0] * nt + i, src(j)[1])),
                      pl.BlockSpec((None, tr, tn), lambda j, i, core_ref: (src(j)[0], i, src(j)[1]))],
            out_specs=pl.BlockSpec((None, tr, tn), lambda j, i, core_ref: (j // per_chip, i, j % per_chip))),
        out_shape=jax.ShapeDtypeStruct((N_CHIPS, rh, IN_COLS // N_CHIPS), bf16), compiler_params=_params("parallel", "parallel"),
    )(core, part, got)


def _chip_exchange(sums):
    n = len(sums)

    def body(*refs):
        start, finish = _chip_exchange_steps(refs[:n], refs[n:2 * n], *refs[2 * n:])
        start()
        finish()

    return pl.pallas_call(
        body, name="chip_exchange", in_specs=[ANY] * n, out_specs=[ANY] * n,
        out_shape=[jax.ShapeDtypeStruct(a.shape, a.dtype) for a in sums],
        scratch_shapes=_chip_exchange_sems(n),
    )(*sums)


def _chip_exchange_sems(n):
    return [pltpu.SemaphoreType.DMA((3 * n,)), pltpu.SemaphoreType.DMA((3 * n,))]


def _chip_exchange_steps(ins, outs, send, recv):
    n = len(ins)

    def copy(a, k, own_slot):
        x, y, c, chips = _place()
        px, py = chips[k]
        slot = 2 * x + y if own_slot else 2 * px + py
        return pltpu.make_async_remote_copy(src_ref=ins[a].at[2 * px + py], dst_ref=outs[a].at[slot], send_sem=send.at[3 * a + k],
                                            recv_sem=recv.at[3 * a + k], device_id=(px, py, c), device_id_type=MESH)

    def start():
        for a in range(n):
            for k in range(3):
                copy(a, k, True).start()

    def finish():
        for a in range(n):
            for k in range(3):
                copy(a, k, False).wait_recv()
        for a in range(n):
            for k in range(3):
                copy(a, k, True).wait_send()

    return start, finish


def _sum_chips(name, slots, sums, where):
    nchip, rh, C = slots.shape
    tr = _tile(rh, 256, 2 * SUBLANES)
    nt = rh // tr

    def body(where_ref, own_ref, s1_ref, s2_ref, s3_ref, o_ref):
        me = where_ref[0]
        by_dist = [r[...].astype(f32) for r in (own_ref, s1_ref, s2_ref, s3_ref)]
        acc = None
        for j in range(nchip):
            d = me ^ j
            term = jnp.where(d == 0, by_dist[0], jnp.where(d == 1, by_dist[1], jnp.where(d == 2, by_dist[2], by_dist[3])))
            acc = term if acc is None else acc + term
        o_ref[...] = acc

    def other(d):
        return pl.BlockSpec((None, tr, C), lambda i, w: (w[0] ^ d, i, 0))

    return pl.pallas_call(
        body, name=name,
        grid_spec=pltpu.PrefetchScalarGridSpec(
            num_scalar_prefetch=1, grid=(nt,),
            in_specs=[other(0), other(1), other(2), other(3)],
            out_specs=pl.BlockSpec((tr, C), lambda i, w: (w[1] * nt + i, 0))),
        out_shape=jax.ShapeDtypeStruct((2 * rh, C), f32), compiler_params=_params("parallel"),
    )(where, sums, slots, slots, slots)


def _sum_slots(name, slots):
    ns, R, C = slots.shape
    tr = _tile(R, 256)

    def body(s_ref, o_ref):
        acc = s_ref[0]
        for j in range(1, ns):
            acc = acc + s_ref[j]
        o_ref[...] = acc

    return pl.pallas_call(
        body, name=name, grid=(R // tr,), in_specs=[pl.BlockSpec((ns, tr, C), lambda i: (0, i, 0))],
        out_specs=pl.BlockSpec((tr, C), lambda i: (i, 0)), out_shape=jax.ShapeDtypeStruct((R, C), f32),
        compiler_params=_params("parallel"),
    )(slots)


def _sibling_join(fulls):
    n = len(fulls)

    def body(*refs):
        ins, outs = refs[:n], refs[n:2 * n]
        send, recv = refs[2 * n:]
        x, y, c, _ = _place()

        def copy(a, core):
            rh = ins[a].shape[0] // 2
            rows = pl.ds(core * rh, rh)
            return pltpu.make_async_remote_copy(src_ref=ins[a].at[rows, :], dst_ref=outs[a].at[rows, :], send_sem=send.at[a],
                                                recv_sem=recv.at[a], device_id=(x, y, 1 - c), device_id_type=MESH)

        for a in range(n):
            copy(a, c).start()
        for a in range(n):
            copy(a, 1 - c).wait_recv()
        for a in range(n):
            copy(a, c).wait_send()

    return pl.pallas_call(
        body, name="sibling_join", in_specs=[ANY] * n, out_specs=[ANY] * n,
        out_shape=[jax.ShapeDtypeStruct(a.shape, a.dtype) for a in fulls],
        scratch_shapes=[pltpu.SemaphoreType.DMA((n,)), pltpu.SemaphoreType.DMA((n,))],
        input_output_aliases={a: a for a in range(n)},
    )(*fulls)


def _allgather_devices(v):
    def body(v_ref, out_ref, send, recv):
        x, y, c, chips = _place()
        me, sibling = (x, y, c), (x, y, 1 - c)

        def slot(px, py, pc):
            return out_ref.at[4 * px + 2 * py + pc]

        def copy(k, block, to, src=None):
            return pltpu.make_async_remote_copy(src_ref=slot(*block) if src is None else src, dst_ref=slot(*block),
                                                send_sem=send.at[k], recv_sem=recv.at[k], device_id=to, device_id_type=MESH)

        first = [copy(0, me, sibling, src=v_ref)] + [copy(1 + j, me, (*chip, c), src=v_ref) for j, chip in enumerate(chips)]
        for cp in first:
            cp.start()
        passed = [copy(4 + j, (*chip, c), sibling) for j, chip in enumerate(chips)]
        for j, chip in enumerate(chips):
            copy(1 + j, (*chip, c), me).wait_recv()
            passed[j].start()
        copy(0, sibling, me).wait_recv()
        for j, chip in enumerate(chips):
            copy(4 + j, (*chip, 1 - c), me).wait_recv()
        for cp in first + passed:
            cp.wait_send()

    return pl.pallas_call(
        body, name="allgather_devices", in_specs=[ANY], out_specs=ANY,
        out_shape=jax.ShapeDtypeStruct((N_DEV,) + v.shape, v.dtype),
        scratch_shapes=[pltpu.SemaphoreType.DMA((N_DEV - 1,)), pltpu.SemaphoreType.DMA((N_DEV - 1,))],
    )(v)


def _adamw(name, w, g, m, v):
    R, C = w.shape
    tr = _tile(R, 256)
    c1 = 1.0 / (1.0 - ADAM_B1 ** ADAM_STEP)
    c2 = 1.0 / (1.0 - ADAM_B2 ** ADAM_STEP)

    def body(w_ref, g_ref, m_ref, v_ref, d_ref, nm_ref, nv_ref):
        gv = g_ref[...]
        nm = ADAM_B1 * m_ref[...] + (1.0 - ADAM_B1) * gv
        nv = ADAM_B2 * v_ref[...] + (1.0 - ADAM_B2) * gv * gv
        d_ref[...] = -ADAM_LR * ((nm * c1) / (jnp.sqrt(nv * c2) + ADAM_EPS) + ADAM_WD * w_ref[...])
        nm_ref[...] = nm
        nv_ref[...] = nv

    row = pl.BlockSpec((tr, C), lambda i: (i, 0))
    sh = jax.ShapeDtypeStruct((R, C), f32)
    return pl.pallas_call(body, name=name, grid=(R // tr,), in_specs=[row] * 4, out_specs=[row] * 3,
                          out_shape=[sh, sh, sh], compiler_params=_params("parallel"))(w, g, m, v)


def _adamw_update(w, g, m, v):
    c1 = 1.0 / (1.0 - ADAM_B1 ** ADAM_STEP)
    c2 = 1.0 / (1.0 - ADAM_B2 ** ADAM_STEP)
    nm = ADAM_B1 * m + (1.0 - ADAM_B1) * g
    nv = ADAM_B2 * v + (1.0 - ADAM_B2) * g * g
    return -ADAM_LR * ((nm * c1) / (jnp.sqrt(nv * c2) + ADAM_EPS) + ADAM_WD * w), nm, nv


def _adamw_many(ws, gs, ms, vs):
    n = len(ws)

    def body(*refs):
        ins, outs = refs[:4 * n], refs[4 * n:]
        for a in range(n):
            d, nm, nv = _adamw_update(ins[a][...], ins[n + a][...], ins[2 * n + a][...], ins[3 * n + a][...])
            outs[a][...] = d
            outs[n + a][...] = nm
            outs[2 * n + a][...] = nv

    shapes = [jax.ShapeDtypeStruct(a.shape, f32) for a in ws]
    return pl.pallas_call(body, name="adamw_small", out_shape=shapes * 3)(*ws, *gs, *ms, *vs)


def _zoh(lr, li, log_dt, b_re, b_im):
    dt = jnp.exp(log_dt)[:, None]
    mag = jnp.exp(lr * dt)
    ab_re = mag * jnp.cos(li * dt)
    ab_im = mag * jnp.sin(li * dt)
    den = lr * lr + li * li
    nr = ab_re - 1.0
    coef_re = (nr * lr + ab_im * li) / den
    coef_im = (ab_im * lr - nr * li) / den
    bb_re = coef_re[..., None] * b_re - coef_im[..., None] * b_im
    bb_im = coef_re[..., None] * b_im + coef_im[..., None] * b_re
    return ab_re, ab_im, bb_re, bb_im


def _s5_tables(ab_re, ab_im, bb_re, bb_im, c_re, c_im, seg):
    eye = jnp.eye(SLAB_GROUPS, dtype=f32)

    def blk_in(bb):
        return jnp.einsum("sgph,gk->sghkp", bb.reshape(N_SLAB, SLAB_GROUPS, SSM_STATE, SSM_GROUP), eye).reshape(
            N_SLAB, SLAB_CH, SLAB_NS)

    def blk_out(cc):
        return jnp.einsum("sghp,gk->skpgh", cc.reshape(N_SLAB, SLAB_GROUPS, SSM_GROUP, SSM_STATE), eye).reshape(
            N_SLAB, SLAB_NS, SLAB_CH)

    bs = jnp.concatenate([blk_in(bb_re), blk_in(bb_im)], axis=2).astype(bf16)
    cs = jnp.concatenate([blk_out(c_re), blk_out(-c_im)], axis=1).astype(bf16)
    n = SSM_GROUPS * SSM_STATE
    pw = _power_table(jnp.stack([ab_re.reshape(1, n), ab_im.reshape(1, n)]), -(-seg // SUBLANES))
    return bs, cs, pw


def _power_table(ab, tiles):
    n = ab.shape[2]

    def body(a_ref, o_ref):
        row = lax.broadcasted_iota(jnp.int32, (SUBLANES, n), 0)
        ar, ai = a_ref[0], a_ref[1]
        tr, ti = jnp.broadcast_to(ar, (SUBLANES, n)), jnp.broadcast_to(ai, (SUBLANES, n))
        pr, pi = ar, ai
        for r in range(1, SUBLANES):
            pr, pi = pr * ar - pi * ai, pr * ai + pi * ar
            tr = jnp.where(row == r, pr, tr)
            ti = jnp.where(row == r, pi, ti)
        o_ref[0, 0:SUBLANES, :] = tr
        o_ref[1, 0:SUBLANES, :] = ti

        def step(j, carry):
            cr, ci = carry
            cr, ci = cr * pr - ci * pi, cr * pi + ci * pr
            o_ref[0, _rows8(j), :] = cr
            o_ref[1, _rows8(j), :] = ci
            return cr, ci

        lax.fori_loop(1, tiles, step, (tr, ti))

    return pl.pallas_call(body, name="power_table", out_shape=jax.ShapeDtypeStruct((2, SUBLANES * tiles, n), f32))(ab)


def _s5_table_grads(dbs, dcs, da):
    eye = jnp.eye(SLAB_GROUPS, dtype=f32)
    d6 = dbs.reshape(N_SLAB, SLAB_GROUPS, SSM_GROUP, 2, SLAB_GROUPS, SSM_STATE)
    dbb = jnp.einsum("sghrkp,gk->rsgph", d6, eye).reshape(2, SSM_GROUPS, SSM_STATE, SSM_GROUP)
    c6 = dcs.reshape(N_SLAB, 2, SLAB_GROUPS, SSM_STATE, SLAB_GROUPS, SSM_GROUP)
    dcc = jnp.einsum("srkpgh,gk->rsghp", c6, eye).reshape(2, SSM_GROUPS, SSM_GROUP, SSM_STATE)
    dab = da.transpose(1, 0, 2).reshape(2, SSM_GROUPS, SSM_STATE)
    return dab[0], dab[1], dbb[0], dbb[1], dcc[0], -dcc[1]


SMALL = ["mix_norm_g", "ssm_lambda_re", "ssm_lambda_im", "ssm_log_dt", "ssm_b_re", "ssm_b_im", "ssm_c_re", "ssm_c_im",
         "ssm_d", "hgrn_lb_logits", "hgrn_norm_g", "ffn_norm_g", "conv_b", "final_norm_g"]
SHARDED_SMALL = ["meta_tokens", "conv_w"]
BIG = ["w_in", "ssm_w_glu", "w_ssm_proj", "w_hgrn_proj", "w_out", "w_up", "w_down"]
WEIGHTS = ['meta_tokens', 'mix_norm_g', 'w_in', 'ssm_lambda_re', 'ssm_lambda_im', 'ssm_log_dt', 'ssm_b_re', 'ssm_b_im',
           'ssm_c_re', 'ssm_c_im', 'ssm_d', 'ssm_w_glu', 'w_ssm_proj', 'hgrn_lb_logits', 'hgrn_norm_g', 'w_hgrn_proj',
           'w_out', 'ffn_norm_g', 'w_up', 'conv_w', 'conv_b', 'w_down', 'final_norm_g']


LATER = [k for k in BIG if k != "w_in"]


def _full_weights(gathered, shards, chip):
    Dm = D_MODEL
    g = {k: lax.dynamic_update_slice(gathered[k], shards[k][None], (chip, 0, 0)) for k in gathered}
    full = {}
    for k, v in g.items():
        if k == "w_in":
            full[k] = jnp.roll(v.transpose(1, 0, 2).reshape(Dm, IN_COLS), -Dm, axis=1)
        elif k == "w_up":
            full[k] = v.transpose(1, 0, 2).reshape(Dm, 2 * D_FF)
        else:
            full[k] = v.reshape(-1, Dm)
    return full


def _local_grads(x, tgt, meta, w, full, shards, chip, core):
    B, S, Dm = x.shape
    L = S + N_META
    T = B * L
    h0 = jnp.concatenate([jnp.broadcast_to(meta[None], (B, N_META, Dm)), x], axis=1).reshape(T, Dm)

    lb_all = jax.nn.softmax(w["hgrn_lb_logits"], axis=0)
    lb = lb_all[0:1]
    zoh_out, zoh_vjp = jax.vjp(_zoh, w["ssm_lambda_re"][0], w["ssm_lambda_im"][0], w["ssm_log_dt"][0],
                               w["ssm_b_re"][0], w["ssm_b_im"][0])
    bs, cs, pw = _s5_tables(*zoh_out, w["ssm_c_re"][0], w["ssm_c_im"][0], L // SUBLANES)

    z1 = _rmsnorm_fwd("mix_norm", h0, w["mix_norm_g"])
    p, gathered = _in_proj_gather(z1, full["w_in"], [shards[k] for k in LATER])
    full = {**full, **_full_weights(dict(zip(LATER, gathered)), shards, chip)}
    ya0 = _s5_fwd(p, bs, cs, pw, w["ssm_d"], B, L)
    gl, ya = _glu_proj_fwd(ya0, full["ssm_w_glu"])
    yb = _hgrn_fwd(p, lb, w["hgrn_norm_g"], B, L)
    pa, pb, merged = _proj_merge_fwd(ya, yb, full["w_ssm_proj"], full["w_hgrn_proj"], p)
    h1, z2 = _out_proj_norm(merged, full["w_out"], h0, w["ffn_norm_g"])
    up = _mm_rows("up_proj", z2, full["w_up"], "nn", f32, D_FF // 2)
    ff = _conv_fwd(up, full["conv_w"], w["conv_b"], B, L)
    h2 = _mm_rows("down_proj", ff, full["w_down"], "nn", f32, 1024, res=h1, tk=D_FF // 2)

    h2x = h2.reshape(B, L, Dm)[:, N_META:].reshape(B * S, Dm)
    dh2x, loss, d_final_g = _final_loss(h2x, tgt.reshape(B * S, Dm), w["final_norm_g"].reshape(1, Dm))
    dh2 = jnp.pad(dh2x.reshape(B, S, Dm), ((0, 0), (N_META, 0), (0, 0))).reshape(T, Dm)

    dff = _mm_rows("d_ff", dh2, full["w_down"], "nt", f32, D_FF // 2)
    g_w_down = _mm_wgrad("dw_down", ff, dh2, tn=512)
    dup, dconv = _conv_bwd(up, dff, full["conv_w"], w["conv_b"], B, L)
    g_w_up = _dw_up(z2, dup)
    dh1, d_ffn_g = _dz2_norm(dup, full["w_up"], h1, w["ffn_norm_g"], dh2)

    g_w_out = _mm_wgrad("dw_out", merged, dh1)
    dpa, dpb, dp = _merge_bwd_fused(dh1, full["w_out"], p, pa, pb)
    dgl, dya0_direct = _glu_bwd_fused(dpa, full["w_ssm_proj"], ya0, gl)
    g_w_ssm_proj = _mm_wgrad("dw_ssm_proj", ya, dpa)
    dyb = _mm_rows("d_yb", dpb, full["w_hgrn_proj"], "nt", f32, 1024)
    g_w_hgrn_proj = _mm_wgrad("dw_hgrn_proj", yb, dpb)
    dp, d_lb, d_hgrn_g = _hgrn_bwd(p, dyb, dp, lb, w["hgrn_norm_g"], B, L)
    dya0 = _mm_rows("d_ya0", dgl, full["ssm_w_glu"], "nt", f32, 1024, res=dya0_direct)
    g_w_glu = _mm_wgrad("dw_glu", ya0, dgl)
    parts = {
        "ssm_w_glu": g_w_glu.reshape(N_CHIPS, Dm // N_CHIPS, Dm), "w_ssm_proj": g_w_ssm_proj.reshape(N_CHIPS, Dm // N_CHIPS, Dm),
        "w_hgrn_proj": g_w_hgrn_proj.reshape(N_CHIPS, Dm // N_CHIPS, Dm), "w_out": g_w_out.reshape(N_CHIPS, Dm // N_CHIPS, Dm),
        "w_up": g_w_up, "w_down": g_w_down.reshape(N_CHIPS, D_FF // N_CHIPS, Dm),
    }
    got = _sibling_halves([parts[k] for k in LATER])
    sums = {k: _add_own_half("add_half_" + k, parts[k], gt, core) for k, gt in zip(LATER, got)}
    (dp, dbs, dcs, da, d_skip), slots_later = _s5_bwd(p, dya0, dp, bs, cs, pw, w["ssm_d"], B, L, [sums[k] for k in LATER])
    slots = dict(zip(LATER, slots_later))
    g_w_in = _dw_in(z1, dp)
    dh0, d_mix_g = _dz1_norm(dp, full["w_in"], h0, w["mix_norm_g"], dh1)

    dh0 = dh0.reshape(B, L, Dm)
    grad_x = dh0[:, N_META:]
    d_meta = _meta_grad(dh0[:, :N_META])

    d_ab_re, d_ab_im, d_bb_re, d_bb_im, d_c_re, d_c_im = _s5_table_grads(dbs, dcs, da)
    d_lr, d_li, d_log_dt, d_b_re, d_b_im = zoh_vjp((d_ab_re, d_ab_im, d_bb_re, d_bb_im))
    sm0, sm1 = lb_all[0:1], lb_all[1:2]
    d_logits = jnp.concatenate([sm0 * (1.0 - sm0) * d_lb, -sm0 * sm1 * d_lb], axis=0)
    small = {
        "meta_tokens": d_meta, "mix_norm_g": d_mix_g, "ssm_lambda_re": d_lr[None], "ssm_lambda_im": d_li[None],
        "ssm_log_dt": d_log_dt[None], "ssm_b_re": d_b_re[None], "ssm_b_im": d_b_im[None], "ssm_c_re": d_c_re[None],
        "ssm_c_im": d_c_im[None], "ssm_d": d_skip, "hgrn_lb_logits": d_logits, "hgrn_norm_g": d_hgrn_g,
        "ffn_norm_g": d_ffn_g, "conv_w": dconv[:, 0:3, :].transpose(1, 0, 2).reshape(3, 2 * D_FF),
        "conv_b": dconv[:, 3, :].reshape(1, 2 * D_FF), "final_norm_g": d_final_g.reshape(Dm),
    }
    sums["w_in"] = _add_own_half_w_in(g_w_in, _sibling_halves([g_w_in], "sibling_halves_w_in")[0], core)
    slots["w_in"] = _chip_exchange([sums["w_in"]])[0]
    return loss, grad_x, sums, slots, small


PACK_ROWS = 256


def _pack(parts):
    flat = jnp.concatenate([parts[k].reshape(-1) for k in parts])
    n = flat.shape[0]
    rows = -(-n // (PACK_ROWS * LANES)) * PACK_ROWS
    flat = jnp.pad(flat, (0, rows * LANES - n))
    return flat.reshape(rows, LANES)


def _unpack(packed, like):
    flat = packed.reshape(-1)
    out, o = {}, 0
    for k, ref in like.items():
        n = math.prod(ref.shape)
        out[k] = flat[o:o + n].reshape(ref.shape)
        o += n
    return out


def kernel(x, meta_tokens, mix_norm_g, w_in, ssm_lambda_re, ssm_lambda_im, ssm_log_dt, ssm_b_re, ssm_b_im, ssm_c_re, ssm_c_im, ssm_d, ssm_w_glu, w_ssm_proj, hgrn_lb_logits, hgrn_norm_g, w_hgrn_proj, w_out, ffn_norm_g, w_up, conv_w, conv_b, w_down, final_norm_g, loss_target, m_meta_tokens, m_mix_norm_g, m_w_in, m_ssm_lambda_re, m_ssm_lambda_im, m_ssm_log_dt, m_ssm_b_re, m_ssm_b_im, m_ssm_c_re, m_ssm_c_im, m_ssm_d, m_ssm_w_glu, m_w_ssm_proj, m_hgrn_lb_logits, m_hgrn_norm_g, m_w_hgrn_proj, m_w_out, m_ffn_norm_g, m_w_up, m_conv_w, m_conv_b, m_w_down, m_final_norm_g, v_meta_tokens, v_mix_norm_g, v_w_in, v_ssm_lambda_re, v_ssm_lambda_im, v_ssm_log_dt, v_ssm_b_re, v_ssm_b_im, v_ssm_c_re, v_ssm_c_im, v_ssm_d, v_ssm_w_glu, v_w_ssm_proj, v_hgrn_lb_logits, v_hgrn_norm_g, v_w_hgrn_proj, v_w_out, v_ffn_norm_g, v_w_up, v_conv_w, v_conv_b, v_w_down, v_final_norm_g):
    args = dict(locals())
    w = {k: args[k] for k in WEIGHTS}
    mom = {k: args["m_" + k] for k in WEIGHTS}
    var = {k: args["v_" + k] for k in WEIGHTS}
    Dm = D_MODEL
    cx, cy, cc = lax.axis_index("x"), lax.axis_index("y"), lax.axis_index("c")
    chip = 2 * cx + cy

    shards = {k: w[k][0].astype(bf16) for k in BIG}
    g_meta, g_cw = _allgather_chips([w["meta_tokens"], w["conv_w"][0]])
    full = _full_weights({"w_in": _allgather_split([shards["w_in"]])[0]}, shards, chip)
    full["conv_w"] = g_cw.transpose(1, 0, 2).reshape(3, 2 * D_FF)
    meta_full = g_meta.transpose(1, 0, 2).reshape(N_META, Dm)

    core = cc.reshape(1).astype(jnp.int32)
    loss_part, grad_x, sums, slots, small = _local_grads(x, loss_target, meta_full, w, full, shards, chip, core)

    where = jnp.stack([chip, cc]).astype(jnp.int32)
    fulls = [_sum_chips("sum_chips_" + k, slots[k], sums[k], where) for k in BIG]
    g_big = dict(zip(BIG, _sibling_join(fulls)))

    small_all = dict(small)
    small_all["loss"] = loss_part[0, 0:1]
    packed = _pack(small_all)
    slots_dev = lax.dynamic_update_slice(_allgather_devices(packed), packed[None], (2 * chip + cc, 0, 0))
    reduced = _unpack(_sum_slots("sum_devices", slots_dev), small_all)
    loss = reduced.pop("loss")[0]
    mcols = Dm // N_CHIPS
    ccols = 2 * D_FF // N_CHIPS
    grads = {k: reduced[k] for k in SMALL}
    grads["meta_tokens"] = lax.dynamic_slice(reduced["meta_tokens"], (0, chip * mcols), (N_META, mcols))
    grads["conv_w"] = lax.dynamic_slice(reduced["conv_w"], (0, chip * ccols), (3, ccols))[None]
    for k in BIG:
        grads[k] = g_big[k][None]

    delta, new_m, new_v = {}, {}, {}
    for k in BIG:
        shp = w[k].shape
        d, nm, nv = _adamw("adamw_" + k, w[k][0], grads[k][0], mom[k][0], var[k][0])
        delta[k], new_m[k], new_v[k] = d.reshape(shp), nm.reshape(shp), nv.reshape(shp)
    rest = SMALL + SHARDED_SMALL

    def flat2(a):
        return a.reshape(-1, a.shape[-1])

    outs = _adamw_many(*[[flat2(t[k]) for k in rest] for t in (w, grads, mom, var)])
    n = len(rest)
    for j, dst in enumerate((delta, new_m, new_v)):
        dst.update({k: o.reshape(w[k].shape) for k, o in zip(rest, outs[j * n:(j + 1) * n])})

    return (loss, grad_x, *[grads[k].reshape(w[k].shape) for k in WEIGHTS], *[delta[k] for k in WEIGHTS],
            *[new_m[k] for k in WEIGHTS], *[new_v[k] for k in WEIGHTS])
```

```python
import functools
import math

import jax
import jax.numpy as jnp
from jax import lax
from jax.experimental import pallas as pl
from jax.experimental.pallas import tpu as pltpu

f32 = jnp.float32
bf16 = jnp.bfloat16

D_MODEL = 1024
N_META = 16
SSM_GROUP = 16
SSM_GROUPS = 64
SSM_STATE = 64
SLAB_GROUPS = 8
N_SLAB = SSM_GROUPS // SLAB_GROUPS
SLAB_CH = SLAB_GROUPS * SSM_GROUP
SLAB_NS = SLAB_GROUPS * SSM_STATE
HEADS = 8
HEAD_DIM = 128
CHUNK = 16
D_FF = 2816
IN_COLS = 7168
EPS = 1e-6
SUBLANES = 8
LANES = 128
N_CHIPS = 4
N_DEV = 8
ADAM_LR, ADAM_B1, ADAM_B2, ADAM_EPS, ADAM_WD, ADAM_STEP = 0.001, 0.9, 0.999, 1e-08, 0.01, 10
MESH = pl.DeviceIdType.MESH
ANY = pl.BlockSpec(memory_space=pl.ANY)

SEG_Q, SEG_F, SEG_I, SEG_OG, SEG_GA, SEG_GB, SEG_U = range(7)
N_SEG = 7


def _tile(n, target, mult=SUBLANES):
    best = None
    for d in range(mult, min(n, target) + 1, mult):
        if n % d == 0:
            best = d
    return n if best is None else best


def _params(*sem):
    return pltpu.CompilerParams(dimension_semantics=sem)


def _sigmoid(x):
    return 1.0 / (1.0 + jnp.exp(-x))


_DIMS = {"nn": (((1,), (0,)), ((), ())), "nt": (((1,), (1,)), ((), ())), "tn": (((0,), (0,)), ((), ()))}


def _mm(name, a, b, dims, grid, a_spec, b_spec, out_shape, out_spec, acc_shape, res=None, res_spec=None):
    nk = grid[2]
    dn = _DIMS[dims]

    def body(*refs):
        if res is None:
            a_ref, b_ref, o_ref, acc = refs
        else:
            a_ref, b_ref, r_ref, o_ref, acc = refs
        k = pl.program_id(2)

        @pl.when(k == 0)
        def _():
            acc[...] = jnp.zeros_like(acc)

        acc[...] += lax.dot_general(a_ref[...].astype(bf16), b_ref[...].astype(bf16), dn, preferred_element_type=f32)

        @pl.when(k == nk - 1)
        def _():
            r = acc[...]
            if res is not None:
                r = r + r_ref[...]
            o_ref[...] = r.astype(o_ref.dtype)

    ins = [a, b] + ([] if res is None else [res])
    specs = [a_spec, b_spec] + ([] if res is None else [res_spec])
    return pl.pallas_call(
        body, name=name, grid=grid, in_specs=specs, out_specs=out_spec, out_shape=out_shape,
        scratch_shapes=[pltpu.VMEM(acc_shape, f32)],
        compiler_params=_params("parallel", "parallel", "arbitrary"),
    )(*ins)


def _mm_rows(name, a, w, dims, out_dtype, tn, res=None, tk=None):
    T, K = a.shape
    N = w.shape[1] if dims == "nn" else w.shape[0]
    tm = _tile(T, 1032)
    tk = K if tk is None else tk
    grid = (T // tm, N // tn, K // tk)
    a_spec = pl.BlockSpec((tm, tk), lambda i, j, k: (i, k))
    if dims == "nn":
        b_spec = pl.BlockSpec((tk, tn), lambda i, j, k: (k, j))
    else:
        b_spec = pl.BlockSpec((tn, tk), lambda i, j, k: (j, k))
    o_spec = pl.BlockSpec((tm, tn), lambda i, j, k: (i, j))
    return _mm(name, a, w, dims, grid, a_spec, b_spec, jax.ShapeDtypeStruct((T, N), out_dtype), o_spec, (tm, tn),
               res=res, res_spec=None if res is None else o_spec)


def _mm_fused(name, pairs, dims, extras, epilogue, outs, rows=()):
    T, K = pairs[0][0].shape
    N = pairs[0][1].shape[1] if dims == "nn" else pairs[0][1].shape[0]
    tm = _tile(T, 344)
    tn = N
    grid = (T // tm, N // tn)
    npair, nex = len(pairs), len(extras) + len(rows)
    dn = _DIMS[dims]

    def body(*refs):
        ab = refs[:2 * npair]
        ex = refs[2 * npair:2 * npair + nex]
        o_refs = refs[2 * npair + nex:]
        accs = [lax.dot_general(ab[2 * q][...].astype(bf16), ab[2 * q + 1][...].astype(bf16), dn, preferred_element_type=f32)
                for q in range(npair)]
        vals = epilogue(accs, [e[...] for e in ex])
        for o_ref, v in zip(o_refs, vals):
            if isinstance(v, (list, tuple)):
                for s_, vs in enumerate(v):
                    o_ref[s_] = vs.astype(o_ref.dtype)
            else:
                o_ref[...] = v.astype(o_ref.dtype)

    ins, specs = [], []
    for a, w in pairs:
        ins += [a, w]
        specs.append(pl.BlockSpec((tm, K), lambda i, j: (i, 0)))
        specs.append(pl.BlockSpec((K, tn), lambda i, j: (0, j)) if dims == "nn" else pl.BlockSpec((tn, K), lambda i, j: (j, 0)))
    for arr, off in extras:
        ins.append(arr)
        specs.append(pl.BlockSpec((tm, tn), lambda i, j, off=off: (i, off + j)))
    for arr in rows:
        ins.append(arr)
        specs.append(pl.BlockSpec((1, tn), lambda i, j: (0, j)))
    shapes, ospecs = [], []
    for o in outs:
        if isinstance(o, tuple):
            dt, nseg, total, blk = o
            shapes.append(jax.ShapeDtypeStruct((total, T, N), dt))
            ospecs.append(pl.BlockSpec((nseg, tm, tn), lambda i, j, blk=blk: (blk, i, j)))
        else:
            shapes.append(jax.ShapeDtypeStruct((T, N), o))
            ospecs.append(pl.BlockSpec((tm, tn), lambda i, j: (i, j)))
    return pl.pallas_call(body, name=name, grid=grid, in_specs=specs, out_specs=ospecs, out_shape=shapes,
                          compiler_params=_params("parallel", "parallel"))(*ins)


def _glu_proj_fwd(ya0, w_glu):
    def epi(accs, tiles):
        return accs[0], tiles[0] * _sigmoid(accs[0])

    return _mm_fused("glu_proj", [(ya0, w_glu)], "nn", [(ya0, 0)], epi, [f32, bf16])


def _proj_merge_fwd(ya, yb, w_sp, w_hp, p):
    def epi(accs, tiles):
        return accs[0], accs[1], _sigmoid(tiles[0]) * accs[0] + _sigmoid(tiles[1]) * accs[1]

    return _mm_fused("proj_merge", [(ya, w_sp), (yb, w_hp)], "nn", [(p, SEG_GA), (p, SEG_GB)], epi, [f32, f32, bf16])


def _merge_bwd_fused(dh1, w_out, p, pa, pb):
    def epi(accs, tiles):
        d = accs[0]
        sa, sb = _sigmoid(tiles[0]), _sigmoid(tiles[1])
        return d * sa, d * sb, [d * tiles[2] * sa * (1.0 - sa), d * tiles[3] * sb * (1.0 - sb)]

    return _mm_fused("d_merged", [(dh1, w_out)], "nt", [(p, SEG_GA), (p, SEG_GB), (pa, 0), (pb, 0)], epi,
                     [bf16, bf16, (bf16, 2, N_SEG, SEG_GA // 2)])


def _out_proj_norm(merged, w_out, h0, g):
    def epi(accs, tiles):
        h1 = tiles[0] + accs[0]
        r = lax.rsqrt(jnp.mean(h1 * h1, axis=-1, keepdims=True) + EPS)
        return h1, h1 * r * tiles[1]

    return _mm_fused("out_proj", [(merged, w_out)], "nn", [(h0, 0)], epi, [f32, bf16], rows=[g])


def _mm_rmsnorm_bwd(name, a, b, grid, a_spec, b_spec, x, g, dres):
    T, Dm = x.shape
    tm = T // grid[0]
    nk = grid[2]

    def body(a_ref, b_ref, x_ref, g_ref, dres_ref, dx_ref, dg_ref, acc):
        i, k = pl.program_id(0), pl.program_id(2)

        @pl.when(k == 0)
        def _():
            acc[...] = jnp.zeros_like(acc)

        @pl.when((i == 0) & (k == 0))
        def _():
            dg_ref[...] = jnp.zeros_like(dg_ref)

        acc[...] += lax.dot_general(a_ref[...].astype(bf16), b_ref[...].astype(bf16), _DIMS["nt"], preferred_element_type=f32)

        @pl.when(k == nk - 1)
        def _():
            xv = x_ref[...]
            r = lax.rsqrt(jnp.mean(xv * xv, axis=-1, keepdims=True) + EPS)
            xn = xv * r
            dzv = acc[...]
            dzg = dzv * g_ref[...]
            dx_ref[...] = dres_ref[...] + r * (dzg - xn * jnp.mean(dzg * xn, axis=-1, keepdims=True))
            dg_ref[...] += jnp.sum(dzv * xn, axis=0, keepdims=True)

    row = pl.BlockSpec((tm, Dm), lambda i, j, k: (i, 0))
    par = pl.BlockSpec((1, Dm), lambda i, j, k: (0, 0))
    return pl.pallas_call(
        body, name=name, grid=grid, in_specs=[a_spec, b_spec, row, par, row], out_specs=[row, par],
        out_shape=[jax.ShapeDtypeStruct((T, Dm), f32), jax.ShapeDtypeStruct((1, Dm), f32)],
        scratch_shapes=[pltpu.VMEM((tm, Dm), f32)],
        compiler_params=_params("arbitrary", "arbitrary", "arbitrary"),
    )(a, b, x, g, dres)


def _glu_bwd_fused(dpa, w_sp, ya0, gl):
    def epi(accs, tiles):
        d = accs[0]
        s = _sigmoid(tiles[1])
        return d * tiles[0] * s * (1.0 - s), d * s

    return _mm_fused("d_ya", [(dpa, w_sp)], "nt", [(ya0, 0), (gl, 0)], epi, [bf16, f32])


def _mm_wgrad(name, a, g, tn=None):
    T, K = a.shape
    N = g.shape[1]
    tk = _tile(T, 688)
    tn = N if tn is None else tn
    grid = (1, N // tn, T // tk)
    a_spec = pl.BlockSpec((tk, K), lambda i, j, k: (k, 0))
    g_spec = pl.BlockSpec((tk, tn), lambda i, j, k: (k, j))
    o_spec = pl.BlockSpec((K, tn), lambda i, j, k: (0, j))
    return _mm(name, a, g, "tn", grid, a_spec, g_spec, jax.ShapeDtypeStruct((K, N), f32), o_spec, (K, tn))


def _rmsnorm_fwd(name, x, g):
    T, Dm = x.shape
    tr = _tile(T, 688)

    def body(x_ref, g_ref, z_ref):
        xv = x_ref[...]
        r = lax.rsqrt(jnp.mean(xv * xv, axis=-1, keepdims=True) + EPS)
        z_ref[...] = (xv * r * g_ref[...]).astype(z_ref.dtype)

    return pl.pallas_call(
        body, name=name, grid=(T // tr,),
        in_specs=[pl.BlockSpec((tr, Dm), lambda i: (i, 0)), pl.BlockSpec((1, Dm), lambda i: (0, 0))],
        out_specs=pl.BlockSpec((tr, Dm), lambda i: (i, 0)),
        out_shape=jax.ShapeDtypeStruct((T, Dm), bf16), compiler_params=_params("parallel"),
    )(x, g)


def _rmsnorm_bwd(name, x, g, dz, dres):
    T, Dm = x.shape
    tr = _tile(T, 688)

    def body(x_ref, g_ref, dz_ref, dres_ref, dx_ref, dg_ref):
        xv = x_ref[...]
        r = lax.rsqrt(jnp.mean(xv * xv, axis=-1, keepdims=True) + EPS)
        xn = xv * r
        dzv = dz_ref[...]
        dzg = dzv * g_ref[...]
        dx_ref[...] = dres_ref[...] + r * (dzg - xn * jnp.mean(dzg * xn, axis=-1, keepdims=True))

        @pl.when(pl.program_id(0) == 0)
        def _():
            dg_ref[...] = jnp.zeros_like(dg_ref)

        dg_ref[...] += jnp.sum(dzv * xn, axis=0, keepdims=True)

    row = pl.BlockSpec((tr, Dm), lambda i: (i, 0))
    par = pl.BlockSpec((1, Dm), lambda i: (0, 0))
    return pl.pallas_call(
        body, name=name, grid=(T // tr,), in_specs=[row, par, row, row], out_specs=[row, par],
        out_shape=[jax.ShapeDtypeStruct((T, Dm), f32), jax.ShapeDtypeStruct((1, Dm), f32)],
        compiler_params=_params("arbitrary"),
    )(x, g, dz, dres)


def _glu_fwd(ya0, gl):
    T, Dm = ya0.shape
    tr = _tile(T, 688)

    def body(y_ref, g_ref, o_ref):
        o_ref[...] = (y_ref[...] * _sigmoid(g_ref[...])).astype(o_ref.dtype)

    row = pl.BlockSpec((tr, Dm), lambda i: (i, 0))
    return pl.pallas_call(body, name="glu_fwd", grid=(T // tr,), in_specs=[row, row], out_specs=row,
                          out_shape=jax.ShapeDtypeStruct((T, Dm), bf16), compiler_params=_params("parallel"))(ya0, gl)


def _glu_bwd(dya, ya0, gl):
    T, Dm = ya0.shape
    tr = _tile(T, 688)

    def body(d_ref, y_ref, g_ref, dg_ref, dy_ref):
        s = _sigmoid(g_ref[...])
        d = d_ref[...]
        dg_ref[...] = (d * y_ref[...] * s * (1.0 - s)).astype(dg_ref.dtype)
        dy_ref[...] = d * s

    row = pl.BlockSpec((tr, Dm), lambda i: (i, 0))
    return pl.pallas_call(body, name="glu_bwd", grid=(T // tr,), in_specs=[row, row, row], out_specs=[row, row],
                          out_shape=[jax.ShapeDtypeStruct((T, Dm), bf16), jax.ShapeDtypeStruct((T, Dm), f32)],
                          compiler_params=_params("parallel"))(dya, ya0, gl)


def _merge_fwd(p, pa, pb):
    T, Dm = pa.shape
    tr = _tile(T, 688)

    def body(ga_ref, gb_ref, pa_ref, pb_ref, o_ref):
        o_ref[...] = (_sigmoid(ga_ref[...]) * pa_ref[...] + _sigmoid(gb_ref[...]) * pb_ref[...]).astype(o_ref.dtype)

    row = pl.BlockSpec((tr, Dm), lambda i: (i, 0))
    return pl.pallas_call(
        body, name="merge_fwd", grid=(T // tr,),
        in_specs=[pl.BlockSpec((tr, Dm), lambda i: (i, SEG_GA)), pl.BlockSpec((tr, Dm), lambda i: (i, SEG_GB)), row, row],
        out_specs=row, out_shape=jax.ShapeDtypeStruct((T, Dm), bf16), compiler_params=_params("parallel"),
    )(p, p, pa, pb)


def _merge_bwd(dm, p, pa, pb):
    T, Dm = pa.shape
    tr = _tile(T, 688)

    def body(dm_ref, ga_ref, gb_ref, pa_ref, pb_ref, dpa_ref, dpb_ref, dp_ref):
        d = dm_ref[...]
        sa = _sigmoid(ga_ref[...])
        sb = _sigmoid(gb_ref[...])
        dpa_ref[...] = (d * sa).astype(dpa_ref.dtype)
        dpb_ref[...] = (d * sb).astype(dpb_ref.dtype)
        dp_ref[0] = (d * pa_ref[...] * sa * (1.0 - sa)).astype(dp_ref.dtype)
        dp_ref[1] = (d * pb_ref[...] * sb * (1.0 - sb)).astype(dp_ref.dtype)

    row = pl.BlockSpec((tr, Dm), lambda i: (i, 0))
    return pl.pallas_call(
        body, name="merge_bwd", grid=(T // tr,),
        in_specs=[row, pl.BlockSpec((tr, Dm), lambda i: (i, SEG_GA)), pl.BlockSpec((tr, Dm), lambda i: (i, SEG_GB)), row, row],
        out_specs=[row, row, pl.BlockSpec((2, tr, Dm), lambda i: (SEG_GA // 2, i, 0))],
        out_shape=[jax.ShapeDtypeStruct((T, Dm), bf16), jax.ShapeDtypeStruct((T, Dm), bf16),
                   jax.ShapeDtypeStruct((N_SEG, T, Dm), bf16)],
        compiler_params=_params("parallel"),
    )(dm, p, p, pa, pb)


def _final_loss(h2x, tgt, g):
    T, Dm = h2x.shape
    tr = _tile(T, 512)

    def body(h_ref, t_ref, g_ref, dh_ref, loss_ref, dg_ref):
        hv = h_ref[...]
        r = lax.rsqrt(jnp.mean(hv * hv, axis=-1, keepdims=True) + EPS)
        xn = hv * r
        gv = g_ref[...]
        err = xn * gv - t_ref[...]
        dy = err * (1.0 / Dm)
        dyg = dy * gv
        dh_ref[...] = r * (dyg - xn * jnp.mean(dyg * xn, axis=-1, keepdims=True))

        @pl.when(pl.program_id(0) == 0)
        def _():
            dg_ref[...] = jnp.zeros_like(dg_ref)
            loss_ref[...] = jnp.zeros_like(loss_ref)

        dg_ref[...] += jnp.sum(dy * xn, axis=0, keepdims=True)
        loss_ref[...] += jnp.sum(err * err) * (0.5 / Dm)

    row = pl.BlockSpec((tr, Dm), lambda i: (i, 0))
    par = pl.BlockSpec((1, Dm), lambda i: (0, 0))
    return pl.pallas_call(
        body, name="final_loss", grid=(T // tr,), in_specs=[row, row, par],
        out_specs=[row, pl.BlockSpec((1, LANES), lambda i: (0, 0)), par],
        out_shape=[jax.ShapeDtypeStruct((T, Dm), f32), jax.ShapeDtypeStruct((1, LANES), f32), jax.ShapeDtypeStruct((1, Dm), f32)],
        compiler_params=_params("arbitrary"),
    )(h2x, tgt, g)


def _meta_grad(dh0_meta):
    B = dh0_meta.shape[0]

    def body(d_ref, o_ref):
        acc = d_ref[0]
        for b in range(1, B):
            acc = acc + d_ref[b]
        o_ref[...] = acc

    return pl.pallas_call(body, name="meta_grad", out_shape=jax.ShapeDtypeStruct(dh0_meta.shape[1:], f32))(dh0_meta)


def _shift_down(x, k, row):
    return jnp.where(row >= k, pltpu.roll(x, k, 0), 0.0)


def _shift_up(x, k, row):
    n = x.shape[0]
    return jnp.where(row < n - k, pltpu.roll(x, n - k, 0), 0.0)


def _conv_fwd(up, conv_w, conv_b, B, L):
    tc = 256
    nt = D_FF // tc

    def body(xa_ref, xb_ref, wa_ref, wb_ref, ba_ref, bb_ref, o_ref):
        row = lax.broadcasted_iota(jnp.int32, (L, tc), 0)

        def conv(x_ref, w_ref, b_ref):
            x = x_ref[...]
            return (b_ref[...] + w_ref[0:1, :] * _shift_down(x, 2, row) + w_ref[1:2, :] * _shift_down(x, 1, row)
                    + w_ref[2:3, :] * x)

        a = conv(xa_ref, wa_ref, ba_ref)
        b = conv(xb_ref, wb_ref, bb_ref)
        o_ref[...] = (a * _sigmoid(a) * b).astype(o_ref.dtype)

    return pl.pallas_call(
        body, name="conv_fwd", grid=(B, nt),
        in_specs=[pl.BlockSpec((L, tc), lambda b, j: (b, j)), pl.BlockSpec((L, tc), lambda b, j: (b, j + nt)),
                  pl.BlockSpec((3, tc), lambda b, j: (0, j)), pl.BlockSpec((3, tc), lambda b, j: (0, j + nt)),
                  pl.BlockSpec((1, tc), lambda b, j: (0, j)), pl.BlockSpec((1, tc), lambda b, j: (0, j + nt))],
        out_specs=pl.BlockSpec((L, tc), lambda b, j: (b, j)),
        out_shape=jax.ShapeDtypeStruct((B * L, D_FF), bf16), compiler_params=_params("parallel", "parallel"),
    )(up, up, conv_w, conv_w, conv_b, conv_b)


def _conv_bwd(up, dff, conv_w, conv_b, B, L):
    tc = 256
    nt = D_FF // tc

    def body(xa_ref, xb_ref, d_ref, wa_ref, wb_ref, ba_ref, bb_ref, dup_ref, dw_ref):
        row = lax.broadcasted_iota(jnp.int32, (L, tc), 0)
        xs, pre = [], []
        for x_ref, w_ref, b_ref in ((xa_ref, wa_ref, ba_ref), (xb_ref, wb_ref, bb_ref)):
            x = x_ref[...]
            x1 = _shift_down(x, 1, row)
            x2 = _shift_down(x, 2, row)
            xs.append((x, x1, x2))
            pre.append(b_ref[...] + w_ref[0:1, :] * x2 + w_ref[1:2, :] * x1 + w_ref[2:3, :] * x)
        a, b = pre
        s = _sigmoid(a)
        d = d_ref[...]
        grads = (d * b * s * (1.0 + a * (1.0 - s)), d * a * s)

        @pl.when(pl.program_id(1) == 0)
        def _():
            dw_ref[...] = jnp.zeros_like(dw_ref)

        for h, (gr, (x, x1, x2), w_ref) in enumerate(zip(grads, xs, (wa_ref, wb_ref))):
            dup_ref[h] = (w_ref[2:3, :] * gr + w_ref[1:2, :] * _shift_up(gr, 1, row)
                          + w_ref[0:1, :] * _shift_up(gr, 2, row)).astype(dup_ref.dtype)
            dw_ref[h, 0:1, :] += jnp.sum(gr * x2, axis=0, keepdims=True)
            dw_ref[h, 1:2, :] += jnp.sum(gr * x1, axis=0, keepdims=True)
            dw_ref[h, 2:3, :] += jnp.sum(gr * x, axis=0, keepdims=True)
            dw_ref[h, 3:4, :] += jnp.sum(gr, axis=0, keepdims=True)

    return pl.pallas_call(
        body, name="conv_bwd", grid=(nt, B),
        in_specs=[pl.BlockSpec((L, tc), lambda j, b: (b, j)), pl.BlockSpec((L, tc), lambda j, b: (b, j + nt)),
                  pl.BlockSpec((L, tc), lambda j, b: (b, j)),
                  pl.BlockSpec((3, tc), lambda j, b: (0, j)), pl.BlockSpec((3, tc), lambda j, b: (0, j + nt)),
                  pl.BlockSpec((1, tc), lambda j, b: (0, j)), pl.BlockSpec((1, tc), lambda j, b: (0, j + nt))],
        out_specs=[pl.BlockSpec((2, L, tc), lambda j, b: (0, b, j)), pl.BlockSpec((2, SUBLANES, tc), lambda j, b: (0, 0, j))],
        out_shape=[jax.ShapeDtypeStruct((2, B * L, D_FF), bf16), jax.ShapeDtypeStruct((2, SUBLANES, D_FF), f32)],
        compiler_params=_params("parallel", "arbitrary"),
    )(up, up, dff, conv_w, conv_w, conv_b, conv_b)


CONV_ROWS = 2 * SUBLANES


def _rows16(i):
    return pl.ds(pl.multiple_of(i * CONV_ROWS, CONV_ROWS), CONV_ROWS)


def _conv_taps(x_ref, i, row):
    x = x_ref[_rows16(i), :]
    live = jnp.where(i > 0, 1.0, 0.0)
    r0 = jnp.maximum(i * CONV_ROWS, 2)
    p1 = x_ref[pl.ds(r0 - 1, 1), :] * live
    p2 = x_ref[pl.ds(r0 - 2, 1), :] * live
    x1 = jnp.where(row == 0, p1, pltpu.roll(x, 1, 0))
    x2 = jnp.where(row == 0, p2, jnp.where(row == 1, p1, pltpu.roll(x, 2, 0)))
    return x, x1, x2


def _conv_bwd(up, dff, conv_w, conv_b, B, L):
    tc = 256
    nt = D_FF // tc
    n = L // CONV_ROWS

    def body(xa_ref, xb_ref, d_ref, wa_ref, wb_ref, ba_ref, bb_ref, dup_ref, dw_ref, ga_ref, gb_ref):
        row = lax.broadcasted_iota(jnp.int32, (CONV_ROWS, tc), 0)

        @pl.when(pl.program_id(1) == 0)
        def _():
            dw_ref[...] = jnp.zeros_like(dw_ref)

        zero_tail = jnp.zeros((CONV_ROWS, tc), f32)
        ga_ref[L:L + CONV_ROWS, :] = zero_tail
        gb_ref[L:L + CONV_ROWS, :] = zero_tail

        def fold(v):
            return v[0:SUBLANES, :] + v[SUBLANES:CONV_ROWS, :]

        def step(i, acc):
            taps_a = _conv_taps(xa_ref, i, row)
            taps_b = _conv_taps(xb_ref, i, row)
            a = ba_ref[...] + wa_ref[0:1, :] * taps_a[2] + wa_ref[1:2, :] * taps_a[1] + wa_ref[2:3, :] * taps_a[0]
            b = bb_ref[...] + wb_ref[0:1, :] * taps_b[2] + wb_ref[1:2, :] * taps_b[1] + wb_ref[2:3, :] * taps_b[0]
            s = _sigmoid(a)
            d = d_ref[_rows16(i), :]
            g_a = d * b * s * (1.0 + a * (1.0 - s))
            g_b = d * a * s
            ga_ref[_rows16(i), :] = g_a
            gb_ref[_rows16(i), :] = g_b
            new = []
            for g, (x, x1, x2) in ((g_a, taps_a), (g_b, taps_b)):
                new += [fold(g * x2), fold(g * x1), fold(g * x), fold(g)]
            return tuple(o + v for o, v in zip(acc, new))

        z = jnp.zeros((SUBLANES, tc), f32)
        acc = _repeat_loop(n, step, (z,) * 8)
        for h in range(2):
            for t in range(4):
                dw_ref[h, t:t + 1, :] += jnp.sum(acc[4 * h + t], axis=0, keepdims=True)

        def back(i, c):
            for h, (g_ref, w_ref) in enumerate(((ga_ref, wa_ref), (gb_ref, wb_ref))):
                g = g_ref[_rows16(i), :]
                n1 = g_ref[pl.ds(i * CONV_ROWS + CONV_ROWS, 1), :]
                n2 = g_ref[pl.ds(i * CONV_ROWS + CONV_ROWS + 1, 1), :]
                u1 = jnp.where(row == CONV_ROWS - 1, n1, pltpu.roll(g, CONV_ROWS - 1, 0))
                u2 = jnp.where(row == CONV_ROWS - 1, n2, jnp.where(row == CONV_ROWS - 2, n1, pltpu.roll(g, CONV_ROWS - 2, 0)))
                dup_ref[h, _rows16(i), :] = (w_ref[2:3, :] * g + w_ref[1:2, :] * u1 + w_ref[0:1, :] * u2).astype(dup_ref.dtype)
            return c

        _repeat_loop(n, back, 0)

    return pl.pallas_call(
        body, name="conv_bwd", grid=(nt, B),
        in_specs=[pl.BlockSpec((L, tc), lambda j, b: (b, j)), pl.BlockSpec((L, tc), lambda j, b: (b, j + nt)),
                  pl.BlockSpec((L, tc), lambda j, b: (b, j)),
                  pl.BlockSpec((3, tc), lambda j, b: (0, j)), pl.BlockSpec((3, tc), lambda j, b: (0, j + nt)),
                  pl.BlockSpec((1, tc), lambda j, b: (0, j)), pl.BlockSpec((1, tc), lambda j, b: (0, j + nt))],
        out_specs=[pl.BlockSpec((2, L, tc), lambda j, b: (0, b, j)), pl.BlockSpec((2, SUBLANES, tc), lambda j, b: (0, 0, j))],
        out_shape=[jax.ShapeDtypeStruct((2, B * L, D_FF), bf16), jax.ShapeDtypeStruct((2, SUBLANES, D_FF), f32)],
        scratch_shapes=[pltpu.VMEM((L + CONV_ROWS, tc), f32), pltpu.VMEM((L + CONV_ROWS, tc), f32)],
        compiler_params=_params("parallel", "arbitrary"),
    )(up, up, dff, conv_w, conv_w, conv_b, conv_b)


GELU_C = math.sqrt(2.0 / math.pi)
GELU_A = 0.044715


def _gelu(x):
    return 0.5 * x * (1.0 + jnp.tanh(GELU_C * (x + GELU_A * x * x * x)))


def _gelu_grad(x):
    t = jnp.tanh(GELU_C * (x + GELU_A * x * x * x))
    return 0.5 * (1.0 + t) + 0.5 * x * (1.0 - t * t) * GELU_C * (1.0 + 3.0 * GELU_A * x * x)


def _cmul_add(xr, xi, ar, ai, sr, si):
    return xr + ar * sr - ai * si, xi + ar * si + ai * sr


def _s5_scan_fwd(s_ref, pw_ref, L):
    ns = SLAB_NS
    row = lax.broadcasted_iota(jnp.int32, (SUBLANES, ns), 0)
    pr = pw_ref[0, 0:SUBLANES, :]
    pi = pw_ref[1, 0:SUBLANES, :]

    def step(i, carry):
        cr, ci = carry
        r0 = pl.multiple_of(i * SUBLANES, SUBLANES)
        xr = s_ref[pl.ds(r0, SUBLANES), 0:ns]
        xi = s_ref[pl.ds(r0, SUBLANES), ns:2 * ns]
        for k in (1, 2, 4):
            xr, xi = _cmul_add(xr, xi, pr[k - 1:k, :], pi[k - 1:k, :], _shift_down(xr, k, row), _shift_down(xi, k, row))
        xr, xi = _cmul_add(xr, xi, pr, pi, cr, ci)
        s_ref[pl.ds(r0, SUBLANES), 0:ns] = xr
        s_ref[pl.ds(r0, SUBLANES), ns:2 * ns] = xi
        return xr[SUBLANES - 1:SUBLANES, :], xi[SUBLANES - 1:SUBLANES, :]

    z = jnp.zeros((1, ns), f32)
    lax.fori_loop(0, L // SUBLANES, step, (z, z))


def _s5_project_in(u_ref, bs_ref, s_ref, L, rc):
    for r in range(0, L, rc):
        s_ref[r:r + rc, :] = jnp.dot(u_ref[r:r + rc, :].astype(bf16), bs_ref[...], preferred_element_type=f32)


def _s5_fwd(p, bs, cs, pw, d_skip, B, L):
    rc = _tile(L, 344)

    def body(u_ref, bs_ref, cs_ref, pw_ref, d_ref, y_ref, s_ref):
        _s5_project_in(u_ref, bs_ref, s_ref, L, rc)
        _s5_scan_fwd(s_ref, pw_ref, L)
        for r in range(0, L, rc):
            ypre = (jnp.dot(s_ref[r:r + rc, :].astype(bf16), cs_ref[...], preferred_element_type=f32)
                    + d_ref[...] * u_ref[r:r + rc, :])
            y_ref[r:r + rc, :] = _gelu(ypre)

    ucol = SEG_U * (D_MODEL // SLAB_CH)
    return pl.pallas_call(
        body, name="s5_fwd", grid=(B, N_SLAB),
        in_specs=[pl.BlockSpec((L, SLAB_CH), lambda b, s: (b, ucol + s)),
                  pl.BlockSpec((None, SLAB_CH, 2 * SLAB_NS), lambda b, s: (s, 0, 0)),
                  pl.BlockSpec((None, 2 * SLAB_NS, SLAB_CH), lambda b, s: (s, 0, 0)),
                  pl.BlockSpec((None, 2, 2 * SUBLANES, SLAB_NS), lambda b, s: (s, 0, 0, 0)),
                  pl.BlockSpec((1, SLAB_CH), lambda b, s: (0, s))],
        out_specs=pl.BlockSpec((L, SLAB_CH), lambda b, s: (b, s)),
        out_shape=jax.ShapeDtypeStruct((B * L, D_MODEL), f32),
        scratch_shapes=[pltpu.VMEM((L, 2 * SLAB_NS), f32)],
        compiler_params=_params("parallel", "parallel"),
    )(p, bs, cs, pw, d_skip)


def _s5_bwd(p, dya0, dp, bs, cs, pw, d_skip, B, L):
    rc = _tile(L, 344)
    ns = SLAB_NS
    nt = L // SUBLANES

    def body(u_ref, dy_ref, dp_in, bs_ref, cs_ref, pw_ref, d_ref, du_ref, dbs_ref, dcs_ref, da_ref, dd_ref,
             s_ref, lam_ref, dyp_ref):
        del dp_in
        b = pl.program_id(1)

        @pl.when(b == 0)
        def _():
            dbs_ref[...] = jnp.zeros_like(dbs_ref)
            dcs_ref[...] = jnp.zeros_like(dcs_ref)
            da_ref[...] = jnp.zeros_like(da_ref)
            dd_ref[...] = jnp.zeros_like(dd_ref)

        _s5_project_in(u_ref, bs_ref, s_ref, L, rc)
        _s5_scan_fwd(s_ref, pw_ref, L)
        for r in range(0, L, rc):
            u = u_ref[r:r + rc, :]
            sb = s_ref[r:r + rc, :].astype(bf16)
            ypre = jnp.dot(sb, cs_ref[...], preferred_element_type=f32) + d_ref[...] * u
            dyp = dy_ref[r:r + rc, :] * _gelu_grad(ypre)
            dyp_ref[r:r + rc, :] = dyp
            dd_ref[...] += jnp.sum(dyp * u, axis=0, keepdims=True)
            dypb = dyp.astype(bf16)
            dcs_ref[...] += lax.dot_general(sb, dypb, _DIMS["tn"], preferred_element_type=f32)
            lam_ref[r:r + rc, :] = lax.dot_general(dypb, cs_ref[...], _DIMS["nt"], preferred_element_type=f32)

        row = lax.broadcasted_iota(jnp.int32, (SUBLANES, ns), 0)
        pr = pw_ref[0, 0:SUBLANES, :]
        pi = -pw_ref[1, 0:SUBLANES, :]
        qr = pw_ref[0, SUBLANES:2 * SUBLANES, :]
        qi = -pw_ref[1, SUBLANES:2 * SUBLANES, :]

        def step(j, carry):
            cr, ci, ar, ai = carry
            i = nt - 1 - j
            r0 = pl.multiple_of(i * SUBLANES, SUBLANES)
            xr = lam_ref[pl.ds(r0, SUBLANES), 0:ns]
            xi = lam_ref[pl.ds(r0, SUBLANES), ns:2 * ns]
            for k in (1, 2, 4):
                xr, xi = _cmul_add(xr, xi, pr[k - 1:k, :], pi[k - 1:k, :], _shift_up(xr, k, row), _shift_up(xi, k, row))
            xr, xi = _cmul_add(xr, xi, qr, qi, cr, ci)
            lam_ref[pl.ds(r0, SUBLANES), 0:ns] = xr
            lam_ref[pl.ds(r0, SUBLANES), ns:2 * ns] = xi
            rp = pl.multiple_of(jnp.maximum(i - 1, 0) * SUBLANES, SUBLANES)
            live = jnp.where(i > 0, 1.0, 0.0)
            lr_ = s_ref[pl.ds(rp + SUBLANES - 1, 1), 0:ns] * live
            li_ = s_ref[pl.ds(rp + SUBLANES - 1, 1), ns:2 * ns] * live
            spr = jnp.where(row == 0, lr_, pltpu.roll(s_ref[pl.ds(r0, SUBLANES), 0:ns], 1, 0))
            spi = jnp.where(row == 0, li_, pltpu.roll(s_ref[pl.ds(r0, SUBLANES), ns:2 * ns], 1, 0))
            ar = ar + xr * spr + xi * spi
            ai = ai + xi * spr - xr * spi
            return xr[0:1, :], xi[0:1, :], ar, ai

        z1 = jnp.zeros((1, ns), f32)
        z8 = jnp.zeros((SUBLANES, ns), f32)
        _, _, ar, ai = lax.fori_loop(0, nt, step, (z1, z1, z8, z8))
        da_ref[0:1, :] += jnp.sum(ar, axis=0, keepdims=True)
        da_ref[1:2, :] += jnp.sum(ai, axis=0, keepdims=True)

        for r in range(0, L, rc):
            lamb = lam_ref[r:r + rc, :].astype(bf16)
            dbs_ref[...] += lax.dot_general(u_ref[r:r + rc, :].astype(bf16), lamb, _DIMS["tn"], preferred_element_type=f32)
            du = (lax.dot_general(lamb, bs_ref[...], _DIMS["nt"], preferred_element_type=f32)
                  + d_ref[...] * dyp_ref[r:r + rc, :])
            du_ref[r:r + rc, :] = du.astype(du_ref.dtype)

    ucol = SEG_U * (D_MODEL // SLAB_CH)
    T = B * L
    return pl.pallas_call(
        body, name="s5_bwd", grid=(N_SLAB, B),
        in_specs=[pl.BlockSpec((L, SLAB_CH), lambda s, b: (b, ucol + s)),
                  pl.BlockSpec((L, SLAB_CH), lambda s, b: (b, s)),
                  ANY,
                  pl.BlockSpec((None, SLAB_CH, 2 * SLAB_NS), lambda s, b: (s, 0, 0)),
                  pl.BlockSpec((None, 2 * SLAB_NS, SLAB_CH), lambda s, b: (s, 0, 0)),
                  pl.BlockSpec((None, 2, 2 * SUBLANES, SLAB_NS), lambda s, b: (s, 0, 0, 0)),
                  pl.BlockSpec((1, SLAB_CH), lambda s, b: (0, s))],
        out_specs=[pl.BlockSpec((None, L, SLAB_CH), lambda s, b: (SEG_U, b, s)),
                   pl.BlockSpec((None, SLAB_CH, 2 * SLAB_NS), lambda s, b: (s, 0, 0)),
                   pl.BlockSpec((None, 2 * SLAB_NS, SLAB_CH), lambda s, b: (s, 0, 0)),
                   pl.BlockSpec((None, 2, SLAB_NS), lambda s, b: (s, 0, 0)),
                   pl.BlockSpec((1, SLAB_CH), lambda s, b: (0, s))],
        out_shape=[jax.ShapeDtypeStruct((N_SEG, T, D_MODEL), bf16),
                   jax.ShapeDtypeStruct((N_SLAB, SLAB_CH, 2 * SLAB_NS), f32),
                   jax.ShapeDtypeStruct((N_SLAB, 2 * SLAB_NS, SLAB_CH), f32),
                   jax.ShapeDtypeStruct((N_SLAB, 2, SLAB_NS), f32),
                   jax.ShapeDtypeStruct((1, D_MODEL), f32)],
        scratch_shapes=[pltpu.VMEM((L, 2 * SLAB_NS), f32), pltpu.VMEM((L, 2 * SLAB_NS), f32), pltpu.VMEM((L, SLAB_CH), f32)],
        input_output_aliases={2: 0},
        compiler_params=_params("parallel", "arbitrary"),
    )(p, dya0, dp, bs, cs, pw, d_skip)


def _rows8(i):
    return pl.ds(pl.multiple_of(i * SUBLANES, SUBLANES), SUBLANES)


def _repeat_loop(n, step, init):
    rep = max(u for u in (6, 4, 3, 2, 1) if n % u == 0)

    def body(t, carry):
        for u in range(rep):
            carry = step(t * rep + u, carry)
        return carry

    return lax.fori_loop(0, n // rep, body, init)


def _to_segments(src_ref, dst_ref, seg):
    def step(i, c):
        dst_ref[_rows8(i), :] = src_ref[pl.ds(i, SUBLANES, stride=seg), :]
        return c

    _repeat_loop(seg, step, 0)


def _from_segments(src_ref, dst_ref, seg):
    def step(i, c):
        dst_ref[pl.ds(i, SUBLANES, stride=seg), :] = src_ref[_rows8(i), :]
        return c

    _repeat_loop(seg, step, 0)


def _seg_local_scan(s_ref, ar, ai, seg, reverse):
    ns = SLAB_NS

    def step(j, carry):
        cr, ci = carry
        rows = _rows8(seg - 1 - j if reverse else j)
        cr, ci = _cmul_add(s_ref[rows, 0:ns], s_ref[rows, ns:2 * ns], ar, ai, cr, ci)
        s_ref[rows, 0:ns] = cr
        s_ref[rows, ns:2 * ns] = ci
        return cr, ci

    z = jnp.zeros((SUBLANES, ns), f32)
    return _repeat_loop(seg, step, (z, z))


def _seg_boundaries(fr, fi, alr, ali, reverse):
    row = lax.broadcasted_iota(jnp.int32, fr.shape, 0)
    br = jnp.zeros_like(fr)
    bi = jnp.zeros_like(fi)
    for r in (range(SUBLANES - 2, -1, -1) if reverse else range(1, SUBLANES)):
        s = r + 1 if reverse else r - 1
        nr, ni = _cmul_add(fr[s:s + 1, :], fi[s:s + 1, :], alr, ali, br[s:s + 1, :], bi[s:s + 1, :])
        br = jnp.where(row == r, nr, br)
        bi = jnp.where(row == r, ni, bi)
    return br, bi


def _s5_states(u_ref, bs_ref, pw_ref, up_ref, s_ref, L, rc):
    seg = L // SUBLANES
    ns = SLAB_NS
    _to_segments(u_ref, up_ref, seg)
    _s5_project_in(up_ref, bs_ref, s_ref, L, rc)
    ar, ai = pw_ref[0, 0:1, :], pw_ref[1, 0:1, :]
    fr, fi = _seg_local_scan(s_ref, ar, ai, seg, False)
    br, bi = _seg_boundaries(fr, fi, pw_ref[0, seg - 1:seg, :], pw_ref[1, seg - 1:seg, :], False)

    def fix(i, c):
        rows = _rows8(i)
        xr, xi = _cmul_add(s_ref[rows, 0:ns], s_ref[rows, ns:2 * ns], pw_ref[0, pl.ds(i, 1), :], pw_ref[1, pl.ds(i, 1), :], br, bi)
        s_ref[rows, 0:ns] = xr
        s_ref[rows, ns:2 * ns] = xi
        return c

    _repeat_loop(seg, fix, 0)


def _pw_spec(seg_rows, order):
    if order == "bs":
        return pl.BlockSpec((2, seg_rows, SLAB_NS), lambda b, s: (0, 0, s))
    return pl.BlockSpec((2, seg_rows, SLAB_NS), lambda s, b: (0, 0, s))


def _s5_fwd(p, bs, cs, pw, d_skip, B, L):
    rc = _tile(L, 344)
    seg = L // SUBLANES

    def body(u_ref, bs_ref, cs_ref, pw_ref, d_ref, y_ref, s_ref, up_ref, yp_ref):
        _s5_states(u_ref, bs_ref, pw_ref, up_ref, s_ref, L, rc)
        for r in range(0, L, rc):
            ypre = (jnp.dot(s_ref[r:r + rc, :].astype(bf16), cs_ref[...], preferred_element_type=f32)
                    + d_ref[...] * up_ref[r:r + rc, :])
            yp_ref[r:r + rc, :] = _gelu(ypre)
        _from_segments(yp_ref, y_ref, seg)

    ucol = SEG_U * (D_MODEL // SLAB_CH)
    return pl.pallas_call(
        body, name="s5_fwd", grid=(B, N_SLAB),
        in_specs=[pl.BlockSpec((L, SLAB_CH), lambda b, s: (b, ucol + s)),
                  pl.BlockSpec((None, SLAB_CH, 2 * SLAB_NS), lambda b, s: (s, 0, 0)),
                  pl.BlockSpec((None, 2 * SLAB_NS, SLAB_CH), lambda b, s: (s, 0, 0)),
                  _pw_spec(pw.shape[1], "bs"),
                  pl.BlockSpec((1, SLAB_CH), lambda b, s: (0, s))],
        out_specs=pl.BlockSpec((L, SLAB_CH), lambda b, s: (b, s)),
        out_shape=jax.ShapeDtypeStruct((B * L, D_MODEL), f32),
        scratch_shapes=[pltpu.VMEM((L, 2 * SLAB_NS), f32), pltpu.VMEM((L, SLAB_CH), f32), pltpu.VMEM((L, SLAB_CH), f32)],
        compiler_params=_params("parallel", "parallel"),
    )(p, bs, cs, pw, d_skip)


def _s5_bwd(p, dya0, dp, bs, cs, pw, d_skip, B, L, sums):
    rc = _tile(L, 344)
    ns = SLAB_NS
    seg = L // SUBLANES
    nx = len(sums)

    def body(u_ref, dy_ref, dp_in, bs_ref, cs_ref, pw_ref, d_ref, *rest):
        xin, (du_ref, dbs_ref, dcs_ref, da_ref, dd_ref), xout = rest[:nx], rest[nx:nx + 5], rest[nx + 5:2 * nx + 5]
        s_ref, lam_ref, up_ref, dyp_ref, nat_ref, send, recv = rest[2 * nx + 5:]
        del dp_in
        start, finish = _chip_exchange_steps(xin, xout, send, recv)

        @pl.when((pl.program_id(0) == 0) & (pl.program_id(1) == 0))
        def _():
            start()

        @pl.when(pl.program_id(1) == 0)
        def _():
            dbs_ref[...] = jnp.zeros_like(dbs_ref)
            dcs_ref[...] = jnp.zeros_like(dcs_ref)
            da_ref[...] = jnp.zeros_like(da_ref)
            dd_ref[...] = jnp.zeros_like(dd_ref)

        _s5_states(u_ref, bs_ref, pw_ref, up_ref, s_ref, L, rc)
        _to_segments(dy_ref, dyp_ref, seg)
        for r in range(0, L, rc):
            u = up_ref[r:r + rc, :]
            sb = s_ref[r:r + rc, :].astype(bf16)
            ypre = jnp.dot(sb, cs_ref[...], preferred_element_type=f32) + d_ref[...] * u
            dyp = dyp_ref[r:r + rc, :] * _gelu_grad(ypre)
            dyp_ref[r:r + rc, :] = dyp
            dd_ref[...] += jnp.sum(dyp * u, axis=0, keepdims=True)
            dypb = dyp.astype(bf16)
            dcs_ref[...] += lax.dot_general(sb, dypb, _DIMS["tn"], preferred_element_type=f32)
            lam_ref[r:r + rc, :] = lax.dot_general(dypb, cs_ref[...], _DIMS["nt"], preferred_element_type=f32)

        ar, ai = pw_ref[0, 0:1, :], -pw_ref[1, 0:1, :]
        fr, fi = _seg_local_scan(lam_ref, ar, ai, seg, True)
        br, bi = _seg_boundaries(fr, fi, pw_ref[0, seg - 1:seg, :], -pw_ref[1, seg - 1:seg, :], True)

        def fix(i, acc):
            accr, acci = acc
            rows = _rows8(i)
            k = seg - 1 - i
            xr, xi = _cmul_add(lam_ref[rows, 0:ns], lam_ref[rows, ns:2 * ns], pw_ref[0, pl.ds(k, 1), :],
                               -pw_ref[1, pl.ds(k, 1), :], br, bi)
            lam_ref[rows, 0:ns] = xr
            lam_ref[rows, ns:2 * ns] = xi
            prev = _rows8(jnp.maximum(i - 1, 0))
            live = jnp.where(i > 0, 1.0, 0.0)
            spr = s_ref[prev, 0:ns] * live
            spi = s_ref[prev, ns:2 * ns] * live
            return accr + xr * spr + xi * spi, acci + xi * spr - xr * spi

        z = jnp.zeros((SUBLANES, ns), f32)
        accr, acci = _repeat_loop(seg, fix, (z, z))
        row = lax.broadcasted_iota(jnp.int32, (SUBLANES, ns), 0)
        last = _rows8(seg - 1)
        spr = jnp.where(row == 0, 0.0, pltpu.roll(s_ref[last, 0:ns], 1, 0))
        spi = jnp.where(row == 0, 0.0, pltpu.roll(s_ref[last, ns:2 * ns], 1, 0))
        xr, xi = lam_ref[0:SUBLANES, 0:ns], lam_ref[0:SUBLANES, ns:2 * ns]
        accr = accr + xr * spr + xi * spi
        acci = acci + xi * spr - xr * spi
        da_ref[0:1, :] += jnp.sum(accr, axis=0, keepdims=True)
        da_ref[1:2, :] += jnp.sum(acci, axis=0, keepdims=True)

        for r in range(0, L, rc):
            lamb = lam_ref[r:r + rc, :].astype(bf16)
            dbs_ref[...] += lax.dot_general(up_ref[r:r + rc, :].astype(bf16), lamb, _DIMS["tn"], preferred_element_type=f32)
            nat_ref[r:r + rc, :] = (lax.dot_general(lamb, bs_ref[...], _DIMS["nt"], preferred_element_type=f32)
                                    + d_ref[...] * dyp_ref[r:r + rc, :])
        _from_segments(nat_ref, up_ref, seg)
        du_ref[...] = up_ref[...].astype(du_ref.dtype)

        @pl.when((pl.program_id(0) == N_SLAB - 1) & (pl.program_id(1) == B - 1))
        def _():
            finish()

    ucol = SEG_U * (D_MODEL // SLAB_CH)
    T = B * L
    col = pltpu.VMEM((L, SLAB_CH), f32)
    res = pl.pallas_call(
        body, name="s5_bwd", grid=(N_SLAB, B),
        in_specs=[pl.BlockSpec((L, SLAB_CH), lambda s, b: (b, ucol + s)),
                  pl.BlockSpec((L, SLAB_CH), lambda s, b: (b, s)),
                  ANY,
                  pl.BlockSpec((None, SLAB_CH, 2 * SLAB_NS), lambda s, b: (s, 0, 0)),
                  pl.BlockSpec((None, 2 * SLAB_NS, SLAB_CH), lambda s, b: (s, 0, 0)),
                  _pw_spec(pw.shape[1], "sb"),
                  pl.BlockSpec((1, SLAB_CH), lambda s, b: (0, s))] + [ANY] * nx,
        out_specs=[pl.BlockSpec((None, L, SLAB_CH), lambda s, b: (SEG_U, b, s)),
                   pl.BlockSpec((None, SLAB_CH, 2 * SLAB_NS), lambda s, b: (s, 0, 0)),
                   pl.BlockSpec((None, 2 * SLAB_NS, SLAB_CH), lambda s, b: (s, 0, 0)),
                   pl.BlockSpec((None, 2, SLAB_NS), lambda s, b: (s, 0, 0)),
                   pl.BlockSpec((1, SLAB_CH), lambda s, b: (0, s))] + [ANY] * nx,
        out_shape=[jax.ShapeDtypeStruct((N_SEG, T, D_MODEL), bf16),
                   jax.ShapeDtypeStruct((N_SLAB, SLAB_CH, 2 * SLAB_NS), f32),
                   jax.ShapeDtypeStruct((N_SLAB, 2 * SLAB_NS, SLAB_CH), f32),
                   jax.ShapeDtypeStruct((N_SLAB, 2, SLAB_NS), f32),
                   jax.ShapeDtypeStruct((1, D_MODEL), f32)] + [jax.ShapeDtypeStruct(a.shape, a.dtype) for a in sums],
        scratch_shapes=[pltpu.VMEM((L, 2 * SLAB_NS), f32), pltpu.VMEM((L, 2 * SLAB_NS), f32), col, col, col]
        + _chip_exchange_sems(nx),
        input_output_aliases={2: 0},
        compiler_params=_params("arbitrary", "arbitrary"),
    )(p, dya0, dp, bs, cs, pw, d_skip, *sums)
    return res[:5], res[5:]


def _dotb(a, b, dims="nn"):
    return lax.dot_general(a.astype(bf16), b.astype(bf16), _DIMS[dims], preferred_element_type=f32)


def _tile_scan(x, reverse):
    n, w = x.shape
    v = x.reshape(n // SUBLANES, SUBLANES, w)
    row = lax.broadcasted_iota(jnp.int32, v.shape, 1)
    for k in (1, 2, 4):
        if reverse:
            v = v + jnp.where(row < SUBLANES - k, pltpu.roll(v, SUBLANES - k, 1), 0.0)
        else:
            v = v + jnp.where(row >= k, pltpu.roll(v, k, 1), 0.0)
    p = v.reshape(n // CHUNK, 2, SUBLANES, w)
    lo, hi = p[:, 0], p[:, 1]
    if reverse:
        lo = lo + hi[:, 0:1, :]
    else:
        hi = hi + lo[:, SUBLANES - 1:SUBLANES, :]
    return jnp.stack([lo, hi], axis=1).reshape(n, w)


def _chunk_cumsum(x):
    return _tile_scan(x, False)


def _chunk_rev_cumsum(x):
    return _tile_scan(x, True)


def _chunk_last(x):
    n, w = x.shape
    p = x.reshape(n // CHUNK, CHUNK, w)
    return jnp.broadcast_to(p[:, CHUNK - 1:CHUNK, :], p.shape).reshape(n, w)


def _hgrn_local(q, fl, lb):
    sg = _sigmoid(fl)
    f = lb + (1.0 - lb) * sg
    g = jnp.log(f)
    cum = _chunk_cumsum(g)
    rest = _chunk_last(cum) - cum
    e = jnp.exp(cum)
    em = jnp.exp(-cum)
    eo = jnp.exp(rest)
    k = 1.0 - f
    return sg, f, e, em, eo, q * e, k * em, k * eo, cum + rest


def _hgrn_block_mask(n):
    r = lax.broadcasted_iota(jnp.int32, (n, n), 0)
    c = lax.broadcasted_iota(jnp.int32, (n, n), 1)
    return ((r & -CHUNK) == (c & -CHUNK)) & (c <= r)


def _chunk_pos(n):
    return lax.broadcasted_iota(jnp.int32, (n, HEAD_DIM), 0) & (CHUNK - 1)


def _hgrn_block_rows(L):
    return _tile(L, 688, CHUNK)


def _chunk_rows(c):
    return pl.ds(pl.multiple_of(c * CHUNK, CHUNK), CHUNK)


def _chunk_loop(nc, step):
    rep = max(u for u in range(1, 49) if nc % u == 0)

    def body(i, carry):
        for u in range(rep):
            step(i * rep + u)
        return carry

    lax.fori_loop(0, nc // rep, body, 0)


def _hgrn_specs(L, order):
    hb = D_MODEL // HEAD_DIM

    def spec(seg):
        if order == "bh":
            return pl.BlockSpec((L, HEAD_DIM), lambda b, h: (b, seg * hb + h))
        return pl.BlockSpec((L, HEAD_DIM), lambda h, b: (b, seg * hb + h))

    return [spec(SEG_Q), spec(SEG_F), spec(SEG_I), spec(SEG_OG)]


def _hgrn_fwd(p, lb, norm_g, B, L):
    nc = L // CHUNK

    rb = _hgrn_block_rows(L)

    def body(q_ref, f_ref, v_ref, og_ref, lb_ref, ng_ref, y_ref, qt_s, ko_s, vb_s, dec_s, o_s, u_s, sb_s):
        lbv = lb_ref[...]
        ngv = ng_ref[...]
        mask = _hgrn_block_mask(rb)

        for r in range(0, L, rb):
            rows = slice(r, r + rb)
            _, _, _, _, _, qt, kt, ko, dec = _hgrn_local(q_ref[rows, :], f_ref[rows, :], lbv)
            vb = v_ref[rows, :].astype(bf16)
            qtb = qt.astype(bf16)
            pm = jnp.where(mask, _dotb(qtb, kt, "nt"), 0.0)
            o_s[rows, :] = _dotb(pm, vb)
            qt_s[rows, :] = qtb
            ko_s[rows, :] = ko.astype(bf16)
            vb_s[rows, :] = vb
            dec_s[rows, :] = dec

        def update(c):
            rows = _chunk_rows(c)
            u_s[c] = _dotb(vb_s[rows, :], ko_s[rows, :], "tn")

        def chain(c, st):
            sb_s[c] = st.astype(bf16)
            return st * dec_s[_chunk_rows(c), :][0:1, :] + u_s[c]

        def attend(c):
            rows = _chunk_rows(c)
            o_s[rows, :] += _dotb(qt_s[rows, :], sb_s[c], "nt")

        _chunk_loop(nc, update)
        lax.fori_loop(0, nc, chain, jnp.zeros((HEAD_DIM, HEAD_DIM), f32))
        _chunk_loop(nc, attend)

        for r in range(0, L, rb):
            rows = slice(r, r + rb)
            o = o_s[rows, :]
            og = og_ref[rows, :]
            on = o * lax.rsqrt(jnp.mean(o * o, axis=-1, keepdims=True) + EPS) * ngv
            y_ref[rows, :] = (on * og * _sigmoid(og)).astype(y_ref.dtype)

    return pl.pallas_call(
        body, name="hgrn_fwd", grid=(B, HEADS),
        in_specs=_hgrn_specs(L, "bh") + [pl.BlockSpec((1, HEAD_DIM), lambda b, h: (0, h)),
                                          pl.BlockSpec((1, HEAD_DIM), lambda b, h: (0, 0))],
        out_specs=pl.BlockSpec((L, HEAD_DIM), lambda b, h: (b, h)),
        out_shape=jax.ShapeDtypeStruct((B * L, D_MODEL), bf16),
        scratch_shapes=[pltpu.VMEM((L, HEAD_DIM), bf16), pltpu.VMEM((L, HEAD_DIM), bf16), pltpu.VMEM((L, HEAD_DIM), bf16),
                        pltpu.VMEM((L, HEAD_DIM), f32), pltpu.VMEM((L, HEAD_DIM), f32),
                        pltpu.VMEM((nc, HEAD_DIM, HEAD_DIM), f32), pltpu.VMEM((nc, HEAD_DIM, HEAD_DIM), bf16)],
        compiler_params=_params("parallel", "parallel"),
    )(p, p, p, p, lb, norm_g)


def _hgrn_bwd(p, dyb, dp, lb, norm_g, B, L):
    nc = L // CHUNK

    rb = _hgrn_block_rows(L)

    def body(q_ref, f_ref, v_ref, og_ref, dy_ref, dp_in, lb_ref, ng_ref, dseg_ref, dlb_ref, dng_ref,
             st_ref, u_s, dsb_s, qt_s, kt_s, ko_s, vb_s, do_s, dec_s, o_s, dqt_s, dkt_s, dko_s, dv_s, ddec_s):
        del dp_in
        lbv = lb_ref[...]
        ngv = ng_ref[...]
        mask = _hgrn_block_mask(rb)
        pos = _chunk_pos(rb)
        blocks = [slice(r, r + rb) for r in range(0, L, rb)]

        @pl.when(pl.program_id(1) == 0)
        def _():
            dlb_ref[...] = jnp.zeros_like(dlb_ref)

        @pl.when((pl.program_id(0) == 0) & (pl.program_id(1) == 0))
        def _():
            dng_ref[...] = jnp.zeros_like(dng_ref)

        def scores(rows):
            return jnp.where(mask, _dotb(qt_s[rows, :], kt_s[rows, :], "nt"), 0.0).astype(bf16)

        for rows in blocks:
            _, _, _, _, _, qt, kt, ko, dec = _hgrn_local(q_ref[rows, :], f_ref[rows, :], lbv)
            qt_s[rows, :] = qt.astype(bf16)
            kt_s[rows, :] = kt.astype(bf16)
            ko_s[rows, :] = ko.astype(bf16)
            vb_s[rows, :] = v_ref[rows, :].astype(bf16)
            dec_s[rows, :] = dec
            o_s[rows, :] = _dotb(scores(rows), vb_s[rows, :])

        def update(c):
            rows = _chunk_rows(c)
            u_s[c] = _dotb(vb_s[rows, :], ko_s[rows, :], "tn")

        def chain(c, st):
            st_ref[c] = st
            return st * dec_s[_chunk_rows(c), :][0:1, :] + u_s[c]

        def attend(c):
            rows = _chunk_rows(c)
            o_s[rows, :] += _dotb(qt_s[rows, :], st_ref[c], "nt")

        _chunk_loop(nc, update)
        lax.fori_loop(0, nc, chain, jnp.zeros((HEAD_DIM, HEAD_DIM), f32))
        _chunk_loop(nc, attend)

        dng = jnp.zeros((1, HEAD_DIM), f32)
        for rows in blocks:
            o = o_s[rows, :]
            og = og_ref[rows, :]
            dy = dy_ref[rows, :]
            rs = lax.rsqrt(jnp.mean(o * o, axis=-1, keepdims=True) + EPS)
            xn = o * rs
            so = _sigmoid(og)
            dseg_ref[SEG_OG, rows, :] = (dy * xn * ngv * so * (1.0 + og * (1.0 - so))).astype(dseg_ref.dtype)
            don = dy * og * so
            dng = dng + jnp.sum(don * xn, axis=0, keepdims=True)
            dxo = don * ngv
            do = (rs * (dxo - xn * jnp.mean(dxo * xn, axis=-1, keepdims=True))).astype(bf16)
            do_s[rows, :] = do
            dpm = jnp.where(mask, _dotb(do, vb_s[rows, :], "nt"), 0.0).astype(bf16)
            dqt_s[rows, :] = _dotb(dpm, kt_s[rows, :])
            dkt_s[rows, :] = _dotb(dpm, qt_s[rows, :], "tn")
            dv_s[rows, :] = _dotb(scores(rows), do, "tn")
        dng_ref[...] += dng

        def rupdate(c):
            rows = _chunk_rows(c)
            u_s[c] = _dotb(do_s[rows, :], qt_s[rows, :], "tn")

        def rchain(j, dst):
            c = nc - 1 - j
            rows = _chunk_rows(c)
            dsb_s[c] = dst.astype(bf16)
            ddec_s[rows, :] = jnp.broadcast_to(jnp.sum(dst * st_ref[c], axis=0, keepdims=True), (CHUNK, HEAD_DIM))
            return dst * dec_s[rows, :][0:1, :] + u_s[c]

        def rattend(c):
            rows = _chunk_rows(c)
            dst = dsb_s[c]
            dqt_s[rows, :] += _dotb(do_s[rows, :], st_ref[c])
            dv_s[rows, :] += _dotb(ko_s[rows, :], dst, "nt")
            dko_s[rows, :] = _dotb(vb_s[rows, :], dst)

        _chunk_loop(nc, rupdate)
        lax.fori_loop(0, nc, rchain, jnp.zeros((HEAD_DIM, HEAD_DIM), f32))
        _chunk_loop(nc, rattend)

        dlb = jnp.zeros((1, HEAD_DIM), f32)
        for rows in blocks:
            sg, f, e, em, eo, qt, kt, ko, dec = _hgrn_local(q_ref[rows, :], f_ref[rows, :], lbv)
            dqt = dqt_s[rows, :]
            dkt = dkt_s[rows, :]
            dko = dko_s[rows, :]
            dko_ko = dko * ko
            dcum = dqt * qt - dkt * kt - dko_ko
            dcum = dcum + jnp.where(pos == CHUNK - 1, _chunk_cumsum(dko_ko) + ddec_s[rows, :] * dec, 0.0)
            df = _chunk_rev_cumsum(dcum) / f - (dkt * em + dko * eo)
            dlb = dlb + jnp.sum(df * (1.0 - sg), axis=0, keepdims=True)
            dseg_ref[SEG_Q, rows, :] = (dqt * e).astype(dseg_ref.dtype)
            dseg_ref[SEG_F, rows, :] = (df * (1.0 - lbv) * sg * (1.0 - sg)).astype(dseg_ref.dtype)
            dseg_ref[SEG_I, rows, :] = dv_s[rows, :].astype(dseg_ref.dtype)
        dlb_ref[...] += dlb

    T = B * L
    sb = pltpu.VMEM((L, HEAD_DIM), bf16)
    sf = pltpu.VMEM((L, HEAD_DIM), f32)
    return pl.pallas_call(
        body, name="hgrn_bwd", grid=(HEADS, B),
        in_specs=_hgrn_specs(L, "hb") + [pl.BlockSpec((L, HEAD_DIM), lambda h, b: (b, h)), ANY,
                                          pl.BlockSpec((1, HEAD_DIM), lambda h, b: (0, h)),
                                          pl.BlockSpec((1, HEAD_DIM), lambda h, b: (0, 0))],
        out_specs=[pl.BlockSpec((4, L, HEAD_DIM), lambda h, b: (0, b, h)),
                   pl.BlockSpec((1, HEAD_DIM), lambda h, b: (0, h)),
                   pl.BlockSpec((1, HEAD_DIM), lambda h, b: (0, 0))],
        out_shape=[jax.ShapeDtypeStruct((N_SEG, T, D_MODEL), bf16), jax.ShapeDtypeStruct((1, D_MODEL), f32),
                   jax.ShapeDtypeStruct((1, HEAD_DIM), f32)],
        scratch_shapes=[pltpu.VMEM((nc, HEAD_DIM, HEAD_DIM), f32), pltpu.VMEM((nc, HEAD_DIM, HEAD_DIM), f32),
                        pltpu.VMEM((nc, HEAD_DIM, HEAD_DIM), bf16), sb, sb, sb, sb, sb, sf, sf, sf, sf, sf, sf, sf],
        input_output_aliases={5: 0},
        compiler_params=_params("arbitrary", "arbitrary"),
    )(p, p, p, p, dyb, dp, lb, norm_g)


PAIR = 2 * CHUNK


def _pair_steps(L, rb):
    steps = []
    nch = rb // CHUNK
    for r in range(0, L, rb):
        steps += [(r + p * PAIR, PAIR) for p in range(nch // 2)]
        if nch % 2:
            steps.append((r + (nch - 1) * CHUNK, CHUNK))
    return steps


def _pair_flags(rb):
    ci = lax.broadcasted_iota(jnp.int32, (rb, HEAD_DIM), 0) >> 4
    odd = (ci & 1) == 1
    has_next = jnp.logical_and(jnp.logical_not(odd), ci < rb // CHUNK - 1)
    return odd, has_next


def _pair_masks(rb):
    r = lax.broadcasted_iota(jnp.int32, (rb, rb), 0)
    c = lax.broadcasted_iota(jnp.int32, (rb, rb), 1)
    rc, cc = r >> 4, c >> 4
    same = (rc == cc) & (c <= r)
    prev = ((rc & 1) == 1) & (cc == rc - 1)
    return same, prev


def _hgrn_pair_local(q, fl, lb, odd, has_next):
    sg, f, e, em, eo, qt, kt, ko, cend = _hgrn_local(q, fl, lb)
    n = q.shape[0]
    a = jnp.where(odd, pltpu.roll(cend, CHUNK, 0), 0.0)
    z = jnp.where(has_next, pltpu.roll(cend, n - CHUNK, 0), 0.0)
    ea, ez = jnp.exp(a), jnp.exp(z)
    return dict(sg=sg, f=f, e=e, em=em, eo=eo, qt=qt, kt=kt, ko=ko, ea=ea, ez=ez, qs=qt * ea, ks=ko * ez,
                decp=jnp.exp(cend + a + z))


def _pair_scores(qt, kt, ko, same, prev):
    return (jnp.where(same, _dotb(qt, kt, "nt"), 0.0) + jnp.where(prev, _dotb(qt, ko, "nt"), 0.0)).astype(bf16)


def _hgrn_fwd(p, lb, norm_g, B, L):
    rb = _hgrn_block_rows(L)
    steps = _pair_steps(L, rb)
    blocks = [slice(r, r + rb) for r in range(0, L, rb)]

    def body(q_ref, f_ref, v_ref, og_ref, lb_ref, ng_ref, y_ref, qs_s, ks_s, vb_s, decp_s, o_s, o2_s, u_s, sb_s):
        lbv = lb_ref[...]
        ngv = ng_ref[...]
        same, prev = _pair_masks(rb)
        odd, has_next = _pair_flags(rb)

        for rows in blocks:
            t = _hgrn_pair_local(q_ref[rows, :], f_ref[rows, :], lbv, odd, has_next)
            vb = v_ref[rows, :].astype(bf16)
            o_s[rows, :] = _dotb(_pair_scores(t["qt"], t["kt"], t["ko"], same, prev), vb)
            qs_s[rows, :] = t["qs"].astype(bf16)
            ks_s[rows, :] = t["ks"].astype(bf16)
            vb_s[rows, :] = vb
            decp_s[rows, :] = t["decp"]

        for n, (r0, nr) in enumerate(steps):
            u_s[n] = _dotb(vb_s[r0:r0 + nr, :], ks_s[r0:r0 + nr, :], "tn")
        st = jnp.zeros((HEAD_DIM, HEAD_DIM), f32)
        for n, (r0, nr) in enumerate(steps):
            sb_s[n] = st.astype(bf16)
            st = st * decp_s[r0:r0 + 1, :] + u_s[n]
        for n, (r0, nr) in enumerate(steps):
            o2_s[r0:r0 + nr, :] = _dotb(qs_s[r0:r0 + nr, :], sb_s[n], "nt")

        for rows in blocks:
            o = o_s[rows, :] + o2_s[rows, :]
            og = og_ref[rows, :]
            on = o * lax.rsqrt(jnp.mean(o * o, axis=-1, keepdims=True) + EPS) * ngv
            y_ref[rows, :] = (on * og * _sigmoid(og)).astype(y_ref.dtype)

    sb = pltpu.VMEM((L, HEAD_DIM), bf16)
    sf = pltpu.VMEM((L, HEAD_DIM), f32)
    return pl.pallas_call(
        body, name="hgrn_fwd", grid=(B, HEADS),
        in_specs=_hgrn_specs(L, "bh") + [pl.BlockSpec((1, HEAD_DIM), lambda b, h: (0, h)),
                                          pl.BlockSpec((1, HEAD_DIM), lambda b, h: (0, 0))],
        out_specs=pl.BlockSpec((L, HEAD_DIM), lambda b, h: (b, h)),
        out_shape=jax.ShapeDtypeStruct((B * L, D_MODEL), bf16),
        scratch_shapes=[sb, sb, sb, sf, sf, sf, pltpu.VMEM((len(steps), HEAD_DIM, HEAD_DIM), f32),
                        pltpu.VMEM((len(steps), HEAD_DIM, HEAD_DIM), bf16)],
        compiler_params=_params("parallel", "parallel"),
    )(p, p, p, p, lb, norm_g)


def _hgrn_bwd(p, dyb, dp, lb, norm_g, B, L):
    rb = _hgrn_block_rows(L)
    steps = _pair_steps(L, rb)
    blocks = [slice(r, r + rb) for r in range(0, L, rb)]

    def body(q_ref, f_ref, v_ref, og_ref, dy_ref, dp_in, lb_ref, ng_ref, dseg_ref, dlb_ref, dng_ref,
             st_ref, u_s, dsb_s, qt_s, kt_s, ko_s, qs_s, ks_s, vb_s, do_s,
             decp_s, o_s, o2_s, dqt_s, dkt_s, dko_s, dv_s, dv2_s, dqs_s, dks_s, ddecp_s):
        del dp_in
        lbv = lb_ref[...]
        ngv = ng_ref[...]
        same, prev = _pair_masks(rb)
        odd, has_next = _pair_flags(rb)
        pos = _chunk_pos(rb)

        @pl.when(pl.program_id(1) == 0)
        def _():
            dlb_ref[...] = jnp.zeros_like(dlb_ref)

        @pl.when((pl.program_id(0) == 0) & (pl.program_id(1) == 0))
        def _():
            dng_ref[...] = jnp.zeros_like(dng_ref)

        def scores(rows):
            return _pair_scores(qt_s[rows, :], kt_s[rows, :], ko_s[rows, :], same, prev)

        for rows in blocks:
            t = _hgrn_pair_local(q_ref[rows, :], f_ref[rows, :], lbv, odd, has_next)
            for dst, key in ((qt_s, "qt"), (kt_s, "kt"), (ko_s, "ko"), (qs_s, "qs"), (ks_s, "ks")):
                dst[rows, :] = t[key].astype(bf16)
            vb_s[rows, :] = v_ref[rows, :].astype(bf16)
            decp_s[rows, :] = t["decp"]
            o_s[rows, :] = _dotb(scores(rows), vb_s[rows, :])

        for n, (r0, nr) in enumerate(steps):
            u_s[n] = _dotb(vb_s[r0:r0 + nr, :], ks_s[r0:r0 + nr, :], "tn")
        st = jnp.zeros((HEAD_DIM, HEAD_DIM), f32)
        for n, (r0, nr) in enumerate(steps):
            st_ref[n] = st
            st = st * decp_s[r0:r0 + 1, :] + u_s[n]
        for n, (r0, nr) in enumerate(steps):
            o2_s[r0:r0 + nr, :] = _dotb(qs_s[r0:r0 + nr, :], st_ref[n], "nt")

        dng = jnp.zeros((1, HEAD_DIM), f32)
        for rows in blocks:
            o = o_s[rows, :] + o2_s[rows, :]
            og = og_ref[rows, :]
            dy = dy_ref[rows, :]
            rs = lax.rsqrt(jnp.mean(o * o, axis=-1, keepdims=True) + EPS)
            xn = o * rs
            so = _sigmoid(og)
            dseg_ref[SEG_OG, rows, :] = (dy * xn * ngv * so * (1.0 + og * (1.0 - so))).astype(dseg_ref.dtype)
            don = dy * og * so
            dng = dng + jnp.sum(don * xn, axis=0, keepdims=True)
            dxo = don * ngv
            do = (rs * (dxo - xn * jnp.mean(dxo * xn, axis=-1, keepdims=True))).astype(bf16)
            do_s[rows, :] = do
            dpf = _dotb(do, vb_s[rows, :], "nt")
            dp1 = jnp.where(same, dpf, 0.0).astype(bf16)
            dp2 = jnp.where(prev, dpf, 0.0).astype(bf16)
            dqt_s[rows, :] = _dotb(dp1, kt_s[rows, :]) + _dotb(dp2, ko_s[rows, :])
            dkt_s[rows, :] = _dotb(dp1, qt_s[rows, :], "tn")
            dko_s[rows, :] = _dotb(dp2, qt_s[rows, :], "tn")
            dv_s[rows, :] = _dotb(scores(rows), do, "tn")
        dng_ref[...] += dng

        for n, (r0, nr) in enumerate(steps):
            u_s[n] = _dotb(do_s[r0:r0 + nr, :], qs_s[r0:r0 + nr, :], "tn")
        dst = jnp.zeros((HEAD_DIM, HEAD_DIM), f32)
        for n, (r0, nr) in reversed(list(enumerate(steps))):
            dsb_s[n] = dst.astype(bf16)
            ddecp_s[r0:r0 + nr, :] = jnp.broadcast_to(jnp.sum(dst * st_ref[n], axis=0, keepdims=True), (nr, HEAD_DIM))
            dst = dst * decp_s[r0:r0 + 1, :] + u_s[n]
        for n, (r0, nr) in enumerate(steps):
            rows = slice(r0, r0 + nr)
            dqs_s[rows, :] = _dotb(do_s[rows, :], st_ref[n])
            dv2_s[rows, :] = _dotb(ks_s[rows, :], dsb_s[n], "nt")
            dks_s[rows, :] = _dotb(vb_s[rows, :], dsb_s[n])

        def chunk_sum(x):
            return _chunk_last(_chunk_cumsum(x))

        dlb = jnp.zeros((1, HEAD_DIM), f32)
        for rows in blocks:
            t = _hgrn_pair_local(q_ref[rows, :], f_ref[rows, :], lbv, odd, has_next)
            dqs, dks = dqs_s[rows, :], dks_s[rows, :]
            dqt = dqt_s[rows, :] + dqs * t["ea"]
            dko = dko_s[rows, :] + dks * t["ez"]
            dkt = dkt_s[rows, :]
            dko_ko = dko * t["ko"]
            dcum = dqt * t["qt"] - dkt * t["kt"] - dko_ko
            from_next = pltpu.roll(chunk_sum(jnp.where(odd, dqs * t["qs"], 0.0)), rb - CHUNK, 0)
            from_prev = pltpu.roll(chunk_sum(jnp.where(has_next, dks * t["ks"], 0.0)), CHUNK, 0)
            d_end = (chunk_sum(dko_ko) + jnp.where(has_next, from_next, 0.0) + jnp.where(odd, from_prev, 0.0)
                     + ddecp_s[rows, :] * t["decp"])
            dcum = dcum + jnp.where(pos == CHUNK - 1, d_end, 0.0)
            df = _chunk_rev_cumsum(dcum) / t["f"] - (dkt * t["em"] + dko * t["eo"])
            dlb = dlb + jnp.sum(df * (1.0 - t["sg"]), axis=0, keepdims=True)
            dseg_ref[SEG_Q, rows, :] = (dqt * t["e"]).astype(dseg_ref.dtype)
            dseg_ref[SEG_F, rows, :] = (df * (1.0 - lbv) * t["sg"] * (1.0 - t["sg"])).astype(dseg_ref.dtype)
            dseg_ref[SEG_I, rows, :] = (dv_s[rows, :] + dv2_s[rows, :]).astype(dseg_ref.dtype)
        dlb_ref[...] += dlb

    T = B * L
    ns = len(steps)
    sb = pltpu.VMEM((L, HEAD_DIM), bf16)
    sf = pltpu.VMEM((L, HEAD_DIM), f32)
    return pl.pallas_call(
        body, name="hgrn_bwd", grid=(HEADS, B),
        in_specs=_hgrn_specs(L, "hb") + [pl.BlockSpec((L, HEAD_DIM), lambda h, b: (b, h)), ANY,
                                          pl.BlockSpec((1, HEAD_DIM), lambda h, b: (0, h)),
                                          pl.BlockSpec((1, HEAD_DIM), lambda h, b: (0, 0))],
        out_specs=[pl.BlockSpec((4, L, HEAD_DIM), lambda h, b: (0, b, h)),
                   pl.BlockSpec((1, HEAD_DIM), lambda h, b: (0, h)),
                   pl.BlockSpec((1, HEAD_DIM), lambda h, b: (0, 0))],
        out_shape=[jax.ShapeDtypeStruct((N_SEG, T, D_MODEL), bf16), jax.ShapeDtypeStruct((1, D_MODEL), f32),
                   jax.ShapeDtypeStruct((1, HEAD_DIM), f32)],
        scratch_shapes=[pltpu.VMEM((ns, HEAD_DIM, HEAD_DIM), f32), pltpu.VMEM((ns, HEAD_DIM, HEAD_DIM), f32),
                        pltpu.VMEM((ns, HEAD_DIM, HEAD_DIM), bf16)] + [sb] * 7 + [sf] * 11,
        input_output_aliases={5: 0},
        compiler_params=_params("arbitrary", "arbitrary"),
    )(p, p, p, p, dyb, dp, lb, norm_g)


def _dz1(dp, w_in_phys):
    _, T, Dm = dp.shape
    tm = _tile(T, 1032)
    return _mm("dz1", dp, w_in_phys, "nt", (T // tm, 1, N_SEG),
               pl.BlockSpec((None, tm, Dm), lambda i, j, k: (k, i, 0)),
               pl.BlockSpec((Dm, Dm), lambda i, j, k: (0, k)),
               jax.ShapeDtypeStruct((T, Dm), f32), pl.BlockSpec((tm, Dm), lambda i, j, k: (i, 0)), (tm, Dm))


def _dz1_norm(dp, w_in_phys, h0, g, dh1):
    _, T, Dm = dp.shape
    tm = _tile(T, 688)
    return _mm_rmsnorm_bwd("dz1", dp, w_in_phys, (T // tm, 1, N_SEG),
                           pl.BlockSpec((None, tm, Dm), lambda i, j, k: (k, i, 0)),
                           pl.BlockSpec((Dm, Dm), lambda i, j, k: (0, k)), h0, g, dh1)


def _dz2_norm(dup, w_up, h1, g, dh2):
    _, T, _ = dup.shape
    tm = _tile(T, 688)
    tk = D_FF // 2
    return _mm_rmsnorm_bwd("dz2", dup, w_up, (T // tm, 1, 4),
                           pl.BlockSpec((None, tm, tk), lambda i, j, k: (k // 2, i, k % 2)),
                           pl.BlockSpec((D_MODEL, tk), lambda i, j, k: (0, k)), h1, g, dh2)


def _dw_in(z1, dp):
    _, T, Dm = dp.shape
    tk = _tile(T, 1376)
    return _mm("dw_in", z1, dp, "tn", (1, N_SEG, T // tk),
               pl.BlockSpec((tk, Dm), lambda i, j, k: (k, 0)),
               pl.BlockSpec((None, tk, Dm), lambda i, j, k: (j, k, 0)),
               jax.ShapeDtypeStruct((N_SEG, Dm, Dm), f32),
               pl.BlockSpec((None, Dm, Dm), lambda i, j, k: (j, 0, 0)), (Dm, Dm))


def _dz2(dup, w_up):
    _, T, _ = dup.shape
    tm = _tile(T, 1032)
    tk = D_FF // 2
    return _mm("dz2", dup, w_up, "nt", (T // tm, 1, 4),
               pl.BlockSpec((None, tm, tk), lambda i, j, k: (k // 2, i, k % 2)),
               pl.BlockSpec((D_MODEL, tk), lambda i, j, k: (0, k)),
               jax.ShapeDtypeStruct((T, D_MODEL), f32), pl.BlockSpec((tm, D_MODEL), lambda i, j, k: (i, 0)), (tm, D_MODEL))


def _dw_up(z2, dup):
    _, T, _ = dup.shape
    tn = D_FF // 2
    tk = _tile(T, 688)
    return _mm("dw_up", z2, dup, "tn", (1, N_CHIPS, T // tk),
               pl.BlockSpec((tk, D_MODEL), lambda i, j, k: (k, 0)),
               pl.BlockSpec((None, tk, tn), lambda i, j, k: (j // 2, k, j % 2)),
               jax.ShapeDtypeStruct((N_CHIPS, D_MODEL, tn), f32),
               pl.BlockSpec((None, D_MODEL, tn), lambda i, j, k: (j, 0, 0)), (D_MODEL, tn))


def _place():
    x, y, c = lax.axis_index("x"), lax.axis_index("y"), lax.axis_index("c")
    chips = [(1 - x, y), (x, 1 - y), (1 - x, 1 - y)]
    return x, y, c, chips


def _allgather_chips(arrs):
    n = len(arrs)

    def body(*refs):
        ins, outs = refs[:n], refs[n:2 * n]
        send, recv, local = refs[2 * n:]
        x, y, c, chips = _place()
        me = 2 * x + y

        def copy(a, k, slot):
            px, py = chips[k]
            return pltpu.make_async_remote_copy(src_ref=ins[a], dst_ref=outs[a].at[slot], send_sem=send.at[3 * a + k],
                                                recv_sem=recv.at[3 * a + k], device_id=(px, py, c), device_id_type=MESH)

        for a in range(n):
            pltpu.make_async_copy(ins[a], outs[a].at[me], local.at[a]).start()
            for k in range(3):
                copy(a, k, me).start()
        for a in range(n):
            for k, (px, py) in enumerate(chips):
                copy(a, k, 2 * px + py).wait_recv()
        for a in range(n):
            pltpu.make_async_copy(ins[a], outs[a].at[me], local.at[a]).wait()
            for k in range(3):
                copy(a, k, me).wait_send()

    return pl.pallas_call(
        body, name="allgather_chips", in_specs=[ANY] * n, out_specs=[ANY] * n,
        out_shape=[jax.ShapeDtypeStruct((N_CHIPS,) + a.shape, a.dtype) for a in arrs],
        scratch_shapes=[pltpu.SemaphoreType.DMA((3 * n,)), pltpu.SemaphoreType.DMA((3 * n,)), pltpu.SemaphoreType.DMA((n,))],
    )(*arrs)


def _allgather_split(arrs):
    n = len(arrs)

    def body(*refs):
        start, finish = _gather_split_steps(refs[:n], refs[n:2 * n], *refs[2 * n:])
        start()
        finish()

    return pl.pallas_call(
        body, name="allgather_split", in_specs=[ANY] * n, out_specs=[ANY] * n,
        out_shape=[jax.ShapeDtypeStruct((N_CHIPS,) + a.shape, a.dtype) for a in arrs],
        scratch_shapes=_gather_split_sems(n),
    )(*arrs)


def _gather_split_sems(n):
    return [pltpu.SemaphoreType.DMA((3 * n,)) for _ in range(4)]


def _gather_split_steps(ins, outs, send, recv, fsend, frecv):
    n = len(ins)

    def place():
        x, y, c, chips = _place()
        return x, y, c, chips, 2 * x + y

    def half(a, core):
        rh = ins[a].shape[0] // 2
        return pl.ds(core * rh, rh)

    def copy(a, k, slot):
        x, y, c, chips, _ = place()
        px, py = chips[k]
        return pltpu.make_async_remote_copy(src_ref=ins[a].at[half(a, c), :], dst_ref=outs[a].at[slot, half(a, c), :],
                                            send_sem=send.at[3 * a + k], recv_sem=recv.at[3 * a + k],
                                            device_id=(px, py, c), device_id_type=MESH)

    def forward(a, k, core):
        x, y, c, chips, _ = place()
        px, py = chips[k]
        rows = outs[a].at[2 * px + py, half(a, core), :]
        return pltpu.make_async_remote_copy(src_ref=rows, dst_ref=rows, send_sem=fsend.at[3 * a + k],
                                            recv_sem=frecv.at[3 * a + k], device_id=(x, y, 1 - c), device_id_type=MESH)

    def start():
        me = place()[4]
        for a in range(n):
            for k in range(3):
                copy(a, k, me).start()

    def finish():
        x, y, c, chips, me = place()
        for a in range(n):
            for k, (px, py) in enumerate(chips):
                copy(a, k, 2 * px + py).wait_recv()
                forward(a, k, c).start()
        for a in range(n):
            for k in range(3):
                forward(a, k, 1 - c).wait_recv()
        for a in range(n):
            for k in range(3):
                copy(a, k, me).wait_send()
                forward(a, k, c).wait_send()

    return start, finish


def _in_proj_gather(z1, w_in, shards):
    n = len(shards)
    T, K = z1.shape
    N = w_in.shape[1]
    tm = _tile(T, 1032)
    tn = 1024
    grid = (T // tm, N // tn)

    def body(a_ref, b_ref, *rest):
        ins, o_ref, outs, sems = rest[:n], rest[n], rest[n + 1:2 * n + 1], rest[2 * n + 1:]
        start, finish = _gather_split_steps(ins, outs, *sems)
        i, j = pl.program_id(0), pl.program_id(1)

        @pl.when((i == 0) & (j == 0))
        def _():
            start()

        o_ref[...] = jnp.dot(a_ref[...], b_ref[...], preferred_element_type=f32)

        @pl.when((i == grid[0] - 1) & (j == grid[1] - 1))
        def _():
            finish()

    res = pl.pallas_call(
        body, name="in_proj", grid=grid,
        in_specs=[pl.BlockSpec((tm, K), lambda i, j: (i, 0)), pl.BlockSpec((K, tn), lambda i, j: (0, j))] + [ANY] * n,
        out_specs=[pl.BlockSpec((tm, tn), lambda i, j: (i, j))] + [ANY] * n,
        out_shape=[jax.ShapeDtypeStruct((T, N), f32)] + [jax.ShapeDtypeStruct((N_CHIPS,) + a.shape, a.dtype) for a in shards],
        scratch_shapes=_gather_split_sems(n),
        compiler_params=_params("arbitrary", "arbitrary"),
    )(z1, w_in, *shards)
    return res[0], res[1:]


def _sibling_halves(parts, name="sibling_halves"):
    n = len(parts)

    def body(*refs):
        ins, outs = refs[:n], refs[n:2 * n]
        send, recv = refs[2 * n:]
        x, y, c, _ = _place()

        def copy(a):
            rh = ins[a].shape[1] // 2
            return pltpu.make_async_remote_copy(src_ref=ins[a].at[:, pl.ds((1 - c) * rh, rh), :], dst_ref=outs[a],
                                                send_sem=send.at[a], recv_sem=recv.at[a], device_id=(x, y, 1 - c),
                                                device_id_type=MESH)

        for a in range(n):
            copy(a).start()
        for a in range(n):
            copy(a).wait_recv()
        for a in range(n):
            copy(a).wait_send()

    return pl.pallas_call(
        body, name=name, in_specs=[ANY] * n, out_specs=[ANY] * n,
        out_shape=[jax.ShapeDtypeStruct((a.shape[0], a.shape[1] // 2, a.shape[2]), a.dtype) for a in parts],
        scratch_shapes=[pltpu.SemaphoreType.DMA((n,)), pltpu.SemaphoreType.DMA((n,))],
    )(*parts)


def _add_own_half(name, part, got, core):
    nchip, R, C = part.shape
    rh = R // 2
    tr = _tile(rh, 256, 2 * SUBLANES)
    nt = rh // tr

    def body(core_ref, a_ref, b_ref, o_ref):
        del core_ref
        o_ref[...] = (a_ref[...] + b_ref[...]).astype(o_ref.dtype)

    return pl.pallas_call(
        body, name=name,
        grid_spec=pltpu.PrefetchScalarGridSpec(
            num_scalar_prefetch=1, grid=(nchip, nt),
            in_specs=[pl.BlockSpec((None, tr, C), lambda j, i, core_ref: (j, core_ref[0] * nt + i, 0)),
                      pl.BlockSpec((None, tr, C), lambda j, i, core_ref: (j, i, 0))],
            out_specs=pl.BlockSpec((None, tr, C), lambda j, i, core_ref: (j, i, 0))),
        out_shape=jax.ShapeDtypeStruct((nchip, rh, C), bf16), compiler_params=_params("parallel", "parallel"),
    )(core, part, got)


def _add_own_half_w_in(part, got, core):
    _, R, C = part.shape
    rh = R // 2
    tr = _tile(rh, 256, 2 * SUBLANES)
    nt = rh // tr
    tn = 256
    per_seg = C // tn
    per_chip = IN_COLS // N_CHIPS // tn

    def src(j):
        return ((j // per_seg + N_SEG - 1) % N_SEG, j % per_seg)

    def body(core_ref, a_ref, b_ref, o_ref):
        del core_ref
        o_ref[...] = (a_ref[...] + b_ref[...]).astype(o_ref.dtype)

    return pl.pallas_call(
        body, name="add_half_w_in",
        grid_spec=pltpu.PrefetchScalarGridSpec(
            num_scalar_prefetch=1, grid=(IN_COLS // tn, nt),
            in_specs=[pl.BlockSpec((None, tr, tn), lambda j, i, core_ref: (src(j)[0], core_ref[0] * nt + i, src(j)[1])),
                      pl.BlockSpec((None, tr, tn), lambda j, i, core_ref: (src(j)[0], i, src(j)[1]))],
            out_specs=pl.BlockSpec((None, tr, tn), lambda j, i, core_ref: (j // per_chip, i, j % per_chip))),
        out_shape=jax.ShapeDtypeStruct((N_CHIPS, rh, IN_COLS // N_CHIPS), bf16), compiler_params=_params("parallel", "parallel"),
    )(core, part, got)


def _chip_exchange(sums):
    n = len(sums)

    def body(*refs):
        start, finish = _chip_exchange_steps(refs[:n], refs[n:2 * n], *refs[2 * n:])
        start()
        finish()

    return pl.pallas_call(
        body, name="chip_exchange", in_specs=[ANY] * n, out_specs=[ANY] * n,
        out_shape=[jax.ShapeDtypeStruct(a.shape, a.dtype) for a in sums],
        scratch_shapes=_chip_exchange_sems(n),
    )(*sums)


def _chip_exchange_sems(n):
    return [pltpu.SemaphoreType.DMA((3 * n,)), pltpu.SemaphoreType.DMA((3 * n,))]


def _chip_exchange_steps(ins, outs, send, recv):
    n = len(ins)

    def copy(a, k, own_slot):
        x, y, c, chips = _place()
        px, py = chips[k]
        slot = 2 * x + y if own_slot else 2 * px + py
        return pltpu.make_async_remote_copy(src_ref=ins[a].at[2 * px + py], dst_ref=outs[a].at[slot], send_sem=send.at[3 * a + k],
                                            recv_sem=recv.at[3 * a + k], device_id=(px, py, c), device_id_type=MESH)

    def start():
        for a in range(n):
            for k in range(3):
                copy(a, k, True).start()

    def finish():
        for a in range(n):
            for k in range(3):
                copy(a, k, False).wait_recv()
        for a in range(n):
            for k in range(3):
                copy(a, k, True).wait_send()

    return start, finish


def _sum_chips(name, slots, sums, where):
    nchip, rh, C = slots.shape
    tr = _tile(rh, 256, 2 * SUBLANES)
    nt = rh // tr

    def body(where_ref, own_ref, s1_ref, s2_ref, s3_ref, o_ref):
        me = where_ref[0]
        by_dist = [r[...].astype(f32) for r in (own_ref, s1_ref, s2_ref, s3_ref)]
        acc = None
        for j in range(nchip):
            d = me ^ j
            term = jnp.where(d == 0, by_dist[0], jnp.where(d == 1, by_dist[1], jnp.where(d == 2, by_dist[2], by_dist[3])))
            acc = term if acc is None else acc + term
        o_ref[...] = acc

    def other(d):
        return pl.BlockSpec((None, tr, C), lambda i, w: (w[0] ^ d, i, 0))

    return pl.pallas_call(
        body, name=name,
        grid_spec=pltpu.PrefetchScalarGridSpec(
            num_scalar_prefetch=1, grid=(nt,),
            in_specs=[other(0), other(1), other(2), other(3)],
            out_specs=pl.BlockSpec((tr, C), lambda i, w: (w[1] * nt + i, 0))),
        out_shape=jax.ShapeDtypeStruct((2 * rh, C), f32), compiler_params=_params("parallel"),
    )(where, sums, slots, slots, slots)


def _sum_slots(name, slots):
    ns, R, C = slots.shape
    tr = _tile(R, 256)

    def body(s_ref, o_ref):
        acc = s_ref[0]
        for j in range(1, ns):
            acc = acc + s_ref[j]
        o_ref[...] = acc

    return pl.pallas_call(
        body, name=name, grid=(R // tr,), in_specs=[pl.BlockSpec((ns, tr, C), lambda i: (0, i, 0))],
        out_specs=pl.BlockSpec((tr, C), lambda i: (i, 0)), out_shape=jax.ShapeDtypeStruct((R, C), f32),
        compiler_params=_params("parallel"),
    )(slots)


def _sibling_join(fulls):
    n = len(fulls)

    def body(*refs):
        ins, outs = refs[:n], refs[n:2 * n]
        send, recv = refs[2 * n:]
        x, y, c, _ = _place()

        def copy(a, core):
            rh = ins[a].shape[0] // 2
            rows = pl.ds(core * rh, rh)
            return pltpu.make_async_remote_copy(src_ref=ins[a].at[rows, :], dst_ref=outs[a].at[rows, :], send_sem=send.at[a],
                                                recv_sem=recv.at[a], device_id=(x, y, 1 - c), device_id_type=MESH)

        for a in range(n):
            copy(a, c).start()
        for a in range(n):
            copy(a, 1 - c).wait_recv()
        for a in range(n):
            copy(a, c).wait_send()

    return pl.pallas_call(
        body, name="sibling_join", in_specs=[ANY] * n, out_specs=[ANY] * n,
        out_shape=[jax.ShapeDtypeStruct(a.shape, a.dtype) for a in fulls],
        scratch_shapes=[pltpu.SemaphoreType.DMA((n,)), pltpu.SemaphoreType.DMA((n,))],
        input_output_aliases={a: a for a in range(n)},
    )(*fulls)


def _allgather_devices(v):
    def body(v_ref, out_ref, send, recv):
        x, y, c, chips = _place()
        me, sibling = (x, y, c), (x, y, 1 - c)

        def slot(px, py, pc):
            return out_ref.at[4 * px + 2 * py + pc]

        def copy(k, block, to, src=None):
            return pltpu.make_async_remote_copy(src_ref=slot(*block) if src is None else src, dst_ref=slot(*block),
                                                send_sem=send.at[k], recv_sem=recv.at[k], device_id=to, device_id_type=MESH)

        first = [copy(0, me, sibling, src=v_ref)] + [copy(1 + j, me, (*chip, c), src=v_ref) for j, chip in enumerate(chips)]
        for cp in first:
            cp.start()
        passed = [copy(4 + j, (*chip, c), sibling) for j, chip in enumerate(chips)]
        for j, chip in enumerate(chips):
            copy(1 + j, (*chip, c), me).wait_recv()
            passed[j].start()
        copy(0, sibling, me).wait_recv()
        for j, chip in enumerate(chips):
            copy(4 + j, (*chip, 1 - c), me).wait_recv()
        for cp in first + passed:
            cp.wait_send()

    return pl.pallas_call(
        body, name="allgather_devices", in_specs=[ANY], out_specs=ANY,
        out_shape=jax.ShapeDtypeStruct((N_DEV,) + v.shape, v.dtype),
        scratch_shapes=[pltpu.SemaphoreType.DMA((N_DEV - 1,)), pltpu.SemaphoreType.DMA((N_DEV - 1,))],
    )(v)


def _adamw(name, w, g, m, v):
    R, C = w.shape
    tr = _tile(R, 256)
    c1 = 1.0 / (1.0 - ADAM_B1 ** ADAM_STEP)
    c2 = 1.0 / (1.0 - ADAM_B2 ** ADAM_STEP)

    def body(w_ref, g_ref, m_ref, v_ref, d_ref, nm_ref, nv_ref):
        gv = g_ref[...]
        nm = ADAM_B1 * m_ref[...] + (1.0 - ADAM_B1) * gv
        nv = ADAM_B2 * v_ref[...] + (1.0 - ADAM_B2) * gv * gv
        d_ref[...] = -ADAM_LR * ((nm * c1) / (jnp.sqrt(nv * c2) + ADAM_EPS) + ADAM_WD * w_ref[...])
        nm_ref[...] = nm
        nv_ref[...] = nv

    row = pl.BlockSpec((tr, C), lambda i: (i, 0))
    sh = jax.ShapeDtypeStruct((R, C), f32)
    return pl.pallas_call(body, name=name, grid=(R // tr,), in_specs=[row] * 4, out_specs=[row] * 3,
                          out_shape=[sh, sh, sh], compiler_params=_params("parallel"))(w, g, m, v)


def _adamw_update(w, g, m, v):
    c1 = 1.0 / (1.0 - ADAM_B1 ** ADAM_STEP)
    c2 = 1.0 / (1.0 - ADAM_B2 ** ADAM_STEP)
    nm = ADAM_B1 * m + (1.0 - ADAM_B1) * g
    nv = ADAM_B2 * v + (1.0 - ADAM_B2) * g * g
    return -ADAM_LR * ((nm * c1) / (jnp.sqrt(nv * c2) + ADAM_EPS) + ADAM_WD * w), nm, nv


def _adamw_many(ws, gs, ms, vs):
    n = len(ws)

    def body(*refs):
        ins, outs = refs[:4 * n], refs[4 * n:]
        for a in range(n):
            d, nm, nv = _adamw_update(ins[a][...], ins[n + a][...], ins[2 * n + a][...], ins[3 * n + a][...])
            outs[a][...] = d
            outs[n + a][...] = nm
            outs[2 * n + a][...] = nv

    shapes = [jax.ShapeDtypeStruct(a.shape, f32) for a in ws]
    return pl.pallas_call(body, name="adamw_small", out_shape=shapes * 3)(*ws, *gs, *ms, *vs)


def _zoh(lr, li, log_dt, b_re, b_im):
    dt = jnp.exp(log_dt)[:, None]
    mag = jnp.exp(lr * dt)
    ab_re = mag * jnp.cos(li * dt)
    ab_im = mag * jnp.sin(li * dt)
    den = lr * lr + li * li
    nr = ab_re - 1.0
    coef_re = (nr * lr + ab_im * li) / den
    coef_im = (ab_im * lr - nr * li) / den
    bb_re = coef_re[..., None] * b_re - coef_im[..., None] * b_im
    bb_im = coef_re[..., None] * b_im + coef_im[..., None] * b_re
    return ab_re, ab_im, bb_re, bb_im


def _s5_tables(ab_re, ab_im, bb_re, bb_im, c_re, c_im, seg):
    eye = jnp.eye(SLAB_GROUPS, dtype=f32)

    def blk_in(bb):
        return jnp.einsum("sgph,gk->sghkp", bb.reshape(N_SLAB, SLAB_GROUPS, SSM_STATE, SSM_GROUP), eye).reshape(
            N_SLAB, SLAB_CH, SLAB_NS)

    def blk_out(cc):
        return jnp.einsum("sghp,gk->skpgh", cc.reshape(N_SLAB, SLAB_GROUPS, SSM_GROUP, SSM_STATE), eye).reshape(
            N_SLAB, SLAB_NS, SLAB_CH)

    bs = jnp.concatenate([blk_in(bb_re), blk_in(bb_im)], axis=2).astype(bf16)
    cs = jnp.concatenate([blk_out(c_re), blk_out(-c_im)], axis=1).astype(bf16)
    n = SSM_GROUPS * SSM_STATE
    pw = _power_table(jnp.stack([ab_re.reshape(1, n), ab_im.reshape(1, n)]), -(-seg // SUBLANES))
    return bs, cs, pw


def _power_table(ab, tiles):
    n = ab.shape[2]

    def body(a_ref, o_ref):
        row = lax.broadcasted_iota(jnp.int32, (SUBLANES, n), 0)
        ar, ai = a_ref[0], a_ref[1]
        tr, ti = jnp.broadcast_to(ar, (SUBLANES, n)), jnp.broadcast_to(ai, (SUBLANES, n))
        pr, pi = ar, ai
        for r in range(1, SUBLANES):
            pr, pi = pr * ar - pi * ai, pr * ai + pi * ar
            tr = jnp.where(row == r, pr, tr)
            ti = jnp.where(row == r, pi, ti)
        o_ref[0, 0:SUBLANES, :] = tr
        o_ref[1, 0:SUBLANES, :] = ti

        def step(j, carry):
            cr, ci = carry
            cr, ci = cr * pr - ci * pi, cr * pi + ci * pr
            o_ref[0, _rows8(j), :] = cr
            o_ref[1, _rows8(j), :] = ci
            return cr, ci

        lax.fori_loop(1, tiles, step, (tr, ti))

    return pl.pallas_call(body, name="power_table", out_shape=jax.ShapeDtypeStruct((2, SUBLANES * tiles, n), f32))(ab)


def _s5_table_grads(dbs, dcs, da):
    eye = jnp.eye(SLAB_GROUPS, dtype=f32)
    d6 = dbs.reshape(N_SLAB, SLAB_GROUPS, SSM_GROUP, 2, SLAB_GROUPS, SSM_STATE)
    dbb = jnp.einsum("sghrkp,gk->rsgph", d6, eye).reshape(2, SSM_GROUPS, SSM_STATE, SSM_GROUP)
    c6 = dcs.reshape(N_SLAB, 2, SLAB_GROUPS, SSM_STATE, SLAB_GROUPS, SSM_GROUP)
    dcc = jnp.einsum("srkpgh,gk->rsghp", c6, eye).reshape(2, SSM_GROUPS, SSM_GROUP, SSM_STATE)
    dab = da.transpose(1, 0, 2).reshape(2, SSM_GROUPS, SSM_STATE)
    return dab[0], dab[1], dbb[0], dbb[1], dcc[0], -dcc[1]


SMALL = ["mix_norm_g", "ssm_lambda_re", "ssm_lambda_im", "ssm_log_dt", "ssm_b_re", "ssm_b_im", "ssm_c_re", "ssm_c_im",
         "ssm_d", "hgrn_lb_logits", "hgrn_norm_g", "ffn_norm_g", "conv_b", "final_norm_g"]
SHARDED_SMALL = ["meta_tokens", "conv_w"]
BIG = ["w_in", "ssm_w_glu", "w_ssm_proj", "w_hgrn_proj", "w_out", "w_up", "w_down"]
WEIGHTS = ['meta_tokens', 'mix_norm_g', 'w_in', 'ssm_lambda_re', 'ssm_lambda_im', 'ssm_log_dt', 'ssm_b_re', 'ssm_b_im',
           'ssm_c_re', 'ssm_c_im', 'ssm_d', 'ssm_w_glu', 'w_ssm_proj', 'hgrn_lb_logits', 'hgrn_norm_g', 'w_hgrn_proj',
           'w_out', 'ffn_norm_g', 'w_up', 'conv_w', 'conv_b', 'w_down', 'final_norm_g']


LATER = [k for k in BIG if k != "w_in"]


def _full_weights(gathered, shards, chip):
    Dm = D_MODEL
    g = {k: lax.dynamic_update_slice(gathered[k], shards[k][None], (chip, 0, 0)) for k in gathered}
    full = {}
    for k, v in g.items():
        if k == "w_in":
            full[k] = jnp.roll(v.transpose(1, 0, 2).reshape(Dm, IN_COLS), -Dm, axis=1)
        elif k == "w_up":
            full[k] = v.transpose(1, 0, 2).reshape(Dm, 2 * D_FF)
        else:
            full[k] = v.reshape(-1, Dm)
    return full


def _local_grads(x, tgt, meta, w, full, shards, chip, core):
    B, S, Dm = x.shape
    L = S + N_META
    T = B * L
    h0 = jnp.concatenate([jnp.broadcast_to(meta[None], (B, N_META, Dm)), x], axis=1).reshape(T, Dm)

    lb_all = jax.nn.softmax(w["hgrn_lb_logits"], axis=0)
    lb = lb_all[0:1]
    zoh_out, zoh_vjp = jax.vjp(_zoh, w["ssm_lambda_re"][0], w["ssm_lambda_im"][0], w["ssm_log_dt"][0],
                               w["ssm_b_re"][0], w["ssm_b_im"][0])
    bs, cs, pw = _s5_tables(*zoh_out, w["ssm_c_re"][0], w["ssm_c_im"][0], L // SUBLANES)

    z1 = _rmsnorm_fwd("mix_norm", h0, w["mix_norm_g"])
    p, gathered = _in_proj_gather(z1, full["w_in"], [shards[k] for k in LATER])
    full = {**full, **_full_weights(dict(zip(LATER, gathered)), shards, chip)}
    ya0 = _s5_fwd(p, bs, cs, pw, w["ssm_d"], B, L)
    gl, ya = _glu_proj_fwd(ya0, full["ssm_w_glu"])
    yb = _hgrn_fwd(p, lb, w["hgrn_norm_g"], B, L)
    pa, pb, merged = _proj_merge_fwd(ya, yb, full["w_ssm_proj"], full["w_hgrn_proj"], p)
    h1, z2 = _out_proj_norm(merged, full["w_out"], h0, w["ffn_norm_g"])
    up = _mm_rows("up_proj", z2, full["w_up"], "nn", f32, D_FF // 2)
    ff = _conv_fwd(up, full["conv_w"], w["conv_b"], B, L)
    h2 = _mm_rows("down_proj", ff, full["w_down"], "nn", f32, 1024, res=h1, tk=D_FF // 2)

    h2x = h2.reshape(B, L, Dm)[:, N_META:].reshape(B * S, Dm)
    dh2x, loss, d_final_g = _final_loss(h2x, tgt.reshape(B * S, Dm), w["final_norm_g"].reshape(1, Dm))
    dh2 = jnp.pad(dh2x.reshape(B, S, Dm), ((0, 0), (N_META, 0), (0, 0))).reshape(T, Dm)

    dff = _mm_rows("d_ff", dh2, full["w_down"], "nt", f32, D_FF // 2)
    g_w_down = _mm_wgrad("dw_down", ff, dh2, tn=512)
    dup, dconv = _conv_bwd(up, dff, full["conv_w"], w["conv_b"], B, L)
    g_w_up = _dw_up(z2, dup)
    dh1, d_ffn_g = _dz2_norm(dup, full["w_up"], h1, w["ffn_norm_g"], dh2)

    g_w_out = _mm_wgrad("dw_out", merged, dh1)
    dpa, dpb, dp = _merge_bwd_fused(dh1, full["w_out"], p, pa, pb)
    dgl, dya0_direct = _glu_bwd_fused(dpa, full["w_ssm_proj"], ya0, gl)
    g_w_ssm_proj = _mm_wgrad("dw_ssm_proj", ya, dpa)
    dyb = _mm_rows("d_yb", dpb, full["w_hgrn_proj"], "nt", f32, 1024)
    g_w_hgrn_proj = _mm_wgrad("dw_hgrn_proj", yb, dpb)
    dp, d_lb, d_hgrn_g = _hgrn_bwd(p, dyb, dp, lb, w["hgrn_norm_g"], B, L)
    dya0 = _mm_rows("d_ya0", dgl, full["ssm_w_glu"], "nt", f32, 1024, res=dya0_direct)
    g_w_glu = _mm_wgrad("dw_glu", ya0, dgl)
    parts = {
        "ssm_w_glu": g_w_glu.reshape(N_CHIPS, Dm // N_CHIPS, Dm), "w_ssm_proj": g_w_ssm_proj.reshape(N_CHIPS, Dm // N_CHIPS, Dm),
        "w_hgrn_proj": g_w_hgrn_proj.reshape(N_CHIPS, Dm // N_CHIPS, Dm), "w_out": g_w_out.reshape(N_CHIPS, Dm // N_CHIPS, Dm),
        "w_up": g_w_up, "w_down": g_w_down.reshape(N_CHIPS, D_FF // N_CHIPS, Dm),
    }
    got = _sibling_halves([parts[k] for k in LATER])
    sums = {k: _add_own_half("add_half_" + k, parts[k], gt, core) for k, gt in zip(LATER, got)}
    (dp, dbs, dcs, da, d_skip), slots_later = _s5_bwd(p, dya0, dp, bs, cs, pw, w["ssm_d"], B, L, [sums[k] for k in LATER])
    slots = dict(zip(LATER, slots_later))
    g_w_in = _dw_in(z1, dp)
    dh0, d_mix_g = _dz1_norm(dp, full["w_in"], h0, w["mix_norm_g"], dh1)

    dh0 = dh0.reshape(B, L, Dm)
    grad_x = dh0[:, N_META:]
    d_meta = _meta_grad(dh0[:, :N_META])

    d_ab_re, d_ab_im, d_bb_re, d_bb_im, d_c_re, d_c_im = _s5_table_grads(dbs, dcs, da)
    d_lr, d_li, d_log_dt, d_b_re, d_b_im = zoh_vjp((d_ab_re, d_ab_im, d_bb_re, d_bb_im))
    sm0, sm1 = lb_all[0:1], lb_all[1:2]
    d_logits = jnp.concatenate([sm0 * (1.0 - sm0) * d_lb, -sm0 * sm1 * d_lb], axis=0)
    small = {
        "meta_tokens": d_meta, "mix_norm_g": d_mix_g, "ssm_lambda_re": d_lr[None], "ssm_lambda_im": d_li[None],
        "ssm_log_dt": d_log_dt[None], "ssm_b_re": d_b_re[None], "ssm_b_im": d_b_im[None], "ssm_c_re": d_c_re[None],
        "ssm_c_im": d_c_im[None], "ssm_d": d_skip, "hgrn_lb_logits": d_logits, "hgrn_norm_g": d_hgrn_g,
        "ffn_norm_g": d_ffn_g, "conv_w": dconv[:, 0:3, :].transpose(1, 0, 2).reshape(3, 2 * D_FF),
        "conv_b": dconv[:, 3, :].reshape(1, 2 * D_FF), "final_norm_g": d_final_g.reshape(Dm),
    }
    sums["w_in"] = _add_own_half_w_in(g_w_in, _sibling_halves([g_w_in], "sibling_halves_w_in")[0], core)
    slots["w_in"] = _chip_exchange([sums["w_in"]])[0]
    return loss, grad_x, sums, slots, small


PACK_ROWS = 256


def _pack(parts):
    flat = jnp.concatenate([parts[k].reshape(-1) for k in parts])
    n = flat.shape[0]
    rows = -(-n // (PACK_ROWS * LANES)) * PACK_ROWS
    flat = jnp.pad(flat, (0, rows * LANES - n))
    return flat.reshape(rows, LANES)


def _unpack(packed, like):
    flat = packed.reshape(-1)
    out, o = {}, 0
    for k, ref in like.items():
        n = math.prod(ref.shape)
        out[k] = flat[o:o + n].reshape(ref.shape)
        o += n
    return out


def kernel(x, meta_tokens, mix_norm_g, w_in, ssm_lambda_re, ssm_lambda_im, ssm_log_dt, ssm_b_re, ssm_b_im, ssm_c_re, ssm_c_im, ssm_d, ssm_w_glu, w_ssm_proj, hgrn_lb_logits, hgrn_norm_g, w_hgrn_proj, w_out, ffn_norm_g, w_up, conv_w, conv_b, w_down, final_norm_g, loss_target, m_meta_tokens, m_mix_norm_g, m_w_in, m_ssm_lambda_re, m_ssm_lambda_im, m_ssm_log_dt, m_ssm_b_re, m_ssm_b_im, m_ssm_c_re, m_ssm_c_im, m_ssm_d, m_ssm_w_glu, m_w_ssm_proj, m_hgrn_lb_logits, m_hgrn_norm_g, m_w_hgrn_proj, m_w_out, m_ffn_norm_g, m_w_up, m_conv_w, m_conv_b, m_w_down, m_final_norm_g, v_meta_tokens, v_mix_norm_g, v_w_in, v_ssm_lambda_re, v_ssm_lambda_im, v_ssm_log_dt, v_ssm_b_re, v_ssm_b_im, v_ssm_c_re, v_ssm_c_im, v_ssm_d, v_ssm_w_glu, v_w_ssm_proj, v_hgrn_lb_logits, v_hgrn_norm_g, v_w_hgrn_proj, v_w_out, v_ffn_norm_g, v_w_up, v_conv_w, v_conv_b, v_w_down, v_final_norm_g):
    args = dict(locals())
    w = {k: args[k] for k in WEIGHTS}
    mom = {k: args["m_" + k] for k in WEIGHTS}
    var = {k: args["v_" + k] for k in WEIGHTS}
    Dm = D_MODEL
    cx, cy, cc = lax.axis_index("x"), lax.axis_index("y"), lax.axis_index("c")
    chip = 2 * cx + cy

    shards = {k: w[k][0].astype(bf16) for k in BIG}
    g_meta, g_cw = _allgather_chips([w["meta_tokens"], w["conv_w"][0]])
    full = _full_weights({"w_in": _allgather_split([shards["w_in"]])[0]}, shards, chip)
    full["conv_w"] = g_cw.transpose(1, 0, 2).reshape(3, 2 * D_FF)
    meta_full = g_meta.transpose(1, 0, 2).reshape(N_META, Dm)

    core = cc.reshape(1).astype(jnp.int32)
    loss_part, grad_x, sums, slots, small = _local_grads(x, loss_target, meta_full, w, full, shards, chip, core)

    where = jnp.stack([chip, cc]).astype(jnp.int32)
    fulls = [_sum_chips("sum_chips_" + k, slots[k], sums[k], where) for k in BIG]
    g_big = dict(zip(BIG, _sibling_join(fulls)))

    small_all = dict(small)
    small_all["loss"] = loss_part[0, 0:1]
    packed = _pack(small_all)
    slots_dev = lax.dynamic_update_slice(_allgather_devices(packed), packed[None], (2 * chip + cc, 0, 0))
    reduced = _unpack(_sum_slots("sum_devices", slots_dev), small_all)
    loss = reduced.pop("loss")[0]
    mcols = Dm // N_CHIPS
    ccols = 2 * D_FF // N_CHIPS
    grads = {k: reduced[k] for k in SMALL}
    grads["meta_tokens"] = lax.dynamic_slice(reduced["meta_tokens"], (0, chip * mcols), (N_META, mcols))
    grads["conv_w"] = lax.dynamic_slice(reduced["conv_w"], (0, chip * ccols), (3, ccols))[None]
    for k in BIG:
        grads[k] = g_big[k][None]

    delta, new_m, new_v = {}, {}, {}
    for k in BIG:
        shp = w[k].shape
        d, nm, nv = _adamw("adamw_" + k, w[k][0], grads[k][0], mom[k][0], var[k][0])
        delta[k], new_m[k], new_v[k] = d.reshape(shp), nm.reshape(shp), nv.reshape(shp)
    rest = SMALL + SHARDED_SMALL

    def flat2(a):
        return a.reshape(-1, a.shape[-1])

    outs = _adamw_many(*[[flat2(t[k]) for k in rest] for t in (w, grads, mom, var)])
    n = len(rest)
    for j, dst in enumerate((delta, new_m, new_v)):
        dst.update({k: o.reshape(w[k].shape) for k, o in zip(rest, outs[j * n:(j + 1) * n])})

    return (loss, grad_x, *[grads[k].reshape(w[k].shape) for k in WEIGHTS], *[delta[k] for k in WEIGHTS],
            *[new_m[k] for k in WEIGHTS], *[new_v[k] for k in WEIGHTS])
```

```python
import math

import jax
import jax.numpy as jnp
from jax import lax
from jax.experimental import pallas as pl
from jax.experimental.pallas import tpu as pltpu

f32 = jnp.float32
bf16 = jnp.bfloat16

D_MODEL = 1024
N_META = 16
SSM_GROUP = 16
SSM_GROUPS = 64
SSM_STATE = 64
SLAB_GROUPS = 8
N_SLAB = SSM_GROUPS // SLAB_GROUPS
SLAB_CH = SLAB_GROUPS * SSM_GROUP
SLAB_NS = SLAB_GROUPS * SSM_STATE
HEADS = 8
HEAD_DIM = 128
CHUNK = 16
D_FF = 2816
IN_COLS = 7168
EPS = 1e-6
SUBLANES = 8
LANES = 128
N_CHIPS = 4
N_DEV = 8
ADAM_LR, ADAM_B1, ADAM_B2, ADAM_EPS, ADAM_WD, ADAM_STEP = 0.001, 0.9, 0.999, 1e-08, 0.01, 10
MESH = pl.DeviceIdType.MESH
ANY = pl.BlockSpec(memory_space=pl.ANY)

SEG_Q, SEG_F, SEG_I, SEG_OG, SEG_GA, SEG_GB, SEG_U = range(7)
N_SEG = 7


def _tile(n, target, mult=SUBLANES):
    best = None
    for d in range(mult, min(n, target) + 1, mult):
        if n % d == 0:
            best = d
    return n if best is None else best


def _params(*sem):
    return pltpu.CompilerParams(dimension_semantics=sem)


def _sigmoid(x):
    return 1.0 / (1.0 + jnp.exp(-x))


_DIMS = {"nn": (((1,), (0,)), ((), ())), "nt": (((1,), (1,)), ((), ())), "tn": (((0,), (0,)), ((), ()))}


def _mm(name, a, b, dims, grid, a_spec, b_spec, out_shape, out_spec, acc_shape, res=None, res_spec=None):
    nk = grid[2]
    dn = _DIMS[dims]

    def body(*refs):
        if res is None:
            a_ref, b_ref, o_ref, acc = refs
        else:
            a_ref, b_ref, r_ref, o_ref, acc = refs
        k = pl.program_id(2)

        @pl.when(k == 0)
        def _():
            acc[...] = jnp.zeros_like(acc)

        acc[...] += lax.dot_general(a_ref[...].astype(bf16), b_ref[...].astype(bf16), dn, preferred_element_type=f32)

        @pl.when(k == nk - 1)
        def _():
            r = acc[...]
            if res is not None:
                r = r + r_ref[...]
            o_ref[...] = r.astype(o_ref.dtype)

    ins = [a, b] + ([] if res is None else [res])
    specs = [a_spec, b_spec] + ([] if res is None else [res_spec])
    return pl.pallas_call(
        body, name=name, grid=grid, in_specs=specs, out_specs=out_spec, out_shape=out_shape,
        scratch_shapes=[pltpu.VMEM(acc_shape, f32)],
        compiler_params=_params("parallel", "parallel", "arbitrary"),
    )(*ins)


def _mm_rows(name, a, w, dims, out_dtype, tn, res=None, tk=None):
    T, K = a.shape
    N = w.shape[1] if dims == "nn" else w.shape[0]
    tm = _tile(T, 1032)
    tk = K if tk is None else tk
    grid = (T // tm, N // tn, K // tk)
    a_spec = pl.BlockSpec((tm, tk), lambda i, j, k: (i, k))
    if dims == "nn":
        b_spec = pl.BlockSpec((tk, tn), lambda i, j, k: (k, j))
    else:
        b_spec = pl.BlockSpec((tn, tk), lambda i, j, k: (j, k))
    o_spec = pl.BlockSpec((tm, tn), lambda i, j, k: (i, j))
    return _mm(name, a, w, dims, grid, a_spec, b_spec, jax.ShapeDtypeStruct((T, N), out_dtype), o_spec, (tm, tn),
               res=res, res_spec=None if res is None else o_spec)


def _mm_fused(name, pairs, dims, extras, epilogue, outs, rows=()):
    T, K = pairs[0][0].shape
    N = pairs[0][1].shape[1] if dims == "nn" else pairs[0][1].shape[0]
    tm = _tile(T, 344)
    tn = N
    grid = (T // tm, N // tn)
    npair, nex = len(pairs), len(extras) + len(rows)
    dn = _DIMS[dims]

    def body(*refs):
        ab = refs[:2 * npair]
        ex = refs[2 * npair:2 * npair + nex]
        o_refs = refs[2 * npair + nex:]
        accs = [lax.dot_general(ab[2 * q][...].astype(bf16), ab[2 * q + 1][...].astype(bf16), dn, preferred_element_type=f32)
                for q in range(npair)]
        vals = epilogue(accs, [e[...] for e in ex])
        for o_ref, v in zip(o_refs, vals):
            if isinstance(v, (list, tuple)):
                for s_, vs in enumerate(v):
                    o_ref[s_] = vs.astype(o_ref.dtype)
            else:
                o_ref[...] = v.astype(o_ref.dtype)

    ins, specs = [], []
    for a, w in pairs:
        ins += [a, w]
        specs.append(pl.BlockSpec((tm, K), lambda i, j: (i, 0)))
        specs.append(pl.BlockSpec((K, tn), lambda i, j: (0, j)) if dims == "nn" else pl.BlockSpec((tn, K), lambda i, j: (j, 0)))
    for arr, off in extras:
        ins.append(arr)
        specs.append(pl.BlockSpec((tm, tn), lambda i, j, off=off: (i, off + j)))
    for arr in rows:
        ins.append(arr)
        specs.append(pl.BlockSpec((1, tn), lambda i, j: (0, j)))
    shapes, ospecs = [], []
    for o in outs:
        if isinstance(o, tuple):
            dt, nseg, total, blk = o
            shapes.append(jax.ShapeDtypeStruct((total, T, N), dt))
            ospecs.append(pl.BlockSpec((nseg, tm, tn), lambda i, j, blk=blk: (blk, i, j)))
        else:
            shapes.append(jax.ShapeDtypeStruct((T, N), o))
            ospecs.append(pl.BlockSpec((tm, tn), lambda i, j: (i, j)))
    return pl.pallas_call(body, name=name, grid=grid, in_specs=specs, out_specs=ospecs, out_shape=shapes,
                          compiler_params=_params("parallel", "parallel"))(*ins)


def _glu_proj_fwd(ya0, w_glu):
    def epi(accs, tiles):
        return accs[0], tiles[0] * _sigmoid(accs[0])

    return _mm_fused("glu_proj", [(ya0, w_glu)], "nn", [(ya0, 0)], epi, [f32, bf16])


def _proj_merge_fwd(ya, yb, w_sp, w_hp, p):
    def epi(accs, tiles):
        return accs[0], accs[1], _sigmoid(tiles[0]) * accs[0] + _sigmoid(tiles[1]) * accs[1]

    return _mm_fused("proj_merge", [(ya, w_sp), (yb, w_hp)], "nn", [(p, SEG_GA), (p, SEG_GB)], epi, [f32, f32, bf16])


def _merge_bwd_fused(dh1, w_out, p, pa, pb):
    def epi(accs, tiles):
        d = accs[0]
        sa, sb = _sigmoid(tiles[0]), _sigmoid(tiles[1])
        return d * sa, d * sb, [d * tiles[2] * sa * (1.0 - sa), d * tiles[3] * sb * (1.0 - sb)]

    return _mm_fused("d_merged", [(dh1, w_out)], "nt", [(p, SEG_GA), (p, SEG_GB), (pa, 0), (pb, 0)], epi,
                     [bf16, bf16, (bf16, 2, N_SEG, SEG_GA // 2)])


def _out_proj_norm(merged, w_out, h0, g):
    def epi(accs, tiles):
        h1 = tiles[0] + accs[0]
        r = lax.rsqrt(jnp.mean(h1 * h1, axis=-1, keepdims=True) + EPS)
        return h1, h1 * r * tiles[1]

    return _mm_fused("out_proj", [(merged, w_out)], "nn", [(h0, 0)], epi, [f32, bf16], rows=[g])


def _mm_rmsnorm_bwd(name, a, b, grid, a_spec, b_spec, x, g, dres):
    T, Dm = x.shape
    tm = T // grid[0]
    nk = grid[2]

    def body(a_ref, b_ref, x_ref, g_ref, dres_ref, dx_ref, dg_ref, acc):
        i, k = pl.program_id(0), pl.program_id(2)

        @pl.when(k == 0)
        def _():
            acc[...] = jnp.zeros_like(acc)

        @pl.when((i == 0) & (k == 0))
        def _():
            dg_ref[...] = jnp.zeros_like(dg_ref)

        acc[...] += lax.dot_general(a_ref[...].astype(bf16), b_ref[...].astype(bf16), _DIMS["nt"], preferred_element_type=f32)

        @pl.when(k == nk - 1)
        def _():
            xv = x_ref[...]
            r = lax.rsqrt(jnp.mean(xv * xv, axis=-1, keepdims=True) + EPS)
            xn = xv * r
            dzv = acc[...]
            dzg = dzv * g_ref[...]
            dx_ref[...] = dres_ref[...] + r * (dzg - xn * jnp.mean(dzg * xn, axis=-1, keepdims=True))
            dg_ref[...] += jnp.sum(dzv * xn, axis=0, keepdims=True)

    row = pl.BlockSpec((tm, Dm), lambda i, j, k: (i, 0))
    par = pl.BlockSpec((1, Dm), lambda i, j, k: (0, 0))
    return pl.pallas_call(
        body, name=name, grid=grid, in_specs=[a_spec, b_spec, row, par, row], out_specs=[row, par],
        out_shape=[jax.ShapeDtypeStruct((T, Dm), f32), jax.ShapeDtypeStruct((1, Dm), f32)],
        scratch_shapes=[pltpu.VMEM((tm, Dm), f32)],
        compiler_params=_params("arbitrary", "arbitrary", "arbitrary"),
    )(a, b, x, g, dres)


def _glu_bwd_fused(dpa, w_sp, ya0, gl):
    def epi(accs, tiles):
        d = accs[0]
        s = _sigmoid(tiles[1])
        return d * tiles[0] * s * (1.0 - s), d * s

    return _mm_fused("d_ya", [(dpa, w_sp)], "nt", [(ya0, 0), (gl, 0)], epi, [bf16, f32])


def _mm_wgrad(name, a, g, tn=None):
    T, K = a.shape
    N = g.shape[1]
    tk = _tile(T, 688)
    tn = N if tn is None else tn
    grid = (1, N // tn, T // tk)
    a_spec = pl.BlockSpec((tk, K), lambda i, j, k: (k, 0))
    g_spec = pl.BlockSpec((tk, tn), lambda i, j, k: (k, j))
    o_spec = pl.BlockSpec((K, tn), lambda i, j, k: (0, j))
    return _mm(name, a, g, "tn", grid, a_spec, g_spec, jax.ShapeDtypeStruct((K, N), f32), o_spec, (K, tn))


def _rmsnorm_fwd(name, x, g):
    T, Dm = x.shape
    tr = _tile(T, 688)

    def body(x_ref, g_ref, z_ref):
        xv = x_ref[...]
        r = lax.rsqrt(jnp.mean(xv * xv, axis=-1, keepdims=True) + EPS)
        z_ref[...] = (xv * r * g_ref[...]).astype(z_ref.dtype)

    return pl.pallas_call(
        body, name=name, grid=(T // tr,),
        in_specs=[pl.BlockSpec((tr, Dm), lambda i: (i, 0)), pl.BlockSpec((1, Dm), lambda i: (0, 0))],
        out_specs=pl.BlockSpec((tr, Dm), lambda i: (i, 0)),
        out_shape=jax.ShapeDtypeStruct((T, Dm), bf16), compiler_params=_params("parallel"),
    )(x, g)


def _final_loss(h2x, tgt, g):
    T, Dm = h2x.shape
    tr = _tile(T, 512)

    def body(h_ref, t_ref, g_ref, dh_ref, loss_ref, dg_ref):
        hv = h_ref[...]
        r = lax.rsqrt(jnp.mean(hv * hv, axis=-1, keepdims=True) + EPS)
        xn = hv * r
        gv = g_ref[...]
        err = xn * gv - t_ref[...]
        dy = err * (1.0 / Dm)
        dyg = dy * gv
        dh_ref[...] = r * (dyg - xn * jnp.mean(dyg * xn, axis=-1, keepdims=True))

        @pl.when(pl.program_id(0) == 0)
        def _():
            dg_ref[...] = jnp.zeros_like(dg_ref)
            loss_ref[...] = jnp.zeros_like(loss_ref)

        dg_ref[...] += jnp.sum(dy * xn, axis=0, keepdims=True)
        loss_ref[...] += jnp.sum(err * err) * (0.5 / Dm)

    row = pl.BlockSpec((tr, Dm), lambda i: (i, 0))
    par = pl.BlockSpec((1, Dm), lambda i: (0, 0))
    return pl.pallas_call(
        body, name="final_loss", grid=(T // tr,), in_specs=[row, row, par],
        out_specs=[row, pl.BlockSpec((1, LANES), lambda i: (0, 0)), par],
        out_shape=[jax.ShapeDtypeStruct((T, Dm), f32), jax.ShapeDtypeStruct((1, LANES), f32), jax.ShapeDtypeStruct((1, Dm), f32)],
        compiler_params=_params("arbitrary"),
    )(h2x, tgt, g)


def _meta_grad(dh0_meta):
    B = dh0_meta.shape[0]

    def body(d_ref, o_ref):
        acc = d_ref[0]
        for b in range(1, B):
            acc = acc + d_ref[b]
        o_ref[...] = acc

    return pl.pallas_call(body, name="meta_grad", out_shape=jax.ShapeDtypeStruct(dh0_meta.shape[1:], f32))(dh0_meta)


def _shift_down(x, k, row):
    return jnp.where(row >= k, pltpu.roll(x, k, 0), 0.0)


def _conv_fwd(up, conv_w, conv_b, B, L):
    tc = 256
    nt = D_FF // tc

    def body(xa_ref, xb_ref, wa_ref, wb_ref, ba_ref, bb_ref, o_ref):
        row = lax.broadcasted_iota(jnp.int32, (L, tc), 0)

        def conv(x_ref, w_ref, b_ref):
            x = x_ref[...]
            return (b_ref[...] + w_ref[0:1, :] * _shift_down(x, 2, row) + w_ref[1:2, :] * _shift_down(x, 1, row)
                    + w_ref[2:3, :] * x)

        a = conv(xa_ref, wa_ref, ba_ref)
        b = conv(xb_ref, wb_ref, bb_ref)
        o_ref[...] = (a * _sigmoid(a) * b).astype(o_ref.dtype)

    return pl.pallas_call(
        body, name="conv_fwd", grid=(B, nt),
        in_specs=[pl.BlockSpec((L, tc), lambda b, j: (b, j)), pl.BlockSpec((L, tc), lambda b, j: (b, j + nt)),
                  pl.BlockSpec((3, tc), lambda b, j: (0, j)), pl.BlockSpec((3, tc), lambda b, j: (0, j + nt)),
                  pl.BlockSpec((1, tc), lambda b, j: (0, j)), pl.BlockSpec((1, tc), lambda b, j: (0, j + nt))],
        out_specs=pl.BlockSpec((L, tc), lambda b, j: (b, j)),
        out_shape=jax.ShapeDtypeStruct((B * L, D_FF), bf16), compiler_params=_params("parallel", "parallel"),
    )(up, up, conv_w, conv_w, conv_b, conv_b)


CONV_ROWS = 2 * SUBLANES


def _rows16(i):
    return pl.ds(pl.multiple_of(i * CONV_ROWS, CONV_ROWS), CONV_ROWS)


def _conv_taps(x_ref, i, row):
    x = x_ref[_rows16(i), :]
    live = jnp.where(i > 0, 1.0, 0.0)
    r0 = jnp.maximum(i * CONV_ROWS, 2)
    p1 = x_ref[pl.ds(r0 - 1, 1), :] * live
    p2 = x_ref[pl.ds(r0 - 2, 1), :] * live
    x1 = jnp.where(row == 0, p1, pltpu.roll(x, 1, 0))
    x2 = jnp.where(row == 0, p2, jnp.where(row == 1, p1, pltpu.roll(x, 2, 0)))
    return x, x1, x2


def _conv_bwd(up, dff, conv_w, conv_b, B, L):
    tc = 256
    nt = D_FF // tc
    n = L // CONV_ROWS

    def body(xa_ref, xb_ref, d_ref, wa_ref, wb_ref, ba_ref, bb_ref, dup_ref, dw_ref, ga_ref, gb_ref):
        row = lax.broadcasted_iota(jnp.int32, (CONV_ROWS, tc), 0)

        @pl.when(pl.program_id(1) == 0)
        def _():
            dw_ref[...] = jnp.zeros_like(dw_ref)

        zero_tail = jnp.zeros((CONV_ROWS, tc), f32)
        ga_ref[L:L + CONV_ROWS, :] = zero_tail
        gb_ref[L:L + CONV_ROWS, :] = zero_tail

        def fold(v):
            return v[0:SUBLANES, :] + v[SUBLANES:CONV_ROWS, :]

        def step(i, acc):
            taps_a = _conv_taps(xa_ref, i, row)
            taps_b = _conv_taps(xb_ref, i, row)
            a = ba_ref[...] + wa_ref[0:1, :] * taps_a[2] + wa_ref[1:2, :] * taps_a[1] + wa_ref[2:3, :] * taps_a[0]
            b = bb_ref[...] + wb_ref[0:1, :] * taps_b[2] + wb_ref[1:2, :] * taps_b[1] + wb_ref[2:3, :] * taps_b[0]
            s = _sigmoid(a)
            d = d_ref[_rows16(i), :]
            g_a = d * b * s * (1.0 + a * (1.0 - s))
            g_b = d * a * s
            ga_ref[_rows16(i), :] = g_a
            gb_ref[_rows16(i), :] = g_b
            new = []
            for g, (x, x1, x2) in ((g_a, taps_a), (g_b, taps_b)):
                new += [fold(g * x2), fold(g * x1), fold(g * x), fold(g)]
            return tuple(o + v for o, v in zip(acc, new))

        z = jnp.zeros((SUBLANES, tc), f32)
        acc = _repeat_loop(n, step, (z,) * 8)
        for h in range(2):
            for t in range(4):
                dw_ref[h, t:t + 1, :] += jnp.sum(acc[4 * h + t], axis=0, keepdims=True)

        def back(i, c):
            for h, (g_ref, w_ref) in enumerate(((ga_ref, wa_ref), (gb_ref, wb_ref))):
                g = g_ref[_rows16(i), :]
                n1 = g_ref[pl.ds(i * CONV_ROWS + CONV_ROWS, 1), :]
                n2 = g_ref[pl.ds(i * CONV_ROWS + CONV_ROWS + 1, 1), :]
                u1 = jnp.where(row == CONV_ROWS - 1, n1, pltpu.roll(g, CONV_ROWS - 1, 0))
                u2 = jnp.where(row == CONV_ROWS - 1, n2, jnp.where(row == CONV_ROWS - 2, n1, pltpu.roll(g, CONV_ROWS - 2, 0)))
                dup_ref[h, _rows16(i), :] = (w_ref[2:3, :] * g + w_ref[1:2, :] * u1 + w_ref[0:1, :] * u2).astype(dup_ref.dtype)
            return c

        _repeat_loop(n, back, 0)

    return pl.pallas_call(
        body, name="conv_bwd", grid=(nt, B),
        in_specs=[pl.BlockSpec((L, tc), lambda j, b: (b, j)), pl.BlockSpec((L, tc), lambda j, b: (b, j + nt)),
                  pl.BlockSpec((L, tc), lambda j, b: (b, j)),
                  pl.BlockSpec((3, tc), lambda j, b: (0, j)), pl.BlockSpec((3, tc), lambda j, b: (0, j + nt)),
                  pl.BlockSpec((1, tc), lambda j, b: (0, j)), pl.BlockSpec((1, tc), lambda j, b: (0, j + nt))],
        out_specs=[pl.BlockSpec((2, L, tc), lambda j, b: (0, b, j)), pl.BlockSpec((2, SUBLANES, tc), lambda j, b: (0, 0, j))],
        out_shape=[jax.ShapeDtypeStruct((2, B * L, D_FF), bf16), jax.ShapeDtypeStruct((2, SUBLANES, D_FF), f32)],
        scratch_shapes=[pltpu.VMEM((L + CONV_ROWS, tc), f32), pltpu.VMEM((L + CONV_ROWS, tc), f32)],
        compiler_params=_params("parallel", "arbitrary"),
    )(up, up, dff, conv_w, conv_w, conv_b, conv_b)


GELU_C = math.sqrt(2.0 / math.pi)
GELU_A = 0.044715


def _gelu(x):
    return 0.5 * x * (1.0 + jnp.tanh(GELU_C * (x + GELU_A * x * x * x)))


def _gelu_grad(x):
    t = jnp.tanh(GELU_C * (x + GELU_A * x * x * x))
    return 0.5 * (1.0 + t) + 0.5 * x * (1.0 - t * t) * GELU_C * (1.0 + 3.0 * GELU_A * x * x)


def _cmul_add(xr, xi, ar, ai, sr, si):
    return xr + ar * sr - ai * si, xi + ar * si + ai * sr


def _s5_project_in(u_ref, bs_ref, s_ref, L, rc):
    for r in range(0, L, rc):
        s_ref[r:r + rc, :] = jnp.dot(u_ref[r:r + rc, :].astype(bf16), bs_ref[...], preferred_element_type=f32)


def _rows8(i):
    return pl.ds(pl.multiple_of(i * SUBLANES, SUBLANES), SUBLANES)


def _repeat_loop(n, step, init):
    rep = max(u for u in (6, 4, 3, 2, 1) if n % u == 0)

    def body(t, carry):
        for u in range(rep):
            carry = step(t * rep + u, carry)
        return carry

    return lax.fori_loop(0, n // rep, body, init)


def _to_segments(src_ref, dst_ref, seg):
    def step(i, c):
        dst_ref[_rows8(i), :] = src_ref[pl.ds(i, SUBLANES, stride=seg), :]
        return c

    _repeat_loop(seg, step, 0)


def _from_segments(src_ref, dst_ref, seg):
    def step(i, c):
        dst_ref[pl.ds(i, SUBLANES, stride=seg), :] = src_ref[_rows8(i), :]
        return c

    _repeat_loop(seg, step, 0)


def _seg_local_scan(s_ref, ar, ai, seg, reverse):
    ns = SLAB_NS

    def step(j, carry):
        cr, ci = carry
        rows = _rows8(seg - 1 - j if reverse else j)
        cr, ci = _cmul_add(s_ref[rows, 0:ns], s_ref[rows, ns:2 * ns], ar, ai, cr, ci)
        s_ref[rows, 0:ns] = cr
        s_ref[rows, ns:2 * ns] = ci
        return cr, ci

    z = jnp.zeros((SUBLANES, ns), f32)
    return _repeat_loop(seg, step, (z, z))


def _seg_boundaries(fr, fi, alr, ali, reverse):
    row = lax.broadcasted_iota(jnp.int32, fr.shape, 0)
    br = jnp.zeros_like(fr)
    bi = jnp.zeros_like(fi)
    for r in (range(SUBLANES - 2, -1, -1) if reverse else range(1, SUBLANES)):
        s = r + 1 if reverse else r - 1
        nr, ni = _cmul_add(fr[s:s + 1, :], fi[s:s + 1, :], alr, ali, br[s:s + 1, :], bi[s:s + 1, :])
        br = jnp.where(row == r, nr, br)
        bi = jnp.where(row == r, ni, bi)
    return br, bi


def _s5_states(u_ref, bs_ref, pw_ref, up_ref, s_ref, L, rc):
    seg = L // SUBLANES
    ns = SLAB_NS
    _to_segments(u_ref, up_ref, seg)
    _s5_project_in(up_ref, bs_ref, s_ref, L, rc)
    ar, ai = pw_ref[0, 0:1, :], pw_ref[1, 0:1, :]
    fr, fi = _seg_local_scan(s_ref, ar, ai, seg, False)
    br, bi = _seg_boundaries(fr, fi, pw_ref[0, seg - 1:seg, :], pw_ref[1, seg - 1:seg, :], False)

    def fix(i, c):
        rows = _rows8(i)
        xr, xi = _cmul_add(s_ref[rows, 0:ns], s_ref[rows, ns:2 * ns], pw_ref[0, pl.ds(i, 1), :], pw_ref[1, pl.ds(i, 1), :], br, bi)
        s_ref[rows, 0:ns] = xr
        s_ref[rows, ns:2 * ns] = xi
        return c

    _repeat_loop(seg, fix, 0)


def _pw_spec(seg_rows, order):
    if order == "bs":
        return pl.BlockSpec((2, seg_rows, SLAB_NS), lambda b, s: (0, 0, s))
    return pl.BlockSpec((2, seg_rows, SLAB_NS), lambda s, b: (0, 0, s))


def _s5_fwd(p, bs, cs, pw, d_skip, B, L):
    rc = _tile(L, 344)
    seg = L // SUBLANES

    def body(u_ref, bs_ref, cs_ref, pw_ref, d_ref, y_ref, s_ref, up_ref, yp_ref):
        _s5_states(u_ref, bs_ref, pw_ref, up_ref, s_ref, L, rc)
        for r in range(0, L, rc):
            ypre = (jnp.dot(s_ref[r:r + rc, :].astype(bf16), cs_ref[...], preferred_element_type=f32)
                    + d_ref[...] * up_ref[r:r + rc, :])
            yp_ref[r:r + rc, :] = _gelu(ypre)
        _from_segments(yp_ref, y_ref, seg)

    ucol = SEG_U * (D_MODEL // SLAB_CH)
    return pl.pallas_call(
        body, name="s5_fwd", grid=(B, N_SLAB),
        in_specs=[pl.BlockSpec((L, SLAB_CH), lambda b, s: (b, ucol + s)),
                  pl.BlockSpec((None, SLAB_CH, 2 * SLAB_NS), lambda b, s: (s, 0, 0)),
                  pl.BlockSpec((None, 2 * SLAB_NS, SLAB_CH), lambda b, s: (s, 0, 0)),
                  _pw_spec(pw.shape[1], "bs"),
                  pl.BlockSpec((1, SLAB_CH), lambda b, s: (0, s))],
        out_specs=pl.BlockSpec((L, SLAB_CH), lambda b, s: (b, s)),
        out_shape=jax.ShapeDtypeStruct((B * L, D_MODEL), f32),
        scratch_shapes=[pltpu.VMEM((L, 2 * SLAB_NS), f32), pltpu.VMEM((L, SLAB_CH), f32), pltpu.VMEM((L, SLAB_CH), f32)],
        compiler_params=_params("parallel", "parallel"),
    )(p, bs, cs, pw, d_skip)


def _s5_bwd(p, dya0, dp, bs, cs, pw, d_skip, B, L, sums):
    rc = _tile(L, 344)
    ns = SLAB_NS
    seg = L // SUBLANES
    nx = len(sums)

    def body(u_ref, dy_ref, dp_in, bs_ref, cs_ref, pw_ref, d_ref, *rest):
        xin, (du_ref, dbs_ref, dcs_ref, da_ref, dd_ref), xout = rest[:nx], rest[nx:nx + 5], rest[nx + 5:2 * nx + 5]
        s_ref, lam_ref, up_ref, dyp_ref, nat_ref, send, recv = rest[2 * nx + 5:]
        del dp_in
        start, finish = _chip_exchange_steps(xin, xout, send, recv)

        @pl.when((pl.program_id(0) == 0) & (pl.program_id(1) == 0))
        def _():
            start()

        @pl.when(pl.program_id(1) == 0)
        def _():
            dbs_ref[...] = jnp.zeros_like(dbs_ref)
            dcs_ref[...] = jnp.zeros_like(dcs_ref)
            da_ref[...] = jnp.zeros_like(da_ref)
            dd_ref[...] = jnp.zeros_like(dd_ref)

        _s5_states(u_ref, bs_ref, pw_ref, up_ref, s_ref, L, rc)
        _to_segments(dy_ref, dyp_ref, seg)
        for r in range(0, L, rc):
            u = up_ref[r:r + rc, :]
            sb = s_ref[r:r + rc, :].astype(bf16)
            ypre = jnp.dot(sb, cs_ref[...], preferred_element_type=f32) + d_ref[...] * u
            dyp = dyp_ref[r:r + rc, :] * _gelu_grad(ypre)
            dyp_ref[r:r + rc, :] = dyp
            dd_ref[...] += jnp.sum(dyp * u, axis=0, keepdims=True)
            dypb = dyp.astype(bf16)
            dcs_ref[...] += lax.dot_general(sb, dypb, _DIMS["tn"], preferred_element_type=f32)
            lam_ref[r:r + rc, :] = lax.dot_general(dypb, cs_ref[...], _DIMS["nt"], preferred_element_type=f32)

        ar, ai = pw_ref[0, 0:1, :], -pw_ref[1, 0:1, :]
        fr, fi = _seg_local_scan(lam_ref, ar, ai, seg, True)
        br, bi = _seg_boundaries(fr, fi, pw_ref[0, seg - 1:seg, :], -pw_ref[1, seg - 1:seg, :], True)

        def fix(i, acc):
            accr, acci = acc
            rows = _rows8(i)
            k = seg - 1 - i
            xr, xi = _cmul_add(lam_ref[rows, 0:ns], lam_ref[rows, ns:2 * ns], pw_ref[0, pl.ds(k, 1), :],
                               -pw_ref[1, pl.ds(k, 1), :], br, bi)
            lam_ref[rows, 0:ns] = xr
            lam_ref[rows, ns:2 * ns] = xi
            prev = _rows8(jnp.maximum(i - 1, 0))
            live = jnp.where(i > 0, 1.0, 0.0)
            spr = s_ref[prev, 0:ns] * live
            spi = s_ref[prev, ns:2 * ns] * live
            return accr + xr * spr + xi * spi, acci + xi * spr - xr * spi

        z = jnp.zeros((SUBLANES, ns), f32)
        accr, acci = _repeat_loop(seg, fix, (z, z))
        row = lax.broadcasted_iota(jnp.int32, (SUBLANES, ns), 0)
        last = _rows8(seg - 1)
        spr = jnp.where(row == 0, 0.0, pltpu.roll(s_ref[last, 0:ns], 1, 0))
        spi = jnp.where(row == 0, 0.0, pltpu.roll(s_ref[last, ns:2 * ns], 1, 0))
        xr, xi = lam_ref[0:SUBLANES, 0:ns], lam_ref[0:SUBLANES, ns:2 * ns]
        accr = accr + xr * spr + xi * spi
        acci = acci + xi * spr - xr * spi
        da_ref[0:1, :] += jnp.sum(accr, axis=0, keepdims=True)
        da_ref[1:2, :] += jnp.sum(acci, axis=0, keepdims=True)

        for r in range(0, L, rc):
            lamb = lam_ref[r:r + rc, :].astype(bf16)
            dbs_ref[...] += lax.dot_general(up_ref[r:r + rc, :].astype(bf16), lamb, _DIMS["tn"], preferred_element_type=f32)
            nat_ref[r:r + rc, :] = (lax.dot_general(lamb, bs_ref[...], _DIMS["nt"], preferred_element_type=f32)
                                    + d_ref[...] * dyp_ref[r:r + rc, :])
        _from_segments(nat_ref, up_ref, seg)
        du_ref[...] = up_ref[...].astype(du_ref.dtype)

        @pl.when((pl.program_id(0) == N_SLAB - 1) & (pl.program_id(1) == B - 1))
        def _():
            finish()

    ucol = SEG_U * (D_MODEL // SLAB_CH)
    T = B * L
    col = pltpu.VMEM((L, SLAB_CH), f32)
    res = pl.pallas_call(
        body, name="s5_bwd", grid=(N_SLAB, B),
        in_specs=[pl.BlockSpec((L, SLAB_CH), lambda s, b: (b, ucol + s)),
                  pl.BlockSpec((L, SLAB_CH), lambda s, b: (b, s)),
                  ANY,
                  pl.BlockSpec((None, SLAB_CH, 2 * SLAB_NS), lambda s, b: (s, 0, 0)),
                  pl.BlockSpec((None, 2 * SLAB_NS, SLAB_CH), lambda s, b: (s, 0, 0)),
                  _pw_spec(pw.shape[1], "sb"),
                  pl.BlockSpec((1, SLAB_CH), lambda s, b: (0, s))] + [ANY] * nx,
        out_specs=[pl.BlockSpec((None, L, SLAB_CH), lambda s, b: (SEG_U, b, s)),
                   pl.BlockSpec((None, SLAB_CH, 2 * SLAB_NS), lambda s, b: (s, 0, 0)),
                   pl.BlockSpec((None, 2 * SLAB_NS, SLAB_CH), lambda s, b: (s, 0, 0)),
                   pl.BlockSpec((None, 2, SLAB_NS), lambda s, b: (s, 0, 0)),
                   pl.BlockSpec((1, SLAB_CH), lambda s, b: (0, s))] + [ANY] * nx,
        out_shape=[jax.ShapeDtypeStruct((N_SEG, T, D_MODEL), bf16),
                   jax.ShapeDtypeStruct((N_SLAB, SLAB_CH, 2 * SLAB_NS), f32),
                   jax.ShapeDtypeStruct((N_SLAB, 2 * SLAB_NS, SLAB_CH), f32),
                   jax.ShapeDtypeStruct((N_SLAB, 2, SLAB_NS), f32),
                   jax.ShapeDtypeStruct((1, D_MODEL), f32)] + [jax.ShapeDtypeStruct(a.shape, a.dtype) for a in sums],
        scratch_shapes=[pltpu.VMEM((L, 2 * SLAB_NS), f32), pltpu.VMEM((L, 2 * SLAB_NS), f32), col, col, col]
        + _chip_exchange_sems(nx),
        input_output_aliases={2: 0},
        compiler_params=_params("arbitrary", "arbitrary"),
    )(p, dya0, dp, bs, cs, pw, d_skip, *sums)
    return res[:5], res[5:]


def _dotb(a, b, dims="nn"):
    return lax.dot_general(a.astype(bf16), b.astype(bf16), _DIMS[dims], preferred_element_type=f32)


def _tile_scan(x, reverse):
    n, w = x.shape
    v = x.reshape(n // SUBLANES, SUBLANES, w)
    row = lax.broadcasted_iota(jnp.int32, v.shape, 1)
    for k in (1, 2, 4):
        if reverse:
            v = v + jnp.where(row < SUBLANES - k, pltpu.roll(v, SUBLANES - k, 1), 0.0)
        else:
            v = v + jnp.where(row >= k, pltpu.roll(v, k, 1), 0.0)
    p = v.reshape(n // CHUNK, 2, SUBLANES, w)
    lo, hi = p[:, 0], p[:, 1]
    if reverse:
        lo = lo + hi[:, 0:1, :]
    else:
        hi = hi + lo[:, SUBLANES - 1:SUBLANES, :]
    return jnp.stack([lo, hi], axis=1).reshape(n, w)


def _chunk_cumsum(x):
    return _tile_scan(x, False)


def _chunk_rev_cumsum(x):
    return _tile_scan(x, True)


def _chunk_last(x):
    n, w = x.shape
    p = x.reshape(n // CHUNK, CHUNK, w)
    return jnp.broadcast_to(p[:, CHUNK - 1:CHUNK, :], p.shape).reshape(n, w)


def _hgrn_local(q, fl, lb):
    sg = _sigmoid(fl)
    f = lb + (1.0 - lb) * sg
    g = jnp.log(f)
    cum = _chunk_cumsum(g)
    rest = _chunk_last(cum) - cum
    e = jnp.exp(cum)
    em = jnp.exp(-cum)
    eo = jnp.exp(rest)
    k = 1.0 - f
    return sg, f, e, em, eo, q * e, k * em, k * eo, cum + rest


def _chunk_pos(n):
    return lax.broadcasted_iota(jnp.int32, (n, HEAD_DIM), 0) & (CHUNK - 1)


def _hgrn_block_rows(L):
    return _tile(L, 688, CHUNK)


def _hgrn_specs(L, order):
    hb = D_MODEL // HEAD_DIM

    def spec(seg):
        if order == "bh":
            return pl.BlockSpec((L, HEAD_DIM), lambda b, h: (b, seg * hb + h))
        return pl.BlockSpec((L, HEAD_DIM), lambda h, b: (b, seg * hb + h))

    return [spec(SEG_Q), spec(SEG_F), spec(SEG_I), spec(SEG_OG)]


PAIR = 2 * CHUNK
CHUNK_SHIFT = CHUNK.bit_length() - 1


def _pair_steps(L, rb):
    steps = []
    nch = rb // CHUNK
    for r in range(0, L, rb):
        steps += [(r + p * PAIR, PAIR) for p in range(nch // 2)]
        if nch % 2:
            steps.append((r + (nch - 1) * CHUNK, CHUNK))
    return steps


def _pair_flags(rb):
    ci = lax.broadcasted_iota(jnp.int32, (rb, HEAD_DIM), 0) >> CHUNK_SHIFT
    odd = (ci & 1) == 1
    has_next = jnp.logical_and(jnp.logical_not(odd), ci < rb // CHUNK - 1)
    return odd, has_next


def _pair_masks(rb):
    r = lax.broadcasted_iota(jnp.int32, (rb, rb), 0)
    c = lax.broadcasted_iota(jnp.int32, (rb, rb), 1)
    rc, cc = r >> CHUNK_SHIFT, c >> CHUNK_SHIFT
    same = (rc == cc) & (c <= r)
    prev = ((rc & 1) == 1) & (cc == rc - 1)
    return same, prev


def _hgrn_pair_local(q, fl, lb, odd, has_next):
    sg, f, e, em, eo, qt, kt, ko, cend = _hgrn_local(q, fl, lb)
    n = q.shape[0]
    a = jnp.where(odd, pltpu.roll(cend, CHUNK, 0), 0.0)
    z = jnp.where(has_next, pltpu.roll(cend, n - CHUNK, 0), 0.0)
    ea, ez = jnp.exp(a), jnp.exp(z)
    return dict(sg=sg, f=f, e=e, em=em, eo=eo, qt=qt, kt=kt, ko=ko, ea=ea, ez=ez, qs=qt * ea, ks=ko * ez,
                decp=jnp.exp(cend + a + z))


def _pair_scores(qt, kt, ko, same, prev):
    return (jnp.where(same, _dotb(qt, kt, "nt"), 0.0) + jnp.where(prev, _dotb(qt, ko, "nt"), 0.0)).astype(bf16)


def _hgrn_fwd(p, lb, norm_g, B, L):
    rb = _hgrn_block_rows(L)
    steps = _pair_steps(L, rb)
    blocks = [slice(r, r + rb) for r in range(0, L, rb)]

    def body(q_ref, f_ref, v_ref, og_ref, lb_ref, ng_ref, y_ref, qs_s, ks_s, vb_s, decp_s, o_s, o2_s, u_s, sb_s):
        lbv = lb_ref[...]
        ngv = ng_ref[...]
        same, prev = _pair_masks(rb)
        odd, has_next = _pair_flags(rb)

        for rows in blocks:
            t = _hgrn_pair_local(q_ref[rows, :], f_ref[rows, :], lbv, odd, has_next)
            vb = v_ref[rows, :].astype(bf16)
            o_s[rows, :] = _dotb(_pair_scores(t["qt"], t["kt"], t["ko"], same, prev), vb)
            qs_s[rows, :] = t["qs"].astype(bf16)
            ks_s[rows, :] = t["ks"].astype(bf16)
            vb_s[rows, :] = vb
            decp_s[rows, :] = t["decp"]

        for n, (r0, nr) in enumerate(steps):
            u_s[n] = _dotb(vb_s[r0:r0 + nr, :], ks_s[r0:r0 + nr, :], "tn")
        st = jnp.zeros((HEAD_DIM, HEAD_DIM), f32)
        for n, (r0, nr) in enumerate(steps):
            sb_s[n] = st.astype(bf16)
            st = st * decp_s[r0:r0 + 1, :] + u_s[n]
        for n, (r0, nr) in enumerate(steps):
            o2_s[r0:r0 + nr, :] = _dotb(qs_s[r0:r0 + nr, :], sb_s[n], "nt")

        for rows in blocks:
            o = o_s[rows, :] + o2_s[rows, :]
            og = og_ref[rows, :]
            on = o * lax.rsqrt(jnp.mean(o * o, axis=-1, keepdims=True) + EPS) * ngv
            y_ref[rows, :] = (on * og * _sigmoid(og)).astype(y_ref.dtype)

    sb = pltpu.VMEM((L, HEAD_DIM), bf16)
    sf = pltpu.VMEM((L, HEAD_DIM), f32)
    return pl.pallas_call(
        body, name="hgrn_fwd", grid=(B, HEADS),
        in_specs=_hgrn_specs(L, "bh") + [pl.BlockSpec((1, HEAD_DIM), lambda b, h: (0, h)),
                                          pl.BlockSpec((1, HEAD_DIM), lambda b, h: (0, 0))],
        out_specs=pl.BlockSpec((L, HEAD_DIM), lambda b, h: (b, h)),
        out_shape=jax.ShapeDtypeStruct((B * L, D_MODEL), bf16),
        scratch_shapes=[sb, sb, sb, sf, sf, sf, pltpu.VMEM((len(steps), HEAD_DIM, HEAD_DIM), f32),
                        pltpu.VMEM((len(steps), HEAD_DIM, HEAD_DIM), bf16)],
        compiler_params=_params("parallel", "parallel"),
    )(p, p, p, p, lb, norm_g)


def _hgrn_bwd(p, dyb, dp, lb, norm_g, B, L):
    rb = _hgrn_block_rows(L)
    steps = _pair_steps(L, rb)
    blocks = [slice(r, r + rb) for r in range(0, L, rb)]

    def body(q_ref, f_ref, v_ref, og_ref, dy_ref, dp_in, lb_ref, ng_ref, dseg_ref, dlb_ref, dng_ref,
             st_ref, u_s, dsb_s, qt_s, kt_s, ko_s, qs_s, ks_s, vb_s, do_s,
             decp_s, o_s, o2_s, dqt_s, dkt_s, dko_s, dv_s, dv2_s, dqs_s, dks_s, ddecp_s):
        del dp_in
        lbv = lb_ref[...]
        ngv = ng_ref[...]
        same, prev = _pair_masks(rb)
        odd, has_next = _pair_flags(rb)
        pos = _chunk_pos(rb)

        @pl.when(pl.program_id(1) == 0)
        def _():
            dlb_ref[...] = jnp.zeros_like(dlb_ref)

        @pl.when((pl.program_id(0) == 0) & (pl.program_id(1) == 0))
        def _():
            dng_ref[...] = jnp.zeros_like(dng_ref)

        def scores(rows):
            return _pair_scores(qt_s[rows, :], kt_s[rows, :], ko_s[rows, :], same, prev)

        for rows in blocks:
            t = _hgrn_pair_local(q_ref[rows, :], f_ref[rows, :], lbv, odd, has_next)
            for dst, key in ((qt_s, "qt"), (kt_s, "kt"), (ko_s, "ko"), (qs_s, "qs"), (ks_s, "ks")):
                dst[rows, :] = t[key].astype(bf16)
            vb_s[rows, :] = v_ref[rows, :].astype(bf16)
            decp_s[rows, :] = t["decp"]
            o_s[rows, :] = _dotb(scores(rows), vb_s[rows, :])

        for n, (r0, nr) in enumerate(steps):
            u_s[n] = _dotb(vb_s[r0:r0 + nr, :], ks_s[r0:r0 + nr, :], "tn")
        st = jnp.zeros((HEAD_DIM, HEAD_DIM), f32)
        for n, (r0, nr) in enumerate(steps):
            st_ref[n] = st
            st = st * decp_s[r0:r0 + 1, :] + u_s[n]
        for n, (r0, nr) in enumerate(steps):
            o2_s[r0:r0 + nr, :] = _dotb(qs_s[r0:r0 + nr, :], st_ref[n], "nt")

        dng = jnp.zeros((1, HEAD_DIM), f32)
        for rows in blocks:
            o = o_s[rows, :] + o2_s[rows, :]
            og = og_ref[rows, :]
            dy = dy_ref[rows, :]
            rs = lax.rsqrt(jnp.mean(o * o, axis=-1, keepdims=True) + EPS)
            xn = o * rs
            so = _sigmoid(og)
            dseg_ref[SEG_OG, rows, :] = (dy * xn * ngv * so * (1.0 + og * (1.0 - so))).astype(dseg_ref.dtype)
            don = dy * og * so
            dng = dng + jnp.sum(don * xn, axis=0, keepdims=True)
            dxo = don * ngv
            do = (rs * (dxo - xn * jnp.mean(dxo * xn, axis=-1, keepdims=True))).astype(bf16)
            do_s[rows, :] = do
            dpf = _dotb(do, vb_s[rows, :], "nt")
            dp1 = jnp.where(same, dpf, 0.0).astype(bf16)
            dp2 = jnp.where(prev, dpf, 0.0).astype(bf16)
            dqt_s[rows, :] = _dotb(dp1, kt_s[rows, :]) + _dotb(dp2, ko_s[rows, :])
            dkt_s[rows, :] = _dotb(dp1, qt_s[rows, :], "tn")
            dko_s[rows, :] = _dotb(dp2, qt_s[rows, :], "tn")
            dv_s[rows, :] = _dotb(scores(rows), do, "tn")
        dng_ref[...] += dng

        for n, (r0, nr) in enumerate(steps):
            u_s[n] = _dotb(do_s[r0:r0 + nr, :], qs_s[r0:r0 + nr, :], "tn")
        dst = jnp.zeros((HEAD_DIM, HEAD_DIM), f32)
        for n, (r0, nr) in reversed(list(enumerate(steps))):
            dsb_s[n] = dst.astype(bf16)
            ddecp_s[r0:r0 + nr, :] = jnp.broadcast_to(jnp.sum(dst * st_ref[n], axis=0, keepdims=True), (nr, HEAD_DIM))
            dst = dst * decp_s[r0:r0 + 1, :] + u_s[n]
        for n, (r0, nr) in enumerate(steps):
            rows = slice(r0, r0 + nr)
            dqs_s[rows, :] = _dotb(do_s[rows, :], st_ref[n])
            dv2_s[rows, :] = _dotb(ks_s[rows, :], dsb_s[n], "nt")
            dks_s[rows, :] = _dotb(vb_s[rows, :], dsb_s[n])

        def chunk_sum(x):
            return _chunk_last(_chunk_cumsum(x))

        dlb = jnp.zeros((1, HEAD_DIM), f32)
        for rows in blocks:
            t = _hgrn_pair_local(q_ref[rows, :], f_ref[rows, :], lbv, odd, has_next)
            dqs, dks = dqs_s[rows, :], dks_s[rows, :]
            dqt = dqt_s[rows, :] + dqs * t["ea"]
            dko = dko_s[rows, :] + dks * t["ez"]
            dkt = dkt_s[rows, :]
            dko_ko = dko * t["ko"]
            dcum = dqt * t["qt"] - dkt * t["kt"] - dko_ko
            from_next = pltpu.roll(chunk_sum(jnp.where(odd, dqs * t["qs"], 0.0)), rb - CHUNK, 0)
            from_prev = pltpu.roll(chunk_sum(jnp.where(has_next, dks * t["ks"], 0.0)), CHUNK, 0)
            d_end = (chunk_sum(dko_ko) + jnp.where(has_next, from_next, 0.0) + jnp.where(odd, from_prev, 0.0)
                     + ddecp_s[rows, :] * t["decp"])
            dcum = dcum + jnp.where(pos == CHUNK - 1, d_end, 0.0)
            df = _chunk_rev_cumsum(dcum) / t["f"] - (dkt * t["em"] + dko * t["eo"])
            dlb = dlb + jnp.sum(df * (1.0 - t["sg"]), axis=0, keepdims=True)
            dseg_ref[SEG_Q, rows, :] = (dqt * t["e"]).astype(dseg_ref.dtype)
            dseg_ref[SEG_F, rows, :] = (df * (1.0 - lbv) * t["sg"] * (1.0 - t["sg"])).astype(dseg_ref.dtype)
            dseg_ref[SEG_I, rows, :] = (dv_s[rows, :] + dv2_s[rows, :]).astype(dseg_ref.dtype)
        dlb_ref[...] += dlb

    T = B * L
    ns = len(steps)
    sb = pltpu.VMEM((L, HEAD_DIM), bf16)
    sf = pltpu.VMEM((L, HEAD_DIM), f32)
    return pl.pallas_call(
        body, name="hgrn_bwd", grid=(HEADS, B),
        in_specs=_hgrn_specs(L, "hb") + [pl.BlockSpec((L, HEAD_DIM), lambda h, b: (b, h)), ANY,
                                          pl.BlockSpec((1, HEAD_DIM), lambda h, b: (0, h)),
                                          pl.BlockSpec((1, HEAD_DIM), lambda h, b: (0, 0))],
        out_specs=[pl.BlockSpec((4, L, HEAD_DIM), lambda h, b: (0, b, h)),
                   pl.BlockSpec((1, HEAD_DIM), lambda h, b: (0, h)),
                   pl.BlockSpec((1, HEAD_DIM), lambda h, b: (0, 0))],
        out_shape=[jax.ShapeDtypeStruct((N_SEG, T, D_MODEL), bf16), jax.ShapeDtypeStruct((1, D_MODEL), f32),
                   jax.ShapeDtypeStruct((1, HEAD_DIM), f32)],
        scratch_shapes=[pltpu.VMEM((ns, HEAD_DIM, HEAD_DIM), f32), pltpu.VMEM((ns, HEAD_DIM, HEAD_DIM), f32),
                        pltpu.VMEM((ns, HEAD_DIM, HEAD_DIM), bf16)] + [sb] * 7 + [sf] * 11,
        input_output_aliases={5: 0},
        compiler_params=_params("arbitrary", "arbitrary"),
    )(p, p, p, p, dyb, dp, lb, norm_g)


def _dz1_norm(dp, w_in_phys, h0, g, dh1):
    _, T, Dm = dp.shape
    tm = _tile(T, 688)
    return _mm_rmsnorm_bwd("dz1", dp, w_in_phys, (T // tm, 1, N_SEG),
                           pl.BlockSpec((None, tm, Dm), lambda i, j, k: (k, i, 0)),
                           pl.BlockSpec((Dm, Dm), lambda i, j, k: (0, k)), h0, g, dh1)


def _dz2_norm(dup, w_up, h1, g, dh2):
    _, T, _ = dup.shape
    tm = _tile(T, 688)
    tk = D_FF // 2
    return _mm_rmsnorm_bwd("dz2", dup, w_up, (T // tm, 1, 4),
                           pl.BlockSpec((None, tm, tk), lambda i, j, k: (k // 2, i, k % 2)),
                           pl.BlockSpec((D_MODEL, tk), lambda i, j, k: (0, k)), h1, g, dh2)


def _dw_in(z1, dp):
    _, T, Dm = dp.shape
    tk = _tile(T, 1376)
    return _mm("dw_in", z1, dp, "tn", (1, N_SEG, T // tk),
               pl.BlockSpec((tk, Dm), lambda i, j, k: (k, 0)),
               pl.BlockSpec((None, tk, Dm), lambda i, j, k: (j, k, 0)),
               jax.ShapeDtypeStruct((N_SEG, Dm, Dm), f32),
               pl.BlockSpec((None, Dm, Dm), lambda i, j, k: (j, 0, 0)), (Dm, Dm))


def _dw_up(z2, dup):
    _, T, _ = dup.shape
    tn = D_FF // 2
    tk = _tile(T, 688)
    return _mm("dw_up", z2, dup, "tn", (1, N_CHIPS, T // tk),
               pl.BlockSpec((tk, D_MODEL), lambda i, j, k: (k, 0)),
               pl.BlockSpec((None, tk, tn), lambda i, j, k: (j // 2, k, j % 2)),
               jax.ShapeDtypeStruct((N_CHIPS, D_MODEL, tn), f32),
               pl.BlockSpec((None, D_MODEL, tn), lambda i, j, k: (j, 0, 0)), (D_MODEL, tn))


def _place():
    x, y, c = lax.axis_index("x"), lax.axis_index("y"), lax.axis_index("c")
    chips = [(1 - x, y), (x, 1 - y), (1 - x, 1 - y)]
    return x, y, c, chips


def _allgather_chips(arrs):
    n = len(arrs)

    def body(*refs):
        ins, outs = refs[:n], refs[n:2 * n]
        send, recv, local = refs[2 * n:]
        x, y, c, chips = _place()
        me = 2 * x + y

        def copy(a, k, slot):
            px, py = chips[k]
            return pltpu.make_async_remote_copy(src_ref=ins[a], dst_ref=outs[a].at[slot], send_sem=send.at[3 * a + k],
                                                recv_sem=recv.at[3 * a + k], device_id=(px, py, c), device_id_type=MESH)

        for a in range(n):
            pltpu.make_async_copy(ins[a], outs[a].at[me], local.at[a]).start()
            for k in range(3):
                copy(a, k, me).start()
        for a in range(n):
            for k, (px, py) in enumerate(chips):
                copy(a, k, 2 * px + py).wait_recv()
        for a in range(n):
            pltpu.make_async_copy(ins[a], outs[a].at[me], local.at[a]).wait()
            for k in range(3):
                copy(a, k, me).wait_send()

    return pl.pallas_call(
        body, name="allgather_chips", in_specs=[ANY] * n, out_specs=[ANY] * n,
        out_shape=[jax.ShapeDtypeStruct((N_CHIPS,) + a.shape, a.dtype) for a in arrs],
        scratch_shapes=[pltpu.SemaphoreType.DMA((3 * n,)), pltpu.SemaphoreType.DMA((3 * n,)), pltpu.SemaphoreType.DMA((n,))],
    )(*arrs)


def _allgather_split(arrs):
    n = len(arrs)

    def body(*refs):
        start, finish = _gather_split_steps(refs[:n], refs[n:2 * n], *refs[2 * n:])
        start()
        finish()

    return pl.pallas_call(
        body, name="allgather_split", in_specs=[ANY] * n, out_specs=[ANY] * n,
        out_shape=[jax.ShapeDtypeStruct((N_CHIPS,) + a.shape, a.dtype) for a in arrs],
        scratch_shapes=_gather_split_sems(n),
    )(*arrs)


def _gather_split_sems(n):
    return [pltpu.SemaphoreType.DMA((3 * n,)) for _ in range(4)]


def _gather_split_steps(ins, outs, send, recv, fsend, frecv):
    n = len(ins)

    def place():
        x, y, c, chips = _place()
        return x, y, c, chips, 2 * x + y

    def half(a, core):
        rh = ins[a].shape[0] // 2
        return pl.ds(core * rh, rh)

    def copy(a, k, slot):
        x, y, c, chips, _ = place()
        px, py = chips[k]
        return pltpu.make_async_remote_copy(src_ref=ins[a].at[half(a, c), :], dst_ref=outs[a].at[slot, half(a, c), :],
                                            send_sem=send.at[3 * a + k], recv_sem=recv.at[3 * a + k],
                                            device_id=(px, py, c), device_id_type=MESH)

    def forward(a, k, core):
        x, y, c, chips, _ = place()
        px, py = chips[k]
        rows = outs[a].at[2 * px + py, half(a, core), :]
        return pltpu.make_async_remote_copy(src_ref=rows, dst_ref=rows, send_sem=fsend.at[3 * a + k],
                                            recv_sem=frecv.at[3 * a + k], device_id=(x, y, 1 - c), device_id_type=MESH)

    def start():
        me = place()[4]
        for a in range(n):
            for k in range(3):
                copy(a, k, me).start()

    def finish():
        x, y, c, chips, me = place()
        for a in range(n):
            for k, (px, py) in enumerate(chips):
                copy(a, k, 2 * px + py).wait_recv()
                forward(a, k, c).start()
        for a in range(n):
            for k in range(3):
                forward(a, k, 1 - c).wait_recv()
        for a in range(n):
            for k in range(3):
                copy(a, k, me).wait_send()
                forward(a, k, c).wait_send()

    return start, finish


def _in_proj_gather(z1, w_in, shards):
    n = len(shards)
    T, K = z1.shape
    N = w_in.shape[1]
    tm = _tile(T, 1032)
    tn = 1024
    grid = (T // tm, N // tn)

    def body(a_ref, b_ref, *rest):
        ins, o_ref, outs, sems = rest[:n], rest[n], rest[n + 1:2 * n + 1], rest[2 * n + 1:]
        start, finish = _gather_split_steps(ins, outs, *sems)
        i, j = pl.program_id(0), pl.program_id(1)

        @pl.when((i == 0) & (j == 0))
        def _():
            start()

        o_ref[...] = jnp.dot(a_ref[...], b_ref[...], preferred_element_type=f32)

        @pl.when((i == grid[0] - 1) & (j == grid[1] - 1))
        def _():
            finish()

    res = pl.pallas_call(
        body, name="in_proj", grid=grid,
        in_specs=[pl.BlockSpec((tm, K), lambda i, j: (i, 0)), pl.BlockSpec((K, tn), lambda i, j: (0, j))] + [ANY] * n,
        out_specs=[pl.BlockSpec((tm, tn), lambda i, j: (i, j))] + [ANY] * n,
        out_shape=[jax.ShapeDtypeStruct((T, N), f32)] + [jax.ShapeDtypeStruct((N_CHIPS,) + a.shape, a.dtype) for a in shards],
        scratch_shapes=_gather_split_sems(n),
        compiler_params=_params("arbitrary", "arbitrary"),
    )(z1, w_in, *shards)
    return res[0], res[1:]


def _sibling_halves(parts, name="sibling_halves"):
    n = len(parts)

    def body(*refs):
        ins, outs = refs[:n], refs[n:2 * n]
        send, recv = refs[2 * n:]
        x, y, c, _ = _place()

        def copy(a):
            rh = ins[a].shape[1] // 2
            return pltpu.make_async_remote_copy(src_ref=ins[a].at[:, pl.ds((1 - c) * rh, rh), :], dst_ref=outs[a],
                                                send_sem=send.at[a], recv_sem=recv.at[a], device_id=(x, y, 1 - c),
                                                device_id_type=MESH)

        for a in range(n):
            copy(a).start()
        for a in range(n):
            copy(a).wait_recv()
        for a in range(n):
            copy(a).wait_send()

    return pl.pallas_call(
        body, name=name, in_specs=[ANY] * n, out_specs=[ANY] * n,
        out_shape=[jax.ShapeDtypeStruct((a.shape[0], a.shape[1] // 2, a.shape[2]), a.dtype) for a in parts],
        scratch_shapes=[pltpu.SemaphoreType.DMA((n,)), pltpu.SemaphoreType.DMA((n,))],
    )(*parts)


def _add_own_half(name, part, got, core):
    nchip, R, C = part.shape
    rh = R // 2
    tr = _tile(rh, 256, 2 * SUBLANES)
    nt = rh // tr

    def body(core_ref, a_ref, b_ref, o_ref):
        del core_ref
        o_ref[...] = (a_ref[...] + b_ref[...]).astype(o_ref.dtype)

    return pl.pallas_call(
        body, name=name,
        grid_spec=pltpu.PrefetchScalarGridSpec(
            num_scalar_prefetch=1, grid=(nchip, nt),
            in_specs=[pl.BlockSpec((None, tr, C), lambda j, i, core_ref: (j, core_ref[0] * nt + i, 0)),
                      pl.BlockSpec((None, tr, C), lambda j, i, core_ref: (j, i, 0))],
            out_specs=pl.BlockSpec((None, tr, C), lambda j, i, core_ref: (j, i, 0))),
        out_shape=jax.ShapeDtypeStruct((nchip, rh, C), bf16), compiler_params=_params("parallel", "parallel"),
    )(core, part, got)


def _add_own_half_w_in(part, got, core):
    _, R, C = part.shape
    rh = R // 2
    tr = _tile(rh, 256, 2 * SUBLANES)
    nt = rh // tr
    tn = 256
    per_seg = C // tn
    per_chip = IN_COLS // N_CHIPS // tn

    def src(j):
        return ((j // per_seg + N_SEG - 1) % N_SEG, j % per_seg)

    def body(core_ref, a_ref, b_ref, o_ref):
        del core_ref
        o_ref[...] = (a_ref[...] + b_ref[...]).astype(o_ref.dtype)

    return pl.pallas_call(
        body, name="add_half_w_in",
        grid_spec=pltpu.PrefetchScalarGridSpec(
            num_scalar_prefetch=1, grid=(IN_COLS // tn, nt),
            in_specs=[pl.BlockSpec((None, tr, tn), lambda j, i, core_ref: (src(j)[0], core_ref[0] * nt + i, src(j)[1])),
                      pl.BlockSpec((None, tr, tn), lambda j, i, core_ref: (src(j)[0], i, src(j)[1]))],
            out_specs=pl.BlockSpec((None, tr, tn), lambda j, i, core_ref: (j // per_chip, i, j % per_chip))),
        out_shape=jax.ShapeDtypeStruct((N_CHIPS, rh, IN_COLS // N_CHIPS), bf16), compiler_params=_params("parallel", "parallel"),
    )(core, part, got)


def _chip_exchange(sums):
    n = len(sums)

    def body(*refs):
        start, finish = _chip_exchange_steps(refs[:n], refs[n:2 * n], *refs[2 * n:])
        start()
        finish()

    return pl.pallas_call(
        body, name="chip_exchange", in_specs=[ANY] * n, out_specs=[ANY] * n,
        out_shape=[jax.ShapeDtypeStruct(a.shape, a.dtype) for a in sums],
        scratch_shapes=_chip_exchange_sems(n),
    )(*sums)


def _chip_exchange_sems(n):
    return [pltpu.SemaphoreType.DMA((3 * n,)), pltpu.SemaphoreType.DMA((3 * n,))]


def _chip_exchange_steps(ins, outs, send, recv):
    n = len(ins)

    def copy(a, k, own_slot):
        x, y, c, chips = _place()
        px, py = chips[k]
        slot = 2 * x + y if own_slot else 2 * px + py
        return pltpu.make_async_remote_copy(src_ref=ins[a].at[2 * px + py], dst_ref=outs[a].at[slot], send_sem=send.at[3 * a + k],
                                            recv_sem=recv.at[3 * a + k], device_id=(px, py, c), device_id_type=MESH)

    def start():
        for a in range(n):
            for k in range(3):
                copy(a, k, True).start()

    def finish():
        for a in range(n):
            for k in range(3):
                copy(a, k, False).wait_recv()
        for a in range(n):
            for k in range(3):
                copy(a, k, True).wait_send()

    return start, finish


def _sum_chips(name, slots, sums, where):
    nchip, rh, C = slots.shape
    tr = _tile(rh, 256, 2 * SUBLANES)
    nt = rh // tr

    def body(where_ref, own_ref, s1_ref, s2_ref, s3_ref, o_ref):
        me = where_ref[0]
        by_dist = [r[...].astype(f32) for r in (own_ref, s1_ref, s2_ref, s3_ref)]
        acc = None
        for j in range(nchip):
            d = me ^ j
            term = jnp.where(d == 0, by_dist[0], jnp.where(d == 1, by_dist[1], jnp.where(d == 2, by_dist[2], by_dist[3])))
            acc = term if acc is None else acc + term
        o_ref[...] = acc

    def other(d):
        return pl.BlockSpec((None, tr, C), lambda i, w: (w[0] ^ d, i, 0))

    return pl.pallas_call(
        body, name=name,
        grid_spec=pltpu.PrefetchScalarGridSpec(
            num_scalar_prefetch=1, grid=(nt,),
            in_specs=[other(0), other(1), other(2), other(3)],
            out_specs=pl.BlockSpec((tr, C), lambda i, w: (w[1] * nt + i, 0))),
        out_shape=jax.ShapeDtypeStruct((2 * rh, C), f32), compiler_params=_params("parallel"),
    )(where, sums, slots, slots, slots)


def _sum_slots(name, slots):
    ns, R, C = slots.shape
    tr = _tile(R, 256)

    def body(s_ref, o_ref):
        acc = s_ref[0]
        for j in range(1, ns):
            acc = acc + s_ref[j]
        o_ref[...] = acc

    return pl.pallas_call(
        body, name=name, grid=(R // tr,), in_specs=[pl.BlockSpec((ns, tr, C), lambda i: (0, i, 0))],
        out_specs=pl.BlockSpec((tr, C), lambda i: (i, 0)), out_shape=jax.ShapeDtypeStruct((R, C), f32),
        compiler_params=_params("parallel"),
    )(slots)


def _sibling_join(fulls):
    n = len(fulls)

    def body(*refs):
        ins, outs = refs[:n], refs[n:2 * n]
        send, recv = refs[2 * n:]
        x, y, c, _ = _place()

        def copy(a, core):
            rh = ins[a].shape[0] // 2
            rows = pl.ds(core * rh, rh)
            return pltpu.make_async_remote_copy(src_ref=ins[a].at[rows, :], dst_ref=outs[a].at[rows, :], send_sem=send.at[a],
                                                recv_sem=recv.at[a], device_id=(x, y, 1 - c), device_id_type=MESH)

        for a in range(n):
            copy(a, c).start()
        for a in range(n):
            copy(a, 1 - c).wait_recv()
        for a in range(n):
            copy(a, c).wait_send()

    return pl.pallas_call(
        body, name="sibling_join", in_specs=[ANY] * n, out_specs=[ANY] * n,
        out_shape=[jax.ShapeDtypeStruct(a.shape, a.dtype) for a in fulls],
        scratch_shapes=[pltpu.SemaphoreType.DMA((n,)), pltpu.SemaphoreType.DMA((n,))],
        input_output_aliases={a: a for a in range(n)},
    )(*fulls)


def _allgather_devices(v):
    def body(v_ref, out_ref, send, recv):
        x, y, c, chips = _place()
        me, sibling = (x, y, c), (x, y, 1 - c)

        def slot(px, py, pc):
            return out_ref.at[4 * px + 2 * py + pc]

        def copy(k, block, to, src=None):
            return pltpu.make_async_remote_copy(src_ref=slot(*block) if src is None else src, dst_ref=slot(*block),
                                                send_sem=send.at[k], recv_sem=recv.at[k], device_id=to, device_id_type=MESH)

        first = [copy(0, me, sibling, src=v_ref)] + [copy(1 + j, me, (*chip, c), src=v_ref) for j, chip in enumerate(chips)]
        for cp in first:
            cp.start()
        passed = [copy(4 + j, (*chip, c), sibling) for j, chip in enumerate(chips)]
        for j, chip in enumerate(chips):
            copy(1 + j, (*chip, c), me).wait_recv()
            passed[j].start()
        copy(0, sibling, me).wait_recv()
        for j, chip in enumerate(chips):
            copy(4 + j, (*chip, 1 - c), me).wait_recv()
        for cp in first + passed:
            cp.wait_send()

    return pl.pallas_call(
        body, name="allgather_devices", in_specs=[ANY], out_specs=ANY,
        out_shape=jax.ShapeDtypeStruct((N_DEV,) + v.shape, v.dtype),
        scratch_shapes=[pltpu.SemaphoreType.DMA((N_DEV - 1,)), pltpu.SemaphoreType.DMA((N_DEV - 1,))],
    )(v)


def _adamw(name, w, g, m, v):
    R, C = w.shape
    tr = _tile(R, 256)
    c1 = 1.0 / (1.0 - ADAM_B1 ** ADAM_STEP)
    c2 = 1.0 / (1.0 - ADAM_B2 ** ADAM_STEP)

    def body(w_ref, g_ref, m_ref, v_ref, d_ref, nm_ref, nv_ref):
        gv = g_ref[...]
        nm = ADAM_B1 * m_ref[...] + (1.0 - ADAM_B1) * gv
        nv = ADAM_B2 * v_ref[...] + (1.0 - ADAM_B2) * gv * gv
        d_ref[...] = -ADAM_LR * ((nm * c1) / (jnp.sqrt(nv * c2) + ADAM_EPS) + ADAM_WD * w_ref[...])
        nm_ref[...] = nm
        nv_ref[...] = nv

    row = pl.BlockSpec((tr, C), lambda i: (i, 0))
    sh = jax.ShapeDtypeStruct((R, C), f32)
    return pl.pallas_call(body, name=name, grid=(R // tr,), in_specs=[row] * 4, out_specs=[row] * 3,
                          out_shape=[sh, sh, sh], compiler_params=_params("parallel"))(w, g, m, v)


def _adamw_update(w, g, m, v):
    c1 = 1.0 / (1.0 - ADAM_B1 ** ADAM_STEP)
    c2 = 1.0 / (1.0 - ADAM_B2 ** ADAM_STEP)
    nm = ADAM_B1 * m + (1.0 - ADAM_B1) * g
    nv = ADAM_B2 * v + (1.0 - ADAM_B2) * g * g
    return -ADAM_LR * ((nm * c1) / (jnp.sqrt(nv * c2) + ADAM_EPS) + ADAM_WD * w), nm, nv


def _adamw_many(ws, gs, ms, vs):
    n = len(ws)

    def body(*refs):
        ins, outs = refs[:4 * n], refs[4 * n:]
        for a in range(n):
            d, nm, nv = _adamw_update(ins[a][...], ins[n + a][...], ins[2 * n + a][...], ins[3 * n + a][...])
            outs[a][...] = d
            outs[n + a][...] = nm
            outs[2 * n + a][...] = nv

    shapes = [jax.ShapeDtypeStruct(a.shape, f32) for a in ws]
    return pl.pallas_call(body, name="adamw_small", out_shape=shapes * 3)(*ws, *gs, *ms, *vs)


def _zoh_parts(lr, li, log_dt):
    dt = jnp.exp(log_dt)
    mag = jnp.exp(lr * dt)
    c, s = jnp.cos(li * dt), jnp.sin(li * dt)
    ab_re, ab_im = mag * c, mag * s
    den = lr * lr + li * li
    nr = ab_re - 1.0
    coef_re = (nr * lr + ab_im * li) / den
    coef_im = (ab_im * lr - nr * li) / den
    return dt, mag, c, s, ab_re, ab_im, den, nr, coef_re, coef_im


def _zoh_fwd(lr, li, log_dt, b_re, b_im):
    def body(lr_ref, li_ref, ld_ref, br_ref, bi_ref, ar_ref, ai_ref, bbr_ref, bbi_ref):
        _, _, _, _, ab_re, ab_im, _, _, coef_re, coef_im = _zoh_parts(lr_ref[...], li_ref[...], ld_ref[...])
        ar_ref[...] = ab_re
        ai_ref[...] = ab_im
        bbr_ref[...] = coef_re * br_ref[...] - coef_im * bi_ref[...]
        bbi_ref[...] = coef_re * bi_ref[...] + coef_im * br_ref[...]

    col = jax.ShapeDtypeStruct(lr.shape, f32)
    mat = jax.ShapeDtypeStruct(b_re.shape, f32)
    return pl.pallas_call(body, name="zoh_fwd", out_shape=[col, col, mat, mat])(lr, li, log_dt, b_re, b_im)


def _zoh_bwd(lr, li, log_dt, b_re, b_im, d_ar, d_ai, d_bbr, d_bbi):
    groups = lr.shape[0] // SSM_STATE

    def body(lr_ref, li_ref, ld_ref, br_ref, bi_ref, dar_ref, dai_ref, dbbr_ref, dbbi_ref,
             dlr_ref, dli_ref, dld_ref, dbr_ref, dbi_ref):
        lr_, li_ = lr_ref[...], li_ref[...]
        dt, mag, c, s, _, ab_im, den, nr, coef_re, coef_im = _zoh_parts(lr_, li_, ld_ref[...])
        br, bi, dbbr, dbbi = br_ref[...], bi_ref[...], dbbr_ref[...], dbbi_ref[...]
        dbr_ref[...] = coef_re * dbbr + coef_im * dbbi
        dbi_ref[...] = coef_re * dbbi - coef_im * dbbr
        d_cr = jnp.sum(dbbr * br + dbbi * bi, axis=1, keepdims=True)
        d_ci = jnp.sum(dbbi * br - dbbr * bi, axis=1, keepdims=True)
        d_nr = (d_cr * lr_ - d_ci * li_) / den
        d_abi = dai_ref[...] + (d_cr * li_ + d_ci * lr_) / den
        d_abr = dar_ref[...] + d_nr
        d_den = -(d_cr * coef_re + d_ci * coef_im) / den
        d_lr = (d_cr * nr + d_ci * ab_im) / den + 2.0 * lr_ * d_den
        d_li = (d_cr * ab_im - d_ci * nr) / den + 2.0 * li_ * d_den
        d_theta = mag * (d_abi * c - d_abr * s)
        d_arg = mag * (d_abr * c + d_abi * s)
        dlr_ref[...] = d_lr + d_arg * dt
        dli_ref[...] = d_li + d_theta * dt
        d_dt = d_arg * lr_ + d_theta * li_
        dld_ref[...] = jnp.sum((d_dt * dt).reshape(groups, SSM_STATE, 1), axis=1)

    col = jax.ShapeDtypeStruct(lr.shape, f32)
    mat = jax.ShapeDtypeStruct(b_re.shape, f32)
    return pl.pallas_call(body, name="zoh_bwd", out_shape=[col, col, jax.ShapeDtypeStruct((groups, 1), f32), mat, mat])(
        lr, li, log_dt, b_re, b_im, d_ar, d_ai, d_bbr, d_bbi)


def _lower_bound_fwd(logits):
    def body(x_ref, o_ref):
        x = x_ref[...]
        e = jnp.exp(x - jnp.max(x, axis=0, keepdims=True))
        o_ref[...] = e / jnp.sum(e, axis=0, keepdims=True)

    return pl.pallas_call(body, name="lower_bound_fwd", out_shape=jax.ShapeDtypeStruct(logits.shape, f32))(logits)


def _lower_bound_bwd(sm, d_lb):
    def body(sm_ref, d_ref, o_ref):
        smv = sm_ref[...]
        row = lax.broadcasted_iota(jnp.int32, smv.shape, 0)
        sm0 = smv[0:1, :]
        o_ref[...] = sm0 * d_ref[...] * (jnp.where(row == 0, 1.0, 0.0) - smv)

    return pl.pallas_call(body, name="lower_bound_bwd", out_shape=jax.ShapeDtypeStruct(sm.shape, f32))(sm, d_lb)


def _s5_tables(ab_re, ab_im, bb_re, bb_im, c_re, c_im, seg):
    eye = jnp.eye(SLAB_GROUPS, dtype=f32)

    def blk_in(bb):
        return jnp.einsum("sgph,gk->sghkp", bb.reshape(N_SLAB, SLAB_GROUPS, SSM_STATE, SSM_GROUP), eye).reshape(
            N_SLAB, SLAB_CH, SLAB_NS)

    def blk_out(cc):
        return jnp.einsum("sghp,gk->skpgh", cc.reshape(N_SLAB, SLAB_GROUPS, SSM_GROUP, SSM_STATE), eye).reshape(
            N_SLAB, SLAB_NS, SLAB_CH)

    bs = jnp.concatenate([blk_in(bb_re), blk_in(bb_im)], axis=2).astype(bf16)
    cs = jnp.concatenate([blk_out(c_re), blk_out(-c_im)], axis=1).astype(bf16)
    n = SSM_GROUPS * SSM_STATE
    pw = _power_table(jnp.stack([ab_re.reshape(1, n), ab_im.reshape(1, n)]), -(-seg // SUBLANES))
    return bs, cs, pw


def _power_table(ab, tiles):
    n = ab.shape[2]

    def body(a_ref, o_ref):
        row = lax.broadcasted_iota(jnp.int32, (SUBLANES, n), 0)
        ar, ai = a_ref[0], a_ref[1]
        tr, ti = jnp.broadcast_to(ar, (SUBLANES, n)), jnp.broadcast_to(ai, (SUBLANES, n))
        pr, pi = ar, ai
        for r in range(1, SUBLANES):
            pr, pi = pr * ar - pi * ai, pr * ai + pi * ar
            tr = jnp.where(row == r, pr, tr)
            ti = jnp.where(row == r, pi, ti)
        o_ref[0, 0:SUBLANES, :] = tr
        o_ref[1, 0:SUBLANES, :] = ti

        def step(j, carry):
            cr, ci = carry
            cr, ci = cr * pr - ci * pi, cr * pi + ci * pr
            o_ref[0, _rows8(j), :] = cr
            o_ref[1, _rows8(j), :] = ci
            return cr, ci

        lax.fori_loop(1, tiles, step, (tr, ti))

    return pl.pallas_call(body, name="power_table", out_shape=jax.ShapeDtypeStruct((2, SUBLANES * tiles, n), f32))(ab)


def _s5_table_grads(dbs, dcs, da):
    eye = jnp.eye(SLAB_GROUPS, dtype=f32)
    d6 = dbs.reshape(N_SLAB, SLAB_GROUPS, SSM_GROUP, 2, SLAB_GROUPS, SSM_STATE)
    dbb = jnp.einsum("sghrkp,gk->rsgph", d6, eye).reshape(2, SSM_GROUPS, SSM_STATE, SSM_GROUP)
    c6 = dcs.reshape(N_SLAB, 2, SLAB_GROUPS, SSM_STATE, SLAB_GROUPS, SSM_GROUP)
    dcc = jnp.einsum("srkpgh,gk->rsghp", c6, eye).reshape(2, SSM_GROUPS, SSM_GROUP, SSM_STATE)
    dab = da.transpose(1, 0, 2).reshape(2, SSM_GROUPS, SSM_STATE)
    return dab[0], dab[1], dbb[0], dbb[1], dcc[0], -dcc[1]


SMALL = ["mix_norm_g", "ssm_lambda_re", "ssm_lambda_im", "ssm_log_dt", "ssm_b_re", "ssm_b_im", "ssm_c_re", "ssm_c_im",
         "ssm_d", "hgrn_lb_logits", "hgrn_norm_g", "ffn_norm_g", "conv_b", "final_norm_g"]
SHARDED_SMALL = ["meta_tokens", "conv_w"]
BIG = ["w_in", "ssm_w_glu", "w_ssm_proj", "w_hgrn_proj", "w_out", "w_up", "w_down"]
WEIGHTS = ['meta_tokens', 'mix_norm_g', 'w_in', 'ssm_lambda_re', 'ssm_lambda_im', 'ssm_log_dt', 'ssm_b_re', 'ssm_b_im',
           'ssm_c_re', 'ssm_c_im', 'ssm_d', 'ssm_w_glu', 'w_ssm_proj', 'hgrn_lb_logits', 'hgrn_norm_g', 'w_hgrn_proj',
           'w_out', 'ffn_norm_g', 'w_up', 'conv_w', 'conv_b', 'w_down', 'final_norm_g']


LATER = [k for k in BIG if k != "w_in"]


def _full_weights(gathered, shards, chip):
    Dm = D_MODEL
    g = {k: lax.dynamic_update_slice(gathered[k], shards[k][None], (chip, 0, 0)) for k in gathered}
    full = {}
    for k, v in g.items():
        if k == "w_in":
            full[k] = jnp.roll(v.transpose(1, 0, 2).reshape(Dm, IN_COLS), -Dm, axis=1)
        elif k == "w_up":
            full[k] = v.transpose(1, 0, 2).reshape(Dm, 2 * D_FF)
        else:
            full[k] = v.reshape(-1, Dm)
    return full


def _local_grads(x, tgt, meta, w, full, shards, chip, core):
    B, S, Dm = x.shape
    L = S + N_META
    T = B * L
    h0 = jnp.concatenate([jnp.broadcast_to(meta[None], (B, N_META, Dm)), x], axis=1).reshape(T, Dm)

    lb_all = _lower_bound_fwd(w["hgrn_lb_logits"])
    lb = lb_all[0:1]
    gp = SSM_GROUPS * SSM_STATE
    zoh_in = (w["ssm_lambda_re"].reshape(gp, 1), w["ssm_lambda_im"].reshape(gp, 1),
              jnp.repeat(w["ssm_log_dt"].reshape(SSM_GROUPS, 1), SSM_STATE, axis=0),
              w["ssm_b_re"].reshape(gp, SSM_GROUP), w["ssm_b_im"].reshape(gp, SSM_GROUP))
    ab_re, ab_im, bb_re, bb_im = _zoh_fwd(*zoh_in)
    gps = (SSM_GROUPS, SSM_STATE)
    bs, cs, pw = _s5_tables(ab_re.reshape(gps), ab_im.reshape(gps), bb_re.reshape(gps + (SSM_GROUP,)),
                            bb_im.reshape(gps + (SSM_GROUP,)), w["ssm_c_re"][0], w["ssm_c_im"][0], L // SUBLANES)

    z1 = _rmsnorm_fwd("mix_norm", h0, w["mix_norm_g"])
    p, gathered = _in_proj_gather(z1, full["w_in"], [shards[k] for k in LATER])
    full = {**full, **_full_weights(dict(zip(LATER, gathered)), shards, chip)}
    ya0 = _s5_fwd(p, bs, cs, pw, w["ssm_d"], B, L)
    gl, ya = _glu_proj_fwd(ya0, full["ssm_w_glu"])
    yb = _hgrn_fwd(p, lb, w["hgrn_norm_g"], B, L)
    pa, pb, merged = _proj_merge_fwd(ya, yb, full["w_ssm_proj"], full["w_hgrn_proj"], p)
    h1, z2 = _out_proj_norm(merged, full["w_out"], h0, w["ffn_norm_g"])
    up = _mm_rows("up_proj", z2, full["w_up"], "nn", f32, D_FF // 2)
    ff = _conv_fwd(up, full["conv_w"], w["conv_b"], B, L)
    h2 = _mm_rows("down_proj", ff, full["w_down"], "nn", f32, 1024, res=h1, tk=D_FF // 2)

    h2x = h2.reshape(B, L, Dm)[:, N_META:].reshape(B * S, Dm)
    dh2x, loss, d_final_g = _final_loss(h2x, tgt.reshape(B * S, Dm), w["final_norm_g"].reshape(1, Dm))
    dh2 = jnp.pad(dh2x.reshape(B, S, Dm), ((0, 0), (N_META, 0), (0, 0))).reshape(T, Dm)

    dff = _mm_rows("d_ff", dh2, full["w_down"], "nt", f32, D_FF // 2)
    g_w_down = _mm_wgrad("dw_down", ff, dh2, tn=512)
    dup, dconv = _conv_bwd(up, dff, full["conv_w"], w["conv_b"], B, L)
    g_w_up = _dw_up(z2, dup)
    dh1, d_ffn_g = _dz2_norm(dup, full["w_up"], h1, w["ffn_norm_g"], dh2)

    g_w_out = _mm_wgrad("dw_out", merged, dh1)
    dpa, dpb, dp = _merge_bwd_fused(dh1, full["w_out"], p, pa, pb)
    dgl, dya0_direct = _glu_bwd_fused(dpa, full["w_ssm_proj"], ya0, gl)
    g_w_ssm_proj = _mm_wgrad("dw_ssm_proj", ya, dpa)
    dyb = _mm_rows("d_yb", dpb, full["w_hgrn_proj"], "nt", f32, 1024)
    g_w_hgrn_proj = _mm_wgrad("dw_hgrn_proj", yb, dpb)
    dp, d_lb, d_hgrn_g = _hgrn_bwd(p, dyb, dp, lb, w["hgrn_norm_g"], B, L)
    dya0 = _mm_rows("d_ya0", dgl, full["ssm_w_glu"], "nt", f32, 1024, res=dya0_direct)
    g_w_glu = _mm_wgrad("dw_glu", ya0, dgl)
    parts = {
        "ssm_w_glu": g_w_glu.reshape(N_CHIPS, Dm // N_CHIPS, Dm), "w_ssm_proj": g_w_ssm_proj.reshape(N_CHIPS, Dm // N_CHIPS, Dm),
        "w_hgrn_proj": g_w_hgrn_proj.reshape(N_CHIPS, Dm // N_CHIPS, Dm), "w_out": g_w_out.reshape(N_CHIPS, Dm // N_CHIPS, Dm),
        "w_up": g_w_up, "w_down": g_w_down.reshape(N_CHIPS, D_FF // N_CHIPS, Dm),
    }
    got = _sibling_halves([parts[k] for k in LATER])
    sums = {k: _add_own_half("add_half_" + k, parts[k], gt, core) for k, gt in zip(LATER, got)}
    (dp, dbs, dcs, da, d_skip), slots_later = _s5_bwd(p, dya0, dp, bs, cs, pw, w["ssm_d"], B, L, [sums[k] for k in LATER])
    slots = dict(zip(LATER, slots_later))
    g_w_in = _dw_in(z1, dp)
    dh0, d_mix_g = _dz1_norm(dp, full["w_in"], h0, w["mix_norm_g"], dh1)

    dh0 = dh0.reshape(B, L, Dm)
    grad_x = dh0[:, N_META:]
    d_meta = _meta_grad(dh0[:, :N_META])

    d_ab_re, d_ab_im, d_bb_re, d_bb_im, d_c_re, d_c_im = _s5_table_grads(dbs, dcs, da)
    d_lr, d_li, d_log_dt, d_b_re, d_b_im = _zoh_bwd(*zoh_in, d_ab_re.reshape(gp, 1), d_ab_im.reshape(gp, 1),
                                                    d_bb_re.reshape(gp, SSM_GROUP), d_bb_im.reshape(gp, SSM_GROUP))
    d_lr, d_li, d_log_dt = d_lr.reshape(gps), d_li.reshape(gps), d_log_dt.reshape(SSM_GROUPS)
    d_b_re, d_b_im = d_b_re.reshape(gps + (SSM_GROUP,)), d_b_im.reshape(gps + (SSM_GROUP,))
    d_logits = _lower_bound_bwd(lb_all, d_lb)
    small = {
        "meta_tokens": d_meta, "mix_norm_g": d_mix_g, "ssm_lambda_re": d_lr[None], "ssm_lambda_im": d_li[None],
        "ssm_log_dt": d_log_dt[None], "ssm_b_re": d_b_re[None], "ssm_b_im": d_b_im[None], "ssm_c_re": d_c_re[None],
        "ssm_c_im": d_c_im[None], "ssm_d": d_skip, "hgrn_lb_logits": d_logits, "hgrn_norm_g": d_hgrn_g,
        "ffn_norm_g": d_ffn_g, "conv_w": dconv[:, 0:3, :].transpose(1, 0, 2).reshape(3, 2 * D_FF),
        "conv_b": dconv[:, 3, :].reshape(1, 2 * D_FF), "final_norm_g": d_final_g.reshape(Dm),
    }
    sums["w_in"] = _add_own_half_w_in(g_w_in, _sibling_halves([g_w_in], "sibling_halves_w_in")[0], core)
    slots["w_in"] = _chip_exchange([sums["w_in"]])[0]
    return loss, grad_x, sums, slots, small


PACK_ROWS = 256


def _pack(parts):
    flat = jnp.concatenate([parts[k].reshape(-1) for k in parts])
    n = flat.shape[0]
    rows = -(-n // (PACK_ROWS * LANES)) * PACK_ROWS
    flat = jnp.pad(flat, (0, rows * LANES - n))
    return flat.reshape(rows, LANES)


def _unpack(packed, like):
    flat = packed.reshape(-1)
    out, o = {}, 0
    for k, ref in like.items():
        n = math.prod(ref.shape)
        out[k] = flat[o:o + n].reshape(ref.shape)
        o += n
    return out


def kernel(x, meta_tokens, mix_norm_g, w_in, ssm_lambda_re, ssm_lambda_im, ssm_log_dt, ssm_b_re, ssm_b_im, ssm_c_re, ssm_c_im, ssm_d, ssm_w_glu, w_ssm_proj, hgrn_lb_logits, hgrn_norm_g, w_hgrn_proj, w_out, ffn_norm_g, w_up, conv_w, conv_b, w_down, final_norm_g, loss_target, m_meta_tokens, m_mix_norm_g, m_w_in, m_ssm_lambda_re, m_ssm_lambda_im, m_ssm_log_dt, m_ssm_b_re, m_ssm_b_im, m_ssm_c_re, m_ssm_c_im, m_ssm_d, m_ssm_w_glu, m_w_ssm_proj, m_hgrn_lb_logits, m_hgrn_norm_g, m_w_hgrn_proj, m_w_out, m_ffn_norm_g, m_w_up, m_conv_w, m_conv_b, m_w_down, m_final_norm_g, v_meta_tokens, v_mix_norm_g, v_w_in, v_ssm_lambda_re, v_ssm_lambda_im, v_ssm_log_dt, v_ssm_b_re, v_ssm_b_im, v_ssm_c_re, v_ssm_c_im, v_ssm_d, v_ssm_w_glu, v_w_ssm_proj, v_hgrn_lb_logits, v_hgrn_norm_g, v_w_hgrn_proj, v_w_out, v_ffn_norm_g, v_w_up, v_conv_w, v_conv_b, v_w_down, v_final_norm_g):
    args = dict(locals())
    w = {k: args[k] for k in WEIGHTS}
    mom = {k: args["m_" + k] for k in WEIGHTS}
    var = {k: args["v_" + k] for k in WEIGHTS}
    Dm = D_MODEL
    cx, cy, cc = lax.axis_index("x"), lax.axis_index("y"), lax.axis_index("c")
    chip = 2 * cx + cy

    shards = {k: w[k][0].astype(bf16) for k in BIG}
    g_meta, g_cw = _allgather_chips([w["meta_tokens"], w["conv_w"][0]])
    full = _full_weights({"w_in": _allgather_split([shards["w_in"]])[0]}, shards, chip)
    full["conv_w"] = g_cw.transpose(1, 0, 2).reshape(3, 2 * D_FF)
    meta_full = g_meta.transpose(1, 0, 2).reshape(N_META, Dm)

    core = cc.reshape(1).astype(jnp.int32)
    loss_part, grad_x, sums, slots, small = _local_grads(x, loss_target, meta_full, w, full, shards, chip, core)

    where = jnp.stack([chip, cc]).astype(jnp.int32)
    fulls = [_sum_chips("sum_chips_" + k, slots[k], sums[k], where) for k in BIG]
    g_big = dict(zip(BIG, _sibling_join(fulls)))

    small_all = dict(small)
    small_all["loss"] = loss_part[0, 0:1]
    packed = _pack(small_all)
    slots_dev = lax.dynamic_update_slice(_allgather_devices(packed), packed[None], (2 * chip + cc, 0, 0))
    reduced = _unpack(_sum_slots("sum_devices", slots_dev), small_all)
    loss = reduced.pop("loss")[0]
    mcols = Dm // N_CHIPS
    ccols = 2 * D_FF // N_CHIPS
    grads = {k: reduced[k] for k in SMALL}
    grads["meta_tokens"] = lax.dynamic_slice(reduced["meta_tokens"], (0, chip * mcols), (N_META, mcols))
    grads["conv_w"] = lax.dynamic_slice(reduced["conv_w"], (0, chip * ccols), (3, ccols))[None]
    for k in BIG:
        grads[k] = g_big[k][None]

    delta, new_m, new_v = {}, {}, {}
    for k in BIG:
        shp = w[k].shape
        d, nm, nv = _adamw("adamw_" + k, w[k][0], grads[k][0], mom[k][0], var[k][0])
        delta[k], new_m[k], new_v[k] = d.reshape(shp), nm.reshape(shp), nv.reshape(shp)
    rest = SMALL + SHARDED_SMALL

    def flat2(a):
        return a.reshape(-1, a.shape[-1])

    outs = _adamw_many(*[[flat2(t[k]) for k in rest] for t in (w, grads, mom, var)])
    n = len(rest)
    for j, dst in enumerate((delta, new_m, new_v)):
        dst.update({k: o.reshape(w[k].shape) for k, o in zip(rest, outs[j * n:(j + 1) * n])})

    return (loss, grad_x, *[grads[k].reshape(w[k].shape) for k in WEIGHTS], *[delta[k] for k in WEIGHTS],
            *[new_m[k] for k in WEIGHTS], *[new_v[k] for k in WEIGHTS])
```

```python
import math

import jax
import jax.numpy as jnp
from jax import lax
from jax.experimental import pallas as pl
from jax.experimental.pallas import tpu as pltpu

f32 = jnp.float32
bf16 = jnp.bfloat16

D_MODEL = 1024
N_META = 16
SSM_GROUP = 16
SSM_GROUPS = 64
SSM_STATE = 64
SLAB_GROUPS = 8
N_SLAB = SSM_GROUPS // SLAB_GROUPS
SLAB_CH = SLAB_GROUPS * SSM_GROUP
SLAB_NS = SLAB_GROUPS * SSM_STATE
HEADS = 8
HEAD_DIM = 128
CHUNK = 16
D_FF = 2816
IN_COLS = 7168
EPS = 1e-6
SUBLANES = 8
LANES = 128
N_CHIPS = 4
N_DEV = 8
ADAM_LR, ADAM_B1, ADAM_B2, ADAM_EPS, ADAM_WD, ADAM_STEP = 0.001, 0.9, 0.999, 1e-08, 0.01, 10
MESH = pl.DeviceIdType.MESH
ANY = pl.BlockSpec(memory_space=pl.ANY)

SEG_Q, SEG_F, SEG_I, SEG_OG, SEG_GA, SEG_GB, SEG_U = range(7)
N_SEG = 7


def _tile(n, target, mult=SUBLANES):
    best = None
    for d in range(mult, min(n, target) + 1, mult):
        if n % d == 0:
            best = d
    return n if best is None else best


def _params(*sem):
    return pltpu.CompilerParams(dimension_semantics=sem)


def _sigmoid(x):
    return 1.0 / (1.0 + jnp.exp(-x))


_DIMS = {"nn": (((1,), (0,)), ((), ())), "nt": (((1,), (1,)), ((), ())), "tn": (((0,), (0,)), ((), ()))}


def _mm(name, a, b, dims, grid, a_spec, b_spec, out_shape, out_spec, acc_shape, res=None, res_spec=None):
    nk = grid[2]
    dn = _DIMS[dims]

    def body(*refs):
        if res is None:
            a_ref, b_ref, o_ref, acc = refs
        else:
            a_ref, b_ref, r_ref, o_ref, acc = refs
        k = pl.program_id(2)

        @pl.when(k == 0)
        def _():
            acc[...] = jnp.zeros_like(acc)

        acc[...] += lax.dot_general(a_ref[...].astype(bf16), b_ref[...].astype(bf16), dn, preferred_element_type=f32)

        @pl.when(k == nk - 1)
        def _():
            r = acc[...]
            if res is not None:
                r = r + r_ref[...]
            o_ref[...] = r.astype(o_ref.dtype)

    ins = [a, b] + ([] if res is None else [res])
    specs = [a_spec, b_spec] + ([] if res is None else [res_spec])
    return pl.pallas_call(
        body, name=name, grid=grid, in_specs=specs, out_specs=out_spec, out_shape=out_shape,
        scratch_shapes=[pltpu.VMEM(acc_shape, f32)],
        compiler_params=_params("parallel", "parallel", "arbitrary"),
    )(*ins)


def _mm_rows(name, a, w, dims, out_dtype, tn, res=None, tk=None):
    T, K = a.shape
    N = w.shape[1] if dims == "nn" else w.shape[0]
    tm = _tile(T, 1032)
    tk = K if tk is None else tk
    grid = (T // tm, N // tn, K // tk)
    a_spec = pl.BlockSpec((tm, tk), lambda i, j, k: (i, k))
    if dims == "nn":
        b_spec = pl.BlockSpec((tk, tn), lambda i, j, k: (k, j))
    else:
        b_spec = pl.BlockSpec((tn, tk), lambda i, j, k: (j, k))
    o_spec = pl.BlockSpec((tm, tn), lambda i, j, k: (i, j))
    return _mm(name, a, w, dims, grid, a_spec, b_spec, jax.ShapeDtypeStruct((T, N), out_dtype), o_spec, (tm, tn),
               res=res, res_spec=None if res is None else o_spec)


def _mm_fused(name, pairs, dims, extras, epilogue, outs, rows=()):
    T, K = pairs[0][0].shape
    N = pairs[0][1].shape[1] if dims == "nn" else pairs[0][1].shape[0]
    tm = _tile(T, 344)
    tn = N
    grid = (T // tm, N // tn)
    npair, nex = len(pairs), len(extras) + len(rows)
    dn = _DIMS[dims]

    def body(*refs):
        ab = refs[:2 * npair]
        ex = refs[2 * npair:2 * npair + nex]
        o_refs = refs[2 * npair + nex:]
        accs = [lax.dot_general(ab[2 * q][...].astype(bf16), ab[2 * q + 1][...].astype(bf16), dn, preferred_element_type=f32)
                for q in range(npair)]
        vals = epilogue(accs, [e[...] for e in ex])
        for o_ref, v in zip(o_refs, vals):
            if isinstance(v, (list, tuple)):
                for s_, vs in enumerate(v):
                    o_ref[s_] = vs.astype(o_ref.dtype)
            else:
                o_ref[...] = v.astype(o_ref.dtype)

    ins, specs = [], []
    for a, w in pairs:
        ins += [a, w]
        specs.append(pl.BlockSpec((tm, K), lambda i, j: (i, 0)))
        specs.append(pl.BlockSpec((K, tn), lambda i, j: (0, j)) if dims == "nn" else pl.BlockSpec((tn, K), lambda i, j: (j, 0)))
    for arr, off in extras:
        ins.append(arr)
        specs.append(pl.BlockSpec((tm, tn), lambda i, j, off=off: (i, off + j)))
    for arr in rows:
        ins.append(arr)
        specs.append(pl.BlockSpec((1, tn), lambda i, j: (0, j)))
    shapes, ospecs = [], []
    for o in outs:
        if isinstance(o, tuple):
            dt, nseg, total, blk = o
            shapes.append(jax.ShapeDtypeStruct((total, T, N), dt))
            ospecs.append(pl.BlockSpec((nseg, tm, tn), lambda i, j, blk=blk: (blk, i, j)))
        else:
            shapes.append(jax.ShapeDtypeStruct((T, N), o))
            ospecs.append(pl.BlockSpec((tm, tn), lambda i, j: (i, j)))
    return pl.pallas_call(body, name=name, grid=grid, in_specs=specs, out_specs=ospecs, out_shape=shapes,
                          compiler_params=_params("parallel", "parallel"))(*ins)


def _glu_proj_fwd(ya0, w_glu):
    def epi(accs, tiles):
        return accs[0], tiles[0] * _sigmoid(accs[0])

    return _mm_fused("glu_proj", [(ya0, w_glu)], "nn", [(ya0, 0)], epi, [f32, bf16])


def _proj_merge_fwd(ya, yb, w_sp, w_hp, p):
    def epi(accs, tiles):
        return accs[0], accs[1], _sigmoid(tiles[0]) * accs[0] + _sigmoid(tiles[1]) * accs[1]

    return _mm_fused("proj_merge", [(ya, w_sp), (yb, w_hp)], "nn", [(p, SEG_GA), (p, SEG_GB)], epi, [f32, f32, bf16])


def _merge_bwd_fused(dh1, w_out, p, pa, pb):
    def epi(accs, tiles):
        d = accs[0]
        sa, sb = _sigmoid(tiles[0]), _sigmoid(tiles[1])
        return d * sa, d * sb, [d * tiles[2] * sa * (1.0 - sa), d * tiles[3] * sb * (1.0 - sb)]

    return _mm_fused("d_merged", [(dh1, w_out)], "nt", [(p, SEG_GA), (p, SEG_GB), (pa, 0), (pb, 0)], epi,
                     [bf16, bf16, (bf16, 2, N_SEG, SEG_GA // 2)])


def _out_proj_norm(merged, w_out, h0, g):
    def epi(accs, tiles):
        h1 = tiles[0] + accs[0]
        r = lax.rsqrt(jnp.mean(h1 * h1, axis=-1, keepdims=True) + EPS)
        return h1, h1 * r * tiles[1]

    return _mm_fused("out_proj", [(merged, w_out)], "nn", [(h0, 0)], epi, [f32, bf16], rows=[g])


def _mm_rmsnorm_bwd(name, a, b, grid, a_spec, b_spec, x, g, dres):
    T, Dm = x.shape
    tm = T // grid[0]
    nk = grid[2]

    def body(a_ref, b_ref, x_ref, g_ref, dres_ref, dx_ref, dg_ref, acc):
        i, k = pl.program_id(0), pl.program_id(2)

        @pl.when(k == 0)
        def _():
            acc[...] = jnp.zeros_like(acc)

        @pl.when((i == 0) & (k == 0))
        def _():
            dg_ref[...] = jnp.zeros_like(dg_ref)

        acc[...] += lax.dot_general(a_ref[...].astype(bf16), b_ref[...].astype(bf16), _DIMS["nt"], preferred_element_type=f32)

        @pl.when(k == nk - 1)
        def _():
            xv = x_ref[...]
            r = lax.rsqrt(jnp.mean(xv * xv, axis=-1, keepdims=True) + EPS)
            xn = xv * r
            dzv = acc[...]
            dzg = dzv * g_ref[...]
            dx_ref[...] = dres_ref[...] + r * (dzg - xn * jnp.mean(dzg * xn, axis=-1, keepdims=True))
            dg_ref[...] += jnp.sum(dzv * xn, axis=0, keepdims=True)

    row = pl.BlockSpec((tm, Dm), lambda i, j, k: (i, 0))
    par = pl.BlockSpec((1, Dm), lambda i, j, k: (0, 0))
    return pl.pallas_call(
        body, name=name, grid=grid, in_specs=[a_spec, b_spec, row, par, row], out_specs=[row, par],
        out_shape=[jax.ShapeDtypeStruct((T, Dm), f32), jax.ShapeDtypeStruct((1, Dm), f32)],
        scratch_shapes=[pltpu.VMEM((tm, Dm), f32)],
        compiler_params=_params("arbitrary", "arbitrary", "arbitrary"),
    )(a, b, x, g, dres)


def _glu_bwd_fused(dpa, w_sp, ya0, gl):
    def epi(accs, tiles):
        d = accs[0]
        s = _sigmoid(tiles[1])
        return d * tiles[0] * s * (1.0 - s), d * s

    return _mm_fused("d_ya", [(dpa, w_sp)], "nt", [(ya0, 0), (gl, 0)], epi, [bf16, f32])


def _mm_wgrad(name, a, g, tn=None):
    T, K = a.shape
    N = g.shape[1]
    tk = _tile(T, 688)
    tn = N if tn is None else tn
    grid = (1, N // tn, T // tk)
    a_spec = pl.BlockSpec((tk, K), lambda i, j, k: (k, 0))
    g_spec = pl.BlockSpec((tk, tn), lambda i, j, k: (k, j))
    o_spec = pl.BlockSpec((K, tn), lambda i, j, k: (0, j))
    return _mm(name, a, g, "tn", grid, a_spec, g_spec, jax.ShapeDtypeStruct((K, N), f32), o_spec, (K, tn))


def _rmsnorm_fwd(name, x, g):
    T, Dm = x.shape
    tr = _tile(T, 688)

    def body(x_ref, g_ref, z_ref):
        xv = x_ref[...]
        r = lax.rsqrt(jnp.mean(xv * xv, axis=-1, keepdims=True) + EPS)
        z_ref[...] = (xv * r * g_ref[...]).astype(z_ref.dtype)

    return pl.pallas_call(
        body, name=name, grid=(T // tr,),
        in_specs=[pl.BlockSpec((tr, Dm), lambda i: (i, 0)), pl.BlockSpec((1, Dm), lambda i: (0, 0))],
        out_specs=pl.BlockSpec((tr, Dm), lambda i: (i, 0)),
        out_shape=jax.ShapeDtypeStruct((T, Dm), bf16), compiler_params=_params("parallel"),
    )(x, g)


def _final_loss(h2x, tgt, g):
    T, Dm = h2x.shape
    tr = _tile(T, 512)

    def body(h_ref, t_ref, g_ref, dh_ref, loss_ref, dg_ref):
        hv = h_ref[...]
        r = lax.rsqrt(jnp.mean(hv * hv, axis=-1, keepdims=True) + EPS)
        xn = hv * r
        gv = g_ref[...]
        err = xn * gv - t_ref[...]
        dy = err * (1.0 / Dm)
        dyg = dy * gv
        dh_ref[...] = r * (dyg - xn * jnp.mean(dyg * xn, axis=-1, keepdims=True))

        @pl.when(pl.program_id(0) == 0)
        def _():
            dg_ref[...] = jnp.zeros_like(dg_ref)
            loss_ref[...] = jnp.zeros_like(loss_ref)

        dg_ref[...] += jnp.sum(dy * xn, axis=0, keepdims=True)
        loss_ref[...] += jnp.sum(err * err) * (0.5 / Dm)

    row = pl.BlockSpec((tr, Dm), lambda i: (i, 0))
    par = pl.BlockSpec((1, Dm), lambda i: (0, 0))
    return pl.pallas_call(
        body, name="final_loss", grid=(T // tr,), in_specs=[row, row, par],
        out_specs=[row, pl.BlockSpec((1, LANES), lambda i: (0, 0)), par],
        out_shape=[jax.ShapeDtypeStruct((T, Dm), f32), jax.ShapeDtypeStruct((1, LANES), f32), jax.ShapeDtypeStruct((1, Dm), f32)],
        compiler_params=_params("arbitrary"),
    )(h2x, tgt, g)


def _meta_grad(dh0_meta):
    B = dh0_meta.shape[0]

    def body(d_ref, o_ref):
        acc = d_ref[0]
        for b in range(1, B):
            acc = acc + d_ref[b]
        o_ref[...] = acc

    return pl.pallas_call(body, name="meta_grad", out_shape=jax.ShapeDtypeStruct(dh0_meta.shape[1:], f32))(dh0_meta)


def _shift_down(x, k, row):
    return jnp.where(row >= k, pltpu.roll(x, k, 0), 0.0)


def _conv_fwd(up, conv_w, conv_b, B, L):
    tc = 256
    nt = D_FF // tc

    def body(xa_ref, xb_ref, wa_ref, wb_ref, ba_ref, bb_ref, o_ref):
        row = lax.broadcasted_iota(jnp.int32, (L, tc), 0)

        def conv(x_ref, w_ref, b_ref):
            x = x_ref[...]
            return (b_ref[...] + w_ref[0:1, :] * _shift_down(x, 2, row) + w_ref[1:2, :] * _shift_down(x, 1, row)
                    + w_ref[2:3, :] * x)

        a = conv(xa_ref, wa_ref, ba_ref)
        b = conv(xb_ref, wb_ref, bb_ref)
        o_ref[...] = (a * _sigmoid(a) * b).astype(o_ref.dtype)

    return pl.pallas_call(
        body, name="conv_fwd", grid=(B, nt),
        in_specs=[pl.BlockSpec((L, tc), lambda b, j: (b, j)), pl.BlockSpec((L, tc), lambda b, j: (b, j + nt)),
                  pl.BlockSpec((3, tc), lambda b, j: (0, j)), pl.BlockSpec((3, tc), lambda b, j: (0, j + nt)),
                  pl.BlockSpec((1, tc), lambda b, j: (0, j)), pl.BlockSpec((1, tc), lambda b, j: (0, j + nt))],
        out_specs=pl.BlockSpec((L, tc), lambda b, j: (b, j)),
        out_shape=jax.ShapeDtypeStruct((B * L, D_FF), bf16), compiler_params=_params("parallel", "parallel"),
    )(up, up, conv_w, conv_w, conv_b, conv_b)


CONV_ROWS = 2 * SUBLANES


def _rows16(i):
    return pl.ds(pl.multiple_of(i * CONV_ROWS, CONV_ROWS), CONV_ROWS)


def _conv_taps(x_ref, i, row):
    x = x_ref[_rows16(i), :]
    live = jnp.where(i > 0, 1.0, 0.0)
    r0 = jnp.maximum(i * CONV_ROWS, 2)
    p1 = x_ref[pl.ds(r0 - 1, 1), :] * live
    p2 = x_ref[pl.ds(r0 - 2, 1), :] * live
    x1 = jnp.where(row == 0, p1, pltpu.roll(x, 1, 0))
    x2 = jnp.where(row == 0, p2, jnp.where(row == 1, p1, pltpu.roll(x, 2, 0)))
    return x, x1, x2


def _conv_bwd(up, dff, conv_w, conv_b, B, L):
    tc = 256
    nt = D_FF // tc
    n = L // CONV_ROWS

    def body(xa_ref, xb_ref, d_ref, wa_ref, wb_ref, ba_ref, bb_ref, dup_ref, dw_ref, ga_ref, gb_ref):
        row = lax.broadcasted_iota(jnp.int32, (CONV_ROWS, tc), 0)

        @pl.when(pl.program_id(1) == 0)
        def _():
            dw_ref[...] = jnp.zeros_like(dw_ref)

        zero_tail = jnp.zeros((CONV_ROWS, tc), f32)
        ga_ref[L:L + CONV_ROWS, :] = zero_tail
        gb_ref[L:L + CONV_ROWS, :] = zero_tail

        def fold(v):
            return v[0:SUBLANES, :] + v[SUBLANES:CONV_ROWS, :]

        def step(i, acc):
            taps_a = _conv_taps(xa_ref, i, row)
            taps_b = _conv_taps(xb_ref, i, row)
            a = ba_ref[...] + wa_ref[0:1, :] * taps_a[2] + wa_ref[1:2, :] * taps_a[1] + wa_ref[2:3, :] * taps_a[0]
            b = bb_ref[...] + wb_ref[0:1, :] * taps_b[2] + wb_ref[1:2, :] * taps_b[1] + wb_ref[2:3, :] * taps_b[0]
            s = _sigmoid(a)
            d = d_ref[_rows16(i), :]
            g_a = d * b * s * (1.0 + a * (1.0 - s))
            g_b = d * a * s
            ga_ref[_rows16(i), :] = g_a
            gb_ref[_rows16(i), :] = g_b
            new = []
            for g, (x, x1, x2) in ((g_a, taps_a), (g_b, taps_b)):
                new += [fold(g * x2), fold(g * x1), fold(g * x), fold(g)]
            return tuple(o + v for o, v in zip(acc, new))

        z = jnp.zeros((SUBLANES, tc), f32)
        acc = _repeat_loop(n, step, (z,) * 8)
        for h in range(2):
            for t in range(4):
                dw_ref[h, t:t + 1, :] += jnp.sum(acc[4 * h + t], axis=0, keepdims=True)

        def back(i, c):
            for h, (g_ref, w_ref) in enumerate(((ga_ref, wa_ref), (gb_ref, wb_ref))):
                g = g_ref[_rows16(i), :]
                n1 = g_ref[pl.ds(i * CONV_ROWS + CONV_ROWS, 1), :]
                n2 = g_ref[pl.ds(i * CONV_ROWS + CONV_ROWS + 1, 1), :]
                u1 = jnp.where(row == CONV_ROWS - 1, n1, pltpu.roll(g, CONV_ROWS - 1, 0))
                u2 = jnp.where(row == CONV_ROWS - 1, n2, jnp.where(row == CONV_ROWS - 2, n1, pltpu.roll(g, CONV_ROWS - 2, 0)))
                dup_ref[h, _rows16(i), :] = (w_ref[2:3, :] * g + w_ref[1:2, :] * u1 + w_ref[0:1, :] * u2).astype(dup_ref.dtype)
            return c

        _repeat_loop(n, back, 0)

    return pl.pallas_call(
        body, name="conv_bwd", grid=(nt, B),
        in_specs=[pl.BlockSpec((L, tc), lambda j, b: (b, j)), pl.BlockSpec((L, tc), lambda j, b: (b, j + nt)),
                  pl.BlockSpec((L, tc), lambda j, b: (b, j)),
                  pl.BlockSpec((3, tc), lambda j, b: (0, j)), pl.BlockSpec((3, tc), lambda j, b: (0, j + nt)),
                  pl.BlockSpec((1, tc), lambda j, b: (0, j)), pl.BlockSpec((1, tc), lambda j, b: (0, j + nt))],
        out_specs=[pl.BlockSpec((2, L, tc), lambda j, b: (0, b, j)), pl.BlockSpec((2, SUBLANES, tc), lambda j, b: (0, 0, j))],
        out_shape=[jax.ShapeDtypeStruct((2, B * L, D_FF), bf16), jax.ShapeDtypeStruct((2, SUBLANES, D_FF), f32)],
        scratch_shapes=[pltpu.VMEM((L + CONV_ROWS, tc), f32), pltpu.VMEM((L + CONV_ROWS, tc), f32)],
        compiler_params=_params("parallel", "arbitrary"),
    )(up, up, dff, conv_w, conv_w, conv_b, conv_b)


GELU_C = math.sqrt(2.0 / math.pi)
GELU_A = 0.044715


def _gelu(x):
    return 0.5 * x * (1.0 + jnp.tanh(GELU_C * (x + GELU_A * x * x * x)))


def _gelu_grad(x):
    t = jnp.tanh(GELU_C * (x + GELU_A * x * x * x))
    return 0.5 * (1.0 + t) + 0.5 * x * (1.0 - t * t) * GELU_C * (1.0 + 3.0 * GELU_A * x * x)


def _cmul_add(xr, xi, ar, ai, sr, si):
    return xr + ar * sr - ai * si, xi + ar * si + ai * sr


def _s5_project_in(u_ref, bs_ref, s_ref, L, rc):
    for r in range(0, L, rc):
        s_ref[r:r + rc, :] = jnp.dot(u_ref[r:r + rc, :].astype(bf16), bs_ref[...], preferred_element_type=f32)


def _rows8(i):
    return pl.ds(pl.multiple_of(i * SUBLANES, SUBLANES), SUBLANES)


def _repeat_loop(n, step, init):
    rep = max(u for u in (6, 4, 3, 2, 1) if n % u == 0)

    def body(t, carry):
        for u in range(rep):
            carry = step(t * rep + u, carry)
        return carry

    return lax.fori_loop(0, n // rep, body, init)


def _to_segments(src_ref, dst_ref, seg):
    def step(i, c):
        dst_ref[_rows8(i), :] = src_ref[pl.ds(i, SUBLANES, stride=seg), :]
        return c

    _repeat_loop(seg, step, 0)


def _from_segments(src_ref, dst_ref, seg):
    def step(i, c):
        dst_ref[pl.ds(i, SUBLANES, stride=seg), :] = src_ref[_rows8(i), :]
        return c

    _repeat_loop(seg, step, 0)


def _seg_local_scan(s_ref, ar, ai, seg, reverse):
    ns = SLAB_NS

    def step(j, carry):
        cr, ci = carry
        rows = _rows8(seg - 1 - j if reverse else j)
        cr, ci = _cmul_add(s_ref[rows, 0:ns], s_ref[rows, ns:2 * ns], ar, ai, cr, ci)
        s_ref[rows, 0:ns] = cr
        s_ref[rows, ns:2 * ns] = ci
        return cr, ci

    z = jnp.zeros((SUBLANES, ns), f32)
    return _repeat_loop(seg, step, (z, z))


def _seg_boundaries(fr, fi, alr, ali, reverse):
    row = lax.broadcasted_iota(jnp.int32, fr.shape, 0)
    br = jnp.zeros_like(fr)
    bi = jnp.zeros_like(fi)
    for r in (range(SUBLANES - 2, -1, -1) if reverse else range(1, SUBLANES)):
        s = r + 1 if reverse else r - 1
        nr, ni = _cmul_add(fr[s:s + 1, :], fi[s:s + 1, :], alr, ali, br[s:s + 1, :], bi[s:s + 1, :])
        br = jnp.where(row == r, nr, br)
        bi = jnp.where(row == r, ni, bi)
    return br, bi


def _s5_states(u_ref, bs_ref, pw_ref, up_ref, s_ref, L, rc):
    seg = L // SUBLANES
    ns = SLAB_NS
    _to_segments(u_ref, up_ref, seg)
    _s5_project_in(up_ref, bs_ref, s_ref, L, rc)
    ar, ai = pw_ref[0, 0:1, :], pw_ref[1, 0:1, :]
    fr, fi = _seg_local_scan(s_ref, ar, ai, seg, False)
    br, bi = _seg_boundaries(fr, fi, pw_ref[0, seg - 1:seg, :], pw_ref[1, seg - 1:seg, :], False)

    def fix(i, c):
        rows = _rows8(i)
        xr, xi = _cmul_add(s_ref[rows, 0:ns], s_ref[rows, ns:2 * ns], pw_ref[0, pl.ds(i, 1), :], pw_ref[1, pl.ds(i, 1), :], br, bi)
        s_ref[rows, 0:ns] = xr
        s_ref[rows, ns:2 * ns] = xi
        return c

    _repeat_loop(seg, fix, 0)


def _pw_spec(seg_rows, order):
    if order == "bs":
        return pl.BlockSpec((2, seg_rows, SLAB_NS), lambda b, s: (0, 0, s))
    return pl.BlockSpec((2, seg_rows, SLAB_NS), lambda s, b: (0, 0, s))


def _s5_fwd(p, bs, cs, pw, d_skip, B, L):
    rc = _tile(L, 344)
    seg = L // SUBLANES

    def body(u_ref, bs_ref, cs_ref, pw_ref, d_ref, y_ref, s_ref, up_ref, yp_ref):
        _s5_states(u_ref, bs_ref, pw_ref, up_ref, s_ref, L, rc)
        for r in range(0, L, rc):
            ypre = (jnp.dot(s_ref[r:r + rc, :].astype(bf16), cs_ref[...], preferred_element_type=f32)
                    + d_ref[...] * up_ref[r:r + rc, :])
            yp_ref[r:r + rc, :] = _gelu(ypre)
        _from_segments(yp_ref, y_ref, seg)

    ucol = SEG_U * (D_MODEL // SLAB_CH)
    return pl.pallas_call(
        body, name="s5_fwd", grid=(B, N_SLAB),
        in_specs=[pl.BlockSpec((L, SLAB_CH), lambda b, s: (b, ucol + s)),
                  pl.BlockSpec((None, SLAB_CH, 2 * SLAB_NS), lambda b, s: (s, 0, 0)),
                  pl.BlockSpec((None, 2 * SLAB_NS, SLAB_CH), lambda b, s: (s, 0, 0)),
                  _pw_spec(pw.shape[1], "bs"),
                  pl.BlockSpec((1, SLAB_CH), lambda b, s: (0, s))],
        out_specs=pl.BlockSpec((L, SLAB_CH), lambda b, s: (b, s)),
        out_shape=jax.ShapeDtypeStruct((B * L, D_MODEL), f32),
        scratch_shapes=[pltpu.VMEM((L, 2 * SLAB_NS), f32), pltpu.VMEM((L, SLAB_CH), f32), pltpu.VMEM((L, SLAB_CH), f32)],
        compiler_params=_params("parallel", "parallel"),
    )(p, bs, cs, pw, d_skip)


def _s5_bwd(p, dya0, dp, bs, cs, pw, d_skip, B, L, sums):
    rc = _tile(L, 344)
    ns = SLAB_NS
    seg = L // SUBLANES
    nx = len(sums)

    def body(u_ref, dy_ref, dp_in, bs_ref, cs_ref, pw_ref, d_ref, *rest):
        xin, (du_ref, dbs_ref, dcs_ref, da_ref, dd_ref), xout = rest[:nx], rest[nx:nx + 5], rest[nx + 5:2 * nx + 5]
        s_ref, lam_ref, up_ref, dyp_ref, nat_ref, send, recv = rest[2 * nx + 5:]
        del dp_in
        start, finish = _chip_exchange_steps(xin, xout, send, recv)

        @pl.when((pl.program_id(0) == 0) & (pl.program_id(1) == 0))
        def _():
            start()

        @pl.when(pl.program_id(1) == 0)
        def _():
            dbs_ref[...] = jnp.zeros_like(dbs_ref)
            dcs_ref[...] = jnp.zeros_like(dcs_ref)
            da_ref[...] = jnp.zeros_like(da_ref)
            dd_ref[...] = jnp.zeros_like(dd_ref)

        _s5_states(u_ref, bs_ref, pw_ref, up_ref, s_ref, L, rc)
        _to_segments(dy_ref, dyp_ref, seg)
        for r in range(0, L, rc):
            u = up_ref[r:r + rc, :]
            sb = s_ref[r:r + rc, :].astype(bf16)
            ypre = jnp.dot(sb, cs_ref[...], preferred_element_type=f32) + d_ref[...] * u
            dyp = dyp_ref[r:r + rc, :] * _gelu_grad(ypre)
            dyp_ref[r:r + rc, :] = dyp
            dd_ref[...] += jnp.sum(dyp * u, axis=0, keepdims=True)
            dypb = dyp.astype(bf16)
            dcs_ref[...] += lax.dot_general(sb, dypb, _DIMS["tn"], preferred_element_type=f32)
            lam_ref[r:r + rc, :] = lax.dot_general(dypb, cs_ref[...], _DIMS["nt"], preferred_element_type=f32)

        ar, ai = pw_ref[0, 0:1, :], -pw_ref[1, 0:1, :]
        fr, fi = _seg_local_scan(lam_ref, ar, ai, seg, True)
        br, bi = _seg_boundaries(fr, fi, pw_ref[0, seg - 1:seg, :], -pw_ref[1, seg - 1:seg, :], True)

        def fix(i, acc):
            accr, acci = acc
            rows = _rows8(i)
            k = seg - 1 - i
            xr, xi = _cmul_add(lam_ref[rows, 0:ns], lam_ref[rows, ns:2 * ns], pw_ref[0, pl.ds(k, 1), :],
                               -pw_ref[1, pl.ds(k, 1), :], br, bi)
            lam_ref[rows, 0:ns] = xr
            lam_ref[rows, ns:2 * ns] = xi
            prev = _rows8(jnp.maximum(i - 1, 0))
            live = jnp.where(i > 0, 1.0, 0.0)
            spr = s_ref[prev, 0:ns] * live
            spi = s_ref[prev, ns:2 * ns] * live
            return accr + xr * spr + xi * spi, acci + xi * spr - xr * spi

        z = jnp.zeros((SUBLANES, ns), f32)
        accr, acci = _repeat_loop(seg, fix, (z, z))
        row = lax.broadcasted_iota(jnp.int32, (SUBLANES, ns), 0)
        last = _rows8(seg - 1)
        spr = jnp.where(row == 0, 0.0, pltpu.roll(s_ref[last, 0:ns], 1, 0))
        spi = jnp.where(row == 0, 0.0, pltpu.roll(s_ref[last, ns:2 * ns], 1, 0))
        xr, xi = lam_ref[0:SUBLANES, 0:ns], lam_ref[0:SUBLANES, ns:2 * ns]
        accr = accr + xr * spr + xi * spi
        acci = acci + xi * spr - xr * spi
        da_ref[0:1, :] += jnp.sum(accr, axis=0, keepdims=True)
        da_ref[1:2, :] += jnp.sum(acci, axis=0, keepdims=True)

        for r in range(0, L, rc):
            lamb = lam_ref[r:r + rc, :].astype(bf16)
            dbs_ref[...] += lax.dot_general(up_ref[r:r + rc, :].astype(bf16), lamb, _DIMS["tn"], preferred_element_type=f32)
            nat_ref[r:r + rc, :] = (lax.dot_general(lamb, bs_ref[...], _DIMS["nt"], preferred_element_type=f32)
                                    + d_ref[...] * dyp_ref[r:r + rc, :])
        _from_segments(nat_ref, up_ref, seg)
        du_ref[...] = up_ref[...].astype(du_ref.dtype)

        @pl.when((pl.program_id(0) == N_SLAB - 1) & (pl.program_id(1) == B - 1))
        def _():
            finish()

    ucol = SEG_U * (D_MODEL // SLAB_CH)
    T = B * L
    col = pltpu.VMEM((L, SLAB_CH), f32)
    res = pl.pallas_call(
        body, name="s5_bwd", grid=(N_SLAB, B),
        in_specs=[pl.BlockSpec((L, SLAB_CH), lambda s, b: (b, ucol + s)),
                  pl.BlockSpec((L, SLAB_CH), lambda s, b: (b, s)),
                  ANY,
                  pl.BlockSpec((None, SLAB_CH, 2 * SLAB_NS), lambda s, b: (s, 0, 0)),
                  pl.BlockSpec((None, 2 * SLAB_NS, SLAB_CH), lambda s, b: (s, 0, 0)),
                  _pw_spec(pw.shape[1], "sb"),
                  pl.BlockSpec((1, SLAB_CH), lambda s, b: (0, s))] + [ANY] * nx,
        out_specs=[pl.BlockSpec((None, L, SLAB_CH), lambda s, b: (SEG_U, b, s)),
                   pl.BlockSpec((None, SLAB_CH, 2 * SLAB_NS), lambda s, b: (s, 0, 0)),
                   pl.BlockSpec((None, 2 * SLAB_NS, SLAB_CH), lambda s, b: (s, 0, 0)),
                   pl.BlockSpec((None, 2, SLAB_NS), lambda s, b: (s, 0, 0)),
                   pl.BlockSpec((1, SLAB_CH), lambda s, b: (0, s))] + [ANY] * nx,
        out_shape=[jax.ShapeDtypeStruct((N_SEG, T, D_MODEL), bf16),
                   jax.ShapeDtypeStruct((N_SLAB, SLAB_CH, 2 * SLAB_NS), f32),
                   jax.ShapeDtypeStruct((N_SLAB, 2 * SLAB_NS, SLAB_CH), f32),
                   jax.ShapeDtypeStruct((N_SLAB, 2, SLAB_NS), f32),
                   jax.ShapeDtypeStruct((1, D_MODEL), f32)] + [jax.ShapeDtypeStruct(a.shape, a.dtype) for a in sums],
        scratch_shapes=[pltpu.VMEM((L, 2 * SLAB_NS), f32), pltpu.VMEM((L, 2 * SLAB_NS), f32), col, col, col]
        + _chip_exchange_sems(nx),
        input_output_aliases={2: 0},
        compiler_params=_params("arbitrary", "arbitrary"),
    )(p, dya0, dp, bs, cs, pw, d_skip, *sums)
    return res[:5], res[5:]


def _dotb(a, b, dims="nn"):
    return lax.dot_general(a.astype(bf16), b.astype(bf16), _DIMS[dims], preferred_element_type=f32)


def _tile_scan(x, reverse):
    n, w = x.shape
    v = x.reshape(n // SUBLANES, SUBLANES, w)
    row = lax.broadcasted_iota(jnp.int32, v.shape, 1)
    for k in (1, 2, 4):
        if reverse:
            v = v + jnp.where(row < SUBLANES - k, pltpu.roll(v, SUBLANES - k, 1), 0.0)
        else:
            v = v + jnp.where(row >= k, pltpu.roll(v, k, 1), 0.0)
    p = v.reshape(n // CHUNK, 2, SUBLANES, w)
    lo, hi = p[:, 0], p[:, 1]
    if reverse:
        lo = lo + hi[:, 0:1, :]
    else:
        hi = hi + lo[:, SUBLANES - 1:SUBLANES, :]
    return jnp.stack([lo, hi], axis=1).reshape(n, w)


def _chunk_cumsum(x):
    return _tile_scan(x, False)


def _chunk_rev_cumsum(x):
    return _tile_scan(x, True)


def _chunk_last(x):
    n, w = x.shape
    p = x.reshape(n // CHUNK, CHUNK, w)
    return jnp.broadcast_to(p[:, CHUNK - 1:CHUNK, :], p.shape).reshape(n, w)


def _hgrn_local(q, fl, lb):
    sg = _sigmoid(fl)
    f = lb + (1.0 - lb) * sg
    g = jnp.log(f)
    cum = _chunk_cumsum(g)
    rest = _chunk_last(cum) - cum
    e = jnp.exp(cum)
    em = jnp.exp(-cum)
    eo = jnp.exp(rest)
    k = 1.0 - f
    return sg, f, e, em, eo, q * e, k * em, k * eo, cum + rest


def _chunk_pos(n):
    return lax.broadcasted_iota(jnp.int32, (n, HEAD_DIM), 0) & (CHUNK - 1)


def _hgrn_block_rows(L):
    return _tile(L, 688, CHUNK)


def _hgrn_specs(L, order):
    hb = D_MODEL // HEAD_DIM

    def spec(seg):
        if order == "bh":
            return pl.BlockSpec((L, HEAD_DIM), lambda b, h: (b, seg * hb + h))
        return pl.BlockSpec((L, HEAD_DIM), lambda h, b: (b, seg * hb + h))

    return [spec(SEG_Q), spec(SEG_F), spec(SEG_I), spec(SEG_OG)]


PAIR = 2 * CHUNK
CHUNK_SHIFT = CHUNK.bit_length() - 1


def _pair_steps(L, rb):
    steps = []
    nch = rb // CHUNK
    for r in range(0, L, rb):
        steps += [(r + p * PAIR, PAIR) for p in range(nch // 2)]
        if nch % 2:
            steps.append((r + (nch - 1) * CHUNK, CHUNK))
    return steps


def _pair_flags(rb):
    ci = lax.broadcasted_iota(jnp.int32, (rb, HEAD_DIM), 0) >> CHUNK_SHIFT
    odd = (ci & 1) == 1
    has_next = jnp.logical_and(jnp.logical_not(odd), ci < rb // CHUNK - 1)
    return odd, has_next


def _pair_masks(rb):
    r = lax.broadcasted_iota(jnp.int32, (rb, rb), 0)
    c = lax.broadcasted_iota(jnp.int32, (rb, rb), 1)
    rc, cc = r >> CHUNK_SHIFT, c >> CHUNK_SHIFT
    same = (rc == cc) & (c <= r)
    prev = ((rc & 1) == 1) & (cc == rc - 1)
    return same, prev


def _hgrn_pair_local(q, fl, lb, odd, has_next):
    sg, f, e, em, eo, qt, kt, ko, cend = _hgrn_local(q, fl, lb)
    n = q.shape[0]
    a = jnp.where(odd, pltpu.roll(cend, CHUNK, 0), 0.0)
    z = jnp.where(has_next, pltpu.roll(cend, n - CHUNK, 0), 0.0)
    ea, ez = jnp.exp(a), jnp.exp(z)
    return dict(sg=sg, f=f, e=e, em=em, eo=eo, qt=qt, kt=kt, ko=ko, ea=ea, ez=ez, qs=qt * ea, ks=ko * ez,
                decp=jnp.exp(cend + a + z))


def _pair_scores(qt, kt, ko, same, prev):
    return (jnp.where(same, _dotb(qt, kt, "nt"), 0.0) + jnp.where(prev, _dotb(qt, ko, "nt"), 0.0)).astype(bf16)


def _hgrn_fwd(p, lb, norm_g, B, L):
    rb = _hgrn_block_rows(L)
    steps = _pair_steps(L, rb)
    blocks = [slice(r, r + rb) for r in range(0, L, rb)]

    def body(q_ref, f_ref, v_ref, og_ref, lb_ref, ng_ref, y_ref, qs_s, ks_s, vb_s, decp_s, o_s, o2_s, u_s, sb_s):
        lbv = lb_ref[...]
        ngv = ng_ref[...]
        same, prev = _pair_masks(rb)
        odd, has_next = _pair_flags(rb)

        for rows in blocks:
            t = _hgrn_pair_local(q_ref[rows, :], f_ref[rows, :], lbv, odd, has_next)
            vb = v_ref[rows, :].astype(bf16)
            o_s[rows, :] = _dotb(_pair_scores(t["qt"], t["kt"], t["ko"], same, prev), vb)
            qs_s[rows, :] = t["qs"].astype(bf16)
            ks_s[rows, :] = t["ks"].astype(bf16)
            vb_s[rows, :] = vb
            decp_s[rows, :] = t["decp"]

        for n, (r0, nr) in enumerate(steps):
            u_s[n] = _dotb(vb_s[r0:r0 + nr, :], ks_s[r0:r0 + nr, :], "tn")
        st = jnp.zeros((HEAD_DIM, HEAD_DIM), f32)
        for n, (r0, nr) in enumerate(steps):
            sb_s[n] = st.astype(bf16)
            st = st * decp_s[r0:r0 + 1, :] + u_s[n]
        for n, (r0, nr) in enumerate(steps):
            o2_s[r0:r0 + nr, :] = _dotb(qs_s[r0:r0 + nr, :], sb_s[n], "nt")

        for rows in blocks:
            o = o_s[rows, :] + o2_s[rows, :]
            og = og_ref[rows, :]
            on = o * lax.rsqrt(jnp.mean(o * o, axis=-1, keepdims=True) + EPS) * ngv
            y_ref[rows, :] = (on * og * _sigmoid(og)).astype(y_ref.dtype)

    sb = pltpu.VMEM((L, HEAD_DIM), bf16)
    sf = pltpu.VMEM((L, HEAD_DIM), f32)
    return pl.pallas_call(
        body, name="hgrn_fwd", grid=(B, HEADS),
        in_specs=_hgrn_specs(L, "bh") + [pl.BlockSpec((1, HEAD_DIM), lambda b, h: (0, h)),
                                          pl.BlockSpec((1, HEAD_DIM), lambda b, h: (0, 0))],
        out_specs=pl.BlockSpec((L, HEAD_DIM), lambda b, h: (b, h)),
        out_shape=jax.ShapeDtypeStruct((B * L, D_MODEL), bf16),
        scratch_shapes=[sb, sb, sb, sf, sf, sf, pltpu.VMEM((len(steps), HEAD_DIM, HEAD_DIM), f32),
                        pltpu.VMEM((len(steps), HEAD_DIM, HEAD_DIM), bf16)],
        compiler_params=_params("parallel", "parallel"),
    )(p, p, p, p, lb, norm_g)


def _hgrn_bwd(p, dyb, dp, lb, norm_g, B, L):
    rb = _hgrn_block_rows(L)
    steps = _pair_steps(L, rb)
    blocks = [slice(r, r + rb) for r in range(0, L, rb)]

    def body(q_ref, f_ref, v_ref, og_ref, dy_ref, dp_in, lb_ref, ng_ref, dseg_ref, dlb_ref, dng_ref,
             st_ref, u_s, dsb_s, qt_s, kt_s, ko_s, qs_s, ks_s, vb_s, do_s,
             decp_s, o_s, o2_s, dqt_s, dkt_s, dko_s, dv_s, dv2_s, dqs_s, dks_s, ddecp_s):
        del dp_in
        lbv = lb_ref[...]
        ngv = ng_ref[...]
        same, prev = _pair_masks(rb)
        odd, has_next = _pair_flags(rb)
        pos = _chunk_pos(rb)

        @pl.when(pl.program_id(1) == 0)
        def _():
            dlb_ref[...] = jnp.zeros_like(dlb_ref)

        @pl.when((pl.program_id(0) == 0) & (pl.program_id(1) == 0))
        def _():
            dng_ref[...] = jnp.zeros_like(dng_ref)

        def scores(rows):
            return _pair_scores(qt_s[rows, :], kt_s[rows, :], ko_s[rows, :], same, prev)

        for rows in blocks:
            t = _hgrn_pair_local(q_ref[rows, :], f_ref[rows, :], lbv, odd, has_next)
            for dst, key in ((qt_s, "qt"), (kt_s, "kt"), (ko_s, "ko"), (qs_s, "qs"), (ks_s, "ks")):
                dst[rows, :] = t[key].astype(bf16)
            vb_s[rows, :] = v_ref[rows, :].astype(bf16)
            decp_s[rows, :] = t["decp"]
            o_s[rows, :] = _dotb(scores(rows), vb_s[rows, :])

        for n, (r0, nr) in enumerate(steps):
            u_s[n] = _dotb(vb_s[r0:r0 + nr, :], ks_s[r0:r0 + nr, :], "tn")
        st = jnp.zeros((HEAD_DIM, HEAD_DIM), f32)
        for n, (r0, nr) in enumerate(steps):
            st_ref[n] = st
            st = st * decp_s[r0:r0 + 1, :] + u_s[n]
        for n, (r0, nr) in enumerate(steps):
            o2_s[r0:r0 + nr, :] = _dotb(qs_s[r0:r0 + nr, :], st_ref[n], "nt")

        dng = jnp.zeros((1, HEAD_DIM), f32)
        for rows in blocks:
            o = o_s[rows, :] + o2_s[rows, :]
            og = og_ref[rows, :]
            dy = dy_ref[rows, :]
            rs = lax.rsqrt(jnp.mean(o * o, axis=-1, keepdims=True) + EPS)
            xn = o * rs
            so = _sigmoid(og)
            dseg_ref[SEG_OG, rows, :] = (dy * xn * ngv * so * (1.0 + og * (1.0 - so))).astype(dseg_ref.dtype)
            don = dy * og * so
            dng = dng + jnp.sum(don * xn, axis=0, keepdims=True)
            dxo = don * ngv
            do = (rs * (dxo - xn * jnp.mean(dxo * xn, axis=-1, keepdims=True))).astype(bf16)
            do_s[rows, :] = do
            dpf = _dotb(do, vb_s[rows, :], "nt")
            dp1 = jnp.where(same, dpf, 0.0).astype(bf16)
            dp2 = jnp.where(prev, dpf, 0.0).astype(bf16)
            dqt_s[rows, :] = _dotb(dp1, kt_s[rows, :]) + _dotb(dp2, ko_s[rows, :])
            dkt_s[rows, :] = _dotb(dp1, qt_s[rows, :], "tn")
            dko_s[rows, :] = _dotb(dp2, qt_s[rows, :], "tn")
            dv_s[rows, :] = _dotb(scores(rows), do, "tn")
        dng_ref[...] += dng

        for n, (r0, nr) in enumerate(steps):
            u_s[n] = _dotb(do_s[r0:r0 + nr, :], qs_s[r0:r0 + nr, :], "tn")
        dst = jnp.zeros((HEAD_DIM, HEAD_DIM), f32)
        for n, (r0, nr) in reversed(list(enumerate(steps))):
            dsb_s[n] = dst.astype(bf16)
            ddecp_s[r0:r0 + nr, :] = jnp.broadcast_to(jnp.sum(dst * st_ref[n], axis=0, keepdims=True), (nr, HEAD_DIM))
            dst = dst * decp_s[r0:r0 + 1, :] + u_s[n]
        for n, (r0, nr) in enumerate(steps):
            rows = slice(r0, r0 + nr)
            dqs_s[rows, :] = _dotb(do_s[rows, :], st_ref[n])
            dv2_s[rows, :] = _dotb(ks_s[rows, :], dsb_s[n], "nt")
            dks_s[rows, :] = _dotb(vb_s[rows, :], dsb_s[n])

        def chunk_sum(x):
            return _chunk_last(_chunk_cumsum(x))

        dlb = jnp.zeros((1, HEAD_DIM), f32)
        for rows in blocks:
            t = _hgrn_pair_local(q_ref[rows, :], f_ref[rows, :], lbv, odd, has_next)
            dqs, dks = dqs_s[rows, :], dks_s[rows, :]
            dqt = dqt_s[rows, :] + dqs * t["ea"]
            dko = dko_s[rows, :] + dks * t["ez"]
            dkt = dkt_s[rows, :]
            dko_ko = dko * t["ko"]
            dcum = dqt * t["qt"] - dkt * t["kt"] - dko_ko
            from_next = pltpu.roll(chunk_sum(jnp.where(odd, dqs * t["qs"], 0.0)), rb - CHUNK, 0)
            from_prev = pltpu.roll(chunk_sum(jnp.where(has_next, dks * t["ks"], 0.0)), CHUNK, 0)
            d_end = (chunk_sum(dko_ko) + jnp.where(has_next, from_next, 0.0) + jnp.where(odd, from_prev, 0.0)
                     + ddecp_s[rows, :] * t["decp"])
            dcum = dcum + jnp.where(pos == CHUNK - 1, d_end, 0.0)
            df = _chunk_rev_cumsum(dcum) / t["f"] - (dkt * t["em"] + dko * t["eo"])
            dlb = dlb + jnp.sum(df * (1.0 - t["sg"]), axis=0, keepdims=True)
            dseg_ref[SEG_Q, rows, :] = (dqt * t["e"]).astype(dseg_ref.dtype)
            dseg_ref[SEG_F, rows, :] = (df * (1.0 - lbv) * t["sg"] * (1.0 - t["sg"])).astype(dseg_ref.dtype)
            dseg_ref[SEG_I, rows, :] = (dv_s[rows, :] + dv2_s[rows, :]).astype(dseg_ref.dtype)
        dlb_ref[...] += dlb

    T = B * L
    ns = len(steps)
    sb = pltpu.VMEM((L, HEAD_DIM), bf16)
    sf = pltpu.VMEM((L, HEAD_DIM), f32)
    return pl.pallas_call(
        body, name="hgrn_bwd", grid=(HEADS, B),
        in_specs=_hgrn_specs(L, "hb") + [pl.BlockSpec((L, HEAD_DIM), lambda h, b: (b, h)), ANY,
                                          pl.BlockSpec((1, HEAD_DIM), lambda h, b: (0, h)),
                                          pl.BlockSpec((1, HEAD_DIM), lambda h, b: (0, 0))],
        out_specs=[pl.BlockSpec((4, L, HEAD_DIM), lambda h, b: (0, b, h)),
                   pl.BlockSpec((1, HEAD_DIM), lambda h, b: (0, h)),
                   pl.BlockSpec((1, HEAD_DIM), lambda h, b: (0, 0))],
        out_shape=[jax.ShapeDtypeStruct((N_SEG, T, D_MODEL), bf16), jax.ShapeDtypeStruct((1, D_MODEL), f32),
                   jax.ShapeDtypeStruct((1, HEAD_DIM), f32)],
        scratch_shapes=[pltpu.VMEM((ns, HEAD_DIM, HEAD_DIM), f32), pltpu.VMEM((ns, HEAD_DIM, HEAD_DIM), f32),
                        pltpu.VMEM((ns, HEAD_DIM, HEAD_DIM), bf16)] + [sb] * 7 + [sf] * 11,
        input_output_aliases={5: 0},
        compiler_params=_params("arbitrary", "arbitrary"),
    )(p, p, p, p, dyb, dp, lb, norm_g)


def _dz1_norm(dp, w_in_phys, h0, g, dh1):
    _, T, Dm = dp.shape
    tm = _tile(T, 688)
    return _mm_rmsnorm_bwd("dz1", dp, w_in_phys, (T // tm, 1, N_SEG),
                           pl.BlockSpec((None, tm, Dm), lambda i, j, k: (k, i, 0)),
                           pl.BlockSpec((Dm, Dm), lambda i, j, k: (0, k)), h0, g, dh1)


def _dz2_norm(dup, w_up, h1, g, dh2):
    _, T, _ = dup.shape
    tm = _tile(T, 688)
    tk = D_FF // 2
    return _mm_rmsnorm_bwd("dz2", dup, w_up, (T // tm, 1, 4),
                           pl.BlockSpec((None, tm, tk), lambda i, j, k: (k // 2, i, k % 2)),
                           pl.BlockSpec((D_MODEL, tk), lambda i, j, k: (0, k)), h1, g, dh2)


def _dw_in(z1, dp):
    _, T, Dm = dp.shape
    tk = _tile(T, 1376)
    return _mm("dw_in", z1, dp, "tn", (1, N_SEG, T // tk),
               pl.BlockSpec((tk, Dm), lambda i, j, k: (k, 0)),
               pl.BlockSpec((None, tk, Dm), lambda i, j, k: (j, k, 0)),
               jax.ShapeDtypeStruct((N_SEG, Dm, Dm), f32),
               pl.BlockSpec((None, Dm, Dm), lambda i, j, k: (j, 0, 0)), (Dm, Dm))


def _dw_up(z2, dup):
    _, T, _ = dup.shape
    tn = D_FF // 2
    tk = _tile(T, 688)
    return _mm("dw_up", z2, dup, "tn", (1, N_CHIPS, T // tk),
               pl.BlockSpec((tk, D_MODEL), lambda i, j, k: (k, 0)),
               pl.BlockSpec((None, tk, tn), lambda i, j, k: (j // 2, k, j % 2)),
               jax.ShapeDtypeStruct((N_CHIPS, D_MODEL, tn), f32),
               pl.BlockSpec((None, D_MODEL, tn), lambda i, j, k: (j, 0, 0)), (D_MODEL, tn))


def _place():
    x, y, c = lax.axis_index("x"), lax.axis_index("y"), lax.axis_index("c")
    chips = [(1 - x, y), (x, 1 - y), (1 - x, 1 - y)]
    return x, y, c, chips


def _allgather_chips(arrs):
    n = len(arrs)

    def body(*refs):
        ins, outs = refs[:n], refs[n:2 * n]
        send, recv, local = refs[2 * n:]
        x, y, c, chips = _place()
        me = 2 * x + y

        def copy(a, k, slot):
            px, py = chips[k]
            return pltpu.make_async_remote_copy(src_ref=ins[a], dst_ref=outs[a].at[slot], send_sem=send.at[3 * a + k],
                                                recv_sem=recv.at[3 * a + k], device_id=(px, py, c), device_id_type=MESH)

        for a in range(n):
            pltpu.make_async_copy(ins[a], outs[a].at[me], local.at[a]).start()
            for k in range(3):
                copy(a, k, me).start()
        for a in range(n):
            for k, (px, py) in enumerate(chips):
                copy(a, k, 2 * px + py).wait_recv()
        for a in range(n):
            pltpu.make_async_copy(ins[a], outs[a].at[me], local.at[a]).wait()
            for k in range(3):
                copy(a, k, me).wait_send()

    return pl.pallas_call(
        body, name="allgather_chips", in_specs=[ANY] * n, out_specs=[ANY] * n,
        out_shape=[jax.ShapeDtypeStruct((N_CHIPS,) + a.shape, a.dtype) for a in arrs],
        scratch_shapes=[pltpu.SemaphoreType.DMA((3 * n,)), pltpu.SemaphoreType.DMA((3 * n,)), pltpu.SemaphoreType.DMA((n,))],
    )(*arrs)


def _allgather_split(arrs):
    n = len(arrs)

    def body(*refs):
        start, finish = _gather_split_steps(refs[:n], refs[n:2 * n], *refs[2 * n:])
        start()
        finish()

    return pl.pallas_call(
        body, name="allgather_split", in_specs=[ANY] * n, out_specs=[ANY] * n,
        out_shape=[jax.ShapeDtypeStruct((N_CHIPS,) + a.shape, a.dtype) for a in arrs],
        scratch_shapes=_gather_split_sems(n),
    )(*arrs)


def _gather_split_sems(n):
    return [pltpu.SemaphoreType.DMA((3 * n,)) for _ in range(4)]


def _gather_split_steps(ins, outs, send, recv, fsend, frecv):
    n = len(ins)

    def place():
        x, y, c, chips = _place()
        return x, y, c, chips, 2 * x + y

    def half(a, core):
        rh = ins[a].shape[0] // 2
        return pl.ds(core * rh, rh)

    def copy(a, k, slot):
        x, y, c, chips, _ = place()
        px, py = chips[k]
        return pltpu.make_async_remote_copy(src_ref=ins[a].at[half(a, c), :], dst_ref=outs[a].at[slot, half(a, c), :],
                                            send_sem=send.at[3 * a + k], recv_sem=recv.at[3 * a + k],
                                            device_id=(px, py, c), device_id_type=MESH)

    def forward(a, k, core):
        x, y, c, chips, _ = place()
        px, py = chips[k]
        rows = outs[a].at[2 * px + py, half(a, core), :]
        return pltpu.make_async_remote_copy(src_ref=rows, dst_ref=rows, send_sem=fsend.at[3 * a + k],
                                            recv_sem=frecv.at[3 * a + k], device_id=(x, y, 1 - c), device_id_type=MESH)

    def start():
        me = place()[4]
        for a in range(n):
            for k in range(3):
                copy(a, k, me).start()

    def finish():
        x, y, c, chips, me = place()
        for a in range(n):
            for k, (px, py) in enumerate(chips):
                copy(a, k, 2 * px + py).wait_recv()
                forward(a, k, c).start()
        for a in range(n):
            for k in range(3):
                forward(a, k, 1 - c).wait_recv()
        for a in range(n):
            for k in range(3):
                copy(a, k, me).wait_send()
                forward(a, k, c).wait_send()

    return start, finish


def _in_proj_gather(z1, w_in, shards):
    n = len(shards)
    T, K = z1.shape
    N = w_in.shape[1]
    tm = _tile(T, 1032)
    tn = 1024
    grid = (T // tm, N // tn)

    def body(a_ref, b_ref, *rest):
        ins, o_ref, outs, sems = rest[:n], rest[n], rest[n + 1:2 * n + 1], rest[2 * n + 1:]
        start, finish = _gather_split_steps(ins, outs, *sems)
        i, j = pl.program_id(0), pl.program_id(1)

        @pl.when((i == 0) & (j == 0))
        def _():
            start()

        o_ref[...] = jnp.dot(a_ref[...], b_ref[...], preferred_element_type=f32)

        @pl.when((i == grid[0] - 1) & (j == grid[1] - 1))
        def _():
            finish()

    res = pl.pallas_call(
        body, name="in_proj", grid=grid,
        in_specs=[pl.BlockSpec((tm, K), lambda i, j: (i, 0)), pl.BlockSpec((K, tn), lambda i, j: (0, j))] + [ANY] * n,
        out_specs=[pl.BlockSpec((tm, tn), lambda i, j: (i, j))] + [ANY] * n,
        out_shape=[jax.ShapeDtypeStruct((T, N), f32)] + [jax.ShapeDtypeStruct((N_CHIPS,) + a.shape, a.dtype) for a in shards],
        scratch_shapes=_gather_split_sems(n),
        compiler_params=_params("arbitrary", "arbitrary"),
    )(z1, w_in, *shards)
    return res[0], res[1:]


def _sibling_halves(parts, name="sibling_halves"):
    n = len(parts)

    def body(*refs):
        ins, outs = refs[:n], refs[n:2 * n]
        send, recv = refs[2 * n:]
        x, y, c, _ = _place()

        def copy(a):
            rh = ins[a].shape[1] // 2
            return pltpu.make_async_remote_copy(src_ref=ins[a].at[:, pl.ds((1 - c) * rh, rh), :], dst_ref=outs[a],
                                                send_sem=send.at[a], recv_sem=recv.at[a], device_id=(x, y, 1 - c),
                                                device_id_type=MESH)

        for a in range(n):
            copy(a).start()
        for a in range(n):
            copy(a).wait_recv()
        for a in range(n):
            copy(a).wait_send()

    return pl.pallas_call(
        body, name=name, in_specs=[ANY] * n, out_specs=[ANY] * n,
        out_shape=[jax.ShapeDtypeStruct((a.shape[0], a.shape[1] // 2, a.shape[2]), a.dtype) for a in parts],
        scratch_shapes=[pltpu.SemaphoreType.DMA((n,)), pltpu.SemaphoreType.DMA((n,))],
    )(*parts)


def _add_own_half(name, part, got, core):
    nchip, R, C = part.shape
    rh = R // 2
    tr = _tile(rh, 256, 2 * SUBLANES)
    nt = rh // tr

    def body(core_ref, a_ref, b_ref, o_ref):
        del core_ref
        o_ref[...] = (a_ref[...] + b_ref[...]).astype(o_ref.dtype)

    return pl.pallas_call(
        body, name=name,
        grid_spec=pltpu.PrefetchScalarGridSpec(
            num_scalar_prefetch=1, grid=(nchip, nt),
            in_specs=[pl.BlockSpec((None, tr, C), lambda j, i, core_ref: (j, core_ref[0] * nt + i, 0)),
                      pl.BlockSpec((None, tr, C), lambda j, i, core_ref: (j, i, 0))],
            out_specs=pl.BlockSpec((None, tr, C), lambda j, i, core_ref: (j, i, 0))),
        out_shape=jax.ShapeDtypeStruct((nchip, rh, C), bf16), compiler_params=_params("parallel", "parallel"),
    )(core, part, got)


def _add_own_half_w_in(part, got, core):
    _, R, C = part.shape
    rh = R // 2
    tr = _tile(rh, 256, 2 * SUBLANES)
    nt = rh // tr
    tn = 256
    per_seg = C // tn
    per_chip = IN_COLS // N_CHIPS // tn

    def src(j):
        return ((j // per_seg + N_SEG - 1) % N_SEG, j % per_seg)

    def body(core_ref, a_ref, b_ref, o_ref):
        del core_ref
        o_ref[...] = (a_ref[...] + b_ref[...]).astype(o_ref.dtype)

    return pl.pallas_call(
        body, name="add_half_w_in",
        grid_spec=pltpu.PrefetchScalarGridSpec(
            num_scalar_prefetch=1, grid=(IN_COLS // tn, nt),
            in_specs=[pl.BlockSpec((None, tr, tn), lambda j, i, core_ref: (src(j)[0], core_ref[0] * nt + i, src(j)[1])),
                      pl.BlockSpec((None, tr, tn), lambda j, i, core_ref: (src(j)[0], i, src(j)[1]))],
            out_specs=pl.BlockSpec((None, tr, tn), lambda j, i, core_ref: (j // per_chip, i, j % per_chip))),
        out_shape=jax.ShapeDtypeStruct((N_CHIPS, rh, IN_COLS // N_CHIPS), bf16), compiler_params=_params("parallel", "parallel"),
    )(core, part, got)


def _chip_exchange(sums):
    n = len(sums)

    def body(*refs):
        start, finish = _chip_exchange_steps(refs[:n], refs[n:2 * n], *refs[2 * n:])
        start()
        finish()

    return pl.pallas_call(
        body, name="chip_exchange", in_specs=[ANY] * n, out_specs=[ANY] * n,
        out_shape=[jax.ShapeDtypeStruct(a.shape, a.dtype) for a in sums],
        scratch_shapes=_chip_exchange_sems(n),
    )(*sums)


def _chip_exchange_sems(n):
    return [pltpu.SemaphoreType.DMA((3 * n,)), pltpu.SemaphoreType.DMA((3 * n,))]


def _chip_exchange_steps(ins, outs, send, recv):
    n = len(ins)

    def copy(a, k, own_slot):
        x, y, c, chips = _place()
        px, py = chips[k]
        slot = 2 * x + y if own_slot else 2 * px + py
        return pltpu.make_async_remote_copy(src_ref=ins[a].at[2 * px + py], dst_ref=outs[a].at[slot], send_sem=send.at[3 * a + k],
                                            recv_sem=recv.at[3 * a + k], device_id=(px, py, c), device_id_type=MESH)

    def start():
        for a in range(n):
            for k in range(3):
                copy(a, k, True).start()

    def finish():
        for a in range(n):
            for k in range(3):
                copy(a, k, False).wait_recv()
        for a in range(n):
            for k in range(3):
                copy(a, k, True).wait_send()

    return start, finish


def _sum_chips(name, slots, sums, where):
    nchip, rh, C = slots.shape
    tr = _tile(rh, 256, 2 * SUBLANES)
    nt = rh // tr

    def body(where_ref, own_ref, s1_ref, s2_ref, s3_ref, o_ref):
        me = where_ref[0]
        by_dist = [r[...].astype(f32) for r in (own_ref, s1_ref, s2_ref, s3_ref)]
        acc = None
        for j in range(nchip):
            d = me ^ j
            term = jnp.where(d == 0, by_dist[0], jnp.where(d == 1, by_dist[1], jnp.where(d == 2, by_dist[2], by_dist[3])))
            acc = term if acc is None else acc + term
        o_ref[...] = acc

    def other(d):
        return pl.BlockSpec((None, tr, C), lambda i, w: (w[0] ^ d, i, 0))

    return pl.pallas_call(
        body, name=name,
        grid_spec=pltpu.PrefetchScalarGridSpec(
            num_scalar_prefetch=1, grid=(nt,),
            in_specs=[other(0), other(1), other(2), other(3)],
            out_specs=pl.BlockSpec((tr, C), lambda i, w: (w[1] * nt + i, 0))),
        out_shape=jax.ShapeDtypeStruct((2 * rh, C), f32), compiler_params=_params("parallel"),
    )(where, sums, slots, slots, slots)


def _sum_slots(name, slots):
    ns, R, C = slots.shape
    tr = _tile(R, 256)

    def body(s_ref, o_ref):
        acc = s_ref[0]
        for j in range(1, ns):
            acc = acc + s_ref[j]
        o_ref[...] = acc

    return pl.pallas_call(
        body, name=name, grid=(R // tr,), in_specs=[pl.BlockSpec((ns, tr, C), lambda i: (0, i, 0))],
        out_specs=pl.BlockSpec((tr, C), lambda i: (i, 0)), out_shape=jax.ShapeDtypeStruct((R, C), f32),
        compiler_params=_params("parallel"),
    )(slots)


def _sibling_join(fulls):
    n = len(fulls)

    def body(*refs):
        ins, outs = refs[:n], refs[n:2 * n]
        send, recv = refs[2 * n:]
        x, y, c, _ = _place()

        def copy(a, core):
            rh = ins[a].shape[0] // 2
            rows = pl.ds(core * rh, rh)
            return pltpu.make_async_remote_copy(src_ref=ins[a].at[rows, :], dst_ref=outs[a].at[rows, :], send_sem=send.at[a],
                                                recv_sem=recv.at[a], device_id=(x, y, 1 - c), device_id_type=MESH)

        for a in range(n):
            copy(a, c).start()
        for a in range(n):
            copy(a, 1 - c).wait_recv()
        for a in range(n):
            copy(a, c).wait_send()

    return pl.pallas_call(
        body, name="sibling_join", in_specs=[ANY] * n, out_specs=[ANY] * n,
        out_shape=[jax.ShapeDtypeStruct(a.shape, a.dtype) for a in fulls],
        scratch_shapes=[pltpu.SemaphoreType.DMA((n,)), pltpu.SemaphoreType.DMA((n,))],
        input_output_aliases={a: a for a in range(n)},
    )(*fulls)


def _allgather_devices(v):
    def body(v_ref, out_ref, send, recv):
        x, y, c, chips = _place()
        me, sibling = (x, y, c), (x, y, 1 - c)

        def slot(px, py, pc):
            return out_ref.at[4 * px + 2 * py + pc]

        def copy(k, block, to, src=None):
            return pltpu.make_async_remote_copy(src_ref=slot(*block) if src is None else src, dst_ref=slot(*block),
                                                send_sem=send.at[k], recv_sem=recv.at[k], device_id=to, device_id_type=MESH)

        first = [copy(0, me, sibling, src=v_ref)] + [copy(1 + j, me, (*chip, c), src=v_ref) for j, chip in enumerate(chips)]
        for cp in first:
            cp.start()
        passed = [copy(4 + j, (*chip, c), sibling) for j, chip in enumerate(chips)]
        for j, chip in enumerate(chips):
            copy(1 + j, (*chip, c), me).wait_recv()
            passed[j].start()
        copy(0, sibling, me).wait_recv()
        for j, chip in enumerate(chips):
            copy(4 + j, (*chip, 1 - c), me).wait_recv()
        for cp in first + passed:
            cp.wait_send()

    return pl.pallas_call(
        body, name="allgather_devices", in_specs=[ANY], out_specs=ANY,
        out_shape=jax.ShapeDtypeStruct((N_DEV,) + v.shape, v.dtype),
        scratch_shapes=[pltpu.SemaphoreType.DMA((N_DEV - 1,)), pltpu.SemaphoreType.DMA((N_DEV - 1,))],
    )(v)


def _adamw(name, w, g, m, v):
    R, C = w.shape
    tr = _tile(R, 256)
    c1 = 1.0 / (1.0 - ADAM_B1 ** ADAM_STEP)
    c2 = 1.0 / (1.0 - ADAM_B2 ** ADAM_STEP)

    def body(w_ref, g_ref, m_ref, v_ref, d_ref, nm_ref, nv_ref):
        gv = g_ref[...]
        nm = ADAM_B1 * m_ref[...] + (1.0 - ADAM_B1) * gv
        nv = ADAM_B2 * v_ref[...] + (1.0 - ADAM_B2) * gv * gv
        d_ref[...] = -ADAM_LR * ((nm * c1) / (jnp.sqrt(nv * c2) + ADAM_EPS) + ADAM_WD * w_ref[...])
        nm_ref[...] = nm
        nv_ref[...] = nv

    row = pl.BlockSpec((tr, C), lambda i: (i, 0))
    sh = jax.ShapeDtypeStruct((R, C), f32)
    return pl.pallas_call(body, name=name, grid=(R // tr,), in_specs=[row] * 4, out_specs=[row] * 3,
                          out_shape=[sh, sh, sh], compiler_params=_params("parallel"))(w, g, m, v)


def _adamw_update(w, g, m, v):
    c1 = 1.0 / (1.0 - ADAM_B1 ** ADAM_STEP)
    c2 = 1.0 / (1.0 - ADAM_B2 ** ADAM_STEP)
    nm = ADAM_B1 * m + (1.0 - ADAM_B1) * g
    nv = ADAM_B2 * v + (1.0 - ADAM_B2) * g * g
    return -ADAM_LR * ((nm * c1) / (jnp.sqrt(nv * c2) + ADAM_EPS) + ADAM_WD * w), nm, nv


def _adamw_many(ws, gs, ms, vs):
    n = len(ws)

    def body(*refs):
        ins, outs = refs[:4 * n], refs[4 * n:]
        for a in range(n):
            d, nm, nv = _adamw_update(ins[a][...], ins[n + a][...], ins[2 * n + a][...], ins[3 * n + a][...])
            outs[a][...] = d
            outs[n + a][...] = nm
            outs[2 * n + a][...] = nv

    shapes = [jax.ShapeDtypeStruct(a.shape, f32) for a in ws]
    return pl.pallas_call(body, name="adamw_small", out_shape=shapes * 3)(*ws, *gs, *ms, *vs)


def _zoh_parts(lr, li, log_dt):
    dt = jnp.exp(log_dt)
    mag = jnp.exp(lr * dt)
    c, s = jnp.cos(li * dt), jnp.sin(li * dt)
    ab_re, ab_im = mag * c, mag * s
    den = lr * lr + li * li
    nr = ab_re - 1.0
    coef_re = (nr * lr + ab_im * li) / den
    coef_im = (ab_im * lr - nr * li) / den
    return dt, mag, c, s, ab_re, ab_im, den, nr, coef_re, coef_im


def _zoh_fwd(lr, li, log_dt, b_re, b_im):
    def body(lr_ref, li_ref, ld_ref, br_ref, bi_ref, ar_ref, ai_ref, bbr_ref, bbi_ref):
        _, _, _, _, ab_re, ab_im, _, _, coef_re, coef_im = _zoh_parts(lr_ref[...], li_ref[...], ld_ref[...])
        ar_ref[...] = ab_re
        ai_ref[...] = ab_im
        bbr_ref[...] = coef_re * br_ref[...] - coef_im * bi_ref[...]
        bbi_ref[...] = coef_re * bi_ref[...] + coef_im * br_ref[...]

    col = jax.ShapeDtypeStruct(lr.shape, f32)
    mat = jax.ShapeDtypeStruct(b_re.shape, f32)
    return pl.pallas_call(body, name="zoh_fwd", out_shape=[col, col, mat, mat])(lr, li, log_dt, b_re, b_im)


def _zoh_bwd(lr, li, log_dt, b_re, b_im, d_ar, d_ai, d_bbr, d_bbi):
    n = lr.shape[1]
    groups = n // SSM_STATE

    def body(lr_ref, li_ref, ld_ref, br_ref, bi_ref, dar_ref, dai_ref, dbbr_ref, dbbi_ref,
             dlr_ref, dli_ref, dld_ref, dbr_ref, dbi_ref):
        lr_, li_ = lr_ref[...], li_ref[...]
        dt, mag, c, s, _, ab_im, den, nr, coef_re, coef_im = _zoh_parts(lr_, li_, ld_ref[...])
        br, bi, dbbr, dbbi = br_ref[...], bi_ref[...], dbbr_ref[...], dbbi_ref[...]
        dbr_ref[...] = coef_re * dbbr + coef_im * dbbi
        dbi_ref[...] = coef_re * dbbi - coef_im * dbbr
        d_cr = jnp.sum(dbbr * br + dbbi * bi, axis=0, keepdims=True)
        d_ci = jnp.sum(dbbi * br - dbbr * bi, axis=0, keepdims=True)
        d_nr = (d_cr * lr_ - d_ci * li_) / den
        d_abi = dai_ref[...] + (d_cr * li_ + d_ci * lr_) / den
        d_abr = dar_ref[...] + d_nr
        d_den = -(d_cr * coef_re + d_ci * coef_im) / den
        d_lr = (d_cr * nr + d_ci * ab_im) / den + 2.0 * lr_ * d_den
        d_li = (d_cr * ab_im - d_ci * nr) / den + 2.0 * li_ * d_den
        d_theta = mag * (d_abi * c - d_abr * s)
        d_arg = mag * (d_abr * c + d_abi * s)
        dlr_ref[...] = d_lr + d_arg * dt
        dli_ref[...] = d_li + d_theta * dt
        d_dt = d_arg * lr_ + d_theta * li_
        member = (lax.broadcasted_iota(jnp.int32, (n, groups), 0) >> (SSM_STATE.bit_length() - 1)
                  == lax.broadcasted_iota(jnp.int32, (n, groups), 1)).astype(f32)
        dld_ref[...] = jnp.dot(d_dt * dt, member, preferred_element_type=f32, precision=lax.Precision.HIGHEST)

    col = jax.ShapeDtypeStruct(lr.shape, f32)
    mat = jax.ShapeDtypeStruct(b_re.shape, f32)
    return pl.pallas_call(body, name="zoh_bwd", out_shape=[col, col, jax.ShapeDtypeStruct((1, groups), f32), mat, mat])(
        lr, li, log_dt, b_re, b_im, d_ar, d_ai, d_bbr, d_bbi)


def _lower_bound_fwd(logits):
    def body(x_ref, o_ref):
        x = x_ref[...]
        e = jnp.exp(x - jnp.max(x, axis=0, keepdims=True))
        o_ref[...] = e / jnp.sum(e, axis=0, keepdims=True)

    return pl.pallas_call(body, name="lower_bound_fwd", out_shape=jax.ShapeDtypeStruct(logits.shape, f32))(logits)


def _lower_bound_bwd(sm, d_lb):
    def body(sm_ref, d_ref, o_ref):
        smv = sm_ref[...]
        row = lax.broadcasted_iota(jnp.int32, smv.shape, 0)
        sm0 = smv[0:1, :]
        o_ref[...] = sm0 * d_ref[...] * (jnp.where(row == 0, 1.0, 0.0) - smv)

    return pl.pallas_call(body, name="lower_bound_bwd", out_shape=jax.ShapeDtypeStruct(sm.shape, f32))(sm, d_lb)


def _s5_tables(ab_re, ab_im, bb_re, bb_im, c_re, c_im, seg):
    eye = jnp.eye(SLAB_GROUPS, dtype=f32)

    def blk_in(bb):
        return jnp.einsum("hsgp,gk->sghkp", bb.reshape(SSM_GROUP, N_SLAB, SLAB_GROUPS, SSM_STATE), eye).reshape(
            N_SLAB, SLAB_CH, SLAB_NS)

    def blk_out(cc):
        return jnp.einsum("sghp,gk->skpgh", cc.reshape(N_SLAB, SLAB_GROUPS, SSM_GROUP, SSM_STATE), eye).reshape(
            N_SLAB, SLAB_NS, SLAB_CH)

    bs = jnp.concatenate([blk_in(bb_re), blk_in(bb_im)], axis=2).astype(bf16)
    cs = jnp.concatenate([blk_out(c_re), blk_out(-c_im)], axis=1).astype(bf16)
    n = SSM_GROUPS * SSM_STATE
    pw = _power_table(jnp.stack([ab_re.reshape(1, n), ab_im.reshape(1, n)]), -(-seg // SUBLANES))
    return bs, cs, pw


def _power_table(ab, tiles):
    n = ab.shape[2]

    def body(a_ref, o_ref):
        row = lax.broadcasted_iota(jnp.int32, (SUBLANES, n), 0)
        ar, ai = a_ref[0], a_ref[1]
        tr, ti = jnp.broadcast_to(ar, (SUBLANES, n)), jnp.broadcast_to(ai, (SUBLANES, n))
        pr, pi = ar, ai
        for r in range(1, SUBLANES):
            pr, pi = pr * ar - pi * ai, pr * ai + pi * ar
            tr = jnp.where(row == r, pr, tr)
            ti = jnp.where(row == r, pi, ti)
        o_ref[0, 0:SUBLANES, :] = tr
        o_ref[1, 0:SUBLANES, :] = ti

        def step(j, carry):
            cr, ci = carry
            cr, ci = cr * pr - ci * pi, cr * pi + ci * pr
            o_ref[0, _rows8(j), :] = cr
            o_ref[1, _rows8(j), :] = ci
            return cr, ci

        lax.fori_loop(1, tiles, step, (tr, ti))

    return pl.pallas_call(body, name="power_table", out_shape=jax.ShapeDtypeStruct((2, SUBLANES * tiles, n), f32))(ab)


def _s5_table_grads(dbs, dcs, da):
    eye = jnp.eye(SLAB_GROUPS, dtype=f32)
    d6 = dbs.reshape(N_SLAB, SLAB_GROUPS, SSM_GROUP, 2, SLAB_GROUPS, SSM_STATE)
    dbb = jnp.einsum("sghrkp,gk->rhsgp", d6, eye).reshape(2, SSM_GROUP, SSM_GROUPS * SSM_STATE)
    c6 = dcs.reshape(N_SLAB, 2, SLAB_GROUPS, SSM_STATE, SLAB_GROUPS, SSM_GROUP)
    dcc = jnp.einsum("srkpgh,gk->rsghp", c6, eye).reshape(2, SSM_GROUPS, SSM_GROUP, SSM_STATE)
    dab = da.transpose(1, 0, 2).reshape(2, SSM_GROUPS, SSM_STATE)
    return dab[0], dab[1], dbb[0], dbb[1], dcc[0], -dcc[1]


SMALL = ["mix_norm_g", "ssm_lambda_re", "ssm_lambda_im", "ssm_log_dt", "ssm_b_re", "ssm_b_im", "ssm_c_re", "ssm_c_im",
         "ssm_d", "hgrn_lb_logits", "hgrn_norm_g", "ffn_norm_g", "conv_b", "final_norm_g"]
SHARDED_SMALL = ["meta_tokens", "conv_w"]
BIG = ["w_in", "ssm_w_glu", "w_ssm_proj", "w_hgrn_proj", "w_out", "w_up", "w_down"]
WEIGHTS = ['meta_tokens', 'mix_norm_g', 'w_in', 'ssm_lambda_re', 'ssm_lambda_im', 'ssm_log_dt', 'ssm_b_re', 'ssm_b_im',
           'ssm_c_re', 'ssm_c_im', 'ssm_d', 'ssm_w_glu', 'w_ssm_proj', 'hgrn_lb_logits', 'hgrn_norm_g', 'w_hgrn_proj',
           'w_out', 'ffn_norm_g', 'w_up', 'conv_w', 'conv_b', 'w_down', 'final_norm_g']


LATER = [k for k in BIG if k != "w_in"]


def _full_weights(gathered, shards, chip):
    Dm = D_MODEL
    g = {k: lax.dynamic_update_slice(gathered[k], shards[k][None], (chip, 0, 0)) for k in gathered}
    full = {}
    for k, v in g.items():
        if k == "w_in":
            full[k] = jnp.roll(v.transpose(1, 0, 2).reshape(Dm, IN_COLS), -Dm, axis=1)
        elif k == "w_up":
            full[k] = v.transpose(1, 0, 2).reshape(Dm, 2 * D_FF)
        else:
            full[k] = v.reshape(-1, Dm)
    return full


def _local_grads(x, tgt, meta, w, full, shards, chip, core):
    B, S, Dm = x.shape
    L = S + N_META
    T = B * L
    h0 = jnp.concatenate([jnp.broadcast_to(meta[None], (B, N_META, Dm)), x], axis=1).reshape(T, Dm)

    lb_all = _lower_bound_fwd(w["hgrn_lb_logits"])
    lb = lb_all[0:1]
    gp = SSM_GROUPS * SSM_STATE
    zoh_in = (w["ssm_lambda_re"].reshape(1, gp), w["ssm_lambda_im"].reshape(1, gp),
              jnp.repeat(w["ssm_log_dt"].reshape(SSM_GROUPS, 1), SSM_STATE, axis=1).reshape(1, gp),
              w["ssm_b_re"].reshape(gp, SSM_GROUP).T, w["ssm_b_im"].reshape(gp, SSM_GROUP).T)
    ab_re, ab_im, bb_re, bb_im = _zoh_fwd(*zoh_in)
    bs, cs, pw = _s5_tables(ab_re, ab_im, bb_re, bb_im, w["ssm_c_re"][0], w["ssm_c_im"][0], L // SUBLANES)

    z1 = _rmsnorm_fwd("mix_norm", h0, w["mix_norm_g"])
    p, gathered = _in_proj_gather(z1, full["w_in"], [shards[k] for k in LATER])
    full = {**full, **_full_weights(dict(zip(LATER, gathered)), shards, chip)}
    ya0 = _s5_fwd(p, bs, cs, pw, w["ssm_d"], B, L)
    gl, ya = _glu_proj_fwd(ya0, full["ssm_w_glu"])
    yb = _hgrn_fwd(p, lb, w["hgrn_norm_g"], B, L)
    pa, pb, merged = _proj_merge_fwd(ya, yb, full["w_ssm_proj"], full["w_hgrn_proj"], p)
    h1, z2 = _out_proj_norm(merged, full["w_out"], h0, w["ffn_norm_g"])
    up = _mm_rows("up_proj", z2, full["w_up"], "nn", f32, D_FF // 2)
    ff = _conv_fwd(up, full["conv_w"], w["conv_b"], B, L)
    h2 = _mm_rows("down_proj", ff, full["w_down"], "nn", f32, 1024, res=h1, tk=D_FF // 2)

    h2x = h2.reshape(B, L, Dm)[:, N_META:].reshape(B * S, Dm)
    dh2x, loss, d_final_g = _final_loss(h2x, tgt.reshape(B * S, Dm), w["final_norm_g"].reshape(1, Dm))
    dh2 = jnp.pad(dh2x.reshape(B, S, Dm), ((0, 0), (N_META, 0), (0, 0))).reshape(T, Dm)

    dff = _mm_rows("d_ff", dh2, full["w_down"], "nt", f32, D_FF // 2)
    g_w_down = _mm_wgrad("dw_down", ff, dh2, tn=512)
    dup, dconv = _conv_bwd(up, dff, full["conv_w"], w["conv_b"], B, L)
    g_w_up = _dw_up(z2, dup)
    dh1, d_ffn_g = _dz2_norm(dup, full["w_up"], h1, w["ffn_norm_g"], dh2)

    g_w_out = _mm_wgrad("dw_out", merged, dh1)
    dpa, dpb, dp = _merge_bwd_fused(dh1, full["w_out"], p, pa, pb)
    dgl, dya0_direct = _glu_bwd_fused(dpa, full["w_ssm_proj"], ya0, gl)
    g_w_ssm_proj = _mm_wgrad("dw_ssm_proj", ya, dpa)
    dyb = _mm_rows("d_yb", dpb, full["w_hgrn_proj"], "nt", f32, 1024)
    g_w_hgrn_proj = _mm_wgrad("dw_hgrn_proj", yb, dpb)
    dp, d_lb, d_hgrn_g = _hgrn_bwd(p, dyb, dp, lb, w["hgrn_norm_g"], B, L)
    dya0 = _mm_rows("d_ya0", dgl, full["ssm_w_glu"], "nt", f32, 1024, res=dya0_direct)
    g_w_glu = _mm_wgrad("dw_glu", ya0, dgl)
    parts = {
        "ssm_w_glu": g_w_glu.reshape(N_CHIPS, Dm // N_CHIPS, Dm), "w_ssm_proj": g_w_ssm_proj.reshape(N_CHIPS, Dm // N_CHIPS, Dm),
        "w_hgrn_proj": g_w_hgrn_proj.reshape(N_CHIPS, Dm // N_CHIPS, Dm), "w_out": g_w_out.reshape(N_CHIPS, Dm // N_CHIPS, Dm),
        "w_up": g_w_up, "w_down": g_w_down.reshape(N_CHIPS, D_FF // N_CHIPS, Dm),
    }
    got = _sibling_halves([parts[k] for k in LATER])
    sums = {k: _add_own_half("add_half_" + k, parts[k], gt, core) for k, gt in zip(LATER, got)}
    (dp, dbs, dcs, da, d_skip), slots_later = _s5_bwd(p, dya0, dp, bs, cs, pw, w["ssm_d"], B, L, [sums[k] for k in LATER])
    slots = dict(zip(LATER, slots_later))
    g_w_in = _dw_in(z1, dp)
    dh0, d_mix_g = _dz1_norm(dp, full["w_in"], h0, w["mix_norm_g"], dh1)

    dh0 = dh0.reshape(B, L, Dm)
    grad_x = dh0[:, N_META:]
    d_meta = _meta_grad(dh0[:, :N_META])

    d_ab_re, d_ab_im, d_bb_re, d_bb_im, d_c_re, d_c_im = _s5_table_grads(dbs, dcs, da)
    d_lr, d_li, d_log_dt, d_b_re, d_b_im = _zoh_bwd(*zoh_in, d_ab_re.reshape(1, gp), d_ab_im.reshape(1, gp), d_bb_re, d_bb_im)
    gps = (SSM_GROUPS, SSM_STATE)
    d_lr, d_li, d_log_dt = d_lr.reshape(gps), d_li.reshape(gps), d_log_dt.reshape(SSM_GROUPS)
    d_b_re, d_b_im = d_b_re.T.reshape(gps + (SSM_GROUP,)), d_b_im.T.reshape(gps + (SSM_GROUP,))
    d_logits = _lower_bound_bwd(lb_all, d_lb)
    small = {
        "meta_tokens": d_meta, "mix_norm_g": d_mix_g, "ssm_lambda_re": d_lr[None], "ssm_lambda_im": d_li[None],
        "ssm_log_dt": d_log_dt[None], "ssm_b_re": d_b_re[None], "ssm_b_im": d_b_im[None], "ssm_c_re": d_c_re[None],
        "ssm_c_im": d_c_im[None], "ssm_d": d_skip, "hgrn_lb_logits": d_logits, "hgrn_norm_g": d_hgrn_g,
        "ffn_norm_g": d_ffn_g, "conv_w": dconv[:, 0:3, :].transpose(1, 0, 2).reshape(3, 2 * D_FF),
        "conv_b": dconv[:, 3, :].reshape(1, 2 * D_FF), "final_norm_g": d_final_g.reshape(Dm),
    }
    sums["w_in"] = _add_own_half_w_in(g_w_in, _sibling_halves([g_w_in], "sibling_halves_w_in")[0], core)
    slots["w_in"] = _chip_exchange([sums["w_in"]])[0]
    return loss, grad_x, sums, slots, small


PACK_ROWS = 256


def _pack(parts):
    flat = jnp.concatenate([parts[k].reshape(-1) for k in parts])
    n = flat.shape[0]
    rows = -(-n // (PACK_ROWS * LANES)) * PACK_ROWS
    flat = jnp.pad(flat, (0, rows * LANES - n))
    return flat.reshape(rows, LANES)


def _unpack(packed, like):
    flat = packed.reshape(-1)
    out, o = {}, 0
    for k, ref in like.items():
        n = math.prod(ref.shape)
        out[k] = flat[o:o + n].reshape(ref.shape)
        o += n
    return out


def kernel(x, meta_tokens, mix_norm_g, w_in, ssm_lambda_re, ssm_lambda_im, ssm_log_dt, ssm_b_re, ssm_b_im, ssm_c_re, ssm_c_im, ssm_d, ssm_w_glu, w_ssm_proj, hgrn_lb_logits, hgrn_norm_g, w_hgrn_proj, w_out, ffn_norm_g, w_up, conv_w, conv_b, w_down, final_norm_g, loss_target, m_meta_tokens, m_mix_norm_g, m_w_in, m_ssm_lambda_re, m_ssm_lambda_im, m_ssm_log_dt, m_ssm_b_re, m_ssm_b_im, m_ssm_c_re, m_ssm_c_im, m_ssm_d, m_ssm_w_glu, m_w_ssm_proj, m_hgrn_lb_logits, m_hgrn_norm_g, m_w_hgrn_proj, m_w_out, m_ffn_norm_g, m_w_up, m_conv_w, m_conv_b, m_w_down, m_final_norm_g, v_meta_tokens, v_mix_norm_g, v_w_in, v_ssm_lambda_re, v_ssm_lambda_im, v_ssm_log_dt, v_ssm_b_re, v_ssm_b_im, v_ssm_c_re, v_ssm_c_im, v_ssm_d, v_ssm_w_glu, v_w_ssm_proj, v_hgrn_lb_logits, v_hgrn_norm_g, v_w_hgrn_proj, v_w_out, v_ffn_norm_g, v_w_up, v_conv_w, v_conv_b, v_w_down, v_final_norm_g):
    args = dict(locals())
    w = {k: args[k] for k in WEIGHTS}
    mom = {k: args["m_" + k] for k in WEIGHTS}
    var = {k: args["v_" + k] for k in WEIGHTS}
    Dm = D_MODEL
    cx, cy, cc = lax.axis_index("x"), lax.axis_index("y"), lax.axis_index("c")
    chip = 2 * cx + cy

    shards = {k: w[k][0].astype(bf16) for k in BIG}
    g_meta, g_cw = _allgather_chips([w["meta_tokens"], w["conv_w"][0]])
    full = _full_weights({"w_in": _allgather_split([shards["w_in"]])[0]}, shards, chip)
    full["conv_w"] = g_cw.transpose(1, 0, 2).reshape(3, 2 * D_FF)
    meta_full = g_meta.transpose(1, 0, 2).reshape(N_META, Dm)

    core = cc.reshape(1).astype(jnp.int32)
    loss_part, grad_x, sums, slots, small = _local_grads(x, loss_target, meta_full, w, full, shards, chip, core)

    where = jnp.stack([chip, cc]).astype(jnp.int32)
    fulls = [_sum_chips("sum_chips_" + k, slots[k], sums[k], where) for k in BIG]
    g_big = dict(zip(BIG, _sibling_join(fulls)))

    small_all = dict(small)
    small_all["loss"] = loss_part[0, 0:1]
    packed = _pack(small_all)
    slots_dev = lax.dynamic_update_slice(_allgather_devices(packed), packed[None], (2 * chip + cc, 0, 0))
    reduced = _unpack(_sum_slots("sum_devices", slots_dev), small_all)
    loss = reduced.pop("loss")[0]
    mcols = Dm // N_CHIPS
    ccols = 2 * D_FF // N_CHIPS
    grads = {k: reduced[k] for k in SMALL}
    grads["meta_tokens"] = lax.dynamic_slice(reduced["meta_tokens"], (0, chip * mcols), (N_META, mcols))
    grads["conv_w"] = lax.dynamic_slice(reduced["conv_w"], (0, chip * ccols), (3, ccols))[None]
    for k in BIG:
        grads[k] = g_big[k][None]

    delta, new_m, new_v = {}, {}, {}
    for k in BIG:
        shp = w[k].shape
        d, nm, nv = _adamw("adamw_" + k, w[k][0], grads[k][0], mom[k][0], var[k][0])
        delta[k], new_m[k], new_v[k] = d.reshape(shp), nm.reshape(shp), nv.reshape(shp)
    rest = SMALL + SHARDED_SMALL

    def flat2(a):
        return a.reshape(-1, a.shape[-1])

    outs = _adamw_many(*[[flat2(t[k]) for k in rest] for t in (w, grads, mom, var)])
    n = len(rest)
    for j, dst in enumerate((delta, new_m, new_v)):
        dst.update({k: o.reshape(w[k].shape) for k, o in zip(rest, outs[j * n:(j + 1) * n])})

    return (loss, grad_x, *[grads[k].reshape(w[k].shape) for k in WEIGHTS], *[delta[k] for k in WEIGHTS],
            *[new_m[k] for k in WEIGHTS], *[new_v[k] for k in WEIGHTS])
```

```python
import math

import jax
import jax.numpy as jnp
from jax import lax
from jax.experimental import pallas as pl
from jax.experimental.pallas import tpu as pltpu

f32 = jnp.float32
bf16 = jnp.bfloat16

D_MODEL = 1024
N_META = 16
SSM_GROUP = 16
SSM_GROUPS = 64
SSM_STATE = 64
SLAB_GROUPS = 8
N_SLAB = SSM_GROUPS // SLAB_GROUPS
SLAB_CH = SLAB_GROUPS * SSM_GROUP
SLAB_NS = SLAB_GROUPS * SSM_STATE
HEADS = 8
HEAD_DIM = 128
CHUNK = 16
D_FF = 2816
IN_COLS = 7168
EPS = 1e-6
SUBLANES = 8
LANES = 128
N_CHIPS = 4
N_DEV = 8
ADAM_LR, ADAM_B1, ADAM_B2, ADAM_EPS, ADAM_WD, ADAM_STEP = 0.001, 0.9, 0.999, 1e-08, 0.01, 10
MESH = pl.DeviceIdType.MESH
ANY = pl.BlockSpec(memory_space=pl.ANY)

SEG_Q, SEG_F, SEG_I, SEG_OG, SEG_GA, SEG_GB, SEG_U = range(7)
N_SEG = 7


def _tile(n, target, mult=SUBLANES):
    best = None
    for d in range(mult, min(n, target) + 1, mult):
        if n % d == 0:
            best = d
    return n if best is None else best


def _params(*sem):
    return pltpu.CompilerParams(dimension_semantics=sem)


def _sigmoid(x):
    return 1.0 / (1.0 + jnp.exp(-x))


_DIMS = {"nn": (((1,), (0,)), ((), ())), "nt": (((1,), (1,)), ((), ())), "tn": (((0,), (0,)), ((), ()))}


def _mm(name, a, b, dims, grid, a_spec, b_spec, out_shape, out_spec, acc_shape, res=None, res_spec=None):
    nk = grid[2]
    dn = _DIMS[dims]

    def body(*refs):
        if res is None:
            a_ref, b_ref, o_ref, acc = refs
        else:
            a_ref, b_ref, r_ref, o_ref, acc = refs
        k = pl.program_id(2)

        @pl.when(k == 0)
        def _():
            acc[...] = jnp.zeros_like(acc)

        acc[...] += lax.dot_general(a_ref[...].astype(bf16), b_ref[...].astype(bf16), dn, preferred_element_type=f32)

        @pl.when(k == nk - 1)
        def _():
            r = acc[...]
            if res is not None:
                r = r + r_ref[...]
            o_ref[...] = r.astype(o_ref.dtype)

    ins = [a, b] + ([] if res is None else [res])
    specs = [a_spec, b_spec] + ([] if res is None else [res_spec])
    return pl.pallas_call(
        body, name=name, grid=grid, in_specs=specs, out_specs=out_spec, out_shape=out_shape,
        scratch_shapes=[pltpu.VMEM(acc_shape, f32)],
        compiler_params=_params("parallel", "parallel", "arbitrary"),
    )(*ins)


def _mm_rows(name, a, w, dims, out_dtype, tn, res=None, tk=None):
    T, K = a.shape
    N = w.shape[1] if dims == "nn" else w.shape[0]
    tm = _tile(T, 1032)
    tk = K if tk is None else tk
    grid = (T // tm, N // tn, K // tk)
    a_spec = pl.BlockSpec((tm, tk), lambda i, j, k: (i, k))
    if dims == "nn":
        b_spec = pl.BlockSpec((tk, tn), lambda i, j, k: (k, j))
    else:
        b_spec = pl.BlockSpec((tn, tk), lambda i, j, k: (j, k))
    o_spec = pl.BlockSpec((tm, tn), lambda i, j, k: (i, j))
    return _mm(name, a, w, dims, grid, a_spec, b_spec, jax.ShapeDtypeStruct((T, N), out_dtype), o_spec, (tm, tn),
               res=res, res_spec=None if res is None else o_spec)


def _mm_fused(name, pairs, dims, extras, epilogue, outs, rows=()):
    T, K = pairs[0][0].shape
    N = pairs[0][1].shape[1] if dims == "nn" else pairs[0][1].shape[0]
    tm = _tile(T, 344)
    tn = N
    grid = (T // tm, N // tn)
    npair, nex = len(pairs), len(extras) + len(rows)
    dn = _DIMS[dims]

    def body(*refs):
        ab = refs[:2 * npair]
        ex = refs[2 * npair:2 * npair + nex]
        o_refs = refs[2 * npair + nex:]
        accs = [lax.dot_general(ab[2 * q][...].astype(bf16), ab[2 * q + 1][...].astype(bf16), dn, preferred_element_type=f32)
                for q in range(npair)]
        vals = epilogue(accs, [e[...] for e in ex])
        for o_ref, v in zip(o_refs, vals):
            if isinstance(v, (list, tuple)):
                for s_, vs in enumerate(v):
                    o_ref[s_] = vs.astype(o_ref.dtype)
            else:
                o_ref[...] = v.astype(o_ref.dtype)

    ins, specs = [], []
    for a, w in pairs:
        ins += [a, w]
        specs.append(pl.BlockSpec((tm, K), lambda i, j: (i, 0)))
        specs.append(pl.BlockSpec((K, tn), lambda i, j: (0, j)) if dims == "nn" else pl.BlockSpec((tn, K), lambda i, j: (j, 0)))
    for arr, off in extras:
        ins.append(arr)
        specs.append(pl.BlockSpec((tm, tn), lambda i, j, off=off: (i, off + j)))
    for arr in rows:
        ins.append(arr)
        specs.append(pl.BlockSpec((1, tn), lambda i, j: (0, j)))
    shapes, ospecs = [], []
    for o in outs:
        if isinstance(o, tuple):
            dt, nseg, total, blk = o
            shapes.append(jax.ShapeDtypeStruct((total, T, N), dt))
            ospecs.append(pl.BlockSpec((nseg, tm, tn), lambda i, j, blk=blk: (blk, i, j)))
        else:
            shapes.append(jax.ShapeDtypeStruct((T, N), o))
            ospecs.append(pl.BlockSpec((tm, tn), lambda i, j: (i, j)))
    return pl.pallas_call(body, name=name, grid=grid, in_specs=specs, out_specs=ospecs, out_shape=shapes,
                          compiler_params=_params("parallel", "parallel"))(*ins)


def _glu_proj_fwd(ya0, w_glu):
    def epi(accs, tiles):
        return accs[0], tiles[0] * _sigmoid(accs[0])

    return _mm_fused("glu_proj", [(ya0, w_glu)], "nn", [(ya0, 0)], epi, [f32, bf16])


def _proj_merge_fwd(ya, yb, w_sp, w_hp, p):
    def epi(accs, tiles):
        return accs[0], accs[1], _sigmoid(tiles[0]) * accs[0] + _sigmoid(tiles[1]) * accs[1]

    return _mm_fused("proj_merge", [(ya, w_sp), (yb, w_hp)], "nn", [(p, SEG_GA), (p, SEG_GB)], epi, [f32, f32, bf16])


def _merge_bwd_fused(dh1, w_out, p, pa, pb):
    def epi(accs, tiles):
        d = accs[0]
        sa, sb = _sigmoid(tiles[0]), _sigmoid(tiles[1])
        return d * sa, d * sb, [d * tiles[2] * sa * (1.0 - sa), d * tiles[3] * sb * (1.0 - sb)]

    return _mm_fused("d_merged", [(dh1, w_out)], "nt", [(p, SEG_GA), (p, SEG_GB), (pa, 0), (pb, 0)], epi,
                     [bf16, bf16, (bf16, 2, N_SEG, SEG_GA // 2)])


def _out_proj_norm(merged, w_out, h0, g):
    def epi(accs, tiles):
        h1 = tiles[0] + accs[0]
        r = lax.rsqrt(jnp.mean(h1 * h1, axis=-1, keepdims=True) + EPS)
        return h1, h1 * r * tiles[1]

    return _mm_fused("out_proj", [(merged, w_out)], "nn", [(h0, 0)], epi, [f32, bf16], rows=[g])


def _mm_rmsnorm_bwd(name, a, b, grid, a_spec, b_spec, x, g, dres):
    T, Dm = x.shape
    tm = T // grid[0]
    nk = grid[2]

    def body(a_ref, b_ref, x_ref, g_ref, dres_ref, dx_ref, dg_ref, acc):
        i, k = pl.program_id(0), pl.program_id(2)

        @pl.when(k == 0)
        def _():
            acc[...] = jnp.zeros_like(acc)

        @pl.when((i == 0) & (k == 0))
        def _():
            dg_ref[...] = jnp.zeros_like(dg_ref)

        acc[...] += lax.dot_general(a_ref[...].astype(bf16), b_ref[...].astype(bf16), _DIMS["nt"], preferred_element_type=f32)

        @pl.when(k == nk - 1)
        def _():
            xv = x_ref[...]
            r = lax.rsqrt(jnp.mean(xv * xv, axis=-1, keepdims=True) + EPS)
            xn = xv * r
            dzv = acc[...]
            dzg = dzv * g_ref[...]
            dx_ref[...] = dres_ref[...] + r * (dzg - xn * jnp.mean(dzg * xn, axis=-1, keepdims=True))
            dg_ref[...] += jnp.sum(dzv * xn, axis=0, keepdims=True)

    row = pl.BlockSpec((tm, Dm), lambda i, j, k: (i, 0))
    par = pl.BlockSpec((1, Dm), lambda i, j, k: (0, 0))
    return pl.pallas_call(
        body, name=name, grid=grid, in_specs=[a_spec, b_spec, row, par, row], out_specs=[row, par],
        out_shape=[jax.ShapeDtypeStruct((T, Dm), f32), jax.ShapeDtypeStruct((1, Dm), f32)],
        scratch_shapes=[pltpu.VMEM((tm, Dm), f32)],
        compiler_params=_params("arbitrary", "arbitrary", "arbitrary"),
    )(a, b, x, g, dres)


def _glu_bwd_fused(dpa, w_sp, ya0, gl):
    def epi(accs, tiles):
        d = accs[0]
        s = _sigmoid(tiles[1])
        return d * tiles[0] * s * (1.0 - s), d * s

    return _mm_fused("d_ya", [(dpa, w_sp)], "nt", [(ya0, 0), (gl, 0)], epi, [bf16, f32])


def _mm_wgrad(name, a, g, tn=None):
    T, K = a.shape
    N = g.shape[1]
    tk = _tile(T, 688)
    tn = N if tn is None else tn
    grid = (1, N // tn, T // tk)
    a_spec = pl.BlockSpec((tk, K), lambda i, j, k: (k, 0))
    g_spec = pl.BlockSpec((tk, tn), lambda i, j, k: (k, j))
    o_spec = pl.BlockSpec((K, tn), lambda i, j, k: (0, j))
    return _mm(name, a, g, "tn", grid, a_spec, g_spec, jax.ShapeDtypeStruct((K, N), f32), o_spec, (K, tn))


def _rmsnorm_fwd(name, x, g):
    T, Dm = x.shape
    tr = _tile(T, 688)

    def body(x_ref, g_ref, z_ref):
        xv = x_ref[...]
        r = lax.rsqrt(jnp.mean(xv * xv, axis=-1, keepdims=True) + EPS)
        z_ref[...] = (xv * r * g_ref[...]).astype(z_ref.dtype)

    return pl.pallas_call(
        body, name=name, grid=(T // tr,),
        in_specs=[pl.BlockSpec((tr, Dm), lambda i: (i, 0)), pl.BlockSpec((1, Dm), lambda i: (0, 0))],
        out_specs=pl.BlockSpec((tr, Dm), lambda i: (i, 0)),
        out_shape=jax.ShapeDtypeStruct((T, Dm), bf16), compiler_params=_params("parallel"),
    )(x, g)


def _final_loss(h2, tgt, g, L):
    T, Dm = h2.shape
    tr = _tile(L, 688)
    per_seq = L // tr

    def body(h_ref, t_ref, g_ref, dh_ref, loss_ref, dg_ref):
        pos = (pl.program_id(0) % per_seq) * tr + lax.broadcasted_iota(jnp.int32, (tr, 1), 0)
        live = jnp.where(pos >= N_META, 1.0, 0.0)
        hv = h_ref[...]
        r = lax.rsqrt(jnp.mean(hv * hv, axis=-1, keepdims=True) + EPS)
        xn = hv * r
        gv = g_ref[...]
        err = (xn * gv - t_ref[...]) * live
        dy = err * (1.0 / Dm)
        dyg = dy * gv
        dh_ref[...] = r * (dyg - xn * jnp.mean(dyg * xn, axis=-1, keepdims=True))

        @pl.when(pl.program_id(0) == 0)
        def _():
            dg_ref[...] = jnp.zeros_like(dg_ref)
            loss_ref[...] = jnp.zeros_like(loss_ref)

        dg_ref[...] += jnp.sum(dy * xn, axis=0, keepdims=True)
        loss_ref[...] += jnp.sum(err * err) * (0.5 / Dm)

    row = pl.BlockSpec((tr, Dm), lambda i: (i, 0))
    par = pl.BlockSpec((1, Dm), lambda i: (0, 0))
    return pl.pallas_call(
        body, name="final_loss", grid=(T // tr,), in_specs=[row, row, par],
        out_specs=[row, pl.BlockSpec((1, LANES), lambda i: (0, 0)), par],
        out_shape=[jax.ShapeDtypeStruct((T, Dm), f32), jax.ShapeDtypeStruct((1, LANES), f32), jax.ShapeDtypeStruct((1, Dm), f32)],
        compiler_params=_params("arbitrary"),
    )(h2, tgt, g)


def _meta_grad(dh0_meta):
    B = dh0_meta.shape[0]

    def body(d_ref, o_ref):
        acc = d_ref[0]
        for b in range(1, B):
            acc = acc + d_ref[b]
        o_ref[...] = acc

    return pl.pallas_call(body, name="meta_grad", out_shape=jax.ShapeDtypeStruct(dh0_meta.shape[1:], f32))(dh0_meta)


def _shift_down(x, k, row):
    return jnp.where(row >= k, pltpu.roll(x, k, 0), 0.0)


def _conv_fwd(up, conv_w, conv_b, B, L):
    tc = 256
    nt = D_FF // tc

    def body(xa_ref, xb_ref, wa_ref, wb_ref, ba_ref, bb_ref, o_ref):
        head = 2 * SUBLANES
        row = lax.broadcasted_iota(jnp.int32, (head, tc), 0)

        def gated(conv):
            a = conv(xa_ref, wa_ref, ba_ref)
            b = conv(xb_ref, wb_ref, bb_ref)
            return (a * _sigmoid(a) * b).astype(o_ref.dtype)

        def conv_rolled(x_ref, w_ref, b_ref):
            x = x_ref[...]
            return b_ref[...] + w_ref[0:1, :] * pltpu.roll(x, 2, 0) + w_ref[1:2, :] * pltpu.roll(x, 1, 0) + w_ref[2:3, :] * x

        def conv_head(x_ref, w_ref, b_ref):
            x = x_ref[0:head, :]
            return (b_ref[...] + w_ref[0:1, :] * _shift_down(x, 2, row) + w_ref[1:2, :] * _shift_down(x, 1, row)
                    + w_ref[2:3, :] * x)

        o_ref[...] = gated(conv_rolled)
        o_ref[0:head, :] = gated(conv_head)

    return pl.pallas_call(
        body, name="conv_fwd", grid=(B, nt),
        in_specs=[pl.BlockSpec((L, tc), lambda b, j: (b, j)), pl.BlockSpec((L, tc), lambda b, j: (b, j + nt)),
                  pl.BlockSpec((3, tc), lambda b, j: (0, j)), pl.BlockSpec((3, tc), lambda b, j: (0, j + nt)),
                  pl.BlockSpec((1, tc), lambda b, j: (0, j)), pl.BlockSpec((1, tc), lambda b, j: (0, j + nt))],
        out_specs=pl.BlockSpec((L, tc), lambda b, j: (b, j)),
        out_shape=jax.ShapeDtypeStruct((B * L, D_FF), bf16), compiler_params=_params("parallel", "parallel"),
    )(up, up, conv_w, conv_w, conv_b, conv_b)


CONV_ROWS = 2 * SUBLANES


def _rows16(i):
    return pl.ds(pl.multiple_of(i * CONV_ROWS, CONV_ROWS), CONV_ROWS)


def _conv_taps(x_ref, i, row):
    x = x_ref[_rows16(i), :]
    live = jnp.where(i > 0, 1.0, 0.0)
    r0 = jnp.maximum(i * CONV_ROWS, 2)
    p1 = x_ref[pl.ds(r0 - 1, 1), :] * live
    p2 = x_ref[pl.ds(r0 - 2, 1), :] * live
    x1 = jnp.where(row == 0, p1, pltpu.roll(x, 1, 0))
    x2 = jnp.where(row == 0, p2, jnp.where(row == 1, p1, pltpu.roll(x, 2, 0)))
    return x, x1, x2


def _conv_bwd(up, dff, conv_w, conv_b, B, L):
    tc = 256
    nt = D_FF // tc
    n = L // CONV_ROWS

    def body(xa_ref, xb_ref, d_ref, wa_ref, wb_ref, ba_ref, bb_ref, dup_ref, dw_ref, ga_ref, gb_ref):
        row = lax.broadcasted_iota(jnp.int32, (CONV_ROWS, tc), 0)

        @pl.when(pl.program_id(1) == 0)
        def _():
            dw_ref[...] = jnp.zeros_like(dw_ref)

        zero_tail = jnp.zeros((CONV_ROWS, tc), f32)
        ga_ref[L:L + CONV_ROWS, :] = zero_tail
        gb_ref[L:L + CONV_ROWS, :] = zero_tail

        def fold(v):
            return v[0:SUBLANES, :] + v[SUBLANES:CONV_ROWS, :]

        def step(i, acc):
            taps_a = _conv_taps(xa_ref, i, row)
            taps_b = _conv_taps(xb_ref, i, row)
            a = ba_ref[...] + wa_ref[0:1, :] * taps_a[2] + wa_ref[1:2, :] * taps_a[1] + wa_ref[2:3, :] * taps_a[0]
            b = bb_ref[...] + wb_ref[0:1, :] * taps_b[2] + wb_ref[1:2, :] * taps_b[1] + wb_ref[2:3, :] * taps_b[0]
            s = _sigmoid(a)
            d = d_ref[_rows16(i), :]
            g_a = d * b * s * (1.0 + a * (1.0 - s))
            g_b = d * a * s
            ga_ref[_rows16(i), :] = g_a
            gb_ref[_rows16(i), :] = g_b
            new = []
            for g, (x, x1, x2) in ((g_a, taps_a), (g_b, taps_b)):
                new += [fold(g * x2), fold(g * x1), fold(g * x), fold(g)]
            return tuple(o + v for o, v in zip(acc, new))

        z = jnp.zeros((SUBLANES, tc), f32)
        acc = _repeat_loop(n, step, (z,) * 8)
        for h in range(2):
            for t in range(4):
                dw_ref[h, t:t + 1, :] += jnp.sum(acc[4 * h + t], axis=0, keepdims=True)

        def back(i, c):
            for h, (g_ref, w_ref) in enumerate(((ga_ref, wa_ref), (gb_ref, wb_ref))):
                g = g_ref[_rows16(i), :]
                n1 = g_ref[pl.ds(i * CONV_ROWS + CONV_ROWS, 1), :]
                n2 = g_ref[pl.ds(i * CONV_ROWS + CONV_ROWS + 1, 1), :]
                u1 = jnp.where(row == CONV_ROWS - 1, n1, pltpu.roll(g, CONV_ROWS - 1, 0))
                u2 = jnp.where(row == CONV_ROWS - 1, n2, jnp.where(row == CONV_ROWS - 2, n1, pltpu.roll(g, CONV_ROWS - 2, 0)))
                dup_ref[h, _rows16(i), :] = (w_ref[2:3, :] * g + w_ref[1:2, :] * u1 + w_ref[0:1, :] * u2).astype(dup_ref.dtype)
            return c

        _repeat_loop(n, back, 0)

    return pl.pallas_call(
        body, name="conv_bwd", grid=(nt, B),
        in_specs=[pl.BlockSpec((L, tc), lambda j, b: (b, j)), pl.BlockSpec((L, tc), lambda j, b: (b, j + nt)),
                  pl.BlockSpec((L, tc), lambda j, b: (b, j)),
                  pl.BlockSpec((3, tc), lambda j, b: (0, j)), pl.BlockSpec((3, tc), lambda j, b: (0, j + nt)),
                  pl.BlockSpec((1, tc), lambda j, b: (0, j)), pl.BlockSpec((1, tc), lambda j, b: (0, j + nt))],
        out_specs=[pl.BlockSpec((2, L, tc), lambda j, b: (0, b, j)), pl.BlockSpec((2, SUBLANES, tc), lambda j, b: (0, 0, j))],
        out_shape=[jax.ShapeDtypeStruct((2, B * L, D_FF), bf16), jax.ShapeDtypeStruct((2, SUBLANES, D_FF), f32)],
        scratch_shapes=[pltpu.VMEM((L + CONV_ROWS, tc), f32), pltpu.VMEM((L + CONV_ROWS, tc), f32)],
        compiler_params=_params("parallel", "arbitrary"),
    )(up, up, dff, conv_w, conv_w, conv_b, conv_b)


GELU_C = math.sqrt(2.0 / math.pi)
GELU_A = 0.044715


def _gelu(x):
    return 0.5 * x * (1.0 + jnp.tanh(GELU_C * (x + GELU_A * x * x * x)))


def _gelu_grad(x):
    t = jnp.tanh(GELU_C * (x + GELU_A * x * x * x))
    return 0.5 * (1.0 + t) + 0.5 * x * (1.0 - t * t) * GELU_C * (1.0 + 3.0 * GELU_A * x * x)


def _cmul_add(xr, xi, ar, ai, sr, si):
    return xr + ar * sr - ai * si, xi + ar * si + ai * sr


def _s5_project_in(u_ref, bs_ref, s_ref, L, rc):
    for r in range(0, L, rc):
        s_ref[r:r + rc, :] = jnp.dot(u_ref[r:r + rc, :].astype(bf16), bs_ref[...], preferred_element_type=f32)


def _rows8(i):
    return pl.ds(pl.multiple_of(i * SUBLANES, SUBLANES), SUBLANES)


def _repeat_loop(n, step, init):
    rep = max(u for u in (6, 4, 3, 2, 1) if n % u == 0)

    def body(t, carry):
        for u in range(rep):
            carry = step(t * rep + u, carry)
        return carry

    return lax.fori_loop(0, n // rep, body, init)


def _to_segments(src_ref, dst_ref, seg):
    def step(i, c):
        dst_ref[_rows8(i), :] = src_ref[pl.ds(i, SUBLANES, stride=seg), :]
        return c

    _repeat_loop(seg, step, 0)


def _from_segments(src_ref, dst_ref, seg):
    def step(i, c):
        dst_ref[pl.ds(i, SUBLANES, stride=seg), :] = src_ref[_rows8(i), :]
        return c

    _repeat_loop(seg, step, 0)


def _seg_local_scan(s_ref, ar, ai, seg, reverse):
    ns = SLAB_NS

    def step(j, carry):
        cr, ci = carry
        rows = _rows8(seg - 1 - j if reverse else j)
        cr, ci = _cmul_add(s_ref[rows, 0:ns], s_ref[rows, ns:2 * ns], ar, ai, cr, ci)
        s_ref[rows, 0:ns] = cr
        s_ref[rows, ns:2 * ns] = ci
        return cr, ci

    z = jnp.zeros((SUBLANES, ns), f32)
    return _repeat_loop(seg, step, (z, z))


def _seg_boundaries(fr, fi, alr, ali, reverse):
    row = lax.broadcasted_iota(jnp.int32, fr.shape, 0)
    br = jnp.zeros_like(fr)
    bi = jnp.zeros_like(fi)
    for r in (range(SUBLANES - 2, -1, -1) if reverse else range(1, SUBLANES)):
        s = r + 1 if reverse else r - 1
        nr, ni = _cmul_add(fr[s:s + 1, :], fi[s:s + 1, :], alr, ali, br[s:s + 1, :], bi[s:s + 1, :])
        br = jnp.where(row == r, nr, br)
        bi = jnp.where(row == r, ni, bi)
    return br, bi


def _s5_states(u_ref, bs_ref, pw_ref, up_ref, s_ref, L, rc):
    seg = L // SUBLANES
    ns = SLAB_NS
    _to_segments(u_ref, up_ref, seg)
    _s5_project_in(up_ref, bs_ref, s_ref, L, rc)
    ar, ai = pw_ref[0, 0:1, :], pw_ref[1, 0:1, :]
    fr, fi = _seg_local_scan(s_ref, ar, ai, seg, False)
    br, bi = _seg_boundaries(fr, fi, pw_ref[0, seg - 1:seg, :], pw_ref[1, seg - 1:seg, :], False)

    def fix(i, c):
        rows = _rows8(i)
        xr, xi = _cmul_add(s_ref[rows, 0:ns], s_ref[rows, ns:2 * ns], pw_ref[0, pl.ds(i, 1), :], pw_ref[1, pl.ds(i, 1), :], br, bi)
        s_ref[rows, 0:ns] = xr
        s_ref[rows, ns:2 * ns] = xi
        return c

    _repeat_loop(seg, fix, 0)


def _pw_spec(seg_rows, order):
    if order == "bs":
        return pl.BlockSpec((2, seg_rows, SLAB_NS), lambda b, s: (0, 0, s))
    return pl.BlockSpec((2, seg_rows, SLAB_NS), lambda s, b: (0, 0, s))


def _s5_fwd(p, bs, cs, pw, d_skip, B, L):
    rc = _tile(L, 344)
    seg = L // SUBLANES

    def body(u_ref, bs_ref, cs_ref, pw_ref, d_ref, y_ref, s_ref, up_ref, yp_ref):
        _s5_states(u_ref, bs_ref, pw_ref, up_ref, s_ref, L, rc)
        for r in range(0, L, rc):
            ypre = (jnp.dot(s_ref[r:r + rc, :].astype(bf16), cs_ref[...], preferred_element_type=f32)
                    + d_ref[...] * up_ref[r:r + rc, :])
            yp_ref[r:r + rc, :] = _gelu(ypre)
        _from_segments(yp_ref, y_ref, seg)

    ucol = SEG_U * (D_MODEL // SLAB_CH)
    return pl.pallas_call(
        body, name="s5_fwd", grid=(B, N_SLAB),
        in_specs=[pl.BlockSpec((L, SLAB_CH), lambda b, s: (b, ucol + s)),
                  pl.BlockSpec((None, SLAB_CH, 2 * SLAB_NS), lambda b, s: (s, 0, 0)),
                  pl.BlockSpec((None, 2 * SLAB_NS, SLAB_CH), lambda b, s: (s, 0, 0)),
                  _pw_spec(pw.shape[1], "bs"),
                  pl.BlockSpec((1, SLAB_CH), lambda b, s: (0, s))],
        out_specs=pl.BlockSpec((L, SLAB_CH), lambda b, s: (b, s)),
        out_shape=jax.ShapeDtypeStruct((B * L, D_MODEL), f32),
        scratch_shapes=[pltpu.VMEM((L, 2 * SLAB_NS), f32), pltpu.VMEM((L, SLAB_CH), f32), pltpu.VMEM((L, SLAB_CH), f32)],
        compiler_params=_params("parallel", "parallel"),
    )(p, bs, cs, pw, d_skip)


def _s5_bwd(p, dya0, dp, bs, cs, pw, d_skip, B, L, sums):
    rc = _tile(L, 344)
    ns = SLAB_NS
    seg = L // SUBLANES
    nx = len(sums)

    def body(u_ref, dy_ref, dp_in, bs_ref, cs_ref, pw_ref, d_ref, *rest):
        xin, (du_ref, dbs_ref, dcs_ref, da_ref, dd_ref), xout = rest[:nx], rest[nx:nx + 5], rest[nx + 5:2 * nx + 5]
        s_ref, lam_ref, up_ref, dyp_ref, nat_ref, send, recv = rest[2 * nx + 5:]
        del dp_in
        start, finish = _chip_exchange_steps(xin, xout, send, recv)

        @pl.when((pl.program_id(0) == 0) & (pl.program_id(1) == 0))
        def _():
            start()

        @pl.when(pl.program_id(1) == 0)
        def _():
            dbs_ref[...] = jnp.zeros_like(dbs_ref)
            dcs_ref[...] = jnp.zeros_like(dcs_ref)
            da_ref[...] = jnp.zeros_like(da_ref)
            dd_ref[...] = jnp.zeros_like(dd_ref)

        _s5_states(u_ref, bs_ref, pw_ref, up_ref, s_ref, L, rc)
        _to_segments(dy_ref, dyp_ref, seg)
        for r in range(0, L, rc):
            u = up_ref[r:r + rc, :]
            sb = s_ref[r:r + rc, :].astype(bf16)
            ypre = jnp.dot(sb, cs_ref[...], preferred_element_type=f32) + d_ref[...] * u
            dyp = dyp_ref[r:r + rc, :] * _gelu_grad(ypre)
            dyp_ref[r:r + rc, :] = dyp
            dd_ref[...] += jnp.sum(dyp * u, axis=0, keepdims=True)
            dypb = dyp.astype(bf16)
            dcs_ref[...] += lax.dot_general(sb, dypb, _DIMS["tn"], preferred_element_type=f32)
            lam_ref[r:r + rc, :] = lax.dot_general(dypb, cs_ref[...], _DIMS["nt"], preferred_element_type=f32)

        ar, ai = pw_ref[0, 0:1, :], -pw_ref[1, 0:1, :]
        fr, fi = _seg_local_scan(lam_ref, ar, ai, seg, True)
        br, bi = _seg_boundaries(fr, fi, pw_ref[0, seg - 1:seg, :], -pw_ref[1, seg - 1:seg, :], True)

        def fix(i, acc):
            accr, acci = acc
            rows = _rows8(i)
            k = seg - 1 - i
            xr, xi = _cmul_add(lam_ref[rows, 0:ns], lam_ref[rows, ns:2 * ns], pw_ref[0, pl.ds(k, 1), :],
                               -pw_ref[1, pl.ds(k, 1), :], br, bi)
            lam_ref[rows, 0:ns] = xr
            lam_ref[rows, ns:2 * ns] = xi
            prev = _rows8(jnp.maximum(i - 1, 0))
            live = jnp.where(i > 0, 1.0, 0.0)
            spr = s_ref[prev, 0:ns] * live
            spi = s_ref[prev, ns:2 * ns] * live
            return accr + xr * spr + xi * spi, acci + xi * spr - xr * spi

        z = jnp.zeros((SUBLANES, ns), f32)
        accr, acci = _repeat_loop(seg, fix, (z, z))
        row = lax.broadcasted_iota(jnp.int32, (SUBLANES, ns), 0)
        last = _rows8(seg - 1)
        spr = jnp.where(row == 0, 0.0, pltpu.roll(s_ref[last, 0:ns], 1, 0))
        spi = jnp.where(row == 0, 0.0, pltpu.roll(s_ref[last, ns:2 * ns], 1, 0))
        xr, xi = lam_ref[0:SUBLANES, 0:ns], lam_ref[0:SUBLANES, ns:2 * ns]
        accr = accr + xr * spr + xi * spi
        acci = acci + xi * spr - xr * spi
        da_ref[0:1, :] += jnp.sum(accr, axis=0, keepdims=True)
        da_ref[1:2, :] += jnp.sum(acci, axis=0, keepdims=True)

        for r in range(0, L, rc):
            lamb = lam_ref[r:r + rc, :].astype(bf16)
            dbs_ref[...] += lax.dot_general(up_ref[r:r + rc, :].astype(bf16), lamb, _DIMS["tn"], preferred_element_type=f32)
            nat_ref[r:r + rc, :] = (lax.dot_general(lamb, bs_ref[...], _DIMS["nt"], preferred_element_type=f32)
                                    + d_ref[...] * dyp_ref[r:r + rc, :])
        _from_segments(nat_ref, up_ref, seg)
        du_ref[...] = up_ref[...].astype(du_ref.dtype)

        @pl.when((pl.program_id(0) == N_SLAB - 1) & (pl.program_id(1) == B - 1))
        def _():
            finish()

    ucol = SEG_U * (D_MODEL // SLAB_CH)
    T = B * L
    col = pltpu.VMEM((L, SLAB_CH), f32)
    res = pl.pallas_call(
        body, name="s5_bwd", grid=(N_SLAB, B),
        in_specs=[pl.BlockSpec((L, SLAB_CH), lambda s, b: (b, ucol + s)),
                  pl.BlockSpec((L, SLAB_CH), lambda s, b: (b, s)),
                  ANY,
                  pl.BlockSpec((None, SLAB_CH, 2 * SLAB_NS), lambda s, b: (s, 0, 0)),
                  pl.BlockSpec((None, 2 * SLAB_NS, SLAB_CH), lambda s, b: (s, 0, 0)),
                  _pw_spec(pw.shape[1], "sb"),
                  pl.BlockSpec((1, SLAB_CH), lambda s, b: (0, s))] + [ANY] * nx,
        out_specs=[pl.BlockSpec((None, L, SLAB_CH), lambda s, b: (SEG_U, b, s)),
                   pl.BlockSpec((None, SLAB_CH, 2 * SLAB_NS), lambda s, b: (s, 0, 0)),
                   pl.BlockSpec((None, 2 * SLAB_NS, SLAB_CH), lambda s, b: (s, 0, 0)),
                   pl.BlockSpec((None, 2, SLAB_NS), lambda s, b: (s, 0, 0)),
                   pl.BlockSpec((1, SLAB_CH), lambda s, b: (0, s))] + [ANY] * nx,
        out_shape=[jax.ShapeDtypeStruct((N_SEG, T, D_MODEL), bf16),
                   jax.ShapeDtypeStruct((N_SLAB, SLAB_CH, 2 * SLAB_NS), f32),
                   jax.ShapeDtypeStruct((N_SLAB, 2 * SLAB_NS, SLAB_CH), f32),
                   jax.ShapeDtypeStruct((N_SLAB, 2, SLAB_NS), f32),
                   jax.ShapeDtypeStruct((1, D_MODEL), f32)] + [jax.ShapeDtypeStruct(a.shape, a.dtype) for a in sums],
        scratch_shapes=[pltpu.VMEM((L, 2 * SLAB_NS), f32), pltpu.VMEM((L, 2 * SLAB_NS), f32), col, col, col]
        + _chip_exchange_sems(nx),
        input_output_aliases={2: 0},
        compiler_params=_params("arbitrary", "arbitrary"),
    )(p, dya0, dp, bs, cs, pw, d_skip, *sums)
    return res[:5], res[5:]


def _dotb(a, b, dims="nn"):
    return lax.dot_general(a.astype(bf16), b.astype(bf16), _DIMS[dims], preferred_element_type=f32)


def _tile_scan(x, reverse):
    n, w = x.shape
    v = x.reshape(n // SUBLANES, SUBLANES, w)
    row = lax.broadcasted_iota(jnp.int32, v.shape, 1)
    for k in (1, 2, 4):
        if reverse:
            v = v + jnp.where(row < SUBLANES - k, pltpu.roll(v, SUBLANES - k, 1), 0.0)
        else:
            v = v + jnp.where(row >= k, pltpu.roll(v, k, 1), 0.0)
    p = v.reshape(n // CHUNK, 2, SUBLANES, w)
    lo, hi = p[:, 0], p[:, 1]
    if reverse:
        lo = lo + hi[:, 0:1, :]
    else:
        hi = hi + lo[:, SUBLANES - 1:SUBLANES, :]
    return jnp.stack([lo, hi], axis=1).reshape(n, w)


def _chunk_cumsum(x):
    return _tile_scan(x, False)


def _chunk_rev_cumsum(x):
    return _tile_scan(x, True)


def _chunk_last(x):
    n, w = x.shape
    p = x.reshape(n // CHUNK, CHUNK, w)
    return jnp.broadcast_to(p[:, CHUNK - 1:CHUNK, :], p.shape).reshape(n, w)


def _hgrn_local(q, fl, lb):
    sg = _sigmoid(fl)
    f = lb + (1.0 - lb) * sg
    g = jnp.log(f)
    cum = _chunk_cumsum(g)
    rest = _chunk_last(cum) - cum
    e = jnp.exp(cum)
    em = jnp.exp(-cum)
    eo = jnp.exp(rest)
    k = 1.0 - f
    return sg, f, e, em, eo, q * e, k * em, k * eo, cum + rest


def _chunk_pos(n):
    return lax.broadcasted_iota(jnp.int32, (n, HEAD_DIM), 0) & (CHUNK - 1)


def _hgrn_block_rows(L):
    return _tile(L, 688, CHUNK)


def _hgrn_specs(L, order):
    hb = D_MODEL // HEAD_DIM

    def spec(seg):
        if order == "bh":
            return pl.BlockSpec((L, HEAD_DIM), lambda b, h: (b, seg * hb + h))
        return pl.BlockSpec((L, HEAD_DIM), lambda h, b: (b, seg * hb + h))

    return [spec(SEG_Q), spec(SEG_F), spec(SEG_I), spec(SEG_OG)]


PAIR = 2 * CHUNK
CHUNK_SHIFT = CHUNK.bit_length() - 1


def _pair_steps(L, rb):
    steps = []
    nch = rb // CHUNK
    for r in range(0, L, rb):
        steps += [(r + p * PAIR, PAIR) for p in range(nch // 2)]
        if nch % 2:
            steps.append((r + (nch - 1) * CHUNK, CHUNK))
    return steps


def _pair_flags(rb):
    ci = lax.broadcasted_iota(jnp.int32, (rb, HEAD_DIM), 0) >> CHUNK_SHIFT
    odd = (ci & 1) == 1
    has_next = jnp.logical_and(jnp.logical_not(odd), ci < rb // CHUNK - 1)
    return odd, has_next


def _pair_masks(rb):
    r = lax.broadcasted_iota(jnp.int32, (rb, rb), 0)
    c = lax.broadcasted_iota(jnp.int32, (rb, rb), 1)
    rc, cc = r >> CHUNK_SHIFT, c >> CHUNK_SHIFT
    same = (rc == cc) & (c <= r)
    prev = ((rc & 1) == 1) & (cc == rc - 1)
    return same, prev


def _hgrn_pair_local(q, fl, lb, odd, has_next):
    sg, f, e, em, eo, qt, kt, ko, cend = _hgrn_local(q, fl, lb)
    n = q.shape[0]
    a = jnp.where(odd, pltpu.roll(cend, CHUNK, 0), 0.0)
    z = jnp.where(has_next, pltpu.roll(cend, n - CHUNK, 0), 0.0)
    ea, ez = jnp.exp(a), jnp.exp(z)
    return dict(sg=sg, f=f, e=e, em=em, eo=eo, qt=qt, kt=kt, ko=ko, ea=ea, ez=ez, qs=qt * ea, ks=ko * ez,
                decp=jnp.exp(cend + a + z))


def _pair_scores(qt, kt, ko, same, prev):
    return (jnp.where(same, _dotb(qt, kt, "nt"), 0.0) + jnp.where(prev, _dotb(qt, ko, "nt"), 0.0)).astype(bf16)


def _hgrn_fwd(p, lb, norm_g, B, L):
    rb = _hgrn_block_rows(L)
    steps = _pair_steps(L, rb)
    blocks = [slice(r, r + rb) for r in range(0, L, rb)]

    def body(q_ref, f_ref, v_ref, og_ref, lb_ref, ng_ref, y_ref, qs_s, ks_s, vb_s, decp_s, o_s, o2_s, u_s, sb_s):
        lbv = lb_ref[...]
        ngv = ng_ref[...]
        same, prev = _pair_masks(rb)
        odd, has_next = _pair_flags(rb)

        for rows in blocks:
            t = _hgrn_pair_local(q_ref[rows, :], f_ref[rows, :], lbv, odd, has_next)
            vb = v_ref[rows, :].astype(bf16)
            o_s[rows, :] = _dotb(_pair_scores(t["qt"], t["kt"], t["ko"], same, prev), vb)
            qs_s[rows, :] = t["qs"].astype(bf16)
            ks_s[rows, :] = t["ks"].astype(bf16)
            vb_s[rows, :] = vb
            decp_s[rows, :] = t["decp"]

        for n, (r0, nr) in enumerate(steps):
            u_s[n] = _dotb(vb_s[r0:r0 + nr, :], ks_s[r0:r0 + nr, :], "tn")
        st = jnp.zeros((HEAD_DIM, HEAD_DIM), f32)
        for n, (r0, nr) in enumerate(steps):
            sb_s[n] = st.astype(bf16)
            st = st * decp_s[r0:r0 + 1, :] + u_s[n]
        for n, (r0, nr) in enumerate(steps):
            o2_s[r0:r0 + nr, :] = _dotb(qs_s[r0:r0 + nr, :], sb_s[n], "nt")

        for rows in blocks:
            o = o_s[rows, :] + o2_s[rows, :]
            og = og_ref[rows, :]
            on = o * lax.rsqrt(jnp.mean(o * o, axis=-1, keepdims=True) + EPS) * ngv
            y_ref[rows, :] = (on * og * _sigmoid(og)).astype(y_ref.dtype)

    sb = pltpu.VMEM((L, HEAD_DIM), bf16)
    sf = pltpu.VMEM((L, HEAD_DIM), f32)
    return pl.pallas_call(
        body, name="hgrn_fwd", grid=(B, HEADS),
        in_specs=_hgrn_specs(L, "bh") + [pl.BlockSpec((1, HEAD_DIM), lambda b, h: (0, h)),
                                          pl.BlockSpec((1, HEAD_DIM), lambda b, h: (0, 0))],
        out_specs=pl.BlockSpec((L, HEAD_DIM), lambda b, h: (b, h)),
        out_shape=jax.ShapeDtypeStruct((B * L, D_MODEL), bf16),
        scratch_shapes=[sb, sb, sb, sf, sf, sf, pltpu.VMEM((len(steps), HEAD_DIM, HEAD_DIM), f32),
                        pltpu.VMEM((len(steps), HEAD_DIM, HEAD_DIM), bf16)],
        compiler_params=_params("parallel", "parallel"),
    )(p, p, p, p, lb, norm_g)


def _hgrn_bwd(p, dyb, dp, lb, norm_g, B, L):
    rb = _hgrn_block_rows(L)
    steps = _pair_steps(L, rb)
    blocks = [slice(r, r + rb) for r in range(0, L, rb)]

    def body(q_ref, f_ref, v_ref, og_ref, dy_ref, dp_in, lb_ref, ng_ref, dseg_ref, dlb_ref, dng_ref,
             st_ref, u_s, dsb_s, qt_s, kt_s, ko_s, qs_s, ks_s, vb_s, do_s,
             decp_s, o_s, o2_s, dqt_s, dkt_s, dko_s, dv_s, dv2_s, dqs_s, dks_s, ddecp_s):
        del dp_in
        lbv = lb_ref[...]
        ngv = ng_ref[...]
        same, prev = _pair_masks(rb)
        odd, has_next = _pair_flags(rb)
        pos = _chunk_pos(rb)

        @pl.when(pl.program_id(1) == 0)
        def _():
            dlb_ref[...] = jnp.zeros_like(dlb_ref)

        @pl.when((pl.program_id(0) == 0) & (pl.program_id(1) == 0))
        def _():
            dng_ref[...] = jnp.zeros_like(dng_ref)

        def scores(rows):
            return _pair_scores(qt_s[rows, :], kt_s[rows, :], ko_s[rows, :], same, prev)

        for rows in blocks:
            t = _hgrn_pair_local(q_ref[rows, :], f_ref[rows, :], lbv, odd, has_next)
            for dst, key in ((qt_s, "qt"), (kt_s, "kt"), (ko_s, "ko"), (qs_s, "qs"), (ks_s, "ks")):
                dst[rows, :] = t[key].astype(bf16)
            vb_s[rows, :] = v_ref[rows, :].astype(bf16)
            decp_s[rows, :] = t["decp"]
            o_s[rows, :] = _dotb(scores(rows), vb_s[rows, :])

        for n, (r0, nr) in enumerate(steps):
            u_s[n] = _dotb(vb_s[r0:r0 + nr, :], ks_s[r0:r0 + nr, :], "tn")
        st = jnp.zeros((HEAD_DIM, HEAD_DIM), f32)
        for n, (r0, nr) in enumerate(steps):
            st_ref[n] = st
            st = st * decp_s[r0:r0 + 1, :] + u_s[n]
        for n, (r0, nr) in enumerate(steps):
            o2_s[r0:r0 + nr, :] = _dotb(qs_s[r0:r0 + nr, :], st_ref[n], "nt")

        dng = jnp.zeros((1, HEAD_DIM), f32)
        for rows in blocks:
            o = o_s[rows, :] + o2_s[rows, :]
            og = og_ref[rows, :]
            dy = dy_ref[rows, :]
            rs = lax.rsqrt(jnp.mean(o * o, axis=-1, keepdims=True) + EPS)
            xn = o * rs
            so = _sigmoid(og)
            dseg_ref[SEG_OG, rows, :] = (dy * xn * ngv * so * (1.0 + og * (1.0 - so))).astype(dseg_ref.dtype)
            don = dy * og * so
            dng = dng + jnp.sum(don * xn, axis=0, keepdims=True)
            dxo = don * ngv
            do = (rs * (dxo - xn * jnp.mean(dxo * xn, axis=-1, keepdims=True))).astype(bf16)
            do_s[rows, :] = do
            dpf = _dotb(do, vb_s[rows, :], "nt")
            dp1 = jnp.where(same, dpf, 0.0).astype(bf16)
            dp2 = jnp.where(prev, dpf, 0.0).astype(bf16)
            dqt_s[rows, :] = _dotb(dp1, kt_s[rows, :]) + _dotb(dp2, ko_s[rows, :])
            dkt_s[rows, :] = _dotb(dp1, qt_s[rows, :], "tn")
            dko_s[rows, :] = _dotb(dp2, qt_s[rows, :], "tn")
            dv_s[rows, :] = _dotb(scores(rows), do, "tn")
        dng_ref[...] += dng

        for n, (r0, nr) in enumerate(steps):
            u_s[n] = _dotb(do_s[r0:r0 + nr, :], qs_s[r0:r0 + nr, :], "tn")
        dst = jnp.zeros((HEAD_DIM, HEAD_DIM), f32)
        for n, (r0, nr) in reversed(list(enumerate(steps))):
            dsb_s[n] = dst.astype(bf16)
            ddecp_s[r0:r0 + nr, :] = jnp.broadcast_to(jnp.sum(dst * st_ref[n], axis=0, keepdims=True), (nr, HEAD_DIM))
            dst = dst * decp_s[r0:r0 + 1, :] + u_s[n]
        for n, (r0, nr) in enumerate(steps):
            rows = slice(r0, r0 + nr)
            dqs_s[rows, :] = _dotb(do_s[rows, :], st_ref[n])
            dv2_s[rows, :] = _dotb(ks_s[rows, :], dsb_s[n], "nt")
            dks_s[rows, :] = _dotb(vb_s[rows, :], dsb_s[n])

        def chunk_sum(x):
            return _chunk_last(_chunk_cumsum(x))

        dlb = jnp.zeros((1, HEAD_DIM), f32)
        for rows in blocks:
            t = _hgrn_pair_local(q_ref[rows, :], f_ref[rows, :], lbv, odd, has_next)
            dqs, dks = dqs_s[rows, :], dks_s[rows, :]
            dqt = dqt_s[rows, :] + dqs * t["ea"]
            dko = dko_s[rows, :] + dks * t["ez"]
            dkt = dkt_s[rows, :]
            dko_ko = dko * t["ko"]
            dcum = dqt * t["qt"] - dkt * t["kt"] - dko_ko
            from_next = pltpu.roll(chunk_sum(jnp.where(odd, dqs * t["qs"], 0.0)), rb - CHUNK, 0)
            from_prev = pltpu.roll(chunk_sum(jnp.where(has_next, dks * t["ks"], 0.0)), CHUNK, 0)
            d_end = (chunk_sum(dko_ko) + jnp.where(has_next, from_next, 0.0) + jnp.where(odd, from_prev, 0.0)
                     + ddecp_s[rows, :] * t["decp"])
            dcum = dcum + jnp.where(pos == CHUNK - 1, d_end, 0.0)
            df = _chunk_rev_cumsum(dcum) / t["f"] - (dkt * t["em"] + dko * t["eo"])
            dlb = dlb + jnp.sum(df * (1.0 - t["sg"]), axis=0, keepdims=True)
            dseg_ref[SEG_Q, rows, :] = (dqt * t["e"]).astype(dseg_ref.dtype)
            dseg_ref[SEG_F, rows, :] = (df * (1.0 - lbv) * t["sg"] * (1.0 - t["sg"])).astype(dseg_ref.dtype)
            dseg_ref[SEG_I, rows, :] = (dv_s[rows, :] + dv2_s[rows, :]).astype(dseg_ref.dtype)
        dlb_ref[...] += dlb

    T = B * L
    ns = len(steps)
    sb = pltpu.VMEM((L, HEAD_DIM), bf16)
    sf = pltpu.VMEM((L, HEAD_DIM), f32)
    return pl.pallas_call(
        body, name="hgrn_bwd", grid=(HEADS, B),
        in_specs=_hgrn_specs(L, "hb") + [pl.BlockSpec((L, HEAD_DIM), lambda h, b: (b, h)), ANY,
                                          pl.BlockSpec((1, HEAD_DIM), lambda h, b: (0, h)),
                                          pl.BlockSpec((1, HEAD_DIM), lambda h, b: (0, 0))],
        out_specs=[pl.BlockSpec((4, L, HEAD_DIM), lambda h, b: (0, b, h)),
                   pl.BlockSpec((1, HEAD_DIM), lambda h, b: (0, h)),
                   pl.BlockSpec((1, HEAD_DIM), lambda h, b: (0, 0))],
        out_shape=[jax.ShapeDtypeStruct((N_SEG, T, D_MODEL), bf16), jax.ShapeDtypeStruct((1, D_MODEL), f32),
                   jax.ShapeDtypeStruct((1, HEAD_DIM), f32)],
        scratch_shapes=[pltpu.VMEM((ns, HEAD_DIM, HEAD_DIM), f32), pltpu.VMEM((ns, HEAD_DIM, HEAD_DIM), f32),
                        pltpu.VMEM((ns, HEAD_DIM, HEAD_DIM), bf16)] + [sb] * 7 + [sf] * 11,
        input_output_aliases={5: 0},
        compiler_params=_params("arbitrary", "arbitrary"),
    )(p, p, p, p, dyb, dp, lb, norm_g)


def _dz1_norm(dp, w_in_phys, h0, g, dh1):
    _, T, Dm = dp.shape
    tm = _tile(T, 688)
    return _mm_rmsnorm_bwd("dz1", dp, w_in_phys, (T // tm, 1, N_SEG),
                           pl.BlockSpec((None, tm, Dm), lambda i, j, k: (k, i, 0)),
                           pl.BlockSpec((Dm, Dm), lambda i, j, k: (0, k)), h0, g, dh1)


def _dz2_norm(dup, w_up, h1, g, dh2):
    _, T, _ = dup.shape
    tm = _tile(T, 688)
    tk = D_FF // 2
    return _mm_rmsnorm_bwd("dz2", dup, w_up, (T // tm, 1, 4),
                           pl.BlockSpec((None, tm, tk), lambda i, j, k: (k // 2, i, k % 2)),
                           pl.BlockSpec((D_MODEL, tk), lambda i, j, k: (0, k)), h1, g, dh2)


def _dw_in(z1, dp):
    _, T, Dm = dp.shape
    tk = _tile(T, 1376)
    return _mm("dw_in", z1, dp, "tn", (1, N_SEG, T // tk),
               pl.BlockSpec((tk, Dm), lambda i, j, k: (k, 0)),
               pl.BlockSpec((None, tk, Dm), lambda i, j, k: (j, k, 0)),
               jax.ShapeDtypeStruct((N_SEG, Dm, Dm), f32),
               pl.BlockSpec((None, Dm, Dm), lambda i, j, k: (j, 0, 0)), (Dm, Dm))


def _dw_up(z2, dup):
    _, T, _ = dup.shape
    tn = D_FF // 2
    tk = _tile(T, 688)
    return _mm("dw_up", z2, dup, "tn", (1, N_CHIPS, T // tk),
               pl.BlockSpec((tk, D_MODEL), lambda i, j, k: (k, 0)),
               pl.BlockSpec((None, tk, tn), lambda i, j, k: (j // 2, k, j % 2)),
               jax.ShapeDtypeStruct((N_CHIPS, D_MODEL, tn), f32),
               pl.BlockSpec((None, D_MODEL, tn), lambda i, j, k: (j, 0, 0)), (D_MODEL, tn))


def _place():
    x, y, c = lax.axis_index("x"), lax.axis_index("y"), lax.axis_index("c")
    chips = [(1 - x, y), (x, 1 - y), (1 - x, 1 - y)]
    return x, y, c, chips


def _allgather_chips(arrs):
    n = len(arrs)

    def body(*refs):
        ins, outs = refs[:n], refs[n:2 * n]
        send, recv, local = refs[2 * n:]
        x, y, c, chips = _place()
        me = 2 * x + y

        def copy(a, k, slot):
            px, py = chips[k]
            return pltpu.make_async_remote_copy(src_ref=ins[a], dst_ref=outs[a].at[slot], send_sem=send.at[3 * a + k],
                                                recv_sem=recv.at[3 * a + k], device_id=(px, py, c), device_id_type=MESH)

        for a in range(n):
            pltpu.make_async_copy(ins[a], outs[a].at[me], local.at[a]).start()
            for k in range(3):
                copy(a, k, me).start()
        for a in range(n):
            for k, (px, py) in enumerate(chips):
                copy(a, k, 2 * px + py).wait_recv()
        for a in range(n):
            pltpu.make_async_copy(ins[a], outs[a].at[me], local.at[a]).wait()
            for k in range(3):
                copy(a, k, me).wait_send()

    return pl.pallas_call(
        body, name="allgather_chips", in_specs=[ANY] * n, out_specs=[ANY] * n,
        out_shape=[jax.ShapeDtypeStruct((N_CHIPS,) + a.shape, a.dtype) for a in arrs],
        scratch_shapes=[pltpu.SemaphoreType.DMA((3 * n,)), pltpu.SemaphoreType.DMA((3 * n,)), pltpu.SemaphoreType.DMA((n,))],
    )(*arrs)


def _allgather_split(arrs):
    n = len(arrs)

    def body(*refs):
        start, finish = _gather_split_steps(refs[:n], refs[n:2 * n], *refs[2 * n:])
        start()
        finish()

    return pl.pallas_call(
        body, name="allgather_split", in_specs=[ANY] * n, out_specs=[ANY] * n,
        out_shape=[jax.ShapeDtypeStruct((N_CHIPS,) + a.shape, a.dtype) for a in arrs],
        scratch_shapes=_gather_split_sems(n),
    )(*arrs)


def _gather_split_sems(n):
    return [pltpu.SemaphoreType.DMA((3 * n,)) for _ in range(4)]


def _gather_split_steps(ins, outs, send, recv, fsend, frecv):
    n = len(ins)

    def place():
        x, y, c, chips = _place()
        return x, y, c, chips, 2 * x + y

    def half(a, core):
        rh = ins[a].shape[0] // 2
        return pl.ds(core * rh, rh)

    def copy(a, k, slot):
        x, y, c, chips, _ = place()
        px, py = chips[k]
        return pltpu.make_async_remote_copy(src_ref=ins[a].at[half(a, c), :], dst_ref=outs[a].at[slot, half(a, c), :],
                                            send_sem=send.at[3 * a + k], recv_sem=recv.at[3 * a + k],
                                            device_id=(px, py, c), device_id_type=MESH)

    def forward(a, k, core):
        x, y, c, chips, _ = place()
        px, py = chips[k]
        rows = outs[a].at[2 * px + py, half(a, core), :]
        return pltpu.make_async_remote_copy(src_ref=rows, dst_ref=rows, send_sem=fsend.at[3 * a + k],
                                            recv_sem=frecv.at[3 * a + k], device_id=(x, y, 1 - c), device_id_type=MESH)

    def start():
        me = place()[4]
        for a in range(n):
            for k in range(3):
                copy(a, k, me).start()

    def finish():
        x, y, c, chips, me = place()
        for a in range(n):
            for k, (px, py) in enumerate(chips):
                copy(a, k, 2 * px + py).wait_recv()
                forward(a, k, c).start()
        for a in range(n):
            for k in range(3):
                forward(a, k, 1 - c).wait_recv()
        for a in range(n):
            for k in range(3):
                copy(a, k, me).wait_send()
                forward(a, k, c).wait_send()

    return start, finish


def _in_proj_gather(z1, w_in, shards):
    n = len(shards)
    T, K = z1.shape
    N = w_in.shape[1]
    tm = _tile(T, 1032)
    tn = 1024
    grid = (T // tm, N // tn)

    def body(a_ref, b_ref, *rest):
        ins, o_ref, outs, sems = rest[:n], rest[n], rest[n + 1:2 * n + 1], rest[2 * n + 1:]
        start, finish = _gather_split_steps(ins, outs, *sems)
        i, j = pl.program_id(0), pl.program_id(1)

        @pl.when((i == 0) & (j == 0))
        def _():
            start()

        o_ref[...] = jnp.dot(a_ref[...], b_ref[...], preferred_element_type=f32)

        @pl.when((i == grid[0] - 1) & (j == grid[1] - 1))
        def _():
            finish()

    res = pl.pallas_call(
        body, name="in_proj", grid=grid,
        in_specs=[pl.BlockSpec((tm, K), lambda i, j: (i, 0)), pl.BlockSpec((K, tn), lambda i, j: (0, j))] + [ANY] * n,
        out_specs=[pl.BlockSpec((tm, tn), lambda i, j: (i, j))] + [ANY] * n,
        out_shape=[jax.ShapeDtypeStruct((T, N), f32)] + [jax.ShapeDtypeStruct((N_CHIPS,) + a.shape, a.dtype) for a in shards],
        scratch_shapes=_gather_split_sems(n),
        compiler_params=_params("arbitrary", "arbitrary"),
    )(z1, w_in, *shards)
    return res[0], res[1:]


def _sibling_halves(parts, name="sibling_halves"):
    n = len(parts)

    def body(*refs):
        ins, outs = refs[:n], refs[n:2 * n]
        send, recv = refs[2 * n:]
        x, y, c, _ = _place()

        def copy(a):
            rh = ins[a].shape[1] // 2
            return pltpu.make_async_remote_copy(src_ref=ins[a].at[:, pl.ds((1 - c) * rh, rh), :], dst_ref=outs[a],
                                                send_sem=send.at[a], recv_sem=recv.at[a], device_id=(x, y, 1 - c),
                                                device_id_type=MESH)

        for a in range(n):
            copy(a).start()
        for a in range(n):
            copy(a).wait_recv()
        for a in range(n):
            copy(a).wait_send()

    return pl.pallas_call(
        body, name=name, in_specs=[ANY] * n, out_specs=[ANY] * n,
        out_shape=[jax.ShapeDtypeStruct((a.shape[0], a.shape[1] // 2, a.shape[2]), a.dtype) for a in parts],
        scratch_shapes=[pltpu.SemaphoreType.DMA((n,)), pltpu.SemaphoreType.DMA((n,))],
    )(*parts)


def _add_own_half(name, part, got, core):
    nchip, R, C = part.shape
    rh = R // 2
    tr = _tile(rh, 256, 2 * SUBLANES)
    nt = rh // tr

    def body(core_ref, a_ref, b_ref, o_ref):
        del core_ref
        o_ref[...] = (a_ref[...] + b_ref[...]).astype(o_ref.dtype)

    return pl.pallas_call(
        body, name=name,
        grid_spec=pltpu.PrefetchScalarGridSpec(
            num_scalar_prefetch=1, grid=(nchip, nt),
            in_specs=[pl.BlockSpec((None, tr, C), lambda j, i, core_ref: (j, core_ref[0] * nt + i, 0)),
                      pl.BlockSpec((None, tr, C), lambda j, i, core_ref: (j, i, 0))],
            out_specs=pl.BlockSpec((None, tr, C), lambda j, i, core_ref: (j, i, 0))),
        out_shape=jax.ShapeDtypeStruct((nchip, rh, C), bf16), compiler_params=_params("parallel", "parallel"),
    )(core, part, got)


def _add_own_half_w_in(part, got, core):
    _, R, C = part.shape
    rh = R // 2
    tr = _tile(rh, 256, 2 * SUBLANES)
    nt = rh // tr
    tn = 256
    per_seg = C // tn
    per_chip = IN_COLS // N_CHIPS // tn

    def src(j):
        return ((j // per_seg + N_SEG - 1) % N_SEG, j % per_seg)

    def body(core_ref, a_ref, b_ref, o_ref):
        del core_ref
        o_ref[...] = (a_ref[...] + b_ref[...]).astype(o_ref.dtype)

    return pl.pallas_call(
        body, name="add_half_w_in",
        grid_spec=pltpu.PrefetchScalarGridSpec(
            num_scalar_prefetch=1, grid=(IN_COLS // tn, nt),
            in_specs=[pl.BlockSpec((None, tr, tn), lambda j, i, core_ref: (src(j)[0], core_ref[0] * nt + i, src(j)[1])),
                      pl.BlockSpec((None, tr, tn), lambda j, i, core_ref: (src(j)[0], i, src(j)[1]))],
            out_specs=pl.BlockSpec((None, tr, tn), lambda j, i, core_ref: (j // per_chip, i, j % per_chip))),
        out_shape=jax.ShapeDtypeStruct((N_CHIPS, rh, IN_COLS // N_CHIPS), bf16), compiler_params=_params("parallel", "parallel"),
    )(core, part, got)


def _chip_exchange(sums):
    n = len(sums)

    def body(*refs):
        start, finish = _chip_exchange_steps(refs[:n], refs[n:2 * n], *refs[2 * n:])
        start()
        finish()

    return pl.pallas_call(
        body, name="chip_exchange", in_specs=[ANY] * n, out_specs=[ANY] * n,
        out_shape=[jax.ShapeDtypeStruct(a.shape, a.dtype) for a in sums],
        scratch_shapes=_chip_exchange_sems(n),
    )(*sums)


def _chip_exchange_sems(n):
    return [pltpu.SemaphoreType.DMA((3 * n,)), pltpu.SemaphoreType.DMA((3 * n,))]


def _chip_exchange_steps(ins, outs, send, recv):
    n = len(ins)

    def copy(a, k, own_slot):
        x, y, c, chips = _place()
        px, py = chips[k]
        slot = 2 * x + y if own_slot else 2 * px + py
        return pltpu.make_async_remote_copy(src_ref=ins[a].at[2 * px + py], dst_ref=outs[a].at[slot], send_sem=send.at[3 * a + k],
                                            recv_sem=recv.at[3 * a + k], device_id=(px, py, c), device_id_type=MESH)

    def start():
        for a in range(n):
            for k in range(3):
                copy(a, k, True).start()

    def finish():
        for a in range(n):
            for k in range(3):
                copy(a, k, False).wait_recv()
        for a in range(n):
            for k in range(3):
                copy(a, k, True).wait_send()

    return start, finish


def _sum_chips(name, slots, sums, where):
    nchip, rh, C = slots.shape
    tr = _tile(rh, 256, 2 * SUBLANES)
    nt = rh // tr

    def body(where_ref, own_ref, s1_ref, s2_ref, s3_ref, o_ref):
        me = where_ref[0]
        by_dist = [r[...].astype(f32) for r in (own_ref, s1_ref, s2_ref, s3_ref)]
        acc = None
        for j in range(nchip):
            d = me ^ j
            term = jnp.where(d == 0, by_dist[0], jnp.where(d == 1, by_dist[1], jnp.where(d == 2, by_dist[2], by_dist[3])))
            acc = term if acc is None else acc + term
        o_ref[...] = acc

    def other(d):
        return pl.BlockSpec((None, tr, C), lambda i, w: (w[0] ^ d, i, 0))

    return pl.pallas_call(
        body, name=name,
        grid_spec=pltpu.PrefetchScalarGridSpec(
            num_scalar_prefetch=1, grid=(nt,),
            in_specs=[other(0), other(1), other(2), other(3)],
            out_specs=pl.BlockSpec((tr, C), lambda i, w: (w[1] * nt + i, 0))),
        out_shape=jax.ShapeDtypeStruct((2 * rh, C), f32), compiler_params=_params("parallel"),
    )(where, sums, slots, slots, slots)


def _sum_slots(name, slots):
    ns, R, C = slots.shape
    tr = _tile(R, 256)

    def body(s_ref, o_ref):
        acc = s_ref[0]
        for j in range(1, ns):
            acc = acc + s_ref[j]
        o_ref[...] = acc

    return pl.pallas_call(
        body, name=name, grid=(R // tr,), in_specs=[pl.BlockSpec((ns, tr, C), lambda i: (0, i, 0))],
        out_specs=pl.BlockSpec((tr, C), lambda i: (i, 0)), out_shape=jax.ShapeDtypeStruct((R, C), f32),
        compiler_params=_params("parallel"),
    )(slots)


def _sibling_join(fulls):
    n = len(fulls)

    def body(*refs):
        ins, outs = refs[:n], refs[n:2 * n]
        send, recv = refs[2 * n:]
        x, y, c, _ = _place()

        def copy(a, core):
            rh = ins[a].shape[0] // 2
            rows = pl.ds(core * rh, rh)
            return pltpu.make_async_remote_copy(src_ref=ins[a].at[rows, :], dst_ref=outs[a].at[rows, :], send_sem=send.at[a],
                                                recv_sem=recv.at[a], device_id=(x, y, 1 - c), device_id_type=MESH)

        for a in range(n):
            copy(a, c).start()
        for a in range(n):
            copy(a, 1 - c).wait_recv()
        for a in range(n):
            copy(a, c).wait_send()

    return pl.pallas_call(
        body, name="sibling_join", in_specs=[ANY] * n, out_specs=[ANY] * n,
        out_shape=[jax.ShapeDtypeStruct(a.shape, a.dtype) for a in fulls],
        scratch_shapes=[pltpu.SemaphoreType.DMA((n,)), pltpu.SemaphoreType.DMA((n,))],
        input_output_aliases={a: a for a in range(n)},
    )(*fulls)


def _allgather_devices(v):
    def body(v_ref, out_ref, send, recv):
        x, y, c, chips = _place()
        me, sibling = (x, y, c), (x, y, 1 - c)

        def slot(px, py, pc):
            return out_ref.at[4 * px + 2 * py + pc]

        def copy(k, block, to, src=None):
            return pltpu.make_async_remote_copy(src_ref=slot(*block) if src is None else src, dst_ref=slot(*block),
                                                send_sem=send.at[k], recv_sem=recv.at[k], device_id=to, device_id_type=MESH)

        first = [copy(0, me, sibling, src=v_ref)] + [copy(1 + j, me, (*chip, c), src=v_ref) for j, chip in enumerate(chips)]
        for cp in first:
            cp.start()
        passed = [copy(4 + j, (*chip, c), sibling) for j, chip in enumerate(chips)]
        for j, chip in enumerate(chips):
            copy(1 + j, (*chip, c), me).wait_recv()
            passed[j].start()
        copy(0, sibling, me).wait_recv()
        for j, chip in enumerate(chips):
            copy(4 + j, (*chip, 1 - c), me).wait_recv()
        for cp in first + passed:
            cp.wait_send()

    return pl.pallas_call(
        body, name="allgather_devices", in_specs=[ANY], out_specs=ANY,
        out_shape=jax.ShapeDtypeStruct((N_DEV,) + v.shape, v.dtype),
        scratch_shapes=[pltpu.SemaphoreType.DMA((N_DEV - 1,)), pltpu.SemaphoreType.DMA((N_DEV - 1,))],
    )(v)


def _adamw(name, w, g, m, v):
    R, C = w.shape
    tr = _tile(R, 256)
    c1 = 1.0 / (1.0 - ADAM_B1 ** ADAM_STEP)
    c2 = 1.0 / (1.0 - ADAM_B2 ** ADAM_STEP)

    def body(w_ref, g_ref, m_ref, v_ref, d_ref, nm_ref, nv_ref):
        gv = g_ref[...]
        nm = ADAM_B1 * m_ref[...] + (1.0 - ADAM_B1) * gv
        nv = ADAM_B2 * v_ref[...] + (1.0 - ADAM_B2) * gv * gv
        d_ref[...] = -ADAM_LR * ((nm * c1) / (jnp.sqrt(nv * c2) + ADAM_EPS) + ADAM_WD * w_ref[...])
        nm_ref[...] = nm
        nv_ref[...] = nv

    row = pl.BlockSpec((tr, C), lambda i: (i, 0))
    sh = jax.ShapeDtypeStruct((R, C), f32)
    return pl.pallas_call(body, name=name, grid=(R // tr,), in_specs=[row] * 4, out_specs=[row] * 3,
                          out_shape=[sh, sh, sh], compiler_params=_params("parallel"))(w, g, m, v)


def _adamw_update(w, g, m, v):
    c1 = 1.0 / (1.0 - ADAM_B1 ** ADAM_STEP)
    c2 = 1.0 / (1.0 - ADAM_B2 ** ADAM_STEP)
    nm = ADAM_B1 * m + (1.0 - ADAM_B1) * g
    nv = ADAM_B2 * v + (1.0 - ADAM_B2) * g * g
    return -ADAM_LR * ((nm * c1) / (jnp.sqrt(nv * c2) + ADAM_EPS) + ADAM_WD * w), nm, nv


def _adamw_many(ws, gs, ms, vs):
    n = len(ws)

    def body(*refs):
        ins, outs = refs[:4 * n], refs[4 * n:]
        for a in range(n):
            d, nm, nv = _adamw_update(ins[a][...], ins[n + a][...], ins[2 * n + a][...], ins[3 * n + a][...])
            outs[a][...] = d
            outs[n + a][...] = nm
            outs[2 * n + a][...] = nv

    shapes = [jax.ShapeDtypeStruct(a.shape, f32) for a in ws]
    return pl.pallas_call(body, name="adamw_small", out_shape=shapes * 3)(*ws, *gs, *ms, *vs)


def _zoh_parts(lr, li, log_dt):
    dt = jnp.exp(log_dt)
    mag = jnp.exp(lr * dt)
    c, s = jnp.cos(li * dt), jnp.sin(li * dt)
    ab_re, ab_im = mag * c, mag * s
    den = lr * lr + li * li
    nr = ab_re - 1.0
    coef_re = (nr * lr + ab_im * li) / den
    coef_im = (ab_im * lr - nr * li) / den
    return dt, mag, c, s, ab_re, ab_im, den, nr, coef_re, coef_im


def _zoh_fwd(lr, li, log_dt, b_re, b_im):
    def body(lr_ref, li_ref, ld_ref, br_ref, bi_ref, ar_ref, ai_ref, bbr_ref, bbi_ref):
        _, _, _, _, ab_re, ab_im, _, _, coef_re, coef_im = _zoh_parts(lr_ref[...], li_ref[...], ld_ref[...])
        ar_ref[...] = ab_re
        ai_ref[...] = ab_im
        bbr_ref[...] = coef_re * br_ref[...] - coef_im * bi_ref[...]
        bbi_ref[...] = coef_re * bi_ref[...] + coef_im * br_ref[...]

    col = jax.ShapeDtypeStruct(lr.shape, f32)
    mat = jax.ShapeDtypeStruct(b_re.shape, f32)
    return pl.pallas_call(body, name="zoh_fwd", out_shape=[col, col, mat, mat])(lr, li, log_dt, b_re, b_im)


def _zoh_bwd(lr, li, log_dt, b_re, b_im, d_ar, d_ai, d_bbr, d_bbi):
    n = lr.shape[1]
    groups = n // SSM_STATE

    def body(lr_ref, li_ref, ld_ref, br_ref, bi_ref, dar_ref, dai_ref, dbbr_ref, dbbi_ref,
             dlr_ref, dli_ref, dld_ref, dbr_ref, dbi_ref):
        lr_, li_ = lr_ref[...], li_ref[...]
        dt, mag, c, s, _, ab_im, den, nr, coef_re, coef_im = _zoh_parts(lr_, li_, ld_ref[...])
        br, bi, dbbr, dbbi = br_ref[...], bi_ref[...], dbbr_ref[...], dbbi_ref[...]
        dbr_ref[...] = coef_re * dbbr + coef_im * dbbi
        dbi_ref[...] = coef_re * dbbi - coef_im * dbbr
        d_cr = jnp.sum(dbbr * br + dbbi * bi, axis=0, keepdims=True)
        d_ci = jnp.sum(dbbi * br - dbbr * bi, axis=0, keepdims=True)
        d_nr = (d_cr * lr_ - d_ci * li_) / den
        d_abi = dai_ref[...] + (d_cr * li_ + d_ci * lr_) / den
        d_abr = dar_ref[...] + d_nr
        d_den = -(d_cr * coef_re + d_ci * coef_im) / den
        d_lr = (d_cr * nr + d_ci * ab_im) / den + 2.0 * lr_ * d_den
        d_li = (d_cr * ab_im - d_ci * nr) / den + 2.0 * li_ * d_den
        d_theta = mag * (d_abi * c - d_abr * s)
        d_arg = mag * (d_abr * c + d_abi * s)
        dlr_ref[...] = d_lr + d_arg * dt
        dli_ref[...] = d_li + d_theta * dt
        d_dt = d_arg * lr_ + d_theta * li_
        member = (lax.broadcasted_iota(jnp.int32, (n, groups), 0) >> (SSM_STATE.bit_length() - 1)
                  == lax.broadcasted_iota(jnp.int32, (n, groups), 1)).astype(f32)
        dld_ref[...] = jnp.dot(d_dt * dt, member, preferred_element_type=f32, precision=lax.Precision.HIGHEST)

    col = jax.ShapeDtypeStruct(lr.shape, f32)
    mat = jax.ShapeDtypeStruct(b_re.shape, f32)
    return pl.pallas_call(body, name="zoh_bwd", out_shape=[col, col, jax.ShapeDtypeStruct((1, groups), f32), mat, mat])(
        lr, li, log_dt, b_re, b_im, d_ar, d_ai, d_bbr, d_bbi)


def _lower_bound_fwd(logits):
    def body(x_ref, o_ref):
        x = x_ref[...]
        e = jnp.exp(x - jnp.max(x, axis=0, keepdims=True))
        o_ref[...] = e / jnp.sum(e, axis=0, keepdims=True)

    return pl.pallas_call(body, name="lower_bound_fwd", out_shape=jax.ShapeDtypeStruct(logits.shape, f32))(logits)


def _lower_bound_bwd(sm, d_lb):
    def body(sm_ref, d_ref, o_ref):
        smv = sm_ref[...]
        row = lax.broadcasted_iota(jnp.int32, smv.shape, 0)
        sm0 = smv[0:1, :]
        o_ref[...] = sm0 * d_ref[...] * (jnp.where(row == 0, 1.0, 0.0) - smv)

    return pl.pallas_call(body, name="lower_bound_bwd", out_shape=jax.ShapeDtypeStruct(sm.shape, f32))(sm, d_lb)


def _s5_tables(ab_re, ab_im, bb_re, bb_im, c_re, c_im, seg):
    eye = jnp.eye(SLAB_GROUPS, dtype=f32)

    def blk_in(bb):
        return jnp.einsum("hsgp,gk->sghkp", bb.reshape(SSM_GROUP, N_SLAB, SLAB_GROUPS, SSM_STATE), eye).reshape(
            N_SLAB, SLAB_CH, SLAB_NS)

    def blk_out(cc):
        return jnp.einsum("sghp,gk->skpgh", cc.reshape(N_SLAB, SLAB_GROUPS, SSM_GROUP, SSM_STATE), eye).reshape(
            N_SLAB, SLAB_NS, SLAB_CH)

    bs = jnp.concatenate([blk_in(bb_re), blk_in(bb_im)], axis=2).astype(bf16)
    cs = jnp.concatenate([blk_out(c_re), blk_out(-c_im)], axis=1).astype(bf16)
    n = SSM_GROUPS * SSM_STATE
    pw = _power_table(jnp.stack([ab_re.reshape(1, n), ab_im.reshape(1, n)]), -(-seg // SUBLANES))
    return bs, cs, pw


def _power_table(ab, tiles):
    n = ab.shape[2]

    def body(a_ref, o_ref):
        row = lax.broadcasted_iota(jnp.int32, (SUBLANES, n), 0)
        ar, ai = a_ref[0], a_ref[1]
        tr, ti = jnp.broadcast_to(ar, (SUBLANES, n)), jnp.broadcast_to(ai, (SUBLANES, n))
        pr, pi = ar, ai
        for r in range(1, SUBLANES):
            pr, pi = pr * ar - pi * ai, pr * ai + pi * ar
            tr = jnp.where(row == r, pr, tr)
            ti = jnp.where(row == r, pi, ti)
        o_ref[0, 0:SUBLANES, :] = tr
        o_ref[1, 0:SUBLANES, :] = ti

        def step(j, carry):
            cr, ci = carry
            cr, ci = cr * pr - ci * pi, cr * pi + ci * pr
            o_ref[0, _rows8(j), :] = cr
            o_ref[1, _rows8(j), :] = ci
            return cr, ci

        lax.fori_loop(1, tiles, step, (tr, ti))

    return pl.pallas_call(body, name="power_table", out_shape=jax.ShapeDtypeStruct((2, SUBLANES * tiles, n), f32))(ab)


def _s5_table_grads(dbs, dcs, da):
    eye = jnp.eye(SLAB_GROUPS, dtype=f32)
    d6 = dbs.reshape(N_SLAB, SLAB_GROUPS, SSM_GROUP, 2, SLAB_GROUPS, SSM_STATE)
    dbb = jnp.einsum("sghrkp,gk->rhsgp", d6, eye).reshape(2, SSM_GROUP, SSM_GROUPS * SSM_STATE)
    c6 = dcs.reshape(N_SLAB, 2, SLAB_GROUPS, SSM_STATE, SLAB_GROUPS, SSM_GROUP)
    dcc = jnp.einsum("srkpgh,gk->rsghp", c6, eye).reshape(2, SSM_GROUPS, SSM_GROUP, SSM_STATE)
    dab = da.transpose(1, 0, 2).reshape(2, SSM_GROUPS, SSM_STATE)
    return dab[0], dab[1], dbb[0], dbb[1], dcc[0], -dcc[1]


SMALL = ["mix_norm_g", "ssm_lambda_re", "ssm_lambda_im", "ssm_log_dt", "ssm_b_re", "ssm_b_im", "ssm_c_re", "ssm_c_im",
         "ssm_d", "hgrn_lb_logits", "hgrn_norm_g", "ffn_norm_g", "conv_b", "final_norm_g"]
SHARDED_SMALL = ["meta_tokens", "conv_w"]
BIG = ["w_in", "ssm_w_glu", "w_ssm_proj", "w_hgrn_proj", "w_out", "w_up", "w_down"]
WEIGHTS = ['meta_tokens', 'mix_norm_g', 'w_in', 'ssm_lambda_re', 'ssm_lambda_im', 'ssm_log_dt', 'ssm_b_re', 'ssm_b_im',
           'ssm_c_re', 'ssm_c_im', 'ssm_d', 'ssm_w_glu', 'w_ssm_proj', 'hgrn_lb_logits', 'hgrn_norm_g', 'w_hgrn_proj',
           'w_out', 'ffn_norm_g', 'w_up', 'conv_w', 'conv_b', 'w_down', 'final_norm_g']


LATER = [k for k in BIG if k != "w_in"]


def _full_weights(gathered, shards, chip):
    Dm = D_MODEL
    g = {k: lax.dynamic_update_slice(gathered[k], shards[k][None], (chip, 0, 0)) for k in gathered}
    full = {}
    for k, v in g.items():
        if k == "w_in":
            full[k] = jnp.roll(v.transpose(1, 0, 2).reshape(Dm, IN_COLS), -Dm, axis=1)
        elif k == "w_up":
            full[k] = v.transpose(1, 0, 2).reshape(Dm, 2 * D_FF)
        else:
            full[k] = v.reshape(-1, Dm)
    return full


def _local_grads(x, tgt, meta, w, full, shards, chip, core):
    B, S, Dm = x.shape
    L = S + N_META
    T = B * L
    h0 = jnp.concatenate([jnp.broadcast_to(meta[None], (B, N_META, Dm)), x], axis=1).reshape(T, Dm)

    lb_all = _lower_bound_fwd(w["hgrn_lb_logits"])
    lb = lb_all[0:1]
    gp = SSM_GROUPS * SSM_STATE
    zoh_in = (w["ssm_lambda_re"].reshape(1, gp), w["ssm_lambda_im"].reshape(1, gp),
              jnp.repeat(w["ssm_log_dt"].reshape(SSM_GROUPS, 1), SSM_STATE, axis=1).reshape(1, gp),
              w["ssm_b_re"].reshape(gp, SSM_GROUP).T, w["ssm_b_im"].reshape(gp, SSM_GROUP).T)
    ab_re, ab_im, bb_re, bb_im = _zoh_fwd(*zoh_in)
    bs, cs, pw = _s5_tables(ab_re, ab_im, bb_re, bb_im, w["ssm_c_re"][0], w["ssm_c_im"][0], L // SUBLANES)

    z1 = _rmsnorm_fwd("mix_norm", h0, w["mix_norm_g"])
    p, gathered = _in_proj_gather(z1, full["w_in"], [shards[k] for k in LATER])
    full = {**full, **_full_weights(dict(zip(LATER, gathered)), shards, chip)}
    ya0 = _s5_fwd(p, bs, cs, pw, w["ssm_d"], B, L)
    gl, ya = _glu_proj_fwd(ya0, full["ssm_w_glu"])
    yb = _hgrn_fwd(p, lb, w["hgrn_norm_g"], B, L)
    pa, pb, merged = _proj_merge_fwd(ya, yb, full["w_ssm_proj"], full["w_hgrn_proj"], p)
    h1, z2 = _out_proj_norm(merged, full["w_out"], h0, w["ffn_norm_g"])
    up = _mm_rows("up_proj", z2, full["w_up"], "nn", f32, D_FF // 2)
    ff = _conv_fwd(up, full["conv_w"], w["conv_b"], B, L)
    h2 = _mm_rows("down_proj", ff, full["w_down"], "nn", f32, 1024, res=h1, tk=D_FF // 2)

    tgt_rows = jnp.pad(tgt, ((0, 0), (N_META, 0), (0, 0))).reshape(T, Dm)
    dh2, loss, d_final_g = _final_loss(h2, tgt_rows, w["final_norm_g"].reshape(1, Dm), L)

    dff = _mm_rows("d_ff", dh2, full["w_down"], "nt", f32, D_FF // 2)
    g_w_down = _mm_wgrad("dw_down", ff, dh2, tn=512)
    dup, dconv = _conv_bwd(up, dff, full["conv_w"], w["conv_b"], B, L)
    g_w_up = _dw_up(z2, dup)
    dh1, d_ffn_g = _dz2_norm(dup, full["w_up"], h1, w["ffn_norm_g"], dh2)

    g_w_out = _mm_wgrad("dw_out", merged, dh1)
    dpa, dpb, dp = _merge_bwd_fused(dh1, full["w_out"], p, pa, pb)
    dgl, dya0_direct = _glu_bwd_fused(dpa, full["w_ssm_proj"], ya0, gl)
    g_w_ssm_proj = _mm_wgrad("dw_ssm_proj", ya, dpa)
    dyb = _mm_rows("d_yb", dpb, full["w_hgrn_proj"], "nt", f32, 1024)
    g_w_hgrn_proj = _mm_wgrad("dw_hgrn_proj", yb, dpb)
    dp, d_lb, d_hgrn_g = _hgrn_bwd(p, dyb, dp, lb, w["hgrn_norm_g"], B, L)
    dya0 = _mm_rows("d_ya0", dgl, full["ssm_w_glu"], "nt", f32, 1024, res=dya0_direct)
    g_w_glu = _mm_wgrad("dw_glu", ya0, dgl)
    parts = {
        "ssm_w_glu": g_w_glu.reshape(N_CHIPS, Dm // N_CHIPS, Dm), "w_ssm_proj": g_w_ssm_proj.reshape(N_CHIPS, Dm // N_CHIPS, Dm),
        "w_hgrn_proj": g_w_hgrn_proj.reshape(N_CHIPS, Dm // N_CHIPS, Dm), "w_out": g_w_out.reshape(N_CHIPS, Dm // N_CHIPS, Dm),
        "w_up": g_w_up, "w_down": g_w_down.reshape(N_CHIPS, D_FF // N_CHIPS, Dm),
    }
    got = _sibling_halves([parts[k] for k in LATER])
    sums = {k: _add_own_half("add_half_" + k, parts[k], gt, core) for k, gt in zip(LATER, got)}
    (dp, dbs, dcs, da, d_skip), slots_later = _s5_bwd(p, dya0, dp, bs, cs, pw, w["ssm_d"], B, L, [sums[k] for k in LATER])
    slots = dict(zip(LATER, slots_later))
    g_w_in = _dw_in(z1, dp)
    dh0, d_mix_g = _dz1_norm(dp, full["w_in"], h0, w["mix_norm_g"], dh1)

    dh0 = dh0.reshape(B, L, Dm)
    grad_x = dh0[:, N_META:]
    d_meta = _meta_grad(dh0[:, :N_META])

    d_ab_re, d_ab_im, d_bb_re, d_bb_im, d_c_re, d_c_im = _s5_table_grads(dbs, dcs, da)
    d_lr, d_li, d_log_dt, d_b_re, d_b_im = _zoh_bwd(*zoh_in, d_ab_re.reshape(1, gp), d_ab_im.reshape(1, gp), d_bb_re, d_bb_im)
    gps = (SSM_GROUPS, SSM_STATE)
    d_lr, d_li, d_log_dt = d_lr.reshape(gps), d_li.reshape(gps), d_log_dt.reshape(SSM_GROUPS)
    d_b_re, d_b_im = d_b_re.T.reshape(gps + (SSM_GROUP,)), d_b_im.T.reshape(gps + (SSM_GROUP,))
    d_logits = _lower_bound_bwd(lb_all, d_lb)
    small = {
        "meta_tokens": d_meta, "mix_norm_g": d_mix_g, "ssm_lambda_re": d_lr[None], "ssm_lambda_im": d_li[None],
        "ssm_log_dt": d_log_dt[None], "ssm_b_re": d_b_re[None], "ssm_b_im": d_b_im[None], "ssm_c_re": d_c_re[None],
        "ssm_c_im": d_c_im[None], "ssm_d": d_skip, "hgrn_lb_logits": d_logits, "hgrn_norm_g": d_hgrn_g,
        "ffn_norm_g": d_ffn_g, "conv_w": dconv[:, 0:3, :].transpose(1, 0, 2).reshape(3, 2 * D_FF),
        "conv_b": dconv[:, 3, :].reshape(1, 2 * D_FF), "final_norm_g": d_final_g.reshape(Dm),
    }
    sums["w_in"] = _add_own_half_w_in(g_w_in, _sibling_halves([g_w_in], "sibling_halves_w_in")[0], core)
    slots["w_in"] = _chip_exchange([sums["w_in"]])[0]
    return loss, grad_x, sums, slots, small


PACK_ROWS = 256


def _pack(parts):
    flat = jnp.concatenate([parts[k].reshape(-1) for k in parts])
    n = flat.shape[0]
    rows = -(-n // (PACK_ROWS * LANES)) * PACK_ROWS
    flat = jnp.pad(flat, (0, rows * LANES - n))
    return flat.reshape(rows, LANES)


def _unpack(packed, like):
    flat = packed.reshape(-1)
    out, o = {}, 0
    for k, ref in like.items():
        n = math.prod(ref.shape)
        out[k] = flat[o:o + n].reshape(ref.shape)
        o += n
    return out


def kernel(x, meta_tokens, mix_norm_g, w_in, ssm_lambda_re, ssm_lambda_im, ssm_log_dt, ssm_b_re, ssm_b_im, ssm_c_re, ssm_c_im, ssm_d, ssm_w_glu, w_ssm_proj, hgrn_lb_logits, hgrn_norm_g, w_hgrn_proj, w_out, ffn_norm_g, w_up, conv_w, conv_b, w_down, final_norm_g, loss_target, m_meta_tokens, m_mix_norm_g, m_w_in, m_ssm_lambda_re, m_ssm_lambda_im, m_ssm_log_dt, m_ssm_b_re, m_ssm_b_im, m_ssm_c_re, m_ssm_c_im, m_ssm_d, m_ssm_w_glu, m_w_ssm_proj, m_hgrn_lb_logits, m_hgrn_norm_g, m_w_hgrn_proj, m_w_out, m_ffn_norm_g, m_w_up, m_conv_w, m_conv_b, m_w_down, m_final_norm_g, v_meta_tokens, v_mix_norm_g, v_w_in, v_ssm_lambda_re, v_ssm_lambda_im, v_ssm_log_dt, v_ssm_b_re, v_ssm_b_im, v_ssm_c_re, v_ssm_c_im, v_ssm_d, v_ssm_w_glu, v_w_ssm_proj, v_hgrn_lb_logits, v_hgrn_norm_g, v_w_hgrn_proj, v_w_out, v_ffn_norm_g, v_w_up, v_conv_w, v_conv_b, v_w_down, v_final_norm_g):
    args = dict(locals())
    w = {k: args[k] for k in WEIGHTS}
    mom = {k: args["m_" + k] for k in WEIGHTS}
    var = {k: args["v_" + k] for k in WEIGHTS}
    Dm = D_MODEL
    cx, cy, cc = lax.axis_index("x"), lax.axis_index("y"), lax.axis_index("c")
    chip = 2 * cx + cy

    shards = {k: w[k][0].astype(bf16) for k in BIG}
    g_meta, g_cw = _allgather_chips([w["meta_tokens"], w["conv_w"][0]])
    full = _full_weights({"w_in": _allgather_split([shards["w_in"]])[0]}, shards, chip)
    full["conv_w"] = g_cw.transpose(1, 0, 2).reshape(3, 2 * D_FF)
    meta_full = g_meta.transpose(1, 0, 2).reshape(N_META, Dm)

    core = cc.reshape(1).astype(jnp.int32)
    loss_part, grad_x, sums, slots, small = _local_grads(x, loss_target, meta_full, w, full, shards, chip, core)

    where = jnp.stack([chip, cc]).astype(jnp.int32)
    fulls = [_sum_chips("sum_chips_" + k, slots[k], sums[k], where) for k in BIG]
    g_big = dict(zip(BIG, _sibling_join(fulls)))

    small_all = dict(small)
    small_all["loss"] = loss_part[0, 0:1]
    packed = _pack(small_all)
    slots_dev = lax.dynamic_update_slice(_allgather_devices(packed), packed[None], (2 * chip + cc, 0, 0))
    reduced = _unpack(_sum_slots("sum_devices", slots_dev), small_all)
    loss = reduced.pop("loss")[0]
    mcols = Dm // N_CHIPS
    ccols = 2 * D_FF // N_CHIPS
    grads = {k: reduced[k] for k in SMALL}
    grads["meta_tokens"] = lax.dynamic_slice(reduced["meta_tokens"], (0, chip * mcols), (N_META, mcols))
    grads["conv_w"] = lax.dynamic_slice(reduced["conv_w"], (0, chip * ccols), (3, ccols))[None]
    for k in BIG:
        grads[k] = g_big[k][None]

    delta, new_m, new_v = {}, {}, {}
    for k in BIG:
        shp = w[k].shape
        d, nm, nv = _adamw("adamw_" + k, w[k][0], grads[k][0], mom[k][0], var[k][0])
        delta[k], new_m[k], new_v[k] = d.reshape(shp), nm.reshape(shp), nv.reshape(shp)
    rest = SMALL + SHARDED_SMALL

    def flat2(a):
        return a.reshape(-1, a.shape[-1])

    outs = _adamw_many(*[[flat2(t[k]) for k in rest] for t in (w, grads, mom, var)])
    n = len(rest)
    for j, dst in enumerate((delta, new_m, new_v)):
        dst.update({k: o.reshape(w[k].shape) for k, o in zip(rest, outs[j * n:(j + 1) * n])})

    return (loss, grad_x, *[grads[k].reshape(w[k].shape) for k in WEIGHTS], *[delta[k] for k in WEIGHTS],
            *[new_m[k] for k in WEIGHTS], *[new_v[k] for k in WEIGHTS])
```

```python
import math

import jax
import jax.numpy as jnp
from jax import lax
from jax.experimental import pallas as pl
from jax.experimental.pallas import tpu as pltpu

f32 = jnp.float32
bf16 = jnp.bfloat16

D_MODEL = 1024
N_META = 16
SSM_GROUP = 16
SSM_GROUPS = 64
SSM_STATE = 64
SLAB_GROUPS = 8
N_SLAB = SSM_GROUPS // SLAB_GROUPS
SLAB_CH = SLAB_GROUPS * SSM_GROUP
SLAB_NS = SLAB_GROUPS * SSM_STATE
HEADS = 8
HEAD_DIM = 128
CHUNK = 16
D_FF = 2816
IN_COLS = 7168
EPS = 1e-6
SUBLANES = 8
LANES = 128
N_CHIPS = 4
N_DEV = 8
ADAM_LR, ADAM_B1, ADAM_B2, ADAM_EPS, ADAM_WD, ADAM_STEP = 0.001, 0.9, 0.999, 1e-08, 0.01, 10
MESH = pl.DeviceIdType.MESH
ANY = pl.BlockSpec(memory_space=pl.ANY)

SEG_Q, SEG_F, SEG_I, SEG_OG, SEG_GA, SEG_GB, SEG_U = range(7)
N_SEG = 7


def _tile(n, target, mult=SUBLANES):
    best = None
    for d in range(mult, min(n, target) + 1, mult):
        if n % d == 0:
            best = d
    return n if best is None else best


def _params(*sem):
    return pltpu.CompilerParams(dimension_semantics=sem)


def _sigmoid(x):
    return 1.0 / (1.0 + jnp.exp(-x))


_DIMS = {"nn": (((1,), (0,)), ((), ())), "nt": (((1,), (1,)), ((), ())), "tn": (((0,), (0,)), ((), ()))}


def _mm(name, a, b, dims, grid, a_spec, b_spec, out_shape, out_spec, acc_shape, res=None, res_spec=None):
    nk = grid[2]
    dn = _DIMS[dims]

    def body(*refs):
        if res is None:
            a_ref, b_ref, o_ref, acc = refs
        else:
            a_ref, b_ref, r_ref, o_ref, acc = refs
        k = pl.program_id(2)

        @pl.when(k == 0)
        def _():
            acc[...] = jnp.zeros_like(acc)

        acc[...] += lax.dot_general(a_ref[...].astype(bf16), b_ref[...].astype(bf16), dn, preferred_element_type=f32)

        @pl.when(k == nk - 1)
        def _():
            r = acc[...]
            if res is not None:
                r = r + r_ref[...]
            o_ref[...] = r.astype(o_ref.dtype)

    ins = [a, b] + ([] if res is None else [res])
    specs = [a_spec, b_spec] + ([] if res is None else [res_spec])
    return pl.pallas_call(
        body, name=name, grid=grid, in_specs=specs, out_specs=out_spec, out_shape=out_shape,
        scratch_shapes=[pltpu.VMEM(acc_shape, f32)],
        compiler_params=_params("parallel", "parallel", "arbitrary"),
    )(*ins)


def _mm_rows(name, a, w, dims, out_dtype, tn, res=None, tk=None):
    T, K = a.shape
    N = w.shape[1] if dims == "nn" else w.shape[0]
    tm = _tile(T, 1032)
    tk = K if tk is None else tk
    grid = (T // tm, N // tn, K // tk)
    a_spec = pl.BlockSpec((tm, tk), lambda i, j, k: (i, k))
    if dims == "nn":
        b_spec = pl.BlockSpec((tk, tn), lambda i, j, k: (k, j))
    else:
        b_spec = pl.BlockSpec((tn, tk), lambda i, j, k: (j, k))
    o_spec = pl.BlockSpec((tm, tn), lambda i, j, k: (i, j))
    return _mm(name, a, w, dims, grid, a_spec, b_spec, jax.ShapeDtypeStruct((T, N), out_dtype), o_spec, (tm, tn),
               res=res, res_spec=None if res is None else o_spec)


def _mm_fused(name, pairs, dims, extras, epilogue, outs, rows=()):
    T, K = pairs[0][0].shape
    N = pairs[0][1].shape[1] if dims == "nn" else pairs[0][1].shape[0]
    tm = _tile(T, 344)
    tn = N
    grid = (T // tm, N // tn)
    npair, nex = len(pairs), len(extras) + len(rows)
    dn = _DIMS[dims]

    def body(*refs):
        ab = refs[:2 * npair]
        ex = refs[2 * npair:2 * npair + nex]
        o_refs = refs[2 * npair + nex:]
        accs = [lax.dot_general(ab[2 * q][...].astype(bf16), ab[2 * q + 1][...].astype(bf16), dn, preferred_element_type=f32)
                for q in range(npair)]
        vals = epilogue(accs, [e[...] for e in ex])
        for o_ref, v in zip(o_refs, vals):
            if isinstance(v, (list, tuple)):
                for s_, vs in enumerate(v):
                    o_ref[s_] = vs.astype(o_ref.dtype)
            else:
                o_ref[...] = v.astype(o_ref.dtype)

    ins, specs = [], []
    for a, w in pairs:
        ins += [a, w]
        specs.append(pl.BlockSpec((tm, K), lambda i, j: (i, 0)))
        specs.append(pl.BlockSpec((K, tn), lambda i, j: (0, j)) if dims == "nn" else pl.BlockSpec((tn, K), lambda i, j: (j, 0)))
    for arr, off in extras:
        ins.append(arr)
        specs.append(pl.BlockSpec((tm, tn), lambda i, j, off=off: (i, off + j)))
    for arr in rows:
        ins.append(arr)
        specs.append(pl.BlockSpec((1, tn), lambda i, j: (0, j)))
    shapes, ospecs = [], []
    for o in outs:
        if isinstance(o, tuple):
            dt, nseg, total, blk = o
            shapes.append(jax.ShapeDtypeStruct((total, T, N), dt))
            ospecs.append(pl.BlockSpec((nseg, tm, tn), lambda i, j, blk=blk: (blk, i, j)))
        else:
            shapes.append(jax.ShapeDtypeStruct((T, N), o))
            ospecs.append(pl.BlockSpec((tm, tn), lambda i, j: (i, j)))
    return pl.pallas_call(body, name=name, grid=grid, in_specs=specs, out_specs=ospecs, out_shape=shapes,
                          compiler_params=_params("parallel", "parallel"))(*ins)


def _glu_proj_fwd(ya0, w_glu):
    def epi(accs, tiles):
        return accs[0], tiles[0] * _sigmoid(accs[0])

    return _mm_fused("glu_proj", [(ya0, w_glu)], "nn", [(ya0, 0)], epi, [f32, bf16])


def _proj_merge_fwd(ya, yb, w_sp, w_hp, p):
    def epi(accs, tiles):
        return accs[0], accs[1], _sigmoid(tiles[0]) * accs[0] + _sigmoid(tiles[1]) * accs[1]

    return _mm_fused("proj_merge", [(ya, w_sp), (yb, w_hp)], "nn", [(p, SEG_GA), (p, SEG_GB)], epi, [f32, f32, bf16])


def _merge_bwd_fused(dh1, w_out, p, pa, pb):
    def epi(accs, tiles):
        d = accs[0]
        sa, sb = _sigmoid(tiles[0]), _sigmoid(tiles[1])
        return d * sa, d * sb, [d * tiles[2] * sa * (1.0 - sa), d * tiles[3] * sb * (1.0 - sb)]

    return _mm_fused("d_merged", [(dh1, w_out)], "nt", [(p, SEG_GA), (p, SEG_GB), (pa, 0), (pb, 0)], epi,
                     [bf16, bf16, (bf16, 2, N_SEG, SEG_GA // 2)])


def _out_proj_norm(merged, w_out, h0, g):
    def epi(accs, tiles):
        h1 = tiles[0] + accs[0]
        r = lax.rsqrt(jnp.mean(h1 * h1, axis=-1, keepdims=True) + EPS)
        return h1, h1 * r * tiles[1]

    return _mm_fused("out_proj", [(merged, w_out)], "nn", [(h0, 0)], epi, [f32, bf16], rows=[g])


def _mm_rmsnorm_bwd(name, a, b, grid, a_spec, b_spec, x, g, dres):
    T, Dm = x.shape
    tm = T // grid[0]
    nk = grid[2]

    def body(a_ref, b_ref, x_ref, g_ref, dres_ref, dx_ref, dg_ref, acc):
        i, k = pl.program_id(0), pl.program_id(2)

        @pl.when(k == 0)
        def _():
            acc[...] = jnp.zeros_like(acc)

        @pl.when((i == 0) & (k == 0))
        def _():
            dg_ref[...] = jnp.zeros_like(dg_ref)

        acc[...] += lax.dot_general(a_ref[...].astype(bf16), b_ref[...].astype(bf16), _DIMS["nt"], preferred_element_type=f32)

        @pl.when(k == nk - 1)
        def _():
            xv = x_ref[...]
            r = lax.rsqrt(jnp.mean(xv * xv, axis=-1, keepdims=True) + EPS)
            xn = xv * r
            dzv = acc[...]
            dzg = dzv * g_ref[...]
            dx_ref[...] = dres_ref[...] + r * (dzg - xn * jnp.mean(dzg * xn, axis=-1, keepdims=True))
            dg_ref[...] += jnp.sum(dzv * xn, axis=0, keepdims=True)

    row = pl.BlockSpec((tm, Dm), lambda i, j, k: (i, 0))
    par = pl.BlockSpec((1, Dm), lambda i, j, k: (0, 0))
    return pl.pallas_call(
        body, name=name, grid=grid, in_specs=[a_spec, b_spec, row, par, row], out_specs=[row, par],
        out_shape=[jax.ShapeDtypeStruct((T, Dm), f32), jax.ShapeDtypeStruct((1, Dm), f32)],
        scratch_shapes=[pltpu.VMEM((tm, Dm), f32)],
        compiler_params=_params("arbitrary", "arbitrary", "arbitrary"),
    )(a, b, x, g, dres)


def _glu_bwd_fused(dpa, w_sp, ya0, gl):
    def epi(accs, tiles):
        d = accs[0]
        s = _sigmoid(tiles[1])
        return d * tiles[0] * s * (1.0 - s), d * s

    return _mm_fused("d_ya", [(dpa, w_sp)], "nt", [(ya0, 0), (gl, 0)], epi, [bf16, f32])


def _mm_wgrad(name, a, g, tn=None):
    T, K = a.shape
    N = g.shape[1]
    tk = _tile(T, 688)
    tn = N if tn is None else tn
    grid = (1, N // tn, T // tk)
    a_spec = pl.BlockSpec((tk, K), lambda i, j, k: (k, 0))
    g_spec = pl.BlockSpec((tk, tn), lambda i, j, k: (k, j))
    o_spec = pl.BlockSpec((K, tn), lambda i, j, k: (0, j))
    return _mm(name, a, g, "tn", grid, a_spec, g_spec, jax.ShapeDtypeStruct((K, N), f32), o_spec, (K, tn))


def _rmsnorm_fwd(name, x, g):
    T, Dm = x.shape
    tr = _tile(T, 688)

    def body(x_ref, g_ref, z_ref):
        xv = x_ref[...]
        r = lax.rsqrt(jnp.mean(xv * xv, axis=-1, keepdims=True) + EPS)
        z_ref[...] = (xv * r * g_ref[...]).astype(z_ref.dtype)

    return pl.pallas_call(
        body, name=name, grid=(T // tr,),
        in_specs=[pl.BlockSpec((tr, Dm), lambda i: (i, 0)), pl.BlockSpec((1, Dm), lambda i: (0, 0))],
        out_specs=pl.BlockSpec((tr, Dm), lambda i: (i, 0)),
        out_shape=jax.ShapeDtypeStruct((T, Dm), bf16), compiler_params=_params("parallel"),
    )(x, g)


def _final_loss(h2, tgt, g, L):
    T, Dm = h2.shape
    tr = _tile(L, 688)
    per_seq = L // tr

    def body(h_ref, t_ref, g_ref, dh_ref, loss_ref, dg_ref):
        pos = (pl.program_id(0) % per_seq) * tr + lax.broadcasted_iota(jnp.int32, (tr, 1), 0)
        live = jnp.where(pos >= N_META, 1.0, 0.0)
        hv = h_ref[...]
        r = lax.rsqrt(jnp.mean(hv * hv, axis=-1, keepdims=True) + EPS)
        xn = hv * r
        gv = g_ref[...]
        err = (xn * gv - t_ref[...]) * live
        dy = err * (1.0 / Dm)
        dyg = dy * gv
        dh_ref[...] = r * (dyg - xn * jnp.mean(dyg * xn, axis=-1, keepdims=True))

        @pl.when(pl.program_id(0) == 0)
        def _():
            dg_ref[...] = jnp.zeros_like(dg_ref)
            loss_ref[...] = jnp.zeros_like(loss_ref)

        dg_ref[...] += jnp.sum(dy * xn, axis=0, keepdims=True)
        loss_ref[...] += jnp.sum(err * err) * (0.5 / Dm)

    row = pl.BlockSpec((tr, Dm), lambda i: (i, 0))
    par = pl.BlockSpec((1, Dm), lambda i: (0, 0))
    return pl.pallas_call(
        body, name="final_loss", grid=(T // tr,), in_specs=[row, row, par],
        out_specs=[row, pl.BlockSpec((1, LANES), lambda i: (0, 0)), par],
        out_shape=[jax.ShapeDtypeStruct((T, Dm), f32), jax.ShapeDtypeStruct((1, LANES), f32), jax.ShapeDtypeStruct((1, Dm), f32)],
        compiler_params=_params("arbitrary"),
    )(h2, tgt, g)


def _meta_grad(dh0_meta):
    B = dh0_meta.shape[0]

    def body(d_ref, o_ref):
        acc = d_ref[0]
        for b in range(1, B):
            acc = acc + d_ref[b]
        o_ref[...] = acc

    return pl.pallas_call(body, name="meta_grad", out_shape=jax.ShapeDtypeStruct(dh0_meta.shape[1:], f32))(dh0_meta)


def _shift_down(x, k, row):
    return jnp.where(row >= k, pltpu.roll(x, k, 0), 0.0)


def _conv_fwd(up, conv_w, conv_b, B, L):
    tc = 256
    nt = D_FF // tc

    def body(xa_ref, xb_ref, wa_ref, wb_ref, ba_ref, bb_ref, o_ref):
        head = 2 * SUBLANES
        row = lax.broadcasted_iota(jnp.int32, (head, tc), 0)

        def gated(conv):
            a = conv(xa_ref, wa_ref, ba_ref)
            b = conv(xb_ref, wb_ref, bb_ref)
            return (a * _sigmoid(a) * b).astype(o_ref.dtype)

        def conv_rolled(x_ref, w_ref, b_ref):
            x = x_ref[...]
            return b_ref[...] + w_ref[0:1, :] * pltpu.roll(x, 2, 0) + w_ref[1:2, :] * pltpu.roll(x, 1, 0) + w_ref[2:3, :] * x

        def conv_head(x_ref, w_ref, b_ref):
            x = x_ref[0:head, :]
            return (b_ref[...] + w_ref[0:1, :] * _shift_down(x, 2, row) + w_ref[1:2, :] * _shift_down(x, 1, row)
                    + w_ref[2:3, :] * x)

        o_ref[...] = gated(conv_rolled)
        o_ref[0:head, :] = gated(conv_head)

    return pl.pallas_call(
        body, name="conv_fwd", grid=(B, nt),
        in_specs=[pl.BlockSpec((L, tc), lambda b, j: (b, j)), pl.BlockSpec((L, tc), lambda b, j: (b, j + nt)),
                  pl.BlockSpec((3, tc), lambda b, j: (0, j)), pl.BlockSpec((3, tc), lambda b, j: (0, j + nt)),
                  pl.BlockSpec((1, tc), lambda b, j: (0, j)), pl.BlockSpec((1, tc), lambda b, j: (0, j + nt))],
        out_specs=pl.BlockSpec((L, tc), lambda b, j: (b, j)),
        out_shape=jax.ShapeDtypeStruct((B * L, D_FF), bf16), compiler_params=_params("parallel", "parallel"),
    )(up, up, conv_w, conv_w, conv_b, conv_b)


CONV_ROWS = 2 * SUBLANES


def _rows16(i):
    return pl.ds(pl.multiple_of(i * CONV_ROWS, CONV_ROWS), CONV_ROWS)


def _conv_taps(x_ref, i, row):
    x = x_ref[_rows16(i), :]
    live = jnp.where(i > 0, 1.0, 0.0)
    r0 = jnp.maximum(i * CONV_ROWS, 2)
    p1 = x_ref[pl.ds(r0 - 1, 1), :] * live
    p2 = x_ref[pl.ds(r0 - 2, 1), :] * live
    x1 = jnp.where(row == 0, p1, pltpu.roll(x, 1, 0))
    x2 = jnp.where(row == 0, p2, jnp.where(row == 1, p1, pltpu.roll(x, 2, 0)))
    return x, x1, x2


def _conv_bwd(up, dff, conv_w, conv_b, B, L):
    tc = 256
    nt = D_FF // tc
    n = L // CONV_ROWS

    def body(xa_ref, xb_ref, d_ref, wa_ref, wb_ref, ba_ref, bb_ref, dup_ref, dw_ref, ga_ref, gb_ref):
        row = lax.broadcasted_iota(jnp.int32, (CONV_ROWS, tc), 0)

        @pl.when(pl.program_id(1) == 0)
        def _():
            dw_ref[...] = jnp.zeros_like(dw_ref)

        zero_tail = jnp.zeros((CONV_ROWS, tc), f32)
        ga_ref[L:L + CONV_ROWS, :] = zero_tail
        gb_ref[L:L + CONV_ROWS, :] = zero_tail

        def fold(v):
            return v[0:SUBLANES, :] + v[SUBLANES:CONV_ROWS, :]

        def step(i, acc):
            taps_a = _conv_taps(xa_ref, i, row)
            taps_b = _conv_taps(xb_ref, i, row)
            a = ba_ref[...] + wa_ref[0:1, :] * taps_a[2] + wa_ref[1:2, :] * taps_a[1] + wa_ref[2:3, :] * taps_a[0]
            b = bb_ref[...] + wb_ref[0:1, :] * taps_b[2] + wb_ref[1:2, :] * taps_b[1] + wb_ref[2:3, :] * taps_b[0]
            s = _sigmoid(a)
            d = d_ref[_rows16(i), :]
            g_a = d * b * s * (1.0 + a * (1.0 - s))
            g_b = d * a * s
            ga_ref[_rows16(i), :] = g_a
            gb_ref[_rows16(i), :] = g_b
            new = []
            for g, (x, x1, x2) in ((g_a, taps_a), (g_b, taps_b)):
                new += [fold(g * x2), fold(g * x1), fold(g * x), fold(g)]
            return tuple(o + v for o, v in zip(acc, new))

        z = jnp.zeros((SUBLANES, tc), f32)
        acc = _repeat_loop(n, step, (z,) * 8)
        for h in range(2):
            for t in range(4):
                dw_ref[h, t:t + 1, :] += jnp.sum(acc[4 * h + t], axis=0, keepdims=True)

        def back(i, c):
            for h, (g_ref, w_ref) in enumerate(((ga_ref, wa_ref), (gb_ref, wb_ref))):
                g = g_ref[_rows16(i), :]
                n1 = g_ref[pl.ds(i * CONV_ROWS + CONV_ROWS, 1), :]
                n2 = g_ref[pl.ds(i * CONV_ROWS + CONV_ROWS + 1, 1), :]
                u1 = jnp.where(row == CONV_ROWS - 1, n1, pltpu.roll(g, CONV_ROWS - 1, 0))
                u2 = jnp.where(row == CONV_ROWS - 1, n2, jnp.where(row == CONV_ROWS - 2, n1, pltpu.roll(g, CONV_ROWS - 2, 0)))
                dup_ref[h, _rows16(i), :] = (w_ref[2:3, :] * g + w_ref[1:2, :] * u1 + w_ref[0:1, :] * u2).astype(dup_ref.dtype)
            return c

        _repeat_loop(n, back, 0)

    return pl.pallas_call(
        body, name="conv_bwd", grid=(nt, B),
        in_specs=[pl.BlockSpec((L, tc), lambda j, b: (b, j)), pl.BlockSpec((L, tc), lambda j, b: (b, j + nt)),
                  pl.BlockSpec((L, tc), lambda j, b: (b, j)),
                  pl.BlockSpec((3, tc), lambda j, b: (0, j)), pl.BlockSpec((3, tc), lambda j, b: (0, j + nt)),
                  pl.BlockSpec((1, tc), lambda j, b: (0, j)), pl.BlockSpec((1, tc), lambda j, b: (0, j + nt))],
        out_specs=[pl.BlockSpec((2, L, tc), lambda j, b: (0, b, j)), pl.BlockSpec((2, SUBLANES, tc), lambda j, b: (0, 0, j))],
        out_shape=[jax.ShapeDtypeStruct((2, B * L, D_FF), bf16), jax.ShapeDtypeStruct((2, SUBLANES, D_FF), f32)],
        scratch_shapes=[pltpu.VMEM((L + CONV_ROWS, tc), f32), pltpu.VMEM((L + CONV_ROWS, tc), f32)],
        compiler_params=_params("parallel", "arbitrary"),
    )(up, up, dff, conv_w, conv_w, conv_b, conv_b)


GELU_C = math.sqrt(2.0 / math.pi)
GELU_A = 0.044715


def _gelu(x):
    return 0.5 * x * (1.0 + jnp.tanh(GELU_C * (x + GELU_A * x * x * x)))


def _gelu_grad(x):
    t = jnp.tanh(GELU_C * (x + GELU_A * x * x * x))
    return 0.5 * (1.0 + t) + 0.5 * x * (1.0 - t * t) * GELU_C * (1.0 + 3.0 * GELU_A * x * x)


def _cmul_add(xr, xi, ar, ai, sr, si):
    return xr + ar * sr - ai * si, xi + ar * si + ai * sr


def _s5_project_in(u_ref, bs_ref, s_ref, L, rc):
    for r in range(0, L, rc):
        s_ref[r:r + rc, :] = jnp.dot(u_ref[r:r + rc, :].astype(bf16), bs_ref[...], preferred_element_type=f32)


def _rows8(i):
    return pl.ds(pl.multiple_of(i * SUBLANES, SUBLANES), SUBLANES)


def _repeat_loop(n, step, init):
    rep = max(u for u in (6, 4, 3, 2, 1) if n % u == 0)

    def body(t, carry):
        for u in range(rep):
            carry = step(t * rep + u, carry)
        return carry

    return lax.fori_loop(0, n // rep, body, init)


def _to_segments(src_ref, dst_ref, seg):
    def step(i, c):
        dst_ref[_rows8(i), :] = src_ref[pl.ds(i, SUBLANES, stride=seg), :]
        return c

    _repeat_loop(seg, step, 0)


def _from_segments(src_ref, dst_ref, seg):
    def step(i, c):
        dst_ref[pl.ds(i, SUBLANES, stride=seg), :] = src_ref[_rows8(i), :]
        return c

    _repeat_loop(seg, step, 0)


def _half_tiles(j, seg, reverse):
    h = seg // 2
    return (_rows8(seg - 1 - j), _rows8(h - 1 - j)) if reverse else (_rows8(j), _rows8(j + h))


def _seg_local_scan(s_ref, ar, ai, seg, reverse):
    ns = SLAB_NS

    def step(j, carry):
        tiles = _half_tiles(j, seg, reverse)
        loaded = [(s_ref[rows, 0:ns], s_ref[rows, ns:2 * ns]) for rows in tiles]
        out = []
        for (xr, xi), (cr, ci) in zip(loaded, (carry[0:2], carry[2:4])):
            out += list(_cmul_add(xr, xi, ar, ai, cr, ci))
        for rows, cr, ci in zip(tiles, out[0::2], out[1::2]):
            s_ref[rows, 0:ns] = cr
            s_ref[rows, ns:2 * ns] = ci
        return tuple(out)

    z = jnp.zeros((SUBLANES, ns), f32)
    return _repeat_loop(seg // 2, step, (z, z, z, z))


def _seg_boundaries(finals, ahr, ahi, reverse):
    fxr, fxi, fyr, fyi = finals
    row = lax.broadcasted_iota(jnp.int32, fxr.shape, 0)
    zero = jnp.zeros_like(fxr[0:1, :])
    xr, xi, yr, yi = (jnp.zeros_like(fxr) for _ in range(4))
    prev = None
    for r in (range(SUBLANES - 1, -1, -1) if reverse else range(SUBLANES)):
        if prev is None:
            nxr, nxi = zero, zero
        else:
            nxr, nxi = _cmul_add(fyr[prev:prev + 1, :], fyi[prev:prev + 1, :], ahr, ahi, nyr, nyi)
        nyr, nyi = _cmul_add(fxr[r:r + 1, :], fxi[r:r + 1, :], ahr, ahi, nxr, nxi)
        xr, xi = jnp.where(row == r, nxr, xr), jnp.where(row == r, nxi, xi)
        yr, yi = jnp.where(row == r, nyr, yr), jnp.where(row == r, nyi, yi)
        prev = r
    return (xr, xi), (yr, yi)


def _s5_states(u_ref, bs_ref, pw_ref, up_ref, s_ref, L, rc):
    seg = L // SUBLANES
    h = seg // 2
    ns = SLAB_NS
    _to_segments(u_ref, up_ref, seg)
    _s5_project_in(up_ref, bs_ref, s_ref, L, rc)
    ar, ai = pw_ref[0, 0:1, :], pw_ref[1, 0:1, :]
    finals = _seg_local_scan(s_ref, ar, ai, seg, False)
    enter = _seg_boundaries(finals, pw_ref[0, h - 1:h, :], pw_ref[1, h - 1:h, :], False)

    def fix(j, c):
        pr, pi = pw_ref[0, pl.ds(j, 1), :], pw_ref[1, pl.ds(j, 1), :]
        tiles = _half_tiles(j, seg, False)
        loaded = [(s_ref[rows, 0:ns], s_ref[rows, ns:2 * ns]) for rows in tiles]
        for rows, (xr, xi), (br, bi) in zip(tiles, loaded, enter):
            xr, xi = _cmul_add(xr, xi, pr, pi, br, bi)
            s_ref[rows, 0:ns] = xr
            s_ref[rows, ns:2 * ns] = xi
        return c

    _repeat_loop(h, fix, 0)


def _pw_spec(seg_rows, order):
    if order == "bs":
        return pl.BlockSpec((2, seg_rows, SLAB_NS), lambda b, s: (0, 0, s))
    return pl.BlockSpec((2, seg_rows, SLAB_NS), lambda s, b: (0, 0, s))


def _s5_fwd(p, bs, cs, pw, d_skip, B, L):
    rc = _tile(L, 344)
    seg = L // SUBLANES

    def body(u_ref, bs_ref, cs_ref, pw_ref, d_ref, y_ref, s_ref, up_ref, yp_ref):
        _s5_states(u_ref, bs_ref, pw_ref, up_ref, s_ref, L, rc)
        for r in range(0, L, rc):
            ypre = (jnp.dot(s_ref[r:r + rc, :].astype(bf16), cs_ref[...], preferred_element_type=f32)
                    + d_ref[...] * up_ref[r:r + rc, :])
            yp_ref[r:r + rc, :] = _gelu(ypre)
        _from_segments(yp_ref, y_ref, seg)

    ucol = SEG_U * (D_MODEL // SLAB_CH)
    return pl.pallas_call(
        body, name="s5_fwd", grid=(B, N_SLAB),
        in_specs=[pl.BlockSpec((L, SLAB_CH), lambda b, s: (b, ucol + s)),
                  pl.BlockSpec((None, SLAB_CH, 2 * SLAB_NS), lambda b, s: (s, 0, 0)),
                  pl.BlockSpec((None, 2 * SLAB_NS, SLAB_CH), lambda b, s: (s, 0, 0)),
                  _pw_spec(pw.shape[1], "bs"),
                  pl.BlockSpec((1, SLAB_CH), lambda b, s: (0, s))],
        out_specs=pl.BlockSpec((L, SLAB_CH), lambda b, s: (b, s)),
        out_shape=jax.ShapeDtypeStruct((B * L, D_MODEL), f32),
        scratch_shapes=[pltpu.VMEM((L, 2 * SLAB_NS), f32), pltpu.VMEM((L, SLAB_CH), f32), pltpu.VMEM((L, SLAB_CH), f32)],
        compiler_params=_params("parallel", "parallel"),
    )(p, bs, cs, pw, d_skip)


def _s5_bwd(p, dya0, dp, bs, cs, pw, d_skip, B, L, sums):
    rc = _tile(L, 344)
    ns = SLAB_NS
    seg = L // SUBLANES
    nx = len(sums)

    def body(u_ref, dy_ref, dp_in, bs_ref, cs_ref, pw_ref, d_ref, *rest):
        xin, (du_ref, dbs_ref, dcs_ref, da_ref, dd_ref), xout = rest[:nx], rest[nx:nx + 5], rest[nx + 5:2 * nx + 5]
        s_ref, lam_ref, up_ref, dyp_ref, nat_ref, send, recv = rest[2 * nx + 5:]
        del dp_in
        start, finish = _chip_exchange_steps(xin, xout, send, recv)

        @pl.when((pl.program_id(0) == 0) & (pl.program_id(1) == 0))
        def _():
            start()

        @pl.when(pl.program_id(1) == 0)
        def _():
            dbs_ref[...] = jnp.zeros_like(dbs_ref)
            dcs_ref[...] = jnp.zeros_like(dcs_ref)
            da_ref[...] = jnp.zeros_like(da_ref)
            dd_ref[...] = jnp.zeros_like(dd_ref)

        _s5_states(u_ref, bs_ref, pw_ref, up_ref, s_ref, L, rc)
        _to_segments(dy_ref, dyp_ref, seg)
        for r in range(0, L, rc):
            u = up_ref[r:r + rc, :]
            sb = s_ref[r:r + rc, :].astype(bf16)
            ypre = jnp.dot(sb, cs_ref[...], preferred_element_type=f32) + d_ref[...] * u
            dyp = dyp_ref[r:r + rc, :] * _gelu_grad(ypre)
            dyp_ref[r:r + rc, :] = dyp
            dd_ref[...] += jnp.sum(dyp * u, axis=0, keepdims=True)
            dypb = dyp.astype(bf16)
            dcs_ref[...] += lax.dot_general(sb, dypb, _DIMS["tn"], preferred_element_type=f32)
            lam_ref[r:r + rc, :] = lax.dot_general(dypb, cs_ref[...], _DIMS["nt"], preferred_element_type=f32)

        h = seg // 2
        ar, ai = pw_ref[0, 0:1, :], -pw_ref[1, 0:1, :]
        finals = _seg_local_scan(lam_ref, ar, ai, seg, True)
        enter = _seg_boundaries(finals, pw_ref[0, h - 1:h, :], -pw_ref[1, h - 1:h, :], True)

        def fix(j, acc):
            accr, acci = acc
            pr, pi = pw_ref[0, pl.ds(j, 1), :], -pw_ref[1, pl.ds(j, 1), :]
            tiles = _half_tiles(j, seg, True)
            loaded = [(lam_ref[rows, 0:ns], lam_ref[rows, ns:2 * ns]) for rows in tiles]
            for rows, (xr, xi), (br, bi), t in zip(tiles, loaded, enter, (seg - 1 - j, h - 1 - j)):
                xr, xi = _cmul_add(xr, xi, pr, pi, br, bi)
                lam_ref[rows, 0:ns] = xr
                lam_ref[rows, ns:2 * ns] = xi
                prev = _rows8(jnp.maximum(t - 1, 0))
                live = jnp.where(t > 0, 1.0, 0.0)
                spr = s_ref[prev, 0:ns] * live
                spi = s_ref[prev, ns:2 * ns] * live
                accr, acci = accr + xr * spr + xi * spi, acci + xi * spr - xr * spi
            return accr, acci

        z = jnp.zeros((SUBLANES, ns), f32)
        accr, acci = _repeat_loop(h, fix, (z, z))
        row = lax.broadcasted_iota(jnp.int32, (SUBLANES, ns), 0)
        last = _rows8(seg - 1)
        spr = jnp.where(row == 0, 0.0, pltpu.roll(s_ref[last, 0:ns], 1, 0))
        spi = jnp.where(row == 0, 0.0, pltpu.roll(s_ref[last, ns:2 * ns], 1, 0))
        xr, xi = lam_ref[0:SUBLANES, 0:ns], lam_ref[0:SUBLANES, ns:2 * ns]
        accr = accr + xr * spr + xi * spi
        acci = acci + xi * spr - xr * spi
        da_ref[0:1, :] += jnp.sum(accr, axis=0, keepdims=True)
        da_ref[1:2, :] += jnp.sum(acci, axis=0, keepdims=True)

        for r in range(0, L, rc):
            lamb = lam_ref[r:r + rc, :].astype(bf16)
            dbs_ref[...] += lax.dot_general(up_ref[r:r + rc, :].astype(bf16), lamb, _DIMS["tn"], preferred_element_type=f32)
            nat_ref[r:r + rc, :] = (lax.dot_general(lamb, bs_ref[...], _DIMS["nt"], preferred_element_type=f32)
                                    + d_ref[...] * dyp_ref[r:r + rc, :])
        _from_segments(nat_ref, up_ref, seg)
        du_ref[...] = up_ref[...].astype(du_ref.dtype)

        @pl.when((pl.program_id(0) == N_SLAB - 1) & (pl.program_id(1) == B - 1))
        def _():
            finish()

    ucol = SEG_U * (D_MODEL // SLAB_CH)
    T = B * L
    col = pltpu.VMEM((L, SLAB_CH), f32)
    res = pl.pallas_call(
        body, name="s5_bwd", grid=(N_SLAB, B),
        in_specs=[pl.BlockSpec((L, SLAB_CH), lambda s, b: (b, ucol + s)),
                  pl.BlockSpec((L, SLAB_CH), lambda s, b: (b, s)),
                  ANY,
                  pl.BlockSpec((None, SLAB_CH, 2 * SLAB_NS), lambda s, b: (s, 0, 0)),
                  pl.BlockSpec((None, 2 * SLAB_NS, SLAB_CH), lambda s, b: (s, 0, 0)),
                  _pw_spec(pw.shape[1], "sb"),
                  pl.BlockSpec((1, SLAB_CH), lambda s, b: (0, s))] + [ANY] * nx,
        out_specs=[pl.BlockSpec((None, L, SLAB_CH), lambda s, b: (SEG_U, b, s)),
                   pl.BlockSpec((None, SLAB_CH, 2 * SLAB_NS), lambda s, b: (s, 0, 0)),
                   pl.BlockSpec((None, 2 * SLAB_NS, SLAB_CH), lambda s, b: (s, 0, 0)),
                   pl.BlockSpec((None, 2, SLAB_NS), lambda s, b: (s, 0, 0)),
                   pl.BlockSpec((1, SLAB_CH), lambda s, b: (0, s))] + [ANY] * nx,
        out_shape=[jax.ShapeDtypeStruct((N_SEG, T, D_MODEL), bf16),
                   jax.ShapeDtypeStruct((N_SLAB, SLAB_CH, 2 * SLAB_NS), f32),
                   jax.ShapeDtypeStruct((N_SLAB, 2 * SLAB_NS, SLAB_CH), f32),
                   jax.ShapeDtypeStruct((N_SLAB, 2, SLAB_NS), f32),
                   jax.ShapeDtypeStruct((1, D_MODEL), f32)] + [jax.ShapeDtypeStruct(a.shape, a.dtype) for a in sums],
        scratch_shapes=[pltpu.VMEM((L, 2 * SLAB_NS), f32), pltpu.VMEM((L, 2 * SLAB_NS), f32), col, col, col]
        + _chip_exchange_sems(nx),
        input_output_aliases={2: 0},
        compiler_params=_params("arbitrary", "arbitrary"),
    )(p, dya0, dp, bs, cs, pw, d_skip, *sums)
    return res[:5], res[5:]


def _dotb(a, b, dims="nn"):
    return lax.dot_general(a.astype(bf16), b.astype(bf16), _DIMS[dims], preferred_element_type=f32)


def _tile_scan(x, reverse):
    n, w = x.shape
    v = x.reshape(n // SUBLANES, SUBLANES, w)
    row = lax.broadcasted_iota(jnp.int32, v.shape, 1)
    for k in (1, 2, 4):
        if reverse:
            v = v + jnp.where(row < SUBLANES - k, pltpu.roll(v, SUBLANES - k, 1), 0.0)
        else:
            v = v + jnp.where(row >= k, pltpu.roll(v, k, 1), 0.0)
    p = v.reshape(n // CHUNK, 2, SUBLANES, w)
    lo, hi = p[:, 0], p[:, 1]
    if reverse:
        lo = lo + hi[:, 0:1, :]
    else:
        hi = hi + lo[:, SUBLANES - 1:SUBLANES, :]
    return jnp.stack([lo, hi], axis=1).reshape(n, w)


def _chunk_cumsum(x):
    return _tile_scan(x, False)


def _chunk_rev_cumsum(x):
    return _tile_scan(x, True)


def _chunk_last(x):
    n, w = x.shape
    p = x.reshape(n // CHUNK, CHUNK, w)
    return jnp.broadcast_to(p[:, CHUNK - 1:CHUNK, :], p.shape).reshape(n, w)


def _hgrn_local(q, fl, lb):
    sg = _sigmoid(fl)
    f = lb + (1.0 - lb) * sg
    g = jnp.log(f)
    cum = _chunk_cumsum(g)
    rest = _chunk_last(cum) - cum
    e = jnp.exp(cum)
    em = jnp.exp(-cum)
    eo = jnp.exp(rest)
    k = 1.0 - f
    return sg, f, e, em, eo, q * e, k * em, k * eo, cum + rest


def _chunk_pos(n):
    return lax.broadcasted_iota(jnp.int32, (n, HEAD_DIM), 0) & (CHUNK - 1)


def _hgrn_block_rows(L):
    return _tile(L, 688, CHUNK)


def _hgrn_specs(L, order):
    hb = D_MODEL // HEAD_DIM

    def spec(seg):
        if order == "bh":
            return pl.BlockSpec((L, HEAD_DIM), lambda b, h: (b, seg * hb + h))
        return pl.BlockSpec((L, HEAD_DIM), lambda h, b: (b, seg * hb + h))

    return [spec(SEG_Q), spec(SEG_F), spec(SEG_I), spec(SEG_OG)]


PAIR = 2 * CHUNK
CHUNK_SHIFT = CHUNK.bit_length() - 1


def _pair_steps(L, rb):
    steps = []
    nch = rb // CHUNK
    for r in range(0, L, rb):
        steps += [(r + p * PAIR, PAIR) for p in range(nch // 2)]
        if nch % 2:
            steps.append((r + (nch - 1) * CHUNK, CHUNK))
    return steps


def _pair_flags(rb):
    ci = lax.broadcasted_iota(jnp.int32, (rb, HEAD_DIM), 0) >> CHUNK_SHIFT
    odd = (ci & 1) == 1
    has_next = jnp.logical_and(jnp.logical_not(odd), ci < rb // CHUNK - 1)
    return odd, has_next


def _pair_masks(rb):
    r = lax.broadcasted_iota(jnp.int32, (rb, rb), 0)
    c = lax.broadcasted_iota(jnp.int32, (rb, rb), 1)
    rc, cc = r >> CHUNK_SHIFT, c >> CHUNK_SHIFT
    same = (rc == cc) & (c <= r)
    prev = ((rc & 1) == 1) & (cc == rc - 1)
    return same, prev


def _hgrn_pair_local(q, fl, lb, odd, has_next):
    sg, f, e, em, eo, qt, kt, ko, cend = _hgrn_local(q, fl, lb)
    n = q.shape[0]
    a = jnp.where(odd, pltpu.roll(cend, CHUNK, 0), 0.0)
    z = jnp.where(has_next, pltpu.roll(cend, n - CHUNK, 0), 0.0)
    ea, ez = jnp.exp(a), jnp.exp(z)
    return dict(sg=sg, f=f, e=e, em=em, eo=eo, qt=qt, kt=kt, ko=ko, ea=ea, ez=ez, qs=qt * ea, ks=ko * ez,
                decp=jnp.exp(cend + a + z))


def _pair_scores(qt, kt, ko, same, prev):
    return (jnp.where(same, _dotb(qt, kt, "nt"), 0.0) + jnp.where(prev, _dotb(qt, ko, "nt"), 0.0)).astype(bf16)


def _hgrn_fwd(p, lb, norm_g, B, L):
    rb = _hgrn_block_rows(L)
    steps = _pair_steps(L, rb)
    blocks = [slice(r, r + rb) for r in range(0, L, rb)]

    def body(q_ref, f_ref, v_ref, og_ref, lb_ref, ng_ref, y_ref, qs_s, ks_s, vb_s, decp_s, o_s, o2_s, u_s, sb_s):
        lbv = lb_ref[...]
        ngv = ng_ref[...]
        same, prev = _pair_masks(rb)
        odd, has_next = _pair_flags(rb)

        for rows in blocks:
            t = _hgrn_pair_local(q_ref[rows, :], f_ref[rows, :], lbv, odd, has_next)
            vb = v_ref[rows, :].astype(bf16)
            o_s[rows, :] = _dotb(_pair_scores(t["qt"], t["kt"], t["ko"], same, prev), vb)
            qs_s[rows, :] = t["qs"].astype(bf16)
            ks_s[rows, :] = t["ks"].astype(bf16)
            vb_s[rows, :] = vb
            decp_s[rows, :] = t["decp"]

        for n, (r0, nr) in enumerate(steps):
            u_s[n] = _dotb(vb_s[r0:r0 + nr, :], ks_s[r0:r0 + nr, :], "tn")
        st = jnp.zeros((HEAD_DIM, HEAD_DIM), f32)
        for n, (r0, nr) in enumerate(steps):
            sb_s[n] = st.astype(bf16)
            st = st * decp_s[r0:r0 + 1, :] + u_s[n]
        for n, (r0, nr) in enumerate(steps):
            o2_s[r0:r0 + nr, :] = _dotb(qs_s[r0:r0 + nr, :], sb_s[n], "nt")

        for rows in blocks:
            o = o_s[rows, :] + o2_s[rows, :]
            og = og_ref[rows, :]
            on = o * lax.rsqrt(jnp.mean(o * o, axis=-1, keepdims=True) + EPS) * ngv
            y_ref[rows, :] = (on * og * _sigmoid(og)).astype(y_ref.dtype)

    sb = pltpu.VMEM((L, HEAD_DIM), bf16)
    sf = pltpu.VMEM((L, HEAD_DIM), f32)
    return pl.pallas_call(
        body, name="hgrn_fwd", grid=(B, HEADS),
        in_specs=_hgrn_specs(L, "bh") + [pl.BlockSpec((1, HEAD_DIM), lambda b, h: (0, h)),
                                          pl.BlockSpec((1, HEAD_DIM), lambda b, h: (0, 0))],
        out_specs=pl.BlockSpec((L, HEAD_DIM), lambda b, h: (b, h)),
        out_shape=jax.ShapeDtypeStruct((B * L, D_MODEL), bf16),
        scratch_shapes=[sb, sb, sb, sf, sf, sf, pltpu.VMEM((len(steps), HEAD_DIM, HEAD_DIM), f32),
                        pltpu.VMEM((len(steps), HEAD_DIM, HEAD_DIM), bf16)],
        compiler_params=_params("parallel", "parallel"),
    )(p, p, p, p, lb, norm_g)


def _hgrn_bwd(p, dyb, dp, lb, norm_g, B, L):
    rb = _hgrn_block_rows(L)
    steps = _pair_steps(L, rb)
    blocks = [slice(r, r + rb) for r in range(0, L, rb)]

    def body(q_ref, f_ref, v_ref, og_ref, dy_ref, dp_in, lb_ref, ng_ref, dseg_ref, dlb_ref, dng_ref,
             st_ref, u_s, dsb_s, qt_s, kt_s, ko_s, qs_s, ks_s, vb_s, do_s,
             decp_s, o_s, o2_s, dqt_s, dkt_s, dko_s, dv_s, dv2_s, dqs_s, dks_s, ddecp_s):
        del dp_in
        lbv = lb_ref[...]
        ngv = ng_ref[...]
        same, prev = _pair_masks(rb)
        odd, has_next = _pair_flags(rb)
        pos = _chunk_pos(rb)

        @pl.when(pl.program_id(1) == 0)
        def _():
            dlb_ref[...] = jnp.zeros_like(dlb_ref)

        @pl.when((pl.program_id(0) == 0) & (pl.program_id(1) == 0))
        def _():
            dng_ref[...] = jnp.zeros_like(dng_ref)

        def scores(rows):
            return _pair_scores(qt_s[rows, :], kt_s[rows, :], ko_s[rows, :], same, prev)

        for rows in blocks:
            t = _hgrn_pair_local(q_ref[rows, :], f_ref[rows, :], lbv, odd, has_next)
            for dst, key in ((qt_s, "qt"), (kt_s, "kt"), (ko_s, "ko"), (qs_s, "qs"), (ks_s, "ks")):
                dst[rows, :] = t[key].astype(bf16)
            vb_s[rows, :] = v_ref[rows, :].astype(bf16)
            decp_s[rows, :] = t["decp"]
            o_s[rows, :] = _dotb(scores(rows), vb_s[rows, :])

        for n, (r0, nr) in enumerate(steps):
            u_s[n] = _dotb(vb_s[r0:r0 + nr, :], ks_s[r0:r0 + nr, :], "tn")
        st = jnp.zeros((HEAD_DIM, HEAD_DIM), f32)
        for n, (r0, nr) in enumerate(steps):
            st_ref[n] = st
            st = st * decp_s[r0:r0 + 1, :] + u_s[n]
        for n, (r0, nr) in enumerate(steps):
            o2_s[r0:r0 + nr, :] = _dotb(qs_s[r0:r0 + nr, :], st_ref[n], "nt")

        dng = jnp.zeros((1, HEAD_DIM), f32)
        for rows in blocks:
            o = o_s[rows, :] + o2_s[rows, :]
            og = og_ref[rows, :]
            dy = dy_ref[rows, :]
            rs = lax.rsqrt(jnp.mean(o * o, axis=-1, keepdims=True) + EPS)
            xn = o * rs
            so = _sigmoid(og)
            dseg_ref[SEG_OG, rows, :] = (dy * xn * ngv * so * (1.0 + og * (1.0 - so))).astype(dseg_ref.dtype)
            don = dy * og * so
            dng = dng + jnp.sum(don * xn, axis=0, keepdims=True)
            dxo = don * ngv
            do = (rs * (dxo - xn * jnp.mean(dxo * xn, axis=-1, keepdims=True))).astype(bf16)
            do_s[rows, :] = do
            dpf = _dotb(do, vb_s[rows, :], "nt")
            dp1 = jnp.where(same, dpf, 0.0).astype(bf16)
            dp2 = jnp.where(prev, dpf, 0.0).astype(bf16)
            dqt_s[rows, :] = _dotb(dp1, kt_s[rows, :]) + _dotb(dp2, ko_s[rows, :])
            dkt_s[rows, :] = _dotb(dp1, qt_s[rows, :], "tn")
            dko_s[rows, :] = _dotb(dp2, qt_s[rows, :], "tn")
            dv_s[rows, :] = _dotb(scores(rows), do, "tn")
        dng_ref[...] += dng

        for n, (r0, nr) in enumerate(steps):
            u_s[n] = _dotb(do_s[r0:r0 + nr, :], qs_s[r0:r0 + nr, :], "tn")
        dst = jnp.zeros((HEAD_DIM, HEAD_DIM), f32)
        for n, (r0, nr) in reversed(list(enumerate(steps))):
            dsb_s[n] = dst.astype(bf16)
            ddecp_s[r0:r0 + nr, :] = jnp.broadcast_to(jnp.sum(dst * st_ref[n], axis=0, keepdims=True), (nr, HEAD_DIM))
            dst = dst * decp_s[r0:r0 + 1, :] + u_s[n]
        for n, (r0, nr) in enumerate(steps):
            rows = slice(r0, r0 + nr)
            dqs_s[rows, :] = _dotb(do_s[rows, :], st_ref[n])
            dv2_s[rows, :] = _dotb(ks_s[rows, :], dsb_s[n], "nt")
            dks_s[rows, :] = _dotb(vb_s[rows, :], dsb_s[n])

        def chunk_sum(x):
            return _chunk_last(_chunk_cumsum(x))

        dlb = jnp.zeros((1, HEAD_DIM), f32)
        for rows in blocks:
            t = _hgrn_pair_local(q_ref[rows, :], f_ref[rows, :], lbv, odd, has_next)
            dqs, dks = dqs_s[rows, :], dks_s[rows, :]
            dqt = dqt_s[rows, :] + dqs * t["ea"]
            dko = dko_s[rows, :] + dks * t["ez"]
            dkt = dkt_s[rows, :]
            dko_ko = dko * t["ko"]
            dcum = dqt * t["qt"] - dkt * t["kt"] - dko_ko
            from_next = pltpu.roll(chunk_sum(jnp.where(odd, dqs * t["qs"], 0.0)), rb - CHUNK, 0)
            from_prev = pltpu.roll(chunk_sum(jnp.where(has_next, dks * t["ks"], 0.0)), CHUNK, 0)
            d_end = (chunk_sum(dko_ko) + jnp.where(has_next, from_next, 0.0) + jnp.where(odd, from_prev, 0.0)
                     + ddecp_s[rows, :] * t["decp"])
            dcum = dcum + jnp.where(pos == CHUNK - 1, d_end, 0.0)
            df = _chunk_rev_cumsum(dcum) / t["f"] - (dkt * t["em"] + dko * t["eo"])
            dlb = dlb + jnp.sum(df * (1.0 - t["sg"]), axis=0, keepdims=True)
            dseg_ref[SEG_Q, rows, :] = (dqt * t["e"]).astype(dseg_ref.dtype)
            dseg_ref[SEG_F, rows, :] = (df * (1.0 - lbv) * t["sg"] * (1.0 - t["sg"])).astype(dseg_ref.dtype)
            dseg_ref[SEG_I, rows, :] = (dv_s[rows, :] + dv2_s[rows, :]).astype(dseg_ref.dtype)
        dlb_ref[...] += dlb

    T = B * L
    ns = len(steps)
    sb = pltpu.VMEM((L, HEAD_DIM), bf16)
    sf = pltpu.VMEM((L, HEAD_DIM), f32)
    return pl.pallas_call(
        body, name="hgrn_bwd", grid=(HEADS, B),
        in_specs=_hgrn_specs(L, "hb") + [pl.BlockSpec((L, HEAD_DIM), lambda h, b: (b, h)), ANY,
                                          pl.BlockSpec((1, HEAD_DIM), lambda h, b: (0, h)),
                                          pl.BlockSpec((1, HEAD_DIM), lambda h, b: (0, 0))],
        out_specs=[pl.BlockSpec((4, L, HEAD_DIM), lambda h, b: (0, b, h)),
                   pl.BlockSpec((1, HEAD_DIM), lambda h, b: (0, h)),
                   pl.BlockSpec((1, HEAD_DIM), lambda h, b: (0, 0))],
        out_shape=[jax.ShapeDtypeStruct((N_SEG, T, D_MODEL), bf16), jax.ShapeDtypeStruct((1, D_MODEL), f32),
                   jax.ShapeDtypeStruct((1, HEAD_DIM), f32)],
        scratch_shapes=[pltpu.VMEM((ns, HEAD_DIM, HEAD_DIM), f32), pltpu.VMEM((ns, HEAD_DIM, HEAD_DIM), f32),
                        pltpu.VMEM((ns, HEAD_DIM, HEAD_DIM), bf16)] + [sb] * 7 + [sf] * 11,
        input_output_aliases={5: 0},
        compiler_params=_params("arbitrary", "arbitrary"),
    )(p, p, p, p, dyb, dp, lb, norm_g)


def _dz1_norm(dp, w_in_phys, h0, g, dh1):
    _, T, Dm = dp.shape
    tm = _tile(T, 688)
    return _mm_rmsnorm_bwd("dz1", dp, w_in_phys, (T // tm, 1, N_SEG),
                           pl.BlockSpec((None, tm, Dm), lambda i, j, k: (k, i, 0)),
                           pl.BlockSpec((Dm, Dm), lambda i, j, k: (0, k)), h0, g, dh1)


def _dz2_norm(dup, w_up, h1, g, dh2):
    _, T, _ = dup.shape
    tm = _tile(T, 688)
    tk = D_FF // 2
    return _mm_rmsnorm_bwd("dz2", dup, w_up, (T // tm, 1, 4),
                           pl.BlockSpec((None, tm, tk), lambda i, j, k: (k // 2, i, k % 2)),
                           pl.BlockSpec((D_MODEL, tk), lambda i, j, k: (0, k)), h1, g, dh2)


def _dw_in(z1, dp):
    _, T, Dm = dp.shape
    tk = _tile(T, 1376)
    return _mm("dw_in", z1, dp, "tn", (1, N_SEG, T // tk),
               pl.BlockSpec((tk, Dm), lambda i, j, k: (k, 0)),
               pl.BlockSpec((None, tk, Dm), lambda i, j, k: (j, k, 0)),
               jax.ShapeDtypeStruct((N_SEG, Dm, Dm), f32),
               pl.BlockSpec((None, Dm, Dm), lambda i, j, k: (j, 0, 0)), (Dm, Dm))


def _dw_up(z2, dup):
    _, T, _ = dup.shape
    tn = D_FF // 2
    tk = _tile(T, 688)
    return _mm("dw_up", z2, dup, "tn", (1, N_CHIPS, T // tk),
               pl.BlockSpec((tk, D_MODEL), lambda i, j, k: (k, 0)),
               pl.BlockSpec((None, tk, tn), lambda i, j, k: (j // 2, k, j % 2)),
               jax.ShapeDtypeStruct((N_CHIPS, D_MODEL, tn), f32),
               pl.BlockSpec((None, D_MODEL, tn), lambda i, j, k: (j, 0, 0)), (D_MODEL, tn))


def _place():
    x, y, c = lax.axis_index("x"), lax.axis_index("y"), lax.axis_index("c")
    chips = [(1 - x, y), (x, 1 - y), (1 - x, 1 - y)]
    return x, y, c, chips


def _allgather_chips(arrs):
    n = len(arrs)

    def body(*refs):
        ins, outs = refs[:n], refs[n:2 * n]
        send, recv, local = refs[2 * n:]
        x, y, c, chips = _place()
        me = 2 * x + y

        def copy(a, k, slot):
            px, py = chips[k]
            return pltpu.make_async_remote_copy(src_ref=ins[a], dst_ref=outs[a].at[slot], send_sem=send.at[3 * a + k],
                                                recv_sem=recv.at[3 * a + k], device_id=(px, py, c), device_id_type=MESH)

        for a in range(n):
            pltpu.make_async_copy(ins[a], outs[a].at[me], local.at[a]).start()
            for k in range(3):
                copy(a, k, me).start()
        for a in range(n):
            for k, (px, py) in enumerate(chips):
                copy(a, k, 2 * px + py).wait_recv()
        for a in range(n):
            pltpu.make_async_copy(ins[a], outs[a].at[me], local.at[a]).wait()
            for k in range(3):
                copy(a, k, me).wait_send()

    return pl.pallas_call(
        body, name="allgather_chips", in_specs=[ANY] * n, out_specs=[ANY] * n,
        out_shape=[jax.ShapeDtypeStruct((N_CHIPS,) + a.shape, a.dtype) for a in arrs],
        scratch_shapes=[pltpu.SemaphoreType.DMA((3 * n,)), pltpu.SemaphoreType.DMA((3 * n,)), pltpu.SemaphoreType.DMA((n,))],
    )(*arrs)


def _allgather_split(arrs):
    n = len(arrs)

    def body(*refs):
        start, finish = _gather_split_steps(refs[:n], refs[n:2 * n], *refs[2 * n:])
        start()
        finish()

    return pl.pallas_call(
        body, name="allgather_split", in_specs=[ANY] * n, out_specs=[ANY] * n,
        out_shape=[jax.ShapeDtypeStruct((N_CHIPS,) + a.shape, a.dtype) for a in arrs],
        scratch_shapes=_gather_split_sems(n),
    )(*arrs)


def _gather_split_sems(n):
    return [pltpu.SemaphoreType.DMA((3 * n,)) for _ in range(4)]


def _gather_split_steps(ins, outs, send, recv, fsend, frecv):
    n = len(ins)

    def place():
        x, y, c, chips = _place()
        return x, y, c, chips, 2 * x + y

    def half(a, core):
        rh = ins[a].shape[0] // 2
        return pl.ds(core * rh, rh)

    def copy(a, k, slot):
        x, y, c, chips, _ = place()
        px, py = chips[k]
        return pltpu.make_async_remote_copy(src_ref=ins[a].at[half(a, c), :], dst_ref=outs[a].at[slot, half(a, c), :],
                                            send_sem=send.at[3 * a + k], recv_sem=recv.at[3 * a + k],
                                            device_id=(px, py, c), device_id_type=MESH)

    def forward(a, k, core):
        x, y, c, chips, _ = place()
        px, py = chips[k]
        rows = outs[a].at[2 * px + py, half(a, core), :]
        return pltpu.make_async_remote_copy(src_ref=rows, dst_ref=rows, send_sem=fsend.at[3 * a + k],
                                            recv_sem=frecv.at[3 * a + k], device_id=(x, y, 1 - c), device_id_type=MESH)

    def start():
        me = place()[4]
        for a in range(n):
            for k in range(3):
                copy(a, k, me).start()

    def finish():
        x, y, c, chips, me = place()
        for a in range(n):
            for k, (px, py) in enumerate(chips):
                copy(a, k, 2 * px + py).wait_recv()
                forward(a, k, c).start()
        for a in range(n):
            for k in range(3):
                forward(a, k, 1 - c).wait_recv()
        for a in range(n):
            for k in range(3):
                copy(a, k, me).wait_send()
                forward(a, k, c).wait_send()

    return start, finish


def _in_proj_gather(z1, w_in, shards):
    n = len(shards)
    T, K = z1.shape
    N = w_in.shape[1]
    tm = _tile(T, 1032)
    tn = 1024
    grid = (T // tm, N // tn)

    def body(a_ref, b_ref, *rest):
        ins, o_ref, outs, sems = rest[:n], rest[n], rest[n + 1:2 * n + 1], rest[2 * n + 1:]
        start, finish = _gather_split_steps(ins, outs, *sems)
        i, j = pl.program_id(0), pl.program_id(1)

        @pl.when((i == 0) & (j == 0))
        def _():
            start()

        o_ref[...] = jnp.dot(a_ref[...], b_ref[...], preferred_element_type=f32)

        @pl.when((i == grid[0] - 1) & (j == grid[1] - 1))
        def _():
            finish()

    res = pl.pallas_call(
        body, name="in_proj", grid=grid,
        in_specs=[pl.BlockSpec((tm, K), lambda i, j: (i, 0)), pl.BlockSpec((K, tn), lambda i, j: (0, j))] + [ANY] * n,
        out_specs=[pl.BlockSpec((tm, tn), lambda i, j: (i, j))] + [ANY] * n,
        out_shape=[jax.ShapeDtypeStruct((T, N), f32)] + [jax.ShapeDtypeStruct((N_CHIPS,) + a.shape, a.dtype) for a in shards],
        scratch_shapes=_gather_split_sems(n),
        compiler_params=_params("arbitrary", "arbitrary"),
    )(z1, w_in, *shards)
    return res[0], res[1:]


def _sibling_halves(parts, name="sibling_halves"):
    n = len(parts)

    def body(*refs):
        ins, outs = refs[:n], refs[n:2 * n]
        send, recv = refs[2 * n:]
        x, y, c, _ = _place()

        def copy(a):
            rh = ins[a].shape[1] // 2
            return pltpu.make_async_remote_copy(src_ref=ins[a].at[:, pl.ds((1 - c) * rh, rh), :], dst_ref=outs[a],
                                                send_sem=send.at[a], recv_sem=recv.at[a], device_id=(x, y, 1 - c),
                                                device_id_type=MESH)

        for a in range(n):
            copy(a).start()
        for a in range(n):
            copy(a).wait_recv()
        for a in range(n):
            copy(a).wait_send()

    return pl.pallas_call(
        body, name=name, in_specs=[ANY] * n, out_specs=[ANY] * n,
        out_shape=[jax.ShapeDtypeStruct((a.shape[0], a.shape[1] // 2, a.shape[2]), a.dtype) for a in parts],
        scratch_shapes=[pltpu.SemaphoreType.DMA((n,)), pltpu.SemaphoreType.DMA((n,))],
    )(*parts)


def _add_own_half(name, part, got, core):
    nchip, R, C = part.shape
    rh = R // 2
    tr = _tile(rh, 256, 2 * SUBLANES)
    nt = rh // tr

    def body(core_ref, a_ref, b_ref, o_ref):
        del core_ref
        o_ref[...] = (a_ref[...] + b_ref[...]).astype(o_ref.dtype)

    return pl.pallas_call(
        body, name=name,
        grid_spec=pltpu.PrefetchScalarGridSpec(
            num_scalar_prefetch=1, grid=(nchip, nt),
            in_specs=[pl.BlockSpec((None, tr, C), lambda j, i, core_ref: (j, core_ref[0] * nt + i, 0)),
                      pl.BlockSpec((None, tr, C), lambda j, i, core_ref: (j, i, 0))],
            out_specs=pl.BlockSpec((None, tr, C), lambda j, i, core_ref: (j, i, 0))),
        out_shape=jax.ShapeDtypeStruct((nchip, rh, C), bf16), compiler_params=_params("parallel", "parallel"),
    )(core, part, got)


def _add_own_half_w_in(part, got, core):
    _, R, C = part.shape
    rh = R // 2
    tr = _tile(rh, 256, 2 * SUBLANES)
    nt = rh // tr
    tn = 256
    per_seg = C // tn
    per_chip = IN_COLS // N_CHIPS // tn

    def src(j):
        return ((j // per_seg + N_SEG - 1) % N_SEG, j % per_seg)

    def body(core_ref, a_ref, b_ref, o_ref):
        del core_ref
        o_ref[...] = (a_ref[...] + b_ref[...]).astype(o_ref.dtype)

    return pl.pallas_call(
        body, name="add_half_w_in",
        grid_spec=pltpu.PrefetchScalarGridSpec(
            num_scalar_prefetch=1, grid=(IN_COLS // tn, nt),
            in_specs=[pl.BlockSpec((None, tr, tn), lambda j, i, core_ref: (src(j)[0], core_ref[0] * nt + i, src(j)[1])),
                      pl.BlockSpec((None, tr, tn), lambda j, i, core_ref: (src(j)[0], i, src(j)[1]))],
            out_specs=pl.BlockSpec((None, tr, tn), lambda j, i, core_ref: (j // per_chip, i, j % per_chip))),
        out_shape=jax.ShapeDtypeStruct((N_CHIPS, rh, IN_COLS // N_CHIPS), bf16), compiler_params=_params("parallel", "parallel"),
    )(core, part, got)


def _chip_exchange(sums):
    n = len(sums)

    def body(*refs):
        start, finish = _chip_exchange_steps(refs[:n], refs[n:2 * n], *refs[2 * n:])
        start()
        finish()

    return pl.pallas_call(
        body, name="chip_exchange", in_specs=[ANY] * n, out_specs=[ANY] * n,
        out_shape=[jax.ShapeDtypeStruct(a.shape, a.dtype) for a in sums],
        scratch_shapes=_chip_exchange_sems(n),
    )(*sums)


def _chip_exchange_sems(n):
    return [pltpu.SemaphoreType.DMA((3 * n,)), pltpu.SemaphoreType.DMA((3 * n,))]


def _chip_exchange_steps(ins, outs, send, recv):
    n = len(ins)

    def copy(a, k, own_slot):
        x, y, c, chips = _place()
        px, py = chips[k]
        slot = 2 * x + y if own_slot else 2 * px + py
        return pltpu.make_async_remote_copy(src_ref=ins[a].at[2 * px + py], dst_ref=outs[a].at[slot], send_sem=send.at[3 * a + k],
                                            recv_sem=recv.at[3 * a + k], device_id=(px, py, c), device_id_type=MESH)

    def start():
        for a in range(n):
            for k in range(3):
                copy(a, k, True).start()

    def finish():
        for a in range(n):
            for k in range(3):
                copy(a, k, False).wait_recv()
        for a in range(n):
            for k in range(3):
                copy(a, k, True).wait_send()

    return start, finish


def _sum_chips(name, slots, sums, where):
    nchip, rh, C = slots.shape
    tr = _tile(rh, 256, 2 * SUBLANES)
    nt = rh // tr

    def body(where_ref, own_ref, s1_ref, s2_ref, s3_ref, o_ref):
        me = where_ref[0]
        by_dist = [r[...].astype(f32) for r in (own_ref, s1_ref, s2_ref, s3_ref)]
        acc = None
        for j in range(nchip):
            d = me ^ j
            term = jnp.where(d == 0, by_dist[0], jnp.where(d == 1, by_dist[1], jnp.where(d == 2, by_dist[2], by_dist[3])))
            acc = term if acc is None else acc + term
        o_ref[...] = acc

    def other(d):
        return pl.BlockSpec((None, tr, C), lambda i, w: (w[0] ^ d, i, 0))

    return pl.pallas_call(
        body, name=name,
        grid_spec=pltpu.PrefetchScalarGridSpec(
            num_scalar_prefetch=1, grid=(nt,),
            in_specs=[other(0), other(1), other(2), other(3)],
            out_specs=pl.BlockSpec((tr, C), lambda i, w: (w[1] * nt + i, 0))),
        out_shape=jax.ShapeDtypeStruct((2 * rh, C), f32), compiler_params=_params("parallel"),
    )(where, sums, slots, slots, slots)


def _sum_slots(name, slots):
    ns, R, C = slots.shape
    tr = _tile(R, 256)

    def body(s_ref, o_ref):
        acc = s_ref[0]
        for j in range(1, ns):
            acc = acc + s_ref[j]
        o_ref[...] = acc

    return pl.pallas_call(
        body, name=name, grid=(R // tr,), in_specs=[pl.BlockSpec((ns, tr, C), lambda i: (0, i, 0))],
        out_specs=pl.BlockSpec((tr, C), lambda i: (i, 0)), out_shape=jax.ShapeDtypeStruct((R, C), f32),
        compiler_params=_params("parallel"),
    )(slots)


def _sibling_join(fulls):
    n = len(fulls)

    def body(*refs):
        ins, outs = refs[:n], refs[n:2 * n]
        send, recv = refs[2 * n:]
        x, y, c, _ = _place()

        def copy(a, core):
            rh = ins[a].shape[0] // 2
            rows = pl.ds(core * rh, rh)
            return pltpu.make_async_remote_copy(src_ref=ins[a].at[rows, :], dst_ref=outs[a].at[rows, :], send_sem=send.at[a],
                                                recv_sem=recv.at[a], device_id=(x, y, 1 - c), device_id_type=MESH)

        for a in range(n):
            copy(a, c).start()
        for a in range(n):
            copy(a, 1 - c).wait_recv()
        for a in range(n):
            copy(a, c).wait_send()

    return pl.pallas_call(
        body, name="sibling_join", in_specs=[ANY] * n, out_specs=[ANY] * n,
        out_shape=[jax.ShapeDtypeStruct(a.shape, a.dtype) for a in fulls],
        scratch_shapes=[pltpu.SemaphoreType.DMA((n,)), pltpu.SemaphoreType.DMA((n,))],
        input_output_aliases={a: a for a in range(n)},
    )(*fulls)


def _allgather_devices(v):
    def body(v_ref, out_ref, send, recv):
        x, y, c, chips = _place()
        me, sibling = (x, y, c), (x, y, 1 - c)

        def slot(px, py, pc):
            return out_ref.at[4 * px + 2 * py + pc]

        def copy(k, block, to, src=None):
            return pltpu.make_async_remote_copy(src_ref=slot(*block) if src is None else src, dst_ref=slot(*block),
                                                send_sem=send.at[k], recv_sem=recv.at[k], device_id=to, device_id_type=MESH)

        first = [copy(0, me, sibling, src=v_ref)] + [copy(1 + j, me, (*chip, c), src=v_ref) for j, chip in enumerate(chips)]
        for cp in first:
            cp.start()
        passed = [copy(4 + j, (*chip, c), sibling) for j, chip in enumerate(chips)]
        for j, chip in enumerate(chips):
            copy(1 + j, (*chip, c), me).wait_recv()
            passed[j].start()
        copy(0, sibling, me).wait_recv()
        for j, chip in enumerate(chips):
            copy(4 + j, (*chip, 1 - c), me).wait_recv()
        for cp in first + passed:
            cp.wait_send()

    return pl.pallas_call(
        body, name="allgather_devices", in_specs=[ANY], out_specs=ANY,
        out_shape=jax.ShapeDtypeStruct((N_DEV,) + v.shape, v.dtype),
        scratch_shapes=[pltpu.SemaphoreType.DMA((N_DEV - 1,)), pltpu.SemaphoreType.DMA((N_DEV - 1,))],
    )(v)


def _adamw(name, w, g, m, v):
    R, C = w.shape
    tr = _tile(R, 256)
    c1 = 1.0 / (1.0 - ADAM_B1 ** ADAM_STEP)
    c2 = 1.0 / (1.0 - ADAM_B2 ** ADAM_STEP)

    def body(w_ref, g_ref, m_ref, v_ref, d_ref, nm_ref, nv_ref):
        gv = g_ref[...]
        nm = ADAM_B1 * m_ref[...] + (1.0 - ADAM_B1) * gv
        nv = ADAM_B2 * v_ref[...] + (1.0 - ADAM_B2) * gv * gv
        d_ref[...] = -ADAM_LR * ((nm * c1) / (jnp.sqrt(nv * c2) + ADAM_EPS) + ADAM_WD * w_ref[...])
        nm_ref[...] = nm
        nv_ref[...] = nv

    row = pl.BlockSpec((tr, C), lambda i: (i, 0))
    sh = jax.ShapeDtypeStruct((R, C), f32)
    return pl.pallas_call(body, name=name, grid=(R // tr,), in_specs=[row] * 4, out_specs=[row] * 3,
                          out_shape=[sh, sh, sh], compiler_params=_params("parallel"))(w, g, m, v)


def _adamw_update(w, g, m, v):
    c1 = 1.0 / (1.0 - ADAM_B1 ** ADAM_STEP)
    c2 = 1.0 / (1.0 - ADAM_B2 ** ADAM_STEP)
    nm = ADAM_B1 * m + (1.0 - ADAM_B1) * g
    nv = ADAM_B2 * v + (1.0 - ADAM_B2) * g * g
    return -ADAM_LR * ((nm * c1) / (jnp.sqrt(nv * c2) + ADAM_EPS) + ADAM_WD * w), nm, nv


def _adamw_many(ws, gs, ms, vs):
    n = len(ws)

    def body(*refs):
        ins, outs = refs[:4 * n], refs[4 * n:]
        for a in range(n):
            d, nm, nv = _adamw_update(ins[a][...], ins[n + a][...], ins[2 * n + a][...], ins[3 * n + a][...])
            outs[a][...] = d
            outs[n + a][...] = nm
            outs[2 * n + a][...] = nv

    shapes = [jax.ShapeDtypeStruct(a.shape, f32) for a in ws]
    return pl.pallas_call(body, name="adamw_small", out_shape=shapes * 3)(*ws, *gs, *ms, *vs)


def _zoh_parts(lr, li, log_dt):
    dt = jnp.exp(log_dt)
    mag = jnp.exp(lr * dt)
    c, s = jnp.cos(li * dt), jnp.sin(li * dt)
    ab_re, ab_im = mag * c, mag * s
    den = lr * lr + li * li
    nr = ab_re - 1.0
    coef_re = (nr * lr + ab_im * li) / den
    coef_im = (ab_im * lr - nr * li) / den
    return dt, mag, c, s, ab_re, ab_im, den, nr, coef_re, coef_im


def _zoh_fwd(lr, li, log_dt, b_re, b_im):
    def body(lr_ref, li_ref, ld_ref, br_ref, bi_ref, ar_ref, ai_ref, bbr_ref, bbi_ref):
        _, _, _, _, ab_re, ab_im, _, _, coef_re, coef_im = _zoh_parts(lr_ref[...], li_ref[...], ld_ref[...])
        ar_ref[...] = ab_re
        ai_ref[...] = ab_im
        bbr_ref[...] = coef_re * br_ref[...] - coef_im * bi_ref[...]
        bbi_ref[...] = coef_re * bi_ref[...] + coef_im * br_ref[...]

    col = jax.ShapeDtypeStruct(lr.shape, f32)
    mat = jax.ShapeDtypeStruct(b_re.shape, f32)
    return pl.pallas_call(body, name="zoh_fwd", out_shape=[col, col, mat, mat])(lr, li, log_dt, b_re, b_im)


def _zoh_bwd(lr, li, log_dt, b_re, b_im, d_ar, d_ai, d_bbr, d_bbi):
    n = lr.shape[1]
    groups = n // SSM_STATE

    def body(lr_ref, li_ref, ld_ref, br_ref, bi_ref, dar_ref, dai_ref, dbbr_ref, dbbi_ref,
             dlr_ref, dli_ref, dld_ref, dbr_ref, dbi_ref):
        lr_, li_ = lr_ref[...], li_ref[...]
        dt, mag, c, s, _, ab_im, den, nr, coef_re, coef_im = _zoh_parts(lr_, li_, ld_ref[...])
        br, bi, dbbr, dbbi = br_ref[...], bi_ref[...], dbbr_ref[...], dbbi_ref[...]
        dbr_ref[...] = coef_re * dbbr + coef_im * dbbi
        dbi_ref[...] = coef_re * dbbi - coef_im * dbbr
        d_cr = jnp.sum(dbbr * br + dbbi * bi, axis=0, keepdims=True)
        d_ci = jnp.sum(dbbi * br - dbbr * bi, axis=0, keepdims=True)
        d_nr = (d_cr * lr_ - d_ci * li_) / den
        d_abi = dai_ref[...] + (d_cr * li_ + d_ci * lr_) / den
        d_abr = dar_ref[...] + d_nr
        d_den = -(d_cr * coef_re + d_ci * coef_im) / den
        d_lr = (d_cr * nr + d_ci * ab_im) / den + 2.0 * lr_ * d_den
        d_li = (d_cr * ab_im - d_ci * nr) / den + 2.0 * li_ * d_den
        d_theta = mag * (d_abi * c - d_abr * s)
        d_arg = mag * (d_abr * c + d_abi * s)
        dlr_ref[...] = d_lr + d_arg * dt
        dli_ref[...] = d_li + d_theta * dt
        d_dt = d_arg * lr_ + d_theta * li_
        member = (lax.broadcasted_iota(jnp.int32, (n, groups), 0) >> (SSM_STATE.bit_length() - 1)
                  == lax.broadcasted_iota(jnp.int32, (n, groups), 1)).astype(f32)
        dld_ref[...] = jnp.dot(d_dt * dt, member, preferred_element_type=f32, precision=lax.Precision.HIGHEST)

    col = jax.ShapeDtypeStruct(lr.shape, f32)
    mat = jax.ShapeDtypeStruct(b_re.shape, f32)
    return pl.pallas_call(body, name="zoh_bwd", out_shape=[col, col, jax.ShapeDtypeStruct((1, groups), f32), mat, mat])(
        lr, li, log_dt, b_re, b_im, d_ar, d_ai, d_bbr, d_bbi)


def _lower_bound_fwd(logits):
    def body(x_ref, o_ref):
        x = x_ref[...]
        e = jnp.exp(x - jnp.max(x, axis=0, keepdims=True))
        o_ref[...] = e / jnp.sum(e, axis=0, keepdims=True)

    return pl.pallas_call(body, name="lower_bound_fwd", out_shape=jax.ShapeDtypeStruct(logits.shape, f32))(logits)


def _lower_bound_bwd(sm, d_lb):
    def body(sm_ref, d_ref, o_ref):
        smv = sm_ref[...]
        row = lax.broadcasted_iota(jnp.int32, smv.shape, 0)
        sm0 = smv[0:1, :]
        o_ref[...] = sm0 * d_ref[...] * (jnp.where(row == 0, 1.0, 0.0) - smv)

    return pl.pallas_call(body, name="lower_bound_bwd", out_shape=jax.ShapeDtypeStruct(sm.shape, f32))(sm, d_lb)


def _s5_tables(ab_re, ab_im, bb_re, bb_im, c_re, c_im, seg):
    eye = jnp.eye(SLAB_GROUPS, dtype=f32)

    def blk_in(bb):
        return jnp.einsum("hsgp,gk->sghkp", bb.reshape(SSM_GROUP, N_SLAB, SLAB_GROUPS, SSM_STATE), eye).reshape(
            N_SLAB, SLAB_CH, SLAB_NS)

    def blk_out(cc):
        return jnp.einsum("sghp,gk->skpgh", cc.reshape(N_SLAB, SLAB_GROUPS, SSM_GROUP, SSM_STATE), eye).reshape(
            N_SLAB, SLAB_NS, SLAB_CH)

    bs = jnp.concatenate([blk_in(bb_re), blk_in(bb_im)], axis=2).astype(bf16)
    cs = jnp.concatenate([blk_out(c_re), blk_out(-c_im)], axis=1).astype(bf16)
    n = SSM_GROUPS * SSM_STATE
    pw = _power_table(jnp.stack([ab_re.reshape(1, n), ab_im.reshape(1, n)]), -(-seg // SUBLANES))
    return bs, cs, pw


def _power_table(ab, tiles):
    n = ab.shape[2]

    def body(a_ref, o_ref):
        row = lax.broadcasted_iota(jnp.int32, (SUBLANES, n), 0)
        ar, ai = a_ref[0], a_ref[1]
        tr, ti = jnp.broadcast_to(ar, (SUBLANES, n)), jnp.broadcast_to(ai, (SUBLANES, n))
        pr, pi = ar, ai
        for r in range(1, SUBLANES):
            pr, pi = pr * ar - pi * ai, pr * ai + pi * ar
            tr = jnp.where(row == r, pr, tr)
            ti = jnp.where(row == r, pi, ti)
        o_ref[0, 0:SUBLANES, :] = tr
        o_ref[1, 0:SUBLANES, :] = ti

        def step(j, carry):
            cr, ci = carry
            cr, ci = cr * pr - ci * pi, cr * pi + ci * pr
            o_ref[0, _rows8(j), :] = cr
            o_ref[1, _rows8(j), :] = ci
            return cr, ci

        lax.fori_loop(1, tiles, step, (tr, ti))

    return pl.pallas_call(body, name="power_table", out_shape=jax.ShapeDtypeStruct((2, SUBLANES * tiles, n), f32))(ab)


def _s5_table_grads(dbs, dcs, da):
    eye = jnp.eye(SLAB_GROUPS, dtype=f32)
    d6 = dbs.reshape(N_SLAB, SLAB_GROUPS, SSM_GROUP, 2, SLAB_GROUPS, SSM_STATE)
    dbb = jnp.einsum("sghrkp,gk->rhsgp", d6, eye).reshape(2, SSM_GROUP, SSM_GROUPS * SSM_STATE)
    c6 = dcs.reshape(N_SLAB, 2, SLAB_GROUPS, SSM_STATE, SLAB_GROUPS, SSM_GROUP)
    dcc = jnp.einsum("srkpgh,gk->rsghp", c6, eye).reshape(2, SSM_GROUPS, SSM_GROUP, SSM_STATE)
    dab = da.transpose(1, 0, 2).reshape(2, SSM_GROUPS, SSM_STATE)
    return dab[0], dab[1], dbb[0], dbb[1], dcc[0], -dcc[1]


SMALL = ["mix_norm_g", "ssm_lambda_re", "ssm_lambda_im", "ssm_log_dt", "ssm_b_re", "ssm_b_im", "ssm_c_re", "ssm_c_im",
         "ssm_d", "hgrn_lb_logits", "hgrn_norm_g", "ffn_norm_g", "conv_b", "final_norm_g"]
SHARDED_SMALL = ["meta_tokens", "conv_w"]
BIG = ["w_in", "ssm_w_glu", "w_ssm_proj", "w_hgrn_proj", "w_out", "w_up", "w_down"]
WEIGHTS = ['meta_tokens', 'mix_norm_g', 'w_in', 'ssm_lambda_re', 'ssm_lambda_im', 'ssm_log_dt', 'ssm_b_re', 'ssm_b_im',
           'ssm_c_re', 'ssm_c_im', 'ssm_d', 'ssm_w_glu', 'w_ssm_proj', 'hgrn_lb_logits', 'hgrn_norm_g', 'w_hgrn_proj',
           'w_out', 'ffn_norm_g', 'w_up', 'conv_w', 'conv_b', 'w_down', 'final_norm_g']


LATER = [k for k in BIG if k != "w_in"]


def _full_weights(gathered, shards, chip):
    Dm = D_MODEL
    g = {k: lax.dynamic_update_slice(gathered[k], shards[k][None], (chip, 0, 0)) for k in gathered}
    full = {}
    for k, v in g.items():
        if k == "w_in":
            full[k] = jnp.roll(v.transpose(1, 0, 2).reshape(Dm, IN_COLS), -Dm, axis=1)
        elif k == "w_up":
            full[k] = v.transpose(1, 0, 2).reshape(Dm, 2 * D_FF)
        else:
            full[k] = v.reshape(-1, Dm)
    return full


def _local_grads(x, tgt, meta, w, full, shards, chip, core):
    B, S, Dm = x.shape
    L = S + N_META
    T = B * L
    h0 = jnp.concatenate([jnp.broadcast_to(meta[None], (B, N_META, Dm)), x], axis=1).reshape(T, Dm)

    lb_all = _lower_bound_fwd(w["hgrn_lb_logits"])
    lb = lb_all[0:1]
    gp = SSM_GROUPS * SSM_STATE
    zoh_in = (w["ssm_lambda_re"].reshape(1, gp), w["ssm_lambda_im"].reshape(1, gp),
              jnp.repeat(w["ssm_log_dt"].reshape(SSM_GROUPS, 1), SSM_STATE, axis=1).reshape(1, gp),
              w["ssm_b_re"].reshape(gp, SSM_GROUP).T, w["ssm_b_im"].reshape(gp, SSM_GROUP).T)
    ab_re, ab_im, bb_re, bb_im = _zoh_fwd(*zoh_in)
    bs, cs, pw = _s5_tables(ab_re, ab_im, bb_re, bb_im, w["ssm_c_re"][0], w["ssm_c_im"][0], L // SUBLANES)

    z1 = _rmsnorm_fwd("mix_norm", h0, w["mix_norm_g"])
    p, gathered = _in_proj_gather(z1, full["w_in"], [shards[k] for k in LATER])
    full = {**full, **_full_weights(dict(zip(LATER, gathered)), shards, chip)}
    ya0 = _s5_fwd(p, bs, cs, pw, w["ssm_d"], B, L)
    gl, ya = _glu_proj_fwd(ya0, full["ssm_w_glu"])
    yb = _hgrn_fwd(p, lb, w["hgrn_norm_g"], B, L)
    pa, pb, merged = _proj_merge_fwd(ya, yb, full["w_ssm_proj"], full["w_hgrn_proj"], p)
    h1, z2 = _out_proj_norm(merged, full["w_out"], h0, w["ffn_norm_g"])
    up = _mm_rows("up_proj", z2, full["w_up"], "nn", f32, D_FF // 2)
    ff = _conv_fwd(up, full["conv_w"], w["conv_b"], B, L)
    h2 = _mm_rows("down_proj", ff, full["w_down"], "nn", f32, 1024, res=h1, tk=D_FF // 2)

    tgt_rows = jnp.pad(tgt, ((0, 0), (N_META, 0), (0, 0))).reshape(T, Dm)
    dh2, loss, d_final_g = _final_loss(h2, tgt_rows, w["final_norm_g"].reshape(1, Dm), L)

    dff = _mm_rows("d_ff", dh2, full["w_down"], "nt", f32, D_FF // 2)
    g_w_down = _mm_wgrad("dw_down", ff, dh2, tn=512)
    dup, dconv = _conv_bwd(up, dff, full["conv_w"], w["conv_b"], B, L)
    g_w_up = _dw_up(z2, dup)
    dh1, d_ffn_g = _dz2_norm(dup, full["w_up"], h1, w["ffn_norm_g"], dh2)

    g_w_out = _mm_wgrad("dw_out", merged, dh1)
    dpa, dpb, dp = _merge_bwd_fused(dh1, full["w_out"], p, pa, pb)
    dgl, dya0_direct = _glu_bwd_fused(dpa, full["w_ssm_proj"], ya0, gl)
    g_w_ssm_proj = _mm_wgrad("dw_ssm_proj", ya, dpa)
    dyb = _mm_rows("d_yb", dpb, full["w_hgrn_proj"], "nt", f32, 1024)
    g_w_hgrn_proj = _mm_wgrad("dw_hgrn_proj", yb, dpb)
    dp, d_lb, d_hgrn_g = _hgrn_bwd(p, dyb, dp, lb, w["hgrn_norm_g"], B, L)
    dya0 = _mm_rows("d_ya0", dgl, full["ssm_w_glu"], "nt", f32, 1024, res=dya0_direct)
    g_w_glu = _mm_wgrad("dw_glu", ya0, dgl)
    parts = {
        "ssm_w_glu": g_w_glu.reshape(N_CHIPS, Dm // N_CHIPS, Dm), "w_ssm_proj": g_w_ssm_proj.reshape(N_CHIPS, Dm // N_CHIPS, Dm),
        "w_hgrn_proj": g_w_hgrn_proj.reshape(N_CHIPS, Dm // N_CHIPS, Dm), "w_out": g_w_out.reshape(N_CHIPS, Dm // N_CHIPS, Dm),
        "w_up": g_w_up, "w_down": g_w_down.reshape(N_CHIPS, D_FF // N_CHIPS, Dm),
    }
    got = _sibling_halves([parts[k] for k in LATER])
    sums = {k: _add_own_half("add_half_" + k, parts[k], gt, core) for k, gt in zip(LATER, got)}
    (dp, dbs, dcs, da, d_skip), slots_later = _s5_bwd(p, dya0, dp, bs, cs, pw, w["ssm_d"], B, L, [sums[k] for k in LATER])
    slots = dict(zip(LATER, slots_later))
    g_w_in = _dw_in(z1, dp)
    dh0, d_mix_g = _dz1_norm(dp, full["w_in"], h0, w["mix_norm_g"], dh1)

    dh0 = dh0.reshape(B, L, Dm)
    grad_x = dh0[:, N_META:]
    d_meta = _meta_grad(dh0[:, :N_META])

    d_ab_re, d_ab_im, d_bb_re, d_bb_im, d_c_re, d_c_im = _s5_table_grads(dbs, dcs, da)
    d_lr, d_li, d_log_dt, d_b_re, d_b_im = _zoh_bwd(*zoh_in, d_ab_re.reshape(1, gp), d_ab_im.reshape(1, gp), d_bb_re, d_bb_im)
    gps = (SSM_GROUPS, SSM_STATE)
    d_lr, d_li, d_log_dt = d_lr.reshape(gps), d_li.reshape(gps), d_log_dt.reshape(SSM_GROUPS)
    d_b_re, d_b_im = d_b_re.T.reshape(gps + (SSM_GROUP,)), d_b_im.T.reshape(gps + (SSM_GROUP,))
    d_logits = _lower_bound_bwd(lb_all, d_lb)
    small = {
        "meta_tokens": d_meta, "mix_norm_g": d_mix_g, "ssm_lambda_re": d_lr[None], "ssm_lambda_im": d_li[None],
        "ssm_log_dt": d_log_dt[None], "ssm_b_re": d_b_re[None], "ssm_b_im": d_b_im[None], "ssm_c_re": d_c_re[None],
        "ssm_c_im": d_c_im[None], "ssm_d": d_skip, "hgrn_lb_logits": d_logits, "hgrn_norm_g": d_hgrn_g,
        "ffn_norm_g": d_ffn_g, "conv_w": dconv[:, 0:3, :].transpose(1, 0, 2).reshape(3, 2 * D_FF),
        "conv_b": dconv[:, 3, :].reshape(1, 2 * D_FF), "final_norm_g": d_final_g.reshape(Dm),
    }
    sums["w_in"] = _add_own_half_w_in(g_w_in, _sibling_halves([g_w_in], "sibling_halves_w_in")[0], core)
    slots["w_in"] = _chip_exchange([sums["w_in"]])[0]
    return loss, grad_x, sums, slots, small


PACK_ROWS = 256


def _pack(parts):
    flat = jnp.concatenate([parts[k].reshape(-1) for k in parts])
    n = flat.shape[0]
    rows = -(-n // (PACK_ROWS * LANES)) * PACK_ROWS
    flat = jnp.pad(flat, (0, rows * LANES - n))
    return flat.reshape(rows, LANES)


def _unpack(packed, like):
    flat = packed.reshape(-1)
    out, o = {}, 0
    for k, ref in like.items():
        n = math.prod(ref.shape)
        out[k] = flat[o:o + n].reshape(ref.shape)
        o += n
    return out


def kernel(x, meta_tokens, mix_norm_g, w_in, ssm_lambda_re, ssm_lambda_im, ssm_log_dt, ssm_b_re, ssm_b_im, ssm_c_re, ssm_c_im, ssm_d, ssm_w_glu, w_ssm_proj, hgrn_lb_logits, hgrn_norm_g, w_hgrn_proj, w_out, ffn_norm_g, w_up, conv_w, conv_b, w_down, final_norm_g, loss_target, m_meta_tokens, m_mix_norm_g, m_w_in, m_ssm_lambda_re, m_ssm_lambda_im, m_ssm_log_dt, m_ssm_b_re, m_ssm_b_im, m_ssm_c_re, m_ssm_c_im, m_ssm_d, m_ssm_w_glu, m_w_ssm_proj, m_hgrn_lb_logits, m_hgrn_norm_g, m_w_hgrn_proj, m_w_out, m_ffn_norm_g, m_w_up, m_conv_w, m_conv_b, m_w_down, m_final_norm_g, v_meta_tokens, v_mix_norm_g, v_w_in, v_ssm_lambda_re, v_ssm_lambda_im, v_ssm_log_dt, v_ssm_b_re, v_ssm_b_im, v_ssm_c_re, v_ssm_c_im, v_ssm_d, v_ssm_w_glu, v_w_ssm_proj, v_hgrn_lb_logits, v_hgrn_norm_g, v_w_hgrn_proj, v_w_out, v_ffn_norm_g, v_w_up, v_conv_w, v_conv_b, v_w_down, v_final_norm_g):
    args = dict(locals())
    w = {k: args[k] for k in WEIGHTS}
    mom = {k: args["m_" + k] for k in WEIGHTS}
    var = {k: args["v_" + k] for k in WEIGHTS}
    Dm = D_MODEL
    cx, cy, cc = lax.axis_index("x"), lax.axis_index("y"), lax.axis_index("c")
    chip = 2 * cx + cy

    shards = {k: w[k][0].astype(bf16) for k in BIG}
    g_meta, g_cw = _allgather_chips([w["meta_tokens"], w["conv_w"][0]])
    full = _full_weights({"w_in": _allgather_split([shards["w_in"]])[0]}, shards, chip)
    full["conv_w"] = g_cw.transpose(1, 0, 2).reshape(3, 2 * D_FF)
    meta_full = g_meta.transpose(1, 0, 2).reshape(N_META, Dm)

    core = cc.reshape(1).astype(jnp.int32)
    loss_part, grad_x, sums, slots, small = _local_grads(x, loss_target, meta_full, w, full, shards, chip, core)

    where = jnp.stack([chip, cc]).astype(jnp.int32)
    fulls = [_sum_chips("sum_chips_" + k, slots[k], sums[k], where) for k in BIG]
    g_big = dict(zip(BIG, _sibling_join(fulls)))

    small_all = dict(small)
    small_all["loss"] = loss_part[0, 0:1]
    packed = _pack(small_all)
    slots_dev = lax.dynamic_update_slice(_allgather_devices(packed), packed[None], (2 * chip + cc, 0, 0))
    reduced = _unpack(_sum_slots("sum_devices", slots_dev), small_all)
    loss = reduced.pop("loss")[0]
    mcols = Dm // N_CHIPS
    ccols = 2 * D_FF // N_CHIPS
    grads = {k: reduced[k] for k in SMALL}
    grads["meta_tokens"] = lax.dynamic_slice(reduced["meta_tokens"], (0, chip * mcols), (N_META, mcols))
    grads["conv_w"] = lax.dynamic_slice(reduced["conv_w"], (0, chip * ccols), (3, ccols))[None]
    for k in BIG:
        grads[k] = g_big[k][None]

    delta, new_m, new_v = {}, {}, {}
    for k in BIG:
        shp = w[k].shape
        d, nm, nv = _adamw("adamw_" + k, w[k][0], grads[k][0], mom[k][0], var[k][0])
        delta[k], new_m[k], new_v[k] = d.reshape(shp), nm.reshape(shp), nv.reshape(shp)
    rest = SMALL + SHARDED_SMALL

    def flat2(a):
        return a.reshape(-1, a.shape[-1])

    outs = _adamw_many(*[[flat2(t[k]) for k in rest] for t in (w, grads, mom, var)])
    n = len(rest)
    for j, dst in enumerate((delta, new_m, new_v)):
        dst.update({k: o.reshape(w[k].shape) for k, o in zip(rest, outs[j * n:(j + 1) * n])})

    return (loss, grad_x, *[grads[k].reshape(w[k].shape) for k in WEIGHTS], *[delta[k] for k in WEIGHTS],
            *[new_m[k] for k in WEIGHTS], *[new_v[k] for k in WEIGHTS])
```

```python
import math

import jax
import jax.numpy as jnp
from jax import lax
from jax.experimental import pallas as pl
from jax.experimental.pallas import tpu as pltpu

f32 = jnp.float32
bf16 = jnp.bfloat16

D_MODEL = 1024
N_META = 16
SSM_GROUP = 16
SSM_GROUPS = 64
SSM_STATE = 64
SLAB_GROUPS = 8
N_SLAB = SSM_GROUPS // SLAB_GROUPS
SLAB_CH = SLAB_GROUPS * SSM_GROUP
SLAB_NS = SLAB_GROUPS * SSM_STATE
HEADS = 8
HEAD_DIM = 128
CHUNK = 16
D_FF = 2816
IN_COLS = 7168
EPS = 1e-6
SUBLANES = 8
LANES = 128
N_CHIPS = 4
N_DEV = 8
ADAM_LR, ADAM_B1, ADAM_B2, ADAM_EPS, ADAM_WD, ADAM_STEP = 0.001, 0.9, 0.999, 1e-08, 0.01, 10
MESH = pl.DeviceIdType.MESH
ANY = pl.BlockSpec(memory_space=pl.ANY)

SEG_Q, SEG_F, SEG_I, SEG_OG, SEG_GA, SEG_GB, SEG_U = range(7)
N_SEG = 7


def _tile(n, target, mult=SUBLANES):
    best = None
    for d in range(mult, min(n, target) + 1, mult):
        if n % d == 0:
            best = d
    return n if best is None else best


def _params(*sem):
    return pltpu.CompilerParams(dimension_semantics=sem)


def _sigmoid(x):
    return 1.0 / (1.0 + jnp.exp(-x))


_DIMS = {"nn": (((1,), (0,)), ((), ())), "nt": (((1,), (1,)), ((), ())), "tn": (((0,), (0,)), ((), ()))}


def _mm(name, a, b, dims, grid, a_spec, b_spec, out_shape, out_spec, acc_shape, res=None, res_spec=None):
    nk = grid[2]
    dn = _DIMS[dims]

    def body(*refs):
        if res is None:
            a_ref, b_ref, o_ref, acc = refs
        else:
            a_ref, b_ref, r_ref, o_ref, acc = refs
        k = pl.program_id(2)

        @pl.when(k == 0)
        def _():
            acc[...] = jnp.zeros_like(acc)

        acc[...] += lax.dot_general(a_ref[...].astype(bf16), b_ref[...].astype(bf16), dn, preferred_element_type=f32)

        @pl.when(k == nk - 1)
        def _():
            r = acc[...]
            if res is not None:
                r = r + r_ref[...]
            o_ref[...] = r.astype(o_ref.dtype)

    ins = [a, b] + ([] if res is None else [res])
    specs = [a_spec, b_spec] + ([] if res is None else [res_spec])
    return pl.pallas_call(
        body, name=name, grid=grid, in_specs=specs, out_specs=out_spec, out_shape=out_shape,
        scratch_shapes=[pltpu.VMEM(acc_shape, f32)],
        compiler_params=_params("parallel", "parallel", "arbitrary"),
    )(*ins)


def _mm_rows(name, a, w, dims, out_dtype, tn, res=None, tk=None):
    T, K = a.shape
    N = w.shape[1] if dims == "nn" else w.shape[0]
    tm = _tile(T, 1032)
    tk = K if tk is None else tk
    grid = (T // tm, N // tn, K // tk)
    a_spec = pl.BlockSpec((tm, tk), lambda i, j, k: (i, k))
    if dims == "nn":
        b_spec = pl.BlockSpec((tk, tn), lambda i, j, k: (k, j))
    else:
        b_spec = pl.BlockSpec((tn, tk), lambda i, j, k: (j, k))
    o_spec = pl.BlockSpec((tm, tn), lambda i, j, k: (i, j))
    return _mm(name, a, w, dims, grid, a_spec, b_spec, jax.ShapeDtypeStruct((T, N), out_dtype), o_spec, (tm, tn),
               res=res, res_spec=None if res is None else o_spec)


def _mm_fused(name, pairs, dims, extras, epilogue, outs, rows=(), tm_target=688):
    T, K = pairs[0][0].shape
    N = pairs[0][1].shape[1] if dims == "nn" else pairs[0][1].shape[0]
    tm = _tile(T, tm_target)
    tn = N
    grid = (T // tm, N // tn)
    npair, nex = len(pairs), len(extras) + len(rows)
    dn = _DIMS[dims]

    def body(*refs):
        ab = refs[:2 * npair]
        ex = refs[2 * npair:2 * npair + nex]
        o_refs = refs[2 * npair + nex:]
        accs = [lax.dot_general(ab[2 * q][...].astype(bf16), ab[2 * q + 1][...].astype(bf16), dn, preferred_element_type=f32)
                for q in range(npair)]
        vals = epilogue(accs, [e[...] for e in ex])
        for o_ref, v in zip(o_refs, vals):
            if isinstance(v, (list, tuple)):
                for s_, vs in enumerate(v):
                    o_ref[s_] = vs.astype(o_ref.dtype)
            else:
                o_ref[...] = v.astype(o_ref.dtype)

    ins, specs = [], []
    for a, w in pairs:
        ins += [a, w]
        specs.append(pl.BlockSpec((tm, K), lambda i, j: (i, 0)))
        specs.append(pl.BlockSpec((K, tn), lambda i, j: (0, j)) if dims == "nn" else pl.BlockSpec((tn, K), lambda i, j: (j, 0)))
    for arr, off in extras:
        ins.append(arr)
        specs.append(pl.BlockSpec((tm, tn), lambda i, j, off=off: (i, off + j)))
    for arr in rows:
        ins.append(arr)
        specs.append(pl.BlockSpec((1, tn), lambda i, j: (0, j)))
    shapes, ospecs = [], []
    for o in outs:
        if isinstance(o, tuple):
            dt, nseg, total, blk = o
            shapes.append(jax.ShapeDtypeStruct((total, T, N), dt))
            ospecs.append(pl.BlockSpec((nseg, tm, tn), lambda i, j, blk=blk: (blk, i, j)))
        else:
            shapes.append(jax.ShapeDtypeStruct((T, N), o))
            ospecs.append(pl.BlockSpec((tm, tn), lambda i, j: (i, j)))
    return pl.pallas_call(body, name=name, grid=grid, in_specs=specs, out_specs=ospecs, out_shape=shapes,
                          compiler_params=_params("parallel", "parallel"))(*ins)


def _glu_proj_fwd(ya0, w_glu):
    def epi(accs, tiles):
        return accs[0], tiles[0] * _sigmoid(accs[0])

    return _mm_fused("glu_proj", [(ya0, w_glu)], "nn", [(ya0, 0)], epi, [f32, bf16])


def _proj_merge_fwd(ya, yb, w_sp, w_hp, p):
    def epi(accs, tiles):
        return accs[0], accs[1], _sigmoid(tiles[0]) * accs[0] + _sigmoid(tiles[1]) * accs[1]

    return _mm_fused("proj_merge", [(ya, w_sp), (yb, w_hp)], "nn", [(p, SEG_GA), (p, SEG_GB)], epi, [f32, f32, bf16])


def _merge_bwd_fused(dh1, w_out, p, pa, pb):
    def epi(accs, tiles):
        d = accs[0]
        sa, sb = _sigmoid(tiles[0]), _sigmoid(tiles[1])
        return d * sa, d * sb, [d * tiles[2] * sa * (1.0 - sa), d * tiles[3] * sb * (1.0 - sb)]

    return _mm_fused("d_merged", [(dh1, w_out)], "nt", [(p, SEG_GA), (p, SEG_GB), (pa, 0), (pb, 0)], epi,
                     [bf16, bf16, (bf16, 2, N_SEG, SEG_GA // 2)], tm_target=344)


def _out_proj_norm(merged, w_out, h0, g):
    def epi(accs, tiles):
        h1 = tiles[0] + accs[0]
        r = lax.rsqrt(jnp.mean(h1 * h1, axis=-1, keepdims=True) + EPS)
        return h1, h1 * r * tiles[1]

    return _mm_fused("out_proj", [(merged, w_out)], "nn", [(h0, 0)], epi, [f32, bf16], rows=[g])


def _mm_rmsnorm_bwd(name, a, b, grid, a_spec, b_spec, x, g, dres):
    T, Dm = x.shape
    tm = T // grid[0]
    nk = grid[2]

    def body(a_ref, b_ref, x_ref, g_ref, dres_ref, dx_ref, dg_ref, acc):
        i, k = pl.program_id(0), pl.program_id(2)

        @pl.when(k == 0)
        def _():
            acc[...] = jnp.zeros_like(acc)

        @pl.when((i == 0) & (k == 0))
        def _():
            dg_ref[...] = jnp.zeros_like(dg_ref)

        acc[...] += lax.dot_general(a_ref[...].astype(bf16), b_ref[...].astype(bf16), _DIMS["nt"], preferred_element_type=f32)

        @pl.when(k == nk - 1)
        def _():
            xv = x_ref[...]
            r = lax.rsqrt(jnp.mean(xv * xv, axis=-1, keepdims=True) + EPS)
            xn = xv * r
            dzv = acc[...]
            dzg = dzv * g_ref[...]
            dx_ref[...] = dres_ref[...] + r * (dzg - xn * jnp.mean(dzg * xn, axis=-1, keepdims=True))
            dg_ref[...] += jnp.sum(dzv * xn, axis=0, keepdims=True)

    row = pl.BlockSpec((tm, Dm), lambda i, j, k: (i, 0))
    par = pl.BlockSpec((1, Dm), lambda i, j, k: (0, 0))
    return pl.pallas_call(
        body, name=name, grid=grid, in_specs=[a_spec, b_spec, row, par, row], out_specs=[row, par],
        out_shape=[jax.ShapeDtypeStruct((T, Dm), f32), jax.ShapeDtypeStruct((1, Dm), f32)],
        scratch_shapes=[pltpu.VMEM((tm, Dm), f32)],
        compiler_params=_params("arbitrary", "arbitrary", "arbitrary"),
    )(a, b, x, g, dres)


def _glu_bwd_fused(dpa, w_sp, ya0, gl):
    def epi(accs, tiles):
        d = accs[0]
        s = _sigmoid(tiles[1])
        return d * tiles[0] * s * (1.0 - s), d * s

    return _mm_fused("d_ya", [(dpa, w_sp)], "nt", [(ya0, 0), (gl, 0)], epi, [bf16, f32])


def _mm_wgrad(name, a, g, tn=None):
    T, K = a.shape
    N = g.shape[1]
    tk = _tile(T, 688)
    tn = N if tn is None else tn
    grid = (1, N // tn, T // tk)
    a_spec = pl.BlockSpec((tk, K), lambda i, j, k: (k, 0))
    g_spec = pl.BlockSpec((tk, tn), lambda i, j, k: (k, j))
    o_spec = pl.BlockSpec((K, tn), lambda i, j, k: (0, j))
    return _mm(name, a, g, "tn", grid, a_spec, g_spec, jax.ShapeDtypeStruct((K, N), f32), o_spec, (K, tn))


def _rmsnorm_fwd(name, x, g):
    T, Dm = x.shape
    tr = _tile(T, 688)

    def body(x_ref, g_ref, z_ref):
        xv = x_ref[...]
        r = lax.rsqrt(jnp.mean(xv * xv, axis=-1, keepdims=True) + EPS)
        z_ref[...] = (xv * r * g_ref[...]).astype(z_ref.dtype)

    return pl.pallas_call(
        body, name=name, grid=(T // tr,),
        in_specs=[pl.BlockSpec((tr, Dm), lambda i: (i, 0)), pl.BlockSpec((1, Dm), lambda i: (0, 0))],
        out_specs=pl.BlockSpec((tr, Dm), lambda i: (i, 0)),
        out_shape=jax.ShapeDtypeStruct((T, Dm), bf16), compiler_params=_params("parallel"),
    )(x, g)


def _final_loss(h2, tgt, g, L):
    T, Dm = h2.shape
    tr = _tile(L, 688)
    per_seq = L // tr

    def body(h_ref, t_ref, g_ref, dh_ref, loss_ref, dg_ref):
        pos = (pl.program_id(0) % per_seq) * tr + lax.broadcasted_iota(jnp.int32, (tr, 1), 0)
        live = jnp.where(pos >= N_META, 1.0, 0.0)
        hv = h_ref[...]
        r = lax.rsqrt(jnp.mean(hv * hv, axis=-1, keepdims=True) + EPS)
        xn = hv * r
        gv = g_ref[...]
        err = (xn * gv - t_ref[...]) * live
        dy = err * (1.0 / Dm)
        dyg = dy * gv
        dh_ref[...] = r * (dyg - xn * jnp.mean(dyg * xn, axis=-1, keepdims=True))

        @pl.when(pl.program_id(0) == 0)
        def _():
            dg_ref[...] = jnp.zeros_like(dg_ref)
            loss_ref[...] = jnp.zeros_like(loss_ref)

        dg_ref[...] += jnp.sum(dy * xn, axis=0, keepdims=True)
        loss_ref[...] += jnp.sum(err * err) * (0.5 / Dm)

    row = pl.BlockSpec((tr, Dm), lambda i: (i, 0))
    par = pl.BlockSpec((1, Dm), lambda i: (0, 0))
    return pl.pallas_call(
        body, name="final_loss", grid=(T // tr,), in_specs=[row, row, par],
        out_specs=[row, pl.BlockSpec((1, LANES), lambda i: (0, 0)), par],
        out_shape=[jax.ShapeDtypeStruct((T, Dm), f32), jax.ShapeDtypeStruct((1, LANES), f32), jax.ShapeDtypeStruct((1, Dm), f32)],
        compiler_params=_params("arbitrary"),
    )(h2, tgt, g)


def _meta_grad(dh0_meta):
    B = dh0_meta.shape[0]

    def body(d_ref, o_ref):
        acc = d_ref[0]
        for b in range(1, B):
            acc = acc + d_ref[b]
        o_ref[...] = acc

    return pl.pallas_call(body, name="meta_grad", out_shape=jax.ShapeDtypeStruct(dh0_meta.shape[1:], f32))(dh0_meta)


def _shift_down(x, k, row):
    return jnp.where(row >= k, pltpu.roll(x, k, 0), 0.0)


def _conv_fwd(up, conv_w, conv_b, B, L):
    tc = 256
    nt = D_FF // tc

    def body(xa_ref, xb_ref, wa_ref, wb_ref, ba_ref, bb_ref, o_ref):
        head = 2 * SUBLANES
        row = lax.broadcasted_iota(jnp.int32, (head, tc), 0)

        def gated(conv):
            a = conv(xa_ref, wa_ref, ba_ref)
            b = conv(xb_ref, wb_ref, bb_ref)
            return (a * _sigmoid(a) * b).astype(o_ref.dtype)

        def conv_rolled(x_ref, w_ref, b_ref):
            x = x_ref[...]
            return b_ref[...] + w_ref[0:1, :] * pltpu.roll(x, 2, 0) + w_ref[1:2, :] * pltpu.roll(x, 1, 0) + w_ref[2:3, :] * x

        def conv_head(x_ref, w_ref, b_ref):
            x = x_ref[0:head, :]
            return (b_ref[...] + w_ref[0:1, :] * _shift_down(x, 2, row) + w_ref[1:2, :] * _shift_down(x, 1, row)
                    + w_ref[2:3, :] * x)

        o_ref[...] = gated(conv_rolled)
        o_ref[0:head, :] = gated(conv_head)

    return pl.pallas_call(
        body, name="conv_fwd", grid=(B, nt),
        in_specs=[pl.BlockSpec((L, tc), lambda b, j: (b, j)), pl.BlockSpec((L, tc), lambda b, j: (b, j + nt)),
                  pl.BlockSpec((3, tc), lambda b, j: (0, j)), pl.BlockSpec((3, tc), lambda b, j: (0, j + nt)),
                  pl.BlockSpec((1, tc), lambda b, j: (0, j)), pl.BlockSpec((1, tc), lambda b, j: (0, j + nt))],
        out_specs=pl.BlockSpec((L, tc), lambda b, j: (b, j)),
        out_shape=jax.ShapeDtypeStruct((B * L, D_FF), bf16), compiler_params=_params("parallel", "parallel"),
    )(up, up, conv_w, conv_w, conv_b, conv_b)


CONV_ROWS = 2 * SUBLANES


def _rows16(i):
    return pl.ds(pl.multiple_of(i * CONV_ROWS, CONV_ROWS), CONV_ROWS)


def _conv_taps(x_ref, i, row):
    x = x_ref[_rows16(i), :]
    live = jnp.where(i > 0, 1.0, 0.0)
    r0 = jnp.maximum(i * CONV_ROWS, 2)
    p1 = x_ref[pl.ds(r0 - 1, 1), :] * live
    p2 = x_ref[pl.ds(r0 - 2, 1), :] * live
    x1 = jnp.where(row == 0, p1, pltpu.roll(x, 1, 0))
    x2 = jnp.where(row == 0, p2, jnp.where(row == 1, p1, pltpu.roll(x, 2, 0)))
    return x, x1, x2


def _conv_bwd(up, dff, conv_w, conv_b, B, L):
    tc = 256
    nt = D_FF // tc
    n = L // CONV_ROWS

    def body(xa_ref, xb_ref, d_ref, wa_ref, wb_ref, ba_ref, bb_ref, dup_ref, dw_ref, ga_ref, gb_ref):
        row = lax.broadcasted_iota(jnp.int32, (CONV_ROWS, tc), 0)

        @pl.when(pl.program_id(1) == 0)
        def _():
            dw_ref[...] = jnp.zeros_like(dw_ref)

        zero_tail = jnp.zeros((CONV_ROWS, tc), f32)
        ga_ref[L:L + CONV_ROWS, :] = zero_tail
        gb_ref[L:L + CONV_ROWS, :] = zero_tail

        def fold(v):
            return v[0:SUBLANES, :] + v[SUBLANES:CONV_ROWS, :]

        def step(i, acc):
            taps_a = _conv_taps(xa_ref, i, row)
            taps_b = _conv_taps(xb_ref, i, row)
            a = ba_ref[...] + wa_ref[0:1, :] * taps_a[2] + wa_ref[1:2, :] * taps_a[1] + wa_ref[2:3, :] * taps_a[0]
            b = bb_ref[...] + wb_ref[0:1, :] * taps_b[2] + wb_ref[1:2, :] * taps_b[1] + wb_ref[2:3, :] * taps_b[0]
            s = _sigmoid(a)
            d = d_ref[_rows16(i), :]
            g_a = d * b * s * (1.0 + a * (1.0 - s))
            g_b = d * a * s
            ga_ref[_rows16(i), :] = g_a
            gb_ref[_rows16(i), :] = g_b
            new = []
            for g, (x, x1, x2) in ((g_a, taps_a), (g_b, taps_b)):
                new += [fold(g * x2), fold(g * x1), fold(g * x), fold(g)]
            return tuple(o + v for o, v in zip(acc, new))

        z = jnp.zeros((SUBLANES, tc), f32)
        acc = _repeat_loop(n, step, (z,) * 8)
        for h in range(2):
            for t in range(4):
                dw_ref[h, t:t + 1, :] += jnp.sum(acc[4 * h + t], axis=0, keepdims=True)

        def back(i, c):
            for h, (g_ref, w_ref) in enumerate(((ga_ref, wa_ref), (gb_ref, wb_ref))):
                g = g_ref[_rows16(i), :]
                n1 = g_ref[pl.ds(i * CONV_ROWS + CONV_ROWS, 1), :]
                n2 = g_ref[pl.ds(i * CONV_ROWS + CONV_ROWS + 1, 1), :]
                u1 = jnp.where(row == CONV_ROWS - 1, n1, pltpu.roll(g, CONV_ROWS - 1, 0))
                u2 = jnp.where(row == CONV_ROWS - 1, n2, jnp.where(row == CONV_ROWS - 2, n1, pltpu.roll(g, CONV_ROWS - 2, 0)))
                dup_ref[h, _rows16(i), :] = (w_ref[2:3, :] * g + w_ref[1:2, :] * u1 + w_ref[0:1, :] * u2).astype(dup_ref.dtype)
            return c

        _repeat_loop(n, back, 0)

    return pl.pallas_call(
        body, name="conv_bwd", grid=(nt, B),
        in_specs=[pl.BlockSpec((L, tc), lambda j, b: (b, j)), pl.BlockSpec((L, tc), lambda j, b: (b, j + nt)),
                  pl.BlockSpec((L, tc), lambda j, b: (b, j)),
                  pl.BlockSpec((3, tc), lambda j, b: (0, j)), pl.BlockSpec((3, tc), lambda j, b: (0, j + nt)),
                  pl.BlockSpec((1, tc), lambda j, b: (0, j)), pl.BlockSpec((1, tc), lambda j, b: (0, j + nt))],
        out_specs=[pl.BlockSpec((2, L, tc), lambda j, b: (0, b, j)), pl.BlockSpec((2, SUBLANES, tc), lambda j, b: (0, 0, j))],
        out_shape=[jax.ShapeDtypeStruct((2, B * L, D_FF), bf16), jax.ShapeDtypeStruct((2, SUBLANES, D_FF), f32)],
        scratch_shapes=[pltpu.VMEM((L + CONV_ROWS, tc), f32), pltpu.VMEM((L + CONV_ROWS, tc), f32)],
        compiler_params=_params("parallel", "arbitrary"),
    )(up, up, dff, conv_w, conv_w, conv_b, conv_b)


GELU_C = math.sqrt(2.0 / math.pi)
GELU_A = 0.044715


def _gelu(x):
    return 0.5 * x * (1.0 + jnp.tanh(GELU_C * (x + GELU_A * x * x * x)))


def _gelu_grad(x):
    t = jnp.tanh(GELU_C * (x + GELU_A * x * x * x))
    return 0.5 * (1.0 + t) + 0.5 * x * (1.0 - t * t) * GELU_C * (1.0 + 3.0 * GELU_A * x * x)


def _cmul_add(xr, xi, ar, ai, sr, si):
    return xr + ar * sr - ai * si, xi + ar * si + ai * sr


def _s5_project_in(u_ref, bs_ref, s_ref, L, rc):
    for r in range(0, L, rc):
        s_ref[r:r + rc, :] = jnp.dot(u_ref[r:r + rc, :].astype(bf16), bs_ref[...], preferred_element_type=f32)


def _rows8(i):
    return pl.ds(pl.multiple_of(i * SUBLANES, SUBLANES), SUBLANES)


def _repeat_loop(n, step, init):
    rep = max(u for u in (6, 4, 3, 2, 1) if n % u == 0)

    def body(t, carry):
        for u in range(rep):
            carry = step(t * rep + u, carry)
        return carry

    return lax.fori_loop(0, n // rep, body, init)


def _to_segments(src_ref, dst_ref, seg):
    def step(i, c):
        dst_ref[_rows8(i), :] = src_ref[pl.ds(i, SUBLANES, stride=seg), :]
        return c

    _repeat_loop(seg, step, 0)


def _from_segments(src_ref, dst_ref, seg):
    def step(i, c):
        dst_ref[pl.ds(i, SUBLANES, stride=seg), :] = src_ref[_rows8(i), :]
        return c

    _repeat_loop(seg, step, 0)


def _half_tiles(j, seg, reverse):
    h = seg // 2
    return (_rows8(seg - 1 - j), _rows8(h - 1 - j)) if reverse else (_rows8(j), _rows8(j + h))


def _seg_local_scan(s_ref, ar, ai, seg, reverse):
    ns = SLAB_NS

    def step(j, carry):
        tiles = _half_tiles(j, seg, reverse)
        loaded = [(s_ref[rows, 0:ns], s_ref[rows, ns:2 * ns]) for rows in tiles]
        out = []
        for (xr, xi), (cr, ci) in zip(loaded, (carry[0:2], carry[2:4])):
            out += list(_cmul_add(xr, xi, ar, ai, cr, ci))
        for rows, cr, ci in zip(tiles, out[0::2], out[1::2]):
            s_ref[rows, 0:ns] = cr
            s_ref[rows, ns:2 * ns] = ci
        return tuple(out)

    z = jnp.zeros((SUBLANES, ns), f32)
    return _repeat_loop(seg // 2, step, (z, z, z, z))


def _seg_boundaries(finals, ahr, ahi, reverse):
    fxr, fxi, fyr, fyi = finals
    row = lax.broadcasted_iota(jnp.int32, fxr.shape, 0)
    zero = jnp.zeros_like(fxr[0:1, :])
    xr, xi, yr, yi = (jnp.zeros_like(fxr) for _ in range(4))
    prev = None
    for r in (range(SUBLANES - 1, -1, -1) if reverse else range(SUBLANES)):
        if prev is None:
            nxr, nxi = zero, zero
        else:
            nxr, nxi = _cmul_add(fyr[prev:prev + 1, :], fyi[prev:prev + 1, :], ahr, ahi, nyr, nyi)
        nyr, nyi = _cmul_add(fxr[r:r + 1, :], fxi[r:r + 1, :], ahr, ahi, nxr, nxi)
        xr, xi = jnp.where(row == r, nxr, xr), jnp.where(row == r, nxi, xi)
        yr, yi = jnp.where(row == r, nyr, yr), jnp.where(row == r, nyi, yi)
        prev = r
    return (xr, xi), (yr, yi)


def _s5_states(u_ref, bs_ref, pw_ref, up_ref, s_ref, L, rc):
    seg = L // SUBLANES
    h = seg // 2
    ns = SLAB_NS
    _to_segments(u_ref, up_ref, seg)
    _s5_project_in(up_ref, bs_ref, s_ref, L, rc)
    ar, ai = pw_ref[0, 0:1, :], pw_ref[1, 0:1, :]
    finals = _seg_local_scan(s_ref, ar, ai, seg, False)
    enter = _seg_boundaries(finals, pw_ref[0, h - 1:h, :], pw_ref[1, h - 1:h, :], False)

    def fix(j, c):
        pr, pi = pw_ref[0, pl.ds(j, 1), :], pw_ref[1, pl.ds(j, 1), :]
        tiles = _half_tiles(j, seg, False)
        loaded = [(s_ref[rows, 0:ns], s_ref[rows, ns:2 * ns]) for rows in tiles]
        for rows, (xr, xi), (br, bi) in zip(tiles, loaded, enter):
            xr, xi = _cmul_add(xr, xi, pr, pi, br, bi)
            s_ref[rows, 0:ns] = xr
            s_ref[rows, ns:2 * ns] = xi
        return c

    _repeat_loop(h, fix, 0)


def _pw_spec(seg_rows, order):
    if order == "bs":
        return pl.BlockSpec((2, seg_rows, SLAB_NS), lambda b, s: (0, 0, s))
    return pl.BlockSpec((2, seg_rows, SLAB_NS), lambda s, b: (0, 0, s))


def _s5_fwd(p, bs, cs, pw, d_skip, B, L):
    rc = _tile(L, 344)
    seg = L // SUBLANES

    def body(u_ref, bs_ref, cs_ref, pw_ref, d_ref, y_ref, s_ref, up_ref, yp_ref):
        _s5_states(u_ref, bs_ref, pw_ref, up_ref, s_ref, L, rc)
        for r in range(0, L, rc):
            ypre = (jnp.dot(s_ref[r:r + rc, :].astype(bf16), cs_ref[...], preferred_element_type=f32)
                    + d_ref[...] * up_ref[r:r + rc, :])
            yp_ref[r:r + rc, :] = _gelu(ypre)
        _from_segments(yp_ref, y_ref, seg)

    ucol = SEG_U * (D_MODEL // SLAB_CH)
    return pl.pallas_call(
        body, name="s5_fwd", grid=(B, N_SLAB),
        in_specs=[pl.BlockSpec((L, SLAB_CH), lambda b, s: (b, ucol + s)),
                  pl.BlockSpec((None, SLAB_CH, 2 * SLAB_NS), lambda b, s: (s, 0, 0)),
                  pl.BlockSpec((None, 2 * SLAB_NS, SLAB_CH), lambda b, s: (s, 0, 0)),
                  _pw_spec(pw.shape[1], "bs"),
                  pl.BlockSpec((1, SLAB_CH), lambda b, s: (0, s))],
        out_specs=pl.BlockSpec((L, SLAB_CH), lambda b, s: (b, s)),
        out_shape=jax.ShapeDtypeStruct((B * L, D_MODEL), f32),
        scratch_shapes=[pltpu.VMEM((L, 2 * SLAB_NS), f32), pltpu.VMEM((L, SLAB_CH), f32), pltpu.VMEM((L, SLAB_CH), f32)],
        compiler_params=_params("parallel", "parallel"),
    )(p, bs, cs, pw, d_skip)


def _s5_bwd(p, dya0, dp, bs, cs, pw, d_skip, B, L, sums):
    rc = _tile(L, 344)
    ns = SLAB_NS
    seg = L // SUBLANES
    nx = len(sums)

    def body(u_ref, dy_ref, dp_in, bs_ref, cs_ref, pw_ref, d_ref, *rest):
        xin, (du_ref, dbs_ref, dcs_ref, da_ref, dd_ref), xout = rest[:nx], rest[nx:nx + 5], rest[nx + 5:2 * nx + 5]
        s_ref, lam_ref, up_ref, dyp_ref, nat_ref, send, recv = rest[2 * nx + 5:]
        del dp_in
        start, finish = _chip_exchange_steps(xin, xout, send, recv)

        @pl.when((pl.program_id(0) == 0) & (pl.program_id(1) == 0))
        def _():
            start()

        @pl.when(pl.program_id(1) == 0)
        def _():
            dbs_ref[...] = jnp.zeros_like(dbs_ref)
            dcs_ref[...] = jnp.zeros_like(dcs_ref)
            da_ref[...] = jnp.zeros_like(da_ref)
            dd_ref[...] = jnp.zeros_like(dd_ref)

        _s5_states(u_ref, bs_ref, pw_ref, up_ref, s_ref, L, rc)
        _to_segments(dy_ref, dyp_ref, seg)
        for r in range(0, L, rc):
            u = up_ref[r:r + rc, :]
            sb = s_ref[r:r + rc, :].astype(bf16)
            ypre = jnp.dot(sb, cs_ref[...], preferred_element_type=f32) + d_ref[...] * u
            dyp = dyp_ref[r:r + rc, :] * _gelu_grad(ypre)
            dyp_ref[r:r + rc, :] = dyp
            dd_ref[...] += jnp.sum(dyp * u, axis=0, keepdims=True)
            dypb = dyp.astype(bf16)
            dcs_ref[...] += lax.dot_general(sb, dypb, _DIMS["tn"], preferred_element_type=f32)
            lam_ref[r:r + rc, :] = lax.dot_general(dypb, cs_ref[...], _DIMS["nt"], preferred_element_type=f32)

        h = seg // 2
        ar, ai = pw_ref[0, 0:1, :], -pw_ref[1, 0:1, :]
        finals = _seg_local_scan(lam_ref, ar, ai, seg, True)
        enter = _seg_boundaries(finals, pw_ref[0, h - 1:h, :], -pw_ref[1, h - 1:h, :], True)

        def fix(j, acc):
            accr, acci = acc
            pr, pi = pw_ref[0, pl.ds(j, 1), :], -pw_ref[1, pl.ds(j, 1), :]
            tiles = _half_tiles(j, seg, True)
            loaded = [(lam_ref[rows, 0:ns], lam_ref[rows, ns:2 * ns]) for rows in tiles]
            for rows, (xr, xi), (br, bi), t in zip(tiles, loaded, enter, (seg - 1 - j, h - 1 - j)):
                xr, xi = _cmul_add(xr, xi, pr, pi, br, bi)
                lam_ref[rows, 0:ns] = xr
                lam_ref[rows, ns:2 * ns] = xi
                prev = _rows8(jnp.maximum(t - 1, 0))
                live = jnp.where(t > 0, 1.0, 0.0)
                spr = s_ref[prev, 0:ns] * live
                spi = s_ref[prev, ns:2 * ns] * live
                accr, acci = accr + xr * spr + xi * spi, acci + xi * spr - xr * spi
            return accr, acci

        z = jnp.zeros((SUBLANES, ns), f32)
        accr, acci = _repeat_loop(h, fix, (z, z))
        row = lax.broadcasted_iota(jnp.int32, (SUBLANES, ns), 0)
        last = _rows8(seg - 1)
        spr = jnp.where(row == 0, 0.0, pltpu.roll(s_ref[last, 0:ns], 1, 0))
        spi = jnp.where(row == 0, 0.0, pltpu.roll(s_ref[last, ns:2 * ns], 1, 0))
        xr, xi = lam_ref[0:SUBLANES, 0:ns], lam_ref[0:SUBLANES, ns:2 * ns]
        accr = accr + xr * spr + xi * spi
        acci = acci + xi * spr - xr * spi
        da_ref[0:1, :] += jnp.sum(accr, axis=0, keepdims=True)
        da_ref[1:2, :] += jnp.sum(acci, axis=0, keepdims=True)

        for r in range(0, L, rc):
            lamb = lam_ref[r:r + rc, :].astype(bf16)
            dbs_ref[...] += lax.dot_general(up_ref[r:r + rc, :].astype(bf16), lamb, _DIMS["tn"], preferred_element_type=f32)
            nat_ref[r:r + rc, :] = (lax.dot_general(lamb, bs_ref[...], _DIMS["nt"], preferred_element_type=f32)
                                    + d_ref[...] * dyp_ref[r:r + rc, :])
        _from_segments(nat_ref, up_ref, seg)
        du_ref[...] = up_ref[...].astype(du_ref.dtype)

        @pl.when((pl.program_id(0) == N_SLAB - 1) & (pl.program_id(1) == B - 1))
        def _():
            finish()

    ucol = SEG_U * (D_MODEL // SLAB_CH)
    T = B * L
    col = pltpu.VMEM((L, SLAB_CH), f32)
    res = pl.pallas_call(
        body, name="s5_bwd", grid=(N_SLAB, B),
        in_specs=[pl.BlockSpec((L, SLAB_CH), lambda s, b: (b, ucol + s)),
                  pl.BlockSpec((L, SLAB_CH), lambda s, b: (b, s)),
                  ANY,
                  pl.BlockSpec((None, SLAB_CH, 2 * SLAB_NS), lambda s, b: (s, 0, 0)),
                  pl.BlockSpec((None, 2 * SLAB_NS, SLAB_CH), lambda s, b: (s, 0, 0)),
                  _pw_spec(pw.shape[1], "sb"),
                  pl.BlockSpec((1, SLAB_CH), lambda s, b: (0, s))] + [ANY] * nx,
        out_specs=[pl.BlockSpec((None, L, SLAB_CH), lambda s, b: (SEG_U, b, s)),
                   pl.BlockSpec((None, SLAB_CH, 2 * SLAB_NS), lambda s, b: (s, 0, 0)),
                   pl.BlockSpec((None, 2 * SLAB_NS, SLAB_CH), lambda s, b: (s, 0, 0)),
                   pl.BlockSpec((None, 2, SLAB_NS), lambda s, b: (s, 0, 0)),
                   pl.BlockSpec((1, SLAB_CH), lambda s, b: (0, s))] + [ANY] * nx,
        out_shape=[jax.ShapeDtypeStruct((N_SEG, T, D_MODEL), bf16),
                   jax.ShapeDtypeStruct((N_SLAB, SLAB_CH, 2 * SLAB_NS), f32),
                   jax.ShapeDtypeStruct((N_SLAB, 2 * SLAB_NS, SLAB_CH), f32),
                   jax.ShapeDtypeStruct((N_SLAB, 2, SLAB_NS), f32),
                   jax.ShapeDtypeStruct((1, D_MODEL), f32)] + [jax.ShapeDtypeStruct(a.shape, a.dtype) for a in sums],
        scratch_shapes=[pltpu.VMEM((L, 2 * SLAB_NS), f32), pltpu.VMEM((L, 2 * SLAB_NS), f32), col, col, col]
        + _chip_exchange_sems(nx),
        input_output_aliases={2: 0},
        compiler_params=_params("arbitrary", "arbitrary"),
    )(p, dya0, dp, bs, cs, pw, d_skip, *sums)
    return res[:5], res[5:]


def _dotb(a, b, dims="nn"):
    return lax.dot_general(a.astype(bf16), b.astype(bf16), _DIMS[dims], preferred_element_type=f32)


def _tile_scan(x, reverse):
    n, w = x.shape
    v = x.reshape(n // SUBLANES, SUBLANES, w)
    row = lax.broadcasted_iota(jnp.int32, v.shape, 1)
    for k in (1, 2, 4):
        if reverse:
            v = v + jnp.where(row < SUBLANES - k, pltpu.roll(v, SUBLANES - k, 1), 0.0)
        else:
            v = v + jnp.where(row >= k, pltpu.roll(v, k, 1), 0.0)
    p = v.reshape(n // CHUNK, 2, SUBLANES, w)
    lo, hi = p[:, 0], p[:, 1]
    if reverse:
        lo = lo + hi[:, 0:1, :]
    else:
        hi = hi + lo[:, SUBLANES - 1:SUBLANES, :]
    return jnp.stack([lo, hi], axis=1).reshape(n, w)


def _chunk_cumsum(x):
    return _tile_scan(x, False)


def _chunk_rev_cumsum(x):
    return _tile_scan(x, True)


def _chunk_last(x):
    n, w = x.shape
    p = x.reshape(n // CHUNK, CHUNK, w)
    return jnp.broadcast_to(p[:, CHUNK - 1:CHUNK, :], p.shape).reshape(n, w)


def _hgrn_local(q, fl, lb):
    sg = _sigmoid(fl)
    f = lb + (1.0 - lb) * sg
    g = jnp.log(f)
    cum = _chunk_cumsum(g)
    rest = _chunk_last(cum) - cum
    e = jnp.exp(cum)
    em = jnp.exp(-cum)
    eo = jnp.exp(rest)
    k = 1.0 - f
    return sg, f, e, em, eo, q * e, k * em, k * eo, cum + rest


def _chunk_pos(n):
    return lax.broadcasted_iota(jnp.int32, (n, HEAD_DIM), 0) & (CHUNK - 1)


def _hgrn_block_rows(L):
    return _tile(L, 688, CHUNK)


def _hgrn_specs(L, order):
    hb = D_MODEL // HEAD_DIM

    def spec(seg):
        if order == "bh":
            return pl.BlockSpec((L, HEAD_DIM), lambda b, h: (b, seg * hb + h))
        return pl.BlockSpec((L, HEAD_DIM), lambda h, b: (b, seg * hb + h))

    return [spec(SEG_Q), spec(SEG_F), spec(SEG_I), spec(SEG_OG)]


PAIR = 2 * CHUNK
CHUNK_SHIFT = CHUNK.bit_length() - 1


def _pair_steps(L, rb):
    steps = []
    nch = rb // CHUNK
    for r in range(0, L, rb):
        steps += [(r + p * PAIR, PAIR) for p in range(nch // 2)]
        if nch % 2:
            steps.append((r + (nch - 1) * CHUNK, CHUNK))
    return steps


def _pair_flags(rb):
    ci = lax.broadcasted_iota(jnp.int32, (rb, HEAD_DIM), 0) >> CHUNK_SHIFT
    odd = (ci & 1) == 1
    has_next = jnp.logical_and(jnp.logical_not(odd), ci < rb // CHUNK - 1)
    return odd, has_next


def _pair_masks(rb):
    r = lax.broadcasted_iota(jnp.int32, (rb, rb), 0)
    c = lax.broadcasted_iota(jnp.int32, (rb, rb), 1)
    rc, cc = r >> CHUNK_SHIFT, c >> CHUNK_SHIFT
    same = (rc == cc) & (c <= r)
    prev = ((rc & 1) == 1) & (cc == rc - 1)
    return same, prev


def _hgrn_pair_local(q, fl, lb, odd, has_next):
    sg, f, e, em, eo, qt, kt, ko, cend = _hgrn_local(q, fl, lb)
    n = q.shape[0]
    a = jnp.where(odd, pltpu.roll(cend, CHUNK, 0), 0.0)
    z = jnp.where(has_next, pltpu.roll(cend, n - CHUNK, 0), 0.0)
    ea, ez = jnp.exp(a), jnp.exp(z)
    return dict(sg=sg, f=f, e=e, em=em, eo=eo, qt=qt, kt=kt, ko=ko, ea=ea, ez=ez, qs=qt * ea, ks=ko * ez,
                decp=jnp.exp(cend + a + z))


def _pair_scores(qt, kt, ko, same, prev):
    return (jnp.where(same, _dotb(qt, kt, "nt"), 0.0) + jnp.where(prev, _dotb(qt, ko, "nt"), 0.0)).astype(bf16)


def _hgrn_fwd(p, lb, norm_g, B, L):
    rb = _hgrn_block_rows(L)
    steps = _pair_steps(L, rb)
    blocks = [slice(r, r + rb) for r in range(0, L, rb)]

    def body(q_ref, f_ref, v_ref, og_ref, lb_ref, ng_ref, y_ref, qs_s, ks_s, vb_s, decp_s, o_s, o2_s, u_s, sb_s):
        lbv = lb_ref[...]
        ngv = ng_ref[...]
        same, prev = _pair_masks(rb)
        odd, has_next = _pair_flags(rb)

        for rows in blocks:
            t = _hgrn_pair_local(q_ref[rows, :], f_ref[rows, :], lbv, odd, has_next)
            vb = v_ref[rows, :].astype(bf16)
            o_s[rows, :] = _dotb(_pair_scores(t["qt"], t["kt"], t["ko"], same, prev), vb)
            qs_s[rows, :] = t["qs"].astype(bf16)
            ks_s[rows, :] = t["ks"].astype(bf16)
            vb_s[rows, :] = vb
            decp_s[rows, :] = t["decp"]

        for n, (r0, nr) in enumerate(steps):
            u_s[n] = _dotb(vb_s[r0:r0 + nr, :], ks_s[r0:r0 + nr, :], "tn")
        st = jnp.zeros((HEAD_DIM, HEAD_DIM), f32)
        for n, (r0, nr) in enumerate(steps):
            sb_s[n] = st.astype(bf16)
            st = st * decp_s[r0:r0 + 1, :] + u_s[n]
        for n, (r0, nr) in enumerate(steps):
            o2_s[r0:r0 + nr, :] = _dotb(qs_s[r0:r0 + nr, :], sb_s[n], "nt")

        for rows in blocks:
            o = o_s[rows, :] + o2_s[rows, :]
            og = og_ref[rows, :]
            on = o * lax.rsqrt(jnp.mean(o * o, axis=-1, keepdims=True) + EPS) * ngv
            y_ref[rows, :] = (on * og * _sigmoid(og)).astype(y_ref.dtype)

    sb = pltpu.VMEM((L, HEAD_DIM), bf16)
    sf = pltpu.VMEM((L, HEAD_DIM), f32)
    return pl.pallas_call(
        body, name="hgrn_fwd", grid=(B, HEADS),
        in_specs=_hgrn_specs(L, "bh") + [pl.BlockSpec((1, HEAD_DIM), lambda b, h: (0, h)),
                                          pl.BlockSpec((1, HEAD_DIM), lambda b, h: (0, 0))],
        out_specs=pl.BlockSpec((L, HEAD_DIM), lambda b, h: (b, h)),
        out_shape=jax.ShapeDtypeStruct((B * L, D_MODEL), bf16),
        scratch_shapes=[sb, sb, sb, sf, sf, sf, pltpu.VMEM((len(steps), HEAD_DIM, HEAD_DIM), f32),
                        pltpu.VMEM((len(steps), HEAD_DIM, HEAD_DIM), bf16)],
        compiler_params=_params("parallel", "parallel"),
    )(p, p, p, p, lb, norm_g)


def _hgrn_bwd(p, dyb, dp, lb, norm_g, B, L):
    rb = _hgrn_block_rows(L)
    steps = _pair_steps(L, rb)
    blocks = [slice(r, r + rb) for r in range(0, L, rb)]

    def body(q_ref, f_ref, v_ref, og_ref, dy_ref, dp_in, lb_ref, ng_ref, dseg_ref, dlb_ref, dng_ref,
             st_ref, u_s, dsb_s, qt_s, kt_s, ko_s, qs_s, ks_s, vb_s, do_s,
             decp_s, o_s, o2_s, dqt_s, dkt_s, dko_s, dv_s, dv2_s, dqs_s, dks_s, ddecp_s):
        del dp_in
        lbv = lb_ref[...]
        ngv = ng_ref[...]
        same, prev = _pair_masks(rb)
        odd, has_next = _pair_flags(rb)
        pos = _chunk_pos(rb)

        @pl.when(pl.program_id(1) == 0)
        def _():
            dlb_ref[...] = jnp.zeros_like(dlb_ref)

        @pl.when((pl.program_id(0) == 0) & (pl.program_id(1) == 0))
        def _():
            dng_ref[...] = jnp.zeros_like(dng_ref)

        def scores(rows):
            return _pair_scores(qt_s[rows, :], kt_s[rows, :], ko_s[rows, :], same, prev)

        for rows in blocks:
            t = _hgrn_pair_local(q_ref[rows, :], f_ref[rows, :], lbv, odd, has_next)
            for dst, key in ((qt_s, "qt"), (kt_s, "kt"), (ko_s, "ko"), (qs_s, "qs"), (ks_s, "ks")):
                dst[rows, :] = t[key].astype(bf16)
            vb_s[rows, :] = v_ref[rows, :].astype(bf16)
            decp_s[rows, :] = t["decp"]
            o_s[rows, :] = _dotb(scores(rows), vb_s[rows, :])

        for n, (r0, nr) in enumerate(steps):
            u_s[n] = _dotb(vb_s[r0:r0 + nr, :], ks_s[r0:r0 + nr, :], "tn")
        st = jnp.zeros((HEAD_DIM, HEAD_DIM), f32)
        for n, (r0, nr) in enumerate(steps):
            st_ref[n] = st
            st = st * decp_s[r0:r0 + 1, :] + u_s[n]
        for n, (r0, nr) in enumerate(steps):
            o2_s[r0:r0 + nr, :] = _dotb(qs_s[r0:r0 + nr, :], st_ref[n], "nt")

        dng = jnp.zeros((1, HEAD_DIM), f32)
        for rows in blocks:
            o = o_s[rows, :] + o2_s[rows, :]
            og = og_ref[rows, :]
            dy = dy_ref[rows, :]
            rs = lax.rsqrt(jnp.mean(o * o, axis=-1, keepdims=True) + EPS)
            xn = o * rs
            so = _sigmoid(og)
            dseg_ref[SEG_OG, rows, :] = (dy * xn * ngv * so * (1.0 + og * (1.0 - so))).astype(dseg_ref.dtype)
            don = dy * og * so
            dng = dng + jnp.sum(don * xn, axis=0, keepdims=True)
            dxo = don * ngv
            do = (rs * (dxo - xn * jnp.mean(dxo * xn, axis=-1, keepdims=True))).astype(bf16)
            do_s[rows, :] = do
            dpf = _dotb(do, vb_s[rows, :], "nt")
            dp1 = jnp.where(same, dpf, 0.0).astype(bf16)
            dp2 = jnp.where(prev, dpf, 0.0).astype(bf16)
            dqt_s[rows, :] = _dotb(dp1, kt_s[rows, :]) + _dotb(dp2, ko_s[rows, :])
            dkt_s[rows, :] = _dotb(dp1, qt_s[rows, :], "tn")
            dko_s[rows, :] = _dotb(dp2, qt_s[rows, :], "tn")
            dv_s[rows, :] = _dotb(scores(rows), do, "tn")
        dng_ref[...] += dng

        for n, (r0, nr) in enumerate(steps):
            u_s[n] = _dotb(do_s[r0:r0 + nr, :], qs_s[r0:r0 + nr, :], "tn")
        dst = jnp.zeros((HEAD_DIM, HEAD_DIM), f32)
        for n, (r0, nr) in reversed(list(enumerate(steps))):
            dsb_s[n] = dst.astype(bf16)
            ddecp_s[r0:r0 + nr, :] = jnp.broadcast_to(jnp.sum(dst * st_ref[n], axis=0, keepdims=True), (nr, HEAD_DIM))
            dst = dst * decp_s[r0:r0 + 1, :] + u_s[n]
        for n, (r0, nr) in enumerate(steps):
            rows = slice(r0, r0 + nr)
            dqs_s[rows, :] = _dotb(do_s[rows, :], st_ref[n])
            dv2_s[rows, :] = _dotb(ks_s[rows, :], dsb_s[n], "nt")
            dks_s[rows, :] = _dotb(vb_s[rows, :], dsb_s[n])

        def chunk_sum(x):
            return _chunk_last(_chunk_cumsum(x))

        dlb = jnp.zeros((1, HEAD_DIM), f32)
        for rows in blocks:
            t = _hgrn_pair_local(q_ref[rows, :], f_ref[rows, :], lbv, odd, has_next)
            dqs, dks = dqs_s[rows, :], dks_s[rows, :]
            dqt = dqt_s[rows, :] + dqs * t["ea"]
            dko = dko_s[rows, :] + dks * t["ez"]
            dkt = dkt_s[rows, :]
            dko_ko = dko * t["ko"]
            dcum = dqt * t["qt"] - dkt * t["kt"] - dko_ko
            from_next = pltpu.roll(chunk_sum(jnp.where(odd, dqs * t["qs"], 0.0)), rb - CHUNK, 0)
            from_prev = pltpu.roll(chunk_sum(jnp.where(has_next, dks * t["ks"], 0.0)), CHUNK, 0)
            d_end = (chunk_sum(dko_ko) + jnp.where(has_next, from_next, 0.0) + jnp.where(odd, from_prev, 0.0)
                     + ddecp_s[rows, :] * t["decp"])
            dcum = dcum + jnp.where(pos == CHUNK - 1, d_end, 0.0)
            df = _chunk_rev_cumsum(dcum) / t["f"] - (dkt * t["em"] + dko * t["eo"])
            dlb = dlb + jnp.sum(df * (1.0 - t["sg"]), axis=0, keepdims=True)
            dseg_ref[SEG_Q, rows, :] = (dqt * t["e"]).astype(dseg_ref.dtype)
            dseg_ref[SEG_F, rows, :] = (df * (1.0 - lbv) * t["sg"] * (1.0 - t["sg"])).astype(dseg_ref.dtype)
            dseg_ref[SEG_I, rows, :] = (dv_s[rows, :] + dv2_s[rows, :]).astype(dseg_ref.dtype)
        dlb_ref[...] += dlb

    T = B * L
    ns = len(steps)
    sb = pltpu.VMEM((L, HEAD_DIM), bf16)
    sf = pltpu.VMEM((L, HEAD_DIM), f32)
    return pl.pallas_call(
        body, name="hgrn_bwd", grid=(HEADS, B),
        in_specs=_hgrn_specs(L, "hb") + [pl.BlockSpec((L, HEAD_DIM), lambda h, b: (b, h)), ANY,
                                          pl.BlockSpec((1, HEAD_DIM), lambda h, b: (0, h)),
                                          pl.BlockSpec((1, HEAD_DIM), lambda h, b: (0, 0))],
        out_specs=[pl.BlockSpec((4, L, HEAD_DIM), lambda h, b: (0, b, h)),
                   pl.BlockSpec((1, HEAD_DIM), lambda h, b: (0, h)),
                   pl.BlockSpec((1, HEAD_DIM), lambda h, b: (0, 0))],
        out_shape=[jax.ShapeDtypeStruct((N_SEG, T, D_MODEL), bf16), jax.ShapeDtypeStruct((1, D_MODEL), f32),
                   jax.ShapeDtypeStruct((1, HEAD_DIM), f32)],
        scratch_shapes=[pltpu.VMEM((ns, HEAD_DIM, HEAD_DIM), f32), pltpu.VMEM((ns, HEAD_DIM, HEAD_DIM), f32),
                        pltpu.VMEM((ns, HEAD_DIM, HEAD_DIM), bf16)] + [sb] * 7 + [sf] * 11,
        input_output_aliases={5: 0},
        compiler_params=_params("arbitrary", "arbitrary"),
    )(p, p, p, p, dyb, dp, lb, norm_g)


def _dz1_norm(dp, w_in_phys, h0, g, dh1):
    _, T, Dm = dp.shape
    tm = _tile(T, 688)
    return _mm_rmsnorm_bwd("dz1", dp, w_in_phys, (T // tm, 1, N_SEG),
                           pl.BlockSpec((None, tm, Dm), lambda i, j, k: (k, i, 0)),
                           pl.BlockSpec((Dm, Dm), lambda i, j, k: (0, k)), h0, g, dh1)


def _dz2_norm(dup, w_up, h1, g, dh2):
    _, T, _ = dup.shape
    tm = _tile(T, 688)
    tk = D_FF // 2
    return _mm_rmsnorm_bwd("dz2", dup, w_up, (T // tm, 1, 4),
                           pl.BlockSpec((None, tm, tk), lambda i, j, k: (k // 2, i, k % 2)),
                           pl.BlockSpec((D_MODEL, tk), lambda i, j, k: (0, k)), h1, g, dh2)


def _dw_in(z1, dp):
    _, T, Dm = dp.shape
    tk = _tile(T, 1376)
    return _mm("dw_in", z1, dp, "tn", (1, N_SEG, T // tk),
               pl.BlockSpec((tk, Dm), lambda i, j, k: (k, 0)),
               pl.BlockSpec((None, tk, Dm), lambda i, j, k: (j, k, 0)),
               jax.ShapeDtypeStruct((N_SEG, Dm, Dm), f32),
               pl.BlockSpec((None, Dm, Dm), lambda i, j, k: (j, 0, 0)), (Dm, Dm))


def _dw_up(z2, dup):
    _, T, _ = dup.shape
    tn = D_FF // 2
    tk = _tile(T, 688)
    return _mm("dw_up", z2, dup, "tn", (1, N_CHIPS, T // tk),
               pl.BlockSpec((tk, D_MODEL), lambda i, j, k: (k, 0)),
               pl.BlockSpec((None, tk, tn), lambda i, j, k: (j // 2, k, j % 2)),
               jax.ShapeDtypeStruct((N_CHIPS, D_MODEL, tn), f32),
               pl.BlockSpec((None, D_MODEL, tn), lambda i, j, k: (j, 0, 0)), (D_MODEL, tn))


def _place():
    x, y, c = lax.axis_index("x"), lax.axis_index("y"), lax.axis_index("c")
    chips = [(1 - x, y), (x, 1 - y), (1 - x, 1 - y)]
    return x, y, c, chips


def _allgather_chips(arrs):
    n = len(arrs)

    def body(*refs):
        ins, outs = refs[:n], refs[n:2 * n]
        send, recv, local = refs[2 * n:]
        x, y, c, chips = _place()
        me = 2 * x + y

        def copy(a, k, slot):
            px, py = chips[k]
            return pltpu.make_async_remote_copy(src_ref=ins[a], dst_ref=outs[a].at[slot], send_sem=send.at[3 * a + k],
                                                recv_sem=recv.at[3 * a + k], device_id=(px, py, c), device_id_type=MESH)

        for a in range(n):
            pltpu.make_async_copy(ins[a], outs[a].at[me], local.at[a]).start()
            for k in range(3):
                copy(a, k, me).start()
        for a in range(n):
            for k, (px, py) in enumerate(chips):
                copy(a, k, 2 * px + py).wait_recv()
        for a in range(n):
            pltpu.make_async_copy(ins[a], outs[a].at[me], local.at[a]).wait()
            for k in range(3):
                copy(a, k, me).wait_send()

    return pl.pallas_call(
        body, name="allgather_chips", in_specs=[ANY] * n, out_specs=[ANY] * n,
        out_shape=[jax.ShapeDtypeStruct((N_CHIPS,) + a.shape, a.dtype) for a in arrs],
        scratch_shapes=[pltpu.SemaphoreType.DMA((3 * n,)), pltpu.SemaphoreType.DMA((3 * n,)), pltpu.SemaphoreType.DMA((n,))],
    )(*arrs)


def _allgather_split(arrs):
    n = len(arrs)

    def body(*refs):
        start, finish = _gather_split_steps(refs[:n], refs[n:2 * n], *refs[2 * n:])
        start()
        finish()

    return pl.pallas_call(
        body, name="allgather_split", in_specs=[ANY] * n, out_specs=[ANY] * n,
        out_shape=[jax.ShapeDtypeStruct((N_CHIPS,) + a.shape, a.dtype) for a in arrs],
        scratch_shapes=_gather_split_sems(n),
    )(*arrs)


def _gather_split_sems(n):
    return [pltpu.SemaphoreType.DMA((3 * n,)) for _ in range(4)]


def _gather_split_steps(ins, outs, send, recv, fsend, frecv):
    n = len(ins)

    def place():
        x, y, c, chips = _place()
        return x, y, c, chips, 2 * x + y

    def half(a, core):
        rh = ins[a].shape[0] // 2
        return pl.ds(core * rh, rh)

    def copy(a, k, slot):
        x, y, c, chips, _ = place()
        px, py = chips[k]
        return pltpu.make_async_remote_copy(src_ref=ins[a].at[half(a, c), :], dst_ref=outs[a].at[slot, half(a, c), :],
                                            send_sem=send.at[3 * a + k], recv_sem=recv.at[3 * a + k],
                                            device_id=(px, py, c), device_id_type=MESH)

    def forward(a, k, core):
        x, y, c, chips, _ = place()
        px, py = chips[k]
        rows = outs[a].at[2 * px + py, half(a, core), :]
        return pltpu.make_async_remote_copy(src_ref=rows, dst_ref=rows, send_sem=fsend.at[3 * a + k],
                                            recv_sem=frecv.at[3 * a + k], device_id=(x, y, 1 - c), device_id_type=MESH)

    def start():
        me = place()[4]
        for a in range(n):
            for k in range(3):
                copy(a, k, me).start()

    def finish():
        x, y, c, chips, me = place()
        for a in range(n):
            for k, (px, py) in enumerate(chips):
                copy(a, k, 2 * px + py).wait_recv()
                forward(a, k, c).start()
        for a in range(n):
            for k in range(3):
                forward(a, k, 1 - c).wait_recv()
        for a in range(n):
            for k in range(3):
                copy(a, k, me).wait_send()
                forward(a, k, c).wait_send()

    return start, finish


def _in_proj_gather(z1, w_in, shards):
    n = len(shards)
    T, K = z1.shape
    N = w_in.shape[1]
    tm = _tile(T, 1032)
    tn = 1024
    grid = (T // tm, N // tn)

    def body(a_ref, b_ref, *rest):
        ins, o_ref, outs, sems = rest[:n], rest[n], rest[n + 1:2 * n + 1], rest[2 * n + 1:]
        start, finish = _gather_split_steps(ins, outs, *sems)
        i, j = pl.program_id(0), pl.program_id(1)

        @pl.when((i == 0) & (j == 0))
        def _():
            start()

        o_ref[...] = jnp.dot(a_ref[...], b_ref[...], preferred_element_type=f32)

        @pl.when((i == grid[0] - 1) & (j == grid[1] - 1))
        def _():
            finish()

    res = pl.pallas_call(
        body, name="in_proj", grid=grid,
        in_specs=[pl.BlockSpec((tm, K), lambda i, j: (i, 0)), pl.BlockSpec((K, tn), lambda i, j: (0, j))] + [ANY] * n,
        out_specs=[pl.BlockSpec((tm, tn), lambda i, j: (i, j))] + [ANY] * n,
        out_shape=[jax.ShapeDtypeStruct((T, N), f32)] + [jax.ShapeDtypeStruct((N_CHIPS,) + a.shape, a.dtype) for a in shards],
        scratch_shapes=_gather_split_sems(n),
        compiler_params=_params("arbitrary", "arbitrary"),
    )(z1, w_in, *shards)
    return res[0], res[1:]


def _sibling_halves(parts, name="sibling_halves"):
    n = len(parts)

    def body(*refs):
        ins, outs = refs[:n], refs[n:2 * n]
        send, recv = refs[2 * n:]
        x, y, c, _ = _place()

        def copy(a):
            rh = ins[a].shape[1] // 2
            return pltpu.make_async_remote_copy(src_ref=ins[a].at[:, pl.ds((1 - c) * rh, rh), :], dst_ref=outs[a],
                                                send_sem=send.at[a], recv_sem=recv.at[a], device_id=(x, y, 1 - c),
                                                device_id_type=MESH)

        for a in range(n):
            copy(a).start()
        for a in range(n):
            copy(a).wait_recv()
        for a in range(n):
            copy(a).wait_send()

    return pl.pallas_call(
        body, name=name, in_specs=[ANY] * n, out_specs=[ANY] * n,
        out_shape=[jax.ShapeDtypeStruct((a.shape[0], a.shape[1] // 2, a.shape[2]), a.dtype) for a in parts],
        scratch_shapes=[pltpu.SemaphoreType.DMA((n,)), pltpu.SemaphoreType.DMA((n,))],
    )(*parts)


def _add_own_half(name, part, got, core):
    nchip, R, C = part.shape
    rh = R // 2
    tr = _tile(rh, 512, 2 * SUBLANES)
    nt = rh // tr

    def body(core_ref, a_ref, b_ref, o_ref):
        del core_ref
        o_ref[...] = (a_ref[...] + b_ref[...]).astype(o_ref.dtype)

    return pl.pallas_call(
        body, name=name,
        grid_spec=pltpu.PrefetchScalarGridSpec(
            num_scalar_prefetch=1, grid=(nchip, nt),
            in_specs=[pl.BlockSpec((None, tr, C), lambda j, i, core_ref: (j, core_ref[0] * nt + i, 0)),
                      pl.BlockSpec((None, tr, C), lambda j, i, core_ref: (j, i, 0))],
            out_specs=pl.BlockSpec((None, tr, C), lambda j, i, core_ref: (j, i, 0))),
        out_shape=jax.ShapeDtypeStruct((nchip, rh, C), bf16), compiler_params=_params("parallel", "parallel"),
    )(core, part, got)


def _add_own_half_w_in(part, got, core):
    _, R, C = part.shape
    rh = R // 2
    tr = _tile(rh, 512, 2 * SUBLANES)
    nt = rh // tr
    tn = 256
    per_seg = C // tn
    per_chip = IN_COLS // N_CHIPS // tn

    def src(j):
        return ((j // per_seg + N_SEG - 1) % N_SEG, j % per_seg)

    def body(core_ref, a_ref, b_ref, o_ref):
        del core_ref
        o_ref[...] = (a_ref[...] + b_ref[...]).astype(o_ref.dtype)

    return pl.pallas_call(
        body, name="add_half_w_in",
        grid_spec=pltpu.PrefetchScalarGridSpec(
            num_scalar_prefetch=1, grid=(IN_COLS // tn, nt),
            in_specs=[pl.BlockSpec((None, tr, tn), lambda j, i, core_ref: (src(j)[0], core_ref[0] * nt + i, src(j)[1])),
                      pl.BlockSpec((None, tr, tn), lambda j, i, core_ref: (src(j)[0], i, src(j)[1]))],
            out_specs=pl.BlockSpec((None, tr, tn), lambda j, i, core_ref: (j // per_chip, i, j % per_chip))),
        out_shape=jax.ShapeDtypeStruct((N_CHIPS, rh, IN_COLS // N_CHIPS), bf16), compiler_params=_params("parallel", "parallel"),
    )(core, part, got)


def _chip_exchange(sums):
    n = len(sums)

    def body(*refs):
        start, finish = _chip_exchange_steps(refs[:n], refs[n:2 * n], *refs[2 * n:])
        start()
        finish()

    return pl.pallas_call(
        body, name="chip_exchange", in_specs=[ANY] * n, out_specs=[ANY] * n,
        out_shape=[jax.ShapeDtypeStruct(a.shape, a.dtype) for a in sums],
        scratch_shapes=_chip_exchange_sems(n),
    )(*sums)


def _chip_exchange_sems(n):
    return [pltpu.SemaphoreType.DMA((3 * n,)), pltpu.SemaphoreType.DMA((3 * n,))]


def _chip_exchange_steps(ins, outs, send, recv):
    n = len(ins)

    def copy(a, k, own_slot):
        x, y, c, chips = _place()
        px, py = chips[k]
        slot = 2 * x + y if own_slot else 2 * px + py
        return pltpu.make_async_remote_copy(src_ref=ins[a].at[2 * px + py], dst_ref=outs[a].at[slot], send_sem=send.at[3 * a + k],
                                            recv_sem=recv.at[3 * a + k], device_id=(px, py, c), device_id_type=MESH)

    def start():
        for a in range(n):
            for k in range(3):
                copy(a, k, True).start()

    def finish():
        for a in range(n):
            for k in range(3):
                copy(a, k, False).wait_recv()
        for a in range(n):
            for k in range(3):
                copy(a, k, True).wait_send()

    return start, finish


def _sum_chips(name, slots, sums, where):
    nchip, rh, C = slots.shape
    tr = _tile(rh, 512, 2 * SUBLANES)
    nt = rh // tr

    def body(where_ref, own_ref, s1_ref, s2_ref, s3_ref, o_ref):
        me = where_ref[0]
        by_dist = [r[...].astype(f32) for r in (own_ref, s1_ref, s2_ref, s3_ref)]
        acc = None
        for j in range(nchip):
            d = me ^ j
            term = jnp.where(d == 0, by_dist[0], jnp.where(d == 1, by_dist[1], jnp.where(d == 2, by_dist[2], by_dist[3])))
            acc = term if acc is None else acc + term
        o_ref[...] = acc

    def other(d):
        return pl.BlockSpec((None, tr, C), lambda i, w: (w[0] ^ d, i, 0))

    return pl.pallas_call(
        body, name=name,
        grid_spec=pltpu.PrefetchScalarGridSpec(
            num_scalar_prefetch=1, grid=(nt,),
            in_specs=[other(0), other(1), other(2), other(3)],
            out_specs=pl.BlockSpec((tr, C), lambda i, w: (w[1] * nt + i, 0))),
        out_shape=jax.ShapeDtypeStruct((2 * rh, C), f32), compiler_params=_params("parallel"),
    )(where, sums, slots, slots, slots)


def _sum_slots(name, slots):
    ns, R, C = slots.shape
    tr = _tile(R, 256)

    def body(s_ref, o_ref):
        acc = s_ref[0]
        for j in range(1, ns):
            acc = acc + s_ref[j]
        o_ref[...] = acc

    return pl.pallas_call(
        body, name=name, grid=(R // tr,), in_specs=[pl.BlockSpec((ns, tr, C), lambda i: (0, i, 0))],
        out_specs=pl.BlockSpec((tr, C), lambda i: (i, 0)), out_shape=jax.ShapeDtypeStruct((R, C), f32),
        compiler_params=_params("parallel"),
    )(slots)


def _sibling_join(fulls):
    n = len(fulls)

    def body(*refs):
        ins, outs = refs[:n], refs[n:2 * n]
        send, recv = refs[2 * n:]
        x, y, c, _ = _place()

        def copy(a, core):
            rh = ins[a].shape[0] // 2
            rows = pl.ds(core * rh, rh)
            return pltpu.make_async_remote_copy(src_ref=ins[a].at[rows, :], dst_ref=outs[a].at[rows, :], send_sem=send.at[a],
                                                recv_sem=recv.at[a], device_id=(x, y, 1 - c), device_id_type=MESH)

        for a in range(n):
            copy(a, c).start()
        for a in range(n):
            copy(a, 1 - c).wait_recv()
        for a in range(n):
            copy(a, c).wait_send()

    return pl.pallas_call(
        body, name="sibling_join", in_specs=[ANY] * n, out_specs=[ANY] * n,
        out_shape=[jax.ShapeDtypeStruct(a.shape, a.dtype) for a in fulls],
        scratch_shapes=[pltpu.SemaphoreType.DMA((n,)), pltpu.SemaphoreType.DMA((n,))],
        input_output_aliases={a: a for a in range(n)},
    )(*fulls)


def _allgather_devices(v):
    def body(v_ref, out_ref, send, recv):
        x, y, c, chips = _place()
        me, sibling = (x, y, c), (x, y, 1 - c)

        def slot(px, py, pc):
            return out_ref.at[4 * px + 2 * py + pc]

        def copy(k, block, to, src=None):
            return pltpu.make_async_remote_copy(src_ref=slot(*block) if src is None else src, dst_ref=slot(*block),
                                                send_sem=send.at[k], recv_sem=recv.at[k], device_id=to, device_id_type=MESH)

        first = [copy(0, me, sibling, src=v_ref)] + [copy(1 + j, me, (*chip, c), src=v_ref) for j, chip in enumerate(chips)]
        for cp in first:
            cp.start()
        passed = [copy(4 + j, (*chip, c), sibling) for j, chip in enumerate(chips)]
        for j, chip in enumerate(chips):
            copy(1 + j, (*chip, c), me).wait_recv()
            passed[j].start()
        copy(0, sibling, me).wait_recv()
        for j, chip in enumerate(chips):
            copy(4 + j, (*chip, 1 - c), me).wait_recv()
        for cp in first + passed:
            cp.wait_send()

    return pl.pallas_call(
        body, name="allgather_devices", in_specs=[ANY], out_specs=ANY,
        out_shape=jax.ShapeDtypeStruct((N_DEV,) + v.shape, v.dtype),
        scratch_shapes=[pltpu.SemaphoreType.DMA((N_DEV - 1,)), pltpu.SemaphoreType.DMA((N_DEV - 1,))],
    )(v)


def _adamw(name, w, g, m, v):
    R, C = w.shape
    tr = _tile(R, 256)
    c1 = 1.0 / (1.0 - ADAM_B1 ** ADAM_STEP)
    c2 = 1.0 / (1.0 - ADAM_B2 ** ADAM_STEP)

    def body(w_ref, g_ref, m_ref, v_ref, d_ref, nm_ref, nv_ref):
        gv = g_ref[...]
        nm = ADAM_B1 * m_ref[...] + (1.0 - ADAM_B1) * gv
        nv = ADAM_B2 * v_ref[...] + (1.0 - ADAM_B2) * gv * gv
        d_ref[...] = -ADAM_LR * ((nm * c1) / (jnp.sqrt(nv * c2) + ADAM_EPS) + ADAM_WD * w_ref[...])
        nm_ref[...] = nm
        nv_ref[...] = nv

    row = pl.BlockSpec((tr, C), lambda i: (i, 0))
    sh = jax.ShapeDtypeStruct((R, C), f32)
    return pl.pallas_call(body, name=name, grid=(R // tr,), in_specs=[row] * 4, out_specs=[row] * 3,
                          out_shape=[sh, sh, sh], compiler_params=_params("parallel"))(w, g, m, v)


def _adamw_update(w, g, m, v):
    c1 = 1.0 / (1.0 - ADAM_B1 ** ADAM_STEP)
    c2 = 1.0 / (1.0 - ADAM_B2 ** ADAM_STEP)
    nm = ADAM_B1 * m + (1.0 - ADAM_B1) * g
    nv = ADAM_B2 * v + (1.0 - ADAM_B2) * g * g
    return -ADAM_LR * ((nm * c1) / (jnp.sqrt(nv * c2) + ADAM_EPS) + ADAM_WD * w), nm, nv


def _adamw_many(ws, gs, ms, vs):
    n = len(ws)

    def body(*refs):
        ins, outs = refs[:4 * n], refs[4 * n:]
        for a in range(n):
            d, nm, nv = _adamw_update(ins[a][...], ins[n + a][...], ins[2 * n + a][...], ins[3 * n + a][...])
            outs[a][...] = d
            outs[n + a][...] = nm
            outs[2 * n + a][...] = nv

    shapes = [jax.ShapeDtypeStruct(a.shape, f32) for a in ws]
    return pl.pallas_call(body, name="adamw_small", out_shape=shapes * 3)(*ws, *gs, *ms, *vs)


def _zoh_parts(lr, li, log_dt):
    dt = jnp.exp(log_dt)
    mag = jnp.exp(lr * dt)
    c, s = jnp.cos(li * dt), jnp.sin(li * dt)
    ab_re, ab_im = mag * c, mag * s
    den = lr * lr + li * li
    nr = ab_re - 1.0
    coef_re = (nr * lr + ab_im * li) / den
    coef_im = (ab_im * lr - nr * li) / den
    return dt, mag, c, s, ab_re, ab_im, den, nr, coef_re, coef_im


def _zoh_fwd(lr, li, log_dt, b_re, b_im):
    def body(lr_ref, li_ref, ld_ref, br_ref, bi_ref, ar_ref, ai_ref, bbr_ref, bbi_ref):
        _, _, _, _, ab_re, ab_im, _, _, coef_re, coef_im = _zoh_parts(lr_ref[...], li_ref[...], ld_ref[...])
        ar_ref[...] = ab_re
        ai_ref[...] = ab_im
        bbr_ref[...] = coef_re * br_ref[...] - coef_im * bi_ref[...]
        bbi_ref[...] = coef_re * bi_ref[...] + coef_im * br_ref[...]

    col = jax.ShapeDtypeStruct(lr.shape, f32)
    mat = jax.ShapeDtypeStruct(b_re.shape, f32)
    return pl.pallas_call(body, name="zoh_fwd", out_shape=[col, col, mat, mat])(lr, li, log_dt, b_re, b_im)


def _zoh_bwd(lr, li, log_dt, b_re, b_im, d_ar, d_ai, d_bbr, d_bbi):
    n = lr.shape[1]
    groups = n // SSM_STATE

    def body(lr_ref, li_ref, ld_ref, br_ref, bi_ref, dar_ref, dai_ref, dbbr_ref, dbbi_ref,
             dlr_ref, dli_ref, dld_ref, dbr_ref, dbi_ref):
        lr_, li_ = lr_ref[...], li_ref[...]
        dt, mag, c, s, _, ab_im, den, nr, coef_re, coef_im = _zoh_parts(lr_, li_, ld_ref[...])
        br, bi, dbbr, dbbi = br_ref[...], bi_ref[...], dbbr_ref[...], dbbi_ref[...]
        dbr_ref[...] = coef_re * dbbr + coef_im * dbbi
        dbi_ref[...] = coef_re * dbbi - coef_im * dbbr
        d_cr = jnp.sum(dbbr * br + dbbi * bi, axis=0, keepdims=True)
        d_ci = jnp.sum(dbbi * br - dbbr * bi, axis=0, keepdims=True)
        d_nr = (d_cr * lr_ - d_ci * li_) / den
        d_abi = dai_ref[...] + (d_cr * li_ + d_ci * lr_) / den
        d_abr = dar_ref[...] + d_nr
        d_den = -(d_cr * coef_re + d_ci * coef_im) / den
        d_lr = (d_cr * nr + d_ci * ab_im) / den + 2.0 * lr_ * d_den
        d_li = (d_cr * ab_im - d_ci * nr) / den + 2.0 * li_ * d_den
        d_theta = mag * (d_abi * c - d_abr * s)
        d_arg = mag * (d_abr * c + d_abi * s)
        dlr_ref[...] = d_lr + d_arg * dt
        dli_ref[...] = d_li + d_theta * dt
        d_dt = d_arg * lr_ + d_theta * li_
        member = (lax.broadcasted_iota(jnp.int32, (n, groups), 0) >> (SSM_STATE.bit_length() - 1)
                  == lax.broadcasted_iota(jnp.int32, (n, groups), 1)).astype(f32)
        dld_ref[...] = jnp.dot(d_dt * dt, member, preferred_element_type=f32, precision=lax.Precision.HIGHEST)

    col = jax.ShapeDtypeStruct(lr.shape, f32)
    mat = jax.ShapeDtypeStruct(b_re.shape, f32)
    return pl.pallas_call(body, name="zoh_bwd", out_shape=[col, col, jax.ShapeDtypeStruct((1, groups), f32), mat, mat])(
        lr, li, log_dt, b_re, b_im, d_ar, d_ai, d_bbr, d_bbi)


def _lower_bound_fwd(logits):
    def body(x_ref, o_ref):
        x = x_ref[...]
        e = jnp.exp(x - jnp.max(x, axis=0, keepdims=True))
        o_ref[...] = e / jnp.sum(e, axis=0, keepdims=True)

    return pl.pallas_call(body, name="lower_bound_fwd", out_shape=jax.ShapeDtypeStruct(logits.shape, f32))(logits)


def _lower_bound_bwd(sm, d_lb):
    def body(sm_ref, d_ref, o_ref):
        smv = sm_ref[...]
        row = lax.broadcasted_iota(jnp.int32, smv.shape, 0)
        sm0 = smv[0:1, :]
        o_ref[...] = sm0 * d_ref[...] * (jnp.where(row == 0, 1.0, 0.0) - smv)

    return pl.pallas_call(body, name="lower_bound_bwd", out_shape=jax.ShapeDtypeStruct(sm.shape, f32))(sm, d_lb)


def _s5_tables(ab_re, ab_im, bb_re, bb_im, c_re, c_im, seg):
    eye = jnp.eye(SLAB_GROUPS, dtype=f32)

    def blk_in(bb):
        return jnp.einsum("hsgp,gk->sghkp", bb.reshape(SSM_GROUP, N_SLAB, SLAB_GROUPS, SSM_STATE), eye).reshape(
            N_SLAB, SLAB_CH, SLAB_NS)

    def blk_out(cc):
        return jnp.einsum("sghp,gk->skpgh", cc.reshape(N_SLAB, SLAB_GROUPS, SSM_GROUP, SSM_STATE), eye).reshape(
            N_SLAB, SLAB_NS, SLAB_CH)

    bs = jnp.concatenate([blk_in(bb_re), blk_in(bb_im)], axis=2).astype(bf16)
    cs = jnp.concatenate([blk_out(c_re), blk_out(-c_im)], axis=1).astype(bf16)
    n = SSM_GROUPS * SSM_STATE
    pw = _power_table(jnp.stack([ab_re.reshape(1, n), ab_im.reshape(1, n)]), -(-seg // SUBLANES))
    return bs, cs, pw


def _power_table(ab, tiles):
    n = ab.shape[2]

    def body(a_ref, o_ref):
        row = lax.broadcasted_iota(jnp.int32, (SUBLANES, n), 0)
        ar, ai = a_ref[0], a_ref[1]
        tr, ti = jnp.broadcast_to(ar, (SUBLANES, n)), jnp.broadcast_to(ai, (SUBLANES, n))
        pr, pi = ar, ai
        for r in range(1, SUBLANES):
            pr, pi = pr * ar - pi * ai, pr * ai + pi * ar
            tr = jnp.where(row == r, pr, tr)
            ti = jnp.where(row == r, pi, ti)
        o_ref[0, 0:SUBLANES, :] = tr
        o_ref[1, 0:SUBLANES, :] = ti

        def step(j, carry):
            cr, ci = carry
            cr, ci = cr * pr - ci * pi, cr * pi + ci * pr
            o_ref[0, _rows8(j), :] = cr
            o_ref[1, _rows8(j), :] = ci
            return cr, ci

        lax.fori_loop(1, tiles, step, (tr, ti))

    return pl.pallas_call(body, name="power_table", out_shape=jax.ShapeDtypeStruct((2, SUBLANES * tiles, n), f32))(ab)


def _s5_table_grads(dbs, dcs, da):
    eye = jnp.eye(SLAB_GROUPS, dtype=f32)
    d6 = dbs.reshape(N_SLAB, SLAB_GROUPS, SSM_GROUP, 2, SLAB_GROUPS, SSM_STATE)
    dbb = jnp.einsum("sghrkp,gk->rhsgp", d6, eye).reshape(2, SSM_GROUP, SSM_GROUPS * SSM_STATE)
    c6 = dcs.reshape(N_SLAB, 2, SLAB_GROUPS, SSM_STATE, SLAB_GROUPS, SSM_GROUP)
    dcc = jnp.einsum("srkpgh,gk->rsghp", c6, eye).reshape(2, SSM_GROUPS, SSM_GROUP, SSM_STATE)
    dab = da.transpose(1, 0, 2).reshape(2, SSM_GROUPS, SSM_STATE)
    return dab[0], dab[1], dbb[0], dbb[1], dcc[0], -dcc[1]


SMALL = ["mix_norm_g", "ssm_lambda_re", "ssm_lambda_im", "ssm_log_dt", "ssm_b_re", "ssm_b_im", "ssm_c_re", "ssm_c_im",
         "ssm_d", "hgrn_lb_logits", "hgrn_norm_g", "ffn_norm_g", "conv_b", "final_norm_g"]
SHARDED_SMALL = ["meta_tokens", "conv_w"]
BIG = ["w_in", "ssm_w_glu", "w_ssm_proj", "w_hgrn_proj", "w_out", "w_up", "w_down"]
WEIGHTS = ['meta_tokens', 'mix_norm_g', 'w_in', 'ssm_lambda_re', 'ssm_lambda_im', 'ssm_log_dt', 'ssm_b_re', 'ssm_b_im',
           'ssm_c_re', 'ssm_c_im', 'ssm_d', 'ssm_w_glu', 'w_ssm_proj', 'hgrn_lb_logits', 'hgrn_norm_g', 'w_hgrn_proj',
           'w_out', 'ffn_norm_g', 'w_up', 'conv_w', 'conv_b', 'w_down', 'final_norm_g']


LATER = [k for k in BIG if k != "w_in"]


def _full_weights(gathered, shards, chip):
    Dm = D_MODEL
    g = {k: lax.dynamic_update_slice(gathered[k], shards[k][None], (chip, 0, 0)) for k in gathered}
    full = {}
    for k, v in g.items():
        if k == "w_in":
            full[k] = jnp.roll(v.transpose(1, 0, 2).reshape(Dm, IN_COLS), -Dm, axis=1)
        elif k == "w_up":
            full[k] = v.transpose(1, 0, 2).reshape(Dm, 2 * D_FF)
        else:
            full[k] = v.reshape(-1, Dm)
    return full


def _local_grads(x, tgt, meta, w, full, shards, chip, core):
    B, S, Dm = x.shape
    L = S + N_META
    T = B * L
    h0 = jnp.concatenate([jnp.broadcast_to(meta[None], (B, N_META, Dm)), x], axis=1).reshape(T, Dm)

    lb_all = _lower_bound_fwd(w["hgrn_lb_logits"])
    lb = lb_all[0:1]
    gp = SSM_GROUPS * SSM_STATE
    zoh_in = (w["ssm_lambda_re"].reshape(1, gp), w["ssm_lambda_im"].reshape(1, gp),
              jnp.repeat(w["ssm_log_dt"].reshape(SSM_GROUPS, 1), SSM_STATE, axis=1).reshape(1, gp),
              w["ssm_b_re"].reshape(gp, SSM_GROUP).T, w["ssm_b_im"].reshape(gp, SSM_GROUP).T)
    ab_re, ab_im, bb_re, bb_im = _zoh_fwd(*zoh_in)
    bs, cs, pw = _s5_tables(ab_re, ab_im, bb_re, bb_im, w["ssm_c_re"][0], w["ssm_c_im"][0], L // SUBLANES)

    z1 = _rmsnorm_fwd("mix_norm", h0, w["mix_norm_g"])
    p, gathered = _in_proj_gather(z1, full["w_in"], [shards[k] for k in LATER])
    full = {**full, **_full_weights(dict(zip(LATER, gathered)), shards, chip)}
    ya0 = _s5_fwd(p, bs, cs, pw, w["ssm_d"], B, L)
    gl, ya = _glu_proj_fwd(ya0, full["ssm_w_glu"])
    yb = _hgrn_fwd(p, lb, w["hgrn_norm_g"], B, L)
    pa, pb, merged = _proj_merge_fwd(ya, yb, full["w_ssm_proj"], full["w_hgrn_proj"], p)
    h1, z2 = _out_proj_norm(merged, full["w_out"], h0, w["ffn_norm_g"])
    up = _mm_rows("up_proj", z2, full["w_up"], "nn", f32, D_FF // 2)
    ff = _conv_fwd(up, full["conv_w"], w["conv_b"], B, L)
    h2 = _mm_rows("down_proj", ff, full["w_down"], "nn", f32, 1024, res=h1, tk=D_FF // 2)

    tgt_rows = jnp.pad(tgt, ((0, 0), (N_META, 0), (0, 0))).reshape(T, Dm)
    dh2, loss, d_final_g = _final_loss(h2, tgt_rows, w["final_norm_g"].reshape(1, Dm), L)

    dff = _mm_rows("d_ff", dh2, full["w_down"], "nt", f32, D_FF // 2)
    g_w_down = _mm_wgrad("dw_down", ff, dh2, tn=512)
    dup, dconv = _conv_bwd(up, dff, full["conv_w"], w["conv_b"], B, L)
    g_w_up = _dw_up(z2, dup)
    dh1, d_ffn_g = _dz2_norm(dup, full["w_up"], h1, w["ffn_norm_g"], dh2)

    g_w_out = _mm_wgrad("dw_out", merged, dh1)
    dpa, dpb, dp = _merge_bwd_fused(dh1, full["w_out"], p, pa, pb)
    dgl, dya0_direct = _glu_bwd_fused(dpa, full["w_ssm_proj"], ya0, gl)
    g_w_ssm_proj = _mm_wgrad("dw_ssm_proj", ya, dpa)
    dyb = _mm_rows("d_yb", dpb, full["w_hgrn_proj"], "nt", f32, 1024)
    g_w_hgrn_proj = _mm_wgrad("dw_hgrn_proj", yb, dpb)
    dp, d_lb, d_hgrn_g = _hgrn_bwd(p, dyb, dp, lb, w["hgrn_norm_g"], B, L)
    dya0 = _mm_rows("d_ya0", dgl, full["ssm_w_glu"], "nt", f32, 1024, res=dya0_direct)
    g_w_glu = _mm_wgrad("dw_glu", ya0, dgl)
    parts = {
        "ssm_w_glu": g_w_glu.reshape(N_CHIPS, Dm // N_CHIPS, Dm), "w_ssm_proj": g_w_ssm_proj.reshape(N_CHIPS, Dm // N_CHIPS, Dm),
        "w_hgrn_proj": g_w_hgrn_proj.reshape(N_CHIPS, Dm // N_CHIPS, Dm), "w_out": g_w_out.reshape(N_CHIPS, Dm // N_CHIPS, Dm),
        "w_up": g_w_up, "w_down": g_w_down.reshape(N_CHIPS, D_FF // N_CHIPS, Dm),
    }
    got = _sibling_halves([parts[k] for k in LATER])
    sums = {k: _add_own_half("add_half_" + k, parts[k], gt, core) for k, gt in zip(LATER, got)}
    (dp, dbs, dcs, da, d_skip), slots_later = _s5_bwd(p, dya0, dp, bs, cs, pw, w["ssm_d"], B, L, [sums[k] for k in LATER])
    slots = dict(zip(LATER, slots_later))
    g_w_in = _dw_in(z1, dp)
    dh0, d_mix_g = _dz1_norm(dp, full["w_in"], h0, w["mix_norm_g"], dh1)

    dh0 = dh0.reshape(B, L, Dm)
    grad_x = dh0[:, N_META:]
    d_meta = _meta_grad(dh0[:, :N_META])

    d_ab_re, d_ab_im, d_bb_re, d_bb_im, d_c_re, d_c_im = _s5_table_grads(dbs, dcs, da)
    d_lr, d_li, d_log_dt, d_b_re, d_b_im = _zoh_bwd(*zoh_in, d_ab_re.reshape(1, gp), d_ab_im.reshape(1, gp), d_bb_re, d_bb_im)
    gps = (SSM_GROUPS, SSM_STATE)
    d_lr, d_li, d_log_dt = d_lr.reshape(gps), d_li.reshape(gps), d_log_dt.reshape(SSM_GROUPS)
    d_b_re, d_b_im = d_b_re.T.reshape(gps + (SSM_GROUP,)), d_b_im.T.reshape(gps + (SSM_GROUP,))
    d_logits = _lower_bound_bwd(lb_all, d_lb)
    small = {
        "meta_tokens": d_meta, "mix_norm_g": d_mix_g, "ssm_lambda_re": d_lr[None], "ssm_lambda_im": d_li[None],
        "ssm_log_dt": d_log_dt[None], "ssm_b_re": d_b_re[None], "ssm_b_im": d_b_im[None], "ssm_c_re": d_c_re[None],
        "ssm_c_im": d_c_im[None], "ssm_d": d_skip, "hgrn_lb_logits": d_logits, "hgrn_norm_g": d_hgrn_g,
        "ffn_norm_g": d_ffn_g, "conv_w": dconv[:, 0:3, :].transpose(1, 0, 2).reshape(3, 2 * D_FF),
        "conv_b": dconv[:, 3, :].reshape(1, 2 * D_FF), "final_norm_g": d_final_g.reshape(Dm),
    }
    sums["w_in"] = _add_own_half_w_in(g_w_in, _sibling_halves([g_w_in], "sibling_halves_w_in")[0], core)
    slots["w_in"] = _chip_exchange([sums["w_in"]])[0]
    return loss, grad_x, sums, slots, small


PACK_ROWS = 256


def _pack(parts):
    flat = jnp.concatenate([parts[k].reshape(-1) for k in parts])
    n = flat.shape[0]
    rows = -(-n // (PACK_ROWS * LANES)) * PACK_ROWS
    flat = jnp.pad(flat, (0, rows * LANES - n))
    return flat.reshape(rows, LANES)


def _unpack(packed, like):
    flat = packed.reshape(-1)
    out, o = {}, 0
    for k, ref in like.items():
        n = math.prod(ref.shape)
        out[k] = flat[o:o + n].reshape(ref.shape)
        o += n
    return out


def kernel(x, meta_tokens, mix_norm_g, w_in, ssm_lambda_re, ssm_lambda_im, ssm_log_dt, ssm_b_re, ssm_b_im, ssm_c_re, ssm_c_im, ssm_d, ssm_w_glu, w_ssm_proj, hgrn_lb_logits, hgrn_norm_g, w_hgrn_proj, w_out, ffn_norm_g, w_up, conv_w, conv_b, w_down, final_norm_g, loss_target, m_meta_tokens, m_mix_norm_g, m_w_in, m_ssm_lambda_re, m_ssm_lambda_im, m_ssm_log_dt, m_ssm_b_re, m_ssm_b_im, m_ssm_c_re, m_ssm_c_im, m_ssm_d, m_ssm_w_glu, m_w_ssm_proj, m_hgrn_lb_logits, m_hgrn_norm_g, m_w_hgrn_proj, m_w_out, m_ffn_norm_g, m_w_up, m_conv_w, m_conv_b, m_w_down, m_final_norm_g, v_meta_tokens, v_mix_norm_g, v_w_in, v_ssm_lambda_re, v_ssm_lambda_im, v_ssm_log_dt, v_ssm_b_re, v_ssm_b_im, v_ssm_c_re, v_ssm_c_im, v_ssm_d, v_ssm_w_glu, v_w_ssm_proj, v_hgrn_lb_logits, v_hgrn_norm_g, v_w_hgrn_proj, v_w_out, v_ffn_norm_g, v_w_up, v_conv_w, v_conv_b, v_w_down, v_final_norm_g):
    args = dict(locals())
    w = {k: args[k] for k in WEIGHTS}
    mom = {k: args["m_" + k] for k in WEIGHTS}
    var = {k: args["v_" + k] for k in WEIGHTS}
    Dm = D_MODEL
    cx, cy, cc = lax.axis_index("x"), lax.axis_index("y"), lax.axis_index("c")
    chip = 2 * cx + cy

    shards = {k: w[k][0].astype(bf16) for k in BIG}
    g_meta, g_cw = _allgather_chips([w["meta_tokens"], w["conv_w"][0]])
    full = _full_weights({"w_in": _allgather_split([shards["w_in"]])[0]}, shards, chip)
    full["conv_w"] = g_cw.transpose(1, 0, 2).reshape(3, 2 * D_FF)
    meta_full = g_meta.transpose(1, 0, 2).reshape(N_META, Dm)

    core = cc.reshape(1).astype(jnp.int32)
    loss_part, grad_x, sums, slots, small = _local_grads(x, loss_target, meta_full, w, full, shards, chip, core)

    where = jnp.stack([chip, cc]).astype(jnp.int32)
    fulls = [_sum_chips("sum_chips_" + k, slots[k], sums[k], where) for k in BIG]
    g_big = dict(zip(BIG, _sibling_join(fulls)))

    small_all = dict(small)
    small_all["loss"] = loss_part[0, 0:1]
    packed = _pack(small_all)
    slots_dev = lax.dynamic_update_slice(_allgather_devices(packed), packed[None], (2 * chip + cc, 0, 0))
    reduced = _unpack(_sum_slots("sum_devices", slots_dev), small_all)
    loss = reduced.pop("loss")[0]
    mcols = Dm // N_CHIPS
    ccols = 2 * D_FF // N_CHIPS
    grads = {k: reduced[k] for k in SMALL}
    grads["meta_tokens"] = lax.dynamic_slice(reduced["meta_tokens"], (0, chip * mcols), (N_META, mcols))
    grads["conv_w"] = lax.dynamic_slice(reduced["conv_w"], (0, chip * ccols), (3, ccols))[None]
    for k in BIG:
        grads[k] = g_big[k][None]

    delta, new_m, new_v = {}, {}, {}
    for k in BIG:
        shp = w[k].shape
        d, nm, nv = _adamw("adamw_" + k, w[k][0], grads[k][0], mom[k][0], var[k][0])
        delta[k], new_m[k], new_v[k] = d.reshape(shp), nm.reshape(shp), nv.reshape(shp)
    rest = SMALL + SHARDED_SMALL

    def flat2(a):
        return a.reshape(-1, a.shape[-1])

    outs = _adamw_many(*[[flat2(t[k]) for k in rest] for t in (w, grads, mom, var)])
    n = len(rest)
    for j, dst in enumerate((delta, new_m, new_v)):
        dst.update({k: o.reshape(w[k].shape) for k, o in zip(rest, outs[j * n:(j + 1) * n])})

    return (loss, grad_x, *[grads[k].reshape(w[k].shape) for k in WEIGHTS], *[delta[k] for k in WEIGHTS],
            *[new_m[k] for k in WEIGHTS], *[new_v[k] for k in WEIGHTS])
```

```python
import math

import jax
import jax.numpy as jnp
from jax import lax
from jax.experimental import pallas as pl
from jax.experimental.pallas import tpu as pltpu

f32 = jnp.float32
bf16 = jnp.bfloat16

D_MODEL = 1024
N_META = 16
SSM_GROUP = 16
SSM_GROUPS = 64
SSM_STATE = 64
SLAB_GROUPS = 8
N_SLAB = SSM_GROUPS // SLAB_GROUPS
SLAB_CH = SLAB_GROUPS * SSM_GROUP
SLAB_NS = SLAB_GROUPS * SSM_STATE
HEADS = 8
HEAD_DIM = 128
CHUNK = 16
D_FF = 2816
IN_COLS = 7168
EPS = 1e-6
SUBLANES = 8
LANES = 128
N_CHIPS = 4
N_DEV = 8
ADAM_LR, ADAM_B1, ADAM_B2, ADAM_EPS, ADAM_WD, ADAM_STEP = 0.001, 0.9, 0.999, 1e-08, 0.01, 10
MESH = pl.DeviceIdType.MESH
ANY = pl.BlockSpec(memory_space=pl.ANY)

SEG_Q, SEG_F, SEG_I, SEG_OG, SEG_GA, SEG_GB, SEG_U = range(7)
N_SEG = 7


def _tile(n, target, mult=SUBLANES):
    best = None
    for d in range(mult, min(n, target) + 1, mult):
        if n % d == 0:
            best = d
    return n if best is None else best


def _params(*sem):
    return pltpu.CompilerParams(dimension_semantics=sem)


def _sigmoid(x):
    return 1.0 / (1.0 + jnp.exp(-x))


_DIMS = {"nn": (((1,), (0,)), ((), ())), "nt": (((1,), (1,)), ((), ())), "tn": (((0,), (0,)), ((), ()))}


def _mm(name, a, b, dims, grid, a_spec, b_spec, out_shape, out_spec, acc_shape, res=None, res_spec=None):
    nk = grid[2]
    dn = _DIMS[dims]

    def body(*refs):
        if res is None:
            a_ref, b_ref, o_ref, acc = refs
        else:
            a_ref, b_ref, r_ref, o_ref, acc = refs
        k = pl.program_id(2)

        @pl.when(k == 0)
        def _():
            acc[...] = jnp.zeros_like(acc)

        acc[...] += lax.dot_general(a_ref[...].astype(bf16), b_ref[...].astype(bf16), dn, preferred_element_type=f32)

        @pl.when(k == nk - 1)
        def _():
            r = acc[...]
            if res is not None:
                r = r + r_ref[...]
            o_ref[...] = r.astype(o_ref.dtype)

    ins = [a, b] + ([] if res is None else [res])
    specs = [a_spec, b_spec] + ([] if res is None else [res_spec])
    return pl.pallas_call(
        body, name=name, grid=grid, in_specs=specs, out_specs=out_spec, out_shape=out_shape,
        scratch_shapes=[pltpu.VMEM(acc_shape, f32)],
        compiler_params=_params("parallel", "parallel", "arbitrary"),
    )(*ins)


def _mm_rows(name, a, w, dims, out_dtype, tn, res=None, tk=None):
    T, K = a.shape
    N = w.shape[1] if dims == "nn" else w.shape[0]
    tm = _tile(T, 1032)
    tk = K if tk is None else tk
    grid = (T // tm, N // tn, K // tk)
    a_spec = pl.BlockSpec((tm, tk), lambda i, j, k: (i, k))
    if dims == "nn":
        b_spec = pl.BlockSpec((tk, tn), lambda i, j, k: (k, j))
    else:
        b_spec = pl.BlockSpec((tn, tk), lambda i, j, k: (j, k))
    o_spec = pl.BlockSpec((tm, tn), lambda i, j, k: (i, j))
    return _mm(name, a, w, dims, grid, a_spec, b_spec, jax.ShapeDtypeStruct((T, N), out_dtype), o_spec, (tm, tn),
               res=res, res_spec=None if res is None else o_spec)


def _mm_fused(name, pairs, dims, extras, epilogue, outs, rows=(), tm_target=688):
    T, K = pairs[0][0].shape
    N = pairs[0][1].shape[1] if dims == "nn" else pairs[0][1].shape[0]
    tm = _tile(T, tm_target)
    tn = N
    grid = (T // tm, N // tn)
    npair, nex = len(pairs), len(extras) + len(rows)
    dn = _DIMS[dims]

    def body(*refs):
        ab = refs[:2 * npair]
        ex = refs[2 * npair:2 * npair + nex]
        o_refs = refs[2 * npair + nex:]
        accs = [lax.dot_general(ab[2 * q][...].astype(bf16), ab[2 * q + 1][...].astype(bf16), dn, preferred_element_type=f32)
                for q in range(npair)]
        vals = epilogue(accs, [e[...] for e in ex])
        for o_ref, v in zip(o_refs, vals):
            if isinstance(v, (list, tuple)):
                for s_, vs in enumerate(v):
                    o_ref[s_] = vs.astype(o_ref.dtype)
            else:
                o_ref[...] = v.astype(o_ref.dtype)

    ins, specs = [], []
    for a, w in pairs:
        ins += [a, w]
        specs.append(pl.BlockSpec((tm, K), lambda i, j: (i, 0)))
        specs.append(pl.BlockSpec((K, tn), lambda i, j: (0, j)) if dims == "nn" else pl.BlockSpec((tn, K), lambda i, j: (j, 0)))
    for arr, off in extras:
        ins.append(arr)
        specs.append(pl.BlockSpec((tm, tn), lambda i, j, off=off: (i, off + j)))
    for arr in rows:
        ins.append(arr)
        specs.append(pl.BlockSpec((1, tn), lambda i, j: (0, j)))
    shapes, ospecs = [], []
    for o in outs:
        if isinstance(o, tuple):
            dt, nseg, total, blk = o
            shapes.append(jax.ShapeDtypeStruct((total, T, N), dt))
            ospecs.append(pl.BlockSpec((nseg, tm, tn), lambda i, j, blk=blk: (blk, i, j)))
        else:
            shapes.append(jax.ShapeDtypeStruct((T, N), o))
            ospecs.append(pl.BlockSpec((tm, tn), lambda i, j: (i, j)))
    return pl.pallas_call(body, name=name, grid=grid, in_specs=specs, out_specs=ospecs, out_shape=shapes,
                          compiler_params=_params("parallel", "parallel"))(*ins)


def _glu_proj_fwd(ya0, w_glu):
    def epi(accs, tiles):
        return accs[0], tiles[0] * _sigmoid(accs[0])

    return _mm_fused("glu_proj", [(ya0, w_glu)], "nn", [(ya0, 0)], epi, [f32, bf16])


def _proj_merge_fwd(ya, yb, w_sp, w_hp, p):
    def epi(accs, tiles):
        return accs[0], accs[1], _sigmoid(tiles[0]) * accs[0] + _sigmoid(tiles[1]) * accs[1]

    return _mm_fused("proj_merge", [(ya, w_sp), (yb, w_hp)], "nn", [(p, SEG_GA), (p, SEG_GB)], epi, [f32, f32, bf16])


def _merge_bwd_fused(dh1, w_out, p, pa, pb):
    def epi(accs, tiles):
        d = accs[0]
        sa, sb = _sigmoid(tiles[0]), _sigmoid(tiles[1])
        return d * sa, d * sb, [d * tiles[2] * sa * (1.0 - sa), d * tiles[3] * sb * (1.0 - sb)]

    return _mm_fused("d_merged", [(dh1, w_out)], "nt", [(p, SEG_GA), (p, SEG_GB), (pa, 0), (pb, 0)], epi,
                     [bf16, bf16, (bf16, 2, N_SEG, SEG_GA // 2)], tm_target=344)


def _out_proj_norm(merged, w_out, h0, g):
    def epi(accs, tiles):
        h1 = tiles[0] + accs[0]
        r = lax.rsqrt(jnp.mean(h1 * h1, axis=-1, keepdims=True) + EPS)
        return h1, h1 * r * tiles[1]

    return _mm_fused("out_proj", [(merged, w_out)], "nn", [(h0, 0)], epi, [f32, bf16], rows=[g])


def _mm_rmsnorm_bwd(name, a, b, grid, a_spec, b_spec, x, g, dres):
    T, Dm = x.shape
    tm = T // grid[0]
    nk = grid[2]

    def body(a_ref, b_ref, x_ref, g_ref, dres_ref, dx_ref, dg_ref, acc):
        i, k = pl.program_id(0), pl.program_id(2)

        @pl.when(k == 0)
        def _():
            acc[...] = jnp.zeros_like(acc)

        @pl.when((i == 0) & (k == 0))
        def _():
            dg_ref[...] = jnp.zeros_like(dg_ref)

        acc[...] += lax.dot_general(a_ref[...].astype(bf16), b_ref[...].astype(bf16), _DIMS["nt"], preferred_element_type=f32)

        @pl.when(k == nk - 1)
        def _():
            xv = x_ref[...]
            r = lax.rsqrt(jnp.mean(xv * xv, axis=-1, keepdims=True) + EPS)
            xn = xv * r
            dzv = acc[...]
            dzg = dzv * g_ref[...]
            dx_ref[...] = dres_ref[...] + r * (dzg - xn * jnp.mean(dzg * xn, axis=-1, keepdims=True))
            dg_ref[...] += jnp.sum(dzv * xn, axis=0, keepdims=True)

    row = pl.BlockSpec((tm, Dm), lambda i, j, k: (i, 0))
    par = pl.BlockSpec((1, Dm), lambda i, j, k: (0, 0))
    return pl.pallas_call(
        body, name=name, grid=grid, in_specs=[a_spec, b_spec, row, par, row], out_specs=[row, par],
        out_shape=[jax.ShapeDtypeStruct((T, Dm), f32), jax.ShapeDtypeStruct((1, Dm), f32)],
        scratch_shapes=[pltpu.VMEM((tm, Dm), f32)],
        compiler_params=_params("arbitrary", "arbitrary", "arbitrary"),
    )(a, b, x, g, dres)


def _glu_bwd_fused(dpa, w_sp, ya0, gl):
    def epi(accs, tiles):
        d = accs[0]
        s = _sigmoid(tiles[1])
        return d * tiles[0] * s * (1.0 - s), d * s

    return _mm_fused("d_ya", [(dpa, w_sp)], "nt", [(ya0, 0), (gl, 0)], epi, [bf16, f32])


def _mm_wgrad(name, a, g, tn=None):
    T, K = a.shape
    N = g.shape[1]
    tk = _tile(T, 1376 if K <= D_MODEL else 688)
    tn = N if tn is None else tn
    grid = (1, N // tn, T // tk)
    a_spec = pl.BlockSpec((tk, K), lambda i, j, k: (k, 0))
    g_spec = pl.BlockSpec((tk, tn), lambda i, j, k: (k, j))
    o_spec = pl.BlockSpec((K, tn), lambda i, j, k: (0, j))
    return _mm(name, a, g, "tn", grid, a_spec, g_spec, jax.ShapeDtypeStruct((K, N), f32), o_spec, (K, tn))


def _rmsnorm_fwd(name, x, g):
    T, Dm = x.shape
    tr = _tile(T, 688)

    def body(x_ref, g_ref, z_ref):
        xv = x_ref[...]
        r = lax.rsqrt(jnp.mean(xv * xv, axis=-1, keepdims=True) + EPS)
        z_ref[...] = (xv * r * g_ref[...]).astype(z_ref.dtype)

    return pl.pallas_call(
        body, name=name, grid=(T // tr,),
        in_specs=[pl.BlockSpec((tr, Dm), lambda i: (i, 0)), pl.BlockSpec((1, Dm), lambda i: (0, 0))],
        out_specs=pl.BlockSpec((tr, Dm), lambda i: (i, 0)),
        out_shape=jax.ShapeDtypeStruct((T, Dm), bf16), compiler_params=_params("parallel"),
    )(x, g)


def _final_loss(h2, tgt, g, L):
    T, Dm = h2.shape
    tr = _tile(L, 688)
    per_seq = L // tr

    def body(h_ref, t_ref, g_ref, dh_ref, loss_ref, dg_ref):
        pos = (pl.program_id(0) % per_seq) * tr + lax.broadcasted_iota(jnp.int32, (tr, 1), 0)
        live = jnp.where(pos >= N_META, 1.0, 0.0)
        hv = h_ref[...]
        r = lax.rsqrt(jnp.mean(hv * hv, axis=-1, keepdims=True) + EPS)
        xn = hv * r
        gv = g_ref[...]
        err = (xn * gv - t_ref[...]) * live
        dy = err * (1.0 / Dm)
        dyg = dy * gv
        dh_ref[...] = r * (dyg - xn * jnp.mean(dyg * xn, axis=-1, keepdims=True))

        @pl.when(pl.program_id(0) == 0)
        def _():
            dg_ref[...] = jnp.zeros_like(dg_ref)
            loss_ref[...] = jnp.zeros_like(loss_ref)

        dg_ref[...] += jnp.sum(dy * xn, axis=0, keepdims=True)
        loss_ref[...] += jnp.sum(err * err) * (0.5 / Dm)

    row = pl.BlockSpec((tr, Dm), lambda i: (i, 0))
    par = pl.BlockSpec((1, Dm), lambda i: (0, 0))
    return pl.pallas_call(
        body, name="final_loss", grid=(T // tr,), in_specs=[row, row, par],
        out_specs=[row, pl.BlockSpec((1, LANES), lambda i: (0, 0)), par],
        out_shape=[jax.ShapeDtypeStruct((T, Dm), f32), jax.ShapeDtypeStruct((1, LANES), f32), jax.ShapeDtypeStruct((1, Dm), f32)],
        compiler_params=_params("arbitrary"),
    )(h2, tgt, g)


def _meta_grad(dh0_meta):
    B = dh0_meta.shape[0]

    def body(d_ref, o_ref):
        acc = d_ref[0]
        for b in range(1, B):
            acc = acc + d_ref[b]
        o_ref[...] = acc

    return pl.pallas_call(body, name="meta_grad", out_shape=jax.ShapeDtypeStruct(dh0_meta.shape[1:], f32))(dh0_meta)


def _shift_down(x, k, row):
    return jnp.where(row >= k, pltpu.roll(x, k, 0), 0.0)


def _conv_fwd(up, conv_w, conv_b, B, L):
    tc = 256
    nt = D_FF // tc

    def body(xa_ref, xb_ref, wa_ref, wb_ref, ba_ref, bb_ref, o_ref):
        head = 2 * SUBLANES
        row = lax.broadcasted_iota(jnp.int32, (head, tc), 0)

        def gated(conv):
            a = conv(xa_ref, wa_ref, ba_ref)
            b = conv(xb_ref, wb_ref, bb_ref)
            return (a * _sigmoid(a) * b).astype(o_ref.dtype)

        def conv_rolled(x_ref, w_ref, b_ref):
            x = x_ref[...]
            return b_ref[...] + w_ref[0:1, :] * pltpu.roll(x, 2, 0) + w_ref[1:2, :] * pltpu.roll(x, 1, 0) + w_ref[2:3, :] * x

        def conv_head(x_ref, w_ref, b_ref):
            x = x_ref[0:head, :]
            return (b_ref[...] + w_ref[0:1, :] * _shift_down(x, 2, row) + w_ref[1:2, :] * _shift_down(x, 1, row)
                    + w_ref[2:3, :] * x)

        o_ref[...] = gated(conv_rolled)
        o_ref[0:head, :] = gated(conv_head)

    return pl.pallas_call(
        body, name="conv_fwd", grid=(B, nt),
        in_specs=[pl.BlockSpec((L, tc), lambda b, j: (b, j)), pl.BlockSpec((L, tc), lambda b, j: (b, j + nt)),
                  pl.BlockSpec((3, tc), lambda b, j: (0, j)), pl.BlockSpec((3, tc), lambda b, j: (0, j + nt)),
                  pl.BlockSpec((1, tc), lambda b, j: (0, j)), pl.BlockSpec((1, tc), lambda b, j: (0, j + nt))],
        out_specs=pl.BlockSpec((L, tc), lambda b, j: (b, j)),
        out_shape=jax.ShapeDtypeStruct((B * L, D_FF), bf16), compiler_params=_params("parallel", "parallel"),
    )(up, up, conv_w, conv_w, conv_b, conv_b)


CONV_ROWS = 2 * SUBLANES


def _rows16(i):
    return pl.ds(pl.multiple_of(i * CONV_ROWS, CONV_ROWS), CONV_ROWS)


def _conv_taps(x_ref, i, row):
    x = x_ref[_rows16(i), :]
    live = jnp.where(i > 0, 1.0, 0.0)
    r0 = jnp.maximum(i * CONV_ROWS, 2)
    p1 = x_ref[pl.ds(r0 - 1, 1), :] * live
    p2 = x_ref[pl.ds(r0 - 2, 1), :] * live
    x1 = jnp.where(row == 0, p1, pltpu.roll(x, 1, 0))
    x2 = jnp.where(row == 0, p2, jnp.where(row == 1, p1, pltpu.roll(x, 2, 0)))
    return x, x1, x2


def _conv_bwd(up, dff, conv_w, conv_b, B, L):
    tc = 256
    nt = D_FF // tc
    n = L // CONV_ROWS

    def body(xa_ref, xb_ref, d_ref, wa_ref, wb_ref, ba_ref, bb_ref, dup_ref, dw_ref, ga_ref, gb_ref):
        row = lax.broadcasted_iota(jnp.int32, (CONV_ROWS, tc), 0)

        @pl.when(pl.program_id(1) == 0)
        def _():
            dw_ref[...] = jnp.zeros_like(dw_ref)

        zero_tail = jnp.zeros((CONV_ROWS, tc), f32)
        ga_ref[L:L + CONV_ROWS, :] = zero_tail
        gb_ref[L:L + CONV_ROWS, :] = zero_tail

        def fold(v):
            return v[0:SUBLANES, :] + v[SUBLANES:CONV_ROWS, :]

        def step(i, acc):
            taps_a = _conv_taps(xa_ref, i, row)
            taps_b = _conv_taps(xb_ref, i, row)
            a = ba_ref[...] + wa_ref[0:1, :] * taps_a[2] + wa_ref[1:2, :] * taps_a[1] + wa_ref[2:3, :] * taps_a[0]
            b = bb_ref[...] + wb_ref[0:1, :] * taps_b[2] + wb_ref[1:2, :] * taps_b[1] + wb_ref[2:3, :] * taps_b[0]
            s = _sigmoid(a)
            d = d_ref[_rows16(i), :]
            g_a = d * b * s * (1.0 + a * (1.0 - s))
            g_b = d * a * s
            ga_ref[_rows16(i), :] = g_a
            gb_ref[_rows16(i), :] = g_b
            new = []
            for g, (x, x1, x2) in ((g_a, taps_a), (g_b, taps_b)):
                new += [fold(g * x2), fold(g * x1), fold(g * x), fold(g)]
            return tuple(o + v for o, v in zip(acc, new))

        z = jnp.zeros((SUBLANES, tc), f32)
        acc = _repeat_loop(n, step, (z,) * 8)
        for h in range(2):
            for t in range(4):
                dw_ref[h, t:t + 1, :] += jnp.sum(acc[4 * h + t], axis=0, keepdims=True)

        def back(i, c):
            for h, (g_ref, w_ref) in enumerate(((ga_ref, wa_ref), (gb_ref, wb_ref))):
                g = g_ref[_rows16(i), :]
                n1 = g_ref[pl.ds(i * CONV_ROWS + CONV_ROWS, 1), :]
                n2 = g_ref[pl.ds(i * CONV_ROWS + CONV_ROWS + 1, 1), :]
                u1 = jnp.where(row == CONV_ROWS - 1, n1, pltpu.roll(g, CONV_ROWS - 1, 0))
                u2 = jnp.where(row == CONV_ROWS - 1, n2, jnp.where(row == CONV_ROWS - 2, n1, pltpu.roll(g, CONV_ROWS - 2, 0)))
                dup_ref[h, _rows16(i), :] = (w_ref[2:3, :] * g + w_ref[1:2, :] * u1 + w_ref[0:1, :] * u2).astype(dup_ref.dtype)
            return c

        _repeat_loop(n, back, 0)

    return pl.pallas_call(
        body, name="conv_bwd", grid=(nt, B),
        in_specs=[pl.BlockSpec((L, tc), lambda j, b: (b, j)), pl.BlockSpec((L, tc), lambda j, b: (b, j + nt)),
                  pl.BlockSpec((L, tc), lambda j, b: (b, j)),
                  pl.BlockSpec((3, tc), lambda j, b: (0, j)), pl.BlockSpec((3, tc), lambda j, b: (0, j + nt)),
                  pl.BlockSpec((1, tc), lambda j, b: (0, j)), pl.BlockSpec((1, tc), lambda j, b: (0, j + nt))],
        out_specs=[pl.BlockSpec((2, L, tc), lambda j, b: (0, b, j)), pl.BlockSpec((2, SUBLANES, tc), lambda j, b: (0, 0, j))],
        out_shape=[jax.ShapeDtypeStruct((2, B * L, D_FF), bf16), jax.ShapeDtypeStruct((2, SUBLANES, D_FF), f32)],
        scratch_shapes=[pltpu.VMEM((L + CONV_ROWS, tc), f32), pltpu.VMEM((L + CONV_ROWS, tc), f32)],
        compiler_params=_params("parallel", "arbitrary"),
    )(up, up, dff, conv_w, conv_w, conv_b, conv_b)


GELU_C = math.sqrt(2.0 / math.pi)
GELU_A = 0.044715


def _gelu(x):
    return 0.5 * x * (1.0 + jnp.tanh(GELU_C * (x + GELU_A * x * x * x)))


def _gelu_grad(x):
    t = jnp.tanh(GELU_C * (x + GELU_A * x * x * x))
    return 0.5 * (1.0 + t) + 0.5 * x * (1.0 - t * t) * GELU_C * (1.0 + 3.0 * GELU_A * x * x)


def _cmul_add(xr, xi, ar, ai, sr, si):
    return xr + ar * sr - ai * si, xi + ar * si + ai * sr


def _s5_project_in(u_ref, bs_ref, s_ref, L, rc):
    for r in range(0, L, rc):
        s_ref[r:r + rc, :] = jnp.dot(u_ref[r:r + rc, :].astype(bf16), bs_ref[...], preferred_element_type=f32)


def _rows8(i):
    return pl.ds(pl.multiple_of(i * SUBLANES, SUBLANES), SUBLANES)


def _repeat_loop(n, step, init):
    rep = max(u for u in (6, 4, 3, 2, 1) if n % u == 0)

    def body(t, carry):
        for u in range(rep):
            carry = step(t * rep + u, carry)
        return carry

    return lax.fori_loop(0, n // rep, body, init)


def _to_segments(src_ref, dst_ref, seg):
    def step(i, c):
        dst_ref[_rows8(i), :] = src_ref[pl.ds(i, SUBLANES, stride=seg), :]
        return c

    _repeat_loop(seg, step, 0)


def _from_segments(src_ref, dst_ref, seg):
    def step(i, c):
        dst_ref[pl.ds(i, SUBLANES, stride=seg), :] = src_ref[_rows8(i), :]
        return c

    _repeat_loop(seg, step, 0)


def _half_tiles(j, seg, reverse):
    h = seg // 2
    return (_rows8(seg - 1 - j), _rows8(h - 1 - j)) if reverse else (_rows8(j), _rows8(j + h))


def _seg_local_scan(s_ref, ar, ai, seg, reverse):
    ns = SLAB_NS

    def step(j, carry):
        tiles = _half_tiles(j, seg, reverse)
        loaded = [(s_ref[rows, 0:ns], s_ref[rows, ns:2 * ns]) for rows in tiles]
        out = []
        for (xr, xi), (cr, ci) in zip(loaded, (carry[0:2], carry[2:4])):
            out += list(_cmul_add(xr, xi, ar, ai, cr, ci))
        for rows, cr, ci in zip(tiles, out[0::2], out[1::2]):
            s_ref[rows, 0:ns] = cr
            s_ref[rows, ns:2 * ns] = ci
        return tuple(out)

    z = jnp.zeros((SUBLANES, ns), f32)
    return _repeat_loop(seg // 2, step, (z, z, z, z))


def _seg_boundaries(finals, ahr, ahi, reverse):
    fxr, fxi, fyr, fyi = finals
    row = lax.broadcasted_iota(jnp.int32, fxr.shape, 0)
    zero = jnp.zeros_like(fxr[0:1, :])
    xr, xi, yr, yi = (jnp.zeros_like(fxr) for _ in range(4))
    prev = None
    for r in (range(SUBLANES - 1, -1, -1) if reverse else range(SUBLANES)):
        if prev is None:
            nxr, nxi = zero, zero
        else:
            nxr, nxi = _cmul_add(fyr[prev:prev + 1, :], fyi[prev:prev + 1, :], ahr, ahi, nyr, nyi)
        nyr, nyi = _cmul_add(fxr[r:r + 1, :], fxi[r:r + 1, :], ahr, ahi, nxr, nxi)
        xr, xi = jnp.where(row == r, nxr, xr), jnp.where(row == r, nxi, xi)
        yr, yi = jnp.where(row == r, nyr, yr), jnp.where(row == r, nyi, yi)
        prev = r
    return (xr, xi), (yr, yi)


def _s5_states(u_ref, bs_ref, pw_ref, up_ref, s_ref, L, rc):
    seg = L // SUBLANES
    h = seg // 2
    ns = SLAB_NS
    _to_segments(u_ref, up_ref, seg)
    _s5_project_in(up_ref, bs_ref, s_ref, L, rc)
    ar, ai = pw_ref[0, 0:1, :], pw_ref[1, 0:1, :]
    finals = _seg_local_scan(s_ref, ar, ai, seg, False)
    enter = _seg_boundaries(finals, pw_ref[0, h - 1:h, :], pw_ref[1, h - 1:h, :], False)

    def fix(j, c):
        pr, pi = pw_ref[0, pl.ds(j, 1), :], pw_ref[1, pl.ds(j, 1), :]
        tiles = _half_tiles(j, seg, False)
        loaded = [(s_ref[rows, 0:ns], s_ref[rows, ns:2 * ns]) for rows in tiles]
        for rows, (xr, xi), (br, bi) in zip(tiles, loaded, enter):
            xr, xi = _cmul_add(xr, xi, pr, pi, br, bi)
            s_ref[rows, 0:ns] = xr
            s_ref[rows, ns:2 * ns] = xi
        return c

    _repeat_loop(h, fix, 0)


def _pw_spec(seg_rows, order):
    if order == "bs":
        return pl.BlockSpec((2, seg_rows, SLAB_NS), lambda b, s: (0, 0, s))
    return pl.BlockSpec((2, seg_rows, SLAB_NS), lambda s, b: (0, 0, s))


def _s5_fwd(p, bs, cs, pw, d_skip, B, L):
    rc = _tile(L, 344)
    seg = L // SUBLANES

    def body(u_ref, bs_ref, cs_ref, pw_ref, d_ref, y_ref, s_ref, up_ref, yp_ref):
        _s5_states(u_ref, bs_ref, pw_ref, up_ref, s_ref, L, rc)
        for r in range(0, L, rc):
            ypre = (jnp.dot(s_ref[r:r + rc, :].astype(bf16), cs_ref[...], preferred_element_type=f32)
                    + d_ref[...] * up_ref[r:r + rc, :])
            yp_ref[r:r + rc, :] = _gelu(ypre)
        _from_segments(yp_ref, y_ref, seg)

    ucol = SEG_U * (D_MODEL // SLAB_CH)
    return pl.pallas_call(
        body, name="s5_fwd", grid=(B, N_SLAB),
        in_specs=[pl.BlockSpec((L, SLAB_CH), lambda b, s: (b, ucol + s)),
                  pl.BlockSpec((None, SLAB_CH, 2 * SLAB_NS), lambda b, s: (s, 0, 0)),
                  pl.BlockSpec((None, 2 * SLAB_NS, SLAB_CH), lambda b, s: (s, 0, 0)),
                  _pw_spec(pw.shape[1], "bs"),
                  pl.BlockSpec((1, SLAB_CH), lambda b, s: (0, s))],
        out_specs=pl.BlockSpec((L, SLAB_CH), lambda b, s: (b, s)),
        out_shape=jax.ShapeDtypeStruct((B * L, D_MODEL), f32),
        scratch_shapes=[pltpu.VMEM((L, 2 * SLAB_NS), f32), pltpu.VMEM((L, SLAB_CH), f32), pltpu.VMEM((L, SLAB_CH), f32)],
        compiler_params=_params("parallel", "parallel"),
    )(p, bs, cs, pw, d_skip)


def _s5_bwd(p, dya0, dp, bs, cs, pw, d_skip, B, L, sums):
    rc = _tile(L, 344)
    ns = SLAB_NS
    seg = L // SUBLANES
    nx = len(sums)

    def body(u_ref, dy_ref, dp_in, bs_ref, cs_ref, pw_ref, d_ref, *rest):
        xin, (du_ref, dbs_ref, dcs_ref, da_ref, dd_ref), xout = rest[:nx], rest[nx:nx + 5], rest[nx + 5:2 * nx + 5]
        s_ref, lam_ref, up_ref, dyp_ref, nat_ref, send, recv = rest[2 * nx + 5:]
        del dp_in
        start, finish = _chip_exchange_steps(xin, xout, send, recv)

        @pl.when((pl.program_id(0) == 0) & (pl.program_id(1) == 0))
        def _():
            start()

        @pl.when(pl.program_id(1) == 0)
        def _():
            dbs_ref[...] = jnp.zeros_like(dbs_ref)
            dcs_ref[...] = jnp.zeros_like(dcs_ref)
            da_ref[...] = jnp.zeros_like(da_ref)
            dd_ref[...] = jnp.zeros_like(dd_ref)

        _s5_states(u_ref, bs_ref, pw_ref, up_ref, s_ref, L, rc)
        _to_segments(dy_ref, dyp_ref, seg)
        for r in range(0, L, rc):
            u = up_ref[r:r + rc, :]
            sb = s_ref[r:r + rc, :].astype(bf16)
            ypre = jnp.dot(sb, cs_ref[...], preferred_element_type=f32) + d_ref[...] * u
            dyp = dyp_ref[r:r + rc, :] * _gelu_grad(ypre)
            dyp_ref[r:r + rc, :] = dyp
            dd_ref[...] += jnp.sum(dyp * u, axis=0, keepdims=True)
            dypb = dyp.astype(bf16)
            dcs_ref[...] += lax.dot_general(sb, dypb, _DIMS["tn"], preferred_element_type=f32)
            lam_ref[r:r + rc, :] = lax.dot_general(dypb, cs_ref[...], _DIMS["nt"], preferred_element_type=f32)

        h = seg // 2
        ar, ai = pw_ref[0, 0:1, :], -pw_ref[1, 0:1, :]
        finals = _seg_local_scan(lam_ref, ar, ai, seg, True)
        enter = _seg_boundaries(finals, pw_ref[0, h - 1:h, :], -pw_ref[1, h - 1:h, :], True)

        def fix(j, acc):
            accr, acci = acc
            pr, pi = pw_ref[0, pl.ds(j, 1), :], -pw_ref[1, pl.ds(j, 1), :]
            tiles = _half_tiles(j, seg, True)
            loaded = [(lam_ref[rows, 0:ns], lam_ref[rows, ns:2 * ns]) for rows in tiles]
            for rows, (xr, xi), (br, bi), t in zip(tiles, loaded, enter, (seg - 1 - j, h - 1 - j)):
                xr, xi = _cmul_add(xr, xi, pr, pi, br, bi)
                lam_ref[rows, 0:ns] = xr
                lam_ref[rows, ns:2 * ns] = xi
                prev = _rows8(jnp.maximum(t - 1, 0))
                live = jnp.where(t > 0, 1.0, 0.0)
                spr = s_ref[prev, 0:ns] * live
                spi = s_ref[prev, ns:2 * ns] * live
                accr, acci = accr + xr * spr + xi * spi, acci + xi * spr - xr * spi
            return accr, acci

        z = jnp.zeros((SUBLANES, ns), f32)
        accr, acci = _repeat_loop(h, fix, (z, z))
        row = lax.broadcasted_iota(jnp.int32, (SUBLANES, ns), 0)
        last = _rows8(seg - 1)
        spr = jnp.where(row == 0, 0.0, pltpu.roll(s_ref[last, 0:ns], 1, 0))
        spi = jnp.where(row == 0, 0.0, pltpu.roll(s_ref[last, ns:2 * ns], 1, 0))
        xr, xi = lam_ref[0:SUBLANES, 0:ns], lam_ref[0:SUBLANES, ns:2 * ns]
        accr = accr + xr * spr + xi * spi
        acci = acci + xi * spr - xr * spi
        da_ref[0:1, :] += jnp.sum(accr, axis=0, keepdims=True)
        da_ref[1:2, :] += jnp.sum(acci, axis=0, keepdims=True)

        for r in range(0, L, rc):
            lamb = lam_ref[r:r + rc, :].astype(bf16)
            dbs_ref[...] += lax.dot_general(up_ref[r:r + rc, :].astype(bf16), lamb, _DIMS["tn"], preferred_element_type=f32)
            nat_ref[r:r + rc, :] = (lax.dot_general(lamb, bs_ref[...], _DIMS["nt"], preferred_element_type=f32)
                                    + d_ref[...] * dyp_ref[r:r + rc, :])
        _from_segments(nat_ref, up_ref, seg)
        du_ref[...] = up_ref[...].astype(du_ref.dtype)

        @pl.when((pl.program_id(0) == N_SLAB - 1) & (pl.program_id(1) == B - 1))
        def _():
            finish()

    ucol = SEG_U * (D_MODEL // SLAB_CH)
    T = B * L
    col = pltpu.VMEM((L, SLAB_CH), f32)
    res = pl.pallas_call(
        body, name="s5_bwd", grid=(N_SLAB, B),
        in_specs=[pl.BlockSpec((L, SLAB_CH), lambda s, b: (b, ucol + s)),
                  pl.BlockSpec((L, SLAB_CH), lambda s, b: (b, s)),
                  ANY,
                  pl.BlockSpec((None, SLAB_CH, 2 * SLAB_NS), lambda s, b: (s, 0, 0)),
                  pl.BlockSpec((None, 2 * SLAB_NS, SLAB_CH), lambda s, b: (s, 0, 0)),
                  _pw_spec(pw.shape[1], "sb"),
                  pl.BlockSpec((1, SLAB_CH), lambda s, b: (0, s))] + [ANY] * nx,
        out_specs=[pl.BlockSpec((None, L, SLAB_CH), lambda s, b: (SEG_U, b, s)),
                   pl.BlockSpec((None, SLAB_CH, 2 * SLAB_NS), lambda s, b: (s, 0, 0)),
                   pl.BlockSpec((None, 2 * SLAB_NS, SLAB_CH), lambda s, b: (s, 0, 0)),
                   pl.BlockSpec((None, 2, SLAB_NS), lambda s, b: (s, 0, 0)),
                   pl.BlockSpec((1, SLAB_CH), lambda s, b: (0, s))] + [ANY] * nx,
        out_shape=[jax.ShapeDtypeStruct((N_SEG, T, D_MODEL), bf16),
                   jax.ShapeDtypeStruct((N_SLAB, SLAB_CH, 2 * SLAB_NS), f32),
                   jax.ShapeDtypeStruct((N_SLAB, 2 * SLAB_NS, SLAB_CH), f32),
                   jax.ShapeDtypeStruct((N_SLAB, 2, SLAB_NS), f32),
                   jax.ShapeDtypeStruct((1, D_MODEL), f32)] + [jax.ShapeDtypeStruct(a.shape, a.dtype) for a in sums],
        scratch_shapes=[pltpu.VMEM((L, 2 * SLAB_NS), f32), pltpu.VMEM((L, 2 * SLAB_NS), f32), col, col, col]
        + _chip_exchange_sems(nx),
        input_output_aliases={2: 0},
        compiler_params=_params("arbitrary", "arbitrary"),
    )(p, dya0, dp, bs, cs, pw, d_skip, *sums)
    return res[:5], res[5:]


def _dotb(a, b, dims="nn"):
    return lax.dot_general(a.astype(bf16), b.astype(bf16), _DIMS[dims], preferred_element_type=f32)


def _tile_scan(x, reverse):
    n, w = x.shape
    v = x.reshape(n // SUBLANES, SUBLANES, w)
    row = lax.broadcasted_iota(jnp.int32, v.shape, 1)
    for k in (1, 2, 4):
        if reverse:
            v = v + jnp.where(row < SUBLANES - k, pltpu.roll(v, SUBLANES - k, 1), 0.0)
        else:
            v = v + jnp.where(row >= k, pltpu.roll(v, k, 1), 0.0)
    p = v.reshape(n // CHUNK, 2, SUBLANES, w)
    lo, hi = p[:, 0], p[:, 1]
    if reverse:
        lo = lo + hi[:, 0:1, :]
    else:
        hi = hi + lo[:, SUBLANES - 1:SUBLANES, :]
    return jnp.stack([lo, hi], axis=1).reshape(n, w)


def _chunk_cumsum(x):
    return _tile_scan(x, False)


def _chunk_rev_cumsum(x):
    return _tile_scan(x, True)


def _chunk_last(x):
    n, w = x.shape
    p = x.reshape(n // CHUNK, CHUNK, w)
    return jnp.broadcast_to(p[:, CHUNK - 1:CHUNK, :], p.shape).reshape(n, w)


def _hgrn_local(q, fl, lb):
    sg = _sigmoid(fl)
    f = lb + (1.0 - lb) * sg
    g = jnp.log(f)
    cum = _chunk_cumsum(g)
    rest = _chunk_last(cum) - cum
    e = jnp.exp(cum)
    em = jnp.exp(-cum)
    eo = jnp.exp(rest)
    k = 1.0 - f
    return sg, f, e, em, eo, q * e, k * em, k * eo, cum + rest


def _chunk_pos(n):
    return lax.broadcasted_iota(jnp.int32, (n, HEAD_DIM), 0) & (CHUNK - 1)


def _hgrn_block_rows(L):
    return _tile(L, 688, CHUNK)


def _hgrn_specs(L, order):
    hb = D_MODEL // HEAD_DIM

    def spec(seg):
        if order == "bh":
            return pl.BlockSpec((L, HEAD_DIM), lambda b, h: (b, seg * hb + h))
        return pl.BlockSpec((L, HEAD_DIM), lambda h, b: (b, seg * hb + h))

    return [spec(SEG_Q), spec(SEG_F), spec(SEG_I), spec(SEG_OG)]


PAIR = 2 * CHUNK
CHUNK_SHIFT = CHUNK.bit_length() - 1


def _pair_steps(L, rb):
    steps = []
    nch = rb // CHUNK
    for r in range(0, L, rb):
        steps += [(r + p * PAIR, PAIR) for p in range(nch // 2)]
        if nch % 2:
            steps.append((r + (nch - 1) * CHUNK, CHUNK))
    return steps


def _pair_flags(rb):
    ci = lax.broadcasted_iota(jnp.int32, (rb, HEAD_DIM), 0) >> CHUNK_SHIFT
    odd = (ci & 1) == 1
    has_next = jnp.logical_and(jnp.logical_not(odd), ci < rb // CHUNK - 1)
    return odd, has_next


def _pair_masks(rb):
    r = lax.broadcasted_iota(jnp.int32, (rb, rb), 0)
    c = lax.broadcasted_iota(jnp.int32, (rb, rb), 1)
    rc, cc = r >> CHUNK_SHIFT, c >> CHUNK_SHIFT
    same = (rc == cc) & (c <= r)
    prev = ((rc & 1) == 1) & (cc == rc - 1)
    return same, prev


def _hgrn_pair_local(q, fl, lb, odd, has_next):
    sg, f, e, em, eo, qt, kt, ko, cend = _hgrn_local(q, fl, lb)
    n = q.shape[0]
    a = jnp.where(odd, pltpu.roll(cend, CHUNK, 0), 0.0)
    z = jnp.where(has_next, pltpu.roll(cend, n - CHUNK, 0), 0.0)
    ea, ez = jnp.exp(a), jnp.exp(z)
    return dict(sg=sg, f=f, e=e, em=em, eo=eo, qt=qt, kt=kt, ko=ko, ea=ea, ez=ez, qs=qt * ea, ks=ko * ez,
                decp=jnp.exp(cend + a + z))


def _pair_scores(qt, kt, ko, same, prev):
    return (jnp.where(same, _dotb(qt, kt, "nt"), 0.0) + jnp.where(prev, _dotb(qt, ko, "nt"), 0.0)).astype(bf16)


def _hgrn_fwd(p, lb, norm_g, B, L):
    rb = _hgrn_block_rows(L)
    steps = _pair_steps(L, rb)
    blocks = [slice(r, r + rb) for r in range(0, L, rb)]

    def body(q_ref, f_ref, v_ref, og_ref, lb_ref, ng_ref, y_ref, qs_s, ks_s, vb_s, decp_s, o_s, o2_s, u_s, sb_s):
        lbv = lb_ref[...]
        ngv = ng_ref[...]
        same, prev = _pair_masks(rb)
        odd, has_next = _pair_flags(rb)

        for rows in blocks:
            t = _hgrn_pair_local(q_ref[rows, :], f_ref[rows, :], lbv, odd, has_next)
            vb = v_ref[rows, :].astype(bf16)
            o_s[rows, :] = _dotb(_pair_scores(t["qt"], t["kt"], t["ko"], same, prev), vb)
            qs_s[rows, :] = t["qs"].astype(bf16)
            ks_s[rows, :] = t["ks"].astype(bf16)
            vb_s[rows, :] = vb
            decp_s[rows, :] = t["decp"]

        for n, (r0, nr) in enumerate(steps):
            u_s[n] = _dotb(vb_s[r0:r0 + nr, :], ks_s[r0:r0 + nr, :], "tn")
        st = jnp.zeros((HEAD_DIM, HEAD_DIM), f32)
        for n, (r0, nr) in enumerate(steps):
            sb_s[n] = st.astype(bf16)
            st = st * decp_s[r0:r0 + 1, :] + u_s[n]
        for n, (r0, nr) in enumerate(steps):
            o2_s[r0:r0 + nr, :] = _dotb(qs_s[r0:r0 + nr, :], sb_s[n], "nt")

        for rows in blocks:
            o = o_s[rows, :] + o2_s[rows, :]
            og = og_ref[rows, :]
            on = o * lax.rsqrt(jnp.mean(o * o, axis=-1, keepdims=True) + EPS) * ngv
            y_ref[rows, :] = (on * og * _sigmoid(og)).astype(y_ref.dtype)

    sb = pltpu.VMEM((L, HEAD_DIM), bf16)
    sf = pltpu.VMEM((L, HEAD_DIM), f32)
    return pl.pallas_call(
        body, name="hgrn_fwd", grid=(B, HEADS),
        in_specs=_hgrn_specs(L, "bh") + [pl.BlockSpec((1, HEAD_DIM), lambda b, h: (0, h)),
                                          pl.BlockSpec((1, HEAD_DIM), lambda b, h: (0, 0))],
        out_specs=pl.BlockSpec((L, HEAD_DIM), lambda b, h: (b, h)),
        out_shape=jax.ShapeDtypeStruct((B * L, D_MODEL), bf16),
        scratch_shapes=[sb, sb, sb, sf, sf, sf, pltpu.VMEM((len(steps), HEAD_DIM, HEAD_DIM), f32),
                        pltpu.VMEM((len(steps), HEAD_DIM, HEAD_DIM), bf16)],
        compiler_params=_params("parallel", "parallel"),
    )(p, p, p, p, lb, norm_g)


def _hgrn_bwd(p, dyb, dp, lb, norm_g, B, L):
    rb = _hgrn_block_rows(L)
    steps = _pair_steps(L, rb)
    blocks = [slice(r, r + rb) for r in range(0, L, rb)]

    def body(q_ref, f_ref, v_ref, og_ref, dy_ref, dp_in, lb_ref, ng_ref, dseg_ref, dlb_ref, dng_ref,
             st_ref, u_s, dsb_s, qt_s, kt_s, ko_s, qs_s, ks_s, vb_s, do_s,
             decp_s, o_s, o2_s, dqt_s, dkt_s, dko_s, dv_s, dv2_s, dqs_s, dks_s, ddecp_s):
        del dp_in
        lbv = lb_ref[...]
        ngv = ng_ref[...]
        same, prev = _pair_masks(rb)
        odd, has_next = _pair_flags(rb)
        pos = _chunk_pos(rb)

        @pl.when(pl.program_id(1) == 0)
        def _():
            dlb_ref[...] = jnp.zeros_like(dlb_ref)

        @pl.when((pl.program_id(0) == 0) & (pl.program_id(1) == 0))
        def _():
            dng_ref[...] = jnp.zeros_like(dng_ref)

        def scores(rows):
            return _pair_scores(qt_s[rows, :], kt_s[rows, :], ko_s[rows, :], same, prev)

        for rows in blocks:
            t = _hgrn_pair_local(q_ref[rows, :], f_ref[rows, :], lbv, odd, has_next)
            for dst, key in ((qt_s, "qt"), (kt_s, "kt"), (ko_s, "ko"), (qs_s, "qs"), (ks_s, "ks")):
                dst[rows, :] = t[key].astype(bf16)
            vb_s[rows, :] = v_ref[rows, :].astype(bf16)
            decp_s[rows, :] = t["decp"]
            o_s[rows, :] = _dotb(scores(rows), vb_s[rows, :])

        for n, (r0, nr) in enumerate(steps):
            u_s[n] = _dotb(vb_s[r0:r0 + nr, :], ks_s[r0:r0 + nr, :], "tn")
        st = jnp.zeros((HEAD_DIM, HEAD_DIM), f32)
        for n, (r0, nr) in enumerate(steps):
            st_ref[n] = st
            st = st * decp_s[r0:r0 + 1, :] + u_s[n]
        for n, (r0, nr) in enumerate(steps):
            o2_s[r0:r0 + nr, :] = _dotb(qs_s[r0:r0 + nr, :], st_ref[n], "nt")

        dng = jnp.zeros((1, HEAD_DIM), f32)
        for rows in blocks:
            o = o_s[rows, :] + o2_s[rows, :]
            og = og_ref[rows, :]
            dy = dy_ref[rows, :]
            rs = lax.rsqrt(jnp.mean(o * o, axis=-1, keepdims=True) + EPS)
            xn = o * rs
            so = _sigmoid(og)
            dseg_ref[SEG_OG, rows, :] = (dy * xn * ngv * so * (1.0 + og * (1.0 - so))).astype(dseg_ref.dtype)
            don = dy * og * so
            dng = dng + jnp.sum(don * xn, axis=0, keepdims=True)
            dxo = don * ngv
            do = (rs * (dxo - xn * jnp.mean(dxo * xn, axis=-1, keepdims=True))).astype(bf16)
            do_s[rows, :] = do
            dpf = _dotb(do, vb_s[rows, :], "nt")
            dp1 = jnp.where(same, dpf, 0.0).astype(bf16)
            dp2 = jnp.where(prev, dpf, 0.0).astype(bf16)
            dqt_s[rows, :] = _dotb(dp1, kt_s[rows, :]) + _dotb(dp2, ko_s[rows, :])
            dkt_s[rows, :] = _dotb(dp1, qt_s[rows, :], "tn")
            dko_s[rows, :] = _dotb(dp2, qt_s[rows, :], "tn")
            dv_s[rows, :] = _dotb(scores(rows), do, "tn")
        dng_ref[...] += dng

        for n, (r0, nr) in enumerate(steps):
            u_s[n] = _dotb(do_s[r0:r0 + nr, :], qs_s[r0:r0 + nr, :], "tn")
        dst = jnp.zeros((HEAD_DIM, HEAD_DIM), f32)
        for n, (r0, nr) in reversed(list(enumerate(steps))):
            dsb_s[n] = dst.astype(bf16)
            ddecp_s[r0:r0 + nr, :] = jnp.broadcast_to(jnp.sum(dst * st_ref[n], axis=0, keepdims=True), (nr, HEAD_DIM))
            dst = dst * decp_s[r0:r0 + 1, :] + u_s[n]
        for n, (r0, nr) in enumerate(steps):
            rows = slice(r0, r0 + nr)
            dqs_s[rows, :] = _dotb(do_s[rows, :], st_ref[n])
            dv2_s[rows, :] = _dotb(ks_s[rows, :], dsb_s[n], "nt")
            dks_s[rows, :] = _dotb(vb_s[rows, :], dsb_s[n])

        def chunk_sum(x):
            return _chunk_last(_chunk_cumsum(x))

        dlb = jnp.zeros((1, HEAD_DIM), f32)
        for rows in blocks:
            t = _hgrn_pair_local(q_ref[rows, :], f_ref[rows, :], lbv, odd, has_next)
            dqs, dks = dqs_s[rows, :], dks_s[rows, :]
            dqt = dqt_s[rows, :] + dqs * t["ea"]
            dko = dko_s[rows, :] + dks * t["ez"]
            dkt = dkt_s[rows, :]
            dko_ko = dko * t["ko"]
            dcum = dqt * t["qt"] - dkt * t["kt"] - dko_ko
            from_next = pltpu.roll(chunk_sum(jnp.where(odd, dqs * t["qs"], 0.0)), rb - CHUNK, 0)
            from_prev = pltpu.roll(chunk_sum(jnp.where(has_next, dks * t["ks"], 0.0)), CHUNK, 0)
            d_end = (chunk_sum(dko_ko) + jnp.where(has_next, from_next, 0.0) + jnp.where(odd, from_prev, 0.0)
                     + ddecp_s[rows, :] * t["decp"])
            dcum = dcum + jnp.where(pos == CHUNK - 1, d_end, 0.0)
            df = _chunk_rev_cumsum(dcum) / t["f"] - (dkt * t["em"] + dko * t["eo"])
            dlb = dlb + jnp.sum(df * (1.0 - t["sg"]), axis=0, keepdims=True)
            dseg_ref[SEG_Q, rows, :] = (dqt * t["e"]).astype(dseg_ref.dtype)
            dseg_ref[SEG_F, rows, :] = (df * (1.0 - lbv) * t["sg"] * (1.0 - t["sg"])).astype(dseg_ref.dtype)
            dseg_ref[SEG_I, rows, :] = (dv_s[rows, :] + dv2_s[rows, :]).astype(dseg_ref.dtype)
        dlb_ref[...] += dlb

    T = B * L
    ns = len(steps)
    sb = pltpu.VMEM((L, HEAD_DIM), bf16)
    sf = pltpu.VMEM((L, HEAD_DIM), f32)
    return pl.pallas_call(
        body, name="hgrn_bwd", grid=(HEADS, B),
        in_specs=_hgrn_specs(L, "hb") + [pl.BlockSpec((L, HEAD_DIM), lambda h, b: (b, h)), ANY,
                                          pl.BlockSpec((1, HEAD_DIM), lambda h, b: (0, h)),
                                          pl.BlockSpec((1, HEAD_DIM), lambda h, b: (0, 0))],
        out_specs=[pl.BlockSpec((4, L, HEAD_DIM), lambda h, b: (0, b, h)),
                   pl.BlockSpec((1, HEAD_DIM), lambda h, b: (0, h)),
                   pl.BlockSpec((1, HEAD_DIM), lambda h, b: (0, 0))],
        out_shape=[jax.ShapeDtypeStruct((N_SEG, T, D_MODEL), bf16), jax.ShapeDtypeStruct((1, D_MODEL), f32),
                   jax.ShapeDtypeStruct((1, HEAD_DIM), f32)],
        scratch_shapes=[pltpu.VMEM((ns, HEAD_DIM, HEAD_DIM), f32), pltpu.VMEM((ns, HEAD_DIM, HEAD_DIM), f32),
                        pltpu.VMEM((ns, HEAD_DIM, HEAD_DIM), bf16)] + [sb] * 7 + [sf] * 11,
        input_output_aliases={5: 0},
        compiler_params=_params("arbitrary", "arbitrary"),
    )(p, p, p, p, dyb, dp, lb, norm_g)


def _dz1_norm(dp, w_in_phys, h0, g, dh1):
    _, T, Dm = dp.shape
    tm = _tile(T, 1032)
    return _mm_rmsnorm_bwd("dz1", dp, w_in_phys, (T // tm, 1, N_SEG),
                           pl.BlockSpec((None, tm, Dm), lambda i, j, k: (k, i, 0)),
                           pl.BlockSpec((Dm, Dm), lambda i, j, k: (0, k)), h0, g, dh1)


def _dz2_norm(dup, w_up, h1, g, dh2):
    _, T, _ = dup.shape
    tm = _tile(T, 1032)
    tk = D_FF // 2
    return _mm_rmsnorm_bwd("dz2", dup, w_up, (T // tm, 1, 4),
                           pl.BlockSpec((None, tm, tk), lambda i, j, k: (k // 2, i, k % 2)),
                           pl.BlockSpec((D_MODEL, tk), lambda i, j, k: (0, k)), h1, g, dh2)


def _dw_in(z1, dp):
    _, T, Dm = dp.shape
    tk = _tile(T, 1376)
    return _mm("dw_in", z1, dp, "tn", (1, N_SEG, T // tk),
               pl.BlockSpec((tk, Dm), lambda i, j, k: (k, 0)),
               pl.BlockSpec((None, tk, Dm), lambda i, j, k: (j, k, 0)),
               jax.ShapeDtypeStruct((N_SEG, Dm, Dm), f32),
               pl.BlockSpec((None, Dm, Dm), lambda i, j, k: (j, 0, 0)), (Dm, Dm))


def _dw_up(z2, dup):
    _, T, _ = dup.shape
    tn = D_FF // 2
    tk = _tile(T, 1376)
    return _mm("dw_up", z2, dup, "tn", (1, N_CHIPS, T // tk),
               pl.BlockSpec((tk, D_MODEL), lambda i, j, k: (k, 0)),
               pl.BlockSpec((None, tk, tn), lambda i, j, k: (j // 2, k, j % 2)),
               jax.ShapeDtypeStruct((N_CHIPS, D_MODEL, tn), f32),
               pl.BlockSpec((None, D_MODEL, tn), lambda i, j, k: (j, 0, 0)), (D_MODEL, tn))


def _place():
    x, y, c = lax.axis_index("x"), lax.axis_index("y"), lax.axis_index("c")
    chips = [(1 - x, y), (x, 1 - y), (1 - x, 1 - y)]
    return x, y, c, chips


def _allgather_chips(arrs):
    n = len(arrs)

    def body(*refs):
        ins, outs = refs[:n], refs[n:2 * n]
        send, recv, local = refs[2 * n:]
        x, y, c, chips = _place()
        me = 2 * x + y

        def copy(a, k, slot):
            px, py = chips[k]
            return pltpu.make_async_remote_copy(src_ref=ins[a], dst_ref=outs[a].at[slot], send_sem=send.at[3 * a + k],
                                                recv_sem=recv.at[3 * a + k], device_id=(px, py, c), device_id_type=MESH)

        for a in range(n):
            pltpu.make_async_copy(ins[a], outs[a].at[me], local.at[a]).start()
            for k in range(3):
                copy(a, k, me).start()
        for a in range(n):
            for k, (px, py) in enumerate(chips):
                copy(a, k, 2 * px + py).wait_recv()
        for a in range(n):
            pltpu.make_async_copy(ins[a], outs[a].at[me], local.at[a]).wait()
            for k in range(3):
                copy(a, k, me).wait_send()

    return pl.pallas_call(
        body, name="allgather_chips", in_specs=[ANY] * n, out_specs=[ANY] * n,
        out_shape=[jax.ShapeDtypeStruct((N_CHIPS,) + a.shape, a.dtype) for a in arrs],
        scratch_shapes=[pltpu.SemaphoreType.DMA((3 * n,)), pltpu.SemaphoreType.DMA((3 * n,)), pltpu.SemaphoreType.DMA((n,))],
    )(*arrs)


def _allgather_split(arrs):
    n = len(arrs)

    def body(*refs):
        start, finish = _gather_split_steps(refs[:n], refs[n:2 * n], *refs[2 * n:])
        start()
        finish()

    return pl.pallas_call(
        body, name="allgather_split", in_specs=[ANY] * n, out_specs=[ANY] * n,
        out_shape=[jax.ShapeDtypeStruct((N_CHIPS,) + a.shape, a.dtype) for a in arrs],
        scratch_shapes=_gather_split_sems(n),
    )(*arrs)


def _gather_split_sems(n):
    return [pltpu.SemaphoreType.DMA((3 * n,)) for _ in range(4)]


def _gather_split_steps(ins, outs, send, recv, fsend, frecv):
    n = len(ins)

    def place():
        x, y, c, chips = _place()
        return x, y, c, chips, 2 * x + y

    def half(a, core):
        rh = ins[a].shape[0] // 2
        return pl.ds(core * rh, rh)

    def copy(a, k, slot):
        x, y, c, chips, _ = place()
        px, py = chips[k]
        return pltpu.make_async_remote_copy(src_ref=ins[a].at[half(a, c), :], dst_ref=outs[a].at[slot, half(a, c), :],
                                            send_sem=send.at[3 * a + k], recv_sem=recv.at[3 * a + k],
                                            device_id=(px, py, c), device_id_type=MESH)

    def forward(a, k, core):
        x, y, c, chips, _ = place()
        px, py = chips[k]
        rows = outs[a].at[2 * px + py, half(a, core), :]
        return pltpu.make_async_remote_copy(src_ref=rows, dst_ref=rows, send_sem=fsend.at[3 * a + k],
                                            recv_sem=frecv.at[3 * a + k], device_id=(x, y, 1 - c), device_id_type=MESH)

    def start():
        me = place()[4]
        for a in range(n):
            for k in range(3):
                copy(a, k, me).start()

    def finish():
        x, y, c, chips, me = place()
        for a in range(n):
            for k, (px, py) in enumerate(chips):
                copy(a, k, 2 * px + py).wait_recv()
                forward(a, k, c).start()
        for a in range(n):
            for k in range(3):
                forward(a, k, 1 - c).wait_recv()
        for a in range(n):
            for k in range(3):
                copy(a, k, me).wait_send()
                forward(a, k, c).wait_send()

    return start, finish


def _in_proj_gather(z1, w_in, shards):
    n = len(shards)
    T, K = z1.shape
    N = w_in.shape[1]
    tm = _tile(T, 1032)
    tn = 1024
    grid = (T // tm, N // tn)

    def body(a_ref, b_ref, *rest):
        ins, o_ref, outs, sems = rest[:n], rest[n], rest[n + 1:2 * n + 1], rest[2 * n + 1:]
        start, finish = _gather_split_steps(ins, outs, *sems)
        i, j = pl.program_id(0), pl.program_id(1)

        @pl.when((i == 0) & (j == 0))
        def _():
            start()

        o_ref[...] = jnp.dot(a_ref[...], b_ref[...], preferred_element_type=f32)

        @pl.when((i == grid[0] - 1) & (j == grid[1] - 1))
        def _():
            finish()

    res = pl.pallas_call(
        body, name="in_proj", grid=grid,
        in_specs=[pl.BlockSpec((tm, K), lambda i, j: (i, 0)), pl.BlockSpec((K, tn), lambda i, j: (0, j))] + [ANY] * n,
        out_specs=[pl.BlockSpec((tm, tn), lambda i, j: (i, j))] + [ANY] * n,
        out_shape=[jax.ShapeDtypeStruct((T, N), f32)] + [jax.ShapeDtypeStruct((N_CHIPS,) + a.shape, a.dtype) for a in shards],
        scratch_shapes=_gather_split_sems(n),
        compiler_params=_params("arbitrary", "arbitrary"),
    )(z1, w_in, *shards)
    return res[0], res[1:]


def _sibling_halves(parts, name="sibling_halves"):
    n = len(parts)

    def body(*refs):
        ins, outs = refs[:n], refs[n:2 * n]
        send, recv = refs[2 * n:]
        x, y, c, _ = _place()

        def copy(a):
            rh = ins[a].shape[1] // 2
            return pltpu.make_async_remote_copy(src_ref=ins[a].at[:, pl.ds((1 - c) * rh, rh), :], dst_ref=outs[a],
                                                send_sem=send.at[a], recv_sem=recv.at[a], device_id=(x, y, 1 - c),
                                                device_id_type=MESH)

        for a in range(n):
            copy(a).start()
        for a in range(n):
            copy(a).wait_recv()
        for a in range(n):
            copy(a).wait_send()

    return pl.pallas_call(
        body, name=name, in_specs=[ANY] * n, out_specs=[ANY] * n,
        out_shape=[jax.ShapeDtypeStruct((a.shape[0], a.shape[1] // 2, a.shape[2]), a.dtype) for a in parts],
        scratch_shapes=[pltpu.SemaphoreType.DMA((n,)), pltpu.SemaphoreType.DMA((n,))],
    )(*parts)


def _add_own_half(name, part, got, core):
    nchip, R, C = part.shape
    rh = R // 2
    tr = _tile(rh, 512, 2 * SUBLANES)
    nt = rh // tr

    def body(core_ref, a_ref, b_ref, o_ref):
        del core_ref
        o_ref[...] = (a_ref[...] + b_ref[...]).astype(o_ref.dtype)

    return pl.pallas_call(
        body, name=name,
        grid_spec=pltpu.PrefetchScalarGridSpec(
            num_scalar_prefetch=1, grid=(nchip, nt),
            in_specs=[pl.BlockSpec((None, tr, C), lambda j, i, core_ref: (j, core_ref[0] * nt + i, 0)),
                      pl.BlockSpec((None, tr, C), lambda j, i, core_ref: (j, i, 0))],
            out_specs=pl.BlockSpec((None, tr, C), lambda j, i, core_ref: (j, i, 0))),
        out_shape=jax.ShapeDtypeStruct((nchip, rh, C), bf16), compiler_params=_params("parallel", "parallel"),
    )(core, part, got)


def _add_own_half_w_in(part, got, core):
    _, R, C = part.shape
    rh = R // 2
    tr = _tile(rh, 512, 2 * SUBLANES)
    nt = rh // tr
    tn = 256
    per_seg = C // tn
    per_chip = IN_COLS // N_CHIPS // tn

    def src(j):
        return ((j // per_seg + N_SEG - 1) % N_SEG, j % per_seg)

    def body(core_ref, a_ref, b_ref, o_ref):
        del core_ref
        o_ref[...] = (a_ref[...] + b_ref[...]).astype(o_ref.dtype)

    return pl.pallas_call(
        body, name="add_half_w_in",
        grid_spec=pltpu.PrefetchScalarGridSpec(
            num_scalar_prefetch=1, grid=(IN_COLS // tn, nt),
            in_specs=[pl.BlockSpec((None, tr, tn), lambda j, i, core_ref: (src(j)[0], core_ref[0] * nt + i, src(j)[1])),
                      pl.BlockSpec((None, tr, tn), lambda j, i, core_ref: (src(j)[0], i, src(j)[1]))],
            out_specs=pl.BlockSpec((None, tr, tn), lambda j, i, core_ref: (j // per_chip, i, j % per_chip))),
        out_shape=jax.ShapeDtypeStruct((N_CHIPS, rh, IN_COLS // N_CHIPS), bf16), compiler_params=_params("parallel", "parallel"),
    )(core, part, got)


def _chip_exchange(sums):
    n = len(sums)

    def body(*refs):
        start, finish = _chip_exchange_steps(refs[:n], refs[n:2 * n], *refs[2 * n:])
        start()
        finish()

    return pl.pallas_call(
        body, name="chip_exchange", in_specs=[ANY] * n, out_specs=[ANY] * n,
        out_shape=[jax.ShapeDtypeStruct(a.shape, a.dtype) for a in sums],
        scratch_shapes=_chip_exchange_sems(n),
    )(*sums)


def _chip_exchange_sems(n):
    return [pltpu.SemaphoreType.DMA((3 * n,)), pltpu.SemaphoreType.DMA((3 * n,))]


def _chip_exchange_steps(ins, outs, send, recv):
    n = len(ins)

    def copy(a, k, own_slot):
        x, y, c, chips = _place()
        px, py = chips[k]
        slot = 2 * x + y if own_slot else 2 * px + py
        return pltpu.make_async_remote_copy(src_ref=ins[a].at[2 * px + py], dst_ref=outs[a].at[slot], send_sem=send.at[3 * a + k],
                                            recv_sem=recv.at[3 * a + k], device_id=(px, py, c), device_id_type=MESH)

    def start():
        for a in range(n):
            for k in range(3):
                copy(a, k, True).start()

    def finish():
        for a in range(n):
            for k in range(3):
                copy(a, k, False).wait_recv()
        for a in range(n):
            for k in range(3):
                copy(a, k, True).wait_send()

    return start, finish


def _sum_chips(name, slots, sums, where):
    nchip, rh, C = slots.shape
    tr = _tile(rh, 512, 2 * SUBLANES)
    nt = rh // tr

    def body(where_ref, own_ref, s1_ref, s2_ref, s3_ref, o_ref):
        me = where_ref[0]
        by_dist = [r[...].astype(f32) for r in (own_ref, s1_ref, s2_ref, s3_ref)]
        acc = None
        for j in range(nchip):
            d = me ^ j
            term = jnp.where(d == 0, by_dist[0], jnp.where(d == 1, by_dist[1], jnp.where(d == 2, by_dist[2], by_dist[3])))
            acc = term if acc is None else acc + term
        o_ref[...] = acc

    def other(d):
        return pl.BlockSpec((None, tr, C), lambda i, w: (w[0] ^ d, i, 0))

    return pl.pallas_call(
        body, name=name,
        grid_spec=pltpu.PrefetchScalarGridSpec(
            num_scalar_prefetch=1, grid=(nt,),
            in_specs=[other(0), other(1), other(2), other(3)],
            out_specs=pl.BlockSpec((tr, C), lambda i, w: (w[1] * nt + i, 0))),
        out_shape=jax.ShapeDtypeStruct((2 * rh, C), f32), compiler_params=_params("parallel"),
    )(where, sums, slots, slots, slots)


def _sum_slots(name, slots):
    ns, R, C = slots.shape
    tr = _tile(R, 256)

    def body(s_ref, o_ref):
        acc = s_ref[0]
        for j in range(1, ns):
            acc = acc + s_ref[j]
        o_ref[...] = acc

    return pl.pallas_call(
        body, name=name, grid=(R // tr,), in_specs=[pl.BlockSpec((ns, tr, C), lambda i: (0, i, 0))],
        out_specs=pl.BlockSpec((tr, C), lambda i: (i, 0)), out_shape=jax.ShapeDtypeStruct((R, C), f32),
        compiler_params=_params("parallel"),
    )(slots)


def _sibling_join(fulls):
    n = len(fulls)

    def body(*refs):
        ins, outs = refs[:n], refs[n:2 * n]
        send, recv = refs[2 * n:]
        x, y, c, _ = _place()

        def copy(a, core):
            rh = ins[a].shape[0] // 2
            rows = pl.ds(core * rh, rh)
            return pltpu.make_async_remote_copy(src_ref=ins[a].at[rows, :], dst_ref=outs[a].at[rows, :], send_sem=send.at[a],
                                                recv_sem=recv.at[a], device_id=(x, y, 1 - c), device_id_type=MESH)

        for a in range(n):
            copy(a, c).start()
        for a in range(n):
            copy(a, 1 - c).wait_recv()
        for a in range(n):
            copy(a, c).wait_send()

    return pl.pallas_call(
        body, name="sibling_join", in_specs=[ANY] * n, out_specs=[ANY] * n,
        out_shape=[jax.ShapeDtypeStruct(a.shape, a.dtype) for a in fulls],
        scratch_shapes=[pltpu.SemaphoreType.DMA((n,)), pltpu.SemaphoreType.DMA((n,))],
        input_output_aliases={a: a for a in range(n)},
    )(*fulls)


def _allgather_devices(v):
    def body(v_ref, out_ref, send, recv):
        x, y, c, chips = _place()
        me, sibling = (x, y, c), (x, y, 1 - c)

        def slot(px, py, pc):
            return out_ref.at[4 * px + 2 * py + pc]

        def copy(k, block, to, src=None):
            return pltpu.make_async_remote_copy(src_ref=slot(*block) if src is None else src, dst_ref=slot(*block),
                                                send_sem=send.at[k], recv_sem=recv.at[k], device_id=to, device_id_type=MESH)

        first = [copy(0, me, sibling, src=v_ref)] + [copy(1 + j, me, (*chip, c), src=v_ref) for j, chip in enumerate(chips)]
        for cp in first:
            cp.start()
        passed = [copy(4 + j, (*chip, c), sibling) for j, chip in enumerate(chips)]
        for j, chip in enumerate(chips):
            copy(1 + j, (*chip, c), me).wait_recv()
            passed[j].start()
        copy(0, sibling, me).wait_recv()
        for j, chip in enumerate(chips):
            copy(4 + j, (*chip, 1 - c), me).wait_recv()
        for cp in first + passed:
            cp.wait_send()

    return pl.pallas_call(
        body, name="allgather_devices", in_specs=[ANY], out_specs=ANY,
        out_shape=jax.ShapeDtypeStruct((N_DEV,) + v.shape, v.dtype),
        scratch_shapes=[pltpu.SemaphoreType.DMA((N_DEV - 1,)), pltpu.SemaphoreType.DMA((N_DEV - 1,))],
    )(v)


def _adamw(name, w, g, m, v):
    R, C = w.shape
    tr = _tile(R, 256)
    c1 = 1.0 / (1.0 - ADAM_B1 ** ADAM_STEP)
    c2 = 1.0 / (1.0 - ADAM_B2 ** ADAM_STEP)

    def body(w_ref, g_ref, m_ref, v_ref, d_ref, nm_ref, nv_ref):
        gv = g_ref[...]
        nm = ADAM_B1 * m_ref[...] + (1.0 - ADAM_B1) * gv
        nv = ADAM_B2 * v_ref[...] + (1.0 - ADAM_B2) * gv * gv
        d_ref[...] = -ADAM_LR * ((nm * c1) / (jnp.sqrt(nv * c2) + ADAM_EPS) + ADAM_WD * w_ref[...])
        nm_ref[...] = nm
        nv_ref[...] = nv

    row = pl.BlockSpec((tr, C), lambda i: (i, 0))
    sh = jax.ShapeDtypeStruct((R, C), f32)
    return pl.pallas_call(body, name=name, grid=(R // tr,), in_specs=[row] * 4, out_specs=[row] * 3,
                          out_shape=[sh, sh, sh], compiler_params=_params("parallel"))(w, g, m, v)


def _adamw_update(w, g, m, v):
    c1 = 1.0 / (1.0 - ADAM_B1 ** ADAM_STEP)
    c2 = 1.0 / (1.0 - ADAM_B2 ** ADAM_STEP)
    nm = ADAM_B1 * m + (1.0 - ADAM_B1) * g
    nv = ADAM_B2 * v + (1.0 - ADAM_B2) * g * g
    return -ADAM_LR * ((nm * c1) / (jnp.sqrt(nv * c2) + ADAM_EPS) + ADAM_WD * w), nm, nv


def _adamw_many(ws, gs, ms, vs):
    n = len(ws)

    def body(*refs):
        ins, outs = refs[:4 * n], refs[4 * n:]
        for a in range(n):
            d, nm, nv = _adamw_update(ins[a][...], ins[n + a][...], ins[2 * n + a][...], ins[3 * n + a][...])
            outs[a][...] = d
            outs[n + a][...] = nm
            outs[2 * n + a][...] = nv

    shapes = [jax.ShapeDtypeStruct(a.shape, f32) for a in ws]
    return pl.pallas_call(body, name="adamw_small", out_shape=shapes * 3)(*ws, *gs, *ms, *vs)


def _zoh_parts(lr, li, log_dt):
    dt = jnp.exp(log_dt)
    mag = jnp.exp(lr * dt)
    c, s = jnp.cos(li * dt), jnp.sin(li * dt)
    ab_re, ab_im = mag * c, mag * s
    den = lr * lr + li * li
    nr = ab_re - 1.0
    coef_re = (nr * lr + ab_im * li) / den
    coef_im = (ab_im * lr - nr * li) / den
    return dt, mag, c, s, ab_re, ab_im, den, nr, coef_re, coef_im


def _zoh_fwd(lr, li, log_dt, b_re, b_im):
    def body(lr_ref, li_ref, ld_ref, br_ref, bi_ref, ar_ref, ai_ref, bbr_ref, bbi_ref):
        _, _, _, _, ab_re, ab_im, _, _, coef_re, coef_im = _zoh_parts(lr_ref[...], li_ref[...], ld_ref[...])
        ar_ref[...] = ab_re
        ai_ref[...] = ab_im
        bbr_ref[...] = coef_re * br_ref[...] - coef_im * bi_ref[...]
        bbi_ref[...] = coef_re * bi_ref[...] + coef_im * br_ref[...]

    col = jax.ShapeDtypeStruct(lr.shape, f32)
    mat = jax.ShapeDtypeStruct(b_re.shape, f32)
    return pl.pallas_call(body, name="zoh_fwd", out_shape=[col, col, mat, mat])(lr, li, log_dt, b_re, b_im)


def _zoh_bwd(lr, li, log_dt, b_re, b_im, d_ar, d_ai, d_bbr, d_bbi):
    n = lr.shape[1]
    groups = n // SSM_STATE

    def body(lr_ref, li_ref, ld_ref, br_ref, bi_ref, dar_ref, dai_ref, dbbr_ref, dbbi_ref,
             dlr_ref, dli_ref, dld_ref, dbr_ref, dbi_ref):
        lr_, li_ = lr_ref[...], li_ref[...]
        dt, mag, c, s, _, ab_im, den, nr, coef_re, coef_im = _zoh_parts(lr_, li_, ld_ref[...])
        br, bi, dbbr, dbbi = br_ref[...], bi_ref[...], dbbr_ref[...], dbbi_ref[...]
        dbr_ref[...] = coef_re * dbbr + coef_im * dbbi
        dbi_ref[...] = coef_re * dbbi - coef_im * dbbr
        d_cr = jnp.sum(dbbr * br + dbbi * bi, axis=0, keepdims=True)
        d_ci = jnp.sum(dbbi * br - dbbr * bi, axis=0, keepdims=True)
        d_nr = (d_cr * lr_ - d_ci * li_) / den
        d_abi = dai_ref[...] + (d_cr * li_ + d_ci * lr_) / den
        d_abr = dar_ref[...] + d_nr
        d_den = -(d_cr * coef_re + d_ci * coef_im) / den
        d_lr = (d_cr * nr + d_ci * ab_im) / den + 2.0 * lr_ * d_den
        d_li = (d_cr * ab_im - d_ci * nr) / den + 2.0 * li_ * d_den
        d_theta = mag * (d_abi * c - d_abr * s)
        d_arg = mag * (d_abr * c + d_abi * s)
        dlr_ref[...] = d_lr + d_arg * dt
        dli_ref[...] = d_li + d_theta * dt
        d_dt = d_arg * lr_ + d_theta * li_
        member = (lax.broadcasted_iota(jnp.int32, (n, groups), 0) >> (SSM_STATE.bit_length() - 1)
                  == lax.broadcasted_iota(jnp.int32, (n, groups), 1)).astype(f32)
        dld_ref[...] = jnp.dot(d_dt * dt, member, preferred_element_type=f32, precision=lax.Precision.HIGHEST)

    col = jax.ShapeDtypeStruct(lr.shape, f32)
    mat = jax.ShapeDtypeStruct(b_re.shape, f32)
    return pl.pallas_call(body, name="zoh_bwd", out_shape=[col, col, jax.ShapeDtypeStruct((1, groups), f32), mat, mat])(
        lr, li, log_dt, b_re, b_im, d_ar, d_ai, d_bbr, d_bbi)


def _lower_bound_fwd(logits):
    def body(x_ref, o_ref):
        x = x_ref[...]
        e = jnp.exp(x - jnp.max(x, axis=0, keepdims=True))
        o_ref[...] = e / jnp.sum(e, axis=0, keepdims=True)

    return pl.pallas_call(body, name="lower_bound_fwd", out_shape=jax.ShapeDtypeStruct(logits.shape, f32))(logits)


def _lower_bound_bwd(sm, d_lb):
    def body(sm_ref, d_ref, o_ref):
        smv = sm_ref[...]
        row = lax.broadcasted_iota(jnp.int32, smv.shape, 0)
        sm0 = smv[0:1, :]
        o_ref[...] = sm0 * d_ref[...] * (jnp.where(row == 0, 1.0, 0.0) - smv)

    return pl.pallas_call(body, name="lower_bound_bwd", out_shape=jax.ShapeDtypeStruct(sm.shape, f32))(sm, d_lb)


def _s5_tables(ab_re, ab_im, bb_re, bb_im, c_re, c_im, seg):
    eye = jnp.eye(SLAB_GROUPS, dtype=f32)

    def blk_in(bb):
        return jnp.einsum("hsgp,gk->sghkp", bb.reshape(SSM_GROUP, N_SLAB, SLAB_GROUPS, SSM_STATE), eye).reshape(
            N_SLAB, SLAB_CH, SLAB_NS)

    def blk_out(cc):
        return jnp.einsum("sghp,gk->skpgh", cc.reshape(N_SLAB, SLAB_GROUPS, SSM_GROUP, SSM_STATE), eye).reshape(
            N_SLAB, SLAB_NS, SLAB_CH)

    bs = jnp.concatenate([blk_in(bb_re), blk_in(bb_im)], axis=2).astype(bf16)
    cs = jnp.concatenate([blk_out(c_re), blk_out(-c_im)], axis=1).astype(bf16)
    n = SSM_GROUPS * SSM_STATE
    pw = _power_table(jnp.stack([ab_re.reshape(1, n), ab_im.reshape(1, n)]), -(-seg // SUBLANES))
    return bs, cs, pw


def _power_table(ab, tiles):
    n = ab.shape[2]

    def body(a_ref, o_ref):
        row = lax.broadcasted_iota(jnp.int32, (SUBLANES, n), 0)
        ar, ai = a_ref[0], a_ref[1]
        tr, ti = jnp.broadcast_to(ar, (SUBLANES, n)), jnp.broadcast_to(ai, (SUBLANES, n))
        pr, pi = ar, ai
        for r in range(1, SUBLANES):
            pr, pi = pr * ar - pi * ai, pr * ai + pi * ar
            tr = jnp.where(row == r, pr, tr)
            ti = jnp.where(row == r, pi, ti)
        o_ref[0, 0:SUBLANES, :] = tr
        o_ref[1, 0:SUBLANES, :] = ti

        def step(j, carry):
            cr, ci = carry
            cr, ci = cr * pr - ci * pi, cr * pi + ci * pr
            o_ref[0, _rows8(j), :] = cr
            o_ref[1, _rows8(j), :] = ci
            return cr, ci

        lax.fori_loop(1, tiles, step, (tr, ti))

    return pl.pallas_call(body, name="power_table", out_shape=jax.ShapeDtypeStruct((2, SUBLANES * tiles, n), f32))(ab)


def _s5_table_grads(dbs, dcs, da):
    eye = jnp.eye(SLAB_GROUPS, dtype=f32)
    d6 = dbs.reshape(N_SLAB, SLAB_GROUPS, SSM_GROUP, 2, SLAB_GROUPS, SSM_STATE)
    dbb = jnp.einsum("sghrkp,gk->rhsgp", d6, eye).reshape(2, SSM_GROUP, SSM_GROUPS * SSM_STATE)
    c6 = dcs.reshape(N_SLAB, 2, SLAB_GROUPS, SSM_STATE, SLAB_GROUPS, SSM_GROUP)
    dcc = jnp.einsum("srkpgh,gk->rsghp", c6, eye).reshape(2, SSM_GROUPS, SSM_GROUP, SSM_STATE)
    dab = da.transpose(1, 0, 2).reshape(2, SSM_GROUPS, SSM_STATE)
    return dab[0], dab[1], dbb[0], dbb[1], dcc[0], -dcc[1]


SMALL = ["mix_norm_g", "ssm_lambda_re", "ssm_lambda_im", "ssm_log_dt", "ssm_b_re", "ssm_b_im", "ssm_c_re", "ssm_c_im",
         "ssm_d", "hgrn_lb_logits", "hgrn_norm_g", "ffn_norm_g", "conv_b", "final_norm_g"]
SHARDED_SMALL = ["meta_tokens", "conv_w"]
BIG = ["w_in", "ssm_w_glu", "w_ssm_proj", "w_hgrn_proj", "w_out", "w_up", "w_down"]
WEIGHTS = ['meta_tokens', 'mix_norm_g', 'w_in', 'ssm_lambda_re', 'ssm_lambda_im', 'ssm_log_dt', 'ssm_b_re', 'ssm_b_im',
           'ssm_c_re', 'ssm_c_im', 'ssm_d', 'ssm_w_glu', 'w_ssm_proj', 'hgrn_lb_logits', 'hgrn_norm_g', 'w_hgrn_proj',
           'w_out', 'ffn_norm_g', 'w_up', 'conv_w', 'conv_b', 'w_down', 'final_norm_g']


LATER = [k for k in BIG if k != "w_in"]


def _full_weights(gathered, shards, chip):
    Dm = D_MODEL
    g = {k: lax.dynamic_update_slice(gathered[k], shards[k][None], (chip, 0, 0)) for k in gathered}
    full = {}
    for k, v in g.items():
        if k == "w_in":
            full[k] = jnp.roll(v.transpose(1, 0, 2).reshape(Dm, IN_COLS), -Dm, axis=1)
        elif k == "w_up":
            full[k] = v.transpose(1, 0, 2).reshape(Dm, 2 * D_FF)
        else:
            full[k] = v.reshape(-1, Dm)
    return full


def _local_grads(x, tgt, meta, w, full, shards, chip, core):
    B, S, Dm = x.shape
    L = S + N_META
    T = B * L
    h0 = jnp.concatenate([jnp.broadcast_to(meta[None], (B, N_META, Dm)), x], axis=1).reshape(T, Dm)

    lb_all = _lower_bound_fwd(w["hgrn_lb_logits"])
    lb = lb_all[0:1]
    gp = SSM_GROUPS * SSM_STATE
    zoh_in = (w["ssm_lambda_re"].reshape(1, gp), w["ssm_lambda_im"].reshape(1, gp),
              jnp.repeat(w["ssm_log_dt"].reshape(SSM_GROUPS, 1), SSM_STATE, axis=1).reshape(1, gp),
              w["ssm_b_re"].reshape(gp, SSM_GROUP).T, w["ssm_b_im"].reshape(gp, SSM_GROUP).T)
    ab_re, ab_im, bb_re, bb_im = _zoh_fwd(*zoh_in)
    bs, cs, pw = _s5_tables(ab_re, ab_im, bb_re, bb_im, w["ssm_c_re"][0], w["ssm_c_im"][0], L // SUBLANES)

    z1 = _rmsnorm_fwd("mix_norm", h0, w["mix_norm_g"])
    p, gathered = _in_proj_gather(z1, full["w_in"], [shards[k] for k in LATER])
    full = {**full, **_full_weights(dict(zip(LATER, gathered)), shards, chip)}
    ya0 = _s5_fwd(p, bs, cs, pw, w["ssm_d"], B, L)
    gl, ya = _glu_proj_fwd(ya0, full["ssm_w_glu"])
    yb = _hgrn_fwd(p, lb, w["hgrn_norm_g"], B, L)
    pa, pb, merged = _proj_merge_fwd(ya, yb, full["w_ssm_proj"], full["w_hgrn_proj"], p)
    h1, z2 = _out_proj_norm(merged, full["w_out"], h0, w["ffn_norm_g"])
    up = _mm_rows("up_proj", z2, full["w_up"], "nn", f32, D_FF // 2)
    ff = _conv_fwd(up, full["conv_w"], w["conv_b"], B, L)
    h2 = _mm_rows("down_proj", ff, full["w_down"], "nn", f32, 1024, res=h1, tk=D_FF // 2)

    tgt_rows = jnp.pad(tgt, ((0, 0), (N_META, 0), (0, 0))).reshape(T, Dm)
    dh2, loss, d_final_g = _final_loss(h2, tgt_rows, w["final_norm_g"].reshape(1, Dm), L)

    dff = _mm_rows("d_ff", dh2, full["w_down"], "nt", f32, D_FF // 2)
    g_w_down = _mm_wgrad("dw_down", ff, dh2, tn=512)
    dup, dconv = _conv_bwd(up, dff, full["conv_w"], w["conv_b"], B, L)
    g_w_up = _dw_up(z2, dup)
    dh1, d_ffn_g = _dz2_norm(dup, full["w_up"], h1, w["ffn_norm_g"], dh2)

    g_w_out = _mm_wgrad("dw_out", merged, dh1)
    dpa, dpb, dp = _merge_bwd_fused(dh1, full["w_out"], p, pa, pb)
    dgl, dya0_direct = _glu_bwd_fused(dpa, full["w_ssm_proj"], ya0, gl)
    g_w_ssm_proj = _mm_wgrad("dw_ssm_proj", ya, dpa)
    dyb = _mm_rows("d_yb", dpb, full["w_hgrn_proj"], "nt", f32, 1024)
    g_w_hgrn_proj = _mm_wgrad("dw_hgrn_proj", yb, dpb)
    dp, d_lb, d_hgrn_g = _hgrn_bwd(p, dyb, dp, lb, w["hgrn_norm_g"], B, L)
    dya0 = _mm_rows("d_ya0", dgl, full["ssm_w_glu"], "nt", f32, 1024, res=dya0_direct)
    g_w_glu = _mm_wgrad("dw_glu", ya0, dgl)
    parts = {
        "ssm_w_glu": g_w_glu.reshape(N_CHIPS, Dm // N_CHIPS, Dm), "w_ssm_proj": g_w_ssm_proj.reshape(N_CHIPS, Dm // N_CHIPS, Dm),
        "w_hgrn_proj": g_w_hgrn_proj.reshape(N_CHIPS, Dm // N_CHIPS, Dm), "w_out": g_w_out.reshape(N_CHIPS, Dm // N_CHIPS, Dm),
        "w_up": g_w_up, "w_down": g_w_down.reshape(N_CHIPS, D_FF // N_CHIPS, Dm),
    }
    got = _sibling_halves([parts[k] for k in LATER])
    sums = {k: _add_own_half("add_half_" + k, parts[k], gt, core) for k, gt in zip(LATER, got)}
    (dp, dbs, dcs, da, d_skip), slots_later = _s5_bwd(p, dya0, dp, bs, cs, pw, w["ssm_d"], B, L, [sums[k] for k in LATER])
    slots = dict(zip(LATER, slots_later))
    g_w_in = _dw_in(z1, dp)
    dh0, d_mix_g = _dz1_norm(dp, full["w_in"], h0, w["mix_norm_g"], dh1)

    dh0 = dh0.reshape(B, L, Dm)
    grad_x = dh0[:, N_META:]
    d_meta = _meta_grad(dh0[:, :N_META])

    d_ab_re, d_ab_im, d_bb_re, d_bb_im, d_c_re, d_c_im = _s5_table_grads(dbs, dcs, da)
    d_lr, d_li, d_log_dt, d_b_re, d_b_im = _zoh_bwd(*zoh_in, d_ab_re.reshape(1, gp), d_ab_im.reshape(1, gp), d_bb_re, d_bb_im)
    gps = (SSM_GROUPS, SSM_STATE)
    d_lr, d_li, d_log_dt = d_lr.reshape(gps), d_li.reshape(gps), d_log_dt.reshape(SSM_GROUPS)
    d_b_re, d_b_im = d_b_re.T.reshape(gps + (SSM_GROUP,)), d_b_im.T.reshape(gps + (SSM_GROUP,))
    d_logits = _lower_bound_bwd(lb_all, d_lb)
    small = {
        "meta_tokens": d_meta, "mix_norm_g": d_mix_g, "ssm_lambda_re": d_lr[None], "ssm_lambda_im": d_li[None],
        "ssm_log_dt": d_log_dt[None], "ssm_b_re": d_b_re[None], "ssm_b_im": d_b_im[None], "ssm_c_re": d_c_re[None],
        "ssm_c_im": d_c_im[None], "ssm_d": d_skip, "hgrn_lb_logits": d_logits, "hgrn_norm_g": d_hgrn_g,
        "ffn_norm_g": d_ffn_g, "conv_w": dconv[:, 0:3, :].transpose(1, 0, 2).reshape(3, 2 * D_FF),
        "conv_b": dconv[:, 3, :].reshape(1, 2 * D_FF), "final_norm_g": d_final_g.reshape(Dm),
    }
    sums["w_in"] = _add_own_half_w_in(g_w_in, _sibling_halves([g_w_in], "sibling_halves_w_in")[0], core)
    slots["w_in"] = _chip_exchange([sums["w_in"]])[0]
    return loss, grad_x, sums, slots, small


PACK_ROWS = 256


def _pack(parts):
    flat = jnp.concatenate([parts[k].reshape(-1) for k in parts])
    n = flat.shape[0]
    rows = -(-n // (PACK_ROWS * LANES)) * PACK_ROWS
    flat = jnp.pad(flat, (0, rows * LANES - n))
    return flat.reshape(rows, LANES)


def _unpack(packed, like):
    flat = packed.reshape(-1)
    out, o = {}, 0
    for k, ref in like.items():
        n = math.prod(ref.shape)
        out[k] = flat[o:o + n].reshape(ref.shape)
        o += n
    return out


def kernel(x, meta_tokens, mix_norm_g, w_in, ssm_lambda_re, ssm_lambda_im, ssm_log_dt, ssm_b_re, ssm_b_im, ssm_c_re, ssm_c_im, ssm_d, ssm_w_glu, w_ssm_proj, hgrn_lb_logits, hgrn_norm_g, w_hgrn_proj, w_out, ffn_norm_g, w_up, conv_w, conv_b, w_down, final_norm_g, loss_target, m_meta_tokens, m_mix_norm_g, m_w_in, m_ssm_lambda_re, m_ssm_lambda_im, m_ssm_log_dt, m_ssm_b_re, m_ssm_b_im, m_ssm_c_re, m_ssm_c_im, m_ssm_d, m_ssm_w_glu, m_w_ssm_proj, m_hgrn_lb_logits, m_hgrn_norm_g, m_w_hgrn_proj, m_w_out, m_ffn_norm_g, m_w_up, m_conv_w, m_conv_b, m_w_down, m_final_norm_g, v_meta_tokens, v_mix_norm_g, v_w_in, v_ssm_lambda_re, v_ssm_lambda_im, v_ssm_log_dt, v_ssm_b_re, v_ssm_b_im, v_ssm_c_re, v_ssm_c_im, v_ssm_d, v_ssm_w_glu, v_w_ssm_proj, v_hgrn_lb_logits, v_hgrn_norm_g, v_w_hgrn_proj, v_w_out, v_ffn_norm_g, v_w_up, v_conv_w, v_conv_b, v_w_down, v_final_norm_g):
    args = dict(locals())
    w = {k: args[k] for k in WEIGHTS}
    mom = {k: args["m_" + k] for k in WEIGHTS}
    var = {k: args["v_" + k] for k in WEIGHTS}
    Dm = D_MODEL
    cx, cy, cc = lax.axis_index("x"), lax.axis_index("y"), lax.axis_index("c")
    chip = 2 * cx + cy

    shards = {k: w[k][0].astype(bf16) for k in BIG}
    g_meta, g_cw = _allgather_chips([w["meta_tokens"], w["conv_w"][0]])
    full = _full_weights({"w_in": _allgather_split([shards["w_in"]])[0]}, shards, chip)
    full["conv_w"] = g_cw.transpose(1, 0, 2).reshape(3, 2 * D_FF)
    meta_full = g_meta.transpose(1, 0, 2).reshape(N_META, Dm)

    core = cc.reshape(1).astype(jnp.int32)
    loss_part, grad_x, sums, slots, small = _local_grads(x, loss_target, meta_full, w, full, shards, chip, core)

    where = jnp.stack([chip, cc]).astype(jnp.int32)
    fulls = [_sum_chips("sum_chips_" + k, slots[k], sums[k], where) for k in BIG]
    g_big = dict(zip(BIG, _sibling_join(fulls)))

    small_all = dict(small)
    small_all["loss"] = loss_part[0, 0:1]
    packed = _pack(small_all)
    slots_dev = lax.dynamic_update_slice(_allgather_devices(packed), packed[None], (2 * chip + cc, 0, 0))
    reduced = _unpack(_sum_slots("sum_devices", slots_dev), small_all)
    loss = reduced.pop("loss")[0]
    mcols = Dm // N_CHIPS
    ccols = 2 * D_FF // N_CHIPS
    grads = {k: reduced[k] for k in SMALL}
    grads["meta_tokens"] = lax.dynamic_slice(reduced["meta_tokens"], (0, chip * mcols), (N_META, mcols))
    grads["conv_w"] = lax.dynamic_slice(reduced["conv_w"], (0, chip * ccols), (3, ccols))[None]
    for k in BIG:
        grads[k] = g_big[k][None]

    delta, new_m, new_v = {}, {}, {}
    for k in BIG:
        shp = w[k].shape
        d, nm, nv = _adamw("adamw_" + k, w[k][0], grads[k][0], mom[k][0], var[k][0])
        delta[k], new_m[k], new_v[k] = d.reshape(shp), nm.reshape(shp), nv.reshape(shp)
    rest = SMALL + SHARDED_SMALL

    def flat2(a):
        return a.reshape(-1, a.shape[-1])

    outs = _adamw_many(*[[flat2(t[k]) for k in rest] for t in (w, grads, mom, var)])
    n = len(rest)
    for j, dst in enumerate((delta, new_m, new_v)):
        dst.update({k: o.reshape(w[k].shape) for k, o in zip(rest, outs[j * n:(j + 1) * n])})

    return (loss, grad_x, *[grads[k].reshape(w[k].shape) for k in WEIGHTS], *[delta[k] for k in WEIGHTS],
            *[new_m[k] for k in WEIGHTS], *[new_v[k] for k in WEIGHTS])
```

```python
import math

import jax
import jax.numpy as jnp
from jax import lax
from jax.experimental import pallas as pl
from jax.experimental.pallas import tpu as pltpu

f32 = jnp.float32
bf16 = jnp.bfloat16

D_MODEL = 1024
N_META = 16
SSM_GROUP = 16
SSM_GROUPS = 64
SSM_STATE = 64
SLAB_GROUPS = 8
N_SLAB = SSM_GROUPS // SLAB_GROUPS
SLAB_CH = SLAB_GROUPS * SSM_GROUP
SLAB_NS = SLAB_GROUPS * SSM_STATE
HEADS = 8
HEAD_DIM = 128
CHUNK = 16
D_FF = 2816
IN_COLS = 7168
EPS = 1e-6
SUBLANES = 8
LANES = 128
N_CHIPS = 4
N_DEV = 8
ADAM_LR, ADAM_B1, ADAM_B2, ADAM_EPS, ADAM_WD, ADAM_STEP = 0.001, 0.9, 0.999, 1e-08, 0.01, 10
MESH = pl.DeviceIdType.MESH
ANY = pl.BlockSpec(memory_space=pl.ANY)

SEG_Q, SEG_F, SEG_I, SEG_OG, SEG_GA, SEG_GB, SEG_U = range(7)
N_SEG = 7


def _tile(n, target, mult=SUBLANES):
    best = None
    for d in range(mult, min(n, target) + 1, mult):
        if n % d == 0:
            best = d
    return n if best is None else best


def _params(*sem):
    return pltpu.CompilerParams(dimension_semantics=sem)


def _sigmoid(x):
    return 1.0 / (1.0 + jnp.exp(-x))


_DIMS = {"nn": (((1,), (0,)), ((), ())), "nt": (((1,), (1,)), ((), ())), "tn": (((0,), (0,)), ((), ()))}


def _mm(name, a, b, dims, grid, a_spec, b_spec, out_shape, out_spec, acc_shape, res=None, res_spec=None):
    nk = grid[2]
    dn = _DIMS[dims]

    def body(*refs):
        if res is None:
            a_ref, b_ref, o_ref, acc = refs
        else:
            a_ref, b_ref, r_ref, o_ref, acc = refs
        k = pl.program_id(2)

        @pl.when(k == 0)
        def _():
            acc[...] = jnp.zeros_like(acc)

        acc[...] += lax.dot_general(a_ref[...].astype(bf16), b_ref[...].astype(bf16), dn, preferred_element_type=f32)

        @pl.when(k == nk - 1)
        def _():
            r = acc[...]
            if res is not None:
                r = r + r_ref[...]
            o_ref[...] = r.astype(o_ref.dtype)

    ins = [a, b] + ([] if res is None else [res])
    specs = [a_spec, b_spec] + ([] if res is None else [res_spec])
    return pl.pallas_call(
        body, name=name, grid=grid, in_specs=specs, out_specs=out_spec, out_shape=out_shape,
        scratch_shapes=[pltpu.VMEM(acc_shape, f32)],
        compiler_params=_params("parallel", "parallel", "arbitrary"),
    )(*ins)


def _mm_rows(name, a, w, dims, out_dtype, tn, res=None, tk=None, tm_target=1032):
    T, K = a.shape
    N = w.shape[1] if dims == "nn" else w.shape[0]
    tm = _tile(T, tm_target)
    tk = K if tk is None else tk
    grid = (T // tm, N // tn, K // tk)
    a_spec = pl.BlockSpec((tm, tk), lambda i, j, k: (i, k))
    if dims == "nn":
        b_spec = pl.BlockSpec((tk, tn), lambda i, j, k: (k, j))
    else:
        b_spec = pl.BlockSpec((tn, tk), lambda i, j, k: (j, k))
    o_spec = pl.BlockSpec((tm, tn), lambda i, j, k: (i, j))
    return _mm(name, a, w, dims, grid, a_spec, b_spec, jax.ShapeDtypeStruct((T, N), out_dtype), o_spec, (tm, tn),
               res=res, res_spec=None if res is None else o_spec)


def _mm_fused(name, pairs, dims, extras, epilogue, outs, rows=(), tm_target=688):
    T, K = pairs[0][0].shape
    N = pairs[0][1].shape[1] if dims == "nn" else pairs[0][1].shape[0]
    tm = _tile(T, tm_target)
    tn = N
    grid = (T // tm, N // tn)
    npair, nex = len(pairs), len(extras) + len(rows)
    dn = _DIMS[dims]

    def body(*refs):
        ab = refs[:2 * npair]
        ex = refs[2 * npair:2 * npair + nex]
        o_refs = refs[2 * npair + nex:]
        accs = [lax.dot_general(ab[2 * q][...].astype(bf16), ab[2 * q + 1][...].astype(bf16), dn, preferred_element_type=f32)
                for q in range(npair)]
        vals = epilogue(accs, [e[...] for e in ex])
        for o_ref, v in zip(o_refs, vals):
            if isinstance(v, (list, tuple)):
                for s_, vs in enumerate(v):
                    o_ref[s_] = vs.astype(o_ref.dtype)
            else:
                o_ref[...] = v.astype(o_ref.dtype)

    ins, specs = [], []
    for a, w in pairs:
        ins += [a, w]
        specs.append(pl.BlockSpec((tm, K), lambda i, j: (i, 0)))
        specs.append(pl.BlockSpec((K, tn), lambda i, j: (0, j)) if dims == "nn" else pl.BlockSpec((tn, K), lambda i, j: (j, 0)))
    for arr, off in extras:
        ins.append(arr)
        specs.append(pl.BlockSpec((tm, tn), lambda i, j, off=off: (i, off + j)))
    for arr in rows:
        ins.append(arr)
        specs.append(pl.BlockSpec((1, tn), lambda i, j: (0, j)))
    shapes, ospecs = [], []
    for o in outs:
        if isinstance(o, tuple):
            dt, nseg, total, blk = o
            shapes.append(jax.ShapeDtypeStruct((total, T, N), dt))
            ospecs.append(pl.BlockSpec((nseg, tm, tn), lambda i, j, blk=blk: (blk, i, j)))
        else:
            shapes.append(jax.ShapeDtypeStruct((T, N), o))
            ospecs.append(pl.BlockSpec((tm, tn), lambda i, j: (i, j)))
    return pl.pallas_call(body, name=name, grid=grid, in_specs=specs, out_specs=ospecs, out_shape=shapes,
                          compiler_params=_params("parallel", "parallel"))(*ins)


def _glu_proj_fwd(ya0, w_glu):
    def epi(accs, tiles):
        return accs[0], tiles[0] * _sigmoid(accs[0])

    return _mm_fused("glu_proj", [(ya0, w_glu)], "nn", [(ya0, 0)], epi, [f32, bf16])


def _proj_merge_fwd(ya, yb, w_sp, w_hp, p):
    def epi(accs, tiles):
        return accs[0], accs[1], _sigmoid(tiles[0]) * accs[0] + _sigmoid(tiles[1]) * accs[1]

    return _mm_fused("proj_merge", [(ya, w_sp), (yb, w_hp)], "nn", [(p, SEG_GA), (p, SEG_GB)], epi, [f32, f32, bf16])


def _merge_bwd_fused(dh1, w_out, p, pa, pb):
    def epi(accs, tiles):
        d = accs[0]
        sa, sb = _sigmoid(tiles[0]), _sigmoid(tiles[1])
        return d * sa, d * sb, [d * tiles[2] * sa * (1.0 - sa), d * tiles[3] * sb * (1.0 - sb)]

    return _mm_fused("d_merged", [(dh1, w_out)], "nt", [(p, SEG_GA), (p, SEG_GB), (pa, 0), (pb, 0)], epi,
                     [bf16, bf16, (bf16, 2, N_SEG, SEG_GA // 2)], tm_target=344)


def _out_proj_norm(merged, w_out, h0, g):
    def epi(accs, tiles):
        h1 = tiles[0] + accs[0]
        r = lax.rsqrt(jnp.mean(h1 * h1, axis=-1, keepdims=True) + EPS)
        return h1, h1 * r * tiles[1]

    return _mm_fused("out_proj", [(merged, w_out)], "nn", [(h0, 0)], epi, [f32, bf16], rows=[g])


def _mm_rmsnorm_bwd(name, a, b, grid, a_spec, b_spec, x, g, dres):
    T, Dm = x.shape
    tm = T // grid[0]
    nk = grid[2]

    def body(a_ref, b_ref, x_ref, g_ref, dres_ref, dx_ref, dg_ref, acc):
        i, k = pl.program_id(0), pl.program_id(2)

        @pl.when(k == 0)
        def _():
            acc[...] = jnp.zeros_like(acc)

        @pl.when((i == 0) & (k == 0))
        def _():
            dg_ref[...] = jnp.zeros_like(dg_ref)

        acc[...] += lax.dot_general(a_ref[...].astype(bf16), b_ref[...].astype(bf16), _DIMS["nt"], preferred_element_type=f32)

        @pl.when(k == nk - 1)
        def _():
            xv = x_ref[...]
            r = lax.rsqrt(jnp.mean(xv * xv, axis=-1, keepdims=True) + EPS)
            xn = xv * r
            dzv = acc[...]
            dzg = dzv * g_ref[...]
            dx_ref[...] = dres_ref[...] + r * (dzg - xn * jnp.mean(dzg * xn, axis=-1, keepdims=True))
            dg_ref[...] += jnp.sum(dzv * xn, axis=0, keepdims=True)

    row = pl.BlockSpec((tm, Dm), lambda i, j, k: (i, 0))
    par = pl.BlockSpec((1, Dm), lambda i, j, k: (0, 0))
    return pl.pallas_call(
        body, name=name, grid=grid, in_specs=[a_spec, b_spec, row, par, row], out_specs=[row, par],
        out_shape=[jax.ShapeDtypeStruct((T, Dm), f32), jax.ShapeDtypeStruct((1, Dm), f32)],
        scratch_shapes=[pltpu.VMEM((tm, Dm), f32)],
        compiler_params=_params("arbitrary", "arbitrary", "arbitrary"),
    )(a, b, x, g, dres)


def _glu_bwd_fused(dpa, w_sp, ya0, gl):
    def epi(accs, tiles):
        d = accs[0]
        s = _sigmoid(tiles[1])
        return d * tiles[0] * s * (1.0 - s), d * s

    return _mm_fused("d_ya", [(dpa, w_sp)], "nt", [(ya0, 0), (gl, 0)], epi, [bf16, f32])


def _mm_wgrad(name, a, g, tn=None):
    T, K = a.shape
    N = g.shape[1]
    tk = _tile(T, 1376 if K <= D_MODEL else 688)
    tn = N if tn is None else tn
    grid = (1, N // tn, T // tk)
    a_spec = pl.BlockSpec((tk, K), lambda i, j, k: (k, 0))
    g_spec = pl.BlockSpec((tk, tn), lambda i, j, k: (k, j))
    o_spec = pl.BlockSpec((K, tn), lambda i, j, k: (0, j))
    return _mm(name, a, g, "tn", grid, a_spec, g_spec, jax.ShapeDtypeStruct((K, N), f32), o_spec, (K, tn))


def _rmsnorm_fwd(name, x, g):
    T, Dm = x.shape
    tr = _tile(T, 688)

    def body(x_ref, g_ref, z_ref):
        xv = x_ref[...]
        r = lax.rsqrt(jnp.mean(xv * xv, axis=-1, keepdims=True) + EPS)
        z_ref[...] = (xv * r * g_ref[...]).astype(z_ref.dtype)

    return pl.pallas_call(
        body, name=name, grid=(T // tr,),
        in_specs=[pl.BlockSpec((tr, Dm), lambda i: (i, 0)), pl.BlockSpec((1, Dm), lambda i: (0, 0))],
        out_specs=pl.BlockSpec((tr, Dm), lambda i: (i, 0)),
        out_shape=jax.ShapeDtypeStruct((T, Dm), bf16), compiler_params=_params("parallel"),
    )(x, g)


def _final_loss(h2, tgt, g, L):
    T, Dm = h2.shape
    tr = _tile(L, 688)
    per_seq = L // tr

    def body(h_ref, t_ref, g_ref, dh_ref, loss_ref, dg_ref):
        pos = (pl.program_id(0) % per_seq) * tr + lax.broadcasted_iota(jnp.int32, (tr, 1), 0)
        live = jnp.where(pos >= N_META, 1.0, 0.0)
        hv = h_ref[...]
        r = lax.rsqrt(jnp.mean(hv * hv, axis=-1, keepdims=True) + EPS)
        xn = hv * r
        gv = g_ref[...]
        err = (xn * gv - t_ref[...]) * live
        dy = err * (1.0 / Dm)
        dyg = dy * gv
        dh_ref[...] = r * (dyg - xn * jnp.mean(dyg * xn, axis=-1, keepdims=True))

        @pl.when(pl.program_id(0) == 0)
        def _():
            dg_ref[...] = jnp.zeros_like(dg_ref)
            loss_ref[...] = jnp.zeros_like(loss_ref)

        dg_ref[...] += jnp.sum(dy * xn, axis=0, keepdims=True)
        loss_ref[...] += jnp.sum(err * err) * (0.5 / Dm)

    row = pl.BlockSpec((tr, Dm), lambda i: (i, 0))
    par = pl.BlockSpec((1, Dm), lambda i: (0, 0))
    return pl.pallas_call(
        body, name="final_loss", grid=(T // tr,), in_specs=[row, row, par],
        out_specs=[row, pl.BlockSpec((1, LANES), lambda i: (0, 0)), par],
        out_shape=[jax.ShapeDtypeStruct((T, Dm), f32), jax.ShapeDtypeStruct((1, LANES), f32), jax.ShapeDtypeStruct((1, Dm), f32)],
        compiler_params=_params("arbitrary"),
    )(h2, tgt, g)


def _meta_grad(dh0_meta):
    B = dh0_meta.shape[0]

    def body(d_ref, o_ref):
        acc = d_ref[0]
        for b in range(1, B):
            acc = acc + d_ref[b]
        o_ref[...] = acc

    return pl.pallas_call(body, name="meta_grad", out_shape=jax.ShapeDtypeStruct(dh0_meta.shape[1:], f32))(dh0_meta)


def _shift_down(x, k, row):
    return jnp.where(row >= k, pltpu.roll(x, k, 0), 0.0)


def _conv_fwd(up, conv_w, conv_b, B, L):
    tc = 256
    nt = D_FF // tc

    def body(xa_ref, xb_ref, wa_ref, wb_ref, ba_ref, bb_ref, o_ref):
        head = 2 * SUBLANES
        row = lax.broadcasted_iota(jnp.int32, (head, tc), 0)

        def gated(conv):
            a = conv(xa_ref, wa_ref, ba_ref)
            b = conv(xb_ref, wb_ref, bb_ref)
            return (a * _sigmoid(a) * b).astype(o_ref.dtype)

        def conv_rolled(x_ref, w_ref, b_ref):
            x = x_ref[...]
            return b_ref[...] + w_ref[0:1, :] * pltpu.roll(x, 2, 0) + w_ref[1:2, :] * pltpu.roll(x, 1, 0) + w_ref[2:3, :] * x

        def conv_head(x_ref, w_ref, b_ref):
            x = x_ref[0:head, :]
            return (b_ref[...] + w_ref[0:1, :] * _shift_down(x, 2, row) + w_ref[1:2, :] * _shift_down(x, 1, row)
                    + w_ref[2:3, :] * x)

        o_ref[...] = gated(conv_rolled)
        o_ref[0:head, :] = gated(conv_head)

    return pl.pallas_call(
        body, name="conv_fwd", grid=(B, nt),
        in_specs=[pl.BlockSpec((L, tc), lambda b, j: (b, j)), pl.BlockSpec((L, tc), lambda b, j: (b, j + nt)),
                  pl.BlockSpec((3, tc), lambda b, j: (0, j)), pl.BlockSpec((3, tc), lambda b, j: (0, j + nt)),
                  pl.BlockSpec((1, tc), lambda b, j: (0, j)), pl.BlockSpec((1, tc), lambda b, j: (0, j + nt))],
        out_specs=pl.BlockSpec((L, tc), lambda b, j: (b, j)),
        out_shape=jax.ShapeDtypeStruct((B * L, D_FF), bf16), compiler_params=_params("parallel", "parallel"),
    )(up, up, conv_w, conv_w, conv_b, conv_b)


CONV_ROWS = 2 * SUBLANES


def _rows16(i):
    return pl.ds(pl.multiple_of(i * CONV_ROWS, CONV_ROWS), CONV_ROWS)


def _conv_taps(x_ref, i, row):
    x = x_ref[_rows16(i), :]
    live = jnp.where(i > 0, 1.0, 0.0)
    r0 = jnp.maximum(i * CONV_ROWS, 2)
    p1 = x_ref[pl.ds(r0 - 1, 1), :] * live
    p2 = x_ref[pl.ds(r0 - 2, 1), :] * live
    x1 = jnp.where(row == 0, p1, pltpu.roll(x, 1, 0))
    x2 = jnp.where(row == 0, p2, jnp.where(row == 1, p1, pltpu.roll(x, 2, 0)))
    return x, x1, x2


def _conv_bwd(up, dff, conv_w, conv_b, B, L):
    tc = 256
    nt = D_FF // tc
    n = L // CONV_ROWS

    def body(xa_ref, xb_ref, d_ref, wa_ref, wb_ref, ba_ref, bb_ref, dup_ref, dw_ref, ga_ref, gb_ref):
        row = lax.broadcasted_iota(jnp.int32, (CONV_ROWS, tc), 0)

        @pl.when(pl.program_id(1) == 0)
        def _():
            dw_ref[...] = jnp.zeros_like(dw_ref)

        zero_tail = jnp.zeros((CONV_ROWS, tc), f32)
        ga_ref[L:L + CONV_ROWS, :] = zero_tail
        gb_ref[L:L + CONV_ROWS, :] = zero_tail

        def fold(v):
            return v[0:SUBLANES, :] + v[SUBLANES:CONV_ROWS, :]

        def step(i, acc):
            taps_a = _conv_taps(xa_ref, i, row)
            taps_b = _conv_taps(xb_ref, i, row)
            a = ba_ref[...] + wa_ref[0:1, :] * taps_a[2] + wa_ref[1:2, :] * taps_a[1] + wa_ref[2:3, :] * taps_a[0]
            b = bb_ref[...] + wb_ref[0:1, :] * taps_b[2] + wb_ref[1:2, :] * taps_b[1] + wb_ref[2:3, :] * taps_b[0]
            s = _sigmoid(a)
            d = d_ref[_rows16(i), :]
            g_a = d * b * s * (1.0 + a * (1.0 - s))
            g_b = d * a * s
            ga_ref[_rows16(i), :] = g_a
            gb_ref[_rows16(i), :] = g_b
            new = []
            for g, (x, x1, x2) in ((g_a, taps_a), (g_b, taps_b)):
                new += [fold(g * x2), fold(g * x1), fold(g * x), fold(g)]
            return tuple(o + v for o, v in zip(acc, new))

        z = jnp.zeros((SUBLANES, tc), f32)
        acc = _repeat_loop(n, step, (z,) * 8)
        for h in range(2):
            for t in range(4):
                dw_ref[h, t:t + 1, :] += jnp.sum(acc[4 * h + t], axis=0, keepdims=True)

        def back(i, c):
            for h, (g_ref, w_ref) in enumerate(((ga_ref, wa_ref), (gb_ref, wb_ref))):
                g = g_ref[_rows16(i), :]
                n1 = g_ref[pl.ds(i * CONV_ROWS + CONV_ROWS, 1), :]
                n2 = g_ref[pl.ds(i * CONV_ROWS + CONV_ROWS + 1, 1), :]
                u1 = jnp.where(row == CONV_ROWS - 1, n1, pltpu.roll(g, CONV_ROWS - 1, 0))
                u2 = jnp.where(row == CONV_ROWS - 1, n2, jnp.where(row == CONV_ROWS - 2, n1, pltpu.roll(g, CONV_ROWS - 2, 0)))
                dup_ref[h, _rows16(i), :] = (w_ref[2:3, :] * g + w_ref[1:2, :] * u1 + w_ref[0:1, :] * u2).astype(dup_ref.dtype)
            return c

        _repeat_loop(n, back, 0)

    return pl.pallas_call(
        body, name="conv_bwd", grid=(nt, B),
        in_specs=[pl.BlockSpec((L, tc), lambda j, b: (b, j)), pl.BlockSpec((L, tc), lambda j, b: (b, j + nt)),
                  pl.BlockSpec((L, tc), lambda j, b: (b, j)),
                  pl.BlockSpec((3, tc), lambda j, b: (0, j)), pl.BlockSpec((3, tc), lambda j, b: (0, j + nt)),
                  pl.BlockSpec((1, tc), lambda j, b: (0, j)), pl.BlockSpec((1, tc), lambda j, b: (0, j + nt))],
        out_specs=[pl.BlockSpec((2, L, tc), lambda j, b: (0, b, j)), pl.BlockSpec((2, SUBLANES, tc), lambda j, b: (0, 0, j))],
        out_shape=[jax.ShapeDtypeStruct((2, B * L, D_FF), bf16), jax.ShapeDtypeStruct((2, SUBLANES, D_FF), f32)],
        scratch_shapes=[pltpu.VMEM((L + CONV_ROWS, tc), f32), pltpu.VMEM((L + CONV_ROWS, tc), f32)],
        compiler_params=_params("parallel", "arbitrary"),
    )(up, up, dff, conv_w, conv_w, conv_b, conv_b)


GELU_C = math.sqrt(2.0 / math.pi)
GELU_A = 0.044715


def _gelu(x):
    return 0.5 * x * (1.0 + jnp.tanh(GELU_C * (x + GELU_A * x * x * x)))


def _gelu_grad(x):
    t = jnp.tanh(GELU_C * (x + GELU_A * x * x * x))
    return 0.5 * (1.0 + t) + 0.5 * x * (1.0 - t * t) * GELU_C * (1.0 + 3.0 * GELU_A * x * x)


def _cmul_add(xr, xi, ar, ai, sr, si):
    return xr + ar * sr - ai * si, xi + ar * si + ai * sr


def _s5_project_in(u_ref, bs_ref, s_ref, L, rc):
    for r in range(0, L, rc):
        s_ref[r:r + rc, :] = jnp.dot(u_ref[r:r + rc, :].astype(bf16), bs_ref[...], preferred_element_type=f32)


def _rows8(i):
    return pl.ds(pl.multiple_of(i * SUBLANES, SUBLANES), SUBLANES)


def _repeat_loop(n, step, init):
    rep = max(u for u in (6, 4, 3, 2, 1) if n % u == 0)

    def body(t, carry):
        for u in range(rep):
            carry = step(t * rep + u, carry)
        return carry

    return lax.fori_loop(0, n // rep, body, init)


def _to_segments(src_ref, dst_ref, seg):
    def step(i, c):
        dst_ref[_rows8(i), :] = src_ref[pl.ds(i, SUBLANES, stride=seg), :]
        return c

    _repeat_loop(seg, step, 0)


def _from_segments(src_ref, dst_ref, seg):
    def step(i, c):
        dst_ref[pl.ds(i, SUBLANES, stride=seg), :] = src_ref[_rows8(i), :]
        return c

    _repeat_loop(seg, step, 0)


def _half_tiles(j, seg, reverse):
    h = seg // 2
    return (_rows8(seg - 1 - j), _rows8(h - 1 - j)) if reverse else (_rows8(j), _rows8(j + h))


def _seg_local_scan(s_ref, ar, ai, seg, reverse):
    ns = SLAB_NS

    def step(j, carry):
        tiles = _half_tiles(j, seg, reverse)
        loaded = [(s_ref[rows, 0:ns], s_ref[rows, ns:2 * ns]) for rows in tiles]
        out = []
        for (xr, xi), (cr, ci) in zip(loaded, (carry[0:2], carry[2:4])):
            out += list(_cmul_add(xr, xi, ar, ai, cr, ci))
        for rows, cr, ci in zip(tiles, out[0::2], out[1::2]):
            s_ref[rows, 0:ns] = cr
            s_ref[rows, ns:2 * ns] = ci
        return tuple(out)

    z = jnp.zeros((SUBLANES, ns), f32)
    return _repeat_loop(seg // 2, step, (z, z, z, z))


def _seg_boundaries(finals, ahr, ahi, reverse):
    fxr, fxi, fyr, fyi = finals
    row = lax.broadcasted_iota(jnp.int32, fxr.shape, 0)
    zero = jnp.zeros_like(fxr[0:1, :])
    xr, xi, yr, yi = (jnp.zeros_like(fxr) for _ in range(4))
    prev = None
    for r in (range(SUBLANES - 1, -1, -1) if reverse else range(SUBLANES)):
        if prev is None:
            nxr, nxi = zero, zero
        else:
            nxr, nxi = _cmul_add(fyr[prev:prev + 1, :], fyi[prev:prev + 1, :], ahr, ahi, nyr, nyi)
        nyr, nyi = _cmul_add(fxr[r:r + 1, :], fxi[r:r + 1, :], ahr, ahi, nxr, nxi)
        xr, xi = jnp.where(row == r, nxr, xr), jnp.where(row == r, nxi, xi)
        yr, yi = jnp.where(row == r, nyr, yr), jnp.where(row == r, nyi, yi)
        prev = r
    return (xr, xi), (yr, yi)


def _s5_states(u_ref, bs_ref, pw_ref, up_ref, s_ref, L, rc):
    seg = L // SUBLANES
    h = seg // 2
    ns = SLAB_NS
    _to_segments(u_ref, up_ref, seg)
    _s5_project_in(up_ref, bs_ref, s_ref, L, rc)
    ar, ai = pw_ref[0, 0:1, :], pw_ref[1, 0:1, :]
    finals = _seg_local_scan(s_ref, ar, ai, seg, False)
    enter = _seg_boundaries(finals, pw_ref[0, h - 1:h, :], pw_ref[1, h - 1:h, :], False)

    def fix(j, c):
        pr, pi = pw_ref[0, pl.ds(j, 1), :], pw_ref[1, pl.ds(j, 1), :]
        tiles = _half_tiles(j, seg, False)
        loaded = [(s_ref[rows, 0:ns], s_ref[rows, ns:2 * ns]) for rows in tiles]
        for rows, (xr, xi), (br, bi) in zip(tiles, loaded, enter):
            xr, xi = _cmul_add(xr, xi, pr, pi, br, bi)
            s_ref[rows, 0:ns] = xr
            s_ref[rows, ns:2 * ns] = xi
        return c

    _repeat_loop(h, fix, 0)


def _pw_spec(seg_rows, order):
    if order == "bs":
        return pl.BlockSpec((2, seg_rows, SLAB_NS), lambda b, s: (0, 0, s))
    return pl.BlockSpec((2, seg_rows, SLAB_NS), lambda s, b: (0, 0, s))


def _s5_fwd(p, bs, cs, pw, d_skip, B, L):
    rc = _tile(L, 344)
    seg = L // SUBLANES

    def body(u_ref, bs_ref, cs_ref, pw_ref, d_ref, y_ref, s_ref, up_ref, yp_ref):
        _s5_states(u_ref, bs_ref, pw_ref, up_ref, s_ref, L, rc)
        for r in range(0, L, rc):
            ypre = (jnp.dot(s_ref[r:r + rc, :].astype(bf16), cs_ref[...], preferred_element_type=f32)
                    + d_ref[...] * up_ref[r:r + rc, :])
            yp_ref[r:r + rc, :] = _gelu(ypre)
        _from_segments(yp_ref, y_ref, seg)

    ucol = SEG_U * (D_MODEL // SLAB_CH)
    return pl.pallas_call(
        body, name="s5_fwd", grid=(B, N_SLAB),
        in_specs=[pl.BlockSpec((L, SLAB_CH), lambda b, s: (b, ucol + s)),
                  pl.BlockSpec((None, SLAB_CH, 2 * SLAB_NS), lambda b, s: (s, 0, 0)),
                  pl.BlockSpec((None, 2 * SLAB_NS, SLAB_CH), lambda b, s: (s, 0, 0)),
                  _pw_spec(pw.shape[1], "bs"),
                  pl.BlockSpec((1, SLAB_CH), lambda b, s: (0, s))],
        out_specs=pl.BlockSpec((L, SLAB_CH), lambda b, s: (b, s)),
        out_shape=jax.ShapeDtypeStruct((B * L, D_MODEL), f32),
        scratch_shapes=[pltpu.VMEM((L, 2 * SLAB_NS), f32), pltpu.VMEM((L, SLAB_CH), f32), pltpu.VMEM((L, SLAB_CH), f32)],
        compiler_params=_params("parallel", "parallel"),
    )(p, bs, cs, pw, d_skip)


def _s5_bwd(p, dya0, dp, bs, cs, pw, d_skip, B, L, sums):
    rc = _tile(L, 688)
    ns = SLAB_NS
    seg = L // SUBLANES
    nx = len(sums)

    def body(u_ref, dy_ref, dp_in, bs_ref, cs_ref, pw_ref, d_ref, *rest):
        xin, (du_ref, dbs_ref, dcs_ref, da_ref, dd_ref), xout = rest[:nx], rest[nx:nx + 5], rest[nx + 5:2 * nx + 5]
        s_ref, lam_ref, up_ref, dyp_ref, nat_ref, send, recv = rest[2 * nx + 5:]
        del dp_in
        start, finish = _chip_exchange_steps(xin, xout, send, recv)

        @pl.when((pl.program_id(0) == 0) & (pl.program_id(1) == 0))
        def _():
            start()

        @pl.when(pl.program_id(1) == 0)
        def _():
            dbs_ref[...] = jnp.zeros_like(dbs_ref)
            dcs_ref[...] = jnp.zeros_like(dcs_ref)
            da_ref[...] = jnp.zeros_like(da_ref)
            dd_ref[...] = jnp.zeros_like(dd_ref)

        _s5_states(u_ref, bs_ref, pw_ref, up_ref, s_ref, L, rc)
        _to_segments(dy_ref, dyp_ref, seg)
        for r in range(0, L, rc):
            u = up_ref[r:r + rc, :]
            sb = s_ref[r:r + rc, :].astype(bf16)
            ypre = jnp.dot(sb, cs_ref[...], preferred_element_type=f32) + d_ref[...] * u
            dyp = dyp_ref[r:r + rc, :] * _gelu_grad(ypre)
            dyp_ref[r:r + rc, :] = dyp
            dd_ref[...] += jnp.sum(dyp * u, axis=0, keepdims=True)
            dypb = dyp.astype(bf16)
            dcs_ref[...] += lax.dot_general(sb, dypb, _DIMS["tn"], preferred_element_type=f32)
            lam_ref[r:r + rc, :] = lax.dot_general(dypb, cs_ref[...], _DIMS["nt"], preferred_element_type=f32)

        h = seg // 2
        ar, ai = pw_ref[0, 0:1, :], -pw_ref[1, 0:1, :]
        finals = _seg_local_scan(lam_ref, ar, ai, seg, True)
        enter = _seg_boundaries(finals, pw_ref[0, h - 1:h, :], -pw_ref[1, h - 1:h, :], True)

        def fix(j, acc):
            accr, acci = acc
            pr, pi = pw_ref[0, pl.ds(j, 1), :], -pw_ref[1, pl.ds(j, 1), :]
            tiles = _half_tiles(j, seg, True)
            loaded = [(lam_ref[rows, 0:ns], lam_ref[rows, ns:2 * ns]) for rows in tiles]
            for rows, (xr, xi), (br, bi), t in zip(tiles, loaded, enter, (seg - 1 - j, h - 1 - j)):
                xr, xi = _cmul_add(xr, xi, pr, pi, br, bi)
                lam_ref[rows, 0:ns] = xr
                lam_ref[rows, ns:2 * ns] = xi
                prev = _rows8(jnp.maximum(t - 1, 0))
                live = jnp.where(t > 0, 1.0, 0.0)
                spr = s_ref[prev, 0:ns] * live
                spi = s_ref[prev, ns:2 * ns] * live
                accr, acci = accr + xr * spr + xi * spi, acci + xi * spr - xr * spi
            return accr, acci

        z = jnp.zeros((SUBLANES, ns), f32)
        accr, acci = _repeat_loop(h, fix, (z, z))
        row = lax.broadcasted_iota(jnp.int32, (SUBLANES, ns), 0)
        last = _rows8(seg - 1)
        spr = jnp.where(row == 0, 0.0, pltpu.roll(s_ref[last, 0:ns], 1, 0))
        spi = jnp.where(row == 0, 0.0, pltpu.roll(s_ref[last, ns:2 * ns], 1, 0))
        xr, xi = lam_ref[0:SUBLANES, 0:ns], lam_ref[0:SUBLANES, ns:2 * ns]
        accr = accr + xr * spr + xi * spi
        acci = acci + xi * spr - xr * spi
        da_ref[0:1, :] += jnp.sum(accr, axis=0, keepdims=True)
        da_ref[1:2, :] += jnp.sum(acci, axis=0, keepdims=True)

        for r in range(0, L, rc):
            lamb = lam_ref[r:r + rc, :].astype(bf16)
            dbs_ref[...] += lax.dot_general(up_ref[r:r + rc, :].astype(bf16), lamb, _DIMS["tn"], preferred_element_type=f32)
            nat_ref[r:r + rc, :] = (lax.dot_general(lamb, bs_ref[...], _DIMS["nt"], preferred_element_type=f32)
                                    + d_ref[...] * dyp_ref[r:r + rc, :])
        _from_segments(nat_ref, up_ref, seg)
        du_ref[...] = up_ref[...].astype(du_ref.dtype)

        @pl.when((pl.program_id(0) == N_SLAB - 1) & (pl.program_id(1) == B - 1))
        def _():
            finish()

    ucol = SEG_U * (D_MODEL // SLAB_CH)
    T = B * L
    col = pltpu.VMEM((L, SLAB_CH), f32)
    res = pl.pallas_call(
        body, name="s5_bwd", grid=(N_SLAB, B),
        in_specs=[pl.BlockSpec((L, SLAB_CH), lambda s, b: (b, ucol + s)),
                  pl.BlockSpec((L, SLAB_CH), lambda s, b: (b, s)),
                  ANY,
                  pl.BlockSpec((None, SLAB_CH, 2 * SLAB_NS), lambda s, b: (s, 0, 0)),
                  pl.BlockSpec((None, 2 * SLAB_NS, SLAB_CH), lambda s, b: (s, 0, 0)),
                  _pw_spec(pw.shape[1], "sb"),
                  pl.BlockSpec((1, SLAB_CH), lambda s, b: (0, s))] + [ANY] * nx,
        out_specs=[pl.BlockSpec((None, L, SLAB_CH), lambda s, b: (SEG_U, b, s)),
                   pl.BlockSpec((None, SLAB_CH, 2 * SLAB_NS), lambda s, b: (s, 0, 0)),
                   pl.BlockSpec((None, 2 * SLAB_NS, SLAB_CH), lambda s, b: (s, 0, 0)),
                   pl.BlockSpec((None, 2, SLAB_NS), lambda s, b: (s, 0, 0)),
                   pl.BlockSpec((1, SLAB_CH), lambda s, b: (0, s))] + [ANY] * nx,
        out_shape=[jax.ShapeDtypeStruct((N_SEG, T, D_MODEL), bf16),
                   jax.ShapeDtypeStruct((N_SLAB, SLAB_CH, 2 * SLAB_NS), f32),
                   jax.ShapeDtypeStruct((N_SLAB, 2 * SLAB_NS, SLAB_CH), f32),
                   jax.ShapeDtypeStruct((N_SLAB, 2, SLAB_NS), f32),
                   jax.ShapeDtypeStruct((1, D_MODEL), f32)] + [jax.ShapeDtypeStruct(a.shape, a.dtype) for a in sums],
        scratch_shapes=[pltpu.VMEM((L, 2 * SLAB_NS), f32), pltpu.VMEM((L, 2 * SLAB_NS), f32), col, col, col]
        + _chip_exchange_sems(nx),
        input_output_aliases={2: 0},
        compiler_params=_params("arbitrary", "arbitrary"),
    )(p, dya0, dp, bs, cs, pw, d_skip, *sums)
    return res[:5], res[5:]


def _dotb(a, b, dims="nn"):
    return lax.dot_general(a.astype(bf16), b.astype(bf16), _DIMS[dims], preferred_element_type=f32)


def _tile_scan(x, reverse):
    n, w = x.shape
    v = x.reshape(n // SUBLANES, SUBLANES, w)
    row = lax.broadcasted_iota(jnp.int32, v.shape, 1)
    for k in (1, 2, 4):
        if reverse:
            v = v + jnp.where(row < SUBLANES - k, pltpu.roll(v, SUBLANES - k, 1), 0.0)
        else:
            v = v + jnp.where(row >= k, pltpu.roll(v, k, 1), 0.0)
    p = v.reshape(n // CHUNK, 2, SUBLANES, w)
    lo, hi = p[:, 0], p[:, 1]
    if reverse:
        lo = lo + hi[:, 0:1, :]
    else:
        hi = hi + lo[:, SUBLANES - 1:SUBLANES, :]
    return jnp.stack([lo, hi], axis=1).reshape(n, w)


def _chunk_cumsum(x):
    return _tile_scan(x, False)


def _chunk_rev_cumsum(x):
    return _tile_scan(x, True)


def _chunk_last(x):
    n, w = x.shape
    p = x.reshape(n // CHUNK, CHUNK, w)
    return jnp.broadcast_to(p[:, CHUNK - 1:CHUNK, :], p.shape).reshape(n, w)


def _hgrn_local(q, fl, lb):
    sg = _sigmoid(fl)
    f = lb + (1.0 - lb) * sg
    g = jnp.log(f)
    cum = _chunk_cumsum(g)
    rest = _chunk_last(cum) - cum
    e = jnp.exp(cum)
    em = jnp.exp(-cum)
    eo = jnp.exp(rest)
    k = 1.0 - f
    return sg, f, e, em, eo, q * e, k * em, k * eo, cum + rest


def _chunk_pos(n):
    return lax.broadcasted_iota(jnp.int32, (n, HEAD_DIM), 0) & (CHUNK - 1)


def _hgrn_block_rows(L):
    return _tile(L, 688, CHUNK)


def _hgrn_specs(L, order):
    hb = D_MODEL // HEAD_DIM

    def spec(seg):
        if order == "bh":
            return pl.BlockSpec((L, HEAD_DIM), lambda b, h: (b, seg * hb + h))
        return pl.BlockSpec((L, HEAD_DIM), lambda h, b: (b, seg * hb + h))

    return [spec(SEG_Q), spec(SEG_F), spec(SEG_I), spec(SEG_OG)]


PAIR = 2 * CHUNK
CHUNK_SHIFT = CHUNK.bit_length() - 1


def _pair_steps(L, rb):
    steps = []
    nch = rb // CHUNK
    for r in range(0, L, rb):
        steps += [(r + p * PAIR, PAIR) for p in range(nch // 2)]
        if nch % 2:
            steps.append((r + (nch - 1) * CHUNK, CHUNK))
    return steps


def _pair_flags(rb):
    ci = lax.broadcasted_iota(jnp.int32, (rb, HEAD_DIM), 0) >> CHUNK_SHIFT
    odd = (ci & 1) == 1
    has_next = jnp.logical_and(jnp.logical_not(odd), ci < rb // CHUNK - 1)
    return odd, has_next


def _pair_masks(rb):
    r = lax.broadcasted_iota(jnp.int32, (rb, rb), 0)
    c = lax.broadcasted_iota(jnp.int32, (rb, rb), 1)
    rc, cc = r >> CHUNK_SHIFT, c >> CHUNK_SHIFT
    same = (rc == cc) & (c <= r)
    prev = ((rc & 1) == 1) & (cc == rc - 1)
    return same, prev


def _hgrn_pair_local(q, fl, lb, odd, has_next):
    sg, f, e, em, eo, qt, kt, ko, cend = _hgrn_local(q, fl, lb)
    n = q.shape[0]
    a = jnp.where(odd, pltpu.roll(cend, CHUNK, 0), 0.0)
    z = jnp.where(has_next, pltpu.roll(cend, n - CHUNK, 0), 0.0)
    ea, ez = jnp.exp(a), jnp.exp(z)
    return dict(sg=sg, f=f, e=e, em=em, eo=eo, qt=qt, kt=kt, ko=ko, ea=ea, ez=ez, qs=qt * ea, ks=ko * ez,
                decp=jnp.exp(cend + a + z))


def _pair_scores(qt, kt, ko, same, prev):
    return (jnp.where(same, _dotb(qt, kt, "nt"), 0.0) + jnp.where(prev, _dotb(qt, ko, "nt"), 0.0)).astype(bf16)


def _hgrn_fwd(p, lb, norm_g, B, L):
    rb = _hgrn_block_rows(L)
    steps = _pair_steps(L, rb)
    blocks = [slice(r, r + rb) for r in range(0, L, rb)]

    def body(q_ref, f_ref, v_ref, og_ref, lb_ref, ng_ref, y_ref, qs_s, ks_s, vb_s, decp_s, o_s, o2_s, u_s, sb_s):
        lbv = lb_ref[...]
        ngv = ng_ref[...]
        same, prev = _pair_masks(rb)
        odd, has_next = _pair_flags(rb)

        for rows in blocks:
            t = _hgrn_pair_local(q_ref[rows, :], f_ref[rows, :], lbv, odd, has_next)
            vb = v_ref[rows, :].astype(bf16)
            o_s[rows, :] = _dotb(_pair_scores(t["qt"], t["kt"], t["ko"], same, prev), vb)
            qs_s[rows, :] = t["qs"].astype(bf16)
            ks_s[rows, :] = t["ks"].astype(bf16)
            vb_s[rows, :] = vb
            decp_s[rows, :] = t["decp"]

        for n, (r0, nr) in enumerate(steps):
            u_s[n] = _dotb(vb_s[r0:r0 + nr, :], ks_s[r0:r0 + nr, :], "tn")
        st = jnp.zeros((HEAD_DIM, HEAD_DIM), f32)
        for n, (r0, nr) in enumerate(steps):
            sb_s[n] = st.astype(bf16)
            st = st * decp_s[r0:r0 + 1, :] + u_s[n]
        for n, (r0, nr) in enumerate(steps):
            o2_s[r0:r0 + nr, :] = _dotb(qs_s[r0:r0 + nr, :], sb_s[n], "nt")

        for rows in blocks:
            o = o_s[rows, :] + o2_s[rows, :]
            og = og_ref[rows, :]
            on = o * lax.rsqrt(jnp.mean(o * o, axis=-1, keepdims=True) + EPS) * ngv
            y_ref[rows, :] = (on * og * _sigmoid(og)).astype(y_ref.dtype)

    sb = pltpu.VMEM((L, HEAD_DIM), bf16)
    sf = pltpu.VMEM((L, HEAD_DIM), f32)
    return pl.pallas_call(
        body, name="hgrn_fwd", grid=(B, HEADS),
        in_specs=_hgrn_specs(L, "bh") + [pl.BlockSpec((1, HEAD_DIM), lambda b, h: (0, h)),
                                          pl.BlockSpec((1, HEAD_DIM), lambda b, h: (0, 0))],
        out_specs=pl.BlockSpec((L, HEAD_DIM), lambda b, h: (b, h)),
        out_shape=jax.ShapeDtypeStruct((B * L, D_MODEL), bf16),
        scratch_shapes=[sb, sb, sb, sf, sf, sf, pltpu.VMEM((len(steps), HEAD_DIM, HEAD_DIM), f32),
                        pltpu.VMEM((len(steps), HEAD_DIM, HEAD_DIM), bf16)],
        compiler_params=_params("parallel", "parallel"),
    )(p, p, p, p, lb, norm_g)


def _hgrn_bwd(p, dyb, dp, lb, norm_g, B, L):
    rb = _hgrn_block_rows(L)
    steps = _pair_steps(L, rb)
    blocks = [slice(r, r + rb) for r in range(0, L, rb)]

    def body(q_ref, f_ref, v_ref, og_ref, dy_ref, dp_in, lb_ref, ng_ref, dseg_ref, dlb_ref, dng_ref,
             st_ref, u_s, dsb_s, qt_s, kt_s, ko_s, qs_s, ks_s, vb_s, do_s,
             decp_s, o_s, o2_s, dqt_s, dkt_s, dko_s, dv_s, dv2_s, dqs_s, dks_s, ddecp_s):
        del dp_in
        lbv = lb_ref[...]
        ngv = ng_ref[...]
        same, prev = _pair_masks(rb)
        odd, has_next = _pair_flags(rb)
        pos = _chunk_pos(rb)

        @pl.when(pl.program_id(1) == 0)
        def _():
            dlb_ref[...] = jnp.zeros_like(dlb_ref)

        @pl.when((pl.program_id(0) == 0) & (pl.program_id(1) == 0))
        def _():
            dng_ref[...] = jnp.zeros_like(dng_ref)

        def scores(rows):
            return _pair_scores(qt_s[rows, :], kt_s[rows, :], ko_s[rows, :], same, prev)

        for rows in blocks:
            t = _hgrn_pair_local(q_ref[rows, :], f_ref[rows, :], lbv, odd, has_next)
            for dst, key in ((qt_s, "qt"), (kt_s, "kt"), (ko_s, "ko"), (qs_s, "qs"), (ks_s, "ks")):
                dst[rows, :] = t[key].astype(bf16)
            vb_s[rows, :] = v_ref[rows, :].astype(bf16)
            decp_s[rows, :] = t["decp"]
            o_s[rows, :] = _dotb(scores(rows), vb_s[rows, :])

        for n, (r0, nr) in enumerate(steps):
            u_s[n] = _dotb(vb_s[r0:r0 + nr, :], ks_s[r0:r0 + nr, :], "tn")
        st = jnp.zeros((HEAD_DIM, HEAD_DIM), f32)
        for n, (r0, nr) in enumerate(steps):
            st_ref[n] = st
            st = st * decp_s[r0:r0 + 1, :] + u_s[n]
        for n, (r0, nr) in enumerate(steps):
            o2_s[r0:r0 + nr, :] = _dotb(qs_s[r0:r0 + nr, :], st_ref[n], "nt")

        dng = jnp.zeros((1, HEAD_DIM), f32)
        for rows in blocks:
            o = o_s[rows, :] + o2_s[rows, :]
            og = og_ref[rows, :]
            dy = dy_ref[rows, :]
            rs = lax.rsqrt(jnp.mean(o * o, axis=-1, keepdims=True) + EPS)
            xn = o * rs
            so = _sigmoid(og)
            dseg_ref[SEG_OG, rows, :] = (dy * xn * ngv * so * (1.0 + og * (1.0 - so))).astype(dseg_ref.dtype)
            don = dy * og * so
            dng = dng + jnp.sum(don * xn, axis=0, keepdims=True)
            dxo = don * ngv
            do = (rs * (dxo - xn * jnp.mean(dxo * xn, axis=-1, keepdims=True))).astype(bf16)
            do_s[rows, :] = do
            dpf = _dotb(do, vb_s[rows, :], "nt")
            dp1 = jnp.where(same, dpf, 0.0).astype(bf16)
            dp2 = jnp.where(prev, dpf, 0.0).astype(bf16)
            dqt_s[rows, :] = _dotb(dp1, kt_s[rows, :]) + _dotb(dp2, ko_s[rows, :])
            dkt_s[rows, :] = _dotb(dp1, qt_s[rows, :], "tn")
            dko_s[rows, :] = _dotb(dp2, qt_s[rows, :], "tn")
            dv_s[rows, :] = _dotb(scores(rows), do, "tn")
        dng_ref[...] += dng

        for n, (r0, nr) in enumerate(steps):
            u_s[n] = _dotb(do_s[r0:r0 + nr, :], qs_s[r0:r0 + nr, :], "tn")
        dst = jnp.zeros((HEAD_DIM, HEAD_DIM), f32)
        for n, (r0, nr) in reversed(list(enumerate(steps))):
            dsb_s[n] = dst.astype(bf16)
            ddecp_s[r0:r0 + nr, :] = jnp.broadcast_to(jnp.sum(dst * st_ref[n], axis=0, keepdims=True), (nr, HEAD_DIM))
            dst = dst * decp_s[r0:r0 + 1, :] + u_s[n]
        for n, (r0, nr) in enumerate(steps):
            rows = slice(r0, r0 + nr)
            dqs_s[rows, :] = _dotb(do_s[rows, :], st_ref[n])
            dv2_s[rows, :] = _dotb(ks_s[rows, :], dsb_s[n], "nt")
            dks_s[rows, :] = _dotb(vb_s[rows, :], dsb_s[n])

        def chunk_sum(x):
            return _chunk_last(_chunk_cumsum(x))

        dlb = jnp.zeros((1, HEAD_DIM), f32)
        for rows in blocks:
            t = _hgrn_pair_local(q_ref[rows, :], f_ref[rows, :], lbv, odd, has_next)
            dqs, dks = dqs_s[rows, :], dks_s[rows, :]
            dqt = dqt_s[rows, :] + dqs * t["ea"]
            dko = dko_s[rows, :] + dks * t["ez"]
            dkt = dkt_s[rows, :]
            dko_ko = dko * t["ko"]
            dcum = dqt * t["qt"] - dkt * t["kt"] - dko_ko
            from_next = pltpu.roll(chunk_sum(jnp.where(odd, dqs * t["qs"], 0.0)), rb - CHUNK, 0)
            from_prev = pltpu.roll(chunk_sum(jnp.where(has_next, dks * t["ks"], 0.0)), CHUNK, 0)
            d_end = (chunk_sum(dko_ko) + jnp.where(has_next, from_next, 0.0) + jnp.where(odd, from_prev, 0.0)
                     + ddecp_s[rows, :] * t["decp"])
            dcum = dcum + jnp.where(pos == CHUNK - 1, d_end, 0.0)
            df = _chunk_rev_cumsum(dcum) / t["f"] - (dkt * t["em"] + dko * t["eo"])
            dlb = dlb + jnp.sum(df * (1.0 - t["sg"]), axis=0, keepdims=True)
            dseg_ref[SEG_Q, rows, :] = (dqt * t["e"]).astype(dseg_ref.dtype)
            dseg_ref[SEG_F, rows, :] = (df * (1.0 - lbv) * t["sg"] * (1.0 - t["sg"])).astype(dseg_ref.dtype)
            dseg_ref[SEG_I, rows, :] = (dv_s[rows, :] + dv2_s[rows, :]).astype(dseg_ref.dtype)
        dlb_ref[...] += dlb

    T = B * L
    ns = len(steps)
    sb = pltpu.VMEM((L, HEAD_DIM), bf16)
    sf = pltpu.VMEM((L, HEAD_DIM), f32)
    return pl.pallas_call(
        body, name="hgrn_bwd", grid=(HEADS, B),
        in_specs=_hgrn_specs(L, "hb") + [pl.BlockSpec((L, HEAD_DIM), lambda h, b: (b, h)), ANY,
                                          pl.BlockSpec((1, HEAD_DIM), lambda h, b: (0, h)),
                                          pl.BlockSpec((1, HEAD_DIM), lambda h, b: (0, 0))],
        out_specs=[pl.BlockSpec((4, L, HEAD_DIM), lambda h, b: (0, b, h)),
                   pl.BlockSpec((1, HEAD_DIM), lambda h, b: (0, h)),
                   pl.BlockSpec((1, HEAD_DIM), lambda h, b: (0, 0))],
        out_shape=[jax.ShapeDtypeStruct((N_SEG, T, D_MODEL), bf16), jax.ShapeDtypeStruct((1, D_MODEL), f32),
                   jax.ShapeDtypeStruct((1, HEAD_DIM), f32)],
        scratch_shapes=[pltpu.VMEM((ns, HEAD_DIM, HEAD_DIM), f32), pltpu.VMEM((ns, HEAD_DIM, HEAD_DIM), f32),
                        pltpu.VMEM((ns, HEAD_DIM, HEAD_DIM), bf16)] + [sb] * 7 + [sf] * 11,
        input_output_aliases={5: 0},
        compiler_params=_params("arbitrary", "arbitrary"),
    )(p, p, p, p, dyb, dp, lb, norm_g)


def _dz1_norm(dp, w_in_phys, h0, g, dh1):
    _, T, Dm = dp.shape
    tm = _tile(T, 1032)
    return _mm_rmsnorm_bwd("dz1", dp, w_in_phys, (T // tm, 1, N_SEG),
                           pl.BlockSpec((None, tm, Dm), lambda i, j, k: (k, i, 0)),
                           pl.BlockSpec((Dm, Dm), lambda i, j, k: (0, k)), h0, g, dh1)


def _dz2_norm(dup, w_up, h1, g, dh2):
    _, T, _ = dup.shape
    tm = _tile(T, 1032)
    tk = D_FF // 2
    return _mm_rmsnorm_bwd("dz2", dup, w_up, (T // tm, 1, 4),
                           pl.BlockSpec((None, tm, tk), lambda i, j, k: (k // 2, i, k % 2)),
                           pl.BlockSpec((D_MODEL, tk), lambda i, j, k: (0, k)), h1, g, dh2)


def _dw_in(z1, dp):
    _, T, Dm = dp.shape
    tk = _tile(T, 1376)
    return _mm("dw_in", z1, dp, "tn", (1, N_SEG, T // tk),
               pl.BlockSpec((tk, Dm), lambda i, j, k: (k, 0)),
               pl.BlockSpec((None, tk, Dm), lambda i, j, k: (j, k, 0)),
               jax.ShapeDtypeStruct((N_SEG, Dm, Dm), f32),
               pl.BlockSpec((None, Dm, Dm), lambda i, j, k: (j, 0, 0)), (Dm, Dm))


def _dw_up(z2, dup):
    _, T, _ = dup.shape
    tn = D_FF // 2
    tk = _tile(T, 1376)
    return _mm("dw_up", z2, dup, "tn", (1, N_CHIPS, T // tk),
               pl.BlockSpec((tk, D_MODEL), lambda i, j, k: (k, 0)),
               pl.BlockSpec((None, tk, tn), lambda i, j, k: (j // 2, k, j % 2)),
               jax.ShapeDtypeStruct((N_CHIPS, D_MODEL, tn), f32),
               pl.BlockSpec((None, D_MODEL, tn), lambda i, j, k: (j, 0, 0)), (D_MODEL, tn))


def _place():
    x, y, c = lax.axis_index("x"), lax.axis_index("y"), lax.axis_index("c")
    chips = [(1 - x, y), (x, 1 - y), (1 - x, 1 - y)]
    return x, y, c, chips


def _allgather_chips(arrs):
    n = len(arrs)

    def body(*refs):
        ins, outs = refs[:n], refs[n:2 * n]
        send, recv, local = refs[2 * n:]
        x, y, c, chips = _place()
        me = 2 * x + y

        def copy(a, k, slot):
            px, py = chips[k]
            return pltpu.make_async_remote_copy(src_ref=ins[a], dst_ref=outs[a].at[slot], send_sem=send.at[3 * a + k],
                                                recv_sem=recv.at[3 * a + k], device_id=(px, py, c), device_id_type=MESH)

        for a in range(n):
            pltpu.make_async_copy(ins[a], outs[a].at[me], local.at[a]).start()
            for k in range(3):
                copy(a, k, me).start()
        for a in range(n):
            for k, (px, py) in enumerate(chips):
                copy(a, k, 2 * px + py).wait_recv()
        for a in range(n):
            pltpu.make_async_copy(ins[a], outs[a].at[me], local.at[a]).wait()
            for k in range(3):
                copy(a, k, me).wait_send()

    return pl.pallas_call(
        body, name="allgather_chips", in_specs=[ANY] * n, out_specs=[ANY] * n,
        out_shape=[jax.ShapeDtypeStruct((N_CHIPS,) + a.shape, a.dtype) for a in arrs],
        scratch_shapes=[pltpu.SemaphoreType.DMA((3 * n,)), pltpu.SemaphoreType.DMA((3 * n,)), pltpu.SemaphoreType.DMA((n,))],
    )(*arrs)


def _allgather_split(arrs):
    n = len(arrs)

    def body(*refs):
        start, finish = _gather_split_steps(refs[:n], refs[n:2 * n], *refs[2 * n:])
        start()
        finish()

    return pl.pallas_call(
        body, name="allgather_split", in_specs=[ANY] * n, out_specs=[ANY] * n,
        out_shape=[jax.ShapeDtypeStruct((N_CHIPS,) + a.shape, a.dtype) for a in arrs],
        scratch_shapes=_gather_split_sems(n),
    )(*arrs)


def _gather_split_sems(n):
    return [pltpu.SemaphoreType.DMA((3 * n,)) for _ in range(4)]


def _gather_split_steps(ins, outs, send, recv, fsend, frecv):
    n = len(ins)

    def place():
        x, y, c, chips = _place()
        return x, y, c, chips, 2 * x + y

    def half(a, core):
        rh = ins[a].shape[0] // 2
        return pl.ds(core * rh, rh)

    def copy(a, k, slot):
        x, y, c, chips, _ = place()
        px, py = chips[k]
        return pltpu.make_async_remote_copy(src_ref=ins[a].at[half(a, c), :], dst_ref=outs[a].at[slot, half(a, c), :],
                                            send_sem=send.at[3 * a + k], recv_sem=recv.at[3 * a + k],
                                            device_id=(px, py, c), device_id_type=MESH)

    def forward(a, k, core):
        x, y, c, chips, _ = place()
        px, py = chips[k]
        rows = outs[a].at[2 * px + py, half(a, core), :]
        return pltpu.make_async_remote_copy(src_ref=rows, dst_ref=rows, send_sem=fsend.at[3 * a + k],
                                            recv_sem=frecv.at[3 * a + k], device_id=(x, y, 1 - c), device_id_type=MESH)

    def start():
        me = place()[4]
        for a in range(n):
            for k in range(3):
                copy(a, k, me).start()

    def finish():
        x, y, c, chips, me = place()
        for a in range(n):
            for k, (px, py) in enumerate(chips):
                copy(a, k, 2 * px + py).wait_recv()
                forward(a, k, c).start()
        for a in range(n):
            for k in range(3):
                forward(a, k, 1 - c).wait_recv()
        for a in range(n):
            for k in range(3):
                copy(a, k, me).wait_send()
                forward(a, k, c).wait_send()

    return start, finish


def _in_proj_gather(z1, w_in, shards):
    n = len(shards)
    T, K = z1.shape
    N = w_in.shape[1]
    tm = _tile(T, 2064)
    tn = 1024
    grid = (T // tm, N // tn)

    def body(a_ref, b_ref, *rest):
        ins, o_ref, outs, sems = rest[:n], rest[n], rest[n + 1:2 * n + 1], rest[2 * n + 1:]
        start, finish = _gather_split_steps(ins, outs, *sems)
        i, j = pl.program_id(0), pl.program_id(1)

        @pl.when((i == 0) & (j == 0))
        def _():
            start()

        o_ref[...] = jnp.dot(a_ref[...], b_ref[...], preferred_element_type=f32)

        @pl.when((i == grid[0] - 1) & (j == grid[1] - 1))
        def _():
            finish()

    res = pl.pallas_call(
        body, name="in_proj", grid=grid,
        in_specs=[pl.BlockSpec((tm, K), lambda i, j: (i, 0)), pl.BlockSpec((K, tn), lambda i, j: (0, j))] + [ANY] * n,
        out_specs=[pl.BlockSpec((tm, tn), lambda i, j: (i, j))] + [ANY] * n,
        out_shape=[jax.ShapeDtypeStruct((T, N), f32)] + [jax.ShapeDtypeStruct((N_CHIPS,) + a.shape, a.dtype) for a in shards],
        scratch_shapes=_gather_split_sems(n),
        compiler_params=_params("arbitrary", "arbitrary"),
    )(z1, w_in, *shards)
    return res[0], res[1:]


def _sibling_halves(parts, name="sibling_halves"):
    n = len(parts)

    def body(*refs):
        ins, outs = refs[:n], refs[n:2 * n]
        send, recv = refs[2 * n:]
        x, y, c, _ = _place()

        def copy(a):
            rh = ins[a].shape[1] // 2
            return pltpu.make_async_remote_copy(src_ref=ins[a].at[:, pl.ds((1 - c) * rh, rh), :], dst_ref=outs[a],
                                                send_sem=send.at[a], recv_sem=recv.at[a], device_id=(x, y, 1 - c),
                                                device_id_type=MESH)

        for a in range(n):
            copy(a).start()
        for a in range(n):
            copy(a).wait_recv()
        for a in range(n):
            copy(a).wait_send()

    return pl.pallas_call(
        body, name=name, in_specs=[ANY] * n, out_specs=[ANY] * n,
        out_shape=[jax.ShapeDtypeStruct((a.shape[0], a.shape[1] // 2, a.shape[2]), a.dtype) for a in parts],
        scratch_shapes=[pltpu.SemaphoreType.DMA((n,)), pltpu.SemaphoreType.DMA((n,))],
    )(*parts)


def _add_own_half(name, part, got, core):
    nchip, R, C = part.shape
    rh = R // 2
    tr = _tile(rh, 512, 2 * SUBLANES)
    nt = rh // tr

    def body(core_ref, a_ref, b_ref, o_ref):
        del core_ref
        o_ref[...] = (a_ref[...] + b_ref[...]).astype(o_ref.dtype)

    return pl.pallas_call(
        body, name=name,
        grid_spec=pltpu.PrefetchScalarGridSpec(
            num_scalar_prefetch=1, grid=(nchip, nt),
            in_specs=[pl.BlockSpec((None, tr, C), lambda j, i, core_ref: (j, core_ref[0] * nt + i, 0)),
                      pl.BlockSpec((None, tr, C), lambda j, i, core_ref: (j, i, 0))],
            out_specs=pl.BlockSpec((None, tr, C), lambda j, i, core_ref: (j, i, 0))),
        out_shape=jax.ShapeDtypeStruct((nchip, rh, C), bf16), compiler_params=_params("parallel", "parallel"),
    )(core, part, got)


def _add_own_half_w_in(part, got, core):
    _, R, C = part.shape
    rh = R // 2
    tr = _tile(rh, 512, 2 * SUBLANES)
    nt = rh // tr
    tn = 256
    per_seg = C // tn
    per_chip = IN_COLS // N_CHIPS // tn

    def src(j):
        return ((j // per_seg + N_SEG - 1) % N_SEG, j % per_seg)

    def body(core_ref, a_ref, b_ref, o_ref):
        del core_ref
        o_ref[...] = (a_ref[...] + b_ref[...]).astype(o_ref.dtype)

    return pl.pallas_call(
        body, name="add_half_w_in",
        grid_spec=pltpu.PrefetchScalarGridSpec(
            num_scalar_prefetch=1, grid=(IN_COLS // tn, nt),
            in_specs=[pl.BlockSpec((None, tr, tn), lambda j, i, core_ref: (src(j)[0], core_ref[0] * nt + i, src(j)[1])),
                      pl.BlockSpec((None, tr, tn), lambda j, i, core_ref: (src(j)[0], i, src(j)[1]))],
            out_specs=pl.BlockSpec((None, tr, tn), lambda j, i, core_ref: (j // per_chip, i, j % per_chip))),
        out_shape=jax.ShapeDtypeStruct((N_CHIPS, rh, IN_COLS // N_CHIPS), bf16), compiler_params=_params("parallel", "parallel"),
    )(core, part, got)


def _chip_exchange(sums):
    n = len(sums)

    def body(*refs):
        start, finish = _chip_exchange_steps(refs[:n], refs[n:2 * n], *refs[2 * n:])
        start()
        finish()

    return pl.pallas_call(
        body, name="chip_exchange", in_specs=[ANY] * n, out_specs=[ANY] * n,
        out_shape=[jax.ShapeDtypeStruct(a.shape, a.dtype) for a in sums],
        scratch_shapes=_chip_exchange_sems(n),
    )(*sums)


def _chip_exchange_sems(n):
    return [pltpu.SemaphoreType.DMA((3 * n,)), pltpu.SemaphoreType.DMA((3 * n,))]


def _chip_exchange_steps(ins, outs, send, recv):
    n = len(ins)

    def copy(a, k, own_slot):
        x, y, c, chips = _place()
        px, py = chips[k]
        slot = 2 * x + y if own_slot else 2 * px + py
        return pltpu.make_async_remote_copy(src_ref=ins[a].at[2 * px + py], dst_ref=outs[a].at[slot], send_sem=send.at[3 * a + k],
                                            recv_sem=recv.at[3 * a + k], device_id=(px, py, c), device_id_type=MESH)

    def start():
        for a in range(n):
            for k in range(3):
                copy(a, k, True).start()

    def finish():
        for a in range(n):
            for k in range(3):
                copy(a, k, False).wait_recv()
        for a in range(n):
            for k in range(3):
                copy(a, k, True).wait_send()

    return start, finish


def _sum_chips(name, slots, sums, where):
    nchip, rh, C = slots.shape
    tr = _tile(rh, 512, 2 * SUBLANES)
    nt = rh // tr

    def body(where_ref, own_ref, s1_ref, s2_ref, s3_ref, o_ref):
        me = where_ref[0]
        by_dist = [r[...].astype(f32) for r in (own_ref, s1_ref, s2_ref, s3_ref)]
        acc = None
        for j in range(nchip):
            d = me ^ j
            term = jnp.where(d == 0, by_dist[0], jnp.where(d == 1, by_dist[1], jnp.where(d == 2, by_dist[2], by_dist[3])))
            acc = term if acc is None else acc + term
        o_ref[...] = acc

    def other(d):
        return pl.BlockSpec((None, tr, C), lambda i, w: (w[0] ^ d, i, 0))

    return pl.pallas_call(
        body, name=name,
        grid_spec=pltpu.PrefetchScalarGridSpec(
            num_scalar_prefetch=1, grid=(nt,),
            in_specs=[other(0), other(1), other(2), other(3)],
            out_specs=pl.BlockSpec((tr, C), lambda i, w: (w[1] * nt + i, 0))),
        out_shape=jax.ShapeDtypeStruct((2 * rh, C), f32), compiler_params=_params("parallel"),
    )(where, sums, slots, slots, slots)


def _sum_slots(name, slots):
    ns, R, C = slots.shape
    tr = _tile(R, 256)

    def body(s_ref, o_ref):
        acc = s_ref[0]
        for j in range(1, ns):
            acc = acc + s_ref[j]
        o_ref[...] = acc

    return pl.pallas_call(
        body, name=name, grid=(R // tr,), in_specs=[pl.BlockSpec((ns, tr, C), lambda i: (0, i, 0))],
        out_specs=pl.BlockSpec((tr, C), lambda i: (i, 0)), out_shape=jax.ShapeDtypeStruct((R, C), f32),
        compiler_params=_params("parallel"),
    )(slots)


def _sibling_join(fulls):
    n = len(fulls)

    def body(*refs):
        ins, outs = refs[:n], refs[n:2 * n]
        send, recv = refs[2 * n:]
        x, y, c, _ = _place()

        def copy(a, core):
            rh = ins[a].shape[0] // 2
            rows = pl.ds(core * rh, rh)
            return pltpu.make_async_remote_copy(src_ref=ins[a].at[rows, :], dst_ref=outs[a].at[rows, :], send_sem=send.at[a],
                                                recv_sem=recv.at[a], device_id=(x, y, 1 - c), device_id_type=MESH)

        for a in range(n):
            copy(a, c).start()
        for a in range(n):
            copy(a, 1 - c).wait_recv()
        for a in range(n):
            copy(a, c).wait_send()

    return pl.pallas_call(
        body, name="sibling_join", in_specs=[ANY] * n, out_specs=[ANY] * n,
        out_shape=[jax.ShapeDtypeStruct(a.shape, a.dtype) for a in fulls],
        scratch_shapes=[pltpu.SemaphoreType.DMA((n,)), pltpu.SemaphoreType.DMA((n,))],
        input_output_aliases={a: a for a in range(n)},
    )(*fulls)


def _allgather_devices(v):
    def body(v_ref, out_ref, send, recv):
        x, y, c, chips = _place()
        me, sibling = (x, y, c), (x, y, 1 - c)

        def slot(px, py, pc):
            return out_ref.at[4 * px + 2 * py + pc]

        def copy(k, block, to, src=None):
            return pltpu.make_async_remote_copy(src_ref=slot(*block) if src is None else src, dst_ref=slot(*block),
                                                send_sem=send.at[k], recv_sem=recv.at[k], device_id=to, device_id_type=MESH)

        first = [copy(0, me, sibling, src=v_ref)] + [copy(1 + j, me, (*chip, c), src=v_ref) for j, chip in enumerate(chips)]
        for cp in first:
            cp.start()
        passed = [copy(4 + j, (*chip, c), sibling) for j, chip in enumerate(chips)]
        for j, chip in enumerate(chips):
            copy(1 + j, (*chip, c), me).wait_recv()
            passed[j].start()
        copy(0, sibling, me).wait_recv()
        for j, chip in enumerate(chips):
            copy(4 + j, (*chip, 1 - c), me).wait_recv()
        for cp in first + passed:
            cp.wait_send()

    return pl.pallas_call(
        body, name="allgather_devices", in_specs=[ANY], out_specs=ANY,
        out_shape=jax.ShapeDtypeStruct((N_DEV,) + v.shape, v.dtype),
        scratch_shapes=[pltpu.SemaphoreType.DMA((N_DEV - 1,)), pltpu.SemaphoreType.DMA((N_DEV - 1,))],
    )(v)


def _adamw(name, w, g, m, v):
    R, C = w.shape
    tr = _tile(R, 256)
    c1 = 1.0 / (1.0 - ADAM_B1 ** ADAM_STEP)
    c2 = 1.0 / (1.0 - ADAM_B2 ** ADAM_STEP)

    def body(w_ref, g_ref, m_ref, v_ref, d_ref, nm_ref, nv_ref):
        gv = g_ref[...]
        nm = ADAM_B1 * m_ref[...] + (1.0 - ADAM_B1) * gv
        nv = ADAM_B2 * v_ref[...] + (1.0 - ADAM_B2) * gv * gv
        d_ref[...] = -ADAM_LR * ((nm * c1) / (jnp.sqrt(nv * c2) + ADAM_EPS) + ADAM_WD * w_ref[...])
        nm_ref[...] = nm
        nv_ref[...] = nv

    row = pl.BlockSpec((tr, C), lambda i: (i, 0))
    sh = jax.ShapeDtypeStruct((R, C), f32)
    return pl.pallas_call(body, name=name, grid=(R // tr,), in_specs=[row] * 4, out_specs=[row] * 3,
                          out_shape=[sh, sh, sh], compiler_params=_params("parallel"))(w, g, m, v)


def _adamw_update(w, g, m, v):
    c1 = 1.0 / (1.0 - ADAM_B1 ** ADAM_STEP)
    c2 = 1.0 / (1.0 - ADAM_B2 ** ADAM_STEP)
    nm = ADAM_B1 * m + (1.0 - ADAM_B1) * g
    nv = ADAM_B2 * v + (1.0 - ADAM_B2) * g * g
    return -ADAM_LR * ((nm * c1) / (jnp.sqrt(nv * c2) + ADAM_EPS) + ADAM_WD * w), nm, nv


def _adamw_many(ws, gs, ms, vs):
    n = len(ws)

    def body(*refs):
        ins, outs = refs[:4 * n], refs[4 * n:]
        for a in range(n):
            d, nm, nv = _adamw_update(ins[a][...], ins[n + a][...], ins[2 * n + a][...], ins[3 * n + a][...])
            outs[a][...] = d
            outs[n + a][...] = nm
            outs[2 * n + a][...] = nv

    shapes = [jax.ShapeDtypeStruct(a.shape, f32) for a in ws]
    return pl.pallas_call(body, name="adamw_small", out_shape=shapes * 3)(*ws, *gs, *ms, *vs)


def _zoh_parts(lr, li, log_dt):
    dt = jnp.exp(log_dt)
    mag = jnp.exp(lr * dt)
    c, s = jnp.cos(li * dt), jnp.sin(li * dt)
    ab_re, ab_im = mag * c, mag * s
    den = lr * lr + li * li
    nr = ab_re - 1.0
    coef_re = (nr * lr + ab_im * li) / den
    coef_im = (ab_im * lr - nr * li) / den
    return dt, mag, c, s, ab_re, ab_im, den, nr, coef_re, coef_im


def _zoh_fwd(lr, li, log_dt, b_re, b_im):
    def body(lr_ref, li_ref, ld_ref, br_ref, bi_ref, ar_ref, ai_ref, bbr_ref, bbi_ref):
        _, _, _, _, ab_re, ab_im, _, _, coef_re, coef_im = _zoh_parts(lr_ref[...], li_ref[...], ld_ref[...])
        ar_ref[...] = ab_re
        ai_ref[...] = ab_im
        bbr_ref[...] = coef_re * br_ref[...] - coef_im * bi_ref[...]
        bbi_ref[...] = coef_re * bi_ref[...] + coef_im * br_ref[...]

    col = jax.ShapeDtypeStruct(lr.shape, f32)
    mat = jax.ShapeDtypeStruct(b_re.shape, f32)
    return pl.pallas_call(body, name="zoh_fwd", out_shape=[col, col, mat, mat])(lr, li, log_dt, b_re, b_im)


def _zoh_bwd(lr, li, log_dt, b_re, b_im, d_ar, d_ai, d_bbr, d_bbi):
    n = lr.shape[1]
    groups = n // SSM_STATE

    def body(lr_ref, li_ref, ld_ref, br_ref, bi_ref, dar_ref, dai_ref, dbbr_ref, dbbi_ref,
             dlr_ref, dli_ref, dld_ref, dbr_ref, dbi_ref):
        lr_, li_ = lr_ref[...], li_ref[...]
        dt, mag, c, s, _, ab_im, den, nr, coef_re, coef_im = _zoh_parts(lr_, li_, ld_ref[...])
        br, bi, dbbr, dbbi = br_ref[...], bi_ref[...], dbbr_ref[...], dbbi_ref[...]
        dbr_ref[...] = coef_re * dbbr + coef_im * dbbi
        dbi_ref[...] = coef_re * dbbi - coef_im * dbbr
        d_cr = jnp.sum(dbbr * br + dbbi * bi, axis=0, keepdims=True)
        d_ci = jnp.sum(dbbi * br - dbbr * bi, axis=0, keepdims=True)
        d_nr = (d_cr * lr_ - d_ci * li_) / den
        d_abi = dai_ref[...] + (d_cr * li_ + d_ci * lr_) / den
        d_abr = dar_ref[...] + d_nr
        d_den = -(d_cr * coef_re + d_ci * coef_im) / den
        d_lr = (d_cr * nr + d_ci * ab_im) / den + 2.0 * lr_ * d_den
        d_li = (d_cr * ab_im - d_ci * nr) / den + 2.0 * li_ * d_den
        d_theta = mag * (d_abi * c - d_abr * s)
        d_arg = mag * (d_abr * c + d_abi * s)
        dlr_ref[...] = d_lr + d_arg * dt
        dli_ref[...] = d_li + d_theta * dt
        d_dt = d_arg * lr_ + d_theta * li_
        member = (lax.broadcasted_iota(jnp.int32, (n, groups), 0) >> (SSM_STATE.bit_length() - 1)
                  == lax.broadcasted_iota(jnp.int32, (n, groups), 1)).astype(f32)
        dld_ref[...] = jnp.dot(d_dt * dt, member, preferred_element_type=f32, precision=lax.Precision.HIGHEST)

    col = jax.ShapeDtypeStruct(lr.shape, f32)
    mat = jax.ShapeDtypeStruct(b_re.shape, f32)
    return pl.pallas_call(body, name="zoh_bwd", out_shape=[col, col, jax.ShapeDtypeStruct((1, groups), f32), mat, mat])(
        lr, li, log_dt, b_re, b_im, d_ar, d_ai, d_bbr, d_bbi)


def _lower_bound_fwd(logits):
    def body(x_ref, o_ref):
        x = x_ref[...]
        e = jnp.exp(x - jnp.max(x, axis=0, keepdims=True))
        o_ref[...] = e / jnp.sum(e, axis=0, keepdims=True)

    return pl.pallas_call(body, name="lower_bound_fwd", out_shape=jax.ShapeDtypeStruct(logits.shape, f32))(logits)


def _lower_bound_bwd(sm, d_lb):
    def body(sm_ref, d_ref, o_ref):
        smv = sm_ref[...]
        row = lax.broadcasted_iota(jnp.int32, smv.shape, 0)
        sm0 = smv[0:1, :]
        o_ref[...] = sm0 * d_ref[...] * (jnp.where(row == 0, 1.0, 0.0) - smv)

    return pl.pallas_call(body, name="lower_bound_bwd", out_shape=jax.ShapeDtypeStruct(sm.shape, f32))(sm, d_lb)


def _s5_tables(ab_re, ab_im, bb_re, bb_im, c_re, c_im, seg):
    eye = jnp.eye(SLAB_GROUPS, dtype=f32)

    def blk_in(bb):
        return jnp.einsum("hsgp,gk->sghkp", bb.reshape(SSM_GROUP, N_SLAB, SLAB_GROUPS, SSM_STATE), eye).reshape(
            N_SLAB, SLAB_CH, SLAB_NS)

    def blk_out(cc):
        return jnp.einsum("sghp,gk->skpgh", cc.reshape(N_SLAB, SLAB_GROUPS, SSM_GROUP, SSM_STATE), eye).reshape(
            N_SLAB, SLAB_NS, SLAB_CH)

    bs = jnp.concatenate([blk_in(bb_re), blk_in(bb_im)], axis=2).astype(bf16)
    cs = jnp.concatenate([blk_out(c_re), blk_out(-c_im)], axis=1).astype(bf16)
    n = SSM_GROUPS * SSM_STATE
    pw = _power_table(jnp.stack([ab_re.reshape(1, n), ab_im.reshape(1, n)]), -(-seg // SUBLANES))
    return bs, cs, pw


def _power_table(ab, tiles):
    n = ab.shape[2]

    def body(a_ref, o_ref):
        row = lax.broadcasted_iota(jnp.int32, (SUBLANES, n), 0)
        ar, ai = a_ref[0], a_ref[1]
        tr, ti = jnp.broadcast_to(ar, (SUBLANES, n)), jnp.broadcast_to(ai, (SUBLANES, n))
        pr, pi = ar, ai
        for r in range(1, SUBLANES):
            pr, pi = pr * ar - pi * ai, pr * ai + pi * ar
            tr = jnp.where(row == r, pr, tr)
            ti = jnp.where(row == r, pi, ti)
        o_ref[0, 0:SUBLANES, :] = tr
        o_ref[1, 0:SUBLANES, :] = ti

        def step(j, carry):
            cr, ci = carry
            cr, ci = cr * pr - ci * pi, cr * pi + ci * pr
            o_ref[0, _rows8(j), :] = cr
            o_ref[1, _rows8(j), :] = ci
            return cr, ci

        lax.fori_loop(1, tiles, step, (tr, ti))

    return pl.pallas_call(body, name="power_table", out_shape=jax.ShapeDtypeStruct((2, SUBLANES * tiles, n), f32))(ab)


def _s5_table_grads(dbs, dcs, da):
    eye = jnp.eye(SLAB_GROUPS, dtype=f32)
    d6 = dbs.reshape(N_SLAB, SLAB_GROUPS, SSM_GROUP, 2, SLAB_GROUPS, SSM_STATE)
    dbb = jnp.einsum("sghrkp,gk->rhsgp", d6, eye).reshape(2, SSM_GROUP, SSM_GROUPS * SSM_STATE)
    c6 = dcs.reshape(N_SLAB, 2, SLAB_GROUPS, SSM_STATE, SLAB_GROUPS, SSM_GROUP)
    dcc = jnp.einsum("srkpgh,gk->rsghp", c6, eye).reshape(2, SSM_GROUPS, SSM_GROUP, SSM_STATE)
    dab = da.transpose(1, 0, 2).reshape(2, SSM_GROUPS, SSM_STATE)
    return dab[0], dab[1], dbb[0], dbb[1], dcc[0], -dcc[1]


SMALL = ["mix_norm_g", "ssm_lambda_re", "ssm_lambda_im", "ssm_log_dt", "ssm_b_re", "ssm_b_im", "ssm_c_re", "ssm_c_im",
         "ssm_d", "hgrn_lb_logits", "hgrn_norm_g", "ffn_norm_g", "conv_b", "final_norm_g"]
SHARDED_SMALL = ["meta_tokens", "conv_w"]
BIG = ["w_in", "ssm_w_glu", "w_ssm_proj", "w_hgrn_proj", "w_out", "w_up", "w_down"]
WEIGHTS = ['meta_tokens', 'mix_norm_g', 'w_in', 'ssm_lambda_re', 'ssm_lambda_im', 'ssm_log_dt', 'ssm_b_re', 'ssm_b_im',
           'ssm_c_re', 'ssm_c_im', 'ssm_d', 'ssm_w_glu', 'w_ssm_proj', 'hgrn_lb_logits', 'hgrn_norm_g', 'w_hgrn_proj',
           'w_out', 'ffn_norm_g', 'w_up', 'conv_w', 'conv_b', 'w_down', 'final_norm_g']


LATER = [k for k in BIG if k != "w_in"]


def _full_weights(gathered, shards, chip):
    Dm = D_MODEL
    g = {k: lax.dynamic_update_slice(gathered[k], shards[k][None], (chip, 0, 0)) for k in gathered}
    full = {}
    for k, v in g.items():
        if k == "w_in":
            full[k] = jnp.roll(v.transpose(1, 0, 2).reshape(Dm, IN_COLS), -Dm, axis=1)
        elif k == "w_up":
            full[k] = v.transpose(1, 0, 2).reshape(Dm, 2 * D_FF)
        else:
            full[k] = v.reshape(-1, Dm)
    return full


def _local_grads(x, tgt, meta, w, full, shards, chip, core):
    B, S, Dm = x.shape
    L = S + N_META
    T = B * L
    h0 = jnp.concatenate([jnp.broadcast_to(meta[None], (B, N_META, Dm)), x], axis=1).reshape(T, Dm)

    lb_all = _lower_bound_fwd(w["hgrn_lb_logits"])
    lb = lb_all[0:1]
    gp = SSM_GROUPS * SSM_STATE
    zoh_in = (w["ssm_lambda_re"].reshape(1, gp), w["ssm_lambda_im"].reshape(1, gp),
              jnp.repeat(w["ssm_log_dt"].reshape(SSM_GROUPS, 1), SSM_STATE, axis=1).reshape(1, gp),
              w["ssm_b_re"].reshape(gp, SSM_GROUP).T, w["ssm_b_im"].reshape(gp, SSM_GROUP).T)
    ab_re, ab_im, bb_re, bb_im = _zoh_fwd(*zoh_in)
    bs, cs, pw = _s5_tables(ab_re, ab_im, bb_re, bb_im, w["ssm_c_re"][0], w["ssm_c_im"][0], L // SUBLANES)

    z1 = _rmsnorm_fwd("mix_norm", h0, w["mix_norm_g"])
    p, gathered = _in_proj_gather(z1, full["w_in"], [shards[k] for k in LATER])
    full = {**full, **_full_weights(dict(zip(LATER, gathered)), shards, chip)}
    ya0 = _s5_fwd(p, bs, cs, pw, w["ssm_d"], B, L)
    gl, ya = _glu_proj_fwd(ya0, full["ssm_w_glu"])
    yb = _hgrn_fwd(p, lb, w["hgrn_norm_g"], B, L)
    pa, pb, merged = _proj_merge_fwd(ya, yb, full["w_ssm_proj"], full["w_hgrn_proj"], p)
    h1, z2 = _out_proj_norm(merged, full["w_out"], h0, w["ffn_norm_g"])
    up = _mm_rows("up_proj", z2, full["w_up"], "nn", f32, D_FF // 2, tm_target=2064)
    ff = _conv_fwd(up, full["conv_w"], w["conv_b"], B, L)
    h2 = _mm_rows("down_proj", ff, full["w_down"], "nn", f32, 1024, res=h1, tk=D_FF // 2)

    tgt_rows = jnp.pad(tgt, ((0, 0), (N_META, 0), (0, 0))).reshape(T, Dm)
    dh2, loss, d_final_g = _final_loss(h2, tgt_rows, w["final_norm_g"].reshape(1, Dm), L)

    dff = _mm_rows("d_ff", dh2, full["w_down"], "nt", f32, D_FF // 2)
    g_w_down = _mm_wgrad("dw_down", ff, dh2, tn=512)
    dup, dconv = _conv_bwd(up, dff, full["conv_w"], w["conv_b"], B, L)
    g_w_up = _dw_up(z2, dup)
    dh1, d_ffn_g = _dz2_norm(dup, full["w_up"], h1, w["ffn_norm_g"], dh2)

    g_w_out = _mm_wgrad("dw_out", merged, dh1)
    dpa, dpb, dp = _merge_bwd_fused(dh1, full["w_out"], p, pa, pb)
    dgl, dya0_direct = _glu_bwd_fused(dpa, full["w_ssm_proj"], ya0, gl)
    g_w_ssm_proj = _mm_wgrad("dw_ssm_proj", ya, dpa)
    dyb = _mm_rows("d_yb", dpb, full["w_hgrn_proj"], "nt", f32, 1024)
    g_w_hgrn_proj = _mm_wgrad("dw_hgrn_proj", yb, dpb)
    dp, d_lb, d_hgrn_g = _hgrn_bwd(p, dyb, dp, lb, w["hgrn_norm_g"], B, L)
    dya0 = _mm_rows("d_ya0", dgl, full["ssm_w_glu"], "nt", f32, 1024, res=dya0_direct)
    g_w_glu = _mm_wgrad("dw_glu", ya0, dgl)
    parts = {
        "ssm_w_glu": g_w_glu.reshape(N_CHIPS, Dm // N_CHIPS, Dm), "w_ssm_proj": g_w_ssm_proj.reshape(N_CHIPS, Dm // N_CHIPS, Dm),
        "w_hgrn_proj": g_w_hgrn_proj.reshape(N_CHIPS, Dm // N_CHIPS, Dm), "w_out": g_w_out.reshape(N_CHIPS, Dm // N_CHIPS, Dm),
        "w_up": g_w_up, "w_down": g_w_down.reshape(N_CHIPS, D_FF // N_CHIPS, Dm),
    }
    got = _sibling_halves([parts[k] for k in LATER])
    sums = {k: _add_own_half("add_half_" + k, parts[k], gt, core) for k, gt in zip(LATER, got)}
    (dp, dbs, dcs, da, d_skip), slots_later = _s5_bwd(p, dya0, dp, bs, cs, pw, w["ssm_d"], B, L, [sums[k] for k in LATER])
    slots = dict(zip(LATER, slots_later))
    g_w_in = _dw_in(z1, dp)
    dh0, d_mix_g = _dz1_norm(dp, full["w_in"], h0, w["mix_norm_g"], dh1)

    dh0 = dh0.reshape(B, L, Dm)
    grad_x = dh0[:, N_META:]
    d_meta = _meta_grad(dh0[:, :N_META])

    d_ab_re, d_ab_im, d_bb_re, d_bb_im, d_c_re, d_c_im = _s5_table_grads(dbs, dcs, da)
    d_lr, d_li, d_log_dt, d_b_re, d_b_im = _zoh_bwd(*zoh_in, d_ab_re.reshape(1, gp), d_ab_im.reshape(1, gp), d_bb_re, d_bb_im)
    gps = (SSM_GROUPS, SSM_STATE)
    d_lr, d_li, d_log_dt = d_lr.reshape(gps), d_li.reshape(gps), d_log_dt.reshape(SSM_GROUPS)
    d_b_re, d_b_im = d_b_re.T.reshape(gps + (SSM_GROUP,)), d_b_im.T.reshape(gps + (SSM_GROUP,))
    d_logits = _lower_bound_bwd(lb_all, d_lb)
    small = {
        "meta_tokens": d_meta, "mix_norm_g": d_mix_g, "ssm_lambda_re": d_lr[None], "ssm_lambda_im": d_li[None],
        "ssm_log_dt": d_log_dt[None], "ssm_b_re": d_b_re[None], "ssm_b_im": d_b_im[None], "ssm_c_re": d_c_re[None],
        "ssm_c_im": d_c_im[None], "ssm_d": d_skip, "hgrn_lb_logits": d_logits, "hgrn_norm_g": d_hgrn_g,
        "ffn_norm_g": d_ffn_g, "conv_w": dconv[:, 0:3, :].transpose(1, 0, 2).reshape(3, 2 * D_FF),
        "conv_b": dconv[:, 3, :].reshape(1, 2 * D_FF), "final_norm_g": d_final_g.reshape(Dm),
    }
    sums["w_in"] = _add_own_half_w_in(g_w_in, _sibling_halves([g_w_in], "sibling_halves_w_in")[0], core)
    slots["w_in"] = _chip_exchange([sums["w_in"]])[0]
    return loss, grad_x, sums, slots, small


PACK_ROWS = 256


def _pack(parts):
    flat = jnp.concatenate([parts[k].reshape(-1) for k in parts])
    n = flat.shape[0]
    rows = -(-n // (PACK_ROWS * LANES)) * PACK_ROWS
    flat = jnp.pad(flat, (0, rows * LANES - n))
    return flat.reshape(rows, LANES)


def _unpack(packed, like):
    flat = packed.reshape(-1)
    out, o = {}, 0
    for k, ref in like.items():
        n = math.prod(ref.shape)
        out[k] = flat[o:o + n].reshape(ref.shape)
        o += n
    return out


def kernel(x, meta_tokens, mix_norm_g, w_in, ssm_lambda_re, ssm_lambda_im, ssm_log_dt, ssm_b_re, ssm_b_im, ssm_c_re, ssm_c_im, ssm_d, ssm_w_glu, w_ssm_proj, hgrn_lb_logits, hgrn_norm_g, w_hgrn_proj, w_out, ffn_norm_g, w_up, conv_w, conv_b, w_down, final_norm_g, loss_target, m_meta_tokens, m_mix_norm_g, m_w_in, m_ssm_lambda_re, m_ssm_lambda_im, m_ssm_log_dt, m_ssm_b_re, m_ssm_b_im, m_ssm_c_re, m_ssm_c_im, m_ssm_d, m_ssm_w_glu, m_w_ssm_proj, m_hgrn_lb_logits, m_hgrn_norm_g, m_w_hgrn_proj, m_w_out, m_ffn_norm_g, m_w_up, m_conv_w, m_conv_b, m_w_down, m_final_norm_g, v_meta_tokens, v_mix_norm_g, v_w_in, v_ssm_lambda_re, v_ssm_lambda_im, v_ssm_log_dt, v_ssm_b_re, v_ssm_b_im, v_ssm_c_re, v_ssm_c_im, v_ssm_d, v_ssm_w_glu, v_w_ssm_proj, v_hgrn_lb_logits, v_hgrn_norm_g, v_w_hgrn_proj, v_w_out, v_ffn_norm_g, v_w_up, v_conv_w, v_conv_b, v_w_down, v_final_norm_g):
    args = dict(locals())
    w = {k: args[k] for k in WEIGHTS}
    mom = {k: args["m_" + k] for k in WEIGHTS}
    var = {k: args["v_" + k] for k in WEIGHTS}
    Dm = D_MODEL
    cx, cy, cc = lax.axis_index("x"), lax.axis_index("y"), lax.axis_index("c")
    chip = 2 * cx + cy

    shards = {k: w[k][0].astype(bf16) for k in BIG}
    g_meta, g_cw = _allgather_chips([w["meta_tokens"], w["conv_w"][0]])
    full = _full_weights({"w_in": _allgather_split([shards["w_in"]])[0]}, shards, chip)
    full["conv_w"] = g_cw.transpose(1, 0, 2).reshape(3, 2 * D_FF)
    meta_full = g_meta.transpose(1, 0, 2).reshape(N_META, Dm)

    core = cc.reshape(1).astype(jnp.int32)
    loss_part, grad_x, sums, slots, small = _local_grads(x, loss_target, meta_full, w, full, shards, chip, core)

    where = jnp.stack([chip, cc]).astype(jnp.int32)
    fulls = [_sum_chips("sum_chips_" + k, slots[k], sums[k], where) for k in BIG]
    g_big = dict(zip(BIG, _sibling_join(fulls)))

    small_all = dict(small)
    small_all["loss"] = loss_part[0, 0:1]
    packed = _pack(small_all)
    slots_dev = lax.dynamic_update_slice(_allgather_devices(packed), packed[None], (2 * chip + cc, 0, 0))
    reduced = _unpack(_sum_slots("sum_devices", slots_dev), small_all)
    loss = reduced.pop("loss")[0]
    mcols = Dm // N_CHIPS
    ccols = 2 * D_FF // N_CHIPS
    grads = {k: reduced[k] for k in SMALL}
    grads["meta_tokens"] = lax.dynamic_slice(reduced["meta_tokens"], (0, chip * mcols), (N_META, mcols))
    grads["conv_w"] = lax.dynamic_slice(reduced["conv_w"], (0, chip * ccols), (3, ccols))[None]
    for k in BIG:
        grads[k] = g_big[k][None]

    delta, new_m, new_v = {}, {}, {}
    for k in BIG:
        shp = w[k].shape
        d, nm, nv = _adamw("adamw_" + k, w[k][0], grads[k][0], mom[k][0], var[k][0])
        delta[k], new_m[k], new_v[k] = d.reshape(shp), nm.reshape(shp), nv.reshape(shp)
    rest = SMALL + SHARDED_SMALL

    def flat2(a):
        return a.reshape(-1, a.shape[-1])

    outs = _adamw_many(*[[flat2(t[k]) for k in rest] for t in (w, grads, mom, var)])
    n = len(rest)
    for j, dst in enumerate((delta, new_m, new_v)):
        dst.update({k: o.reshape(w[k].shape) for k, o in zip(rest, outs[j * n:(j + 1) * n])})

    return (loss, grad_x, *[grads[k].reshape(w[k].shape) for k in WEIGHTS], *[delta[k] for k in WEIGHTS],
            *[new_m[k] for k in WEIGHTS], *[new_v[k] for k in WEIGHTS])
```

```python
import math

import jax
import jax.numpy as jnp
from jax import lax
from jax.experimental import pallas as pl
from jax.experimental.pallas import tpu as pltpu

f32 = jnp.float32
bf16 = jnp.bfloat16

D_MODEL = 1024
N_META = 16
SSM_GROUP = 16
SSM_GROUPS = 64
SSM_STATE = 64
SLAB_GROUPS = 8
N_SLAB = SSM_GROUPS // SLAB_GROUPS
SLAB_CH = SLAB_GROUPS * SSM_GROUP
SLAB_NS = SLAB_GROUPS * SSM_STATE
HEADS = 8
HEAD_DIM = 128
CHUNK = 16
D_FF = 2816
IN_COLS = 7168
EPS = 1e-6
SUBLANES = 8
LANES = 128
N_CHIPS = 4
N_DEV = 8
ADAM_LR, ADAM_B1, ADAM_B2, ADAM_EPS, ADAM_WD, ADAM_STEP = 0.001, 0.9, 0.999, 1e-08, 0.01, 10
MESH = pl.DeviceIdType.MESH
ANY = pl.BlockSpec(memory_space=pl.ANY)

SEG_Q, SEG_F, SEG_I, SEG_OG, SEG_GA, SEG_GB, SEG_U = range(7)
N_SEG = 7


def _tile(n, target, mult=SUBLANES):
    best = None
    for d in range(mult, min(n, target) + 1, mult):
        if n % d == 0:
            best = d
    return n if best is None else best


def _params(*sem):
    return pltpu.CompilerParams(dimension_semantics=sem)


def _sigmoid(x):
    return 1.0 / (1.0 + jnp.exp(-x))


_DIMS = {"nn": (((1,), (0,)), ((), ())), "nt": (((1,), (1,)), ((), ())), "tn": (((0,), (0,)), ((), ()))}


def _mm(name, a, b, dims, grid, a_spec, b_spec, out_shape, out_spec, acc_shape, res=None, res_spec=None):
    nk = grid[2]
    dn = _DIMS[dims]

    def body(*refs):
        if res is None:
            a_ref, b_ref, o_ref, acc = refs
        else:
            a_ref, b_ref, r_ref, o_ref, acc = refs
        k = pl.program_id(2)

        @pl.when(k == 0)
        def _():
            acc[...] = jnp.zeros_like(acc)

        acc[...] += lax.dot_general(a_ref[...].astype(bf16), b_ref[...].astype(bf16), dn, preferred_element_type=f32)

        @pl.when(k == nk - 1)
        def _():
            r = acc[...]
            if res is not None:
                r = r + r_ref[...]
            o_ref[...] = r.astype(o_ref.dtype)

    ins = [a, b] + ([] if res is None else [res])
    specs = [a_spec, b_spec] + ([] if res is None else [res_spec])
    return pl.pallas_call(
        body, name=name, grid=grid, in_specs=specs, out_specs=out_spec, out_shape=out_shape,
        scratch_shapes=[pltpu.VMEM(acc_shape, f32)],
        compiler_params=_params("parallel", "parallel", "arbitrary"),
    )(*ins)


def _mm_rows(name, a, w, dims, out_dtype, tn, res=None, tk=None, tm_target=1032):
    T, K = a.shape
    N = w.shape[1] if dims == "nn" else w.shape[0]
    tm = _tile(T, tm_target)
    tk = K if tk is None else tk
    grid = (T // tm, N // tn, K // tk)
    a_spec = pl.BlockSpec((tm, tk), lambda i, j, k: (i, k))
    if dims == "nn":
        b_spec = pl.BlockSpec((tk, tn), lambda i, j, k: (k, j))
    else:
        b_spec = pl.BlockSpec((tn, tk), lambda i, j, k: (j, k))
    o_spec = pl.BlockSpec((tm, tn), lambda i, j, k: (i, j))
    return _mm(name, a, w, dims, grid, a_spec, b_spec, jax.ShapeDtypeStruct((T, N), out_dtype), o_spec, (tm, tn),
               res=res, res_spec=None if res is None else o_spec)


def _mm_fused(name, pairs, dims, extras, epilogue, outs, rows=(), tm_target=688):
    T, K = pairs[0][0].shape
    N = pairs[0][1].shape[1] if dims == "nn" else pairs[0][1].shape[0]
    tm = _tile(T, tm_target)
    tn = N
    grid = (T // tm, N // tn)
    npair, nex = len(pairs), len(extras) + len(rows)
    dn = _DIMS[dims]

    def body(*refs):
        ab = refs[:2 * npair]
        ex = refs[2 * npair:2 * npair + nex]
        o_refs = refs[2 * npair + nex:]
        accs = [lax.dot_general(ab[2 * q][...].astype(bf16), ab[2 * q + 1][...].astype(bf16), dn, preferred_element_type=f32)
                for q in range(npair)]
        vals = epilogue(accs, [e[...] for e in ex])
        for o_ref, v in zip(o_refs, vals):
            if isinstance(v, (list, tuple)):
                for s_, vs in enumerate(v):
                    o_ref[s_] = vs.astype(o_ref.dtype)
            else:
                o_ref[...] = v.astype(o_ref.dtype)

    ins, specs = [], []
    for a, w in pairs:
        ins += [a, w]
        specs.append(pl.BlockSpec((tm, K), lambda i, j: (i, 0)))
        specs.append(pl.BlockSpec((K, tn), lambda i, j: (0, j)) if dims == "nn" else pl.BlockSpec((tn, K), lambda i, j: (j, 0)))
    for arr, off in extras:
        ins.append(arr)
        specs.append(pl.BlockSpec((tm, tn), lambda i, j, off=off: (i, off + j)))
    for arr in rows:
        ins.append(arr)
        specs.append(pl.BlockSpec((1, tn), lambda i, j: (0, j)))
    shapes, ospecs = [], []
    for o in outs:
        if isinstance(o, tuple):
            dt, nseg, total, blk = o
            shapes.append(jax.ShapeDtypeStruct((total, T, N), dt))
            ospecs.append(pl.BlockSpec((nseg, tm, tn), lambda i, j, blk=blk: (blk, i, j)))
        else:
            shapes.append(jax.ShapeDtypeStruct((T, N), o))
            ospecs.append(pl.BlockSpec((tm, tn), lambda i, j: (i, j)))
    return pl.pallas_call(body, name=name, grid=grid, in_specs=specs, out_specs=ospecs, out_shape=shapes,
                          compiler_params=_params("parallel", "parallel"))(*ins)


def _glu_proj_fwd(ya0, w_glu):
    def epi(accs, tiles):
        return accs[0], tiles[0] * _sigmoid(accs[0])

    return _mm_fused("glu_proj", [(ya0, w_glu)], "nn", [(ya0, 0)], epi, [f32, bf16], tm_target=1032)


def _proj_merge_fwd(ya, yb, w_sp, w_hp, p):
    def epi(accs, tiles):
        return accs[0], accs[1], _sigmoid(tiles[0]) * accs[0] + _sigmoid(tiles[1]) * accs[1]

    return _mm_fused("proj_merge", [(ya, w_sp), (yb, w_hp)], "nn", [(p, SEG_GA), (p, SEG_GB)], epi, [f32, f32, bf16])


def _merge_bwd_fused(dh1, w_out, p, pa, pb):
    def epi(accs, tiles):
        d = accs[0]
        sa, sb = _sigmoid(tiles[0]), _sigmoid(tiles[1])
        return d * sa, d * sb, [d * tiles[2] * sa * (1.0 - sa), d * tiles[3] * sb * (1.0 - sb)]

    return _mm_fused("d_merged", [(dh1, w_out)], "nt", [(p, SEG_GA), (p, SEG_GB), (pa, 0), (pb, 0)], epi,
                     [bf16, bf16, (bf16, 2, N_SEG, SEG_GA // 2)], tm_target=344)


def _out_proj_norm(merged, w_out, h0, g):
    def epi(accs, tiles):
        h1 = tiles[0] + accs[0]
        r = lax.rsqrt(jnp.mean(h1 * h1, axis=-1, keepdims=True) + EPS)
        return h1, h1 * r * tiles[1]

    return _mm_fused("out_proj", [(merged, w_out)], "nn", [(h0, 0)], epi, [f32, bf16], rows=[g], tm_target=1032)


def _mm_rmsnorm_bwd(name, a, b, grid, a_spec, b_spec, x, g, dres):
    T, Dm = x.shape
    tm = T // grid[0]
    nk = grid[2]

    def body(a_ref, b_ref, x_ref, g_ref, dres_ref, dx_ref, dg_ref, acc):
        i, k = pl.program_id(0), pl.program_id(2)

        @pl.when(k == 0)
        def _():
            acc[...] = jnp.zeros_like(acc)

        @pl.when((i == 0) & (k == 0))
        def _():
            dg_ref[...] = jnp.zeros_like(dg_ref)

        acc[...] += lax.dot_general(a_ref[...].astype(bf16), b_ref[...].astype(bf16), _DIMS["nt"], preferred_element_type=f32)

        @pl.when(k == nk - 1)
        def _():
            xv = x_ref[...]
            r = lax.rsqrt(jnp.mean(xv * xv, axis=-1, keepdims=True) + EPS)
            xn = xv * r
            dzv = acc[...]
            dzg = dzv * g_ref[...]
            dx_ref[...] = dres_ref[...] + r * (dzg - xn * jnp.mean(dzg * xn, axis=-1, keepdims=True))
            dg_ref[...] += jnp.sum(dzv * xn, axis=0, keepdims=True)

    row = pl.BlockSpec((tm, Dm), lambda i, j, k: (i, 0))
    par = pl.BlockSpec((1, Dm), lambda i, j, k: (0, 0))
    return pl.pallas_call(
        body, name=name, grid=grid, in_specs=[a_spec, b_spec, row, par, row], out_specs=[row, par],
        out_shape=[jax.ShapeDtypeStruct((T, Dm), f32), jax.ShapeDtypeStruct((1, Dm), f32)],
        scratch_shapes=[pltpu.VMEM((tm, Dm), f32)],
        compiler_params=_params("arbitrary", "arbitrary", "arbitrary"),
    )(a, b, x, g, dres)


def _glu_bwd_fused(dpa, w_sp, ya0, gl):
    def epi(accs, tiles):
        d = accs[0]
        s = _sigmoid(tiles[1])
        return d * tiles[0] * s * (1.0 - s), d * s

    return _mm_fused("d_ya", [(dpa, w_sp)], "nt", [(ya0, 0), (gl, 0)], epi, [bf16, f32], tm_target=1032)


def _mm_wgrad(name, a, g, tn=None):
    T, K = a.shape
    N = g.shape[1]
    tk = _tile(T, 1376 if K <= D_MODEL else 688)
    tn = N if tn is None else tn
    grid = (1, N // tn, T // tk)
    a_spec = pl.BlockSpec((tk, K), lambda i, j, k: (k, 0))
    g_spec = pl.BlockSpec((tk, tn), lambda i, j, k: (k, j))
    o_spec = pl.BlockSpec((K, tn), lambda i, j, k: (0, j))
    return _mm(name, a, g, "tn", grid, a_spec, g_spec, jax.ShapeDtypeStruct((K, N), f32), o_spec, (K, tn))


def _rmsnorm_fwd(name, x, g):
    T, Dm = x.shape
    tr = _tile(T, 688)

    def body(x_ref, g_ref, z_ref):
        xv = x_ref[...]
        r = lax.rsqrt(jnp.mean(xv * xv, axis=-1, keepdims=True) + EPS)
        z_ref[...] = (xv * r * g_ref[...]).astype(z_ref.dtype)

    return pl.pallas_call(
        body, name=name, grid=(T // tr,),
        in_specs=[pl.BlockSpec((tr, Dm), lambda i: (i, 0)), pl.BlockSpec((1, Dm), lambda i: (0, 0))],
        out_specs=pl.BlockSpec((tr, Dm), lambda i: (i, 0)),
        out_shape=jax.ShapeDtypeStruct((T, Dm), bf16), compiler_params=_params("parallel"),
    )(x, g)


def _final_loss(h2, tgt, g, L):
    T, Dm = h2.shape
    tr = _tile(L, 688)
    per_seq = L // tr

    def body(h_ref, t_ref, g_ref, dh_ref, loss_ref, dg_ref):
        pos = (pl.program_id(0) % per_seq) * tr + lax.broadcasted_iota(jnp.int32, (tr, 1), 0)
        live = jnp.where(pos >= N_META, 1.0, 0.0)
        hv = h_ref[...]
        r = lax.rsqrt(jnp.mean(hv * hv, axis=-1, keepdims=True) + EPS)
        xn = hv * r
        gv = g_ref[...]
        err = (xn * gv - t_ref[...]) * live
        dy = err * (1.0 / Dm)
        dyg = dy * gv
        dh_ref[...] = r * (dyg - xn * jnp.mean(dyg * xn, axis=-1, keepdims=True))

        @pl.when(pl.program_id(0) == 0)
        def _():
            dg_ref[...] = jnp.zeros_like(dg_ref)
            loss_ref[...] = jnp.zeros_like(loss_ref)

        dg_ref[...] += jnp.sum(dy * xn, axis=0, keepdims=True)
        loss_ref[...] += jnp.sum(err * err) * (0.5 / Dm)

    row = pl.BlockSpec((tr, Dm), lambda i: (i, 0))
    par = pl.BlockSpec((1, Dm), lambda i: (0, 0))
    return pl.pallas_call(
        body, name="final_loss", grid=(T // tr,), in_specs=[row, row, par],
        out_specs=[row, pl.BlockSpec((1, LANES), lambda i: (0, 0)), par],
        out_shape=[jax.ShapeDtypeStruct((T, Dm), f32), jax.ShapeDtypeStruct((1, LANES), f32), jax.ShapeDtypeStruct((1, Dm), f32)],
        compiler_params=_params("arbitrary"),
    )(h2, tgt, g)


def _meta_grad(dh0_meta):
    B = dh0_meta.shape[0]

    def body(d_ref, o_ref):
        acc = d_ref[0]
        for b in range(1, B):
            acc = acc + d_ref[b]
        o_ref[...] = acc

    return pl.pallas_call(body, name="meta_grad", out_shape=jax.ShapeDtypeStruct(dh0_meta.shape[1:], f32))(dh0_meta)


def _shift_down(x, k, row):
    return jnp.where(row >= k, pltpu.roll(x, k, 0), 0.0)


def _conv_fwd(up, conv_w, conv_b, B, L):
    tc = 256
    nt = D_FF // tc

    def body(xa_ref, xb_ref, wa_ref, wb_ref, ba_ref, bb_ref, o_ref):
        head = 2 * SUBLANES
        row = lax.broadcasted_iota(jnp.int32, (head, tc), 0)

        def gated(conv):
            a = conv(xa_ref, wa_ref, ba_ref)
            b = conv(xb_ref, wb_ref, bb_ref)
            return (a * _sigmoid(a) * b).astype(o_ref.dtype)

        def conv_rolled(x_ref, w_ref, b_ref):
            x = x_ref[...]
            return b_ref[...] + w_ref[0:1, :] * pltpu.roll(x, 2, 0) + w_ref[1:2, :] * pltpu.roll(x, 1, 0) + w_ref[2:3, :] * x

        def conv_head(x_ref, w_ref, b_ref):
            x = x_ref[0:head, :]
            return (b_ref[...] + w_ref[0:1, :] * _shift_down(x, 2, row) + w_ref[1:2, :] * _shift_down(x, 1, row)
                    + w_ref[2:3, :] * x)

        o_ref[...] = gated(conv_rolled)
        o_ref[0:head, :] = gated(conv_head)

    return pl.pallas_call(
        body, name="conv_fwd", grid=(B, nt),
        in_specs=[pl.BlockSpec((L, tc), lambda b, j: (b, j)), pl.BlockSpec((L, tc), lambda b, j: (b, j + nt)),
                  pl.BlockSpec((3, tc), lambda b, j: (0, j)), pl.BlockSpec((3, tc), lambda b, j: (0, j + nt)),
                  pl.BlockSpec((1, tc), lambda b, j: (0, j)), pl.BlockSpec((1, tc), lambda b, j: (0, j + nt))],
        out_specs=pl.BlockSpec((L, tc), lambda b, j: (b, j)),
        out_shape=jax.ShapeDtypeStruct((B * L, D_FF), bf16), compiler_params=_params("parallel", "parallel"),
    )(up, up, conv_w, conv_w, conv_b, conv_b)


CONV_ROWS = 2 * SUBLANES


def _rows16(i):
    return pl.ds(pl.multiple_of(i * CONV_ROWS, CONV_ROWS), CONV_ROWS)


def _conv_taps(x_ref, i, row):
    x = x_ref[_rows16(i), :]
    live = jnp.where(i > 0, 1.0, 0.0)
    r0 = jnp.maximum(i * CONV_ROWS, 2)
    p1 = x_ref[pl.ds(r0 - 1, 1), :] * live
    p2 = x_ref[pl.ds(r0 - 2, 1), :] * live
    x1 = jnp.where(row == 0, p1, pltpu.roll(x, 1, 0))
    x2 = jnp.where(row == 0, p2, jnp.where(row == 1, p1, pltpu.roll(x, 2, 0)))
    return x, x1, x2


def _conv_bwd(up, dff, conv_w, conv_b, B, L):
    tc = 256
    nt = D_FF // tc
    n = L // CONV_ROWS

    def body(xa_ref, xb_ref, d_ref, wa_ref, wb_ref, ba_ref, bb_ref, dup_ref, dw_ref, ga_ref, gb_ref):
        row = lax.broadcasted_iota(jnp.int32, (CONV_ROWS, tc), 0)

        @pl.when(pl.program_id(1) == 0)
        def _():
            dw_ref[...] = jnp.zeros_like(dw_ref)

        zero_tail = jnp.zeros((CONV_ROWS, tc), f32)
        ga_ref[L:L + CONV_ROWS, :] = zero_tail
        gb_ref[L:L + CONV_ROWS, :] = zero_tail

        def fold(v):
            return v[0:SUBLANES, :] + v[SUBLANES:CONV_ROWS, :]

        def step(i, acc):
            taps_a = _conv_taps(xa_ref, i, row)
            taps_b = _conv_taps(xb_ref, i, row)
            a = ba_ref[...] + wa_ref[0:1, :] * taps_a[2] + wa_ref[1:2, :] * taps_a[1] + wa_ref[2:3, :] * taps_a[0]
            b = bb_ref[...] + wb_ref[0:1, :] * taps_b[2] + wb_ref[1:2, :] * taps_b[1] + wb_ref[2:3, :] * taps_b[0]
            s = _sigmoid(a)
            d = d_ref[_rows16(i), :]
            g_a = d * b * s * (1.0 + a * (1.0 - s))
            g_b = d * a * s
            ga_ref[_rows16(i), :] = g_a
            gb_ref[_rows16(i), :] = g_b
            new = []
            for g, (x, x1, x2) in ((g_a, taps_a), (g_b, taps_b)):
                new += [fold(g * x2), fold(g * x1), fold(g * x), fold(g)]
            return tuple(o + v for o, v in zip(acc, new))

        z = jnp.zeros((SUBLANES, tc), f32)
        acc = _repeat_loop(n, step, (z,) * 8)
        for h in range(2):
            for t in range(4):
                dw_ref[h, t:t + 1, :] += jnp.sum(acc[4 * h + t], axis=0, keepdims=True)

        def back(i, c):
            for h, (g_ref, w_ref) in enumerate(((ga_ref, wa_ref), (gb_ref, wb_ref))):
                g = g_ref[_rows16(i), :]
                n1 = g_ref[pl.ds(i * CONV_ROWS + CONV_ROWS, 1), :]
                n2 = g_ref[pl.ds(i * CONV_ROWS + CONV_ROWS + 1, 1), :]
                u1 = jnp.where(row == CONV_ROWS - 1, n1, pltpu.roll(g, CONV_ROWS - 1, 0))
                u2 = jnp.where(row == CONV_ROWS - 1, n2, jnp.where(row == CONV_ROWS - 2, n1, pltpu.roll(g, CONV_ROWS - 2, 0)))
                dup_ref[h, _rows16(i), :] = (w_ref[2:3, :] * g + w_ref[1:2, :] * u1 + w_ref[0:1, :] * u2).astype(dup_ref.dtype)
            return c

        _repeat_loop(n, back, 0)

    return pl.pallas_call(
        body, name="conv_bwd", grid=(nt, B),
        in_specs=[pl.BlockSpec((L, tc), lambda j, b: (b, j)), pl.BlockSpec((L, tc), lambda j, b: (b, j + nt)),
                  pl.BlockSpec((L, tc), lambda j, b: (b, j)),
                  pl.BlockSpec((3, tc), lambda j, b: (0, j)), pl.BlockSpec((3, tc), lambda j, b: (0, j + nt)),
                  pl.BlockSpec((1, tc), lambda j, b: (0, j)), pl.BlockSpec((1, tc), lambda j, b: (0, j + nt))],
        out_specs=[pl.BlockSpec((2, L, tc), lambda j, b: (0, b, j)), pl.BlockSpec((2, SUBLANES, tc), lambda j, b: (0, 0, j))],
        out_shape=[jax.ShapeDtypeStruct((2, B * L, D_FF), bf16), jax.ShapeDtypeStruct((2, SUBLANES, D_FF), f32)],
        scratch_shapes=[pltpu.VMEM((L + CONV_ROWS, tc), f32), pltpu.VMEM((L + CONV_ROWS, tc), f32)],
        compiler_params=_params("parallel", "arbitrary"),
    )(up, up, dff, conv_w, conv_w, conv_b, conv_b)


GELU_C = math.sqrt(2.0 / math.pi)
GELU_A = 0.044715


def _gelu(x):
    return 0.5 * x * (1.0 + jnp.tanh(GELU_C * (x + GELU_A * x * x * x)))


def _gelu_grad(x):
    t = jnp.tanh(GELU_C * (x + GELU_A * x * x * x))
    return 0.5 * (1.0 + t) + 0.5 * x * (1.0 - t * t) * GELU_C * (1.0 + 3.0 * GELU_A * x * x)


def _cmul_add(xr, xi, ar, ai, sr, si):
    return xr + ar * sr - ai * si, xi + ar * si + ai * sr


def _s5_project_in(u_ref, bs_ref, s_ref, L, rc):
    for r in range(0, L, rc):
        s_ref[r:r + rc, :] = jnp.dot(u_ref[r:r + rc, :].astype(bf16), bs_ref[...], preferred_element_type=f32)


def _rows8(i):
    return pl.ds(pl.multiple_of(i * SUBLANES, SUBLANES), SUBLANES)


def _repeat_loop(n, step, init):
    rep = max(u for u in (6, 4, 3, 2, 1) if n % u == 0)

    def body(t, carry):
        for u in range(rep):
            carry = step(t * rep + u, carry)
        return carry

    return lax.fori_loop(0, n // rep, body, init)


def _to_segments(src_ref, dst_ref, seg):
    def step(i, c):
        dst_ref[_rows8(i), :] = src_ref[pl.ds(i, SUBLANES, stride=seg), :]
        return c

    _repeat_loop(seg, step, 0)


def _from_segments(src_ref, dst_ref, seg):
    def step(i, c):
        dst_ref[pl.ds(i, SUBLANES, stride=seg), :] = src_ref[_rows8(i), :]
        return c

    _repeat_loop(seg, step, 0)


def _half_tiles(j, seg, reverse):
    h = seg // 2
    return (_rows8(seg - 1 - j), _rows8(h - 1 - j)) if reverse else (_rows8(j), _rows8(j + h))


def _seg_local_scan(s_ref, ar, ai, seg, reverse):
    ns = SLAB_NS

    def step(j, carry):
        tiles = _half_tiles(j, seg, reverse)
        loaded = [(s_ref[rows, 0:ns], s_ref[rows, ns:2 * ns]) for rows in tiles]
        out = []
        for (xr, xi), (cr, ci) in zip(loaded, (carry[0:2], carry[2:4])):
            out += list(_cmul_add(xr, xi, ar, ai, cr, ci))
        for rows, cr, ci in zip(tiles, out[0::2], out[1::2]):
            s_ref[rows, 0:ns] = cr
            s_ref[rows, ns:2 * ns] = ci
        return tuple(out)

    z = jnp.zeros((SUBLANES, ns), f32)
    return _repeat_loop(seg // 2, step, (z, z, z, z))


def _seg_boundaries(finals, ahr, ahi, reverse):
    fxr, fxi, fyr, fyi = finals
    row = lax.broadcasted_iota(jnp.int32, fxr.shape, 0)
    zero = jnp.zeros_like(fxr[0:1, :])
    xr, xi, yr, yi = (jnp.zeros_like(fxr) for _ in range(4))
    prev = None
    for r in (range(SUBLANES - 1, -1, -1) if reverse else range(SUBLANES)):
        if prev is None:
            nxr, nxi = zero, zero
        else:
            nxr, nxi = _cmul_add(fyr[prev:prev + 1, :], fyi[prev:prev + 1, :], ahr, ahi, nyr, nyi)
        nyr, nyi = _cmul_add(fxr[r:r + 1, :], fxi[r:r + 1, :], ahr, ahi, nxr, nxi)
        xr, xi = jnp.where(row == r, nxr, xr), jnp.where(row == r, nxi, xi)
        yr, yi = jnp.where(row == r, nyr, yr), jnp.where(row == r, nyi, yi)
        prev = r
    return (xr, xi), (yr, yi)


def _s5_states(u_ref, bs_ref, pw_ref, up_ref, s_ref, L, rc):
    seg = L // SUBLANES
    h = seg // 2
    ns = SLAB_NS
    _to_segments(u_ref, up_ref, seg)
    _s5_project_in(up_ref, bs_ref, s_ref, L, rc)
    ar, ai = pw_ref[0, 0:1, :], pw_ref[1, 0:1, :]
    finals = _seg_local_scan(s_ref, ar, ai, seg, False)
    enter = _seg_boundaries(finals, pw_ref[0, h - 1:h, :], pw_ref[1, h - 1:h, :], False)

    def fix(j, c):
        pr, pi = pw_ref[0, pl.ds(j, 1), :], pw_ref[1, pl.ds(j, 1), :]
        tiles = _half_tiles(j, seg, False)
        loaded = [(s_ref[rows, 0:ns], s_ref[rows, ns:2 * ns]) for rows in tiles]
        for rows, (xr, xi), (br, bi) in zip(tiles, loaded, enter):
            xr, xi = _cmul_add(xr, xi, pr, pi, br, bi)
            s_ref[rows, 0:ns] = xr
            s_ref[rows, ns:2 * ns] = xi
        return c

    _repeat_loop(h, fix, 0)


def _pw_spec(seg_rows, order):
    if order == "bs":
        return pl.BlockSpec((2, seg_rows, SLAB_NS), lambda b, s: (0, 0, s))
    return pl.BlockSpec((2, seg_rows, SLAB_NS), lambda s, b: (0, 0, s))


def _s5_fwd(p, bs, cs, pw, d_skip, B, L):
    rc = _tile(L, 344)
    seg = L // SUBLANES

    def body(u_ref, bs_ref, cs_ref, pw_ref, d_ref, y_ref, s_ref, up_ref, yp_ref):
        _s5_states(u_ref, bs_ref, pw_ref, up_ref, s_ref, L, rc)
        for r in range(0, L, rc):
            ypre = (jnp.dot(s_ref[r:r + rc, :].astype(bf16), cs_ref[...], preferred_element_type=f32)
                    + d_ref[...] * up_ref[r:r + rc, :])
            yp_ref[r:r + rc, :] = _gelu(ypre)
        _from_segments(yp_ref, y_ref, seg)

    ucol = SEG_U * (D_MODEL // SLAB_CH)
    return pl.pallas_call(
        body, name="s5_fwd", grid=(B, N_SLAB),
        in_specs=[pl.BlockSpec((L, SLAB_CH), lambda b, s: (b, ucol + s)),
                  pl.BlockSpec((None, SLAB_CH, 2 * SLAB_NS), lambda b, s: (s, 0, 0)),
                  pl.BlockSpec((None, 2 * SLAB_NS, SLAB_CH), lambda b, s: (s, 0, 0)),
                  _pw_spec(pw.shape[1], "bs"),
                  pl.BlockSpec((1, SLAB_CH), lambda b, s: (0, s))],
        out_specs=pl.BlockSpec((L, SLAB_CH), lambda b, s: (b, s)),
        out_shape=jax.ShapeDtypeStruct((B * L, D_MODEL), f32),
        scratch_shapes=[pltpu.VMEM((L, 2 * SLAB_NS), f32), pltpu.VMEM((L, SLAB_CH), f32), pltpu.VMEM((L, SLAB_CH), f32)],
        compiler_params=_params("parallel", "parallel"),
    )(p, bs, cs, pw, d_skip)


def _s5_bwd(p, dya0, dp, bs, cs, pw, d_skip, B, L, sums):
    rc = _tile(L, 688)
    ns = SLAB_NS
    seg = L // SUBLANES
    nx = len(sums)

    def body(u_ref, dy_ref, dp_in, bs_ref, cs_ref, pw_ref, d_ref, *rest):
        xin, (du_ref, dbs_ref, dcs_ref, da_ref, dd_ref), xout = rest[:nx], rest[nx:nx + 5], rest[nx + 5:2 * nx + 5]
        s_ref, lam_ref, up_ref, dyp_ref, nat_ref, send, recv = rest[2 * nx + 5:]
        del dp_in
        start, finish = _chip_exchange_steps(xin, xout, send, recv)

        @pl.when((pl.program_id(0) == 0) & (pl.program_id(1) == 0))
        def _():
            start()

        @pl.when(pl.program_id(1) == 0)
        def _():
            dbs_ref[...] = jnp.zeros_like(dbs_ref)
            dcs_ref[...] = jnp.zeros_like(dcs_ref)
            da_ref[...] = jnp.zeros_like(da_ref)
            dd_ref[...] = jnp.zeros_like(dd_ref)

        _s5_states(u_ref, bs_ref, pw_ref, up_ref, s_ref, L, rc)
        _to_segments(dy_ref, dyp_ref, seg)
        for r in range(0, L, rc):
            u = up_ref[r:r + rc, :]
            sb = s_ref[r:r + rc, :].astype(bf16)
            ypre = jnp.dot(sb, cs_ref[...], preferred_element_type=f32) + d_ref[...] * u
            dyp = dyp_ref[r:r + rc, :] * _gelu_grad(ypre)
            dyp_ref[r:r + rc, :] = dyp
            dd_ref[...] += jnp.sum(dyp * u, axis=0, keepdims=True)
            dypb = dyp.astype(bf16)
            dcs_ref[...] += lax.dot_general(sb, dypb, _DIMS["tn"], preferred_element_type=f32)
            lam_ref[r:r + rc, :] = lax.dot_general(dypb, cs_ref[...], _DIMS["nt"], preferred_element_type=f32)

        h = seg // 2
        ar, ai = pw_ref[0, 0:1, :], -pw_ref[1, 0:1, :]
        finals = _seg_local_scan(lam_ref, ar, ai, seg, True)
        enter = _seg_boundaries(finals, pw_ref[0, h - 1:h, :], -pw_ref[1, h - 1:h, :], True)

        def fix(j, acc):
            accr, acci = acc
            pr, pi = pw_ref[0, pl.ds(j, 1), :], -pw_ref[1, pl.ds(j, 1), :]
            tiles = _half_tiles(j, seg, True)
            loaded = [(lam_ref[rows, 0:ns], lam_ref[rows, ns:2 * ns]) for rows in tiles]
            for rows, (xr, xi), (br, bi), t in zip(tiles, loaded, enter, (seg - 1 - j, h - 1 - j)):
                xr, xi = _cmul_add(xr, xi, pr, pi, br, bi)
                lam_ref[rows, 0:ns] = xr
                lam_ref[rows, ns:2 * ns] = xi
                prev = _rows8(jnp.maximum(t - 1, 0))
                live = jnp.where(t > 0, 1.0, 0.0)
                spr = s_ref[prev, 0:ns] * live
                spi = s_ref[prev, ns:2 * ns] * live
                accr, acci = accr + xr * spr + xi * spi, acci + xi * spr - xr * spi
            return accr, acci

        z = jnp.zeros((SUBLANES, ns), f32)
        accr, acci = _repeat_loop(h, fix, (z, z))
        row = lax.broadcasted_iota(jnp.int32, (SUBLANES, ns), 0)
        last = _rows8(seg - 1)
        spr = jnp.where(row == 0, 0.0, pltpu.roll(s_ref[last, 0:ns], 1, 0))
        spi = jnp.where(row == 0, 0.0, pltpu.roll(s_ref[last, ns:2 * ns], 1, 0))
        xr, xi = lam_ref[0:SUBLANES, 0:ns], lam_ref[0:SUBLANES, ns:2 * ns]
        accr = accr + xr * spr + xi * spi
        acci = acci + xi * spr - xr * spi
        da_ref[0:1, :] += jnp.sum(accr, axis=0, keepdims=True)
        da_ref[1:2, :] += jnp.sum(acci, axis=0, keepdims=True)

        for r in range(0, L, rc):
            lamb = lam_ref[r:r + rc, :].astype(bf16)
            dbs_ref[...] += lax.dot_general(up_ref[r:r + rc, :].astype(bf16), lamb, _DIMS["tn"], preferred_element_type=f32)
            nat_ref[r:r + rc, :] = (lax.dot_general(lamb, bs_ref[...], _DIMS["nt"], preferred_element_type=f32)
                                    + d_ref[...] * dyp_ref[r:r + rc, :])
        _from_segments(nat_ref, up_ref, seg)
        du_ref[...] = up_ref[...].astype(du_ref.dtype)

        @pl.when((pl.program_id(0) == N_SLAB - 1) & (pl.program_id(1) == B - 1))
        def _():
            finish()

    ucol = SEG_U * (D_MODEL // SLAB_CH)
    T = B * L
    col = pltpu.VMEM((L, SLAB_CH), f32)
    res = pl.pallas_call(
        body, name="s5_bwd", grid=(N_SLAB, B),
        in_specs=[pl.BlockSpec((L, SLAB_CH), lambda s, b: (b, ucol + s)),
                  pl.BlockSpec((L, SLAB_CH), lambda s, b: (b, s)),
                  ANY,
                  pl.BlockSpec((None, SLAB_CH, 2 * SLAB_NS), lambda s, b: (s, 0, 0)),
                  pl.BlockSpec((None, 2 * SLAB_NS, SLAB_CH), lambda s, b: (s, 0, 0)),
                  _pw_spec(pw.shape[1], "sb"),
                  pl.BlockSpec((1, SLAB_CH), lambda s, b: (0, s))] + [ANY] * nx,
        out_specs=[pl.BlockSpec((None, L, SLAB_CH), lambda s, b: (SEG_U, b, s)),
                   pl.BlockSpec((None, SLAB_CH, 2 * SLAB_NS), lambda s, b: (s, 0, 0)),
                   pl.BlockSpec((None, 2 * SLAB_NS, SLAB_CH), lambda s, b: (s, 0, 0)),
                   pl.BlockSpec((None, 2, SLAB_NS), lambda s, b: (s, 0, 0)),
                   pl.BlockSpec((1, SLAB_CH), lambda s, b: (0, s))] + [ANY] * nx,
        out_shape=[jax.ShapeDtypeStruct((N_SEG, T, D_MODEL), bf16),
                   jax.ShapeDtypeStruct((N_SLAB, SLAB_CH, 2 * SLAB_NS), f32),
                   jax.ShapeDtypeStruct((N_SLAB, 2 * SLAB_NS, SLAB_CH), f32),
                   jax.ShapeDtypeStruct((N_SLAB, 2, SLAB_NS), f32),
                   jax.ShapeDtypeStruct((1, D_MODEL), f32)] + [jax.ShapeDtypeStruct(a.shape, a.dtype) for a in sums],
        scratch_shapes=[pltpu.VMEM((L, 2 * SLAB_NS), f32), pltpu.VMEM((L, 2 * SLAB_NS), f32), col, col, col]
        + _chip_exchange_sems(nx),
        input_output_aliases={2: 0},
        compiler_params=_params("arbitrary", "arbitrary"),
    )(p, dya0, dp, bs, cs, pw, d_skip, *sums)
    return res[:5], res[5:]


def _dotb(a, b, dims="nn"):
    return lax.dot_general(a.astype(bf16), b.astype(bf16), _DIMS[dims], preferred_element_type=f32)


def _tile_scan(x, reverse):
    n, w = x.shape
    v = x.reshape(n // SUBLANES, SUBLANES, w)
    row = lax.broadcasted_iota(jnp.int32, v.shape, 1)
    for k in (1, 2, 4):
        if reverse:
            v = v + jnp.where(row < SUBLANES - k, pltpu.roll(v, SUBLANES - k, 1), 0.0)
        else:
            v = v + jnp.where(row >= k, pltpu.roll(v, k, 1), 0.0)
    p = v.reshape(n // CHUNK, 2, SUBLANES, w)
    lo, hi = p[:, 0], p[:, 1]
    if reverse:
        lo = lo + hi[:, 0:1, :]
    else:
        hi = hi + lo[:, SUBLANES - 1:SUBLANES, :]
    return jnp.stack([lo, hi], axis=1).reshape(n, w)


def _chunk_cumsum(x):
    return _tile_scan(x, False)


def _chunk_rev_cumsum(x):
    return _tile_scan(x, True)


def _chunk_last(x):
    n, w = x.shape
    p = x.reshape(n // CHUNK, CHUNK, w)
    return jnp.broadcast_to(p[:, CHUNK - 1:CHUNK, :], p.shape).reshape(n, w)


def _hgrn_local(q, fl, lb):
    sg = _sigmoid(fl)
    f = lb + (1.0 - lb) * sg
    g = jnp.log(f)
    cum = _chunk_cumsum(g)
    rest = _chunk_last(cum) - cum
    e = jnp.exp(cum)
    em = jnp.exp(-cum)
    eo = jnp.exp(rest)
    k = 1.0 - f
    return sg, f, e, em, eo, q * e, k * em, k * eo, cum + rest


def _chunk_pos(n):
    return lax.broadcasted_iota(jnp.int32, (n, HEAD_DIM), 0) & (CHUNK - 1)


def _hgrn_block_rows(L):
    return _tile(L, 688, CHUNK)


def _hgrn_specs(L, order):
    hb = D_MODEL // HEAD_DIM

    def spec(seg):
        if order == "bh":
            return pl.BlockSpec((L, HEAD_DIM), lambda b, h: (b, seg * hb + h))
        return pl.BlockSpec((L, HEAD_DIM), lambda h, b: (b, seg * hb + h))

    return [spec(SEG_Q), spec(SEG_F), spec(SEG_I), spec(SEG_OG)]


PAIR = 2 * CHUNK
CHUNK_SHIFT = CHUNK.bit_length() - 1


def _pair_steps(L, rb):
    steps = []
    nch = rb // CHUNK
    for r in range(0, L, rb):
        steps += [(r + p * PAIR, PAIR) for p in range(nch // 2)]
        if nch % 2:
            steps.append((r + (nch - 1) * CHUNK, CHUNK))
    return steps


def _pair_flags(rb):
    ci = lax.broadcasted_iota(jnp.int32, (rb, HEAD_DIM), 0) >> CHUNK_SHIFT
    odd = (ci & 1) == 1
    has_next = jnp.logical_and(jnp.logical_not(odd), ci < rb // CHUNK - 1)
    return odd, has_next


def _pair_masks(rb):
    r = lax.broadcasted_iota(jnp.int32, (rb, rb), 0)
    c = lax.broadcasted_iota(jnp.int32, (rb, rb), 1)
    rc, cc = r >> CHUNK_SHIFT, c >> CHUNK_SHIFT
    same = (rc == cc) & (c <= r)
    prev = ((rc & 1) == 1) & (cc == rc - 1)
    return same, prev


def _hgrn_pair_local(q, fl, lb, odd, has_next):
    sg, f, e, em, eo, qt, kt, ko, cend = _hgrn_local(q, fl, lb)
    n = q.shape[0]
    a = jnp.where(odd, pltpu.roll(cend, CHUNK, 0), 0.0)
    z = jnp.where(has_next, pltpu.roll(cend, n - CHUNK, 0), 0.0)
    ea, ez = jnp.exp(a), jnp.exp(z)
    return dict(sg=sg, f=f, e=e, em=em, eo=eo, qt=qt, kt=kt, ko=ko, ea=ea, ez=ez, qs=qt * ea, ks=ko * ez,
                decp=jnp.exp(cend + a + z))


def _pair_scores(qt, kt, ko, same, prev):
    return (jnp.where(same, _dotb(qt, kt, "nt"), 0.0) + jnp.where(prev, _dotb(qt, ko, "nt"), 0.0)).astype(bf16)


def _hgrn_fwd(p, lb, norm_g, B, L):
    rb = _hgrn_block_rows(L)
    steps = _pair_steps(L, rb)
    blocks = [slice(r, r + rb) for r in range(0, L, rb)]

    def body(q_ref, f_ref, v_ref, og_ref, lb_ref, ng_ref, y_ref, qs_s, ks_s, vb_s, decp_s, o_s, o2_s, u_s, sb_s):
        lbv = lb_ref[...]
        ngv = ng_ref[...]
        same, prev = _pair_masks(rb)
        odd, has_next = _pair_flags(rb)

        for rows in blocks:
            t = _hgrn_pair_local(q_ref[rows, :], f_ref[rows, :], lbv, odd, has_next)
            vb = v_ref[rows, :].astype(bf16)
            o_s[rows, :] = _dotb(_pair_scores(t["qt"], t["kt"], t["ko"], same, prev), vb)
            qs_s[rows, :] = t["qs"].astype(bf16)
            ks_s[rows, :] = t["ks"].astype(bf16)
            vb_s[rows, :] = vb
            decp_s[rows, :] = t["decp"]

        for n, (r0, nr) in enumerate(steps):
            u_s[n] = _dotb(vb_s[r0:r0 + nr, :], ks_s[r0:r0 + nr, :], "tn")
        st = jnp.zeros((HEAD_DIM, HEAD_DIM), f32)
        for n, (r0, nr) in enumerate(steps):
            sb_s[n] = st.astype(bf16)
            st = st * decp_s[r0:r0 + 1, :] + u_s[n]
        for n, (r0, nr) in enumerate(steps):
            o2_s[r0:r0 + nr, :] = _dotb(qs_s[r0:r0 + nr, :], sb_s[n], "nt")

        for rows in blocks:
            o = o_s[rows, :] + o2_s[rows, :]
            og = og_ref[rows, :]
            on = o * lax.rsqrt(jnp.mean(o * o, axis=-1, keepdims=True) + EPS) * ngv
            y_ref[rows, :] = (on * og * _sigmoid(og)).astype(y_ref.dtype)

    sb = pltpu.VMEM((L, HEAD_DIM), bf16)
    sf = pltpu.VMEM((L, HEAD_DIM), f32)
    return pl.pallas_call(
        body, name="hgrn_fwd", grid=(B, HEADS),
        in_specs=_hgrn_specs(L, "bh") + [pl.BlockSpec((1, HEAD_DIM), lambda b, h: (0, h)),
                                          pl.BlockSpec((1, HEAD_DIM), lambda b, h: (0, 0))],
        out_specs=pl.BlockSpec((L, HEAD_DIM), lambda b, h: (b, h)),
        out_shape=jax.ShapeDtypeStruct((B * L, D_MODEL), bf16),
        scratch_shapes=[sb, sb, sb, sf, sf, sf, pltpu.VMEM((len(steps), HEAD_DIM, HEAD_DIM), f32),
                        pltpu.VMEM((len(steps), HEAD_DIM, HEAD_DIM), bf16)],
        compiler_params=_params("parallel", "parallel"),
    )(p, p, p, p, lb, norm_g)


def _hgrn_bwd(p, dyb, dp, lb, norm_g, B, L):
    rb = _hgrn_block_rows(L)
    steps = _pair_steps(L, rb)
    blocks = [slice(r, r + rb) for r in range(0, L, rb)]

    def body(q_ref, f_ref, v_ref, og_ref, dy_ref, dp_in, lb_ref, ng_ref, dseg_ref, dlb_ref, dng_ref,
             st_ref, u_s, dsb_s, qt_s, kt_s, ko_s, qs_s, ks_s, vb_s, do_s,
             decp_s, o_s, o2_s, dqt_s, dkt_s, dko_s, dv_s, dv2_s, dqs_s, dks_s, ddecp_s):
        del dp_in
        lbv = lb_ref[...]
        ngv = ng_ref[...]
        same, prev = _pair_masks(rb)
        odd, has_next = _pair_flags(rb)
        pos = _chunk_pos(rb)

        @pl.when(pl.program_id(1) == 0)
        def _():
            dlb_ref[...] = jnp.zeros_like(dlb_ref)

        @pl.when((pl.program_id(0) == 0) & (pl.program_id(1) == 0))
        def _():
            dng_ref[...] = jnp.zeros_like(dng_ref)

        def scores(rows):
            return _pair_scores(qt_s[rows, :], kt_s[rows, :], ko_s[rows, :], same, prev)

        for rows in blocks:
            t = _hgrn_pair_local(q_ref[rows, :], f_ref[rows, :], lbv, odd, has_next)
            for dst, key in ((qt_s, "qt"), (kt_s, "kt"), (ko_s, "ko"), (qs_s, "qs"), (ks_s, "ks")):
                dst[rows, :] = t[key].astype(bf16)
            vb_s[rows, :] = v_ref[rows, :].astype(bf16)
            decp_s[rows, :] = t["decp"]
            o_s[rows, :] = _dotb(scores(rows), vb_s[rows, :])

        for n, (r0, nr) in enumerate(steps):
            u_s[n] = _dotb(vb_s[r0:r0 + nr, :], ks_s[r0:r0 + nr, :], "tn")
        st = jnp.zeros((HEAD_DIM, HEAD_DIM), f32)
        for n, (r0, nr) in enumerate(steps):
            st_ref[n] = st
            st = st * decp_s[r0:r0 + 1, :] + u_s[n]
        for n, (r0, nr) in enumerate(steps):
            o2_s[r0:r0 + nr, :] = _dotb(qs_s[r0:r0 + nr, :], st_ref[n], "nt")

        dng = jnp.zeros((1, HEAD_DIM), f32)
        for rows in blocks:
            o = o_s[rows, :] + o2_s[rows, :]
            og = og_ref[rows, :]
            dy = dy_ref[rows, :]
            rs = lax.rsqrt(jnp.mean(o * o, axis=-1, keepdims=True) + EPS)
            xn = o * rs
            so = _sigmoid(og)
            dseg_ref[SEG_OG, rows, :] = (dy * xn * ngv * so * (1.0 + og * (1.0 - so))).astype(dseg_ref.dtype)
            don = dy * og * so
            dng = dng + jnp.sum(don * xn, axis=0, keepdims=True)
            dxo = don * ngv
            do = (rs * (dxo - xn * jnp.mean(dxo * xn, axis=-1, keepdims=True))).astype(bf16)
            do_s[rows, :] = do
            dpf = _dotb(do, vb_s[rows, :], "nt")
            dp1 = jnp.where(same, dpf, 0.0).astype(bf16)
            dp2 = jnp.where(prev, dpf, 0.0).astype(bf16)
            dqt_s[rows, :] = _dotb(dp1, kt_s[rows, :]) + _dotb(dp2, ko_s[rows, :])
            dkt_s[rows, :] = _dotb(dp1, qt_s[rows, :], "tn")
            dko_s[rows, :] = _dotb(dp2, qt_s[rows, :], "tn")
            dv_s[rows, :] = _dotb(scores(rows), do, "tn")
        dng_ref[...] += dng

        for n, (r0, nr) in enumerate(steps):
            u_s[n] = _dotb(do_s[r0:r0 + nr, :], qs_s[r0:r0 + nr, :], "tn")
        dst = jnp.zeros((HEAD_DIM, HEAD_DIM), f32)
        for n, (r0, nr) in reversed(list(enumerate(steps))):
            dsb_s[n] = dst.astype(bf16)
            ddecp_s[r0:r0 + nr, :] = jnp.broadcast_to(jnp.sum(dst * st_ref[n], axis=0, keepdims=True), (nr, HEAD_DIM))
            dst = dst * decp_s[r0:r0 + 1, :] + u_s[n]
        for n, (r0, nr) in enumerate(steps):
            rows = slice(r0, r0 + nr)
            dqs_s[rows, :] = _dotb(do_s[rows, :], st_ref[n])
            dv2_s[rows, :] = _dotb(ks_s[rows, :], dsb_s[n], "nt")
            dks_s[rows, :] = _dotb(vb_s[rows, :], dsb_s[n])

        def chunk_sum(x):
            return _chunk_last(_chunk_cumsum(x))

        dlb = jnp.zeros((1, HEAD_DIM), f32)
        for rows in blocks:
            t = _hgrn_pair_local(q_ref[rows, :], f_ref[rows, :], lbv, odd, has_next)
            dqs, dks = dqs_s[rows, :], dks_s[rows, :]
            dqt = dqt_s[rows, :] + dqs * t["ea"]
            dko = dko_s[rows, :] + dks * t["ez"]
            dkt = dkt_s[rows, :]
            dko_ko = dko * t["ko"]
            dcum = dqt * t["qt"] - dkt * t["kt"] - dko_ko
            from_next = pltpu.roll(chunk_sum(jnp.where(odd, dqs * t["qs"], 0.0)), rb - CHUNK, 0)
            from_prev = pltpu.roll(chunk_sum(jnp.where(has_next, dks * t["ks"], 0.0)), CHUNK, 0)
            d_end = (chunk_sum(dko_ko) + jnp.where(has_next, from_next, 0.0) + jnp.where(odd, from_prev, 0.0)
                     + ddecp_s[rows, :] * t["decp"])
            dcum = dcum + jnp.where(pos == CHUNK - 1, d_end, 0.0)
            df = _chunk_rev_cumsum(dcum) / t["f"] - (dkt * t["em"] + dko * t["eo"])
            dlb = dlb + jnp.sum(df * (1.0 - t["sg"]), axis=0, keepdims=True)
            dseg_ref[SEG_Q, rows, :] = (dqt * t["e"]).astype(dseg_ref.dtype)
            dseg_ref[SEG_F, rows, :] = (df * (1.0 - lbv) * t["sg"] * (1.0 - t["sg"])).astype(dseg_ref.dtype)
            dseg_ref[SEG_I, rows, :] = (dv_s[rows, :] + dv2_s[rows, :]).astype(dseg_ref.dtype)
        dlb_ref[...] += dlb

    T = B * L
    ns = len(steps)
    sb = pltpu.VMEM((L, HEAD_DIM), bf16)
    sf = pltpu.VMEM((L, HEAD_DIM), f32)
    return pl.pallas_call(
        body, name="hgrn_bwd", grid=(HEADS, B),
        in_specs=_hgrn_specs(L, "hb") + [pl.BlockSpec((L, HEAD_DIM), lambda h, b: (b, h)), ANY,
                                          pl.BlockSpec((1, HEAD_DIM), lambda h, b: (0, h)),
                                          pl.BlockSpec((1, HEAD_DIM), lambda h, b: (0, 0))],
        out_specs=[pl.BlockSpec((4, L, HEAD_DIM), lambda h, b: (0, b, h)),
                   pl.BlockSpec((1, HEAD_DIM), lambda h, b: (0, h)),
                   pl.BlockSpec((1, HEAD_DIM), lambda h, b: (0, 0))],
        out_shape=[jax.ShapeDtypeStruct((N_SEG, T, D_MODEL), bf16), jax.ShapeDtypeStruct((1, D_MODEL), f32),
                   jax.ShapeDtypeStruct((1, HEAD_DIM), f32)],
        scratch_shapes=[pltpu.VMEM((ns, HEAD_DIM, HEAD_DIM), f32), pltpu.VMEM((ns, HEAD_DIM, HEAD_DIM), f32),
                        pltpu.VMEM((ns, HEAD_DIM, HEAD_DIM), bf16)] + [sb] * 7 + [sf] * 11,
        input_output_aliases={5: 0},
        compiler_params=_params("arbitrary", "arbitrary"),
    )(p, p, p, p, dyb, dp, lb, norm_g)


def _dz1_norm(dp, w_in_phys, h0, g, dh1):
    _, T, Dm = dp.shape
    tm = _tile(T, 1032)
    return _mm_rmsnorm_bwd("dz1", dp, w_in_phys, (T // tm, 1, N_SEG),
                           pl.BlockSpec((None, tm, Dm), lambda i, j, k: (k, i, 0)),
                           pl.BlockSpec((Dm, Dm), lambda i, j, k: (0, k)), h0, g, dh1)


def _dz2_norm(dup, w_up, h1, g, dh2):
    _, T, _ = dup.shape
    tm = _tile(T, 1032)
    tk = D_FF // 2
    return _mm_rmsnorm_bwd("dz2", dup, w_up, (T // tm, 1, 4),
                           pl.BlockSpec((None, tm, tk), lambda i, j, k: (k // 2, i, k % 2)),
                           pl.BlockSpec((D_MODEL, tk), lambda i, j, k: (0, k)), h1, g, dh2)


def _dw_in(z1, dp):
    _, T, Dm = dp.shape
    tk = _tile(T, 2064)
    return _mm("dw_in", z1, dp, "tn", (1, N_SEG, T // tk),
               pl.BlockSpec((tk, Dm), lambda i, j, k: (k, 0)),
               pl.BlockSpec((None, tk, Dm), lambda i, j, k: (j, k, 0)),
               jax.ShapeDtypeStruct((N_SEG, Dm, Dm), f32),
               pl.BlockSpec((None, Dm, Dm), lambda i, j, k: (j, 0, 0)), (Dm, Dm))


def _dw_up(z2, dup):
    _, T, _ = dup.shape
    tn = D_FF // 2
    tk = _tile(T, 1376)
    return _mm("dw_up", z2, dup, "tn", (1, N_CHIPS, T // tk),
               pl.BlockSpec((tk, D_MODEL), lambda i, j, k: (k, 0)),
               pl.BlockSpec((None, tk, tn), lambda i, j, k: (j // 2, k, j % 2)),
               jax.ShapeDtypeStruct((N_CHIPS, D_MODEL, tn), f32),
               pl.BlockSpec((None, D_MODEL, tn), lambda i, j, k: (j, 0, 0)), (D_MODEL, tn))


def _place():
    x, y, c = lax.axis_index("x"), lax.axis_index("y"), lax.axis_index("c")
    chips = [(1 - x, y), (x, 1 - y), (1 - x, 1 - y)]
    return x, y, c, chips


def _allgather_chips(arrs):
    n = len(arrs)

    def body(*refs):
        ins, outs = refs[:n], refs[n:2 * n]
        send, recv, local = refs[2 * n:]
        x, y, c, chips = _place()
        me = 2 * x + y

        def copy(a, k, slot):
            px, py = chips[k]
            return pltpu.make_async_remote_copy(src_ref=ins[a], dst_ref=outs[a].at[slot], send_sem=send.at[3 * a + k],
                                                recv_sem=recv.at[3 * a + k], device_id=(px, py, c), device_id_type=MESH)

        for a in range(n):
            pltpu.make_async_copy(ins[a], outs[a].at[me], local.at[a]).start()
            for k in range(3):
                copy(a, k, me).start()
        for a in range(n):
            for k, (px, py) in enumerate(chips):
                copy(a, k, 2 * px + py).wait_recv()
        for a in range(n):
            pltpu.make_async_copy(ins[a], outs[a].at[me], local.at[a]).wait()
            for k in range(3):
                copy(a, k, me).wait_send()

    return pl.pallas_call(
        body, name="allgather_chips", in_specs=[ANY] * n, out_specs=[ANY] * n,
        out_shape=[jax.ShapeDtypeStruct((N_CHIPS,) + a.shape, a.dtype) for a in arrs],
        scratch_shapes=[pltpu.SemaphoreType.DMA((3 * n,)), pltpu.SemaphoreType.DMA((3 * n,)), pltpu.SemaphoreType.DMA((n,))],
    )(*arrs)


def _allgather_split(arrs):
    n = len(arrs)

    def body(*refs):
        start, finish = _gather_split_steps(refs[:n], refs[n:2 * n], *refs[2 * n:])
        start()
        finish()

    return pl.pallas_call(
        body, name="allgather_split", in_specs=[ANY] * n, out_specs=[ANY] * n,
        out_shape=[jax.ShapeDtypeStruct((N_CHIPS,) + a.shape, a.dtype) for a in arrs],
        scratch_shapes=_gather_split_sems(n),
    )(*arrs)


def _gather_split_sems(n):
    return [pltpu.SemaphoreType.DMA((3 * n,)) for _ in range(4)]


def _gather_split_steps(ins, outs, send, recv, fsend, frecv):
    n = len(ins)

    def place():
        x, y, c, chips = _place()
        return x, y, c, chips, 2 * x + y

    def half(a, core):
        rh = ins[a].shape[0] // 2
        return pl.ds(core * rh, rh)

    def copy(a, k, slot):
        x, y, c, chips, _ = place()
        px, py = chips[k]
        return pltpu.make_async_remote_copy(src_ref=ins[a].at[half(a, c), :], dst_ref=outs[a].at[slot, half(a, c), :],
                                            send_sem=send.at[3 * a + k], recv_sem=recv.at[3 * a + k],
                                            device_id=(px, py, c), device_id_type=MESH)

    def forward(a, k, core):
        x, y, c, chips, _ = place()
        px, py = chips[k]
        rows = outs[a].at[2 * px + py, half(a, core), :]
        return pltpu.make_async_remote_copy(src_ref=rows, dst_ref=rows, send_sem=fsend.at[3 * a + k],
                                            recv_sem=frecv.at[3 * a + k], device_id=(x, y, 1 - c), device_id_type=MESH)

    def start():
        me = place()[4]
        for a in range(n):
            for k in range(3):
                copy(a, k, me).start()

    def finish():
        x, y, c, chips, me = place()
        for a in range(n):
            for k, (px, py) in enumerate(chips):
                copy(a, k, 2 * px + py).wait_recv()
                forward(a, k, c).start()
        for a in range(n):
            for k in range(3):
                forward(a, k, 1 - c).wait_recv()
        for a in range(n):
            for k in range(3):
                copy(a, k, me).wait_send()
                forward(a, k, c).wait_send()

    return start, finish


def _in_proj_gather(z1, w_in, shards):
    n = len(shards)
    T, K = z1.shape
    N = w_in.shape[1]
    tm = _tile(T, 2064)
    tn = 1024
    grid = (T // tm, N // tn)

    def body(a_ref, b_ref, *rest):
        ins, o_ref, outs, sems = rest[:n], rest[n], rest[n + 1:2 * n + 1], rest[2 * n + 1:]
        start, finish = _gather_split_steps(ins, outs, *sems)
        i, j = pl.program_id(0), pl.program_id(1)

        @pl.when((i == 0) & (j == 0))
        def _():
            start()

        o_ref[...] = jnp.dot(a_ref[...], b_ref[...], preferred_element_type=f32)

        @pl.when((i == grid[0] - 1) & (j == grid[1] - 1))
        def _():
            finish()

    res = pl.pallas_call(
        body, name="in_proj", grid=grid,
        in_specs=[pl.BlockSpec((tm, K), lambda i, j: (i, 0)), pl.BlockSpec((K, tn), lambda i, j: (0, j))] + [ANY] * n,
        out_specs=[pl.BlockSpec((tm, tn), lambda i, j: (i, j))] + [ANY] * n,
        out_shape=[jax.ShapeDtypeStruct((T, N), f32)] + [jax.ShapeDtypeStruct((N_CHIPS,) + a.shape, a.dtype) for a in shards],
        scratch_shapes=_gather_split_sems(n),
        compiler_params=_params("arbitrary", "arbitrary"),
    )(z1, w_in, *shards)
    return res[0], res[1:]


def _sibling_halves(parts, name="sibling_halves"):
    n = len(parts)

    def body(*refs):
        ins, outs = refs[:n], refs[n:2 * n]
        send, recv = refs[2 * n:]
        x, y, c, _ = _place()

        def copy(a):
            rh = ins[a].shape[1] // 2
            return pltpu.make_async_remote_copy(src_ref=ins[a].at[:, pl.ds((1 - c) * rh, rh), :], dst_ref=outs[a],
                                                send_sem=send.at[a], recv_sem=recv.at[a], device_id=(x, y, 1 - c),
                                                device_id_type=MESH)

        for a in range(n):
            copy(a).start()
        for a in range(n):
            copy(a).wait_recv()
        for a in range(n):
            copy(a).wait_send()

    return pl.pallas_call(
        body, name=name, in_specs=[ANY] * n, out_specs=[ANY] * n,
        out_shape=[jax.ShapeDtypeStruct((a.shape[0], a.shape[1] // 2, a.shape[2]), a.dtype) for a in parts],
        scratch_shapes=[pltpu.SemaphoreType.DMA((n,)), pltpu.SemaphoreType.DMA((n,))],
    )(*parts)


def _add_own_half(name, part, got, core):
    nchip, R, C = part.shape
    rh = R // 2
    tr = _tile(rh, 512, 2 * SUBLANES)
    nt = rh // tr

    def body(core_ref, a_ref, b_ref, o_ref):
        del core_ref
        o_ref[...] = (a_ref[...] + b_ref[...]).astype(o_ref.dtype)

    return pl.pallas_call(
        body, name=name,
        grid_spec=pltpu.PrefetchScalarGridSpec(
            num_scalar_prefetch=1, grid=(nchip, nt),
            in_specs=[pl.BlockSpec((None, tr, C), lambda j, i, core_ref: (j, core_ref[0] * nt + i, 0)),
                      pl.BlockSpec((None, tr, C), lambda j, i, core_ref: (j, i, 0))],
            out_specs=pl.BlockSpec((None, tr, C), lambda j, i, core_ref: (j, i, 0))),
        out_shape=jax.ShapeDtypeStruct((nchip, rh, C), bf16), compiler_params=_params("parallel", "parallel"),
    )(core, part, got)


def _add_own_half_w_in(part, got, core):
    _, R, C = part.shape
    rh = R // 2
    tr = _tile(rh, 512, 2 * SUBLANES)
    nt = rh // tr
    tn = 256
    per_seg = C // tn
    per_chip = IN_COLS // N_CHIPS // tn

    def src(j):
        return ((j // per_seg + N_SEG - 1) % N_SEG, j % per_seg)

    def body(core_ref, a_ref, b_ref, o_ref):
        del core_ref
        o_ref[...] = (a_ref[...] + b_ref[...]).astype(o_ref.dtype)

    return pl.pallas_call(
        body, name="add_half_w_in",
        grid_spec=pltpu.PrefetchScalarGridSpec(
            num_scalar_prefetch=1, grid=(IN_COLS // tn, nt),
            in_specs=[pl.BlockSpec((None, tr, tn), lambda j, i, core_ref: (src(j)[0], core_ref[0] * nt + i, src(j)[1])),
                      pl.BlockSpec((None, tr, tn), lambda j, i, core_ref: (src(j)[0], i, src(j)[1]))],
            out_specs=pl.BlockSpec((None, tr, tn), lambda j, i, core_ref: (j // per_chip, i, j % per_chip))),
        out_shape=jax.ShapeDtypeStruct((N_CHIPS, rh, IN_COLS // N_CHIPS), bf16), compiler_params=_params("parallel", "parallel"),
    )(core, part, got)


def _chip_exchange(sums):
    n = len(sums)

    def body(*refs):
        start, finish = _chip_exchange_steps(refs[:n], refs[n:2 * n], *refs[2 * n:])
        start()
        finish()

    return pl.pallas_call(
        body, name="chip_exchange", in_specs=[ANY] * n, out_specs=[ANY] * n,
        out_shape=[jax.ShapeDtypeStruct(a.shape, a.dtype) for a in sums],
        scratch_shapes=_chip_exchange_sems(n),
    )(*sums)


def _chip_exchange_sems(n):
    return [pltpu.SemaphoreType.DMA((3 * n,)), pltpu.SemaphoreType.DMA((3 * n,))]


def _chip_exchange_steps(ins, outs, send, recv):
    n = len(ins)

    def copy(a, k, own_slot):
        x, y, c, chips = _place()
        px, py = chips[k]
        slot = 2 * x + y if own_slot else 2 * px + py
        return pltpu.make_async_remote_copy(src_ref=ins[a].at[2 * px + py], dst_ref=outs[a].at[slot], send_sem=send.at[3 * a + k],
                                            recv_sem=recv.at[3 * a + k], device_id=(px, py, c), device_id_type=MESH)

    def start():
        for a in range(n):
            for k in range(3):
                copy(a, k, True).start()

    def finish():
        for a in range(n):
            for k in range(3):
                copy(a, k, False).wait_recv()
        for a in range(n):
            for k in range(3):
                copy(a, k, True).wait_send()

    return start, finish


def _sum_chips(name, slots, sums, where):
    nchip, rh, C = slots.shape
    tr = _tile(rh, 512, 2 * SUBLANES)
    nt = rh // tr

    def body(where_ref, own_ref, s1_ref, s2_ref, s3_ref, o_ref):
        me = where_ref[0]
        by_dist = [r[...].astype(f32) for r in (own_ref, s1_ref, s2_ref, s3_ref)]
        acc = None
        for j in range(nchip):
            d = me ^ j
            term = jnp.where(d == 0, by_dist[0], jnp.where(d == 1, by_dist[1], jnp.where(d == 2, by_dist[2], by_dist[3])))
            acc = term if acc is None else acc + term
        o_ref[...] = acc

    def other(d):
        return pl.BlockSpec((None, tr, C), lambda i, w: (w[0] ^ d, i, 0))

    return pl.pallas_call(
        body, name=name,
        grid_spec=pltpu.PrefetchScalarGridSpec(
            num_scalar_prefetch=1, grid=(nt,),
            in_specs=[other(0), other(1), other(2), other(3)],
            out_specs=pl.BlockSpec((tr, C), lambda i, w: (w[1] * nt + i, 0))),
        out_shape=jax.ShapeDtypeStruct((2 * rh, C), f32), compiler_params=_params("parallel"),
    )(where, sums, slots, slots, slots)


def _sum_slots(name, slots):
    ns, R, C = slots.shape
    tr = _tile(R, 256)

    def body(s_ref, o_ref):
        acc = s_ref[0]
        for j in range(1, ns):
            acc = acc + s_ref[j]
        o_ref[...] = acc

    return pl.pallas_call(
        body, name=name, grid=(R // tr,), in_specs=[pl.BlockSpec((ns, tr, C), lambda i: (0, i, 0))],
        out_specs=pl.BlockSpec((tr, C), lambda i: (i, 0)), out_shape=jax.ShapeDtypeStruct((R, C), f32),
        compiler_params=_params("parallel"),
    )(slots)


def _sibling_join(fulls):
    n = len(fulls)

    def body(*refs):
        ins, outs = refs[:n], refs[n:2 * n]
        send, recv = refs[2 * n:]
        x, y, c, _ = _place()

        def copy(a, core):
            rh = ins[a].shape[0] // 2
            rows = pl.ds(core * rh, rh)
            return pltpu.make_async_remote_copy(src_ref=ins[a].at[rows, :], dst_ref=outs[a].at[rows, :], send_sem=send.at[a],
                                                recv_sem=recv.at[a], device_id=(x, y, 1 - c), device_id_type=MESH)

        for a in range(n):
            copy(a, c).start()
        for a in range(n):
            copy(a, 1 - c).wait_recv()
        for a in range(n):
            copy(a, c).wait_send()

    return pl.pallas_call(
        body, name="sibling_join", in_specs=[ANY] * n, out_specs=[ANY] * n,
        out_shape=[jax.ShapeDtypeStruct(a.shape, a.dtype) for a in fulls],
        scratch_shapes=[pltpu.SemaphoreType.DMA((n,)), pltpu.SemaphoreType.DMA((n,))],
        input_output_aliases={a: a for a in range(n)},
    )(*fulls)


def _allgather_devices(v):
    def body(v_ref, out_ref, send, recv):
        x, y, c, chips = _place()
        me, sibling = (x, y, c), (x, y, 1 - c)

        def slot(px, py, pc):
            return out_ref.at[4 * px + 2 * py + pc]

        def copy(k, block, to, src=None):
            return pltpu.make_async_remote_copy(src_ref=slot(*block) if src is None else src, dst_ref=slot(*block),
                                                send_sem=send.at[k], recv_sem=recv.at[k], device_id=to, device_id_type=MESH)

        first = [copy(0, me, sibling, src=v_ref)] + [copy(1 + j, me, (*chip, c), src=v_ref) for j, chip in enumerate(chips)]
        for cp in first:
            cp.start()
        passed = [copy(4 + j, (*chip, c), sibling) for j, chip in enumerate(chips)]
        for j, chip in enumerate(chips):
            copy(1 + j, (*chip, c), me).wait_recv()
            passed[j].start()
        copy(0, sibling, me).wait_recv()
        for j, chip in enumerate(chips):
            copy(4 + j, (*chip, 1 - c), me).wait_recv()
        for cp in first + passed:
            cp.wait_send()

    return pl.pallas_call(
        body, name="allgather_devices", in_specs=[ANY], out_specs=ANY,
        out_shape=jax.ShapeDtypeStruct((N_DEV,) + v.shape, v.dtype),
        scratch_shapes=[pltpu.SemaphoreType.DMA((N_DEV - 1,)), pltpu.SemaphoreType.DMA((N_DEV - 1,))],
    )(v)


def _adamw(name, w, g, m, v):
    R, C = w.shape
    tr = _tile(R, 256)
    c1 = 1.0 / (1.0 - ADAM_B1 ** ADAM_STEP)
    c2 = 1.0 / (1.0 - ADAM_B2 ** ADAM_STEP)

    def body(w_ref, g_ref, m_ref, v_ref, d_ref, nm_ref, nv_ref):
        gv = g_ref[...]
        nm = ADAM_B1 * m_ref[...] + (1.0 - ADAM_B1) * gv
        nv = ADAM_B2 * v_ref[...] + (1.0 - ADAM_B2) * gv * gv
        d_ref[...] = -ADAM_LR * ((nm * c1) / (jnp.sqrt(nv * c2) + ADAM_EPS) + ADAM_WD * w_ref[...])
        nm_ref[...] = nm
        nv_ref[...] = nv

    row = pl.BlockSpec((tr, C), lambda i: (i, 0))
    sh = jax.ShapeDtypeStruct((R, C), f32)
    return pl.pallas_call(body, name=name, grid=(R // tr,), in_specs=[row] * 4, out_specs=[row] * 3,
                          out_shape=[sh, sh, sh], compiler_params=_params("parallel"))(w, g, m, v)


def _adamw_update(w, g, m, v):
    c1 = 1.0 / (1.0 - ADAM_B1 ** ADAM_STEP)
    c2 = 1.0 / (1.0 - ADAM_B2 ** ADAM_STEP)
    nm = ADAM_B1 * m + (1.0 - ADAM_B1) * g
    nv = ADAM_B2 * v + (1.0 - ADAM_B2) * g * g
    return -ADAM_LR * ((nm * c1) / (jnp.sqrt(nv * c2) + ADAM_EPS) + ADAM_WD * w), nm, nv


def _adamw_many(ws, gs, ms, vs):
    n = len(ws)

    def body(*refs):
        ins, outs = refs[:4 * n], refs[4 * n:]
        for a in range(n):
            d, nm, nv = _adamw_update(ins[a][...], ins[n + a][...], ins[2 * n + a][...], ins[3 * n + a][...])
            outs[a][...] = d
            outs[n + a][...] = nm
            outs[2 * n + a][...] = nv

    shapes = [jax.ShapeDtypeStruct(a.shape, f32) for a in ws]
    return pl.pallas_call(body, name="adamw_small", out_shape=shapes * 3)(*ws, *gs, *ms, *vs)


def _zoh_parts(lr, li, log_dt):
    dt = jnp.exp(log_dt)
    mag = jnp.exp(lr * dt)
    c, s = jnp.cos(li * dt), jnp.sin(li * dt)
    ab_re, ab_im = mag * c, mag * s
    den = lr * lr + li * li
    nr = ab_re - 1.0
    coef_re = (nr * lr + ab_im * li) / den
    coef_im = (ab_im * lr - nr * li) / den
    return dt, mag, c, s, ab_re, ab_im, den, nr, coef_re, coef_im


def _zoh_fwd(lr, li, log_dt, b_re, b_im):
    def body(lr_ref, li_ref, ld_ref, br_ref, bi_ref, ar_ref, ai_ref, bbr_ref, bbi_ref):
        _, _, _, _, ab_re, ab_im, _, _, coef_re, coef_im = _zoh_parts(lr_ref[...], li_ref[...], ld_ref[...])
        ar_ref[...] = ab_re
        ai_ref[...] = ab_im
        bbr_ref[...] = coef_re * br_ref[...] - coef_im * bi_ref[...]
        bbi_ref[...] = coef_re * bi_ref[...] + coef_im * br_ref[...]

    col = jax.ShapeDtypeStruct(lr.shape, f32)
    mat = jax.ShapeDtypeStruct(b_re.shape, f32)
    return pl.pallas_call(body, name="zoh_fwd", out_shape=[col, col, mat, mat])(lr, li, log_dt, b_re, b_im)


def _zoh_bwd(lr, li, log_dt, b_re, b_im, d_ar, d_ai, d_bbr, d_bbi):
    n = lr.shape[1]
    groups = n // SSM_STATE

    def body(lr_ref, li_ref, ld_ref, br_ref, bi_ref, dar_ref, dai_ref, dbbr_ref, dbbi_ref,
             dlr_ref, dli_ref, dld_ref, dbr_ref, dbi_ref):
        lr_, li_ = lr_ref[...], li_ref[...]
        dt, mag, c, s, _, ab_im, den, nr, coef_re, coef_im = _zoh_parts(lr_, li_, ld_ref[...])
        br, bi, dbbr, dbbi = br_ref[...], bi_ref[...], dbbr_ref[...], dbbi_ref[...]
        dbr_ref[...] = coef_re * dbbr + coef_im * dbbi
        dbi_ref[...] = coef_re * dbbi - coef_im * dbbr
        d_cr = jnp.sum(dbbr * br + dbbi * bi, axis=0, keepdims=True)
        d_ci = jnp.sum(dbbi * br - dbbr * bi, axis=0, keepdims=True)
        d_nr = (d_cr * lr_ - d_ci * li_) / den
        d_abi = dai_ref[...] + (d_cr * li_ + d_ci * lr_) / den
        d_abr = dar_ref[...] + d_nr
        d_den = -(d_cr * coef_re + d_ci * coef_im) / den
        d_lr = (d_cr * nr + d_ci * ab_im) / den + 2.0 * lr_ * d_den
        d_li = (d_cr * ab_im - d_ci * nr) / den + 2.0 * li_ * d_den
        d_theta = mag * (d_abi * c - d_abr * s)
        d_arg = mag * (d_abr * c + d_abi * s)
        dlr_ref[...] = d_lr + d_arg * dt
        dli_ref[...] = d_li + d_theta * dt
        d_dt = d_arg * lr_ + d_theta * li_
        member = (lax.broadcasted_iota(jnp.int32, (n, groups), 0) >> (SSM_STATE.bit_length() - 1)
                  == lax.broadcasted_iota(jnp.int32, (n, groups), 1)).astype(f32)
        dld_ref[...] = jnp.dot(d_dt * dt, member, preferred_element_type=f32, precision=lax.Precision.HIGHEST)

    col = jax.ShapeDtypeStruct(lr.shape, f32)
    mat = jax.ShapeDtypeStruct(b_re.shape, f32)
    return pl.pallas_call(body, name="zoh_bwd", out_shape=[col, col, jax.ShapeDtypeStruct((1, groups), f32), mat, mat])(
        lr, li, log_dt, b_re, b_im, d_ar, d_ai, d_bbr, d_bbi)


def _lower_bound_fwd(logits):
    def body(x_ref, o_ref):
        x = x_ref[...]
        e = jnp.exp(x - jnp.max(x, axis=0, keepdims=True))
        o_ref[...] = e / jnp.sum(e, axis=0, keepdims=True)

    return pl.pallas_call(body, name="lower_bound_fwd", out_shape=jax.ShapeDtypeStruct(logits.shape, f32))(logits)


def _lower_bound_bwd(sm, d_lb):
    def body(sm_ref, d_ref, o_ref):
        smv = sm_ref[...]
        row = lax.broadcasted_iota(jnp.int32, smv.shape, 0)
        sm0 = smv[0:1, :]
        o_ref[...] = sm0 * d_ref[...] * (jnp.where(row == 0, 1.0, 0.0) - smv)

    return pl.pallas_call(body, name="lower_bound_bwd", out_shape=jax.ShapeDtypeStruct(sm.shape, f32))(sm, d_lb)


def _s5_tables(ab_re, ab_im, bb_re, bb_im, c_re, c_im, seg):
    eye = jnp.eye(SLAB_GROUPS, dtype=f32)

    def blk_in(bb):
        return jnp.einsum("hsgp,gk->sghkp", bb.reshape(SSM_GROUP, N_SLAB, SLAB_GROUPS, SSM_STATE), eye).reshape(
            N_SLAB, SLAB_CH, SLAB_NS)

    def blk_out(cc):
        return jnp.einsum("sghp,gk->skpgh", cc.reshape(N_SLAB, SLAB_GROUPS, SSM_GROUP, SSM_STATE), eye).reshape(
            N_SLAB, SLAB_NS, SLAB_CH)

    bs = jnp.concatenate([blk_in(bb_re), blk_in(bb_im)], axis=2).astype(bf16)
    cs = jnp.concatenate([blk_out(c_re), blk_out(-c_im)], axis=1).astype(bf16)
    n = SSM_GROUPS * SSM_STATE
    pw = _power_table(jnp.stack([ab_re.reshape(1, n), ab_im.reshape(1, n)]), -(-seg // SUBLANES))
    return bs, cs, pw


def _power_table(ab, tiles):
    n = ab.shape[2]

    def body(a_ref, o_ref):
        row = lax.broadcasted_iota(jnp.int32, (SUBLANES, n), 0)
        ar, ai = a_ref[0], a_ref[1]
        tr, ti = jnp.broadcast_to(ar, (SUBLANES, n)), jnp.broadcast_to(ai, (SUBLANES, n))
        pr, pi = ar, ai
        for r in range(1, SUBLANES):
            pr, pi = pr * ar - pi * ai, pr * ai + pi * ar
            tr = jnp.where(row == r, pr, tr)
            ti = jnp.where(row == r, pi, ti)
        o_ref[0, 0:SUBLANES, :] = tr
        o_ref[1, 0:SUBLANES, :] = ti

        def step(j, carry):
            cr, ci = carry
            cr, ci = cr * pr - ci * pi, cr * pi + ci * pr
            o_ref[0, _rows8(j), :] = cr
            o_ref[1, _rows8(j), :] = ci
            return cr, ci

        lax.fori_loop(1, tiles, step, (tr, ti))

    return pl.pallas_call(body, name="power_table", out_shape=jax.ShapeDtypeStruct((2, SUBLANES * tiles, n), f32))(ab)


def _s5_table_grads(dbs, dcs, da):
    eye = jnp.eye(SLAB_GROUPS, dtype=f32)
    d6 = dbs.reshape(N_SLAB, SLAB_GROUPS, SSM_GROUP, 2, SLAB_GROUPS, SSM_STATE)
    dbb = jnp.einsum("sghrkp,gk->rhsgp", d6, eye).reshape(2, SSM_GROUP, SSM_GROUPS * SSM_STATE)
    c6 = dcs.reshape(N_SLAB, 2, SLAB_GROUPS, SSM_STATE, SLAB_GROUPS, SSM_GROUP)
    dcc = jnp.einsum("srkpgh,gk->rsghp", c6, eye).reshape(2, SSM_GROUPS, SSM_GROUP, SSM_STATE)
    dab = da.transpose(1, 0, 2).reshape(2, SSM_GROUPS, SSM_STATE)
    return dab[0], dab[1], dbb[0], dbb[1], dcc[0], -dcc[1]


SMALL = ["mix_norm_g", "ssm_lambda_re", "ssm_lambda_im", "ssm_log_dt", "ssm_b_re", "ssm_b_im", "ssm_c_re", "ssm_c_im",
         "ssm_d", "hgrn_lb_logits", "hgrn_norm_g", "ffn_norm_g", "conv_b", "final_norm_g"]
SHARDED_SMALL = ["meta_tokens", "conv_w"]
BIG = ["w_in", "ssm_w_glu", "w_ssm_proj", "w_hgrn_proj", "w_out", "w_up", "w_down"]
WEIGHTS = ['meta_tokens', 'mix_norm_g', 'w_in', 'ssm_lambda_re', 'ssm_lambda_im', 'ssm_log_dt', 'ssm_b_re', 'ssm_b_im',
           'ssm_c_re', 'ssm_c_im', 'ssm_d', 'ssm_w_glu', 'w_ssm_proj', 'hgrn_lb_logits', 'hgrn_norm_g', 'w_hgrn_proj',
           'w_out', 'ffn_norm_g', 'w_up', 'conv_w', 'conv_b', 'w_down', 'final_norm_g']


LATER = [k for k in BIG if k != "w_in"]


def _full_weights(gathered, shards, chip):
    Dm = D_MODEL
    g = {k: lax.dynamic_update_slice(gathered[k], shards[k][None], (chip, 0, 0)) for k in gathered}
    full = {}
    for k, v in g.items():
        if k == "w_in":
            full[k] = jnp.roll(v.transpose(1, 0, 2).reshape(Dm, IN_COLS), -Dm, axis=1)
        elif k == "w_up":
            full[k] = v.transpose(1, 0, 2).reshape(Dm, 2 * D_FF)
        else:
            full[k] = v.reshape(-1, Dm)
    return full


def _local_grads(x, tgt, meta, w, full, shards, chip, core):
    B, S, Dm = x.shape
    L = S + N_META
    T = B * L
    h0 = jnp.concatenate([jnp.broadcast_to(meta[None], (B, N_META, Dm)), x], axis=1).reshape(T, Dm)

    lb_all = _lower_bound_fwd(w["hgrn_lb_logits"])
    lb = lb_all[0:1]
    gp = SSM_GROUPS * SSM_STATE
    zoh_in = (w["ssm_lambda_re"].reshape(1, gp), w["ssm_lambda_im"].reshape(1, gp),
              jnp.repeat(w["ssm_log_dt"].reshape(SSM_GROUPS, 1), SSM_STATE, axis=1).reshape(1, gp),
              w["ssm_b_re"].reshape(gp, SSM_GROUP).T, w["ssm_b_im"].reshape(gp, SSM_GROUP).T)
    ab_re, ab_im, bb_re, bb_im = _zoh_fwd(*zoh_in)
    bs, cs, pw = _s5_tables(ab_re, ab_im, bb_re, bb_im, w["ssm_c_re"][0], w["ssm_c_im"][0], L // SUBLANES)

    z1 = _rmsnorm_fwd("mix_norm", h0, w["mix_norm_g"])
    p, gathered = _in_proj_gather(z1, full["w_in"], [shards[k] for k in LATER])
    full = {**full, **_full_weights(dict(zip(LATER, gathered)), shards, chip)}
    ya0 = _s5_fwd(p, bs, cs, pw, w["ssm_d"], B, L)
    gl, ya = _glu_proj_fwd(ya0, full["ssm_w_glu"])
    yb = _hgrn_fwd(p, lb, w["hgrn_norm_g"], B, L)
    pa, pb, merged = _proj_merge_fwd(ya, yb, full["w_ssm_proj"], full["w_hgrn_proj"], p)
    h1, z2 = _out_proj_norm(merged, full["w_out"], h0, w["ffn_norm_g"])
    up = _mm_rows("up_proj", z2, full["w_up"], "nn", f32, D_FF // 2, tm_target=2064)
    ff = _conv_fwd(up, full["conv_w"], w["conv_b"], B, L)
    h2 = _mm_rows("down_proj", ff, full["w_down"], "nn", f32, 1024, res=h1, tk=D_FF // 2)

    tgt_rows = jnp.pad(tgt, ((0, 0), (N_META, 0), (0, 0))).reshape(T, Dm)
    dh2, loss, d_final_g = _final_loss(h2, tgt_rows, w["final_norm_g"].reshape(1, Dm), L)

    dff = _mm_rows("d_ff", dh2, full["w_down"], "nt", f32, D_FF // 2)
    g_w_down = _mm_wgrad("dw_down", ff, dh2, tn=512)
    dup, dconv = _conv_bwd(up, dff, full["conv_w"], w["conv_b"], B, L)
    g_w_up = _dw_up(z2, dup)
    dh1, d_ffn_g = _dz2_norm(dup, full["w_up"], h1, w["ffn_norm_g"], dh2)

    g_w_out = _mm_wgrad("dw_out", merged, dh1)
    dpa, dpb, dp = _merge_bwd_fused(dh1, full["w_out"], p, pa, pb)
    dgl, dya0_direct = _glu_bwd_fused(dpa, full["w_ssm_proj"], ya0, gl)
    g_w_ssm_proj = _mm_wgrad("dw_ssm_proj", ya, dpa)
    dyb = _mm_rows("d_yb", dpb, full["w_hgrn_proj"], "nt", f32, 1024)
    g_w_hgrn_proj = _mm_wgrad("dw_hgrn_proj", yb, dpb)
    dp, d_lb, d_hgrn_g = _hgrn_bwd(p, dyb, dp, lb, w["hgrn_norm_g"], B, L)
    dya0 = _mm_rows("d_ya0", dgl, full["ssm_w_glu"], "nt", f32, 1024, res=dya0_direct)
    g_w_glu = _mm_wgrad("dw_glu", ya0, dgl)
    parts = {
        "ssm_w_glu": g_w_glu.reshape(N_CHIPS, Dm // N_CHIPS, Dm), "w_ssm_proj": g_w_ssm_proj.reshape(N_CHIPS, Dm // N_CHIPS, Dm),
        "w_hgrn_proj": g_w_hgrn_proj.reshape(N_CHIPS, Dm // N_CHIPS, Dm), "w_out": g_w_out.reshape(N_CHIPS, Dm // N_CHIPS, Dm),
        "w_up": g_w_up, "w_down": g_w_down.reshape(N_CHIPS, D_FF // N_CHIPS, Dm),
    }
    got = _sibling_halves([parts[k] for k in LATER])
    sums = {k: _add_own_half("add_half_" + k, parts[k], gt, core) for k, gt in zip(LATER, got)}
    (dp, dbs, dcs, da, d_skip), slots_later = _s5_bwd(p, dya0, dp, bs, cs, pw, w["ssm_d"], B, L, [sums[k] for k in LATER])
    slots = dict(zip(LATER, slots_later))
    g_w_in = _dw_in(z1, dp)
    dh0, d_mix_g = _dz1_norm(dp, full["w_in"], h0, w["mix_norm_g"], dh1)

    dh0 = dh0.reshape(B, L, Dm)
    grad_x = dh0[:, N_META:]
    d_meta = _meta_grad(dh0[:, :N_META])

    d_ab_re, d_ab_im, d_bb_re, d_bb_im, d_c_re, d_c_im = _s5_table_grads(dbs, dcs, da)
    d_lr, d_li, d_log_dt, d_b_re, d_b_im = _zoh_bwd(*zoh_in, d_ab_re.reshape(1, gp), d_ab_im.reshape(1, gp), d_bb_re, d_bb_im)
    gps = (SSM_GROUPS, SSM_STATE)
    d_lr, d_li, d_log_dt = d_lr.reshape(gps), d_li.reshape(gps), d_log_dt.reshape(SSM_GROUPS)
    d_b_re, d_b_im = d_b_re.T.reshape(gps + (SSM_GROUP,)), d_b_im.T.reshape(gps + (SSM_GROUP,))
    d_logits = _lower_bound_bwd(lb_all, d_lb)
    small = {
        "meta_tokens": d_meta, "mix_norm_g": d_mix_g, "ssm_lambda_re": d_lr[None], "ssm_lambda_im": d_li[None],
        "ssm_log_dt": d_log_dt[None], "ssm_b_re": d_b_re[None], "ssm_b_im": d_b_im[None], "ssm_c_re": d_c_re[None],
        "ssm_c_im": d_c_im[None], "ssm_d": d_skip, "hgrn_lb_logits": d_logits, "hgrn_norm_g": d_hgrn_g,
        "ffn_norm_g": d_ffn_g, "conv_w": dconv[:, 0:3, :].transpose(1, 0, 2).reshape(3, 2 * D_FF),
        "conv_b": dconv[:, 3, :].reshape(1, 2 * D_FF), "final_norm_g": d_final_g.reshape(Dm),
    }
    sums["w_in"] = _add_own_half_w_in(g_w_in, _sibling_halves([g_w_in], "sibling_halves_w_in")[0], core)
    slots["w_in"] = _chip_exchange([sums["w_in"]])[0]
    return loss, grad_x, sums, slots, small


PACK_ROWS = 256


def _pack(parts):
    flat = jnp.concatenate([parts[k].reshape(-1) for k in parts])
    n = flat.shape[0]
    rows = -(-n // (PACK_ROWS * LANES)) * PACK_ROWS
    flat = jnp.pad(flat, (0, rows * LANES - n))
    return flat.reshape(rows, LANES)


def _unpack(packed, like):
    flat = packed.reshape(-1)
    out, o = {}, 0
    for k, ref in like.items():
        n = math.prod(ref.shape)
        out[k] = flat[o:o + n].reshape(ref.shape)
        o += n
    return out


def kernel(x, meta_tokens, mix_norm_g, w_in, ssm_lambda_re, ssm_lambda_im, ssm_log_dt, ssm_b_re, ssm_b_im, ssm_c_re, ssm_c_im, ssm_d, ssm_w_glu, w_ssm_proj, hgrn_lb_logits, hgrn_norm_g, w_hgrn_proj, w_out, ffn_norm_g, w_up, conv_w, conv_b, w_down, final_norm_g, loss_target, m_meta_tokens, m_mix_norm_g, m_w_in, m_ssm_lambda_re, m_ssm_lambda_im, m_ssm_log_dt, m_ssm_b_re, m_ssm_b_im, m_ssm_c_re, m_ssm_c_im, m_ssm_d, m_ssm_w_glu, m_w_ssm_proj, m_hgrn_lb_logits, m_hgrn_norm_g, m_w_hgrn_proj, m_w_out, m_ffn_norm_g, m_w_up, m_conv_w, m_conv_b, m_w_down, m_final_norm_g, v_meta_tokens, v_mix_norm_g, v_w_in, v_ssm_lambda_re, v_ssm_lambda_im, v_ssm_log_dt, v_ssm_b_re, v_ssm_b_im, v_ssm_c_re, v_ssm_c_im, v_ssm_d, v_ssm_w_glu, v_w_ssm_proj, v_hgrn_lb_logits, v_hgrn_norm_g, v_w_hgrn_proj, v_w_out, v_ffn_norm_g, v_w_up, v_conv_w, v_conv_b, v_w_down, v_final_norm_g):
    args = dict(locals())
    w = {k: args[k] for k in WEIGHTS}
    mom = {k: args["m_" + k] for k in WEIGHTS}
    var = {k: args["v_" + k] for k in WEIGHTS}
    Dm = D_MODEL
    cx, cy, cc = lax.axis_index("x"), lax.axis_index("y"), lax.axis_index("c")
    chip = 2 * cx + cy

    shards = {k: w[k][0].astype(bf16) for k in BIG}
    g_meta, g_cw = _allgather_chips([w["meta_tokens"], w["conv_w"][0]])
    full = _full_weights({"w_in": _allgather_split([shards["w_in"]])[0]}, shards, chip)
    full["conv_w"] = g_cw.transpose(1, 0, 2).reshape(3, 2 * D_FF)
    meta_full = g_meta.transpose(1, 0, 2).reshape(N_META, Dm)

    core = cc.reshape(1).astype(jnp.int32)
    loss_part, grad_x, sums, slots, small = _local_grads(x, loss_target, meta_full, w, full, shards, chip, core)

    where = jnp.stack([chip, cc]).astype(jnp.int32)
    fulls = [_sum_chips("sum_chips_" + k, slots[k], sums[k], where) for k in BIG]
    g_big = dict(zip(BIG, _sibling_join(fulls)))

    small_all = dict(small)
    small_all["loss"] = loss_part[0, 0:1]
    packed = _pack(small_all)
    slots_dev = lax.dynamic_update_slice(_allgather_devices(packed), packed[None], (2 * chip + cc, 0, 0))
    reduced = _unpack(_sum_slots("sum_devices", slots_dev), small_all)
    loss = reduced.pop("loss")[0]
    mcols = Dm // N_CHIPS
    ccols = 2 * D_FF // N_CHIPS
    grads = {k: reduced[k] for k in SMALL}
    grads["meta_tokens"] = lax.dynamic_slice(reduced["meta_tokens"], (0, chip * mcols), (N_META, mcols))
    grads["conv_w"] = lax.dynamic_slice(reduced["conv_w"], (0, chip * ccols), (3, ccols))[None]
    for k in BIG:
        grads[k] = g_big[k][None]

    delta, new_m, new_v = {}, {}, {}
    for k in BIG:
        shp = w[k].shape
        d, nm, nv = _adamw("adamw_" + k, w[k][0], grads[k][0], mom[k][0], var[k][0])
        delta[k], new_m[k], new_v[k] = d.reshape(shp), nm.reshape(shp), nv.reshape(shp)
    rest = SMALL + SHARDED_SMALL

    def flat2(a):
        return a.reshape(-1, a.shape[-1])

    outs = _adamw_many(*[[flat2(t[k]) for k in rest] for t in (w, grads, mom, var)])
    n = len(rest)
    for j, dst in enumerate((delta, new_m, new_v)):
        dst.update({k: o.reshape(w[k].shape) for k, o in zip(rest, outs[j * n:(j + 1) * n])})

    return (loss, grad_x, *[grads[k].reshape(w[k].shape) for k in WEIGHTS], *[delta[k] for k in WEIGHTS],
            *[new_m[k] for k in WEIGHTS], *[new_v[k] for k in WEIGHTS])
```

```python
import math

import jax
import jax.numpy as jnp
from jax import lax
from jax.experimental import pallas as pl
from jax.experimental.pallas import tpu as pltpu

f32 = jnp.float32
bf16 = jnp.bfloat16

D_MODEL = 1024
N_META = 16
SSM_GROUP = 16
SSM_GROUPS = 64
SSM_STATE = 64
SLAB_GROUPS = 8
N_SLAB = SSM_GROUPS // SLAB_GROUPS
SLAB_CH = SLAB_GROUPS * SSM_GROUP
SLAB_NS = SLAB_GROUPS * SSM_STATE
HEADS = 8
HEAD_DIM = 128
CHUNK = 16
D_FF = 2816
IN_COLS = 7168
EPS = 1e-6
SUBLANES = 8
LANES = 128
N_CHIPS = 4
N_DEV = 8
ADAM_LR, ADAM_B1, ADAM_B2, ADAM_EPS, ADAM_WD, ADAM_STEP = 0.001, 0.9, 0.999, 1e-08, 0.01, 10
MESH = pl.DeviceIdType.MESH
ANY = pl.BlockSpec(memory_space=pl.ANY)

SEG_Q, SEG_F, SEG_I, SEG_OG, SEG_GA, SEG_GB, SEG_U = range(7)
N_SEG = 7


def _tile(n, target, mult=SUBLANES):
    best = None
    for d in range(mult, min(n, target) + 1, mult):
        if n % d == 0:
            best = d
    return n if best is None else best


def _params(*sem):
    return pltpu.CompilerParams(dimension_semantics=sem)


def _sigmoid(x):
    return 1.0 / (1.0 + jnp.exp(-x))


_DIMS = {"nn": (((1,), (0,)), ((), ())), "nt": (((1,), (1,)), ((), ())), "tn": (((0,), (0,)), ((), ()))}


def _mm(name, a, b, dims, grid, a_spec, b_spec, out_shape, out_spec, acc_shape, res=None, res_spec=None):
    nk = grid[2]
    dn = _DIMS[dims]

    def body(*refs):
        if res is None:
            a_ref, b_ref, o_ref, acc = refs
        else:
            a_ref, b_ref, r_ref, o_ref, acc = refs
        k = pl.program_id(2)

        @pl.when(k == 0)
        def _():
            acc[...] = jnp.zeros_like(acc)

        acc[...] += lax.dot_general(a_ref[...].astype(bf16), b_ref[...].astype(bf16), dn, preferred_element_type=f32)

        @pl.when(k == nk - 1)
        def _():
            r = acc[...]
            if res is not None:
                r = r + r_ref[...]
            o_ref[...] = r.astype(o_ref.dtype)

    ins = [a, b] + ([] if res is None else [res])
    specs = [a_spec, b_spec] + ([] if res is None else [res_spec])
    return pl.pallas_call(
        body, name=name, grid=grid, in_specs=specs, out_specs=out_spec, out_shape=out_shape,
        scratch_shapes=[pltpu.VMEM(acc_shape, f32)],
        compiler_params=_params("parallel", "parallel", "arbitrary"),
    )(*ins)


def _mm_rows(name, a, w, dims, out_dtype, tn, res=None, tk=None, tm_target=1032):
    T, K = a.shape
    N = w.shape[1] if dims == "nn" else w.shape[0]
    tm = _tile(T, tm_target)
    tk = K if tk is None else tk
    grid = (T // tm, N // tn, K // tk)
    a_spec = pl.BlockSpec((tm, tk), lambda i, j, k: (i, k))
    if dims == "nn":
        b_spec = pl.BlockSpec((tk, tn), lambda i, j, k: (k, j))
    else:
        b_spec = pl.BlockSpec((tn, tk), lambda i, j, k: (j, k))
    o_spec = pl.BlockSpec((tm, tn), lambda i, j, k: (i, j))
    return _mm(name, a, w, dims, grid, a_spec, b_spec, jax.ShapeDtypeStruct((T, N), out_dtype), o_spec, (tm, tn),
               res=res, res_spec=None if res is None else o_spec)


def _mm_fused(name, pairs, dims, extras, epilogue, outs, rows=(), tm_target=688):
    T, K = pairs[0][0].shape
    N = pairs[0][1].shape[1] if dims == "nn" else pairs[0][1].shape[0]
    tm = _tile(T, tm_target)
    tn = N
    grid = (T // tm, N // tn)
    npair, nex = len(pairs), len(extras) + len(rows)
    dn = _DIMS[dims]

    def body(*refs):
        ab = refs[:2 * npair]
        ex = refs[2 * npair:2 * npair + nex]
        o_refs = refs[2 * npair + nex:]
        accs = [lax.dot_general(ab[2 * q][...].astype(bf16), ab[2 * q + 1][...].astype(bf16), dn, preferred_element_type=f32)
                for q in range(npair)]
        vals = epilogue(accs, [e[...] for e in ex])
        for o_ref, v in zip(o_refs, vals):
            if isinstance(v, (list, tuple)):
                for s_, vs in enumerate(v):
                    o_ref[s_] = vs.astype(o_ref.dtype)
            else:
                o_ref[...] = v.astype(o_ref.dtype)

    ins, specs = [], []
    for a, w in pairs:
        ins += [a, w]
        specs.append(pl.BlockSpec((tm, K), lambda i, j: (i, 0)))
        specs.append(pl.BlockSpec((K, tn), lambda i, j: (0, j)) if dims == "nn" else pl.BlockSpec((tn, K), lambda i, j: (j, 0)))
    for arr, off in extras:
        ins.append(arr)
        specs.append(pl.BlockSpec((tm, tn), lambda i, j, off=off: (i, off + j)))
    for arr in rows:
        ins.append(arr)
        specs.append(pl.BlockSpec((1, tn), lambda i, j: (0, j)))
    shapes, ospecs = [], []
    for o in outs:
        if isinstance(o, tuple):
            dt, nseg, total, blk = o
            shapes.append(jax.ShapeDtypeStruct((total, T, N), dt))
            ospecs.append(pl.BlockSpec((nseg, tm, tn), lambda i, j, blk=blk: (blk, i, j)))
        else:
            shapes.append(jax.ShapeDtypeStruct((T, N), o))
            ospecs.append(pl.BlockSpec((tm, tn), lambda i, j: (i, j)))
    return pl.pallas_call(body, name=name, grid=grid, in_specs=specs, out_specs=ospecs, out_shape=shapes,
                          compiler_params=_params("parallel", "parallel"))(*ins)


def _glu_proj_fwd(ya0, w_glu):
    def epi(accs, tiles):
        return accs[0], tiles[0] * _sigmoid(accs[0])

    return _mm_fused("glu_proj", [(ya0, w_glu)], "nn", [(ya0, 0)], epi, [f32, bf16], tm_target=1032)


def _proj_merge_fwd(ya, yb, w_sp, w_hp, p):
    def epi(accs, tiles):
        return accs[0], accs[1], _sigmoid(tiles[0]) * accs[0] + _sigmoid(tiles[1]) * accs[1]

    return _mm_fused("proj_merge", [(ya, w_sp), (yb, w_hp)], "nn", [(p, SEG_GA), (p, SEG_GB)], epi, [f32, f32, bf16])


def _merge_bwd_fused(dh1, w_out, p, pa, pb):
    def epi(accs, tiles):
        d = accs[0]
        sa, sb = _sigmoid(tiles[0]), _sigmoid(tiles[1])
        return d * sa, d * sb, [d * tiles[2] * sa * (1.0 - sa), d * tiles[3] * sb * (1.0 - sb)]

    return _mm_fused("d_merged", [(dh1, w_out)], "nt", [(p, SEG_GA), (p, SEG_GB), (pa, 0), (pb, 0)], epi,
                     [bf16, bf16, (bf16, 2, N_SEG, SEG_GA // 2)], tm_target=344)


def _out_proj_norm(merged, w_out, h0, g):
    def epi(accs, tiles):
        h1 = tiles[0] + accs[0]
        r = lax.rsqrt(jnp.mean(h1 * h1, axis=-1, keepdims=True) + EPS)
        return h1, h1 * r * tiles[1]

    return _mm_fused("out_proj", [(merged, w_out)], "nn", [(h0, 0)], epi, [f32, bf16], rows=[g], tm_target=1032)


def _mm_rmsnorm_bwd(name, a, b, grid, a_spec, b_spec, x, g, dres):
    T, Dm = x.shape
    tm = T // grid[0]
    nk = grid[2]

    def body(a_ref, b_ref, x_ref, g_ref, dres_ref, dx_ref, dg_ref, acc):
        i, k = pl.program_id(0), pl.program_id(2)

        @pl.when(k == 0)
        def _():
            acc[...] = jnp.zeros_like(acc)

        @pl.when((i == 0) & (k == 0))
        def _():
            dg_ref[...] = jnp.zeros_like(dg_ref)

        acc[...] += lax.dot_general(a_ref[...].astype(bf16), b_ref[...].astype(bf16), _DIMS["nt"], preferred_element_type=f32)

        @pl.when(k == nk - 1)
        def _():
            xv = x_ref[...]
            r = lax.rsqrt(jnp.mean(xv * xv, axis=-1, keepdims=True) + EPS)
            xn = xv * r
            dzv = acc[...]
            dzg = dzv * g_ref[...]
            dx_ref[...] = dres_ref[...] + r * (dzg - xn * jnp.mean(dzg * xn, axis=-1, keepdims=True))
            dg_ref[...] += jnp.sum(dzv * xn, axis=0, keepdims=True)

    row = pl.BlockSpec((tm, Dm), lambda i, j, k: (i, 0))
    par = pl.BlockSpec((1, Dm), lambda i, j, k: (0, 0))
    return pl.pallas_call(
        body, name=name, grid=grid, in_specs=[a_spec, b_spec, row, par, row], out_specs=[row, par],
        out_shape=[jax.ShapeDtypeStruct((T, Dm), f32), jax.ShapeDtypeStruct((1, Dm), f32)],
        scratch_shapes=[pltpu.VMEM((tm, Dm), f32)],
        compiler_params=_params("arbitrary", "arbitrary", "arbitrary"),
    )(a, b, x, g, dres)


def _glu_bwd_fused(dpa, w_sp, ya0, gl):
    def epi(accs, tiles):
        d = accs[0]
        s = _sigmoid(tiles[1])
        return d * tiles[0] * s * (1.0 - s), d * s

    return _mm_fused("d_ya", [(dpa, w_sp)], "nt", [(ya0, 0), (gl, 0)], epi, [bf16, f32], tm_target=1032)


def _mm_wgrad(name, a, g, tn=None):
    T, K = a.shape
    N = g.shape[1]
    tk = _tile(T, 1376 if K <= D_MODEL else 688)
    tn = N if tn is None else tn
    grid = (1, N // tn, T // tk)
    a_spec = pl.BlockSpec((tk, K), lambda i, j, k: (k, 0))
    g_spec = pl.BlockSpec((tk, tn), lambda i, j, k: (k, j))
    o_spec = pl.BlockSpec((K, tn), lambda i, j, k: (0, j))
    return _mm(name, a, g, "tn", grid, a_spec, g_spec, jax.ShapeDtypeStruct((K, N), f32), o_spec, (K, tn))


def _rmsnorm_fwd(name, x, g):
    T, Dm = x.shape
    tr = _tile(T, 1376)

    def body(x_ref, g_ref, z_ref):
        xv = x_ref[...]
        r = lax.rsqrt(jnp.mean(xv * xv, axis=-1, keepdims=True) + EPS)
        z_ref[...] = (xv * r * g_ref[...]).astype(z_ref.dtype)

    return pl.pallas_call(
        body, name=name, grid=(T // tr,),
        in_specs=[pl.BlockSpec((tr, Dm), lambda i: (i, 0)), pl.BlockSpec((1, Dm), lambda i: (0, 0))],
        out_specs=pl.BlockSpec((tr, Dm), lambda i: (i, 0)),
        out_shape=jax.ShapeDtypeStruct((T, Dm), bf16), compiler_params=_params("parallel"),
    )(x, g)


def _final_loss(h2, tgt, g, L):
    T, Dm = h2.shape
    tr = _tile(L, 1032)
    per_seq = L // tr

    def body(h_ref, t_ref, g_ref, dh_ref, loss_ref, dg_ref):
        pos = (pl.program_id(0) % per_seq) * tr + lax.broadcasted_iota(jnp.int32, (tr, 1), 0)
        live = jnp.where(pos >= N_META, 1.0, 0.0)
        hv = h_ref[...]
        r = lax.rsqrt(jnp.mean(hv * hv, axis=-1, keepdims=True) + EPS)
        xn = hv * r
        gv = g_ref[...]
        err = (xn * gv - t_ref[...]) * live
        dy = err * (1.0 / Dm)
        dyg = dy * gv
        dh_ref[...] = r * (dyg - xn * jnp.mean(dyg * xn, axis=-1, keepdims=True))

        @pl.when(pl.program_id(0) == 0)
        def _():
            dg_ref[...] = jnp.zeros_like(dg_ref)
            loss_ref[...] = jnp.zeros_like(loss_ref)

        dg_ref[...] += jnp.sum(dy * xn, axis=0, keepdims=True)
        loss_ref[...] += jnp.sum(err * err) * (0.5 / Dm)

    row = pl.BlockSpec((tr, Dm), lambda i: (i, 0))
    par = pl.BlockSpec((1, Dm), lambda i: (0, 0))
    return pl.pallas_call(
        body, name="final_loss", grid=(T // tr,), in_specs=[row, row, par],
        out_specs=[row, pl.BlockSpec((1, LANES), lambda i: (0, 0)), par],
        out_shape=[jax.ShapeDtypeStruct((T, Dm), f32), jax.ShapeDtypeStruct((1, LANES), f32), jax.ShapeDtypeStruct((1, Dm), f32)],
        compiler_params=_params("arbitrary"),
    )(h2, tgt, g)


def _meta_grad(dh0_meta):
    B = dh0_meta.shape[0]

    def body(d_ref, o_ref):
        acc = d_ref[0]
        for b in range(1, B):
            acc = acc + d_ref[b]
        o_ref[...] = acc

    return pl.pallas_call(body, name="meta_grad", out_shape=jax.ShapeDtypeStruct(dh0_meta.shape[1:], f32))(dh0_meta)


def _shift_down(x, k, row):
    return jnp.where(row >= k, pltpu.roll(x, k, 0), 0.0)


def _conv_fwd(up, conv_w, conv_b, B, L):
    tc = 256
    nt = D_FF // tc

    def body(xa_ref, xb_ref, wa_ref, wb_ref, ba_ref, bb_ref, o_ref):
        head = 2 * SUBLANES
        row = lax.broadcasted_iota(jnp.int32, (head, tc), 0)

        def gated(conv):
            a = conv(xa_ref, wa_ref, ba_ref)
            b = conv(xb_ref, wb_ref, bb_ref)
            return (a * _sigmoid(a) * b).astype(o_ref.dtype)

        def conv_rolled(x_ref, w_ref, b_ref):
            x = x_ref[...]
            return b_ref[...] + w_ref[0:1, :] * pltpu.roll(x, 2, 0) + w_ref[1:2, :] * pltpu.roll(x, 1, 0) + w_ref[2:3, :] * x

        def conv_head(x_ref, w_ref, b_ref):
            x = x_ref[0:head, :]
            return (b_ref[...] + w_ref[0:1, :] * _shift_down(x, 2, row) + w_ref[1:2, :] * _shift_down(x, 1, row)
                    + w_ref[2:3, :] * x)

        o_ref[...] = gated(conv_rolled)
        o_ref[0:head, :] = gated(conv_head)

    return pl.pallas_call(
        body, name="conv_fwd", grid=(B, nt),
        in_specs=[pl.BlockSpec((L, tc), lambda b, j: (b, j)), pl.BlockSpec((L, tc), lambda b, j: (b, j + nt)),
                  pl.BlockSpec((3, tc), lambda b, j: (0, j)), pl.BlockSpec((3, tc), lambda b, j: (0, j + nt)),
                  pl.BlockSpec((1, tc), lambda b, j: (0, j)), pl.BlockSpec((1, tc), lambda b, j: (0, j + nt))],
        out_specs=pl.BlockSpec((L, tc), lambda b, j: (b, j)),
        out_shape=jax.ShapeDtypeStruct((B * L, D_FF), bf16), compiler_params=_params("parallel", "parallel"),
    )(up, up, conv_w, conv_w, conv_b, conv_b)


CONV_ROWS = 2 * SUBLANES


def _rows16(i):
    return pl.ds(pl.multiple_of(i * CONV_ROWS, CONV_ROWS), CONV_ROWS)


def _conv_taps(x_ref, i, row):
    x = x_ref[_rows16(i), :]
    live = jnp.where(i > 0, 1.0, 0.0)
    r0 = jnp.maximum(i * CONV_ROWS, 2)
    p1 = x_ref[pl.ds(r0 - 1, 1), :] * live
    p2 = x_ref[pl.ds(r0 - 2, 1), :] * live
    x1 = jnp.where(row == 0, p1, pltpu.roll(x, 1, 0))
    x2 = jnp.where(row == 0, p2, jnp.where(row == 1, p1, pltpu.roll(x, 2, 0)))
    return x, x1, x2


def _conv_bwd(up, dff, conv_w, conv_b, B, L):
    tc = 256
    nt = D_FF // tc
    n = L // CONV_ROWS

    def body(xa_ref, xb_ref, d_ref, wa_ref, wb_ref, ba_ref, bb_ref, dup_ref, dw_ref, ga_ref, gb_ref):
        row = lax.broadcasted_iota(jnp.int32, (CONV_ROWS, tc), 0)

        @pl.when(pl.program_id(1) == 0)
        def _():
            dw_ref[...] = jnp.zeros_like(dw_ref)

        zero_tail = jnp.zeros((CONV_ROWS, tc), f32)
        ga_ref[L:L + CONV_ROWS, :] = zero_tail
        gb_ref[L:L + CONV_ROWS, :] = zero_tail

        def fold(v):
            return v[0:SUBLANES, :] + v[SUBLANES:CONV_ROWS, :]

        def step(i, acc):
            taps_a = _conv_taps(xa_ref, i, row)
            taps_b = _conv_taps(xb_ref, i, row)
            a = ba_ref[...] + wa_ref[0:1, :] * taps_a[2] + wa_ref[1:2, :] * taps_a[1] + wa_ref[2:3, :] * taps_a[0]
            b = bb_ref[...] + wb_ref[0:1, :] * taps_b[2] + wb_ref[1:2, :] * taps_b[1] + wb_ref[2:3, :] * taps_b[0]
            s = _sigmoid(a)
            d = d_ref[_rows16(i), :]
            g_a = d * b * s * (1.0 + a * (1.0 - s))
            g_b = d * a * s
            ga_ref[_rows16(i), :] = g_a
            gb_ref[_rows16(i), :] = g_b
            new = []
            for g, (x, x1, x2) in ((g_a, taps_a), (g_b, taps_b)):
                new += [fold(g * x2), fold(g * x1), fold(g * x), fold(g)]
            return tuple(o + v for o, v in zip(acc, new))

        z = jnp.zeros((SUBLANES, tc), f32)
        acc = _repeat_loop(n, step, (z,) * 8)
        for h in range(2):
            for t in range(4):
                dw_ref[h, t:t + 1, :] += jnp.sum(acc[4 * h + t], axis=0, keepdims=True)

        def back(i, c):
            for h, (g_ref, w_ref) in enumerate(((ga_ref, wa_ref), (gb_ref, wb_ref))):
                g = g_ref[_rows16(i), :]
                n1 = g_ref[pl.ds(i * CONV_ROWS + CONV_ROWS, 1), :]
                n2 = g_ref[pl.ds(i * CONV_ROWS + CONV_ROWS + 1, 1), :]
                u1 = jnp.where(row == CONV_ROWS - 1, n1, pltpu.roll(g, CONV_ROWS - 1, 0))
                u2 = jnp.where(row == CONV_ROWS - 1, n2, jnp.where(row == CONV_ROWS - 2, n1, pltpu.roll(g, CONV_ROWS - 2, 0)))
                dup_ref[h, _rows16(i), :] = (w_ref[2:3, :] * g + w_ref[1:2, :] * u1 + w_ref[0:1, :] * u2).astype(dup_ref.dtype)
            return c

        _repeat_loop(n, back, 0)

    return pl.pallas_call(
        body, name="conv_bwd", grid=(nt, B),
        in_specs=[pl.BlockSpec((L, tc), lambda j, b: (b, j)), pl.BlockSpec((L, tc), lambda j, b: (b, j + nt)),
                  pl.BlockSpec((L, tc), lambda j, b: (b, j)),
                  pl.BlockSpec((3, tc), lambda j, b: (0, j)), pl.BlockSpec((3, tc), lambda j, b: (0, j + nt)),
                  pl.BlockSpec((1, tc), lambda j, b: (0, j)), pl.BlockSpec((1, tc), lambda j, b: (0, j + nt))],
        out_specs=[pl.BlockSpec((2, L, tc), lambda j, b: (0, b, j)), pl.BlockSpec((2, SUBLANES, tc), lambda j, b: (0, 0, j))],
        out_shape=[jax.ShapeDtypeStruct((2, B * L, D_FF), bf16), jax.ShapeDtypeStruct((2, SUBLANES, D_FF), f32)],
        scratch_shapes=[pltpu.VMEM((L + CONV_ROWS, tc), f32), pltpu.VMEM((L + CONV_ROWS, tc), f32)],
        compiler_params=_params("parallel", "arbitrary"),
    )(up, up, dff, conv_w, conv_w, conv_b, conv_b)


GELU_C = math.sqrt(2.0 / math.pi)
GELU_A = 0.044715


def _gelu(x):
    return 0.5 * x * (1.0 + jnp.tanh(GELU_C * (x + GELU_A * x * x * x)))


def _gelu_grad(x):
    t = jnp.tanh(GELU_C * (x + GELU_A * x * x * x))
    return 0.5 * (1.0 + t) + 0.5 * x * (1.0 - t * t) * GELU_C * (1.0 + 3.0 * GELU_A * x * x)


def _cmul_add(xr, xi, ar, ai, sr, si):
    return xr + ar * sr - ai * si, xi + ar * si + ai * sr


def _s5_project_in(u_ref, bs_ref, s_ref, L, rc):
    for r in range(0, L, rc):
        s_ref[r:r + rc, :] = jnp.dot(u_ref[r:r + rc, :].astype(bf16), bs_ref[...], preferred_element_type=f32)


def _rows8(i):
    return pl.ds(pl.multiple_of(i * SUBLANES, SUBLANES), SUBLANES)


def _repeat_loop(n, step, init):
    rep = max(u for u in (6, 4, 3, 2, 1) if n % u == 0)

    def body(t, carry):
        for u in range(rep):
            carry = step(t * rep + u, carry)
        return carry

    return lax.fori_loop(0, n // rep, body, init)


def _to_segments(src_ref, dst_ref, seg):
    def step(i, c):
        dst_ref[_rows8(i), :] = src_ref[pl.ds(i, SUBLANES, stride=seg), :]
        return c

    _repeat_loop(seg, step, 0)


def _from_segments(src_ref, dst_ref, seg):
    def step(i, c):
        dst_ref[pl.ds(i, SUBLANES, stride=seg), :] = src_ref[_rows8(i), :]
        return c

    _repeat_loop(seg, step, 0)


def _half_tiles(j, seg, reverse):
    h = seg // 2
    return (_rows8(seg - 1 - j), _rows8(h - 1 - j)) if reverse else (_rows8(j), _rows8(j + h))


def _seg_local_scan(s_ref, ar, ai, seg, reverse):
    ns = SLAB_NS

    def step(j, carry):
        tiles = _half_tiles(j, seg, reverse)
        loaded = [(s_ref[rows, 0:ns], s_ref[rows, ns:2 * ns]) for rows in tiles]
        out = []
        for (xr, xi), (cr, ci) in zip(loaded, (carry[0:2], carry[2:4])):
            out += list(_cmul_add(xr, xi, ar, ai, cr, ci))
        for rows, cr, ci in zip(tiles, out[0::2], out[1::2]):
            s_ref[rows, 0:ns] = cr
            s_ref[rows, ns:2 * ns] = ci
        return tuple(out)

    z = jnp.zeros((SUBLANES, ns), f32)
    return _repeat_loop(seg // 2, step, (z, z, z, z))


def _seg_boundaries(finals, ahr, ahi, reverse):
    fxr, fxi, fyr, fyi = finals
    row = lax.broadcasted_iota(jnp.int32, fxr.shape, 0)
    zero = jnp.zeros_like(fxr[0:1, :])
    xr, xi, yr, yi = (jnp.zeros_like(fxr) for _ in range(4))
    prev = None
    for r in (range(SUBLANES - 1, -1, -1) if reverse else range(SUBLANES)):
        if prev is None:
            nxr, nxi = zero, zero
        else:
            nxr, nxi = _cmul_add(fyr[prev:prev + 1, :], fyi[prev:prev + 1, :], ahr, ahi, nyr, nyi)
        nyr, nyi = _cmul_add(fxr[r:r + 1, :], fxi[r:r + 1, :], ahr, ahi, nxr, nxi)
        xr, xi = jnp.where(row == r, nxr, xr), jnp.where(row == r, nxi, xi)
        yr, yi = jnp.where(row == r, nyr, yr), jnp.where(row == r, nyi, yi)
        prev = r
    return (xr, xi), (yr, yi)


def _s5_states(u_ref, bs_ref, pw_ref, up_ref, s_ref, L, rc):
    seg = L // SUBLANES
    h = seg // 2
    ns = SLAB_NS
    _to_segments(u_ref, up_ref, seg)
    _s5_project_in(up_ref, bs_ref, s_ref, L, rc)
    ar, ai = pw_ref[0, 0:1, :], pw_ref[1, 0:1, :]
    finals = _seg_local_scan(s_ref, ar, ai, seg, False)
    enter = _seg_boundaries(finals, pw_ref[0, h - 1:h, :], pw_ref[1, h - 1:h, :], False)

    def fix(j, c):
        pr, pi = pw_ref[0, pl.ds(j, 1), :], pw_ref[1, pl.ds(j, 1), :]
        tiles = _half_tiles(j, seg, False)
        loaded = [(s_ref[rows, 0:ns], s_ref[rows, ns:2 * ns]) for rows in tiles]
        for rows, (xr, xi), (br, bi) in zip(tiles, loaded, enter):
            xr, xi = _cmul_add(xr, xi, pr, pi, br, bi)
            s_ref[rows, 0:ns] = xr
            s_ref[rows, ns:2 * ns] = xi
        return c

    _repeat_loop(h, fix, 0)


def _pw_spec(seg_rows, order):
    if order == "bs":
        return pl.BlockSpec((2, seg_rows, SLAB_NS), lambda b, s: (0, 0, s))
    return pl.BlockSpec((2, seg_rows, SLAB_NS), lambda s, b: (0, 0, s))


def _s5_fwd(p, bs, cs, pw, d_skip, B, L):
    rc = _tile(L, 344)
    seg = L // SUBLANES

    def body(u_ref, bs_ref, cs_ref, pw_ref, d_ref, y_ref, s_ref, up_ref, yp_ref):
        _s5_states(u_ref, bs_ref, pw_ref, up_ref, s_ref, L, rc)
        for r in range(0, L, rc):
            ypre = (jnp.dot(s_ref[r:r + rc, :].astype(bf16), cs_ref[...], preferred_element_type=f32)
                    + d_ref[...] * up_ref[r:r + rc, :])
            yp_ref[r:r + rc, :] = _gelu(ypre)
        _from_segments(yp_ref, y_ref, seg)

    ucol = SEG_U * (D_MODEL // SLAB_CH)
    return pl.pallas_call(
        body, name="s5_fwd", grid=(B, N_SLAB),
        in_specs=[pl.BlockSpec((L, SLAB_CH), lambda b, s: (b, ucol + s)),
                  pl.BlockSpec((None, SLAB_CH, 2 * SLAB_NS), lambda b, s: (s, 0, 0)),
                  pl.BlockSpec((None, 2 * SLAB_NS, SLAB_CH), lambda b, s: (s, 0, 0)),
                  _pw_spec(pw.shape[1], "bs"),
                  pl.BlockSpec((1, SLAB_CH), lambda b, s: (0, s))],
        out_specs=pl.BlockSpec((L, SLAB_CH), lambda b, s: (b, s)),
        out_shape=jax.ShapeDtypeStruct((B * L, D_MODEL), f32),
        scratch_shapes=[pltpu.VMEM((L, 2 * SLAB_NS), f32), pltpu.VMEM((L, SLAB_CH), f32), pltpu.VMEM((L, SLAB_CH), f32)],
        compiler_params=_params("parallel", "parallel"),
    )(p, bs, cs, pw, d_skip)


def _s5_bwd(p, dya0, dp, bs, cs, pw, d_skip, B, L, sums):
    rc = _tile(L, 688)
    ns = SLAB_NS
    seg = L // SUBLANES
    nx = len(sums)

    def body(u_ref, dy_ref, dp_in, bs_ref, cs_ref, pw_ref, d_ref, *rest):
        xin, (du_ref, dbs_ref, dcs_ref, da_ref, dd_ref), xout = rest[:nx], rest[nx:nx + 5], rest[nx + 5:2 * nx + 5]
        s_ref, lam_ref, up_ref, dyp_ref, nat_ref, send, recv = rest[2 * nx + 5:]
        del dp_in
        start, finish = _chip_exchange_steps(xin, xout, send, recv)

        @pl.when((pl.program_id(0) == 0) & (pl.program_id(1) == 0))
        def _():
            start()

        @pl.when(pl.program_id(1) == 0)
        def _():
            dbs_ref[...] = jnp.zeros_like(dbs_ref)
            dcs_ref[...] = jnp.zeros_like(dcs_ref)
            da_ref[...] = jnp.zeros_like(da_ref)
            dd_ref[...] = jnp.zeros_like(dd_ref)

        _s5_states(u_ref, bs_ref, pw_ref, up_ref, s_ref, L, rc)
        _to_segments(dy_ref, dyp_ref, seg)
        for r in range(0, L, rc):
            u = up_ref[r:r + rc, :]
            sb = s_ref[r:r + rc, :].astype(bf16)
            ypre = jnp.dot(sb, cs_ref[...], preferred_element_type=f32) + d_ref[...] * u
            dyp = dyp_ref[r:r + rc, :] * _gelu_grad(ypre)
            dyp_ref[r:r + rc, :] = dyp
            dd_ref[...] += jnp.sum(dyp * u, axis=0, keepdims=True)
            dypb = dyp.astype(bf16)
            dcs_ref[...] += lax.dot_general(sb, dypb, _DIMS["tn"], preferred_element_type=f32)
            lam_ref[r:r + rc, :] = lax.dot_general(dypb, cs_ref[...], _DIMS["nt"], preferred_element_type=f32)

        h = seg // 2
        ar, ai = pw_ref[0, 0:1, :], -pw_ref[1, 0:1, :]
        finals = _seg_local_scan(lam_ref, ar, ai, seg, True)
        enter = _seg_boundaries(finals, pw_ref[0, h - 1:h, :], -pw_ref[1, h - 1:h, :], True)

        def fix(j, acc):
            accr, acci = acc
            pr, pi = pw_ref[0, pl.ds(j, 1), :], -pw_ref[1, pl.ds(j, 1), :]
            tiles = _half_tiles(j, seg, True)
            loaded = [(lam_ref[rows, 0:ns], lam_ref[rows, ns:2 * ns]) for rows in tiles]
            for rows, (xr, xi), (br, bi), t in zip(tiles, loaded, enter, (seg - 1 - j, h - 1 - j)):
                xr, xi = _cmul_add(xr, xi, pr, pi, br, bi)
                lam_ref[rows, 0:ns] = xr
                lam_ref[rows, ns:2 * ns] = xi
                prev = _rows8(jnp.maximum(t - 1, 0))
                live = jnp.where(t > 0, 1.0, 0.0)
                spr = s_ref[prev, 0:ns] * live
                spi = s_ref[prev, ns:2 * ns] * live
                accr, acci = accr + xr * spr + xi * spi, acci + xi * spr - xr * spi
            return accr, acci

        z = jnp.zeros((SUBLANES, ns), f32)
        accr, acci = _repeat_loop(h, fix, (z, z))
        row = lax.broadcasted_iota(jnp.int32, (SUBLANES, ns), 0)
        last = _rows8(seg - 1)
        spr = jnp.where(row == 0, 0.0, pltpu.roll(s_ref[last, 0:ns], 1, 0))
        spi = jnp.where(row == 0, 0.0, pltpu.roll(s_ref[last, ns:2 * ns], 1, 0))
        xr, xi = lam_ref[0:SUBLANES, 0:ns], lam_ref[0:SUBLANES, ns:2 * ns]
        accr = accr + xr * spr + xi * spi
        acci = acci + xi * spr - xr * spi
        da_ref[0:1, :] += jnp.sum(accr, axis=0, keepdims=True)
        da_ref[1:2, :] += jnp.sum(acci, axis=0, keepdims=True)

        for r in range(0, L, rc):
            lamb = lam_ref[r:r + rc, :].astype(bf16)
            dbs_ref[...] += lax.dot_general(up_ref[r:r + rc, :].astype(bf16), lamb, _DIMS["tn"], preferred_element_type=f32)
            nat_ref[r:r + rc, :] = (lax.dot_general(lamb, bs_ref[...], _DIMS["nt"], preferred_element_type=f32)
                                    + d_ref[...] * dyp_ref[r:r + rc, :])
        _from_segments(nat_ref, up_ref, seg)
        du_ref[...] = up_ref[...].astype(du_ref.dtype)

        @pl.when((pl.program_id(0) == N_SLAB - 1) & (pl.program_id(1) == B - 1))
        def _():
            finish()

    ucol = SEG_U * (D_MODEL // SLAB_CH)
    T = B * L
    col = pltpu.VMEM((L, SLAB_CH), f32)
    res = pl.pallas_call(
        body, name="s5_bwd", grid=(N_SLAB, B),
        in_specs=[pl.BlockSpec((L, SLAB_CH), lambda s, b: (b, ucol + s)),
                  pl.BlockSpec((L, SLAB_CH), lambda s, b: (b, s)),
                  ANY,
                  pl.BlockSpec((None, SLAB_CH, 2 * SLAB_NS), lambda s, b: (s, 0, 0)),
                  pl.BlockSpec((None, 2 * SLAB_NS, SLAB_CH), lambda s, b: (s, 0, 0)),
                  _pw_spec(pw.shape[1], "sb"),
                  pl.BlockSpec((1, SLAB_CH), lambda s, b: (0, s))] + [ANY] * nx,
        out_specs=[pl.BlockSpec((None, L, SLAB_CH), lambda s, b: (SEG_U, b, s)),
                   pl.BlockSpec((None, SLAB_CH, 2 * SLAB_NS), lambda s, b: (s, 0, 0)),
                   pl.BlockSpec((None, 2 * SLAB_NS, SLAB_CH), lambda s, b: (s, 0, 0)),
                   pl.BlockSpec((None, 2, SLAB_NS), lambda s, b: (s, 0, 0)),
                   pl.BlockSpec((1, SLAB_CH), lambda s, b: (0, s))] + [ANY] * nx,
        out_shape=[jax.ShapeDtypeStruct((N_SEG, T, D_MODEL), bf16),
                   jax.ShapeDtypeStruct((N_SLAB, SLAB_CH, 2 * SLAB_NS), f32),
                   jax.ShapeDtypeStruct((N_SLAB, 2 * SLAB_NS, SLAB_CH), f32),
                   jax.ShapeDtypeStruct((N_SLAB, 2, SLAB_NS), f32),
                   jax.ShapeDtypeStruct((1, D_MODEL), f32)] + [jax.ShapeDtypeStruct(a.shape, a.dtype) for a in sums],
        scratch_shapes=[pltpu.VMEM((L, 2 * SLAB_NS), f32), pltpu.VMEM((L, 2 * SLAB_NS), f32), col, col, col]
        + _chip_exchange_sems(nx),
        input_output_aliases={2: 0},
        compiler_params=_params("arbitrary", "arbitrary"),
    )(p, dya0, dp, bs, cs, pw, d_skip, *sums)
    return res[:5], res[5:]


def _dotb(a, b, dims="nn"):
    return lax.dot_general(a.astype(bf16), b.astype(bf16), _DIMS[dims], preferred_element_type=f32)


def _tile_scan(x, reverse):
    n, w = x.shape
    v = x.reshape(n // SUBLANES, SUBLANES, w)
    row = lax.broadcasted_iota(jnp.int32, v.shape, 1)
    for k in (1, 2, 4):
        if reverse:
            v = v + jnp.where(row < SUBLANES - k, pltpu.roll(v, SUBLANES - k, 1), 0.0)
        else:
            v = v + jnp.where(row >= k, pltpu.roll(v, k, 1), 0.0)
    p = v.reshape(n // CHUNK, 2, SUBLANES, w)
    lo, hi = p[:, 0], p[:, 1]
    if reverse:
        lo = lo + hi[:, 0:1, :]
    else:
        hi = hi + lo[:, SUBLANES - 1:SUBLANES, :]
    return jnp.stack([lo, hi], axis=1).reshape(n, w)


def _chunk_cumsum(x):
    return _tile_scan(x, False)


def _chunk_rev_cumsum(x):
    return _tile_scan(x, True)


def _chunk_last(x):
    n, w = x.shape
    p = x.reshape(n // CHUNK, CHUNK, w)
    return jnp.broadcast_to(p[:, CHUNK - 1:CHUNK, :], p.shape).reshape(n, w)


def _hgrn_local(q, fl, lb):
    sg = _sigmoid(fl)
    f = lb + (1.0 - lb) * sg
    g = jnp.log(f)
    cum = _chunk_cumsum(g)
    rest = _chunk_last(cum) - cum
    e = jnp.exp(cum)
    em = jnp.exp(-cum)
    eo = jnp.exp(rest)
    k = 1.0 - f
    return sg, f, e, em, eo, q * e, k * em, k * eo, cum + rest


def _chunk_pos(n):
    return lax.broadcasted_iota(jnp.int32, (n, HEAD_DIM), 0) & (CHUNK - 1)


def _hgrn_block_rows(L):
    return _tile(L, 688, CHUNK)


def _hgrn_specs(L, order):
    hb = D_MODEL // HEAD_DIM

    def spec(seg):
        if order == "bh":
            return pl.BlockSpec((L, HEAD_DIM), lambda b, h: (b, seg * hb + h))
        return pl.BlockSpec((L, HEAD_DIM), lambda h, b: (b, seg * hb + h))

    return [spec(SEG_Q), spec(SEG_F), spec(SEG_I), spec(SEG_OG)]


PAIR = 2 * CHUNK
CHUNK_SHIFT = CHUNK.bit_length() - 1


def _pair_steps(L, rb):
    steps = []
    nch = rb // CHUNK
    for r in range(0, L, rb):
        steps += [(r + p * PAIR, PAIR) for p in range(nch // 2)]
        if nch % 2:
            steps.append((r + (nch - 1) * CHUNK, CHUNK))
    return steps


def _pair_flags(rb):
    ci = lax.broadcasted_iota(jnp.int32, (rb, HEAD_DIM), 0) >> CHUNK_SHIFT
    odd = (ci & 1) == 1
    has_next = jnp.logical_and(jnp.logical_not(odd), ci < rb // CHUNK - 1)
    return odd, has_next


def _pair_masks(rb):
    r = lax.broadcasted_iota(jnp.int32, (rb, rb), 0)
    c = lax.broadcasted_iota(jnp.int32, (rb, rb), 1)
    rc, cc = r >> CHUNK_SHIFT, c >> CHUNK_SHIFT
    same = (rc == cc) & (c <= r)
    prev = ((rc & 1) == 1) & (cc == rc - 1)
    return same, prev


def _hgrn_pair_local(q, fl, lb, odd, has_next):
    sg, f, e, em, eo, qt, kt, ko, cend = _hgrn_local(q, fl, lb)
    n = q.shape[0]
    a = jnp.where(odd, pltpu.roll(cend, CHUNK, 0), 0.0)
    z = jnp.where(has_next, pltpu.roll(cend, n - CHUNK, 0), 0.0)
    ea, ez = jnp.exp(a), jnp.exp(z)
    return dict(sg=sg, f=f, e=e, em=em, eo=eo, qt=qt, kt=kt, ko=ko, ea=ea, ez=ez, qs=qt * ea, ks=ko * ez,
                decp=jnp.exp(cend + a + z))


def _pair_scores(qt, kt, ko, same, prev):
    return (jnp.where(same, _dotb(qt, kt, "nt"), 0.0) + jnp.where(prev, _dotb(qt, ko, "nt"), 0.0)).astype(bf16)


def _hgrn_fwd(p, lb, norm_g, B, L):
    rb = _hgrn_block_rows(L)
    steps = _pair_steps(L, rb)
    blocks = [slice(r, r + rb) for r in range(0, L, rb)]

    def body(q_ref, f_ref, v_ref, og_ref, lb_ref, ng_ref, y_ref, qs_s, ks_s, vb_s, decp_s, o_s, o2_s, u_s, sb_s):
        lbv = lb_ref[...]
        ngv = ng_ref[...]
        same, prev = _pair_masks(rb)
        odd, has_next = _pair_flags(rb)

        for rows in blocks:
            t = _hgrn_pair_local(q_ref[rows, :], f_ref[rows, :], lbv, odd, has_next)
            vb = v_ref[rows, :].astype(bf16)
            o_s[rows, :] = _dotb(_pair_scores(t["qt"], t["kt"], t["ko"], same, prev), vb)
            qs_s[rows, :] = t["qs"].astype(bf16)
            ks_s[rows, :] = t["ks"].astype(bf16)
            vb_s[rows, :] = vb
            decp_s[rows, :] = t["decp"]

        for n, (r0, nr) in enumerate(steps):
            u_s[n] = _dotb(vb_s[r0:r0 + nr, :], ks_s[r0:r0 + nr, :], "tn")
        st = jnp.zeros((HEAD_DIM, HEAD_DIM), f32)
        for n, (r0, nr) in enumerate(steps):
            sb_s[n] = st.astype(bf16)
            st = st * decp_s[r0:r0 + 1, :] + u_s[n]
        for n, (r0, nr) in enumerate(steps):
            o2_s[r0:r0 + nr, :] = _dotb(qs_s[r0:r0 + nr, :], sb_s[n], "nt")

        for rows in blocks:
            o = o_s[rows, :] + o2_s[rows, :]
            og = og_ref[rows, :]
            on = o * lax.rsqrt(jnp.mean(o * o, axis=-1, keepdims=True) + EPS) * ngv
            y_ref[rows, :] = (on * og * _sigmoid(og)).astype(y_ref.dtype)

    sb = pltpu.VMEM((L, HEAD_DIM), bf16)
    sf = pltpu.VMEM((L, HEAD_DIM), f32)
    return pl.pallas_call(
        body, name="hgrn_fwd", grid=(B, HEADS),
        in_specs=_hgrn_specs(L, "bh") + [pl.BlockSpec((1, HEAD_DIM), lambda b, h: (0, h)),
                                          pl.BlockSpec((1, HEAD_DIM), lambda b, h: (0, 0))],
        out_specs=pl.BlockSpec((L, HEAD_DIM), lambda b, h: (b, h)),
        out_shape=jax.ShapeDtypeStruct((B * L, D_MODEL), bf16),
        scratch_shapes=[sb, sb, sb, sf, sf, sf, pltpu.VMEM((len(steps), HEAD_DIM, HEAD_DIM), f32),
                        pltpu.VMEM((len(steps), HEAD_DIM, HEAD_DIM), bf16)],
        compiler_params=_params("parallel", "parallel"),
    )(p, p, p, p, lb, norm_g)


def _hgrn_bwd(p, dyb, dp, lb, norm_g, B, L):
    rb = _hgrn_block_rows(L)
    steps = _pair_steps(L, rb)
    blocks = [slice(r, r + rb) for r in range(0, L, rb)]

    def body(q_ref, f_ref, v_ref, og_ref, dy_ref, dp_in, lb_ref, ng_ref, dseg_ref, dlb_ref, dng_ref,
             st_ref, u_s, dsb_s, qt_s, kt_s, ko_s, qs_s, ks_s, vb_s, do_s,
             decp_s, o_s, o2_s, dqt_s, dkt_s, dko_s, dv_s, dv2_s, dqs_s, dks_s, ddecp_s):
        del dp_in
        lbv = lb_ref[...]
        ngv = ng_ref[...]
        same, prev = _pair_masks(rb)
        odd, has_next = _pair_flags(rb)
        pos = _chunk_pos(rb)

        @pl.when(pl.program_id(1) == 0)
        def _():
            dlb_ref[...] = jnp.zeros_like(dlb_ref)

        @pl.when((pl.program_id(0) == 0) & (pl.program_id(1) == 0))
        def _():
            dng_ref[...] = jnp.zeros_like(dng_ref)

        def scores(rows):
            return _pair_scores(qt_s[rows, :], kt_s[rows, :], ko_s[rows, :], same, prev)

        for rows in blocks:
            t = _hgrn_pair_local(q_ref[rows, :], f_ref[rows, :], lbv, odd, has_next)
            for dst, key in ((qt_s, "qt"), (kt_s, "kt"), (ko_s, "ko"), (qs_s, "qs"), (ks_s, "ks")):
                dst[rows, :] = t[key].astype(bf16)
            vb_s[rows, :] = v_ref[rows, :].astype(bf16)
            decp_s[rows, :] = t["decp"]
            o_s[rows, :] = _dotb(scores(rows), vb_s[rows, :])

        for n, (r0, nr) in enumerate(steps):
            u_s[n] = _dotb(vb_s[r0:r0 + nr, :], ks_s[r0:r0 + nr, :], "tn")
        st = jnp.zeros((HEAD_DIM, HEAD_DIM), f32)
        for n, (r0, nr) in enumerate(steps):
            st_ref[n] = st
            st = st * decp_s[r0:r0 + 1, :] + u_s[n]
        for n, (r0, nr) in enumerate(steps):
            o2_s[r0:r0 + nr, :] = _dotb(qs_s[r0:r0 + nr, :], st_ref[n], "nt")

        dng = jnp.zeros((1, HEAD_DIM), f32)
        for rows in blocks:
            o = o_s[rows, :] + o2_s[rows, :]
            og = og_ref[rows, :]
            dy = dy_ref[rows, :]
            rs = lax.rsqrt(jnp.mean(o * o, axis=-1, keepdims=True) + EPS)
            xn = o * rs
            so = _sigmoid(og)
            dseg_ref[SEG_OG, rows, :] = (dy * xn * ngv * so * (1.0 + og * (1.0 - so))).astype(dseg_ref.dtype)
            don = dy * og * so
            dng = dng + jnp.sum(don * xn, axis=0, keepdims=True)
            dxo = don * ngv
            do = (rs * (dxo - xn * jnp.mean(dxo * xn, axis=-1, keepdims=True))).astype(bf16)
            do_s[rows, :] = do
            dpf = _dotb(do, vb_s[rows, :], "nt")
            dp1 = jnp.where(same, dpf, 0.0).astype(bf16)
            dp2 = jnp.where(prev, dpf, 0.0).astype(bf16)
            dqt_s[rows, :] = _dotb(dp1, kt_s[rows, :]) + _dotb(dp2, ko_s[rows, :])
            dkt_s[rows, :] = _dotb(dp1, qt_s[rows, :], "tn")
            dko_s[rows, :] = _dotb(dp2, qt_s[rows, :], "tn")
            dv_s[rows, :] = _dotb(scores(rows), do, "tn")
        dng_ref[...] += dng

        for n, (r0, nr) in enumerate(steps):
            u_s[n] = _dotb(do_s[r0:r0 + nr, :], qs_s[r0:r0 + nr, :], "tn")
        dst = jnp.zeros((HEAD_DIM, HEAD_DIM), f32)
        for n, (r0, nr) in reversed(list(enumerate(steps))):
            dsb_s[n] = dst.astype(bf16)
            ddecp_s[r0:r0 + nr, :] = jnp.broadcast_to(jnp.sum(dst * st_ref[n], axis=0, keepdims=True), (nr, HEAD_DIM))
            dst = dst * decp_s[r0:r0 + 1, :] + u_s[n]
        for n, (r0, nr) in enumerate(steps):
            rows = slice(r0, r0 + nr)
            dqs_s[rows, :] = _dotb(do_s[rows, :], st_ref[n])
            dv2_s[rows, :] = _dotb(ks_s[rows, :], dsb_s[n], "nt")
            dks_s[rows, :] = _dotb(vb_s[rows, :], dsb_s[n])

        def chunk_sum(x):
            return _chunk_last(_chunk_cumsum(x))

        dlb = jnp.zeros((1, HEAD_DIM), f32)
        for rows in blocks:
            t = _hgrn_pair_local(q_ref[rows, :], f_ref[rows, :], lbv, odd, has_next)
            dqs, dks = dqs_s[rows, :], dks_s[rows, :]
            dqt = dqt_s[rows, :] + dqs * t["ea"]
            dko = dko_s[rows, :] + dks * t["ez"]
            dkt = dkt_s[rows, :]
            dko_ko = dko * t["ko"]
            dcum = dqt * t["qt"] - dkt * t["kt"] - dko_ko
            from_next = pltpu.roll(chunk_sum(jnp.where(odd, dqs * t["qs"], 0.0)), rb - CHUNK, 0)
            from_prev = pltpu.roll(chunk_sum(jnp.where(has_next, dks * t["ks"], 0.0)), CHUNK, 0)
            d_end = (chunk_sum(dko_ko) + jnp.where(has_next, from_next, 0.0) + jnp.where(odd, from_prev, 0.0)
                     + ddecp_s[rows, :] * t["decp"])
            dcum = dcum + jnp.where(pos == CHUNK - 1, d_end, 0.0)
            df = _chunk_rev_cumsum(dcum) / t["f"] - (dkt * t["em"] + dko * t["eo"])
            dlb = dlb + jnp.sum(df * (1.0 - t["sg"]), axis=0, keepdims=True)
            dseg_ref[SEG_Q, rows, :] = (dqt * t["e"]).astype(dseg_ref.dtype)
            dseg_ref[SEG_F, rows, :] = (df * (1.0 - lbv) * t["sg"] * (1.0 - t["sg"])).astype(dseg_ref.dtype)
            dseg_ref[SEG_I, rows, :] = (dv_s[rows, :] + dv2_s[rows, :]).astype(dseg_ref.dtype)
        dlb_ref[...] += dlb

    T = B * L
    ns = len(steps)
    sb = pltpu.VMEM((L, HEAD_DIM), bf16)
    sf = pltpu.VMEM((L, HEAD_DIM), f32)
    return pl.pallas_call(
        body, name="hgrn_bwd", grid=(HEADS, B),
        in_specs=_hgrn_specs(L, "hb") + [pl.BlockSpec((L, HEAD_DIM), lambda h, b: (b, h)), ANY,
                                          pl.BlockSpec((1, HEAD_DIM), lambda h, b: (0, h)),
                                          pl.BlockSpec((1, HEAD_DIM), lambda h, b: (0, 0))],
        out_specs=[pl.BlockSpec((4, L, HEAD_DIM), lambda h, b: (0, b, h)),
                   pl.BlockSpec((1, HEAD_DIM), lambda h, b: (0, h)),
                   pl.BlockSpec((1, HEAD_DIM), lambda h, b: (0, 0))],
        out_shape=[jax.ShapeDtypeStruct((N_SEG, T, D_MODEL), bf16), jax.ShapeDtypeStruct((1, D_MODEL), f32),
                   jax.ShapeDtypeStruct((1, HEAD_DIM), f32)],
        scratch_shapes=[pltpu.VMEM((ns, HEAD_DIM, HEAD_DIM), f32), pltpu.VMEM((ns, HEAD_DIM, HEAD_DIM), f32),
                        pltpu.VMEM((ns, HEAD_DIM, HEAD_DIM), bf16)] + [sb] * 7 + [sf] * 11,
        input_output_aliases={5: 0},
        compiler_params=_params("arbitrary", "arbitrary"),
    )(p, p, p, p, dyb, dp, lb, norm_g)


def _dz1_norm(dp, w_in_phys, h0, g, dh1):
    _, T, Dm = dp.shape
    tm = _tile(T, 1032)
    return _mm_rmsnorm_bwd("dz1", dp, w_in_phys, (T // tm, 1, N_SEG),
                           pl.BlockSpec((None, tm, Dm), lambda i, j, k: (k, i, 0)),
                           pl.BlockSpec((Dm, Dm), lambda i, j, k: (0, k)), h0, g, dh1)


def _dz2_norm(dup, w_up, h1, g, dh2):
    _, T, _ = dup.shape
    tm = _tile(T, 1032)
    tk = D_FF // 2
    return _mm_rmsnorm_bwd("dz2", dup, w_up, (T // tm, 1, 4),
                           pl.BlockSpec((None, tm, tk), lambda i, j, k: (k // 2, i, k % 2)),
                           pl.BlockSpec((D_MODEL, tk), lambda i, j, k: (0, k)), h1, g, dh2)


def _dw_in(z1, dp):
    _, T, Dm = dp.shape
    tk = _tile(T, 2064)
    return _mm("dw_in", z1, dp, "tn", (1, N_SEG, T // tk),
               pl.BlockSpec((tk, Dm), lambda i, j, k: (k, 0)),
               pl.BlockSpec((None, tk, Dm), lambda i, j, k: (j, k, 0)),
               jax.ShapeDtypeStruct((N_SEG, Dm, Dm), f32),
               pl.BlockSpec((None, Dm, Dm), lambda i, j, k: (j, 0, 0)), (Dm, Dm))


def _dw_up(z2, dup):
    _, T, _ = dup.shape
    tn = D_FF // 2
    tk = _tile(T, 1376)
    return _mm("dw_up", z2, dup, "tn", (1, N_CHIPS, T // tk),
               pl.BlockSpec((tk, D_MODEL), lambda i, j, k: (k, 0)),
               pl.BlockSpec((None, tk, tn), lambda i, j, k: (j // 2, k, j % 2)),
               jax.ShapeDtypeStruct((N_CHIPS, D_MODEL, tn), f32),
               pl.BlockSpec((None, D_MODEL, tn), lambda i, j, k: (j, 0, 0)), (D_MODEL, tn))


def _place():
    x, y, c = lax.axis_index("x"), lax.axis_index("y"), lax.axis_index("c")
    chips = [(1 - x, y), (x, 1 - y), (1 - x, 1 - y)]
    return x, y, c, chips


def _allgather_chips(arrs):
    n = len(arrs)

    def body(*refs):
        ins, outs = refs[:n], refs[n:2 * n]
        send, recv, local = refs[2 * n:]
        x, y, c, chips = _place()
        me = 2 * x + y

        def copy(a, k, slot):
            px, py = chips[k]
            return pltpu.make_async_remote_copy(src_ref=ins[a], dst_ref=outs[a].at[slot], send_sem=send.at[3 * a + k],
                                                recv_sem=recv.at[3 * a + k], device_id=(px, py, c), device_id_type=MESH)

        for a in range(n):
            pltpu.make_async_copy(ins[a], outs[a].at[me], local.at[a]).start()
            for k in range(3):
                copy(a, k, me).start()
        for a in range(n):
            for k, (px, py) in enumerate(chips):
                copy(a, k, 2 * px + py).wait_recv()
        for a in range(n):
            pltpu.make_async_copy(ins[a], outs[a].at[me], local.at[a]).wait()
            for k in range(3):
                copy(a, k, me).wait_send()

    return pl.pallas_call(
        body, name="allgather_chips", in_specs=[ANY] * n, out_specs=[ANY] * n,
        out_shape=[jax.ShapeDtypeStruct((N_CHIPS,) + a.shape, a.dtype) for a in arrs],
        scratch_shapes=[pltpu.SemaphoreType.DMA((3 * n,)), pltpu.SemaphoreType.DMA((3 * n,)), pltpu.SemaphoreType.DMA((n,))],
    )(*arrs)


def _allgather_split(arrs):
    n = len(arrs)

    def body(*refs):
        start, finish = _gather_split_steps(refs[:n], refs[n:2 * n], *refs[2 * n:])
        start()
        finish()

    return pl.pallas_call(
        body, name="allgather_split", in_specs=[ANY] * n, out_specs=[ANY] * n,
        out_shape=[jax.ShapeDtypeStruct((N_CHIPS,) + a.shape, a.dtype) for a in arrs],
        scratch_shapes=_gather_split_sems(n),
    )(*arrs)


def _gather_split_sems(n):
    return [pltpu.SemaphoreType.DMA((3 * n,)) for _ in range(4)]


def _gather_split_steps(ins, outs, send, recv, fsend, frecv):
    n = len(ins)

    def place():
        x, y, c, chips = _place()
        return x, y, c, chips, 2 * x + y

    def half(a, core):
        rh = ins[a].shape[0] // 2
        return pl.ds(core * rh, rh)

    def copy(a, k, slot):
        x, y, c, chips, _ = place()
        px, py = chips[k]
        return pltpu.make_async_remote_copy(src_ref=ins[a].at[half(a, c), :], dst_ref=outs[a].at[slot, half(a, c), :],
                                            send_sem=send.at[3 * a + k], recv_sem=recv.at[3 * a + k],
                                            device_id=(px, py, c), device_id_type=MESH)

    def forward(a, k, core):
        x, y, c, chips, _ = place()
        px, py = chips[k]
        rows = outs[a].at[2 * px + py, half(a, core), :]
        return pltpu.make_async_remote_copy(src_ref=rows, dst_ref=rows, send_sem=fsend.at[3 * a + k],
                                            recv_sem=frecv.at[3 * a + k], device_id=(x, y, 1 - c), device_id_type=MESH)

    def start():
        me = place()[4]
        for a in range(n):
            for k in range(3):
                copy(a, k, me).start()

    def finish():
        x, y, c, chips, me = place()
        for a in range(n):
            for k, (px, py) in enumerate(chips):
                copy(a, k, 2 * px + py).wait_recv()
                forward(a, k, c).start()
        for a in range(n):
            for k in range(3):
                forward(a, k, 1 - c).wait_recv()
        for a in range(n):
            for k in range(3):
                copy(a, k, me).wait_send()
                forward(a, k, c).wait_send()

    return start, finish


def _in_proj_gather(z1, w_in, shards):
    n = len(shards)
    T, K = z1.shape
    N = w_in.shape[1]
    tm = _tile(T, 2064)
    tn = 1024
    grid = (T // tm, N // tn)

    def body(a_ref, b_ref, *rest):
        ins, o_ref, outs, sems = rest[:n], rest[n], rest[n + 1:2 * n + 1], rest[2 * n + 1:]
        start, finish = _gather_split_steps(ins, outs, *sems)
        i, j = pl.program_id(0), pl.program_id(1)

        @pl.when((i == 0) & (j == 0))
        def _():
            start()

        o_ref[...] = jnp.dot(a_ref[...], b_ref[...], preferred_element_type=f32)

        @pl.when((i == grid[0] - 1) & (j == grid[1] - 1))
        def _():
            finish()

    res = pl.pallas_call(
        body, name="in_proj", grid=grid,
        in_specs=[pl.BlockSpec((tm, K), lambda i, j: (i, 0)), pl.BlockSpec((K, tn), lambda i, j: (0, j))] + [ANY] * n,
        out_specs=[pl.BlockSpec((tm, tn), lambda i, j: (i, j))] + [ANY] * n,
        out_shape=[jax.ShapeDtypeStruct((T, N), f32)] + [jax.ShapeDtypeStruct((N_CHIPS,) + a.shape, a.dtype) for a in shards],
        scratch_shapes=_gather_split_sems(n),
        compiler_params=_params("arbitrary", "arbitrary"),
    )(z1, w_in, *shards)
    return res[0], res[1:]


def _sibling_halves(parts, name="sibling_halves"):
    n = len(parts)

    def body(*refs):
        ins, outs = refs[:n], refs[n:2 * n]
        send, recv = refs[2 * n:]
        x, y, c, _ = _place()

        def copy(a):
            rh = ins[a].shape[1] // 2
            return pltpu.make_async_remote_copy(src_ref=ins[a].at[:, pl.ds((1 - c) * rh, rh), :], dst_ref=outs[a],
                                                send_sem=send.at[a], recv_sem=recv.at[a], device_id=(x, y, 1 - c),
                                                device_id_type=MESH)

        for a in range(n):
            copy(a).start()
        for a in range(n):
            copy(a).wait_recv()
        for a in range(n):
            copy(a).wait_send()

    return pl.pallas_call(
        body, name=name, in_specs=[ANY] * n, out_specs=[ANY] * n,
        out_shape=[jax.ShapeDtypeStruct((a.shape[0], a.shape[1] // 2, a.shape[2]), a.dtype) for a in parts],
        scratch_shapes=[pltpu.SemaphoreType.DMA((n,)), pltpu.SemaphoreType.DMA((n,))],
    )(*parts)


def _add_own_half(name, part, got, core):
    nchip, R, C = part.shape
    rh = R // 2
    tr = _tile(rh, 512, 2 * SUBLANES)
    nt = rh // tr

    def body(core_ref, a_ref, b_ref, o_ref):
        del core_ref
        o_ref[...] = (a_ref[...] + b_ref[...]).astype(o_ref.dtype)

    return pl.pallas_call(
        body, name=name,
        grid_spec=pltpu.PrefetchScalarGridSpec(
            num_scalar_prefetch=1, grid=(nchip, nt),
            in_specs=[pl.BlockSpec((None, tr, C), lambda j, i, core_ref: (j, core_ref[0] * nt + i, 0)),
                      pl.BlockSpec((None, tr, C), lambda j, i, core_ref: (j, i, 0))],
            out_specs=pl.BlockSpec((None, tr, C), lambda j, i, core_ref: (j, i, 0))),
        out_shape=jax.ShapeDtypeStruct((nchip, rh, C), bf16), compiler_params=_params("parallel", "parallel"),
    )(core, part, got)


def _add_own_half_w_in(part, got, core):
    _, R, C = part.shape
    rh = R // 2
    tr = _tile(rh, 512, 2 * SUBLANES)
    nt = rh // tr
    tn = 256
    per_seg = C // tn
    per_chip = IN_COLS // N_CHIPS // tn

    def src(j):
        return ((j // per_seg + N_SEG - 1) % N_SEG, j % per_seg)

    def body(core_ref, a_ref, b_ref, o_ref):
        del core_ref
        o_ref[...] = (a_ref[...] + b_ref[...]).astype(o_ref.dtype)

    return pl.pallas_call(
        body, name="add_half_w_in",
        grid_spec=pltpu.PrefetchScalarGridSpec(
            num_scalar_prefetch=1, grid=(IN_COLS // tn, nt),
            in_specs=[pl.BlockSpec((None, tr, tn), lambda j, i, core_ref: (src(j)[0], core_ref[0] * nt + i, src(j)[1])),
                      pl.BlockSpec((None, tr, tn), lambda j, i, core_ref: (src(j)[0], i, src(j)[1]))],
            out_specs=pl.BlockSpec((None, tr, tn), lambda j, i, core_ref: (j // per_chip, i, j % per_chip))),
        out_shape=jax.ShapeDtypeStruct((N_CHIPS, rh, IN_COLS // N_CHIPS), bf16), compiler_params=_params("parallel", "parallel"),
    )(core, part, got)


def _chip_exchange(sums):
    n = len(sums)

    def body(*refs):
        start, finish = _chip_exchange_steps(refs[:n], refs[n:2 * n], *refs[2 * n:])
        start()
        finish()

    return pl.pallas_call(
        body, name="chip_exchange", in_specs=[ANY] * n, out_specs=[ANY] * n,
        out_shape=[jax.ShapeDtypeStruct(a.shape, a.dtype) for a in sums],
        scratch_shapes=_chip_exchange_sems(n),
    )(*sums)


def _chip_exchange_sems(n):
    return [pltpu.SemaphoreType.DMA((3 * n,)), pltpu.SemaphoreType.DMA((3 * n,))]


def _chip_exchange_steps(ins, outs, send, recv):
    n = len(ins)

    def copy(a, k, own_slot):
        x, y, c, chips = _place()
        px, py = chips[k]
        slot = 2 * x + y if own_slot else 2 * px + py
        return pltpu.make_async_remote_copy(src_ref=ins[a].at[2 * px + py], dst_ref=outs[a].at[slot], send_sem=send.at[3 * a + k],
                                            recv_sem=recv.at[3 * a + k], device_id=(px, py, c), device_id_type=MESH)

    def start():
        for a in range(n):
            for k in range(3):
                copy(a, k, True).start()

    def finish():
        for a in range(n):
            for k in range(3):
                copy(a, k, False).wait_recv()
        for a in range(n):
            for k in range(3):
                copy(a, k, True).wait_send()

    return start, finish


def _sum_chips(name, slots, sums, where):
    nchip, rh, C = slots.shape
    tr = _tile(rh, 512, 2 * SUBLANES)
    nt = rh // tr

    def body(where_ref, own_ref, s1_ref, s2_ref, s3_ref, o_ref):
        me = where_ref[0]
        by_dist = [r[...].astype(f32) for r in (own_ref, s1_ref, s2_ref, s3_ref)]
        acc = None
        for j in range(nchip):
            d = me ^ j
            term = jnp.where(d == 0, by_dist[0], jnp.where(d == 1, by_dist[1], jnp.where(d == 2, by_dist[2], by_dist[3])))
            acc = term if acc is None else acc + term
        o_ref[...] = acc

    def other(d):
        return pl.BlockSpec((None, tr, C), lambda i, w: (w[0] ^ d, i, 0))

    return pl.pallas_call(
        body, name=name,
        grid_spec=pltpu.PrefetchScalarGridSpec(
            num_scalar_prefetch=1, grid=(nt,),
            in_specs=[other(0), other(1), other(2), other(3)],
            out_specs=pl.BlockSpec((tr, C), lambda i, w: (w[1] * nt + i, 0))),
        out_shape=jax.ShapeDtypeStruct((2 * rh, C), f32), compiler_params=_params("parallel"),
    )(where, sums, slots, slots, slots)


def _sum_slots(name, slots):
    ns, R, C = slots.shape
    tr = _tile(R, 256)

    def body(s_ref, o_ref):
        acc = s_ref[0]
        for j in range(1, ns):
            acc = acc + s_ref[j]
        o_ref[...] = acc

    return pl.pallas_call(
        body, name=name, grid=(R // tr,), in_specs=[pl.BlockSpec((ns, tr, C), lambda i: (0, i, 0))],
        out_specs=pl.BlockSpec((tr, C), lambda i: (i, 0)), out_shape=jax.ShapeDtypeStruct((R, C), f32),
        compiler_params=_params("parallel"),
    )(slots)


def _sibling_join(fulls):
    n = len(fulls)

    def body(*refs):
        ins, outs = refs[:n], refs[n:2 * n]
        send, recv = refs[2 * n:]
        x, y, c, _ = _place()

        def copy(a, core):
            rh = ins[a].shape[0] // 2
            rows = pl.ds(core * rh, rh)
            return pltpu.make_async_remote_copy(src_ref=ins[a].at[rows, :], dst_ref=outs[a].at[rows, :], send_sem=send.at[a],
                                                recv_sem=recv.at[a], device_id=(x, y, 1 - c), device_id_type=MESH)

        for a in range(n):
            copy(a, c).start()
        for a in range(n):
            copy(a, 1 - c).wait_recv()
        for a in range(n):
            copy(a, c).wait_send()

    return pl.pallas_call(
        body, name="sibling_join", in_specs=[ANY] * n, out_specs=[ANY] * n,
        out_shape=[jax.ShapeDtypeStruct(a.shape, a.dtype) for a in fulls],
        scratch_shapes=[pltpu.SemaphoreType.DMA((n,)), pltpu.SemaphoreType.DMA((n,))],
        input_output_aliases={a: a for a in range(n)},
    )(*fulls)


def _allgather_devices(v):
    def body(v_ref, out_ref, send, recv):
        x, y, c, chips = _place()
        me, sibling = (x, y, c), (x, y, 1 - c)

        def slot(px, py, pc):
            return out_ref.at[4 * px + 2 * py + pc]

        def copy(k, block, to, src=None):
            return pltpu.make_async_remote_copy(src_ref=slot(*block) if src is None else src, dst_ref=slot(*block),
                                                send_sem=send.at[k], recv_sem=recv.at[k], device_id=to, device_id_type=MESH)

        first = [copy(0, me, sibling, src=v_ref)] + [copy(1 + j, me, (*chip, c), src=v_ref) for j, chip in enumerate(chips)]
        for cp in first:
            cp.start()
        passed = [copy(4 + j, (*chip, c), sibling) for j, chip in enumerate(chips)]
        for j, chip in enumerate(chips):
            copy(1 + j, (*chip, c), me).wait_recv()
            passed[j].start()
        copy(0, sibling, me).wait_recv()
        for j, chip in enumerate(chips):
            copy(4 + j, (*chip, 1 - c), me).wait_recv()
        for cp in first + passed:
            cp.wait_send()

    return pl.pallas_call(
        body, name="allgather_devices", in_specs=[ANY], out_specs=ANY,
        out_shape=jax.ShapeDtypeStruct((N_DEV,) + v.shape, v.dtype),
        scratch_shapes=[pltpu.SemaphoreType.DMA((N_DEV - 1,)), pltpu.SemaphoreType.DMA((N_DEV - 1,))],
    )(v)


def _adamw(name, w, g, m, v):
    R, C = w.shape
    tr = _tile(R, 256)
    c1 = 1.0 / (1.0 - ADAM_B1 ** ADAM_STEP)
    c2 = 1.0 / (1.0 - ADAM_B2 ** ADAM_STEP)

    def body(w_ref, g_ref, m_ref, v_ref, d_ref, nm_ref, nv_ref):
        gv = g_ref[...]
        nm = ADAM_B1 * m_ref[...] + (1.0 - ADAM_B1) * gv
        nv = ADAM_B2 * v_ref[...] + (1.0 - ADAM_B2) * gv * gv
        d_ref[...] = -ADAM_LR * ((nm * c1) / (jnp.sqrt(nv * c2) + ADAM_EPS) + ADAM_WD * w_ref[...])
        nm_ref[...] = nm
        nv_ref[...] = nv

    row = pl.BlockSpec((tr, C), lambda i: (i, 0))
    sh = jax.ShapeDtypeStruct((R, C), f32)
    return pl.pallas_call(body, name=name, grid=(R // tr,), in_specs=[row] * 4, out_specs=[row] * 3,
                          out_shape=[sh, sh, sh], compiler_params=_params("parallel"))(w, g, m, v)


def _adamw_update(w, g, m, v):
    c1 = 1.0 / (1.0 - ADAM_B1 ** ADAM_STEP)
    c2 = 1.0 / (1.0 - ADAM_B2 ** ADAM_STEP)
    nm = ADAM_B1 * m + (1.0 - ADAM_B1) * g
    nv = ADAM_B2 * v + (1.0 - ADAM_B2) * g * g
    return -ADAM_LR * ((nm * c1) / (jnp.sqrt(nv * c2) + ADAM_EPS) + ADAM_WD * w), nm, nv


def _adamw_many(ws, gs, ms, vs):
    n = len(ws)

    def body(*refs):
        ins, outs = refs[:4 * n], refs[4 * n:]
        for a in range(n):
            d, nm, nv = _adamw_update(ins[a][...], ins[n + a][...], ins[2 * n + a][...], ins[3 * n + a][...])
            outs[a][...] = d
            outs[n + a][...] = nm
            outs[2 * n + a][...] = nv

    shapes = [jax.ShapeDtypeStruct(a.shape, f32) for a in ws]
    return pl.pallas_call(body, name="adamw_small", out_shape=shapes * 3)(*ws, *gs, *ms, *vs)


def _zoh_parts(lr, li, log_dt):
    dt = jnp.exp(log_dt)
    mag = jnp.exp(lr * dt)
    c, s = jnp.cos(li * dt), jnp.sin(li * dt)
    ab_re, ab_im = mag * c, mag * s
    den = lr * lr + li * li
    nr = ab_re - 1.0
    coef_re = (nr * lr + ab_im * li) / den
    coef_im = (ab_im * lr - nr * li) / den
    return dt, mag, c, s, ab_re, ab_im, den, nr, coef_re, coef_im


def _zoh_fwd(lr, li, log_dt, b_re, b_im):
    def body(lr_ref, li_ref, ld_ref, br_ref, bi_ref, ar_ref, ai_ref, bbr_ref, bbi_ref):
        _, _, _, _, ab_re, ab_im, _, _, coef_re, coef_im = _zoh_parts(lr_ref[...], li_ref[...], ld_ref[...])
        ar_ref[...] = ab_re
        ai_ref[...] = ab_im
        bbr_ref[...] = coef_re * br_ref[...] - coef_im * bi_ref[...]
        bbi_ref[...] = coef_re * bi_ref[...] + coef_im * br_ref[...]

    col = jax.ShapeDtypeStruct(lr.shape, f32)
    mat = jax.ShapeDtypeStruct(b_re.shape, f32)
    return pl.pallas_call(body, name="zoh_fwd", out_shape=[col, col, mat, mat])(lr, li, log_dt, b_re, b_im)


def _zoh_bwd(lr, li, log_dt, b_re, b_im, d_ar, d_ai, d_bbr, d_bbi):
    n = lr.shape[1]
    groups = n // SSM_STATE

    def body(lr_ref, li_ref, ld_ref, br_ref, bi_ref, dar_ref, dai_ref, dbbr_ref, dbbi_ref,
             dlr_ref, dli_ref, dld_ref, dbr_ref, dbi_ref):
        lr_, li_ = lr_ref[...], li_ref[...]
        dt, mag, c, s, _, ab_im, den, nr, coef_re, coef_im = _zoh_parts(lr_, li_, ld_ref[...])
        br, bi, dbbr, dbbi = br_ref[...], bi_ref[...], dbbr_ref[...], dbbi_ref[...]
        dbr_ref[...] = coef_re * dbbr + coef_im * dbbi
        dbi_ref[...] = coef_re * dbbi - coef_im * dbbr
        d_cr = jnp.sum(dbbr * br + dbbi * bi, axis=0, keepdims=True)
        d_ci = jnp.sum(dbbi * br - dbbr * bi, axis=0, keepdims=True)
        d_nr = (d_cr * lr_ - d_ci * li_) / den
        d_abi = dai_ref[...] + (d_cr * li_ + d_ci * lr_) / den
        d_abr = dar_ref[...] + d_nr
        d_den = -(d_cr * coef_re + d_ci * coef_im) / den
        d_lr = (d_cr * nr + d_ci * ab_im) / den + 2.0 * lr_ * d_den
        d_li = (d_cr * ab_im - d_ci * nr) / den + 2.0 * li_ * d_den
        d_theta = mag * (d_abi * c - d_abr * s)
        d_arg = mag * (d_abr * c + d_abi * s)
        dlr_ref[...] = d_lr + d_arg * dt
        dli_ref[...] = d_li + d_theta * dt
        d_dt = d_arg * lr_ + d_theta * li_
        member = (lax.broadcasted_iota(jnp.int32, (n, groups), 0) >> (SSM_STATE.bit_length() - 1)
                  == lax.broadcasted_iota(jnp.int32, (n, groups), 1)).astype(f32)
        dld_ref[...] = jnp.dot(d_dt * dt, member, preferred_element_type=f32, precision=lax.Precision.HIGHEST)

    col = jax.ShapeDtypeStruct(lr.shape, f32)
    mat = jax.ShapeDtypeStruct(b_re.shape, f32)
    return pl.pallas_call(body, name="zoh_bwd", out_shape=[col, col, jax.ShapeDtypeStruct((1, groups), f32), mat, mat])(
        lr, li, log_dt, b_re, b_im, d_ar, d_ai, d_bbr, d_bbi)


def _lower_bound_fwd(logits):
    def body(x_ref, o_ref):
        x = x_ref[...]
        e = jnp.exp(x - jnp.max(x, axis=0, keepdims=True))
        o_ref[...] = e / jnp.sum(e, axis=0, keepdims=True)

    return pl.pallas_call(body, name="lower_bound_fwd", out_shape=jax.ShapeDtypeStruct(logits.shape, f32))(logits)


def _lower_bound_bwd(sm, d_lb):
    def body(sm_ref, d_ref, o_ref):
        smv = sm_ref[...]
        row = lax.broadcasted_iota(jnp.int32, smv.shape, 0)
        sm0 = smv[0:1, :]
        o_ref[...] = sm0 * d_ref[...] * (jnp.where(row == 0, 1.0, 0.0) - smv)

    return pl.pallas_call(body, name="lower_bound_bwd", out_shape=jax.ShapeDtypeStruct(sm.shape, f32))(sm, d_lb)


def _s5_tables(ab_re, ab_im, bb_re, bb_im, c_re, c_im, seg):
    eye = jnp.eye(SLAB_GROUPS, dtype=f32)

    def blk_in(bb):
        return jnp.einsum("hsgp,gk->sghkp", bb.reshape(SSM_GROUP, N_SLAB, SLAB_GROUPS, SSM_STATE), eye).reshape(
            N_SLAB, SLAB_CH, SLAB_NS)

    def blk_out(cc):
        return jnp.einsum("sghp,gk->skpgh", cc.reshape(N_SLAB, SLAB_GROUPS, SSM_GROUP, SSM_STATE), eye).reshape(
            N_SLAB, SLAB_NS, SLAB_CH)

    bs = jnp.concatenate([blk_in(bb_re), blk_in(bb_im)], axis=2).astype(bf16)
    cs = jnp.concatenate([blk_out(c_re), blk_out(-c_im)], axis=1).astype(bf16)
    n = SSM_GROUPS * SSM_STATE
    pw = _power_table(jnp.stack([ab_re.reshape(1, n), ab_im.reshape(1, n)]), -(-seg // SUBLANES))
    return bs, cs, pw


def _power_table(ab, tiles):
    n = ab.shape[2]

    def body(a_ref, o_ref):
        row = lax.broadcasted_iota(jnp.int32, (SUBLANES, n), 0)
        ar, ai = a_ref[0], a_ref[1]
        tr, ti = jnp.broadcast_to(ar, (SUBLANES, n)), jnp.broadcast_to(ai, (SUBLANES, n))
        pr, pi = ar, ai
        for r in range(1, SUBLANES):
            pr, pi = pr * ar - pi * ai, pr * ai + pi * ar
            tr = jnp.where(row == r, pr, tr)
            ti = jnp.where(row == r, pi, ti)
        o_ref[0, 0:SUBLANES, :] = tr
        o_ref[1, 0:SUBLANES, :] = ti

        def step(j, carry):
            cr, ci = carry
            cr, ci = cr * pr - ci * pi, cr * pi + ci * pr
            o_ref[0, _rows8(j), :] = cr
            o_ref[1, _rows8(j), :] = ci
            return cr, ci

        lax.fori_loop(1, tiles, step, (tr, ti))

    return pl.pallas_call(body, name="power_table", out_shape=jax.ShapeDtypeStruct((2, SUBLANES * tiles, n), f32))(ab)


def _s5_table_grads(dbs, dcs, da):
    eye = jnp.eye(SLAB_GROUPS, dtype=f32)
    d6 = dbs.reshape(N_SLAB, SLAB_GROUPS, SSM_GROUP, 2, SLAB_GROUPS, SSM_STATE)
    dbb = jnp.einsum("sghrkp,gk->rhsgp", d6, eye).reshape(2, SSM_GROUP, SSM_GROUPS * SSM_STATE)
    c6 = dcs.reshape(N_SLAB, 2, SLAB_GROUPS, SSM_STATE, SLAB_GROUPS, SSM_GROUP)
    dcc = jnp.einsum("srkpgh,gk->rsghp", c6, eye).reshape(2, SSM_GROUPS, SSM_GROUP, SSM_STATE)
    dab = da.transpose(1, 0, 2).reshape(2, SSM_GROUPS, SSM_STATE)
    return dab[0], dab[1], dbb[0], dbb[1], dcc[0], -dcc[1]


SMALL = ["mix_norm_g", "ssm_lambda_re", "ssm_lambda_im", "ssm_log_dt", "ssm_b_re", "ssm_b_im", "ssm_c_re", "ssm_c_im",
         "ssm_d", "hgrn_lb_logits", "hgrn_norm_g", "ffn_norm_g", "conv_b", "final_norm_g"]
SHARDED_SMALL = ["meta_tokens", "conv_w"]
BIG = ["w_in", "ssm_w_glu", "w_ssm_proj", "w_hgrn_proj", "w_out", "w_up", "w_down"]
WEIGHTS = ['meta_tokens', 'mix_norm_g', 'w_in', 'ssm_lambda_re', 'ssm_lambda_im', 'ssm_log_dt', 'ssm_b_re', 'ssm_b_im',
           'ssm_c_re', 'ssm_c_im', 'ssm_d', 'ssm_w_glu', 'w_ssm_proj', 'hgrn_lb_logits', 'hgrn_norm_g', 'w_hgrn_proj',
           'w_out', 'ffn_norm_g', 'w_up', 'conv_w', 'conv_b', 'w_down', 'final_norm_g']


LATER = [k for k in BIG if k != "w_in"]


def _full_weights(gathered, shards, chip):
    Dm = D_MODEL
    g = {k: lax.dynamic_update_slice(gathered[k], shards[k][None], (chip, 0, 0)) for k in gathered}
    full = {}
    for k, v in g.items():
        if k == "w_in":
            full[k] = jnp.roll(v.transpose(1, 0, 2).reshape(Dm, IN_COLS), -Dm, axis=1)
        elif k == "w_up":
            full[k] = v.transpose(1, 0, 2).reshape(Dm, 2 * D_FF)
        else:
            full[k] = v.reshape(-1, Dm)
    return full


def _local_grads(x, tgt, meta, w, full, shards, chip, core):
    B, S, Dm = x.shape
    L = S + N_META
    T = B * L
    h0 = jnp.concatenate([jnp.broadcast_to(meta[None], (B, N_META, Dm)), x], axis=1).reshape(T, Dm)

    lb_all = _lower_bound_fwd(w["hgrn_lb_logits"])
    lb = lb_all[0:1]
    gp = SSM_GROUPS * SSM_STATE
    zoh_in = (w["ssm_lambda_re"].reshape(1, gp), w["ssm_lambda_im"].reshape(1, gp),
              jnp.repeat(w["ssm_log_dt"].reshape(SSM_GROUPS, 1), SSM_STATE, axis=1).reshape(1, gp),
              w["ssm_b_re"].reshape(gp, SSM_GROUP).T, w["ssm_b_im"].reshape(gp, SSM_GROUP).T)
    ab_re, ab_im, bb_re, bb_im = _zoh_fwd(*zoh_in)
    bs, cs, pw = _s5_tables(ab_re, ab_im, bb_re, bb_im, w["ssm_c_re"][0], w["ssm_c_im"][0], L // SUBLANES)

    z1 = _rmsnorm_fwd("mix_norm", h0, w["mix_norm_g"])
    p, gathered = _in_proj_gather(z1, full["w_in"], [shards[k] for k in LATER])
    full = {**full, **_full_weights(dict(zip(LATER, gathered)), shards, chip)}
    ya0 = _s5_fwd(p, bs, cs, pw, w["ssm_d"], B, L)
    gl, ya = _glu_proj_fwd(ya0, full["ssm_w_glu"])
    yb = _hgrn_fwd(p, lb, w["hgrn_norm_g"], B, L)
    pa, pb, merged = _proj_merge_fwd(ya, yb, full["w_ssm_proj"], full["w_hgrn_proj"], p)
    h1, z2 = _out_proj_norm(merged, full["w_out"], h0, w["ffn_norm_g"])
    up = _mm_rows("up_proj", z2, full["w_up"], "nn", f32, D_FF // 2, tm_target=2064)
    ff = _conv_fwd(up, full["conv_w"], w["conv_b"], B, L)
    h2 = _mm_rows("down_proj", ff, full["w_down"], "nn", f32, 1024, res=h1, tk=D_FF // 2)

    tgt_rows = jnp.pad(tgt, ((0, 0), (N_META, 0), (0, 0))).reshape(T, Dm)
    dh2, loss, d_final_g = _final_loss(h2, tgt_rows, w["final_norm_g"].reshape(1, Dm), L)

    dff = _mm_rows("d_ff", dh2, full["w_down"], "nt", f32, D_FF // 2)
    g_w_down = _mm_wgrad("dw_down", ff, dh2, tn=512)
    dup, dconv = _conv_bwd(up, dff, full["conv_w"], w["conv_b"], B, L)
    g_w_up = _dw_up(z2, dup)
    dh1, d_ffn_g = _dz2_norm(dup, full["w_up"], h1, w["ffn_norm_g"], dh2)

    g_w_out = _mm_wgrad("dw_out", merged, dh1)
    dpa, dpb, dp = _merge_bwd_fused(dh1, full["w_out"], p, pa, pb)
    dgl, dya0_direct = _glu_bwd_fused(dpa, full["w_ssm_proj"], ya0, gl)
    g_w_ssm_proj = _mm_wgrad("dw_ssm_proj", ya, dpa)
    dyb = _mm_rows("d_yb", dpb, full["w_hgrn_proj"], "nt", f32, 1024)
    g_w_hgrn_proj = _mm_wgrad("dw_hgrn_proj", yb, dpb)
    dp, d_lb, d_hgrn_g = _hgrn_bwd(p, dyb, dp, lb, w["hgrn_norm_g"], B, L)
    dya0 = _mm_rows("d_ya0", dgl, full["ssm_w_glu"], "nt", f32, 1024, res=dya0_direct)
    g_w_glu = _mm_wgrad("dw_glu", ya0, dgl)
    parts = {
        "ssm_w_glu": g_w_glu.reshape(N_CHIPS, Dm // N_CHIPS, Dm), "w_ssm_proj": g_w_ssm_proj.reshape(N_CHIPS, Dm // N_CHIPS, Dm),
        "w_hgrn_proj": g_w_hgrn_proj.reshape(N_CHIPS, Dm // N_CHIPS, Dm), "w_out": g_w_out.reshape(N_CHIPS, Dm // N_CHIPS, Dm),
        "w_up": g_w_up, "w_down": g_w_down.reshape(N_CHIPS, D_FF // N_CHIPS, Dm),
    }
    got = _sibling_halves([parts[k] for k in LATER])
    sums = {k: _add_own_half("add_half_" + k, parts[k], gt, core) for k, gt in zip(LATER, got)}
    (dp, dbs, dcs, da, d_skip), slots_later = _s5_bwd(p, dya0, dp, bs, cs, pw, w["ssm_d"], B, L, [sums[k] for k in LATER])
    slots = dict(zip(LATER, slots_later))
    g_w_in = _dw_in(z1, dp)
    dh0, d_mix_g = _dz1_norm(dp, full["w_in"], h0, w["mix_norm_g"], dh1)

    dh0 = dh0.reshape(B, L, Dm)
    grad_x = dh0[:, N_META:]
    d_meta = _meta_grad(dh0[:, :N_META])

    d_ab_re, d_ab_im, d_bb_re, d_bb_im, d_c_re, d_c_im = _s5_table_grads(dbs, dcs, da)
    d_lr, d_li, d_log_dt, d_b_re, d_b_im = _zoh_bwd(*zoh_in, d_ab_re.reshape(1, gp), d_ab_im.reshape(1, gp), d_bb_re, d_bb_im)
    gps = (SSM_GROUPS, SSM_STATE)
    d_lr, d_li, d_log_dt = d_lr.reshape(gps), d_li.reshape(gps), d_log_dt.reshape(SSM_GROUPS)
    d_b_re, d_b_im = d_b_re.T.reshape(gps + (SSM_GROUP,)), d_b_im.T.reshape(gps + (SSM_GROUP,))
    d_logits = _lower_bound_bwd(lb_all, d_lb)
    small = {
        "meta_tokens": d_meta, "mix_norm_g": d_mix_g, "ssm_lambda_re": d_lr[None], "ssm_lambda_im": d_li[None],
        "ssm_log_dt": d_log_dt[None], "ssm_b_re": d_b_re[None], "ssm_b_im": d_b_im[None], "ssm_c_re": d_c_re[None],
        "ssm_c_im": d_c_im[None], "ssm_d": d_skip, "hgrn_lb_logits": d_logits, "hgrn_norm_g": d_hgrn_g,
        "ffn_norm_g": d_ffn_g, "conv_w": dconv[:, 0:3, :].transpose(1, 0, 2).reshape(3, 2 * D_FF),
        "conv_b": dconv[:, 3, :].reshape(1, 2 * D_FF), "final_norm_g": d_final_g.reshape(Dm),
    }
    sums["w_in"] = _add_own_half_w_in(g_w_in, _sibling_halves([g_w_in], "sibling_halves_w_in")[0], core)
    slots["w_in"] = _chip_exchange([sums["w_in"]])[0]
    return loss, grad_x, sums, slots, small


PACK_ROWS = 256


def _pack(parts):
    flat = jnp.concatenate([parts[k].reshape(-1) for k in parts])
    n = flat.shape[0]
    rows = -(-n // (PACK_ROWS * LANES)) * PACK_ROWS
    flat = jnp.pad(flat, (0, rows * LANES - n))
    return flat.reshape(rows, LANES)


def _unpack(packed, like):
    flat = packed.reshape(-1)
    out, o = {}, 0
    for k, ref in like.items():
        n = math.prod(ref.shape)
        out[k] = flat[o:o + n].reshape(ref.shape)
        o += n
    return out


def kernel(x, meta_tokens, mix_norm_g, w_in, ssm_lambda_re, ssm_lambda_im, ssm_log_dt, ssm_b_re, ssm_b_im, ssm_c_re, ssm_c_im, ssm_d, ssm_w_glu, w_ssm_proj, hgrn_lb_logits, hgrn_norm_g, w_hgrn_proj, w_out, ffn_norm_g, w_up, conv_w, conv_b, w_down, final_norm_g, loss_target, m_meta_tokens, m_mix_norm_g, m_w_in, m_ssm_lambda_re, m_ssm_lambda_im, m_ssm_log_dt, m_ssm_b_re, m_ssm_b_im, m_ssm_c_re, m_ssm_c_im, m_ssm_d, m_ssm_w_glu, m_w_ssm_proj, m_hgrn_lb_logits, m_hgrn_norm_g, m_w_hgrn_proj, m_w_out, m_ffn_norm_g, m_w_up, m_conv_w, m_conv_b, m_w_down, m_final_norm_g, v_meta_tokens, v_mix_norm_g, v_w_in, v_ssm_lambda_re, v_ssm_lambda_im, v_ssm_log_dt, v_ssm_b_re, v_ssm_b_im, v_ssm_c_re, v_ssm_c_im, v_ssm_d, v_ssm_w_glu, v_w_ssm_proj, v_hgrn_lb_logits, v_hgrn_norm_g, v_w_hgrn_proj, v_w_out, v_ffn_norm_g, v_w_up, v_conv_w, v_conv_b, v_w_down, v_final_norm_g):
    args = dict(locals())
    w = {k: args[k] for k in WEIGHTS}
    mom = {k: args["m_" + k] for k in WEIGHTS}
    var = {k: args["v_" + k] for k in WEIGHTS}
    Dm = D_MODEL
    cx, cy, cc = lax.axis_index("x"), lax.axis_index("y"), lax.axis_index("c")
    chip = 2 * cx + cy

    shards = {k: w[k][0].astype(bf16) for k in BIG}
    g_meta, g_cw = _allgather_chips([w["meta_tokens"], w["conv_w"][0]])
    full = _full_weights({"w_in": _allgather_split([shards["w_in"]])[0]}, shards, chip)
    full["conv_w"] = g_cw.transpose(1, 0, 2).reshape(3, 2 * D_FF)
    meta_full = g_meta.transpose(1, 0, 2).reshape(N_META, Dm)

    core = cc.reshape(1).astype(jnp.int32)
    loss_part, grad_x, sums, slots, small = _local_grads(x, loss_target, meta_full, w, full, shards, chip, core)

    where = jnp.stack([chip, cc]).astype(jnp.int32)
    fulls = [_sum_chips("sum_chips_" + k, slots[k], sums[k], where) for k in BIG]
    g_big = dict(zip(BIG, _sibling_join(fulls)))

    small_all = dict(small)
    small_all["loss"] = loss_part[0, 0:1]
    packed = _pack(small_all)
    slots_dev = lax.dynamic_update_slice(_allgather_devices(packed), packed[None], (2 * chip + cc, 0, 0))
    reduced = _unpack(_sum_slots("sum_devices", slots_dev), small_all)
    loss = reduced.pop("loss")[0]
    mcols = Dm // N_CHIPS
    ccols = 2 * D_FF // N_CHIPS
    grads = {k: reduced[k] for k in SMALL}
    grads["meta_tokens"] = lax.dynamic_slice(reduced["meta_tokens"], (0, chip * mcols), (N_META, mcols))
    grads["conv_w"] = lax.dynamic_slice(reduced["conv_w"], (0, chip * ccols), (3, ccols))[None]
    for k in BIG:
        grads[k] = g_big[k][None]

    delta, new_m, new_v = {}, {}, {}
    for k in BIG:
        shp = w[k].shape
        d, nm, nv = _adamw("adamw_" + k, w[k][0], grads[k][0], mom[k][0], var[k][0])
        delta[k], new_m[k], new_v[k] = d.reshape(shp), nm.reshape(shp), nv.reshape(shp)
    rest = SMALL + SHARDED_SMALL

    def flat2(a):
        return a.reshape(-1, a.shape[-1])

    outs = _adamw_many(*[[flat2(t[k]) for k in rest] for t in (w, grads, mom, var)])
    n = len(rest)
    for j, dst in enumerate((delta, new_m, new_v)):
        dst.update({k: o.reshape(w[k].shape) for k, o in zip(rest, outs[j * n:(j + 1) * n])})

    return (loss, grad_x, *[grads[k].reshape(w[k].shape) for k in WEIGHTS], *[delta[k] for k in WEIGHTS],
            *[new_m[k] for k in WEIGHTS], *[new_v[k] for k in WEIGHTS])
```

```python
import math

import jax
import jax.numpy as jnp
from jax import lax
from jax.experimental import pallas as pl
from jax.experimental.pallas import tpu as pltpu

f32 = jnp.float32
bf16 = jnp.bfloat16

D_MODEL = 1024
N_META = 16
SSM_GROUP = 16
SSM_GROUPS = 64
SSM_STATE = 64
SLAB_GROUPS = 8
N_SLAB = SSM_GROUPS // SLAB_GROUPS
SLAB_CH = SLAB_GROUPS * SSM_GROUP
SLAB_NS = SLAB_GROUPS * SSM_STATE
HEADS = 8
HEAD_DIM = 128
CHUNK = 16
D_FF = 2816
IN_COLS = 7168
EPS = 1e-6
SUBLANES = 8
LANES = 128
N_CHIPS = 4
N_DEV = 8
ADAM_LR, ADAM_B1, ADAM_B2, ADAM_EPS, ADAM_WD, ADAM_STEP = 0.001, 0.9, 0.999, 1e-08, 0.01, 10
MESH = pl.DeviceIdType.MESH
ANY = pl.BlockSpec(memory_space=pl.ANY)

SEG_Q, SEG_F, SEG_I, SEG_OG, SEG_GA, SEG_GB, SEG_U = range(7)
N_SEG = 7


def _tile(n, target, mult=SUBLANES):
    best = None
    for d in range(mult, min(n, target) + 1, mult):
        if n % d == 0:
            best = d
    return n if best is None else best


def _params(*sem):
    return pltpu.CompilerParams(dimension_semantics=sem)


def _sigmoid(x):
    return 1.0 / (1.0 + jnp.exp(-x))


_DIMS = {"nn": (((1,), (0,)), ((), ())), "nt": (((1,), (1,)), ((), ())), "tn": (((0,), (0,)), ((), ()))}


def _mm(name, a, b, dims, grid, a_spec, b_spec, out_shape, out_spec, acc_shape, res=None, res_spec=None):
    nk = grid[2]
    dn = _DIMS[dims]

    def body(*refs):
        if res is None:
            a_ref, b_ref, o_ref, acc = refs
        else:
            a_ref, b_ref, r_ref, o_ref, acc = refs
        k = pl.program_id(2)

        @pl.when(k == 0)
        def _():
            acc[...] = jnp.zeros_like(acc)

        acc[...] += lax.dot_general(a_ref[...].astype(bf16), b_ref[...].astype(bf16), dn, preferred_element_type=f32)

        @pl.when(k == nk - 1)
        def _():
            r = acc[...]
            if res is not None:
                r = r + r_ref[...]
            o_ref[...] = r.astype(o_ref.dtype)

    ins = [a, b] + ([] if res is None else [res])
    specs = [a_spec, b_spec] + ([] if res is None else [res_spec])
    return pl.pallas_call(
        body, name=name, grid=grid, in_specs=specs, out_specs=out_spec, out_shape=out_shape,
        scratch_shapes=[pltpu.VMEM(acc_shape, f32)],
        compiler_params=_params("parallel", "parallel", "arbitrary"),
    )(*ins)


def _mm_rows(name, a, w, dims, out_dtype, tn, res=None, tk=None, tm_target=1032):
    T, K = a.shape
    N = w.shape[1] if dims == "nn" else w.shape[0]
    tm = _tile(T, tm_target)
    tk = K if tk is None else tk
    grid = (T // tm, N // tn, K // tk)
    a_spec = pl.BlockSpec((tm, tk), lambda i, j, k: (i, k))
    if dims == "nn":
        b_spec = pl.BlockSpec((tk, tn), lambda i, j, k: (k, j))
    else:
        b_spec = pl.BlockSpec((tn, tk), lambda i, j, k: (j, k))
    o_spec = pl.BlockSpec((tm, tn), lambda i, j, k: (i, j))
    return _mm(name, a, w, dims, grid, a_spec, b_spec, jax.ShapeDtypeStruct((T, N), out_dtype), o_spec, (tm, tn),
               res=res, res_spec=None if res is None else o_spec)


def _mm_fused(name, pairs, dims, extras, epilogue, outs, rows=(), tm_target=688):
    T, K = pairs[0][0].shape
    N = pairs[0][1].shape[1] if dims == "nn" else pairs[0][1].shape[0]
    tm = _tile(T, tm_target)
    tn = N
    grid = (T // tm, N // tn)
    npair, nex = len(pairs), len(extras) + len(rows)
    dn = _DIMS[dims]

    def body(*refs):
        ab = refs[:2 * npair]
        ex = refs[2 * npair:2 * npair + nex]
        o_refs = refs[2 * npair + nex:]
        accs = [lax.dot_general(ab[2 * q][...].astype(bf16), ab[2 * q + 1][...].astype(bf16), dn, preferred_element_type=f32)
                for q in range(npair)]
        vals = epilogue(accs, [e[...] for e in ex])
        for o_ref, v in zip(o_refs, vals):
            if isinstance(v, (list, tuple)):
                for s_, vs in enumerate(v):
                    o_ref[s_] = vs.astype(o_ref.dtype)
            else:
                o_ref[...] = v.astype(o_ref.dtype)

    ins, specs = [], []
    for a, w in pairs:
        ins += [a, w]
        specs.append(pl.BlockSpec((tm, K), lambda i, j: (i, 0)))
        specs.append(pl.BlockSpec((K, tn), lambda i, j: (0, j)) if dims == "nn" else pl.BlockSpec((tn, K), lambda i, j: (j, 0)))
    for arr, off in extras:
        ins.append(arr)
        specs.append(pl.BlockSpec((tm, tn), lambda i, j, off=off: (i, off + j)))
    for arr in rows:
        ins.append(arr)
        specs.append(pl.BlockSpec((1, tn), lambda i, j: (0, j)))
    shapes, ospecs = [], []
    for o in outs:
        if isinstance(o, tuple):
            dt, nseg, total, blk = o
            shapes.append(jax.ShapeDtypeStruct((total, T, N), dt))
            ospecs.append(pl.BlockSpec((nseg, tm, tn), lambda i, j, blk=blk: (blk, i, j)))
        else:
            shapes.append(jax.ShapeDtypeStruct((T, N), o))
            ospecs.append(pl.BlockSpec((tm, tn), lambda i, j: (i, j)))
    return pl.pallas_call(body, name=name, grid=grid, in_specs=specs, out_specs=ospecs, out_shape=shapes,
                          compiler_params=_params("parallel", "parallel"))(*ins)


def _glu_proj_fwd(ya0, w_glu):
    def epi(accs, tiles):
        return accs[0], tiles[0] * _sigmoid(accs[0])

    return _mm_fused("glu_proj", [(ya0, w_glu)], "nn", [(ya0, 0)], epi, [f32, bf16], tm_target=1032)


def _proj_merge_fwd(ya, yb, w_sp, w_hp, p):
    def epi(accs, tiles):
        return accs[0], accs[1], _sigmoid(tiles[0]) * accs[0] + _sigmoid(tiles[1]) * accs[1]

    return _mm_fused("proj_merge", [(ya, w_sp), (yb, w_hp)], "nn", [(p, SEG_GA), (p, SEG_GB)], epi, [f32, f32, bf16])


def _merge_bwd_fused(dh1, w_out, p, pa, pb):
    def epi(accs, tiles):
        d = accs[0]
        sa, sb = _sigmoid(tiles[0]), _sigmoid(tiles[1])
        return d * sa, d * sb, [d * tiles[2] * sa * (1.0 - sa), d * tiles[3] * sb * (1.0 - sb)]

    return _mm_fused("d_merged", [(dh1, w_out)], "nt", [(p, SEG_GA), (p, SEG_GB), (pa, 0), (pb, 0)], epi,
                     [bf16, bf16, (bf16, 2, N_SEG, SEG_GA // 2)], tm_target=344)


def _out_proj_norm(merged, w_out, h0, g):
    def epi(accs, tiles):
        h1 = tiles[0] + accs[0]
        r = lax.rsqrt(jnp.mean(h1 * h1, axis=-1, keepdims=True) + EPS)
        return h1, h1 * r * tiles[1]

    return _mm_fused("out_proj", [(merged, w_out)], "nn", [(h0, 0)], epi, [f32, bf16], rows=[g], tm_target=1032)


def _mm_rmsnorm_bwd(name, a, b, grid, a_spec, b_spec, x, g, dres):
    T, Dm = x.shape
    tm = T // grid[0]
    nk = grid[2]

    def body(a_ref, b_ref, x_ref, g_ref, dres_ref, dx_ref, dg_ref, acc):
        i, k = pl.program_id(0), pl.program_id(2)

        @pl.when(k == 0)
        def _():
            acc[...] = jnp.zeros_like(acc)

        @pl.when((i == 0) & (k == 0))
        def _():
            dg_ref[...] = jnp.zeros_like(dg_ref)

        acc[...] += lax.dot_general(a_ref[...].astype(bf16), b_ref[...].astype(bf16), _DIMS["nt"], preferred_element_type=f32)

        @pl.when(k == nk - 1)
        def _():
            xv = x_ref[...]
            r = lax.rsqrt(jnp.mean(xv * xv, axis=-1, keepdims=True) + EPS)
            xn = xv * r
            dzv = acc[...]
            dzg = dzv * g_ref[...]
            dx_ref[...] = dres_ref[...] + r * (dzg - xn * jnp.mean(dzg * xn, axis=-1, keepdims=True))
            dg_ref[...] += jnp.sum(dzv * xn, axis=0, keepdims=True)

    row = pl.BlockSpec((tm, Dm), lambda i, j, k: (i, 0))
    par = pl.BlockSpec((1, Dm), lambda i, j, k: (0, 0))
    return pl.pallas_call(
        body, name=name, grid=grid, in_specs=[a_spec, b_spec, row, par, row], out_specs=[row, par],
        out_shape=[jax.ShapeDtypeStruct((T, Dm), f32), jax.ShapeDtypeStruct((1, Dm), f32)],
        scratch_shapes=[pltpu.VMEM((tm, Dm), f32)],
        compiler_params=_params("arbitrary", "arbitrary", "arbitrary"),
    )(a, b, x, g, dres)


def _glu_bwd_fused(dpa, w_sp, ya0, gl):
    def epi(accs, tiles):
        d = accs[0]
        s = _sigmoid(tiles[1])
        return d * tiles[0] * s * (1.0 - s), d * s

    return _mm_fused("d_ya", [(dpa, w_sp)], "nt", [(ya0, 0), (gl, 0)], epi, [bf16, f32], tm_target=1032)


def _mm_wgrad(name, a, g, tn=None):
    T, K = a.shape
    N = g.shape[1]
    tk = _tile(T, 1376 if K <= D_MODEL else 688)
    tn = N if tn is None else tn
    grid = (1, N // tn, T // tk)
    a_spec = pl.BlockSpec((tk, K), lambda i, j, k: (k, 0))
    g_spec = pl.BlockSpec((tk, tn), lambda i, j, k: (k, j))
    o_spec = pl.BlockSpec((K, tn), lambda i, j, k: (0, j))
    return _mm(name, a, g, "tn", grid, a_spec, g_spec, jax.ShapeDtypeStruct((K, N), f32), o_spec, (K, tn))


def _rmsnorm_fwd(name, x, g):
    T, Dm = x.shape
    tr = _tile(T, 1376)

    def body(x_ref, g_ref, z_ref):
        xv = x_ref[...]
        r = lax.rsqrt(jnp.mean(xv * xv, axis=-1, keepdims=True) + EPS)
        z_ref[...] = (xv * r * g_ref[...]).astype(z_ref.dtype)

    return pl.pallas_call(
        body, name=name, grid=(T // tr,),
        in_specs=[pl.BlockSpec((tr, Dm), lambda i: (i, 0)), pl.BlockSpec((1, Dm), lambda i: (0, 0))],
        out_specs=pl.BlockSpec((tr, Dm), lambda i: (i, 0)),
        out_shape=jax.ShapeDtypeStruct((T, Dm), bf16), compiler_params=_params("parallel"),
    )(x, g)


def _final_loss(h2, tgt, g, L):
    T, Dm = h2.shape
    tr = _tile(L, 1032)
    per_seq = L // tr

    def body(h_ref, t_ref, g_ref, dh_ref, loss_ref, dg_ref):
        pos = (pl.program_id(0) % per_seq) * tr + lax.broadcasted_iota(jnp.int32, (tr, 1), 0)
        live = jnp.where(pos >= N_META, 1.0, 0.0)
        hv = h_ref[...]
        r = lax.rsqrt(jnp.mean(hv * hv, axis=-1, keepdims=True) + EPS)
        xn = hv * r
        gv = g_ref[...]
        err = (xn * gv - t_ref[...]) * live
        dy = err * (1.0 / Dm)
        dyg = dy * gv
        dh_ref[...] = r * (dyg - xn * jnp.mean(dyg * xn, axis=-1, keepdims=True))

        @pl.when(pl.program_id(0) == 0)
        def _():
            dg_ref[...] = jnp.zeros_like(dg_ref)
            loss_ref[...] = jnp.zeros_like(loss_ref)

        dg_ref[...] += jnp.sum(dy * xn, axis=0, keepdims=True)
        loss_ref[...] += jnp.sum(err * err) * (0.5 / Dm)

    row = pl.BlockSpec((tr, Dm), lambda i: (i, 0))
    par = pl.BlockSpec((1, Dm), lambda i: (0, 0))
    return pl.pallas_call(
        body, name="final_loss", grid=(T // tr,), in_specs=[row, row, par],
        out_specs=[row, pl.BlockSpec((1, LANES), lambda i: (0, 0)), par],
        out_shape=[jax.ShapeDtypeStruct((T, Dm), f32), jax.ShapeDtypeStruct((1, LANES), f32), jax.ShapeDtypeStruct((1, Dm), f32)],
        compiler_params=_params("arbitrary"),
    )(h2, tgt, g)


def _meta_grad(dh0_meta):
    B = dh0_meta.shape[0]

    def body(d_ref, o_ref):
        acc = d_ref[0]
        for b in range(1, B):
            acc = acc + d_ref[b]
        o_ref[...] = acc

    return pl.pallas_call(body, name="meta_grad", out_shape=jax.ShapeDtypeStruct(dh0_meta.shape[1:], f32))(dh0_meta)


def _shift_down(x, k, row):
    return jnp.where(row >= k, pltpu.roll(x, k, 0), 0.0)


def _conv_fwd(up, conv_w, conv_b, B, L):
    tc = 256
    nt = D_FF // tc

    def body(xa_ref, xb_ref, wa_ref, wb_ref, ba_ref, bb_ref, o_ref):
        head = 2 * SUBLANES
        row = lax.broadcasted_iota(jnp.int32, (head, tc), 0)

        def gated(conv):
            a = conv(xa_ref, wa_ref, ba_ref)
            b = conv(xb_ref, wb_ref, bb_ref)
            return (a * _sigmoid(a) * b).astype(o_ref.dtype)

        def conv_rolled(x_ref, w_ref, b_ref):
            x = x_ref[...]
            return b_ref[...] + w_ref[0:1, :] * pltpu.roll(x, 2, 0) + w_ref[1:2, :] * pltpu.roll(x, 1, 0) + w_ref[2:3, :] * x

        def conv_head(x_ref, w_ref, b_ref):
            x = x_ref[0:head, :]
            return (b_ref[...] + w_ref[0:1, :] * _shift_down(x, 2, row) + w_ref[1:2, :] * _shift_down(x, 1, row)
                    + w_ref[2:3, :] * x)

        o_ref[...] = gated(conv_rolled)
        o_ref[0:head, :] = gated(conv_head)

    return pl.pallas_call(
        body, name="conv_fwd", grid=(B, nt),
        in_specs=[pl.BlockSpec((L, tc), lambda b, j: (b, j)), pl.BlockSpec((L, tc), lambda b, j: (b, j + nt)),
                  pl.BlockSpec((3, tc), lambda b, j: (0, j)), pl.BlockSpec((3, tc), lambda b, j: (0, j + nt)),
                  pl.BlockSpec((1, tc), lambda b, j: (0, j)), pl.BlockSpec((1, tc), lambda b, j: (0, j + nt))],
        out_specs=pl.BlockSpec((L, tc), lambda b, j: (b, j)),
        out_shape=jax.ShapeDtypeStruct((B * L, D_FF), bf16), compiler_params=_params("parallel", "parallel"),
    )(up, up, conv_w, conv_w, conv_b, conv_b)


CONV_ROWS = 2 * SUBLANES


def _rows16(i):
    return pl.ds(pl.multiple_of(i * CONV_ROWS, CONV_ROWS), CONV_ROWS)


def _conv_taps(x_ref, i, row):
    x = x_ref[_rows16(i), :]
    live = jnp.where(i > 0, 1.0, 0.0)
    r0 = jnp.maximum(i * CONV_ROWS, 2)
    p1 = x_ref[pl.ds(r0 - 1, 1), :] * live
    p2 = x_ref[pl.ds(r0 - 2, 1), :] * live
    x1 = jnp.where(row == 0, p1, pltpu.roll(x, 1, 0))
    x2 = jnp.where(row == 0, p2, jnp.where(row == 1, p1, pltpu.roll(x, 2, 0)))
    return x, x1, x2


def _conv_bwd(up, dff, conv_w, conv_b, B, L):
    tc = 256
    nt = D_FF // tc
    n = L // CONV_ROWS

    def body(xa_ref, xb_ref, d_ref, wa_ref, wb_ref, ba_ref, bb_ref, dup_ref, dw_ref, ga_ref, gb_ref):
        row = lax.broadcasted_iota(jnp.int32, (CONV_ROWS, tc), 0)

        @pl.when(pl.program_id(1) == 0)
        def _():
            dw_ref[...] = jnp.zeros_like(dw_ref)

        zero_tail = jnp.zeros((CONV_ROWS, tc), f32)
        ga_ref[L:L + CONV_ROWS, :] = zero_tail
        gb_ref[L:L + CONV_ROWS, :] = zero_tail

        def fold(v):
            return v[0:SUBLANES, :] + v[SUBLANES:CONV_ROWS, :]

        def step(i, acc):
            taps_a = _conv_taps(xa_ref, i, row)
            taps_b = _conv_taps(xb_ref, i, row)
            a = ba_ref[...] + wa_ref[0:1, :] * taps_a[2] + wa_ref[1:2, :] * taps_a[1] + wa_ref[2:3, :] * taps_a[0]
            b = bb_ref[...] + wb_ref[0:1, :] * taps_b[2] + wb_ref[1:2, :] * taps_b[1] + wb_ref[2:3, :] * taps_b[0]
            s = _sigmoid(a)
            d = d_ref[_rows16(i), :]
            g_a = d * b * s * (1.0 + a * (1.0 - s))
            g_b = d * a * s
            ga_ref[_rows16(i), :] = g_a
            gb_ref[_rows16(i), :] = g_b
            new = []
            for g, (x, x1, x2) in ((g_a, taps_a), (g_b, taps_b)):
                new += [fold(g * x2), fold(g * x1), fold(g * x), fold(g)]
            return tuple(o + v for o, v in zip(acc, new))

        z = jnp.zeros((SUBLANES, tc), f32)
        acc = _repeat_loop(n, step, (z,) * 8)
        for h in range(2):
            for t in range(4):
                dw_ref[h, t:t + 1, :] += jnp.sum(acc[4 * h + t], axis=0, keepdims=True)

        def back(i, c):
            for h, (g_ref, w_ref) in enumerate(((ga_ref, wa_ref), (gb_ref, wb_ref))):
                g = g_ref[_rows16(i), :]
                n1 = g_ref[pl.ds(i * CONV_ROWS + CONV_ROWS, 1), :]
                n2 = g_ref[pl.ds(i * CONV_ROWS + CONV_ROWS + 1, 1), :]
                u1 = jnp.where(row == CONV_ROWS - 1, n1, pltpu.roll(g, CONV_ROWS - 1, 0))
                u2 = jnp.where(row == CONV_ROWS - 1, n2, jnp.where(row == CONV_ROWS - 2, n1, pltpu.roll(g, CONV_ROWS - 2, 0)))
                dup_ref[h, _rows16(i), :] = (w_ref[2:3, :] * g + w_ref[1:2, :] * u1 + w_ref[0:1, :] * u2).astype(dup_ref.dtype)
            return c

        _repeat_loop(n, back, 0)

    return pl.pallas_call(
        body, name="conv_bwd", grid=(nt, B),
        in_specs=[pl.BlockSpec((L, tc), lambda j, b: (b, j)), pl.BlockSpec((L, tc), lambda j, b: (b, j + nt)),
                  pl.BlockSpec((L, tc), lambda j, b: (b, j)),
                  pl.BlockSpec((3, tc), lambda j, b: (0, j)), pl.BlockSpec((3, tc), lambda j, b: (0, j + nt)),
                  pl.BlockSpec((1, tc), lambda j, b: (0, j)), pl.BlockSpec((1, tc), lambda j, b: (0, j + nt))],
        out_specs=[pl.BlockSpec((2, L, tc), lambda j, b: (0, b, j)), pl.BlockSpec((2, SUBLANES, tc), lambda j, b: (0, 0, j))],
        out_shape=[jax.ShapeDtypeStruct((2, B * L, D_FF), bf16), jax.ShapeDtypeStruct((2, SUBLANES, D_FF), f32)],
        scratch_shapes=[pltpu.VMEM((L + CONV_ROWS, tc), f32), pltpu.VMEM((L + CONV_ROWS, tc), f32)],
        compiler_params=_params("parallel", "arbitrary"),
    )(up, up, dff, conv_w, conv_w, conv_b, conv_b)


GELU_C = math.sqrt(2.0 / math.pi)
GELU_A = 0.044715


def _gelu(x):
    return 0.5 * x * (1.0 + jnp.tanh(GELU_C * (x + GELU_A * x * x * x)))


def _gelu_grad(x):
    t = jnp.tanh(GELU_C * (x + GELU_A * x * x * x))
    return 0.5 * (1.0 + t) + 0.5 * x * (1.0 - t * t) * GELU_C * (1.0 + 3.0 * GELU_A * x * x)


def _cmul_add(xr, xi, ar, ai, sr, si):
    return xr + ar * sr - ai * si, xi + ar * si + ai * sr


def _s5_project_in(u_ref, bs_ref, s_ref, L, rc):
    for r in range(0, L, rc):
        s_ref[r:r + rc, :] = jnp.dot(u_ref[r:r + rc, :].astype(bf16), bs_ref[...], preferred_element_type=f32)


def _rows8(i):
    return pl.ds(pl.multiple_of(i * SUBLANES, SUBLANES), SUBLANES)


def _repeat_loop(n, step, init):
    rep = max(u for u in (6, 4, 3, 2, 1) if n % u == 0)

    def body(t, carry):
        for u in range(rep):
            carry = step(t * rep + u, carry)
        return carry

    return lax.fori_loop(0, n // rep, body, init)


def _to_segments(src_ref, dst_ref, seg):
    def step(i, c):
        dst_ref[_rows8(i), :] = src_ref[pl.ds(i, SUBLANES, stride=seg), :]
        return c

    _repeat_loop(seg, step, 0)


def _from_segments(src_ref, dst_ref, seg):
    def step(i, c):
        dst_ref[pl.ds(i, SUBLANES, stride=seg), :] = src_ref[_rows8(i), :]
        return c

    _repeat_loop(seg, step, 0)


def _half_tiles(j, seg, reverse):
    h = seg // 2
    return (_rows8(seg - 1 - j), _rows8(h - 1 - j)) if reverse else (_rows8(j), _rows8(j + h))


def _seg_local_scan(s_ref, ar, ai, seg, reverse):
    ns = SLAB_NS

    def step(j, carry):
        tiles = _half_tiles(j, seg, reverse)
        loaded = [(s_ref[rows, 0:ns], s_ref[rows, ns:2 * ns]) for rows in tiles]
        out = []
        for (xr, xi), (cr, ci) in zip(loaded, (carry[0:2], carry[2:4])):
            out += list(_cmul_add(xr, xi, ar, ai, cr, ci))
        for rows, cr, ci in zip(tiles, out[0::2], out[1::2]):
            s_ref[rows, 0:ns] = cr
            s_ref[rows, ns:2 * ns] = ci
        return tuple(out)

    z = jnp.zeros((SUBLANES, ns), f32)
    return _repeat_loop(seg // 2, step, (z, z, z, z))


def _seg_boundaries(finals, ahr, ahi, reverse):
    fxr, fxi, fyr, fyi = finals
    row = lax.broadcasted_iota(jnp.int32, fxr.shape, 0)
    zero = jnp.zeros_like(fxr[0:1, :])
    xr, xi, yr, yi = (jnp.zeros_like(fxr) for _ in range(4))
    prev = None
    for r in (range(SUBLANES - 1, -1, -1) if reverse else range(SUBLANES)):
        if prev is None:
            nxr, nxi = zero, zero
        else:
            nxr, nxi = _cmul_add(fyr[prev:prev + 1, :], fyi[prev:prev + 1, :], ahr, ahi, nyr, nyi)
        nyr, nyi = _cmul_add(fxr[r:r + 1, :], fxi[r:r + 1, :], ahr, ahi, nxr, nxi)
        xr, xi = jnp.where(row == r, nxr, xr), jnp.where(row == r, nxi, xi)
        yr, yi = jnp.where(row == r, nyr, yr), jnp.where(row == r, nyi, yi)
        prev = r
    return (xr, xi), (yr, yi)


def _s5_states(u_ref, bs_ref, pw_ref, up_ref, s_ref, L, rc):
    seg = L // SUBLANES
    h = seg // 2
    ns = SLAB_NS
    _to_segments(u_ref, up_ref, seg)
    _s5_project_in(up_ref, bs_ref, s_ref, L, rc)
    ar, ai = pw_ref[0, 0:1, :], pw_ref[1, 0:1, :]
    finals = _seg_local_scan(s_ref, ar, ai, seg, False)
    enter = _seg_boundaries(finals, pw_ref[0, h - 1:h, :], pw_ref[1, h - 1:h, :], False)

    def fix(j, c):
        pr, pi = pw_ref[0, pl.ds(j, 1), :], pw_ref[1, pl.ds(j, 1), :]
        tiles = _half_tiles(j, seg, False)
        loaded = [(s_ref[rows, 0:ns], s_ref[rows, ns:2 * ns]) for rows in tiles]
        for rows, (xr, xi), (br, bi) in zip(tiles, loaded, enter):
            xr, xi = _cmul_add(xr, xi, pr, pi, br, bi)
            s_ref[rows, 0:ns] = xr
            s_ref[rows, ns:2 * ns] = xi
        return c

    _repeat_loop(h, fix, 0)


def _pw_spec(seg_rows, order):
    if order == "bs":
        return pl.BlockSpec((2, seg_rows, SLAB_NS), lambda b, s: (0, 0, s))
    return pl.BlockSpec((2, seg_rows, SLAB_NS), lambda s, b: (0, 0, s))


def _s5_fwd(p, bs, cs, pw, d_skip, B, L):
    rc = _tile(L, 344)
    seg = L // SUBLANES

    def body(u_ref, bs_ref, cs_ref, pw_ref, d_ref, y_ref, s_ref, up_ref, yp_ref):
        _s5_states(u_ref, bs_ref, pw_ref, up_ref, s_ref, L, rc)
        for r in range(0, L, rc):
            ypre = (jnp.dot(s_ref[r:r + rc, :].astype(bf16), cs_ref[...], preferred_element_type=f32)
                    + d_ref[...] * up_ref[r:r + rc, :])
            yp_ref[r:r + rc, :] = _gelu(ypre)
        _from_segments(yp_ref, y_ref, seg)

    ucol = SEG_U * (D_MODEL // SLAB_CH)
    return pl.pallas_call(
        body, name="s5_fwd", grid=(B, N_SLAB),
        in_specs=[pl.BlockSpec((L, SLAB_CH), lambda b, s: (b, ucol + s)),
                  pl.BlockSpec((None, SLAB_CH, 2 * SLAB_NS), lambda b, s: (s, 0, 0)),
                  pl.BlockSpec((None, 2 * SLAB_NS, SLAB_CH), lambda b, s: (s, 0, 0)),
                  _pw_spec(pw.shape[1], "bs"),
                  pl.BlockSpec((1, SLAB_CH), lambda b, s: (0, s))],
        out_specs=pl.BlockSpec((L, SLAB_CH), lambda b, s: (b, s)),
        out_shape=jax.ShapeDtypeStruct((B * L, D_MODEL), f32),
        scratch_shapes=[pltpu.VMEM((L, 2 * SLAB_NS), f32), pltpu.VMEM((L, SLAB_CH), f32), pltpu.VMEM((L, SLAB_CH), f32)],
        compiler_params=_params("parallel", "parallel"),
    )(p, bs, cs, pw, d_skip)


def _s5_bwd(p, dya0, dp, bs, cs, pw, d_skip, B, L, sums):
    rc = _tile(L, 688)
    ns = SLAB_NS
    seg = L // SUBLANES
    nx = len(sums)

    def body(u_ref, dy_ref, dp_in, bs_ref, cs_ref, pw_ref, d_ref, *rest):
        xin, (du_ref, dbs_ref, dcs_ref, da_ref, dd_ref), xout = rest[:nx], rest[nx:nx + 5], rest[nx + 5:2 * nx + 5]
        s_ref, lam_ref, up_ref, dyp_ref, nat_ref, send, recv = rest[2 * nx + 5:]
        del dp_in
        start, finish = _chip_exchange_steps(xin, xout, send, recv)

        @pl.when((pl.program_id(0) == 0) & (pl.program_id(1) == 0))
        def _():
            start()

        @pl.when(pl.program_id(1) == 0)
        def _():
            dbs_ref[...] = jnp.zeros_like(dbs_ref)
            dcs_ref[...] = jnp.zeros_like(dcs_ref)
            da_ref[...] = jnp.zeros_like(da_ref)
            dd_ref[...] = jnp.zeros_like(dd_ref)

        _s5_states(u_ref, bs_ref, pw_ref, up_ref, s_ref, L, rc)
        _to_segments(dy_ref, dyp_ref, seg)
        for r in range(0, L, rc):
            u = up_ref[r:r + rc, :]
            sb = s_ref[r:r + rc, :].astype(bf16)
            ypre = jnp.dot(sb, cs_ref[...], preferred_element_type=f32) + d_ref[...] * u
            dyp = dyp_ref[r:r + rc, :] * _gelu_grad(ypre)
            dyp_ref[r:r + rc, :] = dyp
            dd_ref[...] += jnp.sum(dyp * u, axis=0, keepdims=True)
            dypb = dyp.astype(bf16)
            dcs_ref[...] += lax.dot_general(sb, dypb, _DIMS["tn"], preferred_element_type=f32)
            lam_ref[r:r + rc, :] = lax.dot_general(dypb, cs_ref[...], _DIMS["nt"], preferred_element_type=f32)

        h = seg // 2
        ar, ai = pw_ref[0, 0:1, :], -pw_ref[1, 0:1, :]
        finals = _seg_local_scan(lam_ref, ar, ai, seg, True)
        enter = _seg_boundaries(finals, pw_ref[0, h - 1:h, :], -pw_ref[1, h - 1:h, :], True)

        def fix(j, acc):
            accr, acci = acc
            pr, pi = pw_ref[0, pl.ds(j, 1), :], -pw_ref[1, pl.ds(j, 1), :]
            tiles = _half_tiles(j, seg, True)
            loaded = [(lam_ref[rows, 0:ns], lam_ref[rows, ns:2 * ns]) for rows in tiles]
            for rows, (xr, xi), (br, bi), t in zip(tiles, loaded, enter, (seg - 1 - j, h - 1 - j)):
                xr, xi = _cmul_add(xr, xi, pr, pi, br, bi)
                lam_ref[rows, 0:ns] = xr
                lam_ref[rows, ns:2 * ns] = xi
                prev = _rows8(jnp.maximum(t - 1, 0))
                live = jnp.where(t > 0, 1.0, 0.0)
                spr = s_ref[prev, 0:ns] * live
                spi = s_ref[prev, ns:2 * ns] * live
                accr, acci = accr + xr * spr + xi * spi, acci + xi * spr - xr * spi
            return accr, acci

        z = jnp.zeros((SUBLANES, ns), f32)
        accr, acci = _repeat_loop(h, fix, (z, z))
        row = lax.broadcasted_iota(jnp.int32, (SUBLANES, ns), 0)
        last = _rows8(seg - 1)
        spr = jnp.where(row == 0, 0.0, pltpu.roll(s_ref[last, 0:ns], 1, 0))
        spi = jnp.where(row == 0, 0.0, pltpu.roll(s_ref[last, ns:2 * ns], 1, 0))
        xr, xi = lam_ref[0:SUBLANES, 0:ns], lam_ref[0:SUBLANES, ns:2 * ns]
        accr = accr + xr * spr + xi * spi
        acci = acci + xi * spr - xr * spi
        da_ref[0:1, :] += jnp.sum(accr, axis=0, keepdims=True)
        da_ref[1:2, :] += jnp.sum(acci, axis=0, keepdims=True)

        for r in range(0, L, rc):
            lamb = lam_ref[r:r + rc, :].astype(bf16)
            dbs_ref[...] += lax.dot_general(up_ref[r:r + rc, :].astype(bf16), lamb, _DIMS["tn"], preferred_element_type=f32)
            nat_ref[r:r + rc, :] = (lax.dot_general(lamb, bs_ref[...], _DIMS["nt"], preferred_element_type=f32)
                                    + d_ref[...] * dyp_ref[r:r + rc, :])
        _from_segments(nat_ref, up_ref, seg)
        du_ref[...] = up_ref[...].astype(du_ref.dtype)

        @pl.when((pl.program_id(0) == N_SLAB - 1) & (pl.program_id(1) == B - 1))
        def _():
            finish()

    ucol = SEG_U * (D_MODEL // SLAB_CH)
    T = B * L
    col = pltpu.VMEM((L, SLAB_CH), f32)
    res = pl.pallas_call(
        body, name="s5_bwd", grid=(N_SLAB, B),
        in_specs=[pl.BlockSpec((L, SLAB_CH), lambda s, b: (b, ucol + s)),
                  pl.BlockSpec((L, SLAB_CH), lambda s, b: (b, s)),
                  ANY,
                  pl.BlockSpec((None, SLAB_CH, 2 * SLAB_NS), lambda s, b: (s, 0, 0)),
                  pl.BlockSpec((None, 2 * SLAB_NS, SLAB_CH), lambda s, b: (s, 0, 0)),
                  _pw_spec(pw.shape[1], "sb"),
                  pl.BlockSpec((1, SLAB_CH), lambda s, b: (0, s))] + [ANY] * nx,
        out_specs=[pl.BlockSpec((None, L, SLAB_CH), lambda s, b: (SEG_U, b, s)),
                   pl.BlockSpec((None, SLAB_CH, 2 * SLAB_NS), lambda s, b: (s, 0, 0)),
                   pl.BlockSpec((None, 2 * SLAB_NS, SLAB_CH), lambda s, b: (s, 0, 0)),
                   pl.BlockSpec((None, 2, SLAB_NS), lambda s, b: (s, 0, 0)),
                   pl.BlockSpec((1, SLAB_CH), lambda s, b: (0, s))] + [ANY] * nx,
        out_shape=[jax.ShapeDtypeStruct((N_SEG, T, D_MODEL), bf16),
                   jax.ShapeDtypeStruct((N_SLAB, SLAB_CH, 2 * SLAB_NS), f32),
                   jax.ShapeDtypeStruct((N_SLAB, 2 * SLAB_NS, SLAB_CH), f32),
                   jax.ShapeDtypeStruct((N_SLAB, 2, SLAB_NS), f32),
                   jax.ShapeDtypeStruct((1, D_MODEL), f32)] + [jax.ShapeDtypeStruct(a.shape, a.dtype) for a in sums],
        scratch_shapes=[pltpu.VMEM((L, 2 * SLAB_NS), f32), pltpu.VMEM((L, 2 * SLAB_NS), f32), col, col, col]
        + _chip_exchange_sems(nx),
        input_output_aliases={2: 0},
        compiler_params=_params("arbitrary", "arbitrary"),
    )(p, dya0, dp, bs, cs, pw, d_skip, *sums)
    return res[:5], res[5:]


def _dotb(a, b, dims="nn"):
    return lax.dot_general(a.astype(bf16), b.astype(bf16), _DIMS[dims], preferred_element_type=f32)


def _tile_scan(x, reverse):
    n, w = x.shape
    v = x.reshape(n // SUBLANES, SUBLANES, w)
    row = lax.broadcasted_iota(jnp.int32, v.shape, 1)
    for k in (1, 2, 4):
        if reverse:
            v = v + jnp.where(row < SUBLANES - k, pltpu.roll(v, SUBLANES - k, 1), 0.0)
        else:
            v = v + jnp.where(row >= k, pltpu.roll(v, k, 1), 0.0)
    p = v.reshape(n // CHUNK, 2, SUBLANES, w)
    lo, hi = p[:, 0], p[:, 1]
    if reverse:
        lo = lo + hi[:, 0:1, :]
    else:
        hi = hi + lo[:, SUBLANES - 1:SUBLANES, :]
    return jnp.stack([lo, hi], axis=1).reshape(n, w)


def _chunk_cumsum(x):
    return _tile_scan(x, False)


def _chunk_rev_cumsum(x):
    return _tile_scan(x, True)


def _chunk_last(x):
    n, w = x.shape
    p = x.reshape(n // CHUNK, CHUNK, w)
    return jnp.broadcast_to(p[:, CHUNK - 1:CHUNK, :], p.shape).reshape(n, w)


def _hgrn_local(q, fl, lb):
    sg = _sigmoid(fl)
    f = lb + (1.0 - lb) * sg
    g = jnp.log(f)
    cum = _chunk_cumsum(g)
    rest = _chunk_last(cum) - cum
    e = jnp.exp(cum)
    em = jnp.exp(-cum)
    eo = jnp.exp(rest)
    k = 1.0 - f
    return sg, f, e, em, eo, q * e, k * em, k * eo, cum + rest


def _chunk_pos(n):
    return lax.broadcasted_iota(jnp.int32, (n, HEAD_DIM), 0) & (CHUNK - 1)


def _hgrn_block_rows(L):
    return _tile(L, 688, CHUNK)


def _hgrn_specs(L, order):
    hb = D_MODEL // HEAD_DIM

    def spec(seg):
        if order == "bh":
            return pl.BlockSpec((L, HEAD_DIM), lambda b, h: (b, seg * hb + h))
        return pl.BlockSpec((L, HEAD_DIM), lambda h, b: (b, seg * hb + h))

    return [spec(SEG_Q), spec(SEG_F), spec(SEG_I), spec(SEG_OG)]


PAIR = 2 * CHUNK
CHUNK_SHIFT = CHUNK.bit_length() - 1


def _pair_steps(L, rb):
    steps = []
    nch = rb // CHUNK
    for r in range(0, L, rb):
        steps += [(r + p * PAIR, PAIR) for p in range(nch // 2)]
        if nch % 2:
            steps.append((r + (nch - 1) * CHUNK, CHUNK))
    return steps


def _pair_flags(rb):
    ci = lax.broadcasted_iota(jnp.int32, (rb, HEAD_DIM), 0) >> CHUNK_SHIFT
    odd = (ci & 1) == 1
    has_next = jnp.logical_and(jnp.logical_not(odd), ci < rb // CHUNK - 1)
    return odd, has_next


def _pair_masks(rb):
    r = lax.broadcasted_iota(jnp.int32, (rb, rb), 0)
    c = lax.broadcasted_iota(jnp.int32, (rb, rb), 1)
    rc, cc = r >> CHUNK_SHIFT, c >> CHUNK_SHIFT
    same = (rc == cc) & (c <= r)
    prev = ((rc & 1) == 1) & (cc == rc - 1)
    return same, prev


def _hgrn_pair_local(q, fl, lb, odd, has_next):
    sg, f, e, em, eo, qt, kt, ko, cend = _hgrn_local(q, fl, lb)
    n = q.shape[0]
    a = jnp.where(odd, pltpu.roll(cend, CHUNK, 0), 0.0)
    z = jnp.where(has_next, pltpu.roll(cend, n - CHUNK, 0), 0.0)
    ea, ez = jnp.exp(a), jnp.exp(z)
    return dict(sg=sg, f=f, e=e, em=em, eo=eo, qt=qt, kt=kt, ko=ko, ea=ea, ez=ez, qs=qt * ea, ks=ko * ez,
                decp=jnp.exp(cend + a + z))


def _pair_scores(qt, kt, ko, same, prev):
    return (jnp.where(same, _dotb(qt, kt, "nt"), 0.0) + jnp.where(prev, _dotb(qt, ko, "nt"), 0.0)).astype(bf16)


def _hgrn_fwd(p, lb, norm_g, B, L):
    rb = _hgrn_block_rows(L)
    steps = _pair_steps(L, rb)
    blocks = [slice(r, r + rb) for r in range(0, L, rb)]

    def body(q_ref, f_ref, v_ref, og_ref, lb_ref, ng_ref, y_ref, qs_s, ks_s, vb_s, decp_s, o_s, o2_s, u_s, sb_s):
        lbv = lb_ref[...]
        ngv = ng_ref[...]
        same, prev = _pair_masks(rb)
        odd, has_next = _pair_flags(rb)

        for rows in blocks:
            t = _hgrn_pair_local(q_ref[rows, :], f_ref[rows, :], lbv, odd, has_next)
            vb = v_ref[rows, :].astype(bf16)
            o_s[rows, :] = _dotb(_pair_scores(t["qt"], t["kt"], t["ko"], same, prev), vb)
            qs_s[rows, :] = t["qs"].astype(bf16)
            ks_s[rows, :] = t["ks"].astype(bf16)
            vb_s[rows, :] = vb
            decp_s[rows, :] = t["decp"]

        for n, (r0, nr) in enumerate(steps):
            u_s[n] = _dotb(vb_s[r0:r0 + nr, :], ks_s[r0:r0 + nr, :], "tn")
        st = jnp.zeros((HEAD_DIM, HEAD_DIM), f32)
        for n, (r0, nr) in enumerate(steps):
            sb_s[n] = st.astype(bf16)
            st = st * decp_s[r0:r0 + 1, :] + u_s[n]
        for n, (r0, nr) in enumerate(steps):
            o2_s[r0:r0 + nr, :] = _dotb(qs_s[r0:r0 + nr, :], sb_s[n], "nt")

        for rows in blocks:
            o = o_s[rows, :] + o2_s[rows, :]
            og = og_ref[rows, :]
            on = o * lax.rsqrt(jnp.mean(o * o, axis=-1, keepdims=True) + EPS) * ngv
            y_ref[rows, :] = (on * og * _sigmoid(og)).astype(y_ref.dtype)

    sb = pltpu.VMEM((L, HEAD_DIM), bf16)
    sf = pltpu.VMEM((L, HEAD_DIM), f32)
    return pl.pallas_call(
        body, name="hgrn_fwd", grid=(B, HEADS),
        in_specs=_hgrn_specs(L, "bh") + [pl.BlockSpec((1, HEAD_DIM), lambda b, h: (0, h)),
                                          pl.BlockSpec((1, HEAD_DIM), lambda b, h: (0, 0))],
        out_specs=pl.BlockSpec((L, HEAD_DIM), lambda b, h: (b, h)),
        out_shape=jax.ShapeDtypeStruct((B * L, D_MODEL), bf16),
        scratch_shapes=[sb, sb, sb, sf, sf, sf, pltpu.VMEM((len(steps), HEAD_DIM, HEAD_DIM), f32),
                        pltpu.VMEM((len(steps), HEAD_DIM, HEAD_DIM), bf16)],
        compiler_params=_params("parallel", "parallel"),
    )(p, p, p, p, lb, norm_g)


def _hgrn_bwd(p, dyb, dp, lb, norm_g, B, L, parts):
    rb = _hgrn_block_rows(L)
    steps = _pair_steps(L, rb)
    blocks = [slice(r, r + rb) for r in range(0, L, rb)]

    nx = len(parts)

    def body(q_ref, f_ref, v_ref, og_ref, dy_ref, dp_in, lb_ref, ng_ref, *rest):
        xin, (dseg_ref, dlb_ref, dng_ref), xout = rest[:nx], rest[nx:nx + 3], rest[nx + 3:2 * nx + 3]
        (st_ref, u_s, dsb_s, qt_s, kt_s, ko_s, qs_s, ks_s, vb_s, do_s, decp_s, o_s, o2_s, dqt_s, dkt_s, dko_s, dv_s,
         dv2_s, dqs_s, dks_s, ddecp_s, send, recv) = rest[2 * nx + 3:]
        del dp_in
        start, finish = _sibling_halves_steps(xin, xout, send, recv)

        @pl.when((pl.program_id(0) == 0) & (pl.program_id(1) == 0))
        def _():
            start()

        lbv = lb_ref[...]
        ngv = ng_ref[...]
        same, prev = _pair_masks(rb)
        odd, has_next = _pair_flags(rb)
        pos = _chunk_pos(rb)

        @pl.when(pl.program_id(1) == 0)
        def _():
            dlb_ref[...] = jnp.zeros_like(dlb_ref)

        @pl.when((pl.program_id(0) == 0) & (pl.program_id(1) == 0))
        def _():
            dng_ref[...] = jnp.zeros_like(dng_ref)

        def scores(rows):
            return _pair_scores(qt_s[rows, :], kt_s[rows, :], ko_s[rows, :], same, prev)

        for rows in blocks:
            t = _hgrn_pair_local(q_ref[rows, :], f_ref[rows, :], lbv, odd, has_next)
            for dst, key in ((qt_s, "qt"), (kt_s, "kt"), (ko_s, "ko"), (qs_s, "qs"), (ks_s, "ks")):
                dst[rows, :] = t[key].astype(bf16)
            vb_s[rows, :] = v_ref[rows, :].astype(bf16)
            decp_s[rows, :] = t["decp"]
            o_s[rows, :] = _dotb(scores(rows), vb_s[rows, :])

        for n, (r0, nr) in enumerate(steps):
            u_s[n] = _dotb(vb_s[r0:r0 + nr, :], ks_s[r0:r0 + nr, :], "tn")
        st = jnp.zeros((HEAD_DIM, HEAD_DIM), f32)
        for n, (r0, nr) in enumerate(steps):
            st_ref[n] = st
            st = st * decp_s[r0:r0 + 1, :] + u_s[n]
        for n, (r0, nr) in enumerate(steps):
            o2_s[r0:r0 + nr, :] = _dotb(qs_s[r0:r0 + nr, :], st_ref[n], "nt")

        dng = jnp.zeros((1, HEAD_DIM), f32)
        for rows in blocks:
            o = o_s[rows, :] + o2_s[rows, :]
            og = og_ref[rows, :]
            dy = dy_ref[rows, :]
            rs = lax.rsqrt(jnp.mean(o * o, axis=-1, keepdims=True) + EPS)
            xn = o * rs
            so = _sigmoid(og)
            dseg_ref[SEG_OG, rows, :] = (dy * xn * ngv * so * (1.0 + og * (1.0 - so))).astype(dseg_ref.dtype)
            don = dy * og * so
            dng = dng + jnp.sum(don * xn, axis=0, keepdims=True)
            dxo = don * ngv
            do = (rs * (dxo - xn * jnp.mean(dxo * xn, axis=-1, keepdims=True))).astype(bf16)
            do_s[rows, :] = do
            dpf = _dotb(do, vb_s[rows, :], "nt")
            dp1 = jnp.where(same, dpf, 0.0).astype(bf16)
            dp2 = jnp.where(prev, dpf, 0.0).astype(bf16)
            dqt_s[rows, :] = _dotb(dp1, kt_s[rows, :]) + _dotb(dp2, ko_s[rows, :])
            dkt_s[rows, :] = _dotb(dp1, qt_s[rows, :], "tn")
            dko_s[rows, :] = _dotb(dp2, qt_s[rows, :], "tn")
            dv_s[rows, :] = _dotb(scores(rows), do, "tn")
        dng_ref[...] += dng

        for n, (r0, nr) in enumerate(steps):
            u_s[n] = _dotb(do_s[r0:r0 + nr, :], qs_s[r0:r0 + nr, :], "tn")
        dst = jnp.zeros((HEAD_DIM, HEAD_DIM), f32)
        for n, (r0, nr) in reversed(list(enumerate(steps))):
            dsb_s[n] = dst.astype(bf16)
            ddecp_s[r0:r0 + nr, :] = jnp.broadcast_to(jnp.sum(dst * st_ref[n], axis=0, keepdims=True), (nr, HEAD_DIM))
            dst = dst * decp_s[r0:r0 + 1, :] + u_s[n]
        for n, (r0, nr) in enumerate(steps):
            rows = slice(r0, r0 + nr)
            dqs_s[rows, :] = _dotb(do_s[rows, :], st_ref[n])
            dv2_s[rows, :] = _dotb(ks_s[rows, :], dsb_s[n], "nt")
            dks_s[rows, :] = _dotb(vb_s[rows, :], dsb_s[n])

        def chunk_sum(x):
            return _chunk_last(_chunk_cumsum(x))

        dlb = jnp.zeros((1, HEAD_DIM), f32)
        for rows in blocks:
            t = _hgrn_pair_local(q_ref[rows, :], f_ref[rows, :], lbv, odd, has_next)
            dqs, dks = dqs_s[rows, :], dks_s[rows, :]
            dqt = dqt_s[rows, :] + dqs * t["ea"]
            dko = dko_s[rows, :] + dks * t["ez"]
            dkt = dkt_s[rows, :]
            dko_ko = dko * t["ko"]
            dcum = dqt * t["qt"] - dkt * t["kt"] - dko_ko
            from_next = pltpu.roll(chunk_sum(jnp.where(odd, dqs * t["qs"], 0.0)), rb - CHUNK, 0)
            from_prev = pltpu.roll(chunk_sum(jnp.where(has_next, dks * t["ks"], 0.0)), CHUNK, 0)
            d_end = (chunk_sum(dko_ko) + jnp.where(has_next, from_next, 0.0) + jnp.where(odd, from_prev, 0.0)
                     + ddecp_s[rows, :] * t["decp"])
            dcum = dcum + jnp.where(pos == CHUNK - 1, d_end, 0.0)
            df = _chunk_rev_cumsum(dcum) / t["f"] - (dkt * t["em"] + dko * t["eo"])
            dlb = dlb + jnp.sum(df * (1.0 - t["sg"]), axis=0, keepdims=True)
            dseg_ref[SEG_Q, rows, :] = (dqt * t["e"]).astype(dseg_ref.dtype)
            dseg_ref[SEG_F, rows, :] = (df * (1.0 - lbv) * t["sg"] * (1.0 - t["sg"])).astype(dseg_ref.dtype)
            dseg_ref[SEG_I, rows, :] = (dv_s[rows, :] + dv2_s[rows, :]).astype(dseg_ref.dtype)
        dlb_ref[...] += dlb

        @pl.when((pl.program_id(0) == HEADS - 1) & (pl.program_id(1) == B - 1))
        def _():
            finish()

    T = B * L
    ns = len(steps)
    sb = pltpu.VMEM((L, HEAD_DIM), bf16)
    sf = pltpu.VMEM((L, HEAD_DIM), f32)
    res = pl.pallas_call(
        body, name="hgrn_bwd", grid=(HEADS, B),
        in_specs=_hgrn_specs(L, "hb") + [pl.BlockSpec((L, HEAD_DIM), lambda h, b: (b, h)), ANY,
                                          pl.BlockSpec((1, HEAD_DIM), lambda h, b: (0, h)),
                                          pl.BlockSpec((1, HEAD_DIM), lambda h, b: (0, 0))] + [ANY] * nx,
        out_specs=[pl.BlockSpec((4, L, HEAD_DIM), lambda h, b: (0, b, h)),
                   pl.BlockSpec((1, HEAD_DIM), lambda h, b: (0, h)),
                   pl.BlockSpec((1, HEAD_DIM), lambda h, b: (0, 0))] + [ANY] * nx,
        out_shape=[jax.ShapeDtypeStruct((N_SEG, T, D_MODEL), bf16), jax.ShapeDtypeStruct((1, D_MODEL), f32),
                   jax.ShapeDtypeStruct((1, HEAD_DIM), f32)] + _sibling_halves_shapes(parts),
        scratch_shapes=[pltpu.VMEM((ns, HEAD_DIM, HEAD_DIM), f32), pltpu.VMEM((ns, HEAD_DIM, HEAD_DIM), f32),
                        pltpu.VMEM((ns, HEAD_DIM, HEAD_DIM), bf16)] + [sb] * 7 + [sf] * 11 + _sibling_halves_sems(nx),
        input_output_aliases={5: 0},
        compiler_params=_params("arbitrary", "arbitrary"),
    )(p, p, p, p, dyb, dp, lb, norm_g, *parts)
    return res[:3], res[3:]


def _dz1_norm(dp, w_in_phys, h0, g, dh1):
    _, T, Dm = dp.shape
    tm = _tile(T, 1032)
    return _mm_rmsnorm_bwd("dz1", dp, w_in_phys, (T // tm, 1, N_SEG),
                           pl.BlockSpec((None, tm, Dm), lambda i, j, k: (k, i, 0)),
                           pl.BlockSpec((Dm, Dm), lambda i, j, k: (0, k)), h0, g, dh1)


def _dz2_norm(dup, w_up, h1, g, dh2):
    _, T, _ = dup.shape
    tm = _tile(T, 1032)
    tk = D_FF // 2
    return _mm_rmsnorm_bwd("dz2", dup, w_up, (T // tm, 1, 4),
                           pl.BlockSpec((None, tm, tk), lambda i, j, k: (k // 2, i, k % 2)),
                           pl.BlockSpec((D_MODEL, tk), lambda i, j, k: (0, k)), h1, g, dh2)


def _dw_in(z1, dp):
    _, T, Dm = dp.shape
    tk = _tile(T, 2064)
    return _mm("dw_in", z1, dp, "tn", (1, N_SEG, T // tk),
               pl.BlockSpec((tk, Dm), lambda i, j, k: (k, 0)),
               pl.BlockSpec((None, tk, Dm), lambda i, j, k: (j, k, 0)),
               jax.ShapeDtypeStruct((N_SEG, Dm, Dm), f32),
               pl.BlockSpec((None, Dm, Dm), lambda i, j, k: (j, 0, 0)), (Dm, Dm))


def _dw_up(z2, dup):
    _, T, _ = dup.shape
    tn = D_FF // 2
    tk = _tile(T, 1376)
    return _mm("dw_up", z2, dup, "tn", (1, N_CHIPS, T // tk),
               pl.BlockSpec((tk, D_MODEL), lambda i, j, k: (k, 0)),
               pl.BlockSpec((None, tk, tn), lambda i, j, k: (j // 2, k, j % 2)),
               jax.ShapeDtypeStruct((N_CHIPS, D_MODEL, tn), f32),
               pl.BlockSpec((None, D_MODEL, tn), lambda i, j, k: (j, 0, 0)), (D_MODEL, tn))


def _place():
    x, y, c = lax.axis_index("x"), lax.axis_index("y"), lax.axis_index("c")
    chips = [(1 - x, y), (x, 1 - y), (1 - x, 1 - y)]
    return x, y, c, chips


def _allgather_chips(arrs):
    n = len(arrs)

    def body(*refs):
        ins, outs = refs[:n], refs[n:2 * n]
        send, recv, local = refs[2 * n:]
        x, y, c, chips = _place()
        me = 2 * x + y

        def copy(a, k, slot):
            px, py = chips[k]
            return pltpu.make_async_remote_copy(src_ref=ins[a], dst_ref=outs[a].at[slot], send_sem=send.at[3 * a + k],
                                                recv_sem=recv.at[3 * a + k], device_id=(px, py, c), device_id_type=MESH)

        for a in range(n):
            pltpu.make_async_copy(ins[a], outs[a].at[me], local.at[a]).start()
            for k in range(3):
                copy(a, k, me).start()
        for a in range(n):
            for k, (px, py) in enumerate(chips):
                copy(a, k, 2 * px + py).wait_recv()
        for a in range(n):
            pltpu.make_async_copy(ins[a], outs[a].at[me], local.at[a]).wait()
            for k in range(3):
                copy(a, k, me).wait_send()

    return pl.pallas_call(
        body, name="allgather_chips", in_specs=[ANY] * n, out_specs=[ANY] * n,
        out_shape=[jax.ShapeDtypeStruct((N_CHIPS,) + a.shape, a.dtype) for a in arrs],
        scratch_shapes=[pltpu.SemaphoreType.DMA((3 * n,)), pltpu.SemaphoreType.DMA((3 * n,)), pltpu.SemaphoreType.DMA((n,))],
    )(*arrs)


def _allgather_split(arrs):
    n = len(arrs)

    def body(*refs):
        start, finish = _gather_split_steps(refs[:n], refs[n:2 * n], *refs[2 * n:])
        start()
        finish()

    return pl.pallas_call(
        body, name="allgather_split", in_specs=[ANY] * n, out_specs=[ANY] * n,
        out_shape=[jax.ShapeDtypeStruct((N_CHIPS,) + a.shape, a.dtype) for a in arrs],
        scratch_shapes=_gather_split_sems(n),
    )(*arrs)


def _gather_split_sems(n):
    return [pltpu.SemaphoreType.DMA((3 * n,)) for _ in range(4)]


def _gather_split_steps(ins, outs, send, recv, fsend, frecv):
    n = len(ins)

    def place():
        x, y, c, chips = _place()
        return x, y, c, chips, 2 * x + y

    def half(a, core):
        rh = ins[a].shape[0] // 2
        return pl.ds(core * rh, rh)

    def copy(a, k, slot):
        x, y, c, chips, _ = place()
        px, py = chips[k]
        return pltpu.make_async_remote_copy(src_ref=ins[a].at[half(a, c), :], dst_ref=outs[a].at[slot, half(a, c), :],
                                            send_sem=send.at[3 * a + k], recv_sem=recv.at[3 * a + k],
                                            device_id=(px, py, c), device_id_type=MESH)

    def forward(a, k, core):
        x, y, c, chips, _ = place()
        px, py = chips[k]
        rows = outs[a].at[2 * px + py, half(a, core), :]
        return pltpu.make_async_remote_copy(src_ref=rows, dst_ref=rows, send_sem=fsend.at[3 * a + k],
                                            recv_sem=frecv.at[3 * a + k], device_id=(x, y, 1 - c), device_id_type=MESH)

    def start():
        me = place()[4]
        for a in range(n):
            for k in range(3):
                copy(a, k, me).start()

    def finish():
        x, y, c, chips, me = place()
        for a in range(n):
            for k, (px, py) in enumerate(chips):
                copy(a, k, 2 * px + py).wait_recv()
                forward(a, k, c).start()
        for a in range(n):
            for k in range(3):
                forward(a, k, 1 - c).wait_recv()
        for a in range(n):
            for k in range(3):
                copy(a, k, me).wait_send()
                forward(a, k, c).wait_send()

    return start, finish


def _in_proj_gather(z1, w_in, shards):
    n = len(shards)
    T, K = z1.shape
    N = w_in.shape[1]
    tm = _tile(T, 2064)
    tn = 1024
    grid = (T // tm, N // tn)

    def body(a_ref, b_ref, *rest):
        ins, o_ref, outs, sems = rest[:n], rest[n], rest[n + 1:2 * n + 1], rest[2 * n + 1:]
        start, finish = _gather_split_steps(ins, outs, *sems)
        i, j = pl.program_id(0), pl.program_id(1)

        @pl.when((i == 0) & (j == 0))
        def _():
            start()

        o_ref[...] = jnp.dot(a_ref[...], b_ref[...], preferred_element_type=f32)

        @pl.when((i == grid[0] - 1) & (j == grid[1] - 1))
        def _():
            finish()

    res = pl.pallas_call(
        body, name="in_proj", grid=grid,
        in_specs=[pl.BlockSpec((tm, K), lambda i, j: (i, 0)), pl.BlockSpec((K, tn), lambda i, j: (0, j))] + [ANY] * n,
        out_specs=[pl.BlockSpec((tm, tn), lambda i, j: (i, j))] + [ANY] * n,
        out_shape=[jax.ShapeDtypeStruct((T, N), f32)] + [jax.ShapeDtypeStruct((N_CHIPS,) + a.shape, a.dtype) for a in shards],
        scratch_shapes=_gather_split_sems(n),
        compiler_params=_params("arbitrary", "arbitrary"),
    )(z1, w_in, *shards)
    return res[0], res[1:]


def _sibling_halves(parts, name="sibling_halves"):
    n = len(parts)

    def body(*refs):
        start, finish = _sibling_halves_steps(refs[:n], refs[n:2 * n], *refs[2 * n:])
        start()
        finish()

    return pl.pallas_call(
        body, name=name, in_specs=[ANY] * n, out_specs=[ANY] * n,
        out_shape=_sibling_halves_shapes(parts), scratch_shapes=_sibling_halves_sems(n),
    )(*parts)


def _sibling_halves_shapes(parts):
    return [jax.ShapeDtypeStruct((a.shape[0], a.shape[1] // 2, a.shape[2]), a.dtype) for a in parts]


def _sibling_halves_sems(n):
    return [pltpu.SemaphoreType.DMA((n,)), pltpu.SemaphoreType.DMA((n,))]


def _sibling_halves_steps(ins, outs, send, recv):
    n = len(ins)

    def copy(a):
        x, y, c, _ = _place()
        rh = ins[a].shape[1] // 2
        return pltpu.make_async_remote_copy(src_ref=ins[a].at[:, pl.ds((1 - c) * rh, rh), :], dst_ref=outs[a],
                                            send_sem=send.at[a], recv_sem=recv.at[a], device_id=(x, y, 1 - c),
                                            device_id_type=MESH)

    def start():
        for a in range(n):
            copy(a).start()

    def finish():
        for a in range(n):
            copy(a).wait_recv()
        for a in range(n):
            copy(a).wait_send()

    return start, finish


def _add_own_half(name, part, got, core):
    nchip, R, C = part.shape
    rh = R // 2
    tr = _tile(rh, 512, 2 * SUBLANES)
    nt = rh // tr

    def body(core_ref, a_ref, b_ref, o_ref):
        del core_ref
        o_ref[...] = (a_ref[...] + b_ref[...]).astype(o_ref.dtype)

    return pl.pallas_call(
        body, name=name,
        grid_spec=pltpu.PrefetchScalarGridSpec(
            num_scalar_prefetch=1, grid=(nchip, nt),
            in_specs=[pl.BlockSpec((None, tr, C), lambda j, i, core_ref: (j, core_ref[0] * nt + i, 0)),
                      pl.BlockSpec((None, tr, C), lambda j, i, core_ref: (j, i, 0))],
            out_specs=pl.BlockSpec((None, tr, C), lambda j, i, core_ref: (j, i, 0))),
        out_shape=jax.ShapeDtypeStruct((nchip, rh, C), bf16), compiler_params=_params("parallel", "parallel"),
    )(core, part, got)


def _add_own_half_w_in(part, got, core):
    _, R, C = part.shape
    rh = R // 2
    tr = _tile(rh, 512, 2 * SUBLANES)
    nt = rh // tr
    tn = 256
    per_seg = C // tn
    per_chip = IN_COLS // N_CHIPS // tn

    def src(j):
        return ((j // per_seg + N_SEG - 1) % N_SEG, j % per_seg)

    def body(core_ref, a_ref, b_ref, o_ref):
        del core_ref
        o_ref[...] = (a_ref[...] + b_ref[...]).astype(o_ref.dtype)

    return pl.pallas_call(
        body, name="add_half_w_in",
        grid_spec=pltpu.PrefetchScalarGridSpec(
            num_scalar_prefetch=1, grid=(IN_COLS // tn, nt),
            in_specs=[pl.BlockSpec((None, tr, tn), lambda j, i, core_ref: (src(j)[0], core_ref[0] * nt + i, src(j)[1])),
                      pl.BlockSpec((None, tr, tn), lambda j, i, core_ref: (src(j)[0], i, src(j)[1]))],
            out_specs=pl.BlockSpec((None, tr, tn), lambda j, i, core_ref: (j // per_chip, i, j % per_chip))),
        out_shape=jax.ShapeDtypeStruct((N_CHIPS, rh, IN_COLS // N_CHIPS), bf16), compiler_params=_params("parallel", "parallel"),
    )(core, part, got)


def _chip_exchange(sums):
    n = len(sums)

    def body(*refs):
        start, finish = _chip_exchange_steps(refs[:n], refs[n:2 * n], *refs[2 * n:])
        start()
        finish()

    return pl.pallas_call(
        body, name="chip_exchange", in_specs=[ANY] * n, out_specs=[ANY] * n,
        out_shape=[jax.ShapeDtypeStruct(a.shape, a.dtype) for a in sums],
        scratch_shapes=_chip_exchange_sems(n),
    )(*sums)


def _chip_exchange_sems(n):
    return [pltpu.SemaphoreType.DMA((3 * n,)), pltpu.SemaphoreType.DMA((3 * n,))]


def _chip_exchange_steps(ins, outs, send, recv):
    n = len(ins)

    def copy(a, k, own_slot):
        x, y, c, chips = _place()
        px, py = chips[k]
        slot = 2 * x + y if own_slot else 2 * px + py
        return pltpu.make_async_remote_copy(src_ref=ins[a].at[2 * px + py], dst_ref=outs[a].at[slot], send_sem=send.at[3 * a + k],
                                            recv_sem=recv.at[3 * a + k], device_id=(px, py, c), device_id_type=MESH)

    def start():
        for a in range(n):
            for k in range(3):
                copy(a, k, True).start()

    def finish():
        for a in range(n):
            for k in range(3):
                copy(a, k, False).wait_recv()
        for a in range(n):
            for k in range(3):
                copy(a, k, True).wait_send()

    return start, finish


def _sum_chips(name, slots, sums, where):
    nchip, rh, C = slots.shape
    tr = _tile(rh, 512, 2 * SUBLANES)
    nt = rh // tr

    def body(where_ref, own_ref, s1_ref, s2_ref, s3_ref, o_ref):
        me = where_ref[0]
        by_dist = [r[...].astype(f32) for r in (own_ref, s1_ref, s2_ref, s3_ref)]
        acc = None
        for j in range(nchip):
            d = me ^ j
            term = jnp.where(d == 0, by_dist[0], jnp.where(d == 1, by_dist[1], jnp.where(d == 2, by_dist[2], by_dist[3])))
            acc = term if acc is None else acc + term
        o_ref[...] = acc

    def other(d):
        return pl.BlockSpec((None, tr, C), lambda i, w: (w[0] ^ d, i, 0))

    return pl.pallas_call(
        body, name=name,
        grid_spec=pltpu.PrefetchScalarGridSpec(
            num_scalar_prefetch=1, grid=(nt,),
            in_specs=[other(0), other(1), other(2), other(3)],
            out_specs=pl.BlockSpec((tr, C), lambda i, w: (w[1] * nt + i, 0))),
        out_shape=jax.ShapeDtypeStruct((2 * rh, C), f32), compiler_params=_params("parallel"),
    )(where, sums, slots, slots, slots)


def _sum_slots(name, slots):
    ns, R, C = slots.shape
    tr = _tile(R, 256)

    def body(s_ref, o_ref):
        acc = s_ref[0]
        for j in range(1, ns):
            acc = acc + s_ref[j]
        o_ref[...] = acc

    return pl.pallas_call(
        body, name=name, grid=(R // tr,), in_specs=[pl.BlockSpec((ns, tr, C), lambda i: (0, i, 0))],
        out_specs=pl.BlockSpec((tr, C), lambda i: (i, 0)), out_shape=jax.ShapeDtypeStruct((R, C), f32),
        compiler_params=_params("parallel"),
    )(slots)


def _sibling_join(fulls):
    n = len(fulls)

    def body(*refs):
        ins, outs = refs[:n], refs[n:2 * n]
        send, recv = refs[2 * n:]
        x, y, c, _ = _place()

        def copy(a, core):
            rh = ins[a].shape[0] // 2
            rows = pl.ds(core * rh, rh)
            return pltpu.make_async_remote_copy(src_ref=ins[a].at[rows, :], dst_ref=outs[a].at[rows, :], send_sem=send.at[a],
                                                recv_sem=recv.at[a], device_id=(x, y, 1 - c), device_id_type=MESH)

        for a in range(n):
            copy(a, c).start()
        for a in range(n):
            copy(a, 1 - c).wait_recv()
        for a in range(n):
            copy(a, c).wait_send()

    return pl.pallas_call(
        body, name="sibling_join", in_specs=[ANY] * n, out_specs=[ANY] * n,
        out_shape=[jax.ShapeDtypeStruct(a.shape, a.dtype) for a in fulls],
        scratch_shapes=[pltpu.SemaphoreType.DMA((n,)), pltpu.SemaphoreType.DMA((n,))],
        input_output_aliases={a: a for a in range(n)},
    )(*fulls)


def _allgather_devices(v):
    def body(v_ref, out_ref, send, recv):
        x, y, c, chips = _place()
        me, sibling = (x, y, c), (x, y, 1 - c)

        def slot(px, py, pc):
            return out_ref.at[4 * px + 2 * py + pc]

        def copy(k, block, to, src=None):
            return pltpu.make_async_remote_copy(src_ref=slot(*block) if src is None else src, dst_ref=slot(*block),
                                                send_sem=send.at[k], recv_sem=recv.at[k], device_id=to, device_id_type=MESH)

        first = [copy(0, me, sibling, src=v_ref)] + [copy(1 + j, me, (*chip, c), src=v_ref) for j, chip in enumerate(chips)]
        for cp in first:
            cp.start()
        passed = [copy(4 + j, (*chip, c), sibling) for j, chip in enumerate(chips)]
        for j, chip in enumerate(chips):
            copy(1 + j, (*chip, c), me).wait_recv()
            passed[j].start()
        copy(0, sibling, me).wait_recv()
        for j, chip in enumerate(chips):
            copy(4 + j, (*chip, 1 - c), me).wait_recv()
        for cp in first + passed:
            cp.wait_send()

    return pl.pallas_call(
        body, name="allgather_devices", in_specs=[ANY], out_specs=ANY,
        out_shape=jax.ShapeDtypeStruct((N_DEV,) + v.shape, v.dtype),
        scratch_shapes=[pltpu.SemaphoreType.DMA((N_DEV - 1,)), pltpu.SemaphoreType.DMA((N_DEV - 1,))],
    )(v)


def _adamw(name, w, g, m, v):
    R, C = w.shape
    tr = _tile(R, 256)
    c1 = 1.0 / (1.0 - ADAM_B1 ** ADAM_STEP)
    c2 = 1.0 / (1.0 - ADAM_B2 ** ADAM_STEP)

    def body(w_ref, g_ref, m_ref, v_ref, d_ref, nm_ref, nv_ref):
        gv = g_ref[...]
        nm = ADAM_B1 * m_ref[...] + (1.0 - ADAM_B1) * gv
        nv = ADAM_B2 * v_ref[...] + (1.0 - ADAM_B2) * gv * gv
        d_ref[...] = -ADAM_LR * ((nm * c1) / (jnp.sqrt(nv * c2) + ADAM_EPS) + ADAM_WD * w_ref[...])
        nm_ref[...] = nm
        nv_ref[...] = nv

    row = pl.BlockSpec((tr, C), lambda i: (i, 0))
    sh = jax.ShapeDtypeStruct((R, C), f32)
    return pl.pallas_call(body, name=name, grid=(R // tr,), in_specs=[row] * 4, out_specs=[row] * 3,
                          out_shape=[sh, sh, sh], compiler_params=_params("parallel"))(w, g, m, v)


def _adamw_update(w, g, m, v):
    c1 = 1.0 / (1.0 - ADAM_B1 ** ADAM_STEP)
    c2 = 1.0 / (1.0 - ADAM_B2 ** ADAM_STEP)
    nm = ADAM_B1 * m + (1.0 - ADAM_B1) * g
    nv = ADAM_B2 * v + (1.0 - ADAM_B2) * g * g
    return -ADAM_LR * ((nm * c1) / (jnp.sqrt(nv * c2) + ADAM_EPS) + ADAM_WD * w), nm, nv


def _adamw_many(ws, gs, ms, vs):
    n = len(ws)

    def body(*refs):
        ins, outs = refs[:4 * n], refs[4 * n:]
        for a in range(n):
            d, nm, nv = _adamw_update(ins[a][...], ins[n + a][...], ins[2 * n + a][...], ins[3 * n + a][...])
            outs[a][...] = d
            outs[n + a][...] = nm
            outs[2 * n + a][...] = nv

    shapes = [jax.ShapeDtypeStruct(a.shape, f32) for a in ws]
    return pl.pallas_call(body, name="adamw_small", out_shape=shapes * 3)(*ws, *gs, *ms, *vs)


def _zoh_parts(lr, li, log_dt):
    dt = jnp.exp(log_dt)
    mag = jnp.exp(lr * dt)
    c, s = jnp.cos(li * dt), jnp.sin(li * dt)
    ab_re, ab_im = mag * c, mag * s
    den = lr * lr + li * li
    nr = ab_re - 1.0
    coef_re = (nr * lr + ab_im * li) / den
    coef_im = (ab_im * lr - nr * li) / den
    return dt, mag, c, s, ab_re, ab_im, den, nr, coef_re, coef_im


def _zoh_fwd(lr, li, log_dt, b_re, b_im):
    def body(lr_ref, li_ref, ld_ref, br_ref, bi_ref, ar_ref, ai_ref, bbr_ref, bbi_ref):
        _, _, _, _, ab_re, ab_im, _, _, coef_re, coef_im = _zoh_parts(lr_ref[...], li_ref[...], ld_ref[...])
        ar_ref[...] = ab_re
        ai_ref[...] = ab_im
        bbr_ref[...] = coef_re * br_ref[...] - coef_im * bi_ref[...]
        bbi_ref[...] = coef_re * bi_ref[...] + coef_im * br_ref[...]

    col = jax.ShapeDtypeStruct(lr.shape, f32)
    mat = jax.ShapeDtypeStruct(b_re.shape, f32)
    return pl.pallas_call(body, name="zoh_fwd", out_shape=[col, col, mat, mat])(lr, li, log_dt, b_re, b_im)


def _zoh_bwd(lr, li, log_dt, b_re, b_im, d_ar, d_ai, d_bbr, d_bbi):
    n = lr.shape[1]
    groups = n // SSM_STATE

    def body(lr_ref, li_ref, ld_ref, br_ref, bi_ref, dar_ref, dai_ref, dbbr_ref, dbbi_ref,
             dlr_ref, dli_ref, dld_ref, dbr_ref, dbi_ref):
        lr_, li_ = lr_ref[...], li_ref[...]
        dt, mag, c, s, _, ab_im, den, nr, coef_re, coef_im = _zoh_parts(lr_, li_, ld_ref[...])
        br, bi, dbbr, dbbi = br_ref[...], bi_ref[...], dbbr_ref[...], dbbi_ref[...]
        dbr_ref[...] = coef_re * dbbr + coef_im * dbbi
        dbi_ref[...] = coef_re * dbbi - coef_im * dbbr
        d_cr = jnp.sum(dbbr * br + dbbi * bi, axis=0, keepdims=True)
        d_ci = jnp.sum(dbbi * br - dbbr * bi, axis=0, keepdims=True)
        d_nr = (d_cr * lr_ - d_ci * li_) / den
        d_abi = dai_ref[...] + (d_cr * li_ + d_ci * lr_) / den
        d_abr = dar_ref[...] + d_nr
        d_den = -(d_cr * coef_re + d_ci * coef_im) / den
        d_lr = (d_cr * nr + d_ci * ab_im) / den + 2.0 * lr_ * d_den
        d_li = (d_cr * ab_im - d_ci * nr) / den + 2.0 * li_ * d_den
        d_theta = mag * (d_abi * c - d_abr * s)
        d_arg = mag * (d_abr * c + d_abi * s)
        dlr_ref[...] = d_lr + d_arg * dt
        dli_ref[...] = d_li + d_theta * dt
        d_dt = d_arg * lr_ + d_theta * li_
        member = (lax.broadcasted_iota(jnp.int32, (n, groups), 0) >> (SSM_STATE.bit_length() - 1)
                  == lax.broadcasted_iota(jnp.int32, (n, groups), 1)).astype(f32)
        dld_ref[...] = jnp.dot(d_dt * dt, member, preferred_element_type=f32, precision=lax.Precision.HIGHEST)

    col = jax.ShapeDtypeStruct(lr.shape, f32)
    mat = jax.ShapeDtypeStruct(b_re.shape, f32)
    return pl.pallas_call(body, name="zoh_bwd", out_shape=[col, col, jax.ShapeDtypeStruct((1, groups), f32), mat, mat])(
        lr, li, log_dt, b_re, b_im, d_ar, d_ai, d_bbr, d_bbi)


def _lower_bound_fwd(logits):
    def body(x_ref, o_ref):
        x = x_ref[...]
        e = jnp.exp(x - jnp.max(x, axis=0, keepdims=True))
        o_ref[...] = e / jnp.sum(e, axis=0, keepdims=True)

    return pl.pallas_call(body, name="lower_bound_fwd", out_shape=jax.ShapeDtypeStruct(logits.shape, f32))(logits)


def _lower_bound_bwd(sm, d_lb):
    def body(sm_ref, d_ref, o_ref):
        smv = sm_ref[...]
        row = lax.broadcasted_iota(jnp.int32, smv.shape, 0)
        sm0 = smv[0:1, :]
        o_ref[...] = sm0 * d_ref[...] * (jnp.where(row == 0, 1.0, 0.0) - smv)

    return pl.pallas_call(body, name="lower_bound_bwd", out_shape=jax.ShapeDtypeStruct(sm.shape, f32))(sm, d_lb)


def _s5_tables(ab_re, ab_im, bb_re, bb_im, c_re, c_im, seg):
    eye = jnp.eye(SLAB_GROUPS, dtype=f32)

    def blk_in(bb):
        return jnp.einsum("hsgp,gk->sghkp", bb.reshape(SSM_GROUP, N_SLAB, SLAB_GROUPS, SSM_STATE), eye).reshape(
            N_SLAB, SLAB_CH, SLAB_NS)

    def blk_out(cc):
        return jnp.einsum("sghp,gk->skpgh", cc.reshape(N_SLAB, SLAB_GROUPS, SSM_GROUP, SSM_STATE), eye).reshape(
            N_SLAB, SLAB_NS, SLAB_CH)

    bs = jnp.concatenate([blk_in(bb_re), blk_in(bb_im)], axis=2).astype(bf16)
    cs = jnp.concatenate([blk_out(c_re), blk_out(-c_im)], axis=1).astype(bf16)
    n = SSM_GROUPS * SSM_STATE
    pw = _power_table(jnp.stack([ab_re.reshape(1, n), ab_im.reshape(1, n)]), -(-seg // SUBLANES))
    return bs, cs, pw


def _power_table(ab, tiles):
    n = ab.shape[2]

    def body(a_ref, o_ref):
        row = lax.broadcasted_iota(jnp.int32, (SUBLANES, n), 0)
        ar, ai = a_ref[0], a_ref[1]
        tr, ti = jnp.broadcast_to(ar, (SUBLANES, n)), jnp.broadcast_to(ai, (SUBLANES, n))
        pr, pi = ar, ai
        for r in range(1, SUBLANES):
            pr, pi = pr * ar - pi * ai, pr * ai + pi * ar
            tr = jnp.where(row == r, pr, tr)
            ti = jnp.where(row == r, pi, ti)
        o_ref[0, 0:SUBLANES, :] = tr
        o_ref[1, 0:SUBLANES, :] = ti

        def step(j, carry):
            cr, ci = carry
            cr, ci = cr * pr - ci * pi, cr * pi + ci * pr
            o_ref[0, _rows8(j), :] = cr
            o_ref[1, _rows8(j), :] = ci
            return cr, ci

        lax.fori_loop(1, tiles, step, (tr, ti))

    return pl.pallas_call(body, name="power_table", out_shape=jax.ShapeDtypeStruct((2, SUBLANES * tiles, n), f32))(ab)


def _s5_table_grads(dbs, dcs, da):
    eye = jnp.eye(SLAB_GROUPS, dtype=f32)
    d6 = dbs.reshape(N_SLAB, SLAB_GROUPS, SSM_GROUP, 2, SLAB_GROUPS, SSM_STATE)
    dbb = jnp.einsum("sghrkp,gk->rhsgp", d6, eye).reshape(2, SSM_GROUP, SSM_GROUPS * SSM_STATE)
    c6 = dcs.reshape(N_SLAB, 2, SLAB_GROUPS, SSM_STATE, SLAB_GROUPS, SSM_GROUP)
    dcc = jnp.einsum("srkpgh,gk->rsghp", c6, eye).reshape(2, SSM_GROUPS, SSM_GROUP, SSM_STATE)
    dab = da.transpose(1, 0, 2).reshape(2, SSM_GROUPS, SSM_STATE)
    return dab[0], dab[1], dbb[0], dbb[1], dcc[0], -dcc[1]


SMALL = ["mix_norm_g", "ssm_lambda_re", "ssm_lambda_im", "ssm_log_dt", "ssm_b_re", "ssm_b_im", "ssm_c_re", "ssm_c_im",
         "ssm_d", "hgrn_lb_logits", "hgrn_norm_g", "ffn_norm_g", "conv_b", "final_norm_g"]
SHARDED_SMALL = ["meta_tokens", "conv_w"]
BIG = ["w_in", "ssm_w_glu", "w_ssm_proj", "w_hgrn_proj", "w_out", "w_up", "w_down"]
WEIGHTS = ['meta_tokens', 'mix_norm_g', 'w_in', 'ssm_lambda_re', 'ssm_lambda_im', 'ssm_log_dt', 'ssm_b_re', 'ssm_b_im',
           'ssm_c_re', 'ssm_c_im', 'ssm_d', 'ssm_w_glu', 'w_ssm_proj', 'hgrn_lb_logits', 'hgrn_norm_g', 'w_hgrn_proj',
           'w_out', 'ffn_norm_g', 'w_up', 'conv_w', 'conv_b', 'w_down', 'final_norm_g']


LATER = [k for k in BIG if k != "w_in"]


def _full_weights(gathered, shards, chip):
    Dm = D_MODEL
    g = {k: lax.dynamic_update_slice(gathered[k], shards[k][None], (chip, 0, 0)) for k in gathered}
    full = {}
    for k, v in g.items():
        if k == "w_in":
            full[k] = jnp.roll(v.transpose(1, 0, 2).reshape(Dm, IN_COLS), -Dm, axis=1)
        elif k == "w_up":
            full[k] = v.transpose(1, 0, 2).reshape(Dm, 2 * D_FF)
        else:
            full[k] = v.reshape(-1, Dm)
    return full


def _local_grads(x, tgt, meta, w, full, shards, chip, core):
    B, S, Dm = x.shape
    L = S + N_META
    T = B * L
    h0 = jnp.concatenate([jnp.broadcast_to(meta[None], (B, N_META, Dm)), x], axis=1).reshape(T, Dm)

    lb_all = _lower_bound_fwd(w["hgrn_lb_logits"])
    lb = lb_all[0:1]
    gp = SSM_GROUPS * SSM_STATE
    zoh_in = (w["ssm_lambda_re"].reshape(1, gp), w["ssm_lambda_im"].reshape(1, gp),
              jnp.repeat(w["ssm_log_dt"].reshape(SSM_GROUPS, 1), SSM_STATE, axis=1).reshape(1, gp),
              w["ssm_b_re"].reshape(gp, SSM_GROUP).T, w["ssm_b_im"].reshape(gp, SSM_GROUP).T)
    ab_re, ab_im, bb_re, bb_im = _zoh_fwd(*zoh_in)
    bs, cs, pw = _s5_tables(ab_re, ab_im, bb_re, bb_im, w["ssm_c_re"][0], w["ssm_c_im"][0], L // SUBLANES)

    z1 = _rmsnorm_fwd("mix_norm", h0, w["mix_norm_g"])
    p, gathered = _in_proj_gather(z1, full["w_in"], [shards[k] for k in LATER])
    full = {**full, **_full_weights(dict(zip(LATER, gathered)), shards, chip)}
    ya0 = _s5_fwd(p, bs, cs, pw, w["ssm_d"], B, L)
    gl, ya = _glu_proj_fwd(ya0, full["ssm_w_glu"])
    yb = _hgrn_fwd(p, lb, w["hgrn_norm_g"], B, L)
    pa, pb, merged = _proj_merge_fwd(ya, yb, full["w_ssm_proj"], full["w_hgrn_proj"], p)
    h1, z2 = _out_proj_norm(merged, full["w_out"], h0, w["ffn_norm_g"])
    up = _mm_rows("up_proj", z2, full["w_up"], "nn", f32, D_FF // 2, tm_target=2064)
    ff = _conv_fwd(up, full["conv_w"], w["conv_b"], B, L)
    h2 = _mm_rows("down_proj", ff, full["w_down"], "nn", f32, 1024, res=h1, tk=D_FF // 2)

    tgt_rows = jnp.pad(tgt, ((0, 0), (N_META, 0), (0, 0))).reshape(T, Dm)
    dh2, loss, d_final_g = _final_loss(h2, tgt_rows, w["final_norm_g"].reshape(1, Dm), L)

    dff = _mm_rows("d_ff", dh2, full["w_down"], "nt", f32, D_FF // 2)
    g_w_down = _mm_wgrad("dw_down", ff, dh2, tn=512)
    dup, dconv = _conv_bwd(up, dff, full["conv_w"], w["conv_b"], B, L)
    g_w_up = _dw_up(z2, dup)
    dh1, d_ffn_g = _dz2_norm(dup, full["w_up"], h1, w["ffn_norm_g"], dh2)

    g_w_out = _mm_wgrad("dw_out", merged, dh1)
    dpa, dpb, dp = _merge_bwd_fused(dh1, full["w_out"], p, pa, pb)
    dgl, dya0_direct = _glu_bwd_fused(dpa, full["w_ssm_proj"], ya0, gl)
    g_w_ssm_proj = _mm_wgrad("dw_ssm_proj", ya, dpa)
    dyb = _mm_rows("d_yb", dpb, full["w_hgrn_proj"], "nt", f32, 1024)
    g_w_hgrn_proj = _mm_wgrad("dw_hgrn_proj", yb, dpb)
    parts = {
        "w_ssm_proj": g_w_ssm_proj.reshape(N_CHIPS, Dm // N_CHIPS, Dm),
        "w_hgrn_proj": g_w_hgrn_proj.reshape(N_CHIPS, Dm // N_CHIPS, Dm), "w_out": g_w_out.reshape(N_CHIPS, Dm // N_CHIPS, Dm),
        "w_up": g_w_up, "w_down": g_w_down.reshape(N_CHIPS, D_FF // N_CHIPS, Dm),
    }
    early = [k for k in LATER if k != "ssm_w_glu"]
    (dp, d_lb, d_hgrn_g), got_early = _hgrn_bwd(p, dyb, dp, lb, w["hgrn_norm_g"], B, L, [parts[k] for k in early])
    dya0 = _mm_rows("d_ya0", dgl, full["ssm_w_glu"], "nt", f32, 1024, res=dya0_direct)
    parts["ssm_w_glu"] = _mm_wgrad("dw_glu", ya0, dgl).reshape(N_CHIPS, Dm // N_CHIPS, Dm)
    got = dict(zip(early, got_early))
    got["ssm_w_glu"] = _sibling_halves([parts["ssm_w_glu"]], "sibling_halves_w_glu")[0]
    sums = {k: _add_own_half("add_half_" + k, parts[k], got[k], core) for k in LATER}
    (dp, dbs, dcs, da, d_skip), slots_later = _s5_bwd(p, dya0, dp, bs, cs, pw, w["ssm_d"], B, L, [sums[k] for k in LATER])
    slots = dict(zip(LATER, slots_later))
    g_w_in = _dw_in(z1, dp)
    dh0, d_mix_g = _dz1_norm(dp, full["w_in"], h0, w["mix_norm_g"], dh1)

    dh0 = dh0.reshape(B, L, Dm)
    grad_x = dh0[:, N_META:]
    d_meta = _meta_grad(dh0[:, :N_META])

    d_ab_re, d_ab_im, d_bb_re, d_bb_im, d_c_re, d_c_im = _s5_table_grads(dbs, dcs, da)
    d_lr, d_li, d_log_dt, d_b_re, d_b_im = _zoh_bwd(*zoh_in, d_ab_re.reshape(1, gp), d_ab_im.reshape(1, gp), d_bb_re, d_bb_im)
    gps = (SSM_GROUPS, SSM_STATE)
    d_lr, d_li, d_log_dt = d_lr.reshape(gps), d_li.reshape(gps), d_log_dt.reshape(SSM_GROUPS)
    d_b_re, d_b_im = d_b_re.T.reshape(gps + (SSM_GROUP,)), d_b_im.T.reshape(gps + (SSM_GROUP,))
    d_logits = _lower_bound_bwd(lb_all, d_lb)
    small = {
        "meta_tokens": d_meta, "mix_norm_g": d_mix_g, "ssm_lambda_re": d_lr[None], "ssm_lambda_im": d_li[None],
        "ssm_log_dt": d_log_dt[None], "ssm_b_re": d_b_re[None], "ssm_b_im": d_b_im[None], "ssm_c_re": d_c_re[None],
        "ssm_c_im": d_c_im[None], "ssm_d": d_skip, "hgrn_lb_logits": d_logits, "hgrn_norm_g": d_hgrn_g,
        "ffn_norm_g": d_ffn_g, "conv_w": dconv[:, 0:3, :].transpose(1, 0, 2).reshape(3, 2 * D_FF),
        "conv_b": dconv[:, 3, :].reshape(1, 2 * D_FF), "final_norm_g": d_final_g.reshape(Dm),
    }
    sums["w_in"] = _add_own_half_w_in(g_w_in, _sibling_halves([g_w_in], "sibling_halves_w_in")[0], core)
    slots["w_in"] = _chip_exchange([sums["w_in"]])[0]
    return loss, grad_x, sums, slots, small


PACK_ROWS = 256


def _pack(parts):
    flat = jnp.concatenate([parts[k].reshape(-1) for k in parts])
    n = flat.shape[0]
    rows = -(-n // (PACK_ROWS * LANES)) * PACK_ROWS
    flat = jnp.pad(flat, (0, rows * LANES - n))
    return flat.reshape(rows, LANES)


def _unpack(packed, like):
    flat = packed.reshape(-1)
    out, o = {}, 0
    for k, ref in like.items():
        n = math.prod(ref.shape)
        out[k] = flat[o:o + n].reshape(ref.shape)
        o += n
    return out


def kernel(x, meta_tokens, mix_norm_g, w_in, ssm_lambda_re, ssm_lambda_im, ssm_log_dt, ssm_b_re, ssm_b_im, ssm_c_re, ssm_c_im, ssm_d, ssm_w_glu, w_ssm_proj, hgrn_lb_logits, hgrn_norm_g, w_hgrn_proj, w_out, ffn_norm_g, w_up, conv_w, conv_b, w_down, final_norm_g, loss_target, m_meta_tokens, m_mix_norm_g, m_w_in, m_ssm_lambda_re, m_ssm_lambda_im, m_ssm_log_dt, m_ssm_b_re, m_ssm_b_im, m_ssm_c_re, m_ssm_c_im, m_ssm_d, m_ssm_w_glu, m_w_ssm_proj, m_hgrn_lb_logits, m_hgrn_norm_g, m_w_hgrn_proj, m_w_out, m_ffn_norm_g, m_w_up, m_conv_w, m_conv_b, m_w_down, m_final_norm_g, v_meta_tokens, v_mix_norm_g, v_w_in, v_ssm_lambda_re, v_ssm_lambda_im, v_ssm_log_dt, v_ssm_b_re, v_ssm_b_im, v_ssm_c_re, v_ssm_c_im, v_ssm_d, v_ssm_w_glu, v_w_ssm_proj, v_hgrn_lb_logits, v_hgrn_norm_g, v_w_hgrn_proj, v_w_out, v_ffn_norm_g, v_w_up, v_conv_w, v_conv_b, v_w_down, v_final_norm_g):
    args = dict(locals())
    w = {k: args[k] for k in WEIGHTS}
    mom = {k: args["m_" + k] for k in WEIGHTS}
    var = {k: args["v_" + k] for k in WEIGHTS}
    Dm = D_MODEL
    cx, cy, cc = lax.axis_index("x"), lax.axis_index("y"), lax.axis_index("c")
    chip = 2 * cx + cy

    shards = {k: w[k][0].astype(bf16) for k in BIG}
    g_meta, g_cw = _allgather_chips([w["meta_tokens"], w["conv_w"][0]])
    full = _full_weights({"w_in": _allgather_split([shards["w_in"]])[0]}, shards, chip)
    full["conv_w"] = g_cw.transpose(1, 0, 2).reshape(3, 2 * D_FF)
    meta_full = g_meta.transpose(1, 0, 2).reshape(N_META, Dm)

    core = cc.reshape(1).astype(jnp.int32)
    loss_part, grad_x, sums, slots, small = _local_grads(x, loss_target, meta_full, w, full, shards, chip, core)

    where = jnp.stack([chip, cc]).astype(jnp.int32)
    fulls = [_sum_chips("sum_chips_" + k, slots[k], sums[k], where) for k in BIG]
    g_big = dict(zip(BIG, _sibling_join(fulls)))

    small_all = dict(small)
    small_all["loss"] = loss_part[0, 0:1]
    packed = _pack(small_all)
    slots_dev = lax.dynamic_update_slice(_allgather_devices(packed), packed[None], (2 * chip + cc, 0, 0))
    reduced = _unpack(_sum_slots("sum_devices", slots_dev), small_all)
    loss = reduced.pop("loss")[0]
    mcols = Dm // N_CHIPS
    ccols = 2 * D_FF // N_CHIPS
    grads = {k: reduced[k] for k in SMALL}
    grads["meta_tokens"] = lax.dynamic_slice(reduced["meta_tokens"], (0, chip * mcols), (N_META, mcols))
    grads["conv_w"] = lax.dynamic_slice(reduced["conv_w"], (0, chip * ccols), (3, ccols))[None]
    for k in BIG:
        grads[k] = g_big[k][None]

    delta, new_m, new_v = {}, {}, {}
    for k in BIG:
        shp = w[k].shape
        d, nm, nv = _adamw("adamw_" + k, w[k][0], grads[k][0], mom[k][0], var[k][0])
        delta[k], new_m[k], new_v[k] = d.reshape(shp), nm.reshape(shp), nv.reshape(shp)
    rest = SMALL + SHARDED_SMALL

    def flat2(a):
        return a.reshape(-1, a.shape[-1])

    outs = _adamw_many(*[[flat2(t[k]) for k in rest] for t in (w, grads, mom, var)])
    n = len(rest)
    for j, dst in enumerate((delta, new_m, new_v)):
        dst.update({k: o.reshape(w[k].shape) for k, o in zip(rest, outs[j * n:(j + 1) * n])})

    return (loss, grad_x, *[grads[k].reshape(w[k].shape) for k in WEIGHTS], *[delta[k] for k in WEIGHTS],
            *[new_m[k] for k in WEIGHTS], *[new_v[k] for k in WEIGHTS])
```

```python
import math

import jax
import jax.numpy as jnp
from jax import lax
from jax.experimental import pallas as pl
from jax.experimental.pallas import tpu as pltpu

f32 = jnp.float32
bf16 = jnp.bfloat16

D_MODEL = 1024
N_META = 16
SSM_GROUP = 16
SSM_GROUPS = 64
SSM_STATE = 64
SLAB_GROUPS = 8
N_SLAB = SSM_GROUPS // SLAB_GROUPS
SLAB_CH = SLAB_GROUPS * SSM_GROUP
SLAB_NS = SLAB_GROUPS * SSM_STATE
HEADS = 8
HEAD_DIM = 128
CHUNK = 16
D_FF = 2816
IN_COLS = 7168
EPS = 1e-6
SUBLANES = 8
LANES = 128
N_CHIPS = 4
N_DEV = 8
ADAM_LR, ADAM_B1, ADAM_B2, ADAM_EPS, ADAM_WD, ADAM_STEP = 0.001, 0.9, 0.999, 1e-08, 0.01, 10
MESH = pl.DeviceIdType.MESH
ANY = pl.BlockSpec(memory_space=pl.ANY)

SEG_Q, SEG_F, SEG_I, SEG_OG, SEG_GA, SEG_GB, SEG_U = range(7)
N_SEG = 7


def _tile(n, target, mult=SUBLANES):
    best = None
    for d in range(mult, min(n, target) + 1, mult):
        if n % d == 0:
            best = d
    return n if best is None else best


def _params(*sem):
    return pltpu.CompilerParams(dimension_semantics=sem)


def _sigmoid(x):
    return 1.0 / (1.0 + jnp.exp(-x))


_DIMS = {"nn": (((1,), (0,)), ((), ())), "nt": (((1,), (1,)), ((), ())), "tn": (((0,), (0,)), ((), ()))}


def _mm(name, a, b, dims, grid, a_spec, b_spec, out_shape, out_spec, acc_shape, res=None, res_spec=None):
    nk = grid[2]
    dn = _DIMS[dims]

    def body(*refs):
        if res is None:
            a_ref, b_ref, o_ref, acc = refs
        else:
            a_ref, b_ref, r_ref, o_ref, acc = refs
        k = pl.program_id(2)

        @pl.when(k == 0)
        def _():
            acc[...] = jnp.zeros_like(acc)

        acc[...] += lax.dot_general(a_ref[...].astype(bf16), b_ref[...].astype(bf16), dn, preferred_element_type=f32)

        @pl.when(k == nk - 1)
        def _():
            r = acc[...]
            if res is not None:
                r = r + r_ref[...]
            o_ref[...] = r.astype(o_ref.dtype)

    ins = [a, b] + ([] if res is None else [res])
    specs = [a_spec, b_spec] + ([] if res is None else [res_spec])
    return pl.pallas_call(
        body, name=name, grid=grid, in_specs=specs, out_specs=out_spec, out_shape=out_shape,
        scratch_shapes=[pltpu.VMEM(acc_shape, f32)],
        compiler_params=_params("parallel", "parallel", "arbitrary"),
    )(*ins)


def _mm_rows(name, a, w, dims, out_dtype, tn, res=None, tk=None, tm_target=1032):
    T, K = a.shape
    N = w.shape[1] if dims == "nn" else w.shape[0]
    tm = _tile(T, tm_target)
    tk = K if tk is None else tk
    grid = (T // tm, N // tn, K // tk)
    a_spec = pl.BlockSpec((tm, tk), lambda i, j, k: (i, k))
    if dims == "nn":
        b_spec = pl.BlockSpec((tk, tn), lambda i, j, k: (k, j))
    else:
        b_spec = pl.BlockSpec((tn, tk), lambda i, j, k: (j, k))
    o_spec = pl.BlockSpec((tm, tn), lambda i, j, k: (i, j))
    return _mm(name, a, w, dims, grid, a_spec, b_spec, jax.ShapeDtypeStruct((T, N), out_dtype), o_spec, (tm, tn),
               res=res, res_spec=None if res is None else o_spec)


def _mm_fused(name, pairs, dims, extras, epilogue, outs, rows=(), tm_target=688):
    T, K = pairs[0][0].shape
    N = pairs[0][1].shape[1] if dims == "nn" else pairs[0][1].shape[0]
    tm = _tile(T, tm_target)
    tn = N
    grid = (T // tm, N // tn)
    npair, nex = len(pairs), len(extras) + len(rows)
    dn = _DIMS[dims]

    def body(*refs):
        ab = refs[:2 * npair]
        ex = refs[2 * npair:2 * npair + nex]
        o_refs = refs[2 * npair + nex:]
        accs = [lax.dot_general(ab[2 * q][...].astype(bf16), ab[2 * q + 1][...].astype(bf16), dn, preferred_element_type=f32)
                for q in range(npair)]
        vals = epilogue(accs, [e[...] for e in ex])
        for o_ref, v in zip(o_refs, vals):
            if isinstance(v, (list, tuple)):
                for s_, vs in enumerate(v):
                    o_ref[s_] = vs.astype(o_ref.dtype)
            else:
                o_ref[...] = v.astype(o_ref.dtype)

    ins, specs = [], []
    for a, w in pairs:
        ins += [a, w]
        specs.append(pl.BlockSpec((tm, K), lambda i, j: (i, 0)))
        specs.append(pl.BlockSpec((K, tn), lambda i, j: (0, j)) if dims == "nn" else pl.BlockSpec((tn, K), lambda i, j: (j, 0)))
    for arr, off in extras:
        ins.append(arr)
        specs.append(pl.BlockSpec((tm, tn), lambda i, j, off=off: (i, off + j)))
    for arr in rows:
        ins.append(arr)
        specs.append(pl.BlockSpec((1, tn), lambda i, j: (0, j)))
    shapes, ospecs = [], []
    for o in outs:
        if isinstance(o, tuple):
            dt, nseg, total, blk = o
            shapes.append(jax.ShapeDtypeStruct((total, T, N), dt))
            ospecs.append(pl.BlockSpec((nseg, tm, tn), lambda i, j, blk=blk: (blk, i, j)))
        else:
            shapes.append(jax.ShapeDtypeStruct((T, N), o))
            ospecs.append(pl.BlockSpec((tm, tn), lambda i, j: (i, j)))
    return pl.pallas_call(body, name=name, grid=grid, in_specs=specs, out_specs=ospecs, out_shape=shapes,
                          compiler_params=_params("parallel", "parallel"))(*ins)


def _glu_proj_fwd(ya0, w_glu):
    def epi(accs, tiles):
        return accs[0], tiles[0] * _sigmoid(accs[0])

    return _mm_fused("glu_proj", [(ya0, w_glu)], "nn", [(ya0, 0)], epi, [f32, bf16], tm_target=1032)


def _proj_merge_fwd(ya, yb, w_sp, w_hp, p):
    def epi(accs, tiles):
        return accs[0], accs[1], _sigmoid(tiles[0]) * accs[0] + _sigmoid(tiles[1]) * accs[1]

    return _mm_fused("proj_merge", [(ya, w_sp), (yb, w_hp)], "nn", [(p, SEG_GA), (p, SEG_GB)], epi, [f32, f32, bf16])


def _merge_bwd_fused(dh1, w_out, p, pa, pb):
    def epi(accs, tiles):
        d = accs[0]
        sa, sb = _sigmoid(tiles[0]), _sigmoid(tiles[1])
        return d * sa, d * sb, [d * tiles[2] * sa * (1.0 - sa), d * tiles[3] * sb * (1.0 - sb)]

    return _mm_fused("d_merged", [(dh1, w_out)], "nt", [(p, SEG_GA), (p, SEG_GB), (pa, 0), (pb, 0)], epi,
                     [bf16, bf16, (bf16, 2, N_SEG, SEG_GA // 2)])


def _out_proj_norm(merged, w_out, h0, g):
    def epi(accs, tiles):
        h1 = tiles[0] + accs[0]
        r = lax.rsqrt(jnp.mean(h1 * h1, axis=-1, keepdims=True) + EPS)
        return h1, h1 * r * tiles[1]

    return _mm_fused("out_proj", [(merged, w_out)], "nn", [(h0, 0)], epi, [f32, bf16], rows=[g], tm_target=1032)


def _mm_rmsnorm_bwd(name, a, b, grid, a_spec, b_spec, x, g, dres):
    T, Dm = x.shape
    tm = T // grid[0]
    nk = grid[2]

    def body(a_ref, b_ref, x_ref, g_ref, dres_ref, dx_ref, dg_ref, acc):
        i, k = pl.program_id(0), pl.program_id(2)

        @pl.when(k == 0)
        def _():
            acc[...] = jnp.zeros_like(acc)

        @pl.when((i == 0) & (k == 0))
        def _():
            dg_ref[...] = jnp.zeros_like(dg_ref)

        acc[...] += lax.dot_general(a_ref[...].astype(bf16), b_ref[...].astype(bf16), _DIMS["nt"], preferred_element_type=f32)

        @pl.when(k == nk - 1)
        def _():
            xv = x_ref[...]
            r = lax.rsqrt(jnp.mean(xv * xv, axis=-1, keepdims=True) + EPS)
            xn = xv * r
            dzv = acc[...]
            dzg = dzv * g_ref[...]
            dx_ref[...] = dres_ref[...] + r * (dzg - xn * jnp.mean(dzg * xn, axis=-1, keepdims=True))
            dg_ref[...] += jnp.sum(dzv * xn, axis=0, keepdims=True)

    row = pl.BlockSpec((tm, Dm), lambda i, j, k: (i, 0))
    par = pl.BlockSpec((1, Dm), lambda i, j, k: (0, 0))
    return pl.pallas_call(
        body, name=name, grid=grid, in_specs=[a_spec, b_spec, row, par, row], out_specs=[row, par],
        out_shape=[jax.ShapeDtypeStruct((T, Dm), f32), jax.ShapeDtypeStruct((1, Dm), f32)],
        scratch_shapes=[pltpu.VMEM((tm, Dm), f32)],
        compiler_params=_params("arbitrary", "arbitrary", "arbitrary"),
    )(a, b, x, g, dres)


def _glu_bwd_fused(dpa, w_sp, ya0, gl):
    def epi(accs, tiles):
        d = accs[0]
        s = _sigmoid(tiles[1])
        return d * tiles[0] * s * (1.0 - s), d * s

    return _mm_fused("d_ya", [(dpa, w_sp)], "nt", [(ya0, 0), (gl, 0)], epi, [bf16, f32], tm_target=1032)


def _mm_wgrad(name, a, g, tn=None):
    T, K = a.shape
    N = g.shape[1]
    tk = _tile(T, 1376 if K <= D_MODEL else 688)
    tn = N if tn is None else tn
    grid = (1, N // tn, T // tk)
    a_spec = pl.BlockSpec((tk, K), lambda i, j, k: (k, 0))
    g_spec = pl.BlockSpec((tk, tn), lambda i, j, k: (k, j))
    o_spec = pl.BlockSpec((K, tn), lambda i, j, k: (0, j))
    return _mm(name, a, g, "tn", grid, a_spec, g_spec, jax.ShapeDtypeStruct((K, N), f32), o_spec, (K, tn))


def _rmsnorm_fwd(name, x, g):
    T, Dm = x.shape
    tr = _tile(T, 1376)

    def body(x_ref, g_ref, z_ref):
        xv = x_ref[...]
        r = lax.rsqrt(jnp.mean(xv * xv, axis=-1, keepdims=True) + EPS)
        z_ref[...] = (xv * r * g_ref[...]).astype(z_ref.dtype)

    return pl.pallas_call(
        body, name=name, grid=(T // tr,),
        in_specs=[pl.BlockSpec((tr, Dm), lambda i: (i, 0)), pl.BlockSpec((1, Dm), lambda i: (0, 0))],
        out_specs=pl.BlockSpec((tr, Dm), lambda i: (i, 0)),
        out_shape=jax.ShapeDtypeStruct((T, Dm), bf16), compiler_params=_params("parallel"),
    )(x, g)


def _final_loss(h2, tgt, g, L):
    T, Dm = h2.shape
    tr = _tile(L, 1032)
    per_seq = L // tr

    def body(h_ref, t_ref, g_ref, dh_ref, loss_ref, dg_ref):
        pos = (pl.program_id(0) % per_seq) * tr + lax.broadcasted_iota(jnp.int32, (tr, 1), 0)
        live = jnp.where(pos >= N_META, 1.0, 0.0)
        hv = h_ref[...]
        r = lax.rsqrt(jnp.mean(hv * hv, axis=-1, keepdims=True) + EPS)
        xn = hv * r
        gv = g_ref[...]
        err = (xn * gv - t_ref[...]) * live
        dy = err * (1.0 / Dm)
        dyg = dy * gv
        dh_ref[...] = r * (dyg - xn * jnp.mean(dyg * xn, axis=-1, keepdims=True))

        @pl.when(pl.program_id(0) == 0)
        def _():
            dg_ref[...] = jnp.zeros_like(dg_ref)
            loss_ref[...] = jnp.zeros_like(loss_ref)

        dg_ref[...] += jnp.sum(dy * xn, axis=0, keepdims=True)
        loss_ref[...] += jnp.sum(err * err) * (0.5 / Dm)

    row = pl.BlockSpec((tr, Dm), lambda i: (i, 0))
    par = pl.BlockSpec((1, Dm), lambda i: (0, 0))
    return pl.pallas_call(
        body, name="final_loss", grid=(T // tr,), in_specs=[row, row, par],
        out_specs=[row, pl.BlockSpec((1, LANES), lambda i: (0, 0)), par],
        out_shape=[jax.ShapeDtypeStruct((T, Dm), f32), jax.ShapeDtypeStruct((1, LANES), f32), jax.ShapeDtypeStruct((1, Dm), f32)],
        compiler_params=_params("arbitrary"),
    )(h2, tgt, g)


def _meta_grad(dh0_meta):
    B = dh0_meta.shape[0]

    def body(d_ref, o_ref):
        acc = d_ref[0]
        for b in range(1, B):
            acc = acc + d_ref[b]
        o_ref[...] = acc

    return pl.pallas_call(body, name="meta_grad", out_shape=jax.ShapeDtypeStruct(dh0_meta.shape[1:], f32))(dh0_meta)


def _shift_down(x, k, row):
    return jnp.where(row >= k, pltpu.roll(x, k, 0), 0.0)


def _conv_fwd(up, conv_w, conv_b, B, L):
    tc = 256
    nt = D_FF // tc

    def body(xa_ref, xb_ref, wa_ref, wb_ref, ba_ref, bb_ref, o_ref):
        head = 2 * SUBLANES
        row = lax.broadcasted_iota(jnp.int32, (head, tc), 0)

        def gated(conv):
            a = conv(xa_ref, wa_ref, ba_ref)
            b = conv(xb_ref, wb_ref, bb_ref)
            return (a * _sigmoid(a) * b).astype(o_ref.dtype)

        def conv_rolled(x_ref, w_ref, b_ref):
            x = x_ref[...]
            return b_ref[...] + w_ref[0:1, :] * pltpu.roll(x, 2, 0) + w_ref[1:2, :] * pltpu.roll(x, 1, 0) + w_ref[2:3, :] * x

        def conv_head(x_ref, w_ref, b_ref):
            x = x_ref[0:head, :]
            return (b_ref[...] + w_ref[0:1, :] * _shift_down(x, 2, row) + w_ref[1:2, :] * _shift_down(x, 1, row)
                    + w_ref[2:3, :] * x)

        o_ref[...] = gated(conv_rolled)
        o_ref[0:head, :] = gated(conv_head)

    return pl.pallas_call(
        body, name="conv_fwd", grid=(B, nt),
        in_specs=[pl.BlockSpec((L, tc), lambda b, j: (b, j)), pl.BlockSpec((L, tc), lambda b, j: (b, j + nt)),
                  pl.BlockSpec((3, tc), lambda b, j: (0, j)), pl.BlockSpec((3, tc), lambda b, j: (0, j + nt)),
                  pl.BlockSpec((1, tc), lambda b, j: (0, j)), pl.BlockSpec((1, tc), lambda b, j: (0, j + nt))],
        out_specs=pl.BlockSpec((L, tc), lambda b, j: (b, j)),
        out_shape=jax.ShapeDtypeStruct((B * L, D_FF), bf16), compiler_params=_params("parallel", "parallel"),
    )(up, up, conv_w, conv_w, conv_b, conv_b)


CONV_ROWS = 2 * SUBLANES


def _rows16(i):
    return pl.ds(pl.multiple_of(i * CONV_ROWS, CONV_ROWS), CONV_ROWS)


def _conv_taps(x_ref, i, row):
    x = x_ref[_rows16(i), :]
    live = jnp.where(i > 0, 1.0, 0.0)
    r0 = jnp.maximum(i * CONV_ROWS, 2)
    p1 = x_ref[pl.ds(r0 - 1, 1), :] * live
    p2 = x_ref[pl.ds(r0 - 2, 1), :] * live
    x1 = jnp.where(row == 0, p1, pltpu.roll(x, 1, 0))
    x2 = jnp.where(row == 0, p2, jnp.where(row == 1, p1, pltpu.roll(x, 2, 0)))
    return x, x1, x2


def _conv_bwd(up, dff, conv_w, conv_b, B, L):
    tc = 256
    nt = D_FF // tc
    n = L // CONV_ROWS

    def body(xa_ref, xb_ref, d_ref, wa_ref, wb_ref, ba_ref, bb_ref, dup_ref, dw_ref, ga_ref, gb_ref):
        row = lax.broadcasted_iota(jnp.int32, (CONV_ROWS, tc), 0)

        @pl.when(pl.program_id(1) == 0)
        def _():
            dw_ref[...] = jnp.zeros_like(dw_ref)

        zero_tail = jnp.zeros((CONV_ROWS, tc), f32)
        ga_ref[L:L + CONV_ROWS, :] = zero_tail
        gb_ref[L:L + CONV_ROWS, :] = zero_tail

        def fold(v):
            return v[0:SUBLANES, :] + v[SUBLANES:CONV_ROWS, :]

        def step(i, acc):
            taps_a = _conv_taps(xa_ref, i, row)
            taps_b = _conv_taps(xb_ref, i, row)
            a = ba_ref[...] + wa_ref[0:1, :] * taps_a[2] + wa_ref[1:2, :] * taps_a[1] + wa_ref[2:3, :] * taps_a[0]
            b = bb_ref[...] + wb_ref[0:1, :] * taps_b[2] + wb_ref[1:2, :] * taps_b[1] + wb_ref[2:3, :] * taps_b[0]
            s = _sigmoid(a)
            d = d_ref[_rows16(i), :]
            g_a = d * b * s * (1.0 + a * (1.0 - s))
            g_b = d * a * s
            ga_ref[_rows16(i), :] = g_a
            gb_ref[_rows16(i), :] = g_b
            new = []
            for g, (x, x1, x2) in ((g_a, taps_a), (g_b, taps_b)):
                new += [fold(g * x2), fold(g * x1), fold(g * x), fold(g)]
            return tuple(o + v for o, v in zip(acc, new))

        z = jnp.zeros((SUBLANES, tc), f32)
        acc = _repeat_loop(n, step, (z,) * 8)
        for h in range(2):
            for t in range(4):
                dw_ref[h, t:t + 1, :] += jnp.sum(acc[4 * h + t], axis=0, keepdims=True)

        def back(i, c):
            for h, (g_ref, w_ref) in enumerate(((ga_ref, wa_ref), (gb_ref, wb_ref))):
                g = g_ref[_rows16(i), :]
                n1 = g_ref[pl.ds(i * CONV_ROWS + CONV_ROWS, 1), :]
                n2 = g_ref[pl.ds(i * CONV_ROWS + CONV_ROWS + 1, 1), :]
                u1 = jnp.where(row == CONV_ROWS - 1, n1, pltpu.roll(g, CONV_ROWS - 1, 0))
                u2 = jnp.where(row == CONV_ROWS - 1, n2, jnp.where(row == CONV_ROWS - 2, n1, pltpu.roll(g, CONV_ROWS - 2, 0)))
                dup_ref[h, _rows16(i), :] = (w_ref[2:3, :] * g + w_ref[1:2, :] * u1 + w_ref[0:1, :] * u2).astype(dup_ref.dtype)
            return c

        _repeat_loop(n, back, 0)

    return pl.pallas_call(
        body, name="conv_bwd", grid=(nt, B),
        in_specs=[pl.BlockSpec((L, tc), lambda j, b: (b, j)), pl.BlockSpec((L, tc), lambda j, b: (b, j + nt)),
                  pl.BlockSpec((L, tc), lambda j, b: (b, j)),
                  pl.BlockSpec((3, tc), lambda j, b: (0, j)), pl.BlockSpec((3, tc), lambda j, b: (0, j + nt)),
                  pl.BlockSpec((1, tc), lambda j, b: (0, j)), pl.BlockSpec((1, tc), lambda j, b: (0, j + nt))],
        out_specs=[pl.BlockSpec((2, L, tc), lambda j, b: (0, b, j)), pl.BlockSpec((2, SUBLANES, tc), lambda j, b: (0, 0, j))],
        out_shape=[jax.ShapeDtypeStruct((2, B * L, D_FF), bf16), jax.ShapeDtypeStruct((2, SUBLANES, D_FF), f32)],
        scratch_shapes=[pltpu.VMEM((L + CONV_ROWS, tc), f32), pltpu.VMEM((L + CONV_ROWS, tc), f32)],
        compiler_params=_params("parallel", "arbitrary"),
    )(up, up, dff, conv_w, conv_w, conv_b, conv_b)


GELU_C = math.sqrt(2.0 / math.pi)
GELU_A = 0.044715


def _gelu(x):
    return 0.5 * x * (1.0 + jnp.tanh(GELU_C * (x + GELU_A * x * x * x)))


def _gelu_grad(x):
    t = jnp.tanh(GELU_C * (x + GELU_A * x * x * x))
    return 0.5 * (1.0 + t) + 0.5 * x * (1.0 - t * t) * GELU_C * (1.0 + 3.0 * GELU_A * x * x)


def _cmul_add(xr, xi, ar, ai, sr, si):
    return xr + ar * sr - ai * si, xi + ar * si + ai * sr


def _s5_project_in(u_ref, bs_ref, s_ref, L, rc):
    for r in range(0, L, rc):
        s_ref[r:r + rc, :] = jnp.dot(u_ref[r:r + rc, :].astype(bf16), bs_ref[...], preferred_element_type=f32)


def _rows8(i):
    return pl.ds(pl.multiple_of(i * SUBLANES, SUBLANES), SUBLANES)


def _repeat_loop(n, step, init):
    rep = max(u for u in (6, 4, 3, 2, 1) if n % u == 0)

    def body(t, carry):
        for u in range(rep):
            carry = step(t * rep + u, carry)
        return carry

    return lax.fori_loop(0, n // rep, body, init)


def _to_segments(src_ref, dst_ref, seg):
    def step(i, c):
        dst_ref[_rows8(i), :] = src_ref[pl.ds(i, SUBLANES, stride=seg), :]
        return c

    _repeat_loop(seg, step, 0)


def _from_segments(src_ref, dst_ref, seg):
    def step(i, c):
        dst_ref[pl.ds(i, SUBLANES, stride=seg), :] = src_ref[_rows8(i), :]
        return c

    _repeat_loop(seg, step, 0)


def _half_tiles(j, seg, reverse):
    h = seg // 2
    return (_rows8(seg - 1 - j), _rows8(h - 1 - j)) if reverse else (_rows8(j), _rows8(j + h))


def _seg_local_scan(s_ref, ar, ai, seg, reverse):
    ns = SLAB_NS

    def step(j, carry):
        tiles = _half_tiles(j, seg, reverse)
        loaded = [(s_ref[rows, 0:ns], s_ref[rows, ns:2 * ns]) for rows in tiles]
        out = []
        for (xr, xi), (cr, ci) in zip(loaded, (carry[0:2], carry[2:4])):
            out += list(_cmul_add(xr, xi, ar, ai, cr, ci))
        for rows, cr, ci in zip(tiles, out[0::2], out[1::2]):
            s_ref[rows, 0:ns] = cr
            s_ref[rows, ns:2 * ns] = ci
        return tuple(out)

    z = jnp.zeros((SUBLANES, ns), f32)
    return _repeat_loop(seg // 2, step, (z, z, z, z))


def _seg_boundaries(finals, ahr, ahi, reverse):
    fxr, fxi, fyr, fyi = finals
    row = lax.broadcasted_iota(jnp.int32, fxr.shape, 0)
    zero = jnp.zeros_like(fxr[0:1, :])
    xr, xi, yr, yi = (jnp.zeros_like(fxr) for _ in range(4))
    prev = None
    for r in (range(SUBLANES - 1, -1, -1) if reverse else range(SUBLANES)):
        if prev is None:
            nxr, nxi = zero, zero
        else:
            nxr, nxi = _cmul_add(fyr[prev:prev + 1, :], fyi[prev:prev + 1, :], ahr, ahi, nyr, nyi)
        nyr, nyi = _cmul_add(fxr[r:r + 1, :], fxi[r:r + 1, :], ahr, ahi, nxr, nxi)
        xr, xi = jnp.where(row == r, nxr, xr), jnp.where(row == r, nxi, xi)
        yr, yi = jnp.where(row == r, nyr, yr), jnp.where(row == r, nyi, yi)
        prev = r
    return (xr, xi), (yr, yi)


def _s5_states(u_ref, bs_ref, pw_ref, up_ref, s_ref, L, rc):
    seg = L // SUBLANES
    h = seg // 2
    ns = SLAB_NS
    _to_segments(u_ref, up_ref, seg)
    _s5_project_in(up_ref, bs_ref, s_ref, L, rc)
    ar, ai = pw_ref[0, 0:1, :], pw_ref[1, 0:1, :]
    finals = _seg_local_scan(s_ref, ar, ai, seg, False)
    enter = _seg_boundaries(finals, pw_ref[0, h - 1:h, :], pw_ref[1, h - 1:h, :], False)

    def fix(j, c):
        pr, pi = pw_ref[0, pl.ds(j, 1), :], pw_ref[1, pl.ds(j, 1), :]
        tiles = _half_tiles(j, seg, False)
        loaded = [(s_ref[rows, 0:ns], s_ref[rows, ns:2 * ns]) for rows in tiles]
        for rows, (xr, xi), (br, bi) in zip(tiles, loaded, enter):
            xr, xi = _cmul_add(xr, xi, pr, pi, br, bi)
            s_ref[rows, 0:ns] = xr
            s_ref[rows, ns:2 * ns] = xi
        return c

    _repeat_loop(h, fix, 0)


def _pw_spec(seg_rows, order):
    if order == "bs":
        return pl.BlockSpec((2, seg_rows, SLAB_NS), lambda b, s: (0, 0, s))
    return pl.BlockSpec((2, seg_rows, SLAB_NS), lambda s, b: (0, 0, s))


def _s5_fwd(p, bs, cs, pw, d_skip, B, L):
    rc = _tile(L, 344)
    seg = L // SUBLANES

    def body(u_ref, bs_ref, cs_ref, pw_ref, d_ref, y_ref, s_ref, up_ref, yp_ref):
        _s5_states(u_ref, bs_ref, pw_ref, up_ref, s_ref, L, rc)
        for r in range(0, L, rc):
            ypre = (jnp.dot(s_ref[r:r + rc, :].astype(bf16), cs_ref[...], preferred_element_type=f32)
                    + d_ref[...] * up_ref[r:r + rc, :])
            yp_ref[r:r + rc, :] = _gelu(ypre)
        _from_segments(yp_ref, y_ref, seg)

    ucol = SEG_U * (D_MODEL // SLAB_CH)
    return pl.pallas_call(
        body, name="s5_fwd", grid=(B, N_SLAB),
        in_specs=[pl.BlockSpec((L, SLAB_CH), lambda b, s: (b, ucol + s)),
                  pl.BlockSpec((None, SLAB_CH, 2 * SLAB_NS), lambda b, s: (s, 0, 0)),
                  pl.BlockSpec((None, 2 * SLAB_NS, SLAB_CH), lambda b, s: (s, 0, 0)),
                  _pw_spec(pw.shape[1], "bs"),
                  pl.BlockSpec((1, SLAB_CH), lambda b, s: (0, s))],
        out_specs=pl.BlockSpec((L, SLAB_CH), lambda b, s: (b, s)),
        out_shape=jax.ShapeDtypeStruct((B * L, D_MODEL), f32),
        scratch_shapes=[pltpu.VMEM((L, 2 * SLAB_NS), f32), pltpu.VMEM((L, SLAB_CH), f32), pltpu.VMEM((L, SLAB_CH), f32)],
        compiler_params=_params("parallel", "parallel"),
    )(p, bs, cs, pw, d_skip)


def _s5_bwd(p, dya0, dp, bs, cs, pw, d_skip, B, L, sums):
    rc = _tile(L, 688)
    ns = SLAB_NS
    seg = L // SUBLANES
    nx = len(sums)

    def body(u_ref, dy_ref, dp_in, bs_ref, cs_ref, pw_ref, d_ref, *rest):
        xin, (du_ref, dbs_ref, dcs_ref, da_ref, dd_ref), xout = rest[:nx], rest[nx:nx + 5], rest[nx + 5:2 * nx + 5]
        s_ref, lam_ref, up_ref, dyp_ref, nat_ref, send, recv = rest[2 * nx + 5:]
        del dp_in
        start, finish = _chip_exchange_steps(xin, xout, send, recv)

        @pl.when((pl.program_id(0) == 0) & (pl.program_id(1) == 0))
        def _():
            start()

        @pl.when(pl.program_id(1) == 0)
        def _():
            dbs_ref[...] = jnp.zeros_like(dbs_ref)
            dcs_ref[...] = jnp.zeros_like(dcs_ref)
            da_ref[...] = jnp.zeros_like(da_ref)
            dd_ref[...] = jnp.zeros_like(dd_ref)

        _s5_states(u_ref, bs_ref, pw_ref, up_ref, s_ref, L, rc)
        _to_segments(dy_ref, dyp_ref, seg)
        for r in range(0, L, rc):
            u = up_ref[r:r + rc, :]
            sb = s_ref[r:r + rc, :].astype(bf16)
            ypre = jnp.dot(sb, cs_ref[...], preferred_element_type=f32) + d_ref[...] * u
            dyp = dyp_ref[r:r + rc, :] * _gelu_grad(ypre)
            dyp_ref[r:r + rc, :] = dyp
            dd_ref[...] += jnp.sum(dyp * u, axis=0, keepdims=True)
            dypb = dyp.astype(bf16)
            dcs_ref[...] += lax.dot_general(sb, dypb, _DIMS["tn"], preferred_element_type=f32)
            lam_ref[r:r + rc, :] = lax.dot_general(dypb, cs_ref[...], _DIMS["nt"], preferred_element_type=f32)

        h = seg // 2
        ar, ai = pw_ref[0, 0:1, :], -pw_ref[1, 0:1, :]
        finals = _seg_local_scan(lam_ref, ar, ai, seg, True)
        enter = _seg_boundaries(finals, pw_ref[0, h - 1:h, :], -pw_ref[1, h - 1:h, :], True)

        def fix(j, acc):
            accr, acci = acc
            pr, pi = pw_ref[0, pl.ds(j, 1), :], -pw_ref[1, pl.ds(j, 1), :]
            tiles = _half_tiles(j, seg, True)
            loaded = [(lam_ref[rows, 0:ns], lam_ref[rows, ns:2 * ns]) for rows in tiles]
            for rows, (xr, xi), (br, bi), t in zip(tiles, loaded, enter, (seg - 1 - j, h - 1 - j)):
                xr, xi = _cmul_add(xr, xi, pr, pi, br, bi)
                lam_ref[rows, 0:ns] = xr
                lam_ref[rows, ns:2 * ns] = xi
                prev = _rows8(jnp.maximum(t - 1, 0))
                live = jnp.where(t > 0, 1.0, 0.0)
                spr = s_ref[prev, 0:ns] * live
                spi = s_ref[prev, ns:2 * ns] * live
                accr, acci = accr + xr * spr + xi * spi, acci + xi * spr - xr * spi
            return accr, acci

        z = jnp.zeros((SUBLANES, ns), f32)
        accr, acci = _repeat_loop(h, fix, (z, z))
        row = lax.broadcasted_iota(jnp.int32, (SUBLANES, ns), 0)
        last = _rows8(seg - 1)
        spr = jnp.where(row == 0, 0.0, pltpu.roll(s_ref[last, 0:ns], 1, 0))
        spi = jnp.where(row == 0, 0.0, pltpu.roll(s_ref[last, ns:2 * ns], 1, 0))
        xr, xi = lam_ref[0:SUBLANES, 0:ns], lam_ref[0:SUBLANES, ns:2 * ns]
        accr = accr + xr * spr + xi * spi
        acci = acci + xi * spr - xr * spi
        da_ref[0:1, :] += jnp.sum(accr, axis=0, keepdims=True)
        da_ref[1:2, :] += jnp.sum(acci, axis=0, keepdims=True)

        for r in range(0, L, rc):
            lamb = lam_ref[r:r + rc, :].astype(bf16)
            dbs_ref[...] += lax.dot_general(up_ref[r:r + rc, :].astype(bf16), lamb, _DIMS["tn"], preferred_element_type=f32)
            nat_ref[r:r + rc, :] = (lax.dot_general(lamb, bs_ref[...], _DIMS["nt"], preferred_element_type=f32)
                                    + d_ref[...] * dyp_ref[r:r + rc, :])
        _from_segments(nat_ref, up_ref, seg)
        du_ref[...] = up_ref[...].astype(du_ref.dtype)

        @pl.when((pl.program_id(0) == N_SLAB - 1) & (pl.program_id(1) == B - 1))
        def _():
            finish()

    ucol = SEG_U * (D_MODEL // SLAB_CH)
    T = B * L
    col = pltpu.VMEM((L, SLAB_CH), f32)
    res = pl.pallas_call(
        body, name="s5_bwd", grid=(N_SLAB, B),
        in_specs=[pl.BlockSpec((L, SLAB_CH), lambda s, b: (b, ucol + s)),
                  pl.BlockSpec((L, SLAB_CH), lambda s, b: (b, s)),
                  ANY,
                  pl.BlockSpec((None, SLAB_CH, 2 * SLAB_NS), lambda s, b: (s, 0, 0)),
                  pl.BlockSpec((None, 2 * SLAB_NS, SLAB_CH), lambda s, b: (s, 0, 0)),
                  _pw_spec(pw.shape[1], "sb"),
                  pl.BlockSpec((1, SLAB_CH), lambda s, b: (0, s))] + [ANY] * nx,
        out_specs=[pl.BlockSpec((None, L, SLAB_CH), lambda s, b: (SEG_U, b, s)),
                   pl.BlockSpec((None, SLAB_CH, 2 * SLAB_NS), lambda s, b: (s, 0, 0)),
                   pl.BlockSpec((None, 2 * SLAB_NS, SLAB_CH), lambda s, b: (s, 0, 0)),
                   pl.BlockSpec((None, 2, SLAB_NS), lambda s, b: (s, 0, 0)),
                   pl.BlockSpec((1, SLAB_CH), lambda s, b: (0, s))] + [ANY] * nx,
        out_shape=[jax.ShapeDtypeStruct((N_SEG, T, D_MODEL), bf16),
                   jax.ShapeDtypeStruct((N_SLAB, SLAB_CH, 2 * SLAB_NS), f32),
                   jax.ShapeDtypeStruct((N_SLAB, 2 * SLAB_NS, SLAB_CH), f32),
                   jax.ShapeDtypeStruct((N_SLAB, 2, SLAB_NS), f32),
                   jax.ShapeDtypeStruct((1, D_MODEL), f32)] + [jax.ShapeDtypeStruct(a.shape, a.dtype) for a in sums],
        scratch_shapes=[pltpu.VMEM((L, 2 * SLAB_NS), f32), pltpu.VMEM((L, 2 * SLAB_NS), f32), col, col, col]
        + _chip_exchange_sems(nx),
        input_output_aliases={2: 0},
        compiler_params=_params("arbitrary", "arbitrary"),
    )(p, dya0, dp, bs, cs, pw, d_skip, *sums)
    return res[:5], res[5:]


def _dotb(a, b, dims="nn"):
    return lax.dot_general(a.astype(bf16), b.astype(bf16), _DIMS[dims], preferred_element_type=f32)


def _tile_scan(x, reverse):
    n, w = x.shape
    v = x.reshape(n // SUBLANES, SUBLANES, w)
    row = lax.broadcasted_iota(jnp.int32, v.shape, 1)
    for k in (1, 2, 4):
        if reverse:
            v = v + jnp.where(row < SUBLANES - k, pltpu.roll(v, SUBLANES - k, 1), 0.0)
        else:
            v = v + jnp.where(row >= k, pltpu.roll(v, k, 1), 0.0)
    p = v.reshape(n // CHUNK, 2, SUBLANES, w)
    lo, hi = p[:, 0], p[:, 1]
    if reverse:
        lo = lo + hi[:, 0:1, :]
    else:
        hi = hi + lo[:, SUBLANES - 1:SUBLANES, :]
    return jnp.stack([lo, hi], axis=1).reshape(n, w)


def _chunk_cumsum(x):
    return _tile_scan(x, False)


def _chunk_rev_cumsum(x):
    return _tile_scan(x, True)


def _chunk_last(x):
    n, w = x.shape
    p = x.reshape(n // CHUNK, CHUNK, w)
    return jnp.broadcast_to(p[:, CHUNK - 1:CHUNK, :], p.shape).reshape(n, w)


def _hgrn_local(q, fl, lb):
    sg = _sigmoid(fl)
    f = lb + (1.0 - lb) * sg
    g = jnp.log(f)
    cum = _chunk_cumsum(g)
    rest = _chunk_last(cum) - cum
    e = jnp.exp(cum)
    em = jnp.exp(-cum)
    eo = jnp.exp(rest)
    k = 1.0 - f
    return sg, f, e, em, eo, q * e, k * em, k * eo, cum + rest


def _chunk_pos(n):
    return lax.broadcasted_iota(jnp.int32, (n, HEAD_DIM), 0) & (CHUNK - 1)


def _hgrn_block_rows(L):
    return _tile(L, 688, CHUNK)


def _hgrn_specs(L, order):
    hb = D_MODEL // HEAD_DIM

    def spec(seg):
        if order == "bh":
            return pl.BlockSpec((L, HEAD_DIM), lambda b, h: (b, seg * hb + h))
        return pl.BlockSpec((L, HEAD_DIM), lambda h, b: (b, seg * hb + h))

    return [spec(SEG_Q), spec(SEG_F), spec(SEG_I), spec(SEG_OG)]


PAIR = 2 * CHUNK
CHUNK_SHIFT = CHUNK.bit_length() - 1


def _pair_steps(L, rb):
    steps = []
    nch = rb // CHUNK
    for r in range(0, L, rb):
        steps += [(r + p * PAIR, PAIR) for p in range(nch // 2)]
        if nch % 2:
            steps.append((r + (nch - 1) * CHUNK, CHUNK))
    return steps


def _pair_flags(rb):
    ci = lax.broadcasted_iota(jnp.int32, (rb, HEAD_DIM), 0) >> CHUNK_SHIFT
    odd = (ci & 1) == 1
    has_next = jnp.logical_and(jnp.logical_not(odd), ci < rb // CHUNK - 1)
    return odd, has_next


def _pair_masks(rb):
    r = lax.broadcasted_iota(jnp.int32, (rb, rb), 0)
    c = lax.broadcasted_iota(jnp.int32, (rb, rb), 1)
    rc, cc = r >> CHUNK_SHIFT, c >> CHUNK_SHIFT
    same = (rc == cc) & (c <= r)
    prev = ((rc & 1) == 1) & (cc == rc - 1)
    return same, prev


def _hgrn_pair_local(q, fl, lb, odd, has_next):
    sg, f, e, em, eo, qt, kt, ko, cend = _hgrn_local(q, fl, lb)
    n = q.shape[0]
    a = jnp.where(odd, pltpu.roll(cend, CHUNK, 0), 0.0)
    z = jnp.where(has_next, pltpu.roll(cend, n - CHUNK, 0), 0.0)
    ea, ez = jnp.exp(a), jnp.exp(z)
    return dict(sg=sg, f=f, e=e, em=em, eo=eo, qt=qt, kt=kt, ko=ko, ea=ea, ez=ez, qs=qt * ea, ks=ko * ez,
                decp=jnp.exp(cend + a + z))


def _pair_scores(qt, kt, ko, same, prev):
    return (jnp.where(same, _dotb(qt, kt, "nt"), 0.0) + jnp.where(prev, _dotb(qt, ko, "nt"), 0.0)).astype(bf16)


def _hgrn_fwd(p, lb, norm_g, B, L):
    rb = _hgrn_block_rows(L)
    steps = _pair_steps(L, rb)
    blocks = [slice(r, r + rb) for r in range(0, L, rb)]

    def body(q_ref, f_ref, v_ref, og_ref, lb_ref, ng_ref, y_ref, qs_s, ks_s, vb_s, decp_s, o_s, o2_s, u_s, sb_s):
        lbv = lb_ref[...]
        ngv = ng_ref[...]
        same, prev = _pair_masks(rb)
        odd, has_next = _pair_flags(rb)

        for rows in blocks:
            t = _hgrn_pair_local(q_ref[rows, :], f_ref[rows, :], lbv, odd, has_next)
            vb = v_ref[rows, :].astype(bf16)
            o_s[rows, :] = _dotb(_pair_scores(t["qt"], t["kt"], t["ko"], same, prev), vb)
            qs_s[rows, :] = t["qs"].astype(bf16)
            ks_s[rows, :] = t["ks"].astype(bf16)
            vb_s[rows, :] = vb
            decp_s[rows, :] = t["decp"]

        for n, (r0, nr) in enumerate(steps):
            u_s[n] = _dotb(vb_s[r0:r0 + nr, :], ks_s[r0:r0 + nr, :], "tn")
        st = jnp.zeros((HEAD_DIM, HEAD_DIM), f32)
        for n, (r0, nr) in enumerate(steps):
            sb_s[n] = st.astype(bf16)
            st = st * decp_s[r0:r0 + 1, :] + u_s[n]
        for n, (r0, nr) in enumerate(steps):
            o2_s[r0:r0 + nr, :] = _dotb(qs_s[r0:r0 + nr, :], sb_s[n], "nt")

        for rows in blocks:
            o = o_s[rows, :] + o2_s[rows, :]
            og = og_ref[rows, :]
            on = o * lax.rsqrt(jnp.mean(o * o, axis=-1, keepdims=True) + EPS) * ngv
            y_ref[rows, :] = (on * og * _sigmoid(og)).astype(y_ref.dtype)

    sb = pltpu.VMEM((L, HEAD_DIM), bf16)
    sf = pltpu.VMEM((L, HEAD_DIM), f32)
    return pl.pallas_call(
        body, name="hgrn_fwd", grid=(B, HEADS),
        in_specs=_hgrn_specs(L, "bh") + [pl.BlockSpec((1, HEAD_DIM), lambda b, h: (0, h)),
                                          pl.BlockSpec((1, HEAD_DIM), lambda b, h: (0, 0))],
        out_specs=pl.BlockSpec((L, HEAD_DIM), lambda b, h: (b, h)),
        out_shape=jax.ShapeDtypeStruct((B * L, D_MODEL), bf16),
        scratch_shapes=[sb, sb, sb, sf, sf, sf, pltpu.VMEM((len(steps), HEAD_DIM, HEAD_DIM), f32),
                        pltpu.VMEM((len(steps), HEAD_DIM, HEAD_DIM), bf16)],
        compiler_params=_params("parallel", "parallel"),
    )(p, p, p, p, lb, norm_g)


def _hgrn_bwd(p, dyb, dp, lb, norm_g, B, L, parts):
    rb = _hgrn_block_rows(L)
    steps = _pair_steps(L, rb)
    blocks = [slice(r, r + rb) for r in range(0, L, rb)]

    nx = len(parts)

    def body(q_ref, f_ref, v_ref, og_ref, dy_ref, dp_in, lb_ref, ng_ref, *rest):
        xin, (dseg_ref, dlb_ref, dng_ref), xout = rest[:nx], rest[nx:nx + 3], rest[nx + 3:2 * nx + 3]
        (st_ref, u_s, dsb_s, qt_s, kt_s, ko_s, qs_s, ks_s, vb_s, do_s, decp_s, o_s, o2_s, dqt_s, dkt_s, dko_s, dv_s,
         dv2_s, dqs_s, dks_s, ddecp_s, send, recv) = rest[2 * nx + 3:]
        del dp_in
        start, finish = _sibling_halves_steps(xin, xout, send, recv)

        @pl.when((pl.program_id(0) == 0) & (pl.program_id(1) == 0))
        def _():
            start()

        lbv = lb_ref[...]
        ngv = ng_ref[...]
        same, prev = _pair_masks(rb)
        odd, has_next = _pair_flags(rb)
        pos = _chunk_pos(rb)

        @pl.when(pl.program_id(1) == 0)
        def _():
            dlb_ref[...] = jnp.zeros_like(dlb_ref)

        @pl.when((pl.program_id(0) == 0) & (pl.program_id(1) == 0))
        def _():
            dng_ref[...] = jnp.zeros_like(dng_ref)

        def scores(rows):
            return _pair_scores(qt_s[rows, :], kt_s[rows, :], ko_s[rows, :], same, prev)

        for rows in blocks:
            t = _hgrn_pair_local(q_ref[rows, :], f_ref[rows, :], lbv, odd, has_next)
            for dst, key in ((qt_s, "qt"), (kt_s, "kt"), (ko_s, "ko"), (qs_s, "qs"), (ks_s, "ks")):
                dst[rows, :] = t[key].astype(bf16)
            vb_s[rows, :] = v_ref[rows, :].astype(bf16)
            decp_s[rows, :] = t["decp"]
            o_s[rows, :] = _dotb(scores(rows), vb_s[rows, :])

        for n, (r0, nr) in enumerate(steps):
            u_s[n] = _dotb(vb_s[r0:r0 + nr, :], ks_s[r0:r0 + nr, :], "tn")
        st = jnp.zeros((HEAD_DIM, HEAD_DIM), f32)
        for n, (r0, nr) in enumerate(steps):
            st_ref[n] = st
            st = st * decp_s[r0:r0 + 1, :] + u_s[n]
        for n, (r0, nr) in enumerate(steps):
            o2_s[r0:r0 + nr, :] = _dotb(qs_s[r0:r0 + nr, :], st_ref[n], "nt")

        dng = jnp.zeros((1, HEAD_DIM), f32)
        for rows in blocks:
            o = o_s[rows, :] + o2_s[rows, :]
            og = og_ref[rows, :]
            dy = dy_ref[rows, :]
            rs = lax.rsqrt(jnp.mean(o * o, axis=-1, keepdims=True) + EPS)
            xn = o * rs
            so = _sigmoid(og)
            dseg_ref[SEG_OG, rows, :] = (dy * xn * ngv * so * (1.0 + og * (1.0 - so))).astype(dseg_ref.dtype)
            don = dy * og * so
            dng = dng + jnp.sum(don * xn, axis=0, keepdims=True)
            dxo = don * ngv
            do = (rs * (dxo - xn * jnp.mean(dxo * xn, axis=-1, keepdims=True))).astype(bf16)
            do_s[rows, :] = do
            dpf = _dotb(do, vb_s[rows, :], "nt")
            dp1 = jnp.where(same, dpf, 0.0).astype(bf16)
            dp2 = jnp.where(prev, dpf, 0.0).astype(bf16)
            dqt_s[rows, :] = _dotb(dp1, kt_s[rows, :]) + _dotb(dp2, ko_s[rows, :])
            dkt_s[rows, :] = _dotb(dp1, qt_s[rows, :], "tn")
            dko_s[rows, :] = _dotb(dp2, qt_s[rows, :], "tn")
            dv_s[rows, :] = _dotb(scores(rows), do, "tn")
        dng_ref[...] += dng

        for n, (r0, nr) in enumerate(steps):
            u_s[n] = _dotb(do_s[r0:r0 + nr, :], qs_s[r0:r0 + nr, :], "tn")
        dst = jnp.zeros((HEAD_DIM, HEAD_DIM), f32)
        for n, (r0, nr) in reversed(list(enumerate(steps))):
            dsb_s[n] = dst.astype(bf16)
            ddecp_s[r0:r0 + nr, :] = jnp.broadcast_to(jnp.sum(dst * st_ref[n], axis=0, keepdims=True), (nr, HEAD_DIM))
            dst = dst * decp_s[r0:r0 + 1, :] + u_s[n]
        for n, (r0, nr) in enumerate(steps):
            rows = slice(r0, r0 + nr)
            dqs_s[rows, :] = _dotb(do_s[rows, :], st_ref[n])
            dv2_s[rows, :] = _dotb(ks_s[rows, :], dsb_s[n], "nt")
            dks_s[rows, :] = _dotb(vb_s[rows, :], dsb_s[n])

        def chunk_sum(x):
            return _chunk_last(_chunk_cumsum(x))

        dlb = jnp.zeros((1, HEAD_DIM), f32)
        for rows in blocks:
            t = _hgrn_pair_local(q_ref[rows, :], f_ref[rows, :], lbv, odd, has_next)
            dqs, dks = dqs_s[rows, :], dks_s[rows, :]
            dqt = dqt_s[rows, :] + dqs * t["ea"]
            dko = dko_s[rows, :] + dks * t["ez"]
            dkt = dkt_s[rows, :]
            dko_ko = dko * t["ko"]
            dcum = dqt * t["qt"] - dkt * t["kt"] - dko_ko
            from_next = pltpu.roll(chunk_sum(jnp.where(odd, dqs * t["qs"], 0.0)), rb - CHUNK, 0)
            from_prev = pltpu.roll(chunk_sum(jnp.where(has_next, dks * t["ks"], 0.0)), CHUNK, 0)
            d_end = (chunk_sum(dko_ko) + jnp.where(has_next, from_next, 0.0) + jnp.where(odd, from_prev, 0.0)
                     + ddecp_s[rows, :] * t["decp"])
            dcum = dcum + jnp.where(pos == CHUNK - 1, d_end, 0.0)
            df = _chunk_rev_cumsum(dcum) / t["f"] - (dkt * t["em"] + dko * t["eo"])
            dlb = dlb + jnp.sum(df * (1.0 - t["sg"]), axis=0, keepdims=True)
            dseg_ref[SEG_Q, rows, :] = (dqt * t["e"]).astype(dseg_ref.dtype)
            dseg_ref[SEG_F, rows, :] = (df * (1.0 - lbv) * t["sg"] * (1.0 - t["sg"])).astype(dseg_ref.dtype)
            dseg_ref[SEG_I, rows, :] = (dv_s[rows, :] + dv2_s[rows, :]).astype(dseg_ref.dtype)
        dlb_ref[...] += dlb

        @pl.when((pl.program_id(0) == HEADS - 1) & (pl.program_id(1) == B - 1))
        def _():
            finish()

    T = B * L
    ns = len(steps)
    sb = pltpu.VMEM((L, HEAD_DIM), bf16)
    sf = pltpu.VMEM((L, HEAD_DIM), f32)
    res = pl.pallas_call(
        body, name="hgrn_bwd", grid=(HEADS, B),
        in_specs=_hgrn_specs(L, "hb") + [pl.BlockSpec((L, HEAD_DIM), lambda h, b: (b, h)), ANY,
                                          pl.BlockSpec((1, HEAD_DIM), lambda h, b: (0, h)),
                                          pl.BlockSpec((1, HEAD_DIM), lambda h, b: (0, 0))] + [ANY] * nx,
        out_specs=[pl.BlockSpec((4, L, HEAD_DIM), lambda h, b: (0, b, h)),
                   pl.BlockSpec((1, HEAD_DIM), lambda h, b: (0, h)),
                   pl.BlockSpec((1, HEAD_DIM), lambda h, b: (0, 0))] + [ANY] * nx,
        out_shape=[jax.ShapeDtypeStruct((N_SEG, T, D_MODEL), bf16), jax.ShapeDtypeStruct((1, D_MODEL), f32),
                   jax.ShapeDtypeStruct((1, HEAD_DIM), f32)] + _sibling_halves_shapes(parts),
        scratch_shapes=[pltpu.VMEM((ns, HEAD_DIM, HEAD_DIM), f32), pltpu.VMEM((ns, HEAD_DIM, HEAD_DIM), f32),
                        pltpu.VMEM((ns, HEAD_DIM, HEAD_DIM), bf16)] + [sb] * 7 + [sf] * 11 + _sibling_halves_sems(nx),
        input_output_aliases={5: 0},
        compiler_params=_params("arbitrary", "arbitrary"),
    )(p, p, p, p, dyb, dp, lb, norm_g, *parts)
    return res[:3], res[3:]


def _dz1_norm(dp, w_in_phys, h0, g, dh1):
    _, T, Dm = dp.shape
    tm = _tile(T, 1032)
    return _mm_rmsnorm_bwd("dz1", dp, w_in_phys, (T // tm, 1, N_SEG),
                           pl.BlockSpec((None, tm, Dm), lambda i, j, k: (k, i, 0)),
                           pl.BlockSpec((Dm, Dm), lambda i, j, k: (0, k)), h0, g, dh1)


def _dz2_norm(dup, w_up, h1, g, dh2):
    _, T, _ = dup.shape
    tm = _tile(T, 1032)
    tk = D_FF // 2
    return _mm_rmsnorm_bwd("dz2", dup, w_up, (T // tm, 1, 4),
                           pl.BlockSpec((None, tm, tk), lambda i, j, k: (k // 2, i, k % 2)),
                           pl.BlockSpec((D_MODEL, tk), lambda i, j, k: (0, k)), h1, g, dh2)


def _dw_in(z1, dp):
    _, T, Dm = dp.shape
    tk = _tile(T, 2064)
    return _mm("dw_in", z1, dp, "tn", (1, N_SEG, T // tk),
               pl.BlockSpec((tk, Dm), lambda i, j, k: (k, 0)),
               pl.BlockSpec((None, tk, Dm), lambda i, j, k: (j, k, 0)),
               jax.ShapeDtypeStruct((N_SEG, Dm, Dm), f32),
               pl.BlockSpec((None, Dm, Dm), lambda i, j, k: (j, 0, 0)), (Dm, Dm))


def _dw_up(z2, dup):
    _, T, _ = dup.shape
    tn = D_FF // 2
    tk = _tile(T, 1376)
    return _mm("dw_up", z2, dup, "tn", (1, N_CHIPS, T // tk),
               pl.BlockSpec((tk, D_MODEL), lambda i, j, k: (k, 0)),
               pl.BlockSpec((None, tk, tn), lambda i, j, k: (j // 2, k, j % 2)),
               jax.ShapeDtypeStruct((N_CHIPS, D_MODEL, tn), f32),
               pl.BlockSpec((None, D_MODEL, tn), lambda i, j, k: (j, 0, 0)), (D_MODEL, tn))


def _place():
    x, y, c = lax.axis_index("x"), lax.axis_index("y"), lax.axis_index("c")
    chips = [(1 - x, y), (x, 1 - y), (1 - x, 1 - y)]
    return x, y, c, chips


def _allgather_chips(arrs):
    n = len(arrs)

    def body(*refs):
        ins, outs = refs[:n], refs[n:2 * n]
        send, recv, local = refs[2 * n:]
        x, y, c, chips = _place()
        me = 2 * x + y

        def copy(a, k, slot):
            px, py = chips[k]
            return pltpu.make_async_remote_copy(src_ref=ins[a], dst_ref=outs[a].at[slot], send_sem=send.at[3 * a + k],
                                                recv_sem=recv.at[3 * a + k], device_id=(px, py, c), device_id_type=MESH)

        for a in range(n):
            pltpu.make_async_copy(ins[a], outs[a].at[me], local.at[a]).start()
            for k in range(3):
                copy(a, k, me).start()
        for a in range(n):
            for k, (px, py) in enumerate(chips):
                copy(a, k, 2 * px + py).wait_recv()
        for a in range(n):
            pltpu.make_async_copy(ins[a], outs[a].at[me], local.at[a]).wait()
            for k in range(3):
                copy(a, k, me).wait_send()

    return pl.pallas_call(
        body, name="allgather_chips", in_specs=[ANY] * n, out_specs=[ANY] * n,
        out_shape=[jax.ShapeDtypeStruct((N_CHIPS,) + a.shape, a.dtype) for a in arrs],
        scratch_shapes=[pltpu.SemaphoreType.DMA((3 * n,)), pltpu.SemaphoreType.DMA((3 * n,)), pltpu.SemaphoreType.DMA((n,))],
    )(*arrs)


def _allgather_split(arrs):
    n = len(arrs)

    def body(*refs):
        start, finish = _gather_split_steps(refs[:n], refs[n:2 * n], *refs[2 * n:])
        start()
        finish()

    return pl.pallas_call(
        body, name="allgather_split", in_specs=[ANY] * n, out_specs=[ANY] * n,
        out_shape=[jax.ShapeDtypeStruct((N_CHIPS,) + a.shape, a.dtype) for a in arrs],
        scratch_shapes=_gather_split_sems(n),
    )(*arrs)


def _gather_split_sems(n):
    return [pltpu.SemaphoreType.DMA((3 * n,)) for _ in range(4)]


def _gather_split_steps(ins, outs, send, recv, fsend, frecv):
    n = len(ins)

    def place():
        x, y, c, chips = _place()
        return x, y, c, chips, 2 * x + y

    def half(a, core):
        rh = ins[a].shape[0] // 2
        return pl.ds(core * rh, rh)

    def copy(a, k, slot):
        x, y, c, chips, _ = place()
        px, py = chips[k]
        return pltpu.make_async_remote_copy(src_ref=ins[a].at[half(a, c), :], dst_ref=outs[a].at[slot, half(a, c), :],
                                            send_sem=send.at[3 * a + k], recv_sem=recv.at[3 * a + k],
                                            device_id=(px, py, c), device_id_type=MESH)

    def forward(a, k, core):
        x, y, c, chips, _ = place()
        px, py = chips[k]
        rows = outs[a].at[2 * px + py, half(a, core), :]
        return pltpu.make_async_remote_copy(src_ref=rows, dst_ref=rows, send_sem=fsend.at[3 * a + k],
                                            recv_sem=frecv.at[3 * a + k], device_id=(x, y, 1 - c), device_id_type=MESH)

    def start():
        me = place()[4]
        for a in range(n):
            for k in range(3):
                copy(a, k, me).start()

    def finish():
        x, y, c, chips, me = place()
        for a in range(n):
            for k, (px, py) in enumerate(chips):
                copy(a, k, 2 * px + py).wait_recv()
                forward(a, k, c).start()
        for a in range(n):
            for k in range(3):
                forward(a, k, 1 - c).wait_recv()
        for a in range(n):
            for k in range(3):
                copy(a, k, me).wait_send()
                forward(a, k, c).wait_send()

    return start, finish


def _in_proj_gather(z1, w_in, shards):
    n = len(shards)
    T, K = z1.shape
    N = w_in.shape[1]
    tm = _tile(T, 2064)
    tn = 1024
    grid = (T // tm, N // tn)

    def body(a_ref, b_ref, *rest):
        ins, o_ref, outs, sems = rest[:n], rest[n], rest[n + 1:2 * n + 1], rest[2 * n + 1:]
        start, finish = _gather_split_steps(ins, outs, *sems)
        i, j = pl.program_id(0), pl.program_id(1)

        @pl.when((i == 0) & (j == 0))
        def _():
            start()

        o_ref[...] = jnp.dot(a_ref[...], b_ref[...], preferred_element_type=f32)

        @pl.when((i == grid[0] - 1) & (j == grid[1] - 1))
        def _():
            finish()

    res = pl.pallas_call(
        body, name="in_proj", grid=grid,
        in_specs=[pl.BlockSpec((tm, K), lambda i, j: (i, 0)), pl.BlockSpec((K, tn), lambda i, j: (0, j))] + [ANY] * n,
        out_specs=[pl.BlockSpec((tm, tn), lambda i, j: (i, j))] + [ANY] * n,
        out_shape=[jax.ShapeDtypeStruct((T, N), f32)] + [jax.ShapeDtypeStruct((N_CHIPS,) + a.shape, a.dtype) for a in shards],
        scratch_shapes=_gather_split_sems(n),
        compiler_params=_params("arbitrary", "arbitrary"),
    )(z1, w_in, *shards)
    return res[0], res[1:]


def _sibling_halves(parts, name="sibling_halves"):
    n = len(parts)

    def body(*refs):
        start, finish = _sibling_halves_steps(refs[:n], refs[n:2 * n], *refs[2 * n:])
        start()
        finish()

    return pl.pallas_call(
        body, name=name, in_specs=[ANY] * n, out_specs=[ANY] * n,
        out_shape=_sibling_halves_shapes(parts), scratch_shapes=_sibling_halves_sems(n),
    )(*parts)


def _sibling_halves_shapes(parts):
    return [jax.ShapeDtypeStruct((a.shape[0], a.shape[1] // 2, a.shape[2]), a.dtype) for a in parts]


def _sibling_halves_sems(n):
    return [pltpu.SemaphoreType.DMA((n,)), pltpu.SemaphoreType.DMA((n,))]


def _sibling_halves_steps(ins, outs, send, recv):
    n = len(ins)

    def copy(a):
        x, y, c, _ = _place()
        rh = ins[a].shape[1] // 2
        return pltpu.make_async_remote_copy(src_ref=ins[a].at[:, pl.ds((1 - c) * rh, rh), :], dst_ref=outs[a],
                                            send_sem=send.at[a], recv_sem=recv.at[a], device_id=(x, y, 1 - c),
                                            device_id_type=MESH)

    def start():
        for a in range(n):
            copy(a).start()

    def finish():
        for a in range(n):
            copy(a).wait_recv()
        for a in range(n):
            copy(a).wait_send()

    return start, finish


def _add_own_half(name, part, got, core):
    nchip, R, C = part.shape
    rh = R // 2
    tr = _tile(rh, 512, 2 * SUBLANES)
    nt = rh // tr

    def body(core_ref, a_ref, b_ref, o_ref):
        del core_ref
        o_ref[...] = (a_ref[...] + b_ref[...]).astype(o_ref.dtype)

    return pl.pallas_call(
        body, name=name,
        grid_spec=pltpu.PrefetchScalarGridSpec(
            num_scalar_prefetch=1, grid=(nchip, nt),
            in_specs=[pl.BlockSpec((None, tr, C), lambda j, i, core_ref: (j, core_ref[0] * nt + i, 0)),
                      pl.BlockSpec((None, tr, C), lambda j, i, core_ref: (j, i, 0))],
            out_specs=pl.BlockSpec((None, tr, C), lambda j, i, core_ref: (j, i, 0))),
        out_shape=jax.ShapeDtypeStruct((nchip, rh, C), bf16), compiler_params=_params("parallel", "parallel"),
    )(core, part, got)


def _add_own_half_w_in(part, got, core):
    _, R, C = part.shape
    rh = R // 2
    tr = _tile(rh, 512, 2 * SUBLANES)
    nt = rh // tr
    tn = 256
    per_seg = C // tn
    per_chip = IN_COLS // N_CHIPS // tn

    def src(j):
        return ((j // per_seg + N_SEG - 1) % N_SEG, j % per_seg)

    def body(core_ref, a_ref, b_ref, o_ref):
        del core_ref
        o_ref[...] = (a_ref[...] + b_ref[...]).astype(o_ref.dtype)

    return pl.pallas_call(
        body, name="add_half_w_in",
        grid_spec=pltpu.PrefetchScalarGridSpec(
            num_scalar_prefetch=1, grid=(IN_COLS // tn, nt),
            in_specs=[pl.BlockSpec((None, tr, tn), lambda j, i, core_ref: (src(j)[0], core_ref[0] * nt + i, src(j)[1])),
                      pl.BlockSpec((None, tr, tn), lambda j, i, core_ref: (src(j)[0], i, src(j)[1]))],
            out_specs=pl.BlockSpec((None, tr, tn), lambda j, i, core_ref: (j // per_chip, i, j % per_chip))),
        out_shape=jax.ShapeDtypeStruct((N_CHIPS, rh, IN_COLS // N_CHIPS), bf16), compiler_params=_params("parallel", "parallel"),
    )(core, part, got)


def _chip_exchange(sums):
    n = len(sums)

    def body(*refs):
        start, finish = _chip_exchange_steps(refs[:n], refs[n:2 * n], *refs[2 * n:])
        start()
        finish()

    return pl.pallas_call(
        body, name="chip_exchange", in_specs=[ANY] * n, out_specs=[ANY] * n,
        out_shape=[jax.ShapeDtypeStruct(a.shape, a.dtype) for a in sums],
        scratch_shapes=_chip_exchange_sems(n),
    )(*sums)


def _chip_exchange_sems(n):
    return [pltpu.SemaphoreType.DMA((3 * n,)), pltpu.SemaphoreType.DMA((3 * n,))]


def _chip_exchange_steps(ins, outs, send, recv):
    n = len(ins)

    def copy(a, k, own_slot):
        x, y, c, chips = _place()
        px, py = chips[k]
        slot = 2 * x + y if own_slot else 2 * px + py
        return pltpu.make_async_remote_copy(src_ref=ins[a].at[2 * px + py], dst_ref=outs[a].at[slot], send_sem=send.at[3 * a + k],
                                            recv_sem=recv.at[3 * a + k], device_id=(px, py, c), device_id_type=MESH)

    def start():
        for a in range(n):
            for k in range(3):
                copy(a, k, True).start()

    def finish():
        for a in range(n):
            for k in range(3):
                copy(a, k, False).wait_recv()
        for a in range(n):
            for k in range(3):
                copy(a, k, True).wait_send()

    return start, finish


def _sum_chips(name, slots, sums, where):
    nchip, rh, C = slots.shape
    tr = _tile(rh, 512, 2 * SUBLANES)
    nt = rh // tr

    def body(where_ref, own_ref, s1_ref, s2_ref, s3_ref, o_ref):
        me = where_ref[0]
        by_dist = [r[...].astype(f32) for r in (own_ref, s1_ref, s2_ref, s3_ref)]
        acc = None
        for j in range(nchip):
            d = me ^ j
            term = jnp.where(d == 0, by_dist[0], jnp.where(d == 1, by_dist[1], jnp.where(d == 2, by_dist[2], by_dist[3])))
            acc = term if acc is None else acc + term
        o_ref[...] = acc

    def other(d):
        return pl.BlockSpec((None, tr, C), lambda i, w: (w[0] ^ d, i, 0))

    return pl.pallas_call(
        body, name=name,
        grid_spec=pltpu.PrefetchScalarGridSpec(
            num_scalar_prefetch=1, grid=(nt,),
            in_specs=[other(0), other(1), other(2), other(3)],
            out_specs=pl.BlockSpec((tr, C), lambda i, w: (w[1] * nt + i, 0))),
        out_shape=jax.ShapeDtypeStruct((2 * rh, C), f32), compiler_params=_params("parallel"),
    )(where, sums, slots, slots, slots)


def _sum_slots(name, slots):
    ns, R, C = slots.shape
    tr = _tile(R, 256)

    def body(s_ref, o_ref):
        acc = s_ref[0]
        for j in range(1, ns):
            acc = acc + s_ref[j]
        o_ref[...] = acc

    return pl.pallas_call(
        body, name=name, grid=(R // tr,), in_specs=[pl.BlockSpec((ns, tr, C), lambda i: (0, i, 0))],
        out_specs=pl.BlockSpec((tr, C), lambda i: (i, 0)), out_shape=jax.ShapeDtypeStruct((R, C), f32),
        compiler_params=_params("parallel"),
    )(slots)


def _sibling_join(fulls):
    n = len(fulls)

    def body(*refs):
        ins, outs = refs[:n], refs[n:2 * n]
        send, recv = refs[2 * n:]
        x, y, c, _ = _place()

        def copy(a, core):
            rh = ins[a].shape[0] // 2
            rows = pl.ds(core * rh, rh)
            return pltpu.make_async_remote_copy(src_ref=ins[a].at[rows, :], dst_ref=outs[a].at[rows, :], send_sem=send.at[a],
                                                recv_sem=recv.at[a], device_id=(x, y, 1 - c), device_id_type=MESH)

        for a in range(n):
            copy(a, c).start()
        for a in range(n):
            copy(a, 1 - c).wait_recv()
        for a in range(n):
            copy(a, c).wait_send()

    return pl.pallas_call(
        body, name="sibling_join", in_specs=[ANY] * n, out_specs=[ANY] * n,
        out_shape=[jax.ShapeDtypeStruct(a.shape, a.dtype) for a in fulls],
        scratch_shapes=[pltpu.SemaphoreType.DMA((n,)), pltpu.SemaphoreType.DMA((n,))],
        input_output_aliases={a: a for a in range(n)},
    )(*fulls)


def _allgather_devices(v):
    def body(v_ref, out_ref, send, recv):
        x, y, c, chips = _place()
        me, sibling = (x, y, c), (x, y, 1 - c)

        def slot(px, py, pc):
            return out_ref.at[4 * px + 2 * py + pc]

        def copy(k, block, to, src=None):
            return pltpu.make_async_remote_copy(src_ref=slot(*block) if src is None else src, dst_ref=slot(*block),
                                                send_sem=send.at[k], recv_sem=recv.at[k], device_id=to, device_id_type=MESH)

        first = [copy(0, me, sibling, src=v_ref)] + [copy(1 + j, me, (*chip, c), src=v_ref) for j, chip in enumerate(chips)]
        for cp in first:
            cp.start()
        passed = [copy(4 + j, (*chip, c), sibling) for j, chip in enumerate(chips)]
        for j, chip in enumerate(chips):
            copy(1 + j, (*chip, c), me).wait_recv()
            passed[j].start()
        copy(0, sibling, me).wait_recv()
        for j, chip in enumerate(chips):
            copy(4 + j, (*chip, 1 - c), me).wait_recv()
        for cp in first + passed:
            cp.wait_send()

    return pl.pallas_call(
        body, name="allgather_devices", in_specs=[ANY], out_specs=ANY,
        out_shape=jax.ShapeDtypeStruct((N_DEV,) + v.shape, v.dtype),
        scratch_shapes=[pltpu.SemaphoreType.DMA((N_DEV - 1,)), pltpu.SemaphoreType.DMA((N_DEV - 1,))],
    )(v)


def _adamw(name, w, g, m, v):
    R, C = w.shape
    tr = _tile(R, 256)
    c1 = 1.0 / (1.0 - ADAM_B1 ** ADAM_STEP)
    c2 = 1.0 / (1.0 - ADAM_B2 ** ADAM_STEP)

    def body(w_ref, g_ref, m_ref, v_ref, d_ref, nm_ref, nv_ref):
        gv = g_ref[...]
        nm = ADAM_B1 * m_ref[...] + (1.0 - ADAM_B1) * gv
        nv = ADAM_B2 * v_ref[...] + (1.0 - ADAM_B2) * gv * gv
        d_ref[...] = -ADAM_LR * ((nm * c1) / (jnp.sqrt(nv * c2) + ADAM_EPS) + ADAM_WD * w_ref[...])
        nm_ref[...] = nm
        nv_ref[...] = nv

    row = pl.BlockSpec((tr, C), lambda i: (i, 0))
    sh = jax.ShapeDtypeStruct((R, C), f32)
    return pl.pallas_call(body, name=name, grid=(R // tr,), in_specs=[row] * 4, out_specs=[row] * 3,
                          out_shape=[sh, sh, sh], compiler_params=_params("parallel"))(w, g, m, v)


def _adamw_update(w, g, m, v):
    c1 = 1.0 / (1.0 - ADAM_B1 ** ADAM_STEP)
    c2 = 1.0 / (1.0 - ADAM_B2 ** ADAM_STEP)
    nm = ADAM_B1 * m + (1.0 - ADAM_B1) * g
    nv = ADAM_B2 * v + (1.0 - ADAM_B2) * g * g
    return -ADAM_LR * ((nm * c1) / (jnp.sqrt(nv * c2) + ADAM_EPS) + ADAM_WD * w), nm, nv


def _adamw_many(ws, gs, ms, vs):
    n = len(ws)

    def body(*refs):
        ins, outs = refs[:4 * n], refs[4 * n:]
        for a in range(n):
            d, nm, nv = _adamw_update(ins[a][...], ins[n + a][...], ins[2 * n + a][...], ins[3 * n + a][...])
            outs[a][...] = d
            outs[n + a][...] = nm
            outs[2 * n + a][...] = nv

    shapes = [jax.ShapeDtypeStruct(a.shape, f32) for a in ws]
    return pl.pallas_call(body, name="adamw_small", out_shape=shapes * 3)(*ws, *gs, *ms, *vs)


def _zoh_parts(lr, li, log_dt):
    dt = jnp.exp(log_dt)
    mag = jnp.exp(lr * dt)
    c, s = jnp.cos(li * dt), jnp.sin(li * dt)
    ab_re, ab_im = mag * c, mag * s
    den = lr * lr + li * li
    nr = ab_re - 1.0
    coef_re = (nr * lr + ab_im * li) / den
    coef_im = (ab_im * lr - nr * li) / den
    return dt, mag, c, s, ab_re, ab_im, den, nr, coef_re, coef_im


def _zoh_fwd(lr, li, log_dt, b_re, b_im):
    def body(lr_ref, li_ref, ld_ref, br_ref, bi_ref, ar_ref, ai_ref, bbr_ref, bbi_ref):
        _, _, _, _, ab_re, ab_im, _, _, coef_re, coef_im = _zoh_parts(lr_ref[...], li_ref[...], ld_ref[...])
        ar_ref[...] = ab_re
        ai_ref[...] = ab_im
        bbr_ref[...] = coef_re * br_ref[...] - coef_im * bi_ref[...]
        bbi_ref[...] = coef_re * bi_ref[...] + coef_im * br_ref[...]

    col = jax.ShapeDtypeStruct(lr.shape, f32)
    mat = jax.ShapeDtypeStruct(b_re.shape, f32)
    return pl.pallas_call(body, name="zoh_fwd", out_shape=[col, col, mat, mat])(lr, li, log_dt, b_re, b_im)


def _zoh_bwd(lr, li, log_dt, b_re, b_im, d_ar, d_ai, d_bbr, d_bbi):
    n = lr.shape[1]
    groups = n // SSM_STATE

    def body(lr_ref, li_ref, ld_ref, br_ref, bi_ref, dar_ref, dai_ref, dbbr_ref, dbbi_ref,
             dlr_ref, dli_ref, dld_ref, dbr_ref, dbi_ref):
        lr_, li_ = lr_ref[...], li_ref[...]
        dt, mag, c, s, _, ab_im, den, nr, coef_re, coef_im = _zoh_parts(lr_, li_, ld_ref[...])
        br, bi, dbbr, dbbi = br_ref[...], bi_ref[...], dbbr_ref[...], dbbi_ref[...]
        dbr_ref[...] = coef_re * dbbr + coef_im * dbbi
        dbi_ref[...] = coef_re * dbbi - coef_im * dbbr
        d_cr = jnp.sum(dbbr * br + dbbi * bi, axis=0, keepdims=True)
        d_ci = jnp.sum(dbbi * br - dbbr * bi, axis=0, keepdims=True)
        d_nr = (d_cr * lr_ - d_ci * li_) / den
        d_abi = dai_ref[...] + (d_cr * li_ + d_ci * lr_) / den
        d_abr = dar_ref[...] + d_nr
        d_den = -(d_cr * coef_re + d_ci * coef_im) / den
        d_lr = (d_cr * nr + d_ci * ab_im) / den + 2.0 * lr_ * d_den
        d_li = (d_cr * ab_im - d_ci * nr) / den + 2.0 * li_ * d_den
        d_theta = mag * (d_abi * c - d_abr * s)
        d_arg = mag * (d_abr * c + d_abi * s)
        dlr_ref[...] = d_lr + d_arg * dt
        dli_ref[...] = d_li + d_theta * dt
        d_dt = d_arg * lr_ + d_theta * li_
        member = (lax.broadcasted_iota(jnp.int32, (n, groups), 0) >> (SSM_STATE.bit_length() - 1)
                  == lax.broadcasted_iota(jnp.int32, (n, groups), 1)).astype(f32)
        dld_ref[...] = jnp.dot(d_dt * dt, member, preferred_element_type=f32, precision=lax.Precision.HIGHEST)

    col = jax.ShapeDtypeStruct(lr.shape, f32)
    mat = jax.ShapeDtypeStruct(b_re.shape, f32)
    return pl.pallas_call(body, name="zoh_bwd", out_shape=[col, col, jax.ShapeDtypeStruct((1, groups), f32), mat, mat])(
        lr, li, log_dt, b_re, b_im, d_ar, d_ai, d_bbr, d_bbi)


def _lower_bound_fwd(logits):
    def body(x_ref, o_ref):
        x = x_ref[...]
        e = jnp.exp(x - jnp.max(x, axis=0, keepdims=True))
        o_ref[...] = e / jnp.sum(e, axis=0, keepdims=True)

    return pl.pallas_call(body, name="lower_bound_fwd", out_shape=jax.ShapeDtypeStruct(logits.shape, f32))(logits)


def _lower_bound_bwd(sm, d_lb):
    def body(sm_ref, d_ref, o_ref):
        smv = sm_ref[...]
        row = lax.broadcasted_iota(jnp.int32, smv.shape, 0)
        sm0 = smv[0:1, :]
        o_ref[...] = sm0 * d_ref[...] * (jnp.where(row == 0, 1.0, 0.0) - smv)

    return pl.pallas_call(body, name="lower_bound_bwd", out_shape=jax.ShapeDtypeStruct(sm.shape, f32))(sm, d_lb)


def _s5_tables(ab_re, ab_im, bb_re, bb_im, c_re, c_im, seg):
    eye = jnp.eye(SLAB_GROUPS, dtype=f32)

    def blk_in(bb):
        return jnp.einsum("hsgp,gk->sghkp", bb.reshape(SSM_GROUP, N_SLAB, SLAB_GROUPS, SSM_STATE), eye).reshape(
            N_SLAB, SLAB_CH, SLAB_NS)

    def blk_out(cc):
        return jnp.einsum("sghp,gk->skpgh", cc.reshape(N_SLAB, SLAB_GROUPS, SSM_GROUP, SSM_STATE), eye).reshape(
            N_SLAB, SLAB_NS, SLAB_CH)

    bs = jnp.concatenate([blk_in(bb_re), blk_in(bb_im)], axis=2).astype(bf16)
    cs = jnp.concatenate([blk_out(c_re), blk_out(-c_im)], axis=1).astype(bf16)
    n = SSM_GROUPS * SSM_STATE
    pw = _power_table(jnp.stack([ab_re.reshape(1, n), ab_im.reshape(1, n)]), -(-seg // SUBLANES))
    return bs, cs, pw


def _power_table(ab, tiles):
    n = ab.shape[2]

    def body(a_ref, o_ref):
        row = lax.broadcasted_iota(jnp.int32, (SUBLANES, n), 0)
        ar, ai = a_ref[0], a_ref[1]
        tr, ti = jnp.broadcast_to(ar, (SUBLANES, n)), jnp.broadcast_to(ai, (SUBLANES, n))
        pr, pi = ar, ai
        for r in range(1, SUBLANES):
            pr, pi = pr * ar - pi * ai, pr * ai + pi * ar
            tr = jnp.where(row == r, pr, tr)
            ti = jnp.where(row == r, pi, ti)
        o_ref[0, 0:SUBLANES, :] = tr
        o_ref[1, 0:SUBLANES, :] = ti

        def step(j, carry):
            cr, ci = carry
            cr, ci = cr * pr - ci * pi, cr * pi + ci * pr
            o_ref[0, _rows8(j), :] = cr
            o_ref[1, _rows8(j), :] = ci
            return cr, ci

        lax.fori_loop(1, tiles, step, (tr, ti))

    return pl.pallas_call(body, name="power_table", out_shape=jax.ShapeDtypeStruct((2, SUBLANES * tiles, n), f32))(ab)


def _s5_table_grads(dbs, dcs, da):
    eye = jnp.eye(SLAB_GROUPS, dtype=f32)
    d6 = dbs.reshape(N_SLAB, SLAB_GROUPS, SSM_GROUP, 2, SLAB_GROUPS, SSM_STATE)
    dbb = jnp.einsum("sghrkp,gk->rhsgp", d6, eye).reshape(2, SSM_GROUP, SSM_GROUPS * SSM_STATE)
    c6 = dcs.reshape(N_SLAB, 2, SLAB_GROUPS, SSM_STATE, SLAB_GROUPS, SSM_GROUP)
    dcc = jnp.einsum("srkpgh,gk->rsghp", c6, eye).reshape(2, SSM_GROUPS, SSM_GROUP, SSM_STATE)
    dab = da.transpose(1, 0, 2).reshape(2, SSM_GROUPS, SSM_STATE)
    return dab[0], dab[1], dbb[0], dbb[1], dcc[0], -dcc[1]


SMALL = ["mix_norm_g", "ssm_lambda_re", "ssm_lambda_im", "ssm_log_dt", "ssm_b_re", "ssm_b_im", "ssm_c_re", "ssm_c_im",
         "ssm_d", "hgrn_lb_logits", "hgrn_norm_g", "ffn_norm_g", "conv_b", "final_norm_g"]
SHARDED_SMALL = ["meta_tokens", "conv_w"]
BIG = ["w_in", "ssm_w_glu", "w_ssm_proj", "w_hgrn_proj", "w_out", "w_up", "w_down"]
WEIGHTS = ['meta_tokens', 'mix_norm_g', 'w_in', 'ssm_lambda_re', 'ssm_lambda_im', 'ssm_log_dt', 'ssm_b_re', 'ssm_b_im',
           'ssm_c_re', 'ssm_c_im', 'ssm_d', 'ssm_w_glu', 'w_ssm_proj', 'hgrn_lb_logits', 'hgrn_norm_g', 'w_hgrn_proj',
           'w_out', 'ffn_norm_g', 'w_up', 'conv_w', 'conv_b', 'w_down', 'final_norm_g']


LATER = [k for k in BIG if k != "w_in"]


def _full_weights(gathered, shards, chip):
    Dm = D_MODEL
    g = {k: lax.dynamic_update_slice(gathered[k], shards[k][None], (chip, 0, 0)) for k in gathered}
    full = {}
    for k, v in g.items():
        if k == "w_in":
            full[k] = jnp.roll(v.transpose(1, 0, 2).reshape(Dm, IN_COLS), -Dm, axis=1)
        elif k == "w_up":
            full[k] = v.transpose(1, 0, 2).reshape(Dm, 2 * D_FF)
        else:
            full[k] = v.reshape(-1, Dm)
    return full


def _local_grads(x, tgt, meta, w, full, shards, chip, core):
    B, S, Dm = x.shape
    L = S + N_META
    T = B * L
    h0 = jnp.concatenate([jnp.broadcast_to(meta[None], (B, N_META, Dm)), x], axis=1).reshape(T, Dm)

    lb_all = _lower_bound_fwd(w["hgrn_lb_logits"])
    lb = lb_all[0:1]
    gp = SSM_GROUPS * SSM_STATE
    zoh_in = (w["ssm_lambda_re"].reshape(1, gp), w["ssm_lambda_im"].reshape(1, gp),
              jnp.repeat(w["ssm_log_dt"].reshape(SSM_GROUPS, 1), SSM_STATE, axis=1).reshape(1, gp),
              w["ssm_b_re"].reshape(gp, SSM_GROUP).T, w["ssm_b_im"].reshape(gp, SSM_GROUP).T)
    ab_re, ab_im, bb_re, bb_im = _zoh_fwd(*zoh_in)
    bs, cs, pw = _s5_tables(ab_re, ab_im, bb_re, bb_im, w["ssm_c_re"][0], w["ssm_c_im"][0], L // SUBLANES)

    z1 = _rmsnorm_fwd("mix_norm", h0, w["mix_norm_g"])
    p, gathered = _in_proj_gather(z1, full["w_in"], [shards[k] for k in LATER])
    full = {**full, **_full_weights(dict(zip(LATER, gathered)), shards, chip)}
    ya0 = _s5_fwd(p, bs, cs, pw, w["ssm_d"], B, L)
    gl, ya = _glu_proj_fwd(ya0, full["ssm_w_glu"])
    yb = _hgrn_fwd(p, lb, w["hgrn_norm_g"], B, L)
    pa, pb, merged = _proj_merge_fwd(ya, yb, full["w_ssm_proj"], full["w_hgrn_proj"], p)
    h1, z2 = _out_proj_norm(merged, full["w_out"], h0, w["ffn_norm_g"])
    up = _mm_rows("up_proj", z2, full["w_up"], "nn", f32, D_FF // 2, tm_target=2064)
    ff = _conv_fwd(up, full["conv_w"], w["conv_b"], B, L)
    h2 = _mm_rows("down_proj", ff, full["w_down"], "nn", f32, 1024, res=h1, tk=D_FF // 2)

    tgt_rows = jnp.pad(tgt, ((0, 0), (N_META, 0), (0, 0))).reshape(T, Dm)
    dh2, loss, d_final_g = _final_loss(h2, tgt_rows, w["final_norm_g"].reshape(1, Dm), L)

    dff = _mm_rows("d_ff", dh2, full["w_down"], "nt", f32, D_FF // 2)
    g_w_down = _mm_wgrad("dw_down", ff, dh2, tn=512)
    dup, dconv = _conv_bwd(up, dff, full["conv_w"], w["conv_b"], B, L)
    g_w_up = _dw_up(z2, dup)
    dh1, d_ffn_g = _dz2_norm(dup, full["w_up"], h1, w["ffn_norm_g"], dh2)

    g_w_out = _mm_wgrad("dw_out", merged, dh1)
    dpa, dpb, dp = _merge_bwd_fused(dh1, full["w_out"], p, pa, pb)
    dgl, dya0_direct = _glu_bwd_fused(dpa, full["w_ssm_proj"], ya0, gl)
    g_w_ssm_proj = _mm_wgrad("dw_ssm_proj", ya, dpa)
    dyb = _mm_rows("d_yb", dpb, full["w_hgrn_proj"], "nt", f32, 1024)
    g_w_hgrn_proj = _mm_wgrad("dw_hgrn_proj", yb, dpb)
    parts = {
        "w_ssm_proj": g_w_ssm_proj.reshape(N_CHIPS, Dm // N_CHIPS, Dm),
        "w_hgrn_proj": g_w_hgrn_proj.reshape(N_CHIPS, Dm // N_CHIPS, Dm), "w_out": g_w_out.reshape(N_CHIPS, Dm // N_CHIPS, Dm),
        "w_up": g_w_up, "w_down": g_w_down.reshape(N_CHIPS, D_FF // N_CHIPS, Dm),
    }
    early = [k for k in LATER if k != "ssm_w_glu"]
    (dp, d_lb, d_hgrn_g), got_early = _hgrn_bwd(p, dyb, dp, lb, w["hgrn_norm_g"], B, L, [parts[k] for k in early])
    dya0 = _mm_rows("d_ya0", dgl, full["ssm_w_glu"], "nt", f32, 1024, res=dya0_direct)
    parts["ssm_w_glu"] = _mm_wgrad("dw_glu", ya0, dgl).reshape(N_CHIPS, Dm // N_CHIPS, Dm)
    got = dict(zip(early, got_early))
    got["ssm_w_glu"] = _sibling_halves([parts["ssm_w_glu"]], "sibling_halves_w_glu")[0]
    sums = {k: _add_own_half("add_half_" + k, parts[k], got[k], core) for k in LATER}
    (dp, dbs, dcs, da, d_skip), slots_later = _s5_bwd(p, dya0, dp, bs, cs, pw, w["ssm_d"], B, L, [sums[k] for k in LATER])
    slots = dict(zip(LATER, slots_later))
    g_w_in = _dw_in(z1, dp)
    dh0, d_mix_g = _dz1_norm(dp, full["w_in"], h0, w["mix_norm_g"], dh1)

    dh0 = dh0.reshape(B, L, Dm)
    grad_x = dh0[:, N_META:]
    d_meta = _meta_grad(dh0[:, :N_META])

    d_ab_re, d_ab_im, d_bb_re, d_bb_im, d_c_re, d_c_im = _s5_table_grads(dbs, dcs, da)
    d_lr, d_li, d_log_dt, d_b_re, d_b_im = _zoh_bwd(*zoh_in, d_ab_re.reshape(1, gp), d_ab_im.reshape(1, gp), d_bb_re, d_bb_im)
    gps = (SSM_GROUPS, SSM_STATE)
    d_lr, d_li, d_log_dt = d_lr.reshape(gps), d_li.reshape(gps), d_log_dt.reshape(SSM_GROUPS)
    d_b_re, d_b_im = d_b_re.T.reshape(gps + (SSM_GROUP,)), d_b_im.T.reshape(gps + (SSM_GROUP,))
    d_logits = _lower_bound_bwd(lb_all, d_lb)
    small = {
        "meta_tokens": d_meta, "mix_norm_g": d_mix_g, "ssm_lambda_re": d_lr[None], "ssm_lambda_im": d_li[None],
        "ssm_log_dt": d_log_dt[None], "ssm_b_re": d_b_re[None], "ssm_b_im": d_b_im[None], "ssm_c_re": d_c_re[None],
        "ssm_c_im": d_c_im[None], "ssm_d": d_skip, "hgrn_lb_logits": d_logits, "hgrn_norm_g": d_hgrn_g,
        "ffn_norm_g": d_ffn_g, "conv_w": dconv[:, 0:3, :].transpose(1, 0, 2).reshape(3, 2 * D_FF),
        "conv_b": dconv[:, 3, :].reshape(1, 2 * D_FF), "final_norm_g": d_final_g.reshape(Dm),
    }
    sums["w_in"] = _add_own_half_w_in(g_w_in, _sibling_halves([g_w_in], "sibling_halves_w_in")[0], core)
    slots["w_in"] = _chip_exchange([sums["w_in"]])[0]
    return loss, grad_x, sums, slots, small


PACK_ROWS = 256


def _pack(parts):
    flat = jnp.concatenate([parts[k].reshape(-1) for k in parts])
    n = flat.shape[0]
    rows = -(-n // (PACK_ROWS * LANES)) * PACK_ROWS
    flat = jnp.pad(flat, (0, rows * LANES - n))
    return flat.reshape(rows, LANES)


def _unpack(packed, like):
    flat = packed.reshape(-1)
    out, o = {}, 0
    for k, ref in like.items():
        n = math.prod(ref.shape)
        out[k] = flat[o:o + n].reshape(ref.shape)
        o += n
    return out


def kernel(x, meta_tokens, mix_norm_g, w_in, ssm_lambda_re, ssm_lambda_im, ssm_log_dt, ssm_b_re, ssm_b_im, ssm_c_re, ssm_c_im, ssm_d, ssm_w_glu, w_ssm_proj, hgrn_lb_logits, hgrn_norm_g, w_hgrn_proj, w_out, ffn_norm_g, w_up, conv_w, conv_b, w_down, final_norm_g, loss_target, m_meta_tokens, m_mix_norm_g, m_w_in, m_ssm_lambda_re, m_ssm_lambda_im, m_ssm_log_dt, m_ssm_b_re, m_ssm_b_im, m_ssm_c_re, m_ssm_c_im, m_ssm_d, m_ssm_w_glu, m_w_ssm_proj, m_hgrn_lb_logits, m_hgrn_norm_g, m_w_hgrn_proj, m_w_out, m_ffn_norm_g, m_w_up, m_conv_w, m_conv_b, m_w_down, m_final_norm_g, v_meta_tokens, v_mix_norm_g, v_w_in, v_ssm_lambda_re, v_ssm_lambda_im, v_ssm_log_dt, v_ssm_b_re, v_ssm_b_im, v_ssm_c_re, v_ssm_c_im, v_ssm_d, v_ssm_w_glu, v_w_ssm_proj, v_hgrn_lb_logits, v_hgrn_norm_g, v_w_hgrn_proj, v_w_out, v_ffn_norm_g, v_w_up, v_conv_w, v_conv_b, v_w_down, v_final_norm_g):
    args = dict(locals())
    w = {k: args[k] for k in WEIGHTS}
    mom = {k: args["m_" + k] for k in WEIGHTS}
    var = {k: args["v_" + k] for k in WEIGHTS}
    Dm = D_MODEL
    cx, cy, cc = lax.axis_index("x"), lax.axis_index("y"), lax.axis_index("c")
    chip = 2 * cx + cy

    shards = {k: w[k][0].astype(bf16) for k in BIG}
    g_meta, g_cw = _allgather_chips([w["meta_tokens"], w["conv_w"][0]])
    full = _full_weights({"w_in": _allgather_split([shards["w_in"]])[0]}, shards, chip)
    full["conv_w"] = g_cw.transpose(1, 0, 2).reshape(3, 2 * D_FF)
    meta_full = g_meta.transpose(1, 0, 2).reshape(N_META, Dm)

    core = cc.reshape(1).astype(jnp.int32)
    loss_part, grad_x, sums, slots, small = _local_grads(x, loss_target, meta_full, w, full, shards, chip, core)

    where = jnp.stack([chip, cc]).astype(jnp.int32)
    fulls = [_sum_chips("sum_chips_" + k, slots[k], sums[k], where) for k in BIG]
    g_big = dict(zip(BIG, _sibling_join(fulls)))

    small_all = dict(small)
    small_all["loss"] = loss_part[0, 0:1]
    packed = _pack(small_all)
    slots_dev = lax.dynamic_update_slice(_allgather_devices(packed), packed[None], (2 * chip + cc, 0, 0))
    reduced = _unpack(_sum_slots("sum_devices", slots_dev), small_all)
    loss = reduced.pop("loss")[0]
    mcols = Dm // N_CHIPS
    ccols = 2 * D_FF // N_CHIPS
    grads = {k: reduced[k] for k in SMALL}
    grads["meta_tokens"] = lax.dynamic_slice(reduced["meta_tokens"], (0, chip * mcols), (N_META, mcols))
    grads["conv_w"] = lax.dynamic_slice(reduced["conv_w"], (0, chip * ccols), (3, ccols))[None]
    for k in BIG:
        grads[k] = g_big[k][None]

    delta, new_m, new_v = {}, {}, {}
    for k in BIG:
        shp = w[k].shape
        d, nm, nv = _adamw("adamw_" + k, w[k][0], grads[k][0], mom[k][0], var[k][0])
        delta[k], new_m[k], new_v[k] = d.reshape(shp), nm.reshape(shp), nv.reshape(shp)
    rest = SMALL + SHARDED_SMALL

    def flat2(a):
        return a.reshape(-1, a.shape[-1])

    outs = _adamw_many(*[[flat2(t[k]) for k in rest] for t in (w, grads, mom, var)])
    n = len(rest)
    for j, dst in enumerate((delta, new_m, new_v)):
        dst.update({k: o.reshape(w[k].shape) for k, o in zip(rest, outs[j * n:(j + 1) * n])})

    return (loss, grad_x, *[grads[k].reshape(w[k].shape) for k in WEIGHTS], *[delta[k] for k in WEIGHTS],
            *[new_m[k] for k in WEIGHTS], *[new_v[k] for k in WEIGHTS])
```
